```python
import jax, jax.numpy as jnp
from jax import lax
import numpy as np

D_MODEL = 1024
BATCH = 16
SEQ = 2048
DEPTH = 1

MEM_LEN = 256
MIX_WIDTH = D_MODEL
FOX_HEADS = 8
FOX_HEAD_DIM = (MIX_WIDTH // 2) // FOX_HEADS
FOX_WIDTH = FOX_HEADS * FOX_HEAD_DIM
GDN_HEADS = 4
GDN_HEAD_DIM = (MIX_WIDTH // 2) // GDN_HEADS
GDN_WIDTH = GDN_HEADS * GDN_HEAD_DIM
CONV_WIDTH = 4
GDN_CHUNK = 64
Q_BLOCK = 128
XATTN_HEADS = 4
XATTN_HEAD_DIM = 128
XATTN_WIDTH = XATTN_HEADS * XATTN_HEAD_DIM
D_FF = 4 * D_MODEL
EPS = 1e-6
NEG_INF = -1e30

IN_SIZES = [3 * FOX_WIDTH, FOX_HEADS, 3 * GDN_WIDTH, GDN_HEADS, GDN_HEADS, GDN_WIDTH]
IN_DIM = sum(IN_SIZES)
IN_OFFSETS = [int(o) for o in np.cumsum(IN_SIZES)[:-1]]

kernel_name = "hymba_fox_gdn_memxattn_block"


def rms_norm(x, g):
    xf = x.astype(jnp.float32)
    y = xf * lax.rsqrt(jnp.mean(xf * xf, axis=-1, keepdims=True) + EPS)
    return (y * g.astype(jnp.float32)).astype(x.dtype)


def l2_norm(x):
    xf = x.astype(jnp.float32)
    return (xf * lax.rsqrt(jnp.sum(xf * xf, axis=-1, keepdims=True) + EPS)).astype(x.dtype)


def causal_depthwise_conv(x, w):
    C = x.shape[-1]
    return lax.conv_general_dilated(
        x, w[:, None, :].astype(x.dtype), window_strides=(1,),
        padding=((CONV_WIDTH - 1, 0),),
        dimension_numbers=("NWC", "WIO", "NWC"),
        feature_group_count=C)


def forgetting_attention(q, k, v, f_logit):
    B, S, H, Dh = q.shape
    nb = S // Q_BLOCK
    scale = Dh ** -0.5
    c = jnp.cumsum(jax.nn.log_sigmoid(f_logit.astype(jnp.float32)), axis=1)
    c = c.transpose(0, 2, 1)
    q_blocks = q.reshape(B, nb, Q_BLOCK, H, Dh).transpose(1, 0, 3, 2, 4)
    c_blocks = c.reshape(B, H, nb, Q_BLOCK).transpose(2, 0, 1, 3)
    kpos = jnp.arange(S)

    def block(args):
        i, q_i, c_i = args
        qpos = i * Q_BLOCK + jnp.arange(Q_BLOCK)
        s = jnp.einsum("bhqd,bshd->bhqs", q_i, k,
                       preferred_element_type=jnp.float32) * scale
        s = s + c_i[..., :, None] - c[:, :, None, :]
        s = jnp.where(kpos[None, :] <= qpos[:, None], s, NEG_INF)
        p = jax.nn.softmax(s, axis=-1)
        return jnp.einsum("bhqs,bshd->bqhd", p.astype(v.dtype), v)

    out = lax.map(block, (jnp.arange(nb), q_blocks, c_blocks))
    return out.transpose(1, 0, 2, 3, 4).reshape(B, S, H, Dh)


def gated_delta_rule(q, k, v, g, beta):
    out_dtype = v.dtype
    B, S, H, Dk = q.shape
    Dv = v.shape[-1]
    C = GDN_CHUNK
    N = S // C
    f32 = jnp.float32

    def chunk(t):
        t = t.astype(f32).reshape((B, N, C, H) + t.shape[3:])
        return jnp.moveaxis(t, 3, 1)

    q = chunk(q) * (Dk ** -0.5)
    k = chunk(k)
    v = chunk(v)
    beta = chunk(beta)
    gc = jnp.cumsum(chunk(g), axis=-1)

    idx = jnp.arange(C)
    lower_incl = idx[:, None] >= idx[None, :]
    strict = idx[:, None] > idx[None, :]
    decay = jnp.exp(jnp.where(lower_incl, gc[..., :, None] - gc[..., None, :], NEG_INF))

    k_beta = k * beta[..., None]
    v_beta = v * beta[..., None]
    L = jnp.where(strict, jnp.einsum("bhncd,bhnsd->bhncs", k_beta, k) * decay, 0.0)
    eye = jnp.broadcast_to(jnp.eye(C, dtype=f32), L.shape)
    T = lax.linalg.triangular_solve(eye + L, eye, left_side=True, lower=True)
    u = jnp.einsum("bhncs,bhnse->bhnce", T, v_beta)
    w = jnp.einsum("bhncs,bhnsd->bhncd", T, k_beta * jnp.exp(gc)[..., None])
    intra = jnp.einsum("bhncd,bhnsd->bhncs", q, k) * decay

    def to_scan(t):
        return jnp.moveaxis(t, 2, 0)

    def step(state, xs):
        q_n, k_n, u_n, w_n, gc_n, intra_n = xs
        v_new = u_n - jnp.einsum("bhcd,bhde->bhce", w_n, state)
        o = (jnp.einsum("bhcd,bhde->bhce", q_n * jnp.exp(gc_n)[..., None], state)
             + jnp.einsum("bhcs,bhse->bhce", intra_n, v_new))
        g_last = gc_n[..., -1]
        state = (state * jnp.exp(g_last)[..., None, None]
                 + jnp.einsum("bhcd,bhce->bhde",
                              k_n * jnp.exp(g_last[..., None] - gc_n)[..., None], v_new))
        return state, o

    state0 = jnp.zeros((B, H, Dk, Dv), f32)
    _, o = lax.scan(step, state0, (to_scan(q), to_scan(k), to_scan(u), to_scan(w),
                                   to_scan(gc), to_scan(intra)))
    return o.transpose(1, 0, 3, 2, 4).reshape(B, S, H, Dv).astype(out_dtype)


def _fwd_setup_inputs(seed: int = 0) -> dict:
    key = jax.random.key(seed)
    ks = jax.random.split(key, 26)
    L = DEPTH
    f32 = jnp.float32

    def nrm(k, shape, scale):
        return jax.random.normal(k, shape, f32) * scale

    def gain(k, dim):
        return 1.0 + 0.02 * jax.random.normal(k, (L, dim), f32)

    dt = jnp.exp(jax.random.uniform(ks[10], (L, GDN_HEADS), f32,
                                    minval=np.log(0.001), maxval=np.log(0.1)))
    return {
        "x": nrm(ks[0], (BATCH, SEQ, D_MODEL), 1.0),
        "mem": nrm(ks[1], (BATCH, MEM_LEN, D_MODEL), 1.0),
        "norm_mix_g": gain(ks[2], D_MODEL),
        "w_in": nrm(ks[3], (L, D_MODEL, IN_DIM), D_MODEL ** -0.5),
        "fox_qnorm_g": gain(ks[4], FOX_HEAD_DIM),
        "fox_knorm_g": gain(ks[5], FOX_HEAD_DIM),
        "fox_f_bias": 2.0 + 0.5 * jax.random.normal(ks[6], (L, FOX_HEADS), f32),
        "fox_onorm_g": gain(ks[7], FOX_HEAD_DIM),
        "gdn_conv_w": nrm(ks[8], (L, CONV_WIDTH, 3 * GDN_WIDTH), CONV_WIDTH ** -0.5),
        "gdn_A_log": jnp.log(jax.random.uniform(ks[9], (L, GDN_HEADS), f32, minval=1.0, maxval=16.0)),
        "gdn_dt_bias": dt + jnp.log(-jnp.expm1(-dt)),
        "gdn_onorm_g": gain(ks[11], GDN_HEAD_DIM),
        "w_out": nrm(ks[12], (L, MIX_WIDTH, D_MODEL), MIX_WIDTH ** -0.5),
        "norm_xattn_g": gain(ks[13], D_MODEL),
        "mem_norm_g": gain(ks[14], D_MODEL),
        "w_cq": nrm(ks[15], (L, D_MODEL, XATTN_WIDTH), D_MODEL ** -0.5),
        "w_ckv": nrm(ks[16], (L, D_MODEL, 2 * XATTN_WIDTH), D_MODEL ** -0.5),
        "xattn_qnorm_g": gain(ks[17], XATTN_HEAD_DIM),
        "xattn_knorm_g": gain(ks[18], XATTN_HEAD_DIM),
        "w_co": nrm(ks[19], (L, XATTN_WIDTH, D_MODEL), XATTN_WIDTH ** -0.5),
        "norm_mlp_g": gain(ks[20], D_MODEL),
        "w_mlp1": nrm(ks[21], (L, D_MODEL, D_FF), D_MODEL ** -0.5),
        "w_mlp2": nrm(ks[22], (L, D_FF, D_MODEL), D_FF ** -0.5),
    }


def _fwd_reference(x, mem, norm_mix_g, w_in, fox_qnorm_g, fox_knorm_g, fox_f_bias, fox_onorm_g,
              gdn_conv_w, gdn_A_log, gdn_dt_bias, gdn_onorm_g, w_out,
              norm_xattn_g, mem_norm_g, w_cq, w_ckv, xattn_qnorm_g, xattn_knorm_g, w_co,
              norm_mlp_g, w_mlp1, w_mlp2):
    B, S, _ = x.shape
    M = mem.shape[1]
    for l in range(DEPTH):
        h = rms_norm(x, norm_mix_g[l])
        proj = h @ w_in[l]
        fox_qkv, fox_f, gdn_qkv, gdn_b, gdn_a, gdn_z = jnp.split(proj, IN_OFFSETS, axis=-1)

        fq, fk, fv = jnp.split(fox_qkv.reshape(B, S, 3 * FOX_HEADS, FOX_HEAD_DIM), 3, axis=2)
        fq = rms_norm(fq, fox_qnorm_g[l])
        fk = rms_norm(fk, fox_knorm_g[l])
        o_a = forgetting_attention(fq, fk, fv, fox_f + fox_f_bias[l])
        o_a = rms_norm(o_a, fox_onorm_g[l]).reshape(B, S, FOX_WIDTH)

        gqkv = jax.nn.silu(causal_depthwise_conv(gdn_qkv, gdn_conv_w[l]))
        gq, gk, gv = jnp.split(gqkv.reshape(B, S, 3 * GDN_HEADS, GDN_HEAD_DIM), 3, axis=2)
        gq = l2_norm(gq)
        gk = l2_norm(gk)
        beta = jax.nn.sigmoid(gdn_b.astype(jnp.float32))
        g = -jnp.exp(gdn_A_log[l].astype(jnp.float32)) * jax.nn.softplus(
            (gdn_a + gdn_dt_bias[l]).astype(jnp.float32))
        o_b = gated_delta_rule(gq, gk, gv, g, beta)
        o_b = rms_norm(o_b, gdn_onorm_g[l]) * jax.nn.silu(gdn_z.reshape(B, S, GDN_HEADS, GDN_HEAD_DIM))
        o_b = o_b.reshape(B, S, GDN_WIDTH)

        x = x + jnp.concatenate([o_a, o_b], axis=-1) @ w_out[l]

        hq = rms_norm(x, norm_xattn_g[l])
        hm = rms_norm(mem, mem_norm_g[l])
        cq = rms_norm((hq @ w_cq[l]).reshape(B, S, XATTN_HEADS, XATTN_HEAD_DIM), xattn_qnorm_g[l])
        ck, cv = jnp.split((hm @ w_ckv[l]).reshape(B, M, 2 * XATTN_HEADS, XATTN_HEAD_DIM), 2, axis=2)
        ck = rms_norm(ck, xattn_knorm_g[l])
        s = jnp.einsum("bqhd,bmhd->bhqm", cq, ck,
                       preferred_element_type=jnp.float32) * (XATTN_HEAD_DIM ** -0.5)
        p = jax.nn.softmax(s, axis=-1)
        co = jnp.einsum("bhqm,bmhd->bqhd", p.astype(cv.dtype), cv).reshape(B, S, XATTN_WIDTH)
        x = x + co @ w_co[l]

        hf = rms_norm(x, norm_mlp_g[l])
        x = x + jnp.square(jax.nn.relu(hf @ w_mlp1[l])) @ w_mlp2[l]
    return x


import jax as _jax
import jax.numpy as _jnp

TWIN_FORMAT = 'train_step'
FWD_PARAMS = ['x', 'mem', 'norm_mix_g', 'w_in', 'fox_qnorm_g', 'fox_knorm_g', 'fox_f_bias', 'fox_onorm_g', 'gdn_conv_w', 'gdn_A_log', 'gdn_dt_bias', 'gdn_onorm_g', 'w_out', 'norm_xattn_g', 'mem_norm_g', 'w_cq', 'w_ckv', 'xattn_qnorm_g', 'xattn_knorm_g', 'w_co', 'norm_mlp_g', 'w_mlp1', 'w_mlp2']
TWIN_WEIGHTS = ['norm_mix_g', 'w_in', 'fox_qnorm_g', 'fox_knorm_g', 'fox_f_bias', 'fox_onorm_g', 'gdn_conv_w', 'gdn_A_log', 'gdn_dt_bias', 'gdn_onorm_g', 'w_out', 'norm_xattn_g', 'mem_norm_g', 'w_cq', 'w_ckv', 'xattn_qnorm_g', 'xattn_knorm_g', 'w_co', 'norm_mlp_g', 'w_mlp1', 'w_mlp2']
TWIN_DIFF_INPUT = 'x'
TWIN_INPUTS = ['x', 'mem', 'norm_mix_g', 'w_in', 'fox_qnorm_g', 'fox_knorm_g', 'fox_f_bias', 'fox_onorm_g', 'gdn_conv_w', 'gdn_A_log', 'gdn_dt_bias', 'gdn_onorm_g', 'w_out', 'norm_xattn_g', 'mem_norm_g', 'w_cq', 'w_ckv', 'xattn_qnorm_g', 'xattn_knorm_g', 'w_co', 'norm_mlp_g', 'w_mlp1', 'w_mlp2', 'loss_target', 'm_norm_mix_g', 'm_w_in', 'm_fox_qnorm_g', 'm_fox_knorm_g', 'm_fox_f_bias', 'm_fox_onorm_g', 'm_gdn_conv_w', 'm_gdn_A_log', 'm_gdn_dt_bias', 'm_gdn_onorm_g', 'm_w_out', 'm_norm_xattn_g', 'm_mem_norm_g', 'm_w_cq', 'm_w_ckv', 'm_xattn_qnorm_g', 'm_xattn_knorm_g', 'm_w_co', 'm_norm_mlp_g', 'm_w_mlp1', 'm_w_mlp2', 'v_norm_mix_g', 'v_w_in', 'v_fox_qnorm_g', 'v_fox_knorm_g', 'v_fox_f_bias', 'v_fox_onorm_g', 'v_gdn_conv_w', 'v_gdn_A_log', 'v_gdn_dt_bias', 'v_gdn_onorm_g', 'v_w_out', 'v_norm_xattn_g', 'v_mem_norm_g', 'v_w_cq', 'v_w_ckv', 'v_xattn_qnorm_g', 'v_xattn_knorm_g', 'v_w_co', 'v_norm_mlp_g', 'v_w_mlp1', 'v_w_mlp2']
TWIN_OUTPUTS = ['loss', 'grad_x', 'grad_norm_mix_g', 'grad_w_in', 'grad_fox_qnorm_g', 'grad_fox_knorm_g', 'grad_fox_f_bias', 'grad_fox_onorm_g', 'grad_gdn_conv_w', 'grad_gdn_A_log', 'grad_gdn_dt_bias', 'grad_gdn_onorm_g', 'grad_w_out', 'grad_norm_xattn_g', 'grad_mem_norm_g', 'grad_w_cq', 'grad_w_ckv', 'grad_xattn_qnorm_g', 'grad_xattn_knorm_g', 'grad_w_co', 'grad_norm_mlp_g', 'grad_w_mlp1', 'grad_w_mlp2', 'delta_norm_mix_g', 'delta_w_in', 'delta_fox_qnorm_g', 'delta_fox_knorm_g', 'delta_fox_f_bias', 'delta_fox_onorm_g', 'delta_gdn_conv_w', 'delta_gdn_A_log', 'delta_gdn_dt_bias', 'delta_gdn_onorm_g', 'delta_w_out', 'delta_norm_xattn_g', 'delta_mem_norm_g', 'delta_w_cq', 'delta_w_ckv', 'delta_xattn_qnorm_g', 'delta_xattn_knorm_g', 'delta_w_co', 'delta_norm_mlp_g', 'delta_w_mlp1', 'delta_w_mlp2', 'new_m_norm_mix_g', 'new_m_w_in', 'new_m_fox_qnorm_g', 'new_m_fox_knorm_g', 'new_m_fox_f_bias', 'new_m_fox_onorm_g', 'new_m_gdn_conv_w', 'new_m_gdn_A_log', 'new_m_gdn_dt_bias', 'new_m_gdn_onorm_g', 'new_m_w_out', 'new_m_norm_xattn_g', 'new_m_mem_norm_g', 'new_m_w_cq', 'new_m_w_ckv', 'new_m_xattn_qnorm_g', 'new_m_xattn_knorm_g', 'new_m_w_co', 'new_m_norm_mlp_g', 'new_m_w_mlp1', 'new_m_w_mlp2', 'new_v_norm_mix_g', 'new_v_w_in', 'new_v_fox_qnorm_g', 'new_v_fox_knorm_g', 'new_v_fox_f_bias', 'new_v_fox_onorm_g', 'new_v_gdn_conv_w', 'new_v_gdn_A_log', 'new_v_gdn_dt_bias', 'new_v_gdn_onorm_g', 'new_v_w_out', 'new_v_norm_xattn_g', 'new_v_mem_norm_g', 'new_v_w_cq', 'new_v_w_ckv', 'new_v_xattn_qnorm_g', 'new_v_xattn_knorm_g', 'new_v_w_co', 'new_v_norm_mlp_g', 'new_v_w_mlp1', 'new_v_w_mlp2']
TWIN_LEAF_KINDS = {'loss': 'loss', 'grad_x': 'grad_x', 'grad_norm_mix_g': 'grad_w', 'grad_w_in': 'grad_w', 'grad_fox_qnorm_g': 'grad_w', 'grad_fox_knorm_g': 'grad_w', 'grad_fox_f_bias': 'grad_w', 'grad_fox_onorm_g': 'grad_w', 'grad_gdn_conv_w': 'grad_w', 'grad_gdn_A_log': 'grad_w', 'grad_gdn_dt_bias': 'grad_w', 'grad_gdn_onorm_g': 'grad_w', 'grad_w_out': 'grad_w', 'grad_norm_xattn_g': 'grad_w', 'grad_mem_norm_g': 'grad_w', 'grad_w_cq': 'grad_w', 'grad_w_ckv': 'grad_w', 'grad_xattn_qnorm_g': 'grad_w', 'grad_xattn_knorm_g': 'grad_w', 'grad_w_co': 'grad_w', 'grad_norm_mlp_g': 'grad_w', 'grad_w_mlp1': 'grad_w', 'grad_w_mlp2': 'grad_w', 'delta_norm_mix_g': 'delta_w', 'delta_w_in': 'delta_w', 'delta_fox_qnorm_g': 'delta_w', 'delta_fox_knorm_g': 'delta_w', 'delta_fox_f_bias': 'delta_w', 'delta_fox_onorm_g': 'delta_w', 'delta_gdn_conv_w': 'delta_w', 'delta_gdn_A_log': 'delta_w', 'delta_gdn_dt_bias': 'delta_w', 'delta_gdn_onorm_g': 'delta_w', 'delta_w_out': 'delta_w', 'delta_norm_xattn_g': 'delta_w', 'delta_mem_norm_g': 'delta_w', 'delta_w_cq': 'delta_w', 'delta_w_ckv': 'delta_w', 'delta_xattn_qnorm_g': 'delta_w', 'delta_xattn_knorm_g': 'delta_w', 'delta_w_co': 'delta_w', 'delta_norm_mlp_g': 'delta_w', 'delta_w_mlp1': 'delta_w', 'delta_w_mlp2': 'delta_w', 'new_m_norm_mix_g': 'new_m', 'new_m_w_in': 'new_m', 'new_m_fox_qnorm_g': 'new_m', 'new_m_fox_knorm_g': 'new_m', 'new_m_fox_f_bias': 'new_m', 'new_m_fox_onorm_g': 'new_m', 'new_m_gdn_conv_w': 'new_m', 'new_m_gdn_A_log': 'new_m', 'new_m_gdn_dt_bias': 'new_m', 'new_m_gdn_onorm_g': 'new_m', 'new_m_w_out': 'new_m', 'new_m_norm_xattn_g': 'new_m', 'new_m_mem_norm_g': 'new_m', 'new_m_w_cq': 'new_m', 'new_m_w_ckv': 'new_m', 'new_m_xattn_qnorm_g': 'new_m', 'new_m_xattn_knorm_g': 'new_m', 'new_m_w_co': 'new_m', 'new_m_norm_mlp_g': 'new_m', 'new_m_w_mlp1': 'new_m', 'new_m_w_mlp2': 'new_m', 'new_v_norm_mix_g': 'new_v', 'new_v_w_in': 'new_v', 'new_v_fox_qnorm_g': 'new_v', 'new_v_fox_knorm_g': 'new_v', 'new_v_fox_f_bias': 'new_v', 'new_v_fox_onorm_g': 'new_v', 'new_v_gdn_conv_w': 'new_v', 'new_v_gdn_A_log': 'new_v', 'new_v_gdn_dt_bias': 'new_v', 'new_v_gdn_onorm_g': 'new_v', 'new_v_w_out': 'new_v', 'new_v_norm_xattn_g': 'new_v', 'new_v_mem_norm_g': 'new_v', 'new_v_w_cq': 'new_v', 'new_v_w_ckv': 'new_v', 'new_v_xattn_qnorm_g': 'new_v', 'new_v_xattn_knorm_g': 'new_v', 'new_v_w_co': 'new_v', 'new_v_norm_mlp_g': 'new_v', 'new_v_w_mlp1': 'new_v', 'new_v_w_mlp2': 'new_v'}


def _forward(args):
    return _fwd_reference(*[args[k] for k in FWD_PARAMS])


def _output_shape():
    out = _jax.eval_shape(lambda: _forward(_fwd_setup_inputs(0)))
    return out.shape, out.dtype

N_MICROBATCH = 1
ADAM_LR = 0.001
ADAM_B1 = 0.9
ADAM_B2 = 0.999
ADAM_EPS = 1e-08
ADAM_WD = 0.01
ADAM_STEP = 10
PER_EXAMPLE_BATCH_AXIS = {'x': 0, 'mem': 0, 'loss_target': 0}
SHARED_INPUTS = []
_WEIGHT_DTYPES = {'norm_mix_g': _jnp.float32, 'w_in': _jnp.float32, 'fox_qnorm_g': _jnp.float32, 'fox_knorm_g': _jnp.float32, 'fox_f_bias': _jnp.float32, 'fox_onorm_g': _jnp.float32, 'gdn_conv_w': _jnp.float32, 'gdn_A_log': _jnp.float32, 'gdn_dt_bias': _jnp.float32, 'gdn_onorm_g': _jnp.float32, 'w_out': _jnp.float32, 'norm_xattn_g': _jnp.float32, 'mem_norm_g': _jnp.float32, 'w_cq': _jnp.float32, 'w_ckv': _jnp.float32, 'xattn_qnorm_g': _jnp.float32, 'xattn_knorm_g': _jnp.float32, 'w_co': _jnp.float32, 'norm_mlp_g': _jnp.float32, 'w_mlp1': _jnp.float32, 'w_mlp2': _jnp.float32}
MOMENT_SCALE = {'norm_mix_g': 5.611154e+00, 'w_in': 5.792432e-01, 'fox_qnorm_g': 1.832798e+00, 'fox_knorm_g': 1.866309e+00, 'fox_f_bias': 3.532881e+00, 'fox_onorm_g': 2.560421e+02, 'gdn_conv_w': 1.131931e+00, 'gdn_A_log': 3.048153e+01, 'gdn_dt_bias': 2.888040e+01, 'gdn_onorm_g': 4.919146e+01, 'w_out': 2.204495e+00, 'norm_xattn_g': 1.093774e-01, 'mem_norm_g': 7.874139e-01, 'w_cq': 1.541827e-01, 'w_ckv': 6.827957e-01, 'xattn_qnorm_g': 2.487988e+00, 'xattn_knorm_g': 2.472713e+00, 'w_co': 7.156660e-01, 'norm_mlp_g': 9.600799e+01, 'w_mlp1': 1.209041e+00, 'w_mlp2': 8.162674e+00}


def _to_microbatches(a, axis):
    t = _jnp.moveaxis(a, axis, 0)
    t = t.reshape((N_MICROBATCH, t.shape[0] // N_MICROBATCH) + t.shape[1:])
    return _jnp.moveaxis(t, 1, axis + 1)


def setup_inputs(seed: int = 0) -> dict:
    inp = _fwd_setup_inputs(seed)
    key = _jax.random.fold_in(_jax.random.key(seed), 7919)
    shape, _ = _output_shape()
    out = dict(inp)
    out["loss_target"] = _jax.random.normal(_jax.random.fold_in(key, 0), shape, _jnp.float32)
    for i, name in enumerate(TWIN_WEIGHTS):
        w = inp[name].astype(_jnp.float32)
        if MOMENT_SCALE is None:
            s = _jnp.sqrt(_jnp.mean(_jnp.square(w)) + 1e-30)
        else:
            s = MOMENT_SCALE[name]
        km, kv = _jax.random.split(_jax.random.fold_in(key, i + 1))
        out[name] = w
        out["m_" + name] = s * _jax.random.normal(km, w.shape, _jnp.float32)
        out["v_" + name] = (s * s) * _jax.random.uniform(kv, w.shape, _jnp.float32, 0.5, 1.5)
    if N_MICROBATCH > 1:
        for name, axis in PER_EXAMPLE_BATCH_AXIS.items():
            out[name] = _to_microbatches(out[name], axis)
    return {'x': out['x'], 'mem': out['mem'], 'norm_mix_g': out['norm_mix_g'], 'w_in': out['w_in'], 'fox_qnorm_g': out['fox_qnorm_g'], 'fox_knorm_g': out['fox_knorm_g'], 'fox_f_bias': out['fox_f_bias'], 'fox_onorm_g': out['fox_onorm_g'], 'gdn_conv_w': out['gdn_conv_w'], 'gdn_A_log': out['gdn_A_log'], 'gdn_dt_bias': out['gdn_dt_bias'], 'gdn_onorm_g': out['gdn_onorm_g'], 'w_out': out['w_out'], 'norm_xattn_g': out['norm_xattn_g'], 'mem_norm_g': out['mem_norm_g'], 'w_cq': out['w_cq'], 'w_ckv': out['w_ckv'], 'xattn_qnorm_g': out['xattn_qnorm_g'], 'xattn_knorm_g': out['xattn_knorm_g'], 'w_co': out['w_co'], 'norm_mlp_g': out['norm_mlp_g'], 'w_mlp1': out['w_mlp1'], 'w_mlp2': out['w_mlp2'], 'loss_target': out['loss_target'], 'm_norm_mix_g': out['m_norm_mix_g'], 'm_w_in': out['m_w_in'], 'm_fox_qnorm_g': out['m_fox_qnorm_g'], 'm_fox_knorm_g': out['m_fox_knorm_g'], 'm_fox_f_bias': out['m_fox_f_bias'], 'm_fox_onorm_g': out['m_fox_onorm_g'], 'm_gdn_conv_w': out['m_gdn_conv_w'], 'm_gdn_A_log': out['m_gdn_A_log'], 'm_gdn_dt_bias': out['m_gdn_dt_bias'], 'm_gdn_onorm_g': out['m_gdn_onorm_g'], 'm_w_out': out['m_w_out'], 'm_norm_xattn_g': out['m_norm_xattn_g'], 'm_mem_norm_g': out['m_mem_norm_g'], 'm_w_cq': out['m_w_cq'], 'm_w_ckv': out['m_w_ckv'], 'm_xattn_qnorm_g': out['m_xattn_qnorm_g'], 'm_xattn_knorm_g': out['m_xattn_knorm_g'], 'm_w_co': out['m_w_co'], 'm_norm_mlp_g': out['m_norm_mlp_g'], 'm_w_mlp1': out['m_w_mlp1'], 'm_w_mlp2': out['m_w_mlp2'], 'v_norm_mix_g': out['v_norm_mix_g'], 'v_w_in': out['v_w_in'], 'v_fox_qnorm_g': out['v_fox_qnorm_g'], 'v_fox_knorm_g': out['v_fox_knorm_g'], 'v_fox_f_bias': out['v_fox_f_bias'], 'v_fox_onorm_g': out['v_fox_onorm_g'], 'v_gdn_conv_w': out['v_gdn_conv_w'], 'v_gdn_A_log': out['v_gdn_A_log'], 'v_gdn_dt_bias': out['v_gdn_dt_bias'], 'v_gdn_onorm_g': out['v_gdn_onorm_g'], 'v_w_out': out['v_w_out'], 'v_norm_xattn_g': out['v_norm_xattn_g'], 'v_mem_norm_g': out['v_mem_norm_g'], 'v_w_cq': out['v_w_cq'], 'v_w_ckv': out['v_w_ckv'], 'v_xattn_qnorm_g': out['v_xattn_qnorm_g'], 'v_xattn_knorm_g': out['v_xattn_knorm_g'], 'v_w_co': out['v_w_co'], 'v_norm_mlp_g': out['v_norm_mlp_g'], 'v_w_mlp1': out['v_w_mlp1'], 'v_w_mlp2': out['v_w_mlp2']}


def _loss(weights, diff, rest, loss_target):
    with _jax.named_scope("forward"):
        args = {**rest, TWIN_DIFF_INPUT: diff, **{k: w.astype(_WEIGHT_DTYPES[k]) for k, w in weights.items()}}
        y = _forward(args)
    with _jax.named_scope("loss_head"):
        err = _jnp.square(y.astype(_jnp.float32) - loss_target)
        return 0.5 * _jnp.sum(_jnp.mean(err, axis=-1)) if err.ndim else 0.5 * err


def _adamw(w, g, m, v):
    m = ADAM_B1 * m + (1.0 - ADAM_B1) * g
    v = ADAM_B2 * v + (1.0 - ADAM_B2) * _jnp.square(g)
    m_hat = m / (1.0 - ADAM_B1 ** ADAM_STEP)
    v_hat = v / (1.0 - ADAM_B2 ** ADAM_STEP)
    delta = -ADAM_LR * (m_hat / (_jnp.sqrt(v_hat) + ADAM_EPS) + ADAM_WD * w)
    return delta, m, v


def reference(x, mem, norm_mix_g, w_in, fox_qnorm_g, fox_knorm_g, fox_f_bias, fox_onorm_g, gdn_conv_w, gdn_A_log, gdn_dt_bias, gdn_onorm_g, w_out, norm_xattn_g, mem_norm_g, w_cq, w_ckv, xattn_qnorm_g, xattn_knorm_g, w_co, norm_mlp_g, w_mlp1, w_mlp2, loss_target, m_norm_mix_g, m_w_in, m_fox_qnorm_g, m_fox_knorm_g, m_fox_f_bias, m_fox_onorm_g, m_gdn_conv_w, m_gdn_A_log, m_gdn_dt_bias, m_gdn_onorm_g, m_w_out, m_norm_xattn_g, m_mem_norm_g, m_w_cq, m_w_ckv, m_xattn_qnorm_g, m_xattn_knorm_g, m_w_co, m_norm_mlp_g, m_w_mlp1, m_w_mlp2, v_norm_mix_g, v_w_in, v_fox_qnorm_g, v_fox_knorm_g, v_fox_f_bias, v_fox_onorm_g, v_gdn_conv_w, v_gdn_A_log, v_gdn_dt_bias, v_gdn_onorm_g, v_w_out, v_norm_xattn_g, v_mem_norm_g, v_w_cq, v_w_ckv, v_xattn_qnorm_g, v_xattn_knorm_g, v_w_co, v_norm_mlp_g, v_w_mlp1, v_w_mlp2):
    given = dict(x=x, mem=mem, norm_mix_g=norm_mix_g, w_in=w_in, fox_qnorm_g=fox_qnorm_g, fox_knorm_g=fox_knorm_g, fox_f_bias=fox_f_bias, fox_onorm_g=fox_onorm_g, gdn_conv_w=gdn_conv_w, gdn_A_log=gdn_A_log, gdn_dt_bias=gdn_dt_bias, gdn_onorm_g=gdn_onorm_g, w_out=w_out, norm_xattn_g=norm_xattn_g, mem_norm_g=mem_norm_g, w_cq=w_cq, w_ckv=w_ckv, xattn_qnorm_g=xattn_qnorm_g, xattn_knorm_g=xattn_knorm_g, w_co=w_co, norm_mlp_g=norm_mlp_g, w_mlp1=w_mlp1, w_mlp2=w_mlp2, loss_target=loss_target, m_norm_mix_g=m_norm_mix_g, m_w_in=m_w_in, m_fox_qnorm_g=m_fox_qnorm_g, m_fox_knorm_g=m_fox_knorm_g, m_fox_f_bias=m_fox_f_bias, m_fox_onorm_g=m_fox_onorm_g, m_gdn_conv_w=m_gdn_conv_w, m_gdn_A_log=m_gdn_A_log, m_gdn_dt_bias=m_gdn_dt_bias, m_gdn_onorm_g=m_gdn_onorm_g, m_w_out=m_w_out, m_norm_xattn_g=m_norm_xattn_g, m_mem_norm_g=m_mem_norm_g, m_w_cq=m_w_cq, m_w_ckv=m_w_ckv, m_xattn_qnorm_g=m_xattn_qnorm_g, m_xattn_knorm_g=m_xattn_knorm_g, m_w_co=m_w_co, m_norm_mlp_g=m_norm_mlp_g, m_w_mlp1=m_w_mlp1, m_w_mlp2=m_w_mlp2, v_norm_mix_g=v_norm_mix_g, v_w_in=v_w_in, v_fox_qnorm_g=v_fox_qnorm_g, v_fox_knorm_g=v_fox_knorm_g, v_fox_f_bias=v_fox_f_bias, v_fox_onorm_g=v_fox_onorm_g, v_gdn_conv_w=v_gdn_conv_w, v_gdn_A_log=v_gdn_A_log, v_gdn_dt_bias=v_gdn_dt_bias, v_gdn_onorm_g=v_gdn_onorm_g, v_w_out=v_w_out, v_norm_xattn_g=v_norm_xattn_g, v_mem_norm_g=v_mem_norm_g, v_w_cq=v_w_cq, v_w_ckv=v_w_ckv, v_xattn_qnorm_g=v_xattn_qnorm_g, v_xattn_knorm_g=v_xattn_knorm_g, v_w_co=v_w_co, v_norm_mlp_g=v_norm_mlp_g, v_w_mlp1=v_w_mlp1, v_w_mlp2=v_w_mlp2)
    weights = {n: given[n] for n in TWIN_WEIGHTS}
    shared = {n: given[n] for n in SHARED_INPUTS}
    per_example = {n: given[n] for n in ['x', 'mem']}
    grad_fn = _jax.value_and_grad(_loss, argnums=(0, 1))

    def one_microbatch(ex, loss_target):
        ex = dict(ex)
        diff = ex.pop(TWIN_DIFF_INPUT)
        return grad_fn(weights, diff, {**shared, **ex}, loss_target)

    if N_MICROBATCH == 1:
        loss, (grad_w, grad_x) = one_microbatch(per_example, given["loss_target"])
    else:
        def body(carry, xs):
            loss_sum, grad_sum = carry
            l_k, (gw_k, gx_k) = one_microbatch(xs[0], xs[1])
            with _jax.named_scope("update"):
                return (loss_sum + l_k, _jax.tree.map(_jnp.add, grad_sum, gw_k)), gx_k

        init = (_jnp.zeros((), _jnp.float32), _jax.tree.map(_jnp.zeros_like, weights))
        (loss, grad_w), grad_x = _jax.lax.scan(body, init, (per_example, given["loss_target"]))
    with _jax.named_scope("update"):
        delta_w, new_m, new_v = {}, {}, {}
        for n in TWIN_WEIGHTS:
            delta_w[n], new_m[n], new_v[n] = _adamw(weights[n], grad_w[n], given["m_" + n], given["v_" + n])
    return (loss, grad_x, *[grad_w[n] for n in TWIN_WEIGHTS], *[delta_w[n] for n in TWIN_WEIGHTS],
            *[new_m[n] for n in TWIN_WEIGHTS], *[new_v[n] for n in TWIN_WEIGHTS])
```

```python
import functools

import jax
import jax.numpy as jnp
import numpy as np
from jax import lax
from jax.experimental import pallas as pl
from jax.experimental.pallas import tpu as pltpu

F32 = jnp.float32
BF16 = jnp.bfloat16

D_MODEL = 1024
FOX_HEADS = 8
FOX_HEAD_DIM = 64
FOX_WIDTH = 512
GDN_HEADS = 4
GDN_HEAD_DIM = 128
GDN_WIDTH = 512
CONV_WIDTH = 4
GDN_CHUNK = 64
XATTN_HEADS = 4
XATTN_HEAD_DIM = 128
XATTN_WIDTH = 512
D_FF = 4096
EPS = 1e-6
NEG_INF = -1e30
N_DEV = 8

ADAM_LR = 0.001
ADAM_B1 = 0.9
ADAM_B2 = 0.999
ADAM_EPS = 1e-08
ADAM_WD = 0.01
ADAM_STEP = 10

P_FOX = 0
P_GDN = 1536
P_Z = 3072
P_SMALL = 3584
P_DIM = 3712
SM_F = 0
SM_B = 8
SM_A = 12
SM_ROWS = 16

LANES = 128
VMEM_LIMIT = 56 * 1024 * 1024

NN = (((1,), (0,)), ((), ()))
NT = (((1,), (1,)), ((), ()))
TN = (((0,), (0,)), ((), ()))


def _dot(a, b, dims=NN):
    return lax.dot_general(a.astype(BF16), b.astype(BF16), dims, preferred_element_type=F32)


def _cparams(sem=None):
    kw = dict(vmem_limit_bytes=VMEM_LIMIT)
    if sem is not None:
        kw["dimension_semantics"] = sem
    return pltpu.CompilerParams(**kw)


def _sigmoid(x):
    return 0.5 * (jnp.tanh(0.5 * x) + 1.0)


def _softplus(x):
    return jnp.maximum(x, 0.0) + jnp.log1p(jnp.exp(-jnp.abs(x)))


def _log_sigmoid(x):
    return -_softplus(-x)


def _rms(x, g):
    r = lax.rsqrt(jnp.mean(x * x, axis=-1, keepdims=True) + EPS)
    return x * r * g


def _rms_bwd(x, g, dy):
    r = lax.rsqrt(jnp.mean(x * x, axis=-1, keepdims=True) + EPS)
    xh = x * r
    dg = jnp.sum(dy * xh, axis=0, keepdims=True)
    dyg = dy * g
    dx = r * (dyg - xh * jnp.mean(dyg * xh, axis=-1, keepdims=True))
    return dx, dg


def _pair_stat(t, m0):
    s0 = jnp.sum(jnp.where(m0, t, 0.0), axis=-1, keepdims=True)
    s1 = jnp.sum(jnp.where(m0, 0.0, t), axis=-1, keepdims=True)
    return jnp.where(m0, s0, s1)


def _rms_pair(x, g, m0):
    r = lax.rsqrt(_pair_stat(x * x, m0) * (1.0 / FOX_HEAD_DIM) + EPS)
    return x * r * g


def _rms_pair_bwd(x, g, dy, m0):
    r = lax.rsqrt(_pair_stat(x * x, m0) * (1.0 / FOX_HEAD_DIM) + EPS)
    xh = x * r
    dg = jnp.sum(dy * xh, axis=0, keepdims=True)
    dyg = dy * g
    dx = r * (dyg - xh * (_pair_stat(dyg * xh, m0) * (1.0 / FOX_HEAD_DIM)))
    return dx, dg


@jax.custom_vjp
def _mm_nn(a, b):
    return _dot(a, b, NN)


_mm_nn.defvjp(lambda a, b: (_dot(a, b, NN), (a, b)),
              lambda r, g: (_dot(g, r[1], NT), _dot(r[0], g, TN)))


@jax.custom_vjp
def _mm_nt(a, b):
    return _dot(a, b, NT)


_mm_nt.defvjp(lambda a, b: (_dot(a, b, NT), (a, b)),
              lambda r, g: (_dot(g, r[1], NN), _dot(g, r[0], TN)))


@jax.custom_vjp
def _mm_tn(a, b):
    return _dot(a, b, TN)


_mm_tn.defvjp(lambda a, b: (_dot(a, b, TN), (a, b)),
              lambda r, g: (_dot(r[1], g, NT), _dot(r[0], g, NN)))


def _dot3(a, b, dims):
    ah = a.astype(BF16)
    al = (a - ah.astype(F32)).astype(BF16)
    bh = b.astype(BF16)
    bl = (b - bh.astype(F32)).astype(BF16)
    d = functools.partial(lax.dot_general, dimension_numbers=dims, preferred_element_type=F32)
    return d(ah, bh) + d(ah, bl) + d(al, bh)


@jax.custom_vjp
def _mm3(a, b):
    return _dot3(a, b, NN)


_mm3.defvjp(lambda a, b: (_dot3(a, b, NN), (a, b)),
            lambda r, g: (_dot3(g, r[1], NT), _dot3(r[0], g, TN)))


def _unit_lower_inverse(a):
    c = a.shape[0]
    eye = (lax.broadcasted_iota(jnp.int32, (c, c), 0) == lax.broadcasted_iota(jnp.int32, (c, c), 1)).astype(F32)
    x = eye - a
    p = a
    k = 2
    while k < c + 1:
        p = _mm3(p, p)
        x = x + _mm3(x, p)
        k *= 2
    return x


def _wgrad(a, b, name, bk=512, bn=512, bt=512):
    t_len, k_len = a.shape
    n_len = b.shape[1]
    bk, bn, bt = min(bk, k_len), min(bn, n_len), min(bt, t_len)
    nt = t_len // bt

    def body(a_ref, b_ref, o_ref, acc_ref):
        t = pl.program_id(2)

        @pl.when(t == 0)
        def _():
            acc_ref[...] = jnp.zeros_like(acc_ref)

        acc_ref[...] += _dot(a_ref[...], b_ref[...], TN)

        @pl.when(t == nt - 1)
        def _():
            o_ref[...] = acc_ref[...]

    return pl.pallas_call(
        body, name=name, grid=(k_len // bk, n_len // bn, nt),
        in_specs=[pl.BlockSpec((bt, bk), lambda i, j, t: (t, i)), pl.BlockSpec((bt, bn), lambda i, j, t: (t, j))],
        out_specs=pl.BlockSpec((bk, bn), lambda i, j, t: (i, j)),
        out_shape=jax.ShapeDtypeStruct((k_len, n_len), F32),
        scratch_shapes=[pltpu.VMEM((bk, bn), F32)],
        compiler_params=_cparams(("parallel", "parallel", "arbitrary")),
    )(a, b)


def _rows_matmul(a, b, name, bt=512):
    r_len, t_len = a.shape
    n_len = b.shape[1]
    bt = min(bt, t_len)
    nt = t_len // bt

    def body(a_ref, b_ref, o_ref):
        t = pl.program_id(0)

        @pl.when(t == 0)
        def _():
            o_ref[...] = jnp.zeros_like(o_ref)

        o_ref[...] += _dot(a_ref[...], b_ref[...], NN)

    return pl.pallas_call(
        body, name=name, grid=(nt,),
        in_specs=[pl.BlockSpec((r_len, bt), lambda t: (0, t)), pl.BlockSpec((bt, n_len), lambda t: (t, 0))],
        out_specs=pl.BlockSpec((r_len, n_len), lambda t: (0, 0)),
        out_shape=jax.ShapeDtypeStruct((r_len, n_len), F32),
        compiler_params=_cparams(("arbitrary",)),
    )(a, b)


def _in_proj(x, g, wp, wst, tm=256):
    t_len, d = x.shape
    tm = min(tm, t_len)

    def body(x_ref, g_ref, wp_ref, wst_ref, h_ref, fox_ref, gdn_ref, z_ref, sm_ref, smt_ref):
        h = _rms(x_ref[...], g_ref[...]).astype(BF16)
        h_ref[...] = h
        p = _dot(h, wp_ref[...], NN)
        fox_ref[...] = p[:, P_FOX:P_GDN]
        gdn_ref[...] = p[:, P_GDN:P_Z]
        z_ref[...] = p[:, P_Z:P_SMALL]
        sm_ref[...] = p[:, P_SMALL:P_DIM]
        smt_ref[...] = _dot(wst_ref[...], h, NT)

    row = lambda i: (i, 0)
    fixed = lambda i: (0, 0)
    return pl.pallas_call(
        body, name="in_proj", grid=(t_len // tm,),
        in_specs=[pl.BlockSpec((tm, d), row), pl.BlockSpec((1, d), fixed), pl.BlockSpec((d, P_DIM), fixed),
                  pl.BlockSpec((SM_ROWS, d), fixed)],
        out_specs=[pl.BlockSpec((tm, d), row), pl.BlockSpec((tm, 1536), row), pl.BlockSpec((tm, 1536), row),
                   pl.BlockSpec((tm, 512), row), pl.BlockSpec((tm, LANES), row), pl.BlockSpec((SM_ROWS, tm), lambda i: (0, i))],
        out_shape=[jax.ShapeDtypeStruct((t_len, d), BF16), jax.ShapeDtypeStruct((t_len, 1536), F32),
                   jax.ShapeDtypeStruct((t_len, 1536), F32), jax.ShapeDtypeStruct((t_len, 512), F32),
                   jax.ShapeDtypeStruct((t_len, LANES), F32), jax.ShapeDtypeStruct((SM_ROWS, t_len), F32)],
        compiler_params=_cparams(("parallel",)),
    )(x, g, wp, wst)


def _in_proj_bwd(dproj, dsmt, x, g, wp, wst, dx1, tm=256):
    t_len, d = x.shape
    tm = min(tm, t_len)

    def body(dp_ref, dst_ref, x_ref, g_ref, wp_ref, wst_ref, dx1_ref, dx_ref, dg_ref):
        i = pl.program_id(0)
        dh = _dot(dp_ref[...], wp_ref[...], NT) + _dot(dst_ref[...], wst_ref[...], TN)
        dxn, dg = _rms_bwd(x_ref[...], g_ref[...], dh)
        dx_ref[...] = dx1_ref[...] + dxn

        @pl.when(i == 0)
        def _():
            dg_ref[...] = jnp.zeros_like(dg_ref)

        dg_ref[...] += dg

    row = lambda i: (i, 0)
    fixed = lambda i: (0, 0)
    return pl.pallas_call(
        body, name="in_proj_bwd", grid=(t_len // tm,),
        in_specs=[pl.BlockSpec((tm, P_DIM), row), pl.BlockSpec((SM_ROWS, tm), lambda i: (0, i)), pl.BlockSpec((tm, d), row),
                  pl.BlockSpec((1, d), fixed), pl.BlockSpec((d, P_DIM), fixed), pl.BlockSpec((SM_ROWS, d), fixed),
                  pl.BlockSpec((tm, d), row)],
        out_specs=[pl.BlockSpec((tm, d), row), pl.BlockSpec((1, d), fixed)],
        out_shape=[jax.ShapeDtypeStruct((t_len, d), F32), jax.ShapeDtypeStruct((1, d), F32)],
        compiler_params=_cparams(("arbitrary",)),
    )(dproj, dsmt, x, g, wp, wst, dx1)


def _fox_cum(smt, bias_col, n_batch, s_len, ck=256):
    ck = min(ck, s_len)

    def body(s_ref, b_ref, c_ref):
        tri = (lax.broadcasted_iota(jnp.int32, (ck, ck), 0) <= lax.broadcasted_iota(jnp.int32, (ck, ck), 1)).astype(F32)
        carry = jnp.zeros((SM_ROWS, 1), F32)
        for r in range(s_len // ck):
            ls = _log_sigmoid(s_ref[:, r * ck:(r + 1) * ck] + b_ref[...])
            c = jnp.dot(ls, tri, precision=lax.Precision.HIGHEST, preferred_element_type=F32) + carry
            c_ref[:, r * ck:(r + 1) * ck] = c
            carry = c[:, ck - 1:ck]

    return pl.pallas_call(
        body, name="fox_cum", grid=(n_batch,),
        in_specs=[pl.BlockSpec((SM_ROWS, s_len), lambda b: (0, b)), pl.BlockSpec((SM_ROWS, 1), lambda b: (0, 0))],
        out_specs=pl.BlockSpec((SM_ROWS, s_len), lambda b: (0, b)),
        out_shape=jax.ShapeDtypeStruct(smt.shape, F32),
        compiler_params=_cparams(("parallel",)),
    )(smt, bias_col)


def _fox_cum_bwd(dc, smt, bias_col, n_batch, s_len, ck=256):
    ck = min(ck, s_len)
    nr = s_len // ck

    def body(dc_ref, s_ref, b_ref, dl_ref, db_ref):
        b = pl.program_id(0)
        tri = (lax.broadcasted_iota(jnp.int32, (ck, ck), 0) >= lax.broadcasted_iota(jnp.int32, (ck, ck), 1)).astype(F32)
        carry = jnp.zeros((SM_ROWS, 1), F32)
        tot = jnp.zeros((SM_ROWS, 1), F32)
        for r in reversed(range(nr)):
            sl = slice(r * ck, (r + 1) * ck)
            dls = jnp.dot(dc_ref[:, sl], tri, precision=lax.Precision.HIGHEST, preferred_element_type=F32) + carry
            carry = dls[:, 0:1]
            dl = dls * (1.0 - _sigmoid(s_ref[:, sl] + b_ref[...]))
            dl_ref[:, sl] = dl
            tot = tot + jnp.sum(dl, axis=1, keepdims=True)

        @pl.when(b == 0)
        def _():
            db_ref[...] = jnp.zeros_like(db_ref)

        db_ref[...] += jnp.broadcast_to(tot, db_ref.shape)

    return pl.pallas_call(
        body, name="fox_cum_bwd", grid=(n_batch,),
        in_specs=[pl.BlockSpec((SM_ROWS, s_len), lambda b: (0, b)), pl.BlockSpec((SM_ROWS, s_len), lambda b: (0, b)),
                  pl.BlockSpec((SM_ROWS, 1), lambda b: (0, 0))],
        out_specs=[pl.BlockSpec((SM_ROWS, s_len), lambda b: (0, b)), pl.BlockSpec((SM_ROWS, LANES), lambda b: (0, 0))],
        out_shape=[jax.ShapeDtypeStruct(smt.shape, F32), jax.ShapeDtypeStruct((SM_ROWS, LANES), F32)],
        compiler_params=_cparams(("arbitrary",)),
    )(dc, smt, bias_col)


def _fox_masks(tq, tk, i, kb):
    qpos = i * tq + lax.broadcasted_iota(jnp.int32, (tq, tk), 0)
    kpos = kb * tk + lax.broadcasted_iota(jnp.int32, (tq, tk), 1)
    return kpos <= qpos


def _fox_fwd(pf, cb, gq2, gk2, go2, tq=256):
    n_batch, s_len, _ = pf.shape
    tq = min(tq, s_len)
    nq = s_len // tq
    scale = FOX_HEAD_DIM ** -0.5

    def body(q_ref, k_ref, v_ref, c_ref, gq_ref, gk_ref, go_ref, o_ref, on_ref, lse_ref, kh_ref, vh_ref):
        j = pl.program_id(1)
        i = pl.program_id(2)
        m0 = lax.broadcasted_iota(jnp.int32, (1, LANES), 1) < FOX_HEAD_DIM

        @pl.when(i == 0)
        def _():
            kn = _rms_pair(k_ref[0], gk_ref[...], m0)
            kh_ref[0] = jnp.where(m0, kn, 0.0).astype(BF16)
            kh_ref[1] = jnp.where(m0, 0.0, kn).astype(BF16)
            v = v_ref[0]
            vh_ref[0] = jnp.where(m0, v, 0.0).astype(BF16)
            vh_ref[1] = jnp.where(m0, 0.0, v).astype(BF16)

        qb = _rms_pair(q_ref[0], gq_ref[...], m0).astype(BF16)

        def step(kb, carry):
            ms, ls, acc = carry
            off = pl.multiple_of(kb * tq, tq)
            mask = _fox_masks(tq, tq, i, kb)
            new_m, new_l, alphas, pv = [], [], [], []
            for hh in range(2):
                s = _dot(qb, kh_ref[hh, pl.ds(off, tq), :], NT) * scale
                s = s - c_ref[0, kb, pl.ds(2 * j + hh, 1), :]
                s = jnp.where(mask, s, NEG_INF)
                m_new = jnp.maximum(ms[hh], jnp.max(s, axis=-1, keepdims=True))
                alpha = jnp.exp(ms[hh] - m_new)
                p = jnp.exp(s - m_new)
                new_l.append(alpha * ls[hh] + jnp.sum(p, axis=-1, keepdims=True))
                new_m.append(m_new)
                alphas.append(alpha)
                pv.append(_dot(p, vh_ref[hh, pl.ds(off, tq), :], NN))
            acc = jnp.where(m0, alphas[0], alphas[1]) * acc + pv[0] + pv[1]
            return tuple(new_m), tuple(new_l), acc

        init_m = (jnp.full((tq, 1), NEG_INF, F32),) * 2
        init_l = (jnp.zeros((tq, 1), F32),) * 2
        ms, ls, acc = lax.fori_loop(0, i + 1, step, (init_m, init_l, jnp.zeros((tq, LANES), F32)))
        o = acc / jnp.where(m0, ls[0], ls[1])
        o_ref[0] = o
        on_ref[0] = _rms_pair(o, go_ref[...], m0).astype(BF16)
        lse_ref[0] = jnp.where(m0, ms[0] + jnp.log(ls[0]), ms[1] + jnp.log(ls[1]))

    fixed = lambda b, j, i: (0, 0)
    tile = lambda b, j, i: (b, i, j)
    return pl.pallas_call(
        body, name="fox_fwd", grid=(n_batch, 4, nq),
        in_specs=[pl.BlockSpec((1, tq, LANES), tile), pl.BlockSpec((1, s_len, LANES), lambda b, j, i: (b, 0, 4 + j)),
                  pl.BlockSpec((1, s_len, LANES), lambda b, j, i: (b, 0, 8 + j)),
                  pl.BlockSpec((1, nq, SM_ROWS, tq), lambda b, j, i: (b, 0, 0, 0)),
                  pl.BlockSpec((1, LANES), fixed), pl.BlockSpec((1, LANES), fixed), pl.BlockSpec((1, LANES), fixed)],
        out_specs=[pl.BlockSpec((1, tq, LANES), tile), pl.BlockSpec((1, tq, LANES), tile), pl.BlockSpec((1, tq, LANES), tile)],
        out_shape=[jax.ShapeDtypeStruct((n_batch, s_len, FOX_WIDTH), F32), jax.ShapeDtypeStruct((n_batch, s_len, FOX_WIDTH), BF16),
                   jax.ShapeDtypeStruct((n_batch, s_len, FOX_WIDTH), F32)],
        scratch_shapes=[pltpu.VMEM((2, s_len, LANES), BF16), pltpu.VMEM((2, s_len, LANES), BF16)],
        compiler_params=_cparams(("parallel", "parallel", "arbitrary")),
    )(pf, pf, pf, cb, gq2, gk2, go2)


def _fox_bwd(pf, cb, gq2, gk2, go2, o, lse, don, tq=256):
    n_batch, s_len, _ = pf.shape
    tq = min(tq, s_len)
    nq = s_len // tq
    scale = FOX_HEAD_DIM ** -0.5

    def body(q_ref, k_ref, v_ref, c_ref, gq_ref, gk_ref, go_ref, o_ref, lse_ref, don_ref,
             dq_ref, dk_ref, dv_ref, dc_ref, dgq_ref, dgk_ref, dgo_ref, kh_ref, vh_ref, dka_ref, dva_ref, dca_ref):
        b = pl.program_id(0)
        j = pl.program_id(1)
        i = pl.program_id(2)
        m0 = lax.broadcasted_iota(jnp.int32, (1, LANES), 1) < FOX_HEAD_DIM

        @pl.when((b == 0) & (j == 0) & (i == 0))
        def _():
            dgq_ref[...] = jnp.zeros_like(dgq_ref)
            dgk_ref[...] = jnp.zeros_like(dgk_ref)
            dgo_ref[...] = jnp.zeros_like(dgo_ref)

        @pl.when(i == 0)
        def _():
            kn = _rms_pair(k_ref[0], gk_ref[...], m0)
            kh_ref[0] = jnp.where(m0, kn, 0.0).astype(BF16)
            kh_ref[1] = jnp.where(m0, 0.0, kn).astype(BF16)
            v = v_ref[0]
            vh_ref[0] = jnp.where(m0, v, 0.0).astype(BF16)
            vh_ref[1] = jnp.where(m0, 0.0, v).astype(BF16)
            dka_ref[...] = jnp.zeros_like(dka_ref)
            dva_ref[...] = jnp.zeros_like(dva_ref)
            dca_ref[...] = jnp.zeros_like(dca_ref)

        q = q_ref[0]
        qn = _rms_pair(q, gq_ref[...], m0)
        qb = qn.astype(BF16)
        qh = (jnp.where(m0, qn, 0.0).astype(BF16), jnp.where(m0, 0.0, qn).astype(BF16))
        ot = o_ref[0]
        do, dgo = _rms_pair_bwd(ot, go_ref[...], don_ref[0], m0)
        dgo_ref[...] += dgo
        dd = do * ot
        delta = (jnp.sum(jnp.where(m0, dd, 0.0), axis=-1, keepdims=True), jnp.sum(jnp.where(m0, 0.0, dd), axis=-1, keepdims=True))
        doh = (jnp.where(m0, do, 0.0).astype(BF16), jnp.where(m0, 0.0, do).astype(BF16))
        lse_t = lse_ref[0]
        lse_h = (lse_t[:, 0:1], lse_t[:, FOX_HEAD_DIM:FOX_HEAD_DIM + 1])

        def step(kb, carry):
            dqn, rs = carry
            rs = list(rs)
            off = pl.multiple_of(kb * tq, tq)
            mask = _fox_masks(tq, tq, i, kb)
            for hh in range(2):
                kblk = kh_ref[hh, pl.ds(off, tq), :]
                vblk = vh_ref[hh, pl.ds(off, tq), :]
                s = _dot(qb, kblk, NT) * scale
                s = s - c_ref[0, kb, pl.ds(2 * j + hh, 1), :]
                s = jnp.where(mask, s, NEG_INF)
                p = jnp.exp(s - lse_h[hh])
                dp = _dot(doh[hh], vblk, NT)
                ds = p * (dp - delta[hh])
                dva_ref[pl.ds(off, tq), :] += _dot(p, doh[hh], TN)
                dka_ref[pl.ds(off, tq), :] += _dot(ds, qh[hh], TN) * scale
                dca_ref[kb, hh:hh + 1, :] += -jnp.sum(ds, axis=0, keepdims=True)
                rs[hh] = rs[hh] + jnp.sum(ds, axis=-1, keepdims=True)
                dqn = dqn + _dot(ds, kblk, NN) * scale
            return dqn, tuple(rs)

        dqn, rs = lax.fori_loop(0, i + 1, step, (jnp.zeros((tq, LANES), F32), (jnp.zeros((tq, 1), F32),) * 2))
        rs_rows = jnp.where(m0, rs[0], rs[1]).T
        dca_ref[i, 0:1, :] += rs_rows[0:1, :]
        dca_ref[i, 1:2, :] += rs_rows[FOX_HEAD_DIM:FOX_HEAD_DIM + 1, :]
        dq, dgq = _rms_pair_bwd(q, gq_ref[...], dqn, m0)
        dq_ref[0] = dq.astype(BF16)
        dgq_ref[...] += dgq

        @pl.when(i == nq - 1)
        def _():
            dk, dgk = _rms_pair_bwd(k_ref[0], gk_ref[...], dka_ref[...], m0)
            dk_ref[0] = dk.astype(BF16)
            dgk_ref[...] += dgk
            dv_ref[0] = dva_ref[...].astype(BF16)
            dc_ref[0, 0] = dca_ref[...]

    fixed = lambda b, j, i: (0, 0)
    tile = lambda b, j, i: (b, i, j)
    full = lambda b, j, i: (b, 0, j)
    wide = jax.ShapeDtypeStruct((n_batch, s_len, FOX_WIDTH), BF16)
    gain = jax.ShapeDtypeStruct((1, LANES), F32)
    return pl.pallas_call(
        body, name="fox_bwd", grid=(n_batch, 4, nq),
        in_specs=[pl.BlockSpec((1, tq, LANES), tile), pl.BlockSpec((1, s_len, LANES), lambda b, j, i: (b, 0, 4 + j)),
                  pl.BlockSpec((1, s_len, LANES), lambda b, j, i: (b, 0, 8 + j)),
                  pl.BlockSpec((1, nq, SM_ROWS, tq), lambda b, j, i: (b, 0, 0, 0)),
                  pl.BlockSpec((1, LANES), fixed), pl.BlockSpec((1, LANES), fixed), pl.BlockSpec((1, LANES), fixed),
                  pl.BlockSpec((1, tq, LANES), tile), pl.BlockSpec((1, tq, LANES), tile), pl.BlockSpec((1, tq, LANES), tile)],
        out_specs=[pl.BlockSpec((1, tq, LANES), tile), pl.BlockSpec((1, s_len, LANES), full), pl.BlockSpec((1, s_len, LANES), full),
                   pl.BlockSpec((1, 1, nq, 8, tq), lambda b, j, i: (b, j, 0, 0, 0)),
                   pl.BlockSpec((1, LANES), fixed), pl.BlockSpec((1, LANES), fixed), pl.BlockSpec((1, LANES), fixed)],
        out_shape=[wide, wide, wide, jax.ShapeDtypeStruct((n_batch, 4, nq, 8, tq), F32), gain, gain, gain],
        scratch_shapes=[pltpu.VMEM((2, s_len, LANES), BF16), pltpu.VMEM((2, s_len, LANES), BF16),
                        pltpu.VMEM((s_len, LANES), F32), pltpu.VMEM((s_len, LANES), F32), pltpu.VMEM((nq, 8, tq), F32)],
        compiler_params=_cparams(("arbitrary", "arbitrary", "arbitrary")),
    )(pf, pf, pf, cb, gq2, gk2, go2, o, lse, don)


def _shift_down(x, k):
    row = lax.broadcasted_iota(jnp.int32, x.shape, 0)
    return jnp.where(row >= k, pltpu.roll(x, k, 0), 0.0)


def _shift_up(x, k):
    n = x.shape[0]
    row = lax.broadcasted_iota(jnp.int32, x.shape, 0)
    return jnp.where(row < n - k, pltpu.roll(x, n - k, 0), 0.0)


def _conv_silu(x, w):
    y = w[3:4] * x + w[2:3] * _shift_down(x, 1) + w[1:2] * _shift_down(x, 2) + w[0:1] * _shift_down(x, 3)
    return y, y * _sigmoid(y)


def _gdn_pre(pg, conv_w):
    n_batch, s_len, width = pg.shape
    ncb = width // LANES

    def body(x_ref, w_ref, o_ref):
        cb = pl.program_id(1)
        _, s = _conv_silu(x_ref[0], w_ref[...])
        sn = s * lax.rsqrt(jnp.sum(s * s, axis=-1, keepdims=True) + EPS)
        o_ref[0] = jnp.where(cb < 2 * GDN_HEADS, sn, s)

    return pl.pallas_call(
        body, name="gdn_pre", grid=(n_batch, ncb),
        in_specs=[pl.BlockSpec((1, s_len, LANES), lambda b, c: (b, 0, c)), pl.BlockSpec((8, LANES), lambda b, c: (0, c))],
        out_specs=pl.BlockSpec((1, s_len, LANES), lambda b, c: (b, 0, c)),
        out_shape=jax.ShapeDtypeStruct(pg.shape, F32),
        compiler_params=_cparams(("parallel", "parallel")),
    )(pg, conv_w)


def _gdn_pre_bwd(pg, conv_w, dout):
    n_batch, s_len, width = pg.shape
    ncb = width // LANES

    def body(x_ref, w_ref, d_ref, dx_ref, dw_ref):
        cb = pl.program_id(0)
        b = pl.program_id(1)
        x = x_ref[0]
        w = w_ref[...]
        d = d_ref[0]
        y, s = _conv_silu(x, w)
        rr = lax.rsqrt(jnp.sum(s * s, axis=-1, keepdims=True) + EPS)
        sn = s * rr
        ds_n = rr * (d - sn * jnp.sum(d * sn, axis=-1, keepdims=True))
        ds = jnp.where(cb < 2 * GDN_HEADS, ds_n, d)
        sig = _sigmoid(y)
        dy = ds * (sig * (1.0 + y * (1.0 - sig)))
        dx = w[3:4] * dy + w[2:3] * _shift_up(dy, 1) + w[1:2] * _shift_up(dy, 2) + w[0:1] * _shift_up(dy, 3)
        dx_ref[0] = dx.astype(BF16)
        dw = [jnp.sum(dy * _shift_down(x, 3 - jj), axis=0, keepdims=True) if jj < 3 else jnp.sum(dy * x, axis=0, keepdims=True)
              for jj in range(CONV_WIDTH)]
        rows = lax.broadcasted_iota(jnp.int32, (8, LANES), 0)
        dwb = jnp.zeros((8, LANES), F32)
        for jj in range(CONV_WIDTH):
            dwb = dwb + jnp.where(rows == jj, dw[jj], 0.0)

        @pl.when(b == 0)
        def _():
            dw_ref[...] = jnp.zeros_like(dw_ref)

        dw_ref[...] += dwb

    blk = lambda c, b: (b, 0, c)
    return pl.pallas_call(
        body, name="gdn_pre_bwd", grid=(ncb, n_batch),
        in_specs=[pl.BlockSpec((1, s_len, LANES), blk), pl.BlockSpec((8, LANES), lambda c, b: (0, c)), pl.BlockSpec((1, s_len, LANES), blk)],
        out_specs=[pl.BlockSpec((1, s_len, LANES), blk), pl.BlockSpec((8, LANES), lambda c, b: (0, c))],
        out_shape=[jax.ShapeDtypeStruct(pg.shape, BF16), jax.ShapeDtypeStruct((8, width), F32)],
        compiler_params=_cparams(("parallel", "arbitrary")),
    )(pg, conv_w, dout)


def _gdn_gates(smc, smr, a_c, dt_c, a_r, dt_r, h):
    lane = lax.broadcasted_iota(jnp.int32, (1, LANES), 1)
    sub = lax.broadcasted_iota(jnp.int32, (SM_ROWS, 1), 0)
    beta_c = jnp.sum(jnp.where(lane == SM_B + h, _sigmoid(smc), 0.0), axis=1, keepdims=True)
    g_all_c = -jnp.exp(a_c) * _softplus(smc + dt_c)
    g_c = jnp.sum(jnp.where(lane == SM_A + h, g_all_c, 0.0), axis=1, keepdims=True)
    g_all_r = -jnp.exp(a_r) * _softplus(smr + dt_r)
    g_r = jnp.sum(jnp.where(sub == SM_A + h, g_all_r, 0.0), axis=0, keepdims=True)
    return beta_c, g_c, g_r


def _gdn_head(q, k, v, z, beta_c, g_c, g_r, go, s0):
    c = q.shape[0]
    ii = lax.broadcasted_iota(jnp.int32, (c, c), 0)
    jj = lax.broadcasted_iota(jnp.int32, (c, c), 1)
    incl = ii >= jj
    gc_c = jnp.sum(jnp.where(incl, g_r, 0.0), axis=1, keepdims=True)
    gc_r = jnp.sum(jnp.where(ii <= jj, g_c, 0.0), axis=0, keepdims=True)
    decay = jnp.where(incl, jnp.exp(jnp.where(incl, gc_c - gc_r, 0.0)), 0.0)
    qs = q * (GDN_HEAD_DIM ** -0.5)
    kb = k * beta_c
    vb = v * beta_c
    a = jnp.where(ii > jj, _mm_nt(kb, k) * decay, 0.0)
    t = _unit_lower_inverse(a)
    egc = jnp.exp(gc_c)
    u = _mm_nn(t, vb)
    w = _mm_nn(t, kb * egc)
    intra = _mm_nt(qs, k) * decay
    v_new = u - _mm_nn(w, s0)
    o = _mm_nn(qs * egc, s0) + _mm_nn(intra, v_new)
    g_last = jnp.sum(g_c, axis=0, keepdims=True)
    s1 = s0 * jnp.exp(g_last) + _mm_tn(k * jnp.exp(g_last - gc_c), v_new)
    og = _rms(o, go) * (z * _sigmoid(z))
    return og, s1


def _gdn_chunk_all(qkv, z, smc, smr, a_c, dt_c, a_r, dt_r, go, states):
    outs, nxt = [], []
    for h in range(GDN_HEADS):
        beta_c, g_c, g_r = _gdn_gates(smc, smr, a_c, dt_c, a_r, dt_r, h)
        sl = lambda base: slice(base + h * GDN_HEAD_DIM, base + (h + 1) * GDN_HEAD_DIM)
        og, s1 = _gdn_head(qkv[:, sl(0)], qkv[:, sl(GDN_WIDTH)], qkv[:, sl(2 * GDN_WIDTH)], z[:, sl(0)],
                           beta_c, g_c, g_r, go, states[h])
        outs.append(og)
        nxt.append(s1)
    return outs, nxt


def _gdn_fwd(qkvn, z, smc, smr, a_c, dt_c, a_r, dt_r, go):
    n_batch, s_len, _ = qkvn.shape
    c = GDN_CHUNK
    n = s_len // c
    hd = GDN_HEAD_DIM

    def body(qkv_ref, z_ref, smc_ref, smr_ref, ac_ref, dc_ref, ar_ref, dr_ref, go_ref, og_ref, st_ref, s_ref):
        @pl.when(pl.program_id(1) == 0)
        def _():
            s_ref[...] = jnp.zeros_like(s_ref)

        states = [s_ref[h] for h in range(GDN_HEADS)]
        for h in range(GDN_HEADS):
            st_ref[0, 0, h] = states[h]
        outs, nxt = _gdn_chunk_all(qkv_ref[0], z_ref[0], smc_ref[0], smr_ref[0], ac_ref[...], dc_ref[...], ar_ref[...],
                                   dr_ref[...], go_ref[...], states)
        for h in range(GDN_HEADS):
            og_ref[0, :, h * hd:(h + 1) * hd] = outs[h].astype(BF16)
            s_ref[h] = nxt[h]

    tok = lambda b, i: (b, i, 0)
    fixed = lambda b, i: (0, 0)
    return pl.pallas_call(
        body, name="gdn_fwd", grid=(n_batch, n),
        in_specs=[pl.BlockSpec((1, c, 3 * GDN_WIDTH), tok), pl.BlockSpec((1, c, GDN_WIDTH), tok), pl.BlockSpec((1, c, LANES), tok),
                  pl.BlockSpec((1, SM_ROWS, c), lambda b, i: (b * n + i, 0, 0)),
                  pl.BlockSpec((1, LANES), fixed), pl.BlockSpec((1, LANES), fixed), pl.BlockSpec((SM_ROWS, 1), fixed),
                  pl.BlockSpec((SM_ROWS, 1), fixed), pl.BlockSpec((1, LANES), fixed)],
        out_specs=[pl.BlockSpec((1, c, GDN_WIDTH), tok), pl.BlockSpec((1, 1, GDN_HEADS, hd, hd), lambda b, i: (b, i, 0, 0, 0))],
        out_shape=[jax.ShapeDtypeStruct((n_batch, s_len, GDN_WIDTH), BF16), jax.ShapeDtypeStruct((n_batch, n, GDN_HEADS, hd, hd), F32)],
        scratch_shapes=[pltpu.VMEM((GDN_HEADS, hd, hd), F32)],
        compiler_params=_cparams(("parallel", "arbitrary")),
    )(qkvn, z, smc, smr, a_c, dt_c, a_r, dt_r, go)


def _gdn_bwd(qkvn, z, smc, smr, a_c, dt_c, a_r, dt_r, go, states, dog):
    n_batch, s_len, _ = qkvn.shape
    c = GDN_CHUNK
    n = s_len // c
    hd = GDN_HEAD_DIM

    def body(qkv_ref, z_ref, smc_ref, smr_ref, ac_ref, dc_ref, ar_ref, dr_ref, go_ref, st_ref, dog_ref,
             dqkv_ref, dz_ref, dsmc_ref, dsmr_ref, dac_ref, ddc_ref, dar_ref, ddr_ref, dgo_ref, ds_ref):
        first = (pl.program_id(0) == 0) & (pl.program_id(1) == 0)

        @pl.when(pl.program_id(1) == 0)
        def _():
            ds_ref[...] = jnp.zeros_like(ds_ref)

        @pl.when(first)
        def _():
            for r in (dac_ref, ddc_ref, dar_ref, ddr_ref, dgo_ref):
                r[...] = jnp.zeros_like(r)

        states = [st_ref[0, 0, h] for h in range(GDN_HEADS)]
        prim = (qkv_ref[0], z_ref[0], smc_ref[0], smr_ref[0], ac_ref[...], dc_ref[...], ar_ref[...], dr_ref[...], go_ref[...], states)
        _, vjp = jax.vjp(_gdn_chunk_all, *prim)
        d_out = dog_ref[0]
        cot = ([d_out[:, h * hd:(h + 1) * hd] for h in range(GDN_HEADS)], [ds_ref[h] for h in range(GDN_HEADS)])
        dqkv, dz, dsmc, dsmr, dac, ddc, dar, ddr, dgo, dstates = vjp(cot)
        dqkv_ref[0] = dqkv
        dz_ref[0] = dz.astype(BF16)
        dsmc_ref[0] = dsmc
        dsmr_ref[0] = dsmr
        dac_ref[...] += dac
        ddc_ref[...] += ddc
        dar_ref[...] += dar
        ddr_ref[...] += ddr
        dgo_ref[...] += dgo
        for h in range(GDN_HEADS):
            ds_ref[h] = dstates[h]

    tok = lambda b, i: (b, n - 1 - i, 0)
    fixed = lambda b, i: (0, 0)
    lane_vec = jax.ShapeDtypeStruct((1, LANES), F32)
    row_vec = jax.ShapeDtypeStruct((SM_ROWS, 1), F32)
    return pl.pallas_call(
        body, name="gdn_bwd", grid=(n_batch, n),
        in_specs=[pl.BlockSpec((1, c, 3 * GDN_WIDTH), tok), pl.BlockSpec((1, c, GDN_WIDTH), tok), pl.BlockSpec((1, c, LANES), tok),
                  pl.BlockSpec((1, SM_ROWS, c), lambda b, i: (b * n + n - 1 - i, 0, 0)),
                  pl.BlockSpec((1, LANES), fixed), pl.BlockSpec((1, LANES), fixed), pl.BlockSpec((SM_ROWS, 1), fixed),
                  pl.BlockSpec((SM_ROWS, 1), fixed), pl.BlockSpec((1, LANES), fixed),
                  pl.BlockSpec((1, 1, GDN_HEADS, hd, hd), lambda b, i: (b, n - 1 - i, 0, 0, 0)),
                  pl.BlockSpec((1, c, GDN_WIDTH), lambda b, i: (b, n - 1 - i, 1))],
        out_specs=[pl.BlockSpec((1, c, 3 * GDN_WIDTH), tok), pl.BlockSpec((1, c, GDN_WIDTH), tok), pl.BlockSpec((1, c, LANES), tok),
                   pl.BlockSpec((1, SM_ROWS, c), lambda b, i: (b * n + n - 1 - i, 0, 0)),
                   pl.BlockSpec((1, LANES), fixed), pl.BlockSpec((1, LANES), fixed), pl.BlockSpec((SM_ROWS, 1), fixed),
                   pl.BlockSpec((SM_ROWS, 1), fixed), pl.BlockSpec((1, LANES), fixed)],
        out_shape=[jax.ShapeDtypeStruct((n_batch, s_len, 3 * GDN_WIDTH), F32), jax.ShapeDtypeStruct((n_batch, s_len, GDN_WIDTH), BF16),
                   jax.ShapeDtypeStruct((n_batch, s_len, LANES), F32), jax.ShapeDtypeStruct((n_batch * n, SM_ROWS, c), F32),
                   lane_vec, lane_vec, row_vec, row_vec, lane_vec],
        scratch_shapes=[pltpu.VMEM((GDN_HEADS, hd, hd), F32)],
        compiler_params=_cparams(("arbitrary", "arbitrary")),
    )(qkvn, z, smc, smr, a_c, dt_c, a_r, dt_r, go, states, dog)


def _out_proj(x, oa, ob, w_out, g_x, w_cq, tm=256):
    t_len, d = x.shape
    tm = min(tm, t_len)

    def body(x_ref, oa_ref, ob_ref, wo_ref, g_ref, wq_ref, x1_ref, hq_ref, cq_ref):
        x1 = x_ref[...] + _dot(oa_ref[...], wo_ref[0:FOX_WIDTH, :]) + _dot(ob_ref[...], wo_ref[FOX_WIDTH:2 * FOX_WIDTH, :])
        x1_ref[...] = x1
        hq = _rms(x1, g_ref[...]).astype(BF16)
        hq_ref[...] = hq
        cq_ref[...] = _dot(hq, wq_ref[...])

    row = lambda i: (i, 0)
    fixed = lambda i: (0, 0)
    return pl.pallas_call(
        body, name="out_proj", grid=(t_len // tm,),
        in_specs=[pl.BlockSpec((tm, d), row), pl.BlockSpec((tm, FOX_WIDTH), row), pl.BlockSpec((tm, GDN_WIDTH), row),
                  pl.BlockSpec((d, d), fixed), pl.BlockSpec((1, d), fixed), pl.BlockSpec((d, XATTN_WIDTH), fixed)],
        out_specs=[pl.BlockSpec((tm, d), row), pl.BlockSpec((tm, d), row), pl.BlockSpec((tm, XATTN_WIDTH), row)],
        out_shape=[jax.ShapeDtypeStruct((t_len, d), F32), jax.ShapeDtypeStruct((t_len, d), BF16), jax.ShapeDtypeStruct((t_len, XATTN_WIDTH), F32)],
        compiler_params=_cparams(("parallel",)),
    )(x, oa, ob, w_out, g_x, w_cq)


def _out_proj_bwd(dx1, w_out, tm=512):
    t_len, d = dx1.shape
    tm = min(tm, t_len)

    def body(dx_ref, w_ref, o_ref):
        o_ref[...] = _dot(dx_ref[...], w_ref[...], NT)

    return pl.pallas_call(
        body, name="out_proj_bwd", grid=(t_len // tm,),
        in_specs=[pl.BlockSpec((tm, d), lambda i: (i, 0)), pl.BlockSpec((d, d), lambda i: (0, 0))],
        out_specs=pl.BlockSpec((tm, d), lambda i: (i, 0)),
        out_shape=jax.ShapeDtypeStruct((t_len, d), F32),
        compiler_params=_cparams(("parallel",)),
    )(dx1, w_out)


def _mem_kv(mem, g, w_ckv, tm=256):
    t_len, d = mem.shape
    tm = min(tm, t_len)

    def body(x_ref, g_ref, w_ref, h_ref, o_ref):
        h = _rms(x_ref[...], g_ref[...]).astype(BF16)
        h_ref[...] = h
        o_ref[...] = _dot(h, w_ref[...])

    row = lambda i: (i, 0)
    fixed = lambda i: (0, 0)
    return pl.pallas_call(
        body, name="mem_kv", grid=(t_len // tm,),
        in_specs=[pl.BlockSpec((tm, d), row), pl.BlockSpec((1, d), fixed), pl.BlockSpec((d, 2 * XATTN_WIDTH), fixed)],
        out_specs=[pl.BlockSpec((tm, d), row), pl.BlockSpec((tm, 2 * XATTN_WIDTH), row)],
        out_shape=[jax.ShapeDtypeStruct((t_len, d), BF16), jax.ShapeDtypeStruct((t_len, 2 * XATTN_WIDTH), F32)],
        compiler_params=_cparams(("parallel",)),
    )(mem, g, w_ckv)


def _mem_kv_bwd(dckv, mem, g, w_ckv, tm=256):
    t_len, d = mem.shape
    tm = min(tm, t_len)

    def body(d_ref, x_ref, g_ref, w_ref, dg_ref):
        @pl.when(pl.program_id(0) == 0)
        def _():
            dg_ref[...] = jnp.zeros_like(dg_ref)

        dh = _dot(d_ref[...], w_ref[...], NT)
        _, dg = _rms_bwd(x_ref[...], g_ref[...], dh)
        dg_ref[...] += dg

    row = lambda i: (i, 0)
    fixed = lambda i: (0, 0)
    return pl.pallas_call(
        body, name="mem_kv_bwd", grid=(t_len // tm,),
        in_specs=[pl.BlockSpec((tm, 2 * XATTN_WIDTH), row), pl.BlockSpec((tm, d), row), pl.BlockSpec((1, d), fixed),
                  pl.BlockSpec((d, 2 * XATTN_WIDTH), fixed)],
        out_specs=pl.BlockSpec((1, d), fixed),
        out_shape=jax.ShapeDtypeStruct((1, d), F32),
        compiler_params=_cparams(("arbitrary",)),
    )(dckv, mem, g, w_ckv)


def _xattn_probs(qn, kn):
    s = _dot(qn, kn, NT) * (XATTN_HEAD_DIM ** -0.5)
    p = jnp.exp(s - jnp.max(s, axis=-1, keepdims=True))
    return p / jnp.sum(p, axis=-1, keepdims=True)


def _xattn_fwd(cq, ckv, x1, gq, gk, w_co, g_mlp, n_batch, s_len, m_len, tq=256):
    d = x1.shape[1]
    tq = min(tq, s_len)
    nq = s_len // tq
    hd = XATTN_HEAD_DIM

    def body(cq_ref, kv_ref, x1_ref, gq_ref, gk_ref, wo_ref, gm_ref, co_ref, x2_ref, hf_ref):
        outs = []
        for h in range(XATTN_HEADS):
            qn = _rms(cq_ref[:, h * hd:(h + 1) * hd], gq_ref[...])
            kn = _rms(kv_ref[:, h * hd:(h + 1) * hd], gk_ref[...])
            p = _xattn_probs(qn, kn)
            outs.append(_dot(p, kv_ref[:, XATTN_WIDTH + h * hd:XATTN_WIDTH + (h + 1) * hd]).astype(BF16))
        x2 = x1_ref[...]
        for h in range(XATTN_HEADS):
            co_ref[:, h * hd:(h + 1) * hd] = outs[h]
            x2 = x2 + _dot(outs[h], wo_ref[h * hd:(h + 1) * hd, :])
        x2_ref[...] = x2
        hf_ref[...] = _rms(x2, gm_ref[...]).astype(BF16)

    row = lambda b, i: (b * nq + i, 0)
    fixed = lambda b, i: (0, 0)
    t_len = n_batch * s_len
    return pl.pallas_call(
        body, name="xattn_fwd", grid=(n_batch, nq),
        in_specs=[pl.BlockSpec((tq, XATTN_WIDTH), row), pl.BlockSpec((m_len, 2 * XATTN_WIDTH), lambda b, i: (b, 0)),
                  pl.BlockSpec((tq, d), row), pl.BlockSpec((1, hd), fixed), pl.BlockSpec((1, hd), fixed),
                  pl.BlockSpec((XATTN_WIDTH, d), fixed), pl.BlockSpec((1, d), fixed)],
        out_specs=[pl.BlockSpec((tq, XATTN_WIDTH), row), pl.BlockSpec((tq, d), row), pl.BlockSpec((tq, d), row)],
        out_shape=[jax.ShapeDtypeStruct((t_len, XATTN_WIDTH), BF16), jax.ShapeDtypeStruct((t_len, d), F32),
                   jax.ShapeDtypeStruct((t_len, d), BF16)],
        compiler_params=_cparams(("parallel", "parallel")),
    )(cq, ckv, x1, gq, gk, w_co, g_mlp)


def _xattn_bwd(dx2, cq, ckv, x1, gq, gk, w_co, g_x, w_cq, n_batch, s_len, m_len, tq=256):
    d = x1.shape[1]
    tq = min(tq, s_len)
    nq = s_len // tq
    hd = XATTN_HEAD_DIM
    scale = XATTN_HEAD_DIM ** -0.5

    def body(dx2_ref, cq_ref, kv_ref, x1_ref, gq_ref, gk_ref, wo_ref, gx_ref, wq_ref,
             dx1_ref, dcq_ref, dkv_ref, dgq_ref, dgk_ref, dgx_ref, dk_acc, dv_acc):
        b = pl.program_id(0)
        i = pl.program_id(1)

        @pl.when((b == 0) & (i == 0))
        def _():
            dgq_ref[...] = jnp.zeros_like(dgq_ref)
            dgk_ref[...] = jnp.zeros_like(dgk_ref)
            dgx_ref[...] = jnp.zeros_like(dgx_ref)

        @pl.when(i == 0)
        def _():
            dk_acc[...] = jnp.zeros_like(dk_acc)
            dv_acc[...] = jnp.zeros_like(dv_acc)

        dx2 = dx2_ref[...]
        dhq = jnp.zeros((tq, d), F32)
        for h in range(XATTN_HEADS):
            sl = slice(h * hd, (h + 1) * hd)
            q = cq_ref[:, sl]
            qn = _rms(q, gq_ref[...])
            kn = _rms(kv_ref[:, sl], gk_ref[...])
            v = kv_ref[:, XATTN_WIDTH + h * hd:XATTN_WIDTH + (h + 1) * hd]
            p = _xattn_probs(qn, kn)
            dco = _dot(dx2, wo_ref[sl, :], NT)
            dv_acc[:, sl] += _dot(p, dco, TN)
            dp = _dot(dco, v, NT)
            ds = p * (dp - jnp.sum(dp * p, axis=-1, keepdims=True))
            dqn = _dot(ds, kn) * scale
            dk_acc[:, sl] += _dot(ds, qn, TN) * scale
            dq, dgq = _rms_bwd(q, gq_ref[...], dqn)
            dgq_ref[...] += dgq
            dqb = dq.astype(BF16)
            dcq_ref[:, sl] = dqb
            dhq = dhq + _dot(dqb, wq_ref[:, sl], NT)
        dxn, dgx = _rms_bwd(x1_ref[...], gx_ref[...], dhq)
        dgx_ref[...] += dgx
        dx1_ref[...] = dx2 + dxn

        @pl.when(i == nq - 1)
        def _():
            for h in range(XATTN_HEADS):
                sl = slice(h * hd, (h + 1) * hd)
                dk, dgk = _rms_bwd(kv_ref[:, sl], gk_ref[...], dk_acc[:, sl])
                dgk_ref[...] += dgk
                dkv_ref[:, sl] = dk.astype(BF16)
                dkv_ref[:, XATTN_WIDTH + h * hd:XATTN_WIDTH + (h + 1) * hd] = dv_acc[:, sl].astype(BF16)

    row = lambda b, i: (b * nq + i, 0)
    fixed = lambda b, i: (0, 0)
    t_len = n_batch * s_len
    return pl.pallas_call(
        body, name="xattn_bwd", grid=(n_batch, nq),
        in_specs=[pl.BlockSpec((tq, d), row), pl.BlockSpec((tq, XATTN_WIDTH), row), pl.BlockSpec((m_len, 2 * XATTN_WIDTH), lambda b, i: (b, 0)),
                  pl.BlockSpec((tq, d), row), pl.BlockSpec((1, hd), fixed), pl.BlockSpec((1, hd), fixed),
                  pl.BlockSpec((XATTN_WIDTH, d), fixed), pl.BlockSpec((1, d), fixed), pl.BlockSpec((d, XATTN_WIDTH), fixed)],
        out_specs=[pl.BlockSpec((tq, d), row), pl.BlockSpec((tq, XATTN_WIDTH), row), pl.BlockSpec((m_len, 2 * XATTN_WIDTH), lambda b, i: (b, 0)),
                   pl.BlockSpec((1, hd), fixed), pl.BlockSpec((1, hd), fixed), pl.BlockSpec((1, d), fixed)],
        out_shape=[jax.ShapeDtypeStruct((t_len, d), F32), jax.ShapeDtypeStruct((t_len, XATTN_WIDTH), BF16),
                   jax.ShapeDtypeStruct((n_batch * m_len, 2 * XATTN_WIDTH), BF16),
                   jax.ShapeDtypeStruct((1, hd), F32), jax.ShapeDtypeStruct((1, hd), F32), jax.ShapeDtypeStruct((1, d), F32)],
        scratch_shapes=[pltpu.VMEM((m_len, XATTN_WIDTH), F32), pltpu.VMEM((m_len, XATTN_WIDTH), F32)],
        compiler_params=_cparams(("arbitrary", "arbitrary")),
    )(dx2, cq, ckv, x1, gq, gk, w_co, g_x, w_cq)


def _mlp_fwd(hf, x2, target, w1, w2, tm=512, tf=1024):
    t_len, d = x2.shape
    f = w1.shape[1]
    tm, tf = min(tm, t_len), min(tf, f)
    nf = f // tf

    def body(hf_ref, x2_ref, tg_ref, w1_ref, w2_ref, u_ref, a_ref, dy_ref, ls_ref, acc_ref):
        k = pl.program_id(1)

        @pl.when(k == 0)
        def _():
            acc_ref[...] = x2_ref[...]

        u = _dot(hf_ref[...], w1_ref[...])
        u_ref[...] = u
        r = jnp.maximum(u, 0.0)
        a = (r * r).astype(BF16)
        a_ref[...] = a
        acc_ref[...] += _dot(a, w2_ref[...])

        @pl.when(k == nf - 1)
        def _():
            err = acc_ref[...] - tg_ref[...]
            dy_ref[...] = err * (1.0 / d)
            ls_ref[...] = jnp.broadcast_to(jnp.sum(jnp.sum(err * err, axis=-1, keepdims=True) * (1.0 / d), axis=0, keepdims=True), ls_ref.shape)

    row = lambda i, k: (i, 0)
    return pl.pallas_call(
        body, name="mlp_fwd", grid=(t_len // tm, nf),
        in_specs=[pl.BlockSpec((tm, d), row), pl.BlockSpec((tm, d), row), pl.BlockSpec((tm, d), row),
                  pl.BlockSpec((d, tf), lambda i, k: (0, k)), pl.BlockSpec((tf, d), lambda i, k: (k, 0))],
        out_specs=[pl.BlockSpec((tm, tf), lambda i, k: (i, k)), pl.BlockSpec((tm, tf), lambda i, k: (i, k)), pl.BlockSpec((tm, d), row),
                   pl.BlockSpec((1, 8, LANES), lambda i, k: (i, 0, 0))],
        out_shape=[jax.ShapeDtypeStruct((t_len, f), F32), jax.ShapeDtypeStruct((t_len, f), BF16), jax.ShapeDtypeStruct((t_len, d), F32),
                   jax.ShapeDtypeStruct((t_len // tm, 8, LANES), F32)],
        scratch_shapes=[pltpu.VMEM((tm, d), F32)],
        compiler_params=_cparams(("parallel", "arbitrary")),
    )(hf, x2, target, w1, w2)


def _mlp_bwd(dy, u, x2, g, w1, w2, tm=512, tf=1024):
    t_len, d = x2.shape
    f = w1.shape[1]
    tm, tf = min(tm, t_len), min(tf, f)
    nf = f // tf

    def body(dy_ref, u_ref, x2_ref, g_ref, w1_ref, w2_ref, du_ref, dx2_ref, dg_ref, acc_ref):
        i = pl.program_id(0)
        k = pl.program_id(1)

        @pl.when((i == 0) & (k == 0))
        def _():
            dg_ref[...] = jnp.zeros_like(dg_ref)

        @pl.when(k == 0)
        def _():
            acc_ref[...] = jnp.zeros_like(acc_ref)

        da = _dot(dy_ref[...], w2_ref[...], NT)
        du = (da * (2.0 * jnp.maximum(u_ref[...], 0.0))).astype(BF16)
        du_ref[...] = du
        acc_ref[...] += _dot(du, w1_ref[...], NT)

        @pl.when(k == nf - 1)
        def _():
            dxn, dg = _rms_bwd(x2_ref[...], g_ref[...], acc_ref[...])
            dx2_ref[...] = dy_ref[...] + dxn
            dg_ref[...] += dg

    row = lambda i, k: (i, 0)
    fixed = lambda i, k: (0, 0)
    return pl.pallas_call(
        body, name="mlp_bwd", grid=(t_len // tm, nf),
        in_specs=[pl.BlockSpec((tm, d), row), pl.BlockSpec((tm, tf), lambda i, k: (i, k)), pl.BlockSpec((tm, d), row), pl.BlockSpec((1, d), fixed),
                  pl.BlockSpec((d, tf), lambda i, k: (0, k)), pl.BlockSpec((tf, d), lambda i, k: (k, 0))],
        out_specs=[pl.BlockSpec((tm, tf), lambda i, k: (i, k)), pl.BlockSpec((tm, d), row), pl.BlockSpec((1, d), fixed)],
        out_shape=[jax.ShapeDtypeStruct((t_len, f), BF16), jax.ShapeDtypeStruct((t_len, d), F32), jax.ShapeDtypeStruct((1, d), F32)],
        scratch_shapes=[pltpu.VMEM((tm, d), F32)],
        compiler_params=_cparams(("arbitrary", "arbitrary")),
    )(dy, u, x2, g, w1, w2)


def _pad_lanes(v, offset=0, width=LANES):
    return jnp.zeros((1, width), F32).at[:, offset:offset + v.shape[1]].set(v)


def _col(v, offset=0, rows=SM_ROWS):
    return jnp.zeros((rows, 1), F32).at[offset:offset + v.shape[1], 0].set(v[0])


def _local_step(x, mem, target, norm_mix_g, w_in, fox_qnorm_g, fox_knorm_g, fox_f_bias, fox_onorm_g, gdn_conv_w, gdn_A_log,
                gdn_dt_bias, gdn_onorm_g, w_out, norm_xattn_g, mem_norm_g, w_cq, w_ckv, xattn_qnorm_g, xattn_knorm_g, w_co,
                norm_mlp_g, w_mlp1, w_mlp2):
    n_batch, s_len, d = x.shape
    m_len = mem.shape[1]
    t_len = n_batch * s_len
    tq = min(256, s_len)
    nq = s_len // tq
    n_chunks = s_len // GDN_CHUNK
    x2d = x.reshape(t_len, d)

    wp = jnp.concatenate([w_in[:, 0:1536], w_in[:, 1544:3080], w_in[:, 3088:3600], w_in[:, 1536:1544], w_in[:, 3080:3088],
                          jnp.zeros((d, P_DIM - 3600), BF16)], axis=1)
    wst = jnp.concatenate([w_in[:, 1536:1544], w_in[:, 3080:3088]], axis=1).T
    conv_w = jnp.concatenate([gdn_conv_w, jnp.zeros((8 - CONV_WIDTH, gdn_conv_w.shape[1]), F32)], axis=0)
    bias_col = _col(fox_f_bias, SM_F)
    gq2, gk2, go2 = (jnp.tile(g, (1, 2)) for g in (fox_qnorm_g, fox_knorm_g, fox_onorm_g))
    a_c, dt_c = _pad_lanes(gdn_A_log, SM_A), _pad_lanes(gdn_dt_bias, SM_A)
    a_r, dt_r = _col(gdn_A_log, SM_A), _col(gdn_dt_bias, SM_A)

    h1, pfox, pgdn, pz, sm, smt = _in_proj(x2d, norm_mix_g, wp, wst)
    c_rows = _fox_cum(smt, bias_col, n_batch, s_len)
    cb = c_rows.reshape(SM_ROWS, n_batch, nq, tq).transpose(1, 2, 0, 3)
    pf3 = pfox.reshape(n_batch, s_len, 1536)
    o_fox, oa, lse = _fox_fwd(pf3, cb, gq2, gk2, go2, tq)
    pg3 = pgdn.reshape(n_batch, s_len, 1536)
    qkvn = _gdn_pre(pg3, conv_w)
    z3 = pz.reshape(n_batch, s_len, GDN_WIDTH)
    smc = sm.reshape(n_batch, s_len, LANES)
    smr = smt.reshape(SM_ROWS, n_batch * n_chunks, GDN_CHUNK).transpose(1, 0, 2)
    ob, states = _gdn_fwd(qkvn, z3, smc, smr, a_c, dt_c, a_r, dt_r, gdn_onorm_g)
    oa2, ob2 = oa.reshape(t_len, FOX_WIDTH), ob.reshape(t_len, GDN_WIDTH)
    x1, hq, cq = _out_proj(x2d, oa2, ob2, w_out, norm_xattn_g, w_cq)
    mem2d = mem.reshape(n_batch * m_len, d)
    hm, ckv = _mem_kv(mem2d, mem_norm_g, w_ckv)
    co, x2, hf = _xattn_fwd(cq, ckv, x1, xattn_qnorm_g, xattn_knorm_g, w_co, norm_mlp_g, n_batch, s_len, m_len)
    u, a_act, dy, loss_tiles = _mlp_fwd(hf, x2, target.reshape(t_len, d), w_mlp1, w_mlp2)
    loss = 0.5 * jnp.sum(loss_tiles[:, 0, 0])

    grads = {}
    du, dx2, grads["norm_mlp_g"] = _mlp_bwd(dy, u, x2, norm_mlp_g, w_mlp1, w_mlp2)
    grads["w_mlp2"] = _wgrad(a_act, dy, "wgrad_mlp2")
    grads["w_mlp1"] = _wgrad(hf, du, "wgrad_mlp1")
    grads["w_co"] = _wgrad(co, dx2, "wgrad_co")
    dx1, dcq, dckv, grads["xattn_qnorm_g"], grads["xattn_knorm_g"], grads["norm_xattn_g"] = _xattn_bwd(
        dx2, cq, ckv, x1, xattn_qnorm_g, xattn_knorm_g, w_co, norm_xattn_g, w_cq, n_batch, s_len, m_len)
    grads["w_cq"] = _wgrad(hq, dcq, "wgrad_cq")
    grads["w_ckv"] = _wgrad(hm, dckv, "wgrad_ckv")
    grads["mem_norm_g"] = _mem_kv_bwd(dckv, mem2d, mem_norm_g, w_ckv)
    grads["w_out"] = _wgrad(jnp.concatenate([oa2, ob2], axis=1), dx1, "wgrad_out")
    dcat = _out_proj_bwd(dx1, w_out)
    dcat3 = dcat.reshape(n_batch, s_len, d)

    dqkvn, dz, dsmc, dsmr, dac, ddc, dar, ddr, grads["gdn_onorm_g"] = _gdn_bwd(
        qkvn, z3, smc, smr, a_c, dt_c, a_r, dt_r, gdn_onorm_g, states, dcat3)
    grads["gdn_A_log"] = dac[:, SM_A:SM_A + GDN_HEADS] + dar[SM_A:SM_A + GDN_HEADS, 0][None, :]
    grads["gdn_dt_bias"] = ddc[:, SM_A:SM_A + GDN_HEADS] + ddr[SM_A:SM_A + GDN_HEADS, 0][None, :]
    dpg, dconv = _gdn_pre_bwd(pg3, conv_w, dqkvn)
    grads["gdn_conv_w"] = dconv[0:CONV_WIDTH]

    dq, dk, dv, dcb, dgq, dgk, dgo = _fox_bwd(pf3, cb, gq2, gk2, go2, o_fox, lse, dcat3[:, :, 0:FOX_WIDTH], tq)
    fold = lambda g: g[:, 0:FOX_HEAD_DIM] + g[:, FOX_HEAD_DIM:LANES]
    grads["fox_qnorm_g"], grads["fox_knorm_g"], grads["fox_onorm_g"] = fold(dgq), fold(dgk), fold(dgo)
    dc8 = dcb[:, :, :, 0:2, :].transpose(1, 3, 0, 2, 4).reshape(FOX_HEADS, t_len)
    dc_rows = jnp.concatenate([dc8, jnp.zeros((SM_ROWS - FOX_HEADS, t_len), F32)], axis=0)
    dl_rows, dbias = _fox_cum_bwd(dc_rows, smt, bias_col, n_batch, s_len)
    grads["fox_f_bias"] = dbias[SM_F:SM_F + FOX_HEADS, 0][None, :]
    dsm_rows = jnp.concatenate([dl_rows[0:SM_B], dsmr.transpose(1, 0, 2).reshape(SM_ROWS, t_len)[SM_B:SM_ROWS]], axis=0)

    dproj = jnp.concatenate([dq.reshape(t_len, FOX_WIDTH), dk.reshape(t_len, FOX_WIDTH), dv.reshape(t_len, FOX_WIDTH),
                             dpg.reshape(t_len, 1536), dz.reshape(t_len, GDN_WIDTH), dsmc.reshape(t_len, LANES).astype(BF16)], axis=1)
    grad_x, grads["norm_mix_g"] = _in_proj_bwd(dproj, dsm_rows, x2d, norm_mix_g, wp, wst, dx1)
    dwp = _wgrad(h1, dproj, "wgrad_in", bk=256, bn=P_DIM)
    dwst = _rows_matmul(dsm_rows, h1, "wgrad_in_rows")
    dw_small = dwp[:, P_SMALL:P_SMALL + SM_ROWS] + dwst.T
    grads["w_in"] = jnp.concatenate([dwp[:, 0:1536], dw_small[:, 0:8], dwp[:, 1536:3072], dw_small[:, 8:16], dwp[:, 3072:3584]], axis=1)
    return loss, grad_x.reshape(n_batch, s_len, d), grads


MESH_ID = pl.DeviceIdType.MESH
ANY_SPEC = pl.BlockSpec(memory_space=pl.ANY)


def _place():
    x, y, c = lax.axis_index("x"), lax.axis_index("y"), lax.axis_index("c")
    return x, y, c, [(1 - x, y), (x, 1 - y), (1 - x, 1 - y)]


def _all_gather_body(n, ins, outs, send_sems, recv_sems, local_sems):
    x, y, c, chips = _place()
    me, sibling = (x, y, c), (x, y, 1 - c)

    def copy(a, k, block, to, src=None):
        dst = outs[a].at[4 * block[0] + 2 * block[1] + block[2]]
        return pltpu.make_async_remote_copy(src_ref=dst if src is None else src, dst_ref=dst, send_sem=send_sems.at[a, k],
                                            recv_sem=recv_sems.at[a, k], device_id=to, device_id_type=MESH_ID)

    mine = [pltpu.make_async_copy(ins[a], outs[a].at[4 * x + 2 * y + c], local_sems.at[a]) for a in range(n)]
    for cp in mine:
        cp.start()
    first = []
    for a in range(n):
        first.append(copy(a, 0, me, sibling, src=ins[a]))
        first += [copy(a, 1 + j, me, (*chip, c), src=ins[a]) for j, chip in enumerate(chips)]
    for cp in first:
        cp.start()
    passed = []
    for j, chip in enumerate(chips):
        for a in range(n):
            copy(a, 1 + j, (*chip, c), me).wait_recv()
            fwd = copy(a, 4 + j, (*chip, c), sibling)
            fwd.start()
            passed.append(fwd)
    for a in range(n):
        copy(a, 0, sibling, me).wait_recv()
        for j, chip in enumerate(chips):
            copy(a, 4 + j, (*chip, 1 - c), me).wait_recv()
    for cp in first + passed:
        cp.wait_send()
    for cp in mine:
        cp.wait()


def _all_gather_hbm(arrs, name):
    n = len(arrs)

    def body(*refs):
        _all_gather_body(n, refs[:n], refs[n:2 * n], *refs[2 * n:])

    return pl.pallas_call(
        body, name=name, in_specs=[ANY_SPEC] * n, out_specs=[ANY_SPEC] * n,
        out_shape=[jax.ShapeDtypeStruct((N_DEV,) + a.shape, a.dtype) for a in arrs],
        scratch_shapes=[pltpu.SemaphoreType.DMA((n, 7)), pltpu.SemaphoreType.DMA((n, 7)), pltpu.SemaphoreType.DMA((n,))],
    )(*arrs)


def _pair_exchange(arrs, name):
    n = len(arrs)

    def body(*refs):
        ins, outs = refs[:n], refs[n:2 * n]
        send_sems, recv_sems, local_sems = refs[2 * n:]
        x, y, c, _ = _place()
        local, remote = [], []
        for a in range(n):
            for chip in range(4):
                local.append(pltpu.make_async_copy(ins[a].at[2 * chip + c], outs[a].at[1, chip], local_sems.at[a, chip]))
                remote.append(pltpu.make_async_remote_copy(
                    src_ref=ins[a].at[2 * chip + (1 - c)], dst_ref=outs[a].at[0, chip], send_sem=send_sems.at[a, chip],
                    recv_sem=recv_sems.at[a, chip], device_id=(x, y, 1 - c), device_id_type=MESH_ID))
        for cp in remote + local:
            cp.start()
        for cp in remote + local:
            cp.wait()

    return pl.pallas_call(
        body, name=name, in_specs=[ANY_SPEC] * n, out_specs=[ANY_SPEC] * n,
        out_shape=[jax.ShapeDtypeStruct((2, 4) + a.shape[1:], a.dtype) for a in arrs],
        scratch_shapes=[pltpu.SemaphoreType.DMA((n, 4)), pltpu.SemaphoreType.DMA((n, 4)), pltpu.SemaphoreType.DMA((n, 4))],
    )(*arrs)


def _chip_exchange(arrs, name):
    n = len(arrs)

    def body(*refs):
        ins, outs = refs[:n], refs[n:2 * n]
        send_sems, recv_sems, local_sems = refs[2 * n:]
        x, y, c, chips = _place()
        copies = []
        for a in range(n):
            copies.append(pltpu.make_async_copy(ins[a].at[2 * x + y], outs[a].at[3], local_sems.at[a]))
            for j, chip in enumerate(chips):
                copies.append(pltpu.make_async_remote_copy(
                    src_ref=ins[a].at[2 * chip[0] + chip[1]], dst_ref=outs[a].at[j], send_sem=send_sems.at[a, j],
                    recv_sem=recv_sems.at[a, j], device_id=(*chip, c), device_id_type=MESH_ID))
        for cp in copies:
            cp.start()
        for cp in copies:
            cp.wait()

    return pl.pallas_call(
        body, name=name, in_specs=[ANY_SPEC] * n, out_specs=[ANY_SPEC] * n,
        out_shape=[jax.ShapeDtypeStruct(a.shape, a.dtype) for a in arrs],
        scratch_shapes=[pltpu.SemaphoreType.DMA((n, 3)), pltpu.SemaphoreType.DMA((n, 3)), pltpu.SemaphoreType.DMA((n,))],
    )(*arrs)


def _all_gather_vmem(block, name):
    def body(in_ref, out_ref, send_sems, recv_sems, local_sems):
        _all_gather_body(1, [in_ref], [out_ref], send_sems, recv_sems, local_sems)

    vmem = pl.BlockSpec(memory_space=pltpu.VMEM)
    return pl.pallas_call(
        body, name=name, in_specs=[vmem], out_specs=vmem,
        out_shape=jax.ShapeDtypeStruct((N_DEV,) + block.shape, block.dtype),
        scratch_shapes=[pltpu.SemaphoreType.DMA((1, 7)), pltpu.SemaphoreType.DMA((1, 7)), pltpu.SemaphoreType.DMA((1,))],
    )(block)


def _row_tile(rows, cols):
    if rows <= 256:
        return rows
    return 256 if cols <= 512 else 128


def _pair_sum(xp, name):
    _, _, rows, cols = xp.shape
    tr = _row_tile(rows, cols)

    def body(x_ref, o_ref):
        o_ref[0] = x_ref[0, 0] + x_ref[1, 0]

    return pl.pallas_call(
        body, name=name, grid=(4, rows // tr),
        in_specs=[pl.BlockSpec((2, 1, tr, cols), lambda k, i: (0, k, i, 0))],
        out_specs=pl.BlockSpec((1, tr, cols), lambda k, i: (k, i, 0)),
        out_shape=jax.ShapeDtypeStruct((4, rows, cols), F32),
        compiler_params=_cparams(("parallel", "parallel")),
    )(xp)


def _adamw(w, g, m, v):
    m_new = ADAM_B1 * m + (1.0 - ADAM_B1) * g
    v_new = ADAM_B2 * v + (1.0 - ADAM_B2) * (g * g)
    m_hat = m_new / (1.0 - ADAM_B1 ** ADAM_STEP)
    v_hat = v_new / (1.0 - ADAM_B2 ** ADAM_STEP)
    delta = -ADAM_LR * (m_hat / (jnp.sqrt(v_hat) + ADAM_EPS) + ADAM_WD * w)
    return delta, m_new, v_new


def _sum_adam(parts, order, w, m, v, name):
    n_parts, rows, cols = parts.shape
    tr = _row_tile(rows, cols)

    def body(p_ref, w_ref, m_ref, v_ref, g_ref, d_ref, mo_ref, vo_ref):
        g = p_ref[order[0]]
        for k in order[1:]:
            g = g + p_ref[k]
        g_ref[...] = g
        d_ref[...], mo_ref[...], vo_ref[...] = _adamw(w_ref[...], g, m_ref[...], v_ref[...])

    tile = pl.BlockSpec((tr, cols), lambda i: (i, 0))
    out = jax.ShapeDtypeStruct((rows, cols), F32)
    return pl.pallas_call(
        body, name=name, grid=(rows // tr,),
        in_specs=[pl.BlockSpec((n_parts, tr, cols), lambda i: (0, i, 0)), tile, tile, tile],
        out_specs=[tile, tile, tile, tile], out_shape=[out, out, out, out],
        compiler_params=_cparams(("parallel",)),
    )(parts, w, m, v)


SHARDED = ("w_in", "gdn_conv_w", "w_out", "w_cq", "w_ckv", "w_co", "w_mlp1", "w_mlp2")
COLUMN_SHARDED = ("w_in", "gdn_conv_w", "w_co", "w_mlp1")
REPLICATED = ("norm_mix_g", "fox_qnorm_g", "fox_knorm_g", "fox_f_bias", "fox_onorm_g", "gdn_A_log", "gdn_dt_bias", "gdn_onorm_g",
              "norm_xattn_g", "mem_norm_g", "xattn_qnorm_g", "xattn_knorm_g", "norm_mlp_g")
WEIGHTS = ("norm_mix_g", "w_in", "fox_qnorm_g", "fox_knorm_g", "fox_f_bias", "fox_onorm_g", "gdn_conv_w", "gdn_A_log", "gdn_dt_bias",
           "gdn_onorm_g", "w_out", "norm_xattn_g", "mem_norm_g", "w_cq", "w_ckv", "xattn_qnorm_g", "xattn_knorm_g", "w_co",
           "norm_mlp_g", "w_mlp1", "w_mlp2")
PACK_ROWS = 16
LOSS_ROW = len(REPLICATED)


def _whole(name, gathered):
    if name in COLUMN_SHARDED:
        return gathered.transpose(1, 0, 2).reshape(gathered.shape[1], N_DEV * gathered.shape[2])
    return gathered.reshape(N_DEV * gathered.shape[1], gathered.shape[2])


def _blocks(name, whole):
    if name in COLUMN_SHARDED:
        rows, cols = whole.shape
        return whole.reshape(rows, N_DEV, cols // N_DEV).transpose(1, 0, 2)
    return whole.reshape(N_DEV, whole.shape[0] // N_DEV, whole.shape[1])


def _pack(vals, fill=0.0):
    rows = [jnp.pad(vals[k], ((0, 0), (0, D_MODEL - vals[k].shape[1])), constant_values=fill) for k in REPLICATED]
    rows.append(jnp.full((PACK_ROWS - len(rows), D_MODEL), fill, F32))
    return jnp.concatenate(rows, axis=0)


def kernel(x, mem, norm_mix_g, w_in, fox_qnorm_g, fox_knorm_g, fox_f_bias, fox_onorm_g, gdn_conv_w, gdn_A_log, gdn_dt_bias, gdn_onorm_g, w_out, norm_xattn_g, mem_norm_g, w_cq, w_ckv, xattn_qnorm_g, xattn_knorm_g, w_co, norm_mlp_g, w_mlp1, w_mlp2, loss_target, m_norm_mix_g, m_w_in, m_fox_qnorm_g, m_fox_knorm_g, m_fox_f_bias, m_fox_onorm_g, m_gdn_conv_w, m_gdn_A_log, m_gdn_dt_bias, m_gdn_onorm_g, m_w_out, m_norm_xattn_g, m_mem_norm_g, m_w_cq, m_w_ckv, m_xattn_qnorm_g, m_xattn_knorm_g, m_w_co, m_norm_mlp_g, m_w_mlp1, m_w_mlp2, v_norm_mix_g, v_w_in, v_fox_qnorm_g, v_fox_knorm_g, v_fox_f_bias, v_fox_onorm_g, v_gdn_conv_w, v_gdn_A_log, v_gdn_dt_bias, v_gdn_onorm_g, v_w_out, v_norm_xattn_g, v_mem_norm_g, v_w_cq, v_w_ckv, v_xattn_qnorm_g, v_xattn_knorm_g, v_w_co, v_norm_mlp_g, v_w_mlp1, v_w_mlp2):
    given = dict(locals())
    w = {k: given[k] for k in WEIGHTS}
    m = {k: given["m_" + k] for k in WEIGHTS}
    v = {k: given["v_" + k] for k in WEIGHTS}

    shards = [w[k][0] if k == "gdn_conv_w" else w[k][0].astype(BF16) for k in SHARDED]
    whole = {k: _whole(k, g) for k, g in zip(SHARDED, _all_gather_hbm(shards, "gather_weights"))}

    small = {k: w[k] for k in REPLICATED}
    loss_local, grad_x, grads = _local_step(x, mem, loss_target, **small, **whole)

    pair = _pair_exchange([_blocks(k, grads[k]) for k in SHARDED], "grad_pair_exchange")
    sums = [_pair_sum(p, "grad_pair_sum_" + k) for k, p in zip(SHARDED, pair)]
    parts = _chip_exchange(sums, "grad_chip_exchange")
    out_g, out_d, out_m, out_v = {}, {}, {}, {}
    for k, p in zip(SHARDED, parts):
        res = _sum_adam(p, (3, 0, 1, 2), w[k][0], m[k][0], v[k][0], "adam_" + k)
        out_g[k], out_d[k], out_m[k], out_v[k] = (r[None] for r in res)

    packed = _pack({k: grads[k] for k in REPLICATED}).at[LOSS_ROW, 0].set(loss_local)
    everyone = _all_gather_vmem(packed, "gather_small")
    res = _sum_adam(everyone, tuple(range(N_DEV)), _pack(small), _pack({k: m[k] for k in REPLICATED}),
                    _pack({k: v[k] for k in REPLICATED}, fill=1.0), "adam_small")
    for i, k in enumerate(REPLICATED):
        n = w[k].shape[1]
        out_g[k], out_d[k], out_m[k], out_v[k] = (r[i:i + 1, 0:n] for r in res)
    loss = res[0][LOSS_ROW, 0]

    return (loss, grad_x, *[out_g[k] for k in WEIGHTS], *[out_d[k] for k in WEIGHTS], *[out_m[k] for k in WEIGHTS],
            *[out_v[k] for k in WEIGHTS])
```

```python
import functools

import jax
import jax.numpy as jnp
import numpy as np
from jax import lax
from jax.experimental import pallas as pl
from jax.experimental.pallas import tpu as pltpu

F32 = jnp.float32
BF16 = jnp.bfloat16

D_MODEL = 1024
FOX_HEADS = 8
FOX_HEAD_DIM = 64
FOX_WIDTH = 512
GDN_HEADS = 4
GDN_HEAD_DIM = 128
GDN_WIDTH = 512
CONV_WIDTH = 4
GDN_CHUNK = 64
XATTN_HEADS = 4
XATTN_HEAD_DIM = 128
XATTN_WIDTH = 512
D_FF = 4096
EPS = 1e-6
NEG_INF = -1e30
N_DEV = 8

ADAM_LR = 0.001
ADAM_B1 = 0.9
ADAM_B2 = 0.999
ADAM_EPS = 1e-08
ADAM_WD = 0.01
ADAM_STEP = 10

P_FOX = 0
P_GDN = 1536
P_Z = 3072
P_SMALL = 3584
P_DIM = 3712
SM_F = 0
SM_B = 8
SM_A = 12
SM_ROWS = 16

LANES = 128
VMEM_LIMIT = 56 * 1024 * 1024

NN = (((1,), (0,)), ((), ()))
NT = (((1,), (1,)), ((), ()))
TN = (((0,), (0,)), ((), ()))


def _dot(a, b, dims=NN):
    return lax.dot_general(a.astype(BF16), b.astype(BF16), dims, preferred_element_type=F32)


def _cparams(sem=None):
    kw = dict(vmem_limit_bytes=VMEM_LIMIT)
    if sem is not None:
        kw["dimension_semantics"] = sem
    return pltpu.CompilerParams(**kw)


def _sigmoid(x):
    return 0.5 * (jnp.tanh(0.5 * x) + 1.0)


def _softplus(x):
    return jnp.maximum(x, 0.0) + jnp.log1p(jnp.exp(-jnp.abs(x)))


def _log_sigmoid(x):
    return -_softplus(-x)


def _rms(x, g):
    r = lax.rsqrt(jnp.mean(x * x, axis=-1, keepdims=True) + EPS)
    return x * r * g


def _rms_bwd(x, g, dy):
    r = lax.rsqrt(jnp.mean(x * x, axis=-1, keepdims=True) + EPS)
    xh = x * r
    dg = jnp.sum(dy * xh, axis=0, keepdims=True)
    dyg = dy * g
    dx = r * (dyg - xh * jnp.mean(dyg * xh, axis=-1, keepdims=True))
    return dx, dg


def _pair_stat(t, m0):
    s0 = jnp.sum(jnp.where(m0, t, 0.0), axis=-1, keepdims=True)
    s1 = jnp.sum(jnp.where(m0, 0.0, t), axis=-1, keepdims=True)
    return jnp.where(m0, s0, s1)


def _rms_pair(x, g, m0):
    r = lax.rsqrt(_pair_stat(x * x, m0) * (1.0 / FOX_HEAD_DIM) + EPS)
    return x * r * g


def _rms_pair_bwd(x, g, dy, m0):
    r = lax.rsqrt(_pair_stat(x * x, m0) * (1.0 / FOX_HEAD_DIM) + EPS)
    xh = x * r
    dg = jnp.sum(dy * xh, axis=0, keepdims=True)
    dyg = dy * g
    dx = r * (dyg - xh * (_pair_stat(dyg * xh, m0) * (1.0 / FOX_HEAD_DIM)))
    return dx, dg


@jax.custom_vjp
def _mm_nn(a, b):
    return _dot(a, b, NN)


_mm_nn.defvjp(lambda a, b: (_dot(a, b, NN), (a, b)),
              lambda r, g: (_dot(g, r[1], NT), _dot(r[0], g, TN)))


@jax.custom_vjp
def _mm_nt(a, b):
    return _dot(a, b, NT)


_mm_nt.defvjp(lambda a, b: (_dot(a, b, NT), (a, b)),
              lambda r, g: (_dot(g, r[1], NN), _dot(g, r[0], TN)))


@jax.custom_vjp
def _mm_tn(a, b):
    return _dot(a, b, TN)


_mm_tn.defvjp(lambda a, b: (_dot(a, b, TN), (a, b)),
              lambda r, g: (_dot(r[1], g, NT), _dot(r[0], g, NN)))


def _dot3(a, b, dims):
    ah = a.astype(BF16)
    al = (a - ah.astype(F32)).astype(BF16)
    bh = b.astype(BF16)
    bl = (b - bh.astype(F32)).astype(BF16)
    d = functools.partial(lax.dot_general, dimension_numbers=dims, preferred_element_type=F32)
    return d(ah, bh) + d(ah, bl) + d(al, bh)


@jax.custom_vjp
def _mm3(a, b):
    return _dot3(a, b, NN)


_mm3.defvjp(lambda a, b: (_dot3(a, b, NN), (a, b)),
            lambda r, g: (_dot3(g, r[1], NT), _dot3(r[0], g, TN)))


def _unit_lower_inverse(a):
    c = a.shape[0]
    eye = (lax.broadcasted_iota(jnp.int32, (c, c), 0) == lax.broadcasted_iota(jnp.int32, (c, c), 1)).astype(F32)
    x = eye - a
    p = a
    k = 2
    while k < c + 1:
        p = _mm3(p, p)
        x = x + _mm3(x, p)
        k *= 2
    return x


def _wgrad(a, b, name, bk=512, bn=512, bt=512):
    t_len, k_len = a.shape
    n_len = b.shape[1]
    bk, bn, bt = min(bk, k_len), min(bn, n_len), min(bt, t_len)
    nt = t_len // bt

    def body(a_ref, b_ref, o_ref, acc_ref):
        t = pl.program_id(2)

        @pl.when(t == 0)
        def _():
            acc_ref[...] = jnp.zeros_like(acc_ref)

        acc_ref[...] += _dot(a_ref[...], b_ref[...], TN)

        @pl.when(t == nt - 1)
        def _():
            o_ref[...] = acc_ref[...]

    return pl.pallas_call(
        body, name=name, grid=(k_len // bk, n_len // bn, nt),
        in_specs=[pl.BlockSpec((bt, bk), lambda i, j, t: (t, i)), pl.BlockSpec((bt, bn), lambda i, j, t: (t, j))],
        out_specs=pl.BlockSpec((bk, bn), lambda i, j, t: (i, j)),
        out_shape=jax.ShapeDtypeStruct((k_len, n_len), F32),
        scratch_shapes=[pltpu.VMEM((bk, bn), F32)],
        compiler_params=_cparams(("parallel", "parallel", "arbitrary")),
    )(a, b)


def _rows_matmul(a, b, name, bt=512):
    r_len, t_len = a.shape
    n_len = b.shape[1]
    bt = min(bt, t_len)
    nt = t_len // bt

    def body(a_ref, b_ref, o_ref):
        t = pl.program_id(0)

        @pl.when(t == 0)
        def _():
            o_ref[...] = jnp.zeros_like(o_ref)

        o_ref[...] += _dot(a_ref[...], b_ref[...], NN)

    return pl.pallas_call(
        body, name=name, grid=(nt,),
        in_specs=[pl.BlockSpec((r_len, bt), lambda t: (0, t)), pl.BlockSpec((bt, n_len), lambda t: (t, 0))],
        out_specs=pl.BlockSpec((r_len, n_len), lambda t: (0, 0)),
        out_shape=jax.ShapeDtypeStruct((r_len, n_len), F32),
        compiler_params=_cparams(("arbitrary",)),
    )(a, b)


def _in_proj(x, g, wp, wst, tm=256):
    t_len, d = x.shape
    tm = min(tm, t_len)

    def body(x_ref, g_ref, wp_ref, wst_ref, h_ref, fox_ref, gdn_ref, z_ref, sm_ref, smt_ref):
        h = _rms(x_ref[...], g_ref[...]).astype(BF16)
        h_ref[...] = h
        p = _dot(h, wp_ref[...], NN)
        fox_ref[...] = p[:, P_FOX:P_GDN]
        gdn_ref[...] = p[:, P_GDN:P_Z]
        z_ref[...] = p[:, P_Z:P_SMALL]
        sm_ref[...] = p[:, P_SMALL:P_DIM]
        smt_ref[...] = _dot(wst_ref[...], h, NT)

    row = lambda i: (i, 0)
    fixed = lambda i: (0, 0)
    return pl.pallas_call(
        body, name="in_proj", grid=(t_len // tm,),
        in_specs=[pl.BlockSpec((tm, d), row), pl.BlockSpec((1, d), fixed), pl.BlockSpec((d, P_DIM), fixed),
                  pl.BlockSpec((SM_ROWS, d), fixed)],
        out_specs=[pl.BlockSpec((tm, d), row), pl.BlockSpec((tm, 1536), row), pl.BlockSpec((tm, 1536), row),
                   pl.BlockSpec((tm, 512), row), pl.BlockSpec((tm, LANES), row), pl.BlockSpec((SM_ROWS, tm), lambda i: (0, i))],
        out_shape=[jax.ShapeDtypeStruct((t_len, d), BF16), jax.ShapeDtypeStruct((t_len, 1536), F32),
                   jax.ShapeDtypeStruct((t_len, 1536), F32), jax.ShapeDtypeStruct((t_len, 512), F32),
                   jax.ShapeDtypeStruct((t_len, LANES), F32), jax.ShapeDtypeStruct((SM_ROWS, t_len), F32)],
        compiler_params=_cparams(("parallel",)),
    )(x, g, wp, wst)


def _in_proj_bwd(dproj, dsmt, x, g, wp, wst, dx1, tm=256):
    t_len, d = x.shape
    tm = min(tm, t_len)

    def body(dp_ref, dst_ref, x_ref, g_ref, wp_ref, wst_ref, dx1_ref, dx_ref, dg_ref):
        i = pl.program_id(0)
        dh = _dot(dp_ref[...], wp_ref[...], NT) + _dot(dst_ref[...], wst_ref[...], TN)
        dxn, dg = _rms_bwd(x_ref[...], g_ref[...], dh)
        dx_ref[...] = dx1_ref[...] + dxn

        @pl.when(i == 0)
        def _():
            dg_ref[...] = jnp.zeros_like(dg_ref)

        dg_ref[...] += dg

    row = lambda i: (i, 0)
    fixed = lambda i: (0, 0)
    return pl.pallas_call(
        body, name="in_proj_bwd", grid=(t_len // tm,),
        in_specs=[pl.BlockSpec((tm, P_DIM), row), pl.BlockSpec((SM_ROWS, tm), lambda i: (0, i)), pl.BlockSpec((tm, d), row),
                  pl.BlockSpec((1, d), fixed), pl.BlockSpec((d, P_DIM), fixed), pl.BlockSpec((SM_ROWS, d), fixed),
                  pl.BlockSpec((tm, d), row)],
        out_specs=[pl.BlockSpec((tm, d), row), pl.BlockSpec((1, d), fixed)],
        out_shape=[jax.ShapeDtypeStruct((t_len, d), F32), jax.ShapeDtypeStruct((1, d), F32)],
        compiler_params=_cparams(("arbitrary",)),
    )(dproj, dsmt, x, g, wp, wst, dx1)


def _fox_cum(smt, bias_col, n_batch, s_len, ck=256):
    ck = min(ck, s_len)

    def body(s_ref, b_ref, c_ref):
        tri = (lax.broadcasted_iota(jnp.int32, (ck, ck), 0) <= lax.broadcasted_iota(jnp.int32, (ck, ck), 1)).astype(F32)
        carry = jnp.zeros((SM_ROWS, 1), F32)
        for r in range(s_len // ck):
            ls = _log_sigmoid(s_ref[:, r * ck:(r + 1) * ck] + b_ref[...])
            c = jnp.dot(ls, tri, precision=lax.Precision.HIGHEST, preferred_element_type=F32) + carry
            c_ref[:, r * ck:(r + 1) * ck] = c
            carry = c[:, ck - 1:ck]

    return pl.pallas_call(
        body, name="fox_cum", grid=(n_batch,),
        in_specs=[pl.BlockSpec((SM_ROWS, s_len), lambda b: (0, b)), pl.BlockSpec((SM_ROWS, 1), lambda b: (0, 0))],
        out_specs=pl.BlockSpec((SM_ROWS, s_len), lambda b: (0, b)),
        out_shape=jax.ShapeDtypeStruct(smt.shape, F32),
        compiler_params=_cparams(("parallel",)),
    )(smt, bias_col)


def _fox_cum_bwd(dc, smt, bias_col, n_batch, s_len, ck=256):
    ck = min(ck, s_len)
    nr = s_len // ck

    def body(dc_ref, s_ref, b_ref, dl_ref, db_ref):
        b = pl.program_id(0)
        tri = (lax.broadcasted_iota(jnp.int32, (ck, ck), 0) >= lax.broadcasted_iota(jnp.int32, (ck, ck), 1)).astype(F32)
        carry = jnp.zeros((SM_ROWS, 1), F32)
        tot = jnp.zeros((SM_ROWS, 1), F32)
        for r in reversed(range(nr)):
            sl = slice(r * ck, (r + 1) * ck)
            dls = jnp.dot(dc_ref[:, sl], tri, precision=lax.Precision.HIGHEST, preferred_element_type=F32) + carry
            carry = dls[:, 0:1]
            dl = dls * (1.0 - _sigmoid(s_ref[:, sl] + b_ref[...]))
            dl_ref[:, sl] = dl
            tot = tot + jnp.sum(dl, axis=1, keepdims=True)

        @pl.when(b == 0)
        def _():
            db_ref[...] = jnp.zeros_like(db_ref)

        db_ref[...] += jnp.broadcast_to(tot, db_ref.shape)

    return pl.pallas_call(
        body, name="fox_cum_bwd", grid=(n_batch,),
        in_specs=[pl.BlockSpec((SM_ROWS, s_len), lambda b: (0, b)), pl.BlockSpec((SM_ROWS, s_len), lambda b: (0, b)),
                  pl.BlockSpec((SM_ROWS, 1), lambda b: (0, 0))],
        out_specs=[pl.BlockSpec((SM_ROWS, s_len), lambda b: (0, b)), pl.BlockSpec((SM_ROWS, LANES), lambda b: (0, 0))],
        out_shape=[jax.ShapeDtypeStruct(smt.shape, F32), jax.ShapeDtypeStruct((SM_ROWS, LANES), F32)],
        compiler_params=_cparams(("arbitrary",)),
    )(dc, smt, bias_col)


def _fox_masks(tq, tk, i, kb):
    qpos = i * tq + lax.broadcasted_iota(jnp.int32, (tq, tk), 0)
    kpos = kb * tk + lax.broadcasted_iota(jnp.int32, (tq, tk), 1)
    return kpos <= qpos


def _fox_fwd(pf, cb, gq2, gk2, go2, tq=256):
    n_batch, s_len, _ = pf.shape
    tq = min(tq, s_len)
    nq = s_len // tq
    scale = FOX_HEAD_DIM ** -0.5

    def body(q_ref, k_ref, v_ref, c_ref, gq_ref, gk_ref, go_ref, o_ref, on_ref, lse_ref, kh_ref, vh_ref):
        j = pl.program_id(1)
        i = pl.program_id(2)
        m0 = lax.broadcasted_iota(jnp.int32, (1, LANES), 1) < FOX_HEAD_DIM

        @pl.when(i == 0)
        def _():
            kn = _rms_pair(k_ref[0], gk_ref[...], m0)
            kh_ref[0] = jnp.where(m0, kn, 0.0).astype(BF16)
            kh_ref[1] = jnp.where(m0, 0.0, kn).astype(BF16)
            v = v_ref[0]
            vh_ref[0] = jnp.where(m0, v, 0.0).astype(BF16)
            vh_ref[1] = jnp.where(m0, 0.0, v).astype(BF16)

        qb = _rms_pair(q_ref[0], gq_ref[...], m0).astype(BF16)

        def step(kb, carry):
            ms, ls, acc = carry
            off = pl.multiple_of(kb * tq, tq)
            mask = _fox_masks(tq, tq, i, kb)
            new_m, new_l, alphas, pv = [], [], [], []
            for hh in range(2):
                s = _dot(qb, kh_ref[hh, pl.ds(off, tq), :], NT) * scale
                s = s - c_ref[0, kb, pl.ds(2 * j + hh, 1), :]
                s = jnp.where(mask, s, NEG_INF)
                m_new = jnp.maximum(ms[hh], jnp.max(s, axis=-1, keepdims=True))
                alpha = jnp.exp(ms[hh] - m_new)
                p = jnp.exp(s - m_new)
                new_l.append(alpha * ls[hh] + jnp.sum(p, axis=-1, keepdims=True))
                new_m.append(m_new)
                alphas.append(alpha)
                pv.append(_dot(p, vh_ref[hh, pl.ds(off, tq), :], NN))
            acc = jnp.where(m0, alphas[0], alphas[1]) * acc + pv[0] + pv[1]
            return tuple(new_m), tuple(new_l), acc

        init_m = (jnp.full((tq, 1), NEG_INF, F32),) * 2
        init_l = (jnp.zeros((tq, 1), F32),) * 2
        ms, ls, acc = lax.fori_loop(0, i + 1, step, (init_m, init_l, jnp.zeros((tq, LANES), F32)))
        o = acc / jnp.where(m0, ls[0], ls[1])
        o_ref[0] = o
        on_ref[0] = _rms_pair(o, go_ref[...], m0).astype(BF16)
        lse_ref[0] = jnp.where(m0, ms[0] + jnp.log(ls[0]), ms[1] + jnp.log(ls[1]))

    fixed = lambda b, j, i: (0, 0)
    tile = lambda b, j, i: (b, i, j)
    return pl.pallas_call(
        body, name="fox_fwd", grid=(n_batch, 4, nq),
        in_specs=[pl.BlockSpec((1, tq, LANES), tile), pl.BlockSpec((1, s_len, LANES), lambda b, j, i: (b, 0, 4 + j)),
                  pl.BlockSpec((1, s_len, LANES), lambda b, j, i: (b, 0, 8 + j)),
                  pl.BlockSpec((1, nq, SM_ROWS, tq), lambda b, j, i: (b, 0, 0, 0)),
                  pl.BlockSpec((1, LANES), fixed), pl.BlockSpec((1, LANES), fixed), pl.BlockSpec((1, LANES), fixed)],
        out_specs=[pl.BlockSpec((1, tq, LANES), tile), pl.BlockSpec((1, tq, LANES), tile), pl.BlockSpec((1, tq, LANES), tile)],
        out_shape=[jax.ShapeDtypeStruct((n_batch, s_len, FOX_WIDTH), F32), jax.ShapeDtypeStruct((n_batch, s_len, FOX_WIDTH), BF16),
                   jax.ShapeDtypeStruct((n_batch, s_len, FOX_WIDTH), F32)],
        scratch_shapes=[pltpu.VMEM((2, s_len, LANES), BF16), pltpu.VMEM((2, s_len, LANES), BF16)],
        compiler_params=_cparams(("parallel", "parallel", "arbitrary")),
    )(pf, pf, pf, cb, gq2, gk2, go2)


def _fox_bwd(pf, cb, gq2, gk2, go2, o, lse, don, tq=256):
    n_batch, s_len, _ = pf.shape
    tq = min(tq, s_len)
    nq = s_len // tq
    scale = FOX_HEAD_DIM ** -0.5

    def body(q_ref, k_ref, v_ref, c_ref, gq_ref, gk_ref, go_ref, o_ref, lse_ref, don_ref,
             dq_ref, dk_ref, dv_ref, dc_ref, dgq_ref, dgk_ref, dgo_ref, kh_ref, vh_ref, dka_ref, dva_ref, dca_ref):
        b = pl.program_id(0)
        j = pl.program_id(1)
        i = pl.program_id(2)
        m0 = lax.broadcasted_iota(jnp.int32, (1, LANES), 1) < FOX_HEAD_DIM

        @pl.when((b == 0) & (j == 0) & (i == 0))
        def _():
            dgq_ref[...] = jnp.zeros_like(dgq_ref)
            dgk_ref[...] = jnp.zeros_like(dgk_ref)
            dgo_ref[...] = jnp.zeros_like(dgo_ref)

        @pl.when(i == 0)
        def _():
            kn = _rms_pair(k_ref[0], gk_ref[...], m0)
            kh_ref[0] = jnp.where(m0, kn, 0.0).astype(BF16)
            kh_ref[1] = jnp.where(m0, 0.0, kn).astype(BF16)
            v = v_ref[0]
            vh_ref[0] = jnp.where(m0, v, 0.0).astype(BF16)
            vh_ref[1] = jnp.where(m0, 0.0, v).astype(BF16)
            dka_ref[...] = jnp.zeros_like(dka_ref)
            dva_ref[...] = jnp.zeros_like(dva_ref)
            dca_ref[...] = jnp.zeros_like(dca_ref)

        q = q_ref[0]
        qn = _rms_pair(q, gq_ref[...], m0)
        qb = qn.astype(BF16)
        qh = (jnp.where(m0, qn, 0.0).astype(BF16), jnp.where(m0, 0.0, qn).astype(BF16))
        ot = o_ref[0]
        do, dgo = _rms_pair_bwd(ot, go_ref[...], don_ref[0], m0)
        dgo_ref[...] += dgo
        dd = do * ot
        delta = (jnp.sum(jnp.where(m0, dd, 0.0), axis=-1, keepdims=True), jnp.sum(jnp.where(m0, 0.0, dd), axis=-1, keepdims=True))
        doh = (jnp.where(m0, do, 0.0).astype(BF16), jnp.where(m0, 0.0, do).astype(BF16))
        lse_t = lse_ref[0]
        lse_h = (lse_t[:, 0:1], lse_t[:, FOX_HEAD_DIM:FOX_HEAD_DIM + 1])

        def step(kb, carry):
            dqn, rs = carry
            rs = list(rs)
            off = pl.multiple_of(kb * tq, tq)
            mask = _fox_masks(tq, tq, i, kb)
            for hh in range(2):
                kblk = kh_ref[hh, pl.ds(off, tq), :]
                vblk = vh_ref[hh, pl.ds(off, tq), :]
                s = _dot(qb, kblk, NT) * scale
                s = s - c_ref[0, kb, pl.ds(2 * j + hh, 1), :]
                s = jnp.where(mask, s, NEG_INF)
                p = jnp.exp(s - lse_h[hh])
                dp = _dot(doh[hh], vblk, NT)
                ds = p * (dp - delta[hh])
                dva_ref[pl.ds(off, tq), :] += _dot(p, doh[hh], TN)
                dka_ref[pl.ds(off, tq), :] += _dot(ds, qh[hh], TN) * scale
                dca_ref[kb, hh:hh + 1, :] += -jnp.sum(ds, axis=0, keepdims=True)
                rs[hh] = rs[hh] + jnp.sum(ds, axis=-1, keepdims=True)
                dqn = dqn + _dot(ds, kblk, NN) * scale
            return dqn, tuple(rs)

        dqn, rs = lax.fori_loop(0, i + 1, step, (jnp.zeros((tq, LANES), F32), (jnp.zeros((tq, 1), F32),) * 2))
        rs_rows = jnp.where(m0, rs[0], rs[1]).T
        dca_ref[i, 0:1, :] += rs_rows[0:1, :]
        dca_ref[i, 1:2, :] += rs_rows[FOX_HEAD_DIM:FOX_HEAD_DIM + 1, :]
        dq, dgq = _rms_pair_bwd(q, gq_ref[...], dqn, m0)
        dq_ref[0] = dq.astype(BF16)
        dgq_ref[...] += dgq

        @pl.when(i == nq - 1)
        def _():
            dk, dgk = _rms_pair_bwd(k_ref[0], gk_ref[...], dka_ref[...], m0)
            dk_ref[0] = dk.astype(BF16)
            dgk_ref[...] += dgk
            dv_ref[0] = dva_ref[...].astype(BF16)
            dc_ref[0, 0] = dca_ref[...]

    fixed = lambda b, j, i: (0, 0)
    tile = lambda b, j, i: (b, i, j)
    full = lambda b, j, i: (b, 0, j)
    wide = jax.ShapeDtypeStruct((n_batch, s_len, FOX_WIDTH), BF16)
    gain = jax.ShapeDtypeStruct((1, LANES), F32)
    return pl.pallas_call(
        body, name="fox_bwd", grid=(n_batch, 4, nq),
        in_specs=[pl.BlockSpec((1, tq, LANES), tile), pl.BlockSpec((1, s_len, LANES), lambda b, j, i: (b, 0, 4 + j)),
                  pl.BlockSpec((1, s_len, LANES), lambda b, j, i: (b, 0, 8 + j)),
                  pl.BlockSpec((1, nq, SM_ROWS, tq), lambda b, j, i: (b, 0, 0, 0)),
                  pl.BlockSpec((1, LANES), fixed), pl.BlockSpec((1, LANES), fixed), pl.BlockSpec((1, LANES), fixed),
                  pl.BlockSpec((1, tq, LANES), tile), pl.BlockSpec((1, tq, LANES), tile), pl.BlockSpec((1, tq, LANES), tile)],
        out_specs=[pl.BlockSpec((1, tq, LANES), tile), pl.BlockSpec((1, s_len, LANES), full), pl.BlockSpec((1, s_len, LANES), full),
                   pl.BlockSpec((1, 1, nq, 8, tq), lambda b, j, i: (b, j, 0, 0, 0)),
                   pl.BlockSpec((1, LANES), fixed), pl.BlockSpec((1, LANES), fixed), pl.BlockSpec((1, LANES), fixed)],
        out_shape=[wide, wide, wide, jax.ShapeDtypeStruct((n_batch, 4, nq, 8, tq), F32), gain, gain, gain],
        scratch_shapes=[pltpu.VMEM((2, s_len, LANES), BF16), pltpu.VMEM((2, s_len, LANES), BF16),
                        pltpu.VMEM((s_len, LANES), F32), pltpu.VMEM((s_len, LANES), F32), pltpu.VMEM((nq, 8, tq), F32)],
        compiler_params=_cparams(("arbitrary", "arbitrary", "arbitrary")),
    )(pf, pf, pf, cb, gq2, gk2, go2, o, lse, don)


def _shift_down(x, k):
    row = lax.broadcasted_iota(jnp.int32, x.shape, 0)
    return jnp.where(row >= k, pltpu.roll(x, k, 0), 0.0)


def _shift_up(x, k):
    n = x.shape[0]
    row = lax.broadcasted_iota(jnp.int32, x.shape, 0)
    return jnp.where(row < n - k, pltpu.roll(x, n - k, 0), 0.0)


def _conv_silu(x, w):
    y = w[3:4] * x + w[2:3] * _shift_down(x, 1) + w[1:2] * _shift_down(x, 2) + w[0:1] * _shift_down(x, 3)
    return y, y * _sigmoid(y)


def _gdn_pre(pg, conv_w):
    n_batch, s_len, width = pg.shape
    ncb = width // LANES

    def body(x_ref, w_ref, o_ref):
        cb = pl.program_id(1)
        _, s = _conv_silu(x_ref[0], w_ref[...])
        sn = s * lax.rsqrt(jnp.sum(s * s, axis=-1, keepdims=True) + EPS)
        o_ref[0] = jnp.where(cb < 2 * GDN_HEADS, sn, s)

    return pl.pallas_call(
        body, name="gdn_pre", grid=(n_batch, ncb),
        in_specs=[pl.BlockSpec((1, s_len, LANES), lambda b, c: (b, 0, c)), pl.BlockSpec((8, LANES), lambda b, c: (0, c))],
        out_specs=pl.BlockSpec((1, s_len, LANES), lambda b, c: (b, 0, c)),
        out_shape=jax.ShapeDtypeStruct(pg.shape, F32),
        compiler_params=_cparams(("parallel", "parallel")),
    )(pg, conv_w)


def _gdn_pre_bwd(pg, conv_w, dout):
    n_batch, s_len, width = pg.shape
    ncb = width // LANES

    def body(x_ref, w_ref, d_ref, dx_ref, dw_ref):
        cb = pl.program_id(0)
        b = pl.program_id(1)
        x = x_ref[0]
        w = w_ref[...]
        d = d_ref[0]
        y, s = _conv_silu(x, w)
        rr = lax.rsqrt(jnp.sum(s * s, axis=-1, keepdims=True) + EPS)
        sn = s * rr
        ds_n = rr * (d - sn * jnp.sum(d * sn, axis=-1, keepdims=True))
        ds = jnp.where(cb < 2 * GDN_HEADS, ds_n, d)
        sig = _sigmoid(y)
        dy = ds * (sig * (1.0 + y * (1.0 - sig)))
        dx = w[3:4] * dy + w[2:3] * _shift_up(dy, 1) + w[1:2] * _shift_up(dy, 2) + w[0:1] * _shift_up(dy, 3)
        dx_ref[0] = dx.astype(BF16)
        dw = [jnp.sum(dy * _shift_down(x, 3 - jj), axis=0, keepdims=True) if jj < 3 else jnp.sum(dy * x, axis=0, keepdims=True)
              for jj in range(CONV_WIDTH)]
        rows = lax.broadcasted_iota(jnp.int32, (8, LANES), 0)
        dwb = jnp.zeros((8, LANES), F32)
        for jj in range(CONV_WIDTH):
            dwb = dwb + jnp.where(rows == jj, dw[jj], 0.0)

        @pl.when(b == 0)
        def _():
            dw_ref[...] = jnp.zeros_like(dw_ref)

        dw_ref[...] += dwb

    blk = lambda c, b: (b, 0, c)
    return pl.pallas_call(
        body, name="gdn_pre_bwd", grid=(ncb, n_batch),
        in_specs=[pl.BlockSpec((1, s_len, LANES), blk), pl.BlockSpec((8, LANES), lambda c, b: (0, c)), pl.BlockSpec((1, s_len, LANES), blk)],
        out_specs=[pl.BlockSpec((1, s_len, LANES), blk), pl.BlockSpec((8, LANES), lambda c, b: (0, c))],
        out_shape=[jax.ShapeDtypeStruct(pg.shape, BF16), jax.ShapeDtypeStruct((8, width), F32)],
        compiler_params=_cparams(("parallel", "arbitrary")),
    )(pg, conv_w, dout)


def _gdn_gates(smc, smr, a_c, dt_c, a_r, dt_r, h):
    lane = lax.broadcasted_iota(jnp.int32, (1, LANES), 1)
    sub = lax.broadcasted_iota(jnp.int32, (SM_ROWS, 1), 0)
    beta_c = jnp.sum(jnp.where(lane == SM_B + h, _sigmoid(smc), 0.0), axis=1, keepdims=True)
    g_all_c = -jnp.exp(a_c) * _softplus(smc + dt_c)
    g_c = jnp.sum(jnp.where(lane == SM_A + h, g_all_c, 0.0), axis=1, keepdims=True)
    g_all_r = -jnp.exp(a_r) * _softplus(smr + dt_r)
    g_r = jnp.sum(jnp.where(sub == SM_A + h, g_all_r, 0.0), axis=0, keepdims=True)
    return beta_c, g_c, g_r


def _gdn_head(q, k, v, z, beta_c, g_c, g_r, go, s0):
    c = q.shape[0]
    ii = lax.broadcasted_iota(jnp.int32, (c, c), 0)
    jj = lax.broadcasted_iota(jnp.int32, (c, c), 1)
    incl = ii >= jj
    gc_c = jnp.sum(jnp.where(incl, g_r, 0.0), axis=1, keepdims=True)
    gc_r = jnp.sum(jnp.where(ii <= jj, g_c, 0.0), axis=0, keepdims=True)
    decay = jnp.where(incl, jnp.exp(jnp.where(incl, gc_c - gc_r, 0.0)), 0.0)
    qs = q * (GDN_HEAD_DIM ** -0.5)
    kb = k * beta_c
    vb = v * beta_c
    a = jnp.where(ii > jj, _mm_nt(kb, k) * decay, 0.0)
    t = _unit_lower_inverse(a)
    egc = jnp.exp(gc_c)
    u = _mm_nn(t, vb)
    w = _mm_nn(t, kb * egc)
    intra = _mm_nt(qs, k) * decay
    v_new = u - _mm_nn(w, s0)
    o = _mm_nn(qs * egc, s0) + _mm_nn(intra, v_new)
    g_last = jnp.sum(g_c, axis=0, keepdims=True)
    s1 = s0 * jnp.exp(g_last) + _mm_tn(k * jnp.exp(g_last - gc_c), v_new)
    og = _rms(o, go) * (z * _sigmoid(z))
    return og, s1


def _gdn_chunk_all(qkv, z, smc, smr, a_c, dt_c, a_r, dt_r, go, states):
    outs, nxt = [], []
    for h in range(GDN_HEADS):
        beta_c, g_c, g_r = _gdn_gates(smc, smr, a_c, dt_c, a_r, dt_r, h)
        sl = lambda base: slice(base + h * GDN_HEAD_DIM, base + (h + 1) * GDN_HEAD_DIM)
        og, s1 = _gdn_head(qkv[:, sl(0)], qkv[:, sl(GDN_WIDTH)], qkv[:, sl(2 * GDN_WIDTH)], z[:, sl(0)],
                           beta_c, g_c, g_r, go, states[h])
        outs.append(og)
        nxt.append(s1)
    return outs, nxt


def _gdn_fwd(qkvn, z, smc, smr, a_c, dt_c, a_r, dt_r, go):
    n_batch, s_len, _ = qkvn.shape
    c = GDN_CHUNK
    n = s_len // c
    hd = GDN_HEAD_DIM

    def body(qkv_ref, z_ref, smc_ref, smr_ref, ac_ref, dc_ref, ar_ref, dr_ref, go_ref, og_ref, st_ref, s_ref):
        @pl.when(pl.program_id(1) == 0)
        def _():
            s_ref[...] = jnp.zeros_like(s_ref)

        states = [s_ref[h] for h in range(GDN_HEADS)]
        for h in range(GDN_HEADS):
            st_ref[0, 0, h] = states[h]
        outs, nxt = _gdn_chunk_all(qkv_ref[0], z_ref[0], smc_ref[0], smr_ref[0], ac_ref[...], dc_ref[...], ar_ref[...],
                                   dr_ref[...], go_ref[...], states)
        for h in range(GDN_HEADS):
            og_ref[0, :, h * hd:(h + 1) * hd] = outs[h].astype(BF16)
            s_ref[h] = nxt[h]

    tok = lambda b, i: (b, i, 0)
    fixed = lambda b, i: (0, 0)
    return pl.pallas_call(
        body, name="gdn_fwd", grid=(n_batch, n),
        in_specs=[pl.BlockSpec((1, c, 3 * GDN_WIDTH), tok), pl.BlockSpec((1, c, GDN_WIDTH), tok), pl.BlockSpec((1, c, LANES), tok),
                  pl.BlockSpec((1, SM_ROWS, c), lambda b, i: (b * n + i, 0, 0)),
                  pl.BlockSpec((1, LANES), fixed), pl.BlockSpec((1, LANES), fixed), pl.BlockSpec((SM_ROWS, 1), fixed),
                  pl.BlockSpec((SM_ROWS, 1), fixed), pl.BlockSpec((1, LANES), fixed)],
        out_specs=[pl.BlockSpec((1, c, GDN_WIDTH), tok), pl.BlockSpec((1, 1, GDN_HEADS, hd, hd), lambda b, i: (b, i, 0, 0, 0))],
        out_shape=[jax.ShapeDtypeStruct((n_batch, s_len, GDN_WIDTH), BF16), jax.ShapeDtypeStruct((n_batch, n, GDN_HEADS, hd, hd), F32)],
        scratch_shapes=[pltpu.VMEM((GDN_HEADS, hd, hd), F32)],
        compiler_params=_cparams(("parallel", "arbitrary")),
    )(qkvn, z, smc, smr, a_c, dt_c, a_r, dt_r, go)


def _gdn_bwd(qkvn, z, smc, smr, a_c, dt_c, a_r, dt_r, go, states, dog):
    n_batch, s_len, _ = qkvn.shape
    c = GDN_CHUNK
    n = s_len // c
    hd = GDN_HEAD_DIM

    def body(qkv_ref, z_ref, smc_ref, smr_ref, ac_ref, dc_ref, ar_ref, dr_ref, go_ref, st_ref, dog_ref,
             dqkv_ref, dz_ref, dsmc_ref, dsmr_ref, dac_ref, ddc_ref, dar_ref, ddr_ref, dgo_ref, ds_ref):
        first = (pl.program_id(0) == 0) & (pl.program_id(1) == 0)

        @pl.when(pl.program_id(1) == 0)
        def _():
            ds_ref[...] = jnp.zeros_like(ds_ref)

        @pl.when(first)
        def _():
            for r in (dac_ref, ddc_ref, dar_ref, ddr_ref, dgo_ref):
                r[...] = jnp.zeros_like(r)

        states = [st_ref[0, 0, h] for h in range(GDN_HEADS)]
        prim = (qkv_ref[0], z_ref[0], smc_ref[0], smr_ref[0], ac_ref[...], dc_ref[...], ar_ref[...], dr_ref[...], go_ref[...], states)
        _, vjp = jax.vjp(_gdn_chunk_all, *prim)
        d_out = dog_ref[0]
        cot = ([d_out[:, h * hd:(h + 1) * hd] for h in range(GDN_HEADS)], [ds_ref[h] for h in range(GDN_HEADS)])
        dqkv, dz, dsmc, dsmr, dac, ddc, dar, ddr, dgo, dstates = vjp(cot)
        dqkv_ref[0] = dqkv
        dz_ref[0] = dz.astype(BF16)
        dsmc_ref[0] = dsmc
        dsmr_ref[0] = dsmr
        dac_ref[...] += dac
        ddc_ref[...] += ddc
        dar_ref[...] += dar
        ddr_ref[...] += ddr
        dgo_ref[...] += dgo
        for h in range(GDN_HEADS):
            ds_ref[h] = dstates[h]

    tok = lambda b, i: (b, n - 1 - i, 0)
    fixed = lambda b, i: (0, 0)
    lane_vec = jax.ShapeDtypeStruct((1, LANES), F32)
    row_vec = jax.ShapeDtypeStruct((SM_ROWS, 1), F32)
    return pl.pallas_call(
        body, name="gdn_bwd", grid=(n_batch, n),
        in_specs=[pl.BlockSpec((1, c, 3 * GDN_WIDTH), tok), pl.BlockSpec((1, c, GDN_WIDTH), tok), pl.BlockSpec((1, c, LANES), tok),
                  pl.BlockSpec((1, SM_ROWS, c), lambda b, i: (b * n + n - 1 - i, 0, 0)),
                  pl.BlockSpec((1, LANES), fixed), pl.BlockSpec((1, LANES), fixed), pl.BlockSpec((SM_ROWS, 1), fixed),
                  pl.BlockSpec((SM_ROWS, 1), fixed), pl.BlockSpec((1, LANES), fixed),
                  pl.BlockSpec((1, 1, GDN_HEADS, hd, hd), lambda b, i: (b, n - 1 - i, 0, 0, 0)),
                  pl.BlockSpec((1, c, GDN_WIDTH), lambda b, i: (b, n - 1 - i, 1))],
        out_specs=[pl.BlockSpec((1, c, 3 * GDN_WIDTH), tok), pl.BlockSpec((1, c, GDN_WIDTH), tok), pl.BlockSpec((1, c, LANES), tok),
                   pl.BlockSpec((1, SM_ROWS, c), lambda b, i: (b * n + n - 1 - i, 0, 0)),
                   pl.BlockSpec((1, LANES), fixed), pl.BlockSpec((1, LANES), fixed), pl.BlockSpec((SM_ROWS, 1), fixed),
                   pl.BlockSpec((SM_ROWS, 1), fixed), pl.BlockSpec((1, LANES), fixed)],
        out_shape=[jax.ShapeDtypeStruct((n_batch, s_len, 3 * GDN_WIDTH), F32), jax.ShapeDtypeStruct((n_batch, s_len, GDN_WIDTH), BF16),
                   jax.ShapeDtypeStruct((n_batch, s_len, LANES), F32), jax.ShapeDtypeStruct((n_batch * n, SM_ROWS, c), F32),
                   lane_vec, lane_vec, row_vec, row_vec, lane_vec],
        scratch_shapes=[pltpu.VMEM((GDN_HEADS, hd, hd), F32)],
        compiler_params=_cparams(("arbitrary", "arbitrary")),
    )(qkvn, z, smc, smr, a_c, dt_c, a_r, dt_r, go, states, dog)


def _out_proj(x, oa, ob, w_out, g_x, w_cq, tm=256):
    t_len, d = x.shape
    tm = min(tm, t_len)

    def body(x_ref, oa_ref, ob_ref, wo_ref, g_ref, wq_ref, x1_ref, hq_ref, cq_ref):
        x1 = x_ref[...] + _dot(oa_ref[...], wo_ref[0:FOX_WIDTH, :]) + _dot(ob_ref[...], wo_ref[FOX_WIDTH:2 * FOX_WIDTH, :])
        x1_ref[...] = x1
        hq = _rms(x1, g_ref[...]).astype(BF16)
        hq_ref[...] = hq
        cq_ref[...] = _dot(hq, wq_ref[...])

    row = lambda i: (i, 0)
    fixed = lambda i: (0, 0)
    return pl.pallas_call(
        body, name="out_proj", grid=(t_len // tm,),
        in_specs=[pl.BlockSpec((tm, d), row), pl.BlockSpec((tm, FOX_WIDTH), row), pl.BlockSpec((tm, GDN_WIDTH), row),
                  pl.BlockSpec((d, d), fixed), pl.BlockSpec((1, d), fixed), pl.BlockSpec((d, XATTN_WIDTH), fixed)],
        out_specs=[pl.BlockSpec((tm, d), row), pl.BlockSpec((tm, d), row), pl.BlockSpec((tm, XATTN_WIDTH), row)],
        out_shape=[jax.ShapeDtypeStruct((t_len, d), F32), jax.ShapeDtypeStruct((t_len, d), BF16), jax.ShapeDtypeStruct((t_len, XATTN_WIDTH), F32)],
        compiler_params=_cparams(("parallel",)),
    )(x, oa, ob, w_out, g_x, w_cq)


def _out_proj_bwd(dx1, w_out, tm=512):
    t_len, d = dx1.shape
    tm = min(tm, t_len)

    def body(dx_ref, w_ref, o_ref):
        o_ref[...] = _dot(dx_ref[...], w_ref[...], NT)

    return pl.pallas_call(
        body, name="out_proj_bwd", grid=(t_len // tm,),
        in_specs=[pl.BlockSpec((tm, d), lambda i: (i, 0)), pl.BlockSpec((d, d), lambda i: (0, 0))],
        out_specs=pl.BlockSpec((tm, d), lambda i: (i, 0)),
        out_shape=jax.ShapeDtypeStruct((t_len, d), F32),
        compiler_params=_cparams(("parallel",)),
    )(dx1, w_out)


def _mem_kv(mem, g, w_ckv, tm=256):
    t_len, d = mem.shape
    tm = min(tm, t_len)

    def body(x_ref, g_ref, w_ref, h_ref, o_ref):
        h = _rms(x_ref[...], g_ref[...]).astype(BF16)
        h_ref[...] = h
        o_ref[...] = _dot(h, w_ref[...])

    row = lambda i: (i, 0)
    fixed = lambda i: (0, 0)
    return pl.pallas_call(
        body, name="mem_kv", grid=(t_len // tm,),
        in_specs=[pl.BlockSpec((tm, d), row), pl.BlockSpec((1, d), fixed), pl.BlockSpec((d, 2 * XATTN_WIDTH), fixed)],
        out_specs=[pl.BlockSpec((tm, d), row), pl.BlockSpec((tm, 2 * XATTN_WIDTH), row)],
        out_shape=[jax.ShapeDtypeStruct((t_len, d), BF16), jax.ShapeDtypeStruct((t_len, 2 * XATTN_WIDTH), F32)],
        compiler_params=_cparams(("parallel",)),
    )(mem, g, w_ckv)


def _mem_kv_bwd(dckv, mem, g, w_ckv, tm=256):
    t_len, d = mem.shape
    tm = min(tm, t_len)

    def body(d_ref, x_ref, g_ref, w_ref, dg_ref):
        @pl.when(pl.program_id(0) == 0)
        def _():
            dg_ref[...] = jnp.zeros_like(dg_ref)

        dh = _dot(d_ref[...], w_ref[...], NT)
        _, dg = _rms_bwd(x_ref[...], g_ref[...], dh)
        dg_ref[...] += dg

    row = lambda i: (i, 0)
    fixed = lambda i: (0, 0)
    return pl.pallas_call(
        body, name="mem_kv_bwd", grid=(t_len // tm,),
        in_specs=[pl.BlockSpec((tm, 2 * XATTN_WIDTH), row), pl.BlockSpec((tm, d), row), pl.BlockSpec((1, d), fixed),
                  pl.BlockSpec((d, 2 * XATTN_WIDTH), fixed)],
        out_specs=pl.BlockSpec((1, d), fixed),
        out_shape=jax.ShapeDtypeStruct((1, d), F32),
        compiler_params=_cparams(("arbitrary",)),
    )(dckv, mem, g, w_ckv)


def _xattn_probs(qn, kn):
    s = _dot(qn, kn, NT) * (XATTN_HEAD_DIM ** -0.5)
    p = jnp.exp(s - jnp.max(s, axis=-1, keepdims=True))
    return p / jnp.sum(p, axis=-1, keepdims=True)


def _xattn_fwd(cq, ckv, x1, gq, gk, w_co, g_mlp, n_batch, s_len, m_len, tq=256):
    d = x1.shape[1]
    tq = min(tq, s_len)
    nq = s_len // tq
    hd = XATTN_HEAD_DIM

    def body(cq_ref, kv_ref, x1_ref, gq_ref, gk_ref, wo_ref, gm_ref, co_ref, x2_ref, hf_ref):
        outs = []
        for h in range(XATTN_HEADS):
            qn = _rms(cq_ref[:, h * hd:(h + 1) * hd], gq_ref[...])
            kn = _rms(kv_ref[:, h * hd:(h + 1) * hd], gk_ref[...])
            p = _xattn_probs(qn, kn)
            outs.append(_dot(p, kv_ref[:, XATTN_WIDTH + h * hd:XATTN_WIDTH + (h + 1) * hd]).astype(BF16))
        x2 = x1_ref[...]
        for h in range(XATTN_HEADS):
            co_ref[:, h * hd:(h + 1) * hd] = outs[h]
            x2 = x2 + _dot(outs[h], wo_ref[h * hd:(h + 1) * hd, :])
        x2_ref[...] = x2
        hf_ref[...] = _rms(x2, gm_ref[...]).astype(BF16)

    row = lambda b, i: (b * nq + i, 0)
    fixed = lambda b, i: (0, 0)
    t_len = n_batch * s_len
    return pl.pallas_call(
        body, name="xattn_fwd", grid=(n_batch, nq),
        in_specs=[pl.BlockSpec((tq, XATTN_WIDTH), row), pl.BlockSpec((m_len, 2 * XATTN_WIDTH), lambda b, i: (b, 0)),
                  pl.BlockSpec((tq, d), row), pl.BlockSpec((1, hd), fixed), pl.BlockSpec((1, hd), fixed),
                  pl.BlockSpec((XATTN_WIDTH, d), fixed), pl.BlockSpec((1, d), fixed)],
        out_specs=[pl.BlockSpec((tq, XATTN_WIDTH), row), pl.BlockSpec((tq, d), row), pl.BlockSpec((tq, d), row)],
        out_shape=[jax.ShapeDtypeStruct((t_len, XATTN_WIDTH), BF16), jax.ShapeDtypeStruct((t_len, d), F32),
                   jax.ShapeDtypeStruct((t_len, d), BF16)],
        compiler_params=_cparams(("parallel", "parallel")),
    )(cq, ckv, x1, gq, gk, w_co, g_mlp)


def _xattn_bwd(dx2, cq, ckv, x1, gq, gk, w_co, g_x, w_cq, n_batch, s_len, m_len, tq=256):
    d = x1.shape[1]
    tq = min(tq, s_len)
    nq = s_len // tq
    hd = XATTN_HEAD_DIM
    scale = XATTN_HEAD_DIM ** -0.5

    def body(dx2_ref, cq_ref, kv_ref, x1_ref, gq_ref, gk_ref, wo_ref, gx_ref, wq_ref,
             dx1_ref, dcq_ref, dkv_ref, dgq_ref, dgk_ref, dgx_ref, dk_acc, dv_acc):
        b = pl.program_id(0)
        i = pl.program_id(1)

        @pl.when((b == 0) & (i == 0))
        def _():
            dgq_ref[...] = jnp.zeros_like(dgq_ref)
            dgk_ref[...] = jnp.zeros_like(dgk_ref)
            dgx_ref[...] = jnp.zeros_like(dgx_ref)

        @pl.when(i == 0)
        def _():
            dk_acc[...] = jnp.zeros_like(dk_acc)
            dv_acc[...] = jnp.zeros_like(dv_acc)

        dx2 = dx2_ref[...]
        dhq = jnp.zeros((tq, d), F32)
        for h in range(XATTN_HEADS):
            sl = slice(h * hd, (h + 1) * hd)
            q = cq_ref[:, sl]
            qn = _rms(q, gq_ref[...])
            kn = _rms(kv_ref[:, sl], gk_ref[...])
            v = kv_ref[:, XATTN_WIDTH + h * hd:XATTN_WIDTH + (h + 1) * hd]
            p = _xattn_probs(qn, kn)
            dco = _dot(dx2, wo_ref[sl, :], NT)
            dv_acc[:, sl] += _dot(p, dco, TN)
            dp = _dot(dco, v, NT)
            ds = p * (dp - jnp.sum(dp * p, axis=-1, keepdims=True))
            dqn = _dot(ds, kn) * scale
            dk_acc[:, sl] += _dot(ds, qn, TN) * scale
            dq, dgq = _rms_bwd(q, gq_ref[...], dqn)
            dgq_ref[...] += dgq
            dqb = dq.astype(BF16)
            dcq_ref[:, sl] = dqb
            dhq = dhq + _dot(dqb, wq_ref[:, sl], NT)
        dxn, dgx = _rms_bwd(x1_ref[...], gx_ref[...], dhq)
        dgx_ref[...] += dgx
        dx1_ref[...] = dx2 + dxn

        @pl.when(i == nq - 1)
        def _():
            for h in range(XATTN_HEADS):
                sl = slice(h * hd, (h + 1) * hd)
                dk, dgk = _rms_bwd(kv_ref[:, sl], gk_ref[...], dk_acc[:, sl])
                dgk_ref[...] += dgk
                dkv_ref[:, sl] = dk.astype(BF16)
                dkv_ref[:, XATTN_WIDTH + h * hd:XATTN_WIDTH + (h + 1) * hd] = dv_acc[:, sl].astype(BF16)

    row = lambda b, i: (b * nq + i, 0)
    fixed = lambda b, i: (0, 0)
    t_len = n_batch * s_len
    return pl.pallas_call(
        body, name="xattn_bwd", grid=(n_batch, nq),
        in_specs=[pl.BlockSpec((tq, d), row), pl.BlockSpec((tq, XATTN_WIDTH), row), pl.BlockSpec((m_len, 2 * XATTN_WIDTH), lambda b, i: (b, 0)),
                  pl.BlockSpec((tq, d), row), pl.BlockSpec((1, hd), fixed), pl.BlockSpec((1, hd), fixed),
                  pl.BlockSpec((XATTN_WIDTH, d), fixed), pl.BlockSpec((1, d), fixed), pl.BlockSpec((d, XATTN_WIDTH), fixed)],
        out_specs=[pl.BlockSpec((tq, d), row), pl.BlockSpec((tq, XATTN_WIDTH), row), pl.BlockSpec((m_len, 2 * XATTN_WIDTH), lambda b, i: (b, 0)),
                   pl.BlockSpec((1, hd), fixed), pl.BlockSpec((1, hd), fixed), pl.BlockSpec((1, d), fixed)],
        out_shape=[jax.ShapeDtypeStruct((t_len, d), F32), jax.ShapeDtypeStruct((t_len, XATTN_WIDTH), BF16),
                   jax.ShapeDtypeStruct((n_batch * m_len, 2 * XATTN_WIDTH), BF16),
                   jax.ShapeDtypeStruct((1, hd), F32), jax.ShapeDtypeStruct((1, hd), F32), jax.ShapeDtypeStruct((1, d), F32)],
        scratch_shapes=[pltpu.VMEM((m_len, XATTN_WIDTH), F32), pltpu.VMEM((m_len, XATTN_WIDTH), F32)],
        compiler_params=_cparams(("arbitrary", "arbitrary")),
    )(dx2, cq, ckv, x1, gq, gk, w_co, g_x, w_cq)


def _mlp_fwd(hf, x2, target, w1, w2, tm=512, tf=1024):
    t_len, d = x2.shape
    f = w1.shape[1]
    tm, tf = min(tm, t_len), min(tf, f)
    nf = f // tf

    def body(hf_ref, x2_ref, tg_ref, w1_ref, w2_ref, u_ref, a_ref, dy_ref, ls_ref, acc_ref):
        k = pl.program_id(1)

        @pl.when(k == 0)
        def _():
            acc_ref[...] = x2_ref[...]

        u = _dot(hf_ref[...], w1_ref[...])
        u_ref[...] = u
        r = jnp.maximum(u, 0.0)
        a = (r * r).astype(BF16)
        a_ref[...] = a
        acc_ref[...] += _dot(a, w2_ref[...])

        @pl.when(k == nf - 1)
        def _():
            err = acc_ref[...] - tg_ref[...]
            dy_ref[...] = err * (1.0 / d)
            ls_ref[...] = jnp.broadcast_to(jnp.sum(jnp.sum(err * err, axis=-1, keepdims=True) * (1.0 / d), axis=0, keepdims=True), ls_ref.shape)

    row = lambda i, k: (i, 0)
    return pl.pallas_call(
        body, name="mlp_fwd", grid=(t_len // tm, nf),
        in_specs=[pl.BlockSpec((tm, d), row), pl.BlockSpec((tm, d), row), pl.BlockSpec((tm, d), row),
                  pl.BlockSpec((d, tf), lambda i, k: (0, k)), pl.BlockSpec((tf, d), lambda i, k: (k, 0))],
        out_specs=[pl.BlockSpec((tm, tf), lambda i, k: (i, k)), pl.BlockSpec((tm, tf), lambda i, k: (i, k)), pl.BlockSpec((tm, d), row),
                   pl.BlockSpec((1, 8, LANES), lambda i, k: (i, 0, 0))],
        out_shape=[jax.ShapeDtypeStruct((t_len, f), F32), jax.ShapeDtypeStruct((t_len, f), BF16), jax.ShapeDtypeStruct((t_len, d), F32),
                   jax.ShapeDtypeStruct((t_len // tm, 8, LANES), F32)],
        scratch_shapes=[pltpu.VMEM((tm, d), F32)],
        compiler_params=_cparams(("parallel", "arbitrary")),
    )(hf, x2, target, w1, w2)


def _mlp_bwd(dy, u, x2, g, w1, w2, tm=512, tf=1024):
    t_len, d = x2.shape
    f = w1.shape[1]
    tm, tf = min(tm, t_len), min(tf, f)
    nf = f // tf

    def body(dy_ref, u_ref, x2_ref, g_ref, w1_ref, w2_ref, du_ref, dx2_ref, dg_ref, acc_ref):
        i = pl.program_id(0)
        k = pl.program_id(1)

        @pl.when((i == 0) & (k == 0))
        def _():
            dg_ref[...] = jnp.zeros_like(dg_ref)

        @pl.when(k == 0)
        def _():
            acc_ref[...] = jnp.zeros_like(acc_ref)

        da = _dot(dy_ref[...], w2_ref[...], NT)
        du = (da * (2.0 * jnp.maximum(u_ref[...], 0.0))).astype(BF16)
        du_ref[...] = du
        acc_ref[...] += _dot(du, w1_ref[...], NT)

        @pl.when(k == nf - 1)
        def _():
            dxn, dg = _rms_bwd(x2_ref[...], g_ref[...], acc_ref[...])
            dx2_ref[...] = dy_ref[...] + dxn
            dg_ref[...] += dg

    row = lambda i, k: (i, 0)
    fixed = lambda i, k: (0, 0)
    return pl.pallas_call(
        body, name="mlp_bwd", grid=(t_len // tm, nf),
        in_specs=[pl.BlockSpec((tm, d), row), pl.BlockSpec((tm, tf), lambda i, k: (i, k)), pl.BlockSpec((tm, d), row), pl.BlockSpec((1, d), fixed),
                  pl.BlockSpec((d, tf), lambda i, k: (0, k)), pl.BlockSpec((tf, d), lambda i, k: (k, 0))],
        out_specs=[pl.BlockSpec((tm, tf), lambda i, k: (i, k)), pl.BlockSpec((tm, d), row), pl.BlockSpec((1, d), fixed)],
        out_shape=[jax.ShapeDtypeStruct((t_len, f), BF16), jax.ShapeDtypeStruct((t_len, d), F32), jax.ShapeDtypeStruct((1, d), F32)],
        scratch_shapes=[pltpu.VMEM((tm, d), F32)],
        compiler_params=_cparams(("arbitrary", "arbitrary")),
    )(dy, u, x2, g, w1, w2)


def _pad_lanes(v, offset=0, width=LANES):
    return jnp.zeros((1, width), F32).at[:, offset:offset + v.shape[1]].set(v)


def _col(v, offset=0, rows=SM_ROWS):
    return jnp.zeros((rows, 1), F32).at[offset:offset + v.shape[1], 0].set(v[0])


def _local_step(x, mem, target, norm_mix_g, w_in, fox_qnorm_g, fox_knorm_g, fox_f_bias, fox_onorm_g, gdn_conv_w, gdn_A_log,
                gdn_dt_bias, gdn_onorm_g, w_out, norm_xattn_g, mem_norm_g, w_cq, w_ckv, xattn_qnorm_g, xattn_knorm_g, w_co,
                norm_mlp_g, w_mlp1, w_mlp2):
    n_batch, s_len, d = x.shape
    m_len = mem.shape[1]
    t_len = n_batch * s_len
    tq = min(256, s_len)
    nq = s_len // tq
    n_chunks = s_len // GDN_CHUNK
    x2d = x.reshape(t_len, d)

    wp = jnp.concatenate([w_in[:, 0:1536], w_in[:, 1544:3080], w_in[:, 3088:3600], w_in[:, 1536:1544], w_in[:, 3080:3088],
                          jnp.zeros((d, P_DIM - 3600), BF16)], axis=1)
    wst = jnp.concatenate([w_in[:, 1536:1544], w_in[:, 3080:3088]], axis=1).T
    conv_w = jnp.concatenate([gdn_conv_w, jnp.zeros((8 - CONV_WIDTH, gdn_conv_w.shape[1]), F32)], axis=0)
    bias_col = _col(fox_f_bias, SM_F)
    gq2, gk2, go2 = (jnp.tile(g, (1, 2)) for g in (fox_qnorm_g, fox_knorm_g, fox_onorm_g))
    a_c, dt_c = _pad_lanes(gdn_A_log, SM_A), _pad_lanes(gdn_dt_bias, SM_A)
    a_r, dt_r = _col(gdn_A_log, SM_A), _col(gdn_dt_bias, SM_A)

    h1, pfox, pgdn, pz, sm, smt = _in_proj(x2d, norm_mix_g, wp, wst)
    c_rows = _fox_cum(smt, bias_col, n_batch, s_len)
    cb = c_rows.reshape(SM_ROWS, n_batch, nq, tq).transpose(1, 2, 0, 3)
    pf3 = pfox.reshape(n_batch, s_len, 1536)
    o_fox, oa, lse = _fox_fwd(pf3, cb, gq2, gk2, go2, tq)
    pg3 = pgdn.reshape(n_batch, s_len, 1536)
    qkvn = _gdn_pre(pg3, conv_w)
    z3 = pz.reshape(n_batch, s_len, GDN_WIDTH)
    smc = sm.reshape(n_batch, s_len, LANES)
    smr = smt.reshape(SM_ROWS, n_batch * n_chunks, GDN_CHUNK).transpose(1, 0, 2)
    ob, states = _gdn_fwd(qkvn, z3, smc, smr, a_c, dt_c, a_r, dt_r, gdn_onorm_g)
    oa2, ob2 = oa.reshape(t_len, FOX_WIDTH), ob.reshape(t_len, GDN_WIDTH)
    x1, hq, cq = _out_proj(x2d, oa2, ob2, w_out, norm_xattn_g, w_cq)
    mem2d = mem.reshape(n_batch * m_len, d)
    hm, ckv = _mem_kv(mem2d, mem_norm_g, w_ckv)
    co, x2, hf = _xattn_fwd(cq, ckv, x1, xattn_qnorm_g, xattn_knorm_g, w_co, norm_mlp_g, n_batch, s_len, m_len)
    u, a_act, dy, loss_tiles = _mlp_fwd(hf, x2, target.reshape(t_len, d), w_mlp1, w_mlp2)
    loss = 0.5 * jnp.sum(loss_tiles[:, 0, 0])

    grads = {}
    du, dx2, grads["norm_mlp_g"] = _mlp_bwd(dy, u, x2, norm_mlp_g, w_mlp1, w_mlp2)
    grads["w_mlp2"] = _wgrad(a_act, dy, "wgrad_mlp2")
    grads["w_mlp1"] = _wgrad(hf, du, "wgrad_mlp1")
    grads["w_co"] = _wgrad(co, dx2, "wgrad_co")
    dx1, dcq, dckv, grads["xattn_qnorm_g"], grads["xattn_knorm_g"], grads["norm_xattn_g"] = _xattn_bwd(
        dx2, cq, ckv, x1, xattn_qnorm_g, xattn_knorm_g, w_co, norm_xattn_g, w_cq, n_batch, s_len, m_len)
    grads["w_cq"] = _wgrad(hq, dcq, "wgrad_cq")
    grads["w_ckv"] = _wgrad(hm, dckv, "wgrad_ckv")
    grads["mem_norm_g"] = _mem_kv_bwd(dckv, mem2d, mem_norm_g, w_ckv)
    grads["w_out"] = _wgrad(jnp.concatenate([oa2, ob2], axis=1), dx1, "wgrad_out")
    dcat = _out_proj_bwd(dx1, w_out)
    dcat3 = dcat.reshape(n_batch, s_len, d)

    dqkvn, dz, dsmc, dsmr, dac, ddc, dar, ddr, grads["gdn_onorm_g"] = _gdn_bwd(
        qkvn, z3, smc, smr, a_c, dt_c, a_r, dt_r, gdn_onorm_g, states, dcat3)
    grads["gdn_A_log"] = dac[:, SM_A:SM_A + GDN_HEADS] + dar[SM_A:SM_A + GDN_HEADS, 0][None, :]
    grads["gdn_dt_bias"] = ddc[:, SM_A:SM_A + GDN_HEADS] + ddr[SM_A:SM_A + GDN_HEADS, 0][None, :]
    dpg, dconv = _gdn_pre_bwd(pg3, conv_w, dqkvn)
    grads["gdn_conv_w"] = dconv[0:CONV_WIDTH]

    dq, dk, dv, dcb, dgq, dgk, dgo = _fox_bwd(pf3, cb, gq2, gk2, go2, o_fox, lse, dcat3[:, :, 0:FOX_WIDTH], tq)
    fold = lambda g: g[:, 0:FOX_HEAD_DIM] + g[:, FOX_HEAD_DIM:LANES]
    grads["fox_qnorm_g"], grads["fox_knorm_g"], grads["fox_onorm_g"] = fold(dgq), fold(dgk), fold(dgo)
    dc8 = dcb[:, :, :, 0:2, :].transpose(1, 3, 0, 2, 4).reshape(FOX_HEADS, t_len)
    dc_rows = jnp.concatenate([dc8, jnp.zeros((SM_ROWS - FOX_HEADS, t_len), F32)], axis=0)
    dl_rows, dbias = _fox_cum_bwd(dc_rows, smt, bias_col, n_batch, s_len)
    grads["fox_f_bias"] = dbias[SM_F:SM_F + FOX_HEADS, 0][None, :]
    dsm_rows = jnp.concatenate([dl_rows[0:SM_B], dsmr.transpose(1, 0, 2).reshape(SM_ROWS, t_len)[SM_B:SM_ROWS]], axis=0)

    dproj = jnp.concatenate([dq.reshape(t_len, FOX_WIDTH), dk.reshape(t_len, FOX_WIDTH), dv.reshape(t_len, FOX_WIDTH),
                             dpg.reshape(t_len, 1536), dz.reshape(t_len, GDN_WIDTH), dsmc.reshape(t_len, LANES).astype(BF16)], axis=1)
    grad_x, grads["norm_mix_g"] = _in_proj_bwd(dproj, dsm_rows, x2d, norm_mix_g, wp, wst, dx1)
    dwp = _wgrad(h1, dproj, "wgrad_in", bk=256, bn=P_DIM)
    dwst = _rows_matmul(dsm_rows, h1, "wgrad_in_rows")
    dw_small = dwp[:, P_SMALL:P_SMALL + SM_ROWS] + dwst.T
    grads["w_in"] = jnp.concatenate([dwp[:, 0:1536], dw_small[:, 0:8], dwp[:, 1536:3072], dw_small[:, 8:16], dwp[:, 3072:3584]], axis=1)
    return loss, grad_x.reshape(n_batch, s_len, d), grads


MESH_ID = pl.DeviceIdType.MESH
ANY_SPEC = pl.BlockSpec(memory_space=pl.ANY)


def _place():
    x, y, c = lax.axis_index("x"), lax.axis_index("y"), lax.axis_index("c")
    return x, y, c, [(1 - x, y), (x, 1 - y), (1 - x, 1 - y)]


def _all_gather_body(n, ins, outs, send_sems, recv_sems, local_sems):
    x, y, c, chips = _place()
    me, sibling = (x, y, c), (x, y, 1 - c)

    def copy(a, k, block, to, src=None):
        dst = outs[a].at[4 * block[0] + 2 * block[1] + block[2]]
        return pltpu.make_async_remote_copy(src_ref=dst if src is None else src, dst_ref=dst, send_sem=send_sems.at[a, k],
                                            recv_sem=recv_sems.at[a, k], device_id=to, device_id_type=MESH_ID)

    mine = [pltpu.make_async_copy(ins[a], outs[a].at[4 * x + 2 * y + c], local_sems.at[a]) for a in range(n)]
    for cp in mine:
        cp.start()
    first = []
    for a in range(n):
        first.append(copy(a, 0, me, sibling, src=ins[a]))
        first += [copy(a, 1 + j, me, (*chip, c), src=ins[a]) for j, chip in enumerate(chips)]
    for cp in first:
        cp.start()
    passed = []
    for j, chip in enumerate(chips):
        for a in range(n):
            copy(a, 1 + j, (*chip, c), me).wait_recv()
            fwd = copy(a, 4 + j, (*chip, c), sibling)
            fwd.start()
            passed.append(fwd)
    for a in range(n):
        copy(a, 0, sibling, me).wait_recv()
        for j, chip in enumerate(chips):
            copy(a, 4 + j, (*chip, 1 - c), me).wait_recv()
    for cp in first + passed:
        cp.wait_send()
    for cp in mine:
        cp.wait()


def _all_gather_hbm(arrs, name):
    n = len(arrs)

    def body(*refs):
        _all_gather_body(n, refs[:n], refs[n:2 * n], *refs[2 * n:])

    return pl.pallas_call(
        body, name=name, in_specs=[ANY_SPEC] * n, out_specs=[ANY_SPEC] * n,
        out_shape=[jax.ShapeDtypeStruct((N_DEV,) + a.shape, a.dtype) for a in arrs],
        scratch_shapes=[pltpu.SemaphoreType.DMA((n, 7)), pltpu.SemaphoreType.DMA((n, 7)), pltpu.SemaphoreType.DMA((n,))],
    )(*arrs)


def _pair_exchange(arrs, name):
    n = len(arrs)

    def body(*refs):
        ins, outs = refs[:n], refs[n:2 * n]
        send_sems, recv_sems = refs[2 * n:]
        x, y, c, _ = _place()
        copies = []
        for a in range(n):
            for chip in range(4):
                copies.append(pltpu.make_async_remote_copy(
                    src_ref=ins[a].at[2 * chip + (1 - c)], dst_ref=outs[a].at[chip], send_sem=send_sems.at[a, chip],
                    recv_sem=recv_sems.at[a, chip], device_id=(x, y, 1 - c), device_id_type=MESH_ID))
        for cp in copies:
            cp.start()
        for cp in copies:
            cp.wait()

    return pl.pallas_call(
        body, name=name, in_specs=[ANY_SPEC] * n, out_specs=[ANY_SPEC] * n,
        out_shape=[jax.ShapeDtypeStruct((4,) + a.shape[1:], a.dtype) for a in arrs],
        scratch_shapes=[pltpu.SemaphoreType.DMA((n, 4)), pltpu.SemaphoreType.DMA((n, 4))],
    )(*arrs)


def _chip_exchange(arrs, name):
    n = len(arrs)

    def body(*refs):
        ins, outs = refs[:n], refs[n:2 * n]
        send_sems, recv_sems = refs[2 * n:]
        x, y, c, chips = _place()
        copies = []
        for a in range(n):
            for j, chip in enumerate(chips):
                copies.append(pltpu.make_async_remote_copy(
                    src_ref=ins[a].at[2 * chip[0] + chip[1]], dst_ref=outs[a].at[j], send_sem=send_sems.at[a, j],
                    recv_sem=recv_sems.at[a, j], device_id=(*chip, c), device_id_type=MESH_ID))
        for cp in copies:
            cp.start()
        for cp in copies:
            cp.wait()

    return pl.pallas_call(
        body, name=name, in_specs=[ANY_SPEC] * n, out_specs=[ANY_SPEC] * n,
        out_shape=[jax.ShapeDtypeStruct((3,) + a.shape[1:], a.dtype) for a in arrs],
        scratch_shapes=[pltpu.SemaphoreType.DMA((n, 3)), pltpu.SemaphoreType.DMA((n, 3))],
    )(*arrs)


def _all_gather_vmem(block, name):
    def body(in_ref, out_ref, send_sems, recv_sems, local_sems):
        _all_gather_body(1, [in_ref], [out_ref], send_sems, recv_sems, local_sems)

    vmem = pl.BlockSpec(memory_space=pltpu.VMEM)
    return pl.pallas_call(
        body, name=name, in_specs=[vmem], out_specs=vmem,
        out_shape=jax.ShapeDtypeStruct((N_DEV,) + block.shape, block.dtype),
        scratch_shapes=[pltpu.SemaphoreType.DMA((1, 7)), pltpu.SemaphoreType.DMA((1, 7)), pltpu.SemaphoreType.DMA((1,))],
    )(block)


def _row_tile(rows, cols):
    if rows <= 256:
        return rows
    return 256 if cols <= 512 else 128


def _pair_sum(core, own, got, name):
    _, rows, cols = own.shape
    tr = _row_tile(rows, cols)

    def body(c_ref, own_ref, got_ref, o_ref):
        o_ref[0] = own_ref[0] + got_ref[0]

    return pl.pallas_call(
        body, name=name,
        grid_spec=pltpu.PrefetchScalarGridSpec(
            num_scalar_prefetch=1, grid=(4, rows // tr),
            in_specs=[pl.BlockSpec((1, tr, cols), lambda k, i, c: (2 * k + c[0], i, 0)),
                      pl.BlockSpec((1, tr, cols), lambda k, i, c: (k, i, 0))],
            out_specs=pl.BlockSpec((1, tr, cols), lambda k, i, c: (k, i, 0))),
        out_shape=jax.ShapeDtypeStruct((4, rows, cols), F32),
        compiler_params=_cparams(("parallel", "parallel")),
    )(core, own, got)


def _adamw(w, g, m, v):
    m_new = ADAM_B1 * m + (1.0 - ADAM_B1) * g
    v_new = ADAM_B2 * v + (1.0 - ADAM_B2) * (g * g)
    m_hat = m_new / (1.0 - ADAM_B1 ** ADAM_STEP)
    v_hat = v_new / (1.0 - ADAM_B2 ** ADAM_STEP)
    delta = -ADAM_LR * (m_hat / (jnp.sqrt(v_hat) + ADAM_EPS) + ADAM_WD * w)
    return delta, m_new, v_new


def _sum_adam(chip, sums, parts, w, m, v, name):
    n_parts, rows, cols = parts.shape
    tr = _row_tile(rows, cols)

    def body(chip_ref, *refs):
        if sums is not None:
            g = refs[0][0].astype(F32)
            refs = refs[1:]
        p_ref, w_ref, m_ref, v_ref, g_ref, d_ref, mo_ref, vo_ref = refs
        for k in range(n_parts):
            g = p_ref[k].astype(F32) if (k == 0 and sums is None) else g + p_ref[k].astype(F32)
        g_ref[...] = g
        d_ref[...], mo_ref[...], vo_ref[...] = _adamw(w_ref[...], g, m_ref[...], v_ref[...])

    tile = pl.BlockSpec((tr, cols), lambda i, ch: (i, 0))
    out = jax.ShapeDtypeStruct((rows, cols), F32)
    own = [] if sums is None else [pl.BlockSpec((1, tr, cols), lambda i, ch: (ch[0], i, 0))]
    return pl.pallas_call(
        body, name=name,
        grid_spec=pltpu.PrefetchScalarGridSpec(
            num_scalar_prefetch=1, grid=(rows // tr,),
            in_specs=own + [pl.BlockSpec((n_parts, tr, cols), lambda i, ch: (0, i, 0)), tile, tile, tile],
            out_specs=[tile, tile, tile, tile]),
        out_shape=[out, out, out, out],
        compiler_params=_cparams(("parallel",)),
    )(chip, *([] if sums is None else [sums]), parts, w, m, v)


SHARDED = ("w_in", "gdn_conv_w", "w_out", "w_cq", "w_ckv", "w_co", "w_mlp1", "w_mlp2")
COLUMN_SHARDED = ("w_in", "gdn_conv_w", "w_co", "w_mlp1")
REPLICATED = ("norm_mix_g", "fox_qnorm_g", "fox_knorm_g", "fox_f_bias", "fox_onorm_g", "gdn_A_log", "gdn_dt_bias", "gdn_onorm_g",
              "norm_xattn_g", "mem_norm_g", "xattn_qnorm_g", "xattn_knorm_g", "norm_mlp_g")
WEIGHTS = ("norm_mix_g", "w_in", "fox_qnorm_g", "fox_knorm_g", "fox_f_bias", "fox_onorm_g", "gdn_conv_w", "gdn_A_log", "gdn_dt_bias",
           "gdn_onorm_g", "w_out", "norm_xattn_g", "mem_norm_g", "w_cq", "w_ckv", "xattn_qnorm_g", "xattn_knorm_g", "w_co",
           "norm_mlp_g", "w_mlp1", "w_mlp2")
PACK_ROWS = 16
LOSS_ROW = len(REPLICATED)


def _whole(name, gathered):
    if name in COLUMN_SHARDED:
        return gathered.transpose(1, 0, 2).reshape(gathered.shape[1], N_DEV * gathered.shape[2])
    return gathered.reshape(N_DEV * gathered.shape[1], gathered.shape[2])


def _blocks(name, whole):
    if name in COLUMN_SHARDED:
        rows, cols = whole.shape
        return whole.reshape(rows, N_DEV, cols // N_DEV).transpose(1, 0, 2)
    return whole.reshape(N_DEV, whole.shape[0] // N_DEV, whole.shape[1])


def _pack(vals, fill=0.0):
    rows = [jnp.pad(vals[k], ((0, 0), (0, D_MODEL - vals[k].shape[1])), constant_values=fill) for k in REPLICATED]
    rows.append(jnp.full((PACK_ROWS - len(rows), D_MODEL), fill, F32))
    return jnp.concatenate(rows, axis=0)


def kernel(x, mem, norm_mix_g, w_in, fox_qnorm_g, fox_knorm_g, fox_f_bias, fox_onorm_g, gdn_conv_w, gdn_A_log, gdn_dt_bias, gdn_onorm_g, w_out, norm_xattn_g, mem_norm_g, w_cq, w_ckv, xattn_qnorm_g, xattn_knorm_g, w_co, norm_mlp_g, w_mlp1, w_mlp2, loss_target, m_norm_mix_g, m_w_in, m_fox_qnorm_g, m_fox_knorm_g, m_fox_f_bias, m_fox_onorm_g, m_gdn_conv_w, m_gdn_A_log, m_gdn_dt_bias, m_gdn_onorm_g, m_w_out, m_norm_xattn_g, m_mem_norm_g, m_w_cq, m_w_ckv, m_xattn_qnorm_g, m_xattn_knorm_g, m_w_co, m_norm_mlp_g, m_w_mlp1, m_w_mlp2, v_norm_mix_g, v_w_in, v_fox_qnorm_g, v_fox_knorm_g, v_fox_f_bias, v_fox_onorm_g, v_gdn_conv_w, v_gdn_A_log, v_gdn_dt_bias, v_gdn_onorm_g, v_w_out, v_norm_xattn_g, v_mem_norm_g, v_w_cq, v_w_ckv, v_xattn_qnorm_g, v_xattn_knorm_g, v_w_co, v_norm_mlp_g, v_w_mlp1, v_w_mlp2):
    given = dict(locals())
    w = {k: given[k] for k in WEIGHTS}
    m = {k: given["m_" + k] for k in WEIGHTS}
    v = {k: given["v_" + k] for k in WEIGHTS}

    shards = [w[k][0] if k == "gdn_conv_w" else w[k][0].astype(BF16) for k in SHARDED]
    whole = {k: _whole(k, g) for k, g in zip(SHARDED, _all_gather_hbm(shards, "gather_weights"))}

    small = {k: w[k] for k in REPLICATED}
    loss_local, grad_x, grads = _local_step(x, mem, loss_target, **small, **whole)

    core = lax.axis_index("c").astype(jnp.int32).reshape(1)
    chip = (2 * lax.axis_index("x") + lax.axis_index("y")).astype(jnp.int32).reshape(1)
    own = [_blocks(k, grads[k]) for k in SHARDED]
    got = _pair_exchange(own, "grad_pair_exchange")
    sums = [_pair_sum(core, o, g, "grad_pair_sum_" + k) for k, o, g in zip(SHARDED, own, got)]
    parts = _chip_exchange(sums, "grad_chip_exchange")
    out_g, out_d, out_m, out_v = {}, {}, {}, {}
    for k, s, p in zip(SHARDED, sums, parts):
        res = _sum_adam(chip, s, p, w[k][0], m[k][0], v[k][0], "adam_" + k)
        out_g[k], out_d[k], out_m[k], out_v[k] = (r[None] for r in res)

    packed = _pack({k: grads[k] for k in REPLICATED}).at[LOSS_ROW, 0].set(loss_local)
    everyone = _all_gather_vmem(packed, "gather_small")
    res = _sum_adam(chip, None, everyone, _pack(small), _pack({k: m[k] for k in REPLICATED}),
                    _pack({k: v[k] for k in REPLICATED}, fill=1.0), "adam_small")
    for i, k in enumerate(REPLICATED):
        n = w[k].shape[1]
        out_g[k], out_d[k], out_m[k], out_v[k] = (r[i:i + 1, 0:n] for r in res)
    loss = res[0][LOSS_ROW, 0]

    return (loss, grad_x, *[out_g[k] for k in WEIGHTS], *[out_d[k] for k in WEIGHTS], *[out_m[k] for k in WEIGHTS],
            *[out_v[k] for k in WEIGHTS])
```

```python
import functools

import jax
import jax.numpy as jnp
import numpy as np
from jax import lax
from jax.experimental import pallas as pl
from jax.experimental.pallas import tpu as pltpu

F32 = jnp.float32
BF16 = jnp.bfloat16

D_MODEL = 1024
FOX_HEADS = 8
FOX_HEAD_DIM = 64
FOX_WIDTH = 512
GDN_HEADS = 4
GDN_HEAD_DIM = 128
GDN_WIDTH = 512
CONV_WIDTH = 4
GDN_CHUNK = 64
GDN_GROUP = 4
XATTN_HEADS = 4
XATTN_HEAD_DIM = 128
XATTN_WIDTH = 512
D_FF = 4096
EPS = 1e-6
NEG_INF = -1e30
N_DEV = 8

ADAM_LR = 0.001
ADAM_B1 = 0.9
ADAM_B2 = 0.999
ADAM_EPS = 1e-08
ADAM_WD = 0.01
ADAM_STEP = 10

P_FOX = 0
P_GDN = 1536
P_Z = 3072
P_SMALL = 3584
P_DIM = 3712
SM_F = 0
SM_B = 8
SM_A = 12
SM_ROWS = 16

LANES = 128
VMEM_LIMIT = 56 * 1024 * 1024

NN = (((1,), (0,)), ((), ()))
NT = (((1,), (1,)), ((), ()))
TN = (((0,), (0,)), ((), ()))


def _dot(a, b, dims=NN):
    return lax.dot_general(a.astype(BF16), b.astype(BF16), dims, preferred_element_type=F32)


def _cparams(sem=None):
    kw = dict(vmem_limit_bytes=VMEM_LIMIT)
    if sem is not None:
        kw["dimension_semantics"] = sem
    return pltpu.CompilerParams(**kw)


def _sigmoid(x):
    return 0.5 * (jnp.tanh(0.5 * x) + 1.0)


def _softplus(x):
    return jnp.maximum(x, 0.0) + jnp.log1p(jnp.exp(-jnp.abs(x)))


def _log_sigmoid(x):
    return -_softplus(-x)


def _rms(x, g):
    r = lax.rsqrt(jnp.mean(x * x, axis=-1, keepdims=True) + EPS)
    return x * r * g


def _rms_bwd(x, g, dy):
    r = lax.rsqrt(jnp.mean(x * x, axis=-1, keepdims=True) + EPS)
    xh = x * r
    dg = jnp.sum(dy * xh, axis=0, keepdims=True)
    dyg = dy * g
    dx = r * (dyg - xh * jnp.mean(dyg * xh, axis=-1, keepdims=True))
    return dx, dg


def _pair_stat(t, m0):
    s0 = jnp.sum(jnp.where(m0, t, 0.0), axis=-1, keepdims=True)
    s1 = jnp.sum(jnp.where(m0, 0.0, t), axis=-1, keepdims=True)
    return jnp.where(m0, s0, s1)


def _rms_pair(x, g, m0):
    r = lax.rsqrt(_pair_stat(x * x, m0) * (1.0 / FOX_HEAD_DIM) + EPS)
    return x * r * g


def _rms_pair_bwd(x, g, dy, m0):
    r = lax.rsqrt(_pair_stat(x * x, m0) * (1.0 / FOX_HEAD_DIM) + EPS)
    xh = x * r
    dg = jnp.sum(dy * xh, axis=0, keepdims=True)
    dyg = dy * g
    dx = r * (dyg - xh * (_pair_stat(dyg * xh, m0) * (1.0 / FOX_HEAD_DIM)))
    return dx, dg


@jax.custom_vjp
def _mm_nn(a, b):
    return _dot(a, b, NN)


_mm_nn.defvjp(lambda a, b: (_dot(a, b, NN), (a, b)),
              lambda r, g: (_dot(g, r[1], NT), _dot(r[0], g, TN)))


@jax.custom_vjp
def _mm_nt(a, b):
    return _dot(a, b, NT)


_mm_nt.defvjp(lambda a, b: (_dot(a, b, NT), (a, b)),
              lambda r, g: (_dot(g, r[1], NN), _dot(g, r[0], TN)))


@jax.custom_vjp
def _mm_tn(a, b):
    return _dot(a, b, TN)


_mm_tn.defvjp(lambda a, b: (_dot(a, b, TN), (a, b)),
              lambda r, g: (_dot(r[1], g, NT), _dot(r[0], g, NN)))


def _dot3(a, b, dims):
    ah = a.astype(BF16)
    al = (a - ah.astype(F32)).astype(BF16)
    bh = b.astype(BF16)
    bl = (b - bh.astype(F32)).astype(BF16)
    d = functools.partial(lax.dot_general, dimension_numbers=dims, preferred_element_type=F32)
    return d(ah, bh) + d(ah, bl) + d(al, bh)


@jax.custom_vjp
def _mm3(a, b):
    return _dot3(a, b, NN)


_mm3.defvjp(lambda a, b: (_dot3(a, b, NN), (a, b)),
            lambda r, g: (_dot3(g, r[1], NT), _dot3(r[0], g, TN)))


def _unit_lower_inverses(mats):
    c = mats[0].shape[0]
    eye = (lax.broadcasted_iota(jnp.int32, (c, c), 0) == lax.broadcasted_iota(jnp.int32, (c, c), 1)).astype(F32)
    xs = [eye - a for a in mats]
    ps = list(mats)
    k = 2
    while k < c + 1:
        ps = [_mm3(p, p) for p in ps]
        xs = [x + _mm3(x, p) for x, p in zip(xs, ps)]
        k *= 2
    return xs


def _wgrad(a, b, name, bk=512, bn=512, bt=512):
    t_len, k_len = a.shape
    n_len = b.shape[1]
    bk, bn, bt = min(bk, k_len), min(bn, n_len), min(bt, t_len)
    nt = t_len // bt

    def body(a_ref, b_ref, o_ref, acc_ref):
        t = pl.program_id(2)

        @pl.when(t == 0)
        def _():
            acc_ref[...] = jnp.zeros_like(acc_ref)

        acc_ref[...] += _dot(a_ref[...], b_ref[...], TN)

        @pl.when(t == nt - 1)
        def _():
            o_ref[...] = acc_ref[...]

    return pl.pallas_call(
        body, name=name, grid=(k_len // bk, n_len // bn, nt),
        in_specs=[pl.BlockSpec((bt, bk), lambda i, j, t: (t, i)), pl.BlockSpec((bt, bn), lambda i, j, t: (t, j))],
        out_specs=pl.BlockSpec((bk, bn), lambda i, j, t: (i, j)),
        out_shape=jax.ShapeDtypeStruct((k_len, n_len), F32),
        scratch_shapes=[pltpu.VMEM((bk, bn), F32)],
        compiler_params=_cparams(("parallel", "parallel", "arbitrary")),
    )(a, b)


def _rows_matmul(a, b, name, bt=512):
    r_len, t_len = a.shape
    n_len = b.shape[1]
    bt = min(bt, t_len)
    nt = t_len // bt

    def body(a_ref, b_ref, o_ref):
        t = pl.program_id(0)

        @pl.when(t == 0)
        def _():
            o_ref[...] = jnp.zeros_like(o_ref)

        o_ref[...] += _dot(a_ref[...], b_ref[...], NN)

    return pl.pallas_call(
        body, name=name, grid=(nt,),
        in_specs=[pl.BlockSpec((r_len, bt), lambda t: (0, t)), pl.BlockSpec((bt, n_len), lambda t: (t, 0))],
        out_specs=pl.BlockSpec((r_len, n_len), lambda t: (0, 0)),
        out_shape=jax.ShapeDtypeStruct((r_len, n_len), F32),
        compiler_params=_cparams(("arbitrary",)),
    )(a, b)


def _in_proj(x, g, wp, wst, tm=256):
    t_len, d = x.shape
    tm = min(tm, t_len)

    def body(x_ref, g_ref, wp_ref, wst_ref, h_ref, fox_ref, gdn_ref, z_ref, sm_ref, smt_ref):
        h = _rms(x_ref[...], g_ref[...]).astype(BF16)
        h_ref[...] = h
        p = _dot(h, wp_ref[...], NN)
        fox_ref[...] = p[:, P_FOX:P_GDN]
        gdn_ref[...] = p[:, P_GDN:P_Z]
        z_ref[...] = p[:, P_Z:P_SMALL]
        sm_ref[...] = p[:, P_SMALL:P_DIM]
        smt_ref[...] = _dot(wst_ref[...], h, NT)

    row = lambda i: (i, 0)
    fixed = lambda i: (0, 0)
    return pl.pallas_call(
        body, name="in_proj", grid=(t_len // tm,),
        in_specs=[pl.BlockSpec((tm, d), row), pl.BlockSpec((1, d), fixed), pl.BlockSpec((d, P_DIM), fixed),
                  pl.BlockSpec((SM_ROWS, d), fixed)],
        out_specs=[pl.BlockSpec((tm, d), row), pl.BlockSpec((tm, 1536), row), pl.BlockSpec((tm, 1536), row),
                   pl.BlockSpec((tm, 512), row), pl.BlockSpec((tm, LANES), row), pl.BlockSpec((SM_ROWS, tm), lambda i: (0, i))],
        out_shape=[jax.ShapeDtypeStruct((t_len, d), BF16), jax.ShapeDtypeStruct((t_len, 1536), F32),
                   jax.ShapeDtypeStruct((t_len, 1536), F32), jax.ShapeDtypeStruct((t_len, 512), F32),
                   jax.ShapeDtypeStruct((t_len, LANES), F32), jax.ShapeDtypeStruct((SM_ROWS, t_len), F32)],
        compiler_params=_cparams(("parallel",)),
    )(x, g, wp, wst)


def _in_proj_bwd(dproj, dsmt, x, g, wp, wst, dx1, tm=256):
    t_len, d = x.shape
    tm = min(tm, t_len)

    def body(dp_ref, dst_ref, x_ref, g_ref, wp_ref, wst_ref, dx1_ref, dx_ref, dg_ref):
        i = pl.program_id(0)
        dh = _dot(dp_ref[...], wp_ref[...], NT) + _dot(dst_ref[...], wst_ref[...], TN)
        dxn, dg = _rms_bwd(x_ref[...], g_ref[...], dh)
        dx_ref[...] = dx1_ref[...] + dxn

        @pl.when(i == 0)
        def _():
            dg_ref[...] = jnp.zeros_like(dg_ref)

        dg_ref[...] += dg

    row = lambda i: (i, 0)
    fixed = lambda i: (0, 0)
    return pl.pallas_call(
        body, name="in_proj_bwd", grid=(t_len // tm,),
        in_specs=[pl.BlockSpec((tm, P_DIM), row), pl.BlockSpec((SM_ROWS, tm), lambda i: (0, i)), pl.BlockSpec((tm, d), row),
                  pl.BlockSpec((1, d), fixed), pl.BlockSpec((d, P_DIM), fixed), pl.BlockSpec((SM_ROWS, d), fixed),
                  pl.BlockSpec((tm, d), row)],
        out_specs=[pl.BlockSpec((tm, d), row), pl.BlockSpec((1, d), fixed)],
        out_shape=[jax.ShapeDtypeStruct((t_len, d), F32), jax.ShapeDtypeStruct((1, d), F32)],
        compiler_params=_cparams(("arbitrary",)),
    )(dproj, dsmt, x, g, wp, wst, dx1)


def _fox_cum(smt, bias_col, n_batch, s_len, ck=256):
    ck = min(ck, s_len)

    def body(s_ref, b_ref, c_ref):
        tri = (lax.broadcasted_iota(jnp.int32, (ck, ck), 0) <= lax.broadcasted_iota(jnp.int32, (ck, ck), 1)).astype(F32)
        carry = jnp.zeros((SM_ROWS, 1), F32)
        for r in range(s_len // ck):
            ls = _log_sigmoid(s_ref[:, r * ck:(r + 1) * ck] + b_ref[...])
            c = jnp.dot(ls, tri, precision=lax.Precision.HIGHEST, preferred_element_type=F32) + carry
            c_ref[:, r * ck:(r + 1) * ck] = c
            carry = c[:, ck - 1:ck]

    return pl.pallas_call(
        body, name="fox_cum", grid=(n_batch,),
        in_specs=[pl.BlockSpec((SM_ROWS, s_len), lambda b: (0, b)), pl.BlockSpec((SM_ROWS, 1), lambda b: (0, 0))],
        out_specs=pl.BlockSpec((SM_ROWS, s_len), lambda b: (0, b)),
        out_shape=jax.ShapeDtypeStruct(smt.shape, F32),
        compiler_params=_cparams(("parallel",)),
    )(smt, bias_col)


def _fox_cum_bwd(dc, smt, bias_col, n_batch, s_len, ck=256):
    ck = min(ck, s_len)
    nr = s_len // ck

    def body(dc_ref, s_ref, b_ref, dl_ref, db_ref):
        b = pl.program_id(0)
        tri = (lax.broadcasted_iota(jnp.int32, (ck, ck), 0) >= lax.broadcasted_iota(jnp.int32, (ck, ck), 1)).astype(F32)
        carry = jnp.zeros((SM_ROWS, 1), F32)
        tot = jnp.zeros((SM_ROWS, 1), F32)
        for r in reversed(range(nr)):
            sl = slice(r * ck, (r + 1) * ck)
            dls = jnp.dot(dc_ref[:, sl], tri, precision=lax.Precision.HIGHEST, preferred_element_type=F32) + carry
            carry = dls[:, 0:1]
            dl = dls * (1.0 - _sigmoid(s_ref[:, sl] + b_ref[...]))
            dl_ref[:, sl] = dl
            tot = tot + jnp.sum(dl, axis=1, keepdims=True)

        @pl.when(b == 0)
        def _():
            db_ref[...] = jnp.zeros_like(db_ref)

        db_ref[...] += jnp.broadcast_to(tot, db_ref.shape)

    return pl.pallas_call(
        body, name="fox_cum_bwd", grid=(n_batch,),
        in_specs=[pl.BlockSpec((SM_ROWS, s_len), lambda b: (0, b)), pl.BlockSpec((SM_ROWS, s_len), lambda b: (0, b)),
                  pl.BlockSpec((SM_ROWS, 1), lambda b: (0, 0))],
        out_specs=[pl.BlockSpec((SM_ROWS, s_len), lambda b: (0, b)), pl.BlockSpec((SM_ROWS, LANES), lambda b: (0, 0))],
        out_shape=[jax.ShapeDtypeStruct(smt.shape, F32), jax.ShapeDtypeStruct((SM_ROWS, LANES), F32)],
        compiler_params=_cparams(("arbitrary",)),
    )(dc, smt, bias_col)


def _fox_masks(tq, tk, i, kb):
    qpos = i * tq + lax.broadcasted_iota(jnp.int32, (tq, tk), 0)
    kpos = kb * tk + lax.broadcasted_iota(jnp.int32, (tq, tk), 1)
    return kpos <= qpos


def _fox_fwd(pf, cb, gq2, gk2, go2, tq=256):
    n_batch, s_len, _ = pf.shape
    tq = min(tq, s_len)
    nq = s_len // tq
    scale = FOX_HEAD_DIM ** -0.5

    def body(q_ref, k_ref, v_ref, c_ref, gq_ref, gk_ref, go_ref, o_ref, on_ref, lse_ref, kh_ref, vh_ref):
        j = pl.program_id(1)
        i = pl.program_id(2)
        m0 = lax.broadcasted_iota(jnp.int32, (1, LANES), 1) < FOX_HEAD_DIM

        @pl.when(i == 0)
        def _():
            kn = _rms_pair(k_ref[0], gk_ref[...], m0)
            kh_ref[0] = jnp.where(m0, kn, 0.0).astype(BF16)
            kh_ref[1] = jnp.where(m0, 0.0, kn).astype(BF16)
            v = v_ref[0]
            vh_ref[0] = jnp.where(m0, v, 0.0).astype(BF16)
            vh_ref[1] = jnp.where(m0, 0.0, v).astype(BF16)

        qb = _rms_pair(q_ref[0], gq_ref[...], m0).astype(BF16)

        def step(kb, carry):
            ms, ls, acc = carry
            off = pl.multiple_of(kb * tq, tq)
            mask = _fox_masks(tq, tq, i, kb)
            new_m, new_l, alphas, pv = [], [], [], []
            for hh in range(2):
                s = _dot(qb, kh_ref[hh, pl.ds(off, tq), :], NT) * scale
                s = s - c_ref[0, kb, pl.ds(2 * j + hh, 1), :]
                s = jnp.where(mask, s, NEG_INF)
                m_new = jnp.maximum(ms[hh], jnp.max(s, axis=-1, keepdims=True))
                alpha = jnp.exp(ms[hh] - m_new)
                p = jnp.exp(s - m_new)
                new_l.append(alpha * ls[hh] + jnp.sum(p, axis=-1, keepdims=True))
                new_m.append(m_new)
                alphas.append(alpha)
                pv.append(_dot(p, vh_ref[hh, pl.ds(off, tq), :], NN))
            acc = jnp.where(m0, alphas[0], alphas[1]) * acc + pv[0] + pv[1]
            return tuple(new_m), tuple(new_l), acc

        init_m = (jnp.full((tq, 1), NEG_INF, F32),) * 2
        init_l = (jnp.zeros((tq, 1), F32),) * 2
        ms, ls, acc = lax.fori_loop(0, i + 1, step, (init_m, init_l, jnp.zeros((tq, LANES), F32)))
        o = acc / jnp.where(m0, ls[0], ls[1])
        o_ref[0] = o
        on_ref[0] = _rms_pair(o, go_ref[...], m0).astype(BF16)
        lse_ref[0] = jnp.where(m0, ms[0] + jnp.log(ls[0]), ms[1] + jnp.log(ls[1]))

    fixed = lambda b, j, i: (0, 0)
    tile = lambda b, j, i: (b, i, j)
    return pl.pallas_call(
        body, name="fox_fwd", grid=(n_batch, 4, nq),
        in_specs=[pl.BlockSpec((1, tq, LANES), tile), pl.BlockSpec((1, s_len, LANES), lambda b, j, i: (b, 0, 4 + j)),
                  pl.BlockSpec((1, s_len, LANES), lambda b, j, i: (b, 0, 8 + j)),
                  pl.BlockSpec((1, nq, SM_ROWS, tq), lambda b, j, i: (b, 0, 0, 0)),
                  pl.BlockSpec((1, LANES), fixed), pl.BlockSpec((1, LANES), fixed), pl.BlockSpec((1, LANES), fixed)],
        out_specs=[pl.BlockSpec((1, tq, LANES), tile), pl.BlockSpec((1, tq, LANES), tile), pl.BlockSpec((1, tq, LANES), tile)],
        out_shape=[jax.ShapeDtypeStruct((n_batch, s_len, FOX_WIDTH), F32), jax.ShapeDtypeStruct((n_batch, s_len, FOX_WIDTH), BF16),
                   jax.ShapeDtypeStruct((n_batch, s_len, FOX_WIDTH), F32)],
        scratch_shapes=[pltpu.VMEM((2, s_len, LANES), BF16), pltpu.VMEM((2, s_len, LANES), BF16)],
        compiler_params=_cparams(("parallel", "parallel", "arbitrary")),
    )(pf, pf, pf, cb, gq2, gk2, go2)


def _fox_bwd(pf, cb, gq2, gk2, go2, o, lse, don, tq=256):
    n_batch, s_len, _ = pf.shape
    tq = min(tq, s_len)
    nq = s_len // tq
    scale = FOX_HEAD_DIM ** -0.5

    def body(q_ref, k_ref, v_ref, c_ref, gq_ref, gk_ref, go_ref, o_ref, lse_ref, don_ref,
             dq_ref, dk_ref, dv_ref, dc_ref, dgq_ref, dgk_ref, dgo_ref, kh_ref, vh_ref, dka_ref, dva_ref, dca_ref):
        b = pl.program_id(0)
        j = pl.program_id(1)
        i = pl.program_id(2)
        m0 = lax.broadcasted_iota(jnp.int32, (1, LANES), 1) < FOX_HEAD_DIM

        @pl.when((b == 0) & (j == 0) & (i == 0))
        def _():
            dgq_ref[...] = jnp.zeros_like(dgq_ref)
            dgk_ref[...] = jnp.zeros_like(dgk_ref)
            dgo_ref[...] = jnp.zeros_like(dgo_ref)

        @pl.when(i == 0)
        def _():
            kn = _rms_pair(k_ref[0], gk_ref[...], m0)
            kh_ref[0] = jnp.where(m0, kn, 0.0).astype(BF16)
            kh_ref[1] = jnp.where(m0, 0.0, kn).astype(BF16)
            v = v_ref[0]
            vh_ref[0] = jnp.where(m0, v, 0.0).astype(BF16)
            vh_ref[1] = jnp.where(m0, 0.0, v).astype(BF16)
            dka_ref[...] = jnp.zeros_like(dka_ref)
            dva_ref[...] = jnp.zeros_like(dva_ref)
            dca_ref[...] = jnp.zeros_like(dca_ref)

        q = q_ref[0]
        qn = _rms_pair(q, gq_ref[...], m0)
        qb = qn.astype(BF16)
        qh = (jnp.where(m0, qn, 0.0).astype(BF16), jnp.where(m0, 0.0, qn).astype(BF16))
        ot = o_ref[0]
        do, dgo = _rms_pair_bwd(ot, go_ref[...], don_ref[0], m0)
        dgo_ref[...] += dgo
        dd = do * ot
        delta = (jnp.sum(jnp.where(m0, dd, 0.0), axis=-1, keepdims=True), jnp.sum(jnp.where(m0, 0.0, dd), axis=-1, keepdims=True))
        doh = (jnp.where(m0, do, 0.0).astype(BF16), jnp.where(m0, 0.0, do).astype(BF16))
        lse_t = lse_ref[0]
        lse_h = (lse_t[:, 0:1], lse_t[:, FOX_HEAD_DIM:FOX_HEAD_DIM + 1])

        def step(kb, carry):
            dqn, rs = carry
            rs = list(rs)
            off = pl.multiple_of(kb * tq, tq)
            mask = _fox_masks(tq, tq, i, kb)
            for hh in range(2):
                kblk = kh_ref[hh, pl.ds(off, tq), :]
                vblk = vh_ref[hh, pl.ds(off, tq), :]
                s = _dot(qb, kblk, NT) * scale
                s = s - c_ref[0, kb, pl.ds(2 * j + hh, 1), :]
                s = jnp.where(mask, s, NEG_INF)
                p = jnp.exp(s - lse_h[hh])
                dp = _dot(doh[hh], vblk, NT)
                ds = p * (dp - delta[hh])
                dva_ref[pl.ds(off, tq), :] += _dot(p, doh[hh], TN)
                dka_ref[pl.ds(off, tq), :] += _dot(ds, qh[hh], TN) * scale
                dca_ref[kb, hh:hh + 1, :] += -jnp.sum(ds, axis=0, keepdims=True)
                rs[hh] = rs[hh] + jnp.sum(ds, axis=-1, keepdims=True)
                dqn = dqn + _dot(ds, kblk, NN) * scale
            return dqn, tuple(rs)

        dqn, rs = lax.fori_loop(0, i + 1, step, (jnp.zeros((tq, LANES), F32), (jnp.zeros((tq, 1), F32),) * 2))
        rs_rows = jnp.where(m0, rs[0], rs[1]).T
        dca_ref[i, 0:1, :] += rs_rows[0:1, :]
        dca_ref[i, 1:2, :] += rs_rows[FOX_HEAD_DIM:FOX_HEAD_DIM + 1, :]
        dq, dgq = _rms_pair_bwd(q, gq_ref[...], dqn, m0)
        dq_ref[0] = dq.astype(BF16)
        dgq_ref[...] += dgq

        @pl.when(i == nq - 1)
        def _():
            dk, dgk = _rms_pair_bwd(k_ref[0], gk_ref[...], dka_ref[...], m0)
            dk_ref[0] = dk.astype(BF16)
            dgk_ref[...] += dgk
            dv_ref[0] = dva_ref[...].astype(BF16)
            dc_ref[0, 0] = dca_ref[...]

    fixed = lambda b, j, i: (0, 0)
    tile = lambda b, j, i: (b, i, j)
    full = lambda b, j, i: (b, 0, j)
    wide = jax.ShapeDtypeStruct((n_batch, s_len, FOX_WIDTH), BF16)
    gain = jax.ShapeDtypeStruct((1, LANES), F32)
    return pl.pallas_call(
        body, name="fox_bwd", grid=(n_batch, 4, nq),
        in_specs=[pl.BlockSpec((1, tq, LANES), tile), pl.BlockSpec((1, s_len, LANES), lambda b, j, i: (b, 0, 4 + j)),
                  pl.BlockSpec((1, s_len, LANES), lambda b, j, i: (b, 0, 8 + j)),
                  pl.BlockSpec((1, nq, SM_ROWS, tq), lambda b, j, i: (b, 0, 0, 0)),
                  pl.BlockSpec((1, LANES), fixed), pl.BlockSpec((1, LANES), fixed), pl.BlockSpec((1, LANES), fixed),
                  pl.BlockSpec((1, tq, LANES), tile), pl.BlockSpec((1, tq, LANES), tile), pl.BlockSpec((1, tq, LANES), tile)],
        out_specs=[pl.BlockSpec((1, tq, LANES), tile), pl.BlockSpec((1, s_len, LANES), full), pl.BlockSpec((1, s_len, LANES), full),
                   pl.BlockSpec((1, 1, nq, 8, tq), lambda b, j, i: (b, j, 0, 0, 0)),
                   pl.BlockSpec((1, LANES), fixed), pl.BlockSpec((1, LANES), fixed), pl.BlockSpec((1, LANES), fixed)],
        out_shape=[wide, wide, wide, jax.ShapeDtypeStruct((n_batch, 4, nq, 8, tq), F32), gain, gain, gain],
        scratch_shapes=[pltpu.VMEM((2, s_len, LANES), BF16), pltpu.VMEM((2, s_len, LANES), BF16),
                        pltpu.VMEM((s_len, LANES), F32), pltpu.VMEM((s_len, LANES), F32), pltpu.VMEM((nq, 8, tq), F32)],
        compiler_params=_cparams(("arbitrary", "arbitrary", "arbitrary")),
    )(pf, pf, pf, cb, gq2, gk2, go2, o, lse, don)


def _shift_down(x, k):
    row = lax.broadcasted_iota(jnp.int32, x.shape, 0)
    return jnp.where(row >= k, pltpu.roll(x, k, 0), 0.0)


def _shift_up(x, k):
    n = x.shape[0]
    row = lax.broadcasted_iota(jnp.int32, x.shape, 0)
    return jnp.where(row < n - k, pltpu.roll(x, n - k, 0), 0.0)


def _conv_silu(x, w):
    y = w[3:4] * x + w[2:3] * _shift_down(x, 1) + w[1:2] * _shift_down(x, 2) + w[0:1] * _shift_down(x, 3)
    return y, y * _sigmoid(y)


def _gdn_pre(pg, conv_w):
    n_batch, s_len, width = pg.shape
    ncb = width // LANES

    def body(x_ref, w_ref, o_ref):
        cb = pl.program_id(1)
        _, s = _conv_silu(x_ref[0], w_ref[...])
        sn = s * lax.rsqrt(jnp.sum(s * s, axis=-1, keepdims=True) + EPS)
        o_ref[0] = jnp.where(cb < 2 * GDN_HEADS, sn, s)

    return pl.pallas_call(
        body, name="gdn_pre", grid=(n_batch, ncb),
        in_specs=[pl.BlockSpec((1, s_len, LANES), lambda b, c: (b, 0, c)), pl.BlockSpec((8, LANES), lambda b, c: (0, c))],
        out_specs=pl.BlockSpec((1, s_len, LANES), lambda b, c: (b, 0, c)),
        out_shape=jax.ShapeDtypeStruct(pg.shape, F32),
        compiler_params=_cparams(("parallel", "parallel")),
    )(pg, conv_w)


def _gdn_pre_bwd(pg, conv_w, dout):
    n_batch, s_len, width = pg.shape
    ncb = width // LANES

    def body(x_ref, w_ref, d_ref, dx_ref, dw_ref):
        cb = pl.program_id(0)
        b = pl.program_id(1)
        x = x_ref[0]
        w = w_ref[...]
        d = d_ref[0]
        y, s = _conv_silu(x, w)
        rr = lax.rsqrt(jnp.sum(s * s, axis=-1, keepdims=True) + EPS)
        sn = s * rr
        ds_n = rr * (d - sn * jnp.sum(d * sn, axis=-1, keepdims=True))
        ds = jnp.where(cb < 2 * GDN_HEADS, ds_n, d)
        sig = _sigmoid(y)
        dy = ds * (sig * (1.0 + y * (1.0 - sig)))
        dx = w[3:4] * dy + w[2:3] * _shift_up(dy, 1) + w[1:2] * _shift_up(dy, 2) + w[0:1] * _shift_up(dy, 3)
        dx_ref[0] = dx.astype(BF16)
        dw = [jnp.sum(dy * _shift_down(x, 3 - jj), axis=0, keepdims=True) if jj < 3 else jnp.sum(dy * x, axis=0, keepdims=True)
              for jj in range(CONV_WIDTH)]
        rows = lax.broadcasted_iota(jnp.int32, (8, LANES), 0)
        dwb = jnp.zeros((8, LANES), F32)
        for jj in range(CONV_WIDTH):
            dwb = dwb + jnp.where(rows == jj, dw[jj], 0.0)

        @pl.when(b == 0)
        def _():
            dw_ref[...] = jnp.zeros_like(dw_ref)

        dw_ref[...] += dwb

    blk = lambda c, b: (b, 0, c)
    return pl.pallas_call(
        body, name="gdn_pre_bwd", grid=(ncb, n_batch),
        in_specs=[pl.BlockSpec((1, s_len, LANES), blk), pl.BlockSpec((8, LANES), lambda c, b: (0, c)), pl.BlockSpec((1, s_len, LANES), blk)],
        out_specs=[pl.BlockSpec((1, s_len, LANES), blk), pl.BlockSpec((8, LANES), lambda c, b: (0, c))],
        out_shape=[jax.ShapeDtypeStruct(pg.shape, BF16), jax.ShapeDtypeStruct((8, width), F32)],
        compiler_params=_cparams(("parallel", "arbitrary")),
    )(pg, conv_w, dout)


def _gdn_gates(smc, smr, a_c, dt_c, a_r, dt_r, h):
    lane = lax.broadcasted_iota(jnp.int32, (1, LANES), 1)
    sub = lax.broadcasted_iota(jnp.int32, (SM_ROWS, 1), 0)
    beta_c = jnp.sum(jnp.where(lane == SM_B + h, _sigmoid(smc), 0.0), axis=1, keepdims=True)
    g_all_c = -jnp.exp(a_c) * _softplus(smc + dt_c)
    g_c = jnp.sum(jnp.where(lane == SM_A + h, g_all_c, 0.0), axis=1, keepdims=True)
    g_all_r = -jnp.exp(a_r) * _softplus(smr + dt_r)
    g_r = jnp.sum(jnp.where(sub == SM_A + h, g_all_r, 0.0), axis=0, keepdims=True)
    return beta_c, g_c, g_r


def _gdn_group(qkv, z, smc, smr, a_c, dt_c, a_r, dt_r, go, states):
    n_grp = len(qkv)
    c = qkv[0].shape[0]
    hd = GDN_HEAD_DIM
    pairs = [(g, h) for g in range(n_grp) for h in range(GDN_HEADS)]
    ii = lax.broadcasted_iota(jnp.int32, (c, c), 0)
    jj = lax.broadcasted_iota(jnp.int32, (c, c), 1)
    incl = ii >= jj
    col = lambda arr, base, h: arr[:, base + h * hd:base + (h + 1) * hd]

    qs, ks, kbs, vbs, decays, gcs, g_lasts, amats = [], [], [], [], [], [], [], []
    for g, h in pairs:
        beta_c, g_c, g_r = _gdn_gates(smc[g], smr[g], a_c, dt_c, a_r, dt_r, h)
        gc_c = jnp.sum(jnp.where(incl, g_r, 0.0), axis=1, keepdims=True)
        gc_r = jnp.sum(jnp.where(ii <= jj, g_c, 0.0), axis=0, keepdims=True)
        decay = jnp.where(incl, jnp.exp(jnp.where(incl, gc_c - gc_r, 0.0)), 0.0)
        k = col(qkv[g], GDN_WIDTH, h)
        kb = k * beta_c
        qs.append(col(qkv[g], 0, h) * (hd ** -0.5))
        ks.append(k)
        kbs.append(kb)
        vbs.append(col(qkv[g], 2 * GDN_WIDTH, h) * beta_c)
        decays.append(decay)
        gcs.append(gc_c)
        g_lasts.append(jnp.sum(g_c, axis=0, keepdims=True))
        amats.append(jnp.where(ii > jj, _mm_nt(kb, k) * decay, 0.0))
    ts = _unit_lower_inverses(amats)
    egcs = [jnp.exp(gc) for gc in gcs]
    us = [_mm_nn(t, vb) for t, vb in zip(ts, vbs)]
    ws = [_mm_nn(t, kb * e) for t, kb, e in zip(ts, kbs, egcs)]
    intras = [_mm_nt(q, k) * d for q, k, d in zip(qs, ks, decays)]
    qes = [q * e for q, e in zip(qs, egcs)]
    kds = [k * jnp.exp(gl - gc) for k, gl, gc in zip(ks, g_lasts, gcs)]
    sdecs = [jnp.exp(gl) for gl in g_lasts]

    outs = []
    for g in range(n_grp):
        idx = [g * GDN_HEADS + h for h in range(GDN_HEADS)]
        v_new = [us[i] - _mm_nn(ws[i], states[h]) for h, i in enumerate(idx)]
        o_state = [_mm_nn(qes[i], states[h]) for h, i in enumerate(idx)]
        o_intra = [_mm_nn(intras[i], v_new[h]) for h, i in enumerate(idx)]
        states = [states[h] * sdecs[i] + _mm_tn(kds[i], v_new[h]) for h, i in enumerate(idx)]
        outs.append([_rms(o_state[h] + o_intra[h], go) * (col(z[g], 0, h) * _sigmoid(col(z[g], 0, h))) for h in range(GDN_HEADS)])
    return outs, states


def _gdn_group_size(n_chunks):
    return GDN_GROUP if n_chunks % GDN_GROUP == 0 else 1


def _gdn_fwd(qkvn, z, smc, smr, a_c, dt_c, a_r, dt_r, go):
    n_batch, s_len, _ = qkvn.shape
    c = GDN_CHUNK
    n = s_len // c
    grp = _gdn_group_size(n)
    ng = n // grp
    gc = grp * c
    hd = GDN_HEAD_DIM

    def body(qkv_ref, z_ref, smc_ref, smr_ref, ac_ref, dc_ref, ar_ref, dr_ref, go_ref, og_ref, st_ref, s_ref):
        @pl.when(pl.program_id(1) == 0)
        def _():
            s_ref[...] = jnp.zeros_like(s_ref)

        states = [s_ref[h] for h in range(GDN_HEADS)]
        for h in range(GDN_HEADS):
            st_ref[0, 0, h] = states[h]
        rows = lambda k: slice(k * c, (k + 1) * c)
        outs, nxt = _gdn_group([qkv_ref[0, rows(k), :] for k in range(grp)], [z_ref[0, rows(k), :] for k in range(grp)],
                               [smc_ref[0, rows(k), :] for k in range(grp)], [smr_ref[k] for k in range(grp)],
                               ac_ref[...], dc_ref[...], ar_ref[...], dr_ref[...], go_ref[...], states)
        for k in range(grp):
            for h in range(GDN_HEADS):
                og_ref[0, rows(k), h * hd:(h + 1) * hd] = outs[k][h].astype(BF16)
        for h in range(GDN_HEADS):
            s_ref[h] = nxt[h]

    tok = lambda b, i: (b, i, 0)
    fixed = lambda b, i: (0, 0)
    return pl.pallas_call(
        body, name="gdn_fwd", grid=(n_batch, ng),
        in_specs=[pl.BlockSpec((1, gc, 3 * GDN_WIDTH), tok), pl.BlockSpec((1, gc, GDN_WIDTH), tok), pl.BlockSpec((1, gc, LANES), tok),
                  pl.BlockSpec((grp, SM_ROWS, c), lambda b, i: (b * ng + i, 0, 0)),
                  pl.BlockSpec((1, LANES), fixed), pl.BlockSpec((1, LANES), fixed), pl.BlockSpec((SM_ROWS, 1), fixed),
                  pl.BlockSpec((SM_ROWS, 1), fixed), pl.BlockSpec((1, LANES), fixed)],
        out_specs=[pl.BlockSpec((1, gc, GDN_WIDTH), tok), pl.BlockSpec((1, 1, GDN_HEADS, hd, hd), lambda b, i: (b, i, 0, 0, 0))],
        out_shape=[jax.ShapeDtypeStruct((n_batch, s_len, GDN_WIDTH), BF16), jax.ShapeDtypeStruct((n_batch, ng, GDN_HEADS, hd, hd), F32)],
        scratch_shapes=[pltpu.VMEM((GDN_HEADS, hd, hd), F32)],
        compiler_params=_cparams(("parallel", "arbitrary")),
    )(qkvn, z, smc, smr, a_c, dt_c, a_r, dt_r, go)


def _gdn_bwd(qkvn, z, smc, smr, a_c, dt_c, a_r, dt_r, go, states, dog):
    n_batch, s_len, _ = qkvn.shape
    c = GDN_CHUNK
    n = s_len // c
    grp = _gdn_group_size(n)
    ng = n // grp
    gc = grp * c
    hd = GDN_HEAD_DIM

    def body(qkv_ref, z_ref, smc_ref, smr_ref, ac_ref, dc_ref, ar_ref, dr_ref, go_ref, st_ref, dog_ref,
             dqkv_ref, dz_ref, dsmc_ref, dsmr_ref, dac_ref, ddc_ref, dar_ref, ddr_ref, dgo_ref, ds_ref):
        first = (pl.program_id(0) == 0) & (pl.program_id(1) == 0)

        @pl.when(pl.program_id(1) == 0)
        def _():
            ds_ref[...] = jnp.zeros_like(ds_ref)

        @pl.when(first)
        def _():
            for r in (dac_ref, ddc_ref, dar_ref, ddr_ref, dgo_ref):
                r[...] = jnp.zeros_like(r)

        rows = lambda k: slice(k * c, (k + 1) * c)
        states = [st_ref[0, 0, h] for h in range(GDN_HEADS)]
        prim = ([qkv_ref[0, rows(k), :] for k in range(grp)], [z_ref[0, rows(k), :] for k in range(grp)],
                [smc_ref[0, rows(k), :] for k in range(grp)], [smr_ref[k] for k in range(grp)],
                ac_ref[...], dc_ref[...], ar_ref[...], dr_ref[...], go_ref[...], states)
        _, vjp = jax.vjp(_gdn_group, *prim)
        cot = ([[dog_ref[0, rows(k), h * hd:(h + 1) * hd] for h in range(GDN_HEADS)] for k in range(grp)],
               [ds_ref[h] for h in range(GDN_HEADS)])
        dqkv, dz, dsmc, dsmr, dac, ddc, dar, ddr, dgo, dstates = vjp(cot)
        for k in range(grp):
            dqkv_ref[0, rows(k), :] = dqkv[k]
            dz_ref[0, rows(k), :] = dz[k].astype(BF16)
            dsmc_ref[0, rows(k), :] = dsmc[k]
            dsmr_ref[k] = dsmr[k]
        dac_ref[...] += dac
        ddc_ref[...] += ddc
        dar_ref[...] += dar
        ddr_ref[...] += ddr
        dgo_ref[...] += dgo
        for h in range(GDN_HEADS):
            ds_ref[h] = dstates[h]

    tok = lambda b, i: (b, ng - 1 - i, 0)
    fixed = lambda b, i: (0, 0)
    lane_vec = jax.ShapeDtypeStruct((1, LANES), F32)
    row_vec = jax.ShapeDtypeStruct((SM_ROWS, 1), F32)
    return pl.pallas_call(
        body, name="gdn_bwd", grid=(n_batch, ng),
        in_specs=[pl.BlockSpec((1, gc, 3 * GDN_WIDTH), tok), pl.BlockSpec((1, gc, GDN_WIDTH), tok), pl.BlockSpec((1, gc, LANES), tok),
                  pl.BlockSpec((grp, SM_ROWS, c), lambda b, i: (b * ng + ng - 1 - i, 0, 0)),
                  pl.BlockSpec((1, LANES), fixed), pl.BlockSpec((1, LANES), fixed), pl.BlockSpec((SM_ROWS, 1), fixed),
                  pl.BlockSpec((SM_ROWS, 1), fixed), pl.BlockSpec((1, LANES), fixed),
                  pl.BlockSpec((1, 1, GDN_HEADS, hd, hd), lambda b, i: (b, ng - 1 - i, 0, 0, 0)),
                  pl.BlockSpec((1, gc, GDN_WIDTH), lambda b, i: (b, ng - 1 - i, 1))],
        out_specs=[pl.BlockSpec((1, gc, 3 * GDN_WIDTH), tok), pl.BlockSpec((1, gc, GDN_WIDTH), tok), pl.BlockSpec((1, gc, LANES), tok),
                   pl.BlockSpec((grp, SM_ROWS, c), lambda b, i: (b * ng + ng - 1 - i, 0, 0)),
                   pl.BlockSpec((1, LANES), fixed), pl.BlockSpec((1, LANES), fixed), pl.BlockSpec((SM_ROWS, 1), fixed),
                   pl.BlockSpec((SM_ROWS, 1), fixed), pl.BlockSpec((1, LANES), fixed)],
        out_shape=[jax.ShapeDtypeStruct((n_batch, s_len, 3 * GDN_WIDTH), F32), jax.ShapeDtypeStruct((n_batch, s_len, GDN_WIDTH), BF16),
                   jax.ShapeDtypeStruct((n_batch, s_len, LANES), F32), jax.ShapeDtypeStruct((n_batch * n, SM_ROWS, c), F32),
                   lane_vec, lane_vec, row_vec, row_vec, lane_vec],
        scratch_shapes=[pltpu.VMEM((GDN_HEADS, hd, hd), F32)],
        compiler_params=_cparams(("arbitrary", "arbitrary")),
    )(qkvn, z, smc, smr, a_c, dt_c, a_r, dt_r, go, states, dog)


def _out_proj(x, oa, ob, w_out, g_x, w_cq, tm=256):
    t_len, d = x.shape
    tm = min(tm, t_len)

    def body(x_ref, oa_ref, ob_ref, wo_ref, g_ref, wq_ref, x1_ref, hq_ref, cq_ref):
        x1 = x_ref[...] + _dot(oa_ref[...], wo_ref[0:FOX_WIDTH, :]) + _dot(ob_ref[...], wo_ref[FOX_WIDTH:2 * FOX_WIDTH, :])
        x1_ref[...] = x1
        hq = _rms(x1, g_ref[...]).astype(BF16)
        hq_ref[...] = hq
        cq_ref[...] = _dot(hq, wq_ref[...])

    row = lambda i: (i, 0)
    fixed = lambda i: (0, 0)
    return pl.pallas_call(
        body, name="out_proj", grid=(t_len // tm,),
        in_specs=[pl.BlockSpec((tm, d), row), pl.BlockSpec((tm, FOX_WIDTH), row), pl.BlockSpec((tm, GDN_WIDTH), row),
                  pl.BlockSpec((d, d), fixed), pl.BlockSpec((1, d), fixed), pl.BlockSpec((d, XATTN_WIDTH), fixed)],
        out_specs=[pl.BlockSpec((tm, d), row), pl.BlockSpec((tm, d), row), pl.BlockSpec((tm, XATTN_WIDTH), row)],
        out_shape=[jax.ShapeDtypeStruct((t_len, d), F32), jax.ShapeDtypeStruct((t_len, d), BF16), jax.ShapeDtypeStruct((t_len, XATTN_WIDTH), F32)],
        compiler_params=_cparams(("parallel",)),
    )(x, oa, ob, w_out, g_x, w_cq)


def _out_proj_bwd(dx1, w_out, tm=512):
    t_len, d = dx1.shape
    tm = min(tm, t_len)

    def body(dx_ref, w_ref, o_ref):
        o_ref[...] = _dot(dx_ref[...], w_ref[...], NT)

    return pl.pallas_call(
        body, name="out_proj_bwd", grid=(t_len // tm,),
        in_specs=[pl.BlockSpec((tm, d), lambda i: (i, 0)), pl.BlockSpec((d, d), lambda i: (0, 0))],
        out_specs=pl.BlockSpec((tm, d), lambda i: (i, 0)),
        out_shape=jax.ShapeDtypeStruct((t_len, d), F32),
        compiler_params=_cparams(("parallel",)),
    )(dx1, w_out)


def _mem_kv(mem, g, w_ckv, tm=256):
    t_len, d = mem.shape
    tm = min(tm, t_len)

    def body(x_ref, g_ref, w_ref, h_ref, o_ref):
        h = _rms(x_ref[...], g_ref[...]).astype(BF16)
        h_ref[...] = h
        o_ref[...] = _dot(h, w_ref[...])

    row = lambda i: (i, 0)
    fixed = lambda i: (0, 0)
    return pl.pallas_call(
        body, name="mem_kv", grid=(t_len // tm,),
        in_specs=[pl.BlockSpec((tm, d), row), pl.BlockSpec((1, d), fixed), pl.BlockSpec((d, 2 * XATTN_WIDTH), fixed)],
        out_specs=[pl.BlockSpec((tm, d), row), pl.BlockSpec((tm, 2 * XATTN_WIDTH), row)],
        out_shape=[jax.ShapeDtypeStruct((t_len, d), BF16), jax.ShapeDtypeStruct((t_len, 2 * XATTN_WIDTH), F32)],
        compiler_params=_cparams(("parallel",)),
    )(mem, g, w_ckv)


def _mem_kv_bwd(dckv, mem, g, w_ckv, tm=256):
    t_len, d = mem.shape
    tm = min(tm, t_len)

    def body(d_ref, x_ref, g_ref, w_ref, dg_ref):
        @pl.when(pl.program_id(0) == 0)
        def _():
            dg_ref[...] = jnp.zeros_like(dg_ref)

        dh = _dot(d_ref[...], w_ref[...], NT)
        _, dg = _rms_bwd(x_ref[...], g_ref[...], dh)
        dg_ref[...] += dg

    row = lambda i: (i, 0)
    fixed = lambda i: (0, 0)
    return pl.pallas_call(
        body, name="mem_kv_bwd", grid=(t_len // tm,),
        in_specs=[pl.BlockSpec((tm, 2 * XATTN_WIDTH), row), pl.BlockSpec((tm, d), row), pl.BlockSpec((1, d), fixed),
                  pl.BlockSpec((d, 2 * XATTN_WIDTH), fixed)],
        out_specs=pl.BlockSpec((1, d), fixed),
        out_shape=jax.ShapeDtypeStruct((1, d), F32),
        compiler_params=_cparams(("arbitrary",)),
    )(dckv, mem, g, w_ckv)


def _xattn_probs(qn, kn):
    s = _dot(qn, kn, NT) * (XATTN_HEAD_DIM ** -0.5)
    p = jnp.exp(s - jnp.max(s, axis=-1, keepdims=True))
    return p / jnp.sum(p, axis=-1, keepdims=True)


def _xattn_fwd(cq, ckv, x1, gq, gk, w_co, g_mlp, n_batch, s_len, m_len, tq=256):
    d = x1.shape[1]
    tq = min(tq, s_len)
    nq = s_len // tq
    hd = XATTN_HEAD_DIM

    def body(cq_ref, kv_ref, x1_ref, gq_ref, gk_ref, wo_ref, gm_ref, co_ref, x2_ref, hf_ref):
        outs = []
        for h in range(XATTN_HEADS):
            qn = _rms(cq_ref[:, h * hd:(h + 1) * hd], gq_ref[...])
            kn = _rms(kv_ref[:, h * hd:(h + 1) * hd], gk_ref[...])
            p = _xattn_probs(qn, kn)
            outs.append(_dot(p, kv_ref[:, XATTN_WIDTH + h * hd:XATTN_WIDTH + (h + 1) * hd]).astype(BF16))
        x2 = x1_ref[...]
        for h in range(XATTN_HEADS):
            co_ref[:, h * hd:(h + 1) * hd] = outs[h]
            x2 = x2 + _dot(outs[h], wo_ref[h * hd:(h + 1) * hd, :])
        x2_ref[...] = x2
        hf_ref[...] = _rms(x2, gm_ref[...]).astype(BF16)

    row = lambda b, i: (b * nq + i, 0)
    fixed = lambda b, i: (0, 0)
    t_len = n_batch * s_len
    return pl.pallas_call(
        body, name="xattn_fwd", grid=(n_batch, nq),
        in_specs=[pl.BlockSpec((tq, XATTN_WIDTH), row), pl.BlockSpec((m_len, 2 * XATTN_WIDTH), lambda b, i: (b, 0)),
                  pl.BlockSpec((tq, d), row), pl.BlockSpec((1, hd), fixed), pl.BlockSpec((1, hd), fixed),
                  pl.BlockSpec((XATTN_WIDTH, d), fixed), pl.BlockSpec((1, d), fixed)],
        out_specs=[pl.BlockSpec((tq, XATTN_WIDTH), row), pl.BlockSpec((tq, d), row), pl.BlockSpec((tq, d), row)],
        out_shape=[jax.ShapeDtypeStruct((t_len, XATTN_WIDTH), BF16), jax.ShapeDtypeStruct((t_len, d), F32),
                   jax.ShapeDtypeStruct((t_len, d), BF16)],
        compiler_params=_cparams(("parallel", "parallel")),
    )(cq, ckv, x1, gq, gk, w_co, g_mlp)


def _xattn_bwd(dx2, cq, ckv, x1, gq, gk, w_co, g_x, w_cq, n_batch, s_len, m_len, tq=256):
    d = x1.shape[1]
    tq = min(tq, s_len)
    nq = s_len // tq
    hd = XATTN_HEAD_DIM
    scale = XATTN_HEAD_DIM ** -0.5

    def body(dx2_ref, cq_ref, kv_ref, x1_ref, gq_ref, gk_ref, wo_ref, gx_ref, wq_ref,
             dx1_ref, dcq_ref, dkv_ref, dgq_ref, dgk_ref, dgx_ref, dk_acc, dv_acc):
        b = pl.program_id(0)
        i = pl.program_id(1)

        @pl.when((b == 0) & (i == 0))
        def _():
            dgq_ref[...] = jnp.zeros_like(dgq_ref)
            dgk_ref[...] = jnp.zeros_like(dgk_ref)
            dgx_ref[...] = jnp.zeros_like(dgx_ref)

        @pl.when(i == 0)
        def _():
            dk_acc[...] = jnp.zeros_like(dk_acc)
            dv_acc[...] = jnp.zeros_like(dv_acc)

        dx2 = dx2_ref[...]
        dhq = jnp.zeros((tq, d), F32)
        for h in range(XATTN_HEADS):
            sl = slice(h * hd, (h + 1) * hd)
            q = cq_ref[:, sl]
            qn = _rms(q, gq_ref[...])
            kn = _rms(kv_ref[:, sl], gk_ref[...])
            v = kv_ref[:, XATTN_WIDTH + h * hd:XATTN_WIDTH + (h + 1) * hd]
            p = _xattn_probs(qn, kn)
            dco = _dot(dx2, wo_ref[sl, :], NT)
            dv_acc[:, sl] += _dot(p, dco, TN)
            dp = _dot(dco, v, NT)
            ds = p * (dp - jnp.sum(dp * p, axis=-1, keepdims=True))
            dqn = _dot(ds, kn) * scale
            dk_acc[:, sl] += _dot(ds, qn, TN) * scale
            dq, dgq = _rms_bwd(q, gq_ref[...], dqn)
            dgq_ref[...] += dgq
            dqb = dq.astype(BF16)
            dcq_ref[:, sl] = dqb
            dhq = dhq + _dot(dqb, wq_ref[:, sl], NT)
        dxn, dgx = _rms_bwd(x1_ref[...], gx_ref[...], dhq)
        dgx_ref[...] += dgx
        dx1_ref[...] = dx2 + dxn

        @pl.when(i == nq - 1)
        def _():
            for h in range(XATTN_HEADS):
                sl = slice(h * hd, (h + 1) * hd)
                dk, dgk = _rms_bwd(kv_ref[:, sl], gk_ref[...], dk_acc[:, sl])
                dgk_ref[...] += dgk
                dkv_ref[:, sl] = dk.astype(BF16)
                dkv_ref[:, XATTN_WIDTH + h * hd:XATTN_WIDTH + (h + 1) * hd] = dv_acc[:, sl].astype(BF16)

    row = lambda b, i: (b * nq + i, 0)
    fixed = lambda b, i: (0, 0)
    t_len = n_batch * s_len
    return pl.pallas_call(
        body, name="xattn_bwd", grid=(n_batch, nq),
        in_specs=[pl.BlockSpec((tq, d), row), pl.BlockSpec((tq, XATTN_WIDTH), row), pl.BlockSpec((m_len, 2 * XATTN_WIDTH), lambda b, i: (b, 0)),
                  pl.BlockSpec((tq, d), row), pl.BlockSpec((1, hd), fixed), pl.BlockSpec((1, hd), fixed),
                  pl.BlockSpec((XATTN_WIDTH, d), fixed), pl.BlockSpec((1, d), fixed), pl.BlockSpec((d, XATTN_WIDTH), fixed)],
        out_specs=[pl.BlockSpec((tq, d), row), pl.BlockSpec((tq, XATTN_WIDTH), row), pl.BlockSpec((m_len, 2 * XATTN_WIDTH), lambda b, i: (b, 0)),
                   pl.BlockSpec((1, hd), fixed), pl.BlockSpec((1, hd), fixed), pl.BlockSpec((1, d), fixed)],
        out_shape=[jax.ShapeDtypeStruct((t_len, d), F32), jax.ShapeDtypeStruct((t_len, XATTN_WIDTH), BF16),
                   jax.ShapeDtypeStruct((n_batch * m_len, 2 * XATTN_WIDTH), BF16),
                   jax.ShapeDtypeStruct((1, hd), F32), jax.ShapeDtypeStruct((1, hd), F32), jax.ShapeDtypeStruct((1, d), F32)],
        scratch_shapes=[pltpu.VMEM((m_len, XATTN_WIDTH), F32), pltpu.VMEM((m_len, XATTN_WIDTH), F32)],
        compiler_params=_cparams(("arbitrary", "arbitrary")),
    )(dx2, cq, ckv, x1, gq, gk, w_co, g_x, w_cq)


def _mlp_fwd(hf, x2, target, w1, w2, tm=512, tf=1024):
    t_len, d = x2.shape
    f = w1.shape[1]
    tm, tf = min(tm, t_len), min(tf, f)
    nf = f // tf

    def body(hf_ref, x2_ref, tg_ref, w1_ref, w2_ref, u_ref, a_ref, dy_ref, ls_ref, acc_ref):
        k = pl.program_id(1)

        @pl.when(k == 0)
        def _():
            acc_ref[...] = x2_ref[...]

        u = _dot(hf_ref[...], w1_ref[...])
        u_ref[...] = u
        r = jnp.maximum(u, 0.0)
        a = (r * r).astype(BF16)
        a_ref[...] = a
        acc_ref[...] += _dot(a, w2_ref[...])

        @pl.when(k == nf - 1)
        def _():
            err = acc_ref[...] - tg_ref[...]
            dy_ref[...] = err * (1.0 / d)
            ls_ref[...] = jnp.broadcast_to(jnp.sum(jnp.sum(err * err, axis=-1, keepdims=True) * (1.0 / d), axis=0, keepdims=True), ls_ref.shape)

    row = lambda i, k: (i, 0)
    return pl.pallas_call(
        body, name="mlp_fwd", grid=(t_len // tm, nf),
        in_specs=[pl.BlockSpec((tm, d), row), pl.BlockSpec((tm, d), row), pl.BlockSpec((tm, d), row),
                  pl.BlockSpec((d, tf), lambda i, k: (0, k)), pl.BlockSpec((tf, d), lambda i, k: (k, 0))],
        out_specs=[pl.BlockSpec((tm, tf), lambda i, k: (i, k)), pl.BlockSpec((tm, tf), lambda i, k: (i, k)), pl.BlockSpec((tm, d), row),
                   pl.BlockSpec((1, 8, LANES), lambda i, k: (i, 0, 0))],
        out_shape=[jax.ShapeDtypeStruct((t_len, f), F32), jax.ShapeDtypeStruct((t_len, f), BF16), jax.ShapeDtypeStruct((t_len, d), F32),
                   jax.ShapeDtypeStruct((t_len // tm, 8, LANES), F32)],
        scratch_shapes=[pltpu.VMEM((tm, d), F32)],
        compiler_params=_cparams(("parallel", "arbitrary")),
    )(hf, x2, target, w1, w2)


def _mlp_bwd(dy, u, x2, g, w1, w2, tm=512, tf=1024):
    t_len, d = x2.shape
    f = w1.shape[1]
    tm, tf = min(tm, t_len), min(tf, f)
    nf = f // tf

    def body(dy_ref, u_ref, x2_ref, g_ref, w1_ref, w2_ref, du_ref, dx2_ref, dg_ref, acc_ref):
        i = pl.program_id(0)
        k = pl.program_id(1)

        @pl.when((i == 0) & (k == 0))
        def _():
            dg_ref[...] = jnp.zeros_like(dg_ref)

        @pl.when(k == 0)
        def _():
            acc_ref[...] = jnp.zeros_like(acc_ref)

        da = _dot(dy_ref[...], w2_ref[...], NT)
        du = (da * (2.0 * jnp.maximum(u_ref[...], 0.0))).astype(BF16)
        du_ref[...] = du
        acc_ref[...] += _dot(du, w1_ref[...], NT)

        @pl.when(k == nf - 1)
        def _():
            dxn, dg = _rms_bwd(x2_ref[...], g_ref[...], acc_ref[...])
            dx2_ref[...] = dy_ref[...] + dxn
            dg_ref[...] += dg

    row = lambda i, k: (i, 0)
    fixed = lambda i, k: (0, 0)
    return pl.pallas_call(
        body, name="mlp_bwd", grid=(t_len // tm, nf),
        in_specs=[pl.BlockSpec((tm, d), row), pl.BlockSpec((tm, tf), lambda i, k: (i, k)), pl.BlockSpec((tm, d), row), pl.BlockSpec((1, d), fixed),
                  pl.BlockSpec((d, tf), lambda i, k: (0, k)), pl.BlockSpec((tf, d), lambda i, k: (k, 0))],
        out_specs=[pl.BlockSpec((tm, tf), lambda i, k: (i, k)), pl.BlockSpec((tm, d), row), pl.BlockSpec((1, d), fixed)],
        out_shape=[jax.ShapeDtypeStruct((t_len, f), BF16), jax.ShapeDtypeStruct((t_len, d), F32), jax.ShapeDtypeStruct((1, d), F32)],
        scratch_shapes=[pltpu.VMEM((tm, d), F32)],
        compiler_params=_cparams(("arbitrary", "arbitrary")),
    )(dy, u, x2, g, w1, w2)


def _pad_lanes(v, offset=0, width=LANES):
    return jnp.zeros((1, width), F32).at[:, offset:offset + v.shape[1]].set(v)


def _col(v, offset=0, rows=SM_ROWS):
    return jnp.zeros((rows, 1), F32).at[offset:offset + v.shape[1], 0].set(v[0])


def _local_step(x, mem, target, norm_mix_g, w_in, fox_qnorm_g, fox_knorm_g, fox_f_bias, fox_onorm_g, gdn_conv_w, gdn_A_log,
                gdn_dt_bias, gdn_onorm_g, w_out, norm_xattn_g, mem_norm_g, w_cq, w_ckv, xattn_qnorm_g, xattn_knorm_g, w_co,
                norm_mlp_g, w_mlp1, w_mlp2):
    n_batch, s_len, d = x.shape
    m_len = mem.shape[1]
    t_len = n_batch * s_len
    tq = min(256, s_len)
    nq = s_len // tq
    n_chunks = s_len // GDN_CHUNK
    x2d = x.reshape(t_len, d)

    wp = jnp.concatenate([w_in[:, 0:1536], w_in[:, 1544:3080], w_in[:, 3088:3600], w_in[:, 1536:1544], w_in[:, 3080:3088],
                          jnp.zeros((d, P_DIM - 3600), BF16)], axis=1)
    wst = jnp.concatenate([w_in[:, 1536:1544], w_in[:, 3080:3088]], axis=1).T
    conv_w = jnp.concatenate([gdn_conv_w, jnp.zeros((8 - CONV_WIDTH, gdn_conv_w.shape[1]), F32)], axis=0)
    bias_col = _col(fox_f_bias, SM_F)
    gq2, gk2, go2 = (jnp.tile(g, (1, 2)) for g in (fox_qnorm_g, fox_knorm_g, fox_onorm_g))
    a_c, dt_c = _pad_lanes(gdn_A_log, SM_A), _pad_lanes(gdn_dt_bias, SM_A)
    a_r, dt_r = _col(gdn_A_log, SM_A), _col(gdn_dt_bias, SM_A)

    h1, pfox, pgdn, pz, sm, smt = _in_proj(x2d, norm_mix_g, wp, wst)
    c_rows = _fox_cum(smt, bias_col, n_batch, s_len)
    cb = c_rows.reshape(SM_ROWS, n_batch, nq, tq).transpose(1, 2, 0, 3)
    pf3 = pfox.reshape(n_batch, s_len, 1536)
    o_fox, oa, lse = _fox_fwd(pf3, cb, gq2, gk2, go2, tq)
    pg3 = pgdn.reshape(n_batch, s_len, 1536)
    qkvn = _gdn_pre(pg3, conv_w)
    z3 = pz.reshape(n_batch, s_len, GDN_WIDTH)
    smc = sm.reshape(n_batch, s_len, LANES)
    smr = smt.reshape(SM_ROWS, n_batch * n_chunks, GDN_CHUNK).transpose(1, 0, 2)
    ob, states = _gdn_fwd(qkvn, z3, smc, smr, a_c, dt_c, a_r, dt_r, gdn_onorm_g)
    oa2, ob2 = oa.reshape(t_len, FOX_WIDTH), ob.reshape(t_len, GDN_WIDTH)
    x1, hq, cq = _out_proj(x2d, oa2, ob2, w_out, norm_xattn_g, w_cq)
    mem2d = mem.reshape(n_batch * m_len, d)
    hm, ckv = _mem_kv(mem2d, mem_norm_g, w_ckv)
    co, x2, hf = _xattn_fwd(cq, ckv, x1, xattn_qnorm_g, xattn_knorm_g, w_co, norm_mlp_g, n_batch, s_len, m_len)
    u, a_act, dy, loss_tiles = _mlp_fwd(hf, x2, target.reshape(t_len, d), w_mlp1, w_mlp2)
    loss = 0.5 * jnp.sum(loss_tiles[:, 0, 0])

    grads = {}
    du, dx2, grads["norm_mlp_g"] = _mlp_bwd(dy, u, x2, norm_mlp_g, w_mlp1, w_mlp2)
    grads["w_mlp2"] = _wgrad(a_act, dy, "wgrad_mlp2")
    grads["w_mlp1"] = _wgrad(hf, du, "wgrad_mlp1")
    grads["w_co"] = _wgrad(co, dx2, "wgrad_co")
    dx1, dcq, dckv, grads["xattn_qnorm_g"], grads["xattn_knorm_g"], grads["norm_xattn_g"] = _xattn_bwd(
        dx2, cq, ckv, x1, xattn_qnorm_g, xattn_knorm_g, w_co, norm_xattn_g, w_cq, n_batch, s_len, m_len)
    grads["w_cq"] = _wgrad(hq, dcq, "wgrad_cq")
    grads["w_ckv"] = _wgrad(hm, dckv, "wgrad_ckv")
    grads["mem_norm_g"] = _mem_kv_bwd(dckv, mem2d, mem_norm_g, w_ckv)
    grads["w_out"] = _wgrad(jnp.concatenate([oa2, ob2], axis=1), dx1, "wgrad_out")
    dcat = _out_proj_bwd(dx1, w_out)
    dcat3 = dcat.reshape(n_batch, s_len, d)

    dqkvn, dz, dsmc, dsmr, dac, ddc, dar, ddr, grads["gdn_onorm_g"] = _gdn_bwd(
        qkvn, z3, smc, smr, a_c, dt_c, a_r, dt_r, gdn_onorm_g, states, dcat3)
    grads["gdn_A_log"] = dac[:, SM_A:SM_A + GDN_HEADS] + dar[SM_A:SM_A + GDN_HEADS, 0][None, :]
    grads["gdn_dt_bias"] = ddc[:, SM_A:SM_A + GDN_HEADS] + ddr[SM_A:SM_A + GDN_HEADS, 0][None, :]
    dpg, dconv = _gdn_pre_bwd(pg3, conv_w, dqkvn)
    grads["gdn_conv_w"] = dconv[0:CONV_WIDTH]

    dq, dk, dv, dcb, dgq, dgk, dgo = _fox_bwd(pf3, cb, gq2, gk2, go2, o_fox, lse, dcat3[:, :, 0:FOX_WIDTH], tq)
    fold = lambda g: g[:, 0:FOX_HEAD_DIM] + g[:, FOX_HEAD_DIM:LANES]
    grads["fox_qnorm_g"], grads["fox_knorm_g"], grads["fox_onorm_g"] = fold(dgq), fold(dgk), fold(dgo)
    dc8 = dcb[:, :, :, 0:2, :].transpose(1, 3, 0, 2, 4).reshape(FOX_HEADS, t_len)
    dc_rows = jnp.concatenate([dc8, jnp.zeros((SM_ROWS - FOX_HEADS, t_len), F32)], axis=0)
    dl_rows, dbias = _fox_cum_bwd(dc_rows, smt, bias_col, n_batch, s_len)
    grads["fox_f_bias"] = dbias[SM_F:SM_F + FOX_HEADS, 0][None, :]
    dsm_rows = jnp.concatenate([dl_rows[0:SM_B], dsmr.transpose(1, 0, 2).reshape(SM_ROWS, t_len)[SM_B:SM_ROWS]], axis=0)

    dproj = jnp.concatenate([dq.reshape(t_len, FOX_WIDTH), dk.reshape(t_len, FOX_WIDTH), dv.reshape(t_len, FOX_WIDTH),
                             dpg.reshape(t_len, 1536), dz.reshape(t_len, GDN_WIDTH), dsmc.reshape(t_len, LANES).astype(BF16)], axis=1)
    grad_x, grads["norm_mix_g"] = _in_proj_bwd(dproj, dsm_rows, x2d, norm_mix_g, wp, wst, dx1)
    dwp = _wgrad(h1, dproj, "wgrad_in", bk=256, bn=P_DIM)
    dwst = _rows_matmul(dsm_rows, h1, "wgrad_in_rows")
    dw_small = dwp[:, P_SMALL:P_SMALL + SM_ROWS] + dwst.T
    grads["w_in"] = jnp.concatenate([dwp[:, 0:1536], dw_small[:, 0:8], dwp[:, 1536:3072], dw_small[:, 8:16], dwp[:, 3072:3584]], axis=1)
    return loss, grad_x.reshape(n_batch, s_len, d), grads


MESH_ID = pl.DeviceIdType.MESH
ANY_SPEC = pl.BlockSpec(memory_space=pl.ANY)


def _place():
    x, y, c = lax.axis_index("x"), lax.axis_index("y"), lax.axis_index("c")
    return x, y, c, [(1 - x, y), (x, 1 - y), (1 - x, 1 - y)]


def _all_gather_body(n, ins, outs, send_sems, recv_sems, local_sems):
    x, y, c, chips = _place()
    me, sibling = (x, y, c), (x, y, 1 - c)

    def copy(a, k, block, to, src=None):
        dst = outs[a].at[4 * block[0] + 2 * block[1] + block[2]]
        return pltpu.make_async_remote_copy(src_ref=dst if src is None else src, dst_ref=dst, send_sem=send_sems.at[a, k],
                                            recv_sem=recv_sems.at[a, k], device_id=to, device_id_type=MESH_ID)

    mine = [pltpu.make_async_copy(ins[a], outs[a].at[4 * x + 2 * y + c], local_sems.at[a]) for a in range(n)]
    for cp in mine:
        cp.start()
    first = []
    for a in range(n):
        first.append(copy(a, 0, me, sibling, src=ins[a]))
        first += [copy(a, 1 + j, me, (*chip, c), src=ins[a]) for j, chip in enumerate(chips)]
    for cp in first:
        cp.start()
    passed = []
    for j, chip in enumerate(chips):
        for a in range(n):
            copy(a, 1 + j, (*chip, c), me).wait_recv()
            fwd = copy(a, 4 + j, (*chip, c), sibling)
            fwd.start()
            passed.append(fwd)
    for a in range(n):
        copy(a, 0, sibling, me).wait_recv()
        for j, chip in enumerate(chips):
            copy(a, 4 + j, (*chip, 1 - c), me).wait_recv()
    for cp in first + passed:
        cp.wait_send()
    for cp in mine:
        cp.wait()


def _all_gather_hbm(arrs, name):
    n = len(arrs)

    def body(*refs):
        _all_gather_body(n, refs[:n], refs[n:2 * n], *refs[2 * n:])

    return pl.pallas_call(
        body, name=name, in_specs=[ANY_SPEC] * n, out_specs=[ANY_SPEC] * n,
        out_shape=[jax.ShapeDtypeStruct((N_DEV,) + a.shape, a.dtype) for a in arrs],
        scratch_shapes=[pltpu.SemaphoreType.DMA((n, 7)), pltpu.SemaphoreType.DMA((n, 7)), pltpu.SemaphoreType.DMA((n,))],
    )(*arrs)


def _pair_exchange(arrs, name):
    n = len(arrs)

    def body(*refs):
        ins, outs = refs[:n], refs[n:2 * n]
        send_sems, recv_sems = refs[2 * n:]
        x, y, c, _ = _place()
        copies = []
        for a in range(n):
            for chip in range(4):
                copies.append(pltpu.make_async_remote_copy(
                    src_ref=ins[a].at[2 * chip + (1 - c)], dst_ref=outs[a].at[chip], send_sem=send_sems.at[a, chip],
                    recv_sem=recv_sems.at[a, chip], device_id=(x, y, 1 - c), device_id_type=MESH_ID))
        for cp in copies:
            cp.start()
        for cp in copies:
            cp.wait()

    return pl.pallas_call(
        body, name=name, in_specs=[ANY_SPEC] * n, out_specs=[ANY_SPEC] * n,
        out_shape=[jax.ShapeDtypeStruct((4,) + a.shape[1:], a.dtype) for a in arrs],
        scratch_shapes=[pltpu.SemaphoreType.DMA((n, 4)), pltpu.SemaphoreType.DMA((n, 4))],
    )(*arrs)


def _chip_exchange(arrs, name):
    n = len(arrs)

    def body(*refs):
        ins, outs = refs[:n], refs[n:2 * n]
        send_sems, recv_sems = refs[2 * n:]
        x, y, c, chips = _place()
        copies = []
        for a in range(n):
            for j, chip in enumerate(chips):
                copies.append(pltpu.make_async_remote_copy(
                    src_ref=ins[a].at[2 * chip[0] + chip[1]], dst_ref=outs[a].at[j], send_sem=send_sems.at[a, j],
                    recv_sem=recv_sems.at[a, j], device_id=(*chip, c), device_id_type=MESH_ID))
        for cp in copies:
            cp.start()
        for cp in copies:
            cp.wait()

    return pl.pallas_call(
        body, name=name, in_specs=[ANY_SPEC] * n, out_specs=[ANY_SPEC] * n,
        out_shape=[jax.ShapeDtypeStruct((3,) + a.shape[1:], a.dtype) for a in arrs],
        scratch_shapes=[pltpu.SemaphoreType.DMA((n, 3)), pltpu.SemaphoreType.DMA((n, 3))],
    )(*arrs)


def _all_gather_vmem(block, name):
    def body(in_ref, out_ref, send_sems, recv_sems, local_sems):
        _all_gather_body(1, [in_ref], [out_ref], send_sems, recv_sems, local_sems)

    vmem = pl.BlockSpec(memory_space=pltpu.VMEM)
    return pl.pallas_call(
        body, name=name, in_specs=[vmem], out_specs=vmem,
        out_shape=jax.ShapeDtypeStruct((N_DEV,) + block.shape, block.dtype),
        scratch_shapes=[pltpu.SemaphoreType.DMA((1, 7)), pltpu.SemaphoreType.DMA((1, 7)), pltpu.SemaphoreType.DMA((1,))],
    )(block)


def _row_tile(rows, cols):
    if rows <= 256:
        return rows
    return 256 if cols <= 512 else 128


def _pair_sum(core, own, got, name):
    _, rows, cols = own.shape
    tr = _row_tile(rows, cols)

    def body(c_ref, own_ref, got_ref, o_ref):
        o_ref[0] = own_ref[0] + got_ref[0]

    return pl.pallas_call(
        body, name=name,
        grid_spec=pltpu.PrefetchScalarGridSpec(
            num_scalar_prefetch=1, grid=(4, rows // tr),
            in_specs=[pl.BlockSpec((1, tr, cols), lambda k, i, c: (2 * k + c[0], i, 0)),
                      pl.BlockSpec((1, tr, cols), lambda k, i, c: (k, i, 0))],
            out_specs=pl.BlockSpec((1, tr, cols), lambda k, i, c: (k, i, 0))),
        out_shape=jax.ShapeDtypeStruct((4, rows, cols), F32),
        compiler_params=_cparams(("parallel", "parallel")),
    )(core, own, got)


def _adamw(w, g, m, v):
    m_new = ADAM_B1 * m + (1.0 - ADAM_B1) * g
    v_new = ADAM_B2 * v + (1.0 - ADAM_B2) * (g * g)
    m_hat = m_new / (1.0 - ADAM_B1 ** ADAM_STEP)
    v_hat = v_new / (1.0 - ADAM_B2 ** ADAM_STEP)
    delta = -ADAM_LR * (m_hat / (jnp.sqrt(v_hat) + ADAM_EPS) + ADAM_WD * w)
    return delta, m_new, v_new


def _sum_adam(chip, sums, parts, w, m, v, name):
    n_parts, rows, cols = parts.shape
    tr = _row_tile(rows, cols)

    def body(chip_ref, *refs):
        if sums is not None:
            g = refs[0][0].astype(F32)
            refs = refs[1:]
        p_ref, w_ref, m_ref, v_ref, g_ref, d_ref, mo_ref, vo_ref = refs
        for k in range(n_parts):
            g = p_ref[k].astype(F32) if (k == 0 and sums is None) else g + p_ref[k].astype(F32)
        g_ref[...] = g
        d_ref[...], mo_ref[...], vo_ref[...] = _adamw(w_ref[...], g, m_ref[...], v_ref[...])

    tile = pl.BlockSpec((tr, cols), lambda i, ch: (i, 0))
    out = jax.ShapeDtypeStruct((rows, cols), F32)
    own = [] if sums is None else [pl.BlockSpec((1, tr, cols), lambda i, ch: (ch[0], i, 0))]
    return pl.pallas_call(
        body, name=name,
        grid_spec=pltpu.PrefetchScalarGridSpec(
            num_scalar_prefetch=1, grid=(rows // tr,),
            in_specs=own + [pl.BlockSpec((n_parts, tr, cols), lambda i, ch: (0, i, 0)), tile, tile, tile],
            out_specs=[tile, tile, tile, tile]),
        out_shape=[out, out, out, out],
        compiler_params=_cparams(("parallel",)),
    )(chip, *([] if sums is None else [sums]), parts, w, m, v)


SHARDED = ("w_in", "gdn_conv_w", "w_out", "w_cq", "w_ckv", "w_co", "w_mlp1", "w_mlp2")
COLUMN_SHARDED = ("w_in", "gdn_conv_w", "w_co", "w_mlp1")
REPLICATED = ("norm_mix_g", "fox_qnorm_g", "fox_knorm_g", "fox_f_bias", "fox_onorm_g", "gdn_A_log", "gdn_dt_bias", "gdn_onorm_g",
              "norm_xattn_g", "mem_norm_g", "xattn_qnorm_g", "xattn_knorm_g", "norm_mlp_g")
WEIGHTS = ("norm_mix_g", "w_in", "fox_qnorm_g", "fox_knorm_g", "fox_f_bias", "fox_onorm_g", "gdn_conv_w", "gdn_A_log", "gdn_dt_bias",
           "gdn_onorm_g", "w_out", "norm_xattn_g", "mem_norm_g", "w_cq", "w_ckv", "xattn_qnorm_g", "xattn_knorm_g", "w_co",
           "norm_mlp_g", "w_mlp1", "w_mlp2")
PACK_ROWS = 16
LOSS_ROW = len(REPLICATED)


def _whole(name, gathered):
    if name in COLUMN_SHARDED:
        return gathered.transpose(1, 0, 2).reshape(gathered.shape[1], N_DEV * gathered.shape[2])
    return gathered.reshape(N_DEV * gathered.shape[1], gathered.shape[2])


def _blocks(name, whole):
    if name in COLUMN_SHARDED:
        rows, cols = whole.shape
        return whole.reshape(rows, N_DEV, cols // N_DEV).transpose(1, 0, 2)
    return whole.reshape(N_DEV, whole.shape[0] // N_DEV, whole.shape[1])


def _pack(vals, fill=0.0):
    rows = [jnp.pad(vals[k], ((0, 0), (0, D_MODEL - vals[k].shape[1])), constant_values=fill) for k in REPLICATED]
    rows.append(jnp.full((PACK_ROWS - len(rows), D_MODEL), fill, F32))
    return jnp.concatenate(rows, axis=0)


def kernel(x, mem, norm_mix_g, w_in, fox_qnorm_g, fox_knorm_g, fox_f_bias, fox_onorm_g, gdn_conv_w, gdn_A_log, gdn_dt_bias, gdn_onorm_g, w_out, norm_xattn_g, mem_norm_g, w_cq, w_ckv, xattn_qnorm_g, xattn_knorm_g, w_co, norm_mlp_g, w_mlp1, w_mlp2, loss_target, m_norm_mix_g, m_w_in, m_fox_qnorm_g, m_fox_knorm_g, m_fox_f_bias, m_fox_onorm_g, m_gdn_conv_w, m_gdn_A_log, m_gdn_dt_bias, m_gdn_onorm_g, m_w_out, m_norm_xattn_g, m_mem_norm_g, m_w_cq, m_w_ckv, m_xattn_qnorm_g, m_xattn_knorm_g, m_w_co, m_norm_mlp_g, m_w_mlp1, m_w_mlp2, v_norm_mix_g, v_w_in, v_fox_qnorm_g, v_fox_knorm_g, v_fox_f_bias, v_fox_onorm_g, v_gdn_conv_w, v_gdn_A_log, v_gdn_dt_bias, v_gdn_onorm_g, v_w_out, v_norm_xattn_g, v_mem_norm_g, v_w_cq, v_w_ckv, v_xattn_qnorm_g, v_xattn_knorm_g, v_w_co, v_norm_mlp_g, v_w_mlp1, v_w_mlp2):
    given = dict(locals())
    w = {k: given[k] for k in WEIGHTS}
    m = {k: given["m_" + k] for k in WEIGHTS}
    v = {k: given["v_" + k] for k in WEIGHTS}

    shards = [w[k][0] if k == "gdn_conv_w" else w[k][0].astype(BF16) for k in SHARDED]
    whole = {k: _whole(k, g) for k, g in zip(SHARDED, _all_gather_hbm(shards, "gather_weights"))}

    small = {k: w[k] for k in REPLICATED}
    loss_local, grad_x, grads = _local_step(x, mem, loss_target, **small, **whole)

    core = lax.axis_index("c").astype(jnp.int32).reshape(1)
    chip = (2 * lax.axis_index("x") + lax.axis_index("y")).astype(jnp.int32).reshape(1)
    own = [_blocks(k, grads[k]) for k in SHARDED]
    got = _pair_exchange(own, "grad_pair_exchange")
    sums = [_pair_sum(core, o, g, "grad_pair_sum_" + k) for k, o, g in zip(SHARDED, own, got)]
    parts = _chip_exchange(sums, "grad_chip_exchange")
    out_g, out_d, out_m, out_v = {}, {}, {}, {}
    for k, s, p in zip(SHARDED, sums, parts):
        res = _sum_adam(chip, s, p, w[k][0], m[k][0], v[k][0], "adam_" + k)
        out_g[k], out_d[k], out_m[k], out_v[k] = (r[None] for r in res)

    packed = _pack({k: grads[k] for k in REPLICATED}).at[LOSS_ROW, 0].set(loss_local)
    everyone = _all_gather_vmem(packed, "gather_small")
    res = _sum_adam(chip, None, everyone, _pack(small), _pack({k: m[k] for k in REPLICATED}),
                    _pack({k: v[k] for k in REPLICATED}, fill=1.0), "adam_small")
    for i, k in enumerate(REPLICATED):
        n = w[k].shape[1]
        out_g[k], out_d[k], out_m[k], out_v[k] = (r[i:i + 1, 0:n] for r in res)
    loss = res[0][LOSS_ROW, 0]

    return (loss, grad_x, *[out_g[k] for k in WEIGHTS], *[out_d[k] for k in WEIGHTS], *[out_m[k] for k in WEIGHTS],
            *[out_v[k] for k in WEIGHTS])
```

```python
import functools

import jax
import jax.numpy as jnp
import numpy as np
from jax import lax
from jax.experimental import pallas as pl
from jax.experimental.pallas import tpu as pltpu

F32 = jnp.float32
BF16 = jnp.bfloat16

D_MODEL = 1024
FOX_HEADS = 8
FOX_HEAD_DIM = 64
FOX_WIDTH = 512
GDN_HEADS = 4
GDN_HEAD_DIM = 128
GDN_WIDTH = 512
CONV_WIDTH = 4
GDN_CHUNK = 64
GDN_GROUP = 4
XATTN_HEADS = 4
XATTN_HEAD_DIM = 128
XATTN_WIDTH = 512
D_FF = 4096
EPS = 1e-6
NEG_INF = -1e30
N_DEV = 8

ADAM_LR = 0.001
ADAM_B1 = 0.9
ADAM_B2 = 0.999
ADAM_EPS = 1e-08
ADAM_WD = 0.01
ADAM_STEP = 10

P_FOX = 0
P_GDN = 1536
P_Z = 3072
P_SMALL = 3584
P_DIM = 3712
SM_F = 0
SM_B = 8
SM_A = 12
SM_ROWS = 16

LANES = 128
VMEM_LIMIT = 56 * 1024 * 1024

NN = (((1,), (0,)), ((), ()))
NT = (((1,), (1,)), ((), ()))
TN = (((0,), (0,)), ((), ()))


def _dot(a, b, dims=NN):
    return lax.dot_general(a.astype(BF16), b.astype(BF16), dims, preferred_element_type=F32)


def _cparams(sem=None):
    kw = dict(vmem_limit_bytes=VMEM_LIMIT)
    if sem is not None:
        kw["dimension_semantics"] = sem
    return pltpu.CompilerParams(**kw)


def _sigmoid(x):
    return 0.5 * (jnp.tanh(0.5 * x) + 1.0)


def _softplus(x):
    return jnp.maximum(x, 0.0) + jnp.log1p(jnp.exp(-jnp.abs(x)))


def _log_sigmoid(x):
    return -_softplus(-x)


def _rms(x, g):
    r = lax.rsqrt(jnp.mean(x * x, axis=-1, keepdims=True) + EPS)
    return x * r * g


def _rms_bwd(x, g, dy):
    r = lax.rsqrt(jnp.mean(x * x, axis=-1, keepdims=True) + EPS)
    xh = x * r
    dg = jnp.sum(dy * xh, axis=0, keepdims=True)
    dyg = dy * g
    dx = r * (dyg - xh * jnp.mean(dyg * xh, axis=-1, keepdims=True))
    return dx, dg


def _pair_stat(t, m0):
    s0 = jnp.sum(jnp.where(m0, t, 0.0), axis=-1, keepdims=True)
    s1 = jnp.sum(jnp.where(m0, 0.0, t), axis=-1, keepdims=True)
    return jnp.where(m0, s0, s1)


def _rms_pair(x, g, m0):
    r = lax.rsqrt(_pair_stat(x * x, m0) * (1.0 / FOX_HEAD_DIM) + EPS)
    return x * r * g


def _rms_pair_bwd(x, g, dy, m0):
    r = lax.rsqrt(_pair_stat(x * x, m0) * (1.0 / FOX_HEAD_DIM) + EPS)
    xh = x * r
    dg = jnp.sum(dy * xh, axis=0, keepdims=True)
    dyg = dy * g
    dx = r * (dyg - xh * (_pair_stat(dyg * xh, m0) * (1.0 / FOX_HEAD_DIM)))
    return dx, dg


@jax.custom_vjp
def _mm_nn(a, b):
    return _dot(a, b, NN)


_mm_nn.defvjp(lambda a, b: (_dot(a, b, NN), (a, b)),
              lambda r, g: (_dot(g, r[1], NT), _dot(r[0], g, TN)))


@jax.custom_vjp
def _mm_nt(a, b):
    return _dot(a, b, NT)


_mm_nt.defvjp(lambda a, b: (_dot(a, b, NT), (a, b)),
              lambda r, g: (_dot(g, r[1], NN), _dot(g, r[0], TN)))


@jax.custom_vjp
def _mm_tn(a, b):
    return _dot(a, b, TN)


_mm_tn.defvjp(lambda a, b: (_dot(a, b, TN), (a, b)),
              lambda r, g: (_dot(r[1], g, NT), _dot(r[0], g, NN)))


def _dot3(a, b, dims):
    ah = a.astype(BF16)
    al = (a - ah.astype(F32)).astype(BF16)
    bh = b.astype(BF16)
    bl = (b - bh.astype(F32)).astype(BF16)
    d = functools.partial(lax.dot_general, dimension_numbers=dims, preferred_element_type=F32)
    return d(ah, bh) + d(ah, bl) + d(al, bh)


@jax.custom_vjp
def _mm3(a, b):
    return _dot3(a, b, NN)


_mm3.defvjp(lambda a, b: (_dot3(a, b, NN), (a, b)),
            lambda r, g: (_dot3(g, r[1], NT), _dot3(r[0], g, TN)))


def _unit_lower_inverses(mats):
    c = mats[0].shape[0]
    eye = (lax.broadcasted_iota(jnp.int32, (c, c), 0) == lax.broadcasted_iota(jnp.int32, (c, c), 1)).astype(F32)
    xs = [eye - a for a in mats]
    ps = list(mats)
    k = 2
    while k < c + 1:
        ps = [_mm3(p, p) for p in ps]
        xs = [x + _mm3(x, p) for x, p in zip(xs, ps)]
        k *= 2
    return xs


def _wgrad(a, b, name, bk=1024, bn=1024, bt=512):
    t_len, k_len = a.shape
    n_len = b.shape[1]
    bk, bn, bt = min(bk, k_len), min(bn, n_len), min(bt, t_len)
    nt = t_len // bt

    def body(a_ref, b_ref, o_ref, acc_ref):
        t = pl.program_id(2)

        @pl.when(t == 0)
        def _():
            acc_ref[...] = jnp.zeros_like(acc_ref)

        acc_ref[...] += _dot(a_ref[...], b_ref[...], TN)

        @pl.when(t == nt - 1)
        def _():
            o_ref[...] = acc_ref[...]

    return pl.pallas_call(
        body, name=name, grid=(k_len // bk, n_len // bn, nt),
        in_specs=[pl.BlockSpec((bt, bk), lambda i, j, t: (t, i)), pl.BlockSpec((bt, bn), lambda i, j, t: (t, j))],
        out_specs=pl.BlockSpec((bk, bn), lambda i, j, t: (i, j)),
        out_shape=jax.ShapeDtypeStruct((k_len, n_len), F32),
        scratch_shapes=[pltpu.VMEM((bk, bn), F32)],
        compiler_params=_cparams(("parallel", "parallel", "arbitrary")),
    )(a, b)


def _rows_matmul(a, b, name, bt=512):
    r_len, t_len = a.shape
    n_len = b.shape[1]
    bt = min(bt, t_len)
    nt = t_len // bt

    def body(a_ref, b_ref, o_ref):
        t = pl.program_id(0)

        @pl.when(t == 0)
        def _():
            o_ref[...] = jnp.zeros_like(o_ref)

        o_ref[...] += _dot(a_ref[...], b_ref[...], NN)

    return pl.pallas_call(
        body, name=name, grid=(nt,),
        in_specs=[pl.BlockSpec((r_len, bt), lambda t: (0, t)), pl.BlockSpec((bt, n_len), lambda t: (t, 0))],
        out_specs=pl.BlockSpec((r_len, n_len), lambda t: (0, 0)),
        out_shape=jax.ShapeDtypeStruct((r_len, n_len), F32),
        compiler_params=_cparams(("arbitrary",)),
    )(a, b)


def _in_proj(x, g, wp, wst, tm=256):
    t_len, d = x.shape
    tm = min(tm, t_len)

    def body(x_ref, g_ref, wp_ref, wst_ref, h_ref, fox_ref, gdn_ref, z_ref, sm_ref, smt_ref):
        h = _rms(x_ref[...], g_ref[...]).astype(BF16)
        h_ref[...] = h
        p = _dot(h, wp_ref[...], NN)
        fox_ref[...] = p[:, P_FOX:P_GDN]
        gdn_ref[...] = p[:, P_GDN:P_Z]
        z_ref[...] = p[:, P_Z:P_SMALL]
        sm_ref[...] = p[:, P_SMALL:P_DIM]
        smt_ref[...] = _dot(wst_ref[...], h, NT)

    row = lambda i: (i, 0)
    fixed = lambda i: (0, 0)
    return pl.pallas_call(
        body, name="in_proj", grid=(t_len // tm,),
        in_specs=[pl.BlockSpec((tm, d), row), pl.BlockSpec((1, d), fixed), pl.BlockSpec((d, P_DIM), fixed),
                  pl.BlockSpec((SM_ROWS, d), fixed)],
        out_specs=[pl.BlockSpec((tm, d), row), pl.BlockSpec((tm, 1536), row), pl.BlockSpec((tm, 1536), row),
                   pl.BlockSpec((tm, 512), row), pl.BlockSpec((tm, LANES), row), pl.BlockSpec((SM_ROWS, tm), lambda i: (0, i))],
        out_shape=[jax.ShapeDtypeStruct((t_len, d), BF16), jax.ShapeDtypeStruct((t_len, 1536), F32),
                   jax.ShapeDtypeStruct((t_len, 1536), F32), jax.ShapeDtypeStruct((t_len, 512), F32),
                   jax.ShapeDtypeStruct((t_len, LANES), F32), jax.ShapeDtypeStruct((SM_ROWS, t_len), F32)],
        compiler_params=_cparams(("parallel",)),
    )(x, g, wp, wst)


def _in_proj_bwd(dproj, dsmt, x, g, wp, wst, dx1, tm=256):
    t_len, d = x.shape
    tm = min(tm, t_len)

    def body(dp_ref, dst_ref, x_ref, g_ref, wp_ref, wst_ref, dx1_ref, dx_ref, dg_ref):
        i = pl.program_id(0)
        dh = _dot(dp_ref[...], wp_ref[...], NT) + _dot(dst_ref[...], wst_ref[...], TN)
        dxn, dg = _rms_bwd(x_ref[...], g_ref[...], dh)
        dx_ref[...] = dx1_ref[...] + dxn

        @pl.when(i == 0)
        def _():
            dg_ref[...] = jnp.zeros_like(dg_ref)

        dg_ref[...] += dg

    row = lambda i: (i, 0)
    fixed = lambda i: (0, 0)
    return pl.pallas_call(
        body, name="in_proj_bwd", grid=(t_len // tm,),
        in_specs=[pl.BlockSpec((tm, P_DIM), row), pl.BlockSpec((SM_ROWS, tm), lambda i: (0, i)), pl.BlockSpec((tm, d), row),
                  pl.BlockSpec((1, d), fixed), pl.BlockSpec((d, P_DIM), fixed), pl.BlockSpec((SM_ROWS, d), fixed),
                  pl.BlockSpec((tm, d), row)],
        out_specs=[pl.BlockSpec((tm, d), row), pl.BlockSpec((1, d), fixed)],
        out_shape=[jax.ShapeDtypeStruct((t_len, d), F32), jax.ShapeDtypeStruct((1, d), F32)],
        compiler_params=_cparams(("arbitrary",)),
    )(dproj, dsmt, x, g, wp, wst, dx1)


def _fox_cum(smt, bias_col, n_batch, s_len, ck=256):
    ck = min(ck, s_len)

    def body(s_ref, b_ref, c_ref):
        tri = (lax.broadcasted_iota(jnp.int32, (ck, ck), 0) <= lax.broadcasted_iota(jnp.int32, (ck, ck), 1)).astype(F32)
        carry = jnp.zeros((SM_ROWS, 1), F32)
        for r in range(s_len // ck):
            ls = _log_sigmoid(s_ref[:, r * ck:(r + 1) * ck] + b_ref[...])
            c = jnp.dot(ls, tri, precision=lax.Precision.HIGHEST, preferred_element_type=F32) + carry
            c_ref[:, r * ck:(r + 1) * ck] = c
            carry = c[:, ck - 1:ck]

    return pl.pallas_call(
        body, name="fox_cum", grid=(n_batch,),
        in_specs=[pl.BlockSpec((SM_ROWS, s_len), lambda b: (0, b)), pl.BlockSpec((SM_ROWS, 1), lambda b: (0, 0))],
        out_specs=pl.BlockSpec((SM_ROWS, s_len), lambda b: (0, b)),
        out_shape=jax.ShapeDtypeStruct(smt.shape, F32),
        compiler_params=_cparams(("parallel",)),
    )(smt, bias_col)


def _fox_cum_bwd(dc, smt, bias_col, n_batch, s_len, ck=256):
    ck = min(ck, s_len)
    nr = s_len // ck

    def body(dc_ref, s_ref, b_ref, dl_ref, db_ref):
        b = pl.program_id(0)
        tri = (lax.broadcasted_iota(jnp.int32, (ck, ck), 0) >= lax.broadcasted_iota(jnp.int32, (ck, ck), 1)).astype(F32)
        carry = jnp.zeros((SM_ROWS, 1), F32)
        tot = jnp.zeros((SM_ROWS, 1), F32)
        for r in reversed(range(nr)):
            sl = slice(r * ck, (r + 1) * ck)
            dls = jnp.dot(dc_ref[:, sl], tri, precision=lax.Precision.HIGHEST, preferred_element_type=F32) + carry
            carry = dls[:, 0:1]
            dl = dls * (1.0 - _sigmoid(s_ref[:, sl] + b_ref[...]))
            dl_ref[:, sl] = dl
            tot = tot + jnp.sum(dl, axis=1, keepdims=True)

        @pl.when(b == 0)
        def _():
            db_ref[...] = jnp.zeros_like(db_ref)

        db_ref[...] += jnp.broadcast_to(tot, db_ref.shape)

    return pl.pallas_call(
        body, name="fox_cum_bwd", grid=(n_batch,),
        in_specs=[pl.BlockSpec((SM_ROWS, s_len), lambda b: (0, b)), pl.BlockSpec((SM_ROWS, s_len), lambda b: (0, b)),
                  pl.BlockSpec((SM_ROWS, 1), lambda b: (0, 0))],
        out_specs=[pl.BlockSpec((SM_ROWS, s_len), lambda b: (0, b)), pl.BlockSpec((SM_ROWS, LANES), lambda b: (0, 0))],
        out_shape=[jax.ShapeDtypeStruct(smt.shape, F32), jax.ShapeDtypeStruct((SM_ROWS, LANES), F32)],
        compiler_params=_cparams(("arbitrary",)),
    )(dc, smt, bias_col)


def _fox_diagonal_mask(tq):
    return lax.broadcasted_iota(jnp.int32, (tq, tq), 1) <= lax.broadcasted_iota(jnp.int32, (tq, tq), 0)


def _fox_fwd(pf, cb, gq2, gk2, go2, tq=256):
    n_batch, s_len, _ = pf.shape
    tq = min(tq, s_len)
    nq = s_len // tq
    scale = FOX_HEAD_DIM ** -0.5

    def body(q_ref, k_ref, v_ref, c_ref, gq_ref, gk_ref, go_ref, o_ref, on_ref, lse_ref, kh_ref, vh_ref):
        j = pl.program_id(1)
        i = pl.program_id(2)
        m0 = lax.broadcasted_iota(jnp.int32, (1, LANES), 1) < FOX_HEAD_DIM

        @pl.when(i == 0)
        def _():
            kn = _rms_pair(k_ref[0], gk_ref[...], m0)
            kh_ref[0] = jnp.where(m0, kn, 0.0).astype(BF16)
            kh_ref[1] = jnp.where(m0, 0.0, kn).astype(BF16)
            v = v_ref[0]
            vh_ref[0] = jnp.where(m0, v, 0.0).astype(BF16)
            vh_ref[1] = jnp.where(m0, 0.0, v).astype(BF16)

        qb = (_rms_pair(q_ref[0], gq_ref[...], m0) * scale).astype(BF16)

        def step(kb, carry, diagonal=False):
            ms, ls, acc = carry
            off = pl.multiple_of(kb * tq, tq)
            new_m, new_l, alphas, pv = [], [], [], []
            for hh in range(2):
                s = _dot(qb, kh_ref[hh, pl.ds(off, tq), :], NT)
                s = s - c_ref[0, kb, pl.ds(2 * j + hh, 1), :]
                if diagonal:
                    s = jnp.where(_fox_diagonal_mask(tq), s, NEG_INF)
                m_new = jnp.maximum(ms[hh], jnp.max(s, axis=-1, keepdims=True))
                alpha = jnp.exp(ms[hh] - m_new)
                p = jnp.exp(s - m_new)
                new_l.append(alpha * ls[hh] + jnp.sum(p, axis=-1, keepdims=True))
                new_m.append(m_new)
                alphas.append(alpha)
                pv.append(_dot(p, vh_ref[hh, pl.ds(off, tq), :], NN))
            acc = jnp.where(m0, alphas[0], alphas[1]) * acc + pv[0] + pv[1]
            return tuple(new_m), tuple(new_l), acc

        init_m = (jnp.full((tq, 1), NEG_INF, F32),) * 2
        init_l = (jnp.zeros((tq, 1), F32),) * 2
        carry = lax.fori_loop(0, i, step, (init_m, init_l, jnp.zeros((tq, LANES), F32)))
        ms, ls, acc = step(i, carry, diagonal=True)
        o = acc / jnp.where(m0, ls[0], ls[1])
        o_ref[0] = o
        on_ref[0] = _rms_pair(o, go_ref[...], m0).astype(BF16)
        lse_ref[0] = jnp.where(m0, ms[0] + jnp.log(ls[0]), ms[1] + jnp.log(ls[1]))

    fixed = lambda b, j, i: (0, 0)
    tile = lambda b, j, i: (b, i, j)
    return pl.pallas_call(
        body, name="fox_fwd", grid=(n_batch, 4, nq),
        in_specs=[pl.BlockSpec((1, tq, LANES), tile), pl.BlockSpec((1, s_len, LANES), lambda b, j, i: (b, 0, 4 + j)),
                  pl.BlockSpec((1, s_len, LANES), lambda b, j, i: (b, 0, 8 + j)),
                  pl.BlockSpec((1, nq, SM_ROWS, tq), lambda b, j, i: (b, 0, 0, 0)),
                  pl.BlockSpec((1, LANES), fixed), pl.BlockSpec((1, LANES), fixed), pl.BlockSpec((1, LANES), fixed)],
        out_specs=[pl.BlockSpec((1, tq, LANES), tile), pl.BlockSpec((1, tq, LANES), tile), pl.BlockSpec((1, tq, LANES), tile)],
        out_shape=[jax.ShapeDtypeStruct((n_batch, s_len, FOX_WIDTH), F32), jax.ShapeDtypeStruct((n_batch, s_len, FOX_WIDTH), BF16),
                   jax.ShapeDtypeStruct((n_batch, s_len, FOX_WIDTH), F32)],
        scratch_shapes=[pltpu.VMEM((2, s_len, LANES), BF16), pltpu.VMEM((2, s_len, LANES), BF16)],
        compiler_params=_cparams(("parallel", "parallel", "arbitrary")),
    )(pf, pf, pf, cb, gq2, gk2, go2)


def _fox_bwd(pf, cb, gq2, gk2, go2, o, lse, don, tq=256):
    n_batch, s_len, _ = pf.shape
    tq = min(tq, s_len)
    nq = s_len // tq
    scale = FOX_HEAD_DIM ** -0.5

    def body(q_ref, k_ref, v_ref, c_ref, gq_ref, gk_ref, go_ref, o_ref, lse_ref, don_ref,
             dq_ref, dk_ref, dv_ref, dc_ref, dgq_ref, dgk_ref, dgo_ref, kh_ref, vh_ref, dka_ref, dva_ref, dca_ref):
        b = pl.program_id(0)
        j = pl.program_id(1)
        i = pl.program_id(2)
        m0 = lax.broadcasted_iota(jnp.int32, (1, LANES), 1) < FOX_HEAD_DIM

        @pl.when((b == 0) & (j == 0) & (i == 0))
        def _():
            dgq_ref[...] = jnp.zeros_like(dgq_ref)
            dgk_ref[...] = jnp.zeros_like(dgk_ref)
            dgo_ref[...] = jnp.zeros_like(dgo_ref)

        @pl.when(i == 0)
        def _():
            kn = _rms_pair(k_ref[0], gk_ref[...], m0)
            kh_ref[0] = jnp.where(m0, kn, 0.0).astype(BF16)
            kh_ref[1] = jnp.where(m0, 0.0, kn).astype(BF16)
            v = v_ref[0]
            vh_ref[0] = jnp.where(m0, v, 0.0).astype(BF16)
            vh_ref[1] = jnp.where(m0, 0.0, v).astype(BF16)
            dka_ref[...] = jnp.zeros_like(dka_ref)
            dva_ref[...] = jnp.zeros_like(dva_ref)
            dca_ref[...] = jnp.zeros_like(dca_ref)

        q = q_ref[0]
        qn = _rms_pair(q, gq_ref[...], m0)
        qs = qn * scale
        qb = qs.astype(BF16)
        qh = (jnp.where(m0, qs, 0.0).astype(BF16), jnp.where(m0, 0.0, qs).astype(BF16))
        ot = o_ref[0]
        do, dgo = _rms_pair_bwd(ot, go_ref[...], don_ref[0], m0)
        dgo_ref[...] += dgo
        dd = do * ot
        delta = (jnp.sum(jnp.where(m0, dd, 0.0), axis=-1, keepdims=True), jnp.sum(jnp.where(m0, 0.0, dd), axis=-1, keepdims=True))
        doh = (jnp.where(m0, do, 0.0).astype(BF16), jnp.where(m0, 0.0, do).astype(BF16))
        lse_t = lse_ref[0]
        lse_h = (lse_t[:, 0:1], lse_t[:, FOX_HEAD_DIM:FOX_HEAD_DIM + 1])

        def step(kb, carry, diagonal=False):
            dqn, rs = carry
            rs = list(rs)
            off = pl.multiple_of(kb * tq, tq)
            for hh in range(2):
                kblk = kh_ref[hh, pl.ds(off, tq), :]
                vblk = vh_ref[hh, pl.ds(off, tq), :]
                s = _dot(qb, kblk, NT)
                s = s - c_ref[0, kb, pl.ds(2 * j + hh, 1), :]
                if diagonal:
                    s = jnp.where(_fox_diagonal_mask(tq), s, NEG_INF)
                p = jnp.exp(s - lse_h[hh])
                dp = _dot(doh[hh], vblk, NT)
                ds = p * (dp - delta[hh])
                dva_ref[pl.ds(off, tq), :] += _dot(p, doh[hh], TN)
                dka_ref[pl.ds(off, tq), :] += _dot(ds, qh[hh], TN)
                dca_ref[kb, hh:hh + 1, :] += -jnp.sum(ds, axis=0, keepdims=True)
                rs[hh] = rs[hh] + jnp.sum(ds, axis=-1, keepdims=True)
                dqn = dqn + _dot(ds, kblk, NN)
            return dqn, tuple(rs)

        carry = lax.fori_loop(0, i, step, (jnp.zeros((tq, LANES), F32), (jnp.zeros((tq, 1), F32),) * 2))
        dqn, rs = step(i, carry, diagonal=True)
        dqn = dqn * scale
        rs_rows = jnp.where(m0, rs[0], rs[1]).T
        dca_ref[i, 0:1, :] += rs_rows[0:1, :]
        dca_ref[i, 1:2, :] += rs_rows[FOX_HEAD_DIM:FOX_HEAD_DIM + 1, :]
        dq, dgq = _rms_pair_bwd(q, gq_ref[...], dqn, m0)
        dq_ref[0] = dq.astype(BF16)
        dgq_ref[...] += dgq

        @pl.when(i == nq - 1)
        def _():
            dk, dgk = _rms_pair_bwd(k_ref[0], gk_ref[...], dka_ref[...], m0)
            dk_ref[0] = dk.astype(BF16)
            dgk_ref[...] += dgk
            dv_ref[0] = dva_ref[...].astype(BF16)
            dc_ref[0, 0] = dca_ref[...]

    fixed = lambda b, j, i: (0, 0)
    tile = lambda b, j, i: (b, i, j)
    full = lambda b, j, i: (b, 0, j)
    wide = jax.ShapeDtypeStruct((n_batch, s_len, FOX_WIDTH), BF16)
    gain = jax.ShapeDtypeStruct((1, LANES), F32)
    return pl.pallas_call(
        body, name="fox_bwd", grid=(n_batch, 4, nq),
        in_specs=[pl.BlockSpec((1, tq, LANES), tile), pl.BlockSpec((1, s_len, LANES), lambda b, j, i: (b, 0, 4 + j)),
                  pl.BlockSpec((1, s_len, LANES), lambda b, j, i: (b, 0, 8 + j)),
                  pl.BlockSpec((1, nq, SM_ROWS, tq), lambda b, j, i: (b, 0, 0, 0)),
                  pl.BlockSpec((1, LANES), fixed), pl.BlockSpec((1, LANES), fixed), pl.BlockSpec((1, LANES), fixed),
                  pl.BlockSpec((1, tq, LANES), tile), pl.BlockSpec((1, tq, LANES), tile), pl.BlockSpec((1, tq, LANES), tile)],
        out_specs=[pl.BlockSpec((1, tq, LANES), tile), pl.BlockSpec((1, s_len, LANES), full), pl.BlockSpec((1, s_len, LANES), full),
                   pl.BlockSpec((1, 1, nq, 8, tq), lambda b, j, i: (b, j, 0, 0, 0)),
                   pl.BlockSpec((1, LANES), fixed), pl.BlockSpec((1, LANES), fixed), pl.BlockSpec((1, LANES), fixed)],
        out_shape=[wide, wide, wide, jax.ShapeDtypeStruct((n_batch, 4, nq, 8, tq), F32), gain, gain, gain],
        scratch_shapes=[pltpu.VMEM((2, s_len, LANES), BF16), pltpu.VMEM((2, s_len, LANES), BF16),
                        pltpu.VMEM((s_len, LANES), F32), pltpu.VMEM((s_len, LANES), F32), pltpu.VMEM((nq, 8, tq), F32)],
        compiler_params=_cparams(("arbitrary", "arbitrary", "arbitrary")),
    )(pf, pf, pf, cb, gq2, gk2, go2, o, lse, don)


def _shift_down(x, k):
    row = lax.broadcasted_iota(jnp.int32, x.shape, 0)
    return jnp.where(row >= k, pltpu.roll(x, k, 0), 0.0)


def _shift_up(x, k):
    n = x.shape[0]
    row = lax.broadcasted_iota(jnp.int32, x.shape, 0)
    return jnp.where(row < n - k, pltpu.roll(x, n - k, 0), 0.0)


def _conv_silu(x, w):
    y = w[3:4] * x + w[2:3] * _shift_down(x, 1) + w[1:2] * _shift_down(x, 2) + w[0:1] * _shift_down(x, 3)
    return y, y * _sigmoid(y)


def _gdn_pre(pg, conv_w):
    n_batch, s_len, width = pg.shape
    ncb = width // LANES

    def body(x_ref, w_ref, o_ref):
        cb = pl.program_id(1)
        _, s = _conv_silu(x_ref[0], w_ref[...])
        sn = s * lax.rsqrt(jnp.sum(s * s, axis=-1, keepdims=True) + EPS)
        o_ref[0] = jnp.where(cb < 2 * GDN_HEADS, sn, s)

    return pl.pallas_call(
        body, name="gdn_pre", grid=(n_batch, ncb),
        in_specs=[pl.BlockSpec((1, s_len, LANES), lambda b, c: (b, 0, c)), pl.BlockSpec((8, LANES), lambda b, c: (0, c))],
        out_specs=pl.BlockSpec((1, s_len, LANES), lambda b, c: (b, 0, c)),
        out_shape=jax.ShapeDtypeStruct(pg.shape, F32),
        compiler_params=_cparams(("parallel", "parallel")),
    )(pg, conv_w)


def _gdn_pre_bwd(pg, conv_w, dout):
    n_batch, s_len, width = pg.shape
    ncb = width // LANES

    def body(x_ref, w_ref, d_ref, dx_ref, dw_ref):
        cb = pl.program_id(0)
        b = pl.program_id(1)
        x = x_ref[0]
        w = w_ref[...]
        d = d_ref[0]
        y, s = _conv_silu(x, w)
        rr = lax.rsqrt(jnp.sum(s * s, axis=-1, keepdims=True) + EPS)
        sn = s * rr
        ds_n = rr * (d - sn * jnp.sum(d * sn, axis=-1, keepdims=True))
        ds = jnp.where(cb < 2 * GDN_HEADS, ds_n, d)
        sig = _sigmoid(y)
        dy = ds * (sig * (1.0 + y * (1.0 - sig)))
        dx = w[3:4] * dy + w[2:3] * _shift_up(dy, 1) + w[1:2] * _shift_up(dy, 2) + w[0:1] * _shift_up(dy, 3)
        dx_ref[0] = dx.astype(BF16)
        dw = [jnp.sum(dy * _shift_down(x, 3 - jj), axis=0, keepdims=True) if jj < 3 else jnp.sum(dy * x, axis=0, keepdims=True)
              for jj in range(CONV_WIDTH)]
        rows = lax.broadcasted_iota(jnp.int32, (8, LANES), 0)
        dwb = jnp.zeros((8, LANES), F32)
        for jj in range(CONV_WIDTH):
            dwb = dwb + jnp.where(rows == jj, dw[jj], 0.0)

        @pl.when(b == 0)
        def _():
            dw_ref[...] = jnp.zeros_like(dw_ref)

        dw_ref[...] += dwb

    blk = lambda c, b: (b, 0, c)
    return pl.pallas_call(
        body, name="gdn_pre_bwd", grid=(ncb, n_batch),
        in_specs=[pl.BlockSpec((1, s_len, LANES), blk), pl.BlockSpec((8, LANES), lambda c, b: (0, c)), pl.BlockSpec((1, s_len, LANES), blk)],
        out_specs=[pl.BlockSpec((1, s_len, LANES), blk), pl.BlockSpec((8, LANES), lambda c, b: (0, c))],
        out_shape=[jax.ShapeDtypeStruct(pg.shape, BF16), jax.ShapeDtypeStruct((8, width), F32)],
        compiler_params=_cparams(("parallel", "arbitrary")),
    )(pg, conv_w, dout)


def _gdn_gates(smc, smr, a_c, dt_c, a_r, dt_r, h):
    lane = lax.broadcasted_iota(jnp.int32, (1, LANES), 1)
    sub = lax.broadcasted_iota(jnp.int32, (SM_ROWS, 1), 0)
    beta_c = jnp.sum(jnp.where(lane == SM_B + h, _sigmoid(smc), 0.0), axis=1, keepdims=True)
    g_all_c = -jnp.exp(a_c) * _softplus(smc + dt_c)
    g_c = jnp.sum(jnp.where(lane == SM_A + h, g_all_c, 0.0), axis=1, keepdims=True)
    g_all_r = -jnp.exp(a_r) * _softplus(smr + dt_r)
    g_r = jnp.sum(jnp.where(sub == SM_A + h, g_all_r, 0.0), axis=0, keepdims=True)
    return beta_c, g_c, g_r


def _gdn_group(qkv, z, smc, smr, a_c, dt_c, a_r, dt_r, go, states):
    n_grp = len(qkv)
    c = qkv[0].shape[0]
    hd = GDN_HEAD_DIM
    pairs = [(g, h) for g in range(n_grp) for h in range(GDN_HEADS)]
    ii = lax.broadcasted_iota(jnp.int32, (c, c), 0)
    jj = lax.broadcasted_iota(jnp.int32, (c, c), 1)
    incl = ii >= jj
    col = lambda arr, base, h: arr[:, base + h * hd:base + (h + 1) * hd]

    qs, ks, kbs, vbs, decays, gcs, g_lasts, amats = [], [], [], [], [], [], [], []
    for g, h in pairs:
        beta_c, g_c, g_r = _gdn_gates(smc[g], smr[g], a_c, dt_c, a_r, dt_r, h)
        gc_c = jnp.sum(jnp.where(incl, g_r, 0.0), axis=1, keepdims=True)
        gc_r = jnp.sum(jnp.where(ii <= jj, g_c, 0.0), axis=0, keepdims=True)
        decay = jnp.where(incl, jnp.exp(jnp.where(incl, gc_c - gc_r, 0.0)), 0.0)
        k = col(qkv[g], GDN_WIDTH, h)
        kb = k * beta_c
        qs.append(col(qkv[g], 0, h) * (hd ** -0.5))
        ks.append(k)
        kbs.append(kb)
        vbs.append(col(qkv[g], 2 * GDN_WIDTH, h) * beta_c)
        decays.append(decay)
        gcs.append(gc_c)
        g_lasts.append(jnp.sum(g_c, axis=0, keepdims=True))
        amats.append(jnp.where(ii > jj, _mm_nt(kb, k) * decay, 0.0))
    ts = _unit_lower_inverses(amats)
    egcs = [jnp.exp(gc) for gc in gcs]
    us = [_mm_nn(t, vb) for t, vb in zip(ts, vbs)]
    ws = [_mm_nn(t, kb * e) for t, kb, e in zip(ts, kbs, egcs)]
    intras = [_mm_nt(q, k) * d for q, k, d in zip(qs, ks, decays)]
    qes = [q * e for q, e in zip(qs, egcs)]
    kds = [k * jnp.exp(gl - gc) for k, gl, gc in zip(ks, g_lasts, gcs)]
    sdecs = [jnp.exp(gl) for gl in g_lasts]

    outs = []
    for g in range(n_grp):
        idx = [g * GDN_HEADS + h for h in range(GDN_HEADS)]
        v_new = [us[i] - _mm_nn(ws[i], states[h]) for h, i in enumerate(idx)]
        o_state = [_mm_nn(qes[i], states[h]) for h, i in enumerate(idx)]
        o_intra = [_mm_nn(intras[i], v_new[h]) for h, i in enumerate(idx)]
        states = [states[h] * sdecs[i] + _mm_tn(kds[i], v_new[h]) for h, i in enumerate(idx)]
        outs.append([_rms(o_state[h] + o_intra[h], go) * (col(z[g], 0, h) * _sigmoid(col(z[g], 0, h))) for h in range(GDN_HEADS)])
    return outs, states


def _gdn_group_size(n_chunks):
    return GDN_GROUP if n_chunks % GDN_GROUP == 0 else 1


def _gdn_fwd(qkvn, z, smc, smr, a_c, dt_c, a_r, dt_r, go):
    n_batch, s_len, _ = qkvn.shape
    c = GDN_CHUNK
    n = s_len // c
    grp = _gdn_group_size(n)
    ng = n // grp
    gc = grp * c
    hd = GDN_HEAD_DIM

    def body(qkv_ref, z_ref, smc_ref, smr_ref, ac_ref, dc_ref, ar_ref, dr_ref, go_ref, og_ref, st_ref, s_ref):
        @pl.when(pl.program_id(1) == 0)
        def _():
            s_ref[...] = jnp.zeros_like(s_ref)

        states = [s_ref[h] for h in range(GDN_HEADS)]
        for h in range(GDN_HEADS):
            st_ref[0, 0, h] = states[h]
        rows = lambda k: slice(k * c, (k + 1) * c)
        outs, nxt = _gdn_group([qkv_ref[0, rows(k), :] for k in range(grp)], [z_ref[0, rows(k), :] for k in range(grp)],
                               [smc_ref[0, rows(k), :] for k in range(grp)], [smr_ref[k] for k in range(grp)],
                               ac_ref[...], dc_ref[...], ar_ref[...], dr_ref[...], go_ref[...], states)
        for k in range(grp):
            for h in range(GDN_HEADS):
                og_ref[0, rows(k), h * hd:(h + 1) * hd] = outs[k][h].astype(BF16)
        for h in range(GDN_HEADS):
            s_ref[h] = nxt[h]

    tok = lambda b, i: (b, i, 0)
    fixed = lambda b, i: (0, 0)
    return pl.pallas_call(
        body, name="gdn_fwd", grid=(n_batch, ng),
        in_specs=[pl.BlockSpec((1, gc, 3 * GDN_WIDTH), tok), pl.BlockSpec((1, gc, GDN_WIDTH), tok), pl.BlockSpec((1, gc, LANES), tok),
                  pl.BlockSpec((grp, SM_ROWS, c), lambda b, i: (b * ng + i, 0, 0)),
                  pl.BlockSpec((1, LANES), fixed), pl.BlockSpec((1, LANES), fixed), pl.BlockSpec((SM_ROWS, 1), fixed),
                  pl.BlockSpec((SM_ROWS, 1), fixed), pl.BlockSpec((1, LANES), fixed)],
        out_specs=[pl.BlockSpec((1, gc, GDN_WIDTH), tok), pl.BlockSpec((1, 1, GDN_HEADS, hd, hd), lambda b, i: (b, i, 0, 0, 0))],
        out_shape=[jax.ShapeDtypeStruct((n_batch, s_len, GDN_WIDTH), BF16), jax.ShapeDtypeStruct((n_batch, ng, GDN_HEADS, hd, hd), F32)],
        scratch_shapes=[pltpu.VMEM((GDN_HEADS, hd, hd), F32)],
        compiler_params=_cparams(("parallel", "arbitrary")),
    )(qkvn, z, smc, smr, a_c, dt_c, a_r, dt_r, go)


def _gdn_bwd(qkvn, z, smc, smr, a_c, dt_c, a_r, dt_r, go, states, dog):
    n_batch, s_len, _ = qkvn.shape
    c = GDN_CHUNK
    n = s_len // c
    grp = _gdn_group_size(n)
    ng = n // grp
    gc = grp * c
    hd = GDN_HEAD_DIM

    def body(qkv_ref, z_ref, smc_ref, smr_ref, ac_ref, dc_ref, ar_ref, dr_ref, go_ref, st_ref, dog_ref,
             dqkv_ref, dz_ref, dsmc_ref, dsmr_ref, dac_ref, ddc_ref, dar_ref, ddr_ref, dgo_ref, ds_ref):
        first = (pl.program_id(0) == 0) & (pl.program_id(1) == 0)

        @pl.when(pl.program_id(1) == 0)
        def _():
            ds_ref[...] = jnp.zeros_like(ds_ref)

        @pl.when(first)
        def _():
            for r in (dac_ref, ddc_ref, dar_ref, ddr_ref, dgo_ref):
                r[...] = jnp.zeros_like(r)

        rows = lambda k: slice(k * c, (k + 1) * c)
        states = [st_ref[0, 0, h] for h in range(GDN_HEADS)]
        prim = ([qkv_ref[0, rows(k), :] for k in range(grp)], [z_ref[0, rows(k), :] for k in range(grp)],
                [smc_ref[0, rows(k), :] for k in range(grp)], [smr_ref[k] for k in range(grp)],
                ac_ref[...], dc_ref[...], ar_ref[...], dr_ref[...], go_ref[...], states)
        _, vjp = jax.vjp(_gdn_group, *prim)
        cot = ([[dog_ref[0, rows(k), h * hd:(h + 1) * hd] for h in range(GDN_HEADS)] for k in range(grp)],
               [ds_ref[h] for h in range(GDN_HEADS)])
        dqkv, dz, dsmc, dsmr, dac, ddc, dar, ddr, dgo, dstates = vjp(cot)
        for k in range(grp):
            dqkv_ref[0, rows(k), :] = dqkv[k]
            dz_ref[0, rows(k), :] = dz[k].astype(BF16)
            dsmc_ref[0, rows(k), :] = dsmc[k]
            dsmr_ref[k] = dsmr[k]
        dac_ref[...] += dac
        ddc_ref[...] += ddc
        dar_ref[...] += dar
        ddr_ref[...] += ddr
        dgo_ref[...] += dgo
        for h in range(GDN_HEADS):
            ds_ref[h] = dstates[h]

    tok = lambda b, i: (b, ng - 1 - i, 0)
    fixed = lambda b, i: (0, 0)
    lane_vec = jax.ShapeDtypeStruct((1, LANES), F32)
    row_vec = jax.ShapeDtypeStruct((SM_ROWS, 1), F32)
    return pl.pallas_call(
        body, name="gdn_bwd", grid=(n_batch, ng),
        in_specs=[pl.BlockSpec((1, gc, 3 * GDN_WIDTH), tok), pl.BlockSpec((1, gc, GDN_WIDTH), tok), pl.BlockSpec((1, gc, LANES), tok),
                  pl.BlockSpec((grp, SM_ROWS, c), lambda b, i: (b * ng + ng - 1 - i, 0, 0)),
                  pl.BlockSpec((1, LANES), fixed), pl.BlockSpec((1, LANES), fixed), pl.BlockSpec((SM_ROWS, 1), fixed),
                  pl.BlockSpec((SM_ROWS, 1), fixed), pl.BlockSpec((1, LANES), fixed),
                  pl.BlockSpec((1, 1, GDN_HEADS, hd, hd), lambda b, i: (b, ng - 1 - i, 0, 0, 0)),
                  pl.BlockSpec((1, gc, GDN_WIDTH), lambda b, i: (b, ng - 1 - i, 1))],
        out_specs=[pl.BlockSpec((1, gc, 3 * GDN_WIDTH), tok), pl.BlockSpec((1, gc, GDN_WIDTH), tok), pl.BlockSpec((1, gc, LANES), tok),
                   pl.BlockSpec((grp, SM_ROWS, c), lambda b, i: (b * ng + ng - 1 - i, 0, 0)),
                   pl.BlockSpec((1, LANES), fixed), pl.BlockSpec((1, LANES), fixed), pl.BlockSpec((SM_ROWS, 1), fixed),
                   pl.BlockSpec((SM_ROWS, 1), fixed), pl.BlockSpec((1, LANES), fixed)],
        out_shape=[jax.ShapeDtypeStruct((n_batch, s_len, 3 * GDN_WIDTH), F32), jax.ShapeDtypeStruct((n_batch, s_len, GDN_WIDTH), BF16),
                   jax.ShapeDtypeStruct((n_batch, s_len, LANES), F32), jax.ShapeDtypeStruct((n_batch * n, SM_ROWS, c), F32),
                   lane_vec, lane_vec, row_vec, row_vec, lane_vec],
        scratch_shapes=[pltpu.VMEM((GDN_HEADS, hd, hd), F32)],
        compiler_params=_cparams(("arbitrary", "arbitrary")),
    )(qkvn, z, smc, smr, a_c, dt_c, a_r, dt_r, go, states, dog)


def _out_proj(x, oa, ob, w_out, g_x, w_cq, tm=256):
    t_len, d = x.shape
    tm = min(tm, t_len)

    def body(x_ref, oa_ref, ob_ref, wo_ref, g_ref, wq_ref, x1_ref, hq_ref, cq_ref):
        x1 = x_ref[...] + _dot(oa_ref[...], wo_ref[0:FOX_WIDTH, :]) + _dot(ob_ref[...], wo_ref[FOX_WIDTH:2 * FOX_WIDTH, :])
        x1_ref[...] = x1
        hq = _rms(x1, g_ref[...]).astype(BF16)
        hq_ref[...] = hq
        cq_ref[...] = _dot(hq, wq_ref[...])

    row = lambda i: (i, 0)
    fixed = lambda i: (0, 0)
    return pl.pallas_call(
        body, name="out_proj", grid=(t_len // tm,),
        in_specs=[pl.BlockSpec((tm, d), row), pl.BlockSpec((tm, FOX_WIDTH), row), pl.BlockSpec((tm, GDN_WIDTH), row),
                  pl.BlockSpec((d, d), fixed), pl.BlockSpec((1, d), fixed), pl.BlockSpec((d, XATTN_WIDTH), fixed)],
        out_specs=[pl.BlockSpec((tm, d), row), pl.BlockSpec((tm, d), row), pl.BlockSpec((tm, XATTN_WIDTH), row)],
        out_shape=[jax.ShapeDtypeStruct((t_len, d), F32), jax.ShapeDtypeStruct((t_len, d), BF16), jax.ShapeDtypeStruct((t_len, XATTN_WIDTH), F32)],
        compiler_params=_cparams(("parallel",)),
    )(x, oa, ob, w_out, g_x, w_cq)


def _out_proj_bwd(dx1, w_out, tm=512):
    t_len, d = dx1.shape
    tm = min(tm, t_len)

    def body(dx_ref, w_ref, o_ref):
        o_ref[...] = _dot(dx_ref[...], w_ref[...], NT)

    return pl.pallas_call(
        body, name="out_proj_bwd", grid=(t_len // tm,),
        in_specs=[pl.BlockSpec((tm, d), lambda i: (i, 0)), pl.BlockSpec((d, d), lambda i: (0, 0))],
        out_specs=pl.BlockSpec((tm, d), lambda i: (i, 0)),
        out_shape=jax.ShapeDtypeStruct((t_len, d), F32),
        compiler_params=_cparams(("parallel",)),
    )(dx1, w_out)


def _mem_kv(mem, g, w_ckv, tm=256):
    t_len, d = mem.shape
    tm = min(tm, t_len)

    def body(x_ref, g_ref, w_ref, h_ref, o_ref):
        h = _rms(x_ref[...], g_ref[...]).astype(BF16)
        h_ref[...] = h
        o_ref[...] = _dot(h, w_ref[...])

    row = lambda i: (i, 0)
    fixed = lambda i: (0, 0)
    return pl.pallas_call(
        body, name="mem_kv", grid=(t_len // tm,),
        in_specs=[pl.BlockSpec((tm, d), row), pl.BlockSpec((1, d), fixed), pl.BlockSpec((d, 2 * XATTN_WIDTH), fixed)],
        out_specs=[pl.BlockSpec((tm, d), row), pl.BlockSpec((tm, 2 * XATTN_WIDTH), row)],
        out_shape=[jax.ShapeDtypeStruct((t_len, d), BF16), jax.ShapeDtypeStruct((t_len, 2 * XATTN_WIDTH), F32)],
        compiler_params=_cparams(("parallel",)),
    )(mem, g, w_ckv)


def _mem_kv_bwd(dckv, mem, g, w_ckv, tm=256):
    t_len, d = mem.shape
    tm = min(tm, t_len)

    def body(d_ref, x_ref, g_ref, w_ref, dg_ref):
        @pl.when(pl.program_id(0) == 0)
        def _():
            dg_ref[...] = jnp.zeros_like(dg_ref)

        dh = _dot(d_ref[...], w_ref[...], NT)
        _, dg = _rms_bwd(x_ref[...], g_ref[...], dh)
        dg_ref[...] += dg

    row = lambda i: (i, 0)
    fixed = lambda i: (0, 0)
    return pl.pallas_call(
        body, name="mem_kv_bwd", grid=(t_len // tm,),
        in_specs=[pl.BlockSpec((tm, 2 * XATTN_WIDTH), row), pl.BlockSpec((tm, d), row), pl.BlockSpec((1, d), fixed),
                  pl.BlockSpec((d, 2 * XATTN_WIDTH), fixed)],
        out_specs=pl.BlockSpec((1, d), fixed),
        out_shape=jax.ShapeDtypeStruct((1, d), F32),
        compiler_params=_cparams(("arbitrary",)),
    )(dckv, mem, g, w_ckv)


def _xattn_probs(qn, kn):
    s = _dot(qn, kn, NT) * (XATTN_HEAD_DIM ** -0.5)
    p = jnp.exp(s - jnp.max(s, axis=-1, keepdims=True))
    return p / jnp.sum(p, axis=-1, keepdims=True)


def _xattn_fwd(cq, ckv, x1, gq, gk, w_co, g_mlp, n_batch, s_len, m_len, tq=256):
    d = x1.shape[1]
    tq = min(tq, s_len)
    nq = s_len // tq
    hd = XATTN_HEAD_DIM

    def body(cq_ref, kv_ref, x1_ref, gq_ref, gk_ref, wo_ref, gm_ref, co_ref, x2_ref, hf_ref):
        outs = []
        for h in range(XATTN_HEADS):
            qn = _rms(cq_ref[:, h * hd:(h + 1) * hd], gq_ref[...])
            kn = _rms(kv_ref[:, h * hd:(h + 1) * hd], gk_ref[...])
            p = _xattn_probs(qn, kn)
            outs.append(_dot(p, kv_ref[:, XATTN_WIDTH + h * hd:XATTN_WIDTH + (h + 1) * hd]).astype(BF16))
        x2 = x1_ref[...]
        for h in range(XATTN_HEADS):
            co_ref[:, h * hd:(h + 1) * hd] = outs[h]
            x2 = x2 + _dot(outs[h], wo_ref[h * hd:(h + 1) * hd, :])
        x2_ref[...] = x2
        hf_ref[...] = _rms(x2, gm_ref[...]).astype(BF16)

    row = lambda b, i: (b * nq + i, 0)
    fixed = lambda b, i: (0, 0)
    t_len = n_batch * s_len
    return pl.pallas_call(
        body, name="xattn_fwd", grid=(n_batch, nq),
        in_specs=[pl.BlockSpec((tq, XATTN_WIDTH), row), pl.BlockSpec((m_len, 2 * XATTN_WIDTH), lambda b, i: (b, 0)),
                  pl.BlockSpec((tq, d), row), pl.BlockSpec((1, hd), fixed), pl.BlockSpec((1, hd), fixed),
                  pl.BlockSpec((XATTN_WIDTH, d), fixed), pl.BlockSpec((1, d), fixed)],
        out_specs=[pl.BlockSpec((tq, XATTN_WIDTH), row), pl.BlockSpec((tq, d), row), pl.BlockSpec((tq, d), row)],
        out_shape=[jax.ShapeDtypeStruct((t_len, XATTN_WIDTH), BF16), jax.ShapeDtypeStruct((t_len, d), F32),
                   jax.ShapeDtypeStruct((t_len, d), BF16)],
        compiler_params=_cparams(("parallel", "parallel")),
    )(cq, ckv, x1, gq, gk, w_co, g_mlp)


def _xattn_bwd(dx2, cq, ckv, x1, gq, gk, w_co, g_x, w_cq, n_batch, s_len, m_len, tq=256):
    d = x1.shape[1]
    tq = min(tq, s_len)
    nq = s_len // tq
    hd = XATTN_HEAD_DIM
    scale = XATTN_HEAD_DIM ** -0.5

    def body(dx2_ref, cq_ref, kv_ref, x1_ref, gq_ref, gk_ref, wo_ref, gx_ref, wq_ref,
             dx1_ref, dcq_ref, dkv_ref, dgq_ref, dgk_ref, dgx_ref, dk_acc, dv_acc):
        b = pl.program_id(0)
        i = pl.program_id(1)

        @pl.when((b == 0) & (i == 0))
        def _():
            dgq_ref[...] = jnp.zeros_like(dgq_ref)
            dgk_ref[...] = jnp.zeros_like(dgk_ref)
            dgx_ref[...] = jnp.zeros_like(dgx_ref)

        @pl.when(i == 0)
        def _():
            dk_acc[...] = jnp.zeros_like(dk_acc)
            dv_acc[...] = jnp.zeros_like(dv_acc)

        dx2 = dx2_ref[...]
        dhq = jnp.zeros((tq, d), F32)
        for h in range(XATTN_HEADS):
            sl = slice(h * hd, (h + 1) * hd)
            q = cq_ref[:, sl]
            qn = _rms(q, gq_ref[...])
            kn = _rms(kv_ref[:, sl], gk_ref[...])
            v = kv_ref[:, XATTN_WIDTH + h * hd:XATTN_WIDTH + (h + 1) * hd]
            p = _xattn_probs(qn, kn)
            dco = _dot(dx2, wo_ref[sl, :], NT)
            dv_acc[:, sl] += _dot(p, dco, TN)
            dp = _dot(dco, v, NT)
            ds = p * (dp - jnp.sum(dp * p, axis=-1, keepdims=True))
            dqn = _dot(ds, kn) * scale
            dk_acc[:, sl] += _dot(ds, qn, TN) * scale
            dq, dgq = _rms_bwd(q, gq_ref[...], dqn)
            dgq_ref[...] += dgq
            dqb = dq.astype(BF16)
            dcq_ref[:, sl] = dqb
            dhq = dhq + _dot(dqb, wq_ref[:, sl], NT)
        dxn, dgx = _rms_bwd(x1_ref[...], gx_ref[...], dhq)
        dgx_ref[...] += dgx
        dx1_ref[...] = dx2 + dxn

        @pl.when(i == nq - 1)
        def _():
            for h in range(XATTN_HEADS):
                sl = slice(h * hd, (h + 1) * hd)
                dk, dgk = _rms_bwd(kv_ref[:, sl], gk_ref[...], dk_acc[:, sl])
                dgk_ref[...] += dgk
                dkv_ref[:, sl] = dk.astype(BF16)
                dkv_ref[:, XATTN_WIDTH + h * hd:XATTN_WIDTH + (h + 1) * hd] = dv_acc[:, sl].astype(BF16)

    row = lambda b, i: (b * nq + i, 0)
    fixed = lambda b, i: (0, 0)
    t_len = n_batch * s_len
    return pl.pallas_call(
        body, name="xattn_bwd", grid=(n_batch, nq),
        in_specs=[pl.BlockSpec((tq, d), row), pl.BlockSpec((tq, XATTN_WIDTH), row), pl.BlockSpec((m_len, 2 * XATTN_WIDTH), lambda b, i: (b, 0)),
                  pl.BlockSpec((tq, d), row), pl.BlockSpec((1, hd), fixed), pl.BlockSpec((1, hd), fixed),
                  pl.BlockSpec((XATTN_WIDTH, d), fixed), pl.BlockSpec((1, d), fixed), pl.BlockSpec((d, XATTN_WIDTH), fixed)],
        out_specs=[pl.BlockSpec((tq, d), row), pl.BlockSpec((tq, XATTN_WIDTH), row), pl.BlockSpec((m_len, 2 * XATTN_WIDTH), lambda b, i: (b, 0)),
                   pl.BlockSpec((1, hd), fixed), pl.BlockSpec((1, hd), fixed), pl.BlockSpec((1, d), fixed)],
        out_shape=[jax.ShapeDtypeStruct((t_len, d), F32), jax.ShapeDtypeStruct((t_len, XATTN_WIDTH), BF16),
                   jax.ShapeDtypeStruct((n_batch * m_len, 2 * XATTN_WIDTH), BF16),
                   jax.ShapeDtypeStruct((1, hd), F32), jax.ShapeDtypeStruct((1, hd), F32), jax.ShapeDtypeStruct((1, d), F32)],
        scratch_shapes=[pltpu.VMEM((m_len, XATTN_WIDTH), F32), pltpu.VMEM((m_len, XATTN_WIDTH), F32)],
        compiler_params=_cparams(("arbitrary", "arbitrary")),
    )(dx2, cq, ckv, x1, gq, gk, w_co, g_x, w_cq)


def _resident(shape):
    return pl.BlockSpec(shape, lambda *_: (0,) * len(shape), pipeline_mode=pl.Buffered(1))


def _mlp_fwd(hf, x2, target, w1, w2, tm=256, tf=1024):
    t_len, d = x2.shape
    f = w1.shape[1]
    tm, tf = min(tm, t_len), min(tf, f)

    def body(hf_ref, x2_ref, tg_ref, w1_ref, w2_ref, u_ref, a_ref, dy_ref, ls_ref):
        hf_t = hf_ref[...]
        y = x2_ref[...]
        for k in range(f // tf):
            cols = slice(k * tf, (k + 1) * tf)
            u = _dot(hf_t, w1_ref[:, cols])
            u_ref[:, cols] = u
            r = jnp.maximum(u, 0.0)
            a = (r * r).astype(BF16)
            a_ref[:, cols] = a
            y = y + _dot(a, w2_ref[cols, :])
        err = y - tg_ref[...]
        dy_ref[...] = err * (1.0 / d)
        ls_ref[...] = jnp.broadcast_to(jnp.sum(jnp.sum(err * err, axis=-1, keepdims=True) * (1.0 / d), axis=0, keepdims=True), ls_ref.shape)

    row = lambda i: (i, 0)
    return pl.pallas_call(
        body, name="mlp_fwd", grid=(t_len // tm,),
        in_specs=[pl.BlockSpec((tm, d), row), pl.BlockSpec((tm, d), row), pl.BlockSpec((tm, d), row), _resident((d, f)), _resident((f, d))],
        out_specs=[pl.BlockSpec((tm, f), row), pl.BlockSpec((tm, f), row), pl.BlockSpec((tm, d), row),
                   pl.BlockSpec((1, 8, LANES), lambda i: (i, 0, 0))],
        out_shape=[jax.ShapeDtypeStruct((t_len, f), F32), jax.ShapeDtypeStruct((t_len, f), BF16), jax.ShapeDtypeStruct((t_len, d), F32),
                   jax.ShapeDtypeStruct((t_len // tm, 8, LANES), F32)],
        compiler_params=_cparams(("parallel",)),
    )(hf, x2, target, w1, w2)


def _mlp_bwd(dy, u, x2, g, w1, w2, tm=256, tf=1024):
    t_len, d = x2.shape
    f = w1.shape[1]
    tm, tf = min(tm, t_len), min(tf, f)

    def body(dy_ref, u_ref, x2_ref, g_ref, w1_ref, w2_ref, du_ref, dx2_ref, dg_ref):
        @pl.when(pl.program_id(0) == 0)
        def _():
            dg_ref[...] = jnp.zeros_like(dg_ref)

        dy_t = dy_ref[...]
        dyb = dy_t.astype(BF16)
        dhf = jnp.zeros((tm, d), F32)
        for k in range(f // tf):
            cols = slice(k * tf, (k + 1) * tf)
            da = _dot(dyb, w2_ref[cols, :], NT)
            du = (da * (2.0 * jnp.maximum(u_ref[:, cols], 0.0))).astype(BF16)
            du_ref[:, cols] = du
            dhf = dhf + _dot(du, w1_ref[:, cols], NT)
        dxn, dg = _rms_bwd(x2_ref[...], g_ref[...], dhf)
        dx2_ref[...] = dy_t + dxn
        dg_ref[...] += dg

    row = lambda i: (i, 0)
    fixed = lambda i: (0, 0)
    return pl.pallas_call(
        body, name="mlp_bwd", grid=(t_len // tm,),
        in_specs=[pl.BlockSpec((tm, d), row), pl.BlockSpec((tm, f), row), pl.BlockSpec((tm, d), row), pl.BlockSpec((1, d), fixed),
                  _resident((d, f)), _resident((f, d))],
        out_specs=[pl.BlockSpec((tm, f), row), pl.BlockSpec((tm, d), row), pl.BlockSpec((1, d), fixed)],
        out_shape=[jax.ShapeDtypeStruct((t_len, f), BF16), jax.ShapeDtypeStruct((t_len, d), F32), jax.ShapeDtypeStruct((1, d), F32)],
        compiler_params=_cparams(("arbitrary",)),
    )(dy, u, x2, g, w1, w2)


def _pad_lanes(v, offset=0, width=LANES):
    return jnp.zeros((1, width), F32).at[:, offset:offset + v.shape[1]].set(v)


def _col(v, offset=0, rows=SM_ROWS):
    return jnp.zeros((rows, 1), F32).at[offset:offset + v.shape[1], 0].set(v[0])


def _local_step(x, mem, target, norm_mix_g, w_in, fox_qnorm_g, fox_knorm_g, fox_f_bias, fox_onorm_g, gdn_conv_w, gdn_A_log,
                gdn_dt_bias, gdn_onorm_g, w_out, norm_xattn_g, mem_norm_g, w_cq, w_ckv, xattn_qnorm_g, xattn_knorm_g, w_co,
                norm_mlp_g, w_mlp1, w_mlp2):
    n_batch, s_len, d = x.shape
    m_len = mem.shape[1]
    t_len = n_batch * s_len
    tq = min(256, s_len)
    nq = s_len // tq
    n_chunks = s_len // GDN_CHUNK
    x2d = x.reshape(t_len, d)

    wp = jnp.concatenate([w_in[:, 0:1536], w_in[:, 1544:3080], w_in[:, 3088:3600], w_in[:, 1536:1544], w_in[:, 3080:3088],
                          jnp.zeros((d, P_DIM - 3600), BF16)], axis=1)
    wst = jnp.concatenate([w_in[:, 1536:1544], w_in[:, 3080:3088]], axis=1).T
    conv_w = jnp.concatenate([gdn_conv_w, jnp.zeros((8 - CONV_WIDTH, gdn_conv_w.shape[1]), F32)], axis=0)
    bias_col = _col(fox_f_bias, SM_F)
    gq2, gk2, go2 = (jnp.tile(g, (1, 2)) for g in (fox_qnorm_g, fox_knorm_g, fox_onorm_g))
    a_c, dt_c = _pad_lanes(gdn_A_log, SM_A), _pad_lanes(gdn_dt_bias, SM_A)
    a_r, dt_r = _col(gdn_A_log, SM_A), _col(gdn_dt_bias, SM_A)

    h1, pfox, pgdn, pz, sm, smt = _in_proj(x2d, norm_mix_g, wp, wst)
    c_rows = _fox_cum(smt, bias_col, n_batch, s_len)
    cb = c_rows.reshape(SM_ROWS, n_batch, nq, tq).transpose(1, 2, 0, 3)
    pf3 = pfox.reshape(n_batch, s_len, 1536)
    o_fox, oa, lse = _fox_fwd(pf3, cb, gq2, gk2, go2, tq)
    pg3 = pgdn.reshape(n_batch, s_len, 1536)
    qkvn = _gdn_pre(pg3, conv_w)
    z3 = pz.reshape(n_batch, s_len, GDN_WIDTH)
    smc = sm.reshape(n_batch, s_len, LANES)
    smr = smt.reshape(SM_ROWS, n_batch * n_chunks, GDN_CHUNK).transpose(1, 0, 2)
    ob, states = _gdn_fwd(qkvn, z3, smc, smr, a_c, dt_c, a_r, dt_r, gdn_onorm_g)
    oa2, ob2 = oa.reshape(t_len, FOX_WIDTH), ob.reshape(t_len, GDN_WIDTH)
    x1, hq, cq = _out_proj(x2d, oa2, ob2, w_out, norm_xattn_g, w_cq)
    mem2d = mem.reshape(n_batch * m_len, d)
    hm, ckv = _mem_kv(mem2d, mem_norm_g, w_ckv)
    co, x2, hf = _xattn_fwd(cq, ckv, x1, xattn_qnorm_g, xattn_knorm_g, w_co, norm_mlp_g, n_batch, s_len, m_len)
    u, a_act, dy, loss_tiles = _mlp_fwd(hf, x2, target.reshape(t_len, d), w_mlp1, w_mlp2)
    loss = 0.5 * jnp.sum(loss_tiles[:, 0, 0])

    grads = {}
    du, dx2, grads["norm_mlp_g"] = _mlp_bwd(dy, u, x2, norm_mlp_g, w_mlp1, w_mlp2)
    grads["w_mlp2"] = _wgrad(a_act, dy, "wgrad_mlp2")
    grads["w_mlp1"] = _wgrad(hf, du, "wgrad_mlp1")
    grads["w_co"] = _wgrad(co, dx2, "wgrad_co")
    dx1, dcq, dckv, grads["xattn_qnorm_g"], grads["xattn_knorm_g"], grads["norm_xattn_g"] = _xattn_bwd(
        dx2, cq, ckv, x1, xattn_qnorm_g, xattn_knorm_g, w_co, norm_xattn_g, w_cq, n_batch, s_len, m_len)
    grads["w_cq"] = _wgrad(hq, dcq, "wgrad_cq")
    grads["w_ckv"] = _wgrad(hm, dckv, "wgrad_ckv")
    grads["mem_norm_g"] = _mem_kv_bwd(dckv, mem2d, mem_norm_g, w_ckv)
    grads["w_out"] = _wgrad(jnp.concatenate([oa2, ob2], axis=1), dx1, "wgrad_out")
    dcat = _out_proj_bwd(dx1, w_out)
    dcat3 = dcat.reshape(n_batch, s_len, d)

    dqkvn, dz, dsmc, dsmr, dac, ddc, dar, ddr, grads["gdn_onorm_g"] = _gdn_bwd(
        qkvn, z3, smc, smr, a_c, dt_c, a_r, dt_r, gdn_onorm_g, states, dcat3)
    grads["gdn_A_log"] = dac[:, SM_A:SM_A + GDN_HEADS] + dar[SM_A:SM_A + GDN_HEADS, 0][None, :]
    grads["gdn_dt_bias"] = ddc[:, SM_A:SM_A + GDN_HEADS] + ddr[SM_A:SM_A + GDN_HEADS, 0][None, :]
    dpg, dconv = _gdn_pre_bwd(pg3, conv_w, dqkvn)
    grads["gdn_conv_w"] = dconv[0:CONV_WIDTH]

    dq, dk, dv, dcb, dgq, dgk, dgo = _fox_bwd(pf3, cb, gq2, gk2, go2, o_fox, lse, dcat3[:, :, 0:FOX_WIDTH], tq)
    fold = lambda g: g[:, 0:FOX_HEAD_DIM] + g[:, FOX_HEAD_DIM:LANES]
    grads["fox_qnorm_g"], grads["fox_knorm_g"], grads["fox_onorm_g"] = fold(dgq), fold(dgk), fold(dgo)
    dc8 = dcb[:, :, :, 0:2, :].transpose(1, 3, 0, 2, 4).reshape(FOX_HEADS, t_len)
    dc_rows = jnp.concatenate([dc8, jnp.zeros((SM_ROWS - FOX_HEADS, t_len), F32)], axis=0)
    dl_rows, dbias = _fox_cum_bwd(dc_rows, smt, bias_col, n_batch, s_len)
    grads["fox_f_bias"] = dbias[SM_F:SM_F + FOX_HEADS, 0][None, :]
    dsm_rows = jnp.concatenate([dl_rows[0:SM_B], dsmr.transpose(1, 0, 2).reshape(SM_ROWS, t_len)[SM_B:SM_ROWS]], axis=0)

    dproj = jnp.concatenate([dq.reshape(t_len, FOX_WIDTH), dk.reshape(t_len, FOX_WIDTH), dv.reshape(t_len, FOX_WIDTH),
                             dpg.reshape(t_len, 1536), dz.reshape(t_len, GDN_WIDTH), dsmc.reshape(t_len, LANES).astype(BF16)], axis=1)
    grad_x, grads["norm_mix_g"] = _in_proj_bwd(dproj, dsm_rows, x2d, norm_mix_g, wp, wst, dx1)
    dwp = _wgrad(h1, dproj, "wgrad_in", bk=512, bn=P_DIM)
    dwst = _rows_matmul(dsm_rows, h1, "wgrad_in_rows")
    dw_small = dwp[:, P_SMALL:P_SMALL + SM_ROWS] + dwst.T
    grads["w_in"] = jnp.concatenate([dwp[:, 0:1536], dw_small[:, 0:8], dwp[:, 1536:3072], dw_small[:, 8:16], dwp[:, 3072:3584]], axis=1)
    return loss, grad_x.reshape(n_batch, s_len, d), grads


MESH_ID = pl.DeviceIdType.MESH
ANY_SPEC = pl.BlockSpec(memory_space=pl.ANY)


def _place():
    x, y, c = lax.axis_index("x"), lax.axis_index("y"), lax.axis_index("c")
    return x, y, c, [(1 - x, y), (x, 1 - y), (1 - x, 1 - y)]


def _all_gather_body(n, ins, outs, send_sems, recv_sems, local_sems):
    x, y, c, chips = _place()
    me, sibling = (x, y, c), (x, y, 1 - c)

    def copy(a, k, block, to, src=None):
        dst = outs[a].at[4 * block[0] + 2 * block[1] + block[2]]
        return pltpu.make_async_remote_copy(src_ref=dst if src is None else src, dst_ref=dst, send_sem=send_sems.at[a, k],
                                            recv_sem=recv_sems.at[a, k], device_id=to, device_id_type=MESH_ID)

    mine = [pltpu.make_async_copy(ins[a], outs[a].at[4 * x + 2 * y + c], local_sems.at[a]) for a in range(n)]
    for cp in mine:
        cp.start()
    first = []
    for a in range(n):
        first.append(copy(a, 0, me, sibling, src=ins[a]))
        first += [copy(a, 1 + j, me, (*chip, c), src=ins[a]) for j, chip in enumerate(chips)]
    for cp in first:
        cp.start()
    passed = []
    for j, chip in enumerate(chips):
        for a in range(n):
            copy(a, 1 + j, (*chip, c), me).wait_recv()
            fwd = copy(a, 4 + j, (*chip, c), sibling)
            fwd.start()
            passed.append(fwd)
    for a in range(n):
        copy(a, 0, sibling, me).wait_recv()
        for j, chip in enumerate(chips):
            copy(a, 4 + j, (*chip, 1 - c), me).wait_recv()
    for cp in first + passed:
        cp.wait_send()
    for cp in mine:
        cp.wait()


def _all_gather_hbm(arrs, name):
    n = len(arrs)

    def body(*refs):
        _all_gather_body(n, refs[:n], refs[n:2 * n], *refs[2 * n:])

    return pl.pallas_call(
        body, name=name, in_specs=[ANY_SPEC] * n, out_specs=[ANY_SPEC] * n,
        out_shape=[jax.ShapeDtypeStruct((N_DEV,) + a.shape, a.dtype) for a in arrs],
        scratch_shapes=[pltpu.SemaphoreType.DMA((n, 7)), pltpu.SemaphoreType.DMA((n, 7)), pltpu.SemaphoreType.DMA((n,))],
    )(*arrs)


def _pair_exchange(arrs, name):
    n = len(arrs)

    def body(*refs):
        ins, outs = refs[:n], refs[n:2 * n]
        send_sems, recv_sems = refs[2 * n:]
        x, y, c, _ = _place()
        copies = []
        for a in range(n):
            for chip in range(4):
                copies.append(pltpu.make_async_remote_copy(
                    src_ref=ins[a].at[2 * chip + (1 - c)], dst_ref=outs[a].at[chip], send_sem=send_sems.at[a, chip],
                    recv_sem=recv_sems.at[a, chip], device_id=(x, y, 1 - c), device_id_type=MESH_ID))
        for cp in copies:
            cp.start()
        for cp in copies:
            cp.wait()

    return pl.pallas_call(
        body, name=name, in_specs=[ANY_SPEC] * n, out_specs=[ANY_SPEC] * n,
        out_shape=[jax.ShapeDtypeStruct((4,) + a.shape[1:], a.dtype) for a in arrs],
        scratch_shapes=[pltpu.SemaphoreType.DMA((n, 4)), pltpu.SemaphoreType.DMA((n, 4))],
    )(*arrs)


def _chip_exchange(arrs, name):
    n = len(arrs)

    def body(*refs):
        ins, outs = refs[:n], refs[n:2 * n]
        send_sems, recv_sems = refs[2 * n:]
        x, y, c, chips = _place()
        copies = []
        for a in range(n):
            for j, chip in enumerate(chips):
                copies.append(pltpu.make_async_remote_copy(
                    src_ref=ins[a].at[2 * chip[0] + chip[1]], dst_ref=outs[a].at[j], send_sem=send_sems.at[a, j],
                    recv_sem=recv_sems.at[a, j], device_id=(*chip, c), device_id_type=MESH_ID))
        for cp in copies:
            cp.start()
        for cp in copies:
            cp.wait()

    return pl.pallas_call(
        body, name=name, in_specs=[ANY_SPEC] * n, out_specs=[ANY_SPEC] * n,
        out_shape=[jax.ShapeDtypeStruct((3,) + a.shape[1:], a.dtype) for a in arrs],
        scratch_shapes=[pltpu.SemaphoreType.DMA((n, 3)), pltpu.SemaphoreType.DMA((n, 3))],
    )(*arrs)


def _all_gather_vmem(block, name):
    def body(in_ref, out_ref, send_sems, recv_sems, local_sems):
        _all_gather_body(1, [in_ref], [out_ref], send_sems, recv_sems, local_sems)

    vmem = pl.BlockSpec(memory_space=pltpu.VMEM)
    return pl.pallas_call(
        body, name=name, in_specs=[vmem], out_specs=vmem,
        out_shape=jax.ShapeDtypeStruct((N_DEV,) + block.shape, block.dtype),
        scratch_shapes=[pltpu.SemaphoreType.DMA((1, 7)), pltpu.SemaphoreType.DMA((1, 7)), pltpu.SemaphoreType.DMA((1,))],
    )(block)


def _row_tile(rows, cols):
    if rows <= 256:
        return rows
    return 256 if cols <= 512 else 128


def _pair_sum(core, own, got, name):
    _, rows, cols = own.shape
    tr = _row_tile(rows, cols)

    def body(c_ref, own_ref, got_ref, o_ref):
        o_ref[0] = own_ref[0] + got_ref[0]

    return pl.pallas_call(
        body, name=name,
        grid_spec=pltpu.PrefetchScalarGridSpec(
            num_scalar_prefetch=1, grid=(4, rows // tr),
            in_specs=[pl.BlockSpec((1, tr, cols), lambda k, i, c: (2 * k + c[0], i, 0)),
                      pl.BlockSpec((1, tr, cols), lambda k, i, c: (k, i, 0))],
            out_specs=pl.BlockSpec((1, tr, cols), lambda k, i, c: (k, i, 0))),
        out_shape=jax.ShapeDtypeStruct((4, rows, cols), F32),
        compiler_params=_cparams(("parallel", "parallel")),
    )(core, own, got)


def _adamw(w, g, m, v):
    m_new = ADAM_B1 * m + (1.0 - ADAM_B1) * g
    v_new = ADAM_B2 * v + (1.0 - ADAM_B2) * (g * g)
    m_hat = m_new / (1.0 - ADAM_B1 ** ADAM_STEP)
    v_hat = v_new / (1.0 - ADAM_B2 ** ADAM_STEP)
    delta = -ADAM_LR * (m_hat / (jnp.sqrt(v_hat) + ADAM_EPS) + ADAM_WD * w)
    return delta, m_new, v_new


def _sum_adam(chip, sums, parts, w, m, v, name):
    n_parts, rows, cols = parts.shape
    tr = _row_tile(rows, cols)

    def body(chip_ref, *refs):
        if sums is not None:
            g = refs[0][0].astype(F32)
            refs = refs[1:]
        p_ref, w_ref, m_ref, v_ref, g_ref, d_ref, mo_ref, vo_ref = refs
        for k in range(n_parts):
            g = p_ref[k].astype(F32) if (k == 0 and sums is None) else g + p_ref[k].astype(F32)
        g_ref[...] = g
        d_ref[...], mo_ref[...], vo_ref[...] = _adamw(w_ref[...], g, m_ref[...], v_ref[...])

    tile = pl.BlockSpec((tr, cols), lambda i, ch: (i, 0))
    out = jax.ShapeDtypeStruct((rows, cols), F32)
    own = [] if sums is None else [pl.BlockSpec((1, tr, cols), lambda i, ch: (ch[0], i, 0))]
    return pl.pallas_call(
        body, name=name,
        grid_spec=pltpu.PrefetchScalarGridSpec(
            num_scalar_prefetch=1, grid=(rows // tr,),
            in_specs=own + [pl.BlockSpec((n_parts, tr, cols), lambda i, ch: (0, i, 0)), tile, tile, tile],
            out_specs=[tile, tile, tile, tile]),
        out_shape=[out, out, out, out],
        compiler_params=_cparams(("parallel",)),
    )(chip, *([] if sums is None else [sums]), parts, w, m, v)


SHARDED = ("w_in", "gdn_conv_w", "w_out", "w_cq", "w_ckv", "w_co", "w_mlp1", "w_mlp2")
COLUMN_SHARDED = ("w_in", "gdn_conv_w", "w_co", "w_mlp1")
REPLICATED = ("norm_mix_g", "fox_qnorm_g", "fox_knorm_g", "fox_f_bias", "fox_onorm_g", "gdn_A_log", "gdn_dt_bias", "gdn_onorm_g",
              "norm_xattn_g", "mem_norm_g", "xattn_qnorm_g", "xattn_knorm_g", "norm_mlp_g")
WEIGHTS = ("norm_mix_g", "w_in", "fox_qnorm_g", "fox_knorm_g", "fox_f_bias", "fox_onorm_g", "gdn_conv_w", "gdn_A_log", "gdn_dt_bias",
           "gdn_onorm_g", "w_out", "norm_xattn_g", "mem_norm_g", "w_cq", "w_ckv", "xattn_qnorm_g", "xattn_knorm_g", "w_co",
           "norm_mlp_g", "w_mlp1", "w_mlp2")
PACK_ROWS = 16
LOSS_ROW = len(REPLICATED)


def _whole(name, gathered):
    if name in COLUMN_SHARDED:
        return gathered.transpose(1, 0, 2).reshape(gathered.shape[1], N_DEV * gathered.shape[2])
    return gathered.reshape(N_DEV * gathered.shape[1], gathered.shape[2])


def _blocks(name, whole):
    if name in COLUMN_SHARDED:
        rows, cols = whole.shape
        return whole.reshape(rows, N_DEV, cols // N_DEV).transpose(1, 0, 2)
    return whole.reshape(N_DEV, whole.shape[0] // N_DEV, whole.shape[1])


def _pack(vals, fill=0.0):
    rows = [jnp.pad(vals[k], ((0, 0), (0, D_MODEL - vals[k].shape[1])), constant_values=fill) for k in REPLICATED]
    rows.append(jnp.full((PACK_ROWS - len(rows), D_MODEL), fill, F32))
    return jnp.concatenate(rows, axis=0)


def kernel(x, mem, norm_mix_g, w_in, fox_qnorm_g, fox_knorm_g, fox_f_bias, fox_onorm_g, gdn_conv_w, gdn_A_log, gdn_dt_bias, gdn_onorm_g, w_out, norm_xattn_g, mem_norm_g, w_cq, w_ckv, xattn_qnorm_g, xattn_knorm_g, w_co, norm_mlp_g, w_mlp1, w_mlp2, loss_target, m_norm_mix_g, m_w_in, m_fox_qnorm_g, m_fox_knorm_g, m_fox_f_bias, m_fox_onorm_g, m_gdn_conv_w, m_gdn_A_log, m_gdn_dt_bias, m_gdn_onorm_g, m_w_out, m_norm_xattn_g, m_mem_norm_g, m_w_cq, m_w_ckv, m_xattn_qnorm_g, m_xattn_knorm_g, m_w_co, m_norm_mlp_g, m_w_mlp1, m_w_mlp2, v_norm_mix_g, v_w_in, v_fox_qnorm_g, v_fox_knorm_g, v_fox_f_bias, v_fox_onorm_g, v_gdn_conv_w, v_gdn_A_log, v_gdn_dt_bias, v_gdn_onorm_g, v_w_out, v_norm_xattn_g, v_mem_norm_g, v_w_cq, v_w_ckv, v_xattn_qnorm_g, v_xattn_knorm_g, v_w_co, v_norm_mlp_g, v_w_mlp1, v_w_mlp2):
    given = dict(locals())
    w = {k: given[k] for k in WEIGHTS}
    m = {k: given["m_" + k] for k in WEIGHTS}
    v = {k: given["v_" + k] for k in WEIGHTS}

    shards = [w[k][0] if k == "gdn_conv_w" else w[k][0].astype(BF16) for k in SHARDED]
    whole = {k: _whole(k, g) for k, g in zip(SHARDED, _all_gather_hbm(shards, "gather_weights"))}

    small = {k: w[k] for k in REPLICATED}
    loss_local, grad_x, grads = _local_step(x, mem, loss_target, **small, **whole)

    core = lax.axis_index("c").astype(jnp.int32).reshape(1)
    chip = (2 * lax.axis_index("x") + lax.axis_index("y")).astype(jnp.int32).reshape(1)
    own = [_blocks(k, grads[k]) for k in SHARDED]
    got = _pair_exchange(own, "grad_pair_exchange")
    sums = [_pair_sum(core, o, g, "grad_pair_sum_" + k) for k, o, g in zip(SHARDED, own, got)]
    parts = _chip_exchange(sums, "grad_chip_exchange")
    out_g, out_d, out_m, out_v = {}, {}, {}, {}
    for k, s, p in zip(SHARDED, sums, parts):
        res = _sum_adam(chip, s, p, w[k][0], m[k][0], v[k][0], "adam_" + k)
        out_g[k], out_d[k], out_m[k], out_v[k] = (r[None] for r in res)

    packed = _pack({k: grads[k] for k in REPLICATED}).at[LOSS_ROW, 0].set(loss_local)
    everyone = _all_gather_vmem(packed, "gather_small")
    res = _sum_adam(chip, None, everyone, _pack(small), _pack({k: m[k] for k in REPLICATED}),
                    _pack({k: v[k] for k in REPLICATED}, fill=1.0), "adam_small")
    for i, k in enumerate(REPLICATED):
        n = w[k].shape[1]
        out_g[k], out_d[k], out_m[k], out_v[k] = (r[i:i + 1, 0:n] for r in res)
    loss = res[0][LOSS_ROW, 0]

    return (loss, grad_x, *[out_g[k] for k in WEIGHTS], *[out_d[k] for k in WEIGHTS], *[out_m[k] for k in WEIGHTS],
            *[out_v[k] for k in WEIGHTS])
```

```python
import functools

import jax
import jax.numpy as jnp
import numpy as np
from jax import lax
from jax.experimental import pallas as pl
from jax.experimental.pallas import tpu as pltpu

F32 = jnp.float32
BF16 = jnp.bfloat16

D_MODEL = 1024
FOX_HEADS = 8
FOX_HEAD_DIM = 64
FOX_WIDTH = 512
GDN_HEADS = 4
GDN_HEAD_DIM = 128
GDN_WIDTH = 512
CONV_WIDTH = 4
GDN_CHUNK = 64
GDN_GROUP = 4
FOX_BLOCK = 512
XATTN_HEADS = 4
XATTN_HEAD_DIM = 128
XATTN_WIDTH = 512
D_FF = 4096
EPS = 1e-6
NEG_INF = -1e30
N_DEV = 8

ADAM_LR = 0.001
ADAM_B1 = 0.9
ADAM_B2 = 0.999
ADAM_EPS = 1e-08
ADAM_WD = 0.01
ADAM_STEP = 10

P_FOX = 0
P_GDN = 1536
P_Z = 3072
P_SMALL = 3584
P_DIM = 3712
SM_F = 0
SM_B = 8
SM_A = 12
SM_ROWS = 16

LANES = 128
VMEM_LIMIT = 56 * 1024 * 1024

NN = (((1,), (0,)), ((), ()))
NT = (((1,), (1,)), ((), ()))
TN = (((0,), (0,)), ((), ()))


def _dot(a, b, dims=NN):
    return lax.dot_general(a.astype(BF16), b.astype(BF16), dims, preferred_element_type=F32)


def _cparams(sem=None):
    kw = dict(vmem_limit_bytes=VMEM_LIMIT)
    if sem is not None:
        kw["dimension_semantics"] = sem
    return pltpu.CompilerParams(**kw)


def _sigmoid(x):
    return 0.5 * (jnp.tanh(0.5 * x) + 1.0)


def _softplus(x):
    return jnp.maximum(x, 0.0) + jnp.log1p(jnp.exp(-jnp.abs(x)))


def _log_sigmoid(x):
    return -_softplus(-x)


def _rms(x, g):
    r = lax.rsqrt(jnp.mean(x * x, axis=-1, keepdims=True) + EPS)
    return x * r * g


def _rms_bwd(x, g, dy):
    r = lax.rsqrt(jnp.mean(x * x, axis=-1, keepdims=True) + EPS)
    xh = x * r
    dg = jnp.sum(dy * xh, axis=0, keepdims=True)
    dyg = dy * g
    dx = r * (dyg - xh * jnp.mean(dyg * xh, axis=-1, keepdims=True))
    return dx, dg


def _pair_stat(t, m0):
    s0 = jnp.sum(jnp.where(m0, t, 0.0), axis=-1, keepdims=True)
    s1 = jnp.sum(jnp.where(m0, 0.0, t), axis=-1, keepdims=True)
    return jnp.where(m0, s0, s1)


def _rms_pair(x, g, m0):
    r = lax.rsqrt(_pair_stat(x * x, m0) * (1.0 / FOX_HEAD_DIM) + EPS)
    return x * r * g


def _rms_pair_bwd(x, g, dy, m0):
    r = lax.rsqrt(_pair_stat(x * x, m0) * (1.0 / FOX_HEAD_DIM) + EPS)
    xh = x * r
    dg = jnp.sum(dy * xh, axis=0, keepdims=True)
    dyg = dy * g
    dx = r * (dyg - xh * (_pair_stat(dyg * xh, m0) * (1.0 / FOX_HEAD_DIM)))
    return dx, dg


@jax.custom_vjp
def _mm_nn(a, b):
    return _dot(a, b, NN)


_mm_nn.defvjp(lambda a, b: (_dot(a, b, NN), (a, b)),
              lambda r, g: (_dot(g, r[1], NT), _dot(r[0], g, TN)))


@jax.custom_vjp
def _mm_nt(a, b):
    return _dot(a, b, NT)


_mm_nt.defvjp(lambda a, b: (_dot(a, b, NT), (a, b)),
              lambda r, g: (_dot(g, r[1], NN), _dot(g, r[0], TN)))


@jax.custom_vjp
def _mm_tn(a, b):
    return _dot(a, b, TN)


_mm_tn.defvjp(lambda a, b: (_dot(a, b, TN), (a, b)),
              lambda r, g: (_dot(r[1], g, NT), _dot(r[0], g, NN)))


def _dot3(a, b, dims):
    ah = a.astype(BF16)
    al = (a - ah.astype(F32)).astype(BF16)
    bh = b.astype(BF16)
    bl = (b - bh.astype(F32)).astype(BF16)
    d = functools.partial(lax.dot_general, dimension_numbers=dims, preferred_element_type=F32)
    return d(ah, bh) + d(ah, bl) + d(al, bh)


@jax.custom_vjp
def _mm3(a, b):
    return _dot3(a, b, NN)


_mm3.defvjp(lambda a, b: (_dot3(a, b, NN), (a, b)),
            lambda r, g: (_dot3(g, r[1], NT), _dot3(r[0], g, TN)))


def _unit_lower_inverses(mats):
    c = mats[0].shape[0]
    eye = (lax.broadcasted_iota(jnp.int32, (c, c), 0) == lax.broadcasted_iota(jnp.int32, (c, c), 1)).astype(F32)
    xs = [eye - a for a in mats]
    ps = list(mats)
    k = 2
    while k < c + 1:
        ps = [_mm3(p, p) for p in ps]
        xs = [x + _mm3(x, p) for x, p in zip(xs, ps)]
        k *= 2
    return xs


def _wgrad(a, b, name, bk=1024, bn=1024, bt=512):
    t_len, k_len = a.shape
    n_len = b.shape[1]
    bk, bn, bt = min(bk, k_len), min(bn, n_len), min(bt, t_len)
    nt = t_len // bt

    def body(a_ref, b_ref, o_ref, acc_ref):
        t = pl.program_id(2)

        @pl.when(t == 0)
        def _():
            acc_ref[...] = jnp.zeros_like(acc_ref)

        acc_ref[...] += _dot(a_ref[...], b_ref[...], TN)

        @pl.when(t == nt - 1)
        def _():
            o_ref[...] = acc_ref[...]

    return pl.pallas_call(
        body, name=name, grid=(k_len // bk, n_len // bn, nt),
        in_specs=[pl.BlockSpec((bt, bk), lambda i, j, t: (t, i)), pl.BlockSpec((bt, bn), lambda i, j, t: (t, j))],
        out_specs=pl.BlockSpec((bk, bn), lambda i, j, t: (i, j)),
        out_shape=jax.ShapeDtypeStruct((k_len, n_len), F32),
        scratch_shapes=[pltpu.VMEM((bk, bn), F32)],
        compiler_params=_cparams(("parallel", "parallel", "arbitrary")),
    )(a, b)


def _rows_matmul(a, b, name, bt=512):
    r_len, t_len = a.shape
    n_len = b.shape[1]
    bt = min(bt, t_len)
    nt = t_len // bt

    def body(a_ref, b_ref, o_ref):
        t = pl.program_id(0)

        @pl.when(t == 0)
        def _():
            o_ref[...] = jnp.zeros_like(o_ref)

        o_ref[...] += _dot(a_ref[...], b_ref[...], NN)

    return pl.pallas_call(
        body, name=name, grid=(nt,),
        in_specs=[pl.BlockSpec((r_len, bt), lambda t: (0, t)), pl.BlockSpec((bt, n_len), lambda t: (t, 0))],
        out_specs=pl.BlockSpec((r_len, n_len), lambda t: (0, 0)),
        out_shape=jax.ShapeDtypeStruct((r_len, n_len), F32),
        compiler_params=_cparams(("arbitrary",)),
    )(a, b)


def _in_proj(x, g, wp, wst, tm=256):
    t_len, d = x.shape
    tm = min(tm, t_len)

    def body(x_ref, g_ref, wp_ref, wst_ref, h_ref, fox_ref, gdn_ref, z_ref, sm_ref, smt_ref):
        h = _rms(x_ref[...], g_ref[...]).astype(BF16)
        h_ref[...] = h
        p = _dot(h, wp_ref[...], NN)
        fox_ref[...] = p[:, P_FOX:P_GDN]
        gdn_ref[...] = p[:, P_GDN:P_Z]
        z_ref[...] = p[:, P_Z:P_SMALL]
        sm_ref[...] = p[:, P_SMALL:P_DIM]
        smt_ref[...] = _dot(wst_ref[...], h, NT)

    row = lambda i: (i, 0)
    fixed = lambda i: (0, 0)
    return pl.pallas_call(
        body, name="in_proj", grid=(t_len // tm,),
        in_specs=[pl.BlockSpec((tm, d), row), pl.BlockSpec((1, d), fixed), pl.BlockSpec((d, P_DIM), fixed),
                  pl.BlockSpec((SM_ROWS, d), fixed)],
        out_specs=[pl.BlockSpec((tm, d), row), pl.BlockSpec((tm, 1536), row), pl.BlockSpec((tm, 1536), row),
                   pl.BlockSpec((tm, 512), row), pl.BlockSpec((tm, LANES), row), pl.BlockSpec((SM_ROWS, tm), lambda i: (0, i))],
        out_shape=[jax.ShapeDtypeStruct((t_len, d), BF16), jax.ShapeDtypeStruct((t_len, 1536), F32),
                   jax.ShapeDtypeStruct((t_len, 1536), F32), jax.ShapeDtypeStruct((t_len, 512), F32),
                   jax.ShapeDtypeStruct((t_len, LANES), F32), jax.ShapeDtypeStruct((SM_ROWS, t_len), F32)],
        compiler_params=_cparams(("parallel",)),
    )(x, g, wp, wst)


def _in_proj_bwd(dproj, dsmt, x, g, wp, wst, dx1, tm=256):
    t_len, d = x.shape
    tm = min(tm, t_len)

    def body(dp_ref, dst_ref, x_ref, g_ref, wp_ref, wst_ref, dx1_ref, dx_ref, dg_ref):
        i = pl.program_id(0)
        dh = _dot(dp_ref[...], wp_ref[...], NT) + _dot(dst_ref[...], wst_ref[...], TN)
        dxn, dg = _rms_bwd(x_ref[...], g_ref[...], dh)
        dx_ref[...] = dx1_ref[...] + dxn

        @pl.when(i == 0)
        def _():
            dg_ref[...] = jnp.zeros_like(dg_ref)

        dg_ref[...] += dg

    row = lambda i: (i, 0)
    fixed = lambda i: (0, 0)
    return pl.pallas_call(
        body, name="in_proj_bwd", grid=(t_len // tm,),
        in_specs=[pl.BlockSpec((tm, P_DIM), row), pl.BlockSpec((SM_ROWS, tm), lambda i: (0, i)), pl.BlockSpec((tm, d), row),
                  pl.BlockSpec((1, d), fixed), pl.BlockSpec((d, P_DIM), fixed), pl.BlockSpec((SM_ROWS, d), fixed),
                  pl.BlockSpec((tm, d), row)],
        out_specs=[pl.BlockSpec((tm, d), row), pl.BlockSpec((1, d), fixed)],
        out_shape=[jax.ShapeDtypeStruct((t_len, d), F32), jax.ShapeDtypeStruct((1, d), F32)],
        compiler_params=_cparams(("arbitrary",)),
    )(dproj, dsmt, x, g, wp, wst, dx1)


def _fox_cum(smt, bias_col, n_batch, s_len, ck=256):
    ck = min(ck, s_len)

    def body(s_ref, b_ref, c_ref):
        tri = (lax.broadcasted_iota(jnp.int32, (ck, ck), 0) <= lax.broadcasted_iota(jnp.int32, (ck, ck), 1)).astype(F32)
        carry = jnp.zeros((SM_ROWS, 1), F32)
        for r in range(s_len // ck):
            ls = _log_sigmoid(s_ref[:, r * ck:(r + 1) * ck] + b_ref[...])
            c = jnp.dot(ls, tri, precision=lax.Precision.HIGHEST, preferred_element_type=F32) + carry
            c_ref[:, r * ck:(r + 1) * ck] = c
            carry = c[:, ck - 1:ck]

    return pl.pallas_call(
        body, name="fox_cum", grid=(n_batch,),
        in_specs=[pl.BlockSpec((SM_ROWS, s_len), lambda b: (0, b)), pl.BlockSpec((SM_ROWS, 1), lambda b: (0, 0))],
        out_specs=pl.BlockSpec((SM_ROWS, s_len), lambda b: (0, b)),
        out_shape=jax.ShapeDtypeStruct(smt.shape, F32),
        compiler_params=_cparams(("parallel",)),
    )(smt, bias_col)


def _fox_cum_bwd(dc, smt, bias_col, n_batch, s_len, ck=256):
    ck = min(ck, s_len)
    nr = s_len // ck

    def body(dc_ref, s_ref, b_ref, dl_ref, db_ref):
        b = pl.program_id(0)
        tri = (lax.broadcasted_iota(jnp.int32, (ck, ck), 0) >= lax.broadcasted_iota(jnp.int32, (ck, ck), 1)).astype(F32)
        carry = jnp.zeros((SM_ROWS, 1), F32)
        tot = jnp.zeros((SM_ROWS, 1), F32)
        for r in reversed(range(nr)):
            sl = slice(r * ck, (r + 1) * ck)
            dls = jnp.dot(dc_ref[:, sl], tri, precision=lax.Precision.HIGHEST, preferred_element_type=F32) + carry
            carry = dls[:, 0:1]
            dl = dls * (1.0 - _sigmoid(s_ref[:, sl] + b_ref[...]))
            dl_ref[:, sl] = dl
            tot = tot + jnp.sum(dl, axis=1, keepdims=True)

        @pl.when(b == 0)
        def _():
            db_ref[...] = jnp.zeros_like(db_ref)

        db_ref[...] += jnp.broadcast_to(tot, db_ref.shape)

    return pl.pallas_call(
        body, name="fox_cum_bwd", grid=(n_batch,),
        in_specs=[pl.BlockSpec((SM_ROWS, s_len), lambda b: (0, b)), pl.BlockSpec((SM_ROWS, s_len), lambda b: (0, b)),
                  pl.BlockSpec((SM_ROWS, 1), lambda b: (0, 0))],
        out_specs=[pl.BlockSpec((SM_ROWS, s_len), lambda b: (0, b)), pl.BlockSpec((SM_ROWS, LANES), lambda b: (0, 0))],
        out_shape=[jax.ShapeDtypeStruct(smt.shape, F32), jax.ShapeDtypeStruct((SM_ROWS, LANES), F32)],
        compiler_params=_cparams(("arbitrary",)),
    )(dc, smt, bias_col)


def _fox_diagonal_mask(tq):
    return lax.broadcasted_iota(jnp.int32, (tq, tq), 1) <= lax.broadcasted_iota(jnp.int32, (tq, tq), 0)


def _fox_fwd(pf, cb, gq2, gk2, go2, tq=256):
    n_batch, s_len, _ = pf.shape
    tq = min(tq, s_len)
    nq = s_len // tq
    scale = FOX_HEAD_DIM ** -0.5

    def body(q_ref, k_ref, v_ref, c_ref, gq_ref, gk_ref, go_ref, o_ref, on_ref, lse_ref, kh_ref, vh_ref):
        j = pl.program_id(1)
        i = pl.program_id(2)
        m0 = lax.broadcasted_iota(jnp.int32, (1, LANES), 1) < FOX_HEAD_DIM

        @pl.when(i == 0)
        def _():
            kn = _rms_pair(k_ref[0], gk_ref[...], m0)
            kh_ref[0] = jnp.where(m0, kn, 0.0).astype(BF16)
            kh_ref[1] = jnp.where(m0, 0.0, kn).astype(BF16)
            v = v_ref[0]
            vh_ref[0] = jnp.where(m0, v, 0.0).astype(BF16)
            vh_ref[1] = jnp.where(m0, 0.0, v).astype(BF16)

        qb = (_rms_pair(q_ref[0], gq_ref[...], m0) * scale).astype(BF16)

        def step(kb, carry, diagonal=False):
            ms, ls, acc = carry
            off = pl.multiple_of(kb * tq, tq)
            new_m, new_l, alphas, pv = [], [], [], []
            for hh in range(2):
                s = _dot(qb, kh_ref[hh, pl.ds(off, tq), :], NT)
                s = s - c_ref[0, kb, pl.ds(2 * j + hh, 1), :]
                if diagonal:
                    s = jnp.where(_fox_diagonal_mask(tq), s, NEG_INF)
                m_new = jnp.maximum(ms[hh], jnp.max(s, axis=-1, keepdims=True))
                alpha = jnp.exp(ms[hh] - m_new)
                p = jnp.exp(s - m_new)
                new_l.append(alpha * ls[hh] + jnp.sum(p, axis=-1, keepdims=True))
                new_m.append(m_new)
                alphas.append(alpha)
                pv.append(_dot(p, vh_ref[hh, pl.ds(off, tq), :], NN))
            acc = jnp.where(m0, alphas[0], alphas[1]) * acc + pv[0] + pv[1]
            return tuple(new_m), tuple(new_l), acc

        init_m = (jnp.full((tq, 1), NEG_INF, F32),) * 2
        init_l = (jnp.zeros((tq, 1), F32),) * 2
        carry = lax.fori_loop(0, i, step, (init_m, init_l, jnp.zeros((tq, LANES), F32)))
        ms, ls, acc = step(i, carry, diagonal=True)
        o = acc / jnp.where(m0, ls[0], ls[1])
        o_ref[0] = o
        on_ref[0] = _rms_pair(o, go_ref[...], m0).astype(BF16)
        lse_ref[0] = jnp.where(m0, ms[0] + jnp.log(ls[0]), ms[1] + jnp.log(ls[1]))

    fixed = lambda b, j, i: (0, 0)
    tile = lambda b, j, i: (b, i, j)
    return pl.pallas_call(
        body, name="fox_fwd", grid=(n_batch, 4, nq),
        in_specs=[pl.BlockSpec((1, tq, LANES), tile), pl.BlockSpec((1, s_len, LANES), lambda b, j, i: (b, 0, 4 + j)),
                  pl.BlockSpec((1, s_len, LANES), lambda b, j, i: (b, 0, 8 + j)),
                  pl.BlockSpec((1, nq, SM_ROWS, tq), lambda b, j, i: (b, 0, 0, 0)),
                  pl.BlockSpec((1, LANES), fixed), pl.BlockSpec((1, LANES), fixed), pl.BlockSpec((1, LANES), fixed)],
        out_specs=[pl.BlockSpec((1, tq, LANES), tile), pl.BlockSpec((1, tq, LANES), tile), pl.BlockSpec((1, tq, LANES), tile)],
        out_shape=[jax.ShapeDtypeStruct((n_batch, s_len, FOX_WIDTH), F32), jax.ShapeDtypeStruct((n_batch, s_len, FOX_WIDTH), BF16),
                   jax.ShapeDtypeStruct((n_batch, s_len, FOX_WIDTH), F32)],
        scratch_shapes=[pltpu.VMEM((2, s_len, LANES), BF16), pltpu.VMEM((2, s_len, LANES), BF16)],
        compiler_params=_cparams(("parallel", "parallel", "arbitrary")),
    )(pf, pf, pf, cb, gq2, gk2, go2)


def _fox_bwd(pf, cb, gq2, gk2, go2, o, lse, don, tq=256):
    n_batch, s_len, _ = pf.shape
    tq = min(tq, s_len)
    nq = s_len // tq
    scale = FOX_HEAD_DIM ** -0.5

    def body(q_ref, k_ref, v_ref, c_ref, gq_ref, gk_ref, go_ref, o_ref, lse_ref, don_ref,
             dq_ref, dk_ref, dv_ref, dc_ref, dgq_ref, dgk_ref, dgo_ref, kh_ref, vh_ref, dka_ref, dva_ref, dca_ref):
        b = pl.program_id(0)
        j = pl.program_id(1)
        i = pl.program_id(2)
        m0 = lax.broadcasted_iota(jnp.int32, (1, LANES), 1) < FOX_HEAD_DIM

        @pl.when((b == 0) & (j == 0) & (i == 0))
        def _():
            dgq_ref[...] = jnp.zeros_like(dgq_ref)
            dgk_ref[...] = jnp.zeros_like(dgk_ref)
            dgo_ref[...] = jnp.zeros_like(dgo_ref)

        @pl.when(i == 0)
        def _():
            kn = _rms_pair(k_ref[0], gk_ref[...], m0)
            kh_ref[0] = jnp.where(m0, kn, 0.0).astype(BF16)
            kh_ref[1] = jnp.where(m0, 0.0, kn).astype(BF16)
            v = v_ref[0]
            vh_ref[0] = jnp.where(m0, v, 0.0).astype(BF16)
            vh_ref[1] = jnp.where(m0, 0.0, v).astype(BF16)
            dka_ref[...] = jnp.zeros_like(dka_ref)
            dva_ref[...] = jnp.zeros_like(dva_ref)
            dca_ref[...] = jnp.zeros_like(dca_ref)

        q = q_ref[0]
        qn = _rms_pair(q, gq_ref[...], m0)
        qs = qn * scale
        qb = qs.astype(BF16)
        qh = (jnp.where(m0, qs, 0.0).astype(BF16), jnp.where(m0, 0.0, qs).astype(BF16))
        ot = o_ref[0]
        do, dgo = _rms_pair_bwd(ot, go_ref[...], don_ref[0], m0)
        dgo_ref[...] += dgo
        dd = do * ot
        delta = (jnp.sum(jnp.where(m0, dd, 0.0), axis=-1, keepdims=True), jnp.sum(jnp.where(m0, 0.0, dd), axis=-1, keepdims=True))
        doh = (jnp.where(m0, do, 0.0).astype(BF16), jnp.where(m0, 0.0, do).astype(BF16))
        lse_t = lse_ref[0]
        lse_h = (lse_t[:, 0:1], lse_t[:, FOX_HEAD_DIM:FOX_HEAD_DIM + 1])

        def step(kb, carry, diagonal=False):
            dqn, rs = carry
            rs = list(rs)
            off = pl.multiple_of(kb * tq, tq)
            for hh in range(2):
                kblk = kh_ref[hh, pl.ds(off, tq), :]
                vblk = vh_ref[hh, pl.ds(off, tq), :]
                s = _dot(qb, kblk, NT)
                s = s - c_ref[0, kb, pl.ds(2 * j + hh, 1), :]
                if diagonal:
                    s = jnp.where(_fox_diagonal_mask(tq), s, NEG_INF)
                p = jnp.exp(s - lse_h[hh])
                dp = _dot(doh[hh], vblk, NT)
                ds = p * (dp - delta[hh])
                dva_ref[pl.ds(off, tq), :] += _dot(p, doh[hh], TN)
                dka_ref[pl.ds(off, tq), :] += _dot(ds, qh[hh], TN)
                dca_ref[kb, hh:hh + 1, :] += -jnp.sum(ds, axis=0, keepdims=True)
                rs[hh] = rs[hh] + jnp.sum(ds, axis=-1, keepdims=True)
                dqn = dqn + _dot(ds, kblk, NN)
            return dqn, tuple(rs)

        carry = lax.fori_loop(0, i, step, (jnp.zeros((tq, LANES), F32), (jnp.zeros((tq, 1), F32),) * 2))
        dqn, rs = step(i, carry, diagonal=True)
        dqn = dqn * scale
        rs_rows = jnp.where(m0, rs[0], rs[1]).T
        dca_ref[i, 0:1, :] += rs_rows[0:1, :]
        dca_ref[i, 1:2, :] += rs_rows[FOX_HEAD_DIM:FOX_HEAD_DIM + 1, :]
        dq, dgq = _rms_pair_bwd(q, gq_ref[...], dqn, m0)
        dq_ref[0] = dq.astype(BF16)
        dgq_ref[...] += dgq

        @pl.when(i == nq - 1)
        def _():
            dk, dgk = _rms_pair_bwd(k_ref[0], gk_ref[...], dka_ref[...], m0)
            dk_ref[0] = dk.astype(BF16)
            dgk_ref[...] += dgk
            dv_ref[0] = dva_ref[...].astype(BF16)
            dc_ref[0, 0] = dca_ref[...]

    fixed = lambda b, j, i: (0, 0)
    tile = lambda b, j, i: (b, i, j)
    full = lambda b, j, i: (b, 0, j)
    wide = jax.ShapeDtypeStruct((n_batch, s_len, FOX_WIDTH), BF16)
    gain = jax.ShapeDtypeStruct((1, LANES), F32)
    return pl.pallas_call(
        body, name="fox_bwd", grid=(n_batch, 4, nq),
        in_specs=[pl.BlockSpec((1, tq, LANES), tile), pl.BlockSpec((1, s_len, LANES), lambda b, j, i: (b, 0, 4 + j)),
                  pl.BlockSpec((1, s_len, LANES), lambda b, j, i: (b, 0, 8 + j)),
                  pl.BlockSpec((1, nq, SM_ROWS, tq), lambda b, j, i: (b, 0, 0, 0)),
                  pl.BlockSpec((1, LANES), fixed), pl.BlockSpec((1, LANES), fixed), pl.BlockSpec((1, LANES), fixed),
                  pl.BlockSpec((1, tq, LANES), tile), pl.BlockSpec((1, tq, LANES), tile), pl.BlockSpec((1, tq, LANES), tile)],
        out_specs=[pl.BlockSpec((1, tq, LANES), tile), pl.BlockSpec((1, s_len, LANES), full), pl.BlockSpec((1, s_len, LANES), full),
                   pl.BlockSpec((1, 1, nq, 8, tq), lambda b, j, i: (b, j, 0, 0, 0)),
                   pl.BlockSpec((1, LANES), fixed), pl.BlockSpec((1, LANES), fixed), pl.BlockSpec((1, LANES), fixed)],
        out_shape=[wide, wide, wide, jax.ShapeDtypeStruct((n_batch, 4, nq, 8, tq), F32), gain, gain, gain],
        scratch_shapes=[pltpu.VMEM((2, s_len, LANES), BF16), pltpu.VMEM((2, s_len, LANES), BF16),
                        pltpu.VMEM((s_len, LANES), F32), pltpu.VMEM((s_len, LANES), F32), pltpu.VMEM((nq, 8, tq), F32)],
        compiler_params=_cparams(("arbitrary", "arbitrary", "arbitrary")),
    )(pf, pf, pf, cb, gq2, gk2, go2, o, lse, don)


def _shift_down(x, k):
    row = lax.broadcasted_iota(jnp.int32, x.shape, 0)
    return jnp.where(row >= k, pltpu.roll(x, k, 0), 0.0)


def _shift_up(x, k):
    n = x.shape[0]
    row = lax.broadcasted_iota(jnp.int32, x.shape, 0)
    return jnp.where(row < n - k, pltpu.roll(x, n - k, 0), 0.0)


def _conv_silu(x, w):
    y = w[3:4] * x + w[2:3] * _shift_down(x, 1) + w[1:2] * _shift_down(x, 2) + w[0:1] * _shift_down(x, 3)
    return y, y * _sigmoid(y)


def _gdn_pre(pg, conv_w):
    n_batch, s_len, width = pg.shape
    ncb = width // LANES

    def body(x_ref, w_ref, o_ref):
        cb = pl.program_id(1)
        _, s = _conv_silu(x_ref[0], w_ref[...])
        sn = s * lax.rsqrt(jnp.sum(s * s, axis=-1, keepdims=True) + EPS)
        o_ref[0] = jnp.where(cb < 2 * GDN_HEADS, sn, s)

    return pl.pallas_call(
        body, name="gdn_pre", grid=(n_batch, ncb),
        in_specs=[pl.BlockSpec((1, s_len, LANES), lambda b, c: (b, 0, c)), pl.BlockSpec((8, LANES), lambda b, c: (0, c))],
        out_specs=pl.BlockSpec((1, s_len, LANES), lambda b, c: (b, 0, c)),
        out_shape=jax.ShapeDtypeStruct(pg.shape, F32),
        compiler_params=_cparams(("parallel", "parallel")),
    )(pg, conv_w)


def _gdn_pre_bwd(pg, conv_w, dout):
    n_batch, s_len, width = pg.shape
    ncb = width // LANES

    def body(x_ref, w_ref, d_ref, dx_ref, dw_ref):
        cb = pl.program_id(0)
        b = pl.program_id(1)
        x = x_ref[0]
        w = w_ref[...]
        d = d_ref[0]
        y, s = _conv_silu(x, w)
        rr = lax.rsqrt(jnp.sum(s * s, axis=-1, keepdims=True) + EPS)
        sn = s * rr
        ds_n = rr * (d - sn * jnp.sum(d * sn, axis=-1, keepdims=True))
        ds = jnp.where(cb < 2 * GDN_HEADS, ds_n, d)
        sig = _sigmoid(y)
        dy = ds * (sig * (1.0 + y * (1.0 - sig)))
        dx = w[3:4] * dy + w[2:3] * _shift_up(dy, 1) + w[1:2] * _shift_up(dy, 2) + w[0:1] * _shift_up(dy, 3)
        dx_ref[0] = dx.astype(BF16)
        dw = [jnp.sum(dy * _shift_down(x, 3 - jj), axis=0, keepdims=True) if jj < 3 else jnp.sum(dy * x, axis=0, keepdims=True)
              for jj in range(CONV_WIDTH)]
        rows = lax.broadcasted_iota(jnp.int32, (8, LANES), 0)
        dwb = jnp.zeros((8, LANES), F32)
        for jj in range(CONV_WIDTH):
            dwb = dwb + jnp.where(rows == jj, dw[jj], 0.0)

        @pl.when(b == 0)
        def _():
            dw_ref[...] = jnp.zeros_like(dw_ref)

        dw_ref[...] += dwb

    blk = lambda c, b: (b, 0, c)
    return pl.pallas_call(
        body, name="gdn_pre_bwd", grid=(ncb, n_batch),
        in_specs=[pl.BlockSpec((1, s_len, LANES), blk), pl.BlockSpec((8, LANES), lambda c, b: (0, c)), pl.BlockSpec((1, s_len, LANES), blk)],
        out_specs=[pl.BlockSpec((1, s_len, LANES), blk), pl.BlockSpec((8, LANES), lambda c, b: (0, c))],
        out_shape=[jax.ShapeDtypeStruct(pg.shape, BF16), jax.ShapeDtypeStruct((8, width), F32)],
        compiler_params=_cparams(("parallel", "arbitrary")),
    )(pg, conv_w, dout)


def _gdn_gates(smc, smr, a_c, dt_c, a_r, dt_r, h):
    lane = lax.broadcasted_iota(jnp.int32, (1, LANES), 1)
    sub = lax.broadcasted_iota(jnp.int32, (SM_ROWS, 1), 0)
    beta_c = jnp.sum(jnp.where(lane == SM_B + h, _sigmoid(smc), 0.0), axis=1, keepdims=True)
    g_all_c = -jnp.exp(a_c) * _softplus(smc + dt_c)
    g_c = jnp.sum(jnp.where(lane == SM_A + h, g_all_c, 0.0), axis=1, keepdims=True)
    g_all_r = -jnp.exp(a_r) * _softplus(smr + dt_r)
    g_r = jnp.sum(jnp.where(sub == SM_A + h, g_all_r, 0.0), axis=0, keepdims=True)
    return beta_c, g_c, g_r


def _gdn_group(qkv, z, smc, smr, a_c, dt_c, a_r, dt_r, go, states):
    n_grp = len(qkv)
    c = qkv[0].shape[0]
    hd = GDN_HEAD_DIM
    pairs = [(g, h) for g in range(n_grp) for h in range(GDN_HEADS)]
    ii = lax.broadcasted_iota(jnp.int32, (c, c), 0)
    jj = lax.broadcasted_iota(jnp.int32, (c, c), 1)
    incl = ii >= jj
    col = lambda arr, base, h: arr[:, base + h * hd:base + (h + 1) * hd]

    qs, ks, kbs, vbs, decays, gcs, g_lasts, amats = [], [], [], [], [], [], [], []
    for g, h in pairs:
        beta_c, g_c, g_r = _gdn_gates(smc[g], smr[g], a_c, dt_c, a_r, dt_r, h)
        gc_c = jnp.sum(jnp.where(incl, g_r, 0.0), axis=1, keepdims=True)
        gc_r = jnp.sum(jnp.where(ii <= jj, g_c, 0.0), axis=0, keepdims=True)
        decay = jnp.where(incl, jnp.exp(jnp.where(incl, gc_c - gc_r, 0.0)), 0.0)
        k = col(qkv[g], GDN_WIDTH, h)
        kb = k * beta_c
        qs.append(col(qkv[g], 0, h) * (hd ** -0.5))
        ks.append(k)
        kbs.append(kb)
        vbs.append(col(qkv[g], 2 * GDN_WIDTH, h) * beta_c)
        decays.append(decay)
        gcs.append(gc_c)
        g_lasts.append(jnp.sum(g_c, axis=0, keepdims=True))
        amats.append(jnp.where(ii > jj, _mm_nt(kb, k) * decay, 0.0))
    ts = _unit_lower_inverses(amats)
    egcs = [jnp.exp(gc) for gc in gcs]
    us = [_mm_nn(t, vb) for t, vb in zip(ts, vbs)]
    ws = [_mm_nn(t, kb * e) for t, kb, e in zip(ts, kbs, egcs)]
    intras = [_mm_nt(q, k) * d for q, k, d in zip(qs, ks, decays)]
    qes = [q * e for q, e in zip(qs, egcs)]
    kds = [k * jnp.exp(gl - gc) for k, gl, gc in zip(ks, g_lasts, gcs)]
    sdecs = [jnp.exp(gl) for gl in g_lasts]

    outs = []
    for g in range(n_grp):
        idx = [g * GDN_HEADS + h for h in range(GDN_HEADS)]
        v_new = [us[i] - _mm_nn(ws[i], states[h]) for h, i in enumerate(idx)]
        o_state = [_mm_nn(qes[i], states[h]) for h, i in enumerate(idx)]
        o_intra = [_mm_nn(intras[i], v_new[h]) for h, i in enumerate(idx)]
        states = [states[h] * sdecs[i] + _mm_tn(kds[i], v_new[h]) for h, i in enumerate(idx)]
        outs.append([_rms(o_state[h] + o_intra[h], go) * (col(z[g], 0, h) * _sigmoid(col(z[g], 0, h))) for h in range(GDN_HEADS)])
    return outs, states


def _gdn_group_size(n_chunks):
    return GDN_GROUP if n_chunks % GDN_GROUP == 0 else 1


def _gdn_fwd(qkvn, z, smc, smr, a_c, dt_c, a_r, dt_r, go):
    n_batch, s_len, _ = qkvn.shape
    c = GDN_CHUNK
    n = s_len // c
    grp = _gdn_group_size(n)
    ng = n // grp
    gc = grp * c
    hd = GDN_HEAD_DIM

    def body(qkv_ref, z_ref, smc_ref, smr_ref, ac_ref, dc_ref, ar_ref, dr_ref, go_ref, og_ref, st_ref, s_ref):
        @pl.when(pl.program_id(1) == 0)
        def _():
            s_ref[...] = jnp.zeros_like(s_ref)

        states = [s_ref[h] for h in range(GDN_HEADS)]
        for h in range(GDN_HEADS):
            st_ref[0, 0, h] = states[h]
        rows = lambda k: slice(k * c, (k + 1) * c)
        outs, nxt = _gdn_group([qkv_ref[0, rows(k), :] for k in range(grp)], [z_ref[0, rows(k), :] for k in range(grp)],
                               [smc_ref[0, rows(k), :] for k in range(grp)], [smr_ref[k] for k in range(grp)],
                               ac_ref[...], dc_ref[...], ar_ref[...], dr_ref[...], go_ref[...], states)
        for k in range(grp):
            for h in range(GDN_HEADS):
                og_ref[0, rows(k), h * hd:(h + 1) * hd] = outs[k][h].astype(BF16)
        for h in range(GDN_HEADS):
            s_ref[h] = nxt[h]

    tok = lambda b, i: (b, i, 0)
    fixed = lambda b, i: (0, 0)
    return pl.pallas_call(
        body, name="gdn_fwd", grid=(n_batch, ng),
        in_specs=[pl.BlockSpec((1, gc, 3 * GDN_WIDTH), tok), pl.BlockSpec((1, gc, GDN_WIDTH), tok), pl.BlockSpec((1, gc, LANES), tok),
                  pl.BlockSpec((grp, SM_ROWS, c), lambda b, i: (b * ng + i, 0, 0)),
                  pl.BlockSpec((1, LANES), fixed), pl.BlockSpec((1, LANES), fixed), pl.BlockSpec((SM_ROWS, 1), fixed),
                  pl.BlockSpec((SM_ROWS, 1), fixed), pl.BlockSpec((1, LANES), fixed)],
        out_specs=[pl.BlockSpec((1, gc, GDN_WIDTH), tok), pl.BlockSpec((1, 1, GDN_HEADS, hd, hd), lambda b, i: (b, i, 0, 0, 0))],
        out_shape=[jax.ShapeDtypeStruct((n_batch, s_len, GDN_WIDTH), BF16), jax.ShapeDtypeStruct((n_batch, ng, GDN_HEADS, hd, hd), F32)],
        scratch_shapes=[pltpu.VMEM((GDN_HEADS, hd, hd), F32)],
        compiler_params=_cparams(("parallel", "arbitrary")),
    )(qkvn, z, smc, smr, a_c, dt_c, a_r, dt_r, go)


def _gdn_bwd(qkvn, z, smc, smr, a_c, dt_c, a_r, dt_r, go, states, dog):
    n_batch, s_len, _ = qkvn.shape
    c = GDN_CHUNK
    n = s_len // c
    grp = _gdn_group_size(n)
    ng = n // grp
    gc = grp * c
    hd = GDN_HEAD_DIM

    def body(qkv_ref, z_ref, smc_ref, smr_ref, ac_ref, dc_ref, ar_ref, dr_ref, go_ref, st_ref, dog_ref,
             dqkv_ref, dz_ref, dsmc_ref, dsmr_ref, dac_ref, ddc_ref, dar_ref, ddr_ref, dgo_ref, ds_ref):
        first = (pl.program_id(0) == 0) & (pl.program_id(1) == 0)

        @pl.when(pl.program_id(1) == 0)
        def _():
            ds_ref[...] = jnp.zeros_like(ds_ref)

        @pl.when(first)
        def _():
            for r in (dac_ref, ddc_ref, dar_ref, ddr_ref, dgo_ref):
                r[...] = jnp.zeros_like(r)

        rows = lambda k: slice(k * c, (k + 1) * c)
        states = [st_ref[0, 0, h] for h in range(GDN_HEADS)]
        prim = ([qkv_ref[0, rows(k), :] for k in range(grp)], [z_ref[0, rows(k), :] for k in range(grp)],
                [smc_ref[0, rows(k), :] for k in range(grp)], [smr_ref[k] for k in range(grp)],
                ac_ref[...], dc_ref[...], ar_ref[...], dr_ref[...], go_ref[...], states)
        _, vjp = jax.vjp(_gdn_group, *prim)
        cot = ([[dog_ref[0, rows(k), h * hd:(h + 1) * hd] for h in range(GDN_HEADS)] for k in range(grp)],
               [ds_ref[h] for h in range(GDN_HEADS)])
        dqkv, dz, dsmc, dsmr, dac, ddc, dar, ddr, dgo, dstates = vjp(cot)
        for k in range(grp):
            dqkv_ref[0, rows(k), :] = dqkv[k]
            dz_ref[0, rows(k), :] = dz[k].astype(BF16)
            dsmc_ref[0, rows(k), :] = dsmc[k]
            dsmr_ref[k] = dsmr[k]
        dac_ref[...] += dac
        ddc_ref[...] += ddc
        dar_ref[...] += dar
        ddr_ref[...] += ddr
        dgo_ref[...] += dgo
        for h in range(GDN_HEADS):
            ds_ref[h] = dstates[h]

    tok = lambda b, i: (b, ng - 1 - i, 0)
    fixed = lambda b, i: (0, 0)
    lane_vec = jax.ShapeDtypeStruct((1, LANES), F32)
    row_vec = jax.ShapeDtypeStruct((SM_ROWS, 1), F32)
    return pl.pallas_call(
        body, name="gdn_bwd", grid=(n_batch, ng),
        in_specs=[pl.BlockSpec((1, gc, 3 * GDN_WIDTH), tok), pl.BlockSpec((1, gc, GDN_WIDTH), tok), pl.BlockSpec((1, gc, LANES), tok),
                  pl.BlockSpec((grp, SM_ROWS, c), lambda b, i: (b * ng + ng - 1 - i, 0, 0)),
                  pl.BlockSpec((1, LANES), fixed), pl.BlockSpec((1, LANES), fixed), pl.BlockSpec((SM_ROWS, 1), fixed),
                  pl.BlockSpec((SM_ROWS, 1), fixed), pl.BlockSpec((1, LANES), fixed),
                  pl.BlockSpec((1, 1, GDN_HEADS, hd, hd), lambda b, i: (b, ng - 1 - i, 0, 0, 0)),
                  pl.BlockSpec((1, gc, GDN_WIDTH), lambda b, i: (b, ng - 1 - i, 1))],
        out_specs=[pl.BlockSpec((1, gc, 3 * GDN_WIDTH), tok), pl.BlockSpec((1, gc, GDN_WIDTH), tok), pl.BlockSpec((1, gc, LANES), tok),
                   pl.BlockSpec((grp, SM_ROWS, c), lambda b, i: (b * ng + ng - 1 - i, 0, 0)),
                   pl.BlockSpec((1, LANES), fixed), pl.BlockSpec((1, LANES), fixed), pl.BlockSpec((SM_ROWS, 1), fixed),
                   pl.BlockSpec((SM_ROWS, 1), fixed), pl.BlockSpec((1, LANES), fixed)],
        out_shape=[jax.ShapeDtypeStruct((n_batch, s_len, 3 * GDN_WIDTH), F32), jax.ShapeDtypeStruct((n_batch, s_len, GDN_WIDTH), BF16),
                   jax.ShapeDtypeStruct((n_batch, s_len, LANES), F32), jax.ShapeDtypeStruct((n_batch * n, SM_ROWS, c), F32),
                   lane_vec, lane_vec, row_vec, row_vec, lane_vec],
        scratch_shapes=[pltpu.VMEM((GDN_HEADS, hd, hd), F32)],
        compiler_params=_cparams(("arbitrary", "arbitrary")),
    )(qkvn, z, smc, smr, a_c, dt_c, a_r, dt_r, go, states, dog)


def _out_proj(x, oa, ob, w_out, g_x, w_cq, tm=256):
    t_len, d = x.shape
    tm = min(tm, t_len)

    def body(x_ref, oa_ref, ob_ref, wo_ref, g_ref, wq_ref, x1_ref, hq_ref, cq_ref):
        x1 = x_ref[...] + _dot(oa_ref[...], wo_ref[0:FOX_WIDTH, :]) + _dot(ob_ref[...], wo_ref[FOX_WIDTH:2 * FOX_WIDTH, :])
        x1_ref[...] = x1
        hq = _rms(x1, g_ref[...]).astype(BF16)
        hq_ref[...] = hq
        cq_ref[...] = _dot(hq, wq_ref[...])

    row = lambda i: (i, 0)
    fixed = lambda i: (0, 0)
    return pl.pallas_call(
        body, name="out_proj", grid=(t_len // tm,),
        in_specs=[pl.BlockSpec((tm, d), row), pl.BlockSpec((tm, FOX_WIDTH), row), pl.BlockSpec((tm, GDN_WIDTH), row),
                  pl.BlockSpec((d, d), fixed), pl.BlockSpec((1, d), fixed), pl.BlockSpec((d, XATTN_WIDTH), fixed)],
        out_specs=[pl.BlockSpec((tm, d), row), pl.BlockSpec((tm, d), row), pl.BlockSpec((tm, XATTN_WIDTH), row)],
        out_shape=[jax.ShapeDtypeStruct((t_len, d), F32), jax.ShapeDtypeStruct((t_len, d), BF16), jax.ShapeDtypeStruct((t_len, XATTN_WIDTH), F32)],
        compiler_params=_cparams(("parallel",)),
    )(x, oa, ob, w_out, g_x, w_cq)


def _out_proj_bwd(dx1, w_out, tm=512):
    t_len, d = dx1.shape
    tm = min(tm, t_len)

    def body(dx_ref, w_ref, o_ref):
        o_ref[...] = _dot(dx_ref[...], w_ref[...], NT)

    return pl.pallas_call(
        body, name="out_proj_bwd", grid=(t_len // tm,),
        in_specs=[pl.BlockSpec((tm, d), lambda i: (i, 0)), pl.BlockSpec((d, d), lambda i: (0, 0))],
        out_specs=pl.BlockSpec((tm, d), lambda i: (i, 0)),
        out_shape=jax.ShapeDtypeStruct((t_len, d), F32),
        compiler_params=_cparams(("parallel",)),
    )(dx1, w_out)


def _mem_kv(mem, g, w_ckv, tm=256):
    t_len, d = mem.shape
    tm = min(tm, t_len)

    def body(x_ref, g_ref, w_ref, h_ref, o_ref):
        h = _rms(x_ref[...], g_ref[...]).astype(BF16)
        h_ref[...] = h
        o_ref[...] = _dot(h, w_ref[...])

    row = lambda i: (i, 0)
    fixed = lambda i: (0, 0)
    return pl.pallas_call(
        body, name="mem_kv", grid=(t_len // tm,),
        in_specs=[pl.BlockSpec((tm, d), row), pl.BlockSpec((1, d), fixed), pl.BlockSpec((d, 2 * XATTN_WIDTH), fixed)],
        out_specs=[pl.BlockSpec((tm, d), row), pl.BlockSpec((tm, 2 * XATTN_WIDTH), row)],
        out_shape=[jax.ShapeDtypeStruct((t_len, d), BF16), jax.ShapeDtypeStruct((t_len, 2 * XATTN_WIDTH), F32)],
        compiler_params=_cparams(("parallel",)),
    )(mem, g, w_ckv)


def _mem_kv_bwd(dckv, mem, g, w_ckv, tm=256):
    t_len, d = mem.shape
    tm = min(tm, t_len)

    def body(d_ref, x_ref, g_ref, w_ref, dg_ref):
        @pl.when(pl.program_id(0) == 0)
        def _():
            dg_ref[...] = jnp.zeros_like(dg_ref)

        dh = _dot(d_ref[...], w_ref[...], NT)
        _, dg = _rms_bwd(x_ref[...], g_ref[...], dh)
        dg_ref[...] += dg

    row = lambda i: (i, 0)
    fixed = lambda i: (0, 0)
    return pl.pallas_call(
        body, name="mem_kv_bwd", grid=(t_len // tm,),
        in_specs=[pl.BlockSpec((tm, 2 * XATTN_WIDTH), row), pl.BlockSpec((tm, d), row), pl.BlockSpec((1, d), fixed),
                  pl.BlockSpec((d, 2 * XATTN_WIDTH), fixed)],
        out_specs=pl.BlockSpec((1, d), fixed),
        out_shape=jax.ShapeDtypeStruct((1, d), F32),
        compiler_params=_cparams(("arbitrary",)),
    )(dckv, mem, g, w_ckv)


def _xattn_probs(qn, kn):
    s = _dot(qn, kn, NT) * (XATTN_HEAD_DIM ** -0.5)
    p = jnp.exp(s - jnp.max(s, axis=-1, keepdims=True))
    return p / jnp.sum(p, axis=-1, keepdims=True)


def _xattn_fwd(cq, ckv, x1, gq, gk, w_co, g_mlp, n_batch, s_len, m_len, tq=256):
    d = x1.shape[1]
    tq = min(tq, s_len)
    nq = s_len // tq
    hd = XATTN_HEAD_DIM

    def body(cq_ref, kv_ref, x1_ref, gq_ref, gk_ref, wo_ref, gm_ref, co_ref, x2_ref, hf_ref):
        outs = []
        for h in range(XATTN_HEADS):
            qn = _rms(cq_ref[:, h * hd:(h + 1) * hd], gq_ref[...])
            kn = _rms(kv_ref[:, h * hd:(h + 1) * hd], gk_ref[...])
            p = _xattn_probs(qn, kn)
            outs.append(_dot(p, kv_ref[:, XATTN_WIDTH + h * hd:XATTN_WIDTH + (h + 1) * hd]).astype(BF16))
        x2 = x1_ref[...]
        for h in range(XATTN_HEADS):
            co_ref[:, h * hd:(h + 1) * hd] = outs[h]
            x2 = x2 + _dot(outs[h], wo_ref[h * hd:(h + 1) * hd, :])
        x2_ref[...] = x2
        hf_ref[...] = _rms(x2, gm_ref[...]).astype(BF16)

    row = lambda b, i: (b * nq + i, 0)
    fixed = lambda b, i: (0, 0)
    t_len = n_batch * s_len
    return pl.pallas_call(
        body, name="xattn_fwd", grid=(n_batch, nq),
        in_specs=[pl.BlockSpec((tq, XATTN_WIDTH), row), pl.BlockSpec((m_len, 2 * XATTN_WIDTH), lambda b, i: (b, 0)),
                  pl.BlockSpec((tq, d), row), pl.BlockSpec((1, hd), fixed), pl.BlockSpec((1, hd), fixed),
                  pl.BlockSpec((XATTN_WIDTH, d), fixed), pl.BlockSpec((1, d), fixed)],
        out_specs=[pl.BlockSpec((tq, XATTN_WIDTH), row), pl.BlockSpec((tq, d), row), pl.BlockSpec((tq, d), row)],
        out_shape=[jax.ShapeDtypeStruct((t_len, XATTN_WIDTH), BF16), jax.ShapeDtypeStruct((t_len, d), F32),
                   jax.ShapeDtypeStruct((t_len, d), BF16)],
        compiler_params=_cparams(("parallel", "parallel")),
    )(cq, ckv, x1, gq, gk, w_co, g_mlp)


def _xattn_bwd(dx2, cq, ckv, x1, gq, gk, w_co, g_x, w_cq, n_batch, s_len, m_len, tq=256):
    d = x1.shape[1]
    tq = min(tq, s_len)
    nq = s_len // tq
    hd = XATTN_HEAD_DIM
    scale = XATTN_HEAD_DIM ** -0.5

    def body(dx2_ref, cq_ref, kv_ref, x1_ref, gq_ref, gk_ref, wo_ref, gx_ref, wq_ref,
             dx1_ref, dcq_ref, dkv_ref, dgq_ref, dgk_ref, dgx_ref, dk_acc, dv_acc):
        b = pl.program_id(0)
        i = pl.program_id(1)

        @pl.when((b == 0) & (i == 0))
        def _():
            dgq_ref[...] = jnp.zeros_like(dgq_ref)
            dgk_ref[...] = jnp.zeros_like(dgk_ref)
            dgx_ref[...] = jnp.zeros_like(dgx_ref)

        @pl.when(i == 0)
        def _():
            dk_acc[...] = jnp.zeros_like(dk_acc)
            dv_acc[...] = jnp.zeros_like(dv_acc)

        dx2 = dx2_ref[...]
        dhq = jnp.zeros((tq, d), F32)
        for h in range(XATTN_HEADS):
            sl = slice(h * hd, (h + 1) * hd)
            q = cq_ref[:, sl]
            qn = _rms(q, gq_ref[...])
            kn = _rms(kv_ref[:, sl], gk_ref[...])
            v = kv_ref[:, XATTN_WIDTH + h * hd:XATTN_WIDTH + (h + 1) * hd]
            p = _xattn_probs(qn, kn)
            dco = _dot(dx2, wo_ref[sl, :], NT)
            dv_acc[:, sl] += _dot(p, dco, TN)
            dp = _dot(dco, v, NT)
            ds = p * (dp - jnp.sum(dp * p, axis=-1, keepdims=True))
            dqn = _dot(ds, kn) * scale
            dk_acc[:, sl] += _dot(ds, qn, TN) * scale
            dq, dgq = _rms_bwd(q, gq_ref[...], dqn)
            dgq_ref[...] += dgq
            dqb = dq.astype(BF16)
            dcq_ref[:, sl] = dqb
            dhq = dhq + _dot(dqb, wq_ref[:, sl], NT)
        dxn, dgx = _rms_bwd(x1_ref[...], gx_ref[...], dhq)
        dgx_ref[...] += dgx
        dx1_ref[...] = dx2 + dxn

        @pl.when(i == nq - 1)
        def _():
            for h in range(XATTN_HEADS):
                sl = slice(h * hd, (h + 1) * hd)
                dk, dgk = _rms_bwd(kv_ref[:, sl], gk_ref[...], dk_acc[:, sl])
                dgk_ref[...] += dgk
                dkv_ref[:, sl] = dk.astype(BF16)
                dkv_ref[:, XATTN_WIDTH + h * hd:XATTN_WIDTH + (h + 1) * hd] = dv_acc[:, sl].astype(BF16)

    row = lambda b, i: (b * nq + i, 0)
    fixed = lambda b, i: (0, 0)
    t_len = n_batch * s_len
    return pl.pallas_call(
        body, name="xattn_bwd", grid=(n_batch, nq),
        in_specs=[pl.BlockSpec((tq, d), row), pl.BlockSpec((tq, XATTN_WIDTH), row), pl.BlockSpec((m_len, 2 * XATTN_WIDTH), lambda b, i: (b, 0)),
                  pl.BlockSpec((tq, d), row), pl.BlockSpec((1, hd), fixed), pl.BlockSpec((1, hd), fixed),
                  pl.BlockSpec((XATTN_WIDTH, d), fixed), pl.BlockSpec((1, d), fixed), pl.BlockSpec((d, XATTN_WIDTH), fixed)],
        out_specs=[pl.BlockSpec((tq, d), row), pl.BlockSpec((tq, XATTN_WIDTH), row), pl.BlockSpec((m_len, 2 * XATTN_WIDTH), lambda b, i: (b, 0)),
                   pl.BlockSpec((1, hd), fixed), pl.BlockSpec((1, hd), fixed), pl.BlockSpec((1, d), fixed)],
        out_shape=[jax.ShapeDtypeStruct((t_len, d), F32), jax.ShapeDtypeStruct((t_len, XATTN_WIDTH), BF16),
                   jax.ShapeDtypeStruct((n_batch * m_len, 2 * XATTN_WIDTH), BF16),
                   jax.ShapeDtypeStruct((1, hd), F32), jax.ShapeDtypeStruct((1, hd), F32), jax.ShapeDtypeStruct((1, d), F32)],
        scratch_shapes=[pltpu.VMEM((m_len, XATTN_WIDTH), F32), pltpu.VMEM((m_len, XATTN_WIDTH), F32)],
        compiler_params=_cparams(("arbitrary", "arbitrary")),
    )(dx2, cq, ckv, x1, gq, gk, w_co, g_x, w_cq)


def _resident(shape):
    return pl.BlockSpec(shape, lambda *_: (0,) * len(shape), pipeline_mode=pl.Buffered(1))


def _mlp_fwd(hf, x2, target, w1, w2, tm=256, tf=1024):
    t_len, d = x2.shape
    f = w1.shape[1]
    tm, tf = min(tm, t_len), min(tf, f)

    def body(hf_ref, x2_ref, tg_ref, w1_ref, w2_ref, u_ref, a_ref, dy_ref, ls_ref):
        hf_t = hf_ref[...]
        y = x2_ref[...]
        for k in range(f // tf):
            cols = slice(k * tf, (k + 1) * tf)
            u = _dot(hf_t, w1_ref[:, cols])
            u_ref[:, cols] = u
            r = jnp.maximum(u, 0.0)
            a = (r * r).astype(BF16)
            a_ref[:, cols] = a
            y = y + _dot(a, w2_ref[cols, :])
        err = y - tg_ref[...]
        dy_ref[...] = err * (1.0 / d)
        ls_ref[...] = jnp.broadcast_to(jnp.sum(jnp.sum(err * err, axis=-1, keepdims=True) * (1.0 / d), axis=0, keepdims=True), ls_ref.shape)

    row = lambda i: (i, 0)
    return pl.pallas_call(
        body, name="mlp_fwd", grid=(t_len // tm,),
        in_specs=[pl.BlockSpec((tm, d), row), pl.BlockSpec((tm, d), row), pl.BlockSpec((tm, d), row), _resident((d, f)), _resident((f, d))],
        out_specs=[pl.BlockSpec((tm, f), row), pl.BlockSpec((tm, f), row), pl.BlockSpec((tm, d), row),
                   pl.BlockSpec((1, 8, LANES), lambda i: (i, 0, 0))],
        out_shape=[jax.ShapeDtypeStruct((t_len, f), F32), jax.ShapeDtypeStruct((t_len, f), BF16), jax.ShapeDtypeStruct((t_len, d), F32),
                   jax.ShapeDtypeStruct((t_len // tm, 8, LANES), F32)],
        compiler_params=_cparams(("parallel",)),
    )(hf, x2, target, w1, w2)


def _mlp_bwd(dy, u, x2, g, w1, w2, tm=256, tf=1024):
    t_len, d = x2.shape
    f = w1.shape[1]
    tm, tf = min(tm, t_len), min(tf, f)

    def body(dy_ref, u_ref, x2_ref, g_ref, w1_ref, w2_ref, du_ref, dx2_ref, dg_ref):
        @pl.when(pl.program_id(0) == 0)
        def _():
            dg_ref[...] = jnp.zeros_like(dg_ref)

        dy_t = dy_ref[...]
        dyb = dy_t.astype(BF16)
        dhf = jnp.zeros((tm, d), F32)
        for k in range(f // tf):
            cols = slice(k * tf, (k + 1) * tf)
            da = _dot(dyb, w2_ref[cols, :], NT)
            du = (da * (2.0 * jnp.maximum(u_ref[:, cols], 0.0))).astype(BF16)
            du_ref[:, cols] = du
            dhf = dhf + _dot(du, w1_ref[:, cols], NT)
        dxn, dg = _rms_bwd(x2_ref[...], g_ref[...], dhf)
        dx2_ref[...] = dy_t + dxn
        dg_ref[...] += dg

    row = lambda i: (i, 0)
    fixed = lambda i: (0, 0)
    return pl.pallas_call(
        body, name="mlp_bwd", grid=(t_len // tm,),
        in_specs=[pl.BlockSpec((tm, d), row), pl.BlockSpec((tm, f), row), pl.BlockSpec((tm, d), row), pl.BlockSpec((1, d), fixed),
                  _resident((d, f)), _resident((f, d))],
        out_specs=[pl.BlockSpec((tm, f), row), pl.BlockSpec((tm, d), row), pl.BlockSpec((1, d), fixed)],
        out_shape=[jax.ShapeDtypeStruct((t_len, f), BF16), jax.ShapeDtypeStruct((t_len, d), F32), jax.ShapeDtypeStruct((1, d), F32)],
        compiler_params=_cparams(("arbitrary",)),
    )(dy, u, x2, g, w1, w2)


def _pad_lanes(v, offset=0, width=LANES):
    return jnp.zeros((1, width), F32).at[:, offset:offset + v.shape[1]].set(v)


def _col(v, offset=0, rows=SM_ROWS):
    return jnp.zeros((rows, 1), F32).at[offset:offset + v.shape[1], 0].set(v[0])


def _local_step(x, mem, target, norm_mix_g, w_in, fox_qnorm_g, fox_knorm_g, fox_f_bias, fox_onorm_g, gdn_conv_w, gdn_A_log,
                gdn_dt_bias, gdn_onorm_g, w_out, norm_xattn_g, mem_norm_g, w_cq, w_ckv, xattn_qnorm_g, xattn_knorm_g, w_co,
                norm_mlp_g, w_mlp1, w_mlp2):
    n_batch, s_len, d = x.shape
    m_len = mem.shape[1]
    t_len = n_batch * s_len
    tq = min(FOX_BLOCK, s_len)
    nq = s_len // tq
    n_chunks = s_len // GDN_CHUNK
    x2d = x.reshape(t_len, d)

    wp = jnp.concatenate([w_in[:, 0:1536], w_in[:, 1544:3080], w_in[:, 3088:3600], w_in[:, 1536:1544], w_in[:, 3080:3088],
                          jnp.zeros((d, P_DIM - 3600), BF16)], axis=1)
    wst = jnp.concatenate([w_in[:, 1536:1544], w_in[:, 3080:3088]], axis=1).T
    conv_w = jnp.concatenate([gdn_conv_w, jnp.zeros((8 - CONV_WIDTH, gdn_conv_w.shape[1]), F32)], axis=0)
    bias_col = _col(fox_f_bias, SM_F)
    gq2, gk2, go2 = (jnp.tile(g, (1, 2)) for g in (fox_qnorm_g, fox_knorm_g, fox_onorm_g))
    a_c, dt_c = _pad_lanes(gdn_A_log, SM_A), _pad_lanes(gdn_dt_bias, SM_A)
    a_r, dt_r = _col(gdn_A_log, SM_A), _col(gdn_dt_bias, SM_A)

    h1, pfox, pgdn, pz, sm, smt = _in_proj(x2d, norm_mix_g, wp, wst)
    c_rows = _fox_cum(smt, bias_col, n_batch, s_len)
    cb = c_rows.reshape(SM_ROWS, n_batch, nq, tq).transpose(1, 2, 0, 3)
    pf3 = pfox.reshape(n_batch, s_len, 1536)
    o_fox, oa, lse = _fox_fwd(pf3, cb, gq2, gk2, go2, tq)
    pg3 = pgdn.reshape(n_batch, s_len, 1536)
    qkvn = _gdn_pre(pg3, conv_w)
    z3 = pz.reshape(n_batch, s_len, GDN_WIDTH)
    smc = sm.reshape(n_batch, s_len, LANES)
    smr = smt.reshape(SM_ROWS, n_batch * n_chunks, GDN_CHUNK).transpose(1, 0, 2)
    ob, states = _gdn_fwd(qkvn, z3, smc, smr, a_c, dt_c, a_r, dt_r, gdn_onorm_g)
    oa2, ob2 = oa.reshape(t_len, FOX_WIDTH), ob.reshape(t_len, GDN_WIDTH)
    x1, hq, cq = _out_proj(x2d, oa2, ob2, w_out, norm_xattn_g, w_cq)
    mem2d = mem.reshape(n_batch * m_len, d)
    hm, ckv = _mem_kv(mem2d, mem_norm_g, w_ckv)
    co, x2, hf = _xattn_fwd(cq, ckv, x1, xattn_qnorm_g, xattn_knorm_g, w_co, norm_mlp_g, n_batch, s_len, m_len)
    u, a_act, dy, loss_tiles = _mlp_fwd(hf, x2, target.reshape(t_len, d), w_mlp1, w_mlp2)
    loss = 0.5 * jnp.sum(loss_tiles[:, 0, 0])

    grads = {}
    du, dx2, grads["norm_mlp_g"] = _mlp_bwd(dy, u, x2, norm_mlp_g, w_mlp1, w_mlp2)
    grads["w_mlp2"] = _wgrad(a_act, dy, "wgrad_mlp2")
    grads["w_mlp1"] = _wgrad(hf, du, "wgrad_mlp1")
    grads["w_co"] = _wgrad(co, dx2, "wgrad_co")
    dx1, dcq, dckv, grads["xattn_qnorm_g"], grads["xattn_knorm_g"], grads["norm_xattn_g"] = _xattn_bwd(
        dx2, cq, ckv, x1, xattn_qnorm_g, xattn_knorm_g, w_co, norm_xattn_g, w_cq, n_batch, s_len, m_len)
    grads["w_cq"] = _wgrad(hq, dcq, "wgrad_cq")
    grads["w_ckv"] = _wgrad(hm, dckv, "wgrad_ckv")
    grads["mem_norm_g"] = _mem_kv_bwd(dckv, mem2d, mem_norm_g, w_ckv)
    grads["w_out"] = _wgrad(jnp.concatenate([oa2, ob2], axis=1), dx1, "wgrad_out")
    dcat = _out_proj_bwd(dx1, w_out)
    dcat3 = dcat.reshape(n_batch, s_len, d)

    dqkvn, dz, dsmc, dsmr, dac, ddc, dar, ddr, grads["gdn_onorm_g"] = _gdn_bwd(
        qkvn, z3, smc, smr, a_c, dt_c, a_r, dt_r, gdn_onorm_g, states, dcat3)
    grads["gdn_A_log"] = dac[:, SM_A:SM_A + GDN_HEADS] + dar[SM_A:SM_A + GDN_HEADS, 0][None, :]
    grads["gdn_dt_bias"] = ddc[:, SM_A:SM_A + GDN_HEADS] + ddr[SM_A:SM_A + GDN_HEADS, 0][None, :]
    dpg, dconv = _gdn_pre_bwd(pg3, conv_w, dqkvn)
    grads["gdn_conv_w"] = dconv[0:CONV_WIDTH]

    dq, dk, dv, dcb, dgq, dgk, dgo = _fox_bwd(pf3, cb, gq2, gk2, go2, o_fox, lse, dcat3[:, :, 0:FOX_WIDTH], tq)
    fold = lambda g: g[:, 0:FOX_HEAD_DIM] + g[:, FOX_HEAD_DIM:LANES]
    grads["fox_qnorm_g"], grads["fox_knorm_g"], grads["fox_onorm_g"] = fold(dgq), fold(dgk), fold(dgo)
    dc8 = dcb[:, :, :, 0:2, :].transpose(1, 3, 0, 2, 4).reshape(FOX_HEADS, t_len)
    dc_rows = jnp.concatenate([dc8, jnp.zeros((SM_ROWS - FOX_HEADS, t_len), F32)], axis=0)
    dl_rows, dbias = _fox_cum_bwd(dc_rows, smt, bias_col, n_batch, s_len)
    grads["fox_f_bias"] = dbias[SM_F:SM_F + FOX_HEADS, 0][None, :]
    dsm_rows = jnp.concatenate([dl_rows[0:SM_B], dsmr.transpose(1, 0, 2).reshape(SM_ROWS, t_len)[SM_B:SM_ROWS]], axis=0)

    dproj = jnp.concatenate([dq.reshape(t_len, FOX_WIDTH), dk.reshape(t_len, FOX_WIDTH), dv.reshape(t_len, FOX_WIDTH),
                             dpg.reshape(t_len, 1536), dz.reshape(t_len, GDN_WIDTH), dsmc.reshape(t_len, LANES).astype(BF16)], axis=1)
    grad_x, grads["norm_mix_g"] = _in_proj_bwd(dproj, dsm_rows, x2d, norm_mix_g, wp, wst, dx1)
    dwp = _wgrad(h1, dproj, "wgrad_in", bk=512, bn=P_DIM)
    dwst = _rows_matmul(dsm_rows, h1, "wgrad_in_rows")
    dw_small = dwp[:, P_SMALL:P_SMALL + SM_ROWS] + dwst.T
    grads["w_in"] = jnp.concatenate([dwp[:, 0:1536], dw_small[:, 0:8], dwp[:, 1536:3072], dw_small[:, 8:16], dwp[:, 3072:3584]], axis=1)
    return loss, grad_x.reshape(n_batch, s_len, d), grads


MESH_ID = pl.DeviceIdType.MESH
ANY_SPEC = pl.BlockSpec(memory_space=pl.ANY)


def _place():
    x, y, c = lax.axis_index("x"), lax.axis_index("y"), lax.axis_index("c")
    return x, y, c, [(1 - x, y), (x, 1 - y), (1 - x, 1 - y)]


def _all_gather_body(n, ins, outs, send_sems, recv_sems, local_sems):
    x, y, c, chips = _place()
    me, sibling = (x, y, c), (x, y, 1 - c)

    def copy(a, k, block, to, src=None):
        dst = outs[a].at[4 * block[0] + 2 * block[1] + block[2]]
        return pltpu.make_async_remote_copy(src_ref=dst if src is None else src, dst_ref=dst, send_sem=send_sems.at[a, k],
                                            recv_sem=recv_sems.at[a, k], device_id=to, device_id_type=MESH_ID)

    mine = [pltpu.make_async_copy(ins[a], outs[a].at[4 * x + 2 * y + c], local_sems.at[a]) for a in range(n)]
    for cp in mine:
        cp.start()
    first = []
    for a in range(n):
        first.append(copy(a, 0, me, sibling, src=ins[a]))
        first += [copy(a, 1 + j, me, (*chip, c), src=ins[a]) for j, chip in enumerate(chips)]
    for cp in first:
        cp.start()
    passed = []
    for j, chip in enumerate(chips):
        for a in range(n):
            copy(a, 1 + j, (*chip, c), me).wait_recv()
            fwd = copy(a, 4 + j, (*chip, c), sibling)
            fwd.start()
            passed.append(fwd)
    for a in range(n):
        copy(a, 0, sibling, me).wait_recv()
        for j, chip in enumerate(chips):
            copy(a, 4 + j, (*chip, 1 - c), me).wait_recv()
    for cp in first + passed:
        cp.wait_send()
    for cp in mine:
        cp.wait()


def _all_gather_hbm(arrs, name):
    n = len(arrs)

    def body(*refs):
        _all_gather_body(n, refs[:n], refs[n:2 * n], *refs[2 * n:])

    return pl.pallas_call(
        body, name=name, in_specs=[ANY_SPEC] * n, out_specs=[ANY_SPEC] * n,
        out_shape=[jax.ShapeDtypeStruct((N_DEV,) + a.shape, a.dtype) for a in arrs],
        scratch_shapes=[pltpu.SemaphoreType.DMA((n, 7)), pltpu.SemaphoreType.DMA((n, 7)), pltpu.SemaphoreType.DMA((n,))],
    )(*arrs)


def _pair_exchange(arrs, name):
    n = len(arrs)

    def body(*refs):
        ins, outs = refs[:n], refs[n:2 * n]
        send_sems, recv_sems = refs[2 * n:]
        x, y, c, _ = _place()
        copies = []
        for a in range(n):
            for chip in range(4):
                copies.append(pltpu.make_async_remote_copy(
                    src_ref=ins[a].at[2 * chip + (1 - c)], dst_ref=outs[a].at[chip], send_sem=send_sems.at[a, chip],
                    recv_sem=recv_sems.at[a, chip], device_id=(x, y, 1 - c), device_id_type=MESH_ID))
        for cp in copies:
            cp.start()
        for cp in copies:
            cp.wait()

    return pl.pallas_call(
        body, name=name, in_specs=[ANY_SPEC] * n, out_specs=[ANY_SPEC] * n,
        out_shape=[jax.ShapeDtypeStruct((4,) + a.shape[1:], a.dtype) for a in arrs],
        scratch_shapes=[pltpu.SemaphoreType.DMA((n, 4)), pltpu.SemaphoreType.DMA((n, 4))],
    )(*arrs)


def _chip_exchange(arrs, name):
    n = len(arrs)

    def body(*refs):
        ins, outs = refs[:n], refs[n:2 * n]
        send_sems, recv_sems = refs[2 * n:]
        x, y, c, chips = _place()
        copies = []
        for a in range(n):
            for j, chip in enumerate(chips):
                copies.append(pltpu.make_async_remote_copy(
                    src_ref=ins[a].at[2 * chip[0] + chip[1]], dst_ref=outs[a].at[j], send_sem=send_sems.at[a, j],
                    recv_sem=recv_sems.at[a, j], device_id=(*chip, c), device_id_type=MESH_ID))
        for cp in copies:
            cp.start()
        for cp in copies:
            cp.wait()

    return pl.pallas_call(
        body, name=name, in_specs=[ANY_SPEC] * n, out_specs=[ANY_SPEC] * n,
        out_shape=[jax.ShapeDtypeStruct((3,) + a.shape[1:], a.dtype) for a in arrs],
        scratch_shapes=[pltpu.SemaphoreType.DMA((n, 3)), pltpu.SemaphoreType.DMA((n, 3))],
    )(*arrs)


def _all_gather_vmem(block, name):
    def body(in_ref, out_ref, send_sems, recv_sems, local_sems):
        _all_gather_body(1, [in_ref], [out_ref], send_sems, recv_sems, local_sems)

    vmem = pl.BlockSpec(memory_space=pltpu.VMEM)
    return pl.pallas_call(
        body, name=name, in_specs=[vmem], out_specs=vmem,
        out_shape=jax.ShapeDtypeStruct((N_DEV,) + block.shape, block.dtype),
        scratch_shapes=[pltpu.SemaphoreType.DMA((1, 7)), pltpu.SemaphoreType.DMA((1, 7)), pltpu.SemaphoreType.DMA((1,))],
    )(block)


def _row_tile(rows, cols):
    if rows <= 256:
        return rows
    return 256 if cols <= 512 else 128


def _pair_sum(core, own, got, name):
    _, rows, cols = own.shape
    tr = _row_tile(rows, cols)

    def body(c_ref, own_ref, got_ref, o_ref):
        o_ref[0] = own_ref[0] + got_ref[0]

    return pl.pallas_call(
        body, name=name,
        grid_spec=pltpu.PrefetchScalarGridSpec(
            num_scalar_prefetch=1, grid=(4, rows // tr),
            in_specs=[pl.BlockSpec((1, tr, cols), lambda k, i, c: (2 * k + c[0], i, 0)),
                      pl.BlockSpec((1, tr, cols), lambda k, i, c: (k, i, 0))],
            out_specs=pl.BlockSpec((1, tr, cols), lambda k, i, c: (k, i, 0))),
        out_shape=jax.ShapeDtypeStruct((4, rows, cols), F32),
        compiler_params=_cparams(("parallel", "parallel")),
    )(core, own, got)


def _adamw(w, g, m, v):
    m_new = ADAM_B1 * m + (1.0 - ADAM_B1) * g
    v_new = ADAM_B2 * v + (1.0 - ADAM_B2) * (g * g)
    m_hat = m_new / (1.0 - ADAM_B1 ** ADAM_STEP)
    v_hat = v_new / (1.0 - ADAM_B2 ** ADAM_STEP)
    delta = -ADAM_LR * (m_hat / (jnp.sqrt(v_hat) + ADAM_EPS) + ADAM_WD * w)
    return delta, m_new, v_new


def _sum_adam(chip, sums, parts, w, m, v, name):
    n_parts, rows, cols = parts.shape
    tr = _row_tile(rows, cols)

    def body(chip_ref, *refs):
        if sums is not None:
            g = refs[0][0].astype(F32)
            refs = refs[1:]
        p_ref, w_ref, m_ref, v_ref, g_ref, d_ref, mo_ref, vo_ref = refs
        for k in range(n_parts):
            g = p_ref[k].astype(F32) if (k == 0 and sums is None) else g + p_ref[k].astype(F32)
        g_ref[...] = g
        d_ref[...], mo_ref[...], vo_ref[...] = _adamw(w_ref[...], g, m_ref[...], v_ref[...])

    tile = pl.BlockSpec((tr, cols), lambda i, ch: (i, 0))
    out = jax.ShapeDtypeStruct((rows, cols), F32)
    own = [] if sums is None else [pl.BlockSpec((1, tr, cols), lambda i, ch: (ch[0], i, 0))]
    return pl.pallas_call(
        body, name=name,
        grid_spec=pltpu.PrefetchScalarGridSpec(
            num_scalar_prefetch=1, grid=(rows // tr,),
            in_specs=own + [pl.BlockSpec((n_parts, tr, cols), lambda i, ch: (0, i, 0)), tile, tile, tile],
            out_specs=[tile, tile, tile, tile]),
        out_shape=[out, out, out, out],
        compiler_params=_cparams(("parallel",)),
    )(chip, *([] if sums is None else [sums]), parts, w, m, v)


SHARDED = ("w_in", "gdn_conv_w", "w_out", "w_cq", "w_ckv", "w_co", "w_mlp1", "w_mlp2")
COLUMN_SHARDED = ("w_in", "gdn_conv_w", "w_co", "w_mlp1")
REPLICATED = ("norm_mix_g", "fox_qnorm_g", "fox_knorm_g", "fox_f_bias", "fox_onorm_g", "gdn_A_log", "gdn_dt_bias", "gdn_onorm_g",
              "norm_xattn_g", "mem_norm_g", "xattn_qnorm_g", "xattn_knorm_g", "norm_mlp_g")
WEIGHTS = ("norm_mix_g", "w_in", "fox_qnorm_g", "fox_knorm_g", "fox_f_bias", "fox_onorm_g", "gdn_conv_w", "gdn_A_log", "gdn_dt_bias",
           "gdn_onorm_g", "w_out", "norm_xattn_g", "mem_norm_g", "w_cq", "w_ckv", "xattn_qnorm_g", "xattn_knorm_g", "w_co",
           "norm_mlp_g", "w_mlp1", "w_mlp2")
PACK_ROWS = 16
LOSS_ROW = len(REPLICATED)


def _whole(name, gathered):
    if name in COLUMN_SHARDED:
        return gathered.transpose(1, 0, 2).reshape(gathered.shape[1], N_DEV * gathered.shape[2])
    return gathered.reshape(N_DEV * gathered.shape[1], gathered.shape[2])


def _blocks(name, whole):
    if name in COLUMN_SHARDED:
        rows, cols = whole.shape
        return whole.reshape(rows, N_DEV, cols // N_DEV).transpose(1, 0, 2)
    return whole.reshape(N_DEV, whole.shape[0] // N_DEV, whole.shape[1])


def _pack(vals, fill=0.0):
    rows = [jnp.pad(vals[k], ((0, 0), (0, D_MODEL - vals[k].shape[1])), constant_values=fill) for k in REPLICATED]
    rows.append(jnp.full((PACK_ROWS - len(rows), D_MODEL), fill, F32))
    return jnp.concatenate(rows, axis=0)


def kernel(x, mem, norm_mix_g, w_in, fox_qnorm_g, fox_knorm_g, fox_f_bias, fox_onorm_g, gdn_conv_w, gdn_A_log, gdn_dt_bias, gdn_onorm_g, w_out, norm_xattn_g, mem_norm_g, w_cq, w_ckv, xattn_qnorm_g, xattn_knorm_g, w_co, norm_mlp_g, w_mlp1, w_mlp2, loss_target, m_norm_mix_g, m_w_in, m_fox_qnorm_g, m_fox_knorm_g, m_fox_f_bias, m_fox_onorm_g, m_gdn_conv_w, m_gdn_A_log, m_gdn_dt_bias, m_gdn_onorm_g, m_w_out, m_norm_xattn_g, m_mem_norm_g, m_w_cq, m_w_ckv, m_xattn_qnorm_g, m_xattn_knorm_g, m_w_co, m_norm_mlp_g, m_w_mlp1, m_w_mlp2, v_norm_mix_g, v_w_in, v_fox_qnorm_g, v_fox_knorm_g, v_fox_f_bias, v_fox_onorm_g, v_gdn_conv_w, v_gdn_A_log, v_gdn_dt_bias, v_gdn_onorm_g, v_w_out, v_norm_xattn_g, v_mem_norm_g, v_w_cq, v_w_ckv, v_xattn_qnorm_g, v_xattn_knorm_g, v_w_co, v_norm_mlp_g, v_w_mlp1, v_w_mlp2):
    given = dict(locals())
    w = {k: given[k] for k in WEIGHTS}
    m = {k: given["m_" + k] for k in WEIGHTS}
    v = {k: given["v_" + k] for k in WEIGHTS}

    shards = [w[k][0] if k == "gdn_conv_w" else w[k][0].astype(BF16) for k in SHARDED]
    whole = {k: _whole(k, g) for k, g in zip(SHARDED, _all_gather_hbm(shards, "gather_weights"))}

    small = {k: w[k] for k in REPLICATED}
    loss_local, grad_x, grads = _local_step(x, mem, loss_target, **small, **whole)

    core = lax.axis_index("c").astype(jnp.int32).reshape(1)
    chip = (2 * lax.axis_index("x") + lax.axis_index("y")).astype(jnp.int32).reshape(1)
    own = [_blocks(k, grads[k]) for k in SHARDED]
    got = _pair_exchange(own, "grad_pair_exchange")
    sums = [_pair_sum(core, o, g, "grad_pair_sum_" + k) for k, o, g in zip(SHARDED, own, got)]
    parts = _chip_exchange(sums, "grad_chip_exchange")
    out_g, out_d, out_m, out_v = {}, {}, {}, {}
    for k, s, p in zip(SHARDED, sums, parts):
        res = _sum_adam(chip, s, p, w[k][0], m[k][0], v[k][0], "adam_" + k)
        out_g[k], out_d[k], out_m[k], out_v[k] = (r[None] for r in res)

    packed = _pack({k: grads[k] for k in REPLICATED}).at[LOSS_ROW, 0].set(loss_local)
    everyone = _all_gather_vmem(packed, "gather_small")
    res = _sum_adam(chip, None, everyone, _pack(small), _pack({k: m[k] for k in REPLICATED}),
                    _pack({k: v[k] for k in REPLICATED}, fill=1.0), "adam_small")
    for i, k in enumerate(REPLICATED):
        n = w[k].shape[1]
        out_g[k], out_d[k], out_m[k], out_v[k] = (r[i:i + 1, 0:n] for r in res)
    loss = res[0][LOSS_ROW, 0]

    return (loss, grad_x, *[out_g[k] for k in WEIGHTS], *[out_d[k] for k in WEIGHTS], *[out_m[k] for k in WEIGHTS],
            *[out_v[k] for k in WEIGHTS])
```

```python
import functools

import jax
import jax.numpy as jnp
import numpy as np
from jax import lax
from jax.experimental import pallas as pl
from jax.experimental.pallas import tpu as pltpu

F32 = jnp.float32
BF16 = jnp.bfloat16

D_MODEL = 1024
FOX_HEADS = 8
FOX_HEAD_DIM = 64
FOX_WIDTH = 512
GDN_HEADS = 4
GDN_HEAD_DIM = 128
GDN_WIDTH = 512
CONV_WIDTH = 4
GDN_CHUNK = 64
GDN_GROUP = 4
FOX_BLOCK = 512
XATTN_HEADS = 4
XATTN_HEAD_DIM = 128
XATTN_WIDTH = 512
D_FF = 4096
EPS = 1e-6
NEG_INF = -1e30
N_DEV = 8

ADAM_LR = 0.001
ADAM_B1 = 0.9
ADAM_B2 = 0.999
ADAM_EPS = 1e-08
ADAM_WD = 0.01
ADAM_STEP = 10

P_FOX = 0
P_GDN = 1536
P_Z = 3072
P_SMALL = 3584
P_DIM = 3712
SM_F = 0
SM_B = 8
SM_A = 12
SM_ROWS = 16

LANES = 128
VMEM_LIMIT = 56 * 1024 * 1024

NN = (((1,), (0,)), ((), ()))
NT = (((1,), (1,)), ((), ()))
TN = (((0,), (0,)), ((), ()))


def _dot(a, b, dims=NN):
    return lax.dot_general(a.astype(BF16), b.astype(BF16), dims, preferred_element_type=F32)


def _cparams(sem=None):
    kw = dict(vmem_limit_bytes=VMEM_LIMIT)
    if sem is not None:
        kw["dimension_semantics"] = sem
    return pltpu.CompilerParams(**kw)


def _sigmoid(x):
    return 0.5 * (jnp.tanh(0.5 * x) + 1.0)


def _softplus(x):
    return jnp.maximum(x, 0.0) + jnp.log1p(jnp.exp(-jnp.abs(x)))


def _log_sigmoid(x):
    return -_softplus(-x)


def _rms(x, g):
    r = lax.rsqrt(jnp.mean(x * x, axis=-1, keepdims=True) + EPS)
    return x * r * g


def _rms_bwd(x, g, dy):
    r = lax.rsqrt(jnp.mean(x * x, axis=-1, keepdims=True) + EPS)
    xh = x * r
    dg = jnp.sum(dy * xh, axis=0, keepdims=True)
    dyg = dy * g
    dx = r * (dyg - xh * jnp.mean(dyg * xh, axis=-1, keepdims=True))
    return dx, dg


def _pair_stat(t, m0):
    s0 = jnp.sum(jnp.where(m0, t, 0.0), axis=-1, keepdims=True)
    s1 = jnp.sum(jnp.where(m0, 0.0, t), axis=-1, keepdims=True)
    return jnp.where(m0, s0, s1)


def _rms_pair(x, g, m0):
    r = lax.rsqrt(_pair_stat(x * x, m0) * (1.0 / FOX_HEAD_DIM) + EPS)
    return x * r * g


def _rms_pair_bwd(x, g, dy, m0):
    r = lax.rsqrt(_pair_stat(x * x, m0) * (1.0 / FOX_HEAD_DIM) + EPS)
    xh = x * r
    dg = jnp.sum(dy * xh, axis=0, keepdims=True)
    dyg = dy * g
    dx = r * (dyg - xh * (_pair_stat(dyg * xh, m0) * (1.0 / FOX_HEAD_DIM)))
    return dx, dg


@jax.custom_vjp
def _mm_nn(a, b):
    return _dot(a, b, NN)


_mm_nn.defvjp(lambda a, b: (_dot(a, b, NN), (a, b)),
              lambda r, g: (_dot(g, r[1], NT), _dot(r[0], g, TN)))


@jax.custom_vjp
def _mm_nt(a, b):
    return _dot(a, b, NT)


_mm_nt.defvjp(lambda a, b: (_dot(a, b, NT), (a, b)),
              lambda r, g: (_dot(g, r[1], NN), _dot(g, r[0], TN)))


@jax.custom_vjp
def _mm_tn(a, b):
    return _dot(a, b, TN)


_mm_tn.defvjp(lambda a, b: (_dot(a, b, TN), (a, b)),
              lambda r, g: (_dot(r[1], g, NT), _dot(r[0], g, NN)))


def _dot3(a, b, dims):
    ah = a.astype(BF16)
    al = (a - ah.astype(F32)).astype(BF16)
    bh = b.astype(BF16)
    bl = (b - bh.astype(F32)).astype(BF16)
    d = functools.partial(lax.dot_general, dimension_numbers=dims, preferred_element_type=F32)
    return d(ah, bh) + d(ah, bl) + d(al, bh)


@jax.custom_vjp
def _mm3(a, b):
    return _dot3(a, b, NN)


_mm3.defvjp(lambda a, b: (_dot3(a, b, NN), (a, b)),
            lambda r, g: (_dot3(g, r[1], NT), _dot3(r[0], g, TN)))


def _unit_lower_inverses(mats):
    c = mats[0].shape[0]
    eye = (lax.broadcasted_iota(jnp.int32, (c, c), 0) == lax.broadcasted_iota(jnp.int32, (c, c), 1)).astype(F32)
    xs = [eye - a for a in mats]
    ps = list(mats)
    k = 2
    while k < c + 1:
        ps = [_mm3(p, p) for p in ps]
        xs = [x + _mm3(x, p) for x, p in zip(xs, ps)]
        k *= 2
    return xs


def _wgrad(a, b, name, bk=1024, bn=1024, bt=512):
    t_len, k_len = a.shape
    n_len = b.shape[1]
    bk, bn, bt = min(bk, k_len), min(bn, n_len), min(bt, t_len)
    nt = t_len // bt

    def body(a_ref, b_ref, o_ref, acc_ref):
        t = pl.program_id(2)

        @pl.when(t == 0)
        def _():
            acc_ref[...] = jnp.zeros_like(acc_ref)

        acc_ref[...] += _dot(a_ref[...], b_ref[...], TN)

        @pl.when(t == nt - 1)
        def _():
            o_ref[...] = acc_ref[...]

    return pl.pallas_call(
        body, name=name, grid=(k_len // bk, n_len // bn, nt),
        in_specs=[pl.BlockSpec((bt, bk), lambda i, j, t: (t, i)), pl.BlockSpec((bt, bn), lambda i, j, t: (t, j))],
        out_specs=pl.BlockSpec((bk, bn), lambda i, j, t: (i, j)),
        out_shape=jax.ShapeDtypeStruct((k_len, n_len), F32),
        scratch_shapes=[pltpu.VMEM((bk, bn), F32)],
        compiler_params=_cparams(("parallel", "parallel", "arbitrary")),
    )(a, b)


def _rows_matmul(a, b, name, bt=512):
    r_len, t_len = a.shape
    n_len = b.shape[1]
    bt = min(bt, t_len)
    nt = t_len // bt

    def body(a_ref, b_ref, o_ref):
        t = pl.program_id(0)

        @pl.when(t == 0)
        def _():
            o_ref[...] = jnp.zeros_like(o_ref)

        o_ref[...] += _dot(a_ref[...], b_ref[...], NN)

    return pl.pallas_call(
        body, name=name, grid=(nt,),
        in_specs=[pl.BlockSpec((r_len, bt), lambda t: (0, t)), pl.BlockSpec((bt, n_len), lambda t: (t, 0))],
        out_specs=pl.BlockSpec((r_len, n_len), lambda t: (0, 0)),
        out_shape=jax.ShapeDtypeStruct((r_len, n_len), F32),
        compiler_params=_cparams(("arbitrary",)),
    )(a, b)


def _in_proj(x, g, wp, wst, tm=256):
    t_len, d = x.shape
    tm = min(tm, t_len)

    def body(x_ref, g_ref, wp_ref, wst_ref, h_ref, fox_ref, gdn_ref, z_ref, sm_ref, smt_ref):
        h = _rms(x_ref[...], g_ref[...]).astype(BF16)
        h_ref[...] = h
        p = _dot(h, wp_ref[...], NN)
        fox_ref[...] = p[:, P_FOX:P_GDN]
        gdn_ref[...] = p[:, P_GDN:P_Z]
        z_ref[...] = p[:, P_Z:P_SMALL]
        sm_ref[...] = p[:, P_SMALL:P_DIM]
        smt_ref[...] = _dot(wst_ref[...], h, NT)

    row = lambda i: (i, 0)
    fixed = lambda i: (0, 0)
    return pl.pallas_call(
        body, name="in_proj", grid=(t_len // tm,),
        in_specs=[pl.BlockSpec((tm, d), row), pl.BlockSpec((1, d), fixed), pl.BlockSpec((d, P_DIM), fixed),
                  pl.BlockSpec((SM_ROWS, d), fixed)],
        out_specs=[pl.BlockSpec((tm, d), row), pl.BlockSpec((tm, 1536), row), pl.BlockSpec((tm, 1536), row),
                   pl.BlockSpec((tm, 512), row), pl.BlockSpec((tm, LANES), row), pl.BlockSpec((SM_ROWS, tm), lambda i: (0, i))],
        out_shape=[jax.ShapeDtypeStruct((t_len, d), BF16), jax.ShapeDtypeStruct((t_len, 1536), F32),
                   jax.ShapeDtypeStruct((t_len, 1536), F32), jax.ShapeDtypeStruct((t_len, 512), F32),
                   jax.ShapeDtypeStruct((t_len, LANES), F32), jax.ShapeDtypeStruct((SM_ROWS, t_len), F32)],
        compiler_params=_cparams(("parallel",)),
    )(x, g, wp, wst)


def _in_proj_bwd(dproj, dsmt, x, g, wp, wst, dx1, tm=256):
    t_len, d = x.shape
    tm = min(tm, t_len)

    def body(dp_ref, dst_ref, x_ref, g_ref, wp_ref, wst_ref, dx1_ref, dx_ref, dg_ref):
        i = pl.program_id(0)
        dh = _dot(dp_ref[...], wp_ref[...], NT) + _dot(dst_ref[...], wst_ref[...], TN)
        dxn, dg = _rms_bwd(x_ref[...], g_ref[...], dh)
        dx_ref[...] = dx1_ref[...] + dxn

        @pl.when(i == 0)
        def _():
            dg_ref[...] = jnp.zeros_like(dg_ref)

        dg_ref[...] += dg

    row = lambda i: (i, 0)
    fixed = lambda i: (0, 0)
    return pl.pallas_call(
        body, name="in_proj_bwd", grid=(t_len // tm,),
        in_specs=[pl.BlockSpec((tm, P_DIM), row), pl.BlockSpec((SM_ROWS, tm), lambda i: (0, i)), pl.BlockSpec((tm, d), row),
                  pl.BlockSpec((1, d), fixed), pl.BlockSpec((d, P_DIM), fixed), pl.BlockSpec((SM_ROWS, d), fixed),
                  pl.BlockSpec((tm, d), row)],
        out_specs=[pl.BlockSpec((tm, d), row), pl.BlockSpec((1, d), fixed)],
        out_shape=[jax.ShapeDtypeStruct((t_len, d), F32), jax.ShapeDtypeStruct((1, d), F32)],
        compiler_params=_cparams(("arbitrary",)),
    )(dproj, dsmt, x, g, wp, wst, dx1)


def _fox_cum(smt, bias_col, n_batch, s_len, ck=256):
    ck = min(ck, s_len)

    def body(s_ref, b_ref, c_ref):
        tri = (lax.broadcasted_iota(jnp.int32, (ck, ck), 0) <= lax.broadcasted_iota(jnp.int32, (ck, ck), 1)).astype(F32)
        carry = jnp.zeros((SM_ROWS, 1), F32)
        for r in range(s_len // ck):
            ls = _log_sigmoid(s_ref[:, r * ck:(r + 1) * ck] + b_ref[...])
            c = jnp.dot(ls, tri, precision=lax.Precision.HIGHEST, preferred_element_type=F32) + carry
            c_ref[:, r * ck:(r + 1) * ck] = c
            carry = c[:, ck - 1:ck]

    return pl.pallas_call(
        body, name="fox_cum", grid=(n_batch,),
        in_specs=[pl.BlockSpec((SM_ROWS, s_len), lambda b: (0, b)), pl.BlockSpec((SM_ROWS, 1), lambda b: (0, 0))],
        out_specs=pl.BlockSpec((SM_ROWS, s_len), lambda b: (0, b)),
        out_shape=jax.ShapeDtypeStruct(smt.shape, F32),
        compiler_params=_cparams(("parallel",)),
    )(smt, bias_col)


def _fox_cum_bwd(dc, smt, bias_col, n_batch, s_len, ck=256):
    ck = min(ck, s_len)
    nr = s_len // ck

    def body(dc_ref, s_ref, b_ref, dl_ref, db_ref):
        b = pl.program_id(0)
        tri = (lax.broadcasted_iota(jnp.int32, (ck, ck), 0) >= lax.broadcasted_iota(jnp.int32, (ck, ck), 1)).astype(F32)
        carry = jnp.zeros((SM_ROWS, 1), F32)
        tot = jnp.zeros((SM_ROWS, 1), F32)
        for r in reversed(range(nr)):
            sl = slice(r * ck, (r + 1) * ck)
            dls = jnp.dot(dc_ref[:, sl], tri, precision=lax.Precision.HIGHEST, preferred_element_type=F32) + carry
            carry = dls[:, 0:1]
            dl = dls * (1.0 - _sigmoid(s_ref[:, sl] + b_ref[...]))
            dl_ref[:, sl] = dl
            tot = tot + jnp.sum(dl, axis=1, keepdims=True)

        @pl.when(b == 0)
        def _():
            db_ref[...] = jnp.zeros_like(db_ref)

        db_ref[...] += jnp.broadcast_to(tot, db_ref.shape)

    return pl.pallas_call(
        body, name="fox_cum_bwd", grid=(n_batch,),
        in_specs=[pl.BlockSpec((SM_ROWS, s_len), lambda b: (0, b)), pl.BlockSpec((SM_ROWS, s_len), lambda b: (0, b)),
                  pl.BlockSpec((SM_ROWS, 1), lambda b: (0, 0))],
        out_specs=[pl.BlockSpec((SM_ROWS, s_len), lambda b: (0, b)), pl.BlockSpec((SM_ROWS, LANES), lambda b: (0, 0))],
        out_shape=[jax.ShapeDtypeStruct(smt.shape, F32), jax.ShapeDtypeStruct((SM_ROWS, LANES), F32)],
        compiler_params=_cparams(("arbitrary",)),
    )(dc, smt, bias_col)


def _fox_diagonal_mask(tq):
    return lax.broadcasted_iota(jnp.int32, (tq, tq), 1) <= lax.broadcasted_iota(jnp.int32, (tq, tq), 0)


def _fox_fwd(pf, cb, gq2, gk2, go2, tq=256):
    n_batch, s_len, _ = pf.shape
    tq = min(tq, s_len)
    nq = s_len // tq
    scale = FOX_HEAD_DIM ** -0.5

    def body(q_ref, k_ref, v_ref, c_ref, gq_ref, gk_ref, go_ref, o_ref, on_ref, lse_ref, kh_ref, vh_ref):
        j = pl.program_id(1)
        i = pl.program_id(2)
        m0 = lax.broadcasted_iota(jnp.int32, (1, LANES), 1) < FOX_HEAD_DIM

        @pl.when(i == 0)
        def _():
            kn = _rms_pair(k_ref[0], gk_ref[...], m0)
            kh_ref[0] = jnp.where(m0, kn, 0.0).astype(BF16)
            kh_ref[1] = jnp.where(m0, 0.0, kn).astype(BF16)
            v = v_ref[0]
            vh_ref[0] = jnp.where(m0, v, 0.0).astype(BF16)
            vh_ref[1] = jnp.where(m0, 0.0, v).astype(BF16)

        qb = (_rms_pair(q_ref[0], gq_ref[...], m0) * scale).astype(BF16)

        def step(kb, carry, diagonal=False):
            ms, ls, acc = carry
            off = pl.multiple_of(kb * tq, tq)
            new_m, new_l, alphas, pv = [], [], [], []
            for hh in range(2):
                s = _dot(qb, kh_ref[hh, pl.ds(off, tq), :], NT)
                s = s - c_ref[0, kb, pl.ds(2 * j + hh, 1), :]
                if diagonal:
                    s = jnp.where(_fox_diagonal_mask(tq), s, NEG_INF)
                m_new = jnp.maximum(ms[hh], jnp.max(s, axis=-1, keepdims=True))
                alpha = jnp.exp(ms[hh] - m_new)
                p = jnp.exp(s - m_new)
                new_l.append(alpha * ls[hh] + jnp.sum(p, axis=-1, keepdims=True))
                new_m.append(m_new)
                alphas.append(alpha)
                pv.append(_dot(p, vh_ref[hh, pl.ds(off, tq), :], NN))
            acc = jnp.where(m0, alphas[0], alphas[1]) * acc + pv[0] + pv[1]
            return tuple(new_m), tuple(new_l), acc

        init_m = (jnp.full((tq, 1), NEG_INF, F32),) * 2
        init_l = (jnp.zeros((tq, 1), F32),) * 2
        carry = lax.fori_loop(0, i, step, (init_m, init_l, jnp.zeros((tq, LANES), F32)))
        ms, ls, acc = step(i, carry, diagonal=True)
        o = acc / jnp.where(m0, ls[0], ls[1])
        o_ref[0] = o
        on_ref[0] = _rms_pair(o, go_ref[...], m0).astype(BF16)
        lse_ref[0] = jnp.where(m0, ms[0] + jnp.log(ls[0]), ms[1] + jnp.log(ls[1]))

    fixed = lambda b, j, i: (0, 0)
    tile = lambda b, j, i: (b, i, j)
    return pl.pallas_call(
        body, name="fox_fwd", grid=(n_batch, 4, nq),
        in_specs=[pl.BlockSpec((1, tq, LANES), tile), pl.BlockSpec((1, s_len, LANES), lambda b, j, i: (b, 0, 4 + j)),
                  pl.BlockSpec((1, s_len, LANES), lambda b, j, i: (b, 0, 8 + j)),
                  pl.BlockSpec((1, nq, SM_ROWS, tq), lambda b, j, i: (b, 0, 0, 0)),
                  pl.BlockSpec((1, LANES), fixed), pl.BlockSpec((1, LANES), fixed), pl.BlockSpec((1, LANES), fixed)],
        out_specs=[pl.BlockSpec((1, tq, LANES), tile), pl.BlockSpec((1, tq, LANES), tile), pl.BlockSpec((1, tq, LANES), tile)],
        out_shape=[jax.ShapeDtypeStruct((n_batch, s_len, FOX_WIDTH), F32), jax.ShapeDtypeStruct((n_batch, s_len, FOX_WIDTH), BF16),
                   jax.ShapeDtypeStruct((n_batch, s_len, FOX_WIDTH), F32)],
        scratch_shapes=[pltpu.VMEM((2, s_len, LANES), BF16), pltpu.VMEM((2, s_len, LANES), BF16)],
        compiler_params=_cparams(("parallel", "parallel", "arbitrary")),
    )(pf, pf, pf, cb, gq2, gk2, go2)


def _fox_bwd(pf, cb, gq2, gk2, go2, o, lse, don, tq=256):
    n_batch, s_len, _ = pf.shape
    tq = min(tq, s_len)
    nq = s_len // tq
    scale = FOX_HEAD_DIM ** -0.5

    def body(q_ref, k_ref, v_ref, c_ref, gq_ref, gk_ref, go_ref, o_ref, lse_ref, don_ref,
             dq_ref, dk_ref, dv_ref, dc_ref, dgq_ref, dgk_ref, dgo_ref, kh_ref, vh_ref, dka_ref, dva_ref, dca_ref):
        b = pl.program_id(0)
        j = pl.program_id(1)
        i = pl.program_id(2)
        m0 = lax.broadcasted_iota(jnp.int32, (1, LANES), 1) < FOX_HEAD_DIM

        @pl.when((b == 0) & (j == 0) & (i == 0))
        def _():
            dgq_ref[...] = jnp.zeros_like(dgq_ref)
            dgk_ref[...] = jnp.zeros_like(dgk_ref)
            dgo_ref[...] = jnp.zeros_like(dgo_ref)

        @pl.when(i == 0)
        def _():
            kn = _rms_pair(k_ref[0], gk_ref[...], m0)
            kh_ref[0] = jnp.where(m0, kn, 0.0).astype(BF16)
            kh_ref[1] = jnp.where(m0, 0.0, kn).astype(BF16)
            v = v_ref[0]
            vh_ref[0] = jnp.where(m0, v, 0.0).astype(BF16)
            vh_ref[1] = jnp.where(m0, 0.0, v).astype(BF16)
            dka_ref[...] = jnp.zeros_like(dka_ref)
            dva_ref[...] = jnp.zeros_like(dva_ref)
            dca_ref[...] = jnp.zeros_like(dca_ref)

        q = q_ref[0]
        qn = _rms_pair(q, gq_ref[...], m0)
        qs = qn * scale
        qb = qs.astype(BF16)
        qh = (jnp.where(m0, qs, 0.0).astype(BF16), jnp.where(m0, 0.0, qs).astype(BF16))
        ot = o_ref[0]
        do, dgo = _rms_pair_bwd(ot, go_ref[...], don_ref[0], m0)
        dgo_ref[...] += dgo
        dd = do * ot
        delta = (jnp.sum(jnp.where(m0, dd, 0.0), axis=-1, keepdims=True), jnp.sum(jnp.where(m0, 0.0, dd), axis=-1, keepdims=True))
        doh = (jnp.where(m0, do, 0.0).astype(BF16), jnp.where(m0, 0.0, do).astype(BF16))
        lse_t = lse_ref[0]
        lse_h = (lse_t[:, 0:1], lse_t[:, FOX_HEAD_DIM:FOX_HEAD_DIM + 1])

        def step(kb, carry, diagonal=False):
            dqn, rs = carry
            rs = list(rs)
            off = pl.multiple_of(kb * tq, tq)
            for hh in range(2):
                kblk = kh_ref[hh, pl.ds(off, tq), :]
                vblk = vh_ref[hh, pl.ds(off, tq), :]
                s = _dot(qb, kblk, NT)
                s = s - c_ref[0, kb, pl.ds(2 * j + hh, 1), :]
                if diagonal:
                    s = jnp.where(_fox_diagonal_mask(tq), s, NEG_INF)
                p = jnp.exp(s - lse_h[hh])
                dp = _dot(doh[hh], vblk, NT)
                ds = p * (dp - delta[hh])
                dva_ref[pl.ds(off, tq), :] += _dot(p, doh[hh], TN)
                dka_ref[pl.ds(off, tq), :] += _dot(ds, qh[hh], TN)
                dca_ref[kb, hh:hh + 1, :] += -jnp.sum(ds, axis=0, keepdims=True)
                rs[hh] = rs[hh] + jnp.sum(ds, axis=-1, keepdims=True)
                dqn = dqn + _dot(ds, kblk, NN)
            return dqn, tuple(rs)

        carry = lax.fori_loop(0, i, step, (jnp.zeros((tq, LANES), F32), (jnp.zeros((tq, 1), F32),) * 2))
        dqn, rs = step(i, carry, diagonal=True)
        dqn = dqn * scale
        rs_rows = jnp.where(m0, rs[0], rs[1]).T
        dca_ref[i, 0:1, :] += rs_rows[0:1, :]
        dca_ref[i, 1:2, :] += rs_rows[FOX_HEAD_DIM:FOX_HEAD_DIM + 1, :]
        dq, dgq = _rms_pair_bwd(q, gq_ref[...], dqn, m0)
        dq_ref[0] = dq.astype(BF16)
        dgq_ref[...] += dgq

        @pl.when(i == nq - 1)
        def _():
            dk, dgk = _rms_pair_bwd(k_ref[0], gk_ref[...], dka_ref[...], m0)
            dk_ref[0] = dk.astype(BF16)
            dgk_ref[...] += dgk
            dv_ref[0] = dva_ref[...].astype(BF16)
            dc_ref[0, 0] = dca_ref[...]

    fixed = lambda b, j, i: (0, 0)
    tile = lambda b, j, i: (b, i, j)
    full = lambda b, j, i: (b, 0, j)
    wide = jax.ShapeDtypeStruct((n_batch, s_len, FOX_WIDTH), BF16)
    gain = jax.ShapeDtypeStruct((1, LANES), F32)
    return pl.pallas_call(
        body, name="fox_bwd", grid=(n_batch, 4, nq),
        in_specs=[pl.BlockSpec((1, tq, LANES), tile), pl.BlockSpec((1, s_len, LANES), lambda b, j, i: (b, 0, 4 + j)),
                  pl.BlockSpec((1, s_len, LANES), lambda b, j, i: (b, 0, 8 + j)),
                  pl.BlockSpec((1, nq, SM_ROWS, tq), lambda b, j, i: (b, 0, 0, 0)),
                  pl.BlockSpec((1, LANES), fixed), pl.BlockSpec((1, LANES), fixed), pl.BlockSpec((1, LANES), fixed),
                  pl.BlockSpec((1, tq, LANES), tile), pl.BlockSpec((1, tq, LANES), tile), pl.BlockSpec((1, tq, LANES), tile)],
        out_specs=[pl.BlockSpec((1, tq, LANES), tile), pl.BlockSpec((1, s_len, LANES), full), pl.BlockSpec((1, s_len, LANES), full),
                   pl.BlockSpec((1, 1, nq, 8, tq), lambda b, j, i: (b, j, 0, 0, 0)),
                   pl.BlockSpec((1, LANES), fixed), pl.BlockSpec((1, LANES), fixed), pl.BlockSpec((1, LANES), fixed)],
        out_shape=[wide, wide, wide, jax.ShapeDtypeStruct((n_batch, 4, nq, 8, tq), F32), gain, gain, gain],
        scratch_shapes=[pltpu.VMEM((2, s_len, LANES), BF16), pltpu.VMEM((2, s_len, LANES), BF16),
                        pltpu.VMEM((s_len, LANES), F32), pltpu.VMEM((s_len, LANES), F32), pltpu.VMEM((nq, 8, tq), F32)],
        compiler_params=_cparams(("arbitrary", "arbitrary", "arbitrary")),
    )(pf, pf, pf, cb, gq2, gk2, go2, o, lse, don)


def _shift_down(x, k):
    row = lax.broadcasted_iota(jnp.int32, x.shape, 0)
    return jnp.where(row >= k, pltpu.roll(x, k, 0), 0.0)


def _shift_up(x, k):
    n = x.shape[0]
    row = lax.broadcasted_iota(jnp.int32, x.shape, 0)
    return jnp.where(row < n - k, pltpu.roll(x, n - k, 0), 0.0)


def _conv_silu(x, w):
    y = w[3:4] * x + w[2:3] * _shift_down(x, 1) + w[1:2] * _shift_down(x, 2) + w[0:1] * _shift_down(x, 3)
    return y, y * _sigmoid(y)


def _gdn_pre(pg, conv_w):
    n_batch, s_len, width = pg.shape
    ncb = width // LANES

    def body(x_ref, w_ref, o_ref):
        cb = pl.program_id(1)
        _, s = _conv_silu(x_ref[0], w_ref[...])
        sn = s * lax.rsqrt(jnp.sum(s * s, axis=-1, keepdims=True) + EPS)
        o_ref[0] = jnp.where(cb < 2 * GDN_HEADS, sn, s)

    return pl.pallas_call(
        body, name="gdn_pre", grid=(n_batch, ncb),
        in_specs=[pl.BlockSpec((1, s_len, LANES), lambda b, c: (b, 0, c)), pl.BlockSpec((8, LANES), lambda b, c: (0, c))],
        out_specs=pl.BlockSpec((1, s_len, LANES), lambda b, c: (b, 0, c)),
        out_shape=jax.ShapeDtypeStruct(pg.shape, F32),
        compiler_params=_cparams(("parallel", "parallel")),
    )(pg, conv_w)


def _gdn_pre_bwd(pg, conv_w, dout):
    n_batch, s_len, width = pg.shape
    ncb = width // LANES

    def body(x_ref, w_ref, d_ref, dx_ref, dw_ref):
        cb = pl.program_id(0)
        b = pl.program_id(1)
        x = x_ref[0]
        w = w_ref[...]
        d = d_ref[0]
        y, s = _conv_silu(x, w)
        rr = lax.rsqrt(jnp.sum(s * s, axis=-1, keepdims=True) + EPS)
        sn = s * rr
        ds_n = rr * (d - sn * jnp.sum(d * sn, axis=-1, keepdims=True))
        ds = jnp.where(cb < 2 * GDN_HEADS, ds_n, d)
        sig = _sigmoid(y)
        dy = ds * (sig * (1.0 + y * (1.0 - sig)))
        dx = w[3:4] * dy + w[2:3] * _shift_up(dy, 1) + w[1:2] * _shift_up(dy, 2) + w[0:1] * _shift_up(dy, 3)
        dx_ref[0] = dx.astype(BF16)
        dw = [jnp.sum(dy * _shift_down(x, 3 - jj), axis=0, keepdims=True) if jj < 3 else jnp.sum(dy * x, axis=0, keepdims=True)
              for jj in range(CONV_WIDTH)]
        rows = lax.broadcasted_iota(jnp.int32, (8, LANES), 0)
        dwb = jnp.zeros((8, LANES), F32)
        for jj in range(CONV_WIDTH):
            dwb = dwb + jnp.where(rows == jj, dw[jj], 0.0)

        @pl.when(b == 0)
        def _():
            dw_ref[...] = jnp.zeros_like(dw_ref)

        dw_ref[...] += dwb

    blk = lambda c, b: (b, 0, c)
    return pl.pallas_call(
        body, name="gdn_pre_bwd", grid=(ncb, n_batch),
        in_specs=[pl.BlockSpec((1, s_len, LANES), blk), pl.BlockSpec((8, LANES), lambda c, b: (0, c)), pl.BlockSpec((1, s_len, LANES), blk)],
        out_specs=[pl.BlockSpec((1, s_len, LANES), blk), pl.BlockSpec((8, LANES), lambda c, b: (0, c))],
        out_shape=[jax.ShapeDtypeStruct(pg.shape, BF16), jax.ShapeDtypeStruct((8, width), F32)],
        compiler_params=_cparams(("parallel", "arbitrary")),
    )(pg, conv_w, dout)


def _gdn_gates(smc, smr, a_c, dt_c, a_r, dt_r, h):
    lane = lax.broadcasted_iota(jnp.int32, (1, LANES), 1)
    sub = lax.broadcasted_iota(jnp.int32, (SM_ROWS, 1), 0)
    beta_c = jnp.sum(jnp.where(lane == SM_B + h, _sigmoid(smc), 0.0), axis=1, keepdims=True)
    g_all_c = -jnp.exp(a_c) * _softplus(smc + dt_c)
    g_c = jnp.sum(jnp.where(lane == SM_A + h, g_all_c, 0.0), axis=1, keepdims=True)
    g_all_r = -jnp.exp(a_r) * _softplus(smr + dt_r)
    g_r = jnp.sum(jnp.where(sub == SM_A + h, g_all_r, 0.0), axis=0, keepdims=True)
    return beta_c, g_c, g_r


def _gdn_group(qkv, z, smc, smr, a_c, dt_c, a_r, dt_r, go, states):
    n_grp = len(qkv)
    c = qkv[0].shape[0]
    hd = GDN_HEAD_DIM
    pairs = [(g, h) for g in range(n_grp) for h in range(GDN_HEADS)]
    ii = lax.broadcasted_iota(jnp.int32, (c, c), 0)
    jj = lax.broadcasted_iota(jnp.int32, (c, c), 1)
    incl = ii >= jj
    col = lambda arr, base, h: arr[:, base + h * hd:base + (h + 1) * hd]

    qs, ks, kbs, vbs, decays, gcs, g_lasts, amats = [], [], [], [], [], [], [], []
    for g, h in pairs:
        beta_c, g_c, g_r = _gdn_gates(smc[g], smr[g], a_c, dt_c, a_r, dt_r, h)
        gc_c = jnp.sum(jnp.where(incl, g_r, 0.0), axis=1, keepdims=True)
        gc_r = jnp.sum(jnp.where(ii <= jj, g_c, 0.0), axis=0, keepdims=True)
        decay = jnp.where(incl, jnp.exp(jnp.where(incl, gc_c - gc_r, 0.0)), 0.0)
        k = col(qkv[g], GDN_WIDTH, h)
        kb = k * beta_c
        qs.append(col(qkv[g], 0, h) * (hd ** -0.5))
        ks.append(k)
        kbs.append(kb)
        vbs.append(col(qkv[g], 2 * GDN_WIDTH, h) * beta_c)
        decays.append(decay)
        gcs.append(gc_c)
        g_lasts.append(jnp.sum(g_c, axis=0, keepdims=True))
        amats.append(jnp.where(ii > jj, _mm_nt(kb, k) * decay, 0.0))
    ts = _unit_lower_inverses(amats)
    egcs = [jnp.exp(gc) for gc in gcs]
    us = [_mm_nn(t, vb) for t, vb in zip(ts, vbs)]
    ws = [_mm_nn(t, kb * e) for t, kb, e in zip(ts, kbs, egcs)]
    intras = [_mm_nt(q, k) * d for q, k, d in zip(qs, ks, decays)]
    qes = [q * e for q, e in zip(qs, egcs)]
    kds = [k * jnp.exp(gl - gc) for k, gl, gc in zip(ks, g_lasts, gcs)]
    sdecs = [jnp.exp(gl) for gl in g_lasts]

    outs = []
    for g in range(n_grp):
        idx = [g * GDN_HEADS + h for h in range(GDN_HEADS)]
        v_new = [us[i] - _mm_nn(ws[i], states[h]) for h, i in enumerate(idx)]
        o_state = [_mm_nn(qes[i], states[h]) for h, i in enumerate(idx)]
        o_intra = [_mm_nn(intras[i], v_new[h]) for h, i in enumerate(idx)]
        states = [states[h] * sdecs[i] + _mm_tn(kds[i], v_new[h]) for h, i in enumerate(idx)]
        outs.append([_rms(o_state[h] + o_intra[h], go) * (col(z[g], 0, h) * _sigmoid(col(z[g], 0, h))) for h in range(GDN_HEADS)])
    return outs, states


def _gdn_group_size(n_chunks):
    return GDN_GROUP if n_chunks % GDN_GROUP == 0 else 1


def _gdn_fwd(qkvn, z, smc, smr, a_c, dt_c, a_r, dt_r, go):
    n_batch, s_len, _ = qkvn.shape
    c = GDN_CHUNK
    n = s_len // c
    grp = _gdn_group_size(n)
    ng = n // grp
    gc = grp * c
    hd = GDN_HEAD_DIM

    def body(qkv_ref, z_ref, smc_ref, smr_ref, ac_ref, dc_ref, ar_ref, dr_ref, go_ref, og_ref, st_ref, s_ref):
        @pl.when(pl.program_id(1) == 0)
        def _():
            s_ref[...] = jnp.zeros_like(s_ref)

        states = [s_ref[h] for h in range(GDN_HEADS)]
        for h in range(GDN_HEADS):
            st_ref[0, 0, h] = states[h]
        rows = lambda k: slice(k * c, (k + 1) * c)
        outs, nxt = _gdn_group([qkv_ref[0, rows(k), :] for k in range(grp)], [z_ref[0, rows(k), :] for k in range(grp)],
                               [smc_ref[0, rows(k), :] for k in range(grp)], [smr_ref[k] for k in range(grp)],
                               ac_ref[...], dc_ref[...], ar_ref[...], dr_ref[...], go_ref[...], states)
        for k in range(grp):
            for h in range(GDN_HEADS):
                og_ref[0, rows(k), h * hd:(h + 1) * hd] = outs[k][h].astype(BF16)
        for h in range(GDN_HEADS):
            s_ref[h] = nxt[h]

    tok = lambda b, i: (b, i, 0)
    fixed = lambda b, i: (0, 0)
    return pl.pallas_call(
        body, name="gdn_fwd", grid=(n_batch, ng),
        in_specs=[pl.BlockSpec((1, gc, 3 * GDN_WIDTH), tok), pl.BlockSpec((1, gc, GDN_WIDTH), tok), pl.BlockSpec((1, gc, LANES), tok),
                  pl.BlockSpec((grp, SM_ROWS, c), lambda b, i: (b * ng + i, 0, 0)),
                  pl.BlockSpec((1, LANES), fixed), pl.BlockSpec((1, LANES), fixed), pl.BlockSpec((SM_ROWS, 1), fixed),
                  pl.BlockSpec((SM_ROWS, 1), fixed), pl.BlockSpec((1, LANES), fixed)],
        out_specs=[pl.BlockSpec((1, gc, GDN_WIDTH), tok), pl.BlockSpec((1, 1, GDN_HEADS, hd, hd), lambda b, i: (b, i, 0, 0, 0))],
        out_shape=[jax.ShapeDtypeStruct((n_batch, s_len, GDN_WIDTH), BF16), jax.ShapeDtypeStruct((n_batch, ng, GDN_HEADS, hd, hd), F32)],
        scratch_shapes=[pltpu.VMEM((GDN_HEADS, hd, hd), F32)],
        compiler_params=_cparams(("parallel", "arbitrary")),
    )(qkvn, z, smc, smr, a_c, dt_c, a_r, dt_r, go)


def _gdn_bwd(qkvn, z, smc, smr, a_c, dt_c, a_r, dt_r, go, states, dog):
    n_batch, s_len, _ = qkvn.shape
    c = GDN_CHUNK
    n = s_len // c
    grp = _gdn_group_size(n)
    ng = n // grp
    gc = grp * c
    hd = GDN_HEAD_DIM

    def body(qkv_ref, z_ref, smc_ref, smr_ref, ac_ref, dc_ref, ar_ref, dr_ref, go_ref, st_ref, dog_ref,
             dqkv_ref, dz_ref, dsmc_ref, dsmr_ref, dac_ref, ddc_ref, dar_ref, ddr_ref, dgo_ref, ds_ref):
        first = (pl.program_id(0) == 0) & (pl.program_id(1) == 0)

        @pl.when(pl.program_id(1) == 0)
        def _():
            ds_ref[...] = jnp.zeros_like(ds_ref)

        @pl.when(first)
        def _():
            for r in (dac_ref, ddc_ref, dar_ref, ddr_ref, dgo_ref):
                r[...] = jnp.zeros_like(r)

        rows = lambda k: slice(k * c, (k + 1) * c)
        states = [st_ref[0, 0, h] for h in range(GDN_HEADS)]
        prim = ([qkv_ref[0, rows(k), :] for k in range(grp)], [z_ref[0, rows(k), :] for k in range(grp)],
                [smc_ref[0, rows(k), :] for k in range(grp)], [smr_ref[k] for k in range(grp)],
                ac_ref[...], dc_ref[...], ar_ref[...], dr_ref[...], go_ref[...], states)
        _, vjp = jax.vjp(_gdn_group, *prim)
        cot = ([[dog_ref[0, rows(k), h * hd:(h + 1) * hd] for h in range(GDN_HEADS)] for k in range(grp)],
               [ds_ref[h] for h in range(GDN_HEADS)])
        dqkv, dz, dsmc, dsmr, dac, ddc, dar, ddr, dgo, dstates = vjp(cot)
        for k in range(grp):
            dqkv_ref[0, rows(k), :] = dqkv[k]
            dz_ref[0, rows(k), :] = dz[k].astype(BF16)
            dsmc_ref[0, rows(k), :] = dsmc[k]
            dsmr_ref[k] = dsmr[k]
        dac_ref[...] += dac
        ddc_ref[...] += ddc
        dar_ref[...] += dar
        ddr_ref[...] += ddr
        dgo_ref[...] += dgo
        for h in range(GDN_HEADS):
            ds_ref[h] = dstates[h]

    tok = lambda b, i: (b, ng - 1 - i, 0)
    fixed = lambda b, i: (0, 0)
    lane_vec = jax.ShapeDtypeStruct((1, LANES), F32)
    row_vec = jax.ShapeDtypeStruct((SM_ROWS, 1), F32)
    return pl.pallas_call(
        body, name="gdn_bwd", grid=(n_batch, ng),
        in_specs=[pl.BlockSpec((1, gc, 3 * GDN_WIDTH), tok), pl.BlockSpec((1, gc, GDN_WIDTH), tok), pl.BlockSpec((1, gc, LANES), tok),
                  pl.BlockSpec((grp, SM_ROWS, c), lambda b, i: (b * ng + ng - 1 - i, 0, 0)),
                  pl.BlockSpec((1, LANES), fixed), pl.BlockSpec((1, LANES), fixed), pl.BlockSpec((SM_ROWS, 1), fixed),
                  pl.BlockSpec((SM_ROWS, 1), fixed), pl.BlockSpec((1, LANES), fixed),
                  pl.BlockSpec((1, 1, GDN_HEADS, hd, hd), lambda b, i: (b, ng - 1 - i, 0, 0, 0)),
                  pl.BlockSpec((1, gc, GDN_WIDTH), lambda b, i: (b, ng - 1 - i, 1))],
        out_specs=[pl.BlockSpec((1, gc, 3 * GDN_WIDTH), tok), pl.BlockSpec((1, gc, GDN_WIDTH), tok), pl.BlockSpec((1, gc, LANES), tok),
                   pl.BlockSpec((grp, SM_ROWS, c), lambda b, i: (b * ng + ng - 1 - i, 0, 0)),
                   pl.BlockSpec((1, LANES), fixed), pl.BlockSpec((1, LANES), fixed), pl.BlockSpec((SM_ROWS, 1), fixed),
                   pl.BlockSpec((SM_ROWS, 1), fixed), pl.BlockSpec((1, LANES), fixed)],
        out_shape=[jax.ShapeDtypeStruct((n_batch, s_len, 3 * GDN_WIDTH), F32), jax.ShapeDtypeStruct((n_batch, s_len, GDN_WIDTH), BF16),
                   jax.ShapeDtypeStruct((n_batch, s_len, LANES), F32), jax.ShapeDtypeStruct((n_batch * n, SM_ROWS, c), F32),
                   lane_vec, lane_vec, row_vec, row_vec, lane_vec],
        scratch_shapes=[pltpu.VMEM((GDN_HEADS, hd, hd), F32)],
        compiler_params=_cparams(("arbitrary", "arbitrary")),
    )(qkvn, z, smc, smr, a_c, dt_c, a_r, dt_r, go, states, dog)


def _out_proj(x, oa, ob, w_out, g_x, w_cq, tm=256):
    t_len, d = x.shape
    tm = min(tm, t_len)

    def body(x_ref, oa_ref, ob_ref, wo_ref, g_ref, wq_ref, x1_ref, hq_ref, cq_ref):
        x1 = x_ref[...] + _dot(oa_ref[...], wo_ref[0:FOX_WIDTH, :]) + _dot(ob_ref[...], wo_ref[FOX_WIDTH:2 * FOX_WIDTH, :])
        x1_ref[...] = x1
        hq = _rms(x1, g_ref[...]).astype(BF16)
        hq_ref[...] = hq
        cq_ref[...] = _dot(hq, wq_ref[...])

    row = lambda i: (i, 0)
    fixed = lambda i: (0, 0)
    return pl.pallas_call(
        body, name="out_proj", grid=(t_len // tm,),
        in_specs=[pl.BlockSpec((tm, d), row), pl.BlockSpec((tm, FOX_WIDTH), row), pl.BlockSpec((tm, GDN_WIDTH), row),
                  pl.BlockSpec((d, d), fixed), pl.BlockSpec((1, d), fixed), pl.BlockSpec((d, XATTN_WIDTH), fixed)],
        out_specs=[pl.BlockSpec((tm, d), row), pl.BlockSpec((tm, d), row), pl.BlockSpec((tm, XATTN_WIDTH), row)],
        out_shape=[jax.ShapeDtypeStruct((t_len, d), F32), jax.ShapeDtypeStruct((t_len, d), BF16), jax.ShapeDtypeStruct((t_len, XATTN_WIDTH), F32)],
        compiler_params=_cparams(("parallel",)),
    )(x, oa, ob, w_out, g_x, w_cq)


def _out_proj_bwd(dx1, w_out, tm=512):
    t_len, d = dx1.shape
    tm = min(tm, t_len)

    def body(dx_ref, w_ref, o_ref):
        o_ref[...] = _dot(dx_ref[...], w_ref[...], NT)

    return pl.pallas_call(
        body, name="out_proj_bwd", grid=(t_len // tm,),
        in_specs=[pl.BlockSpec((tm, d), lambda i: (i, 0)), pl.BlockSpec((d, d), lambda i: (0, 0))],
        out_specs=pl.BlockSpec((tm, d), lambda i: (i, 0)),
        out_shape=jax.ShapeDtypeStruct((t_len, d), F32),
        compiler_params=_cparams(("parallel",)),
    )(dx1, w_out)


def _mem_kv(mem, g, w_ckv, tm=256):
    t_len, d = mem.shape
    tm = min(tm, t_len)

    def body(x_ref, g_ref, w_ref, h_ref, o_ref):
        h = _rms(x_ref[...], g_ref[...]).astype(BF16)
        h_ref[...] = h
        o_ref[...] = _dot(h, w_ref[...])

    row = lambda i: (i, 0)
    fixed = lambda i: (0, 0)
    return pl.pallas_call(
        body, name="mem_kv", grid=(t_len // tm,),
        in_specs=[pl.BlockSpec((tm, d), row), pl.BlockSpec((1, d), fixed), pl.BlockSpec((d, 2 * XATTN_WIDTH), fixed)],
        out_specs=[pl.BlockSpec((tm, d), row), pl.BlockSpec((tm, 2 * XATTN_WIDTH), row)],
        out_shape=[jax.ShapeDtypeStruct((t_len, d), BF16), jax.ShapeDtypeStruct((t_len, 2 * XATTN_WIDTH), F32)],
        compiler_params=_cparams(("parallel",)),
    )(mem, g, w_ckv)


def _mem_kv_bwd(dckv, mem, g, w_ckv, tm=256):
    t_len, d = mem.shape
    tm = min(tm, t_len)

    def body(d_ref, x_ref, g_ref, w_ref, dg_ref):
        @pl.when(pl.program_id(0) == 0)
        def _():
            dg_ref[...] = jnp.zeros_like(dg_ref)

        dh = _dot(d_ref[...], w_ref[...], NT)
        _, dg = _rms_bwd(x_ref[...], g_ref[...], dh)
        dg_ref[...] += dg

    row = lambda i: (i, 0)
    fixed = lambda i: (0, 0)
    return pl.pallas_call(
        body, name="mem_kv_bwd", grid=(t_len // tm,),
        in_specs=[pl.BlockSpec((tm, 2 * XATTN_WIDTH), row), pl.BlockSpec((tm, d), row), pl.BlockSpec((1, d), fixed),
                  pl.BlockSpec((d, 2 * XATTN_WIDTH), fixed)],
        out_specs=pl.BlockSpec((1, d), fixed),
        out_shape=jax.ShapeDtypeStruct((1, d), F32),
        compiler_params=_cparams(("arbitrary",)),
    )(dckv, mem, g, w_ckv)


def _xattn_probs(qn, kn):
    s = _dot(qn, kn, NT) * (XATTN_HEAD_DIM ** -0.5)
    p = jnp.exp(s - jnp.max(s, axis=-1, keepdims=True))
    return p / jnp.sum(p, axis=-1, keepdims=True)


def _xattn_fwd(cq, ckv, x1, gq, gk, w_co, g_mlp, n_batch, s_len, m_len, tq=256):
    d = x1.shape[1]
    tq = min(tq, s_len)
    nq = s_len // tq
    hd = XATTN_HEAD_DIM

    def body(cq_ref, kv_ref, x1_ref, gq_ref, gk_ref, wo_ref, gm_ref, co_ref, x2_ref, hf_ref):
        outs = []
        for h in range(XATTN_HEADS):
            qn = _rms(cq_ref[:, h * hd:(h + 1) * hd], gq_ref[...])
            kn = _rms(kv_ref[:, h * hd:(h + 1) * hd], gk_ref[...])
            p = _xattn_probs(qn, kn)
            outs.append(_dot(p, kv_ref[:, XATTN_WIDTH + h * hd:XATTN_WIDTH + (h + 1) * hd]).astype(BF16))
        x2 = x1_ref[...]
        for h in range(XATTN_HEADS):
            co_ref[:, h * hd:(h + 1) * hd] = outs[h]
            x2 = x2 + _dot(outs[h], wo_ref[h * hd:(h + 1) * hd, :])
        x2_ref[...] = x2
        hf_ref[...] = _rms(x2, gm_ref[...]).astype(BF16)

    row = lambda b, i: (b * nq + i, 0)
    fixed = lambda b, i: (0, 0)
    t_len = n_batch * s_len
    return pl.pallas_call(
        body, name="xattn_fwd", grid=(n_batch, nq),
        in_specs=[pl.BlockSpec((tq, XATTN_WIDTH), row), pl.BlockSpec((m_len, 2 * XATTN_WIDTH), lambda b, i: (b, 0)),
                  pl.BlockSpec((tq, d), row), pl.BlockSpec((1, hd), fixed), pl.BlockSpec((1, hd), fixed),
                  pl.BlockSpec((XATTN_WIDTH, d), fixed), pl.BlockSpec((1, d), fixed)],
        out_specs=[pl.BlockSpec((tq, XATTN_WIDTH), row), pl.BlockSpec((tq, d), row), pl.BlockSpec((tq, d), row)],
        out_shape=[jax.ShapeDtypeStruct((t_len, XATTN_WIDTH), BF16), jax.ShapeDtypeStruct((t_len, d), F32),
                   jax.ShapeDtypeStruct((t_len, d), BF16)],
        compiler_params=_cparams(("parallel", "parallel")),
    )(cq, ckv, x1, gq, gk, w_co, g_mlp)


def _xattn_bwd(dx2, cq, ckv, x1, gq, gk, w_co, g_x, w_cq, n_batch, s_len, m_len, tq=256):
    d = x1.shape[1]
    tq = min(tq, s_len)
    nq = s_len // tq
    hd = XATTN_HEAD_DIM
    scale = XATTN_HEAD_DIM ** -0.5

    def body(dx2_ref, cq_ref, kv_ref, x1_ref, gq_ref, gk_ref, wo_ref, gx_ref, wq_ref,
             dx1_ref, dcq_ref, dkv_ref, dgq_ref, dgk_ref, dgx_ref, dk_acc, dv_acc):
        b = pl.program_id(0)
        i = pl.program_id(1)

        @pl.when((b == 0) & (i == 0))
        def _():
            dgq_ref[...] = jnp.zeros_like(dgq_ref)
            dgk_ref[...] = jnp.zeros_like(dgk_ref)
            dgx_ref[...] = jnp.zeros_like(dgx_ref)

        @pl.when(i == 0)
        def _():
            dk_acc[...] = jnp.zeros_like(dk_acc)
            dv_acc[...] = jnp.zeros_like(dv_acc)

        dx2 = dx2_ref[...]
        dhq = jnp.zeros((tq, d), F32)
        for h in range(XATTN_HEADS):
            sl = slice(h * hd, (h + 1) * hd)
            q = cq_ref[:, sl]
            qn = _rms(q, gq_ref[...])
            kn = _rms(kv_ref[:, sl], gk_ref[...])
            v = kv_ref[:, XATTN_WIDTH + h * hd:XATTN_WIDTH + (h + 1) * hd]
            p = _xattn_probs(qn, kn)
            dco = _dot(dx2, wo_ref[sl, :], NT)
            dv_acc[:, sl] += _dot(p, dco, TN)
            dp = _dot(dco, v, NT)
            ds = p * (dp - jnp.sum(dp * p, axis=-1, keepdims=True))
            dqn = _dot(ds, kn) * scale
            dk_acc[:, sl] += _dot(ds, qn, TN) * scale
            dq, dgq = _rms_bwd(q, gq_ref[...], dqn)
            dgq_ref[...] += dgq
            dqb = dq.astype(BF16)
            dcq_ref[:, sl] = dqb
            dhq = dhq + _dot(dqb, wq_ref[:, sl], NT)
        dxn, dgx = _rms_bwd(x1_ref[...], gx_ref[...], dhq)
        dgx_ref[...] += dgx
        dx1_ref[...] = dx2 + dxn

        @pl.when(i == nq - 1)
        def _():
            for h in range(XATTN_HEADS):
                sl = slice(h * hd, (h + 1) * hd)
                dk, dgk = _rms_bwd(kv_ref[:, sl], gk_ref[...], dk_acc[:, sl])
                dgk_ref[...] += dgk
                dkv_ref[:, sl] = dk.astype(BF16)
                dkv_ref[:, XATTN_WIDTH + h * hd:XATTN_WIDTH + (h + 1) * hd] = dv_acc[:, sl].astype(BF16)

    row = lambda b, i: (b * nq + i, 0)
    fixed = lambda b, i: (0, 0)
    t_len = n_batch * s_len
    return pl.pallas_call(
        body, name="xattn_bwd", grid=(n_batch, nq),
        in_specs=[pl.BlockSpec((tq, d), row), pl.BlockSpec((tq, XATTN_WIDTH), row), pl.BlockSpec((m_len, 2 * XATTN_WIDTH), lambda b, i: (b, 0)),
                  pl.BlockSpec((tq, d), row), pl.BlockSpec((1, hd), fixed), pl.BlockSpec((1, hd), fixed),
                  pl.BlockSpec((XATTN_WIDTH, d), fixed), pl.BlockSpec((1, d), fixed), pl.BlockSpec((d, XATTN_WIDTH), fixed)],
        out_specs=[pl.BlockSpec((tq, d), row), pl.BlockSpec((tq, XATTN_WIDTH), row), pl.BlockSpec((m_len, 2 * XATTN_WIDTH), lambda b, i: (b, 0)),
                   pl.BlockSpec((1, hd), fixed), pl.BlockSpec((1, hd), fixed), pl.BlockSpec((1, d), fixed)],
        out_shape=[jax.ShapeDtypeStruct((t_len, d), F32), jax.ShapeDtypeStruct((t_len, XATTN_WIDTH), BF16),
                   jax.ShapeDtypeStruct((n_batch * m_len, 2 * XATTN_WIDTH), BF16),
                   jax.ShapeDtypeStruct((1, hd), F32), jax.ShapeDtypeStruct((1, hd), F32), jax.ShapeDtypeStruct((1, d), F32)],
        scratch_shapes=[pltpu.VMEM((m_len, XATTN_WIDTH), F32), pltpu.VMEM((m_len, XATTN_WIDTH), F32)],
        compiler_params=_cparams(("arbitrary", "arbitrary")),
    )(dx2, cq, ckv, x1, gq, gk, w_co, g_x, w_cq)


def _resident(shape):
    return pl.BlockSpec(shape, lambda *_: (0,) * len(shape), pipeline_mode=pl.Buffered(1))


def _mlp_fwd(hf, x2, target, w1, w2, tm=256, tf=1024):
    t_len, d = x2.shape
    f = w1.shape[1]
    tm, tf = min(tm, t_len), min(tf, f)

    def body(hf_ref, x2_ref, tg_ref, w1_ref, w2_ref, u_ref, a_ref, dy_ref, ls_ref):
        hf_t = hf_ref[...]
        y = x2_ref[...]
        for k in range(f // tf):
            cols = slice(k * tf, (k + 1) * tf)
            u = _dot(hf_t, w1_ref[:, cols])
            u_ref[:, cols] = u
            r = jnp.maximum(u, 0.0)
            a = (r * r).astype(BF16)
            a_ref[:, cols] = a
            y = y + _dot(a, w2_ref[cols, :])
        err = y - tg_ref[...]
        dy_ref[...] = err * (1.0 / d)
        ls_ref[...] = jnp.broadcast_to(jnp.sum(jnp.sum(err * err, axis=-1, keepdims=True) * (1.0 / d), axis=0, keepdims=True), ls_ref.shape)

    row = lambda i: (i, 0)
    return pl.pallas_call(
        body, name="mlp_fwd", grid=(t_len // tm,),
        in_specs=[pl.BlockSpec((tm, d), row), pl.BlockSpec((tm, d), row), pl.BlockSpec((tm, d), row), _resident((d, f)), _resident((f, d))],
        out_specs=[pl.BlockSpec((tm, f), row), pl.BlockSpec((tm, f), row), pl.BlockSpec((tm, d), row),
                   pl.BlockSpec((1, 8, LANES), lambda i: (i, 0, 0))],
        out_shape=[jax.ShapeDtypeStruct((t_len, f), F32), jax.ShapeDtypeStruct((t_len, f), BF16), jax.ShapeDtypeStruct((t_len, d), F32),
                   jax.ShapeDtypeStruct((t_len // tm, 8, LANES), F32)],
        compiler_params=_cparams(("parallel",)),
    )(hf, x2, target, w1, w2)


def _mlp_bwd(dy, u, x2, g, w1, w2, tm=256, tf=1024):
    t_len, d = x2.shape
    f = w1.shape[1]
    tm, tf = min(tm, t_len), min(tf, f)

    def body(dy_ref, u_ref, x2_ref, g_ref, w1_ref, w2_ref, du_ref, dx2_ref, dg_ref):
        @pl.when(pl.program_id(0) == 0)
        def _():
            dg_ref[...] = jnp.zeros_like(dg_ref)

        dy_t = dy_ref[...]
        dyb = dy_t.astype(BF16)
        dhf = jnp.zeros((tm, d), F32)
        for k in range(f // tf):
            cols = slice(k * tf, (k + 1) * tf)
            da = _dot(dyb, w2_ref[cols, :], NT)
            du = (da * (2.0 * jnp.maximum(u_ref[:, cols], 0.0))).astype(BF16)
            du_ref[:, cols] = du
            dhf = dhf + _dot(du, w1_ref[:, cols], NT)
        dxn, dg = _rms_bwd(x2_ref[...], g_ref[...], dhf)
        dx2_ref[...] = dy_t + dxn
        dg_ref[...] += dg

    row = lambda i: (i, 0)
    fixed = lambda i: (0, 0)
    return pl.pallas_call(
        body, name="mlp_bwd", grid=(t_len // tm,),
        in_specs=[pl.BlockSpec((tm, d), row), pl.BlockSpec((tm, f), row), pl.BlockSpec((tm, d), row), pl.BlockSpec((1, d), fixed),
                  _resident((d, f)), _resident((f, d))],
        out_specs=[pl.BlockSpec((tm, f), row), pl.BlockSpec((tm, d), row), pl.BlockSpec((1, d), fixed)],
        out_shape=[jax.ShapeDtypeStruct((t_len, f), BF16), jax.ShapeDtypeStruct((t_len, d), F32), jax.ShapeDtypeStruct((1, d), F32)],
        compiler_params=_cparams(("arbitrary",)),
    )(dy, u, x2, g, w1, w2)


def _pad_lanes(v, offset=0, width=LANES):
    return jnp.zeros((1, width), F32).at[:, offset:offset + v.shape[1]].set(v)


def _col(v, offset=0, rows=SM_ROWS):
    return jnp.zeros((rows, 1), F32).at[offset:offset + v.shape[1], 0].set(v[0])


LATE_WEIGHTS = ("w_out", "w_cq", "w_ckv", "w_co", "w_mlp1", "w_mlp2")
GRAD_GROUPS = (("w_mlp2", "w_mlp1"), ("w_co", "w_cq", "w_ckv", "w_out"), ("w_in", "gdn_conv_w"))


def _local_step(x, mem, target, norm_mix_g, w_in, fox_qnorm_g, fox_knorm_g, fox_f_bias, fox_onorm_g, gdn_conv_w, gdn_A_log,
                gdn_dt_bias, gdn_onorm_g, norm_xattn_g, mem_norm_g, xattn_qnorm_g, xattn_knorm_g, norm_mlp_g,
                late_weights, grads_ready=None, first_token=0.0):
    if grads_ready is None:
        grads_ready = lambda group: 0.0
    n_batch, s_len, d = x.shape
    m_len = mem.shape[1]
    t_len = n_batch * s_len
    tq = min(FOX_BLOCK, s_len)
    nq = s_len // tq
    n_chunks = s_len // GDN_CHUNK
    x2d = x.reshape(t_len, d)

    wp = jnp.concatenate([w_in[:, 0:1536], w_in[:, 1544:3080], w_in[:, 3088:3600], w_in[:, 1536:1544], w_in[:, 3080:3088],
                          jnp.zeros((d, P_DIM - 3600), BF16)], axis=1)
    wst = jnp.concatenate([w_in[:, 1536:1544], w_in[:, 3080:3088]], axis=1).T
    conv_w = jnp.concatenate([gdn_conv_w, jnp.zeros((8 - CONV_WIDTH, gdn_conv_w.shape[1]), F32)], axis=0)
    bias_col = _col(fox_f_bias, SM_F)
    gq2, gk2, go2 = (jnp.tile(g, (1, 2)) for g in (fox_qnorm_g, fox_knorm_g, fox_onorm_g))
    a_c, dt_c = _pad_lanes(gdn_A_log, SM_A), _pad_lanes(gdn_dt_bias, SM_A)
    a_r, dt_r = _col(gdn_A_log, SM_A), _col(gdn_dt_bias, SM_A)

    h1, pfox, pgdn, pz, sm, smt = _in_proj(x2d, norm_mix_g + first_token, wp, wst)
    c_rows = _fox_cum(smt, bias_col, n_batch, s_len)
    cb = c_rows.reshape(SM_ROWS, n_batch, nq, tq).transpose(1, 2, 0, 3)
    pf3 = pfox.reshape(n_batch, s_len, 1536)
    o_fox, oa, lse = _fox_fwd(pf3, cb, gq2, gk2, go2, tq)
    pg3 = pgdn.reshape(n_batch, s_len, 1536)
    qkvn = _gdn_pre(pg3, conv_w)
    z3 = pz.reshape(n_batch, s_len, GDN_WIDTH)
    smc = sm.reshape(n_batch, s_len, LANES)
    smr = smt.reshape(SM_ROWS, n_batch * n_chunks, GDN_CHUNK).transpose(1, 0, 2)
    ob, states = _gdn_fwd(qkvn, z3, smc, smr, a_c, dt_c, a_r, dt_r, gdn_onorm_g)
    oa2, ob2 = oa.reshape(t_len, FOX_WIDTH), ob.reshape(t_len, GDN_WIDTH)
    late = late_weights(ob2)
    w_out, w_cq, w_ckv, w_co, w_mlp1, w_mlp2 = (late[k] for k in LATE_WEIGHTS)
    x1, hq, cq = _out_proj(x2d, oa2, ob2, w_out, norm_xattn_g, w_cq)
    mem2d = mem.reshape(n_batch * m_len, d)
    hm, ckv = _mem_kv(mem2d, mem_norm_g, w_ckv)
    co, x2, hf = _xattn_fwd(cq, ckv, x1, xattn_qnorm_g, xattn_knorm_g, w_co, norm_mlp_g, n_batch, s_len, m_len)
    u, a_act, dy, loss_tiles = _mlp_fwd(hf, x2, target.reshape(t_len, d), w_mlp1, w_mlp2)
    loss = 0.5 * jnp.sum(loss_tiles[:, 0, 0])

    grads = {}
    du, dx2, grads["norm_mlp_g"] = _mlp_bwd(dy, u, x2, norm_mlp_g, w_mlp1, w_mlp2)
    grads["w_mlp2"] = _wgrad(a_act, dy, "wgrad_mlp2")
    grads["w_mlp1"] = _wgrad(hf, du, "wgrad_mlp1")
    token = grads_ready({k: grads[k] for k in GRAD_GROUPS[0]})
    grads["w_co"] = _wgrad(co, dx2, "wgrad_co")
    dx1, dcq, dckv, grads["xattn_qnorm_g"], grads["xattn_knorm_g"], grads["norm_xattn_g"] = _xattn_bwd(
        dx2, cq, ckv, x1, xattn_qnorm_g + token, xattn_knorm_g, w_co, norm_xattn_g, w_cq, n_batch, s_len, m_len)
    grads["w_cq"] = _wgrad(hq, dcq, "wgrad_cq")
    grads["w_ckv"] = _wgrad(hm, dckv, "wgrad_ckv")
    grads["mem_norm_g"] = _mem_kv_bwd(dckv, mem2d, mem_norm_g, w_ckv)
    grads["w_out"] = _wgrad(jnp.concatenate([oa2, ob2], axis=1), dx1, "wgrad_out")
    token = grads_ready({k: grads[k] for k in GRAD_GROUPS[1]})
    dcat = _out_proj_bwd(dx1, w_out)
    dcat3 = dcat.reshape(n_batch, s_len, d)

    dqkvn, dz, dsmc, dsmr, dac, ddc, dar, ddr, grads["gdn_onorm_g"] = _gdn_bwd(
        qkvn, z3, smc, smr, a_c, dt_c, a_r, dt_r, gdn_onorm_g + token, states, dcat3)
    grads["gdn_A_log"] = dac[:, SM_A:SM_A + GDN_HEADS] + dar[SM_A:SM_A + GDN_HEADS, 0][None, :]
    grads["gdn_dt_bias"] = ddc[:, SM_A:SM_A + GDN_HEADS] + ddr[SM_A:SM_A + GDN_HEADS, 0][None, :]
    dpg, dconv = _gdn_pre_bwd(pg3, conv_w, dqkvn)
    grads["gdn_conv_w"] = dconv[0:CONV_WIDTH]

    dq, dk, dv, dcb, dgq, dgk, dgo = _fox_bwd(pf3, cb, gq2, gk2, go2, o_fox, lse, dcat3[:, :, 0:FOX_WIDTH], tq)
    fold = lambda g: g[:, 0:FOX_HEAD_DIM] + g[:, FOX_HEAD_DIM:LANES]
    grads["fox_qnorm_g"], grads["fox_knorm_g"], grads["fox_onorm_g"] = fold(dgq), fold(dgk), fold(dgo)
    dc8 = dcb[:, :, :, 0:2, :].transpose(1, 3, 0, 2, 4).reshape(FOX_HEADS, t_len)
    dc_rows = jnp.concatenate([dc8, jnp.zeros((SM_ROWS - FOX_HEADS, t_len), F32)], axis=0)
    dl_rows, dbias = _fox_cum_bwd(dc_rows, smt, bias_col, n_batch, s_len)
    grads["fox_f_bias"] = dbias[SM_F:SM_F + FOX_HEADS, 0][None, :]
    dsm_rows = jnp.concatenate([dl_rows[0:SM_B], dsmr.transpose(1, 0, 2).reshape(SM_ROWS, t_len)[SM_B:SM_ROWS]], axis=0)

    dproj = jnp.concatenate([dq.reshape(t_len, FOX_WIDTH), dk.reshape(t_len, FOX_WIDTH), dv.reshape(t_len, FOX_WIDTH),
                             dpg.reshape(t_len, 1536), dz.reshape(t_len, GDN_WIDTH), dsmc.reshape(t_len, LANES).astype(BF16)], axis=1)
    dwp = _wgrad(h1, dproj, "wgrad_in", bk=512, bn=P_DIM)
    dwst = _rows_matmul(dsm_rows, h1, "wgrad_in_rows")
    dw_small = dwp[:, P_SMALL:P_SMALL + SM_ROWS] + dwst.T
    grads["w_in"] = jnp.concatenate([dwp[:, 0:1536], dw_small[:, 0:8], dwp[:, 1536:3072], dw_small[:, 8:16], dwp[:, 3072:3584]], axis=1)
    token = grads_ready({k: grads[k] for k in GRAD_GROUPS[2]})
    grad_x, grads["norm_mix_g"] = _in_proj_bwd(dproj, dsm_rows, x2d, norm_mix_g + token, wp, wst, dx1)
    return loss, grad_x.reshape(n_batch, s_len, d), grads


MESH_ID = pl.DeviceIdType.MESH
ANY_SPEC = pl.BlockSpec(memory_space=pl.ANY)


def _place():
    x, y, c = lax.axis_index("x"), lax.axis_index("y"), lax.axis_index("c")
    return x, y, c, [(1 - x, y), (x, 1 - y), (1 - x, 1 - y)]


def _all_gather_body(n, ins, outs, send_sems, recv_sems, local_sems):
    x, y, c, chips = _place()
    me, sibling = (x, y, c), (x, y, 1 - c)

    def copy(a, k, block, to, src=None):
        dst = outs[a].at[4 * block[0] + 2 * block[1] + block[2]]
        return pltpu.make_async_remote_copy(src_ref=dst if src is None else src, dst_ref=dst, send_sem=send_sems.at[a, k],
                                            recv_sem=recv_sems.at[a, k], device_id=to, device_id_type=MESH_ID)

    mine = [pltpu.make_async_copy(ins[a], outs[a].at[4 * x + 2 * y + c], local_sems.at[a]) for a in range(n)]
    for cp in mine:
        cp.start()
    first = []
    for a in range(n):
        first.append(copy(a, 0, me, sibling, src=ins[a]))
        first += [copy(a, 1 + j, me, (*chip, c), src=ins[a]) for j, chip in enumerate(chips)]
    for cp in first:
        cp.start()
    passed = []
    for j, chip in enumerate(chips):
        for a in range(n):
            copy(a, 1 + j, (*chip, c), me).wait_recv()
            fwd = copy(a, 4 + j, (*chip, c), sibling)
            fwd.start()
            passed.append(fwd)
    for a in range(n):
        copy(a, 0, sibling, me).wait_recv()
        for j, chip in enumerate(chips):
            copy(a, 4 + j, (*chip, 1 - c), me).wait_recv()
    for cp in first + passed:
        cp.wait_send()
    for cp in mine:
        cp.wait()


def _all_gather_hbm(arrs, name):
    n = len(arrs)

    def body(*refs):
        _all_gather_body(n, refs[:n], refs[n:2 * n], *refs[2 * n:])

    return pl.pallas_call(
        body, name=name, in_specs=[ANY_SPEC] * n, out_specs=[ANY_SPEC] * n,
        out_shape=[jax.ShapeDtypeStruct((N_DEV,) + a.shape, a.dtype) for a in arrs],
        scratch_shapes=[pltpu.SemaphoreType.DMA((n, 7)), pltpu.SemaphoreType.DMA((n, 7)), pltpu.SemaphoreType.DMA((n,))],
    )(*arrs)


def _pair_exchange(arrs, name):
    n = len(arrs)

    def body(*refs):
        ins, outs = refs[:n], refs[n:2 * n]
        send_sems, recv_sems = refs[2 * n:]
        x, y, c, _ = _place()
        copies = []
        for a in range(n):
            for chip in range(4):
                copies.append(pltpu.make_async_remote_copy(
                    src_ref=ins[a].at[2 * chip + (1 - c)], dst_ref=outs[a].at[chip], send_sem=send_sems.at[a, chip],
                    recv_sem=recv_sems.at[a, chip], device_id=(x, y, 1 - c), device_id_type=MESH_ID))
        for cp in copies:
            cp.start()
        for cp in copies:
            cp.wait()

    return pl.pallas_call(
        body, name=name, in_specs=[ANY_SPEC] * n, out_specs=[ANY_SPEC] * n,
        out_shape=[jax.ShapeDtypeStruct((4,) + a.shape[1:], a.dtype) for a in arrs],
        scratch_shapes=[pltpu.SemaphoreType.DMA((n, 4)), pltpu.SemaphoreType.DMA((n, 4))],
    )(*arrs)


HBM_SPEC = pl.BlockSpec(memory_space=pltpu.HBM)
SEM_SPEC = pl.BlockSpec(memory_space=pltpu.SEMAPHORE)
DATAFLOW = pltpu.SideEffectType.DATAFLOW_SIDE_EFFECTING


def _in_hbm(arrs):
    return [pltpu.with_memory_space_constraint(a, pltpu.HBM) for a in arrs]


def _copies_start(name, srcs, lands, make_copies):
    n = len(srcs)
    n_copies = len(make_copies(srcs, lands, None, None)[0])

    def body(*refs):
        send_sems, recv_sems = refs[2 * n], refs[2 * n + 1]
        for row in make_copies(refs[:n], refs[n:2 * n], send_sems, recv_sems):
            for cp in row:
                cp.start()
        refs[-1][...] = jnp.zeros_like(refs[-1])

    sems = pltpu.SemaphoreType.DMA((n * n_copies,))
    thru = [pltpu.HBM(a.shape, a.dtype) for a in list(srcs) + list(lands)]
    res = pl.pallas_call(
        body, name=name, in_specs=[HBM_SPEC] * (2 * n),
        out_specs=(SEM_SPEC, SEM_SPEC, *[HBM_SPEC] * (2 * n), pl.BlockSpec(memory_space=pltpu.VMEM)),
        out_shape=(sems, sems, *thru, jax.ShapeDtypeStruct((8, LANES), F32)),
        input_output_aliases={i: 2 + i for i in range(2 * n)},
        compiler_params=pltpu.CompilerParams(has_side_effects=DATAFLOW),
    )(*_in_hbm(list(srcs) + list(lands)))
    return res[0], res[1], list(res[2:2 + n]), list(res[2 + n:2 + 2 * n]), res[-1]


def _copies_wait(name, send_sems, recv_sems, srcs, lands, after, make_copies):
    n = len(srcs)

    def body(*refs):
        for row in make_copies(refs[:n], refs[n:2 * n], refs[2 * n], refs[2 * n + 1]):
            for cp in row:
                cp.wait_send()
                cp.wait_recv()

    res = pl.pallas_call(
        body, name=name, in_specs=[HBM_SPEC] * (2 * n) + [SEM_SPEC, SEM_SPEC, ANY_SPEC],
        out_specs=tuple([HBM_SPEC] * (2 * n)),
        out_shape=tuple(pltpu.HBM(a.shape, a.dtype) for a in list(srcs) + list(lands)),
        input_output_aliases={i: i for i in range(2 * n)},
        compiler_params=pltpu.CompilerParams(has_side_effects=DATAFLOW),
    )(*srcs, *lands, send_sems, recv_sems, after)
    return list(res[:n]), list(res[n:])


def _gather_copies(srcs, lands, send_sems, recv_sems):
    if send_sems is None:
        return [[None] * 7]
    x, y, c, _ = _place()
    rows = []
    for a in range(len(srcs)):
        row = []
        for k in range(7):
            r = k + 1
            to = (1 - x if r & 4 else x, 1 - y if r & 2 else y, 1 - c if r & 1 else c)
            row.append(pltpu.make_async_remote_copy(
                src_ref=srcs[a], dst_ref=lands[a].at[4 * x + 2 * y + c], send_sem=send_sems.at[7 * a + k], recv_sem=recv_sems.at[7 * a + k],
                device_id=to, device_id_type=MESH_ID))
        rows.append(row)
    return rows


def _chip_copies(srcs, lands, send_sems, recv_sems):
    if send_sems is None:
        return [[None] * 3]
    x, y, c, chips = _place()
    return [[pltpu.make_async_remote_copy(
        src_ref=srcs[a].at[2 * chip[0] + chip[1]], dst_ref=lands[a].at[j], send_sem=send_sems.at[3 * a + j], recv_sem=recv_sems.at[3 * a + j],
        device_id=(*chip, c), device_id_type=MESH_ID) for j, chip in enumerate(chips)] for a in range(len(srcs))]


def _all_gather_vmem(block, name):
    def body(in_ref, out_ref, send_sems, recv_sems, local_sems):
        _all_gather_body(1, [in_ref], [out_ref], send_sems, recv_sems, local_sems)

    vmem = pl.BlockSpec(memory_space=pltpu.VMEM)
    return pl.pallas_call(
        body, name=name, in_specs=[vmem], out_specs=vmem,
        out_shape=jax.ShapeDtypeStruct((N_DEV,) + block.shape, block.dtype),
        scratch_shapes=[pltpu.SemaphoreType.DMA((1, 7)), pltpu.SemaphoreType.DMA((1, 7)), pltpu.SemaphoreType.DMA((1,))],
    )(block)


def _row_tile(rows, cols):
    if rows <= 256:
        return rows
    return 256 if cols <= 512 else 128


def _pair_sum(core, own, got, name):
    _, rows, cols = own.shape
    tr = _row_tile(rows, cols)

    def body(c_ref, own_ref, got_ref, o_ref):
        o_ref[0] = own_ref[0] + got_ref[0]

    return pl.pallas_call(
        body, name=name,
        grid_spec=pltpu.PrefetchScalarGridSpec(
            num_scalar_prefetch=1, grid=(4, rows // tr),
            in_specs=[pl.BlockSpec((1, tr, cols), lambda k, i, c: (2 * k + c[0], i, 0)),
                      pl.BlockSpec((1, tr, cols), lambda k, i, c: (k, i, 0))],
            out_specs=pl.BlockSpec((1, tr, cols), lambda k, i, c: (k, i, 0))),
        out_shape=jax.ShapeDtypeStruct((4, rows, cols), F32),
        compiler_params=_cparams(("parallel", "parallel")),
    )(core, own, got)


def _adamw(w, g, m, v):
    m_new = ADAM_B1 * m + (1.0 - ADAM_B1) * g
    v_new = ADAM_B2 * v + (1.0 - ADAM_B2) * (g * g)
    m_hat = m_new / (1.0 - ADAM_B1 ** ADAM_STEP)
    v_hat = v_new / (1.0 - ADAM_B2 ** ADAM_STEP)
    delta = -ADAM_LR * (m_hat / (jnp.sqrt(v_hat) + ADAM_EPS) + ADAM_WD * w)
    return delta, m_new, v_new


def _sum_adam(chip, sums, parts, w, m, v, name):
    n_parts, rows, cols = parts.shape
    tr = _row_tile(rows, cols)

    def body(chip_ref, *refs):
        if sums is not None:
            g = refs[0][0].astype(F32)
            refs = refs[1:]
        p_ref, w_ref, m_ref, v_ref, g_ref, d_ref, mo_ref, vo_ref = refs
        for k in range(n_parts):
            g = p_ref[k].astype(F32) if (k == 0 and sums is None) else g + p_ref[k].astype(F32)
        g_ref[...] = g
        d_ref[...], mo_ref[...], vo_ref[...] = _adamw(w_ref[...], g, m_ref[...], v_ref[...])

    tile = pl.BlockSpec((tr, cols), lambda i, ch: (i, 0))
    out = jax.ShapeDtypeStruct((rows, cols), F32)
    own = [] if sums is None else [pl.BlockSpec((1, tr, cols), lambda i, ch: (ch[0], i, 0))]
    return pl.pallas_call(
        body, name=name,
        grid_spec=pltpu.PrefetchScalarGridSpec(
            num_scalar_prefetch=1, grid=(rows // tr,),
            in_specs=own + [pl.BlockSpec((n_parts, tr, cols), lambda i, ch: (0, i, 0)), tile, tile, tile],
            out_specs=[tile, tile, tile, tile]),
        out_shape=[out, out, out, out],
        compiler_params=_cparams(("parallel",)),
    )(chip, *([] if sums is None else [sums]), parts, w, m, v)


SHARDED = ("w_in", "gdn_conv_w", "w_out", "w_cq", "w_ckv", "w_co", "w_mlp1", "w_mlp2")
COLUMN_SHARDED = ("w_in", "gdn_conv_w", "w_co", "w_mlp1")
REPLICATED = ("norm_mix_g", "fox_qnorm_g", "fox_knorm_g", "fox_f_bias", "fox_onorm_g", "gdn_A_log", "gdn_dt_bias", "gdn_onorm_g",
              "norm_xattn_g", "mem_norm_g", "xattn_qnorm_g", "xattn_knorm_g", "norm_mlp_g")
WEIGHTS = ("norm_mix_g", "w_in", "fox_qnorm_g", "fox_knorm_g", "fox_f_bias", "fox_onorm_g", "gdn_conv_w", "gdn_A_log", "gdn_dt_bias",
           "gdn_onorm_g", "w_out", "norm_xattn_g", "mem_norm_g", "w_cq", "w_ckv", "xattn_qnorm_g", "xattn_knorm_g", "w_co",
           "norm_mlp_g", "w_mlp1", "w_mlp2")
PACK_ROWS = 16
LOSS_ROW = len(REPLICATED)


def _whole(name, gathered):
    if name in COLUMN_SHARDED:
        return gathered.transpose(1, 0, 2).reshape(gathered.shape[1], N_DEV * gathered.shape[2])
    return gathered.reshape(N_DEV * gathered.shape[1], gathered.shape[2])


def _blocks(name, whole):
    if name in COLUMN_SHARDED:
        rows, cols = whole.shape
        return whole.reshape(rows, N_DEV, cols // N_DEV).transpose(1, 0, 2)
    return whole.reshape(N_DEV, whole.shape[0] // N_DEV, whole.shape[1])


def _pack(vals, fill=0.0):
    rows = [jnp.pad(vals[k], ((0, 0), (0, D_MODEL - vals[k].shape[1])), constant_values=fill) for k in REPLICATED]
    rows.append(jnp.full((PACK_ROWS - len(rows), D_MODEL), fill, F32))
    return jnp.concatenate(rows, axis=0)


def kernel(x, mem, norm_mix_g, w_in, fox_qnorm_g, fox_knorm_g, fox_f_bias, fox_onorm_g, gdn_conv_w, gdn_A_log, gdn_dt_bias, gdn_onorm_g, w_out, norm_xattn_g, mem_norm_g, w_cq, w_ckv, xattn_qnorm_g, xattn_knorm_g, w_co, norm_mlp_g, w_mlp1, w_mlp2, loss_target, m_norm_mix_g, m_w_in, m_fox_qnorm_g, m_fox_knorm_g, m_fox_f_bias, m_fox_onorm_g, m_gdn_conv_w, m_gdn_A_log, m_gdn_dt_bias, m_gdn_onorm_g, m_w_out, m_norm_xattn_g, m_mem_norm_g, m_w_cq, m_w_ckv, m_xattn_qnorm_g, m_xattn_knorm_g, m_w_co, m_norm_mlp_g, m_w_mlp1, m_w_mlp2, v_norm_mix_g, v_w_in, v_fox_qnorm_g, v_fox_knorm_g, v_fox_f_bias, v_fox_onorm_g, v_gdn_conv_w, v_gdn_A_log, v_gdn_dt_bias, v_gdn_onorm_g, v_w_out, v_norm_xattn_g, v_mem_norm_g, v_w_cq, v_w_ckv, v_xattn_qnorm_g, v_xattn_knorm_g, v_w_co, v_norm_mlp_g, v_w_mlp1, v_w_mlp2):
    given = dict(locals())
    w = {k: given[k] for k in WEIGHTS}
    m = {k: given["m_" + k] for k in WEIGHTS}
    v = {k: given["v_" + k] for k in WEIGHTS}

    core = lax.axis_index("c").astype(jnp.int32).reshape(1)
    chip = (2 * lax.axis_index("x") + lax.axis_index("y")).astype(jnp.int32).reshape(1)
    me = 4 * lax.axis_index("x") + 2 * lax.axis_index("y") + lax.axis_index("c")

    shards = {k: w[k][0] if k == "gdn_conv_w" else w[k][0].astype(BF16) for k in SHARDED}
    early = [k for k in SHARDED if k not in LATE_WEIGHTS]
    whole = {k: _whole(k, g) for k, g in zip(early, _all_gather_hbm([shards[k] for k in early], "gather_early"))}
    late_shards = [shards[k] for k in LATE_WEIGHTS]
    late_lands = [lax.empty((N_DEV,) + s.shape, s.dtype) for s in late_shards]
    gather = _copies_start("gather_late_start", late_shards, late_lands, _gather_copies)

    def late_weights(after):
        srcs, lands = _copies_wait("gather_late_wait", gather[0], gather[1], gather[2], gather[3], after, _gather_copies)
        return {k: _whole(k, lax.dynamic_update_slice(land, src[None], (me, 0, 0))) for k, src, land in zip(LATE_WEIGHTS, srcs, lands)}

    pending = []

    def grads_ready(group):
        names = list(group)
        tag = str(len(pending))
        own = [_blocks(k, group[k]) for k in names]
        got = _pair_exchange(own, "grad_pair_exchange_" + tag)
        sums = [_pair_sum(core, o, g, "grad_pair_sum_" + k) for k, o, g in zip(names, own, got)]
        lands = [lax.empty((3,) + s.shape[1:], s.dtype) for s in sums]
        started = _copies_start("grad_chip_start_" + tag, sums, lands, _chip_copies)
        pending.append((names, started))
        return started[4][0, 0]

    small = {k: w[k] for k in REPLICATED}
    loss_local, grad_x, grads = _local_step(x, mem, loss_target, **small, **whole, late_weights=late_weights,
                                            grads_ready=grads_ready, first_token=gather[4][0, 0])

    out_g, out_d, out_m, out_v = {}, {}, {}, {}
    after = grad_x
    for tag, (names, started) in enumerate(pending):
        sums, parts = _copies_wait("grad_chip_wait_" + str(tag), started[0], started[1], started[2], started[3], after, _chip_copies)
        for k, s, p in zip(names, sums, parts):
            res = _sum_adam(chip, s, p, w[k][0], m[k][0], v[k][0], "adam_" + k)
            out_g[k], out_d[k], out_m[k], out_v[k] = (r[None] for r in res)
            after = res[0]

    packed = _pack({k: grads[k] for k in REPLICATED}).at[LOSS_ROW, 0].set(loss_local)
    everyone = _all_gather_vmem(packed, "gather_small")
    res = _sum_adam(chip, None, everyone, _pack(small), _pack({k: m[k] for k in REPLICATED}),
                    _pack({k: v[k] for k in REPLICATED}, fill=1.0), "adam_small")
    for i, k in enumerate(REPLICATED):
        n = w[k].shape[1]
        out_g[k], out_d[k], out_m[k], out_v[k] = (r[i:i + 1, 0:n] for r in res)
    loss = res[0][LOSS_ROW, 0]

    return (loss, grad_x, *[out_g[k] for k in WEIGHTS], *[out_d[k] for k in WEIGHTS], *[out_m[k] for k in WEIGHTS],
            *[out_v[k] for k in WEIGHTS])
```

```python
import functools

import jax
import jax.numpy as jnp
import numpy as np
from jax import lax
from jax.experimental import pallas as pl
from jax.experimental.pallas import tpu as pltpu

F32 = jnp.float32
BF16 = jnp.bfloat16

D_MODEL = 1024
FOX_HEADS = 8
FOX_HEAD_DIM = 64
FOX_WIDTH = 512
GDN_HEADS = 4
GDN_HEAD_DIM = 128
GDN_WIDTH = 512
CONV_WIDTH = 4
GDN_CHUNK = 64
GDN_GROUP = 4
FOX_BLOCK = 512
XATTN_HEADS = 4
XATTN_HEAD_DIM = 128
XATTN_WIDTH = 512
D_FF = 4096
EPS = 1e-6
NEG_INF = -1e30
N_DEV = 8

ADAM_LR = 0.001
ADAM_B1 = 0.9
ADAM_B2 = 0.999
ADAM_EPS = 1e-08
ADAM_WD = 0.01
ADAM_STEP = 10

P_FOX = 0
P_GDN = 1536
P_Z = 3072
P_SMALL = 3584
P_DIM = 3712
SM_F = 0
SM_B = 8
SM_A = 12
SM_ROWS = 16

LANES = 128
VMEM_LIMIT = 56 * 1024 * 1024

NN = (((1,), (0,)), ((), ()))
NT = (((1,), (1,)), ((), ()))
TN = (((0,), (0,)), ((), ()))


def _dot(a, b, dims=NN):
    return lax.dot_general(a.astype(BF16), b.astype(BF16), dims, preferred_element_type=F32)


def _cparams(sem=None):
    kw = dict(vmem_limit_bytes=VMEM_LIMIT)
    if sem is not None:
        kw["dimension_semantics"] = sem
    return pltpu.CompilerParams(**kw)


def _sigmoid(x):
    return 0.5 * (jnp.tanh(0.5 * x) + 1.0)


def _softplus(x):
    return jnp.maximum(x, 0.0) + jnp.log1p(jnp.exp(-jnp.abs(x)))


def _log_sigmoid(x):
    return -_softplus(-x)


def _rms(x, g):
    r = lax.rsqrt(jnp.mean(x * x, axis=-1, keepdims=True) + EPS)
    return x * r * g


def _rms_bwd(x, g, dy):
    r = lax.rsqrt(jnp.mean(x * x, axis=-1, keepdims=True) + EPS)
    xh = x * r
    dg = jnp.sum(dy * xh, axis=0, keepdims=True)
    dyg = dy * g
    dx = r * (dyg - xh * jnp.mean(dyg * xh, axis=-1, keepdims=True))
    return dx, dg


def _pair_stat(t, m0):
    s0 = jnp.sum(jnp.where(m0, t, 0.0), axis=-1, keepdims=True)
    s1 = jnp.sum(jnp.where(m0, 0.0, t), axis=-1, keepdims=True)
    return jnp.where(m0, s0, s1)


def _rms_pair(x, g, m0):
    r = lax.rsqrt(_pair_stat(x * x, m0) * (1.0 / FOX_HEAD_DIM) + EPS)
    return x * r * g


def _rms_pair_bwd(x, g, dy, m0):
    r = lax.rsqrt(_pair_stat(x * x, m0) * (1.0 / FOX_HEAD_DIM) + EPS)
    xh = x * r
    dg = jnp.sum(dy * xh, axis=0, keepdims=True)
    dyg = dy * g
    dx = r * (dyg - xh * (_pair_stat(dyg * xh, m0) * (1.0 / FOX_HEAD_DIM)))
    return dx, dg


@jax.custom_vjp
def _mm_nn(a, b):
    return _dot(a, b, NN)


_mm_nn.defvjp(lambda a, b: (_dot(a, b, NN), (a, b)),
              lambda r, g: (_dot(g, r[1], NT), _dot(r[0], g, TN)))


@jax.custom_vjp
def _mm_nt(a, b):
    return _dot(a, b, NT)


_mm_nt.defvjp(lambda a, b: (_dot(a, b, NT), (a, b)),
              lambda r, g: (_dot(g, r[1], NN), _dot(g, r[0], TN)))


@jax.custom_vjp
def _mm_tn(a, b):
    return _dot(a, b, TN)


_mm_tn.defvjp(lambda a, b: (_dot(a, b, TN), (a, b)),
              lambda r, g: (_dot(r[1], g, NT), _dot(r[0], g, NN)))


def _dot3(a, b, dims):
    ah = a.astype(BF16)
    al = (a - ah.astype(F32)).astype(BF16)
    bh = b.astype(BF16)
    bl = (b - bh.astype(F32)).astype(BF16)
    d = functools.partial(lax.dot_general, dimension_numbers=dims, preferred_element_type=F32)
    return d(ah, bh) + d(ah, bl) + d(al, bh)


def _neumann_inverses(mats):
    c = mats[0].shape[0]
    eye = (lax.broadcasted_iota(jnp.int32, (c, c), 0) == lax.broadcasted_iota(jnp.int32, (c, c), 1)).astype(F32)
    xs = [eye - a for a in mats]
    ps = list(mats)
    k = 2
    while k < c + 1:
        ps = [_dot3(p, p, NN) for p in ps]
        xs = [x + _dot3(x, p, NN) for x, p in zip(xs, ps)]
        k *= 2
    return xs


@jax.custom_vjp
def _unit_lower_inverses(mats):
    return _neumann_inverses(mats)


def _unit_lower_inverses_fwd(mats):
    ts = _neumann_inverses(mats)
    return ts, ts


def _unit_lower_inverses_bwd(ts, gs):
    left = [_dot3(t, g, TN) for t, g in zip(ts, gs)]
    return ([-_dot3(m, t, NT) for m, t in zip(left, ts)],)


_unit_lower_inverses.defvjp(_unit_lower_inverses_fwd, _unit_lower_inverses_bwd)


def _wgrad(a, b, name, bk=1024, bn=1024, bt=512, column_blocks=False):
    t_len, k_len = a.shape
    n_len = b.shape[1]
    bk, bn, bt = min(bk, k_len), min(bn, n_len), min(bt, t_len)
    nt = t_len // bt

    def body(a_ref, b_ref, o_ref, acc_ref):
        t = pl.program_id(2)

        @pl.when(t == 0)
        def _():
            acc_ref[...] = jnp.zeros_like(acc_ref)

        acc_ref[...] += _dot(a_ref[...], b_ref[...], TN)

        @pl.when(t == nt - 1)
        def _():
            o_ref[...] = acc_ref[...].reshape(o_ref.shape)

    if column_blocks:
        out_spec = pl.BlockSpec((1, bk, bn), lambda i, j, t: (j, i, 0))
        out_shape = jax.ShapeDtypeStruct((n_len // bn, k_len, bn), F32)
    else:
        out_spec = pl.BlockSpec((bk, bn), lambda i, j, t: (i, j))
        out_shape = jax.ShapeDtypeStruct((k_len, n_len), F32)
    return pl.pallas_call(
        body, name=name, grid=(k_len // bk, n_len // bn, nt),
        in_specs=[pl.BlockSpec((bt, bk), lambda i, j, t: (t, i)), pl.BlockSpec((bt, bn), lambda i, j, t: (t, j))],
        out_specs=out_spec, out_shape=out_shape,
        scratch_shapes=[pltpu.VMEM((bk, bn), F32)],
        compiler_params=_cparams(("parallel", "parallel", "arbitrary")),
    )(a, b)


def _rows_matmul(a, b, name, bt=512):
    r_len, t_len = a.shape
    n_len = b.shape[1]
    bt = min(bt, t_len)
    nt = t_len // bt

    def body(a_ref, b_ref, o_ref):
        t = pl.program_id(0)

        @pl.when(t == 0)
        def _():
            o_ref[...] = jnp.zeros_like(o_ref)

        o_ref[...] += _dot(a_ref[...], b_ref[...], NN)

    return pl.pallas_call(
        body, name=name, grid=(nt,),
        in_specs=[pl.BlockSpec((r_len, bt), lambda t: (0, t)), pl.BlockSpec((bt, n_len), lambda t: (t, 0))],
        out_specs=pl.BlockSpec((r_len, n_len), lambda t: (0, 0)),
        out_shape=jax.ShapeDtypeStruct((r_len, n_len), F32),
        compiler_params=_cparams(("arbitrary",)),
    )(a, b)


def _in_proj(x, g, wp, wst, tm=256):
    t_len, d = x.shape
    tm = min(tm, t_len)

    def body(x_ref, g_ref, wp_ref, wst_ref, h_ref, fox_ref, gdn_ref, z_ref, sm_ref, smt_ref):
        h = _rms(x_ref[...], g_ref[...]).astype(BF16)
        h_ref[...] = h
        p = _dot(h, wp_ref[...], NN)
        fox_ref[...] = p[:, P_FOX:P_GDN]
        gdn_ref[...] = p[:, P_GDN:P_Z]
        z_ref[...] = p[:, P_Z:P_SMALL]
        sm_ref[...] = p[:, P_SMALL:P_DIM]
        smt_ref[...] = _dot(wst_ref[...], h, NT)

    row = lambda i: (i, 0)
    fixed = lambda i: (0, 0)
    return pl.pallas_call(
        body, name="in_proj", grid=(t_len // tm,),
        in_specs=[pl.BlockSpec((tm, d), row), pl.BlockSpec((1, d), fixed), pl.BlockSpec((d, P_DIM), fixed),
                  pl.BlockSpec((SM_ROWS, d), fixed)],
        out_specs=[pl.BlockSpec((tm, d), row), pl.BlockSpec((tm, 1536), row), pl.BlockSpec((tm, 1536), row),
                   pl.BlockSpec((tm, 512), row), pl.BlockSpec((tm, LANES), row), pl.BlockSpec((SM_ROWS, tm), lambda i: (0, i))],
        out_shape=[jax.ShapeDtypeStruct((t_len, d), BF16), jax.ShapeDtypeStruct((t_len, 1536), F32),
                   jax.ShapeDtypeStruct((t_len, 1536), F32), jax.ShapeDtypeStruct((t_len, 512), F32),
                   jax.ShapeDtypeStruct((t_len, LANES), F32), jax.ShapeDtypeStruct((SM_ROWS, t_len), F32)],
        compiler_params=_cparams(("parallel",)),
    )(x, g, wp, wst)


def _in_proj_bwd(dproj, dsmt, x, g, wp, wst, dx1, tm=256):
    t_len, d = x.shape
    tm = min(tm, t_len)

    def body(dp_ref, dst_ref, x_ref, g_ref, wp_ref, wst_ref, dx1_ref, dx_ref, dg_ref):
        i = pl.program_id(0)
        dh = _dot(dp_ref[...], wp_ref[...], NT) + _dot(dst_ref[...], wst_ref[...], TN)
        dxn, dg = _rms_bwd(x_ref[...], g_ref[...], dh)
        dx_ref[...] = dx1_ref[...] + dxn

        @pl.when(i == 0)
        def _():
            dg_ref[...] = jnp.zeros_like(dg_ref)

        dg_ref[...] += dg

    row = lambda i: (i, 0)
    fixed = lambda i: (0, 0)
    return pl.pallas_call(
        body, name="in_proj_bwd", grid=(t_len // tm,),
        in_specs=[pl.BlockSpec((tm, P_DIM), row), pl.BlockSpec((SM_ROWS, tm), lambda i: (0, i)), pl.BlockSpec((tm, d), row),
                  pl.BlockSpec((1, d), fixed), pl.BlockSpec((d, P_DIM), fixed), pl.BlockSpec((SM_ROWS, d), fixed),
                  pl.BlockSpec((tm, d), row)],
        out_specs=[pl.BlockSpec((tm, d), row), pl.BlockSpec((1, d), fixed)],
        out_shape=[jax.ShapeDtypeStruct((t_len, d), F32), jax.ShapeDtypeStruct((1, d), F32)],
        compiler_params=_cparams(("arbitrary",)),
    )(dproj, dsmt, x, g, wp, wst, dx1)


def _fox_cum(smt, bias_col, n_batch, s_len, ck=256):
    ck = min(ck, s_len)

    def body(s_ref, b_ref, c_ref):
        tri = (lax.broadcasted_iota(jnp.int32, (ck, ck), 0) <= lax.broadcasted_iota(jnp.int32, (ck, ck), 1)).astype(F32)
        carry = jnp.zeros((SM_ROWS, 1), F32)
        for r in range(s_len // ck):
            ls = _log_sigmoid(s_ref[:, r * ck:(r + 1) * ck] + b_ref[...])
            c = jnp.dot(ls, tri, precision=lax.Precision.HIGHEST, preferred_element_type=F32) + carry
            c_ref[:, r * ck:(r + 1) * ck] = c
            carry = c[:, ck - 1:ck]

    return pl.pallas_call(
        body, name="fox_cum", grid=(n_batch,),
        in_specs=[pl.BlockSpec((SM_ROWS, s_len), lambda b: (0, b)), pl.BlockSpec((SM_ROWS, 1), lambda b: (0, 0))],
        out_specs=pl.BlockSpec((SM_ROWS, s_len), lambda b: (0, b)),
        out_shape=jax.ShapeDtypeStruct(smt.shape, F32),
        compiler_params=_cparams(("parallel",)),
    )(smt, bias_col)


def _fox_cum_bwd(dc, smt, bias_col, n_batch, s_len, ck=256):
    ck = min(ck, s_len)
    nr = s_len // ck

    def body(dc_ref, s_ref, b_ref, dl_ref, db_ref):
        b = pl.program_id(0)
        tri = (lax.broadcasted_iota(jnp.int32, (ck, ck), 0) >= lax.broadcasted_iota(jnp.int32, (ck, ck), 1)).astype(F32)
        carry = jnp.zeros((SM_ROWS, 1), F32)
        tot = jnp.zeros((SM_ROWS, 1), F32)
        for r in reversed(range(nr)):
            sl = slice(r * ck, (r + 1) * ck)
            dls = jnp.dot(dc_ref[:, sl], tri, precision=lax.Precision.HIGHEST, preferred_element_type=F32) + carry
            carry = dls[:, 0:1]
            dl = dls * (1.0 - _sigmoid(s_ref[:, sl] + b_ref[...]))
            dl_ref[:, sl] = dl
            tot = tot + jnp.sum(dl, axis=1, keepdims=True)

        @pl.when(b == 0)
        def _():
            db_ref[...] = jnp.zeros_like(db_ref)

        db_ref[...] += jnp.broadcast_to(tot, db_ref.shape)

    return pl.pallas_call(
        body, name="fox_cum_bwd", grid=(n_batch,),
        in_specs=[pl.BlockSpec((SM_ROWS, s_len), lambda b: (0, b)), pl.BlockSpec((SM_ROWS, s_len), lambda b: (0, b)),
                  pl.BlockSpec((SM_ROWS, 1), lambda b: (0, 0))],
        out_specs=[pl.BlockSpec((SM_ROWS, s_len), lambda b: (0, b)), pl.BlockSpec((SM_ROWS, LANES), lambda b: (0, 0))],
        out_shape=[jax.ShapeDtypeStruct(smt.shape, F32), jax.ShapeDtypeStruct((SM_ROWS, LANES), F32)],
        compiler_params=_cparams(("arbitrary",)),
    )(dc, smt, bias_col)


def _fox_diagonal_mask(tq):
    return lax.broadcasted_iota(jnp.int32, (tq, tq), 1) <= lax.broadcasted_iota(jnp.int32, (tq, tq), 0)


def _fox_fwd(pf, cb, gq2, gk2, go2, tq=256):
    n_batch, s_len, _ = pf.shape
    tq = min(tq, s_len)
    nq = s_len // tq
    scale = FOX_HEAD_DIM ** -0.5

    def body(q_ref, k_ref, v_ref, c_ref, gq_ref, gk_ref, go_ref, o_ref, on_ref, lse_ref, kh_ref, vh_ref):
        j = pl.program_id(1)
        i = pl.program_id(2)
        m0 = lax.broadcasted_iota(jnp.int32, (1, LANES), 1) < FOX_HEAD_DIM

        @pl.when(i == 0)
        def _():
            kn = _rms_pair(k_ref[0], gk_ref[...], m0)
            kh_ref[0] = jnp.where(m0, kn, 0.0).astype(BF16)
            kh_ref[1] = jnp.where(m0, 0.0, kn).astype(BF16)
            v = v_ref[0]
            vh_ref[0] = jnp.where(m0, v, 0.0).astype(BF16)
            vh_ref[1] = jnp.where(m0, 0.0, v).astype(BF16)

        qb = (_rms_pair(q_ref[0], gq_ref[...], m0) * scale).astype(BF16)

        def step(kb, carry, diagonal=False):
            ms, ls, acc = carry
            off = pl.multiple_of(kb * tq, tq)
            new_m, new_l, alphas, pv = [], [], [], []
            for hh in range(2):
                s = _dot(qb, kh_ref[hh, pl.ds(off, tq), :], NT)
                s = s - c_ref[0, kb, pl.ds(2 * j + hh, 1), :]
                if diagonal:
                    s = jnp.where(_fox_diagonal_mask(tq), s, NEG_INF)
                m_new = jnp.maximum(ms[hh], jnp.max(s, axis=-1, keepdims=True))
                alpha = jnp.exp(ms[hh] - m_new)
                p = jnp.exp(s - m_new)
                new_l.append(alpha * ls[hh] + jnp.sum(p, axis=-1, keepdims=True))
                new_m.append(m_new)
                alphas.append(alpha)
                pv.append(_dot(p, vh_ref[hh, pl.ds(off, tq), :], NN))
            acc = jnp.where(m0, alphas[0], alphas[1]) * acc + pv[0] + pv[1]
            return tuple(new_m), tuple(new_l), acc

        init_m = (jnp.full((tq, 1), NEG_INF, F32),) * 2
        init_l = (jnp.zeros((tq, 1), F32),) * 2
        carry = lax.fori_loop(0, i, step, (init_m, init_l, jnp.zeros((tq, LANES), F32)))
        ms, ls, acc = step(i, carry, diagonal=True)
        o = acc / jnp.where(m0, ls[0], ls[1])
        o_ref[0] = o
        on_ref[0] = _rms_pair(o, go_ref[...], m0).astype(BF16)
        lse_ref[0] = jnp.where(m0, ms[0] + jnp.log(ls[0]), ms[1] + jnp.log(ls[1]))

    fixed = lambda b, j, i: (0, 0)
    tile = lambda b, j, i: (b, i, j)
    return pl.pallas_call(
        body, name="fox_fwd", grid=(n_batch, 4, nq),
        in_specs=[pl.BlockSpec((1, tq, LANES), tile), pl.BlockSpec((1, s_len, LANES), lambda b, j, i: (b, 0, 4 + j)),
                  pl.BlockSpec((1, s_len, LANES), lambda b, j, i: (b, 0, 8 + j)),
                  pl.BlockSpec((1, nq, SM_ROWS, tq), lambda b, j, i: (b, 0, 0, 0)),
                  pl.BlockSpec((1, LANES), fixed), pl.BlockSpec((1, LANES), fixed), pl.BlockSpec((1, LANES), fixed)],
        out_specs=[pl.BlockSpec((1, tq, LANES), tile), pl.BlockSpec((1, tq, LANES), tile), pl.BlockSpec((1, tq, LANES), tile)],
        out_shape=[jax.ShapeDtypeStruct((n_batch, s_len, FOX_WIDTH), F32), jax.ShapeDtypeStruct((n_batch, s_len, FOX_WIDTH), BF16),
                   jax.ShapeDtypeStruct((n_batch, s_len, FOX_WIDTH), F32)],
        scratch_shapes=[pltpu.VMEM((2, s_len, LANES), BF16), pltpu.VMEM((2, s_len, LANES), BF16)],
        compiler_params=_cparams(("parallel", "parallel", "arbitrary")),
    )(pf, pf, pf, cb, gq2, gk2, go2)


def _fox_bwd(pf, cb, gq2, gk2, go2, o, lse, don, tq=256):
    n_batch, s_len, _ = pf.shape
    tq = min(tq, s_len)
    nq = s_len // tq
    scale = FOX_HEAD_DIM ** -0.5

    def body(q_ref, k_ref, v_ref, c_ref, gq_ref, gk_ref, go_ref, o_ref, lse_ref, don_ref,
             dq_ref, dk_ref, dv_ref, dc_ref, dgq_ref, dgk_ref, dgo_ref, kh_ref, vh_ref, dka_ref, dva_ref, dca_ref):
        b = pl.program_id(0)
        j = pl.program_id(1)
        i = pl.program_id(2)
        m0 = lax.broadcasted_iota(jnp.int32, (1, LANES), 1) < FOX_HEAD_DIM

        @pl.when((b == 0) & (j == 0) & (i == 0))
        def _():
            dgq_ref[...] = jnp.zeros_like(dgq_ref)
            dgk_ref[...] = jnp.zeros_like(dgk_ref)
            dgo_ref[...] = jnp.zeros_like(dgo_ref)

        @pl.when(i == 0)
        def _():
            kn = _rms_pair(k_ref[0], gk_ref[...], m0)
            kh_ref[0] = jnp.where(m0, kn, 0.0).astype(BF16)
            kh_ref[1] = jnp.where(m0, 0.0, kn).astype(BF16)
            v = v_ref[0]
            vh_ref[0] = jnp.where(m0, v, 0.0).astype(BF16)
            vh_ref[1] = jnp.where(m0, 0.0, v).astype(BF16)
            dka_ref[...] = jnp.zeros_like(dka_ref)
            dva_ref[...] = jnp.zeros_like(dva_ref)
            dca_ref[...] = jnp.zeros_like(dca_ref)

        q = q_ref[0]
        qn = _rms_pair(q, gq_ref[...], m0)
        qs = qn * scale
        qb = qs.astype(BF16)
        qh = (jnp.where(m0, qs, 0.0).astype(BF16), jnp.where(m0, 0.0, qs).astype(BF16))
        ot = o_ref[0]
        do, dgo = _rms_pair_bwd(ot, go_ref[...], don_ref[0], m0)
        dgo_ref[...] += dgo
        dd = do * ot
        delta = (jnp.sum(jnp.where(m0, dd, 0.0), axis=-1, keepdims=True), jnp.sum(jnp.where(m0, 0.0, dd), axis=-1, keepdims=True))
        doh = (jnp.where(m0, do, 0.0).astype(BF16), jnp.where(m0, 0.0, do).astype(BF16))
        lse_t = lse_ref[0]
        lse_h = (lse_t[:, 0:1], lse_t[:, FOX_HEAD_DIM:FOX_HEAD_DIM + 1])

        def step(kb, carry, diagonal=False):
            dqn, rs = carry
            rs = list(rs)
            off = pl.multiple_of(kb * tq, tq)
            for hh in range(2):
                kblk = kh_ref[hh, pl.ds(off, tq), :]
                vblk = vh_ref[hh, pl.ds(off, tq), :]
                s = _dot(qb, kblk, NT)
                s = s - c_ref[0, kb, pl.ds(2 * j + hh, 1), :]
                if diagonal:
                    s = jnp.where(_fox_diagonal_mask(tq), s, NEG_INF)
                p = jnp.exp(s - lse_h[hh])
                dp = _dot(doh[hh], vblk, NT)
                ds = p * (dp - delta[hh])
                dva_ref[pl.ds(off, tq), :] += _dot(p, doh[hh], TN)
                dka_ref[pl.ds(off, tq), :] += _dot(ds, qh[hh], TN)
                dca_ref[kb, hh:hh + 1, :] += -jnp.sum(ds, axis=0, keepdims=True)
                rs[hh] = rs[hh] + jnp.sum(ds, axis=-1, keepdims=True)
                dqn = dqn + _dot(ds, kblk, NN)
            return dqn, tuple(rs)

        carry = lax.fori_loop(0, i, step, (jnp.zeros((tq, LANES), F32), (jnp.zeros((tq, 1), F32),) * 2))
        dqn, rs = step(i, carry, diagonal=True)
        dqn = dqn * scale
        rs_rows = jnp.where(m0, rs[0], rs[1]).T
        dca_ref[i, 0:1, :] += rs_rows[0:1, :]
        dca_ref[i, 1:2, :] += rs_rows[FOX_HEAD_DIM:FOX_HEAD_DIM + 1, :]
        dq, dgq = _rms_pair_bwd(q, gq_ref[...], dqn, m0)
        dq_ref[0] = dq.astype(BF16)
        dgq_ref[...] += dgq

        @pl.when(i == nq - 1)
        def _():
            dk, dgk = _rms_pair_bwd(k_ref[0], gk_ref[...], dka_ref[...], m0)
            dk_ref[0] = dk.astype(BF16)
            dgk_ref[...] += dgk
            dv_ref[0] = dva_ref[...].astype(BF16)
            dc_ref[0, 0] = dca_ref[...]

    fixed = lambda b, j, i: (0, 0)
    tile = lambda b, j, i: (b, i, j)
    full = lambda b, j, i: (b, 0, j)
    wide = jax.ShapeDtypeStruct((n_batch, s_len, FOX_WIDTH), BF16)
    gain = jax.ShapeDtypeStruct((1, LANES), F32)
    return pl.pallas_call(
        body, name="fox_bwd", grid=(n_batch, 4, nq),
        in_specs=[pl.BlockSpec((1, tq, LANES), tile), pl.BlockSpec((1, s_len, LANES), lambda b, j, i: (b, 0, 4 + j)),
                  pl.BlockSpec((1, s_len, LANES), lambda b, j, i: (b, 0, 8 + j)),
                  pl.BlockSpec((1, nq, SM_ROWS, tq), lambda b, j, i: (b, 0, 0, 0)),
                  pl.BlockSpec((1, LANES), fixed), pl.BlockSpec((1, LANES), fixed), pl.BlockSpec((1, LANES), fixed),
                  pl.BlockSpec((1, tq, LANES), tile), pl.BlockSpec((1, tq, LANES), tile), pl.BlockSpec((1, tq, LANES), tile)],
        out_specs=[pl.BlockSpec((1, tq, LANES), tile), pl.BlockSpec((1, s_len, LANES), full), pl.BlockSpec((1, s_len, LANES), full),
                   pl.BlockSpec((1, 1, nq, 8, tq), lambda b, j, i: (b, j, 0, 0, 0)),
                   pl.BlockSpec((1, LANES), fixed), pl.BlockSpec((1, LANES), fixed), pl.BlockSpec((1, LANES), fixed)],
        out_shape=[wide, wide, wide, jax.ShapeDtypeStruct((n_batch, 4, nq, 8, tq), F32), gain, gain, gain],
        scratch_shapes=[pltpu.VMEM((2, s_len, LANES), BF16), pltpu.VMEM((2, s_len, LANES), BF16),
                        pltpu.VMEM((s_len, LANES), F32), pltpu.VMEM((s_len, LANES), F32), pltpu.VMEM((nq, 8, tq), F32)],
        compiler_params=_cparams(("arbitrary", "arbitrary", "arbitrary")),
    )(pf, pf, pf, cb, gq2, gk2, go2, o, lse, don)


def _shift_down(x, k):
    row = lax.broadcasted_iota(jnp.int32, x.shape, 0)
    return jnp.where(row >= k, pltpu.roll(x, k, 0), 0.0)


def _shift_up(x, k):
    n = x.shape[0]
    row = lax.broadcasted_iota(jnp.int32, x.shape, 0)
    return jnp.where(row < n - k, pltpu.roll(x, n - k, 0), 0.0)


def _conv_silu(x, w):
    y = w[3:4] * x + w[2:3] * _shift_down(x, 1) + w[1:2] * _shift_down(x, 2) + w[0:1] * _shift_down(x, 3)
    return y, y * _sigmoid(y)


def _gdn_pre(pg, conv_w):
    n_batch, s_len, width = pg.shape
    ncb = width // LANES

    def body(x_ref, w_ref, o_ref):
        cb = pl.program_id(1)
        _, s = _conv_silu(x_ref[0], w_ref[...])
        sn = s * lax.rsqrt(jnp.sum(s * s, axis=-1, keepdims=True) + EPS)
        o_ref[0] = jnp.where(cb < 2 * GDN_HEADS, sn, s)

    return pl.pallas_call(
        body, name="gdn_pre", grid=(n_batch, ncb),
        in_specs=[pl.BlockSpec((1, s_len, LANES), lambda b, c: (b, 0, c)), pl.BlockSpec((8, LANES), lambda b, c: (0, c))],
        out_specs=pl.BlockSpec((1, s_len, LANES), lambda b, c: (b, 0, c)),
        out_shape=jax.ShapeDtypeStruct(pg.shape, F32),
        compiler_params=_cparams(("parallel", "parallel")),
    )(pg, conv_w)


def _gdn_pre_bwd(pg, conv_w, dout):
    n_batch, s_len, width = pg.shape
    ncb = width // LANES

    def body(x_ref, w_ref, d_ref, dx_ref, dw_ref):
        cb = pl.program_id(0)
        b = pl.program_id(1)
        x = x_ref[0]
        w = w_ref[...]
        d = d_ref[0]
        y, s = _conv_silu(x, w)
        rr = lax.rsqrt(jnp.sum(s * s, axis=-1, keepdims=True) + EPS)
        sn = s * rr
        ds_n = rr * (d - sn * jnp.sum(d * sn, axis=-1, keepdims=True))
        ds = jnp.where(cb < 2 * GDN_HEADS, ds_n, d)
        sig = _sigmoid(y)
        dy = ds * (sig * (1.0 + y * (1.0 - sig)))
        dx = w[3:4] * dy + w[2:3] * _shift_up(dy, 1) + w[1:2] * _shift_up(dy, 2) + w[0:1] * _shift_up(dy, 3)
        dx_ref[0] = dx.astype(BF16)
        dw = [jnp.sum(dy * _shift_down(x, 3 - jj), axis=0, keepdims=True) if jj < 3 else jnp.sum(dy * x, axis=0, keepdims=True)
              for jj in range(CONV_WIDTH)]
        rows = lax.broadcasted_iota(jnp.int32, (8, LANES), 0)
        dwb = jnp.zeros((8, LANES), F32)
        for jj in range(CONV_WIDTH):
            dwb = dwb + jnp.where(rows == jj, dw[jj], 0.0)

        @pl.when(b == 0)
        def _():
            dw_ref[...] = jnp.zeros_like(dw_ref)

        dw_ref[...] += dwb

    blk = lambda c, b: (b, 0, c)
    return pl.pallas_call(
        body, name="gdn_pre_bwd", grid=(ncb, n_batch),
        in_specs=[pl.BlockSpec((1, s_len, LANES), blk), pl.BlockSpec((8, LANES), lambda c, b: (0, c)), pl.BlockSpec((1, s_len, LANES), blk)],
        out_specs=[pl.BlockSpec((1, s_len, LANES), blk), pl.BlockSpec((8, LANES), lambda c, b: (0, c))],
        out_shape=[jax.ShapeDtypeStruct(pg.shape, BF16), jax.ShapeDtypeStruct((8, width), F32)],
        compiler_params=_cparams(("parallel", "arbitrary")),
    )(pg, conv_w, dout)


def _gdn_gates(smc, smr, a_c, dt_c, a_r, dt_r, h):
    lane = lax.broadcasted_iota(jnp.int32, (1, LANES), 1)
    sub = lax.broadcasted_iota(jnp.int32, (SM_ROWS, 1), 0)
    beta_c = jnp.sum(jnp.where(lane == SM_B + h, _sigmoid(smc), 0.0), axis=1, keepdims=True)
    g_all_c = -jnp.exp(a_c) * _softplus(smc + dt_c)
    g_c = jnp.sum(jnp.where(lane == SM_A + h, g_all_c, 0.0), axis=1, keepdims=True)
    g_all_r = -jnp.exp(a_r) * _softplus(smr + dt_r)
    g_r = jnp.sum(jnp.where(sub == SM_A + h, g_all_r, 0.0), axis=0, keepdims=True)
    return beta_c, g_c, g_r


def _gdn_group(qkv, z, smc, smr, a_c, dt_c, a_r, dt_r, go, states):
    n_grp = len(qkv)
    c = qkv[0].shape[0]
    hd = GDN_HEAD_DIM
    pairs = [(g, h) for g in range(n_grp) for h in range(GDN_HEADS)]
    ii = lax.broadcasted_iota(jnp.int32, (c, c), 0)
    jj = lax.broadcasted_iota(jnp.int32, (c, c), 1)
    incl = ii >= jj
    col = lambda arr, base, h: arr[:, base + h * hd:base + (h + 1) * hd]

    qs, ks, kbs, vbs, decays, gcs, g_lasts, amats = [], [], [], [], [], [], [], []
    for g, h in pairs:
        beta_c, g_c, g_r = _gdn_gates(smc[g], smr[g], a_c, dt_c, a_r, dt_r, h)
        gc_c = jnp.sum(jnp.where(incl, g_r, 0.0), axis=1, keepdims=True)
        gc_r = jnp.sum(jnp.where(ii <= jj, g_c, 0.0), axis=0, keepdims=True)
        decay = jnp.where(incl, jnp.exp(jnp.where(incl, gc_c - gc_r, 0.0)), 0.0)
        k = col(qkv[g], GDN_WIDTH, h)
        kb = k * beta_c
        qs.append(col(qkv[g], 0, h) * (hd ** -0.5))
        ks.append(k)
        kbs.append(kb)
        vbs.append(col(qkv[g], 2 * GDN_WIDTH, h) * beta_c)
        decays.append(decay)
        gcs.append(gc_c)
        g_lasts.append(jnp.sum(g_c, axis=0, keepdims=True))
        amats.append(jnp.where(ii > jj, _mm_nt(kb, k) * decay, 0.0))
    ts = _unit_lower_inverses(amats)
    egcs = [jnp.exp(gc) for gc in gcs]
    us = [_mm_nn(t, vb) for t, vb in zip(ts, vbs)]
    ws = [_mm_nn(t, kb * e) for t, kb, e in zip(ts, kbs, egcs)]
    intras = [_mm_nt(q, k) * d for q, k, d in zip(qs, ks, decays)]
    qes = [q * e for q, e in zip(qs, egcs)]
    kds = [k * jnp.exp(gl - gc) for k, gl, gc in zip(ks, g_lasts, gcs)]
    sdecs = [jnp.exp(gl) for gl in g_lasts]

    outs = []
    for g in range(n_grp):
        idx = [g * GDN_HEADS + h for h in range(GDN_HEADS)]
        v_new = [us[i] - _mm_nn(ws[i], states[h]) for h, i in enumerate(idx)]
        o_state = [_mm_nn(qes[i], states[h]) for h, i in enumerate(idx)]
        o_intra = [_mm_nn(intras[i], v_new[h]) for h, i in enumerate(idx)]
        states = [states[h] * sdecs[i] + _mm_tn(kds[i], v_new[h]) for h, i in enumerate(idx)]
        outs.append([_rms(o_state[h] + o_intra[h], go) * (col(z[g], 0, h) * _sigmoid(col(z[g], 0, h))) for h in range(GDN_HEADS)])
    return outs, states


def _gdn_group_size(n_chunks):
    return GDN_GROUP if n_chunks % GDN_GROUP == 0 else 1


def _gdn_fwd(qkvn, z, smc, smr, a_c, dt_c, a_r, dt_r, go):
    n_batch, s_len, _ = qkvn.shape
    c = GDN_CHUNK
    n = s_len // c
    grp = _gdn_group_size(n)
    ng = n // grp
    gc = grp * c
    hd = GDN_HEAD_DIM

    def body(qkv_ref, z_ref, smc_ref, smr_ref, ac_ref, dc_ref, ar_ref, dr_ref, go_ref, og_ref, st_ref, s_ref):
        @pl.when(pl.program_id(1) == 0)
        def _():
            s_ref[...] = jnp.zeros_like(s_ref)

        states = [s_ref[h] for h in range(GDN_HEADS)]
        for h in range(GDN_HEADS):
            st_ref[0, 0, h] = states[h]
        rows = lambda k: slice(k * c, (k + 1) * c)
        outs, nxt = _gdn_group([qkv_ref[0, rows(k), :] for k in range(grp)], [z_ref[0, rows(k), :] for k in range(grp)],
                               [smc_ref[0, rows(k), :] for k in range(grp)], [smr_ref[k] for k in range(grp)],
                               ac_ref[...], dc_ref[...], ar_ref[...], dr_ref[...], go_ref[...], states)
        for k in range(grp):
            for h in range(GDN_HEADS):
                og_ref[0, rows(k), h * hd:(h + 1) * hd] = outs[k][h].astype(BF16)
        for h in range(GDN_HEADS):
            s_ref[h] = nxt[h]

    tok = lambda b, i: (b, i, 0)
    fixed = lambda b, i: (0, 0)
    return pl.pallas_call(
        body, name="gdn_fwd", grid=(n_batch, ng),
        in_specs=[pl.BlockSpec((1, gc, 3 * GDN_WIDTH), tok), pl.BlockSpec((1, gc, GDN_WIDTH), tok), pl.BlockSpec((1, gc, LANES), tok),
                  pl.BlockSpec((grp, SM_ROWS, c), lambda b, i: (b * ng + i, 0, 0)),
                  pl.BlockSpec((1, LANES), fixed), pl.BlockSpec((1, LANES), fixed), pl.BlockSpec((SM_ROWS, 1), fixed),
                  pl.BlockSpec((SM_ROWS, 1), fixed), pl.BlockSpec((1, LANES), fixed)],
        out_specs=[pl.BlockSpec((1, gc, GDN_WIDTH), tok), pl.BlockSpec((1, 1, GDN_HEADS, hd, hd), lambda b, i: (b, i, 0, 0, 0))],
        out_shape=[jax.ShapeDtypeStruct((n_batch, s_len, GDN_WIDTH), BF16), jax.ShapeDtypeStruct((n_batch, ng, GDN_HEADS, hd, hd), F32)],
        scratch_shapes=[pltpu.VMEM((GDN_HEADS, hd, hd), F32)],
        compiler_params=_cparams(("parallel", "arbitrary")),
    )(qkvn, z, smc, smr, a_c, dt_c, a_r, dt_r, go)


def _gdn_bwd(qkvn, z, smc, smr, a_c, dt_c, a_r, dt_r, go, states, dog):
    n_batch, s_len, _ = qkvn.shape
    c = GDN_CHUNK
    n = s_len // c
    grp = _gdn_group_size(n)
    ng = n // grp
    gc = grp * c
    hd = GDN_HEAD_DIM

    def body(qkv_ref, z_ref, smc_ref, smr_ref, ac_ref, dc_ref, ar_ref, dr_ref, go_ref, st_ref, dog_ref,
             dqkv_ref, dz_ref, dsmc_ref, dsmr_ref, dac_ref, ddc_ref, dar_ref, ddr_ref, dgo_ref, ds_ref):
        first = (pl.program_id(0) == 0) & (pl.program_id(1) == 0)

        @pl.when(pl.program_id(1) == 0)
        def _():
            ds_ref[...] = jnp.zeros_like(ds_ref)

        @pl.when(first)
        def _():
            for r in (dac_ref, ddc_ref, dar_ref, ddr_ref, dgo_ref):
                r[...] = jnp.zeros_like(r)

        rows = lambda k: slice(k * c, (k + 1) * c)
        states = [st_ref[0, 0, h] for h in range(GDN_HEADS)]
        prim = ([qkv_ref[0, rows(k), :] for k in range(grp)], [z_ref[0, rows(k), :] for k in range(grp)],
                [smc_ref[0, rows(k), :] for k in range(grp)], [smr_ref[k] for k in range(grp)],
                ac_ref[...], dc_ref[...], ar_ref[...], dr_ref[...], go_ref[...], states)
        _, vjp = jax.vjp(_gdn_group, *prim)
        cot = ([[dog_ref[0, rows(k), h * hd:(h + 1) * hd] for h in range(GDN_HEADS)] for k in range(grp)],
               [ds_ref[h] for h in range(GDN_HEADS)])
        dqkv, dz, dsmc, dsmr, dac, ddc, dar, ddr, dgo, dstates = vjp(cot)
        for k in range(grp):
            dqkv_ref[0, rows(k), :] = dqkv[k]
            dz_ref[0, rows(k), :] = dz[k].astype(BF16)
            dsmc_ref[0, rows(k), :] = dsmc[k]
            dsmr_ref[k] = dsmr[k]
        dac_ref[...] += dac
        ddc_ref[...] += ddc
        dar_ref[...] += dar
        ddr_ref[...] += ddr
        dgo_ref[...] += dgo
        for h in range(GDN_HEADS):
            ds_ref[h] = dstates[h]

    tok = lambda b, i: (b, ng - 1 - i, 0)
    fixed = lambda b, i: (0, 0)
    lane_vec = jax.ShapeDtypeStruct((1, LANES), F32)
    row_vec = jax.ShapeDtypeStruct((SM_ROWS, 1), F32)
    return pl.pallas_call(
        body, name="gdn_bwd", grid=(n_batch, ng),
        in_specs=[pl.BlockSpec((1, gc, 3 * GDN_WIDTH), tok), pl.BlockSpec((1, gc, GDN_WIDTH), tok), pl.BlockSpec((1, gc, LANES), tok),
                  pl.BlockSpec((grp, SM_ROWS, c), lambda b, i: (b * ng + ng - 1 - i, 0, 0)),
                  pl.BlockSpec((1, LANES), fixed), pl.BlockSpec((1, LANES), fixed), pl.BlockSpec((SM_ROWS, 1), fixed),
                  pl.BlockSpec((SM_ROWS, 1), fixed), pl.BlockSpec((1, LANES), fixed),
                  pl.BlockSpec((1, 1, GDN_HEADS, hd, hd), lambda b, i: (b, ng - 1 - i, 0, 0, 0)),
                  pl.BlockSpec((1, gc, GDN_WIDTH), lambda b, i: (b, ng - 1 - i, 1))],
        out_specs=[pl.BlockSpec((1, gc, 3 * GDN_WIDTH), tok), pl.BlockSpec((1, gc, GDN_WIDTH), tok), pl.BlockSpec((1, gc, LANES), tok),
                   pl.BlockSpec((grp, SM_ROWS, c), lambda b, i: (b * ng + ng - 1 - i, 0, 0)),
                   pl.BlockSpec((1, LANES), fixed), pl.BlockSpec((1, LANES), fixed), pl.BlockSpec((SM_ROWS, 1), fixed),
                   pl.BlockSpec((SM_ROWS, 1), fixed), pl.BlockSpec((1, LANES), fixed)],
        out_shape=[jax.ShapeDtypeStruct((n_batch, s_len, 3 * GDN_WIDTH), F32), jax.ShapeDtypeStruct((n_batch, s_len, GDN_WIDTH), BF16),
                   jax.ShapeDtypeStruct((n_batch, s_len, LANES), F32), jax.ShapeDtypeStruct((n_batch * n, SM_ROWS, c), F32),
                   lane_vec, lane_vec, row_vec, row_vec, lane_vec],
        scratch_shapes=[pltpu.VMEM((GDN_HEADS, hd, hd), F32)],
        compiler_params=_cparams(("arbitrary", "arbitrary")),
    )(qkvn, z, smc, smr, a_c, dt_c, a_r, dt_r, go, states, dog)


def _out_proj(x, oa, ob, w_out, g_x, w_cq, tm=256):
    t_len, d = x.shape
    tm = min(tm, t_len)

    def body(x_ref, oa_ref, ob_ref, wo_ref, g_ref, wq_ref, x1_ref, hq_ref, cq_ref):
        x1 = x_ref[...] + _dot(oa_ref[...], wo_ref[0:FOX_WIDTH, :]) + _dot(ob_ref[...], wo_ref[FOX_WIDTH:2 * FOX_WIDTH, :])
        x1_ref[...] = x1
        hq = _rms(x1, g_ref[...]).astype(BF16)
        hq_ref[...] = hq
        cq_ref[...] = _dot(hq, wq_ref[...])

    row = lambda i: (i, 0)
    fixed = lambda i: (0, 0)
    return pl.pallas_call(
        body, name="out_proj", grid=(t_len // tm,),
        in_specs=[pl.BlockSpec((tm, d), row), pl.BlockSpec((tm, FOX_WIDTH), row), pl.BlockSpec((tm, GDN_WIDTH), row),
                  pl.BlockSpec((d, d), fixed), pl.BlockSpec((1, d), fixed), pl.BlockSpec((d, XATTN_WIDTH), fixed)],
        out_specs=[pl.BlockSpec((tm, d), row), pl.BlockSpec((tm, d), row), pl.BlockSpec((tm, XATTN_WIDTH), row)],
        out_shape=[jax.ShapeDtypeStruct((t_len, d), F32), jax.ShapeDtypeStruct((t_len, d), BF16), jax.ShapeDtypeStruct((t_len, XATTN_WIDTH), F32)],
        compiler_params=_cparams(("parallel",)),
    )(x, oa, ob, w_out, g_x, w_cq)


def _out_proj_bwd(dx1, w_out, tm=512):
    t_len, d = dx1.shape
    tm = min(tm, t_len)

    def body(dx_ref, w_ref, o_ref):
        o_ref[...] = _dot(dx_ref[...], w_ref[...], NT)

    return pl.pallas_call(
        body, name="out_proj_bwd", grid=(t_len // tm,),
        in_specs=[pl.BlockSpec((tm, d), lambda i: (i, 0)), pl.BlockSpec((d, d), lambda i: (0, 0))],
        out_specs=pl.BlockSpec((tm, d), lambda i: (i, 0)),
        out_shape=jax.ShapeDtypeStruct((t_len, d), F32),
        compiler_params=_cparams(("parallel",)),
    )(dx1, w_out)


def _mem_kv(mem, g, w_ckv, tm=256):
    t_len, d = mem.shape
    tm = min(tm, t_len)

    def body(x_ref, g_ref, w_ref, h_ref, o_ref):
        h = _rms(x_ref[...], g_ref[...]).astype(BF16)
        h_ref[...] = h
        o_ref[...] = _dot(h, w_ref[...])

    row = lambda i: (i, 0)
    fixed = lambda i: (0, 0)
    return pl.pallas_call(
        body, name="mem_kv", grid=(t_len // tm,),
        in_specs=[pl.BlockSpec((tm, d), row), pl.BlockSpec((1, d), fixed), pl.BlockSpec((d, 2 * XATTN_WIDTH), fixed)],
        out_specs=[pl.BlockSpec((tm, d), row), pl.BlockSpec((tm, 2 * XATTN_WIDTH), row)],
        out_shape=[jax.ShapeDtypeStruct((t_len, d), BF16), jax.ShapeDtypeStruct((t_len, 2 * XATTN_WIDTH), F32)],
        compiler_params=_cparams(("parallel",)),
    )(mem, g, w_ckv)


def _mem_kv_bwd(dckv, mem, g, w_ckv, tm=256):
    t_len, d = mem.shape
    tm = min(tm, t_len)

    def body(d_ref, x_ref, g_ref, w_ref, dg_ref):
        @pl.when(pl.program_id(0) == 0)
        def _():
            dg_ref[...] = jnp.zeros_like(dg_ref)

        dh = _dot(d_ref[...], w_ref[...], NT)
        _, dg = _rms_bwd(x_ref[...], g_ref[...], dh)
        dg_ref[...] += dg

    row = lambda i: (i, 0)
    fixed = lambda i: (0, 0)
    return pl.pallas_call(
        body, name="mem_kv_bwd", grid=(t_len // tm,),
        in_specs=[pl.BlockSpec((tm, 2 * XATTN_WIDTH), row), pl.BlockSpec((tm, d), row), pl.BlockSpec((1, d), fixed),
                  pl.BlockSpec((d, 2 * XATTN_WIDTH), fixed)],
        out_specs=pl.BlockSpec((1, d), fixed),
        out_shape=jax.ShapeDtypeStruct((1, d), F32),
        compiler_params=_cparams(("arbitrary",)),
    )(dckv, mem, g, w_ckv)


def _xattn_probs(qn, kn):
    s = _dot(qn, kn, NT) * (XATTN_HEAD_DIM ** -0.5)
    p = jnp.exp(s - jnp.max(s, axis=-1, keepdims=True))
    return p / jnp.sum(p, axis=-1, keepdims=True)


def _xattn_fwd(cq, ckv, x1, gq, gk, w_co, g_mlp, n_batch, s_len, m_len, tq=256):
    d = x1.shape[1]
    tq = min(tq, s_len)
    nq = s_len // tq
    hd = XATTN_HEAD_DIM

    def body(cq_ref, kv_ref, x1_ref, gq_ref, gk_ref, wo_ref, gm_ref, co_ref, x2_ref, hf_ref):
        outs = []
        for h in range(XATTN_HEADS):
            qn = _rms(cq_ref[:, h * hd:(h + 1) * hd], gq_ref[...])
            kn = _rms(kv_ref[:, h * hd:(h + 1) * hd], gk_ref[...])
            p = _xattn_probs(qn, kn)
            outs.append(_dot(p, kv_ref[:, XATTN_WIDTH + h * hd:XATTN_WIDTH + (h + 1) * hd]).astype(BF16))
        x2 = x1_ref[...]
        for h in range(XATTN_HEADS):
            co_ref[:, h * hd:(h + 1) * hd] = outs[h]
            x2 = x2 + _dot(outs[h], wo_ref[h * hd:(h + 1) * hd, :])
        x2_ref[...] = x2
        hf_ref[...] = _rms(x2, gm_ref[...]).astype(BF16)

    row = lambda b, i: (b * nq + i, 0)
    fixed = lambda b, i: (0, 0)
    t_len = n_batch * s_len
    return pl.pallas_call(
        body, name="xattn_fwd", grid=(n_batch, nq),
        in_specs=[pl.BlockSpec((tq, XATTN_WIDTH), row), pl.BlockSpec((m_len, 2 * XATTN_WIDTH), lambda b, i: (b, 0)),
                  pl.BlockSpec((tq, d), row), pl.BlockSpec((1, hd), fixed), pl.BlockSpec((1, hd), fixed),
                  pl.BlockSpec((XATTN_WIDTH, d), fixed), pl.BlockSpec((1, d), fixed)],
        out_specs=[pl.BlockSpec((tq, XATTN_WIDTH), row), pl.BlockSpec((tq, d), row), pl.BlockSpec((tq, d), row)],
        out_shape=[jax.ShapeDtypeStruct((t_len, XATTN_WIDTH), BF16), jax.ShapeDtypeStruct((t_len, d), F32),
                   jax.ShapeDtypeStruct((t_len, d), BF16)],
        compiler_params=_cparams(("parallel", "parallel")),
    )(cq, ckv, x1, gq, gk, w_co, g_mlp)


def _xattn_bwd(dx2, cq, ckv, x1, gq, gk, w_co, g_x, w_cq, n_batch, s_len, m_len, tq=256):
    d = x1.shape[1]
    tq = min(tq, s_len)
    nq = s_len // tq
    hd = XATTN_HEAD_DIM
    scale = XATTN_HEAD_DIM ** -0.5

    def body(dx2_ref, cq_ref, kv_ref, x1_ref, gq_ref, gk_ref, wo_ref, gx_ref, wq_ref,
             dx1_ref, dcq_ref, dkv_ref, dgq_ref, dgk_ref, dgx_ref, dk_acc, dv_acc):
        b = pl.program_id(0)
        i = pl.program_id(1)

        @pl.when((b == 0) & (i == 0))
        def _():
            dgq_ref[...] = jnp.zeros_like(dgq_ref)
            dgk_ref[...] = jnp.zeros_like(dgk_ref)
            dgx_ref[...] = jnp.zeros_like(dgx_ref)

        @pl.when(i == 0)
        def _():
            dk_acc[...] = jnp.zeros_like(dk_acc)
            dv_acc[...] = jnp.zeros_like(dv_acc)

        dx2 = dx2_ref[...]
        dhq = jnp.zeros((tq, d), F32)
        for h in range(XATTN_HEADS):
            sl = slice(h * hd, (h + 1) * hd)
            q = cq_ref[:, sl]
            qn = _rms(q, gq_ref[...])
            kn = _rms(kv_ref[:, sl], gk_ref[...])
            v = kv_ref[:, XATTN_WIDTH + h * hd:XATTN_WIDTH + (h + 1) * hd]
            p = _xattn_probs(qn, kn)
            dco = _dot(dx2, wo_ref[sl, :], NT)
            dv_acc[:, sl] += _dot(p, dco, TN)
            dp = _dot(dco, v, NT)
            ds = p * (dp - jnp.sum(dp * p, axis=-1, keepdims=True))
            dqn = _dot(ds, kn) * scale
            dk_acc[:, sl] += _dot(ds, qn, TN) * scale
            dq, dgq = _rms_bwd(q, gq_ref[...], dqn)
            dgq_ref[...] += dgq
            dqb = dq.astype(BF16)
            dcq_ref[:, sl] = dqb
            dhq = dhq + _dot(dqb, wq_ref[:, sl], NT)
        dxn, dgx = _rms_bwd(x1_ref[...], gx_ref[...], dhq)
        dgx_ref[...] += dgx
        dx1_ref[...] = dx2 + dxn

        @pl.when(i == nq - 1)
        def _():
            for h in range(XATTN_HEADS):
                sl = slice(h * hd, (h + 1) * hd)
                dk, dgk = _rms_bwd(kv_ref[:, sl], gk_ref[...], dk_acc[:, sl])
                dgk_ref[...] += dgk
                dkv_ref[:, sl] = dk.astype(BF16)
                dkv_ref[:, XATTN_WIDTH + h * hd:XATTN_WIDTH + (h + 1) * hd] = dv_acc[:, sl].astype(BF16)

    row = lambda b, i: (b * nq + i, 0)
    fixed = lambda b, i: (0, 0)
    t_len = n_batch * s_len
    return pl.pallas_call(
        body, name="xattn_bwd", grid=(n_batch, nq),
        in_specs=[pl.BlockSpec((tq, d), row), pl.BlockSpec((tq, XATTN_WIDTH), row), pl.BlockSpec((m_len, 2 * XATTN_WIDTH), lambda b, i: (b, 0)),
                  pl.BlockSpec((tq, d), row), pl.BlockSpec((1, hd), fixed), pl.BlockSpec((1, hd), fixed),
                  pl.BlockSpec((XATTN_WIDTH, d), fixed), pl.BlockSpec((1, d), fixed), pl.BlockSpec((d, XATTN_WIDTH), fixed)],
        out_specs=[pl.BlockSpec((tq, d), row), pl.BlockSpec((tq, XATTN_WIDTH), row), pl.BlockSpec((m_len, 2 * XATTN_WIDTH), lambda b, i: (b, 0)),
                   pl.BlockSpec((1, hd), fixed), pl.BlockSpec((1, hd), fixed), pl.BlockSpec((1, d), fixed)],
        out_shape=[jax.ShapeDtypeStruct((t_len, d), F32), jax.ShapeDtypeStruct((t_len, XATTN_WIDTH), BF16),
                   jax.ShapeDtypeStruct((n_batch * m_len, 2 * XATTN_WIDTH), BF16),
                   jax.ShapeDtypeStruct((1, hd), F32), jax.ShapeDtypeStruct((1, hd), F32), jax.ShapeDtypeStruct((1, d), F32)],
        scratch_shapes=[pltpu.VMEM((m_len, XATTN_WIDTH), F32), pltpu.VMEM((m_len, XATTN_WIDTH), F32)],
        compiler_params=_cparams(("arbitrary", "arbitrary")),
    )(dx2, cq, ckv, x1, gq, gk, w_co, g_x, w_cq)


def _resident(shape):
    return pl.BlockSpec(shape, lambda *_: (0,) * len(shape), pipeline_mode=pl.Buffered(1))


def _mlp_fwd(hf, x2, target, w1, w2, tm=256):
    t_len, d = x2.shape
    nb, _, tf = w1.shape
    f = nb * tf
    tm = min(tm, t_len)

    def body(hf_ref, x2_ref, tg_ref, w1_ref, w2_ref, u_ref, a_ref, dy_ref, ls_ref):
        hf_t = hf_ref[...]
        y = x2_ref[...]
        for k in range(nb):
            cols = slice(k * tf, (k + 1) * tf)
            u = _dot(hf_t, w1_ref[k])
            u_ref[:, cols] = u
            r = jnp.maximum(u, 0.0)
            a = (r * r).astype(BF16)
            a_ref[:, cols] = a
            y = y + _dot(a, w2_ref[cols, :])
        err = y - tg_ref[...]
        dy_ref[...] = err * (1.0 / d)
        ls_ref[...] = jnp.broadcast_to(jnp.sum(jnp.sum(err * err, axis=-1, keepdims=True) * (1.0 / d), axis=0, keepdims=True), ls_ref.shape)

    row = lambda i: (i, 0)
    return pl.pallas_call(
        body, name="mlp_fwd", grid=(t_len // tm,),
        in_specs=[pl.BlockSpec((tm, d), row), pl.BlockSpec((tm, d), row), pl.BlockSpec((tm, d), row), _resident((nb, d, tf)), _resident((f, d))],
        out_specs=[pl.BlockSpec((tm, f), row), pl.BlockSpec((tm, f), row), pl.BlockSpec((tm, d), row),
                   pl.BlockSpec((1, 8, LANES), lambda i: (i, 0, 0))],
        out_shape=[jax.ShapeDtypeStruct((t_len, f), F32), jax.ShapeDtypeStruct((t_len, f), BF16), jax.ShapeDtypeStruct((t_len, d), F32),
                   jax.ShapeDtypeStruct((t_len // tm, 8, LANES), F32)],
        compiler_params=_cparams(("parallel",)),
    )(hf, x2, target, w1, w2)


def _mlp_bwd(dy, u, x2, g, w1, w2, tm=256):
    t_len, d = x2.shape
    nb, _, tf = w1.shape
    f = nb * tf
    tm = min(tm, t_len)

    def body(dy_ref, u_ref, x2_ref, g_ref, w1_ref, w2_ref, du_ref, dx2_ref, dg_ref):
        @pl.when(pl.program_id(0) == 0)
        def _():
            dg_ref[...] = jnp.zeros_like(dg_ref)

        dy_t = dy_ref[...]
        dyb = dy_t.astype(BF16)
        dhf = jnp.zeros((tm, d), F32)
        for k in range(nb):
            cols = slice(k * tf, (k + 1) * tf)
            da = _dot(dyb, w2_ref[cols, :], NT)
            du = (da * (2.0 * jnp.maximum(u_ref[:, cols], 0.0))).astype(BF16)
            du_ref[:, cols] = du
            dhf = dhf + _dot(du, w1_ref[k], NT)
        dxn, dg = _rms_bwd(x2_ref[...], g_ref[...], dhf)
        dx2_ref[...] = dy_t + dxn
        dg_ref[...] += dg

    row = lambda i: (i, 0)
    fixed = lambda i: (0, 0)
    return pl.pallas_call(
        body, name="mlp_bwd", grid=(t_len // tm,),
        in_specs=[pl.BlockSpec((tm, d), row), pl.BlockSpec((tm, f), row), pl.BlockSpec((tm, d), row), pl.BlockSpec((1, d), fixed),
                  _resident((nb, d, tf)), _resident((f, d))],
        out_specs=[pl.BlockSpec((tm, f), row), pl.BlockSpec((tm, d), row), pl.BlockSpec((1, d), fixed)],
        out_shape=[jax.ShapeDtypeStruct((t_len, f), BF16), jax.ShapeDtypeStruct((t_len, d), F32), jax.ShapeDtypeStruct((1, d), F32)],
        compiler_params=_cparams(("arbitrary",)),
    )(dy, u, x2, g, w1, w2)


def _pad_lanes(v, offset=0, width=LANES):
    return jnp.zeros((1, width), F32).at[:, offset:offset + v.shape[1]].set(v)


def _col(v, offset=0, rows=SM_ROWS):
    return jnp.zeros((rows, 1), F32).at[offset:offset + v.shape[1], 0].set(v[0])


LATE_WEIGHTS = ("w_out", "w_cq", "w_ckv", "w_co", "w_mlp1", "w_mlp2")
GRAD_GROUPS = (("w_mlp2", "w_mlp1"), ("w_co", "w_cq", "w_ckv", "w_out"), ("w_in", "gdn_conv_w"))


def _local_step(x, mem, target, norm_mix_g, w_in, fox_qnorm_g, fox_knorm_g, fox_f_bias, fox_onorm_g, gdn_conv_w, gdn_A_log,
                gdn_dt_bias, gdn_onorm_g, norm_xattn_g, mem_norm_g, xattn_qnorm_g, xattn_knorm_g, norm_mlp_g,
                late_weights, grads_ready=None, first_token=0.0):
    if grads_ready is None:
        grads_ready = lambda group: 0.0
    n_batch, s_len, d = x.shape
    m_len = mem.shape[1]
    t_len = n_batch * s_len
    tq = min(FOX_BLOCK, s_len)
    nq = s_len // tq
    n_chunks = s_len // GDN_CHUNK
    x2d = x.reshape(t_len, d)

    wp = jnp.concatenate([w_in[:, 0:1536], w_in[:, 1544:3080], w_in[:, 3088:3600], w_in[:, 1536:1544], w_in[:, 3080:3088],
                          jnp.zeros((d, P_DIM - 3600), BF16)], axis=1)
    wst = jnp.concatenate([w_in[:, 1536:1544], w_in[:, 3080:3088]], axis=1).T
    conv_w = jnp.concatenate([gdn_conv_w, jnp.zeros((8 - CONV_WIDTH, gdn_conv_w.shape[1]), F32)], axis=0)
    bias_col = _col(fox_f_bias, SM_F)
    gq2, gk2, go2 = (jnp.tile(g, (1, 2)) for g in (fox_qnorm_g, fox_knorm_g, fox_onorm_g))
    a_c, dt_c = _pad_lanes(gdn_A_log, SM_A), _pad_lanes(gdn_dt_bias, SM_A)
    a_r, dt_r = _col(gdn_A_log, SM_A), _col(gdn_dt_bias, SM_A)

    h1, pfox, pgdn, pz, sm, smt = _in_proj(x2d, norm_mix_g + first_token, wp, wst)
    c_rows = _fox_cum(smt, bias_col, n_batch, s_len)
    cb = c_rows.reshape(SM_ROWS, n_batch, nq, tq).transpose(1, 2, 0, 3)
    pf3 = pfox.reshape(n_batch, s_len, 1536)
    o_fox, oa, lse = _fox_fwd(pf3, cb, gq2, gk2, go2, tq)
    pg3 = pgdn.reshape(n_batch, s_len, 1536)
    qkvn = _gdn_pre(pg3, conv_w)
    z3 = pz.reshape(n_batch, s_len, GDN_WIDTH)
    smc = sm.reshape(n_batch, s_len, LANES)
    smr = smt.reshape(SM_ROWS, n_batch * n_chunks, GDN_CHUNK).transpose(1, 0, 2)
    ob, states = _gdn_fwd(qkvn, z3, smc, smr, a_c, dt_c, a_r, dt_r, gdn_onorm_g)
    oa2, ob2 = oa.reshape(t_len, FOX_WIDTH), ob.reshape(t_len, GDN_WIDTH)
    late = late_weights(ob2)
    w_out, w_cq, w_ckv, w_co, w_mlp1, w_mlp2 = (late[k] for k in LATE_WEIGHTS)
    x1, hq, cq = _out_proj(x2d, oa2, ob2, w_out, norm_xattn_g, w_cq)
    mem2d = mem.reshape(n_batch * m_len, d)
    hm, ckv = _mem_kv(mem2d, mem_norm_g, w_ckv)
    co, x2, hf = _xattn_fwd(cq, ckv, x1, xattn_qnorm_g, xattn_knorm_g, w_co, norm_mlp_g, n_batch, s_len, m_len)
    u, a_act, dy, loss_tiles = _mlp_fwd(hf, x2, target.reshape(t_len, d), w_mlp1, w_mlp2)
    loss = 0.5 * jnp.sum(loss_tiles[:, 0, 0])

    grads = {}
    du, dx2, grads["norm_mlp_g"] = _mlp_bwd(dy, u, x2, norm_mlp_g, w_mlp1, w_mlp2)
    grads["w_mlp2"] = _wgrad(a_act, dy, "wgrad_mlp2")
    grads["w_mlp1"] = _wgrad(hf, du, "wgrad_mlp1", bn=D_FF // N_DEV, column_blocks=True)
    token = grads_ready({k: grads[k] for k in GRAD_GROUPS[0]})
    grads["w_co"] = _wgrad(co, dx2, "wgrad_co")
    dx1, dcq, dckv, grads["xattn_qnorm_g"], grads["xattn_knorm_g"], grads["norm_xattn_g"] = _xattn_bwd(
        dx2, cq, ckv, x1, xattn_qnorm_g + token, xattn_knorm_g, w_co, norm_xattn_g, w_cq, n_batch, s_len, m_len)
    grads["w_cq"] = _wgrad(hq, dcq, "wgrad_cq")
    grads["w_ckv"] = _wgrad(hm, dckv, "wgrad_ckv")
    grads["mem_norm_g"] = _mem_kv_bwd(dckv, mem2d, mem_norm_g, w_ckv)
    grads["w_out"] = _wgrad(jnp.concatenate([oa2, ob2], axis=1), dx1, "wgrad_out")
    token = grads_ready({k: grads[k] for k in GRAD_GROUPS[1]})
    dcat = _out_proj_bwd(dx1, w_out)
    dcat3 = dcat.reshape(n_batch, s_len, d)

    dqkvn, dz, dsmc, dsmr, dac, ddc, dar, ddr, grads["gdn_onorm_g"] = _gdn_bwd(
        qkvn, z3, smc, smr, a_c, dt_c, a_r, dt_r, gdn_onorm_g + token, states, dcat3)
    grads["gdn_A_log"] = dac[:, SM_A:SM_A + GDN_HEADS] + dar[SM_A:SM_A + GDN_HEADS, 0][None, :]
    grads["gdn_dt_bias"] = ddc[:, SM_A:SM_A + GDN_HEADS] + ddr[SM_A:SM_A + GDN_HEADS, 0][None, :]
    dpg, dconv = _gdn_pre_bwd(pg3, conv_w, dqkvn)
    grads["gdn_conv_w"] = dconv[0:CONV_WIDTH]

    dq, dk, dv, dcb, dgq, dgk, dgo = _fox_bwd(pf3, cb, gq2, gk2, go2, o_fox, lse, dcat3[:, :, 0:FOX_WIDTH], tq)
    fold = lambda g: g[:, 0:FOX_HEAD_DIM] + g[:, FOX_HEAD_DIM:LANES]
    grads["fox_qnorm_g"], grads["fox_knorm_g"], grads["fox_onorm_g"] = fold(dgq), fold(dgk), fold(dgo)
    dc8 = dcb[:, :, :, 0:2, :].transpose(1, 3, 0, 2, 4).reshape(FOX_HEADS, t_len)
    dc_rows = jnp.concatenate([dc8, jnp.zeros((SM_ROWS - FOX_HEADS, t_len), F32)], axis=0)
    dl_rows, dbias = _fox_cum_bwd(dc_rows, smt, bias_col, n_batch, s_len)
    grads["fox_f_bias"] = dbias[SM_F:SM_F + FOX_HEADS, 0][None, :]
    dsm_rows = jnp.concatenate([dl_rows[0:SM_B], dsmr.transpose(1, 0, 2).reshape(SM_ROWS, t_len)[SM_B:SM_ROWS]], axis=0)

    dproj = jnp.concatenate([dq.reshape(t_len, FOX_WIDTH), dk.reshape(t_len, FOX_WIDTH), dv.reshape(t_len, FOX_WIDTH),
                             dpg.reshape(t_len, 1536), dz.reshape(t_len, GDN_WIDTH), dsmc.reshape(t_len, LANES).astype(BF16)], axis=1)
    dwp = _wgrad(h1, dproj, "wgrad_in", bk=512, bn=P_DIM)
    dwst = _rows_matmul(dsm_rows, h1, "wgrad_in_rows")
    dw_small = dwp[:, P_SMALL:P_SMALL + SM_ROWS] + dwst.T
    grads["w_in"] = jnp.concatenate([dwp[:, 0:1536], dw_small[:, 0:8], dwp[:, 1536:3072], dw_small[:, 8:16], dwp[:, 3072:3584]], axis=1)
    token = grads_ready({k: grads[k] for k in GRAD_GROUPS[2]})
    grad_x, grads["norm_mix_g"] = _in_proj_bwd(dproj, dsm_rows, x2d, norm_mix_g + token, wp, wst, dx1)
    return loss, grad_x.reshape(n_batch, s_len, d), grads


MESH_ID = pl.DeviceIdType.MESH
ANY_SPEC = pl.BlockSpec(memory_space=pl.ANY)


def _place():
    x, y, c = lax.axis_index("x"), lax.axis_index("y"), lax.axis_index("c")
    return x, y, c, [(1 - x, y), (x, 1 - y), (1 - x, 1 - y)]


def _all_gather_body(n, ins, outs, send_sems, recv_sems, local_sems):
    x, y, c, chips = _place()
    me, sibling = (x, y, c), (x, y, 1 - c)

    def copy(a, k, block, to, src=None):
        dst = outs[a].at[4 * block[0] + 2 * block[1] + block[2]]
        return pltpu.make_async_remote_copy(src_ref=dst if src is None else src, dst_ref=dst, send_sem=send_sems.at[a, k],
                                            recv_sem=recv_sems.at[a, k], device_id=to, device_id_type=MESH_ID)

    mine = [] if local_sems is None else [pltpu.make_async_copy(ins[a], outs[a].at[4 * x + 2 * y + c], local_sems.at[a]) for a in range(n)]
    for cp in mine:
        cp.start()
    first = []
    for a in range(n):
        first.append(copy(a, 0, me, sibling, src=ins[a]))
        first += [copy(a, 1 + j, me, (*chip, c), src=ins[a]) for j, chip in enumerate(chips)]
    for cp in first:
        cp.start()
    passed = []
    for j, chip in enumerate(chips):
        for a in range(n):
            copy(a, 1 + j, (*chip, c), me).wait_recv()
            fwd = copy(a, 4 + j, (*chip, c), sibling)
            fwd.start()
            passed.append(fwd)
    for a in range(n):
        copy(a, 0, sibling, me).wait_recv()
        for j, chip in enumerate(chips):
            copy(a, 4 + j, (*chip, 1 - c), me).wait_recv()
    for cp in first + passed:
        cp.wait_send()
    for cp in mine:
        cp.wait()


def _all_gather_hbm(arrs, name):
    n = len(arrs)
    me = 4 * lax.axis_index("x") + 2 * lax.axis_index("y") + lax.axis_index("c")

    def body(*refs):
        _all_gather_body(n, refs[:n], refs[n:2 * n], refs[2 * n], refs[2 * n + 1], None)

    got = pl.pallas_call(
        body, name=name, in_specs=[ANY_SPEC] * n, out_specs=[ANY_SPEC] * n,
        out_shape=[jax.ShapeDtypeStruct((N_DEV,) + a.shape, a.dtype) for a in arrs],
        scratch_shapes=[pltpu.SemaphoreType.DMA((n, 7)), pltpu.SemaphoreType.DMA((n, 7))],
    )(*arrs)
    return [lax.dynamic_update_slice(g, a[None], (me,) + (0,) * a.ndim) for g, a in zip(got, arrs)]


def _pair_exchange(arrs, name):
    n = len(arrs)

    def body(*refs):
        ins, outs = refs[:n], refs[n:2 * n]
        send_sems, recv_sems = refs[2 * n:]
        x, y, c, _ = _place()
        copies = []
        for a in range(n):
            for chip in range(4):
                copies.append(pltpu.make_async_remote_copy(
                    src_ref=ins[a].at[2 * chip + (1 - c)], dst_ref=outs[a].at[chip], send_sem=send_sems.at[a, chip],
                    recv_sem=recv_sems.at[a, chip], device_id=(x, y, 1 - c), device_id_type=MESH_ID))
        for cp in copies:
            cp.start()
        for cp in copies:
            cp.wait()

    return pl.pallas_call(
        body, name=name, in_specs=[ANY_SPEC] * n, out_specs=[ANY_SPEC] * n,
        out_shape=[jax.ShapeDtypeStruct((4,) + a.shape[1:], a.dtype) for a in arrs],
        scratch_shapes=[pltpu.SemaphoreType.DMA((n, 4)), pltpu.SemaphoreType.DMA((n, 4))],
    )(*arrs)


HBM_SPEC = pl.BlockSpec(memory_space=pltpu.HBM)
SEM_SPEC = pl.BlockSpec(memory_space=pltpu.SEMAPHORE)
DATAFLOW = pltpu.SideEffectType.DATAFLOW_SIDE_EFFECTING


def _in_hbm(arrs):
    return [pltpu.with_memory_space_constraint(a, pltpu.HBM) for a in arrs]


def _copies_start(name, srcs, lands, make_copies):
    n = len(srcs)
    n_copies = len(make_copies(srcs, lands, None, None)[0])

    def body(*refs):
        send_sems, recv_sems = refs[2 * n], refs[2 * n + 1]
        for row in make_copies(refs[:n], refs[n:2 * n], send_sems, recv_sems):
            for cp in row:
                cp.start()
        refs[-1][...] = jnp.zeros_like(refs[-1])

    sems = pltpu.SemaphoreType.DMA((n * n_copies,))
    thru = [pltpu.HBM(a.shape, a.dtype) for a in list(srcs) + list(lands)]
    res = pl.pallas_call(
        body, name=name, in_specs=[HBM_SPEC] * (2 * n),
        out_specs=(SEM_SPEC, SEM_SPEC, *[HBM_SPEC] * (2 * n), pl.BlockSpec(memory_space=pltpu.VMEM)),
        out_shape=(sems, sems, *thru, jax.ShapeDtypeStruct((8, LANES), F32)),
        input_output_aliases={i: 2 + i for i in range(2 * n)},
        compiler_params=pltpu.CompilerParams(has_side_effects=DATAFLOW),
    )(*_in_hbm(list(srcs) + list(lands)))
    return res[0], res[1], list(res[2:2 + n]), list(res[2 + n:2 + 2 * n]), res[-1]


def _copies_wait(name, send_sems, recv_sems, srcs, lands, after, make_copies):
    n = len(srcs)

    def body(*refs):
        for row in make_copies(refs[:n], refs[n:2 * n], refs[2 * n], refs[2 * n + 1]):
            for cp in row:
                cp.wait_send()
                cp.wait_recv()

    res = pl.pallas_call(
        body, name=name, in_specs=[HBM_SPEC] * (2 * n) + [SEM_SPEC, SEM_SPEC, ANY_SPEC],
        out_specs=tuple([HBM_SPEC] * (2 * n)),
        out_shape=tuple(pltpu.HBM(a.shape, a.dtype) for a in list(srcs) + list(lands)),
        input_output_aliases={i: i for i in range(2 * n)},
        compiler_params=pltpu.CompilerParams(has_side_effects=DATAFLOW),
    )(*srcs, *lands, send_sems, recv_sems, after)
    return list(res[:n]), list(res[n:])


def _gather_copies(srcs, lands, send_sems, recv_sems):
    if send_sems is None:
        return [[None] * 7]
    x, y, c, _ = _place()
    rows = []
    for a in range(len(srcs)):
        row = []
        for k in range(7):
            r = k + 1
            to = (1 - x if r & 4 else x, 1 - y if r & 2 else y, 1 - c if r & 1 else c)
            row.append(pltpu.make_async_remote_copy(
                src_ref=srcs[a], dst_ref=lands[a].at[4 * x + 2 * y + c], send_sem=send_sems.at[7 * a + k], recv_sem=recv_sems.at[7 * a + k],
                device_id=to, device_id_type=MESH_ID))
        rows.append(row)
    return rows


def _scatter_copies(srcs, lands, send_sems, recv_sems):
    if send_sems is None:
        return [[None] * 7]
    x, y, c, _ = _place()
    rows = []
    for a in range(len(srcs)):
        row = []
        for k in range(7):
            r = k + 1
            to = (1 - x if r & 4 else x, 1 - y if r & 2 else y, 1 - c if r & 1 else c)
            row.append(pltpu.make_async_remote_copy(
                src_ref=srcs[a].at[4 * to[0] + 2 * to[1] + to[2]], dst_ref=lands[a].at[k], send_sem=send_sems.at[7 * a + k],
                recv_sem=recv_sems.at[7 * a + k], device_id=to, device_id_type=MESH_ID))
        rows.append(row)
    return rows


def _chip_copies(srcs, lands, send_sems, recv_sems):
    if send_sems is None:
        return [[None] * 3]
    x, y, c, chips = _place()
    return [[pltpu.make_async_remote_copy(
        src_ref=srcs[a].at[2 * chip[0] + chip[1]], dst_ref=lands[a].at[j], send_sem=send_sems.at[3 * a + j], recv_sem=recv_sems.at[3 * a + j],
        device_id=(*chip, c), device_id_type=MESH_ID) for j, chip in enumerate(chips)] for a in range(len(srcs))]


def _all_gather_vmem(block, name):
    def body(in_ref, out_ref, send_sems, recv_sems, local_sems):
        _all_gather_body(1, [in_ref], [out_ref], send_sems, recv_sems, local_sems)

    vmem = pl.BlockSpec(memory_space=pltpu.VMEM)
    return pl.pallas_call(
        body, name=name, in_specs=[vmem], out_specs=vmem,
        out_shape=jax.ShapeDtypeStruct((N_DEV,) + block.shape, block.dtype),
        scratch_shapes=[pltpu.SemaphoreType.DMA((1, 7)), pltpu.SemaphoreType.DMA((1, 7)), pltpu.SemaphoreType.DMA((1,))],
    )(block)


def _row_tile(rows, cols):
    if rows <= 256:
        return rows
    return 256 if cols <= 512 else 128


def _pair_sum(core, own, got, name):
    _, rows, cols = own.shape
    tr = _row_tile(rows, cols)

    def body(c_ref, own_ref, got_ref, o_ref):
        o_ref[0] = own_ref[0] + got_ref[0]

    return pl.pallas_call(
        body, name=name,
        grid_spec=pltpu.PrefetchScalarGridSpec(
            num_scalar_prefetch=1, grid=(4, rows // tr),
            in_specs=[pl.BlockSpec((1, tr, cols), lambda k, i, c: (2 * k + c[0], i, 0)),
                      pl.BlockSpec((1, tr, cols), lambda k, i, c: (k, i, 0))],
            out_specs=pl.BlockSpec((1, tr, cols), lambda k, i, c: (k, i, 0))),
        out_shape=jax.ShapeDtypeStruct((4, rows, cols), F32),
        compiler_params=_cparams(("parallel", "parallel")),
    )(core, own, got)


def _adamw(w, g, m, v):
    m_new = ADAM_B1 * m + (1.0 - ADAM_B1) * g
    v_new = ADAM_B2 * v + (1.0 - ADAM_B2) * (g * g)
    m_hat = m_new / (1.0 - ADAM_B1 ** ADAM_STEP)
    v_hat = v_new / (1.0 - ADAM_B2 ** ADAM_STEP)
    delta = -ADAM_LR * (m_hat / (jnp.sqrt(v_hat) + ADAM_EPS) + ADAM_WD * w)
    return delta, m_new, v_new


def _sum_adam(chip, sums, parts, w, m, v, name):
    n_parts, rows, cols = parts.shape
    tr = _row_tile(rows, cols)

    def body(chip_ref, *refs):
        if sums is not None:
            g = refs[0][0].astype(F32)
            refs = refs[1:]
        p_ref, w_ref, m_ref, v_ref, g_ref, d_ref, mo_ref, vo_ref = refs
        for k in range(n_parts):
            g = p_ref[k].astype(F32) if (k == 0 and sums is None) else g + p_ref[k].astype(F32)
        g_ref[...] = g
        d_ref[...], mo_ref[...], vo_ref[...] = _adamw(w_ref[...], g, m_ref[...], v_ref[...])

    tile = pl.BlockSpec((tr, cols), lambda i, ch: (i, 0))
    out = jax.ShapeDtypeStruct((rows, cols), F32)
    own = [] if sums is None else [pl.BlockSpec((1, tr, cols), lambda i, ch: (ch[0], i, 0))]
    return pl.pallas_call(
        body, name=name,
        grid_spec=pltpu.PrefetchScalarGridSpec(
            num_scalar_prefetch=1, grid=(rows // tr,),
            in_specs=own + [pl.BlockSpec((n_parts, tr, cols), lambda i, ch: (0, i, 0)), tile, tile, tile],
            out_specs=[tile, tile, tile, tile]),
        out_shape=[out, out, out, out],
        compiler_params=_cparams(("parallel",)),
    )(chip, *([] if sums is None else [sums]), parts, w, m, v)


SHARDED = ("w_in", "gdn_conv_w", "w_out", "w_cq", "w_ckv", "w_co", "w_mlp1", "w_mlp2")
COLUMN_SHARDED = ("w_in", "gdn_conv_w", "w_co", "w_mlp1")
REPLICATED = ("norm_mix_g", "fox_qnorm_g", "fox_knorm_g", "fox_f_bias", "fox_onorm_g", "gdn_A_log", "gdn_dt_bias", "gdn_onorm_g",
              "norm_xattn_g", "mem_norm_g", "xattn_qnorm_g", "xattn_knorm_g", "norm_mlp_g")
WEIGHTS = ("norm_mix_g", "w_in", "fox_qnorm_g", "fox_knorm_g", "fox_f_bias", "fox_onorm_g", "gdn_conv_w", "gdn_A_log", "gdn_dt_bias",
           "gdn_onorm_g", "w_out", "norm_xattn_g", "mem_norm_g", "w_cq", "w_ckv", "xattn_qnorm_g", "xattn_knorm_g", "w_co",
           "norm_mlp_g", "w_mlp1", "w_mlp2")
PACK_ROWS = 16
LOSS_ROW = len(REPLICATED)


def _whole(name, gathered):
    if name in COLUMN_SHARDED:
        return gathered.transpose(1, 0, 2).reshape(gathered.shape[1], N_DEV * gathered.shape[2])
    return gathered.reshape(N_DEV * gathered.shape[1], gathered.shape[2])


def _blocks(name, whole):
    if whole.ndim == 3:
        return whole
    if name in COLUMN_SHARDED:
        rows, cols = whole.shape
        return whole.reshape(rows, N_DEV, cols // N_DEV).transpose(1, 0, 2)
    return whole.reshape(N_DEV, whole.shape[0] // N_DEV, whole.shape[1])


def _pack(vals, fill=0.0):
    rows = [jnp.pad(vals[k], ((0, 0), (0, D_MODEL - vals[k].shape[1])), constant_values=fill) for k in REPLICATED]
    rows.append(jnp.full((PACK_ROWS - len(rows), D_MODEL), fill, F32))
    return jnp.concatenate(rows, axis=0)


def kernel(x, mem, norm_mix_g, w_in, fox_qnorm_g, fox_knorm_g, fox_f_bias, fox_onorm_g, gdn_conv_w, gdn_A_log, gdn_dt_bias, gdn_onorm_g, w_out, norm_xattn_g, mem_norm_g, w_cq, w_ckv, xattn_qnorm_g, xattn_knorm_g, w_co, norm_mlp_g, w_mlp1, w_mlp2, loss_target, m_norm_mix_g, m_w_in, m_fox_qnorm_g, m_fox_knorm_g, m_fox_f_bias, m_fox_onorm_g, m_gdn_conv_w, m_gdn_A_log, m_gdn_dt_bias, m_gdn_onorm_g, m_w_out, m_norm_xattn_g, m_mem_norm_g, m_w_cq, m_w_ckv, m_xattn_qnorm_g, m_xattn_knorm_g, m_w_co, m_norm_mlp_g, m_w_mlp1, m_w_mlp2, v_norm_mix_g, v_w_in, v_fox_qnorm_g, v_fox_knorm_g, v_fox_f_bias, v_fox_onorm_g, v_gdn_conv_w, v_gdn_A_log, v_gdn_dt_bias, v_gdn_onorm_g, v_w_out, v_norm_xattn_g, v_mem_norm_g, v_w_cq, v_w_ckv, v_xattn_qnorm_g, v_xattn_knorm_g, v_w_co, v_norm_mlp_g, v_w_mlp1, v_w_mlp2):
    given = dict(locals())
    w = {k: given[k] for k in WEIGHTS}
    m = {k: given["m_" + k] for k in WEIGHTS}
    v = {k: given["v_" + k] for k in WEIGHTS}

    core = lax.axis_index("c").astype(jnp.int32).reshape(1)
    chip = (2 * lax.axis_index("x") + lax.axis_index("y")).astype(jnp.int32).reshape(1)
    me = 4 * lax.axis_index("x") + 2 * lax.axis_index("y") + lax.axis_index("c")

    shards = {k: w[k][0] if k == "gdn_conv_w" else w[k][0].astype(BF16) for k in SHARDED}
    early = [k for k in SHARDED if k not in LATE_WEIGHTS]
    whole = {k: _whole(k, g) for k, g in zip(early, _all_gather_hbm([shards[k] for k in early], "gather_early"))}
    late_shards = [shards[k] for k in LATE_WEIGHTS]
    late_lands = [lax.empty((N_DEV,) + s.shape, s.dtype) for s in late_shards]
    gather = _copies_start("gather_late_start", late_shards, late_lands, _gather_copies)

    def late_weights(after):
        srcs, lands = _copies_wait("gather_late_wait", gather[0], gather[1], gather[2], gather[3], after, _gather_copies)
        blocks = {k: lax.dynamic_update_slice(land, src[None], (me, 0, 0)) for k, src, land in zip(LATE_WEIGHTS, srcs, lands)}
        return {k: b if k == "w_mlp1" else _whole(k, b) for k, b in blocks.items()}

    pending = []

    def grads_ready(group):
        names = list(group)
        tag = str(len(pending))
        own = [_blocks(k, group[k]) for k in names]
        if "w_in" in names:
            got = _pair_exchange(own, "grad_pair_exchange_" + tag)
            srcs = [_pair_sum(core, o, g, "grad_pair_sum_" + k) for k, o, g in zip(names, own, got)]
            copies, index, n_parts = _chip_copies, chip, 3
        else:
            srcs, copies, index, n_parts = own, _scatter_copies, me.astype(jnp.int32).reshape(1), 7
        lands = [lax.empty((n_parts,) + s.shape[1:], s.dtype) for s in srcs]
        started = _copies_start("grad_exchange_start_" + tag, srcs, lands, copies)
        pending.append((names, started, copies, index))
        return started[4][0, 0]

    small = {k: w[k] for k in REPLICATED}
    loss_local, grad_x, grads = _local_step(x, mem, loss_target, **small, **whole, late_weights=late_weights,
                                            grads_ready=grads_ready, first_token=gather[4][0, 0])

    out_g, out_d, out_m, out_v = {}, {}, {}, {}
    after = grad_x
    for tag, (names, started, copies, index) in enumerate(pending):
        srcs, parts = _copies_wait("grad_exchange_wait_" + str(tag), started[0], started[1], started[2], started[3], after, copies)
        for k, s, p in zip(names, srcs, parts):
            res = _sum_adam(index, s, p, w[k][0], m[k][0], v[k][0], "adam_" + k)
            out_g[k], out_d[k], out_m[k], out_v[k] = (r[None] for r in res)
            after = res[0]

    packed = _pack({k: grads[k] for k in REPLICATED}).at[LOSS_ROW, 0].set(loss_local)
    everyone = _all_gather_vmem(packed, "gather_small")
    res = _sum_adam(chip, None, everyone, _pack(small), _pack({k: m[k] for k in REPLICATED}),
                    _pack({k: v[k] for k in REPLICATED}, fill=1.0), "adam_small")
    for i, k in enumerate(REPLICATED):
        n = w[k].shape[1]
        out_g[k], out_d[k], out_m[k], out_v[k] = (r[i:i + 1, 0:n] for r in res)
    loss = res[0][LOSS_ROW, 0]

    return (loss, grad_x, *[out_g[k] for k in WEIGHTS], *[out_d[k] for k in WEIGHTS], *[out_m[k] for k in WEIGHTS],
            *[out_v[k] for k in WEIGHTS])
```

```python
import functools

import jax
import jax.numpy as jnp
import numpy as np
from jax import lax
from jax.experimental import pallas as pl
from jax.experimental.pallas import tpu as pltpu

F32 = jnp.float32
BF16 = jnp.bfloat16

D_MODEL = 1024
FOX_HEADS = 8
FOX_HEAD_DIM = 64
FOX_WIDTH = 512
GDN_HEADS = 4
GDN_HEAD_DIM = 128
GDN_WIDTH = 512
CONV_WIDTH = 4
GDN_CHUNK = 64
GDN_GROUP = 4
FOX_BLOCK = 512
XATTN_HEADS = 4
XATTN_HEAD_DIM = 128
XATTN_WIDTH = 512
D_FF = 4096
EPS = 1e-6
NEG_INF = -1e30
N_DEV = 8

ADAM_LR = 0.001
ADAM_B1 = 0.9
ADAM_B2 = 0.999
ADAM_EPS = 1e-08
ADAM_WD = 0.01
ADAM_STEP = 10

P_FOX = 0
P_GDN = 1536
P_Z = 3072
P_SMALL = 3584
P_DIM = 3712
SM_F = 0
SM_B = 8
SM_A = 12
SM_ROWS = 16

LANES = 128
VMEM_LIMIT = 56 * 1024 * 1024

NN = (((1,), (0,)), ((), ()))
NT = (((1,), (1,)), ((), ()))
TN = (((0,), (0,)), ((), ()))


def _dot(a, b, dims=NN):
    return lax.dot_general(a.astype(BF16), b.astype(BF16), dims, preferred_element_type=F32)


def _cparams(sem=None):
    kw = dict(vmem_limit_bytes=VMEM_LIMIT)
    if sem is not None:
        kw["dimension_semantics"] = sem
    return pltpu.CompilerParams(**kw)


def _sigmoid(x):
    return 0.5 * (jnp.tanh(0.5 * x) + 1.0)


def _softplus(x):
    return jnp.maximum(x, 0.0) + jnp.log1p(jnp.exp(-jnp.abs(x)))


def _log_sigmoid(x):
    return -_softplus(-x)


def _rms(x, g):
    r = lax.rsqrt(jnp.mean(x * x, axis=-1, keepdims=True) + EPS)
    return x * r * g


def _rms_bwd(x, g, dy):
    r = lax.rsqrt(jnp.mean(x * x, axis=-1, keepdims=True) + EPS)
    xh = x * r
    dg = jnp.sum(dy * xh, axis=0, keepdims=True)
    dyg = dy * g
    dx = r * (dyg - xh * jnp.mean(dyg * xh, axis=-1, keepdims=True))
    return dx, dg


def _pair_stat(t, m0):
    s0 = jnp.sum(jnp.where(m0, t, 0.0), axis=-1, keepdims=True)
    s1 = jnp.sum(jnp.where(m0, 0.0, t), axis=-1, keepdims=True)
    return jnp.where(m0, s0, s1)


def _rms_pair(x, g, m0):
    r = lax.rsqrt(_pair_stat(x * x, m0) * (1.0 / FOX_HEAD_DIM) + EPS)
    return x * r * g


def _rms_pair_bwd(x, g, dy, m0):
    r = lax.rsqrt(_pair_stat(x * x, m0) * (1.0 / FOX_HEAD_DIM) + EPS)
    xh = x * r
    dg = jnp.sum(dy * xh, axis=0, keepdims=True)
    dyg = dy * g
    dx = r * (dyg - xh * (_pair_stat(dyg * xh, m0) * (1.0 / FOX_HEAD_DIM)))
    return dx, dg


@jax.custom_vjp
def _mm_nn(a, b):
    return _dot(a, b, NN)


_mm_nn.defvjp(lambda a, b: (_dot(a, b, NN), (a, b)),
              lambda r, g: (_dot(g, r[1], NT), _dot(r[0], g, TN)))


@jax.custom_vjp
def _mm_nt(a, b):
    return _dot(a, b, NT)


_mm_nt.defvjp(lambda a, b: (_dot(a, b, NT), (a, b)),
              lambda r, g: (_dot(g, r[1], NN), _dot(g, r[0], TN)))


@jax.custom_vjp
def _mm_tn(a, b):
    return _dot(a, b, TN)


_mm_tn.defvjp(lambda a, b: (_dot(a, b, TN), (a, b)),
              lambda r, g: (_dot(r[1], g, NT), _dot(r[0], g, NN)))


def _dot3(a, b, dims):
    ah = a.astype(BF16)
    al = (a - ah.astype(F32)).astype(BF16)
    bh = b.astype(BF16)
    bl = (b - bh.astype(F32)).astype(BF16)
    d = functools.partial(lax.dot_general, dimension_numbers=dims, preferred_element_type=F32)
    return d(ah, bh) + d(ah, bl) + d(al, bh)


def _neumann_inverses(mats):
    c = mats[0].shape[0]
    eye = (lax.broadcasted_iota(jnp.int32, (c, c), 0) == lax.broadcasted_iota(jnp.int32, (c, c), 1)).astype(F32)
    xs = [eye - a for a in mats]
    ps = list(mats)
    k = 2
    while k < c + 1:
        ps = [_dot3(p, p, NN) for p in ps]
        xs = [x + _dot3(x, p, NN) for x, p in zip(xs, ps)]
        k *= 2
    return xs


@jax.custom_vjp
def _unit_lower_inverses(mats):
    return _neumann_inverses(mats)


def _unit_lower_inverses_fwd(mats):
    ts = _neumann_inverses(mats)
    return ts, ts


def _unit_lower_inverses_bwd(ts, gs):
    left = [_dot3(t, g, TN) for t, g in zip(ts, gs)]
    return ([-_dot3(m, t, NT) for m, t in zip(left, ts)],)


_unit_lower_inverses.defvjp(_unit_lower_inverses_fwd, _unit_lower_inverses_bwd)


def _wgrad(a, b, name, bk=1024, bn=1024, bt=512, column_blocks=None):
    t_len, k_len = a.shape
    n_len = b.shape[1]
    bk, bn, bt = min(bk, k_len), min(bn, n_len), min(bt, t_len)
    nt = t_len // bt

    def body(a_ref, b_ref, o_ref, acc_ref):
        t = pl.program_id(2)

        @pl.when(t == 0)
        def _():
            acc_ref[...] = jnp.zeros_like(acc_ref)

        acc_ref[...] += _dot(a_ref[...], b_ref[...], TN)

        @pl.when(t == nt - 1)
        def _():
            if column_blocks:
                for jj in range(bn // column_blocks):
                    o_ref[jj] = acc_ref[:, jj * column_blocks:(jj + 1) * column_blocks]
            else:
                o_ref[...] = acc_ref[...]

    if column_blocks:
        out_spec = pl.BlockSpec((bn // column_blocks, bk, column_blocks), lambda i, j, t: (j, i, 0))
        out_shape = jax.ShapeDtypeStruct((n_len // column_blocks, k_len, column_blocks), F32)
    else:
        out_spec = pl.BlockSpec((bk, bn), lambda i, j, t: (i, j))
        out_shape = jax.ShapeDtypeStruct((k_len, n_len), F32)
    return pl.pallas_call(
        body, name=name, grid=(k_len // bk, n_len // bn, nt),
        in_specs=[pl.BlockSpec((bt, bk), lambda i, j, t: (t, i)), pl.BlockSpec((bt, bn), lambda i, j, t: (t, j))],
        out_specs=out_spec, out_shape=out_shape,
        scratch_shapes=[pltpu.VMEM((bk, bn), F32)],
        compiler_params=_cparams(("parallel", "parallel", "arbitrary")),
    )(a, b)


def _rows_matmul(a, b, name, bt=512):
    r_len, t_len = a.shape
    n_len = b.shape[1]
    bt = min(bt, t_len)
    nt = t_len // bt

    def body(a_ref, b_ref, o_ref):
        t = pl.program_id(0)

        @pl.when(t == 0)
        def _():
            o_ref[...] = jnp.zeros_like(o_ref)

        o_ref[...] += _dot(a_ref[...], b_ref[...], NN)

    return pl.pallas_call(
        body, name=name, grid=(nt,),
        in_specs=[pl.BlockSpec((r_len, bt), lambda t: (0, t)), pl.BlockSpec((bt, n_len), lambda t: (t, 0))],
        out_specs=pl.BlockSpec((r_len, n_len), lambda t: (0, 0)),
        out_shape=jax.ShapeDtypeStruct((r_len, n_len), F32),
        compiler_params=_cparams(("arbitrary",)),
    )(a, b)


def _in_proj(x, g, wp, wst, tm=256):
    t_len, d = x.shape
    tm = min(tm, t_len)

    def body(x_ref, g_ref, wp_ref, wst_ref, h_ref, fox_ref, gdn_ref, z_ref, sm_ref, smt_ref):
        h = _rms(x_ref[...], g_ref[...]).astype(BF16)
        h_ref[...] = h
        p = _dot(h, wp_ref[...], NN)
        fox_ref[...] = p[:, P_FOX:P_GDN]
        gdn_ref[...] = p[:, P_GDN:P_Z]
        z_ref[...] = p[:, P_Z:P_SMALL]
        sm_ref[...] = p[:, P_SMALL:P_DIM]
        smt_ref[...] = _dot(wst_ref[...], h, NT)

    row = lambda i: (i, 0)
    fixed = lambda i: (0, 0)
    return pl.pallas_call(
        body, name="in_proj", grid=(t_len // tm,),
        in_specs=[pl.BlockSpec((tm, d), row), pl.BlockSpec((1, d), fixed), pl.BlockSpec((d, P_DIM), fixed),
                  pl.BlockSpec((SM_ROWS, d), fixed)],
        out_specs=[pl.BlockSpec((tm, d), row), pl.BlockSpec((tm, 1536), row), pl.BlockSpec((tm, 1536), row),
                   pl.BlockSpec((tm, 512), row), pl.BlockSpec((tm, LANES), row), pl.BlockSpec((SM_ROWS, tm), lambda i: (0, i))],
        out_shape=[jax.ShapeDtypeStruct((t_len, d), BF16), jax.ShapeDtypeStruct((t_len, 1536), F32),
                   jax.ShapeDtypeStruct((t_len, 1536), F32), jax.ShapeDtypeStruct((t_len, 512), F32),
                   jax.ShapeDtypeStruct((t_len, LANES), F32), jax.ShapeDtypeStruct((SM_ROWS, t_len), F32)],
        compiler_params=_cparams(("parallel",)),
    )(x, g, wp, wst)


def _in_proj_bwd(dproj, dsmt, x, g, wp, wst, dx1, tm=256):
    t_len, d = x.shape
    tm = min(tm, t_len)

    def body(dp_ref, dst_ref, x_ref, g_ref, wp_ref, wst_ref, dx1_ref, dx_ref, dg_ref):
        i = pl.program_id(0)
        dh = _dot(dp_ref[...], wp_ref[...], NT) + _dot(dst_ref[...], wst_ref[...], TN)
        dxn, dg = _rms_bwd(x_ref[...], g_ref[...], dh)
        dx_ref[...] = dx1_ref[...] + dxn

        @pl.when(i == 0)
        def _():
            dg_ref[...] = jnp.zeros_like(dg_ref)

        dg_ref[...] += dg

    row = lambda i: (i, 0)
    fixed = lambda i: (0, 0)
    return pl.pallas_call(
        body, name="in_proj_bwd", grid=(t_len // tm,),
        in_specs=[pl.BlockSpec((tm, P_DIM), row), pl.BlockSpec((SM_ROWS, tm), lambda i: (0, i)), pl.BlockSpec((tm, d), row),
                  pl.BlockSpec((1, d), fixed), pl.BlockSpec((d, P_DIM), fixed), pl.BlockSpec((SM_ROWS, d), fixed),
                  pl.BlockSpec((tm, d), row)],
        out_specs=[pl.BlockSpec((tm, d), row), pl.BlockSpec((1, d), fixed)],
        out_shape=[jax.ShapeDtypeStruct((t_len, d), F32), jax.ShapeDtypeStruct((1, d), F32)],
        compiler_params=_cparams(("arbitrary",)),
    )(dproj, dsmt, x, g, wp, wst, dx1)


def _fox_cum(smt, bias_col, n_batch, s_len, ck=256):
    ck = min(ck, s_len)

    def body(s_ref, b_ref, c_ref):
        tri = (lax.broadcasted_iota(jnp.int32, (ck, ck), 0) <= lax.broadcasted_iota(jnp.int32, (ck, ck), 1)).astype(F32)
        carry = jnp.zeros((SM_ROWS, 1), F32)
        for r in range(s_len // ck):
            ls = _log_sigmoid(s_ref[:, r * ck:(r + 1) * ck] + b_ref[...])
            c = jnp.dot(ls, tri, precision=lax.Precision.HIGHEST, preferred_element_type=F32) + carry
            c_ref[:, r * ck:(r + 1) * ck] = c
            carry = c[:, ck - 1:ck]

    return pl.pallas_call(
        body, name="fox_cum", grid=(n_batch,),
        in_specs=[pl.BlockSpec((SM_ROWS, s_len), lambda b: (0, b)), pl.BlockSpec((SM_ROWS, 1), lambda b: (0, 0))],
        out_specs=pl.BlockSpec((SM_ROWS, s_len), lambda b: (0, b)),
        out_shape=jax.ShapeDtypeStruct(smt.shape, F32),
        compiler_params=_cparams(("parallel",)),
    )(smt, bias_col)


def _fox_cum_bwd(dc, smt, bias_col, n_batch, s_len, ck=256):
    ck = min(ck, s_len)
    nr = s_len // ck

    def body(dc_ref, s_ref, b_ref, dl_ref, db_ref):
        b = pl.program_id(0)
        tri = (lax.broadcasted_iota(jnp.int32, (ck, ck), 0) >= lax.broadcasted_iota(jnp.int32, (ck, ck), 1)).astype(F32)
        carry = jnp.zeros((SM_ROWS, 1), F32)
        tot = jnp.zeros((SM_ROWS, 1), F32)
        for r in reversed(range(nr)):
            sl = slice(r * ck, (r + 1) * ck)
            dls = jnp.dot(dc_ref[:, sl], tri, precision=lax.Precision.HIGHEST, preferred_element_type=F32) + carry
            carry = dls[:, 0:1]
            dl = dls * (1.0 - _sigmoid(s_ref[:, sl] + b_ref[...]))
            dl_ref[:, sl] = dl
            tot = tot + jnp.sum(dl, axis=1, keepdims=True)

        @pl.when(b == 0)
        def _():
            db_ref[...] = jnp.zeros_like(db_ref)

        db_ref[...] += jnp.broadcast_to(tot, db_ref.shape)

    return pl.pallas_call(
        body, name="fox_cum_bwd", grid=(n_batch,),
        in_specs=[pl.BlockSpec((SM_ROWS, s_len), lambda b: (0, b)), pl.BlockSpec((SM_ROWS, s_len), lambda b: (0, b)),
                  pl.BlockSpec((SM_ROWS, 1), lambda b: (0, 0))],
        out_specs=[pl.BlockSpec((SM_ROWS, s_len), lambda b: (0, b)), pl.BlockSpec((SM_ROWS, LANES), lambda b: (0, 0))],
        out_shape=[jax.ShapeDtypeStruct(smt.shape, F32), jax.ShapeDtypeStruct((SM_ROWS, LANES), F32)],
        compiler_params=_cparams(("arbitrary",)),
    )(dc, smt, bias_col)


def _fox_diagonal_mask(tq):
    return lax.broadcasted_iota(jnp.int32, (tq, tq), 1) <= lax.broadcasted_iota(jnp.int32, (tq, tq), 0)


def _fox_fwd(pf, cb, gq2, gk2, go2, tq=256):
    n_batch, s_len, _ = pf.shape
    tq = min(tq, s_len)
    nq = s_len // tq
    scale = FOX_HEAD_DIM ** -0.5

    def body(q_ref, k_ref, v_ref, c_ref, gq_ref, gk_ref, go_ref, o_ref, on_ref, lse_ref, kh_ref, vh_ref):
        j = pl.program_id(1)
        i = pl.program_id(2)
        m0 = lax.broadcasted_iota(jnp.int32, (1, LANES), 1) < FOX_HEAD_DIM

        @pl.when(i == 0)
        def _():
            kn = _rms_pair(k_ref[0], gk_ref[...], m0)
            kh_ref[0] = jnp.where(m0, kn, 0.0).astype(BF16)
            kh_ref[1] = jnp.where(m0, 0.0, kn).astype(BF16)
            v = v_ref[0]
            vh_ref[0] = jnp.where(m0, v, 0.0).astype(BF16)
            vh_ref[1] = jnp.where(m0, 0.0, v).astype(BF16)

        qb = (_rms_pair(q_ref[0], gq_ref[...], m0) * scale).astype(BF16)

        def step(kb, carry, diagonal=False):
            ms, ls, acc = carry
            off = pl.multiple_of(kb * tq, tq)
            new_m, new_l, alphas, pv = [], [], [], []
            for hh in range(2):
                s = _dot(qb, kh_ref[hh, pl.ds(off, tq), :], NT)
                s = s - c_ref[0, kb, pl.ds(2 * j + hh, 1), :]
                if diagonal:
                    s = jnp.where(_fox_diagonal_mask(tq), s, NEG_INF)
                m_new = jnp.maximum(ms[hh], jnp.max(s, axis=-1, keepdims=True))
                alpha = jnp.exp(ms[hh] - m_new)
                p = jnp.exp(s - m_new)
                new_l.append(alpha * ls[hh] + jnp.sum(p, axis=-1, keepdims=True))
                new_m.append(m_new)
                alphas.append(alpha)
                pv.append(_dot(p, vh_ref[hh, pl.ds(off, tq), :], NN))
            acc = jnp.where(m0, alphas[0], alphas[1]) * acc + pv[0] + pv[1]
            return tuple(new_m), tuple(new_l), acc

        init_m = (jnp.full((tq, 1), NEG_INF, F32),) * 2
        init_l = (jnp.zeros((tq, 1), F32),) * 2
        carry = lax.fori_loop(0, i, step, (init_m, init_l, jnp.zeros((tq, LANES), F32)))
        ms, ls, acc = step(i, carry, diagonal=True)
        o = acc / jnp.where(m0, ls[0], ls[1])
        o_ref[0] = o
        on_ref[0] = _rms_pair(o, go_ref[...], m0).astype(BF16)
        lse_ref[0] = jnp.where(m0, ms[0] + jnp.log(ls[0]), ms[1] + jnp.log(ls[1]))

    fixed = lambda b, j, i: (0, 0)
    tile = lambda b, j, i: (b, i, j)
    return pl.pallas_call(
        body, name="fox_fwd", grid=(n_batch, 4, nq),
        in_specs=[pl.BlockSpec((1, tq, LANES), tile), pl.BlockSpec((1, s_len, LANES), lambda b, j, i: (b, 0, 4 + j)),
                  pl.BlockSpec((1, s_len, LANES), lambda b, j, i: (b, 0, 8 + j)),
                  pl.BlockSpec((1, nq, SM_ROWS, tq), lambda b, j, i: (b, 0, 0, 0)),
                  pl.BlockSpec((1, LANES), fixed), pl.BlockSpec((1, LANES), fixed), pl.BlockSpec((1, LANES), fixed)],
        out_specs=[pl.BlockSpec((1, tq, LANES), tile), pl.BlockSpec((1, tq, LANES), tile), pl.BlockSpec((1, tq, LANES), tile)],
        out_shape=[jax.ShapeDtypeStruct((n_batch, s_len, FOX_WIDTH), F32), jax.ShapeDtypeStruct((n_batch, s_len, FOX_WIDTH), BF16),
                   jax.ShapeDtypeStruct((n_batch, s_len, FOX_WIDTH), F32)],
        scratch_shapes=[pltpu.VMEM((2, s_len, LANES), BF16), pltpu.VMEM((2, s_len, LANES), BF16)],
        compiler_params=_cparams(("parallel", "parallel", "arbitrary")),
    )(pf, pf, pf, cb, gq2, gk2, go2)


def _fox_bwd(pf, cb, gq2, gk2, go2, o, lse, don, tq=256):
    n_batch, s_len, _ = pf.shape
    tq = min(tq, s_len)
    nq = s_len // tq
    scale = FOX_HEAD_DIM ** -0.5

    def body(q_ref, k_ref, v_ref, c_ref, gq_ref, gk_ref, go_ref, o_ref, lse_ref, don_ref,
             dq_ref, dk_ref, dv_ref, dc_ref, dgq_ref, dgk_ref, dgo_ref, kh_ref, vh_ref, dka_ref, dva_ref, dca_ref):
        b = pl.program_id(0)
        j = pl.program_id(1)
        i = pl.program_id(2)
        m0 = lax.broadcasted_iota(jnp.int32, (1, LANES), 1) < FOX_HEAD_DIM

        @pl.when((b == 0) & (j == 0) & (i == 0))
        def _():
            dgq_ref[...] = jnp.zeros_like(dgq_ref)
            dgk_ref[...] = jnp.zeros_like(dgk_ref)
            dgo_ref[...] = jnp.zeros_like(dgo_ref)

        @pl.when(i == 0)
        def _():
            kn = _rms_pair(k_ref[0], gk_ref[...], m0)
            kh_ref[0] = jnp.where(m0, kn, 0.0).astype(BF16)
            kh_ref[1] = jnp.where(m0, 0.0, kn).astype(BF16)
            v = v_ref[0]
            vh_ref[0] = jnp.where(m0, v, 0.0).astype(BF16)
            vh_ref[1] = jnp.where(m0, 0.0, v).astype(BF16)
            dka_ref[...] = jnp.zeros_like(dka_ref)
            dva_ref[...] = jnp.zeros_like(dva_ref)
            dca_ref[...] = jnp.zeros_like(dca_ref)

        q = q_ref[0]
        qn = _rms_pair(q, gq_ref[...], m0)
        qs = qn * scale
        qb = qs.astype(BF16)
        qh = (jnp.where(m0, qs, 0.0).astype(BF16), jnp.where(m0, 0.0, qs).astype(BF16))
        ot = o_ref[0]
        do, dgo = _rms_pair_bwd(ot, go_ref[...], don_ref[0], m0)
        dgo_ref[...] += dgo
        dd = do * ot
        delta = (jnp.sum(jnp.where(m0, dd, 0.0), axis=-1, keepdims=True), jnp.sum(jnp.where(m0, 0.0, dd), axis=-1, keepdims=True))
        doh = (jnp.where(m0, do, 0.0).astype(BF16), jnp.where(m0, 0.0, do).astype(BF16))
        lse_t = lse_ref[0]
        lse_h = (lse_t[:, 0:1], lse_t[:, FOX_HEAD_DIM:FOX_HEAD_DIM + 1])

        def step(kb, carry, diagonal=False):
            dqn, rs = carry
            rs = list(rs)
            off = pl.multiple_of(kb * tq, tq)
            for hh in range(2):
                kblk = kh_ref[hh, pl.ds(off, tq), :]
                vblk = vh_ref[hh, pl.ds(off, tq), :]
                s = _dot(qb, kblk, NT)
                s = s - c_ref[0, kb, pl.ds(2 * j + hh, 1), :]
                if diagonal:
                    s = jnp.where(_fox_diagonal_mask(tq), s, NEG_INF)
                p = jnp.exp(s - lse_h[hh])
                dp = _dot(doh[hh], vblk, NT)
                ds = p * (dp - delta[hh])
                dva_ref[pl.ds(off, tq), :] += _dot(p, doh[hh], TN)
                dka_ref[pl.ds(off, tq), :] += _dot(ds, qh[hh], TN)
                dca_ref[kb, hh:hh + 1, :] += -jnp.sum(ds, axis=0, keepdims=True)
                rs[hh] = rs[hh] + jnp.sum(ds, axis=-1, keepdims=True)
                dqn = dqn + _dot(ds, kblk, NN)
            return dqn, tuple(rs)

        carry = lax.fori_loop(0, i, step, (jnp.zeros((tq, LANES), F32), (jnp.zeros((tq, 1), F32),) * 2))
        dqn, rs = step(i, carry, diagonal=True)
        dqn = dqn * scale
        rs_rows = jnp.where(m0, rs[0], rs[1]).T
        dca_ref[i, 0:1, :] += rs_rows[0:1, :]
        dca_ref[i, 1:2, :] += rs_rows[FOX_HEAD_DIM:FOX_HEAD_DIM + 1, :]
        dq, dgq = _rms_pair_bwd(q, gq_ref[...], dqn, m0)
        dq_ref[0] = dq.astype(BF16)
        dgq_ref[...] += dgq

        @pl.when(i == nq - 1)
        def _():
            dk, dgk = _rms_pair_bwd(k_ref[0], gk_ref[...], dka_ref[...], m0)
            dk_ref[0] = dk.astype(BF16)
            dgk_ref[...] += dgk
            dv_ref[0] = dva_ref[...].astype(BF16)
            dc_ref[0, 0] = dca_ref[...]

    fixed = lambda b, j, i: (0, 0)
    tile = lambda b, j, i: (b, i, j)
    full = lambda b, j, i: (b, 0, j)
    wide = jax.ShapeDtypeStruct((n_batch, s_len, FOX_WIDTH), BF16)
    gain = jax.ShapeDtypeStruct((1, LANES), F32)
    return pl.pallas_call(
        body, name="fox_bwd", grid=(n_batch, 4, nq),
        in_specs=[pl.BlockSpec((1, tq, LANES), tile), pl.BlockSpec((1, s_len, LANES), lambda b, j, i: (b, 0, 4 + j)),
                  pl.BlockSpec((1, s_len, LANES), lambda b, j, i: (b, 0, 8 + j)),
                  pl.BlockSpec((1, nq, SM_ROWS, tq), lambda b, j, i: (b, 0, 0, 0)),
                  pl.BlockSpec((1, LANES), fixed), pl.BlockSpec((1, LANES), fixed), pl.BlockSpec((1, LANES), fixed),
                  pl.BlockSpec((1, tq, LANES), tile), pl.BlockSpec((1, tq, LANES), tile), pl.BlockSpec((1, tq, LANES), tile)],
        out_specs=[pl.BlockSpec((1, tq, LANES), tile), pl.BlockSpec((1, s_len, LANES), full), pl.BlockSpec((1, s_len, LANES), full),
                   pl.BlockSpec((1, 1, nq, 8, tq), lambda b, j, i: (b, j, 0, 0, 0)),
                   pl.BlockSpec((1, LANES), fixed), pl.BlockSpec((1, LANES), fixed), pl.BlockSpec((1, LANES), fixed)],
        out_shape=[wide, wide, wide, jax.ShapeDtypeStruct((n_batch, 4, nq, 8, tq), F32), gain, gain, gain],
        scratch_shapes=[pltpu.VMEM((2, s_len, LANES), BF16), pltpu.VMEM((2, s_len, LANES), BF16),
                        pltpu.VMEM((s_len, LANES), F32), pltpu.VMEM((s_len, LANES), F32), pltpu.VMEM((nq, 8, tq), F32)],
        compiler_params=_cparams(("arbitrary", "arbitrary", "arbitrary")),
    )(pf, pf, pf, cb, gq2, gk2, go2, o, lse, don)


def _shift_down(x, k):
    row = lax.broadcasted_iota(jnp.int32, x.shape, 0)
    return jnp.where(row >= k, pltpu.roll(x, k, 0), 0.0)


def _shift_up(x, k):
    n = x.shape[0]
    row = lax.broadcasted_iota(jnp.int32, x.shape, 0)
    return jnp.where(row < n - k, pltpu.roll(x, n - k, 0), 0.0)


def _conv_silu(x, w):
    y = w[3:4] * x + w[2:3] * _shift_down(x, 1) + w[1:2] * _shift_down(x, 2) + w[0:1] * _shift_down(x, 3)
    return y, y * _sigmoid(y)


def _gdn_pre(pg, conv_w):
    n_batch, s_len, width = pg.shape
    ncb = width // LANES

    def body(x_ref, w_ref, o_ref):
        cb = pl.program_id(1)
        _, s = _conv_silu(x_ref[0], w_ref[...])
        sn = s * lax.rsqrt(jnp.sum(s * s, axis=-1, keepdims=True) + EPS)
        o_ref[0] = jnp.where(cb < 2 * GDN_HEADS, sn, s)

    return pl.pallas_call(
        body, name="gdn_pre", grid=(n_batch, ncb),
        in_specs=[pl.BlockSpec((1, s_len, LANES), lambda b, c: (b, 0, c)), pl.BlockSpec((8, LANES), lambda b, c: (0, c))],
        out_specs=pl.BlockSpec((1, s_len, LANES), lambda b, c: (b, 0, c)),
        out_shape=jax.ShapeDtypeStruct(pg.shape, F32),
        compiler_params=_cparams(("parallel", "parallel")),
    )(pg, conv_w)


def _gdn_pre_bwd(pg, conv_w, dout):
    n_batch, s_len, width = pg.shape
    ncb = width // LANES

    def body(x_ref, w_ref, d_ref, dx_ref, dw_ref):
        cb = pl.program_id(0)
        b = pl.program_id(1)
        x = x_ref[0]
        w = w_ref[...]
        d = d_ref[0]
        y, s = _conv_silu(x, w)
        rr = lax.rsqrt(jnp.sum(s * s, axis=-1, keepdims=True) + EPS)
        sn = s * rr
        ds_n = rr * (d - sn * jnp.sum(d * sn, axis=-1, keepdims=True))
        ds = jnp.where(cb < 2 * GDN_HEADS, ds_n, d)
        sig = _sigmoid(y)
        dy = ds * (sig * (1.0 + y * (1.0 - sig)))
        dx = w[3:4] * dy + w[2:3] * _shift_up(dy, 1) + w[1:2] * _shift_up(dy, 2) + w[0:1] * _shift_up(dy, 3)
        dx_ref[0] = dx.astype(BF16)
        dw = [jnp.sum(dy * _shift_down(x, 3 - jj), axis=0, keepdims=True) if jj < 3 else jnp.sum(dy * x, axis=0, keepdims=True)
              for jj in range(CONV_WIDTH)]
        rows = lax.broadcasted_iota(jnp.int32, (8, LANES), 0)
        dwb = jnp.zeros((8, LANES), F32)
        for jj in range(CONV_WIDTH):
            dwb = dwb + jnp.where(rows == jj, dw[jj], 0.0)

        @pl.when(b == 0)
        def _():
            dw_ref[...] = jnp.zeros_like(dw_ref)

        dw_ref[...] += dwb

    blk = lambda c, b: (b, 0, c)
    return pl.pallas_call(
        body, name="gdn_pre_bwd", grid=(ncb, n_batch),
        in_specs=[pl.BlockSpec((1, s_len, LANES), blk), pl.BlockSpec((8, LANES), lambda c, b: (0, c)), pl.BlockSpec((1, s_len, LANES), blk)],
        out_specs=[pl.BlockSpec((1, s_len, LANES), blk), pl.BlockSpec((8, LANES), lambda c, b: (0, c))],
        out_shape=[jax.ShapeDtypeStruct(pg.shape, BF16), jax.ShapeDtypeStruct((8, width), F32)],
        compiler_params=_cparams(("parallel", "arbitrary")),
    )(pg, conv_w, dout)


def _gdn_gates(smc, smr, a_c, dt_c, a_r, dt_r, h):
    lane = lax.broadcasted_iota(jnp.int32, (1, LANES), 1)
    sub = lax.broadcasted_iota(jnp.int32, (SM_ROWS, 1), 0)
    beta_c = jnp.sum(jnp.where(lane == SM_B + h, _sigmoid(smc), 0.0), axis=1, keepdims=True)
    g_all_c = -jnp.exp(a_c) * _softplus(smc + dt_c)
    g_c = jnp.sum(jnp.where(lane == SM_A + h, g_all_c, 0.0), axis=1, keepdims=True)
    g_all_r = -jnp.exp(a_r) * _softplus(smr + dt_r)
    g_r = jnp.sum(jnp.where(sub == SM_A + h, g_all_r, 0.0), axis=0, keepdims=True)
    return beta_c, g_c, g_r


def _gdn_group(qkv, z, smc, smr, a_c, dt_c, a_r, dt_r, go, states):
    n_grp = len(qkv)
    c = qkv[0].shape[0]
    hd = GDN_HEAD_DIM
    pairs = [(g, h) for g in range(n_grp) for h in range(GDN_HEADS)]
    ii = lax.broadcasted_iota(jnp.int32, (c, c), 0)
    jj = lax.broadcasted_iota(jnp.int32, (c, c), 1)
    incl = ii >= jj
    col = lambda arr, base, h: arr[:, base + h * hd:base + (h + 1) * hd]

    qs, ks, kbs, vbs, decays, gcs, g_lasts, amats = [], [], [], [], [], [], [], []
    for g, h in pairs:
        beta_c, g_c, g_r = _gdn_gates(smc[g], smr[g], a_c, dt_c, a_r, dt_r, h)
        gc_c = jnp.sum(jnp.where(incl, g_r, 0.0), axis=1, keepdims=True)
        gc_r = jnp.sum(jnp.where(ii <= jj, g_c, 0.0), axis=0, keepdims=True)
        decay = jnp.where(incl, jnp.exp(jnp.where(incl, gc_c - gc_r, 0.0)), 0.0)
        k = col(qkv[g], GDN_WIDTH, h)
        kb = k * beta_c
        qs.append(col(qkv[g], 0, h) * (hd ** -0.5))
        ks.append(k)
        kbs.append(kb)
        vbs.append(col(qkv[g], 2 * GDN_WIDTH, h) * beta_c)
        decays.append(decay)
        gcs.append(gc_c)
        g_lasts.append(jnp.sum(g_c, axis=0, keepdims=True))
        amats.append(jnp.where(ii > jj, _mm_nt(kb, k) * decay, 0.0))
    ts = _unit_lower_inverses(amats)
    egcs = [jnp.exp(gc) for gc in gcs]
    us = [_mm_nn(t, vb) for t, vb in zip(ts, vbs)]
    ws = [_mm_nn(t, kb * e) for t, kb, e in zip(ts, kbs, egcs)]
    intras = [_mm_nt(q, k) * d for q, k, d in zip(qs, ks, decays)]
    qes = [q * e for q, e in zip(qs, egcs)]
    kds = [k * jnp.exp(gl - gc) for k, gl, gc in zip(ks, g_lasts, gcs)]
    sdecs = [jnp.exp(gl) for gl in g_lasts]

    outs = []
    for g in range(n_grp):
        idx = [g * GDN_HEADS + h for h in range(GDN_HEADS)]
        v_new = [us[i] - _mm_nn(ws[i], states[h]) for h, i in enumerate(idx)]
        o_state = [_mm_nn(qes[i], states[h]) for h, i in enumerate(idx)]
        o_intra = [_mm_nn(intras[i], v_new[h]) for h, i in enumerate(idx)]
        states = [states[h] * sdecs[i] + _mm_tn(kds[i], v_new[h]) for h, i in enumerate(idx)]
        outs.append([_rms(o_state[h] + o_intra[h], go) * (col(z[g], 0, h) * _sigmoid(col(z[g], 0, h))) for h in range(GDN_HEADS)])
    return outs, states


def _gdn_group_size(n_chunks):
    return GDN_GROUP if n_chunks % GDN_GROUP == 0 else 1


def _gdn_fwd(qkvn, z, smc, smr, a_c, dt_c, a_r, dt_r, go):
    n_batch, s_len, _ = qkvn.shape
    c = GDN_CHUNK
    n = s_len // c
    grp = _gdn_group_size(n)
    ng = n // grp
    gc = grp * c
    hd = GDN_HEAD_DIM

    def body(qkv_ref, z_ref, smc_ref, smr_ref, ac_ref, dc_ref, ar_ref, dr_ref, go_ref, og_ref, st_ref, s_ref):
        @pl.when(pl.program_id(1) == 0)
        def _():
            s_ref[...] = jnp.zeros_like(s_ref)

        states = [s_ref[h] for h in range(GDN_HEADS)]
        for h in range(GDN_HEADS):
            st_ref[0, 0, h] = states[h]
        rows = lambda k: slice(k * c, (k + 1) * c)
        outs, nxt = _gdn_group([qkv_ref[0, rows(k), :] for k in range(grp)], [z_ref[0, rows(k), :] for k in range(grp)],
                               [smc_ref[0, rows(k), :] for k in range(grp)], [smr_ref[k] for k in range(grp)],
                               ac_ref[...], dc_ref[...], ar_ref[...], dr_ref[...], go_ref[...], states)
        for k in range(grp):
            for h in range(GDN_HEADS):
                og_ref[0, rows(k), h * hd:(h + 1) * hd] = outs[k][h].astype(BF16)
        for h in range(GDN_HEADS):
            s_ref[h] = nxt[h]

    tok = lambda b, i: (b, i, 0)
    fixed = lambda b, i: (0, 0)
    return pl.pallas_call(
        body, name="gdn_fwd", grid=(n_batch, ng),
        in_specs=[pl.BlockSpec((1, gc, 3 * GDN_WIDTH), tok), pl.BlockSpec((1, gc, GDN_WIDTH), tok), pl.BlockSpec((1, gc, LANES), tok),
                  pl.BlockSpec((grp, SM_ROWS, c), lambda b, i: (b * ng + i, 0, 0)),
                  pl.BlockSpec((1, LANES), fixed), pl.BlockSpec((1, LANES), fixed), pl.BlockSpec((SM_ROWS, 1), fixed),
                  pl.BlockSpec((SM_ROWS, 1), fixed), pl.BlockSpec((1, LANES), fixed)],
        out_specs=[pl.BlockSpec((1, gc, GDN_WIDTH), tok), pl.BlockSpec((1, 1, GDN_HEADS, hd, hd), lambda b, i: (b, i, 0, 0, 0))],
        out_shape=[jax.ShapeDtypeStruct((n_batch, s_len, GDN_WIDTH), BF16), jax.ShapeDtypeStruct((n_batch, ng, GDN_HEADS, hd, hd), F32)],
        scratch_shapes=[pltpu.VMEM((GDN_HEADS, hd, hd), F32)],
        compiler_params=_cparams(("parallel", "arbitrary")),
    )(qkvn, z, smc, smr, a_c, dt_c, a_r, dt_r, go)


def _gdn_bwd(qkvn, z, smc, smr, a_c, dt_c, a_r, dt_r, go, states, dog):
    n_batch, s_len, _ = qkvn.shape
    c = GDN_CHUNK
    n = s_len // c
    grp = _gdn_group_size(n)
    ng = n // grp
    gc = grp * c
    hd = GDN_HEAD_DIM

    def body(qkv_ref, z_ref, smc_ref, smr_ref, ac_ref, dc_ref, ar_ref, dr_ref, go_ref, st_ref, dog_ref,
             dqkv_ref, dz_ref, dsmc_ref, dsmr_ref, dac_ref, ddc_ref, dar_ref, ddr_ref, dgo_ref, ds_ref):
        first = (pl.program_id(0) == 0) & (pl.program_id(1) == 0)

        @pl.when(pl.program_id(1) == 0)
        def _():
            ds_ref[...] = jnp.zeros_like(ds_ref)

        @pl.when(first)
        def _():
            for r in (dac_ref, ddc_ref, dar_ref, ddr_ref, dgo_ref):
                r[...] = jnp.zeros_like(r)

        rows = lambda k: slice(k * c, (k + 1) * c)
        states = [st_ref[0, 0, h] for h in range(GDN_HEADS)]
        prim = ([qkv_ref[0, rows(k), :] for k in range(grp)], [z_ref[0, rows(k), :] for k in range(grp)],
                [smc_ref[0, rows(k), :] for k in range(grp)], [smr_ref[k] for k in range(grp)],
                ac_ref[...], dc_ref[...], ar_ref[...], dr_ref[...], go_ref[...], states)
        _, vjp = jax.vjp(_gdn_group, *prim)
        cot = ([[dog_ref[0, rows(k), h * hd:(h + 1) * hd] for h in range(GDN_HEADS)] for k in range(grp)],
               [ds_ref[h] for h in range(GDN_HEADS)])
        dqkv, dz, dsmc, dsmr, dac, ddc, dar, ddr, dgo, dstates = vjp(cot)
        for k in range(grp):
            dqkv_ref[0, rows(k), :] = dqkv[k]
            dz_ref[0, rows(k), :] = dz[k].astype(BF16)
            dsmc_ref[0, rows(k), :] = dsmc[k]
            dsmr_ref[k] = dsmr[k]
        dac_ref[...] += dac
        ddc_ref[...] += ddc
        dar_ref[...] += dar
        ddr_ref[...] += ddr
        dgo_ref[...] += dgo
        for h in range(GDN_HEADS):
            ds_ref[h] = dstates[h]

    tok = lambda b, i: (b, ng - 1 - i, 0)
    fixed = lambda b, i: (0, 0)
    lane_vec = jax.ShapeDtypeStruct((1, LANES), F32)
    row_vec = jax.ShapeDtypeStruct((SM_ROWS, 1), F32)
    return pl.pallas_call(
        body, name="gdn_bwd", grid=(n_batch, ng),
        in_specs=[pl.BlockSpec((1, gc, 3 * GDN_WIDTH), tok), pl.BlockSpec((1, gc, GDN_WIDTH), tok), pl.BlockSpec((1, gc, LANES), tok),
                  pl.BlockSpec((grp, SM_ROWS, c), lambda b, i: (b * ng + ng - 1 - i, 0, 0)),
                  pl.BlockSpec((1, LANES), fixed), pl.BlockSpec((1, LANES), fixed), pl.BlockSpec((SM_ROWS, 1), fixed),
                  pl.BlockSpec((SM_ROWS, 1), fixed), pl.BlockSpec((1, LANES), fixed),
                  pl.BlockSpec((1, 1, GDN_HEADS, hd, hd), lambda b, i: (b, ng - 1 - i, 0, 0, 0)),
                  pl.BlockSpec((1, gc, GDN_WIDTH), lambda b, i: (b, ng - 1 - i, 1))],
        out_specs=[pl.BlockSpec((1, gc, 3 * GDN_WIDTH), tok), pl.BlockSpec((1, gc, GDN_WIDTH), tok), pl.BlockSpec((1, gc, LANES), tok),
                   pl.BlockSpec((grp, SM_ROWS, c), lambda b, i: (b * ng + ng - 1 - i, 0, 0)),
                   pl.BlockSpec((1, LANES), fixed), pl.BlockSpec((1, LANES), fixed), pl.BlockSpec((SM_ROWS, 1), fixed),
                   pl.BlockSpec((SM_ROWS, 1), fixed), pl.BlockSpec((1, LANES), fixed)],
        out_shape=[jax.ShapeDtypeStruct((n_batch, s_len, 3 * GDN_WIDTH), F32), jax.ShapeDtypeStruct((n_batch, s_len, GDN_WIDTH), BF16),
                   jax.ShapeDtypeStruct((n_batch, s_len, LANES), F32), jax.ShapeDtypeStruct((n_batch * n, SM_ROWS, c), F32),
                   lane_vec, lane_vec, row_vec, row_vec, lane_vec],
        scratch_shapes=[pltpu.VMEM((GDN_HEADS, hd, hd), F32)],
        compiler_params=_cparams(("arbitrary", "arbitrary")),
    )(qkvn, z, smc, smr, a_c, dt_c, a_r, dt_r, go, states, dog)


def _out_proj(x, oa, ob, w_out, g_x, w_cq, tm=256):
    t_len, d = x.shape
    tm = min(tm, t_len)

    def body(x_ref, oa_ref, ob_ref, wo_ref, g_ref, wq_ref, x1_ref, hq_ref, cq_ref):
        x1 = x_ref[...] + _dot(oa_ref[...], wo_ref[0:FOX_WIDTH, :]) + _dot(ob_ref[...], wo_ref[FOX_WIDTH:2 * FOX_WIDTH, :])
        x1_ref[...] = x1
        hq = _rms(x1, g_ref[...]).astype(BF16)
        hq_ref[...] = hq
        cq_ref[...] = _dot(hq, wq_ref[...])

    row = lambda i: (i, 0)
    fixed = lambda i: (0, 0)
    return pl.pallas_call(
        body, name="out_proj", grid=(t_len // tm,),
        in_specs=[pl.BlockSpec((tm, d), row), pl.BlockSpec((tm, FOX_WIDTH), row), pl.BlockSpec((tm, GDN_WIDTH), row),
                  pl.BlockSpec((d, d), fixed), pl.BlockSpec((1, d), fixed), pl.BlockSpec((d, XATTN_WIDTH), fixed)],
        out_specs=[pl.BlockSpec((tm, d), row), pl.BlockSpec((tm, d), row), pl.BlockSpec((tm, XATTN_WIDTH), row)],
        out_shape=[jax.ShapeDtypeStruct((t_len, d), F32), jax.ShapeDtypeStruct((t_len, d), BF16), jax.ShapeDtypeStruct((t_len, XATTN_WIDTH), F32)],
        compiler_params=_cparams(("parallel",)),
    )(x, oa, ob, w_out, g_x, w_cq)


def _out_proj_bwd(dx1, w_out, tm=512):
    t_len, d = dx1.shape
    tm = min(tm, t_len)

    def body(dx_ref, w_ref, o_ref):
        o_ref[...] = _dot(dx_ref[...], w_ref[...], NT)

    return pl.pallas_call(
        body, name="out_proj_bwd", grid=(t_len // tm,),
        in_specs=[pl.BlockSpec((tm, d), lambda i: (i, 0)), pl.BlockSpec((d, d), lambda i: (0, 0))],
        out_specs=pl.BlockSpec((tm, d), lambda i: (i, 0)),
        out_shape=jax.ShapeDtypeStruct((t_len, d), F32),
        compiler_params=_cparams(("parallel",)),
    )(dx1, w_out)


def _mem_kv(mem, g, w_ckv, tm=256):
    t_len, d = mem.shape
    tm = min(tm, t_len)

    def body(x_ref, g_ref, w_ref, h_ref, o_ref):
        h = _rms(x_ref[...], g_ref[...]).astype(BF16)
        h_ref[...] = h
        o_ref[...] = _dot(h, w_ref[...])

    row = lambda i: (i, 0)
    fixed = lambda i: (0, 0)
    return pl.pallas_call(
        body, name="mem_kv", grid=(t_len // tm,),
        in_specs=[pl.BlockSpec((tm, d), row), pl.BlockSpec((1, d), fixed), pl.BlockSpec((d, 2 * XATTN_WIDTH), fixed)],
        out_specs=[pl.BlockSpec((tm, d), row), pl.BlockSpec((tm, 2 * XATTN_WIDTH), row)],
        out_shape=[jax.ShapeDtypeStruct((t_len, d), BF16), jax.ShapeDtypeStruct((t_len, 2 * XATTN_WIDTH), F32)],
        compiler_params=_cparams(("parallel",)),
    )(mem, g, w_ckv)


def _mem_kv_bwd(dckv, mem, g, w_ckv, tm=256):
    t_len, d = mem.shape
    tm = min(tm, t_len)

    def body(d_ref, x_ref, g_ref, w_ref, dg_ref):
        @pl.when(pl.program_id(0) == 0)
        def _():
            dg_ref[...] = jnp.zeros_like(dg_ref)

        dh = _dot(d_ref[...], w_ref[...], NT)
        _, dg = _rms_bwd(x_ref[...], g_ref[...], dh)
        dg_ref[...] += dg

    row = lambda i: (i, 0)
    fixed = lambda i: (0, 0)
    return pl.pallas_call(
        body, name="mem_kv_bwd", grid=(t_len // tm,),
        in_specs=[pl.BlockSpec((tm, 2 * XATTN_WIDTH), row), pl.BlockSpec((tm, d), row), pl.BlockSpec((1, d), fixed),
                  pl.BlockSpec((d, 2 * XATTN_WIDTH), fixed)],
        out_specs=pl.BlockSpec((1, d), fixed),
        out_shape=jax.ShapeDtypeStruct((1, d), F32),
        compiler_params=_cparams(("arbitrary",)),
    )(dckv, mem, g, w_ckv)


def _xattn_probs(qn, kn):
    s = _dot(qn, kn, NT) * (XATTN_HEAD_DIM ** -0.5)
    p = jnp.exp(s - jnp.max(s, axis=-1, keepdims=True))
    return p / jnp.sum(p, axis=-1, keepdims=True)


def _xattn_fwd(cq, ckv, x1, gq, gk, w_co, g_mlp, n_batch, s_len, m_len, tq=512):
    d = x1.shape[1]
    tq = min(tq, s_len)
    nq = s_len // tq
    hd = XATTN_HEAD_DIM

    def body(cq_ref, kv_ref, x1_ref, gq_ref, gk_ref, wo_ref, gm_ref, co_ref, x2_ref, hf_ref):
        outs = []
        for h in range(XATTN_HEADS):
            qn = _rms(cq_ref[:, h * hd:(h + 1) * hd], gq_ref[...])
            kn = _rms(kv_ref[:, h * hd:(h + 1) * hd], gk_ref[...])
            p = _xattn_probs(qn, kn)
            outs.append(_dot(p, kv_ref[:, XATTN_WIDTH + h * hd:XATTN_WIDTH + (h + 1) * hd]).astype(BF16))
        x2 = x1_ref[...]
        for h in range(XATTN_HEADS):
            co_ref[:, h * hd:(h + 1) * hd] = outs[h]
            x2 = x2 + _dot(outs[h], wo_ref[h * hd:(h + 1) * hd, :])
        x2_ref[...] = x2
        hf_ref[...] = _rms(x2, gm_ref[...]).astype(BF16)

    row = lambda b, i: (b * nq + i, 0)
    fixed = lambda b, i: (0, 0)
    t_len = n_batch * s_len
    return pl.pallas_call(
        body, name="xattn_fwd", grid=(n_batch, nq),
        in_specs=[pl.BlockSpec((tq, XATTN_WIDTH), row), pl.BlockSpec((m_len, 2 * XATTN_WIDTH), lambda b, i: (b, 0)),
                  pl.BlockSpec((tq, d), row), pl.BlockSpec((1, hd), fixed), pl.BlockSpec((1, hd), fixed),
                  pl.BlockSpec((XATTN_WIDTH, d), fixed), pl.BlockSpec((1, d), fixed)],
        out_specs=[pl.BlockSpec((tq, XATTN_WIDTH), row), pl.BlockSpec((tq, d), row), pl.BlockSpec((tq, d), row)],
        out_shape=[jax.ShapeDtypeStruct((t_len, XATTN_WIDTH), BF16), jax.ShapeDtypeStruct((t_len, d), F32),
                   jax.ShapeDtypeStruct((t_len, d), BF16)],
        compiler_params=_cparams(("parallel", "parallel")),
    )(cq, ckv, x1, gq, gk, w_co, g_mlp)


def _xattn_bwd(dx2, cq, ckv, x1, gq, gk, w_co, g_x, w_cq, n_batch, s_len, m_len, tq=512):
    d = x1.shape[1]
    tq = min(tq, s_len)
    nq = s_len // tq
    hd = XATTN_HEAD_DIM
    scale = XATTN_HEAD_DIM ** -0.5

    def body(dx2_ref, cq_ref, kv_ref, x1_ref, gq_ref, gk_ref, wo_ref, gx_ref, wq_ref,
             dx1_ref, dcq_ref, dkv_ref, dgq_ref, dgk_ref, dgx_ref, dk_acc, dv_acc):
        b = pl.program_id(0)
        i = pl.program_id(1)

        @pl.when((b == 0) & (i == 0))
        def _():
            dgq_ref[...] = jnp.zeros_like(dgq_ref)
            dgk_ref[...] = jnp.zeros_like(dgk_ref)
            dgx_ref[...] = jnp.zeros_like(dgx_ref)

        @pl.when(i == 0)
        def _():
            dk_acc[...] = jnp.zeros_like(dk_acc)
            dv_acc[...] = jnp.zeros_like(dv_acc)

        dx2 = dx2_ref[...]
        dhq = jnp.zeros((tq, d), F32)
        for h in range(XATTN_HEADS):
            sl = slice(h * hd, (h + 1) * hd)
            q = cq_ref[:, sl]
            qn = _rms(q, gq_ref[...])
            kn = _rms(kv_ref[:, sl], gk_ref[...])
            v = kv_ref[:, XATTN_WIDTH + h * hd:XATTN_WIDTH + (h + 1) * hd]
            p = _xattn_probs(qn, kn)
            dco = _dot(dx2, wo_ref[sl, :], NT)
            dv_acc[:, sl] += _dot(p, dco, TN)
            dp = _dot(dco, v, NT)
            ds = p * (dp - jnp.sum(dp * p, axis=-1, keepdims=True))
            dqn = _dot(ds, kn) * scale
            dk_acc[:, sl] += _dot(ds, qn, TN) * scale
            dq, dgq = _rms_bwd(q, gq_ref[...], dqn)
            dgq_ref[...] += dgq
            dqb = dq.astype(BF16)
            dcq_ref[:, sl] = dqb
            dhq = dhq + _dot(dqb, wq_ref[:, sl], NT)
        dxn, dgx = _rms_bwd(x1_ref[...], gx_ref[...], dhq)
        dgx_ref[...] += dgx
        dx1_ref[...] = dx2 + dxn

        @pl.when(i == nq - 1)
        def _():
            for h in range(XATTN_HEADS):
                sl = slice(h * hd, (h + 1) * hd)
                dk, dgk = _rms_bwd(kv_ref[:, sl], gk_ref[...], dk_acc[:, sl])
                dgk_ref[...] += dgk
                dkv_ref[:, sl] = dk.astype(BF16)
                dkv_ref[:, XATTN_WIDTH + h * hd:XATTN_WIDTH + (h + 1) * hd] = dv_acc[:, sl].astype(BF16)

    row = lambda b, i: (b * nq + i, 0)
    fixed = lambda b, i: (0, 0)
    t_len = n_batch * s_len
    return pl.pallas_call(
        body, name="xattn_bwd", grid=(n_batch, nq),
        in_specs=[pl.BlockSpec((tq, d), row), pl.BlockSpec((tq, XATTN_WIDTH), row), pl.BlockSpec((m_len, 2 * XATTN_WIDTH), lambda b, i: (b, 0)),
                  pl.BlockSpec((tq, d), row), pl.BlockSpec((1, hd), fixed), pl.BlockSpec((1, hd), fixed),
                  pl.BlockSpec((XATTN_WIDTH, d), fixed), pl.BlockSpec((1, d), fixed), pl.BlockSpec((d, XATTN_WIDTH), fixed)],
        out_specs=[pl.BlockSpec((tq, d), row), pl.BlockSpec((tq, XATTN_WIDTH), row), pl.BlockSpec((m_len, 2 * XATTN_WIDTH), lambda b, i: (b, 0)),
                   pl.BlockSpec((1, hd), fixed), pl.BlockSpec((1, hd), fixed), pl.BlockSpec((1, d), fixed)],
        out_shape=[jax.ShapeDtypeStruct((t_len, d), F32), jax.ShapeDtypeStruct((t_len, XATTN_WIDTH), BF16),
                   jax.ShapeDtypeStruct((n_batch * m_len, 2 * XATTN_WIDTH), BF16),
                   jax.ShapeDtypeStruct((1, hd), F32), jax.ShapeDtypeStruct((1, hd), F32), jax.ShapeDtypeStruct((1, d), F32)],
        scratch_shapes=[pltpu.VMEM((m_len, XATTN_WIDTH), F32), pltpu.VMEM((m_len, XATTN_WIDTH), F32)],
        compiler_params=_cparams(("arbitrary", "arbitrary")),
    )(dx2, cq, ckv, x1, gq, gk, w_co, g_x, w_cq)


def _resident(shape):
    return pl.BlockSpec(shape, lambda *_: (0,) * len(shape), pipeline_mode=pl.Buffered(1))


def _mlp_fwd(hf, x2, target, w1, w2, tm=256, tf=1024):
    t_len, d = x2.shape
    f = w1.shape[1]
    tm, tf = min(tm, t_len), min(tf, f)

    def body(hf_ref, x2_ref, tg_ref, w1_ref, w2_ref, u_ref, a_ref, dy_ref, ls_ref):
        hf_t = hf_ref[...]
        y = x2_ref[...]
        for k in range(f // tf):
            cols = slice(k * tf, (k + 1) * tf)
            u = _dot(hf_t, w1_ref[:, cols])
            u_ref[:, cols] = u
            r = jnp.maximum(u, 0.0)
            a = (r * r).astype(BF16)
            a_ref[:, cols] = a
            y = y + _dot(a, w2_ref[cols, :])
        err = y - tg_ref[...]
        dy_ref[...] = err * (1.0 / d)
        ls_ref[...] = jnp.broadcast_to(jnp.sum(jnp.sum(err * err, axis=-1, keepdims=True) * (1.0 / d), axis=0, keepdims=True), ls_ref.shape)

    row = lambda i: (i, 0)
    return pl.pallas_call(
        body, name="mlp_fwd", grid=(t_len // tm,),
        in_specs=[pl.BlockSpec((tm, d), row), pl.BlockSpec((tm, d), row), pl.BlockSpec((tm, d), row), _resident((d, f)), _resident((f, d))],
        out_specs=[pl.BlockSpec((tm, f), row), pl.BlockSpec((tm, f), row), pl.BlockSpec((tm, d), row),
                   pl.BlockSpec((1, 8, LANES), lambda i: (i, 0, 0))],
        out_shape=[jax.ShapeDtypeStruct((t_len, f), F32), jax.ShapeDtypeStruct((t_len, f), BF16), jax.ShapeDtypeStruct((t_len, d), F32),
                   jax.ShapeDtypeStruct((t_len // tm, 8, LANES), F32)],
        compiler_params=_cparams(("parallel",)),
    )(hf, x2, target, w1, w2)


def _mlp_bwd(dy, u, x2, g, w1, w2, tm=256, tf=1024):
    t_len, d = x2.shape
    f = w1.shape[1]
    tm, tf = min(tm, t_len), min(tf, f)

    def body(dy_ref, u_ref, x2_ref, g_ref, w1_ref, w2_ref, du_ref, dx2_ref, dg_ref):
        @pl.when(pl.program_id(0) == 0)
        def _():
            dg_ref[...] = jnp.zeros_like(dg_ref)

        dy_t = dy_ref[...]
        dyb = dy_t.astype(BF16)
        dhf = jnp.zeros((tm, d), F32)
        for k in range(f // tf):
            cols = slice(k * tf, (k + 1) * tf)
            da = _dot(dyb, w2_ref[cols, :], NT)
            du = (da * (2.0 * jnp.maximum(u_ref[:, cols], 0.0))).astype(BF16)
            du_ref[:, cols] = du
            dhf = dhf + _dot(du, w1_ref[:, cols], NT)
        dxn, dg = _rms_bwd(x2_ref[...], g_ref[...], dhf)
        dx2_ref[...] = dy_t + dxn
        dg_ref[...] += dg

    row = lambda i: (i, 0)
    fixed = lambda i: (0, 0)
    return pl.pallas_call(
        body, name="mlp_bwd", grid=(t_len // tm,),
        in_specs=[pl.BlockSpec((tm, d), row), pl.BlockSpec((tm, f), row), pl.BlockSpec((tm, d), row), pl.BlockSpec((1, d), fixed),
                  _resident((d, f)), _resident((f, d))],
        out_specs=[pl.BlockSpec((tm, f), row), pl.BlockSpec((tm, d), row), pl.BlockSpec((1, d), fixed)],
        out_shape=[jax.ShapeDtypeStruct((t_len, f), BF16), jax.ShapeDtypeStruct((t_len, d), F32), jax.ShapeDtypeStruct((1, d), F32)],
        compiler_params=_cparams(("arbitrary",)),
    )(dy, u, x2, g, w1, w2)


def _pad_lanes(v, offset=0, width=LANES):
    return jnp.zeros((1, width), F32).at[:, offset:offset + v.shape[1]].set(v)


def _col(v, offset=0, rows=SM_ROWS):
    return jnp.zeros((rows, 1), F32).at[offset:offset + v.shape[1], 0].set(v[0])


LATE_WEIGHTS = ("w_out", "w_cq", "w_ckv", "w_co", "w_mlp1", "w_mlp2")
GRAD_GROUPS = (("w_mlp2", "w_mlp1"), ("w_co", "w_cq", "w_ckv", "w_out"), ("w_in", "gdn_conv_w"))


def _local_step(x, mem, target, norm_mix_g, w_in, fox_qnorm_g, fox_knorm_g, fox_f_bias, fox_onorm_g, gdn_conv_w, gdn_A_log,
                gdn_dt_bias, gdn_onorm_g, norm_xattn_g, mem_norm_g, xattn_qnorm_g, xattn_knorm_g, norm_mlp_g,
                late_weights, grads_ready=None, first_token=0.0):
    if grads_ready is None:
        grads_ready = lambda group: 0.0
    n_batch, s_len, d = x.shape
    m_len = mem.shape[1]
    t_len = n_batch * s_len
    tq = min(FOX_BLOCK, s_len)
    nq = s_len // tq
    n_chunks = s_len // GDN_CHUNK
    x2d = x.reshape(t_len, d)

    wp = jnp.concatenate([w_in[:, 0:1536], w_in[:, 1544:3080], w_in[:, 3088:3600], w_in[:, 1536:1544], w_in[:, 3080:3088],
                          jnp.zeros((d, P_DIM - 3600), BF16)], axis=1)
    wst = jnp.concatenate([w_in[:, 1536:1544], w_in[:, 3080:3088]], axis=1).T
    conv_w = jnp.concatenate([gdn_conv_w, jnp.zeros((8 - CONV_WIDTH, gdn_conv_w.shape[1]), F32)], axis=0)
    bias_col = _col(fox_f_bias, SM_F)
    gq2, gk2, go2 = (jnp.tile(g, (1, 2)) for g in (fox_qnorm_g, fox_knorm_g, fox_onorm_g))
    a_c, dt_c = _pad_lanes(gdn_A_log, SM_A), _pad_lanes(gdn_dt_bias, SM_A)
    a_r, dt_r = _col(gdn_A_log, SM_A), _col(gdn_dt_bias, SM_A)

    h1, pfox, pgdn, pz, sm, smt = _in_proj(x2d, norm_mix_g + first_token, wp, wst)
    c_rows = _fox_cum(smt, bias_col, n_batch, s_len)
    cb = c_rows.reshape(SM_ROWS, n_batch, nq, tq).transpose(1, 2, 0, 3)
    pf3 = pfox.reshape(n_batch, s_len, 1536)
    o_fox, oa, lse = _fox_fwd(pf3, cb, gq2, gk2, go2, tq)
    pg3 = pgdn.reshape(n_batch, s_len, 1536)
    qkvn = _gdn_pre(pg3, conv_w)
    z3 = pz.reshape(n_batch, s_len, GDN_WIDTH)
    smc = sm.reshape(n_batch, s_len, LANES)
    smr = smt.reshape(SM_ROWS, n_batch * n_chunks, GDN_CHUNK).transpose(1, 0, 2)
    ob, states = _gdn_fwd(qkvn, z3, smc, smr, a_c, dt_c, a_r, dt_r, gdn_onorm_g)
    oa2, ob2 = oa.reshape(t_len, FOX_WIDTH), ob.reshape(t_len, GDN_WIDTH)
    late = late_weights(ob2)
    w_out, w_cq, w_ckv, w_co, w_mlp1, w_mlp2 = (late[k] for k in LATE_WEIGHTS)
    x1, hq, cq = _out_proj(x2d, oa2, ob2, w_out, norm_xattn_g, w_cq)
    mem2d = mem.reshape(n_batch * m_len, d)
    hm, ckv = _mem_kv(mem2d, mem_norm_g, w_ckv)
    co, x2, hf = _xattn_fwd(cq, ckv, x1, xattn_qnorm_g, xattn_knorm_g, w_co, norm_mlp_g, n_batch, s_len, m_len)
    u, a_act, dy, loss_tiles = _mlp_fwd(hf, x2, target.reshape(t_len, d), w_mlp1, w_mlp2)
    loss = 0.5 * jnp.sum(loss_tiles[:, 0, 0])

    grads = {}
    du, dx2, grads["norm_mlp_g"] = _mlp_bwd(dy, u, x2, norm_mlp_g, w_mlp1, w_mlp2)
    grads["w_mlp2"] = _wgrad(a_act, dy, "wgrad_mlp2")
    grads["w_mlp1"] = _wgrad(hf, du, "wgrad_mlp1", column_blocks=D_FF // N_DEV)
    token = grads_ready({k: grads[k] for k in GRAD_GROUPS[0]})
    grads["w_co"] = _wgrad(co, dx2, "wgrad_co", column_blocks=D_MODEL // N_DEV)
    dx1, dcq, dckv, grads["xattn_qnorm_g"], grads["xattn_knorm_g"], grads["norm_xattn_g"] = _xattn_bwd(
        dx2, cq, ckv, x1, xattn_qnorm_g + token, xattn_knorm_g, w_co, norm_xattn_g, w_cq, n_batch, s_len, m_len)
    grads["w_cq"] = _wgrad(hq, dcq, "wgrad_cq")
    grads["w_ckv"] = _wgrad(hm, dckv, "wgrad_ckv")
    grads["mem_norm_g"] = _mem_kv_bwd(dckv, mem2d, mem_norm_g, w_ckv)
    grads["w_out"] = _wgrad(jnp.concatenate([oa2, ob2], axis=1), dx1, "wgrad_out")
    token = grads_ready({k: grads[k] for k in GRAD_GROUPS[1]})
    dcat = _out_proj_bwd(dx1, w_out)
    dcat3 = dcat.reshape(n_batch, s_len, d)

    dqkvn, dz, dsmc, dsmr, dac, ddc, dar, ddr, grads["gdn_onorm_g"] = _gdn_bwd(
        qkvn, z3, smc, smr, a_c, dt_c, a_r, dt_r, gdn_onorm_g + token, states, dcat3)
    grads["gdn_A_log"] = dac[:, SM_A:SM_A + GDN_HEADS] + dar[SM_A:SM_A + GDN_HEADS, 0][None, :]
    grads["gdn_dt_bias"] = ddc[:, SM_A:SM_A + GDN_HEADS] + ddr[SM_A:SM_A + GDN_HEADS, 0][None, :]
    dpg, dconv = _gdn_pre_bwd(pg3, conv_w, dqkvn)
    grads["gdn_conv_w"] = dconv[0:CONV_WIDTH]

    dq, dk, dv, dcb, dgq, dgk, dgo = _fox_bwd(pf3, cb, gq2, gk2, go2, o_fox, lse, dcat3[:, :, 0:FOX_WIDTH], tq)
    fold = lambda g: g[:, 0:FOX_HEAD_DIM] + g[:, FOX_HEAD_DIM:LANES]
    grads["fox_qnorm_g"], grads["fox_knorm_g"], grads["fox_onorm_g"] = fold(dgq), fold(dgk), fold(dgo)
    dc8 = dcb[:, :, :, 0:2, :].transpose(1, 3, 0, 2, 4).reshape(FOX_HEADS, t_len)
    dc_rows = jnp.concatenate([dc8, jnp.zeros((SM_ROWS - FOX_HEADS, t_len), F32)], axis=0)
    dl_rows, dbias = _fox_cum_bwd(dc_rows, smt, bias_col, n_batch, s_len)
    grads["fox_f_bias"] = dbias[SM_F:SM_F + FOX_HEADS, 0][None, :]
    dsm_rows = jnp.concatenate([dl_rows[0:SM_B], dsmr.transpose(1, 0, 2).reshape(SM_ROWS, t_len)[SM_B:SM_ROWS]], axis=0)

    dproj = jnp.concatenate([dq.reshape(t_len, FOX_WIDTH), dk.reshape(t_len, FOX_WIDTH), dv.reshape(t_len, FOX_WIDTH),
                             dpg.reshape(t_len, 1536), dz.reshape(t_len, GDN_WIDTH), dsmc.reshape(t_len, LANES).astype(BF16)], axis=1)
    dwp = _wgrad(h1, dproj, "wgrad_in", bk=512, bn=P_DIM)
    dwst = _rows_matmul(dsm_rows, h1, "wgrad_in_rows")
    dw_small = dwp[:, P_SMALL:P_SMALL + SM_ROWS] + dwst.T
    grads["w_in"] = jnp.concatenate([dwp[:, 0:1536], dw_small[:, 0:8], dwp[:, 1536:3072], dw_small[:, 8:16], dwp[:, 3072:3584]], axis=1)
    token = grads_ready({k: grads[k] for k in GRAD_GROUPS[2]})
    grad_x, grads["norm_mix_g"] = _in_proj_bwd(dproj, dsm_rows, x2d, norm_mix_g + token, wp, wst, dx1)
    return loss, grad_x.reshape(n_batch, s_len, d), grads


MESH_ID = pl.DeviceIdType.MESH
ANY_SPEC = pl.BlockSpec(memory_space=pl.ANY)


def _place():
    x, y, c = lax.axis_index("x"), lax.axis_index("y"), lax.axis_index("c")
    return x, y, c, [(1 - x, y), (x, 1 - y), (1 - x, 1 - y)]


def _all_gather_body(n, ins, outs, send_sems, recv_sems, local_sems):
    x, y, c, chips = _place()
    me, sibling = (x, y, c), (x, y, 1 - c)

    def copy(a, k, block, to, src=None):
        dst = outs[a].at[4 * block[0] + 2 * block[1] + block[2]]
        return pltpu.make_async_remote_copy(src_ref=dst if src is None else src, dst_ref=dst, send_sem=send_sems.at[a, k],
                                            recv_sem=recv_sems.at[a, k], device_id=to, device_id_type=MESH_ID)

    mine = [] if local_sems is None else [pltpu.make_async_copy(ins[a], outs[a].at[4 * x + 2 * y + c], local_sems.at[a]) for a in range(n)]
    for cp in mine:
        cp.start()
    first = []
    for a in range(n):
        first.append(copy(a, 0, me, sibling, src=ins[a]))
        first += [copy(a, 1 + j, me, (*chip, c), src=ins[a]) for j, chip in enumerate(chips)]
    for cp in first:
        cp.start()
    passed = []
    for j, chip in enumerate(chips):
        for a in range(n):
            copy(a, 1 + j, (*chip, c), me).wait_recv()
            fwd = copy(a, 4 + j, (*chip, c), sibling)
            fwd.start()
            passed.append(fwd)
    for a in range(n):
        copy(a, 0, sibling, me).wait_recv()
        for j, chip in enumerate(chips):
            copy(a, 4 + j, (*chip, 1 - c), me).wait_recv()
    for cp in first + passed:
        cp.wait_send()
    for cp in mine:
        cp.wait()


def _all_gather_hbm(arrs, name):
    n = len(arrs)
    me = 4 * lax.axis_index("x") + 2 * lax.axis_index("y") + lax.axis_index("c")

    def body(*refs):
        _all_gather_body(n, refs[:n], refs[n:2 * n], refs[2 * n], refs[2 * n + 1], None)

    got = pl.pallas_call(
        body, name=name, in_specs=[ANY_SPEC] * n, out_specs=[ANY_SPEC] * n,
        out_shape=[jax.ShapeDtypeStruct((N_DEV,) + a.shape, a.dtype) for a in arrs],
        scratch_shapes=[pltpu.SemaphoreType.DMA((n, 7)), pltpu.SemaphoreType.DMA((n, 7))],
    )(*arrs)
    return [lax.dynamic_update_slice(g, a[None], (me,) + (0,) * a.ndim) for g, a in zip(got, arrs)]


def _pair_exchange(arrs, name):
    n = len(arrs)

    def body(*refs):
        ins, outs = refs[:n], refs[n:2 * n]
        send_sems, recv_sems = refs[2 * n:]
        x, y, c, _ = _place()
        copies = []
        for a in range(n):
            for chip in range(4):
                copies.append(pltpu.make_async_remote_copy(
                    src_ref=ins[a].at[2 * chip + (1 - c)], dst_ref=outs[a].at[chip], send_sem=send_sems.at[a, chip],
                    recv_sem=recv_sems.at[a, chip], device_id=(x, y, 1 - c), device_id_type=MESH_ID))
        for cp in copies:
            cp.start()
        for cp in copies:
            cp.wait()

    return pl.pallas_call(
        body, name=name, in_specs=[ANY_SPEC] * n, out_specs=[ANY_SPEC] * n,
        out_shape=[jax.ShapeDtypeStruct((4,) + a.shape[1:], a.dtype) for a in arrs],
        scratch_shapes=[pltpu.SemaphoreType.DMA((n, 4)), pltpu.SemaphoreType.DMA((n, 4))],
    )(*arrs)


HBM_SPEC = pl.BlockSpec(memory_space=pltpu.HBM)
SEM_SPEC = pl.BlockSpec(memory_space=pltpu.SEMAPHORE)
DATAFLOW = pltpu.SideEffectType.DATAFLOW_SIDE_EFFECTING


def _in_hbm(arrs):
    return [pltpu.with_memory_space_constraint(a, pltpu.HBM) for a in arrs]


def _copies_start(name, srcs, lands, make_copies):
    n = len(srcs)
    n_copies = len(make_copies(srcs, lands, None, None)[0])

    def body(*refs):
        send_sems, recv_sems = refs[2 * n], refs[2 * n + 1]
        for row in make_copies(refs[:n], refs[n:2 * n], send_sems, recv_sems):
            for cp in row:
                cp.start()
        refs[-1][...] = jnp.zeros_like(refs[-1])

    sems = pltpu.SemaphoreType.DMA((n * n_copies,))
    thru = [pltpu.HBM(a.shape, a.dtype) for a in list(srcs) + list(lands)]
    res = pl.pallas_call(
        body, name=name, in_specs=[HBM_SPEC] * (2 * n),
        out_specs=(SEM_SPEC, SEM_SPEC, *[HBM_SPEC] * (2 * n), pl.BlockSpec(memory_space=pltpu.VMEM)),
        out_shape=(sems, sems, *thru, jax.ShapeDtypeStruct((8, LANES), F32)),
        input_output_aliases={i: 2 + i for i in range(2 * n)},
        compiler_params=pltpu.CompilerParams(has_side_effects=DATAFLOW),
    )(*_in_hbm(list(srcs) + list(lands)))
    return res[0], res[1], list(res[2:2 + n]), list(res[2 + n:2 + 2 * n]), res[-1]


def _copies_wait(name, send_sems, recv_sems, srcs, lands, after, make_copies):
    n = len(srcs)

    def body(*refs):
        for row in make_copies(refs[:n], refs[n:2 * n], refs[2 * n], refs[2 * n + 1]):
            for cp in row:
                cp.wait_send()
                cp.wait_recv()

    res = pl.pallas_call(
        body, name=name, in_specs=[HBM_SPEC] * (2 * n) + [SEM_SPEC, SEM_SPEC, ANY_SPEC],
        out_specs=tuple([HBM_SPEC] * (2 * n)),
        out_shape=tuple(pltpu.HBM(a.shape, a.dtype) for a in list(srcs) + list(lands)),
        input_output_aliases={i: i for i in range(2 * n)},
        compiler_params=pltpu.CompilerParams(has_side_effects=DATAFLOW),
    )(*srcs, *lands, send_sems, recv_sems, after)
    return list(res[:n]), list(res[n:])


def _gather_copies(srcs, lands, send_sems, recv_sems):
    if send_sems is None:
        return [[None] * 7]
    x, y, c, _ = _place()
    rows = []
    for a in range(len(srcs)):
        row = []
        for k in range(7):
            r = k + 1
            to = (1 - x if r & 4 else x, 1 - y if r & 2 else y, 1 - c if r & 1 else c)
            row.append(pltpu.make_async_remote_copy(
                src_ref=srcs[a], dst_ref=lands[a].at[4 * x + 2 * y + c], send_sem=send_sems.at[7 * a + k], recv_sem=recv_sems.at[7 * a + k],
                device_id=to, device_id_type=MESH_ID))
        rows.append(row)
    return rows


def _scatter_copies(srcs, lands, send_sems, recv_sems):
    if send_sems is None:
        return [[None] * 7]
    x, y, c, _ = _place()
    rows = []
    for a in range(len(srcs)):
        row = []
        for k in range(7):
            r = k + 1
            to = (1 - x if r & 4 else x, 1 - y if r & 2 else y, 1 - c if r & 1 else c)
            row.append(pltpu.make_async_remote_copy(
                src_ref=srcs[a].at[4 * to[0] + 2 * to[1] + to[2]], dst_ref=lands[a].at[k], send_sem=send_sems.at[7 * a + k],
                recv_sem=recv_sems.at[7 * a + k], device_id=to, device_id_type=MESH_ID))
        rows.append(row)
    return rows


def _chip_copies(srcs, lands, send_sems, recv_sems):
    if send_sems is None:
        return [[None] * 3]
    x, y, c, chips = _place()
    return [[pltpu.make_async_remote_copy(
        src_ref=srcs[a].at[2 * chip[0] + chip[1]], dst_ref=lands[a].at[j], send_sem=send_sems.at[3 * a + j], recv_sem=recv_sems.at[3 * a + j],
        device_id=(*chip, c), device_id_type=MESH_ID) for j, chip in enumerate(chips)] for a in range(len(srcs))]


def _all_gather_vmem(block, name):
    def body(in_ref, out_ref, send_sems, recv_sems, local_sems):
        _all_gather_body(1, [in_ref], [out_ref], send_sems, recv_sems, local_sems)

    vmem = pl.BlockSpec(memory_space=pltpu.VMEM)
    return pl.pallas_call(
        body, name=name, in_specs=[vmem], out_specs=vmem,
        out_shape=jax.ShapeDtypeStruct((N_DEV,) + block.shape, block.dtype),
        scratch_shapes=[pltpu.SemaphoreType.DMA((1, 7)), pltpu.SemaphoreType.DMA((1, 7)), pltpu.SemaphoreType.DMA((1,))],
    )(block)


def _row_tile(rows, cols):
    if rows <= 256:
        return rows
    return 256 if cols <= 512 else 128


def _pair_sum(core, own, got, name):
    _, rows, cols = own.shape
    tr = _row_tile(rows, cols)

    def body(c_ref, own_ref, got_ref, o_ref):
        o_ref[0] = own_ref[0] + got_ref[0]

    return pl.pallas_call(
        body, name=name,
        grid_spec=pltpu.PrefetchScalarGridSpec(
            num_scalar_prefetch=1, grid=(4, rows // tr),
            in_specs=[pl.BlockSpec((1, tr, cols), lambda k, i, c: (2 * k + c[0], i, 0)),
                      pl.BlockSpec((1, tr, cols), lambda k, i, c: (k, i, 0))],
            out_specs=pl.BlockSpec((1, tr, cols), lambda k, i, c: (k, i, 0))),
        out_shape=jax.ShapeDtypeStruct((4, rows, cols), F32),
        compiler_params=_cparams(("parallel", "parallel")),
    )(core, own, got)


def _adamw(w, g, m, v):
    m_new = ADAM_B1 * m + (1.0 - ADAM_B1) * g
    v_new = ADAM_B2 * v + (1.0 - ADAM_B2) * (g * g)
    m_hat = m_new / (1.0 - ADAM_B1 ** ADAM_STEP)
    v_hat = v_new / (1.0 - ADAM_B2 ** ADAM_STEP)
    delta = -ADAM_LR * (m_hat / (jnp.sqrt(v_hat) + ADAM_EPS) + ADAM_WD * w)
    return delta, m_new, v_new


def _sum_adam(chip, sums, parts, w, m, v, name):
    n_parts, rows, cols = parts.shape
    tr = _row_tile(rows, cols)

    def body(chip_ref, *refs):
        if sums is not None:
            g = refs[0][0].astype(F32)
            refs = refs[1:]
        p_ref, w_ref, m_ref, v_ref, g_ref, d_ref, mo_ref, vo_ref = refs
        for k in range(n_parts):
            g = p_ref[k].astype(F32) if (k == 0 and sums is None) else g + p_ref[k].astype(F32)
        g_ref[...] = g
        d_ref[...], mo_ref[...], vo_ref[...] = _adamw(w_ref[...], g, m_ref[...], v_ref[...])

    tile = pl.BlockSpec((tr, cols), lambda i, ch: (i, 0))
    out = jax.ShapeDtypeStruct((rows, cols), F32)
    own = [] if sums is None else [pl.BlockSpec((1, tr, cols), lambda i, ch: (ch[0], i, 0))]
    return pl.pallas_call(
        body, name=name,
        grid_spec=pltpu.PrefetchScalarGridSpec(
            num_scalar_prefetch=1, grid=(rows // tr,),
            in_specs=own + [pl.BlockSpec((n_parts, tr, cols), lambda i, ch: (0, i, 0)), tile, tile, tile],
            out_specs=[tile, tile, tile, tile]),
        out_shape=[out, out, out, out],
        compiler_params=_cparams(("parallel",)),
    )(chip, *([] if sums is None else [sums]), parts, w, m, v)


SHARDED = ("w_in", "gdn_conv_w", "w_out", "w_cq", "w_ckv", "w_co", "w_mlp1", "w_mlp2")
COLUMN_SHARDED = ("w_in", "gdn_conv_w", "w_co", "w_mlp1")
REPLICATED = ("norm_mix_g", "fox_qnorm_g", "fox_knorm_g", "fox_f_bias", "fox_onorm_g", "gdn_A_log", "gdn_dt_bias", "gdn_onorm_g",
              "norm_xattn_g", "mem_norm_g", "xattn_qnorm_g", "xattn_knorm_g", "norm_mlp_g")
WEIGHTS = ("norm_mix_g", "w_in", "fox_qnorm_g", "fox_knorm_g", "fox_f_bias", "fox_onorm_g", "gdn_conv_w", "gdn_A_log", "gdn_dt_bias",
           "gdn_onorm_g", "w_out", "norm_xattn_g", "mem_norm_g", "w_cq", "w_ckv", "xattn_qnorm_g", "xattn_knorm_g", "w_co",
           "norm_mlp_g", "w_mlp1", "w_mlp2")
PACK_ROWS = 16
LOSS_ROW = len(REPLICATED)


def _whole(name, gathered):
    if name in COLUMN_SHARDED:
        return gathered.transpose(1, 0, 2).reshape(gathered.shape[1], N_DEV * gathered.shape[2])
    return gathered.reshape(N_DEV * gathered.shape[1], gathered.shape[2])


def _blocks(name, whole):
    if whole.ndim == 3:
        return whole
    if name in COLUMN_SHARDED:
        rows, cols = whole.shape
        return whole.reshape(rows, N_DEV, cols // N_DEV).transpose(1, 0, 2)
    return whole.reshape(N_DEV, whole.shape[0] // N_DEV, whole.shape[1])


def _pack(vals, fill=0.0):
    rows = [jnp.pad(vals[k], ((0, 0), (0, D_MODEL - vals[k].shape[1])), constant_values=fill) for k in REPLICATED]
    rows.append(jnp.full((PACK_ROWS - len(rows), D_MODEL), fill, F32))
    return jnp.concatenate(rows, axis=0)


def kernel(x, mem, norm_mix_g, w_in, fox_qnorm_g, fox_knorm_g, fox_f_bias, fox_onorm_g, gdn_conv_w, gdn_A_log, gdn_dt_bias, gdn_onorm_g, w_out, norm_xattn_g, mem_norm_g, w_cq, w_ckv, xattn_qnorm_g, xattn_knorm_g, w_co, norm_mlp_g, w_mlp1, w_mlp2, loss_target, m_norm_mix_g, m_w_in, m_fox_qnorm_g, m_fox_knorm_g, m_fox_f_bias, m_fox_onorm_g, m_gdn_conv_w, m_gdn_A_log, m_gdn_dt_bias, m_gdn_onorm_g, m_w_out, m_norm_xattn_g, m_mem_norm_g, m_w_cq, m_w_ckv, m_xattn_qnorm_g, m_xattn_knorm_g, m_w_co, m_norm_mlp_g, m_w_mlp1, m_w_mlp2, v_norm_mix_g, v_w_in, v_fox_qnorm_g, v_fox_knorm_g, v_fox_f_bias, v_fox_onorm_g, v_gdn_conv_w, v_gdn_A_log, v_gdn_dt_bias, v_gdn_onorm_g, v_w_out, v_norm_xattn_g, v_mem_norm_g, v_w_cq, v_w_ckv, v_xattn_qnorm_g, v_xattn_knorm_g, v_w_co, v_norm_mlp_g, v_w_mlp1, v_w_mlp2):
    given = dict(locals())
    w = {k: given[k] for k in WEIGHTS}
    m = {k: given["m_" + k] for k in WEIGHTS}
    v = {k: given["v_" + k] for k in WEIGHTS}

    core = lax.axis_index("c").astype(jnp.int32).reshape(1)
    chip = (2 * lax.axis_index("x") + lax.axis_index("y")).astype(jnp.int32).reshape(1)
    me = 4 * lax.axis_index("x") + 2 * lax.axis_index("y") + lax.axis_index("c")

    shards = {k: w[k][0] if k == "gdn_conv_w" else w[k][0].astype(BF16) for k in SHARDED}
    early = [k for k in SHARDED if k not in LATE_WEIGHTS]
    whole = {k: _whole(k, g) for k, g in zip(early, _all_gather_hbm([shards[k] for k in early], "gather_early"))}
    late_shards = [shards[k] for k in LATE_WEIGHTS]
    late_lands = [lax.empty((N_DEV,) + s.shape, s.dtype) for s in late_shards]
    gather = _copies_start("gather_late_start", late_shards, late_lands, _gather_copies)

    def late_weights(after):
        srcs, lands = _copies_wait("gather_late_wait", gather[0], gather[1], gather[2], gather[3], after, _gather_copies)
        return {k: _whole(k, lax.dynamic_update_slice(land, src[None], (me, 0, 0))) for k, src, land in zip(LATE_WEIGHTS, srcs, lands)}

    pending = []

    def grads_ready(group):
        names = list(group)
        tag = str(len(pending))
        own = [_blocks(k, group[k]) for k in names]
        if "w_in" in names:
            got = _pair_exchange(own, "grad_pair_exchange_" + tag)
            srcs = [_pair_sum(core, o, g, "grad_pair_sum_" + k) for k, o, g in zip(names, own, got)]
            copies, index, n_parts = _chip_copies, chip, 3
        else:
            srcs, copies, index, n_parts = own, _scatter_copies, me.astype(jnp.int32).reshape(1), 7
        lands = [lax.empty((n_parts,) + s.shape[1:], s.dtype) for s in srcs]
        started = _copies_start("grad_exchange_start_" + tag, srcs, lands, copies)
        pending.append((names, started, copies, index))
        return started[4][0, 0]

    small = {k: w[k] for k in REPLICATED}
    loss_local, grad_x, grads = _local_step(x, mem, loss_target, **small, **whole, late_weights=late_weights,
                                            grads_ready=grads_ready, first_token=gather[4][0, 0])

    out_g, out_d, out_m, out_v = {}, {}, {}, {}
    after = grad_x
    for tag, (names, started, copies, index) in enumerate(pending):
        srcs, parts = _copies_wait("grad_exchange_wait_" + str(tag), started[0], started[1], started[2], started[3], after, copies)
        for k, s, p in zip(names, srcs, parts):
            res = _sum_adam(index, s, p, w[k][0], m[k][0], v[k][0], "adam_" + k)
            out_g[k], out_d[k], out_m[k], out_v[k] = (r[None] for r in res)
            after = res[0]

    packed = _pack({k: grads[k] for k in REPLICATED}).at[LOSS_ROW, 0].set(loss_local)
    everyone = _all_gather_vmem(packed, "gather_small")
    res = _sum_adam(chip, None, everyone, _pack(small), _pack({k: m[k] for k in REPLICATED}),
                    _pack({k: v[k] for k in REPLICATED}, fill=1.0), "adam_small")
    for i, k in enumerate(REPLICATED):
        n = w[k].shape[1]
        out_g[k], out_d[k], out_m[k], out_v[k] = (r[i:i + 1, 0:n] for r in res)
    loss = res[0][LOSS_ROW, 0]

    return (loss, grad_x, *[out_g[k] for k in WEIGHTS], *[out_d[k] for k in WEIGHTS], *[out_m[k] for k in WEIGHTS],
            *[out_v[k] for k in WEIGHTS])
```

```python
import functools

import jax
import jax.numpy as jnp
import numpy as np
from jax import lax
from jax.experimental import pallas as pl
from jax.experimental.pallas import tpu as pltpu

F32 = jnp.float32
BF16 = jnp.bfloat16

D_MODEL = 1024
FOX_HEADS = 8
FOX_HEAD_DIM = 64
FOX_WIDTH = 512
GDN_HEADS = 4
GDN_HEAD_DIM = 128
GDN_WIDTH = 512
CONV_WIDTH = 4
GDN_CHUNK = 64
GDN_GROUP = 4
FOX_BLOCK = 512
XATTN_HEADS = 4
XATTN_HEAD_DIM = 128
XATTN_WIDTH = 512
D_FF = 4096
EPS = 1e-6
NEG_INF = -1e30
N_DEV = 8

ADAM_LR = 0.001
ADAM_B1 = 0.9
ADAM_B2 = 0.999
ADAM_EPS = 1e-08
ADAM_WD = 0.01
ADAM_STEP = 10

P_FOX = 0
P_GDN = 1536
P_Z = 3072
P_SMALL = 3584
P_DIM = 3712
SM_F = 0
SM_B = 8
SM_A = 12
SM_ROWS = 16

LANES = 128
VMEM_LIMIT = 56 * 1024 * 1024

NN = (((1,), (0,)), ((), ()))
NT = (((1,), (1,)), ((), ()))
TN = (((0,), (0,)), ((), ()))


def _dot(a, b, dims=NN):
    return lax.dot_general(a.astype(BF16), b.astype(BF16), dims, preferred_element_type=F32)


def _cparams(sem=None):
    kw = dict(vmem_limit_bytes=VMEM_LIMIT)
    if sem is not None:
        kw["dimension_semantics"] = sem
    return pltpu.CompilerParams(**kw)


def _sigmoid(x):
    return 0.5 * (jnp.tanh(0.5 * x) + 1.0)


def _softplus(x):
    return jnp.maximum(x, 0.0) + jnp.log1p(jnp.exp(-jnp.abs(x)))


def _log_sigmoid(x):
    return -_softplus(-x)


def _rms(x, g):
    r = lax.rsqrt(jnp.mean(x * x, axis=-1, keepdims=True) + EPS)
    return x * r * g


def _rms_bwd(x, g, dy):
    r = lax.rsqrt(jnp.mean(x * x, axis=-1, keepdims=True) + EPS)
    xh = x * r
    dg = jnp.sum(dy * xh, axis=0, keepdims=True)
    dyg = dy * g
    dx = r * (dyg - xh * jnp.mean(dyg * xh, axis=-1, keepdims=True))
    return dx, dg


def _pair_stat(t, m0):
    s0 = jnp.sum(jnp.where(m0, t, 0.0), axis=-1, keepdims=True)
    s1 = jnp.sum(jnp.where(m0, 0.0, t), axis=-1, keepdims=True)
    return jnp.where(m0, s0, s1)


def _rms_pair(x, g, m0):
    r = lax.rsqrt(_pair_stat(x * x, m0) * (1.0 / FOX_HEAD_DIM) + EPS)
    return x * r * g


def _rms_pair_bwd(x, g, dy, m0):
    r = lax.rsqrt(_pair_stat(x * x, m0) * (1.0 / FOX_HEAD_DIM) + EPS)
    xh = x * r
    dg = jnp.sum(dy * xh, axis=0, keepdims=True)
    dyg = dy * g
    dx = r * (dyg - xh * (_pair_stat(dyg * xh, m0) * (1.0 / FOX_HEAD_DIM)))
    return dx, dg


@jax.custom_vjp
def _mm_nn(a, b):
    return _dot(a, b, NN)


_mm_nn.defvjp(lambda a, b: (_dot(a, b, NN), (a, b)),
              lambda r, g: (_dot(g, r[1], NT), _dot(r[0], g, TN)))


@jax.custom_vjp
def _mm_nt(a, b):
    return _dot(a, b, NT)


_mm_nt.defvjp(lambda a, b: (_dot(a, b, NT), (a, b)),
              lambda r, g: (_dot(g, r[1], NN), _dot(g, r[0], TN)))


@jax.custom_vjp
def _mm_tn(a, b):
    return _dot(a, b, TN)


_mm_tn.defvjp(lambda a, b: (_dot(a, b, TN), (a, b)),
              lambda r, g: (_dot(r[1], g, NT), _dot(r[0], g, NN)))


def _dot3(a, b, dims):
    ah = a.astype(BF16)
    al = (a - ah.astype(F32)).astype(BF16)
    bh = b.astype(BF16)
    bl = (b - bh.astype(F32)).astype(BF16)
    d = functools.partial(lax.dot_general, dimension_numbers=dims, preferred_element_type=F32)
    return d(ah, bh) + d(ah, bl) + d(al, bh)


def _neumann_inverses(mats):
    c = mats[0].shape[0]
    eye = (lax.broadcasted_iota(jnp.int32, (c, c), 0) == lax.broadcasted_iota(jnp.int32, (c, c), 1)).astype(F32)
    xs = [eye - a for a in mats]
    ps = list(mats)
    k = 2
    while k < c + 1:
        ps = [_dot3(p, p, NN) for p in ps]
        xs = [x + _dot3(x, p, NN) for x, p in zip(xs, ps)]
        k *= 2
    return xs


@jax.custom_vjp
def _unit_lower_inverses(mats):
    return _neumann_inverses(mats)


def _unit_lower_inverses_fwd(mats):
    ts = _neumann_inverses(mats)
    return ts, ts


def _unit_lower_inverses_bwd(ts, gs):
    left = [_dot3(t, g, TN) for t, g in zip(ts, gs)]
    return ([-_dot3(m, t, NT) for m, t in zip(left, ts)],)


_unit_lower_inverses.defvjp(_unit_lower_inverses_fwd, _unit_lower_inverses_bwd)


def _wgrad(a, b, name, bk=1024, bn=1024, bt=512, column_blocks=None):
    t_len, k_len = a.shape
    n_len = b.shape[1]
    bk, bn, bt = min(bk, k_len), min(bn, n_len), min(bt, t_len)
    nt = t_len // bt

    def body(a_ref, b_ref, o_ref, acc_ref):
        t = pl.program_id(2)

        @pl.when(t == 0)
        def _():
            acc_ref[...] = jnp.zeros_like(acc_ref)

        acc_ref[...] += _dot(a_ref[...], b_ref[...], TN)

        @pl.when(t == nt - 1)
        def _():
            if column_blocks:
                for jj in range(bn // column_blocks):
                    o_ref[jj] = acc_ref[:, jj * column_blocks:(jj + 1) * column_blocks]
            else:
                o_ref[...] = acc_ref[...]

    if column_blocks:
        out_spec = pl.BlockSpec((bn // column_blocks, bk, column_blocks), lambda i, j, t: (j, i, 0))
        out_shape = jax.ShapeDtypeStruct((n_len // column_blocks, k_len, column_blocks), F32)
    else:
        out_spec = pl.BlockSpec((bk, bn), lambda i, j, t: (i, j))
        out_shape = jax.ShapeDtypeStruct((k_len, n_len), F32)
    return pl.pallas_call(
        body, name=name, grid=(k_len // bk, n_len // bn, nt),
        in_specs=[pl.BlockSpec((bt, bk), lambda i, j, t: (t, i)), pl.BlockSpec((bt, bn), lambda i, j, t: (t, j))],
        out_specs=out_spec, out_shape=out_shape,
        scratch_shapes=[pltpu.VMEM((bk, bn), F32)],
        compiler_params=_cparams(("parallel", "parallel", "arbitrary")),
    )(a, b)


def _rows_matmul(a, b, name, bt=512):
    r_len, t_len = a.shape
    n_len = b.shape[1]
    bt = min(bt, t_len)
    nt = t_len // bt

    def body(a_ref, b_ref, o_ref):
        t = pl.program_id(0)

        @pl.when(t == 0)
        def _():
            o_ref[...] = jnp.zeros_like(o_ref)

        o_ref[...] += _dot(a_ref[...], b_ref[...], NN)

    return pl.pallas_call(
        body, name=name, grid=(nt,),
        in_specs=[pl.BlockSpec((r_len, bt), lambda t: (0, t)), pl.BlockSpec((bt, n_len), lambda t: (t, 0))],
        out_specs=pl.BlockSpec((r_len, n_len), lambda t: (0, 0)),
        out_shape=jax.ShapeDtypeStruct((r_len, n_len), F32),
        compiler_params=_cparams(("arbitrary",)),
    )(a, b)


def _in_proj(x, g, wp, wst, tm=256):
    t_len, d = x.shape
    tm = min(tm, t_len)

    def body(x_ref, g_ref, wp_ref, wst_ref, h_ref, fox_ref, gdn_ref, z_ref, sm_ref, smt_ref):
        h = _rms(x_ref[...], g_ref[...]).astype(BF16)
        h_ref[...] = h
        p = _dot(h, wp_ref[...], NN)
        fox_ref[...] = p[:, P_FOX:P_GDN]
        gdn_ref[...] = p[:, P_GDN:P_Z]
        z_ref[...] = p[:, P_Z:P_SMALL]
        sm_ref[...] = p[:, P_SMALL:P_DIM]
        smt_ref[...] = _dot(wst_ref[...], h, NT)

    row = lambda i: (i, 0)
    fixed = lambda i: (0, 0)
    return pl.pallas_call(
        body, name="in_proj", grid=(t_len // tm,),
        in_specs=[pl.BlockSpec((tm, d), row), pl.BlockSpec((1, d), fixed), pl.BlockSpec((d, P_DIM), fixed),
                  pl.BlockSpec((SM_ROWS, d), fixed)],
        out_specs=[pl.BlockSpec((tm, d), row), pl.BlockSpec((tm, 1536), row), pl.BlockSpec((tm, 1536), row),
                   pl.BlockSpec((tm, 512), row), pl.BlockSpec((tm, LANES), row), pl.BlockSpec((SM_ROWS, tm), lambda i: (0, i))],
        out_shape=[jax.ShapeDtypeStruct((t_len, d), BF16), jax.ShapeDtypeStruct((t_len, 1536), F32),
                   jax.ShapeDtypeStruct((t_len, 1536), F32), jax.ShapeDtypeStruct((t_len, 512), F32),
                   jax.ShapeDtypeStruct((t_len, LANES), F32), jax.ShapeDtypeStruct((SM_ROWS, t_len), F32)],
        compiler_params=_cparams(("parallel",)),
    )(x, g, wp, wst)


def _in_proj_bwd(dproj, dsmt, x, g, wp, wst, dx1, tm=256):
    t_len, d = x.shape
    tm = min(tm, t_len)

    def body(dp_ref, dst_ref, x_ref, g_ref, wp_ref, wst_ref, dx1_ref, dx_ref, dg_ref):
        i = pl.program_id(0)
        dh = _dot(dp_ref[...], wp_ref[...], NT) + _dot(dst_ref[...], wst_ref[...], TN)
        dxn, dg = _rms_bwd(x_ref[...], g_ref[...], dh)
        dx_ref[...] = dx1_ref[...] + dxn

        @pl.when(i == 0)
        def _():
            dg_ref[...] = jnp.zeros_like(dg_ref)

        dg_ref[...] += dg

    row = lambda i: (i, 0)
    fixed = lambda i: (0, 0)
    return pl.pallas_call(
        body, name="in_proj_bwd", grid=(t_len // tm,),
        in_specs=[pl.BlockSpec((tm, P_DIM), row), pl.BlockSpec((SM_ROWS, tm), lambda i: (0, i)), pl.BlockSpec((tm, d), row),
                  pl.BlockSpec((1, d), fixed), pl.BlockSpec((d, P_DIM), fixed), pl.BlockSpec((SM_ROWS, d), fixed),
                  pl.BlockSpec((tm, d), row)],
        out_specs=[pl.BlockSpec((tm, d), row), pl.BlockSpec((1, d), fixed)],
        out_shape=[jax.ShapeDtypeStruct((t_len, d), F32), jax.ShapeDtypeStruct((1, d), F32)],
        compiler_params=_cparams(("arbitrary",)),
    )(dproj, dsmt, x, g, wp, wst, dx1)


def _fox_cum(smt, bias_col, n_batch, s_len, ck=256):
    ck = min(ck, s_len)

    def body(s_ref, b_ref, c_ref):
        tri = (lax.broadcasted_iota(jnp.int32, (ck, ck), 0) <= lax.broadcasted_iota(jnp.int32, (ck, ck), 1)).astype(F32)
        carry = jnp.zeros((SM_ROWS, 1), F32)
        for r in range(s_len // ck):
            ls = _log_sigmoid(s_ref[:, r * ck:(r + 1) * ck] + b_ref[...])
            c = jnp.dot(ls, tri, precision=lax.Precision.HIGHEST, preferred_element_type=F32) + carry
            c_ref[:, r * ck:(r + 1) * ck] = c
            carry = c[:, ck - 1:ck]

    return pl.pallas_call(
        body, name="fox_cum", grid=(n_batch,),
        in_specs=[pl.BlockSpec((SM_ROWS, s_len), lambda b: (0, b)), pl.BlockSpec((SM_ROWS, 1), lambda b: (0, 0))],
        out_specs=pl.BlockSpec((SM_ROWS, s_len), lambda b: (0, b)),
        out_shape=jax.ShapeDtypeStruct(smt.shape, F32),
        compiler_params=_cparams(("parallel",)),
    )(smt, bias_col)


def _fox_cum_bwd(dc, smt, bias_col, n_batch, s_len, ck=256):
    ck = min(ck, s_len)
    nr = s_len // ck

    def body(dc_ref, s_ref, b_ref, dl_ref, db_ref):
        b = pl.program_id(0)
        tri = (lax.broadcasted_iota(jnp.int32, (ck, ck), 0) >= lax.broadcasted_iota(jnp.int32, (ck, ck), 1)).astype(F32)
        carry = jnp.zeros((SM_ROWS, 1), F32)
        tot = jnp.zeros((SM_ROWS, 1), F32)
        for r in reversed(range(nr)):
            sl = slice(r * ck, (r + 1) * ck)
            dls = jnp.dot(dc_ref[:, sl], tri, precision=lax.Precision.HIGHEST, preferred_element_type=F32) + carry
            carry = dls[:, 0:1]
            dl = dls * (1.0 - _sigmoid(s_ref[:, sl] + b_ref[...]))
            dl_ref[:, sl] = dl
            tot = tot + jnp.sum(dl, axis=1, keepdims=True)

        @pl.when(b == 0)
        def _():
            db_ref[...] = jnp.zeros_like(db_ref)

        db_ref[...] += jnp.broadcast_to(tot, db_ref.shape)

    return pl.pallas_call(
        body, name="fox_cum_bwd", grid=(n_batch,),
        in_specs=[pl.BlockSpec((SM_ROWS, s_len), lambda b: (0, b)), pl.BlockSpec((SM_ROWS, s_len), lambda b: (0, b)),
                  pl.BlockSpec((SM_ROWS, 1), lambda b: (0, 0))],
        out_specs=[pl.BlockSpec((SM_ROWS, s_len), lambda b: (0, b)), pl.BlockSpec((SM_ROWS, LANES), lambda b: (0, 0))],
        out_shape=[jax.ShapeDtypeStruct(smt.shape, F32), jax.ShapeDtypeStruct((SM_ROWS, LANES), F32)],
        compiler_params=_cparams(("arbitrary",)),
    )(dc, smt, bias_col)


def _fox_diagonal_mask(tq):
    return lax.broadcasted_iota(jnp.int32, (tq, tq), 1) <= lax.broadcasted_iota(jnp.int32, (tq, tq), 0)


def _fox_fwd(pf, cb, gq2, gk2, go2, tq=256):
    n_batch, s_len, _ = pf.shape
    tq = min(tq, s_len)
    nq = s_len // tq
    scale = FOX_HEAD_DIM ** -0.5

    def body(q_ref, k_ref, v_ref, c_ref, gq_ref, gk_ref, go_ref, o_ref, on_ref, lse_ref, kh_ref, vh_ref):
        j = pl.program_id(1)
        i = pl.program_id(2)
        m0 = lax.broadcasted_iota(jnp.int32, (1, LANES), 1) < FOX_HEAD_DIM

        @pl.when(i == 0)
        def _():
            kn = _rms_pair(k_ref[0], gk_ref[...], m0)
            kh_ref[0] = jnp.where(m0, kn, 0.0).astype(BF16)
            kh_ref[1] = jnp.where(m0, 0.0, kn).astype(BF16)
            v = v_ref[0]
            vh_ref[0] = jnp.where(m0, v, 0.0).astype(BF16)
            vh_ref[1] = jnp.where(m0, 0.0, v).astype(BF16)

        qb = (_rms_pair(q_ref[0], gq_ref[...], m0) * scale).astype(BF16)

        def step(kb, carry, diagonal=False):
            ms, ls, acc = carry
            off = pl.multiple_of(kb * tq, tq)
            new_m, new_l, alphas, pv = [], [], [], []
            for hh in range(2):
                s = _dot(qb, kh_ref[hh, pl.ds(off, tq), :], NT)
                s = s - c_ref[0, kb, pl.ds(2 * j + hh, 1), :]
                if diagonal:
                    s = jnp.where(_fox_diagonal_mask(tq), s, NEG_INF)
                m_new = jnp.maximum(ms[hh], jnp.max(s, axis=-1, keepdims=True))
                alpha = jnp.exp(ms[hh] - m_new)
                p = jnp.exp(s - m_new)
                new_l.append(alpha * ls[hh] + jnp.sum(p, axis=-1, keepdims=True))
                new_m.append(m_new)
                alphas.append(alpha)
                pv.append(_dot(p, vh_ref[hh, pl.ds(off, tq), :], NN))
            acc = jnp.where(m0, alphas[0], alphas[1]) * acc + pv[0] + pv[1]
            return tuple(new_m), tuple(new_l), acc

        init_m = (jnp.full((tq, 1), NEG_INF, F32),) * 2
        init_l = (jnp.zeros((tq, 1), F32),) * 2
        carry = lax.fori_loop(0, i, step, (init_m, init_l, jnp.zeros((tq, LANES), F32)))
        ms, ls, acc = step(i, carry, diagonal=True)
        o = acc / jnp.where(m0, ls[0], ls[1])
        o_ref[0] = o
        on_ref[0] = _rms_pair(o, go_ref[...], m0).astype(BF16)
        lse_ref[0] = jnp.where(m0, ms[0] + jnp.log(ls[0]), ms[1] + jnp.log(ls[1]))

    fixed = lambda b, j, i: (0, 0)
    tile = lambda b, j, i: (b, i, j)
    return pl.pallas_call(
        body, name="fox_fwd", grid=(n_batch, 4, nq),
        in_specs=[pl.BlockSpec((1, tq, LANES), tile), pl.BlockSpec((1, s_len, LANES), lambda b, j, i: (b, 0, 4 + j)),
                  pl.BlockSpec((1, s_len, LANES), lambda b, j, i: (b, 0, 8 + j)),
                  pl.BlockSpec((1, nq, SM_ROWS, tq), lambda b, j, i: (b, 0, 0, 0)),
                  pl.BlockSpec((1, LANES), fixed), pl.BlockSpec((1, LANES), fixed), pl.BlockSpec((1, LANES), fixed)],
        out_specs=[pl.BlockSpec((1, tq, LANES), tile), pl.BlockSpec((1, tq, LANES), tile), pl.BlockSpec((1, tq, LANES), tile)],
        out_shape=[jax.ShapeDtypeStruct((n_batch, s_len, FOX_WIDTH), F32), jax.ShapeDtypeStruct((n_batch, s_len, FOX_WIDTH), BF16),
                   jax.ShapeDtypeStruct((n_batch, s_len, FOX_WIDTH), F32)],
        scratch_shapes=[pltpu.VMEM((2, s_len, LANES), BF16), pltpu.VMEM((2, s_len, LANES), BF16)],
        compiler_params=_cparams(("parallel", "parallel", "arbitrary")),
    )(pf, pf, pf, cb, gq2, gk2, go2)


def _fox_bwd(pf, cb, gq2, gk2, go2, o, lse, don, tq=256):
    n_batch, s_len, _ = pf.shape
    tq = min(tq, s_len)
    nq = s_len // tq
    scale = FOX_HEAD_DIM ** -0.5

    def body(q_ref, k_ref, v_ref, c_ref, gq_ref, gk_ref, go_ref, o_ref, lse_ref, don_ref,
             dq_ref, dk_ref, dv_ref, dc_ref, dgq_ref, dgk_ref, dgo_ref, kh_ref, vh_ref, dka_ref, dva_ref, dca_ref):
        b = pl.program_id(0)
        j = pl.program_id(1)
        i = pl.program_id(2)
        m0 = lax.broadcasted_iota(jnp.int32, (1, LANES), 1) < FOX_HEAD_DIM

        @pl.when((b == 0) & (j == 0) & (i == 0))
        def _():
            dgq_ref[...] = jnp.zeros_like(dgq_ref)
            dgk_ref[...] = jnp.zeros_like(dgk_ref)
            dgo_ref[...] = jnp.zeros_like(dgo_ref)

        @pl.when(i == 0)
        def _():
            kn = _rms_pair(k_ref[0], gk_ref[...], m0)
            kh_ref[0] = jnp.where(m0, kn, 0.0).astype(BF16)
            kh_ref[1] = jnp.where(m0, 0.0, kn).astype(BF16)
            v = v_ref[0]
            vh_ref[0] = jnp.where(m0, v, 0.0).astype(BF16)
            vh_ref[1] = jnp.where(m0, 0.0, v).astype(BF16)
            dka_ref[...] = jnp.zeros_like(dka_ref)
            dva_ref[...] = jnp.zeros_like(dva_ref)
            dca_ref[...] = jnp.zeros_like(dca_ref)

        q = q_ref[0]
        qn = _rms_pair(q, gq_ref[...], m0)
        qs = qn * scale
        qb = qs.astype(BF16)
        qh = (jnp.where(m0, qs, 0.0).astype(BF16), jnp.where(m0, 0.0, qs).astype(BF16))
        ot = o_ref[0]
        do, dgo = _rms_pair_bwd(ot, go_ref[...], don_ref[0], m0)
        dgo_ref[...] += dgo
        dd = do * ot
        delta = (jnp.sum(jnp.where(m0, dd, 0.0), axis=-1, keepdims=True), jnp.sum(jnp.where(m0, 0.0, dd), axis=-1, keepdims=True))
        doh = (jnp.where(m0, do, 0.0).astype(BF16), jnp.where(m0, 0.0, do).astype(BF16))
        lse_t = lse_ref[0]
        lse_h = (lse_t[:, 0:1], lse_t[:, FOX_HEAD_DIM:FOX_HEAD_DIM + 1])

        def step(kb, carry, diagonal=False):
            dqn, rs = carry
            rs = list(rs)
            off = pl.multiple_of(kb * tq, tq)
            for hh in range(2):
                kblk = kh_ref[hh, pl.ds(off, tq), :]
                vblk = vh_ref[hh, pl.ds(off, tq), :]
                s = _dot(qb, kblk, NT)
                s = s - c_ref[0, kb, pl.ds(2 * j + hh, 1), :]
                if diagonal:
                    s = jnp.where(_fox_diagonal_mask(tq), s, NEG_INF)
                p = jnp.exp(s - lse_h[hh])
                dp = _dot(doh[hh], vblk, NT)
                ds = p * (dp - delta[hh])
                dva_ref[pl.ds(off, tq), :] += _dot(p, doh[hh], TN)
                dka_ref[pl.ds(off, tq), :] += _dot(ds, qh[hh], TN)
                dca_ref[kb, hh:hh + 1, :] += -jnp.sum(ds, axis=0, keepdims=True)
                rs[hh] = rs[hh] + jnp.sum(ds, axis=-1, keepdims=True)
                dqn = dqn + _dot(ds, kblk, NN)
            return dqn, tuple(rs)

        carry = lax.fori_loop(0, i, step, (jnp.zeros((tq, LANES), F32), (jnp.zeros((tq, 1), F32),) * 2))
        dqn, rs = step(i, carry, diagonal=True)
        dqn = dqn * scale
        rs_rows = jnp.where(m0, rs[0], rs[1]).T
        dca_ref[i, 0:1, :] += rs_rows[0:1, :]
        dca_ref[i, 1:2, :] += rs_rows[FOX_HEAD_DIM:FOX_HEAD_DIM + 1, :]
        dq, dgq = _rms_pair_bwd(q, gq_ref[...], dqn, m0)
        dq_ref[0] = dq.astype(BF16)
        dgq_ref[...] += dgq

        @pl.when(i == nq - 1)
        def _():
            dk, dgk = _rms_pair_bwd(k_ref[0], gk_ref[...], dka_ref[...], m0)
            dk_ref[0] = dk.astype(BF16)
            dgk_ref[...] += dgk
            dv_ref[0] = dva_ref[...].astype(BF16)
            dc_ref[0, 0] = dca_ref[...]

    fixed = lambda b, j, i: (0, 0)
    tile = lambda b, j, i: (b, i, j)
    full = lambda b, j, i: (b, 0, j)
    wide = jax.ShapeDtypeStruct((n_batch, s_len, FOX_WIDTH), BF16)
    gain = jax.ShapeDtypeStruct((1, LANES), F32)
    return pl.pallas_call(
        body, name="fox_bwd", grid=(n_batch, 4, nq),
        in_specs=[pl.BlockSpec((1, tq, LANES), tile), pl.BlockSpec((1, s_len, LANES), lambda b, j, i: (b, 0, 4 + j)),
                  pl.BlockSpec((1, s_len, LANES), lambda b, j, i: (b, 0, 8 + j)),
                  pl.BlockSpec((1, nq, SM_ROWS, tq), lambda b, j, i: (b, 0, 0, 0)),
                  pl.BlockSpec((1, LANES), fixed), pl.BlockSpec((1, LANES), fixed), pl.BlockSpec((1, LANES), fixed),
                  pl.BlockSpec((1, tq, LANES), tile), pl.BlockSpec((1, tq, LANES), tile), pl.BlockSpec((1, tq, LANES), tile)],
        out_specs=[pl.BlockSpec((1, tq, LANES), tile), pl.BlockSpec((1, s_len, LANES), full), pl.BlockSpec((1, s_len, LANES), full),
                   pl.BlockSpec((1, 1, nq, 8, tq), lambda b, j, i: (b, j, 0, 0, 0)),
                   pl.BlockSpec((1, LANES), fixed), pl.BlockSpec((1, LANES), fixed), pl.BlockSpec((1, LANES), fixed)],
        out_shape=[wide, wide, wide, jax.ShapeDtypeStruct((n_batch, 4, nq, 8, tq), F32), gain, gain, gain],
        scratch_shapes=[pltpu.VMEM((2, s_len, LANES), BF16), pltpu.VMEM((2, s_len, LANES), BF16),
                        pltpu.VMEM((s_len, LANES), F32), pltpu.VMEM((s_len, LANES), F32), pltpu.VMEM((nq, 8, tq), F32)],
        compiler_params=_cparams(("arbitrary", "arbitrary", "arbitrary")),
    )(pf, pf, pf, cb, gq2, gk2, go2, o, lse, don)


def _shift_down(x, k):
    row = lax.broadcasted_iota(jnp.int32, x.shape, 0)
    return jnp.where(row >= k, pltpu.roll(x, k, 0), 0.0)


def _shift_up(x, k):
    n = x.shape[0]
    row = lax.broadcasted_iota(jnp.int32, x.shape, 0)
    return jnp.where(row < n - k, pltpu.roll(x, n - k, 0), 0.0)


def _conv_silu(x, w):
    y = w[3:4] * x + w[2:3] * _shift_down(x, 1) + w[1:2] * _shift_down(x, 2) + w[0:1] * _shift_down(x, 3)
    return y, y * _sigmoid(y)


def _gdn_pre(pg, conv_w):
    n_batch, s_len, width = pg.shape
    ncb = width // LANES

    def body(x_ref, w_ref, o_ref):
        cb = pl.program_id(1)
        _, s = _conv_silu(x_ref[0], w_ref[...])
        sn = s * lax.rsqrt(jnp.sum(s * s, axis=-1, keepdims=True) + EPS)
        o_ref[0] = jnp.where(cb < 2 * GDN_HEADS, sn, s)

    return pl.pallas_call(
        body, name="gdn_pre", grid=(n_batch, ncb),
        in_specs=[pl.BlockSpec((1, s_len, LANES), lambda b, c: (b, 0, c)), pl.BlockSpec((8, LANES), lambda b, c: (0, c))],
        out_specs=pl.BlockSpec((1, s_len, LANES), lambda b, c: (b, 0, c)),
        out_shape=jax.ShapeDtypeStruct(pg.shape, F32),
        compiler_params=_cparams(("parallel", "parallel")),
    )(pg, conv_w)


def _gdn_pre_bwd(pg, conv_w, dout):
    n_batch, s_len, width = pg.shape
    ncb = width // LANES

    def body(x_ref, w_ref, d_ref, dx_ref, dw_ref):
        cb = pl.program_id(0)
        b = pl.program_id(1)
        x = x_ref[0]
        w = w_ref[...]
        d = d_ref[0]
        y, s = _conv_silu(x, w)
        rr = lax.rsqrt(jnp.sum(s * s, axis=-1, keepdims=True) + EPS)
        sn = s * rr
        ds_n = rr * (d - sn * jnp.sum(d * sn, axis=-1, keepdims=True))
        ds = jnp.where(cb < 2 * GDN_HEADS, ds_n, d)
        sig = _sigmoid(y)
        dy = ds * (sig * (1.0 + y * (1.0 - sig)))
        dx = w[3:4] * dy + w[2:3] * _shift_up(dy, 1) + w[1:2] * _shift_up(dy, 2) + w[0:1] * _shift_up(dy, 3)
        dx_ref[0] = dx.astype(BF16)
        dw = [jnp.sum(dy * _shift_down(x, 3 - jj), axis=0, keepdims=True) if jj < 3 else jnp.sum(dy * x, axis=0, keepdims=True)
              for jj in range(CONV_WIDTH)]
        rows = lax.broadcasted_iota(jnp.int32, (8, LANES), 0)
        dwb = jnp.zeros((8, LANES), F32)
        for jj in range(CONV_WIDTH):
            dwb = dwb + jnp.where(rows == jj, dw[jj], 0.0)

        @pl.when(b == 0)
        def _():
            dw_ref[...] = jnp.zeros_like(dw_ref)

        dw_ref[...] += dwb

    blk = lambda c, b: (b, 0, c)
    return pl.pallas_call(
        body, name="gdn_pre_bwd", grid=(ncb, n_batch),
        in_specs=[pl.BlockSpec((1, s_len, LANES), blk), pl.BlockSpec((8, LANES), lambda c, b: (0, c)), pl.BlockSpec((1, s_len, LANES), blk)],
        out_specs=[pl.BlockSpec((1, s_len, LANES), blk), pl.BlockSpec((8, LANES), lambda c, b: (0, c))],
        out_shape=[jax.ShapeDtypeStruct(pg.shape, BF16), jax.ShapeDtypeStruct((8, width), F32)],
        compiler_params=_cparams(("parallel", "arbitrary")),
    )(pg, conv_w, dout)


def _gdn_gates(smc, smr, a_c, dt_c, a_r, dt_r, h):
    lane = lax.broadcasted_iota(jnp.int32, (1, LANES), 1)
    sub = lax.broadcasted_iota(jnp.int32, (SM_ROWS, 1), 0)
    beta_c = jnp.sum(jnp.where(lane == SM_B + h, _sigmoid(smc), 0.0), axis=1, keepdims=True)
    g_all_c = -jnp.exp(a_c) * _softplus(smc + dt_c)
    g_c = jnp.sum(jnp.where(lane == SM_A + h, g_all_c, 0.0), axis=1, keepdims=True)
    g_all_r = -jnp.exp(a_r) * _softplus(smr + dt_r)
    g_r = jnp.sum(jnp.where(sub == SM_A + h, g_all_r, 0.0), axis=0, keepdims=True)
    return beta_c, g_c, g_r


def _gdn_group(qkv, z, smc, smr, a_c, dt_c, a_r, dt_r, go, states):
    n_grp = len(qkv)
    c = qkv[0].shape[0]
    hd = GDN_HEAD_DIM
    pairs = [(g, h) for g in range(n_grp) for h in range(GDN_HEADS)]
    ii = lax.broadcasted_iota(jnp.int32, (c, c), 0)
    jj = lax.broadcasted_iota(jnp.int32, (c, c), 1)
    incl = ii >= jj
    col = lambda arr, base, h: arr[:, base + h * hd:base + (h + 1) * hd]

    qs, ks, kbs, vbs, decays, gcs, g_lasts, amats = [], [], [], [], [], [], [], []
    for g, h in pairs:
        beta_c, g_c, g_r = _gdn_gates(smc[g], smr[g], a_c, dt_c, a_r, dt_r, h)
        gc_c = jnp.sum(jnp.where(incl, g_r, 0.0), axis=1, keepdims=True)
        gc_r = jnp.sum(jnp.where(ii <= jj, g_c, 0.0), axis=0, keepdims=True)
        decay = jnp.where(incl, jnp.exp(jnp.where(incl, gc_c - gc_r, 0.0)), 0.0)
        k = col(qkv[g], GDN_WIDTH, h)
        kb = k * beta_c
        qs.append(col(qkv[g], 0, h) * (hd ** -0.5))
        ks.append(k)
        kbs.append(kb)
        vbs.append(col(qkv[g], 2 * GDN_WIDTH, h) * beta_c)
        decays.append(decay)
        gcs.append(gc_c)
        g_lasts.append(jnp.sum(g_c, axis=0, keepdims=True))
        amats.append(jnp.where(ii > jj, _mm_nt(kb, k) * decay, 0.0))
    ts = _unit_lower_inverses(amats)
    egcs = [jnp.exp(gc) for gc in gcs]
    us = [_mm_nn(t, vb) for t, vb in zip(ts, vbs)]
    ws = [_mm_nn(t, kb * e) for t, kb, e in zip(ts, kbs, egcs)]
    intras = [_mm_nt(q, k) * d for q, k, d in zip(qs, ks, decays)]
    qes = [q * e for q, e in zip(qs, egcs)]
    kds = [k * jnp.exp(gl - gc) for k, gl, gc in zip(ks, g_lasts, gcs)]
    sdecs = [jnp.exp(gl) for gl in g_lasts]

    outs = []
    for g in range(n_grp):
        idx = [g * GDN_HEADS + h for h in range(GDN_HEADS)]
        v_new = [us[i] - _mm_nn(ws[i], states[h]) for h, i in enumerate(idx)]
        o_state = [_mm_nn(qes[i], states[h]) for h, i in enumerate(idx)]
        o_intra = [_mm_nn(intras[i], v_new[h]) for h, i in enumerate(idx)]
        states = [states[h] * sdecs[i] + _mm_tn(kds[i], v_new[h]) for h, i in enumerate(idx)]
        outs.append([_rms(o_state[h] + o_intra[h], go) * (col(z[g], 0, h) * _sigmoid(col(z[g], 0, h))) for h in range(GDN_HEADS)])
    return outs, states


def _gdn_group_size(n_chunks):
    return GDN_GROUP if n_chunks % GDN_GROUP == 0 else 1


def _gdn_fwd(qkvn, z, smc, smr, a_c, dt_c, a_r, dt_r, go):
    n_batch, s_len, _ = qkvn.shape
    c = GDN_CHUNK
    n = s_len // c
    grp = _gdn_group_size(n)
    ng = n // grp
    gc = grp * c
    hd = GDN_HEAD_DIM

    def body(qkv_ref, z_ref, smc_ref, smr_ref, ac_ref, dc_ref, ar_ref, dr_ref, go_ref, og_ref, st_ref, s_ref):
        @pl.when(pl.program_id(1) == 0)
        def _():
            s_ref[...] = jnp.zeros_like(s_ref)

        states = [s_ref[h] for h in range(GDN_HEADS)]
        for h in range(GDN_HEADS):
            st_ref[0, 0, h] = states[h]
        rows = lambda k: slice(k * c, (k + 1) * c)
        outs, nxt = _gdn_group([qkv_ref[0, rows(k), :] for k in range(grp)], [z_ref[0, rows(k), :] for k in range(grp)],
                               [smc_ref[0, rows(k), :] for k in range(grp)], [smr_ref[k] for k in range(grp)],
                               ac_ref[...], dc_ref[...], ar_ref[...], dr_ref[...], go_ref[...], states)
        for k in range(grp):
            for h in range(GDN_HEADS):
                og_ref[0, rows(k), h * hd:(h + 1) * hd] = outs[k][h].astype(BF16)
        for h in range(GDN_HEADS):
            s_ref[h] = nxt[h]

    tok = lambda b, i: (b, i, 0)
    fixed = lambda b, i: (0, 0)
    return pl.pallas_call(
        body, name="gdn_fwd", grid=(n_batch, ng),
        in_specs=[pl.BlockSpec((1, gc, 3 * GDN_WIDTH), tok), pl.BlockSpec((1, gc, GDN_WIDTH), tok), pl.BlockSpec((1, gc, LANES), tok),
                  pl.BlockSpec((grp, SM_ROWS, c), lambda b, i: (b * ng + i, 0, 0)),
                  pl.BlockSpec((1, LANES), fixed), pl.BlockSpec((1, LANES), fixed), pl.BlockSpec((SM_ROWS, 1), fixed),
                  pl.BlockSpec((SM_ROWS, 1), fixed), pl.BlockSpec((1, LANES), fixed)],
        out_specs=[pl.BlockSpec((1, gc, GDN_WIDTH), tok), pl.BlockSpec((1, 1, GDN_HEADS, hd, hd), lambda b, i: (b, i, 0, 0, 0))],
        out_shape=[jax.ShapeDtypeStruct((n_batch, s_len, GDN_WIDTH), BF16), jax.ShapeDtypeStruct((n_batch, ng, GDN_HEADS, hd, hd), F32)],
        scratch_shapes=[pltpu.VMEM((GDN_HEADS, hd, hd), F32)],
        compiler_params=_cparams(("parallel", "arbitrary")),
    )(qkvn, z, smc, smr, a_c, dt_c, a_r, dt_r, go)


def _gdn_bwd(qkvn, z, smc, smr, a_c, dt_c, a_r, dt_r, go, states, dog):
    n_batch, s_len, _ = qkvn.shape
    c = GDN_CHUNK
    n = s_len // c
    grp = _gdn_group_size(n)
    ng = n // grp
    gc = grp * c
    hd = GDN_HEAD_DIM

    def body(qkv_ref, z_ref, smc_ref, smr_ref, ac_ref, dc_ref, ar_ref, dr_ref, go_ref, st_ref, dog_ref,
             dqkv_ref, dz_ref, dsmc_ref, dsmr_ref, dac_ref, ddc_ref, dar_ref, ddr_ref, dgo_ref, ds_ref):
        first = (pl.program_id(0) == 0) & (pl.program_id(1) == 0)

        @pl.when(pl.program_id(1) == 0)
        def _():
            ds_ref[...] = jnp.zeros_like(ds_ref)

        @pl.when(first)
        def _():
            for r in (dac_ref, ddc_ref, dar_ref, ddr_ref, dgo_ref):
                r[...] = jnp.zeros_like(r)

        rows = lambda k: slice(k * c, (k + 1) * c)
        states = [st_ref[0, 0, h] for h in range(GDN_HEADS)]
        prim = ([qkv_ref[0, rows(k), :] for k in range(grp)], [z_ref[0, rows(k), :] for k in range(grp)],
                [smc_ref[0, rows(k), :] for k in range(grp)], [smr_ref[k] for k in range(grp)],
                ac_ref[...], dc_ref[...], ar_ref[...], dr_ref[...], go_ref[...], states)
        _, vjp = jax.vjp(_gdn_group, *prim)
        cot = ([[dog_ref[0, rows(k), h * hd:(h + 1) * hd] for h in range(GDN_HEADS)] for k in range(grp)],
               [ds_ref[h] for h in range(GDN_HEADS)])
        dqkv, dz, dsmc, dsmr, dac, ddc, dar, ddr, dgo, dstates = vjp(cot)
        for k in range(grp):
            dqkv_ref[0, rows(k), :] = dqkv[k]
            dz_ref[0, rows(k), :] = dz[k].astype(BF16)
            dsmc_ref[0, rows(k), :] = dsmc[k]
            dsmr_ref[k] = dsmr[k]
        dac_ref[...] += dac
        ddc_ref[...] += ddc
        dar_ref[...] += dar
        ddr_ref[...] += ddr
        dgo_ref[...] += dgo
        for h in range(GDN_HEADS):
            ds_ref[h] = dstates[h]

    tok = lambda b, i: (b, ng - 1 - i, 0)
    fixed = lambda b, i: (0, 0)
    lane_vec = jax.ShapeDtypeStruct((1, LANES), F32)
    row_vec = jax.ShapeDtypeStruct((SM_ROWS, 1), F32)
    return pl.pallas_call(
        body, name="gdn_bwd", grid=(n_batch, ng),
        in_specs=[pl.BlockSpec((1, gc, 3 * GDN_WIDTH), tok), pl.BlockSpec((1, gc, GDN_WIDTH), tok), pl.BlockSpec((1, gc, LANES), tok),
                  pl.BlockSpec((grp, SM_ROWS, c), lambda b, i: (b * ng + ng - 1 - i, 0, 0)),
                  pl.BlockSpec((1, LANES), fixed), pl.BlockSpec((1, LANES), fixed), pl.BlockSpec((SM_ROWS, 1), fixed),
                  pl.BlockSpec((SM_ROWS, 1), fixed), pl.BlockSpec((1, LANES), fixed),
                  pl.BlockSpec((1, 1, GDN_HEADS, hd, hd), lambda b, i: (b, ng - 1 - i, 0, 0, 0)),
                  pl.BlockSpec((1, gc, GDN_WIDTH), lambda b, i: (b, ng - 1 - i, 1))],
        out_specs=[pl.BlockSpec((1, gc, 3 * GDN_WIDTH), tok), pl.BlockSpec((1, gc, GDN_WIDTH), tok), pl.BlockSpec((1, gc, LANES), tok),
                   pl.BlockSpec((grp, SM_ROWS, c), lambda b, i: (b * ng + ng - 1 - i, 0, 0)),
                   pl.BlockSpec((1, LANES), fixed), pl.BlockSpec((1, LANES), fixed), pl.BlockSpec((SM_ROWS, 1), fixed),
                   pl.BlockSpec((SM_ROWS, 1), fixed), pl.BlockSpec((1, LANES), fixed)],
        out_shape=[jax.ShapeDtypeStruct((n_batch, s_len, 3 * GDN_WIDTH), F32), jax.ShapeDtypeStruct((n_batch, s_len, GDN_WIDTH), BF16),
                   jax.ShapeDtypeStruct((n_batch, s_len, LANES), F32), jax.ShapeDtypeStruct((n_batch * n, SM_ROWS, c), F32),
                   lane_vec, lane_vec, row_vec, row_vec, lane_vec],
        scratch_shapes=[pltpu.VMEM((GDN_HEADS, hd, hd), F32)],
        compiler_params=_cparams(("arbitrary", "arbitrary")),
    )(qkvn, z, smc, smr, a_c, dt_c, a_r, dt_r, go, states, dog)


def _out_proj(x, oa, ob, w_out, g_x, w_cq, tm=256):
    t_len, d = x.shape
    tm = min(tm, t_len)

    def body(x_ref, oa_ref, ob_ref, wo_ref, g_ref, wq_ref, x1_ref, hq_ref, cq_ref):
        x1 = x_ref[...] + _dot(oa_ref[...], wo_ref[0:FOX_WIDTH, :]) + _dot(ob_ref[...], wo_ref[FOX_WIDTH:2 * FOX_WIDTH, :])
        x1_ref[...] = x1
        hq = _rms(x1, g_ref[...]).astype(BF16)
        hq_ref[...] = hq
        cq_ref[...] = _dot(hq, wq_ref[...])

    row = lambda i: (i, 0)
    fixed = lambda i: (0, 0)
    return pl.pallas_call(
        body, name="out_proj", grid=(t_len // tm,),
        in_specs=[pl.BlockSpec((tm, d), row), pl.BlockSpec((tm, FOX_WIDTH), row), pl.BlockSpec((tm, GDN_WIDTH), row),
                  pl.BlockSpec((d, d), fixed), pl.BlockSpec((1, d), fixed), pl.BlockSpec((d, XATTN_WIDTH), fixed)],
        out_specs=[pl.BlockSpec((tm, d), row), pl.BlockSpec((tm, d), row), pl.BlockSpec((tm, XATTN_WIDTH), row)],
        out_shape=[jax.ShapeDtypeStruct((t_len, d), F32), jax.ShapeDtypeStruct((t_len, d), BF16), jax.ShapeDtypeStruct((t_len, XATTN_WIDTH), F32)],
        compiler_params=_cparams(("parallel",)),
    )(x, oa, ob, w_out, g_x, w_cq)


def _out_proj_bwd(dx1, w_out, tm=512):
    t_len, d = dx1.shape
    tm = min(tm, t_len)

    def body(dx_ref, w_ref, o_ref):
        o_ref[...] = _dot(dx_ref[...], w_ref[...], NT)

    return pl.pallas_call(
        body, name="out_proj_bwd", grid=(t_len // tm,),
        in_specs=[pl.BlockSpec((tm, d), lambda i: (i, 0)), pl.BlockSpec((d, d), lambda i: (0, 0))],
        out_specs=pl.BlockSpec((tm, d), lambda i: (i, 0)),
        out_shape=jax.ShapeDtypeStruct((t_len, d), F32),
        compiler_params=_cparams(("parallel",)),
    )(dx1, w_out)


def _mem_kv(mem, g, w_ckv, tm=256):
    t_len, d = mem.shape
    tm = min(tm, t_len)

    def body(x_ref, g_ref, w_ref, h_ref, o_ref):
        h = _rms(x_ref[...], g_ref[...]).astype(BF16)
        h_ref[...] = h
        o_ref[...] = _dot(h, w_ref[...])

    row = lambda i: (i, 0)
    fixed = lambda i: (0, 0)
    return pl.pallas_call(
        body, name="mem_kv", grid=(t_len // tm,),
        in_specs=[pl.BlockSpec((tm, d), row), pl.BlockSpec((1, d), fixed), pl.BlockSpec((d, 2 * XATTN_WIDTH), fixed)],
        out_specs=[pl.BlockSpec((tm, d), row), pl.BlockSpec((tm, 2 * XATTN_WIDTH), row)],
        out_shape=[jax.ShapeDtypeStruct((t_len, d), BF16), jax.ShapeDtypeStruct((t_len, 2 * XATTN_WIDTH), F32)],
        compiler_params=_cparams(("parallel",)),
    )(mem, g, w_ckv)


def _mem_kv_bwd(dckv, mem, g, w_ckv, tm=256):
    t_len, d = mem.shape
    tm = min(tm, t_len)

    def body(d_ref, x_ref, g_ref, w_ref, dg_ref):
        @pl.when(pl.program_id(0) == 0)
        def _():
            dg_ref[...] = jnp.zeros_like(dg_ref)

        dh = _dot(d_ref[...], w_ref[...], NT)
        _, dg = _rms_bwd(x_ref[...], g_ref[...], dh)
        dg_ref[...] += dg

    row = lambda i: (i, 0)
    fixed = lambda i: (0, 0)
    return pl.pallas_call(
        body, name="mem_kv_bwd", grid=(t_len // tm,),
        in_specs=[pl.BlockSpec((tm, 2 * XATTN_WIDTH), row), pl.BlockSpec((tm, d), row), pl.BlockSpec((1, d), fixed),
                  pl.BlockSpec((d, 2 * XATTN_WIDTH), fixed)],
        out_specs=pl.BlockSpec((1, d), fixed),
        out_shape=jax.ShapeDtypeStruct((1, d), F32),
        compiler_params=_cparams(("arbitrary",)),
    )(dckv, mem, g, w_ckv)


def _xattn_probs(qn, kn):
    s = _dot(qn, kn, NT) * (XATTN_HEAD_DIM ** -0.5)
    p = jnp.exp(s - jnp.max(s, axis=-1, keepdims=True))
    return p / jnp.sum(p, axis=-1, keepdims=True)


def _xattn_fwd(cq, ckv, x1, gq, gk, w_co, g_mlp, n_batch, s_len, m_len, tq=512):
    d = x1.shape[1]
    tq = min(tq, s_len)
    nq = s_len // tq
    hd = XATTN_HEAD_DIM

    def body(cq_ref, kv_ref, x1_ref, gq_ref, gk_ref, wo_ref, gm_ref, co_ref, x2_ref, hf_ref):
        outs = []
        for h in range(XATTN_HEADS):
            qn = _rms(cq_ref[:, h * hd:(h + 1) * hd], gq_ref[...])
            kn = _rms(kv_ref[:, h * hd:(h + 1) * hd], gk_ref[...])
            p = _xattn_probs(qn, kn)
            outs.append(_dot(p, kv_ref[:, XATTN_WIDTH + h * hd:XATTN_WIDTH + (h + 1) * hd]).astype(BF16))
        x2 = x1_ref[...]
        for h in range(XATTN_HEADS):
            co_ref[:, h * hd:(h + 1) * hd] = outs[h]
            x2 = x2 + _dot(outs[h], wo_ref[h * hd:(h + 1) * hd, :])
        x2_ref[...] = x2
        hf_ref[...] = _rms(x2, gm_ref[...]).astype(BF16)

    row = lambda b, i: (b * nq + i, 0)
    fixed = lambda b, i: (0, 0)
    t_len = n_batch * s_len
    return pl.pallas_call(
        body, name="xattn_fwd", grid=(n_batch, nq),
        in_specs=[pl.BlockSpec((tq, XATTN_WIDTH), row), pl.BlockSpec((m_len, 2 * XATTN_WIDTH), lambda b, i: (b, 0)),
                  pl.BlockSpec((tq, d), row), pl.BlockSpec((1, hd), fixed), pl.BlockSpec((1, hd), fixed),
                  pl.BlockSpec((XATTN_WIDTH, d), fixed), pl.BlockSpec((1, d), fixed)],
        out_specs=[pl.BlockSpec((tq, XATTN_WIDTH), row), pl.BlockSpec((tq, d), row), pl.BlockSpec((tq, d), row)],
        out_shape=[jax.ShapeDtypeStruct((t_len, XATTN_WIDTH), BF16), jax.ShapeDtypeStruct((t_len, d), F32),
                   jax.ShapeDtypeStruct((t_len, d), BF16)],
        compiler_params=_cparams(("parallel", "parallel")),
    )(cq, ckv, x1, gq, gk, w_co, g_mlp)


def _xattn_bwd(dx2, cq, ckv, x1, gq, gk, w_co, g_x, w_cq, n_batch, s_len, m_len, tq=512):
    d = x1.shape[1]
    tq = min(tq, s_len)
    nq = s_len // tq
    hd = XATTN_HEAD_DIM
    scale = XATTN_HEAD_DIM ** -0.5

    def body(dx2_ref, cq_ref, kv_ref, x1_ref, gq_ref, gk_ref, wo_ref, gx_ref, wq_ref,
             dx1_ref, dcq_ref, dkv_ref, dgq_ref, dgk_ref, dgx_ref, dk_acc, dv_acc):
        b = pl.program_id(0)
        i = pl.program_id(1)

        @pl.when((b == 0) & (i == 0))
        def _():
            dgq_ref[...] = jnp.zeros_like(dgq_ref)
            dgk_ref[...] = jnp.zeros_like(dgk_ref)
            dgx_ref[...] = jnp.zeros_like(dgx_ref)

        @pl.when(i == 0)
        def _():
            dk_acc[...] = jnp.zeros_like(dk_acc)
            dv_acc[...] = jnp.zeros_like(dv_acc)

        dx2 = dx2_ref[...]
        dhq = jnp.zeros((tq, d), F32)
        for h in range(XATTN_HEADS):
            sl = slice(h * hd, (h + 1) * hd)
            q = cq_ref[:, sl]
            qn = _rms(q, gq_ref[...])
            kn = _rms(kv_ref[:, sl], gk_ref[...])
            v = kv_ref[:, XATTN_WIDTH + h * hd:XATTN_WIDTH + (h + 1) * hd]
            p = _xattn_probs(qn, kn)
            dco = _dot(dx2, wo_ref[sl, :], NT)
            dv_acc[:, sl] += _dot(p, dco, TN)
            dp = _dot(dco, v, NT)
            ds = p * (dp - jnp.sum(dp * p, axis=-1, keepdims=True))
            dqn = _dot(ds, kn) * scale
            dk_acc[:, sl] += _dot(ds, qn, TN) * scale
            dq, dgq = _rms_bwd(q, gq_ref[...], dqn)
            dgq_ref[...] += dgq
            dqb = dq.astype(BF16)
            dcq_ref[:, sl] = dqb
            dhq = dhq + _dot(dqb, wq_ref[:, sl], NT)
        dxn, dgx = _rms_bwd(x1_ref[...], gx_ref[...], dhq)
        dgx_ref[...] += dgx
        dx1_ref[...] = dx2 + dxn

        @pl.when(i == nq - 1)
        def _():
            for h in range(XATTN_HEADS):
                sl = slice(h * hd, (h + 1) * hd)
                dk, dgk = _rms_bwd(kv_ref[:, sl], gk_ref[...], dk_acc[:, sl])
                dgk_ref[...] += dgk
                dkv_ref[:, sl] = dk.astype(BF16)
                dkv_ref[:, XATTN_WIDTH + h * hd:XATTN_WIDTH + (h + 1) * hd] = dv_acc[:, sl].astype(BF16)

    row = lambda b, i: (b * nq + i, 0)
    fixed = lambda b, i: (0, 0)
    t_len = n_batch * s_len
    return pl.pallas_call(
        body, name="xattn_bwd", grid=(n_batch, nq),
        in_specs=[pl.BlockSpec((tq, d), row), pl.BlockSpec((tq, XATTN_WIDTH), row), pl.BlockSpec((m_len, 2 * XATTN_WIDTH), lambda b, i: (b, 0)),
                  pl.BlockSpec((tq, d), row), pl.BlockSpec((1, hd), fixed), pl.BlockSpec((1, hd), fixed),
                  pl.BlockSpec((XATTN_WIDTH, d), fixed), pl.BlockSpec((1, d), fixed), pl.BlockSpec((d, XATTN_WIDTH), fixed)],
        out_specs=[pl.BlockSpec((tq, d), row), pl.BlockSpec((tq, XATTN_WIDTH), row), pl.BlockSpec((m_len, 2 * XATTN_WIDTH), lambda b, i: (b, 0)),
                   pl.BlockSpec((1, hd), fixed), pl.BlockSpec((1, hd), fixed), pl.BlockSpec((1, d), fixed)],
        out_shape=[jax.ShapeDtypeStruct((t_len, d), F32), jax.ShapeDtypeStruct((t_len, XATTN_WIDTH), BF16),
                   jax.ShapeDtypeStruct((n_batch * m_len, 2 * XATTN_WIDTH), BF16),
                   jax.ShapeDtypeStruct((1, hd), F32), jax.ShapeDtypeStruct((1, hd), F32), jax.ShapeDtypeStruct((1, d), F32)],
        scratch_shapes=[pltpu.VMEM((m_len, XATTN_WIDTH), F32), pltpu.VMEM((m_len, XATTN_WIDTH), F32)],
        compiler_params=_cparams(("arbitrary", "arbitrary")),
    )(dx2, cq, ckv, x1, gq, gk, w_co, g_x, w_cq)


def _resident(shape):
    return pl.BlockSpec(shape, lambda *_: (0,) * len(shape), pipeline_mode=pl.Buffered(1))


def _mlp_fwd(hf, x2, target, w1, w2, tm=256, tf=1024):
    t_len, d = x2.shape
    f = w1.shape[1]
    tm, tf = min(tm, t_len), min(tf, f)

    def body(hf_ref, x2_ref, tg_ref, w1_ref, w2_ref, u_ref, a_ref, dy_ref, ls_ref):
        hf_t = hf_ref[...]
        y = x2_ref[...]
        for k in range(f // tf):
            cols = slice(k * tf, (k + 1) * tf)
            u = _dot(hf_t, w1_ref[:, cols])
            u_ref[:, cols] = u
            r = jnp.maximum(u, 0.0)
            a = (r * r).astype(BF16)
            a_ref[:, cols] = a
            y = y + _dot(a, w2_ref[cols, :])
        err = y - tg_ref[...]
        dy_ref[...] = err * (1.0 / d)
        ls_ref[...] = jnp.broadcast_to(jnp.sum(jnp.sum(err * err, axis=-1, keepdims=True) * (1.0 / d), axis=0, keepdims=True), ls_ref.shape)

    row = lambda i: (i, 0)
    return pl.pallas_call(
        body, name="mlp_fwd", grid=(t_len // tm,),
        in_specs=[pl.BlockSpec((tm, d), row), pl.BlockSpec((tm, d), row), pl.BlockSpec((tm, d), row), _resident((d, f)), _resident((f, d))],
        out_specs=[pl.BlockSpec((tm, f), row), pl.BlockSpec((tm, f), row), pl.BlockSpec((tm, d), row),
                   pl.BlockSpec((1, 8, LANES), lambda i: (i, 0, 0))],
        out_shape=[jax.ShapeDtypeStruct((t_len, f), F32), jax.ShapeDtypeStruct((t_len, f), BF16), jax.ShapeDtypeStruct((t_len, d), F32),
                   jax.ShapeDtypeStruct((t_len // tm, 8, LANES), F32)],
        compiler_params=_cparams(("parallel",)),
    )(hf, x2, target, w1, w2)


def _mlp_bwd(dy, u, x2, g, w1, w2, tm=256, tf=1024):
    t_len, d = x2.shape
    f = w1.shape[1]
    tm, tf = min(tm, t_len), min(tf, f)

    def body(dy_ref, u_ref, x2_ref, g_ref, w1_ref, w2_ref, du_ref, dx2_ref, dg_ref):
        @pl.when(pl.program_id(0) == 0)
        def _():
            dg_ref[...] = jnp.zeros_like(dg_ref)

        dy_t = dy_ref[...]
        dyb = dy_t.astype(BF16)
        dhf = jnp.zeros((tm, d), F32)
        for k in range(f // tf):
            cols = slice(k * tf, (k + 1) * tf)
            da = _dot(dyb, w2_ref[cols, :], NT)
            du = (da * (2.0 * jnp.maximum(u_ref[:, cols], 0.0))).astype(BF16)
            du_ref[:, cols] = du
            dhf = dhf + _dot(du, w1_ref[:, cols], NT)
        dxn, dg = _rms_bwd(x2_ref[...], g_ref[...], dhf)
        dx2_ref[...] = dy_t + dxn
        dg_ref[...] += dg

    row = lambda i: (i, 0)
    fixed = lambda i: (0, 0)
    return pl.pallas_call(
        body, name="mlp_bwd", grid=(t_len // tm,),
        in_specs=[pl.BlockSpec((tm, d), row), pl.BlockSpec((tm, f), row), pl.BlockSpec((tm, d), row), pl.BlockSpec((1, d), fixed),
                  _resident((d, f)), _resident((f, d))],
        out_specs=[pl.BlockSpec((tm, f), row), pl.BlockSpec((tm, d), row), pl.BlockSpec((1, d), fixed)],
        out_shape=[jax.ShapeDtypeStruct((t_len, f), BF16), jax.ShapeDtypeStruct((t_len, d), F32), jax.ShapeDtypeStruct((1, d), F32)],
        compiler_params=_cparams(("arbitrary",)),
    )(dy, u, x2, g, w1, w2)


def _pad_lanes(v, offset=0, width=LANES):
    return jnp.zeros((1, width), F32).at[:, offset:offset + v.shape[1]].set(v)


def _col(v, offset=0, rows=SM_ROWS):
    return jnp.zeros((rows, 1), F32).at[offset:offset + v.shape[1], 0].set(v[0])


def _pack_small(g_mix, dgq, dgk, dbias, dgo, dac, dar, ddc, ddr, g_gdn_o, g_nx, g_mem, g_xq, g_xk, g_mlp, loss_tiles):
    def body(mix_ref, q_ref, k_ref, b_ref, o_ref, ac_ref, ar_ref, dc_ref, dr_ref, go_ref, nx_ref, mem_ref, xq_ref, xk_ref,
             mlp_ref, lt_ref, out_ref):
        lane = lax.broadcasted_iota(jnp.int32, (1, LANES), 1)
        diag = lax.broadcasted_iota(jnp.int32, (SM_ROWS, LANES), 0) == lax.broadcasted_iota(jnp.int32, (SM_ROWS, LANES), 1)

        def rolled(v, shift):
            return pltpu.roll(jnp.broadcast_to(v, (8, LANES)), shift, 1)[0:1, :]

        def rows_to_lanes(col):
            return jnp.sum(jnp.where(diag, col, 0.0), axis=0, keepdims=True)

        def put(row, v, n):
            out_ref[row:row + 1, 0:LANES] = jnp.where(lane < n, v, 0.0)

        out_ref[...] = jnp.zeros_like(out_ref)
        out_ref[0:1, :] = mix_ref[...]
        for row, ref in ((1, q_ref), (2, k_ref), (4, o_ref)):
            put(row, ref[...] + rolled(ref[...], FOX_HEAD_DIM), FOX_HEAD_DIM)
        put(3, rows_to_lanes(b_ref[...]), FOX_HEADS)
        for row, lane_ref, row_ref in ((5, ac_ref, ar_ref), (6, dc_ref, dr_ref)):
            put(row, rolled(lane_ref[...] + rows_to_lanes(row_ref[...]), LANES - SM_A), GDN_HEADS)
        put(7, go_ref[...], LANES)
        out_ref[8:9, :] = nx_ref[...]
        out_ref[9:10, :] = mem_ref[...]
        put(10, xq_ref[...], LANES)
        put(11, xk_ref[...], LANES)
        out_ref[12:13, :] = mlp_ref[...]
        put(LOSS_ROW, 0.5 * jnp.sum(lt_ref[...], axis=0)[0:1, :], 1)

    args = (g_mix, dgq, dgk, dbias, dgo, dac, dar, ddc, ddr, g_gdn_o, g_nx, g_mem, g_xq, g_xk, g_mlp, loss_tiles)
    return pl.pallas_call(body, name="pack_small", out_shape=jax.ShapeDtypeStruct((PACK_ROWS, D_MODEL), F32))(*args)


LATE_WEIGHTS = ("w_out", "w_cq", "w_ckv", "w_co", "w_mlp1", "w_mlp2")
GRAD_GROUPS = (("w_mlp2", "w_mlp1"), ("w_co", "w_cq", "w_ckv", "w_out"), ("w_in", "gdn_conv_w"))


def _local_step(x, mem, target, norm_mix_g, w_in, fox_qnorm_g, fox_knorm_g, fox_f_bias, fox_onorm_g, gdn_conv_w, gdn_A_log,
                gdn_dt_bias, gdn_onorm_g, norm_xattn_g, mem_norm_g, xattn_qnorm_g, xattn_knorm_g, norm_mlp_g,
                late_weights, grads_ready=None, first_token=0.0):
    if grads_ready is None:
        grads_ready = lambda group: 0.0
    n_batch, s_len, d = x.shape
    m_len = mem.shape[1]
    t_len = n_batch * s_len
    tq = min(FOX_BLOCK, s_len)
    nq = s_len // tq
    n_chunks = s_len // GDN_CHUNK
    x2d = x.reshape(t_len, d)

    wp = jnp.concatenate([w_in[:, 0:1536], w_in[:, 1544:3080], w_in[:, 3088:3600], w_in[:, 1536:1544], w_in[:, 3080:3088],
                          jnp.zeros((d, P_DIM - 3600), BF16)], axis=1)
    wst = jnp.concatenate([w_in[:, 1536:1544], w_in[:, 3080:3088]], axis=1).T
    conv_w = jnp.concatenate([gdn_conv_w, jnp.zeros((8 - CONV_WIDTH, gdn_conv_w.shape[1]), F32)], axis=0)
    bias_col = _col(fox_f_bias, SM_F)
    gq2, gk2, go2 = (jnp.tile(g, (1, 2)) for g in (fox_qnorm_g, fox_knorm_g, fox_onorm_g))
    a_c, dt_c = _pad_lanes(gdn_A_log, SM_A), _pad_lanes(gdn_dt_bias, SM_A)
    a_r, dt_r = _col(gdn_A_log, SM_A), _col(gdn_dt_bias, SM_A)

    h1, pfox, pgdn, pz, sm, smt = _in_proj(x2d, norm_mix_g + first_token, wp, wst)
    c_rows = _fox_cum(smt, bias_col, n_batch, s_len)
    cb = c_rows.reshape(SM_ROWS, n_batch, nq, tq).transpose(1, 2, 0, 3)
    pf3 = pfox.reshape(n_batch, s_len, 1536)
    o_fox, oa, lse = _fox_fwd(pf3, cb, gq2, gk2, go2, tq)
    pg3 = pgdn.reshape(n_batch, s_len, 1536)
    qkvn = _gdn_pre(pg3, conv_w)
    z3 = pz.reshape(n_batch, s_len, GDN_WIDTH)
    smc = sm.reshape(n_batch, s_len, LANES)
    smr = smt.reshape(SM_ROWS, n_batch * n_chunks, GDN_CHUNK).transpose(1, 0, 2)
    ob, states = _gdn_fwd(qkvn, z3, smc, smr, a_c, dt_c, a_r, dt_r, gdn_onorm_g)
    oa2, ob2 = oa.reshape(t_len, FOX_WIDTH), ob.reshape(t_len, GDN_WIDTH)
    late = late_weights(ob2)
    w_out, w_cq, w_ckv, w_co, w_mlp1, w_mlp2 = (late[k] for k in LATE_WEIGHTS)
    x1, hq, cq = _out_proj(x2d, oa2, ob2, w_out, norm_xattn_g, w_cq)
    mem2d = mem.reshape(n_batch * m_len, d)
    hm, ckv = _mem_kv(mem2d, mem_norm_g, w_ckv)
    co, x2, hf = _xattn_fwd(cq, ckv, x1, xattn_qnorm_g, xattn_knorm_g, w_co, norm_mlp_g, n_batch, s_len, m_len)
    u, a_act, dy, loss_tiles = _mlp_fwd(hf, x2, target.reshape(t_len, d), w_mlp1, w_mlp2)

    grads = {}
    du, dx2, grads["norm_mlp_g"] = _mlp_bwd(dy, u, x2, norm_mlp_g, w_mlp1, w_mlp2)
    grads["w_mlp2"] = _wgrad(a_act, dy, "wgrad_mlp2")
    grads["w_mlp1"] = _wgrad(hf, du, "wgrad_mlp1", column_blocks=D_FF // N_DEV)
    token = grads_ready({k: grads[k] for k in GRAD_GROUPS[0]})
    grads["w_co"] = _wgrad(co, dx2, "wgrad_co", column_blocks=D_MODEL // N_DEV)
    dx1, dcq, dckv, grads["xattn_qnorm_g"], grads["xattn_knorm_g"], grads["norm_xattn_g"] = _xattn_bwd(
        dx2, cq, ckv, x1, xattn_qnorm_g + token, xattn_knorm_g, w_co, norm_xattn_g, w_cq, n_batch, s_len, m_len)
    grads["w_cq"] = _wgrad(hq, dcq, "wgrad_cq")
    grads["w_ckv"] = _wgrad(hm, dckv, "wgrad_ckv")
    grads["mem_norm_g"] = _mem_kv_bwd(dckv, mem2d, mem_norm_g, w_ckv)
    grads["w_out"] = _wgrad(jnp.concatenate([oa2, ob2], axis=1), dx1, "wgrad_out")
    token = grads_ready({k: grads[k] for k in GRAD_GROUPS[1]})
    dcat = _out_proj_bwd(dx1, w_out)
    dcat3 = dcat.reshape(n_batch, s_len, d)

    dqkvn, dz, dsmc, dsmr, dac, ddc, dar, ddr, grads["gdn_onorm_g"] = _gdn_bwd(
        qkvn, z3, smc, smr, a_c, dt_c, a_r, dt_r, gdn_onorm_g + token, states, dcat3)
    dpg, dconv = _gdn_pre_bwd(pg3, conv_w, dqkvn)
    grads["gdn_conv_w"] = dconv[0:CONV_WIDTH]

    dq, dk, dv, dcb, dgq, dgk, dgo = _fox_bwd(pf3, cb, gq2, gk2, go2, o_fox, lse, dcat3[:, :, 0:FOX_WIDTH], tq)
    dc8 = dcb[:, :, :, 0:2, :].transpose(1, 3, 0, 2, 4).reshape(FOX_HEADS, t_len)
    dc_rows = jnp.concatenate([dc8, jnp.zeros((SM_ROWS - FOX_HEADS, t_len), F32)], axis=0)
    dl_rows, dbias = _fox_cum_bwd(dc_rows, smt, bias_col, n_batch, s_len)
    dsm_rows = jnp.concatenate([dl_rows[0:SM_B], dsmr.transpose(1, 0, 2).reshape(SM_ROWS, t_len)[SM_B:SM_ROWS]], axis=0)

    dproj = jnp.concatenate([dq.reshape(t_len, FOX_WIDTH), dk.reshape(t_len, FOX_WIDTH), dv.reshape(t_len, FOX_WIDTH),
                             dpg.reshape(t_len, 1536), dz.reshape(t_len, GDN_WIDTH), dsmc.reshape(t_len, LANES).astype(BF16)], axis=1)
    dwp = _wgrad(h1, dproj, "wgrad_in", bk=512, bn=P_DIM)
    dwst = _rows_matmul(dsm_rows, h1, "wgrad_in_rows")
    dw_small = dwp[:, P_SMALL:P_SMALL + SM_ROWS] + dwst.T
    grads["w_in"] = jnp.concatenate([dwp[:, 0:1536], dw_small[:, 0:8], dwp[:, 1536:3072], dw_small[:, 8:16], dwp[:, 3072:3584]], axis=1)
    token = grads_ready({k: grads[k] for k in GRAD_GROUPS[2]})
    grad_x, grads["norm_mix_g"] = _in_proj_bwd(dproj, dsm_rows, x2d, norm_mix_g + token, wp, wst, dx1)
    packed = _pack_small(grads["norm_mix_g"], dgq, dgk, dbias, dgo, dac, dar, ddc, ddr, grads["gdn_onorm_g"], grads["norm_xattn_g"],
                         grads["mem_norm_g"], grads["xattn_qnorm_g"], grads["xattn_knorm_g"], grads["norm_mlp_g"], loss_tiles)
    return packed, grad_x.reshape(n_batch, s_len, d), {k: grads[k] for k in SHARDED}


MESH_ID = pl.DeviceIdType.MESH
ANY_SPEC = pl.BlockSpec(memory_space=pl.ANY)


def _place():
    x, y, c = lax.axis_index("x"), lax.axis_index("y"), lax.axis_index("c")
    return x, y, c, [(1 - x, y), (x, 1 - y), (1 - x, 1 - y)]


def _all_gather_body(n, ins, outs, send_sems, recv_sems, local_sems):
    x, y, c, chips = _place()
    me, sibling = (x, y, c), (x, y, 1 - c)

    def copy(a, k, block, to, src=None):
        dst = outs[a].at[4 * block[0] + 2 * block[1] + block[2]]
        return pltpu.make_async_remote_copy(src_ref=dst if src is None else src, dst_ref=dst, send_sem=send_sems.at[a, k],
                                            recv_sem=recv_sems.at[a, k], device_id=to, device_id_type=MESH_ID)

    mine = [] if local_sems is None else [pltpu.make_async_copy(ins[a], outs[a].at[4 * x + 2 * y + c], local_sems.at[a]) for a in range(n)]
    for cp in mine:
        cp.start()
    first = []
    for a in range(n):
        first.append(copy(a, 0, me, sibling, src=ins[a]))
        first += [copy(a, 1 + j, me, (*chip, c), src=ins[a]) for j, chip in enumerate(chips)]
    for cp in first:
        cp.start()
    passed = []
    for j, chip in enumerate(chips):
        for a in range(n):
            copy(a, 1 + j, (*chip, c), me).wait_recv()
            fwd = copy(a, 4 + j, (*chip, c), sibling)
            fwd.start()
            passed.append(fwd)
    for a in range(n):
        copy(a, 0, sibling, me).wait_recv()
        for j, chip in enumerate(chips):
            copy(a, 4 + j, (*chip, 1 - c), me).wait_recv()
    for cp in first + passed:
        cp.wait_send()
    for cp in mine:
        cp.wait()


def _all_gather_hbm(arrs, name):
    n = len(arrs)
    me = 4 * lax.axis_index("x") + 2 * lax.axis_index("y") + lax.axis_index("c")

    def body(*refs):
        _all_gather_body(n, refs[:n], refs[n:2 * n], refs[2 * n], refs[2 * n + 1], None)

    got = pl.pallas_call(
        body, name=name, in_specs=[ANY_SPEC] * n, out_specs=[ANY_SPEC] * n,
        out_shape=[jax.ShapeDtypeStruct((N_DEV,) + a.shape, a.dtype) for a in arrs],
        scratch_shapes=[pltpu.SemaphoreType.DMA((n, 7)), pltpu.SemaphoreType.DMA((n, 7))],
    )(*arrs)
    return [lax.dynamic_update_slice(g, a[None], (me,) + (0,) * a.ndim) for g, a in zip(got, arrs)]


def _pair_exchange(arrs, name):
    n = len(arrs)

    def body(*refs):
        ins, outs = refs[:n], refs[n:2 * n]
        send_sems, recv_sems = refs[2 * n:]
        x, y, c, _ = _place()
        copies = []
        for a in range(n):
            for chip in range(4):
                copies.append(pltpu.make_async_remote_copy(
                    src_ref=ins[a].at[2 * chip + (1 - c)], dst_ref=outs[a].at[chip], send_sem=send_sems.at[a, chip],
                    recv_sem=recv_sems.at[a, chip], device_id=(x, y, 1 - c), device_id_type=MESH_ID))
        for cp in copies:
            cp.start()
        for cp in copies:
            cp.wait()

    return pl.pallas_call(
        body, name=name, in_specs=[ANY_SPEC] * n, out_specs=[ANY_SPEC] * n,
        out_shape=[jax.ShapeDtypeStruct((4,) + a.shape[1:], a.dtype) for a in arrs],
        scratch_shapes=[pltpu.SemaphoreType.DMA((n, 4)), pltpu.SemaphoreType.DMA((n, 4))],
    )(*arrs)


HBM_SPEC = pl.BlockSpec(memory_space=pltpu.HBM)
SEM_SPEC = pl.BlockSpec(memory_space=pltpu.SEMAPHORE)
DATAFLOW = pltpu.SideEffectType.DATAFLOW_SIDE_EFFECTING


def _in_hbm(arrs):
    return [pltpu.with_memory_space_constraint(a, pltpu.HBM) for a in arrs]


def _copies_start(name, srcs, lands, make_copies):
    n = len(srcs)
    n_copies = len(make_copies(srcs, lands, None, None)[0])

    def body(*refs):
        send_sems, recv_sems = refs[2 * n], refs[2 * n + 1]
        for row in make_copies(refs[:n], refs[n:2 * n], send_sems, recv_sems):
            for cp in row:
                cp.start()
        refs[-1][...] = jnp.zeros_like(refs[-1])

    sems = pltpu.SemaphoreType.DMA((n * n_copies,))
    thru = [pltpu.HBM(a.shape, a.dtype) for a in list(srcs) + list(lands)]
    res = pl.pallas_call(
        body, name=name, in_specs=[HBM_SPEC] * (2 * n),
        out_specs=(SEM_SPEC, SEM_SPEC, *[HBM_SPEC] * (2 * n), pl.BlockSpec(memory_space=pltpu.VMEM)),
        out_shape=(sems, sems, *thru, jax.ShapeDtypeStruct((8, LANES), F32)),
        input_output_aliases={i: 2 + i for i in range(2 * n)},
        compiler_params=pltpu.CompilerParams(has_side_effects=DATAFLOW),
    )(*_in_hbm(list(srcs) + list(lands)))
    return res[0], res[1], list(res[2:2 + n]), list(res[2 + n:2 + 2 * n]), res[-1]


def _copies_wait(name, send_sems, recv_sems, srcs, lands, after, make_copies):
    n = len(srcs)

    def body(*refs):
        for row in make_copies(refs[:n], refs[n:2 * n], refs[2 * n], refs[2 * n + 1]):
            for cp in row:
                cp.wait_send()
                cp.wait_recv()

    res = pl.pallas_call(
        body, name=name, in_specs=[HBM_SPEC] * (2 * n) + [SEM_SPEC, SEM_SPEC, ANY_SPEC],
        out_specs=tuple([HBM_SPEC] * (2 * n)),
        out_shape=tuple(pltpu.HBM(a.shape, a.dtype) for a in list(srcs) + list(lands)),
        input_output_aliases={i: i for i in range(2 * n)},
        compiler_params=pltpu.CompilerParams(has_side_effects=DATAFLOW),
    )(*srcs, *lands, send_sems, recv_sems, after)
    return list(res[:n]), list(res[n:])


def _gather_copies(srcs, lands, send_sems, recv_sems):
    if send_sems is None:
        return [[None] * 7]
    x, y, c, _ = _place()
    rows = []
    for a in range(len(srcs)):
        row = []
        for k in range(7):
            r = k + 1
            to = (1 - x if r & 4 else x, 1 - y if r & 2 else y, 1 - c if r & 1 else c)
            row.append(pltpu.make_async_remote_copy(
                src_ref=srcs[a], dst_ref=lands[a].at[4 * x + 2 * y + c], send_sem=send_sems.at[7 * a + k], recv_sem=recv_sems.at[7 * a + k],
                device_id=to, device_id_type=MESH_ID))
        rows.append(row)
    return rows


def _scatter_copies(srcs, lands, send_sems, recv_sems):
    if send_sems is None:
        return [[None] * 7]
    x, y, c, _ = _place()
    rows = []
    for a in range(len(srcs)):
        row = []
        for k in range(7):
            r = k + 1
            to = (1 - x if r & 4 else x, 1 - y if r & 2 else y, 1 - c if r & 1 else c)
            row.append(pltpu.make_async_remote_copy(
                src_ref=srcs[a].at[4 * to[0] + 2 * to[1] + to[2]], dst_ref=lands[a].at[k], send_sem=send_sems.at[7 * a + k],
                recv_sem=recv_sems.at[7 * a + k], device_id=to, device_id_type=MESH_ID))
        rows.append(row)
    return rows


def _chip_copies(srcs, lands, send_sems, recv_sems):
    if send_sems is None:
        return [[None] * 3]
    x, y, c, chips = _place()
    return [[pltpu.make_async_remote_copy(
        src_ref=srcs[a].at[2 * chip[0] + chip[1]], dst_ref=lands[a].at[j], send_sem=send_sems.at[3 * a + j], recv_sem=recv_sems.at[3 * a + j],
        device_id=(*chip, c), device_id_type=MESH_ID) for j, chip in enumerate(chips)] for a in range(len(srcs))]


def _all_gather_vmem(block, name):
    def body(in_ref, out_ref, send_sems, recv_sems, local_sems):
        _all_gather_body(1, [in_ref], [out_ref], send_sems, recv_sems, local_sems)

    vmem = pl.BlockSpec(memory_space=pltpu.VMEM)
    return pl.pallas_call(
        body, name=name, in_specs=[vmem], out_specs=vmem,
        out_shape=jax.ShapeDtypeStruct((N_DEV,) + block.shape, block.dtype),
        scratch_shapes=[pltpu.SemaphoreType.DMA((1, 7)), pltpu.SemaphoreType.DMA((1, 7)), pltpu.SemaphoreType.DMA((1,))],
    )(block)


def _row_tile(rows, cols):
    if rows <= 256:
        return rows
    return 256 if cols <= 512 else 128


def _pair_sum(core, own, got, name):
    _, rows, cols = own.shape
    tr = _row_tile(rows, cols)

    def body(c_ref, own_ref, got_ref, o_ref):
        o_ref[0] = own_ref[0] + got_ref[0]

    return pl.pallas_call(
        body, name=name,
        grid_spec=pltpu.PrefetchScalarGridSpec(
            num_scalar_prefetch=1, grid=(4, rows // tr),
            in_specs=[pl.BlockSpec((1, tr, cols), lambda k, i, c: (2 * k + c[0], i, 0)),
                      pl.BlockSpec((1, tr, cols), lambda k, i, c: (k, i, 0))],
            out_specs=pl.BlockSpec((1, tr, cols), lambda k, i, c: (k, i, 0))),
        out_shape=jax.ShapeDtypeStruct((4, rows, cols), F32),
        compiler_params=_cparams(("parallel", "parallel")),
    )(core, own, got)


def _adamw(w, g, m, v):
    m_new = ADAM_B1 * m + (1.0 - ADAM_B1) * g
    v_new = ADAM_B2 * v + (1.0 - ADAM_B2) * (g * g)
    m_hat = m_new / (1.0 - ADAM_B1 ** ADAM_STEP)
    v_hat = v_new / (1.0 - ADAM_B2 ** ADAM_STEP)
    delta = -ADAM_LR * (m_hat / (jnp.sqrt(v_hat) + ADAM_EPS) + ADAM_WD * w)
    return delta, m_new, v_new


def _sum_adam(chip, sums, parts, w, m, v, name):
    n_parts, rows, cols = parts.shape
    tr = _row_tile(rows, cols)

    def body(chip_ref, own_ref, p_ref, w_ref, m_ref, v_ref, g_ref, d_ref, mo_ref, vo_ref):
        g = own_ref[0]
        for k in range(n_parts):
            g = g + p_ref[k]
        g_ref[...] = g
        d_ref[...], mo_ref[...], vo_ref[...] = _adamw(w_ref[...], g, m_ref[...], v_ref[...])

    tile = pl.BlockSpec((tr, cols), lambda i, ch: (i, 0))
    out = jax.ShapeDtypeStruct((rows, cols), F32)
    return pl.pallas_call(
        body, name=name,
        grid_spec=pltpu.PrefetchScalarGridSpec(
            num_scalar_prefetch=1, grid=(rows // tr,),
            in_specs=[pl.BlockSpec((1, tr, cols), lambda i, ch: (ch[0], i, 0)),
                      pl.BlockSpec((n_parts, tr, cols), lambda i, ch: (0, i, 0)), tile, tile, tile],
            out_specs=[tile, tile, tile, tile]),
        out_shape=[out, out, out, out],
        compiler_params=_cparams(("parallel",)),
    )(chip, sums, parts, w, m, v)


SHARDED = ("w_in", "gdn_conv_w", "w_out", "w_cq", "w_ckv", "w_co", "w_mlp1", "w_mlp2")
COLUMN_SHARDED = ("w_in", "gdn_conv_w", "w_co", "w_mlp1")
REPLICATED = ("norm_mix_g", "fox_qnorm_g", "fox_knorm_g", "fox_f_bias", "fox_onorm_g", "gdn_A_log", "gdn_dt_bias", "gdn_onorm_g",
              "norm_xattn_g", "mem_norm_g", "xattn_qnorm_g", "xattn_knorm_g", "norm_mlp_g")
WEIGHTS = ("norm_mix_g", "w_in", "fox_qnorm_g", "fox_knorm_g", "fox_f_bias", "fox_onorm_g", "gdn_conv_w", "gdn_A_log", "gdn_dt_bias",
           "gdn_onorm_g", "w_out", "norm_xattn_g", "mem_norm_g", "w_cq", "w_ckv", "xattn_qnorm_g", "xattn_knorm_g", "w_co",
           "norm_mlp_g", "w_mlp1", "w_mlp2")
PACK_ROWS = 16
LOSS_ROW = len(REPLICATED)


def _whole(name, gathered):
    if name in COLUMN_SHARDED:
        return gathered.transpose(1, 0, 2).reshape(gathered.shape[1], N_DEV * gathered.shape[2])
    return gathered.reshape(N_DEV * gathered.shape[1], gathered.shape[2])


def _blocks(name, whole):
    if whole.ndim == 3:
        return whole
    if name in COLUMN_SHARDED:
        rows, cols = whole.shape
        return whole.reshape(rows, N_DEV, cols // N_DEV).transpose(1, 0, 2)
    return whole.reshape(N_DEV, whole.shape[0] // N_DEV, whole.shape[1])


def _adam_small(everyone, ws, ms, vs):
    n_par = len(ws)

    def body(*refs):
        ev_ref = refs[0]
        w_refs, m_refs, v_refs = (refs[1 + j * n_par:1 + (j + 1) * n_par] for j in range(3))
        outs = refs[1 + 3 * n_par:-1]
        sum_ref = refs[-1]
        total = ev_ref[0]
        for dev in range(1, N_DEV):
            total = total + ev_ref[dev]
        sum_ref[...] = total
        for i in range(n_par):
            n = w_refs[i].shape[1]
            g = sum_ref[i:i + 1, 0:n]
            outs[4 * i][...] = g
            outs[4 * i + 1][...], outs[4 * i + 2][...], outs[4 * i + 3][...] = _adamw(w_refs[i][...], g, m_refs[i][...], v_refs[i][...])
        outs[4 * n_par][...] = sum_ref[LOSS_ROW:LOSS_ROW + 1, 0:1]

    shapes = [jax.ShapeDtypeStruct(a.shape, F32) for a in ws for _ in range(4)] + [jax.ShapeDtypeStruct((1, 1), F32)]
    return pl.pallas_call(body, name="adam_small", out_shape=shapes,
                          scratch_shapes=[pltpu.VMEM((PACK_ROWS, D_MODEL), F32)])(everyone, *ws, *ms, *vs)


def kernel(x, mem, norm_mix_g, w_in, fox_qnorm_g, fox_knorm_g, fox_f_bias, fox_onorm_g, gdn_conv_w, gdn_A_log, gdn_dt_bias, gdn_onorm_g, w_out, norm_xattn_g, mem_norm_g, w_cq, w_ckv, xattn_qnorm_g, xattn_knorm_g, w_co, norm_mlp_g, w_mlp1, w_mlp2, loss_target, m_norm_mix_g, m_w_in, m_fox_qnorm_g, m_fox_knorm_g, m_fox_f_bias, m_fox_onorm_g, m_gdn_conv_w, m_gdn_A_log, m_gdn_dt_bias, m_gdn_onorm_g, m_w_out, m_norm_xattn_g, m_mem_norm_g, m_w_cq, m_w_ckv, m_xattn_qnorm_g, m_xattn_knorm_g, m_w_co, m_norm_mlp_g, m_w_mlp1, m_w_mlp2, v_norm_mix_g, v_w_in, v_fox_qnorm_g, v_fox_knorm_g, v_fox_f_bias, v_fox_onorm_g, v_gdn_conv_w, v_gdn_A_log, v_gdn_dt_bias, v_gdn_onorm_g, v_w_out, v_norm_xattn_g, v_mem_norm_g, v_w_cq, v_w_ckv, v_xattn_qnorm_g, v_xattn_knorm_g, v_w_co, v_norm_mlp_g, v_w_mlp1, v_w_mlp2):
    given = dict(locals())
    w = {k: given[k] for k in WEIGHTS}
    m = {k: given["m_" + k] for k in WEIGHTS}
    v = {k: given["v_" + k] for k in WEIGHTS}

    core = lax.axis_index("c").astype(jnp.int32).reshape(1)
    chip = (2 * lax.axis_index("x") + lax.axis_index("y")).astype(jnp.int32).reshape(1)
    me = 4 * lax.axis_index("x") + 2 * lax.axis_index("y") + lax.axis_index("c")

    shards = {k: w[k][0] if k == "gdn_conv_w" else w[k][0].astype(BF16) for k in SHARDED}
    early = [k for k in SHARDED if k not in LATE_WEIGHTS]
    whole = {k: _whole(k, g) for k, g in zip(early, _all_gather_hbm([shards[k] for k in early], "gather_early"))}
    late_shards = [shards[k] for k in LATE_WEIGHTS]
    late_lands = [lax.empty((N_DEV,) + s.shape, s.dtype) for s in late_shards]
    gather = _copies_start("gather_late_start", late_shards, late_lands, _gather_copies)

    def late_weights(after):
        srcs, lands = _copies_wait("gather_late_wait", gather[0], gather[1], gather[2], gather[3], after, _gather_copies)
        return {k: _whole(k, lax.dynamic_update_slice(land, src[None], (me, 0, 0))) for k, src, land in zip(LATE_WEIGHTS, srcs, lands)}

    pending = []

    def grads_ready(group):
        names = list(group)
        tag = str(len(pending))
        own = [_blocks(k, group[k]) for k in names]
        if "w_in" in names:
            got = _pair_exchange(own, "grad_pair_exchange_" + tag)
            srcs = [_pair_sum(core, o, g, "grad_pair_sum_" + k) for k, o, g in zip(names, own, got)]
            copies, index, n_parts = _chip_copies, chip, 3
        else:
            srcs, copies, index, n_parts = own, _scatter_copies, me.astype(jnp.int32).reshape(1), 7
        lands = [lax.empty((n_parts,) + s.shape[1:], s.dtype) for s in srcs]
        started = _copies_start("grad_exchange_start_" + tag, srcs, lands, copies)
        pending.append((names, started, copies, index))
        return started[4][0, 0]

    small = {k: w[k] for k in REPLICATED}
    packed, grad_x, _ = _local_step(x, mem, loss_target, **small, **whole, late_weights=late_weights,
                                    grads_ready=grads_ready, first_token=gather[4][0, 0])

    out_g, out_d, out_m, out_v = {}, {}, {}, {}
    after = grad_x
    for tag, (names, started, copies, index) in enumerate(pending):
        srcs, parts = _copies_wait("grad_exchange_wait_" + str(tag), started[0], started[1], started[2], started[3], after, copies)
        for k, s, p in zip(names, srcs, parts):
            res = _sum_adam(index, s, p, w[k][0], m[k][0], v[k][0], "adam_" + k)
            out_g[k], out_d[k], out_m[k], out_v[k] = (r[None] for r in res)
            after = res[0]

    everyone = _all_gather_vmem(packed, "gather_small")
    res = _adam_small(everyone, [w[k] for k in REPLICATED], [m[k] for k in REPLICATED], [v[k] for k in REPLICATED])
    for i, k in enumerate(REPLICATED):
        out_g[k], out_d[k], out_m[k], out_v[k] = res[4 * i:4 * i + 4]
    loss = res[-1].reshape(())

    return (loss, grad_x, *[out_g[k] for k in WEIGHTS], *[out_d[k] for k in WEIGHTS], *[out_m[k] for k in WEIGHTS],
            *[out_v[k] for k in WEIGHTS])
```

```python
import functools

import jax
import jax.numpy as jnp
import numpy as np
from jax import lax
from jax.experimental import pallas as pl
from jax.experimental.pallas import tpu as pltpu

F32 = jnp.float32
BF16 = jnp.bfloat16

D_MODEL = 1024
FOX_HEADS = 8
FOX_HEAD_DIM = 64
FOX_WIDTH = 512
GDN_HEADS = 4
GDN_HEAD_DIM = 128
GDN_WIDTH = 512
CONV_WIDTH = 4
GDN_CHUNK = 128
GDN_GROUP = 4
FOX_BLOCK = 512
XATTN_HEADS = 4
XATTN_HEAD_DIM = 128
XATTN_WIDTH = 512
D_FF = 4096
EPS = 1e-6
NEG_INF = -1e30
N_DEV = 8

ADAM_LR = 0.001
ADAM_B1 = 0.9
ADAM_B2 = 0.999
ADAM_EPS = 1e-08
ADAM_WD = 0.01
ADAM_STEP = 10

P_FOX = 0
P_GDN = 1536
P_Z = 3072
P_SMALL = 3584
P_DIM = 3712
SM_F = 0
SM_B = 8
SM_A = 12
SM_ROWS = 16

LANES = 128
VMEM_LIMIT = 56 * 1024 * 1024

NN = (((1,), (0,)), ((), ()))
NT = (((1,), (1,)), ((), ()))
TN = (((0,), (0,)), ((), ()))


def _dot(a, b, dims=NN):
    return lax.dot_general(a.astype(BF16), b.astype(BF16), dims, preferred_element_type=F32)


def _cparams(sem=None):
    kw = dict(vmem_limit_bytes=VMEM_LIMIT)
    if sem is not None:
        kw["dimension_semantics"] = sem
    return pltpu.CompilerParams(**kw)


def _sigmoid(x):
    return 0.5 * (jnp.tanh(0.5 * x) + 1.0)


def _softplus(x):
    return jnp.maximum(x, 0.0) + jnp.log1p(jnp.exp(-jnp.abs(x)))


def _log_sigmoid(x):
    return -_softplus(-x)


def _rms(x, g):
    r = lax.rsqrt(jnp.mean(x * x, axis=-1, keepdims=True) + EPS)
    return x * r * g


def _rms_bwd(x, g, dy):
    r = lax.rsqrt(jnp.mean(x * x, axis=-1, keepdims=True) + EPS)
    xh = x * r
    dg = jnp.sum(dy * xh, axis=0, keepdims=True)
    dyg = dy * g
    dx = r * (dyg - xh * jnp.mean(dyg * xh, axis=-1, keepdims=True))
    return dx, dg


def _pair_stat(t, m0):
    s0 = jnp.sum(jnp.where(m0, t, 0.0), axis=-1, keepdims=True)
    s1 = jnp.sum(jnp.where(m0, 0.0, t), axis=-1, keepdims=True)
    return jnp.where(m0, s0, s1)


def _rms_pair(x, g, m0):
    r = lax.rsqrt(_pair_stat(x * x, m0) * (1.0 / FOX_HEAD_DIM) + EPS)
    return x * r * g


def _rms_pair_bwd(x, g, dy, m0):
    r = lax.rsqrt(_pair_stat(x * x, m0) * (1.0 / FOX_HEAD_DIM) + EPS)
    xh = x * r
    dg = jnp.sum(dy * xh, axis=0, keepdims=True)
    dyg = dy * g
    dx = r * (dyg - xh * (_pair_stat(dyg * xh, m0) * (1.0 / FOX_HEAD_DIM)))
    return dx, dg


@jax.custom_vjp
def _mm_nn(a, b):
    return _dot(a, b, NN)


_mm_nn.defvjp(lambda a, b: (_dot(a, b, NN), (a, b)),
              lambda r, g: (_dot(g, r[1], NT), _dot(r[0], g, TN)))


@jax.custom_vjp
def _mm_nt(a, b):
    return _dot(a, b, NT)


_mm_nt.defvjp(lambda a, b: (_dot(a, b, NT), (a, b)),
              lambda r, g: (_dot(g, r[1], NN), _dot(g, r[0], TN)))


@jax.custom_vjp
def _mm_tn(a, b):
    return _dot(a, b, TN)


_mm_tn.defvjp(lambda a, b: (_dot(a, b, TN), (a, b)),
              lambda r, g: (_dot(r[1], g, NT), _dot(r[0], g, NN)))


def _dot3(a, b, dims):
    ah = a.astype(BF16)
    al = (a - ah.astype(F32)).astype(BF16)
    bh = b.astype(BF16)
    bl = (b - bh.astype(F32)).astype(BF16)
    d = functools.partial(lax.dot_general, dimension_numbers=dims, preferred_element_type=F32)
    return d(ah, bh) + d(ah, bl) + d(al, bh)


def _neumann_inverses(mats):
    c = mats[0].shape[0]
    eye = (lax.broadcasted_iota(jnp.int32, (c, c), 0) == lax.broadcasted_iota(jnp.int32, (c, c), 1)).astype(F32)
    xs = [eye - a for a in mats]
    ps = list(mats)
    k = 2
    while k < c + 1:
        ps = [_dot3(p, p, NN) for p in ps]
        xs = [x + _dot3(x, p, NN) for x, p in zip(xs, ps)]
        k *= 2
    return xs


@jax.custom_vjp
def _unit_lower_inverses(mats):
    return _neumann_inverses(mats)


def _unit_lower_inverses_fwd(mats):
    ts = _neumann_inverses(mats)
    return ts, ts


def _unit_lower_inverses_bwd(ts, gs):
    left = [_dot3(t, g, TN) for t, g in zip(ts, gs)]
    return ([-_dot3(m, t, NT) for m, t in zip(left, ts)],)


_unit_lower_inverses.defvjp(_unit_lower_inverses_fwd, _unit_lower_inverses_bwd)


def _wgrad(a, b, name, bk=1024, bn=1024, bt=512, column_blocks=None):
    t_len, k_len = a.shape
    n_len = b.shape[1]
    bk, bn, bt = min(bk, k_len), min(bn, n_len), min(bt, t_len)
    nt = t_len // bt

    def body(a_ref, b_ref, o_ref, acc_ref):
        t = pl.program_id(2)

        @pl.when(t == 0)
        def _():
            acc_ref[...] = jnp.zeros_like(acc_ref)

        acc_ref[...] += _dot(a_ref[...], b_ref[...], TN)

        @pl.when(t == nt - 1)
        def _():
            if column_blocks:
                for jj in range(bn // column_blocks):
                    o_ref[jj] = acc_ref[:, jj * column_blocks:(jj + 1) * column_blocks]
            else:
                o_ref[...] = acc_ref[...]

    if column_blocks:
        out_spec = pl.BlockSpec((bn // column_blocks, bk, column_blocks), lambda i, j, t: (j, i, 0))
        out_shape = jax.ShapeDtypeStruct((n_len // column_blocks, k_len, column_blocks), F32)
    else:
        out_spec = pl.BlockSpec((bk, bn), lambda i, j, t: (i, j))
        out_shape = jax.ShapeDtypeStruct((k_len, n_len), F32)
    return pl.pallas_call(
        body, name=name, grid=(k_len // bk, n_len // bn, nt),
        in_specs=[pl.BlockSpec((bt, bk), lambda i, j, t: (t, i)), pl.BlockSpec((bt, bn), lambda i, j, t: (t, j))],
        out_specs=out_spec, out_shape=out_shape,
        scratch_shapes=[pltpu.VMEM((bk, bn), F32)],
        compiler_params=_cparams(("parallel", "parallel", "arbitrary")),
    )(a, b)


def _rows_matmul(a, b, name, bt=512):
    r_len, t_len = a.shape
    n_len = b.shape[1]
    bt = min(bt, t_len)
    nt = t_len // bt

    def body(a_ref, b_ref, o_ref):
        t = pl.program_id(0)

        @pl.when(t == 0)
        def _():
            o_ref[...] = jnp.zeros_like(o_ref)

        o_ref[...] += _dot(a_ref[...], b_ref[...], NN)

    return pl.pallas_call(
        body, name=name, grid=(nt,),
        in_specs=[pl.BlockSpec((r_len, bt), lambda t: (0, t)), pl.BlockSpec((bt, n_len), lambda t: (t, 0))],
        out_specs=pl.BlockSpec((r_len, n_len), lambda t: (0, 0)),
        out_shape=jax.ShapeDtypeStruct((r_len, n_len), F32),
        compiler_params=_cparams(("arbitrary",)),
    )(a, b)


def _in_proj(x, g, wp, wst, tm=256):
    t_len, d = x.shape
    tm = min(tm, t_len)

    def body(x_ref, g_ref, wp_ref, wst_ref, h_ref, fox_ref, gdn_ref, z_ref, sm_ref, smt_ref):
        h = _rms(x_ref[...], g_ref[...]).astype(BF16)
        h_ref[...] = h
        p = _dot(h, wp_ref[...], NT)
        fox_ref[...] = p[:, P_FOX:P_GDN]
        gdn_ref[...] = p[:, P_GDN:P_Z]
        z_ref[...] = p[:, P_Z:P_SMALL]
        sm_ref[...] = p[:, P_SMALL:P_DIM]
        smt_ref[...] = _dot(wst_ref[...], h, NT)

    row = lambda i: (i, 0)
    fixed = lambda i: (0, 0)
    return pl.pallas_call(
        body, name="in_proj", grid=(t_len // tm,),
        in_specs=[pl.BlockSpec((tm, d), row), pl.BlockSpec((1, d), fixed), pl.BlockSpec((P_DIM, d), fixed),
                  pl.BlockSpec((SM_ROWS, d), fixed)],
        out_specs=[pl.BlockSpec((tm, d), row), pl.BlockSpec((tm, 1536), row), pl.BlockSpec((tm, 1536), row),
                   pl.BlockSpec((tm, 512), row), pl.BlockSpec((tm, LANES), row), pl.BlockSpec((SM_ROWS, tm), lambda i: (0, i))],
        out_shape=[jax.ShapeDtypeStruct((t_len, d), BF16), jax.ShapeDtypeStruct((t_len, 1536), F32),
                   jax.ShapeDtypeStruct((t_len, 1536), F32), jax.ShapeDtypeStruct((t_len, 512), F32),
                   jax.ShapeDtypeStruct((t_len, LANES), F32), jax.ShapeDtypeStruct((SM_ROWS, t_len), F32)],
        compiler_params=_cparams(("parallel",)),
    )(x, g, wp, wst)


def _in_proj_bwd(dproj, dsmt, x, g, wp, wst, dx1, tm=256):
    t_len, d = x.shape
    tm = min(tm, t_len)

    def body(dp_ref, dst_ref, x_ref, g_ref, wp_ref, wst_ref, dx1_ref, dx_ref, dg_ref):
        i = pl.program_id(0)
        dh = _dot(dp_ref[...], wp_ref[...], NN) + _dot(dst_ref[...], wst_ref[...], TN)
        dxn, dg = _rms_bwd(x_ref[...], g_ref[...], dh)
        dx_ref[...] = dx1_ref[...] + dxn

        @pl.when(i == 0)
        def _():
            dg_ref[...] = jnp.zeros_like(dg_ref)

        dg_ref[...] += dg

    row = lambda i: (i, 0)
    fixed = lambda i: (0, 0)
    return pl.pallas_call(
        body, name="in_proj_bwd", grid=(t_len // tm,),
        in_specs=[pl.BlockSpec((tm, P_DIM), row), pl.BlockSpec((SM_ROWS, tm), lambda i: (0, i)), pl.BlockSpec((tm, d), row),
                  pl.BlockSpec((1, d), fixed), pl.BlockSpec((P_DIM, d), fixed), pl.BlockSpec((SM_ROWS, d), fixed),
                  pl.BlockSpec((tm, d), row)],
        out_specs=[pl.BlockSpec((tm, d), row), pl.BlockSpec((1, d), fixed)],
        out_shape=[jax.ShapeDtypeStruct((t_len, d), F32), jax.ShapeDtypeStruct((1, d), F32)],
        compiler_params=_cparams(("arbitrary",)),
    )(dproj, dsmt, x, g, wp, wst, dx1)


def _fox_cum(smt, bias_col, n_batch, s_len, ck=256):
    ck = min(ck, s_len)

    def body(s_ref, b_ref, c_ref):
        tri = (lax.broadcasted_iota(jnp.int32, (ck, ck), 0) <= lax.broadcasted_iota(jnp.int32, (ck, ck), 1)).astype(F32)
        carry = jnp.zeros((SM_ROWS, 1), F32)
        for r in range(s_len // ck):
            ls = _log_sigmoid(s_ref[:, r * ck:(r + 1) * ck] + b_ref[...])
            c = jnp.dot(ls, tri, precision=lax.Precision.HIGHEST, preferred_element_type=F32) + carry
            c_ref[:, r * ck:(r + 1) * ck] = c
            carry = c[:, ck - 1:ck]

    return pl.pallas_call(
        body, name="fox_cum", grid=(n_batch,),
        in_specs=[pl.BlockSpec((SM_ROWS, s_len), lambda b: (0, b)), pl.BlockSpec((SM_ROWS, 1), lambda b: (0, 0))],
        out_specs=pl.BlockSpec((SM_ROWS, s_len), lambda b: (0, b)),
        out_shape=jax.ShapeDtypeStruct(smt.shape, F32),
        compiler_params=_cparams(("parallel",)),
    )(smt, bias_col)


def _fox_cum_bwd(dc, smt, bias_col, n_batch, s_len, ck=256):
    ck = min(ck, s_len)
    nr = s_len // ck

    def body(dc_ref, s_ref, b_ref, dl_ref, db_ref):
        b = pl.program_id(0)
        tri = (lax.broadcasted_iota(jnp.int32, (ck, ck), 0) >= lax.broadcasted_iota(jnp.int32, (ck, ck), 1)).astype(F32)
        carry = jnp.zeros((SM_ROWS, 1), F32)
        tot = jnp.zeros((SM_ROWS, 1), F32)
        for r in reversed(range(nr)):
            sl = slice(r * ck, (r + 1) * ck)
            dls = jnp.dot(dc_ref[:, sl], tri, precision=lax.Precision.HIGHEST, preferred_element_type=F32) + carry
            carry = dls[:, 0:1]
            dl = dls * (1.0 - _sigmoid(s_ref[:, sl] + b_ref[...]))
            dl_ref[:, sl] = dl
            tot = tot + jnp.sum(dl, axis=1, keepdims=True)

        @pl.when(b == 0)
        def _():
            db_ref[...] = jnp.zeros_like(db_ref)

        db_ref[...] += jnp.broadcast_to(tot, db_ref.shape)

    return pl.pallas_call(
        body, name="fox_cum_bwd", grid=(n_batch,),
        in_specs=[pl.BlockSpec((SM_ROWS, s_len), lambda b: (0, b)), pl.BlockSpec((SM_ROWS, s_len), lambda b: (0, b)),
                  pl.BlockSpec((SM_ROWS, 1), lambda b: (0, 0))],
        out_specs=[pl.BlockSpec((SM_ROWS, s_len), lambda b: (0, b)), pl.BlockSpec((SM_ROWS, LANES), lambda b: (0, 0))],
        out_shape=[jax.ShapeDtypeStruct(smt.shape, F32), jax.ShapeDtypeStruct((SM_ROWS, LANES), F32)],
        compiler_params=_cparams(("arbitrary",)),
    )(dc, smt, bias_col)


def _fox_diagonal_mask(tq):
    return lax.broadcasted_iota(jnp.int32, (tq, tq), 1) <= lax.broadcasted_iota(jnp.int32, (tq, tq), 0)


def _fox_fwd(pf, cb, gq2, gk2, go2, tq=256):
    n_batch, s_len, _ = pf.shape
    tq = min(tq, s_len)
    nq = s_len // tq
    scale = FOX_HEAD_DIM ** -0.5

    def body(q_ref, k_ref, v_ref, c_ref, gq_ref, gk_ref, go_ref, o_ref, on_ref, lse_ref, kh_ref, vh_ref):
        j = pl.program_id(1)
        i = pl.program_id(2)
        m0 = lax.broadcasted_iota(jnp.int32, (1, LANES), 1) < FOX_HEAD_DIM

        @pl.when(i == 0)
        def _():
            kn = _rms_pair(k_ref[0], gk_ref[...], m0)
            kh_ref[0] = jnp.where(m0, kn, 0.0).astype(BF16)
            kh_ref[1] = jnp.where(m0, 0.0, kn).astype(BF16)
            v = v_ref[0]
            vh_ref[0] = jnp.where(m0, v, 0.0).astype(BF16)
            vh_ref[1] = jnp.where(m0, 0.0, v).astype(BF16)

        qb = (_rms_pair(q_ref[0], gq_ref[...], m0) * scale).astype(BF16)

        def step(kb, carry, diagonal=False):
            ms, ls, acc = carry
            off = pl.multiple_of(kb * tq, tq)
            new_m, new_l, alphas, pv = [], [], [], []
            for hh in range(2):
                s = _dot(qb, kh_ref[hh, pl.ds(off, tq), :], NT)
                s = s - c_ref[0, kb, pl.ds(2 * j + hh, 1), :]
                if diagonal:
                    s = jnp.where(_fox_diagonal_mask(tq), s, NEG_INF)
                m_new = jnp.maximum(ms[hh], jnp.max(s, axis=-1, keepdims=True))
                alpha = jnp.exp(ms[hh] - m_new)
                p = jnp.exp(s - m_new)
                new_l.append(alpha * ls[hh] + jnp.sum(p, axis=-1, keepdims=True))
                new_m.append(m_new)
                alphas.append(alpha)
                pv.append(_dot(p, vh_ref[hh, pl.ds(off, tq), :], NN))
            acc = jnp.where(m0, alphas[0], alphas[1]) * acc + pv[0] + pv[1]
            return tuple(new_m), tuple(new_l), acc

        init_m = (jnp.full((tq, 1), NEG_INF, F32),) * 2
        init_l = (jnp.zeros((tq, 1), F32),) * 2
        carry = lax.fori_loop(0, i, step, (init_m, init_l, jnp.zeros((tq, LANES), F32)))
        ms, ls, acc = step(i, carry, diagonal=True)
        o = acc / jnp.where(m0, ls[0], ls[1])
        o_ref[0] = o
        on_ref[0] = _rms_pair(o, go_ref[...], m0).astype(BF16)
        lse_ref[0] = jnp.where(m0, ms[0] + jnp.log(ls[0]), ms[1] + jnp.log(ls[1]))

    fixed = lambda b, j, i: (0, 0)
    tile = lambda b, j, i: (b, i, j)
    return pl.pallas_call(
        body, name="fox_fwd", grid=(n_batch, 4, nq),
        in_specs=[pl.BlockSpec((1, tq, LANES), tile), pl.BlockSpec((1, s_len, LANES), lambda b, j, i: (b, 0, 4 + j)),
                  pl.BlockSpec((1, s_len, LANES), lambda b, j, i: (b, 0, 8 + j)),
                  pl.BlockSpec((1, nq, SM_ROWS, tq), lambda b, j, i: (b, 0, 0, 0)),
                  pl.BlockSpec((1, LANES), fixed), pl.BlockSpec((1, LANES), fixed), pl.BlockSpec((1, LANES), fixed)],
        out_specs=[pl.BlockSpec((1, tq, LANES), tile), pl.BlockSpec((1, tq, LANES), tile), pl.BlockSpec((1, tq, LANES), tile)],
        out_shape=[jax.ShapeDtypeStruct((n_batch, s_len, FOX_WIDTH), F32), jax.ShapeDtypeStruct((n_batch, s_len, FOX_WIDTH), BF16),
                   jax.ShapeDtypeStruct((n_batch, s_len, FOX_WIDTH), F32)],
        scratch_shapes=[pltpu.VMEM((2, s_len, LANES), BF16), pltpu.VMEM((2, s_len, LANES), BF16)],
        compiler_params=_cparams(("parallel", "parallel", "arbitrary")),
    )(pf, pf, pf, cb, gq2, gk2, go2)


def _fox_bwd(pf, cb, gq2, gk2, go2, o, lse, don, tq=256):
    n_batch, s_len, _ = pf.shape
    tq = min(tq, s_len)
    nq = s_len // tq
    scale = FOX_HEAD_DIM ** -0.5

    def body(q_ref, k_ref, v_ref, c_ref, gq_ref, gk_ref, go_ref, o_ref, lse_ref, don_ref,
             dq_ref, dk_ref, dv_ref, dc_ref, dgq_ref, dgk_ref, dgo_ref, kh_ref, vh_ref, dka_ref, dva_ref, dca_ref):
        b = pl.program_id(0)
        j = pl.program_id(1)
        i = pl.program_id(2)
        m0 = lax.broadcasted_iota(jnp.int32, (1, LANES), 1) < FOX_HEAD_DIM

        @pl.when((b == 0) & (j == 0) & (i == 0))
        def _():
            dgq_ref[...] = jnp.zeros_like(dgq_ref)
            dgk_ref[...] = jnp.zeros_like(dgk_ref)
            dgo_ref[...] = jnp.zeros_like(dgo_ref)

        @pl.when(i == 0)
        def _():
            kn = _rms_pair(k_ref[0], gk_ref[...], m0)
            kh_ref[0] = jnp.where(m0, kn, 0.0).astype(BF16)
            kh_ref[1] = jnp.where(m0, 0.0, kn).astype(BF16)
            v = v_ref[0]
            vh_ref[0] = jnp.where(m0, v, 0.0).astype(BF16)
            vh_ref[1] = jnp.where(m0, 0.0, v).astype(BF16)
            dka_ref[...] = jnp.zeros_like(dka_ref)
            dva_ref[...] = jnp.zeros_like(dva_ref)
            dca_ref[...] = jnp.zeros_like(dca_ref)

        q = q_ref[0]
        qn = _rms_pair(q, gq_ref[...], m0)
        qs = qn * scale
        qb = qs.astype(BF16)
        qh = (jnp.where(m0, qs, 0.0).astype(BF16), jnp.where(m0, 0.0, qs).astype(BF16))
        ot = o_ref[0]
        do, dgo = _rms_pair_bwd(ot, go_ref[...], don_ref[0], m0)
        dgo_ref[...] += dgo
        dd = do * ot
        delta = (jnp.sum(jnp.where(m0, dd, 0.0), axis=-1, keepdims=True), jnp.sum(jnp.where(m0, 0.0, dd), axis=-1, keepdims=True))
        doh = (jnp.where(m0, do, 0.0).astype(BF16), jnp.where(m0, 0.0, do).astype(BF16))
        lse_t = lse_ref[0]
        lse_h = (lse_t[:, 0:1], lse_t[:, FOX_HEAD_DIM:FOX_HEAD_DIM + 1])

        def step(kb, carry, diagonal=False):
            dqn, rs = carry
            rs = list(rs)
            off = pl.multiple_of(kb * tq, tq)
            for hh in range(2):
                kblk = kh_ref[hh, pl.ds(off, tq), :]
                vblk = vh_ref[hh, pl.ds(off, tq), :]
                s = _dot(qb, kblk, NT)
                s = s - c_ref[0, kb, pl.ds(2 * j + hh, 1), :]
                if diagonal:
                    s = jnp.where(_fox_diagonal_mask(tq), s, NEG_INF)
                p = jnp.exp(s - lse_h[hh])
                dp = _dot(doh[hh], vblk, NT)
                ds = p * (dp - delta[hh])
                dva_ref[pl.ds(off, tq), :] += _dot(p, doh[hh], TN)
                dka_ref[pl.ds(off, tq), :] += _dot(ds, qh[hh], TN)
                dca_ref[kb, hh:hh + 1, :] += -jnp.sum(ds, axis=0, keepdims=True)
                rs[hh] = rs[hh] + jnp.sum(ds, axis=-1, keepdims=True)
                dqn = dqn + _dot(ds, kblk, NN)
            return dqn, tuple(rs)

        carry = lax.fori_loop(0, i, step, (jnp.zeros((tq, LANES), F32), (jnp.zeros((tq, 1), F32),) * 2))
        dqn, rs = step(i, carry, diagonal=True)
        dqn = dqn * scale
        rs_rows = jnp.where(m0, rs[0], rs[1]).T
        dca_ref[i, 0:1, :] += rs_rows[0:1, :]
        dca_ref[i, 1:2, :] += rs_rows[FOX_HEAD_DIM:FOX_HEAD_DIM + 1, :]
        dq, dgq = _rms_pair_bwd(q, gq_ref[...], dqn, m0)
        dq_ref[0] = dq.astype(BF16)
        dgq_ref[...] += dgq

        @pl.when(i == nq - 1)
        def _():
            dk, dgk = _rms_pair_bwd(k_ref[0], gk_ref[...], dka_ref[...], m0)
            dk_ref[0] = dk.astype(BF16)
            dgk_ref[...] += dgk
            dv_ref[0] = dva_ref[...].astype(BF16)
            dc_ref[0, 0] = dca_ref[...]

    fixed = lambda b, j, i: (0, 0)
    tile = lambda b, j, i: (b, i, j)
    full = lambda b, j, i: (b, 0, j)
    wide = jax.ShapeDtypeStruct((n_batch, s_len, FOX_WIDTH), BF16)
    gain = jax.ShapeDtypeStruct((1, LANES), F32)
    return pl.pallas_call(
        body, name="fox_bwd", grid=(n_batch, 4, nq),
        in_specs=[pl.BlockSpec((1, tq, LANES), tile), pl.BlockSpec((1, s_len, LANES), lambda b, j, i: (b, 0, 4 + j)),
                  pl.BlockSpec((1, s_len, LANES), lambda b, j, i: (b, 0, 8 + j)),
                  pl.BlockSpec((1, nq, SM_ROWS, tq), lambda b, j, i: (b, 0, 0, 0)),
                  pl.BlockSpec((1, LANES), fixed), pl.BlockSpec((1, LANES), fixed), pl.BlockSpec((1, LANES), fixed),
                  pl.BlockSpec((1, tq, LANES), tile), pl.BlockSpec((1, tq, LANES), tile), pl.BlockSpec((1, tq, LANES), tile)],
        out_specs=[pl.BlockSpec((1, tq, LANES), tile), pl.BlockSpec((1, s_len, LANES), full), pl.BlockSpec((1, s_len, LANES), full),
                   pl.BlockSpec((1, 1, nq, 8, tq), lambda b, j, i: (b, j, 0, 0, 0)),
                   pl.BlockSpec((1, LANES), fixed), pl.BlockSpec((1, LANES), fixed), pl.BlockSpec((1, LANES), fixed)],
        out_shape=[wide, wide, wide, jax.ShapeDtypeStruct((n_batch, 4, nq, 8, tq), F32), gain, gain, gain],
        scratch_shapes=[pltpu.VMEM((2, s_len, LANES), BF16), pltpu.VMEM((2, s_len, LANES), BF16),
                        pltpu.VMEM((s_len, LANES), F32), pltpu.VMEM((s_len, LANES), F32), pltpu.VMEM((nq, 8, tq), F32)],
        compiler_params=_cparams(("arbitrary", "arbitrary", "arbitrary")),
    )(pf, pf, pf, cb, gq2, gk2, go2, o, lse, don)


def _shift_down(x, k):
    row = lax.broadcasted_iota(jnp.int32, x.shape, 0)
    return jnp.where(row >= k, pltpu.roll(x, k, 0), 0.0)


def _shift_up(x, k):
    n = x.shape[0]
    row = lax.broadcasted_iota(jnp.int32, x.shape, 0)
    return jnp.where(row < n - k, pltpu.roll(x, n - k, 0), 0.0)


def _conv_silu(x, w):
    y = w[3:4] * x + w[2:3] * _shift_down(x, 1) + w[1:2] * _shift_down(x, 2) + w[0:1] * _shift_down(x, 3)
    return y, y * _sigmoid(y)


def _gdn_pre(pg, conv_w):
    n_batch, s_len, width = pg.shape
    ncb = width // LANES

    def body(x_ref, w_ref, o_ref):
        cb = pl.program_id(1)
        _, s = _conv_silu(x_ref[0], w_ref[...])
        sn = s * lax.rsqrt(jnp.sum(s * s, axis=-1, keepdims=True) + EPS)
        o_ref[0] = jnp.where(cb < 2 * GDN_HEADS, sn, s)

    return pl.pallas_call(
        body, name="gdn_pre", grid=(n_batch, ncb),
        in_specs=[pl.BlockSpec((1, s_len, LANES), lambda b, c: (b, 0, c)), pl.BlockSpec((8, LANES), lambda b, c: (0, c))],
        out_specs=pl.BlockSpec((1, s_len, LANES), lambda b, c: (b, 0, c)),
        out_shape=jax.ShapeDtypeStruct(pg.shape, F32),
        compiler_params=_cparams(("parallel", "parallel")),
    )(pg, conv_w)


def _gdn_pre_bwd(pg, conv_w, dout):
    n_batch, s_len, width = pg.shape
    ncb = width // LANES

    def body(x_ref, w_ref, d_ref, dx_ref, dw_ref):
        cb = pl.program_id(0)
        b = pl.program_id(1)
        x = x_ref[0]
        w = w_ref[...]
        d = d_ref[0]
        y, s = _conv_silu(x, w)
        rr = lax.rsqrt(jnp.sum(s * s, axis=-1, keepdims=True) + EPS)
        sn = s * rr
        ds_n = rr * (d - sn * jnp.sum(d * sn, axis=-1, keepdims=True))
        ds = jnp.where(cb < 2 * GDN_HEADS, ds_n, d)
        sig = _sigmoid(y)
        dy = ds * (sig * (1.0 + y * (1.0 - sig)))
        dx = w[3:4] * dy + w[2:3] * _shift_up(dy, 1) + w[1:2] * _shift_up(dy, 2) + w[0:1] * _shift_up(dy, 3)
        dx_ref[0] = dx.astype(BF16)
        dw = [jnp.sum(dy * _shift_down(x, 3 - jj), axis=0, keepdims=True) if jj < 3 else jnp.sum(dy * x, axis=0, keepdims=True)
              for jj in range(CONV_WIDTH)]
        rows = lax.broadcasted_iota(jnp.int32, (8, LANES), 0)
        dwb = jnp.zeros((8, LANES), F32)
        for jj in range(CONV_WIDTH):
            dwb = dwb + jnp.where(rows == jj, dw[jj], 0.0)

        @pl.when(b == 0)
        def _():
            dw_ref[...] = jnp.zeros_like(dw_ref)

        dw_ref[...] += dwb

    blk = lambda c, b: (b, 0, c)
    return pl.pallas_call(
        body, name="gdn_pre_bwd", grid=(ncb, n_batch),
        in_specs=[pl.BlockSpec((1, s_len, LANES), blk), pl.BlockSpec((8, LANES), lambda c, b: (0, c)), pl.BlockSpec((1, s_len, LANES), blk)],
        out_specs=[pl.BlockSpec((1, s_len, LANES), blk), pl.BlockSpec((8, LANES), lambda c, b: (0, c))],
        out_shape=[jax.ShapeDtypeStruct(pg.shape, BF16), jax.ShapeDtypeStruct((8, width), F32)],
        compiler_params=_cparams(("parallel", "arbitrary")),
    )(pg, conv_w, dout)


def _gdn_gates(smc, smr, a_c, dt_c, a_r, dt_r, h):
    lane = lax.broadcasted_iota(jnp.int32, (1, LANES), 1)
    sub = lax.broadcasted_iota(jnp.int32, (SM_ROWS, 1), 0)
    beta_c = jnp.sum(jnp.where(lane == SM_B + h, _sigmoid(smc), 0.0), axis=1, keepdims=True)
    g_all_c = -jnp.exp(a_c) * _softplus(smc + dt_c)
    g_c = jnp.sum(jnp.where(lane == SM_A + h, g_all_c, 0.0), axis=1, keepdims=True)
    g_all_r = -jnp.exp(a_r) * _softplus(smr + dt_r)
    g_r = jnp.sum(jnp.where(sub == SM_A + h, g_all_r, 0.0), axis=0, keepdims=True)
    return beta_c, g_c, g_r


def _gdn_group(qkv, z, smc, smr, a_c, dt_c, a_r, dt_r, go, states):
    n_grp = len(qkv)
    c = qkv[0].shape[0]
    hd = GDN_HEAD_DIM
    pairs = [(g, h) for g in range(n_grp) for h in range(GDN_HEADS)]
    ii = lax.broadcasted_iota(jnp.int32, (c, c), 0)
    jj = lax.broadcasted_iota(jnp.int32, (c, c), 1)
    incl = ii >= jj
    col = lambda arr, base, h: arr[:, base + h * hd:base + (h + 1) * hd]

    qs, ks, kbs, vbs, decays, gcs, g_lasts, amats = [], [], [], [], [], [], [], []
    for g, h in pairs:
        beta_c, g_c, g_r = _gdn_gates(smc[g], smr[g], a_c, dt_c, a_r, dt_r, h)
        gc_c = jnp.sum(jnp.where(incl, g_r, 0.0), axis=1, keepdims=True)
        gc_r = jnp.sum(jnp.where(ii <= jj, g_c, 0.0), axis=0, keepdims=True)
        decay = jnp.where(incl, jnp.exp(jnp.where(incl, gc_c - gc_r, 0.0)), 0.0)
        k = col(qkv[g], GDN_WIDTH, h)
        kb = k * beta_c
        qs.append(col(qkv[g], 0, h) * (hd ** -0.5))
        ks.append(k)
        kbs.append(kb)
        vbs.append(col(qkv[g], 2 * GDN_WIDTH, h) * beta_c)
        decays.append(decay)
        gcs.append(gc_c)
        g_lasts.append(jnp.sum(g_c, axis=0, keepdims=True))
        amats.append(jnp.where(ii > jj, _mm_nt(kb, k) * decay, 0.0))
    ts = _unit_lower_inverses(amats)
    egcs = [jnp.exp(gc) for gc in gcs]
    us = [_mm_nn(t, vb) for t, vb in zip(ts, vbs)]
    ws = [_mm_nn(t, kb * e) for t, kb, e in zip(ts, kbs, egcs)]
    intras = [_mm_nt(q, k) * d for q, k, d in zip(qs, ks, decays)]
    qes = [q * e for q, e in zip(qs, egcs)]
    kds = [k * jnp.exp(gl - gc) for k, gl, gc in zip(ks, g_lasts, gcs)]
    sdecs = [jnp.exp(gl) for gl in g_lasts]

    outs = []
    for g in range(n_grp):
        idx = [g * GDN_HEADS + h for h in range(GDN_HEADS)]
        v_new = [us[i] - _mm_nn(ws[i], states[h]) for h, i in enumerate(idx)]
        o_state = [_mm_nn(qes[i], states[h]) for h, i in enumerate(idx)]
        o_intra = [_mm_nn(intras[i], v_new[h]) for h, i in enumerate(idx)]
        states = [states[h] * sdecs[i] + _mm_tn(kds[i], v_new[h]) for h, i in enumerate(idx)]
        outs.append([_rms(o_state[h] + o_intra[h], go) * (col(z[g], 0, h) * _sigmoid(col(z[g], 0, h))) for h in range(GDN_HEADS)])
    return outs, states


def _gdn_group_size(n_chunks):
    return GDN_GROUP if n_chunks % GDN_GROUP == 0 else 1


def _gdn_fwd(qkvn, z, smc, smr, a_c, dt_c, a_r, dt_r, go):
    n_batch, s_len, _ = qkvn.shape
    c = GDN_CHUNK
    n = s_len // c
    grp = _gdn_group_size(n)
    ng = n // grp
    gc = grp * c
    hd = GDN_HEAD_DIM

    def body(qkv_ref, z_ref, smc_ref, smr_ref, ac_ref, dc_ref, ar_ref, dr_ref, go_ref, og_ref, st_ref, s_ref):
        @pl.when(pl.program_id(1) == 0)
        def _():
            s_ref[...] = jnp.zeros_like(s_ref)

        states = [s_ref[h] for h in range(GDN_HEADS)]
        for h in range(GDN_HEADS):
            st_ref[0, 0, h] = states[h]
        rows = lambda k: slice(k * c, (k + 1) * c)
        outs, nxt = _gdn_group([qkv_ref[0, rows(k), :] for k in range(grp)], [z_ref[0, rows(k), :] for k in range(grp)],
                               [smc_ref[0, rows(k), :] for k in range(grp)], [smr_ref[k] for k in range(grp)],
                               ac_ref[...], dc_ref[...], ar_ref[...], dr_ref[...], go_ref[...], states)
        for k in range(grp):
            for h in range(GDN_HEADS):
                og_ref[0, rows(k), h * hd:(h + 1) * hd] = outs[k][h].astype(BF16)
        for h in range(GDN_HEADS):
            s_ref[h] = nxt[h]

    tok = lambda b, i: (b, i, 0)
    fixed = lambda b, i: (0, 0)
    return pl.pallas_call(
        body, name="gdn_fwd", grid=(n_batch, ng),
        in_specs=[pl.BlockSpec((1, gc, 3 * GDN_WIDTH), tok), pl.BlockSpec((1, gc, GDN_WIDTH), tok), pl.BlockSpec((1, gc, LANES), tok),
                  pl.BlockSpec((grp, SM_ROWS, c), lambda b, i: (b * ng + i, 0, 0)),
                  pl.BlockSpec((1, LANES), fixed), pl.BlockSpec((1, LANES), fixed), pl.BlockSpec((SM_ROWS, 1), fixed),
                  pl.BlockSpec((SM_ROWS, 1), fixed), pl.BlockSpec((1, LANES), fixed)],
        out_specs=[pl.BlockSpec((1, gc, GDN_WIDTH), tok), pl.BlockSpec((1, 1, GDN_HEADS, hd, hd), lambda b, i: (b, i, 0, 0, 0))],
        out_shape=[jax.ShapeDtypeStruct((n_batch, s_len, GDN_WIDTH), BF16), jax.ShapeDtypeStruct((n_batch, ng, GDN_HEADS, hd, hd), F32)],
        scratch_shapes=[pltpu.VMEM((GDN_HEADS, hd, hd), F32)],
        compiler_params=_cparams(("parallel", "arbitrary")),
    )(qkvn, z, smc, smr, a_c, dt_c, a_r, dt_r, go)


def _gdn_bwd(qkvn, z, smc, smr, a_c, dt_c, a_r, dt_r, go, states, dog):
    n_batch, s_len, _ = qkvn.shape
    c = GDN_CHUNK
    n = s_len // c
    grp = _gdn_group_size(n)
    ng = n // grp
    gc = grp * c
    hd = GDN_HEAD_DIM

    def body(qkv_ref, z_ref, smc_ref, smr_ref, ac_ref, dc_ref, ar_ref, dr_ref, go_ref, st_ref, dog_ref,
             dqkv_ref, dz_ref, dsmc_ref, dsmr_ref, dac_ref, ddc_ref, dar_ref, ddr_ref, dgo_ref, ds_ref):
        first = (pl.program_id(0) == 0) & (pl.program_id(1) == 0)

        @pl.when(pl.program_id(1) == 0)
        def _():
            ds_ref[...] = jnp.zeros_like(ds_ref)

        @pl.when(first)
        def _():
            for r in (dac_ref, ddc_ref, dar_ref, ddr_ref, dgo_ref):
                r[...] = jnp.zeros_like(r)

        rows = lambda k: slice(k * c, (k + 1) * c)
        states = [st_ref[0, 0, h] for h in range(GDN_HEADS)]
        prim = ([qkv_ref[0, rows(k), :] for k in range(grp)], [z_ref[0, rows(k), :] for k in range(grp)],
                [smc_ref[0, rows(k), :] for k in range(grp)], [smr_ref[k] for k in range(grp)],
                ac_ref[...], dc_ref[...], ar_ref[...], dr_ref[...], go_ref[...], states)
        _, vjp = jax.vjp(_gdn_group, *prim)
        cot = ([[dog_ref[0, rows(k), h * hd:(h + 1) * hd] for h in range(GDN_HEADS)] for k in range(grp)],
               [ds_ref[h] for h in range(GDN_HEADS)])
        dqkv, dz, dsmc, dsmr, dac, ddc, dar, ddr, dgo, dstates = vjp(cot)
        for k in range(grp):
            dqkv_ref[0, rows(k), :] = dqkv[k]
            dz_ref[0, rows(k), :] = dz[k].astype(BF16)
            dsmc_ref[0, rows(k), :] = dsmc[k]
            dsmr_ref[k] = dsmr[k]
        dac_ref[...] += dac
        ddc_ref[...] += ddc
        dar_ref[...] += dar
        ddr_ref[...] += ddr
        dgo_ref[...] += dgo
        for h in range(GDN_HEADS):
            ds_ref[h] = dstates[h]

    tok = lambda b, i: (b, ng - 1 - i, 0)
    fixed = lambda b, i: (0, 0)
    lane_vec = jax.ShapeDtypeStruct((1, LANES), F32)
    row_vec = jax.ShapeDtypeStruct((SM_ROWS, 1), F32)
    return pl.pallas_call(
        body, name="gdn_bwd", grid=(n_batch, ng),
        in_specs=[pl.BlockSpec((1, gc, 3 * GDN_WIDTH), tok), pl.BlockSpec((1, gc, GDN_WIDTH), tok), pl.BlockSpec((1, gc, LANES), tok),
                  pl.BlockSpec((grp, SM_ROWS, c), lambda b, i: (b * ng + ng - 1 - i, 0, 0)),
                  pl.BlockSpec((1, LANES), fixed), pl.BlockSpec((1, LANES), fixed), pl.BlockSpec((SM_ROWS, 1), fixed),
                  pl.BlockSpec((SM_ROWS, 1), fixed), pl.BlockSpec((1, LANES), fixed),
                  pl.BlockSpec((1, 1, GDN_HEADS, hd, hd), lambda b, i: (b, ng - 1 - i, 0, 0, 0)),
                  pl.BlockSpec((1, gc, GDN_WIDTH), lambda b, i: (b, ng - 1 - i, 1))],
        out_specs=[pl.BlockSpec((1, gc, 3 * GDN_WIDTH), tok), pl.BlockSpec((1, gc, GDN_WIDTH), tok), pl.BlockSpec((1, gc, LANES), tok),
                   pl.BlockSpec((grp, SM_ROWS, c), lambda b, i: (b * ng + ng - 1 - i, 0, 0)),
                   pl.BlockSpec((1, LANES), fixed), pl.BlockSpec((1, LANES), fixed), pl.BlockSpec((SM_ROWS, 1), fixed),
                   pl.BlockSpec((SM_ROWS, 1), fixed), pl.BlockSpec((1, LANES), fixed)],
        out_shape=[jax.ShapeDtypeStruct((n_batch, s_len, 3 * GDN_WIDTH), F32), jax.ShapeDtypeStruct((n_batch, s_len, GDN_WIDTH), BF16),
                   jax.ShapeDtypeStruct((n_batch, s_len, LANES), F32), jax.ShapeDtypeStruct((n_batch * n, SM_ROWS, c), F32),
                   lane_vec, lane_vec, row_vec, row_vec, lane_vec],
        scratch_shapes=[pltpu.VMEM((GDN_HEADS, hd, hd), F32)],
        compiler_params=_cparams(("arbitrary", "arbitrary")),
    )(qkvn, z, smc, smr, a_c, dt_c, a_r, dt_r, go, states, dog)


def _out_proj(x, oa, ob, w_out, g_x, w_cq, tm=256):
    t_len, d = x.shape
    tm = min(tm, t_len)

    def body(x_ref, oa_ref, ob_ref, wo_ref, g_ref, wq_ref, x1_ref, hq_ref, cq_ref):
        x1 = x_ref[...] + _dot(oa_ref[...], wo_ref[0:FOX_WIDTH, :]) + _dot(ob_ref[...], wo_ref[FOX_WIDTH:2 * FOX_WIDTH, :])
        x1_ref[...] = x1
        hq = _rms(x1, g_ref[...]).astype(BF16)
        hq_ref[...] = hq
        cq_ref[...] = _dot(hq, wq_ref[...])

    row = lambda i: (i, 0)
    fixed = lambda i: (0, 0)
    return pl.pallas_call(
        body, name="out_proj", grid=(t_len // tm,),
        in_specs=[pl.BlockSpec((tm, d), row), pl.BlockSpec((tm, FOX_WIDTH), row), pl.BlockSpec((tm, GDN_WIDTH), row),
                  pl.BlockSpec((d, d), fixed), pl.BlockSpec((1, d), fixed), pl.BlockSpec((d, XATTN_WIDTH), fixed)],
        out_specs=[pl.BlockSpec((tm, d), row), pl.BlockSpec((tm, d), row), pl.BlockSpec((tm, XATTN_WIDTH), row)],
        out_shape=[jax.ShapeDtypeStruct((t_len, d), F32), jax.ShapeDtypeStruct((t_len, d), BF16), jax.ShapeDtypeStruct((t_len, XATTN_WIDTH), F32)],
        compiler_params=_cparams(("parallel",)),
    )(x, oa, ob, w_out, g_x, w_cq)


def _out_proj_bwd(dx1, w_out, tm=512):
    t_len, d = dx1.shape
    tm = min(tm, t_len)

    def body(dx_ref, w_ref, o_ref):
        o_ref[...] = _dot(dx_ref[...], w_ref[...], NT)

    return pl.pallas_call(
        body, name="out_proj_bwd", grid=(t_len // tm,),
        in_specs=[pl.BlockSpec((tm, d), lambda i: (i, 0)), pl.BlockSpec((d, d), lambda i: (0, 0))],
        out_specs=pl.BlockSpec((tm, d), lambda i: (i, 0)),
        out_shape=jax.ShapeDtypeStruct((t_len, d), F32),
        compiler_params=_cparams(("parallel",)),
    )(dx1, w_out)


def _mem_kv(mem, g, w_ckv, tm=256):
    t_len, d = mem.shape
    tm = min(tm, t_len)

    def body(x_ref, g_ref, w_ref, h_ref, o_ref):
        h = _rms(x_ref[...], g_ref[...]).astype(BF16)
        h_ref[...] = h
        o_ref[...] = _dot(h, w_ref[...])

    row = lambda i: (i, 0)
    fixed = lambda i: (0, 0)
    return pl.pallas_call(
        body, name="mem_kv", grid=(t_len // tm,),
        in_specs=[pl.BlockSpec((tm, d), row), pl.BlockSpec((1, d), fixed), pl.BlockSpec((d, 2 * XATTN_WIDTH), fixed)],
        out_specs=[pl.BlockSpec((tm, d), row), pl.BlockSpec((tm, 2 * XATTN_WIDTH), row)],
        out_shape=[jax.ShapeDtypeStruct((t_len, d), BF16), jax.ShapeDtypeStruct((t_len, 2 * XATTN_WIDTH), F32)],
        compiler_params=_cparams(("parallel",)),
    )(mem, g, w_ckv)


def _mem_kv_bwd(dckv, mem, g, w_ckv, tm=256):
    t_len, d = mem.shape
    tm = min(tm, t_len)

    def body(d_ref, x_ref, g_ref, w_ref, dg_ref):
        @pl.when(pl.program_id(0) == 0)
        def _():
            dg_ref[...] = jnp.zeros_like(dg_ref)

        dh = _dot(d_ref[...], w_ref[...], NT)
        _, dg = _rms_bwd(x_ref[...], g_ref[...], dh)
        dg_ref[...] += dg

    row = lambda i: (i, 0)
    fixed = lambda i: (0, 0)
    return pl.pallas_call(
        body, name="mem_kv_bwd", grid=(t_len // tm,),
        in_specs=[pl.BlockSpec((tm, 2 * XATTN_WIDTH), row), pl.BlockSpec((tm, d), row), pl.BlockSpec((1, d), fixed),
                  pl.BlockSpec((d, 2 * XATTN_WIDTH), fixed)],
        out_specs=pl.BlockSpec((1, d), fixed),
        out_shape=jax.ShapeDtypeStruct((1, d), F32),
        compiler_params=_cparams(("arbitrary",)),
    )(dckv, mem, g, w_ckv)


def _xattn_probs(qn, kn):
    s = _dot(qn, kn, NT) * (XATTN_HEAD_DIM ** -0.5)
    p = jnp.exp(s - jnp.max(s, axis=-1, keepdims=True))
    return p / jnp.sum(p, axis=-1, keepdims=True)


def _xattn_fwd(cq, ckv, x1, gq, gk, w_co, g_mlp, n_batch, s_len, m_len, tq=512):
    d = x1.shape[1]
    tq = min(tq, s_len)
    nq = s_len // tq
    hd = XATTN_HEAD_DIM

    def body(cq_ref, kv_ref, x1_ref, gq_ref, gk_ref, wo_ref, gm_ref, co_ref, x2_ref, hf_ref):
        outs = []
        for h in range(XATTN_HEADS):
            qn = _rms(cq_ref[:, h * hd:(h + 1) * hd], gq_ref[...])
            kn = _rms(kv_ref[:, h * hd:(h + 1) * hd], gk_ref[...])
            p = _xattn_probs(qn, kn)
            outs.append(_dot(p, kv_ref[:, XATTN_WIDTH + h * hd:XATTN_WIDTH + (h + 1) * hd]).astype(BF16))
        x2 = x1_ref[...]
        for h in range(XATTN_HEADS):
            co_ref[:, h * hd:(h + 1) * hd] = outs[h]
            x2 = x2 + _dot(outs[h], wo_ref[h * hd:(h + 1) * hd, :])
        x2_ref[...] = x2
        hf_ref[...] = _rms(x2, gm_ref[...]).astype(BF16)

    row = lambda b, i: (b * nq + i, 0)
    fixed = lambda b, i: (0, 0)
    t_len = n_batch * s_len
    return pl.pallas_call(
        body, name="xattn_fwd", grid=(n_batch, nq),
        in_specs=[pl.BlockSpec((tq, XATTN_WIDTH), row), pl.BlockSpec((m_len, 2 * XATTN_WIDTH), lambda b, i: (b, 0)),
                  pl.BlockSpec((tq, d), row), pl.BlockSpec((1, hd), fixed), pl.BlockSpec((1, hd), fixed),
                  pl.BlockSpec((XATTN_WIDTH, d), fixed), pl.BlockSpec((1, d), fixed)],
        out_specs=[pl.BlockSpec((tq, XATTN_WIDTH), row), pl.BlockSpec((tq, d), row), pl.BlockSpec((tq, d), row)],
        out_shape=[jax.ShapeDtypeStruct((t_len, XATTN_WIDTH), BF16), jax.ShapeDtypeStruct((t_len, d), F32),
                   jax.ShapeDtypeStruct((t_len, d), BF16)],
        compiler_params=_cparams(("parallel", "parallel")),
    )(cq, ckv, x1, gq, gk, w_co, g_mlp)


def _xattn_bwd(dx2, cq, ckv, x1, gq, gk, w_co, g_x, w_cq, n_batch, s_len, m_len, tq=512):
    d = x1.shape[1]
    tq = min(tq, s_len)
    nq = s_len // tq
    hd = XATTN_HEAD_DIM
    scale = XATTN_HEAD_DIM ** -0.5

    def body(dx2_ref, cq_ref, kv_ref, x1_ref, gq_ref, gk_ref, wo_ref, gx_ref, wq_ref,
             dx1_ref, dcq_ref, dkv_ref, dgq_ref, dgk_ref, dgx_ref, dk_acc, dv_acc):
        b = pl.program_id(0)
        i = pl.program_id(1)

        @pl.when((b == 0) & (i == 0))
        def _():
            dgq_ref[...] = jnp.zeros_like(dgq_ref)
            dgk_ref[...] = jnp.zeros_like(dgk_ref)
            dgx_ref[...] = jnp.zeros_like(dgx_ref)

        @pl.when(i == 0)
        def _():
            dk_acc[...] = jnp.zeros_like(dk_acc)
            dv_acc[...] = jnp.zeros_like(dv_acc)

        dx2 = dx2_ref[...]
        dhq = jnp.zeros((tq, d), F32)
        for h in range(XATTN_HEADS):
            sl = slice(h * hd, (h + 1) * hd)
            q = cq_ref[:, sl]
            qn = _rms(q, gq_ref[...])
            kn = _rms(kv_ref[:, sl], gk_ref[...])
            v = kv_ref[:, XATTN_WIDTH + h * hd:XATTN_WIDTH + (h + 1) * hd]
            p = _xattn_probs(qn, kn)
            dco = _dot(dx2, wo_ref[sl, :], NT)
            dv_acc[:, sl] += _dot(p, dco, TN)
            dp = _dot(dco, v, NT)
            ds = p * (dp - jnp.sum(dp * p, axis=-1, keepdims=True))
            dqn = _dot(ds, kn) * scale
            dk_acc[:, sl] += _dot(ds, qn, TN) * scale
            dq, dgq = _rms_bwd(q, gq_ref[...], dqn)
            dgq_ref[...] += dgq
            dqb = dq.astype(BF16)
            dcq_ref[:, sl] = dqb
            dhq = dhq + _dot(dqb, wq_ref[:, sl], NT)
        dxn, dgx = _rms_bwd(x1_ref[...], gx_ref[...], dhq)
        dgx_ref[...] += dgx
        dx1_ref[...] = dx2 + dxn

        @pl.when(i == nq - 1)
        def _():
            for h in range(XATTN_HEADS):
                sl = slice(h * hd, (h + 1) * hd)
                dk, dgk = _rms_bwd(kv_ref[:, sl], gk_ref[...], dk_acc[:, sl])
                dgk_ref[...] += dgk
                dkv_ref[:, sl] = dk.astype(BF16)
                dkv_ref[:, XATTN_WIDTH + h * hd:XATTN_WIDTH + (h + 1) * hd] = dv_acc[:, sl].astype(BF16)

    row = lambda b, i: (b * nq + i, 0)
    fixed = lambda b, i: (0, 0)
    t_len = n_batch * s_len
    return pl.pallas_call(
        body, name="xattn_bwd", grid=(n_batch, nq),
        in_specs=[pl.BlockSpec((tq, d), row), pl.BlockSpec((tq, XATTN_WIDTH), row), pl.BlockSpec((m_len, 2 * XATTN_WIDTH), lambda b, i: (b, 0)),
                  pl.BlockSpec((tq, d), row), pl.BlockSpec((1, hd), fixed), pl.BlockSpec((1, hd), fixed),
                  pl.BlockSpec((XATTN_WIDTH, d), fixed), pl.BlockSpec((1, d), fixed), pl.BlockSpec((d, XATTN_WIDTH), fixed)],
        out_specs=[pl.BlockSpec((tq, d), row), pl.BlockSpec((tq, XATTN_WIDTH), row), pl.BlockSpec((m_len, 2 * XATTN_WIDTH), lambda b, i: (b, 0)),
                   pl.BlockSpec((1, hd), fixed), pl.BlockSpec((1, hd), fixed), pl.BlockSpec((1, d), fixed)],
        out_shape=[jax.ShapeDtypeStruct((t_len, d), F32), jax.ShapeDtypeStruct((t_len, XATTN_WIDTH), BF16),
                   jax.ShapeDtypeStruct((n_batch * m_len, 2 * XATTN_WIDTH), BF16),
                   jax.ShapeDtypeStruct((1, hd), F32), jax.ShapeDtypeStruct((1, hd), F32), jax.ShapeDtypeStruct((1, d), F32)],
        scratch_shapes=[pltpu.VMEM((m_len, XATTN_WIDTH), F32), pltpu.VMEM((m_len, XATTN_WIDTH), F32)],
        compiler_params=_cparams(("arbitrary", "arbitrary")),
    )(dx2, cq, ckv, x1, gq, gk, w_co, g_x, w_cq)


def _resident(shape):
    return pl.BlockSpec(shape, lambda *_: (0,) * len(shape), pipeline_mode=pl.Buffered(1))


def _mlp_fwd(hf, x2, target, w1, w2, tm=256, tf=1024):
    t_len, d = x2.shape
    f = w1.shape[1]
    tm, tf = min(tm, t_len), min(tf, f)

    def body(hf_ref, x2_ref, tg_ref, w1_ref, w2_ref, u_ref, a_ref, dy_ref, ls_ref):
        hf_t = hf_ref[...]
        y = x2_ref[...]
        for k in range(f // tf):
            cols = slice(k * tf, (k + 1) * tf)
            u = _dot(hf_t, w1_ref[:, cols])
            u_ref[:, cols] = u
            r = jnp.maximum(u, 0.0)
            a = (r * r).astype(BF16)
            a_ref[:, cols] = a
            y = y + _dot(a, w2_ref[cols, :])
        err = y - tg_ref[...]
        dy_ref[...] = err * (1.0 / d)
        ls_ref[...] = jnp.broadcast_to(jnp.sum(jnp.sum(err * err, axis=-1, keepdims=True) * (1.0 / d), axis=0, keepdims=True), ls_ref.shape)

    row = lambda i: (i, 0)
    return pl.pallas_call(
        body, name="mlp_fwd", grid=(t_len // tm,),
        in_specs=[pl.BlockSpec((tm, d), row), pl.BlockSpec((tm, d), row), pl.BlockSpec((tm, d), row), _resident((d, f)), _resident((f, d))],
        out_specs=[pl.BlockSpec((tm, f), row), pl.BlockSpec((tm, f), row), pl.BlockSpec((tm, d), row),
                   pl.BlockSpec((1, 8, LANES), lambda i: (i, 0, 0))],
        out_shape=[jax.ShapeDtypeStruct((t_len, f), F32), jax.ShapeDtypeStruct((t_len, f), BF16), jax.ShapeDtypeStruct((t_len, d), F32),
                   jax.ShapeDtypeStruct((t_len // tm, 8, LANES), F32)],
        compiler_params=_cparams(("parallel",)),
    )(hf, x2, target, w1, w2)


def _mlp_bwd(dy, u, x2, g, w1, w2, tm=256, tf=1024):
    t_len, d = x2.shape
    f = w1.shape[1]
    tm, tf = min(tm, t_len), min(tf, f)

    def body(dy_ref, u_ref, x2_ref, g_ref, w1_ref, w2_ref, du_ref, dx2_ref, dg_ref):
        @pl.when(pl.program_id(0) == 0)
        def _():
            dg_ref[...] = jnp.zeros_like(dg_ref)

        dy_t = dy_ref[...]
        dyb = dy_t.astype(BF16)
        dhf = jnp.zeros((tm, d), F32)
        for k in range(f // tf):
            cols = slice(k * tf, (k + 1) * tf)
            da = _dot(dyb, w2_ref[cols, :], NT)
            du = (da * (2.0 * jnp.maximum(u_ref[:, cols], 0.0))).astype(BF16)
            du_ref[:, cols] = du
            dhf = dhf + _dot(du, w1_ref[:, cols], NT)
        dxn, dg = _rms_bwd(x2_ref[...], g_ref[...], dhf)
        dx2_ref[...] = dy_t + dxn
        dg_ref[...] += dg

    row = lambda i: (i, 0)
    fixed = lambda i: (0, 0)
    return pl.pallas_call(
        body, name="mlp_bwd", grid=(t_len // tm,),
        in_specs=[pl.BlockSpec((tm, d), row), pl.BlockSpec((tm, f), row), pl.BlockSpec((tm, d), row), pl.BlockSpec((1, d), fixed),
                  _resident((d, f)), _resident((f, d))],
        out_specs=[pl.BlockSpec((tm, f), row), pl.BlockSpec((tm, d), row), pl.BlockSpec((1, d), fixed)],
        out_shape=[jax.ShapeDtypeStruct((t_len, f), BF16), jax.ShapeDtypeStruct((t_len, d), F32), jax.ShapeDtypeStruct((1, d), F32)],
        compiler_params=_cparams(("arbitrary",)),
    )(dy, u, x2, g, w1, w2)


def _pad_lanes(v, offset=0, width=LANES):
    return jnp.zeros((1, width), F32).at[:, offset:offset + v.shape[1]].set(v)


def _col(v, offset=0, rows=SM_ROWS):
    return jnp.zeros((rows, 1), F32).at[offset:offset + v.shape[1], 0].set(v[0])


def _pack_small(g_mix, dgq, dgk, dbias, dgo, dac, dar, ddc, ddr, g_gdn_o, g_nx, g_mem, g_xq, g_xk, g_mlp, loss_tiles):
    def body(mix_ref, q_ref, k_ref, b_ref, o_ref, ac_ref, ar_ref, dc_ref, dr_ref, go_ref, nx_ref, mem_ref, xq_ref, xk_ref,
             mlp_ref, lt_ref, out_ref):
        lane = lax.broadcasted_iota(jnp.int32, (1, LANES), 1)
        diag = lax.broadcasted_iota(jnp.int32, (SM_ROWS, LANES), 0) == lax.broadcasted_iota(jnp.int32, (SM_ROWS, LANES), 1)

        def rolled(v, shift):
            return pltpu.roll(jnp.broadcast_to(v, (8, LANES)), shift, 1)[0:1, :]

        def rows_to_lanes(col):
            return jnp.sum(jnp.where(diag, col, 0.0), axis=0, keepdims=True)

        def put(row, v, n):
            out_ref[row:row + 1, 0:LANES] = jnp.where(lane < n, v, 0.0)

        out_ref[...] = jnp.zeros_like(out_ref)
        out_ref[0:1, :] = mix_ref[...]
        for row, ref in ((1, q_ref), (2, k_ref), (4, o_ref)):
            put(row, ref[...] + rolled(ref[...], FOX_HEAD_DIM), FOX_HEAD_DIM)
        put(3, rows_to_lanes(b_ref[...]), FOX_HEADS)
        for row, lane_ref, row_ref in ((5, ac_ref, ar_ref), (6, dc_ref, dr_ref)):
            put(row, rolled(lane_ref[...] + rows_to_lanes(row_ref[...]), LANES - SM_A), GDN_HEADS)
        put(7, go_ref[...], LANES)
        out_ref[8:9, :] = nx_ref[...]
        out_ref[9:10, :] = mem_ref[...]
        put(10, xq_ref[...], LANES)
        put(11, xk_ref[...], LANES)
        out_ref[12:13, :] = mlp_ref[...]
        put(LOSS_ROW, 0.5 * jnp.sum(lt_ref[...], axis=0)[0:1, :], 1)

    args = (g_mix, dgq, dgk, dbias, dgo, dac, dar, ddc, ddr, g_gdn_o, g_nx, g_mem, g_xq, g_xk, g_mlp, loss_tiles)
    return pl.pallas_call(body, name="pack_small", out_shape=jax.ShapeDtypeStruct((PACK_ROWS, D_MODEL), F32))(*args)


LATE_WEIGHTS = ("w_out", "w_cq", "w_ckv", "w_co", "w_mlp1", "w_mlp2")
GRAD_GROUPS = (("w_mlp2", "w_mlp1"), ("w_co", "w_cq", "w_ckv", "w_out"), ("w_in", "gdn_conv_w"))


def _local_step(x, mem, target, norm_mix_g, w_in, fox_qnorm_g, fox_knorm_g, fox_f_bias, fox_onorm_g, gdn_conv_w, gdn_A_log,
                gdn_dt_bias, gdn_onorm_g, norm_xattn_g, mem_norm_g, xattn_qnorm_g, xattn_knorm_g, norm_mlp_g,
                late_weights, grads_ready=None, first_token=0.0):
    if grads_ready is None:
        grads_ready = lambda group: 0.0
    n_batch, s_len, d = x.shape
    m_len = mem.shape[1]
    t_len = n_batch * s_len
    tq = min(FOX_BLOCK, s_len)
    nq = s_len // tq
    n_chunks = s_len // GDN_CHUNK
    x2d = x.reshape(t_len, d)

    wp = jnp.concatenate([w_in[0:1536], w_in[1544:3080], w_in[3088:3600], w_in[1536:1544], w_in[3080:3088],
                          jnp.zeros((P_DIM - 3600, d), BF16)], axis=0)
    wst = jnp.concatenate([w_in[1536:1544], w_in[3080:3088]], axis=0)
    conv_w = jnp.concatenate([gdn_conv_w, jnp.zeros((8 - CONV_WIDTH, gdn_conv_w.shape[1]), F32)], axis=0)
    bias_col = _col(fox_f_bias, SM_F)
    gq2, gk2, go2 = (jnp.tile(g, (1, 2)) for g in (fox_qnorm_g, fox_knorm_g, fox_onorm_g))
    a_c, dt_c = _pad_lanes(gdn_A_log, SM_A), _pad_lanes(gdn_dt_bias, SM_A)
    a_r, dt_r = _col(gdn_A_log, SM_A), _col(gdn_dt_bias, SM_A)

    h1, pfox, pgdn, pz, sm, smt = _in_proj(x2d, norm_mix_g + first_token, wp, wst)
    c_rows = _fox_cum(smt, bias_col, n_batch, s_len)
    cb = c_rows.reshape(SM_ROWS, n_batch, nq, tq).transpose(1, 2, 0, 3)
    pf3 = pfox.reshape(n_batch, s_len, 1536)
    o_fox, oa, lse = _fox_fwd(pf3, cb, gq2, gk2, go2, tq)
    pg3 = pgdn.reshape(n_batch, s_len, 1536)
    qkvn = _gdn_pre(pg3, conv_w)
    z3 = pz.reshape(n_batch, s_len, GDN_WIDTH)
    smc = sm.reshape(n_batch, s_len, LANES)
    smr = smt.reshape(SM_ROWS, n_batch * n_chunks, GDN_CHUNK).transpose(1, 0, 2)
    ob, states = _gdn_fwd(qkvn, z3, smc, smr, a_c, dt_c, a_r, dt_r, gdn_onorm_g)
    oa2, ob2 = oa.reshape(t_len, FOX_WIDTH), ob.reshape(t_len, GDN_WIDTH)
    late = late_weights(ob2)
    w_out, w_cq, w_ckv, w_co, w_mlp1, w_mlp2 = (late[k] for k in LATE_WEIGHTS)
    x1, hq, cq = _out_proj(x2d, oa2, ob2, w_out, norm_xattn_g, w_cq)
    mem2d = mem.reshape(n_batch * m_len, d)
    hm, ckv = _mem_kv(mem2d, mem_norm_g, w_ckv)
    co, x2, hf = _xattn_fwd(cq, ckv, x1, xattn_qnorm_g, xattn_knorm_g, w_co, norm_mlp_g, n_batch, s_len, m_len)
    u, a_act, dy, loss_tiles = _mlp_fwd(hf, x2, target.reshape(t_len, d), w_mlp1, w_mlp2)

    grads = {}
    du, dx2, grads["norm_mlp_g"] = _mlp_bwd(dy, u, x2, norm_mlp_g, w_mlp1, w_mlp2)
    grads["w_mlp2"] = _wgrad(a_act, dy, "wgrad_mlp2")
    grads["w_mlp1"] = _wgrad(hf, du, "wgrad_mlp1", column_blocks=D_FF // N_DEV)
    token = grads_ready({k: grads[k] for k in GRAD_GROUPS[0]})
    grads["w_co"] = _wgrad(co, dx2, "wgrad_co", column_blocks=D_MODEL // N_DEV)
    dx1, dcq, dckv, grads["xattn_qnorm_g"], grads["xattn_knorm_g"], grads["norm_xattn_g"] = _xattn_bwd(
        dx2, cq, ckv, x1, xattn_qnorm_g + token, xattn_knorm_g, w_co, norm_xattn_g, w_cq, n_batch, s_len, m_len)
    grads["w_cq"] = _wgrad(hq, dcq, "wgrad_cq")
    grads["w_ckv"] = _wgrad(hm, dckv, "wgrad_ckv")
    grads["mem_norm_g"] = _mem_kv_bwd(dckv, mem2d, mem_norm_g, w_ckv)
    grads["w_out"] = _wgrad(jnp.concatenate([oa2, ob2], axis=1), dx1, "wgrad_out")
    token = grads_ready({k: grads[k] for k in GRAD_GROUPS[1]})
    dcat = _out_proj_bwd(dx1, w_out)
    dcat3 = dcat.reshape(n_batch, s_len, d)

    dqkvn, dz, dsmc, dsmr, dac, ddc, dar, ddr, grads["gdn_onorm_g"] = _gdn_bwd(
        qkvn, z3, smc, smr, a_c, dt_c, a_r, dt_r, gdn_onorm_g + token, states, dcat3)
    dpg, dconv = _gdn_pre_bwd(pg3, conv_w, dqkvn)
    grads["gdn_conv_w"] = dconv[0:CONV_WIDTH]

    dq, dk, dv, dcb, dgq, dgk, dgo = _fox_bwd(pf3, cb, gq2, gk2, go2, o_fox, lse, dcat3[:, :, 0:FOX_WIDTH], tq)
    dc8 = dcb[:, :, :, 0:2, :].transpose(1, 3, 0, 2, 4).reshape(FOX_HEADS, t_len)
    dc_rows = jnp.concatenate([dc8, jnp.zeros((SM_ROWS - FOX_HEADS, t_len), F32)], axis=0)
    dl_rows, dbias = _fox_cum_bwd(dc_rows, smt, bias_col, n_batch, s_len)
    dsm_rows = jnp.concatenate([dl_rows[0:SM_B], dsmr.transpose(1, 0, 2).reshape(SM_ROWS, t_len)[SM_B:SM_ROWS]], axis=0)

    dproj = jnp.concatenate([dq.reshape(t_len, FOX_WIDTH), dk.reshape(t_len, FOX_WIDTH), dv.reshape(t_len, FOX_WIDTH),
                             dpg.reshape(t_len, 1536), dz.reshape(t_len, GDN_WIDTH), dsmc.reshape(t_len, LANES).astype(BF16)], axis=1)
    dwp = _wgrad(dproj, h1, "wgrad_in", bk=P_DIM, bn=512)
    dwst = _rows_matmul(dsm_rows, h1, "wgrad_in_rows")
    dw_small = dwp[P_SMALL:P_SMALL + SM_ROWS] + dwst
    grads["w_in"] = jnp.concatenate([dwp[0:1536], dw_small[0:8], dwp[1536:3072], dw_small[8:16], dwp[3072:3584]], axis=0)
    token = grads_ready({k: grads[k] for k in GRAD_GROUPS[2]})
    grad_x, grads["norm_mix_g"] = _in_proj_bwd(dproj, dsm_rows, x2d, norm_mix_g + token, wp, wst, dx1)
    packed = _pack_small(grads["norm_mix_g"], dgq, dgk, dbias, dgo, dac, dar, ddc, ddr, grads["gdn_onorm_g"], grads["norm_xattn_g"],
                         grads["mem_norm_g"], grads["xattn_qnorm_g"], grads["xattn_knorm_g"], grads["norm_mlp_g"], loss_tiles)
    return packed, grad_x.reshape(n_batch, s_len, d), {k: grads[k] for k in SHARDED}


MESH_ID = pl.DeviceIdType.MESH
ANY_SPEC = pl.BlockSpec(memory_space=pl.ANY)


def _place():
    x, y, c = lax.axis_index("x"), lax.axis_index("y"), lax.axis_index("c")
    return x, y, c, [(1 - x, y), (x, 1 - y), (1 - x, 1 - y)]


def _all_gather_body(n, ins, outs, send_sems, recv_sems, local_sems):
    x, y, c, chips = _place()
    me, sibling = (x, y, c), (x, y, 1 - c)

    def copy(a, k, block, to, src=None):
        dst = outs[a].at[4 * block[0] + 2 * block[1] + block[2]]
        return pltpu.make_async_remote_copy(src_ref=dst if src is None else src, dst_ref=dst, send_sem=send_sems.at[a, k],
                                            recv_sem=recv_sems.at[a, k], device_id=to, device_id_type=MESH_ID)

    mine = [] if local_sems is None else [pltpu.make_async_copy(ins[a], outs[a].at[4 * x + 2 * y + c], local_sems.at[a]) for a in range(n)]
    for cp in mine:
        cp.start()
    first = []
    for a in range(n):
        first.append(copy(a, 0, me, sibling, src=ins[a]))
        first += [copy(a, 1 + j, me, (*chip, c), src=ins[a]) for j, chip in enumerate(chips)]
    for cp in first:
        cp.start()
    passed = []
    for j, chip in enumerate(chips):
        for a in range(n):
            copy(a, 1 + j, (*chip, c), me).wait_recv()
            fwd = copy(a, 4 + j, (*chip, c), sibling)
            fwd.start()
            passed.append(fwd)
    for a in range(n):
        copy(a, 0, sibling, me).wait_recv()
        for j, chip in enumerate(chips):
            copy(a, 4 + j, (*chip, 1 - c), me).wait_recv()
    for cp in first + passed:
        cp.wait_send()
    for cp in mine:
        cp.wait()


def _all_gather_hbm(arrs, name):
    n = len(arrs)
    me = 4 * lax.axis_index("x") + 2 * lax.axis_index("y") + lax.axis_index("c")

    def body(*refs):
        _all_gather_body(n, refs[:n], refs[n:2 * n], refs[2 * n], refs[2 * n + 1], None)

    got = pl.pallas_call(
        body, name=name, in_specs=[ANY_SPEC] * n, out_specs=[ANY_SPEC] * n,
        out_shape=[jax.ShapeDtypeStruct((N_DEV,) + a.shape, a.dtype) for a in arrs],
        scratch_shapes=[pltpu.SemaphoreType.DMA((n, 7)), pltpu.SemaphoreType.DMA((n, 7))],
    )(*arrs)
    return [lax.dynamic_update_slice(g, a[None], (me,) + (0,) * a.ndim) for g, a in zip(got, arrs)]


def _pair_exchange(arrs, name):
    n = len(arrs)

    def body(*refs):
        ins, outs = refs[:n], refs[n:2 * n]
        send_sems, recv_sems = refs[2 * n:]
        x, y, c, _ = _place()
        copies = []
        for a in range(n):
            for chip in range(4):
                copies.append(pltpu.make_async_remote_copy(
                    src_ref=ins[a].at[2 * chip + (1 - c)], dst_ref=outs[a].at[chip], send_sem=send_sems.at[a, chip],
                    recv_sem=recv_sems.at[a, chip], device_id=(x, y, 1 - c), device_id_type=MESH_ID))
        for cp in copies:
            cp.start()
        for cp in copies:
            cp.wait()

    return pl.pallas_call(
        body, name=name, in_specs=[ANY_SPEC] * n, out_specs=[ANY_SPEC] * n,
        out_shape=[jax.ShapeDtypeStruct((4,) + a.shape[1:], a.dtype) for a in arrs],
        scratch_shapes=[pltpu.SemaphoreType.DMA((n, 4)), pltpu.SemaphoreType.DMA((n, 4))],
    )(*arrs)


HBM_SPEC = pl.BlockSpec(memory_space=pltpu.HBM)
SEM_SPEC = pl.BlockSpec(memory_space=pltpu.SEMAPHORE)
DATAFLOW = pltpu.SideEffectType.DATAFLOW_SIDE_EFFECTING


def _in_hbm(arrs):
    return [pltpu.with_memory_space_constraint(a, pltpu.HBM) for a in arrs]


def _copies_start(name, srcs, lands, make_copies):
    n = len(srcs)
    n_copies = len(make_copies(srcs, lands, None, None)[0])

    def body(*refs):
        send_sems, recv_sems = refs[2 * n], refs[2 * n + 1]
        for row in make_copies(refs[:n], refs[n:2 * n], send_sems, recv_sems):
            for cp in row:
                cp.start()
        refs[-1][...] = jnp.zeros_like(refs[-1])

    sems = pltpu.SemaphoreType.DMA((n * n_copies,))
    thru = [pltpu.HBM(a.shape, a.dtype) for a in list(srcs) + list(lands)]
    res = pl.pallas_call(
        body, name=name, in_specs=[HBM_SPEC] * (2 * n),
        out_specs=(SEM_SPEC, SEM_SPEC, *[HBM_SPEC] * (2 * n), pl.BlockSpec(memory_space=pltpu.VMEM)),
        out_shape=(sems, sems, *thru, jax.ShapeDtypeStruct((8, LANES), F32)),
        input_output_aliases={i: 2 + i for i in range(2 * n)},
        compiler_params=pltpu.CompilerParams(has_side_effects=DATAFLOW),
    )(*_in_hbm(list(srcs) + list(lands)))
    return res[0], res[1], list(res[2:2 + n]), list(res[2 + n:2 + 2 * n]), res[-1]


def _copies_wait(name, send_sems, recv_sems, srcs, lands, after, make_copies):
    n = len(srcs)

    def body(*refs):
        for row in make_copies(refs[:n], refs[n:2 * n], refs[2 * n], refs[2 * n + 1]):
            for cp in row:
                cp.wait_send()
                cp.wait_recv()

    res = pl.pallas_call(
        body, name=name, in_specs=[HBM_SPEC] * (2 * n) + [SEM_SPEC, SEM_SPEC, ANY_SPEC],
        out_specs=tuple([HBM_SPEC] * (2 * n)),
        out_shape=tuple(pltpu.HBM(a.shape, a.dtype) for a in list(srcs) + list(lands)),
        input_output_aliases={i: i for i in range(2 * n)},
        compiler_params=pltpu.CompilerParams(has_side_effects=DATAFLOW),
    )(*srcs, *lands, send_sems, recv_sems, after)
    return list(res[:n]), list(res[n:])


def _gather_copies(srcs, lands, send_sems, recv_sems):
    if send_sems is None:
        return [[None] * 7]
    x, y, c, _ = _place()
    rows = []
    for a in range(len(srcs)):
        row = []
        for k in range(7):
            r = k + 1
            to = (1 - x if r & 4 else x, 1 - y if r & 2 else y, 1 - c if r & 1 else c)
            row.append(pltpu.make_async_remote_copy(
                src_ref=srcs[a], dst_ref=lands[a].at[4 * x + 2 * y + c], send_sem=send_sems.at[7 * a + k], recv_sem=recv_sems.at[7 * a + k],
                device_id=to, device_id_type=MESH_ID))
        rows.append(row)
    return rows


def _scatter_copies(srcs, lands, send_sems, recv_sems):
    if send_sems is None:
        return [[None] * 7]
    x, y, c, _ = _place()
    rows = []
    for a in range(len(srcs)):
        row = []
        for k in range(7):
            r = k + 1
            to = (1 - x if r & 4 else x, 1 - y if r & 2 else y, 1 - c if r & 1 else c)
            row.append(pltpu.make_async_remote_copy(
                src_ref=srcs[a].at[4 * to[0] + 2 * to[1] + to[2]], dst_ref=lands[a].at[k], send_sem=send_sems.at[7 * a + k],
                recv_sem=recv_sems.at[7 * a + k], device_id=to, device_id_type=MESH_ID))
        rows.append(row)
    return rows


def _chip_copies(srcs, lands, send_sems, recv_sems):
    if send_sems is None:
        return [[None] * 3]
    x, y, c, chips = _place()
    return [[pltpu.make_async_remote_copy(
        src_ref=srcs[a].at[2 * chip[0] + chip[1]], dst_ref=lands[a].at[j], send_sem=send_sems.at[3 * a + j], recv_sem=recv_sems.at[3 * a + j],
        device_id=(*chip, c), device_id_type=MESH_ID) for j, chip in enumerate(chips)] for a in range(len(srcs))]


def _all_gather_vmem(block, name):
    def body(in_ref, out_ref, send_sems, recv_sems, local_sems):
        _all_gather_body(1, [in_ref], [out_ref], send_sems, recv_sems, local_sems)

    vmem = pl.BlockSpec(memory_space=pltpu.VMEM)
    return pl.pallas_call(
        body, name=name, in_specs=[vmem], out_specs=vmem,
        out_shape=jax.ShapeDtypeStruct((N_DEV,) + block.shape, block.dtype),
        scratch_shapes=[pltpu.SemaphoreType.DMA((1, 7)), pltpu.SemaphoreType.DMA((1, 7)), pltpu.SemaphoreType.DMA((1,))],
    )(block)


def _tile(rows, cols):
    if rows <= 256:
        return rows, cols
    tr = 256 if cols <= 512 else 128
    if rows % tr == 0:
        return tr, cols
    return rows, 256


def _pair_sum(core, own, got, name):
    _, rows, cols = own.shape
    tr, tc = _tile(rows, cols)

    def body(c_ref, own_ref, got_ref, o_ref):
        o_ref[0] = own_ref[0] + got_ref[0]

    return pl.pallas_call(
        body, name=name,
        grid_spec=pltpu.PrefetchScalarGridSpec(
            num_scalar_prefetch=1, grid=(4, rows // tr, cols // tc),
            in_specs=[pl.BlockSpec((1, tr, tc), lambda k, i, j, c: (2 * k + c[0], i, j)),
                      pl.BlockSpec((1, tr, tc), lambda k, i, j, c: (k, i, j))],
            out_specs=pl.BlockSpec((1, tr, tc), lambda k, i, j, c: (k, i, j))),
        out_shape=jax.ShapeDtypeStruct((4, rows, cols), F32),
        compiler_params=_cparams(("parallel", "parallel", "parallel")),
    )(core, own, got)


def _adamw(w, g, m, v):
    m_new = ADAM_B1 * m + (1.0 - ADAM_B1) * g
    v_new = ADAM_B2 * v + (1.0 - ADAM_B2) * (g * g)
    m_hat = m_new / (1.0 - ADAM_B1 ** ADAM_STEP)
    v_hat = v_new / (1.0 - ADAM_B2 ** ADAM_STEP)
    delta = -ADAM_LR * (m_hat / (jnp.sqrt(v_hat) + ADAM_EPS) + ADAM_WD * w)
    return delta, m_new, v_new


def _sum_adam(chip, sums, parts, w, m, v, name):
    n_parts, rows, cols = parts.shape
    tr, tc = _tile(rows, cols)

    def body(chip_ref, own_ref, p_ref, w_ref, m_ref, v_ref, g_ref, d_ref, mo_ref, vo_ref):
        g = own_ref[0]
        for k in range(n_parts):
            g = g + p_ref[k]
        g_ref[...] = g
        d_ref[...], mo_ref[...], vo_ref[...] = _adamw(w_ref[...], g, m_ref[...], v_ref[...])

    tile = pl.BlockSpec((tr, tc), lambda i, j, ch: (i, j))
    out = jax.ShapeDtypeStruct((rows, cols), F32)
    return pl.pallas_call(
        body, name=name,
        grid_spec=pltpu.PrefetchScalarGridSpec(
            num_scalar_prefetch=1, grid=(rows // tr, cols // tc),
            in_specs=[pl.BlockSpec((1, tr, tc), lambda i, j, ch: (ch[0], i, j)),
                      pl.BlockSpec((n_parts, tr, tc), lambda i, j, ch: (0, i, j)), tile, tile, tile],
            out_specs=[tile, tile, tile, tile]),
        out_shape=[out, out, out, out],
        compiler_params=_cparams(("parallel", "parallel")),
    )(chip, sums, parts, w, m, v)


SHARDED = ("w_in", "gdn_conv_w", "w_out", "w_cq", "w_ckv", "w_co", "w_mlp1", "w_mlp2")
TRANSPOSED = ("w_in",)
COLUMN_SHARDED = ("gdn_conv_w", "w_co", "w_mlp1")
REPLICATED = ("norm_mix_g", "fox_qnorm_g", "fox_knorm_g", "fox_f_bias", "fox_onorm_g", "gdn_A_log", "gdn_dt_bias", "gdn_onorm_g",
              "norm_xattn_g", "mem_norm_g", "xattn_qnorm_g", "xattn_knorm_g", "norm_mlp_g")
WEIGHTS = ("norm_mix_g", "w_in", "fox_qnorm_g", "fox_knorm_g", "fox_f_bias", "fox_onorm_g", "gdn_conv_w", "gdn_A_log", "gdn_dt_bias",
           "gdn_onorm_g", "w_out", "norm_xattn_g", "mem_norm_g", "w_cq", "w_ckv", "xattn_qnorm_g", "xattn_knorm_g", "w_co",
           "norm_mlp_g", "w_mlp1", "w_mlp2")
PACK_ROWS = 16
LOSS_ROW = len(REPLICATED)


def _whole(name, gathered):
    if name in COLUMN_SHARDED:
        return gathered.transpose(1, 0, 2).reshape(gathered.shape[1], N_DEV * gathered.shape[2])
    return gathered.reshape(N_DEV * gathered.shape[1], gathered.shape[2])


def _blocks(name, whole):
    if whole.ndim == 3:
        return whole
    if name in COLUMN_SHARDED:
        rows, cols = whole.shape
        return whole.reshape(rows, N_DEV, cols // N_DEV).transpose(1, 0, 2)
    return whole.reshape(N_DEV, whole.shape[0] // N_DEV, whole.shape[1])


def _adam_small(everyone, ws, ms, vs):
    n_par = len(ws)

    def body(*refs):
        ev_ref = refs[0]
        w_refs, m_refs, v_refs = (refs[1 + j * n_par:1 + (j + 1) * n_par] for j in range(3))
        outs = refs[1 + 3 * n_par:-1]
        sum_ref = refs[-1]
        total = ev_ref[0]
        for dev in range(1, N_DEV):
            total = total + ev_ref[dev]
        sum_ref[...] = total
        for i in range(n_par):
            n = w_refs[i].shape[1]
            g = sum_ref[i:i + 1, 0:n]
            outs[4 * i][...] = g
            outs[4 * i + 1][...], outs[4 * i + 2][...], outs[4 * i + 3][...] = _adamw(w_refs[i][...], g, m_refs[i][...], v_refs[i][...])
        outs[4 * n_par][...] = sum_ref[LOSS_ROW:LOSS_ROW + 1, 0:1]

    shapes = [jax.ShapeDtypeStruct(a.shape, F32) for a in ws for _ in range(4)] + [jax.ShapeDtypeStruct((1, 1), F32)]
    return pl.pallas_call(body, name="adam_small", out_shape=shapes,
                          scratch_shapes=[pltpu.VMEM((PACK_ROWS, D_MODEL), F32)])(everyone, *ws, *ms, *vs)


def kernel(x, mem, norm_mix_g, w_in, fox_qnorm_g, fox_knorm_g, fox_f_bias, fox_onorm_g, gdn_conv_w, gdn_A_log, gdn_dt_bias, gdn_onorm_g, w_out, norm_xattn_g, mem_norm_g, w_cq, w_ckv, xattn_qnorm_g, xattn_knorm_g, w_co, norm_mlp_g, w_mlp1, w_mlp2, loss_target, m_norm_mix_g, m_w_in, m_fox_qnorm_g, m_fox_knorm_g, m_fox_f_bias, m_fox_onorm_g, m_gdn_conv_w, m_gdn_A_log, m_gdn_dt_bias, m_gdn_onorm_g, m_w_out, m_norm_xattn_g, m_mem_norm_g, m_w_cq, m_w_ckv, m_xattn_qnorm_g, m_xattn_knorm_g, m_w_co, m_norm_mlp_g, m_w_mlp1, m_w_mlp2, v_norm_mix_g, v_w_in, v_fox_qnorm_g, v_fox_knorm_g, v_fox_f_bias, v_fox_onorm_g, v_gdn_conv_w, v_gdn_A_log, v_gdn_dt_bias, v_gdn_onorm_g, v_w_out, v_norm_xattn_g, v_mem_norm_g, v_w_cq, v_w_ckv, v_xattn_qnorm_g, v_xattn_knorm_g, v_w_co, v_norm_mlp_g, v_w_mlp1, v_w_mlp2):
    given = dict(locals())
    w = {k: given[k] for k in WEIGHTS}
    m = {k: given["m_" + k] for k in WEIGHTS}
    v = {k: given["v_" + k] for k in WEIGHTS}

    core = lax.axis_index("c").astype(jnp.int32).reshape(1)
    chip = (2 * lax.axis_index("x") + lax.axis_index("y")).astype(jnp.int32).reshape(1)
    me = 4 * lax.axis_index("x") + 2 * lax.axis_index("y") + lax.axis_index("c")

    local = lambda d: {k: jnp.transpose(d[k][0]) if k in TRANSPOSED else d[k][0] for k in SHARDED}
    w2, m2, v2 = local(w), local(m), local(v)
    shards = {k: w2[k] if k == "gdn_conv_w" else w2[k].astype(BF16) for k in SHARDED}
    early = [k for k in SHARDED if k not in LATE_WEIGHTS]
    whole = {k: _whole(k, g) for k, g in zip(early, _all_gather_hbm([shards[k] for k in early], "gather_early"))}
    late_shards = [shards[k] for k in LATE_WEIGHTS]
    late_lands = [lax.empty((N_DEV,) + s.shape, s.dtype) for s in late_shards]
    gather = _copies_start("gather_late_start", late_shards, late_lands, _gather_copies)

    def late_weights(after):
        srcs, lands = _copies_wait("gather_late_wait", gather[0], gather[1], gather[2], gather[3], after, _gather_copies)
        return {k: _whole(k, lax.dynamic_update_slice(land, src[None], (me, 0, 0))) for k, src, land in zip(LATE_WEIGHTS, srcs, lands)}

    pending = []

    def grads_ready(group):
        names = list(group)
        tag = str(len(pending))
        own = [_blocks(k, group[k]) for k in names]
        if "w_in" in names:
            got = _pair_exchange(own, "grad_pair_exchange_" + tag)
            srcs = [_pair_sum(core, o, g, "grad_pair_sum_" + k) for k, o, g in zip(names, own, got)]
            copies, index, n_parts = _chip_copies, chip, 3
        else:
            srcs, copies, index, n_parts = own, _scatter_copies, me.astype(jnp.int32).reshape(1), 7
        lands = [lax.empty((n_parts,) + s.shape[1:], s.dtype) for s in srcs]
        started = _copies_start("grad_exchange_start_" + tag, srcs, lands, copies)
        pending.append((names, started, copies, index))
        return started[4][0, 0]

    small = {k: w[k] for k in REPLICATED}
    packed, grad_x, _ = _local_step(x, mem, loss_target, **small, **whole, late_weights=late_weights,
                                    grads_ready=grads_ready, first_token=gather[4][0, 0])

    out_g, out_d, out_m, out_v = {}, {}, {}, {}
    after = grad_x
    for tag, (names, started, copies, index) in enumerate(pending):
        srcs, parts = _copies_wait("grad_exchange_wait_" + str(tag), started[0], started[1], started[2], started[3], after, copies)
        for k, s, p in zip(names, srcs, parts):
            res = _sum_adam(index, s, p, w2[k], m2[k], v2[k], "adam_" + k)
            out_g[k], out_d[k], out_m[k], out_v[k] = ((jnp.transpose(r) if k in TRANSPOSED else r)[None] for r in res)
            after = res[0]

    everyone = _all_gather_vmem(packed, "gather_small")
    res = _adam_small(everyone, [w[k] for k in REPLICATED], [m[k] for k in REPLICATED], [v[k] for k in REPLICATED])
    for i, k in enumerate(REPLICATED):
        out_g[k], out_d[k], out_m[k], out_v[k] = res[4 * i:4 * i + 4]
    loss = res[-1].reshape(())

    return (loss, grad_x, *[out_g[k] for k in WEIGHTS], *[out_d[k] for k in WEIGHTS], *[out_m[k] for k in WEIGHTS],
            *[out_v[k] for k in WEIGHTS])
```

```python
import functools

import jax
import jax.numpy as jnp
import numpy as np
from jax import lax
from jax.experimental import pallas as pl
from jax.experimental.pallas import tpu as pltpu

F32 = jnp.float32
BF16 = jnp.bfloat16

D_MODEL = 1024
FOX_HEADS = 8
FOX_HEAD_DIM = 64
FOX_WIDTH = 512
GDN_HEADS = 4
GDN_HEAD_DIM = 128
GDN_WIDTH = 512
CONV_WIDTH = 4
GDN_CHUNK = 128
GDN_GROUP = 4
FOX_BLOCK = 512
XATTN_HEADS = 4
XATTN_HEAD_DIM = 128
XATTN_WIDTH = 512
D_FF = 4096
EPS = 1e-6
NEG_INF = -1e30
N_DEV = 8

ADAM_LR = 0.001
ADAM_B1 = 0.9
ADAM_B2 = 0.999
ADAM_EPS = 1e-08
ADAM_WD = 0.01
ADAM_STEP = 10

P_FOX = 0
P_GDN = 1536
P_Z = 3072
P_SMALL = 3584
P_DIM = 3712
SM_F = 0
SM_B = 8
SM_A = 12
SM_ROWS = 16

LANES = 128
BF16_TILE_ROWS = 16
VMEM_LIMIT = 56 * 1024 * 1024

NN = (((1,), (0,)), ((), ()))
NT = (((1,), (1,)), ((), ()))
TN = (((0,), (0,)), ((), ()))


def _dot(a, b, dims=NN):
    return lax.dot_general(a.astype(BF16), b.astype(BF16), dims, preferred_element_type=F32)


def _cparams(sem=None):
    kw = dict(vmem_limit_bytes=VMEM_LIMIT)
    if sem is not None:
        kw["dimension_semantics"] = sem
    return pltpu.CompilerParams(**kw)


def _sigmoid(x):
    return 0.5 * (jnp.tanh(0.5 * x) + 1.0)


def _softplus(x):
    return jnp.maximum(x, 0.0) + jnp.log1p(jnp.exp(-jnp.abs(x)))


def _log_sigmoid(x):
    return -_softplus(-x)


def _rms(x, g):
    r = lax.rsqrt(jnp.mean(x * x, axis=-1, keepdims=True) + EPS)
    return x * r * g


def _rms_bwd(x, g, dy):
    r = lax.rsqrt(jnp.mean(x * x, axis=-1, keepdims=True) + EPS)
    xh = x * r
    dg = jnp.sum(dy * xh, axis=0, keepdims=True)
    dyg = dy * g
    dx = r * (dyg - xh * jnp.mean(dyg * xh, axis=-1, keepdims=True))
    return dx, dg


def _pair_stat(t, m0):
    s0 = jnp.sum(jnp.where(m0, t, 0.0), axis=-1, keepdims=True)
    s1 = jnp.sum(jnp.where(m0, 0.0, t), axis=-1, keepdims=True)
    return jnp.where(m0, s0, s1)


def _rms_pair(x, g, m0):
    r = lax.rsqrt(_pair_stat(x * x, m0) * (1.0 / FOX_HEAD_DIM) + EPS)
    return x * r * g


def _rms_pair_bwd(x, g, dy, m0):
    r = lax.rsqrt(_pair_stat(x * x, m0) * (1.0 / FOX_HEAD_DIM) + EPS)
    xh = x * r
    dg = jnp.sum(dy * xh, axis=0, keepdims=True)
    dyg = dy * g
    dx = r * (dyg - xh * (_pair_stat(dyg * xh, m0) * (1.0 / FOX_HEAD_DIM)))
    return dx, dg


@jax.custom_vjp
def _mm_nn(a, b):
    return _dot(a, b, NN)


_mm_nn.defvjp(lambda a, b: (_dot(a, b, NN), (a, b)),
              lambda r, g: (_dot(g, r[1], NT), _dot(r[0], g, TN)))


@jax.custom_vjp
def _mm_nt(a, b):
    return _dot(a, b, NT)


_mm_nt.defvjp(lambda a, b: (_dot(a, b, NT), (a, b)),
              lambda r, g: (_dot(g, r[1], NN), _dot(g, r[0], TN)))


@jax.custom_vjp
def _mm_tn(a, b):
    return _dot(a, b, TN)


_mm_tn.defvjp(lambda a, b: (_dot(a, b, TN), (a, b)),
              lambda r, g: (_dot(r[1], g, NT), _dot(r[0], g, NN)))


def _dot3(a, b, dims):
    ah = a.astype(BF16)
    al = (a - ah.astype(F32)).astype(BF16)
    bh = b.astype(BF16)
    bl = (b - bh.astype(F32)).astype(BF16)
    d = functools.partial(lax.dot_general, dimension_numbers=dims, preferred_element_type=F32)
    return d(ah, bh) + d(ah, bl) + d(al, bh)


def _neumann_inverses(mats):
    c = mats[0].shape[0]
    eye = (lax.broadcasted_iota(jnp.int32, (c, c), 0) == lax.broadcasted_iota(jnp.int32, (c, c), 1)).astype(F32)
    xs = [eye - a for a in mats]
    ps = list(mats)
    k = 2
    while k < c + 1:
        ps = [_dot3(p, p, NN) for p in ps]
        xs = [x + _dot3(x, p, NN) for x, p in zip(xs, ps)]
        k *= 2
    return xs


@jax.custom_vjp
def _unit_lower_inverses(mats):
    return _neumann_inverses(mats)


def _unit_lower_inverses_fwd(mats):
    ts = _neumann_inverses(mats)
    return ts, ts


def _unit_lower_inverses_bwd(ts, gs):
    left = [_dot3(t, g, TN) for t, g in zip(ts, gs)]
    return ([-_dot3(m, t, NT) for m, t in zip(left, ts)],)


_unit_lower_inverses.defvjp(_unit_lower_inverses_fwd, _unit_lower_inverses_bwd)


def _wgrad(a, b, name, bk=1024, bn=1024, bt=512, column_blocks=None):
    t_len, k_len = a.shape
    n_len = b.shape[1]
    bk, bn, bt = min(bk, k_len), min(bn, n_len), min(bt, t_len)
    nt = t_len // bt

    def body(a_ref, b_ref, o_ref, acc_ref):
        t = pl.program_id(2)

        @pl.when(t == 0)
        def _():
            acc_ref[...] = jnp.zeros_like(acc_ref)

        acc_ref[...] += _dot(a_ref[...], b_ref[...], TN)

        @pl.when(t == nt - 1)
        def _():
            if column_blocks:
                for jj in range(bn // column_blocks):
                    o_ref[jj] = acc_ref[:, jj * column_blocks:(jj + 1) * column_blocks]
            else:
                o_ref[...] = acc_ref[...]

    if column_blocks:
        out_spec = pl.BlockSpec((bn // column_blocks, bk, column_blocks), lambda i, j, t: (j, i, 0))
        out_shape = jax.ShapeDtypeStruct((n_len // column_blocks, k_len, column_blocks), F32)
    else:
        out_spec = pl.BlockSpec((bk, bn), lambda i, j, t: (i, j))
        out_shape = jax.ShapeDtypeStruct((k_len, n_len), F32)
    return pl.pallas_call(
        body, name=name, grid=(k_len // bk, n_len // bn, nt),
        in_specs=[pl.BlockSpec((bt, bk), lambda i, j, t: (t, i)), pl.BlockSpec((bt, bn), lambda i, j, t: (t, j))],
        out_specs=out_spec, out_shape=out_shape,
        scratch_shapes=[pltpu.VMEM((bk, bn), F32)],
        compiler_params=_cparams(("parallel", "parallel", "arbitrary")),
    )(a, b)


def _rows_matmul(a, b, name, bt=512):
    r_len, t_len = a.shape
    n_len = b.shape[1]
    bt = min(bt, t_len)
    nt = t_len // bt

    def body(a_ref, b_ref, o_ref):
        t = pl.program_id(0)

        @pl.when(t == 0)
        def _():
            o_ref[...] = jnp.zeros_like(o_ref)

        o_ref[...] += _dot(a_ref[...], b_ref[...], NN)

    return pl.pallas_call(
        body, name=name, grid=(nt,),
        in_specs=[pl.BlockSpec((r_len, bt), lambda t: (0, t)), pl.BlockSpec((bt, n_len), lambda t: (t, 0))],
        out_specs=pl.BlockSpec((r_len, n_len), lambda t: (0, 0)),
        out_shape=jax.ShapeDtypeStruct((r_len, n_len), F32),
        compiler_params=_cparams(("arbitrary",)),
    )(a, b)


def _in_proj(x, g, wp, wst, tm=256):
    t_len, d = x.shape
    tm = min(tm, t_len)

    def body(x_ref, g_ref, wp_ref, wst_ref, h_ref, fox_ref, gdn_ref, z_ref, sm_ref, smt_ref):
        h = _rms(x_ref[...], g_ref[...]).astype(BF16)
        h_ref[...] = h
        p = _dot(h, wp_ref[...], NT)
        fox_ref[...] = p[:, P_FOX:P_GDN]
        gdn_ref[...] = p[:, P_GDN:P_Z]
        z_ref[...] = p[:, P_Z:P_SMALL]
        sm_ref[...] = p[:, P_SMALL:P_DIM]
        smt_ref[...] = _dot(wst_ref[...], h, NT)

    row = lambda i: (i, 0)
    fixed = lambda i: (0, 0)
    return pl.pallas_call(
        body, name="in_proj", grid=(t_len // tm,),
        in_specs=[pl.BlockSpec((tm, d), row), pl.BlockSpec((1, d), fixed), pl.BlockSpec((P_DIM, d), fixed),
                  pl.BlockSpec((SM_ROWS, d), fixed)],
        out_specs=[pl.BlockSpec((tm, d), row), pl.BlockSpec((tm, 1536), row), pl.BlockSpec((tm, 1536), row),
                   pl.BlockSpec((tm, 512), row), pl.BlockSpec((tm, LANES), row), pl.BlockSpec((SM_ROWS, tm), lambda i: (0, i))],
        out_shape=[jax.ShapeDtypeStruct((t_len, d), BF16), jax.ShapeDtypeStruct((t_len, 1536), F32),
                   jax.ShapeDtypeStruct((t_len, 1536), F32), jax.ShapeDtypeStruct((t_len, 512), F32),
                   jax.ShapeDtypeStruct((t_len, LANES), F32), jax.ShapeDtypeStruct((SM_ROWS, t_len), F32)],
        compiler_params=_cparams(("parallel",)),
    )(x, g, wp, wst)


def _in_proj_bwd(dproj, dsmt, x, g, wp, wst, dx1, tm=256):
    t_len, d = x.shape
    tm = min(tm, t_len)

    def body(dp_ref, dst_ref, x_ref, g_ref, wp_ref, wst_ref, dx1_ref, dx_ref, dg_ref):
        i = pl.program_id(0)
        dh = _dot(dp_ref[...], wp_ref[...], NN) + _dot(dst_ref[...], wst_ref[...], TN)
        dxn, dg = _rms_bwd(x_ref[...], g_ref[...], dh)
        dx_ref[...] = dx1_ref[...] + dxn

        @pl.when(i == 0)
        def _():
            dg_ref[...] = jnp.zeros_like(dg_ref)

        dg_ref[...] += dg

    row = lambda i: (i, 0)
    fixed = lambda i: (0, 0)
    return pl.pallas_call(
        body, name="in_proj_bwd", grid=(t_len // tm,),
        in_specs=[pl.BlockSpec((tm, P_DIM), row), pl.BlockSpec((SM_ROWS, tm), lambda i: (0, i)), pl.BlockSpec((tm, d), row),
                  pl.BlockSpec((1, d), fixed), pl.BlockSpec((P_DIM, d), fixed), pl.BlockSpec((SM_ROWS, d), fixed),
                  pl.BlockSpec((tm, d), row)],
        out_specs=[pl.BlockSpec((tm, d), row), pl.BlockSpec((1, d), fixed)],
        out_shape=[jax.ShapeDtypeStruct((t_len, d), F32), jax.ShapeDtypeStruct((1, d), F32)],
        compiler_params=_cparams(("arbitrary",)),
    )(dproj, dsmt, x, g, wp, wst, dx1)


def _fox_cum(smt, bias_col, n_batch, s_len, ck=256):
    ck = min(ck, s_len)

    def body(s_ref, b_ref, c_ref):
        tri = (lax.broadcasted_iota(jnp.int32, (ck, ck), 0) <= lax.broadcasted_iota(jnp.int32, (ck, ck), 1)).astype(F32)
        carry = jnp.zeros((SM_ROWS, 1), F32)
        for r in range(s_len // ck):
            ls = _log_sigmoid(s_ref[:, r * ck:(r + 1) * ck] + b_ref[...])
            c = jnp.dot(ls, tri, precision=lax.Precision.HIGHEST, preferred_element_type=F32) + carry
            c_ref[:, r * ck:(r + 1) * ck] = c
            carry = c[:, ck - 1:ck]

    return pl.pallas_call(
        body, name="fox_cum", grid=(n_batch,),
        in_specs=[pl.BlockSpec((SM_ROWS, s_len), lambda b: (0, b)), pl.BlockSpec((SM_ROWS, 1), lambda b: (0, 0))],
        out_specs=pl.BlockSpec((SM_ROWS, s_len), lambda b: (0, b)),
        out_shape=jax.ShapeDtypeStruct(smt.shape, F32),
        compiler_params=_cparams(("parallel",)),
    )(smt, bias_col)


def _fox_cum_bwd(dc, smt, bias_col, n_batch, s_len, ck=256):
    ck = min(ck, s_len)
    nr = s_len // ck

    def body(dc_ref, s_ref, b_ref, dl_ref, db_ref):
        b = pl.program_id(0)
        tri = (lax.broadcasted_iota(jnp.int32, (ck, ck), 0) >= lax.broadcasted_iota(jnp.int32, (ck, ck), 1)).astype(F32)
        carry = jnp.zeros((SM_ROWS, 1), F32)
        tot = jnp.zeros((SM_ROWS, 1), F32)
        for r in reversed(range(nr)):
            sl = slice(r * ck, (r + 1) * ck)
            dls = jnp.dot(dc_ref[:, sl], tri, precision=lax.Precision.HIGHEST, preferred_element_type=F32) + carry
            carry = dls[:, 0:1]
            dl = dls * (1.0 - _sigmoid(s_ref[:, sl] + b_ref[...]))
            dl_ref[:, sl] = dl
            tot = tot + jnp.sum(dl, axis=1, keepdims=True)

        @pl.when(b == 0)
        def _():
            db_ref[...] = jnp.zeros_like(db_ref)

        db_ref[...] += jnp.broadcast_to(tot, db_ref.shape)

    return pl.pallas_call(
        body, name="fox_cum_bwd", grid=(n_batch,),
        in_specs=[pl.BlockSpec((SM_ROWS, s_len), lambda b: (0, b)), pl.BlockSpec((SM_ROWS, s_len), lambda b: (0, b)),
                  pl.BlockSpec((SM_ROWS, 1), lambda b: (0, 0))],
        out_specs=[pl.BlockSpec((SM_ROWS, s_len), lambda b: (0, b)), pl.BlockSpec((SM_ROWS, LANES), lambda b: (0, 0))],
        out_shape=[jax.ShapeDtypeStruct(smt.shape, F32), jax.ShapeDtypeStruct((SM_ROWS, LANES), F32)],
        compiler_params=_cparams(("arbitrary",)),
    )(dc, smt, bias_col)


def _fox_diagonal_mask(tq):
    return lax.broadcasted_iota(jnp.int32, (tq, tq), 1) <= lax.broadcasted_iota(jnp.int32, (tq, tq), 0)


def _fox_fwd(pf, cb, gq2, gk2, go2, tq=256):
    n_batch, s_len, _ = pf.shape
    tq = min(tq, s_len)
    nq = s_len // tq
    scale = FOX_HEAD_DIM ** -0.5

    def body(q_ref, k_ref, v_ref, c_ref, gq_ref, gk_ref, go_ref, o_ref, on_ref, lse_ref, kh_ref, vh_ref):
        j = pl.program_id(1)
        i = pl.program_id(2)
        m0 = lax.broadcasted_iota(jnp.int32, (1, LANES), 1) < FOX_HEAD_DIM

        @pl.when(i == 0)
        def _():
            kn = _rms_pair(k_ref[0], gk_ref[...], m0)
            kh_ref[0] = jnp.where(m0, kn, 0.0).astype(BF16)
            kh_ref[1] = jnp.where(m0, 0.0, kn).astype(BF16)
            v = v_ref[0]
            vh_ref[0] = jnp.where(m0, v, 0.0).astype(BF16)
            vh_ref[1] = jnp.where(m0, 0.0, v).astype(BF16)

        qb = (_rms_pair(q_ref[0], gq_ref[...], m0) * scale).astype(BF16)

        def step(kb, carry, diagonal=False):
            ms, ls, acc = carry
            off = pl.multiple_of(kb * tq, tq)
            new_m, new_l, alphas, pv = [], [], [], []
            for hh in range(2):
                s = _dot(qb, kh_ref[hh, pl.ds(off, tq), :], NT)
                s = s - c_ref[0, kb, pl.ds(2 * j + hh, 1), :]
                if diagonal:
                    s = jnp.where(_fox_diagonal_mask(tq), s, NEG_INF)
                m_new = jnp.maximum(ms[hh], jnp.max(s, axis=-1, keepdims=True))
                alpha = jnp.exp(ms[hh] - m_new)
                p = jnp.exp(s - m_new)
                new_l.append(alpha * ls[hh] + jnp.sum(p, axis=-1, keepdims=True))
                new_m.append(m_new)
                alphas.append(alpha)
                pv.append(_dot(p, vh_ref[hh, pl.ds(off, tq), :], NN))
            acc = jnp.where(m0, alphas[0], alphas[1]) * acc + pv[0] + pv[1]
            return tuple(new_m), tuple(new_l), acc

        init_m = (jnp.full((tq, 1), NEG_INF, F32),) * 2
        init_l = (jnp.zeros((tq, 1), F32),) * 2
        carry = lax.fori_loop(0, i, step, (init_m, init_l, jnp.zeros((tq, LANES), F32)))
        ms, ls, acc = step(i, carry, diagonal=True)
        o = acc / jnp.where(m0, ls[0], ls[1])
        o_ref[0] = o
        on_ref[0] = _rms_pair(o, go_ref[...], m0).astype(BF16)
        lse_ref[0] = jnp.where(m0, ms[0] + jnp.log(ls[0]), ms[1] + jnp.log(ls[1]))

    fixed = lambda b, j, i: (0, 0)
    tile = lambda b, j, i: (b, i, j)
    return pl.pallas_call(
        body, name="fox_fwd", grid=(n_batch, 4, nq),
        in_specs=[pl.BlockSpec((1, tq, LANES), tile), pl.BlockSpec((1, s_len, LANES), lambda b, j, i: (b, 0, 4 + j)),
                  pl.BlockSpec((1, s_len, LANES), lambda b, j, i: (b, 0, 8 + j)),
                  pl.BlockSpec((1, nq, SM_ROWS, tq), lambda b, j, i: (b, 0, 0, 0)),
                  pl.BlockSpec((1, LANES), fixed), pl.BlockSpec((1, LANES), fixed), pl.BlockSpec((1, LANES), fixed)],
        out_specs=[pl.BlockSpec((1, tq, LANES), tile), pl.BlockSpec((1, tq, LANES), tile), pl.BlockSpec((1, tq, LANES), tile)],
        out_shape=[jax.ShapeDtypeStruct((n_batch, s_len, FOX_WIDTH), F32), jax.ShapeDtypeStruct((n_batch, s_len, FOX_WIDTH), BF16),
                   jax.ShapeDtypeStruct((n_batch, s_len, FOX_WIDTH), F32)],
        scratch_shapes=[pltpu.VMEM((2, s_len, LANES), BF16), pltpu.VMEM((2, s_len, LANES), BF16)],
        compiler_params=_cparams(("parallel", "parallel", "arbitrary")),
    )(pf, pf, pf, cb, gq2, gk2, go2)


def _fox_bwd(pf, cb, gq2, gk2, go2, o, lse, don, tq=256):
    n_batch, s_len, _ = pf.shape
    tq = min(tq, s_len)
    nq = s_len // tq
    scale = FOX_HEAD_DIM ** -0.5

    def body(q_ref, k_ref, v_ref, c_ref, gq_ref, gk_ref, go_ref, o_ref, lse_ref, don_ref,
             dq_ref, dk_ref, dv_ref, dc_ref, dgq_ref, dgk_ref, dgo_ref, kh_ref, vh_ref, dka_ref, dva_ref, dca_ref):
        b = pl.program_id(0)
        j = pl.program_id(1)
        i = pl.program_id(2)
        m0 = lax.broadcasted_iota(jnp.int32, (1, LANES), 1) < FOX_HEAD_DIM

        @pl.when((b == 0) & (j == 0) & (i == 0))
        def _():
            dgq_ref[...] = jnp.zeros_like(dgq_ref)
            dgk_ref[...] = jnp.zeros_like(dgk_ref)
            dgo_ref[...] = jnp.zeros_like(dgo_ref)

        @pl.when(i == 0)
        def _():
            kn = _rms_pair(k_ref[0], gk_ref[...], m0)
            kh_ref[0] = jnp.where(m0, kn, 0.0).astype(BF16)
            kh_ref[1] = jnp.where(m0, 0.0, kn).astype(BF16)
            v = v_ref[0]
            vh_ref[0] = jnp.where(m0, v, 0.0).astype(BF16)
            vh_ref[1] = jnp.where(m0, 0.0, v).astype(BF16)
            dka_ref[...] = jnp.zeros_like(dka_ref)
            dva_ref[...] = jnp.zeros_like(dva_ref)
            dca_ref[...] = jnp.zeros_like(dca_ref)

        q = q_ref[0]
        qn = _rms_pair(q, gq_ref[...], m0)
        qs = qn * scale
        qb = qs.astype(BF16)
        qh = (jnp.where(m0, qs, 0.0).astype(BF16), jnp.where(m0, 0.0, qs).astype(BF16))
        ot = o_ref[0]
        do, dgo = _rms_pair_bwd(ot, go_ref[...], don_ref[0], m0)
        dgo_ref[...] += dgo
        dd = do * ot
        delta = (jnp.sum(jnp.where(m0, dd, 0.0), axis=-1, keepdims=True), jnp.sum(jnp.where(m0, 0.0, dd), axis=-1, keepdims=True))
        doh = (jnp.where(m0, do, 0.0).astype(BF16), jnp.where(m0, 0.0, do).astype(BF16))
        lse_t = lse_ref[0]
        lse_h = (lse_t[:, 0:1], lse_t[:, FOX_HEAD_DIM:FOX_HEAD_DIM + 1])

        def step(kb, carry, diagonal=False):
            dqn, rs = carry
            rs = list(rs)
            off = pl.multiple_of(kb * tq, tq)
            for hh in range(2):
                kblk = kh_ref[hh, pl.ds(off, tq), :]
                vblk = vh_ref[hh, pl.ds(off, tq), :]
                s = _dot(qb, kblk, NT)
                s = s - c_ref[0, kb, pl.ds(2 * j + hh, 1), :]
                if diagonal:
                    s = jnp.where(_fox_diagonal_mask(tq), s, NEG_INF)
                p = jnp.exp(s - lse_h[hh])
                dp = _dot(doh[hh], vblk, NT)
                ds = p * (dp - delta[hh])
                dva_ref[pl.ds(off, tq), :] += _dot(p, doh[hh], TN)
                dka_ref[pl.ds(off, tq), :] += _dot(ds, qh[hh], TN)
                dca_ref[kb, hh:hh + 1, :] += -jnp.sum(ds, axis=0, keepdims=True)
                rs[hh] = rs[hh] + jnp.sum(ds, axis=-1, keepdims=True)
                dqn = dqn + _dot(ds, kblk, NN)
            return dqn, tuple(rs)

        carry = lax.fori_loop(0, i, step, (jnp.zeros((tq, LANES), F32), (jnp.zeros((tq, 1), F32),) * 2))
        dqn, rs = step(i, carry, diagonal=True)
        dqn = dqn * scale
        rs_rows = jnp.where(m0, rs[0], rs[1]).T
        dca_ref[i, 0:1, :] += rs_rows[0:1, :]
        dca_ref[i, 1:2, :] += rs_rows[FOX_HEAD_DIM:FOX_HEAD_DIM + 1, :]
        dq, dgq = _rms_pair_bwd(q, gq_ref[...], dqn, m0)
        dq_ref[0] = dq.astype(BF16)
        dgq_ref[...] += dgq

        @pl.when(i == nq - 1)
        def _():
            dk, dgk = _rms_pair_bwd(k_ref[0], gk_ref[...], dka_ref[...], m0)
            dk_ref[0] = dk.astype(BF16)
            dgk_ref[...] += dgk
            dv_ref[0] = dva_ref[...].astype(BF16)
            dc_ref[0, 0] = dca_ref[...]

    fixed = lambda b, j, i: (0, 0)
    tile = lambda b, j, i: (b, i, j)
    full = lambda b, j, i: (b, 0, j)
    wide = jax.ShapeDtypeStruct((n_batch, s_len, FOX_WIDTH), BF16)
    gain = jax.ShapeDtypeStruct((1, LANES), F32)
    return pl.pallas_call(
        body, name="fox_bwd", grid=(n_batch, 4, nq),
        in_specs=[pl.BlockSpec((1, tq, LANES), tile), pl.BlockSpec((1, s_len, LANES), lambda b, j, i: (b, 0, 4 + j)),
                  pl.BlockSpec((1, s_len, LANES), lambda b, j, i: (b, 0, 8 + j)),
                  pl.BlockSpec((1, nq, SM_ROWS, tq), lambda b, j, i: (b, 0, 0, 0)),
                  pl.BlockSpec((1, LANES), fixed), pl.BlockSpec((1, LANES), fixed), pl.BlockSpec((1, LANES), fixed),
                  pl.BlockSpec((1, tq, LANES), tile), pl.BlockSpec((1, tq, LANES), tile), pl.BlockSpec((1, tq, LANES), tile)],
        out_specs=[pl.BlockSpec((1, tq, LANES), tile), pl.BlockSpec((1, s_len, LANES), full), pl.BlockSpec((1, s_len, LANES), full),
                   pl.BlockSpec((1, 1, nq, 8, tq), lambda b, j, i: (b, j, 0, 0, 0)),
                   pl.BlockSpec((1, LANES), fixed), pl.BlockSpec((1, LANES), fixed), pl.BlockSpec((1, LANES), fixed)],
        out_shape=[wide, wide, wide, jax.ShapeDtypeStruct((n_batch, 4, nq, 8, tq), F32), gain, gain, gain],
        scratch_shapes=[pltpu.VMEM((2, s_len, LANES), BF16), pltpu.VMEM((2, s_len, LANES), BF16),
                        pltpu.VMEM((s_len, LANES), F32), pltpu.VMEM((s_len, LANES), F32), pltpu.VMEM((nq, 8, tq), F32)],
        compiler_params=_cparams(("arbitrary", "arbitrary", "arbitrary")),
    )(pf, pf, pf, cb, gq2, gk2, go2, o, lse, don)


def _shift_down(x, k):
    row = lax.broadcasted_iota(jnp.int32, x.shape, 0)
    return jnp.where(row >= k, pltpu.roll(x, k, 0), 0.0)


def _shift_up(x, k):
    n = x.shape[0]
    row = lax.broadcasted_iota(jnp.int32, x.shape, 0)
    return jnp.where(row < n - k, pltpu.roll(x, n - k, 0), 0.0)


def _conv_silu(x, w):
    y = w[3:4] * x + w[2:3] * _shift_down(x, 1) + w[1:2] * _shift_down(x, 2) + w[0:1] * _shift_down(x, 3)
    return y, y * _sigmoid(y)


def _gdn_pre(pg, conv_w):
    n_batch, s_len, width = pg.shape
    ncb = width // LANES

    def body(x_ref, w_ref, o_ref):
        cb = pl.program_id(1)
        _, s = _conv_silu(x_ref[0], w_ref[...])
        sn = s * lax.rsqrt(jnp.sum(s * s, axis=-1, keepdims=True) + EPS)
        o_ref[0] = jnp.where(cb < 2 * GDN_HEADS, sn, s)

    return pl.pallas_call(
        body, name="gdn_pre", grid=(n_batch, ncb),
        in_specs=[pl.BlockSpec((1, s_len, LANES), lambda b, c: (b, 0, c)), pl.BlockSpec((8, LANES), lambda b, c: (0, c))],
        out_specs=pl.BlockSpec((1, s_len, LANES), lambda b, c: (b, 0, c)),
        out_shape=jax.ShapeDtypeStruct(pg.shape, F32),
        compiler_params=_cparams(("parallel", "parallel")),
    )(pg, conv_w)


def _gdn_pre_bwd(pg, conv_w, dout):
    n_batch, s_len, width = pg.shape
    ncb = width // LANES

    def body(x_ref, w_ref, d_ref, dx_ref, dw_ref):
        cb = pl.program_id(0)
        b = pl.program_id(1)
        x = x_ref[0]
        w = w_ref[...]
        d = d_ref[0]
        y, s = _conv_silu(x, w)
        rr = lax.rsqrt(jnp.sum(s * s, axis=-1, keepdims=True) + EPS)
        sn = s * rr
        ds_n = rr * (d - sn * jnp.sum(d * sn, axis=-1, keepdims=True))
        ds = jnp.where(cb < 2 * GDN_HEADS, ds_n, d)
        sig = _sigmoid(y)
        dy = ds * (sig * (1.0 + y * (1.0 - sig)))
        dx = w[3:4] * dy + w[2:3] * _shift_up(dy, 1) + w[1:2] * _shift_up(dy, 2) + w[0:1] * _shift_up(dy, 3)
        dx_ref[0] = dx.astype(BF16)
        dw = [jnp.sum(dy * _shift_down(x, 3 - jj), axis=0, keepdims=True) if jj < 3 else jnp.sum(dy * x, axis=0, keepdims=True)
              for jj in range(CONV_WIDTH)]
        rows = lax.broadcasted_iota(jnp.int32, (8, LANES), 0)
        dwb = jnp.zeros((8, LANES), F32)
        for jj in range(CONV_WIDTH):
            dwb = dwb + jnp.where(rows == jj, dw[jj], 0.0)

        @pl.when(b == 0)
        def _():
            dw_ref[...] = jnp.zeros_like(dw_ref)

        dw_ref[...] += dwb

    blk = lambda c, b: (b, 0, c)
    return pl.pallas_call(
        body, name="gdn_pre_bwd", grid=(ncb, n_batch),
        in_specs=[pl.BlockSpec((1, s_len, LANES), blk), pl.BlockSpec((8, LANES), lambda c, b: (0, c)), pl.BlockSpec((1, s_len, LANES), blk)],
        out_specs=[pl.BlockSpec((1, s_len, LANES), blk), pl.BlockSpec((8, LANES), lambda c, b: (0, c))],
        out_shape=[jax.ShapeDtypeStruct(pg.shape, BF16), jax.ShapeDtypeStruct((8, width), F32)],
        compiler_params=_cparams(("parallel", "arbitrary")),
    )(pg, conv_w, dout)


def _gdn_gates(smc, smr, a_c, dt_c, a_r, dt_r, h):
    lane = lax.broadcasted_iota(jnp.int32, (1, LANES), 1)
    sub = lax.broadcasted_iota(jnp.int32, (SM_ROWS, 1), 0)
    beta_c = jnp.sum(jnp.where(lane == SM_B + h, _sigmoid(smc), 0.0), axis=1, keepdims=True)
    g_all_c = -jnp.exp(a_c) * _softplus(smc + dt_c)
    g_c = jnp.sum(jnp.where(lane == SM_A + h, g_all_c, 0.0), axis=1, keepdims=True)
    g_all_r = -jnp.exp(a_r) * _softplus(smr + dt_r)
    g_r = jnp.sum(jnp.where(sub == SM_A + h, g_all_r, 0.0), axis=0, keepdims=True)
    return beta_c, g_c, g_r


def _gdn_group(qkv, z, smc, smr, a_c, dt_c, a_r, dt_r, go, states):
    n_grp = len(qkv)
    c = qkv[0].shape[0]
    hd = GDN_HEAD_DIM
    pairs = [(g, h) for g in range(n_grp) for h in range(GDN_HEADS)]
    ii = lax.broadcasted_iota(jnp.int32, (c, c), 0)
    jj = lax.broadcasted_iota(jnp.int32, (c, c), 1)
    incl = ii >= jj
    col = lambda arr, base, h: arr[:, base + h * hd:base + (h + 1) * hd]

    qs, ks, kbs, vbs, decays, gcs, g_lasts, amats = [], [], [], [], [], [], [], []
    for g, h in pairs:
        beta_c, g_c, g_r = _gdn_gates(smc[g], smr[g], a_c, dt_c, a_r, dt_r, h)
        gc_c = jnp.sum(jnp.where(incl, g_r, 0.0), axis=1, keepdims=True)
        gc_r = jnp.sum(jnp.where(ii <= jj, g_c, 0.0), axis=0, keepdims=True)
        decay = jnp.where(incl, jnp.exp(jnp.where(incl, gc_c - gc_r, 0.0)), 0.0)
        k = col(qkv[g], GDN_WIDTH, h)
        kb = k * beta_c
        qs.append(col(qkv[g], 0, h) * (hd ** -0.5))
        ks.append(k)
        kbs.append(kb)
        vbs.append(col(qkv[g], 2 * GDN_WIDTH, h) * beta_c)
        decays.append(decay)
        gcs.append(gc_c)
        g_lasts.append(jnp.sum(g_c, axis=0, keepdims=True))
        amats.append(jnp.where(ii > jj, _mm_nt(kb, k) * decay, 0.0))
    ts = _unit_lower_inverses(amats)
    egcs = [jnp.exp(gc) for gc in gcs]
    us = [_mm_nn(t, vb) for t, vb in zip(ts, vbs)]
    ws = [_mm_nn(t, kb * e) for t, kb, e in zip(ts, kbs, egcs)]
    intras = [_mm_nt(q, k) * d for q, k, d in zip(qs, ks, decays)]
    qes = [q * e for q, e in zip(qs, egcs)]
    kds = [k * jnp.exp(gl - gc) for k, gl, gc in zip(ks, g_lasts, gcs)]
    sdecs = [jnp.exp(gl) for gl in g_lasts]

    outs = []
    for g in range(n_grp):
        idx = [g * GDN_HEADS + h for h in range(GDN_HEADS)]
        v_new = [us[i] - _mm_nn(ws[i], states[h]) for h, i in enumerate(idx)]
        o_state = [_mm_nn(qes[i], states[h]) for h, i in enumerate(idx)]
        o_intra = [_mm_nn(intras[i], v_new[h]) for h, i in enumerate(idx)]
        states = [states[h] * sdecs[i] + _mm_tn(kds[i], v_new[h]) for h, i in enumerate(idx)]
        outs.append([_rms(o_state[h] + o_intra[h], go) * (col(z[g], 0, h) * _sigmoid(col(z[g], 0, h))) for h in range(GDN_HEADS)])
    return outs, states


def _gdn_group_size(n_chunks):
    return GDN_GROUP if n_chunks % GDN_GROUP == 0 else 1


def _gdn_fwd(qkvn, z, smc, smr, a_c, dt_c, a_r, dt_r, go):
    n_batch, s_len, _ = qkvn.shape
    c = GDN_CHUNK
    n = s_len // c
    grp = _gdn_group_size(n)
    ng = n // grp
    gc = grp * c
    hd = GDN_HEAD_DIM

    def body(qkv_ref, z_ref, smc_ref, smr_ref, ac_ref, dc_ref, ar_ref, dr_ref, go_ref, og_ref, st_ref, s_ref):
        @pl.when(pl.program_id(1) == 0)
        def _():
            s_ref[...] = jnp.zeros_like(s_ref)

        states = [s_ref[h] for h in range(GDN_HEADS)]
        for h in range(GDN_HEADS):
            st_ref[0, 0, h] = states[h]
        rows = lambda k: slice(k * c, (k + 1) * c)
        outs, nxt = _gdn_group([qkv_ref[0, rows(k), :] for k in range(grp)], [z_ref[0, rows(k), :] for k in range(grp)],
                               [smc_ref[0, rows(k), :] for k in range(grp)], [smr_ref[k] for k in range(grp)],
                               ac_ref[...], dc_ref[...], ar_ref[...], dr_ref[...], go_ref[...], states)
        for k in range(grp):
            for h in range(GDN_HEADS):
                og_ref[0, rows(k), h * hd:(h + 1) * hd] = outs[k][h].astype(BF16)
        for h in range(GDN_HEADS):
            s_ref[h] = nxt[h]

    tok = lambda b, i: (b, i, 0)
    fixed = lambda b, i: (0, 0)
    return pl.pallas_call(
        body, name="gdn_fwd", grid=(n_batch, ng),
        in_specs=[pl.BlockSpec((1, gc, 3 * GDN_WIDTH), tok), pl.BlockSpec((1, gc, GDN_WIDTH), tok), pl.BlockSpec((1, gc, LANES), tok),
                  pl.BlockSpec((grp, SM_ROWS, c), lambda b, i: (b * ng + i, 0, 0)),
                  pl.BlockSpec((1, LANES), fixed), pl.BlockSpec((1, LANES), fixed), pl.BlockSpec((SM_ROWS, 1), fixed),
                  pl.BlockSpec((SM_ROWS, 1), fixed), pl.BlockSpec((1, LANES), fixed)],
        out_specs=[pl.BlockSpec((1, gc, GDN_WIDTH), tok), pl.BlockSpec((1, 1, GDN_HEADS, hd, hd), lambda b, i: (b, i, 0, 0, 0))],
        out_shape=[jax.ShapeDtypeStruct((n_batch, s_len, GDN_WIDTH), BF16), jax.ShapeDtypeStruct((n_batch, ng, GDN_HEADS, hd, hd), F32)],
        scratch_shapes=[pltpu.VMEM((GDN_HEADS, hd, hd), F32)],
        compiler_params=_cparams(("parallel", "arbitrary")),
    )(qkvn, z, smc, smr, a_c, dt_c, a_r, dt_r, go)


def _gdn_bwd(qkvn, z, smc, smr, a_c, dt_c, a_r, dt_r, go, states, dog):
    n_batch, s_len, _ = qkvn.shape
    c = GDN_CHUNK
    n = s_len // c
    grp = _gdn_group_size(n)
    ng = n // grp
    gc = grp * c
    hd = GDN_HEAD_DIM

    def body(qkv_ref, z_ref, smc_ref, smr_ref, ac_ref, dc_ref, ar_ref, dr_ref, go_ref, st_ref, dog_ref,
             dqkv_ref, dz_ref, dsmc_ref, dsmr_ref, dac_ref, ddc_ref, dar_ref, ddr_ref, dgo_ref, ds_ref):
        first = (pl.program_id(0) == 0) & (pl.program_id(1) == 0)

        @pl.when(pl.program_id(1) == 0)
        def _():
            ds_ref[...] = jnp.zeros_like(ds_ref)

        @pl.when(first)
        def _():
            for r in (dac_ref, ddc_ref, dar_ref, ddr_ref, dgo_ref):
                r[...] = jnp.zeros_like(r)

        rows = lambda k: slice(k * c, (k + 1) * c)
        states = [st_ref[0, 0, h] for h in range(GDN_HEADS)]
        prim = ([qkv_ref[0, rows(k), :] for k in range(grp)], [z_ref[0, rows(k), :] for k in range(grp)],
                [smc_ref[0, rows(k), :] for k in range(grp)], [smr_ref[k] for k in range(grp)],
                ac_ref[...], dc_ref[...], ar_ref[...], dr_ref[...], go_ref[...], states)
        _, vjp = jax.vjp(_gdn_group, *prim)
        cot = ([[dog_ref[0, rows(k), h * hd:(h + 1) * hd] for h in range(GDN_HEADS)] for k in range(grp)],
               [ds_ref[h] for h in range(GDN_HEADS)])
        dqkv, dz, dsmc, dsmr, dac, ddc, dar, ddr, dgo, dstates = vjp(cot)
        for k in range(grp):
            dqkv_ref[0, rows(k), :] = dqkv[k]
            dz_ref[0, rows(k), :] = dz[k].astype(BF16)
            dsmc_ref[0, rows(k), :] = dsmc[k]
            dsmr_ref[k] = dsmr[k]
        dac_ref[...] += dac
        ddc_ref[...] += ddc
        dar_ref[...] += dar
        ddr_ref[...] += ddr
        dgo_ref[...] += dgo
        for h in range(GDN_HEADS):
            ds_ref[h] = dstates[h]

    tok = lambda b, i: (b, ng - 1 - i, 0)
    fixed = lambda b, i: (0, 0)
    lane_vec = jax.ShapeDtypeStruct((1, LANES), F32)
    row_vec = jax.ShapeDtypeStruct((SM_ROWS, 1), F32)
    return pl.pallas_call(
        body, name="gdn_bwd", grid=(n_batch, ng),
        in_specs=[pl.BlockSpec((1, gc, 3 * GDN_WIDTH), tok), pl.BlockSpec((1, gc, GDN_WIDTH), tok), pl.BlockSpec((1, gc, LANES), tok),
                  pl.BlockSpec((grp, SM_ROWS, c), lambda b, i: (b * ng + ng - 1 - i, 0, 0)),
                  pl.BlockSpec((1, LANES), fixed), pl.BlockSpec((1, LANES), fixed), pl.BlockSpec((SM_ROWS, 1), fixed),
                  pl.BlockSpec((SM_ROWS, 1), fixed), pl.BlockSpec((1, LANES), fixed),
                  pl.BlockSpec((1, 1, GDN_HEADS, hd, hd), lambda b, i: (b, ng - 1 - i, 0, 0, 0)),
                  pl.BlockSpec((1, gc, GDN_WIDTH), lambda b, i: (b, ng - 1 - i, 1))],
        out_specs=[pl.BlockSpec((1, gc, 3 * GDN_WIDTH), tok), pl.BlockSpec((1, gc, GDN_WIDTH), tok), pl.BlockSpec((1, gc, LANES), tok),
                   pl.BlockSpec((grp, SM_ROWS, c), lambda b, i: (b * ng + ng - 1 - i, 0, 0)),
                   pl.BlockSpec((1, LANES), fixed), pl.BlockSpec((1, LANES), fixed), pl.BlockSpec((SM_ROWS, 1), fixed),
                   pl.BlockSpec((SM_ROWS, 1), fixed), pl.BlockSpec((1, LANES), fixed)],
        out_shape=[jax.ShapeDtypeStruct((n_batch, s_len, 3 * GDN_WIDTH), F32), jax.ShapeDtypeStruct((n_batch, s_len, GDN_WIDTH), BF16),
                   jax.ShapeDtypeStruct((n_batch, s_len, LANES), F32), jax.ShapeDtypeStruct((n_batch * n, SM_ROWS, c), F32),
                   lane_vec, lane_vec, row_vec, row_vec, lane_vec],
        scratch_shapes=[pltpu.VMEM((GDN_HEADS, hd, hd), F32)],
        compiler_params=_cparams(("arbitrary", "arbitrary")),
    )(qkvn, z, smc, smr, a_c, dt_c, a_r, dt_r, go, states, dog)


def _out_proj(x, oa, ob, w_out, g_x, w_cq, tm=256):
    t_len, d = x.shape
    tm = min(tm, t_len)

    def body(x_ref, oa_ref, ob_ref, wo_ref, g_ref, wq_ref, x1_ref, hq_ref, cq_ref):
        x1 = x_ref[...] + _dot(oa_ref[...], wo_ref[0:FOX_WIDTH, :]) + _dot(ob_ref[...], wo_ref[FOX_WIDTH:2 * FOX_WIDTH, :])
        x1_ref[...] = x1
        hq = _rms(x1, g_ref[...]).astype(BF16)
        hq_ref[...] = hq
        cq_ref[...] = _dot(hq, wq_ref[...])

    row = lambda i: (i, 0)
    fixed = lambda i: (0, 0)
    return pl.pallas_call(
        body, name="out_proj", grid=(t_len // tm,),
        in_specs=[pl.BlockSpec((tm, d), row), pl.BlockSpec((tm, FOX_WIDTH), row), pl.BlockSpec((tm, GDN_WIDTH), row),
                  pl.BlockSpec((d, d), fixed), pl.BlockSpec((1, d), fixed), pl.BlockSpec((d, XATTN_WIDTH), fixed)],
        out_specs=[pl.BlockSpec((tm, d), row), pl.BlockSpec((tm, d), row), pl.BlockSpec((tm, XATTN_WIDTH), row)],
        out_shape=[jax.ShapeDtypeStruct((t_len, d), F32), jax.ShapeDtypeStruct((t_len, d), BF16), jax.ShapeDtypeStruct((t_len, XATTN_WIDTH), F32)],
        compiler_params=_cparams(("parallel",)),
    )(x, oa, ob, w_out, g_x, w_cq)


def _out_proj_bwd(dx1, w_out, tm=512):
    t_len, d = dx1.shape
    tm = min(tm, t_len)

    def body(dx_ref, w_ref, o_ref):
        o_ref[...] = _dot(dx_ref[...], w_ref[...], NT)

    return pl.pallas_call(
        body, name="out_proj_bwd", grid=(t_len // tm,),
        in_specs=[pl.BlockSpec((tm, d), lambda i: (i, 0)), pl.BlockSpec((d, d), lambda i: (0, 0))],
        out_specs=pl.BlockSpec((tm, d), lambda i: (i, 0)),
        out_shape=jax.ShapeDtypeStruct((t_len, d), F32),
        compiler_params=_cparams(("parallel",)),
    )(dx1, w_out)


def _mem_kv(mem, g, w_ckv, tm=256):
    t_len, d = mem.shape
    tm = min(tm, t_len)

    def body(x_ref, g_ref, w_ref, h_ref, o_ref):
        h = _rms(x_ref[...], g_ref[...]).astype(BF16)
        h_ref[...] = h
        o_ref[...] = _dot(h, w_ref[...])

    row = lambda i: (i, 0)
    fixed = lambda i: (0, 0)
    return pl.pallas_call(
        body, name="mem_kv", grid=(t_len // tm,),
        in_specs=[pl.BlockSpec((tm, d), row), pl.BlockSpec((1, d), fixed), pl.BlockSpec((d, 2 * XATTN_WIDTH), fixed)],
        out_specs=[pl.BlockSpec((tm, d), row), pl.BlockSpec((tm, 2 * XATTN_WIDTH), row)],
        out_shape=[jax.ShapeDtypeStruct((t_len, d), BF16), jax.ShapeDtypeStruct((t_len, 2 * XATTN_WIDTH), F32)],
        compiler_params=_cparams(("parallel",)),
    )(mem, g, w_ckv)


def _mem_kv_bwd(dckv, mem, g, w_ckv, tm=256):
    t_len, d = mem.shape
    tm = min(tm, t_len)

    def body(d_ref, x_ref, g_ref, w_ref, dg_ref):
        @pl.when(pl.program_id(0) == 0)
        def _():
            dg_ref[...] = jnp.zeros_like(dg_ref)

        dh = _dot(d_ref[...], w_ref[...], NT)
        _, dg = _rms_bwd(x_ref[...], g_ref[...], dh)
        dg_ref[...] += dg

    row = lambda i: (i, 0)
    fixed = lambda i: (0, 0)
    return pl.pallas_call(
        body, name="mem_kv_bwd", grid=(t_len // tm,),
        in_specs=[pl.BlockSpec((tm, 2 * XATTN_WIDTH), row), pl.BlockSpec((tm, d), row), pl.BlockSpec((1, d), fixed),
                  pl.BlockSpec((d, 2 * XATTN_WIDTH), fixed)],
        out_specs=pl.BlockSpec((1, d), fixed),
        out_shape=jax.ShapeDtypeStruct((1, d), F32),
        compiler_params=_cparams(("arbitrary",)),
    )(dckv, mem, g, w_ckv)


def _xattn_probs(qn, kn):
    s = _dot(qn, kn, NT) * (XATTN_HEAD_DIM ** -0.5)
    p = jnp.exp(s - jnp.max(s, axis=-1, keepdims=True))
    return p / jnp.sum(p, axis=-1, keepdims=True)


def _xattn_fwd(cq, ckv, x1, gq, gk, w_co, g_mlp, n_batch, s_len, m_len, tq=512):
    d = x1.shape[1]
    tq = min(tq, s_len)
    nq = s_len // tq
    hd = XATTN_HEAD_DIM

    def body(cq_ref, kv_ref, x1_ref, gq_ref, gk_ref, wo_ref, gm_ref, co_ref, x2_ref, hf_ref):
        outs = []
        for h in range(XATTN_HEADS):
            qn = _rms(cq_ref[:, h * hd:(h + 1) * hd], gq_ref[...])
            kn = _rms(kv_ref[:, h * hd:(h + 1) * hd], gk_ref[...])
            p = _xattn_probs(qn, kn)
            outs.append(_dot(p, kv_ref[:, XATTN_WIDTH + h * hd:XATTN_WIDTH + (h + 1) * hd]).astype(BF16))
        x2 = x1_ref[...]
        for h in range(XATTN_HEADS):
            co_ref[:, h * hd:(h + 1) * hd] = outs[h]
            x2 = x2 + _dot(outs[h], wo_ref[h * hd:(h + 1) * hd, :])
        x2_ref[...] = x2
        hf_ref[...] = _rms(x2, gm_ref[...]).astype(BF16)

    row = lambda b, i: (b * nq + i, 0)
    fixed = lambda b, i: (0, 0)
    t_len = n_batch * s_len
    return pl.pallas_call(
        body, name="xattn_fwd", grid=(n_batch, nq),
        in_specs=[pl.BlockSpec((tq, XATTN_WIDTH), row), pl.BlockSpec((m_len, 2 * XATTN_WIDTH), lambda b, i: (b, 0)),
                  pl.BlockSpec((tq, d), row), pl.BlockSpec((1, hd), fixed), pl.BlockSpec((1, hd), fixed),
                  pl.BlockSpec((XATTN_WIDTH, d), fixed), pl.BlockSpec((1, d), fixed)],
        out_specs=[pl.BlockSpec((tq, XATTN_WIDTH), row), pl.BlockSpec((tq, d), row), pl.BlockSpec((tq, d), row)],
        out_shape=[jax.ShapeDtypeStruct((t_len, XATTN_WIDTH), BF16), jax.ShapeDtypeStruct((t_len, d), F32),
                   jax.ShapeDtypeStruct((t_len, d), BF16)],
        compiler_params=_cparams(("parallel", "parallel")),
    )(cq, ckv, x1, gq, gk, w_co, g_mlp)


def _xattn_bwd(dx2, cq, ckv, x1, gq, gk, w_co, g_x, w_cq, n_batch, s_len, m_len, tq=512):
    d = x1.shape[1]
    tq = min(tq, s_len)
    nq = s_len // tq
    hd = XATTN_HEAD_DIM
    scale = XATTN_HEAD_DIM ** -0.5

    def body(dx2_ref, cq_ref, kv_ref, x1_ref, gq_ref, gk_ref, wo_ref, gx_ref, wq_ref,
             dx1_ref, dcq_ref, dkv_ref, dgq_ref, dgk_ref, dgx_ref, dk_acc, dv_acc):
        b = pl.program_id(0)
        i = pl.program_id(1)

        @pl.when((b == 0) & (i == 0))
        def _():
            dgq_ref[...] = jnp.zeros_like(dgq_ref)
            dgk_ref[...] = jnp.zeros_like(dgk_ref)
            dgx_ref[...] = jnp.zeros_like(dgx_ref)

        @pl.when(i == 0)
        def _():
            dk_acc[...] = jnp.zeros_like(dk_acc)
            dv_acc[...] = jnp.zeros_like(dv_acc)

        dx2 = dx2_ref[...]
        dhq = jnp.zeros((tq, d), F32)
        for h in range(XATTN_HEADS):
            sl = slice(h * hd, (h + 1) * hd)
            q = cq_ref[:, sl]
            qn = _rms(q, gq_ref[...])
            kn = _rms(kv_ref[:, sl], gk_ref[...])
            v = kv_ref[:, XATTN_WIDTH + h * hd:XATTN_WIDTH + (h + 1) * hd]
            p = _xattn_probs(qn, kn)
            dco = _dot(dx2, wo_ref[sl, :], NT)
            dv_acc[:, sl] += _dot(p, dco, TN)
            dp = _dot(dco, v, NT)
            ds = p * (dp - jnp.sum(dp * p, axis=-1, keepdims=True))
            dqn = _dot(ds, kn) * scale
            dk_acc[:, sl] += _dot(ds, qn, TN) * scale
            dq, dgq = _rms_bwd(q, gq_ref[...], dqn)
            dgq_ref[...] += dgq
            dqb = dq.astype(BF16)
            dcq_ref[:, sl] = dqb
            dhq = dhq + _dot(dqb, wq_ref[:, sl], NT)
        dxn, dgx = _rms_bwd(x1_ref[...], gx_ref[...], dhq)
        dgx_ref[...] += dgx
        dx1_ref[...] = dx2 + dxn

        @pl.when(i == nq - 1)
        def _():
            for h in range(XATTN_HEADS):
                sl = slice(h * hd, (h + 1) * hd)
                dk, dgk = _rms_bwd(kv_ref[:, sl], gk_ref[...], dk_acc[:, sl])
                dgk_ref[...] += dgk
                dkv_ref[:, sl] = dk.astype(BF16)
                dkv_ref[:, XATTN_WIDTH + h * hd:XATTN_WIDTH + (h + 1) * hd] = dv_acc[:, sl].astype(BF16)

    row = lambda b, i: (b * nq + i, 0)
    fixed = lambda b, i: (0, 0)
    t_len = n_batch * s_len
    return pl.pallas_call(
        body, name="xattn_bwd", grid=(n_batch, nq),
        in_specs=[pl.BlockSpec((tq, d), row), pl.BlockSpec((tq, XATTN_WIDTH), row), pl.BlockSpec((m_len, 2 * XATTN_WIDTH), lambda b, i: (b, 0)),
                  pl.BlockSpec((tq, d), row), pl.BlockSpec((1, hd), fixed), pl.BlockSpec((1, hd), fixed),
                  pl.BlockSpec((XATTN_WIDTH, d), fixed), pl.BlockSpec((1, d), fixed), pl.BlockSpec((d, XATTN_WIDTH), fixed)],
        out_specs=[pl.BlockSpec((tq, d), row), pl.BlockSpec((tq, XATTN_WIDTH), row), pl.BlockSpec((m_len, 2 * XATTN_WIDTH), lambda b, i: (b, 0)),
                   pl.BlockSpec((1, hd), fixed), pl.BlockSpec((1, hd), fixed), pl.BlockSpec((1, d), fixed)],
        out_shape=[jax.ShapeDtypeStruct((t_len, d), F32), jax.ShapeDtypeStruct((t_len, XATTN_WIDTH), BF16),
                   jax.ShapeDtypeStruct((n_batch * m_len, 2 * XATTN_WIDTH), BF16),
                   jax.ShapeDtypeStruct((1, hd), F32), jax.ShapeDtypeStruct((1, hd), F32), jax.ShapeDtypeStruct((1, d), F32)],
        scratch_shapes=[pltpu.VMEM((m_len, XATTN_WIDTH), F32), pltpu.VMEM((m_len, XATTN_WIDTH), F32)],
        compiler_params=_cparams(("arbitrary", "arbitrary")),
    )(dx2, cq, ckv, x1, gq, gk, w_co, g_x, w_cq)


def _resident(shape):
    return pl.BlockSpec(shape, lambda *_: (0,) * len(shape), pipeline_mode=pl.Buffered(1))


def _mlp_fwd(hf, x2, target, w1, w2, tm=256, tf=1024):
    t_len, d = x2.shape
    f = w1.shape[1]
    tm, tf = min(tm, t_len), min(tf, f)

    def body(hf_ref, x2_ref, tg_ref, w1_ref, w2_ref, u_ref, a_ref, dy_ref, ls_ref):
        hf_t = hf_ref[...]
        y = x2_ref[...]
        for k in range(f // tf):
            cols = slice(k * tf, (k + 1) * tf)
            u = _dot(hf_t, w1_ref[:, cols])
            u_ref[:, cols] = u
            r = jnp.maximum(u, 0.0)
            a = (r * r).astype(BF16)
            a_ref[:, cols] = a
            y = y + _dot(a, w2_ref[cols, :])
        err = y - tg_ref[...]
        dy_ref[...] = err * (1.0 / d)
        ls_ref[...] = jnp.broadcast_to(jnp.sum(jnp.sum(err * err, axis=-1, keepdims=True) * (1.0 / d), axis=0, keepdims=True), ls_ref.shape)

    row = lambda i: (i, 0)
    return pl.pallas_call(
        body, name="mlp_fwd", grid=(t_len // tm,),
        in_specs=[pl.BlockSpec((tm, d), row), pl.BlockSpec((tm, d), row), pl.BlockSpec((tm, d), row), _resident((d, f)), _resident((f, d))],
        out_specs=[pl.BlockSpec((tm, f), row), pl.BlockSpec((tm, f), row), pl.BlockSpec((tm, d), row),
                   pl.BlockSpec((1, 8, LANES), lambda i: (i, 0, 0))],
        out_shape=[jax.ShapeDtypeStruct((t_len, f), F32), jax.ShapeDtypeStruct((t_len, f), BF16), jax.ShapeDtypeStruct((t_len, d), F32),
                   jax.ShapeDtypeStruct((t_len // tm, 8, LANES), F32)],
        compiler_params=_cparams(("parallel",)),
    )(hf, x2, target, w1, w2)


def _mlp_bwd(dy, u, x2, g, w1, w2, tm=256, tf=1024):
    t_len, d = x2.shape
    f = w1.shape[1]
    tm, tf = min(tm, t_len), min(tf, f)

    def body(dy_ref, u_ref, x2_ref, g_ref, w1_ref, w2_ref, du_ref, dx2_ref, dg_ref):
        @pl.when(pl.program_id(0) == 0)
        def _():
            dg_ref[...] = jnp.zeros_like(dg_ref)

        dy_t = dy_ref[...]
        dyb = dy_t.astype(BF16)
        dhf = jnp.zeros((tm, d), F32)
        for k in range(f // tf):
            cols = slice(k * tf, (k + 1) * tf)
            da = _dot(dyb, w2_ref[cols, :], NT)
            du = (da * (2.0 * jnp.maximum(u_ref[:, cols], 0.0))).astype(BF16)
            du_ref[:, cols] = du
            dhf = dhf + _dot(du, w1_ref[:, cols], NT)
        dxn, dg = _rms_bwd(x2_ref[...], g_ref[...], dhf)
        dx2_ref[...] = dy_t + dxn
        dg_ref[...] += dg

    row = lambda i: (i, 0)
    fixed = lambda i: (0, 0)
    return pl.pallas_call(
        body, name="mlp_bwd", grid=(t_len // tm,),
        in_specs=[pl.BlockSpec((tm, d), row), pl.BlockSpec((tm, f), row), pl.BlockSpec((tm, d), row), pl.BlockSpec((1, d), fixed),
                  _resident((d, f)), _resident((f, d))],
        out_specs=[pl.BlockSpec((tm, f), row), pl.BlockSpec((tm, d), row), pl.BlockSpec((1, d), fixed)],
        out_shape=[jax.ShapeDtypeStruct((t_len, f), BF16), jax.ShapeDtypeStruct((t_len, d), F32), jax.ShapeDtypeStruct((1, d), F32)],
        compiler_params=_cparams(("arbitrary",)),
    )(dy, u, x2, g, w1, w2)


def _pad_lanes(v, offset=0, width=LANES):
    return jnp.zeros((1, width), F32).at[:, offset:offset + v.shape[1]].set(v)


def _col(v, offset=0, rows=SM_ROWS):
    return jnp.zeros((rows, 1), F32).at[offset:offset + v.shape[1], 0].set(v[0])


def _pack_small(g_mix, dgq, dgk, dbias, dgo, dac, dar, ddc, ddr, g_gdn_o, g_nx, g_mem, g_xq, g_xk, g_mlp, loss_tiles):
    def body(mix_ref, q_ref, k_ref, b_ref, o_ref, ac_ref, ar_ref, dc_ref, dr_ref, go_ref, nx_ref, mem_ref, xq_ref, xk_ref,
             mlp_ref, lt_ref, out_ref):
        lane = lax.broadcasted_iota(jnp.int32, (1, LANES), 1)
        diag = lax.broadcasted_iota(jnp.int32, (SM_ROWS, LANES), 0) == lax.broadcasted_iota(jnp.int32, (SM_ROWS, LANES), 1)

        def rolled(v, shift):
            return pltpu.roll(jnp.broadcast_to(v, (8, LANES)), shift, 1)[0:1, :]

        def rows_to_lanes(col):
            return jnp.sum(jnp.where(diag, col, 0.0), axis=0, keepdims=True)

        def put(row, v, n):
            out_ref[row:row + 1, 0:LANES] = jnp.where(lane < n, v, 0.0)

        out_ref[...] = jnp.zeros_like(out_ref)
        out_ref[0:1, :] = mix_ref[...]
        for row, ref in ((1, q_ref), (2, k_ref), (4, o_ref)):
            put(row, ref[...] + rolled(ref[...], FOX_HEAD_DIM), FOX_HEAD_DIM)
        put(3, rows_to_lanes(b_ref[...]), FOX_HEADS)
        for row, lane_ref, row_ref in ((5, ac_ref, ar_ref), (6, dc_ref, dr_ref)):
            put(row, rolled(lane_ref[...] + rows_to_lanes(row_ref[...]), LANES - SM_A), GDN_HEADS)
        put(7, go_ref[...], LANES)
        out_ref[8:9, :] = nx_ref[...]
        out_ref[9:10, :] = mem_ref[...]
        put(10, xq_ref[...], LANES)
        put(11, xk_ref[...], LANES)
        out_ref[12:13, :] = mlp_ref[...]
        put(LOSS_ROW, 0.5 * jnp.sum(lt_ref[...], axis=0)[0:1, :], 1)

    args = (g_mix, dgq, dgk, dbias, dgo, dac, dar, ddc, ddr, g_gdn_o, g_nx, g_mem, g_xq, g_xk, g_mlp, loss_tiles)
    return pl.pallas_call(body, name="pack_small", out_shape=jax.ShapeDtypeStruct((PACK_ROWS, D_MODEL), F32))(*args)


LATE_WEIGHTS = ("w_out", "w_cq", "w_ckv", "w_co", "w_mlp1", "w_mlp2")
GRAD_GROUPS = (("w_mlp2", "w_mlp1"), ("w_co", "w_cq", "w_ckv", "w_out"), ("w_in", "gdn_conv_w"))


def _local_step(x, mem, target, norm_mix_g, w_in, fox_qnorm_g, fox_knorm_g, fox_f_bias, fox_onorm_g, gdn_conv_w, gdn_A_log,
                gdn_dt_bias, gdn_onorm_g, norm_xattn_g, mem_norm_g, xattn_qnorm_g, xattn_knorm_g, norm_mlp_g,
                late_weights, grads_ready=None, first_token=0.0):
    if grads_ready is None:
        grads_ready = lambda group: 0.0
    n_batch, s_len, d = x.shape
    m_len = mem.shape[1]
    t_len = n_batch * s_len
    tq = min(FOX_BLOCK, s_len)
    nq = s_len // tq
    n_chunks = s_len // GDN_CHUNK
    x2d = x.reshape(t_len, d)

    wp = jnp.concatenate([w_in[0:1536], w_in[1544:3080], w_in[3088:3600], w_in[1536:1544], w_in[3080:3088],
                          jnp.zeros((P_DIM - 3600, d), BF16)], axis=0)
    wst = jnp.concatenate([w_in[1536:1544], w_in[3080:3088]], axis=0)
    conv_w = jnp.concatenate([gdn_conv_w, jnp.zeros((8 - CONV_WIDTH, gdn_conv_w.shape[1]), F32)], axis=0)
    bias_col = _col(fox_f_bias, SM_F)
    gq2, gk2, go2 = (jnp.tile(g, (1, 2)) for g in (fox_qnorm_g, fox_knorm_g, fox_onorm_g))
    a_c, dt_c = _pad_lanes(gdn_A_log, SM_A), _pad_lanes(gdn_dt_bias, SM_A)
    a_r, dt_r = _col(gdn_A_log, SM_A), _col(gdn_dt_bias, SM_A)

    h1, pfox, pgdn, pz, sm, smt = _in_proj(x2d, norm_mix_g + first_token, wp, wst)
    c_rows = _fox_cum(smt, bias_col, n_batch, s_len)
    cb = c_rows.reshape(SM_ROWS, n_batch, nq, tq).transpose(1, 2, 0, 3)
    pf3 = pfox.reshape(n_batch, s_len, 1536)
    o_fox, oa, lse = _fox_fwd(pf3, cb, gq2, gk2, go2, tq)
    pg3 = pgdn.reshape(n_batch, s_len, 1536)
    qkvn = _gdn_pre(pg3, conv_w)
    z3 = pz.reshape(n_batch, s_len, GDN_WIDTH)
    smc = sm.reshape(n_batch, s_len, LANES)
    smr = smt.reshape(SM_ROWS, n_batch * n_chunks, GDN_CHUNK).transpose(1, 0, 2)
    ob, states = _gdn_fwd(qkvn, z3, smc, smr, a_c, dt_c, a_r, dt_r, gdn_onorm_g)
    oa2, ob2 = oa.reshape(t_len, FOX_WIDTH), ob.reshape(t_len, GDN_WIDTH)
    late = late_weights(ob2)
    w_out, w_cq, w_ckv, w_co, w_mlp1, w_mlp2 = (late[k] for k in LATE_WEIGHTS)
    x1, hq, cq = _out_proj(x2d, oa2, ob2, w_out, norm_xattn_g, w_cq)
    mem2d = mem.reshape(n_batch * m_len, d)
    hm, ckv = _mem_kv(mem2d, mem_norm_g, w_ckv)
    co, x2, hf = _xattn_fwd(cq, ckv, x1, xattn_qnorm_g, xattn_knorm_g, w_co, norm_mlp_g, n_batch, s_len, m_len)
    u, a_act, dy, loss_tiles = _mlp_fwd(hf, x2, target.reshape(t_len, d), w_mlp1, w_mlp2)

    grads = {}
    du, dx2, grads["norm_mlp_g"] = _mlp_bwd(dy, u, x2, norm_mlp_g, w_mlp1, w_mlp2)
    grads["w_mlp2"] = _wgrad(a_act, dy, "wgrad_mlp2")
    grads["w_mlp1"] = _wgrad(hf, du, "wgrad_mlp1", column_blocks=D_FF // N_DEV)
    token = grads_ready({k: grads[k] for k in GRAD_GROUPS[0]})
    grads["w_co"] = _wgrad(co, dx2, "wgrad_co", column_blocks=D_MODEL // N_DEV)
    dx1, dcq, dckv, grads["xattn_qnorm_g"], grads["xattn_knorm_g"], grads["norm_xattn_g"] = _xattn_bwd(
        dx2, cq, ckv, x1, xattn_qnorm_g + token, xattn_knorm_g, w_co, norm_xattn_g, w_cq, n_batch, s_len, m_len)
    grads["w_cq"] = _wgrad(hq, dcq, "wgrad_cq")
    grads["w_ckv"] = _wgrad(hm, dckv, "wgrad_ckv")
    grads["mem_norm_g"] = _mem_kv_bwd(dckv, mem2d, mem_norm_g, w_ckv)
    grads["w_out"] = _wgrad(jnp.concatenate([oa2, ob2], axis=1), dx1, "wgrad_out")
    token = grads_ready({k: grads[k] for k in GRAD_GROUPS[1]})
    dcat = _out_proj_bwd(dx1, w_out)
    dcat3 = dcat.reshape(n_batch, s_len, d)

    dqkvn, dz, dsmc, dsmr, dac, ddc, dar, ddr, grads["gdn_onorm_g"] = _gdn_bwd(
        qkvn, z3, smc, smr, a_c, dt_c, a_r, dt_r, gdn_onorm_g + token, states, dcat3)
    dpg, dconv = _gdn_pre_bwd(pg3, conv_w, dqkvn)
    grads["gdn_conv_w"] = dconv[0:CONV_WIDTH]

    dq, dk, dv, dcb, dgq, dgk, dgo = _fox_bwd(pf3, cb, gq2, gk2, go2, o_fox, lse, dcat3[:, :, 0:FOX_WIDTH], tq)
    dc8 = dcb[:, :, :, 0:2, :].transpose(1, 3, 0, 2, 4).reshape(FOX_HEADS, t_len)
    dc_rows = jnp.concatenate([dc8, jnp.zeros((SM_ROWS - FOX_HEADS, t_len), F32)], axis=0)
    dl_rows, dbias = _fox_cum_bwd(dc_rows, smt, bias_col, n_batch, s_len)
    dsm_rows = jnp.concatenate([dl_rows[0:SM_B], dsmr.transpose(1, 0, 2).reshape(SM_ROWS, t_len)[SM_B:SM_ROWS]], axis=0)

    dproj = jnp.concatenate([dq.reshape(t_len, FOX_WIDTH), dk.reshape(t_len, FOX_WIDTH), dv.reshape(t_len, FOX_WIDTH),
                             dpg.reshape(t_len, 1536), dz.reshape(t_len, GDN_WIDTH), dsmc.reshape(t_len, LANES).astype(BF16)], axis=1)
    dwp = _wgrad(dproj, h1, "wgrad_in", bk=P_DIM, bn=512)
    dwst = _rows_matmul(dsm_rows, h1, "wgrad_in_rows")
    dw_small = dwp[P_SMALL:P_SMALL + SM_ROWS] + dwst
    grads["w_in"] = jnp.concatenate([dwp[0:1536], dw_small[0:8], dwp[1536:3072], dw_small[8:16], dwp[3072:3584]], axis=0)
    token = grads_ready({k: grads[k] for k in GRAD_GROUPS[2]})
    grad_x, grads["norm_mix_g"] = _in_proj_bwd(dproj, dsm_rows, x2d, norm_mix_g + token, wp, wst, dx1)
    packed = _pack_small(grads["norm_mix_g"], dgq, dgk, dbias, dgo, dac, dar, ddc, ddr, grads["gdn_onorm_g"], grads["norm_xattn_g"],
                         grads["mem_norm_g"], grads["xattn_qnorm_g"], grads["xattn_knorm_g"], grads["norm_mlp_g"], loss_tiles)
    return packed, grad_x.reshape(n_batch, s_len, d), {k: grads[k] for k in SHARDED}


MESH_ID = pl.DeviceIdType.MESH
ANY_SPEC = pl.BlockSpec(memory_space=pl.ANY)


def _place():
    x, y, c = lax.axis_index("x"), lax.axis_index("y"), lax.axis_index("c")
    return x, y, c, [(1 - x, y), (x, 1 - y), (1 - x, 1 - y)]


def _place_own(src_ref, dst_ref):
    def staged(buf, sem):
        for a, b in ((src_ref, buf), (buf, dst_ref)):
            cp = pltpu.make_async_copy(a, b, sem)
            cp.start()
            cp.wait()

    pl.run_scoped(staged, pltpu.VMEM(src_ref.shape, src_ref.dtype), pltpu.SemaphoreType.DMA)


def _all_gather_body(n, ins, outs, send_sems, recv_sems, local_sems):
    x, y, c, chips = _place()
    me, sibling = (x, y, c), (x, y, 1 - c)

    def copy(a, k, block, to, src=None):
        dst = outs[a].at[4 * block[0] + 2 * block[1] + block[2]]
        return pltpu.make_async_remote_copy(src_ref=dst if src is None else src, dst_ref=dst, send_sem=send_sems.at[a, k],
                                            recv_sem=recv_sems.at[a, k], device_id=to, device_id_type=MESH_ID)

    mine = [] if local_sems is None else [pltpu.make_async_copy(ins[a], outs[a].at[4 * x + 2 * y + c], local_sems.at[a]) for a in range(n)]
    for cp in mine:
        cp.start()
    first = []
    for a in range(n):
        first.append(copy(a, 0, me, sibling, src=ins[a]))
        first += [copy(a, 1 + j, me, (*chip, c), src=ins[a]) for j, chip in enumerate(chips)]
    for cp in first:
        cp.start()
    if local_sems is None:
        for a in range(n):
            _place_own(ins[a], outs[a].at[4 * x + 2 * y + c])
    passed = []
    for j, chip in enumerate(chips):
        for a in range(n):
            copy(a, 1 + j, (*chip, c), me).wait_recv()
            fwd = copy(a, 4 + j, (*chip, c), sibling)
            fwd.start()
            passed.append(fwd)
    for a in range(n):
        copy(a, 0, sibling, me).wait_recv()
        for j, chip in enumerate(chips):
            copy(a, 4 + j, (*chip, 1 - c), me).wait_recv()
    for cp in first + passed:
        cp.wait_send()
    for cp in mine:
        cp.wait()


def _all_gather_hbm(arrs, name):
    n = len(arrs)

    def body(*refs):
        _all_gather_body(n, refs[:n], refs[n:2 * n], refs[2 * n], refs[2 * n + 1], None)

    return pl.pallas_call(
        body, name=name, in_specs=[ANY_SPEC] * n, out_specs=[ANY_SPEC] * n,
        out_shape=[jax.ShapeDtypeStruct((N_DEV,) + a.shape, a.dtype) for a in arrs],
        scratch_shapes=[pltpu.SemaphoreType.DMA((n, 7)), pltpu.SemaphoreType.DMA((n, 7))],
        compiler_params=pltpu.CompilerParams(vmem_limit_bytes=VMEM_LIMIT),
    )(*arrs)


def _pair_exchange(arrs, name):
    n = len(arrs)

    def body(*refs):
        ins, outs = refs[:n], refs[n:2 * n]
        send_sems, recv_sems = refs[2 * n:]
        x, y, c, _ = _place()
        copies = []
        for a in range(n):
            for chip in range(4):
                copies.append(pltpu.make_async_remote_copy(
                    src_ref=ins[a].at[2 * chip + (1 - c)], dst_ref=outs[a].at[chip], send_sem=send_sems.at[a, chip],
                    recv_sem=recv_sems.at[a, chip], device_id=(x, y, 1 - c), device_id_type=MESH_ID))
        for cp in copies:
            cp.start()
        for cp in copies:
            cp.wait()

    return pl.pallas_call(
        body, name=name, in_specs=[ANY_SPEC] * n, out_specs=[ANY_SPEC] * n,
        out_shape=[jax.ShapeDtypeStruct((4,) + a.shape[1:], a.dtype) for a in arrs],
        scratch_shapes=[pltpu.SemaphoreType.DMA((n, 4)), pltpu.SemaphoreType.DMA((n, 4))],
    )(*arrs)


HBM_SPEC = pl.BlockSpec(memory_space=pltpu.HBM)
SEM_SPEC = pl.BlockSpec(memory_space=pltpu.SEMAPHORE)
DATAFLOW = pltpu.SideEffectType.DATAFLOW_SIDE_EFFECTING


def _in_hbm(arrs):
    return [pltpu.with_memory_space_constraint(a, pltpu.HBM) for a in arrs]


def _copies_start(name, srcs, lands, make_copies):
    n = len(srcs)
    n_copies = len(make_copies(srcs, lands, None, None)[0])

    def body(*refs):
        send_sems, recv_sems = refs[2 * n], refs[2 * n + 1]
        for row in make_copies(refs[:n], refs[n:2 * n], send_sems, recv_sems):
            for cp in row:
                cp.start()
        refs[-1][...] = jnp.zeros_like(refs[-1])

    sems = pltpu.SemaphoreType.DMA((n * n_copies,))
    thru = [pltpu.HBM(a.shape, a.dtype) for a in list(srcs) + list(lands)]
    res = pl.pallas_call(
        body, name=name, in_specs=[HBM_SPEC] * (2 * n),
        out_specs=(SEM_SPEC, SEM_SPEC, *[HBM_SPEC] * (2 * n), pl.BlockSpec(memory_space=pltpu.VMEM)),
        out_shape=(sems, sems, *thru, jax.ShapeDtypeStruct((8, LANES), F32)),
        input_output_aliases={i: 2 + i for i in range(2 * n)},
        compiler_params=pltpu.CompilerParams(has_side_effects=DATAFLOW),
    )(*_in_hbm(list(srcs) + list(lands)))
    return res[0], res[1], list(res[2:2 + n]), list(res[2 + n:2 + 2 * n]), res[-1]


def _copies_wait(name, send_sems, recv_sems, srcs, lands, after, make_copies, own_block=False):
    n = len(srcs)

    def body(*refs):
        if own_block:
            me = 4 * lax.axis_index("x") + 2 * lax.axis_index("y") + lax.axis_index("c")
            for a in range(n):
                _place_own(refs[a], refs[3 * n + 3 + a].at[me])
        for row in make_copies(refs[:n], refs[n:2 * n], refs[2 * n], refs[2 * n + 1]):
            for cp in row:
                cp.wait_send()
                cp.wait_recv()

    res = pl.pallas_call(
        body, name=name, in_specs=[HBM_SPEC] * (2 * n) + [SEM_SPEC, SEM_SPEC, ANY_SPEC],
        out_specs=tuple([HBM_SPEC] * (2 * n)),
        out_shape=tuple(pltpu.HBM(a.shape, a.dtype) for a in list(srcs) + list(lands)),
        input_output_aliases={i: i for i in range(2 * n)},
        compiler_params=pltpu.CompilerParams(has_side_effects=DATAFLOW, vmem_limit_bytes=VMEM_LIMIT),
    )(*srcs, *lands, send_sems, recv_sems, after)
    return list(res[:n]), list(res[n:])


def _gather_copies(srcs, lands, send_sems, recv_sems):
    if send_sems is None:
        return [[None] * 7]
    x, y, c, _ = _place()
    rows = []
    for a in range(len(srcs)):
        row = []
        for k in range(7):
            r = k + 1
            to = (1 - x if r & 4 else x, 1 - y if r & 2 else y, 1 - c if r & 1 else c)
            row.append(pltpu.make_async_remote_copy(
                src_ref=srcs[a], dst_ref=lands[a].at[4 * x + 2 * y + c], send_sem=send_sems.at[7 * a + k], recv_sem=recv_sems.at[7 * a + k],
                device_id=to, device_id_type=MESH_ID))
        rows.append(row)
    return rows


def _scatter_copies(srcs, lands, send_sems, recv_sems):
    if send_sems is None:
        return [[None] * 7]
    x, y, c, _ = _place()
    rows = []
    for a in range(len(srcs)):
        row = []
        for k in range(7):
            r = k + 1
            to = (1 - x if r & 4 else x, 1 - y if r & 2 else y, 1 - c if r & 1 else c)
            row.append(pltpu.make_async_remote_copy(
                src_ref=srcs[a].at[4 * to[0] + 2 * to[1] + to[2]], dst_ref=lands[a].at[k], send_sem=send_sems.at[7 * a + k],
                recv_sem=recv_sems.at[7 * a + k], device_id=to, device_id_type=MESH_ID))
        rows.append(row)
    return rows


def _chip_copies(srcs, lands, send_sems, recv_sems):
    if send_sems is None:
        return [[None] * 3]
    x, y, c, chips = _place()
    return [[pltpu.make_async_remote_copy(
        src_ref=srcs[a].at[2 * chip[0] + chip[1]], dst_ref=lands[a].at[j], send_sem=send_sems.at[3 * a + j], recv_sem=recv_sems.at[3 * a + j],
        device_id=(*chip, c), device_id_type=MESH_ID) for j, chip in enumerate(chips)] for a in range(len(srcs))]


def _all_gather_vmem(block, name):
    def body(in_ref, out_ref, send_sems, recv_sems, local_sems):
        _all_gather_body(1, [in_ref], [out_ref], send_sems, recv_sems, local_sems)

    vmem = pl.BlockSpec(memory_space=pltpu.VMEM)
    return pl.pallas_call(
        body, name=name, in_specs=[vmem], out_specs=vmem,
        out_shape=jax.ShapeDtypeStruct((N_DEV,) + block.shape, block.dtype),
        scratch_shapes=[pltpu.SemaphoreType.DMA((1, 7)), pltpu.SemaphoreType.DMA((1, 7)), pltpu.SemaphoreType.DMA((1,))],
    )(block)


def _tile(rows, cols):
    if rows <= 256:
        return rows, cols
    tr = 256 if cols <= 512 else 128
    if rows % tr == 0:
        return tr, cols
    return rows, 256


def _pair_sum(core, own, got, name):
    _, rows, cols = own.shape
    tr, tc = _tile(rows, cols)

    def body(c_ref, own_ref, got_ref, o_ref):
        o_ref[0] = own_ref[0] + got_ref[0]

    return pl.pallas_call(
        body, name=name,
        grid_spec=pltpu.PrefetchScalarGridSpec(
            num_scalar_prefetch=1, grid=(4, rows // tr, cols // tc),
            in_specs=[pl.BlockSpec((1, tr, tc), lambda k, i, j, c: (2 * k + c[0], i, j)),
                      pl.BlockSpec((1, tr, tc), lambda k, i, j, c: (k, i, j))],
            out_specs=pl.BlockSpec((1, tr, tc), lambda k, i, j, c: (k, i, j))),
        out_shape=jax.ShapeDtypeStruct((4, rows, cols), F32),
        compiler_params=_cparams(("parallel", "parallel", "parallel")),
    )(core, own, got)


def _adamw(w, g, m, v):
    m_new = ADAM_B1 * m + (1.0 - ADAM_B1) * g
    v_new = ADAM_B2 * v + (1.0 - ADAM_B2) * (g * g)
    m_hat = m_new / (1.0 - ADAM_B1 ** ADAM_STEP)
    v_hat = v_new / (1.0 - ADAM_B2 ** ADAM_STEP)
    delta = -ADAM_LR * (m_hat / (jnp.sqrt(v_hat) + ADAM_EPS) + ADAM_WD * w)
    return delta, m_new, v_new


def _sum_adam(chip, sums, parts, w, m, v, name):
    n_parts, rows, cols = parts.shape
    tr, tc = _tile(rows, cols)

    def body(chip_ref, own_ref, p_ref, w_ref, m_ref, v_ref, g_ref, d_ref, mo_ref, vo_ref):
        g = own_ref[0]
        for k in range(n_parts):
            g = g + p_ref[k]
        g_ref[...] = g
        d_ref[...], mo_ref[...], vo_ref[...] = _adamw(w_ref[...], g, m_ref[...], v_ref[...])

    tile = pl.BlockSpec((tr, tc), lambda i, j, ch: (i, j))
    out = jax.ShapeDtypeStruct((rows, cols), F32)
    return pl.pallas_call(
        body, name=name,
        grid_spec=pltpu.PrefetchScalarGridSpec(
            num_scalar_prefetch=1, grid=(rows // tr, cols // tc),
            in_specs=[pl.BlockSpec((1, tr, tc), lambda i, j, ch: (ch[0], i, j)),
                      pl.BlockSpec((n_parts, tr, tc), lambda i, j, ch: (0, i, j)), tile, tile, tile],
            out_specs=[tile, tile, tile, tile]),
        out_shape=[out, out, out, out],
        compiler_params=_cparams(("parallel", "parallel")),
    )(chip, sums, parts, w, m, v)


SHARDED = ("w_in", "gdn_conv_w", "w_out", "w_cq", "w_ckv", "w_co", "w_mlp1", "w_mlp2")
TRANSPOSED = ("w_in",)
COLUMN_SHARDED = ("gdn_conv_w", "w_co", "w_mlp1")
REPLICATED = ("norm_mix_g", "fox_qnorm_g", "fox_knorm_g", "fox_f_bias", "fox_onorm_g", "gdn_A_log", "gdn_dt_bias", "gdn_onorm_g",
              "norm_xattn_g", "mem_norm_g", "xattn_qnorm_g", "xattn_knorm_g", "norm_mlp_g")
WEIGHTS = ("norm_mix_g", "w_in", "fox_qnorm_g", "fox_knorm_g", "fox_f_bias", "fox_onorm_g", "gdn_conv_w", "gdn_A_log", "gdn_dt_bias",
           "gdn_onorm_g", "w_out", "norm_xattn_g", "mem_norm_g", "w_cq", "w_ckv", "xattn_qnorm_g", "xattn_knorm_g", "w_co",
           "norm_mlp_g", "w_mlp1", "w_mlp2")
PACK_ROWS = 16
LOSS_ROW = len(REPLICATED)


def _whole(name, gathered):
    if name in COLUMN_SHARDED:
        return gathered.transpose(1, 0, 2).reshape(gathered.shape[1], N_DEV * gathered.shape[2])
    return gathered.reshape(N_DEV * gathered.shape[1], gathered.shape[2])


def _blocks(name, whole):
    if whole.ndim == 3:
        return whole
    if name in COLUMN_SHARDED:
        rows, cols = whole.shape
        return whole.reshape(rows, N_DEV, cols // N_DEV).transpose(1, 0, 2)
    return whole.reshape(N_DEV, whole.shape[0] // N_DEV, whole.shape[1])


def _adam_small(everyone, ws, ms, vs):
    n_par = len(ws)

    def body(*refs):
        ev_ref = refs[0]
        w_refs, m_refs, v_refs = (refs[1 + j * n_par:1 + (j + 1) * n_par] for j in range(3))
        outs = refs[1 + 3 * n_par:-1]
        sum_ref = refs[-1]
        total = ev_ref[0]
        for dev in range(1, N_DEV):
            total = total + ev_ref[dev]
        sum_ref[...] = total
        for i in range(n_par):
            n = w_refs[i].shape[1]
            g = sum_ref[i:i + 1, 0:n]
            outs[4 * i][...] = g
            outs[4 * i + 1][...], outs[4 * i + 2][...], outs[4 * i + 3][...] = _adamw(w_refs[i][...], g, m_refs[i][...], v_refs[i][...])
        outs[4 * n_par][...] = sum_ref[LOSS_ROW:LOSS_ROW + 1, 0:1]

    shapes = [jax.ShapeDtypeStruct(a.shape, F32) for a in ws for _ in range(4)] + [jax.ShapeDtypeStruct((1, 1), F32)]
    return pl.pallas_call(body, name="adam_small", out_shape=shapes,
                          scratch_shapes=[pltpu.VMEM((PACK_ROWS, D_MODEL), F32)])(everyone, *ws, *ms, *vs)


def kernel(x, mem, norm_mix_g, w_in, fox_qnorm_g, fox_knorm_g, fox_f_bias, fox_onorm_g, gdn_conv_w, gdn_A_log, gdn_dt_bias, gdn_onorm_g, w_out, norm_xattn_g, mem_norm_g, w_cq, w_ckv, xattn_qnorm_g, xattn_knorm_g, w_co, norm_mlp_g, w_mlp1, w_mlp2, loss_target, m_norm_mix_g, m_w_in, m_fox_qnorm_g, m_fox_knorm_g, m_fox_f_bias, m_fox_onorm_g, m_gdn_conv_w, m_gdn_A_log, m_gdn_dt_bias, m_gdn_onorm_g, m_w_out, m_norm_xattn_g, m_mem_norm_g, m_w_cq, m_w_ckv, m_xattn_qnorm_g, m_xattn_knorm_g, m_w_co, m_norm_mlp_g, m_w_mlp1, m_w_mlp2, v_norm_mix_g, v_w_in, v_fox_qnorm_g, v_fox_knorm_g, v_fox_f_bias, v_fox_onorm_g, v_gdn_conv_w, v_gdn_A_log, v_gdn_dt_bias, v_gdn_onorm_g, v_w_out, v_norm_xattn_g, v_mem_norm_g, v_w_cq, v_w_ckv, v_xattn_qnorm_g, v_xattn_knorm_g, v_w_co, v_norm_mlp_g, v_w_mlp1, v_w_mlp2):
    given = dict(locals())
    w = {k: given[k] for k in WEIGHTS}
    m = {k: given["m_" + k] for k in WEIGHTS}
    v = {k: given["v_" + k] for k in WEIGHTS}

    core = lax.axis_index("c").astype(jnp.int32).reshape(1)
    chip = (2 * lax.axis_index("x") + lax.axis_index("y")).astype(jnp.int32).reshape(1)
    me = 4 * lax.axis_index("x") + 2 * lax.axis_index("y") + lax.axis_index("c")

    local = lambda d: {k: jnp.transpose(d[k][0]) if k in TRANSPOSED else d[k][0] for k in SHARDED}
    w2, m2, v2 = local(w), local(m), local(v)
    shards = {k: w2[k] if k == "gdn_conv_w" else w2[k].astype(BF16) for k in SHARDED}
    rows_in = shards["w_in"].shape[0]
    shards["w_in"] = jnp.pad(shards["w_in"], ((0, -rows_in % BF16_TILE_ROWS), (0, 0)))
    early = [k for k in SHARDED if k not in LATE_WEIGHTS]
    gathered = dict(zip(early, _all_gather_hbm([shards[k] for k in early], "gather_early")))
    gathered["w_in"] = gathered["w_in"][:, 0:rows_in]
    whole = {k: _whole(k, g) for k, g in gathered.items()}
    late_shards = [shards[k] for k in LATE_WEIGHTS]
    late_lands = [lax.empty((N_DEV,) + s.shape, s.dtype) for s in late_shards]
    gather = _copies_start("gather_late_start", late_shards, late_lands, _gather_copies)

    def late_weights(after):
        _, lands = _copies_wait("gather_late_wait", gather[0], gather[1], gather[2], gather[3], after, _gather_copies, own_block=True)
        return {k: _whole(k, land) for k, land in zip(LATE_WEIGHTS, lands)}

    pending = []

    def grads_ready(group):
        names = list(group)
        tag = str(len(pending))
        own = [_blocks(k, group[k]) for k in names]
        if "w_in" in names:
            got = _pair_exchange(own, "grad_pair_exchange_" + tag)
            srcs = [_pair_sum(core, o, g, "grad_pair_sum_" + k) for k, o, g in zip(names, own, got)]
            copies, index, n_parts = _chip_copies, chip, 3
        else:
            srcs, copies, index, n_parts = own, _scatter_copies, me.astype(jnp.int32).reshape(1), 7
        lands = [lax.empty((n_parts,) + s.shape[1:], s.dtype) for s in srcs]
        started = _copies_start("grad_exchange_start_" + tag, srcs, lands, copies)
        pending.append((names, started, copies, index))
        return started[4][0, 0]

    small = {k: w[k] for k in REPLICATED}
    packed, grad_x, _ = _local_step(x, mem, loss_target, **small, **whole, late_weights=late_weights,
                                    grads_ready=grads_ready, first_token=gather[4][0, 0])

    out_g, out_d, out_m, out_v = {}, {}, {}, {}
    after = grad_x
    for tag, (names, started, copies, index) in enumerate(pending):
        srcs, parts = _copies_wait("grad_exchange_wait_" + str(tag), started[0], started[1], started[2], started[3], after, copies)
        for k, s, p in zip(names, srcs, parts):
            res = _sum_adam(index, s, p, w2[k], m2[k], v2[k], "adam_" + k)
            out_g[k], out_d[k], out_m[k], out_v[k] = ((jnp.transpose(r) if k in TRANSPOSED else r)[None] for r in res)
            after = res[0]

    everyone = _all_gather_vmem(packed, "gather_small")
    res = _adam_small(everyone, [w[k] for k in REPLICATED], [m[k] for k in REPLICATED], [v[k] for k in REPLICATED])
    for i, k in enumerate(REPLICATED):
        out_g[k], out_d[k], out_m[k], out_v[k] = res[4 * i:4 * i + 4]
    loss = res[-1].reshape(())

    return (loss, grad_x, *[out_g[k] for k in WEIGHTS], *[out_d[k] for k in WEIGHTS], *[out_m[k] for k in WEIGHTS],
            *[out_v[k] for k in WEIGHTS])
```

```python
import functools

import jax
import jax.numpy as jnp
import numpy as np
from jax import lax
from jax.experimental import pallas as pl
from jax.experimental.pallas import tpu as pltpu

F32 = jnp.float32
BF16 = jnp.bfloat16

D_MODEL = 1024
FOX_HEADS = 8
FOX_HEAD_DIM = 64
FOX_WIDTH = 512
GDN_HEADS = 4
GDN_HEAD_DIM = 128
GDN_WIDTH = 512
CONV_WIDTH = 4
GDN_CHUNK = 128
GDN_GROUP = 4
FOX_BLOCK = 512
XATTN_HEADS = 4
XATTN_HEAD_DIM = 128
XATTN_WIDTH = 512
D_FF = 4096
EPS = 1e-6
NEG_INF = -1e30
N_DEV = 8

ADAM_LR = 0.001
ADAM_B1 = 0.9
ADAM_B2 = 0.999
ADAM_EPS = 1e-08
ADAM_WD = 0.01
ADAM_STEP = 10

P_FOX = 0
P_GDN = 1536
P_Z = 3072
P_SMALL = 3584
P_DIM = 3712
SM_F = 0
SM_B = 8
SM_A = 12
SM_ROWS = 16

LANES = 128
VMEM_LIMIT = 56 * 1024 * 1024

NN = (((1,), (0,)), ((), ()))
NT = (((1,), (1,)), ((), ()))
TN = (((0,), (0,)), ((), ()))


def _dot(a, b, dims=NN):
    return lax.dot_general(a.astype(BF16), b.astype(BF16), dims, preferred_element_type=F32)


def _cparams(sem=None):
    kw = dict(vmem_limit_bytes=VMEM_LIMIT)
    if sem is not None:
        kw["dimension_semantics"] = sem
    return pltpu.CompilerParams(**kw)


def _sigmoid(x):
    return 0.5 * (jnp.tanh(0.5 * x) + 1.0)


def _softplus(x):
    return jnp.maximum(x, 0.0) + jnp.log1p(jnp.exp(-jnp.abs(x)))


def _log_sigmoid(x):
    return -_softplus(-x)


def _rms(x, g):
    r = lax.rsqrt(jnp.mean(x * x, axis=-1, keepdims=True) + EPS)
    return x * r * g


def _rms_bwd(x, g, dy):
    r = lax.rsqrt(jnp.mean(x * x, axis=-1, keepdims=True) + EPS)
    xh = x * r
    dg = jnp.sum(dy * xh, axis=0, keepdims=True)
    dyg = dy * g
    dx = r * (dyg - xh * jnp.mean(dyg * xh, axis=-1, keepdims=True))
    return dx, dg


def _pair_stat(t, m0):
    s0 = jnp.sum(jnp.where(m0, t, 0.0), axis=-1, keepdims=True)
    s1 = jnp.sum(jnp.where(m0, 0.0, t), axis=-1, keepdims=True)
    return jnp.where(m0, s0, s1)


def _rms_pair(x, g, m0):
    r = lax.rsqrt(_pair_stat(x * x, m0) * (1.0 / FOX_HEAD_DIM) + EPS)
    return x * r * g


def _rms_pair_bwd(x, g, dy, m0):
    r = lax.rsqrt(_pair_stat(x * x, m0) * (1.0 / FOX_HEAD_DIM) + EPS)
    xh = x * r
    dg = jnp.sum(dy * xh, axis=0, keepdims=True)
    dyg = dy * g
    dx = r * (dyg - xh * (_pair_stat(dyg * xh, m0) * (1.0 / FOX_HEAD_DIM)))
    return dx, dg


@jax.custom_vjp
def _mm_nn(a, b):
    return _dot(a, b, NN)


_mm_nn.defvjp(lambda a, b: (_dot(a, b, NN), (a, b)),
              lambda r, g: (_dot(g, r[1], NT), _dot(r[0], g, TN)))


@jax.custom_vjp
def _mm_nt(a, b):
    return _dot(a, b, NT)


_mm_nt.defvjp(lambda a, b: (_dot(a, b, NT), (a, b)),
              lambda r, g: (_dot(g, r[1], NN), _dot(g, r[0], TN)))


@jax.custom_vjp
def _mm_tn(a, b):
    return _dot(a, b, TN)


_mm_tn.defvjp(lambda a, b: (_dot(a, b, TN), (a, b)),
              lambda r, g: (_dot(r[1], g, NT), _dot(r[0], g, NN)))


def _dot3(a, b, dims):
    ah = a.astype(BF16)
    al = (a - ah.astype(F32)).astype(BF16)
    bh = b.astype(BF16)
    bl = (b - bh.astype(F32)).astype(BF16)
    d = functools.partial(lax.dot_general, dimension_numbers=dims, preferred_element_type=F32)
    return d(ah, bh) + d(ah, bl) + d(al, bh)


def _neumann_inverses(mats):
    c = mats[0].shape[0]
    eye = (lax.broadcasted_iota(jnp.int32, (c, c), 0) == lax.broadcasted_iota(jnp.int32, (c, c), 1)).astype(F32)
    xs = [eye - a for a in mats]
    ps = list(mats)
    k = 2
    while k < c + 1:
        ps = [_dot3(p, p, NN) for p in ps]
        xs = [x + _dot3(x, p, NN) for x, p in zip(xs, ps)]
        k *= 2
    return xs


@jax.custom_vjp
def _unit_lower_inverses(mats):
    return _neumann_inverses(mats)


def _unit_lower_inverses_fwd(mats):
    ts = _neumann_inverses(mats)
    return ts, ts


def _unit_lower_inverses_bwd(ts, gs):
    left = [_dot3(t, g, TN) for t, g in zip(ts, gs)]
    return ([-_dot3(m, t, NT) for m, t in zip(left, ts)],)


_unit_lower_inverses.defvjp(_unit_lower_inverses_fwd, _unit_lower_inverses_bwd)


def _wgrad(a, b, name, bk=1024, bn=1024, bt=512, column_blocks=None):
    t_len, k_len = a.shape
    n_len = b.shape[1]
    bk, bn, bt = min(bk, k_len), min(bn, n_len), min(bt, t_len)
    nt = t_len // bt

    def body(a_ref, b_ref, o_ref, acc_ref):
        t = pl.program_id(2)

        @pl.when(t == 0)
        def _():
            acc_ref[...] = jnp.zeros_like(acc_ref)

        acc_ref[...] += _dot(a_ref[...], b_ref[...], TN)

        @pl.when(t == nt - 1)
        def _():
            if column_blocks:
                for jj in range(bn // column_blocks):
                    o_ref[jj] = acc_ref[:, jj * column_blocks:(jj + 1) * column_blocks]
            else:
                o_ref[...] = acc_ref[...]

    if column_blocks:
        out_spec = pl.BlockSpec((bn // column_blocks, bk, column_blocks), lambda i, j, t: (j, i, 0))
        out_shape = jax.ShapeDtypeStruct((n_len // column_blocks, k_len, column_blocks), F32)
    else:
        out_spec = pl.BlockSpec((bk, bn), lambda i, j, t: (i, j))
        out_shape = jax.ShapeDtypeStruct((k_len, n_len), F32)
    return pl.pallas_call(
        body, name=name, grid=(k_len // bk, n_len // bn, nt),
        in_specs=[pl.BlockSpec((bt, bk), lambda i, j, t: (t, i)), pl.BlockSpec((bt, bn), lambda i, j, t: (t, j))],
        out_specs=out_spec, out_shape=out_shape,
        scratch_shapes=[pltpu.VMEM((bk, bn), F32)],
        compiler_params=_cparams(("parallel", "parallel", "arbitrary")),
    )(a, b)


def _rows_matmul(a, b, name, bt=512):
    r_len, t_len = a.shape
    n_len = b.shape[1]
    bt = min(bt, t_len)
    nt = t_len // bt

    def body(a_ref, b_ref, o_ref):
        t = pl.program_id(0)

        @pl.when(t == 0)
        def _():
            o_ref[...] = jnp.zeros_like(o_ref)

        o_ref[...] += _dot(a_ref[...], b_ref[...], NN)

    return pl.pallas_call(
        body, name=name, grid=(nt,),
        in_specs=[pl.BlockSpec((r_len, bt), lambda t: (0, t)), pl.BlockSpec((bt, n_len), lambda t: (t, 0))],
        out_specs=pl.BlockSpec((r_len, n_len), lambda t: (0, 0)),
        out_shape=jax.ShapeDtypeStruct((r_len, n_len), F32),
        compiler_params=_cparams(("arbitrary",)),
    )(a, b)


def _in_proj(x, g, wp, wst, tm=256):
    t_len, d = x.shape
    tm = min(tm, t_len)

    def body(x_ref, g_ref, wp_ref, wst_ref, h_ref, fox_ref, gdn_ref, z_ref, sm_ref, smt_ref):
        h = _rms(x_ref[...], g_ref[...]).astype(BF16)
        h_ref[...] = h
        p = _dot(h, wp_ref[...], NT)
        fox_ref[...] = p[:, P_FOX:P_GDN]
        gdn_ref[...] = p[:, P_GDN:P_Z]
        z_ref[...] = p[:, P_Z:P_SMALL]
        sm_ref[...] = p[:, P_SMALL:P_DIM]
        smt_ref[...] = _dot(wst_ref[...], h, NT)

    row = lambda i: (i, 0)
    fixed = lambda i: (0, 0)
    return pl.pallas_call(
        body, name="in_proj", grid=(t_len // tm,),
        in_specs=[pl.BlockSpec((tm, d), row), pl.BlockSpec((1, d), fixed), pl.BlockSpec((P_DIM, d), fixed),
                  pl.BlockSpec((SM_ROWS, d), fixed)],
        out_specs=[pl.BlockSpec((tm, d), row), pl.BlockSpec((tm, 1536), row), pl.BlockSpec((tm, 1536), row),
                   pl.BlockSpec((tm, 512), row), pl.BlockSpec((tm, LANES), row), pl.BlockSpec((SM_ROWS, tm), lambda i: (0, i))],
        out_shape=[jax.ShapeDtypeStruct((t_len, d), BF16), jax.ShapeDtypeStruct((t_len, 1536), F32),
                   jax.ShapeDtypeStruct((t_len, 1536), F32), jax.ShapeDtypeStruct((t_len, 512), F32),
                   jax.ShapeDtypeStruct((t_len, LANES), F32), jax.ShapeDtypeStruct((SM_ROWS, t_len), F32)],
        compiler_params=_cparams(("parallel",)),
    )(x, g, wp, wst)


def _in_proj_bwd(dproj, dsmt, x, g, wp, wst, dx1, tm=256):
    t_len, d = x.shape
    tm = min(tm, t_len)

    def body(dp_ref, dst_ref, x_ref, g_ref, wp_ref, wst_ref, dx1_ref, dx_ref, dg_ref):
        i = pl.program_id(0)
        dh = _dot(dp_ref[...], wp_ref[...], NN) + _dot(dst_ref[...], wst_ref[...], TN)
        dxn, dg = _rms_bwd(x_ref[...], g_ref[...], dh)
        dx_ref[...] = dx1_ref[...] + dxn

        @pl.when(i == 0)
        def _():
            dg_ref[...] = jnp.zeros_like(dg_ref)

        dg_ref[...] += dg

    row = lambda i: (i, 0)
    fixed = lambda i: (0, 0)
    return pl.pallas_call(
        body, name="in_proj_bwd", grid=(t_len // tm,),
        in_specs=[pl.BlockSpec((tm, P_DIM), row), pl.BlockSpec((SM_ROWS, tm), lambda i: (0, i)), pl.BlockSpec((tm, d), row),
                  pl.BlockSpec((1, d), fixed), pl.BlockSpec((P_DIM, d), fixed), pl.BlockSpec((SM_ROWS, d), fixed),
                  pl.BlockSpec((tm, d), row)],
        out_specs=[pl.BlockSpec((tm, d), row), pl.BlockSpec((1, d), fixed)],
        out_shape=[jax.ShapeDtypeStruct((t_len, d), F32), jax.ShapeDtypeStruct((1, d), F32)],
        compiler_params=_cparams(("arbitrary",)),
    )(dproj, dsmt, x, g, wp, wst, dx1)


def _fox_cum(smt, bias_col, n_batch, s_len, ck=256):
    ck = min(ck, s_len)

    def body(s_ref, b_ref, c_ref):
        tri = (lax.broadcasted_iota(jnp.int32, (ck, ck), 0) <= lax.broadcasted_iota(jnp.int32, (ck, ck), 1)).astype(F32)
        carry = jnp.zeros((SM_ROWS, 1), F32)
        for r in range(s_len // ck):
            ls = _log_sigmoid(s_ref[:, r * ck:(r + 1) * ck] + b_ref[...])
            c = jnp.dot(ls, tri, precision=lax.Precision.HIGHEST, preferred_element_type=F32) + carry
            c_ref[:, r * ck:(r + 1) * ck] = c
            carry = c[:, ck - 1:ck]

    return pl.pallas_call(
        body, name="fox_cum", grid=(n_batch,),
        in_specs=[pl.BlockSpec((SM_ROWS, s_len), lambda b: (0, b)), pl.BlockSpec((SM_ROWS, 1), lambda b: (0, 0))],
        out_specs=pl.BlockSpec((SM_ROWS, s_len), lambda b: (0, b)),
        out_shape=jax.ShapeDtypeStruct(smt.shape, F32),
        compiler_params=_cparams(("parallel",)),
    )(smt, bias_col)


def _fox_cum_bwd(dc, smt, bias_col, n_batch, s_len, ck=256):
    ck = min(ck, s_len)
    nr = s_len // ck

    def body(dc_ref, s_ref, b_ref, dl_ref, db_ref):
        b = pl.program_id(0)
        tri = (lax.broadcasted_iota(jnp.int32, (ck, ck), 0) >= lax.broadcasted_iota(jnp.int32, (ck, ck), 1)).astype(F32)
        carry = jnp.zeros((SM_ROWS, 1), F32)
        tot = jnp.zeros((SM_ROWS, 1), F32)
        for r in reversed(range(nr)):
            sl = slice(r * ck, (r + 1) * ck)
            dls = jnp.dot(dc_ref[:, sl], tri, precision=lax.Precision.HIGHEST, preferred_element_type=F32) + carry
            carry = dls[:, 0:1]
            dl = dls * (1.0 - _sigmoid(s_ref[:, sl] + b_ref[...]))
            dl_ref[:, sl] = dl
            tot = tot + jnp.sum(dl, axis=1, keepdims=True)

        @pl.when(b == 0)
        def _():
            db_ref[...] = jnp.zeros_like(db_ref)

        db_ref[...] += jnp.broadcast_to(tot, db_ref.shape)

    return pl.pallas_call(
        body, name="fox_cum_bwd", grid=(n_batch,),
        in_specs=[pl.BlockSpec((SM_ROWS, s_len), lambda b: (0, b)), pl.BlockSpec((SM_ROWS, s_len), lambda b: (0, b)),
                  pl.BlockSpec((SM_ROWS, 1), lambda b: (0, 0))],
        out_specs=[pl.BlockSpec((SM_ROWS, s_len), lambda b: (0, b)), pl.BlockSpec((SM_ROWS, LANES), lambda b: (0, 0))],
        out_shape=[jax.ShapeDtypeStruct(smt.shape, F32), jax.ShapeDtypeStruct((SM_ROWS, LANES), F32)],
        compiler_params=_cparams(("arbitrary",)),
    )(dc, smt, bias_col)


def _fox_diagonal_mask(tq):
    return lax.broadcasted_iota(jnp.int32, (tq, tq), 1) <= lax.broadcasted_iota(jnp.int32, (tq, tq), 0)


def _fox_fwd(pf, cb, gq2, gk2, go2, tq=256):
    n_batch, s_len, _ = pf.shape
    tq = min(tq, s_len)
    nq = s_len // tq
    scale = FOX_HEAD_DIM ** -0.5

    def body(q_ref, k_ref, v_ref, c_ref, gq_ref, gk_ref, go_ref, o_ref, on_ref, lse_ref, kh_ref, vh_ref):
        j = pl.program_id(1)
        i = pl.program_id(2)
        m0 = lax.broadcasted_iota(jnp.int32, (1, LANES), 1) < FOX_HEAD_DIM

        @pl.when(i == 0)
        def _():
            kn = _rms_pair(k_ref[0], gk_ref[...], m0)
            kh_ref[0] = jnp.where(m0, kn, 0.0).astype(BF16)
            kh_ref[1] = jnp.where(m0, 0.0, kn).astype(BF16)
            v = v_ref[0]
            vh_ref[0] = jnp.where(m0, v, 0.0).astype(BF16)
            vh_ref[1] = jnp.where(m0, 0.0, v).astype(BF16)

        qb = (_rms_pair(q_ref[0], gq_ref[...], m0) * scale).astype(BF16)

        def step(kb, carry, diagonal=False):
            ms, ls, acc = carry
            off = pl.multiple_of(kb * tq, tq)
            new_m, new_l, alphas, pv = [], [], [], []
            for hh in range(2):
                s = _dot(qb, kh_ref[hh, pl.ds(off, tq), :], NT)
                s = s - c_ref[0, kb, pl.ds(2 * j + hh, 1), :]
                if diagonal:
                    s = jnp.where(_fox_diagonal_mask(tq), s, NEG_INF)
                m_new = jnp.maximum(ms[hh], jnp.max(s, axis=-1, keepdims=True))
                alpha = jnp.exp(ms[hh] - m_new)
                p = jnp.exp(s - m_new)
                new_l.append(alpha * ls[hh] + jnp.sum(p, axis=-1, keepdims=True))
                new_m.append(m_new)
                alphas.append(alpha)
                pv.append(_dot(p, vh_ref[hh, pl.ds(off, tq), :], NN))
            acc = jnp.where(m0, alphas[0], alphas[1]) * acc + pv[0] + pv[1]
            return tuple(new_m), tuple(new_l), acc

        init_m = (jnp.full((tq, 1), NEG_INF, F32),) * 2
        init_l = (jnp.zeros((tq, 1), F32),) * 2
        carry = lax.fori_loop(0, i, step, (init_m, init_l, jnp.zeros((tq, LANES), F32)))
        ms, ls, acc = step(i, carry, diagonal=True)
        o = acc / jnp.where(m0, ls[0], ls[1])
        o_ref[0] = o
        on_ref[0] = _rms_pair(o, go_ref[...], m0).astype(BF16)
        lse_ref[0] = jnp.where(m0, ms[0] + jnp.log(ls[0]), ms[1] + jnp.log(ls[1]))

    fixed = lambda b, j, i: (0, 0)
    tile = lambda b, j, i: (b, i, j)
    return pl.pallas_call(
        body, name="fox_fwd", grid=(n_batch, 4, nq),
        in_specs=[pl.BlockSpec((1, tq, LANES), tile), pl.BlockSpec((1, s_len, LANES), lambda b, j, i: (b, 0, 4 + j)),
                  pl.BlockSpec((1, s_len, LANES), lambda b, j, i: (b, 0, 8 + j)),
                  pl.BlockSpec((1, nq, SM_ROWS, tq), lambda b, j, i: (b, 0, 0, 0)),
                  pl.BlockSpec((1, LANES), fixed), pl.BlockSpec((1, LANES), fixed), pl.BlockSpec((1, LANES), fixed)],
        out_specs=[pl.BlockSpec((1, tq, LANES), tile), pl.BlockSpec((1, tq, LANES), tile), pl.BlockSpec((1, tq, LANES), tile)],
        out_shape=[jax.ShapeDtypeStruct((n_batch, s_len, FOX_WIDTH), F32), jax.ShapeDtypeStruct((n_batch, s_len, FOX_WIDTH), BF16),
                   jax.ShapeDtypeStruct((n_batch, s_len, FOX_WIDTH), F32)],
        scratch_shapes=[pltpu.VMEM((2, s_len, LANES), BF16), pltpu.VMEM((2, s_len, LANES), BF16)],
        compiler_params=_cparams(("parallel", "parallel", "arbitrary")),
    )(pf, pf, pf, cb, gq2, gk2, go2)


def _fox_bwd(pf, cb, gq2, gk2, go2, o, lse, don, tq=256):
    n_batch, s_len, _ = pf.shape
    tq = min(tq, s_len)
    nq = s_len // tq
    scale = FOX_HEAD_DIM ** -0.5

    def body(q_ref, k_ref, v_ref, c_ref, gq_ref, gk_ref, go_ref, o_ref, lse_ref, don_ref,
             dq_ref, dk_ref, dv_ref, dc_ref, dgq_ref, dgk_ref, dgo_ref, kh_ref, vh_ref, dka_ref, dva_ref, dca_ref):
        b = pl.program_id(0)
        j = pl.program_id(1)
        i = pl.program_id(2)
        m0 = lax.broadcasted_iota(jnp.int32, (1, LANES), 1) < FOX_HEAD_DIM

        @pl.when((b == 0) & (j == 0) & (i == 0))
        def _():
            dgq_ref[...] = jnp.zeros_like(dgq_ref)
            dgk_ref[...] = jnp.zeros_like(dgk_ref)
            dgo_ref[...] = jnp.zeros_like(dgo_ref)

        @pl.when(i == 0)
        def _():
            kn = _rms_pair(k_ref[0], gk_ref[...], m0)
            kh_ref[0] = jnp.where(m0, kn, 0.0).astype(BF16)
            kh_ref[1] = jnp.where(m0, 0.0, kn).astype(BF16)
            v = v_ref[0]
            vh_ref[0] = jnp.where(m0, v, 0.0).astype(BF16)
            vh_ref[1] = jnp.where(m0, 0.0, v).astype(BF16)
            dka_ref[...] = jnp.zeros_like(dka_ref)
            dva_ref[...] = jnp.zeros_like(dva_ref)
            dca_ref[...] = jnp.zeros_like(dca_ref)

        q = q_ref[0]
        qn = _rms_pair(q, gq_ref[...], m0)
        qs = qn * scale
        qb = qs.astype(BF16)
        qh = (jnp.where(m0, qs, 0.0).astype(BF16), jnp.where(m0, 0.0, qs).astype(BF16))
        ot = o_ref[0]
        do, dgo = _rms_pair_bwd(ot, go_ref[...], don_ref[0], m0)
        dgo_ref[...] += dgo
        dd = do * ot
        delta = (jnp.sum(jnp.where(m0, dd, 0.0), axis=-1, keepdims=True), jnp.sum(jnp.where(m0, 0.0, dd), axis=-1, keepdims=True))
        doh = (jnp.where(m0, do, 0.0).astype(BF16), jnp.where(m0, 0.0, do).astype(BF16))
        lse_t = lse_ref[0]
        lse_h = (lse_t[:, 0:1], lse_t[:, FOX_HEAD_DIM:FOX_HEAD_DIM + 1])

        def step(kb, carry, diagonal=False):
            dqn, rs = carry
            rs = list(rs)
            off = pl.multiple_of(kb * tq, tq)
            for hh in range(2):
                kblk = kh_ref[hh, pl.ds(off, tq), :]
                vblk = vh_ref[hh, pl.ds(off, tq), :]
                s = _dot(qb, kblk, NT)
                s = s - c_ref[0, kb, pl.ds(2 * j + hh, 1), :]
                if diagonal:
                    s = jnp.where(_fox_diagonal_mask(tq), s, NEG_INF)
                p = jnp.exp(s - lse_h[hh])
                dp = _dot(doh[hh], vblk, NT)
                ds = p * (dp - delta[hh])
                dva_ref[pl.ds(off, tq), :] += _dot(p, doh[hh], TN)
                dka_ref[pl.ds(off, tq), :] += _dot(ds, qh[hh], TN)
                dca_ref[kb, hh:hh + 1, :] += -jnp.sum(ds, axis=0, keepdims=True)
                rs[hh] = rs[hh] + jnp.sum(ds, axis=-1, keepdims=True)
                dqn = dqn + _dot(ds, kblk, NN)
            return dqn, tuple(rs)

        carry = lax.fori_loop(0, i, step, (jnp.zeros((tq, LANES), F32), (jnp.zeros((tq, 1), F32),) * 2))
        dqn, rs = step(i, carry, diagonal=True)
        dqn = dqn * scale
        rs_rows = jnp.where(m0, rs[0], rs[1]).T
        dca_ref[i, 0:1, :] += rs_rows[0:1, :]
        dca_ref[i, 1:2, :] += rs_rows[FOX_HEAD_DIM:FOX_HEAD_DIM + 1, :]
        dq, dgq = _rms_pair_bwd(q, gq_ref[...], dqn, m0)
        dq_ref[0] = dq.astype(BF16)
        dgq_ref[...] += dgq

        @pl.when(i == nq - 1)
        def _():
            dk, dgk = _rms_pair_bwd(k_ref[0], gk_ref[...], dka_ref[...], m0)
            dk_ref[0] = dk.astype(BF16)
            dgk_ref[...] += dgk
            dv_ref[0] = dva_ref[...].astype(BF16)
            dc_ref[0, 0] = dca_ref[...]

    fixed = lambda b, j, i: (0, 0)
    tile = lambda b, j, i: (b, i, j)
    full = lambda b, j, i: (b, 0, j)
    wide = jax.ShapeDtypeStruct((n_batch, s_len, FOX_WIDTH), BF16)
    gain = jax.ShapeDtypeStruct((1, LANES), F32)
    return pl.pallas_call(
        body, name="fox_bwd", grid=(n_batch, 4, nq),
        in_specs=[pl.BlockSpec((1, tq, LANES), tile), pl.BlockSpec((1, s_len, LANES), lambda b, j, i: (b, 0, 4 + j)),
                  pl.BlockSpec((1, s_len, LANES), lambda b, j, i: (b, 0, 8 + j)),
                  pl.BlockSpec((1, nq, SM_ROWS, tq), lambda b, j, i: (b, 0, 0, 0)),
                  pl.BlockSpec((1, LANES), fixed), pl.BlockSpec((1, LANES), fixed), pl.BlockSpec((1, LANES), fixed),
                  pl.BlockSpec((1, tq, LANES), tile), pl.BlockSpec((1, tq, LANES), tile), pl.BlockSpec((1, tq, LANES), tile)],
        out_specs=[pl.BlockSpec((1, tq, LANES), tile), pl.BlockSpec((1, s_len, LANES), full), pl.BlockSpec((1, s_len, LANES), full),
                   pl.BlockSpec((1, 1, nq, 8, tq), lambda b, j, i: (b, j, 0, 0, 0)),
                   pl.BlockSpec((1, LANES), fixed), pl.BlockSpec((1, LANES), fixed), pl.BlockSpec((1, LANES), fixed)],
        out_shape=[wide, wide, wide, jax.ShapeDtypeStruct((n_batch, 4, nq, 8, tq), F32), gain, gain, gain],
        scratch_shapes=[pltpu.VMEM((2, s_len, LANES), BF16), pltpu.VMEM((2, s_len, LANES), BF16),
                        pltpu.VMEM((s_len, LANES), F32), pltpu.VMEM((s_len, LANES), F32), pltpu.VMEM((nq, 8, tq), F32)],
        compiler_params=_cparams(("arbitrary", "arbitrary", "arbitrary")),
    )(pf, pf, pf, cb, gq2, gk2, go2, o, lse, don)


def _shift_down(x, k):
    row = lax.broadcasted_iota(jnp.int32, x.shape, 0)
    return jnp.where(row >= k, pltpu.roll(x, k, 0), 0.0)


def _shift_up(x, k):
    n = x.shape[0]
    row = lax.broadcasted_iota(jnp.int32, x.shape, 0)
    return jnp.where(row < n - k, pltpu.roll(x, n - k, 0), 0.0)


def _conv_silu(x, w):
    y = w[3:4] * x + w[2:3] * _shift_down(x, 1) + w[1:2] * _shift_down(x, 2) + w[0:1] * _shift_down(x, 3)
    return y, y * _sigmoid(y)


def _gdn_pre(pg, conv_w):
    n_batch, s_len, width = pg.shape
    ncb = width // LANES

    def body(x_ref, w_ref, o_ref):
        cb = pl.program_id(1)
        _, s = _conv_silu(x_ref[0], w_ref[...])
        sn = s * lax.rsqrt(jnp.sum(s * s, axis=-1, keepdims=True) + EPS)
        o_ref[0] = jnp.where(cb < 2 * GDN_HEADS, sn, s)

    return pl.pallas_call(
        body, name="gdn_pre", grid=(n_batch, ncb),
        in_specs=[pl.BlockSpec((1, s_len, LANES), lambda b, c: (b, 0, c)), pl.BlockSpec((8, LANES), lambda b, c: (0, c))],
        out_specs=pl.BlockSpec((1, s_len, LANES), lambda b, c: (b, 0, c)),
        out_shape=jax.ShapeDtypeStruct(pg.shape, F32),
        compiler_params=_cparams(("parallel", "parallel")),
    )(pg, conv_w)


def _gdn_pre_bwd(pg, conv_w, dout):
    n_batch, s_len, width = pg.shape
    ncb = width // LANES

    def body(x_ref, w_ref, d_ref, dx_ref, dw_ref):
        cb = pl.program_id(0)
        b = pl.program_id(1)
        x = x_ref[0]
        w = w_ref[...]
        d = d_ref[0]
        y, s = _conv_silu(x, w)
        rr = lax.rsqrt(jnp.sum(s * s, axis=-1, keepdims=True) + EPS)
        sn = s * rr
        ds_n = rr * (d - sn * jnp.sum(d * sn, axis=-1, keepdims=True))
        ds = jnp.where(cb < 2 * GDN_HEADS, ds_n, d)
        sig = _sigmoid(y)
        dy = ds * (sig * (1.0 + y * (1.0 - sig)))
        dx = w[3:4] * dy + w[2:3] * _shift_up(dy, 1) + w[1:2] * _shift_up(dy, 2) + w[0:1] * _shift_up(dy, 3)
        dx_ref[0] = dx.astype(BF16)
        dw = [jnp.sum(dy * _shift_down(x, 3 - jj), axis=0, keepdims=True) if jj < 3 else jnp.sum(dy * x, axis=0, keepdims=True)
              for jj in range(CONV_WIDTH)]
        rows = lax.broadcasted_iota(jnp.int32, (8, LANES), 0)
        dwb = jnp.zeros((8, LANES), F32)
        for jj in range(CONV_WIDTH):
            dwb = dwb + jnp.where(rows == jj, dw[jj], 0.0)

        @pl.when(b == 0)
        def _():
            dw_ref[...] = jnp.zeros_like(dw_ref)

        dw_ref[...] += dwb

    blk = lambda c, b: (b, 0, c)
    return pl.pallas_call(
        body, name="gdn_pre_bwd", grid=(ncb, n_batch),
        in_specs=[pl.BlockSpec((1, s_len, LANES), blk), pl.BlockSpec((8, LANES), lambda c, b: (0, c)), pl.BlockSpec((1, s_len, LANES), blk)],
        out_specs=[pl.BlockSpec((1, s_len, LANES), blk), pl.BlockSpec((8, LANES), lambda c, b: (0, c))],
        out_shape=[jax.ShapeDtypeStruct(pg.shape, BF16), jax.ShapeDtypeStruct((8, width), F32)],
        compiler_params=_cparams(("parallel", "arbitrary")),
    )(pg, conv_w, dout)


def _gdn_gates(smc, smr, a_c, dt_c, a_r, dt_r, h):
    lane = lax.broadcasted_iota(jnp.int32, (1, LANES), 1)
    sub = lax.broadcasted_iota(jnp.int32, (SM_ROWS, 1), 0)
    beta_c = jnp.sum(jnp.where(lane == SM_B + h, _sigmoid(smc), 0.0), axis=1, keepdims=True)
    g_all_c = -jnp.exp(a_c) * _softplus(smc + dt_c)
    g_c = jnp.sum(jnp.where(lane == SM_A + h, g_all_c, 0.0), axis=1, keepdims=True)
    g_all_r = -jnp.exp(a_r) * _softplus(smr + dt_r)
    g_r = jnp.sum(jnp.where(sub == SM_A + h, g_all_r, 0.0), axis=0, keepdims=True)
    return beta_c, g_c, g_r


def _gdn_group(qkv, z, smc, smr, a_c, dt_c, a_r, dt_r, go, states):
    n_grp = len(qkv)
    c = qkv[0].shape[0]
    hd = GDN_HEAD_DIM
    pairs = [(g, h) for g in range(n_grp) for h in range(GDN_HEADS)]
    ii = lax.broadcasted_iota(jnp.int32, (c, c), 0)
    jj = lax.broadcasted_iota(jnp.int32, (c, c), 1)
    incl = ii >= jj
    col = lambda arr, base, h: arr[:, base + h * hd:base + (h + 1) * hd]

    qs, ks, kbs, vbs, decays, gcs, g_lasts, amats = [], [], [], [], [], [], [], []
    for g, h in pairs:
        beta_c, g_c, g_r = _gdn_gates(smc[g], smr[g], a_c, dt_c, a_r, dt_r, h)
        gc_c = jnp.sum(jnp.where(incl, g_r, 0.0), axis=1, keepdims=True)
        gc_r = jnp.sum(jnp.where(ii <= jj, g_c, 0.0), axis=0, keepdims=True)
        decay = jnp.where(incl, jnp.exp(jnp.where(incl, gc_c - gc_r, 0.0)), 0.0)
        k = col(qkv[g], GDN_WIDTH, h)
        kb = k * beta_c
        qs.append(col(qkv[g], 0, h) * (hd ** -0.5))
        ks.append(k)
        kbs.append(kb)
        vbs.append(col(qkv[g], 2 * GDN_WIDTH, h) * beta_c)
        decays.append(decay)
        gcs.append(gc_c)
        g_lasts.append(jnp.sum(g_c, axis=0, keepdims=True))
        amats.append(jnp.where(ii > jj, _mm_nt(kb, k) * decay, 0.0))
    ts = _unit_lower_inverses(amats)
    egcs = [jnp.exp(gc) for gc in gcs]
    us = [_mm_nn(t, vb) for t, vb in zip(ts, vbs)]
    ws = [_mm_nn(t, kb * e) for t, kb, e in zip(ts, kbs, egcs)]
    intras = [_mm_nt(q, k) * d for q, k, d in zip(qs, ks, decays)]
    qes = [q * e for q, e in zip(qs, egcs)]
    kds = [k * jnp.exp(gl - gc) for k, gl, gc in zip(ks, g_lasts, gcs)]
    sdecs = [jnp.exp(gl) for gl in g_lasts]

    outs = []
    for g in range(n_grp):
        idx = [g * GDN_HEADS + h for h in range(GDN_HEADS)]
        v_new = [us[i] - _mm_nn(ws[i], states[h]) for h, i in enumerate(idx)]
        o_state = [_mm_nn(qes[i], states[h]) for h, i in enumerate(idx)]
        o_intra = [_mm_nn(intras[i], v_new[h]) for h, i in enumerate(idx)]
        states = [states[h] * sdecs[i] + _mm_tn(kds[i], v_new[h]) for h, i in enumerate(idx)]
        outs.append([_rms(o_state[h] + o_intra[h], go) * (col(z[g], 0, h) * _sigmoid(col(z[g], 0, h))) for h in range(GDN_HEADS)])
    return outs, states


def _gdn_group_size(n_chunks):
    return GDN_GROUP if n_chunks % GDN_GROUP == 0 else 1


def _gdn_fwd(qkvn, z, smc, smr, a_c, dt_c, a_r, dt_r, go):
    n_batch, s_len, _ = qkvn.shape
    c = GDN_CHUNK
    n = s_len // c
    grp = _gdn_group_size(n)
    ng = n // grp
    gc = grp * c
    hd = GDN_HEAD_DIM

    def body(qkv_ref, z_ref, smc_ref, smr_ref, ac_ref, dc_ref, ar_ref, dr_ref, go_ref, og_ref, st_ref, s_ref):
        @pl.when(pl.program_id(1) == 0)
        def _():
            s_ref[...] = jnp.zeros_like(s_ref)

        states = [s_ref[h] for h in range(GDN_HEADS)]
        for h in range(GDN_HEADS):
            st_ref[0, 0, h] = states[h]
        rows = lambda k: slice(k * c, (k + 1) * c)
        outs, nxt = _gdn_group([qkv_ref[0, rows(k), :] for k in range(grp)], [z_ref[0, rows(k), :] for k in range(grp)],
                               [smc_ref[0, rows(k), :] for k in range(grp)], [smr_ref[k] for k in range(grp)],
                               ac_ref[...], dc_ref[...], ar_ref[...], dr_ref[...], go_ref[...], states)
        for k in range(grp):
            for h in range(GDN_HEADS):
                og_ref[0, rows(k), h * hd:(h + 1) * hd] = outs[k][h].astype(BF16)
        for h in range(GDN_HEADS):
            s_ref[h] = nxt[h]

    tok = lambda b, i: (b, i, 0)
    fixed = lambda b, i: (0, 0)
    return pl.pallas_call(
        body, name="gdn_fwd", grid=(n_batch, ng),
        in_specs=[pl.BlockSpec((1, gc, 3 * GDN_WIDTH), tok), pl.BlockSpec((1, gc, GDN_WIDTH), tok), pl.BlockSpec((1, gc, LANES), tok),
                  pl.BlockSpec((grp, SM_ROWS, c), lambda b, i: (b * ng + i, 0, 0)),
                  pl.BlockSpec((1, LANES), fixed), pl.BlockSpec((1, LANES), fixed), pl.BlockSpec((SM_ROWS, 1), fixed),
                  pl.BlockSpec((SM_ROWS, 1), fixed), pl.BlockSpec((1, LANES), fixed)],
        out_specs=[pl.BlockSpec((1, gc, GDN_WIDTH), tok), pl.BlockSpec((1, 1, GDN_HEADS, hd, hd), lambda b, i: (b, i, 0, 0, 0))],
        out_shape=[jax.ShapeDtypeStruct((n_batch, s_len, GDN_WIDTH), BF16), jax.ShapeDtypeStruct((n_batch, ng, GDN_HEADS, hd, hd), F32)],
        scratch_shapes=[pltpu.VMEM((GDN_HEADS, hd, hd), F32)],
        compiler_params=_cparams(("parallel", "arbitrary")),
    )(qkvn, z, smc, smr, a_c, dt_c, a_r, dt_r, go)


def _gdn_bwd(qkvn, z, smc, smr, a_c, dt_c, a_r, dt_r, go, states, dog):
    n_batch, s_len, _ = qkvn.shape
    c = GDN_CHUNK
    n = s_len // c
    grp = _gdn_group_size(n)
    ng = n // grp
    gc = grp * c
    hd = GDN_HEAD_DIM

    def body(qkv_ref, z_ref, smc_ref, smr_ref, ac_ref, dc_ref, ar_ref, dr_ref, go_ref, st_ref, dog_ref,
             dqkv_ref, dz_ref, dsmc_ref, dsmr_ref, dac_ref, ddc_ref, dar_ref, ddr_ref, dgo_ref, ds_ref):
        first = (pl.program_id(0) == 0) & (pl.program_id(1) == 0)

        @pl.when(pl.program_id(1) == 0)
        def _():
            ds_ref[...] = jnp.zeros_like(ds_ref)

        @pl.when(first)
        def _():
            for r in (dac_ref, ddc_ref, dar_ref, ddr_ref, dgo_ref):
                r[...] = jnp.zeros_like(r)

        rows = lambda k: slice(k * c, (k + 1) * c)
        states = [st_ref[0, 0, h] for h in range(GDN_HEADS)]
        prim = ([qkv_ref[0, rows(k), :] for k in range(grp)], [z_ref[0, rows(k), :] for k in range(grp)],
                [smc_ref[0, rows(k), :] for k in range(grp)], [smr_ref[k] for k in range(grp)],
                ac_ref[...], dc_ref[...], ar_ref[...], dr_ref[...], go_ref[...], states)
        _, vjp = jax.vjp(_gdn_group, *prim)
        cot = ([[dog_ref[0, rows(k), h * hd:(h + 1) * hd] for h in range(GDN_HEADS)] for k in range(grp)],
               [ds_ref[h] for h in range(GDN_HEADS)])
        dqkv, dz, dsmc, dsmr, dac, ddc, dar, ddr, dgo, dstates = vjp(cot)
        for k in range(grp):
            dqkv_ref[0, rows(k), :] = dqkv[k]
            dz_ref[0, rows(k), :] = dz[k].astype(BF16)
            dsmc_ref[0, rows(k), :] = dsmc[k]
            dsmr_ref[k] = dsmr[k]
        dac_ref[...] += dac
        ddc_ref[...] += ddc
        dar_ref[...] += dar
        ddr_ref[...] += ddr
        dgo_ref[...] += dgo
        for h in range(GDN_HEADS):
            ds_ref[h] = dstates[h]

    tok = lambda b, i: (b, ng - 1 - i, 0)
    fixed = lambda b, i: (0, 0)
    lane_vec = jax.ShapeDtypeStruct((1, LANES), F32)
    row_vec = jax.ShapeDtypeStruct((SM_ROWS, 1), F32)
    return pl.pallas_call(
        body, name="gdn_bwd", grid=(n_batch, ng),
        in_specs=[pl.BlockSpec((1, gc, 3 * GDN_WIDTH), tok), pl.BlockSpec((1, gc, GDN_WIDTH), tok), pl.BlockSpec((1, gc, LANES), tok),
                  pl.BlockSpec((grp, SM_ROWS, c), lambda b, i: (b * ng + ng - 1 - i, 0, 0)),
                  pl.BlockSpec((1, LANES), fixed), pl.BlockSpec((1, LANES), fixed), pl.BlockSpec((SM_ROWS, 1), fixed),
                  pl.BlockSpec((SM_ROWS, 1), fixed), pl.BlockSpec((1, LANES), fixed),
                  pl.BlockSpec((1, 1, GDN_HEADS, hd, hd), lambda b, i: (b, ng - 1 - i, 0, 0, 0)),
                  pl.BlockSpec((1, gc, GDN_WIDTH), lambda b, i: (b, ng - 1 - i, 1))],
        out_specs=[pl.BlockSpec((1, gc, 3 * GDN_WIDTH), tok), pl.BlockSpec((1, gc, GDN_WIDTH), tok), pl.BlockSpec((1, gc, LANES), tok),
                   pl.BlockSpec((grp, SM_ROWS, c), lambda b, i: (b * ng + ng - 1 - i, 0, 0)),
                   pl.BlockSpec((1, LANES), fixed), pl.BlockSpec((1, LANES), fixed), pl.BlockSpec((SM_ROWS, 1), fixed),
                   pl.BlockSpec((SM_ROWS, 1), fixed), pl.BlockSpec((1, LANES), fixed)],
        out_shape=[jax.ShapeDtypeStruct((n_batch, s_len, 3 * GDN_WIDTH), F32), jax.ShapeDtypeStruct((n_batch, s_len, GDN_WIDTH), BF16),
                   jax.ShapeDtypeStruct((n_batch, s_len, LANES), F32), jax.ShapeDtypeStruct((n_batch * n, SM_ROWS, c), F32),
                   lane_vec, lane_vec, row_vec, row_vec, lane_vec],
        scratch_shapes=[pltpu.VMEM((GDN_HEADS, hd, hd), F32)],
        compiler_params=_cparams(("arbitrary", "arbitrary")),
    )(qkvn, z, smc, smr, a_c, dt_c, a_r, dt_r, go, states, dog)


def _out_proj(x, oa, ob, w_out, g_x, w_cq, tm=256):
    t_len, d = x.shape
    tm = min(tm, t_len)

    def body(x_ref, oa_ref, ob_ref, wo_ref, g_ref, wq_ref, x1_ref, hq_ref, cq_ref):
        x1 = x_ref[...] + _dot(oa_ref[...], wo_ref[0:FOX_WIDTH, :]) + _dot(ob_ref[...], wo_ref[FOX_WIDTH:2 * FOX_WIDTH, :])
        x1_ref[...] = x1
        hq = _rms(x1, g_ref[...]).astype(BF16)
        hq_ref[...] = hq
        cq_ref[...] = _dot(hq, wq_ref[...])

    row = lambda i: (i, 0)
    fixed = lambda i: (0, 0)
    return pl.pallas_call(
        body, name="out_proj", grid=(t_len // tm,),
        in_specs=[pl.BlockSpec((tm, d), row), pl.BlockSpec((tm, FOX_WIDTH), row), pl.BlockSpec((tm, GDN_WIDTH), row),
                  pl.BlockSpec((d, d), fixed), pl.BlockSpec((1, d), fixed), pl.BlockSpec((d, XATTN_WIDTH), fixed)],
        out_specs=[pl.BlockSpec((tm, d), row), pl.BlockSpec((tm, d), row), pl.BlockSpec((tm, XATTN_WIDTH), row)],
        out_shape=[jax.ShapeDtypeStruct((t_len, d), F32), jax.ShapeDtypeStruct((t_len, d), BF16), jax.ShapeDtypeStruct((t_len, XATTN_WIDTH), F32)],
        compiler_params=_cparams(("parallel",)),
    )(x, oa, ob, w_out, g_x, w_cq)


def _out_proj_bwd(dx1, w_out, tm=512):
    t_len, d = dx1.shape
    tm = min(tm, t_len)

    def body(dx_ref, w_ref, o_ref):
        o_ref[...] = _dot(dx_ref[...], w_ref[...], NT)

    return pl.pallas_call(
        body, name="out_proj_bwd", grid=(t_len // tm,),
        in_specs=[pl.BlockSpec((tm, d), lambda i: (i, 0)), pl.BlockSpec((d, d), lambda i: (0, 0))],
        out_specs=pl.BlockSpec((tm, d), lambda i: (i, 0)),
        out_shape=jax.ShapeDtypeStruct((t_len, d), F32),
        compiler_params=_cparams(("parallel",)),
    )(dx1, w_out)


def _mem_kv(mem, g, w_ckv, tm=256):
    t_len, d = mem.shape
    tm = min(tm, t_len)

    def body(x_ref, g_ref, w_ref, h_ref, o_ref):
        h = _rms(x_ref[...], g_ref[...]).astype(BF16)
        h_ref[...] = h
        o_ref[...] = _dot(h, w_ref[...])

    row = lambda i: (i, 0)
    fixed = lambda i: (0, 0)
    return pl.pallas_call(
        body, name="mem_kv", grid=(t_len // tm,),
        in_specs=[pl.BlockSpec((tm, d), row), pl.BlockSpec((1, d), fixed), pl.BlockSpec((d, 2 * XATTN_WIDTH), fixed)],
        out_specs=[pl.BlockSpec((tm, d), row), pl.BlockSpec((tm, 2 * XATTN_WIDTH), row)],
        out_shape=[jax.ShapeDtypeStruct((t_len, d), BF16), jax.ShapeDtypeStruct((t_len, 2 * XATTN_WIDTH), F32)],
        compiler_params=_cparams(("parallel",)),
    )(mem, g, w_ckv)


def _mem_kv_bwd(dckv, mem, g, w_ckv, tm=256):
    t_len, d = mem.shape
    tm = min(tm, t_len)

    def body(d_ref, x_ref, g_ref, w_ref, dg_ref):
        @pl.when(pl.program_id(0) == 0)
        def _():
            dg_ref[...] = jnp.zeros_like(dg_ref)

        dh = _dot(d_ref[...], w_ref[...], NT)
        _, dg = _rms_bwd(x_ref[...], g_ref[...], dh)
        dg_ref[...] += dg

    row = lambda i: (i, 0)
    fixed = lambda i: (0, 0)
    return pl.pallas_call(
        body, name="mem_kv_bwd", grid=(t_len // tm,),
        in_specs=[pl.BlockSpec((tm, 2 * XATTN_WIDTH), row), pl.BlockSpec((tm, d), row), pl.BlockSpec((1, d), fixed),
                  pl.BlockSpec((d, 2 * XATTN_WIDTH), fixed)],
        out_specs=pl.BlockSpec((1, d), fixed),
        out_shape=jax.ShapeDtypeStruct((1, d), F32),
        compiler_params=_cparams(("arbitrary",)),
    )(dckv, mem, g, w_ckv)


def _xattn_probs(qn, kn):
    s = _dot(qn, kn, NT) * (XATTN_HEAD_DIM ** -0.5)
    p = jnp.exp(s - jnp.max(s, axis=-1, keepdims=True))
    return p / jnp.sum(p, axis=-1, keepdims=True)


def _xattn_fwd(cq, ckv, x1, gq, gk, w_co, g_mlp, n_batch, s_len, m_len, tq=512):
    d = x1.shape[1]
    tq = min(tq, s_len)
    nq = s_len // tq
    hd = XATTN_HEAD_DIM

    def body(cq_ref, kv_ref, x1_ref, gq_ref, gk_ref, wo_ref, gm_ref, co_ref, x2_ref, hf_ref):
        outs = []
        for h in range(XATTN_HEADS):
            qn = _rms(cq_ref[:, h * hd:(h + 1) * hd], gq_ref[...])
            kn = _rms(kv_ref[:, h * hd:(h + 1) * hd], gk_ref[...])
            p = _xattn_probs(qn, kn)
            outs.append(_dot(p, kv_ref[:, XATTN_WIDTH + h * hd:XATTN_WIDTH + (h + 1) * hd]).astype(BF16))
        x2 = x1_ref[...]
        for h in range(XATTN_HEADS):
            co_ref[:, h * hd:(h + 1) * hd] = outs[h]
            x2 = x2 + _dot(outs[h], wo_ref[h * hd:(h + 1) * hd, :])
        x2_ref[...] = x2
        hf_ref[...] = _rms(x2, gm_ref[...]).astype(BF16)

    row = lambda b, i: (b * nq + i, 0)
    fixed = lambda b, i: (0, 0)
    t_len = n_batch * s_len
    return pl.pallas_call(
        body, name="xattn_fwd", grid=(n_batch, nq),
        in_specs=[pl.BlockSpec((tq, XATTN_WIDTH), row), pl.BlockSpec((m_len, 2 * XATTN_WIDTH), lambda b, i: (b, 0)),
                  pl.BlockSpec((tq, d), row), pl.BlockSpec((1, hd), fixed), pl.BlockSpec((1, hd), fixed),
                  pl.BlockSpec((XATTN_WIDTH, d), fixed), pl.BlockSpec((1, d), fixed)],
        out_specs=[pl.BlockSpec((tq, XATTN_WIDTH), row), pl.BlockSpec((tq, d), row), pl.BlockSpec((tq, d), row)],
        out_shape=[jax.ShapeDtypeStruct((t_len, XATTN_WIDTH), BF16), jax.ShapeDtypeStruct((t_len, d), F32),
                   jax.ShapeDtypeStruct((t_len, d), BF16)],
        compiler_params=_cparams(("parallel", "parallel")),
    )(cq, ckv, x1, gq, gk, w_co, g_mlp)


def _xattn_bwd(dx2, cq, ckv, x1, gq, gk, w_co, g_x, w_cq, n_batch, s_len, m_len, tq=512):
    d = x1.shape[1]
    tq = min(tq, s_len)
    nq = s_len // tq
    hd = XATTN_HEAD_DIM
    scale = XATTN_HEAD_DIM ** -0.5

    def body(dx2_ref, cq_ref, kv_ref, x1_ref, gq_ref, gk_ref, wo_ref, gx_ref, wq_ref,
             dx1_ref, dcq_ref, dkv_ref, dgq_ref, dgk_ref, dgx_ref, dk_acc, dv_acc):
        b = pl.program_id(0)
        i = pl.program_id(1)

        @pl.when((b == 0) & (i == 0))
        def _():
            dgq_ref[...] = jnp.zeros_like(dgq_ref)
            dgk_ref[...] = jnp.zeros_like(dgk_ref)
            dgx_ref[...] = jnp.zeros_like(dgx_ref)

        @pl.when(i == 0)
        def _():
            dk_acc[...] = jnp.zeros_like(dk_acc)
            dv_acc[...] = jnp.zeros_like(dv_acc)

        dx2 = dx2_ref[...]
        dhq = jnp.zeros((tq, d), F32)
        for h in range(XATTN_HEADS):
            sl = slice(h * hd, (h + 1) * hd)
            q = cq_ref[:, sl]
            qn = _rms(q, gq_ref[...])
            kn = _rms(kv_ref[:, sl], gk_ref[...])
            v = kv_ref[:, XATTN_WIDTH + h * hd:XATTN_WIDTH + (h + 1) * hd]
            p = _xattn_probs(qn, kn)
            dco = _dot(dx2, wo_ref[sl, :], NT)
            dv_acc[:, sl] += _dot(p, dco, TN)
            dp = _dot(dco, v, NT)
            ds = p * (dp - jnp.sum(dp * p, axis=-1, keepdims=True))
            dqn = _dot(ds, kn) * scale
            dk_acc[:, sl] += _dot(ds, qn, TN) * scale
            dq, dgq = _rms_bwd(q, gq_ref[...], dqn)
            dgq_ref[...] += dgq
            dqb = dq.astype(BF16)
            dcq_ref[:, sl] = dqb
            dhq = dhq + _dot(dqb, wq_ref[:, sl], NT)
        dxn, dgx = _rms_bwd(x1_ref[...], gx_ref[...], dhq)
        dgx_ref[...] += dgx
        dx1_ref[...] = dx2 + dxn

        @pl.when(i == nq - 1)
        def _():
            for h in range(XATTN_HEADS):
                sl = slice(h * hd, (h + 1) * hd)
                dk, dgk = _rms_bwd(kv_ref[:, sl], gk_ref[...], dk_acc[:, sl])
                dgk_ref[...] += dgk
                dkv_ref[:, sl] = dk.astype(BF16)
                dkv_ref[:, XATTN_WIDTH + h * hd:XATTN_WIDTH + (h + 1) * hd] = dv_acc[:, sl].astype(BF16)

    row = lambda b, i: (b * nq + i, 0)
    fixed = lambda b, i: (0, 0)
    t_len = n_batch * s_len
    return pl.pallas_call(
        body, name="xattn_bwd", grid=(n_batch, nq),
        in_specs=[pl.BlockSpec((tq, d), row), pl.BlockSpec((tq, XATTN_WIDTH), row), pl.BlockSpec((m_len, 2 * XATTN_WIDTH), lambda b, i: (b, 0)),
                  pl.BlockSpec((tq, d), row), pl.BlockSpec((1, hd), fixed), pl.BlockSpec((1, hd), fixed),
                  pl.BlockSpec((XATTN_WIDTH, d), fixed), pl.BlockSpec((1, d), fixed), pl.BlockSpec((d, XATTN_WIDTH), fixed)],
        out_specs=[pl.BlockSpec((tq, d), row), pl.BlockSpec((tq, XATTN_WIDTH), row), pl.BlockSpec((m_len, 2 * XATTN_WIDTH), lambda b, i: (b, 0)),
                   pl.BlockSpec((1, hd), fixed), pl.BlockSpec((1, hd), fixed), pl.BlockSpec((1, d), fixed)],
        out_shape=[jax.ShapeDtypeStruct((t_len, d), F32), jax.ShapeDtypeStruct((t_len, XATTN_WIDTH), BF16),
                   jax.ShapeDtypeStruct((n_batch * m_len, 2 * XATTN_WIDTH), BF16),
                   jax.ShapeDtypeStruct((1, hd), F32), jax.ShapeDtypeStruct((1, hd), F32), jax.ShapeDtypeStruct((1, d), F32)],
        scratch_shapes=[pltpu.VMEM((m_len, XATTN_WIDTH), F32), pltpu.VMEM((m_len, XATTN_WIDTH), F32)],
        compiler_params=_cparams(("arbitrary", "arbitrary")),
    )(dx2, cq, ckv, x1, gq, gk, w_co, g_x, w_cq)


def _resident(shape):
    return pl.BlockSpec(shape, lambda *_: (0,) * len(shape), pipeline_mode=pl.Buffered(1))


def _mlp_fwd(hf, x2, target, w1, w2, tm=256, tf=1024):
    t_len, d = x2.shape
    f = w1.shape[1]
    tm, tf = min(tm, t_len), min(tf, f)

    def body(hf_ref, x2_ref, tg_ref, w1_ref, w2_ref, u_ref, a_ref, dy_ref, ls_ref):
        hf_t = hf_ref[...]
        y = x2_ref[...]
        for k in range(f // tf):
            cols = slice(k * tf, (k + 1) * tf)
            u = _dot(hf_t, w1_ref[:, cols])
            u_ref[:, cols] = u
            r = jnp.maximum(u, 0.0)
            a = (r * r).astype(BF16)
            a_ref[:, cols] = a
            y = y + _dot(a, w2_ref[cols, :])
        err = y - tg_ref[...]
        dy_ref[...] = err * (1.0 / d)
        ls_ref[...] = jnp.broadcast_to(jnp.sum(jnp.sum(err * err, axis=-1, keepdims=True) * (1.0 / d), axis=0, keepdims=True), ls_ref.shape)

    row = lambda i: (i, 0)
    return pl.pallas_call(
        body, name="mlp_fwd", grid=(t_len // tm,),
        in_specs=[pl.BlockSpec((tm, d), row), pl.BlockSpec((tm, d), row), pl.BlockSpec((tm, d), row), _resident((d, f)), _resident((f, d))],
        out_specs=[pl.BlockSpec((tm, f), row), pl.BlockSpec((tm, f), row), pl.BlockSpec((tm, d), row),
                   pl.BlockSpec((1, 8, LANES), lambda i: (i, 0, 0))],
        out_shape=[jax.ShapeDtypeStruct((t_len, f), F32), jax.ShapeDtypeStruct((t_len, f), BF16), jax.ShapeDtypeStruct((t_len, d), F32),
                   jax.ShapeDtypeStruct((t_len // tm, 8, LANES), F32)],
        compiler_params=_cparams(("parallel",)),
    )(hf, x2, target, w1, w2)


def _mlp_bwd(dy, u, x2, g, w1, w2, tm=256, tf=1024):
    t_len, d = x2.shape
    f = w1.shape[1]
    tm, tf = min(tm, t_len), min(tf, f)

    def body(dy_ref, u_ref, x2_ref, g_ref, w1_ref, w2_ref, du_ref, dx2_ref, dg_ref):
        @pl.when(pl.program_id(0) == 0)
        def _():
            dg_ref[...] = jnp.zeros_like(dg_ref)

        dy_t = dy_ref[...]
        dyb = dy_t.astype(BF16)
        dhf = jnp.zeros((tm, d), F32)
        for k in range(f // tf):
            cols = slice(k * tf, (k + 1) * tf)
            da = _dot(dyb, w2_ref[cols, :], NT)
            du = (da * (2.0 * jnp.maximum(u_ref[:, cols], 0.0))).astype(BF16)
            du_ref[:, cols] = du
            dhf = dhf + _dot(du, w1_ref[:, cols], NT)
        dxn, dg = _rms_bwd(x2_ref[...], g_ref[...], dhf)
        dx2_ref[...] = dy_t + dxn
        dg_ref[...] += dg

    row = lambda i: (i, 0)
    fixed = lambda i: (0, 0)
    return pl.pallas_call(
        body, name="mlp_bwd", grid=(t_len // tm,),
        in_specs=[pl.BlockSpec((tm, d), row), pl.BlockSpec((tm, f), row), pl.BlockSpec((tm, d), row), pl.BlockSpec((1, d), fixed),
                  _resident((d, f)), _resident((f, d))],
        out_specs=[pl.BlockSpec((tm, f), row), pl.BlockSpec((tm, d), row), pl.BlockSpec((1, d), fixed)],
        out_shape=[jax.ShapeDtypeStruct((t_len, f), BF16), jax.ShapeDtypeStruct((t_len, d), F32), jax.ShapeDtypeStruct((1, d), F32)],
        compiler_params=_cparams(("arbitrary",)),
    )(dy, u, x2, g, w1, w2)


def _pad_lanes(v, offset=0, width=LANES):
    return jnp.zeros((1, width), F32).at[:, offset:offset + v.shape[1]].set(v)


def _col(v, offset=0, rows=SM_ROWS):
    return jnp.zeros((rows, 1), F32).at[offset:offset + v.shape[1], 0].set(v[0])


def _pack_small(g_mix, dgq, dgk, dbias, dgo, dac, dar, ddc, ddr, g_gdn_o, g_nx, g_mem, g_xq, g_xk, g_mlp, loss_tiles):
    def body(mix_ref, q_ref, k_ref, b_ref, o_ref, ac_ref, ar_ref, dc_ref, dr_ref, go_ref, nx_ref, mem_ref, xq_ref, xk_ref,
             mlp_ref, lt_ref, out_ref):
        lane = lax.broadcasted_iota(jnp.int32, (1, LANES), 1)
        diag = lax.broadcasted_iota(jnp.int32, (SM_ROWS, LANES), 0) == lax.broadcasted_iota(jnp.int32, (SM_ROWS, LANES), 1)

        def rolled(v, shift):
            return pltpu.roll(jnp.broadcast_to(v, (8, LANES)), shift, 1)[0:1, :]

        def rows_to_lanes(col):
            return jnp.sum(jnp.where(diag, col, 0.0), axis=0, keepdims=True)

        def put(row, v, n):
            out_ref[row:row + 1, 0:LANES] = jnp.where(lane < n, v, 0.0)

        out_ref[...] = jnp.zeros_like(out_ref)
        out_ref[0:1, :] = mix_ref[...]
        for row, ref in ((1, q_ref), (2, k_ref), (4, o_ref)):
            put(row, ref[...] + rolled(ref[...], FOX_HEAD_DIM), FOX_HEAD_DIM)
        put(3, rows_to_lanes(b_ref[...]), FOX_HEADS)
        for row, lane_ref, row_ref in ((5, ac_ref, ar_ref), (6, dc_ref, dr_ref)):
            put(row, rolled(lane_ref[...] + rows_to_lanes(row_ref[...]), LANES - SM_A), GDN_HEADS)
        put(7, go_ref[...], LANES)
        out_ref[8:9, :] = nx_ref[...]
        out_ref[9:10, :] = mem_ref[...]
        put(10, xq_ref[...], LANES)
        put(11, xk_ref[...], LANES)
        out_ref[12:13, :] = mlp_ref[...]
        put(LOSS_ROW, 0.5 * jnp.sum(lt_ref[...], axis=0)[0:1, :], 1)

    args = (g_mix, dgq, dgk, dbias, dgo, dac, dar, ddc, ddr, g_gdn_o, g_nx, g_mem, g_xq, g_xk, g_mlp, loss_tiles)
    return pl.pallas_call(body, name="pack_small", out_shape=jax.ShapeDtypeStruct((PACK_ROWS, D_MODEL), F32))(*args)


LATE_WEIGHTS = ("w_out", "w_cq", "w_ckv", "w_co", "w_mlp1", "w_mlp2")
GRAD_GROUPS = (("w_mlp2", "w_mlp1"), ("w_co", "w_cq", "w_ckv", "w_out"), ("w_in", "gdn_conv_w"))


def _local_step(x, mem, target, norm_mix_g, w_in, fox_qnorm_g, fox_knorm_g, fox_f_bias, fox_onorm_g, gdn_conv_w, gdn_A_log,
                gdn_dt_bias, gdn_onorm_g, norm_xattn_g, mem_norm_g, xattn_qnorm_g, xattn_knorm_g, norm_mlp_g,
                late_weights, grads_ready=None, first_token=0.0):
    if grads_ready is None:
        grads_ready = lambda group: 0.0
    n_batch, s_len, d = x.shape
    m_len = mem.shape[1]
    t_len = n_batch * s_len
    tq = min(FOX_BLOCK, s_len)
    nq = s_len // tq
    n_chunks = s_len // GDN_CHUNK
    x2d = x.reshape(t_len, d)

    wp = jnp.concatenate([w_in[0:1536], w_in[1544:3080], w_in[3088:3600], w_in[1536:1544], w_in[3080:3088],
                          jnp.zeros((P_DIM - 3600, d), BF16)], axis=0)
    wst = jnp.concatenate([w_in[1536:1544], w_in[3080:3088]], axis=0)
    conv_w = jnp.concatenate([gdn_conv_w, jnp.zeros((8 - CONV_WIDTH, gdn_conv_w.shape[1]), F32)], axis=0)
    bias_col = _col(fox_f_bias, SM_F)
    gq2, gk2, go2 = (jnp.tile(g, (1, 2)) for g in (fox_qnorm_g, fox_knorm_g, fox_onorm_g))
    a_c, dt_c = _pad_lanes(gdn_A_log, SM_A), _pad_lanes(gdn_dt_bias, SM_A)
    a_r, dt_r = _col(gdn_A_log, SM_A), _col(gdn_dt_bias, SM_A)

    h1, pfox, pgdn, pz, sm, smt = _in_proj(x2d, norm_mix_g + first_token, wp, wst)
    c_rows = _fox_cum(smt, bias_col, n_batch, s_len)
    cb = c_rows.reshape(SM_ROWS, n_batch, nq, tq).transpose(1, 2, 0, 3)
    pf3 = pfox.reshape(n_batch, s_len, 1536)
    o_fox, oa, lse = _fox_fwd(pf3, cb, gq2, gk2, go2, tq)
    pg3 = pgdn.reshape(n_batch, s_len, 1536)
    qkvn = _gdn_pre(pg3, conv_w)
    z3 = pz.reshape(n_batch, s_len, GDN_WIDTH)
    smc = sm.reshape(n_batch, s_len, LANES)
    smr = smt.reshape(SM_ROWS, n_batch * n_chunks, GDN_CHUNK).transpose(1, 0, 2)
    ob, states = _gdn_fwd(qkvn, z3, smc, smr, a_c, dt_c, a_r, dt_r, gdn_onorm_g)
    oa2, ob2 = oa.reshape(t_len, FOX_WIDTH), ob.reshape(t_len, GDN_WIDTH)
    late = late_weights(ob2)
    w_out, w_cq, w_ckv, w_co, w_mlp1, w_mlp2 = (late[k] for k in LATE_WEIGHTS)
    x1, hq, cq = _out_proj(x2d, oa2, ob2, w_out, norm_xattn_g, w_cq)
    mem2d = mem.reshape(n_batch * m_len, d)
    hm, ckv = _mem_kv(mem2d, mem_norm_g, w_ckv)
    co, x2, hf = _xattn_fwd(cq, ckv, x1, xattn_qnorm_g, xattn_knorm_g, w_co, norm_mlp_g, n_batch, s_len, m_len)
    u, a_act, dy, loss_tiles = _mlp_fwd(hf, x2, target.reshape(t_len, d), w_mlp1, w_mlp2)

    grads = {}
    du, dx2, grads["norm_mlp_g"] = _mlp_bwd(dy, u, x2, norm_mlp_g, w_mlp1, w_mlp2)
    grads["w_mlp2"] = _wgrad(a_act, dy, "wgrad_mlp2")
    grads["w_mlp1"] = _wgrad(hf, du, "wgrad_mlp1", column_blocks=D_FF // N_DEV)
    token = grads_ready({k: grads[k] for k in GRAD_GROUPS[0]})
    grads["w_co"] = _wgrad(co, dx2, "wgrad_co", column_blocks=D_MODEL // N_DEV)
    dx1, dcq, dckv, grads["xattn_qnorm_g"], grads["xattn_knorm_g"], grads["norm_xattn_g"] = _xattn_bwd(
        dx2, cq, ckv, x1, xattn_qnorm_g + token, xattn_knorm_g, w_co, norm_xattn_g, w_cq, n_batch, s_len, m_len)
    grads["w_cq"] = _wgrad(hq, dcq, "wgrad_cq")
    grads["w_ckv"] = _wgrad(hm, dckv, "wgrad_ckv")
    grads["mem_norm_g"] = _mem_kv_bwd(dckv, mem2d, mem_norm_g, w_ckv)
    grads["w_out"] = _wgrad(jnp.concatenate([oa2, ob2], axis=1), dx1, "wgrad_out")
    token = grads_ready({k: grads[k] for k in GRAD_GROUPS[1]})
    dcat = _out_proj_bwd(dx1, w_out)
    dcat3 = dcat.reshape(n_batch, s_len, d)

    dqkvn, dz, dsmc, dsmr, dac, ddc, dar, ddr, grads["gdn_onorm_g"] = _gdn_bwd(
        qkvn, z3, smc, smr, a_c, dt_c, a_r, dt_r, gdn_onorm_g + token, states, dcat3)
    dpg, dconv = _gdn_pre_bwd(pg3, conv_w, dqkvn)
    grads["gdn_conv_w"] = dconv[0:CONV_WIDTH]

    dq, dk, dv, dcb, dgq, dgk, dgo = _fox_bwd(pf3, cb, gq2, gk2, go2, o_fox, lse, dcat3[:, :, 0:FOX_WIDTH], tq)
    dc8 = dcb[:, :, :, 0:2, :].transpose(1, 3, 0, 2, 4).reshape(FOX_HEADS, t_len)
    dc_rows = jnp.concatenate([dc8, jnp.zeros((SM_ROWS - FOX_HEADS, t_len), F32)], axis=0)
    dl_rows, dbias = _fox_cum_bwd(dc_rows, smt, bias_col, n_batch, s_len)
    dsm_rows = jnp.concatenate([dl_rows[0:SM_B], dsmr.transpose(1, 0, 2).reshape(SM_ROWS, t_len)[SM_B:SM_ROWS]], axis=0)

    dproj = jnp.concatenate([dq.reshape(t_len, FOX_WIDTH), dk.reshape(t_len, FOX_WIDTH), dv.reshape(t_len, FOX_WIDTH),
                             dpg.reshape(t_len, 1536), dz.reshape(t_len, GDN_WIDTH), dsmc.reshape(t_len, LANES).astype(BF16)], axis=1)
    dwp = _wgrad(dproj, h1, "wgrad_in", bk=P_DIM, bn=512)
    dwst = _rows_matmul(dsm_rows, h1, "wgrad_in_rows")
    dw_small = dwp[P_SMALL:P_SMALL + SM_ROWS] + dwst
    grads["w_in"] = jnp.concatenate([dwp[0:1536], dw_small[0:8], dwp[1536:3072], dw_small[8:16], dwp[3072:3584]], axis=0)
    token = grads_ready({k: grads[k] for k in GRAD_GROUPS[2]})
    grad_x, grads["norm_mix_g"] = _in_proj_bwd(dproj, dsm_rows, x2d, norm_mix_g + token, wp, wst, dx1)
    packed = _pack_small(grads["norm_mix_g"], dgq, dgk, dbias, dgo, dac, dar, ddc, ddr, grads["gdn_onorm_g"], grads["norm_xattn_g"],
                         grads["mem_norm_g"], grads["xattn_qnorm_g"], grads["xattn_knorm_g"], grads["norm_mlp_g"], loss_tiles)
    return packed, grad_x.reshape(n_batch, s_len, d), {k: grads[k] for k in SHARDED}


MESH_ID = pl.DeviceIdType.MESH
ANY_SPEC = pl.BlockSpec(memory_space=pl.ANY)


def _place():
    x, y, c = lax.axis_index("x"), lax.axis_index("y"), lax.axis_index("c")
    return x, y, c, [(1 - x, y), (x, 1 - y), (1 - x, 1 - y)]


def _place_own(src_ref, dst_ref):
    def staged(buf, sem):
        for a, b in ((src_ref, buf), (buf, dst_ref)):
            cp = pltpu.make_async_copy(a, b, sem)
            cp.start()
            cp.wait()

    pl.run_scoped(staged, pltpu.VMEM(src_ref.shape, src_ref.dtype), pltpu.SemaphoreType.DMA)


def _all_gather_body(n, ins, outs, send_sems, recv_sems, local_sems):
    x, y, c, chips = _place()
    me, sibling = (x, y, c), (x, y, 1 - c)

    def copy(a, k, block, to, src=None):
        dst = outs[a].at[4 * block[0] + 2 * block[1] + block[2]]
        return pltpu.make_async_remote_copy(src_ref=dst if src is None else src, dst_ref=dst, send_sem=send_sems.at[a, k],
                                            recv_sem=recv_sems.at[a, k], device_id=to, device_id_type=MESH_ID)

    mine = [] if local_sems is None else [pltpu.make_async_copy(ins[a], outs[a].at[4 * x + 2 * y + c], local_sems.at[a]) for a in range(n)]
    for cp in mine:
        cp.start()
    first = []
    for a in range(n):
        first.append(copy(a, 0, me, sibling, src=ins[a]))
        first += [copy(a, 1 + j, me, (*chip, c), src=ins[a]) for j, chip in enumerate(chips)]
    for cp in first:
        cp.start()
    if local_sems is None:
        for a in range(n):
            _place_own(ins[a], outs[a].at[4 * x + 2 * y + c])
    passed = []
    for j, chip in enumerate(chips):
        for a in range(n):
            copy(a, 1 + j, (*chip, c), me).wait_recv()
            fwd = copy(a, 4 + j, (*chip, c), sibling)
            fwd.start()
            passed.append(fwd)
    for a in range(n):
        copy(a, 0, sibling, me).wait_recv()
        for j, chip in enumerate(chips):
            copy(a, 4 + j, (*chip, 1 - c), me).wait_recv()
    for cp in first + passed:
        cp.wait_send()
    for cp in mine:
        cp.wait()


def _all_gather_hbm(arrs, name):
    n = len(arrs)

    def body(*refs):
        _all_gather_body(n, refs[:n], refs[n:2 * n], refs[2 * n], refs[2 * n + 1], None)

    return pl.pallas_call(
        body, name=name, in_specs=[ANY_SPEC] * n, out_specs=[ANY_SPEC] * n,
        out_shape=[jax.ShapeDtypeStruct((N_DEV,) + a.shape, a.dtype) for a in arrs],
        scratch_shapes=[pltpu.SemaphoreType.DMA((n, 7)), pltpu.SemaphoreType.DMA((n, 7))],
        compiler_params=pltpu.CompilerParams(vmem_limit_bytes=VMEM_LIMIT),
    )(*arrs)


def _pair_exchange(arrs, name):
    n = len(arrs)

    def body(*refs):
        ins, outs = refs[:n], refs[n:2 * n]
        send_sems, recv_sems = refs[2 * n:]
        x, y, c, _ = _place()
        copies = []
        for a in range(n):
            for chip in range(4):
                copies.append(pltpu.make_async_remote_copy(
                    src_ref=ins[a].at[2 * chip + (1 - c)], dst_ref=outs[a].at[chip], send_sem=send_sems.at[a, chip],
                    recv_sem=recv_sems.at[a, chip], device_id=(x, y, 1 - c), device_id_type=MESH_ID))
        for cp in copies:
            cp.start()
        for cp in copies:
            cp.wait()

    return pl.pallas_call(
        body, name=name, in_specs=[ANY_SPEC] * n, out_specs=[ANY_SPEC] * n,
        out_shape=[jax.ShapeDtypeStruct((4,) + a.shape[1:], a.dtype) for a in arrs],
        scratch_shapes=[pltpu.SemaphoreType.DMA((n, 4)), pltpu.SemaphoreType.DMA((n, 4))],
    )(*arrs)


HBM_SPEC = pl.BlockSpec(memory_space=pltpu.HBM)
SEM_SPEC = pl.BlockSpec(memory_space=pltpu.SEMAPHORE)
DATAFLOW = pltpu.SideEffectType.DATAFLOW_SIDE_EFFECTING


def _in_hbm(arrs):
    return [pltpu.with_memory_space_constraint(a, pltpu.HBM) for a in arrs]


def _copies_start(name, srcs, lands, make_copies, after):
    n = len(srcs)
    n_copies = len(make_copies(srcs, lands, None, None)[0])

    def body(*refs):
        send_sems, recv_sems = refs[2 * n + 1], refs[2 * n + 2]
        for row in make_copies(refs[:n], refs[n:2 * n], send_sems, recv_sems):
            for cp in row:
                cp.start()
        refs[-1][...] = jnp.zeros_like(refs[-1])

    sems = pltpu.SemaphoreType.DMA((n * n_copies,))
    thru = [pltpu.HBM(a.shape, a.dtype) for a in list(srcs) + list(lands)]
    res = pl.pallas_call(
        body, name=name, in_specs=[HBM_SPEC] * (2 * n) + [ANY_SPEC],
        out_specs=(SEM_SPEC, SEM_SPEC, *[HBM_SPEC] * (2 * n), pl.BlockSpec(memory_space=pltpu.VMEM)),
        out_shape=(sems, sems, *thru, jax.ShapeDtypeStruct((8, LANES), F32)),
        input_output_aliases={i: 2 + i for i in range(2 * n)},
        compiler_params=pltpu.CompilerParams(has_side_effects=DATAFLOW),
    )(*_in_hbm(list(srcs) + list(lands)), after)
    return res[0], res[1], list(res[2:2 + n]), list(res[2 + n:2 + 2 * n]), res[-1]


def _copies_wait(name, send_sems, recv_sems, srcs, lands, after, make_copies, own_block=False):
    n = len(srcs)

    def body(*refs):
        if own_block:
            me = 4 * lax.axis_index("x") + 2 * lax.axis_index("y") + lax.axis_index("c")
            for a in range(n):
                _place_own(refs[a], refs[3 * n + 3 + a].at[me])
        for row in make_copies(refs[:n], refs[n:2 * n], refs[2 * n], refs[2 * n + 1]):
            for cp in row:
                cp.wait_send()
                cp.wait_recv()

    res = pl.pallas_call(
        body, name=name, in_specs=[HBM_SPEC] * (2 * n) + [SEM_SPEC, SEM_SPEC, ANY_SPEC],
        out_specs=tuple([HBM_SPEC] * (2 * n)),
        out_shape=tuple(pltpu.HBM(a.shape, a.dtype) for a in list(srcs) + list(lands)),
        input_output_aliases={i: i for i in range(2 * n)},
        compiler_params=pltpu.CompilerParams(has_side_effects=DATAFLOW, vmem_limit_bytes=VMEM_LIMIT),
    )(*srcs, *lands, send_sems, recv_sems, after)
    return list(res[:n]), list(res[n:])


def _gather_copies(srcs, lands, send_sems, recv_sems):
    if send_sems is None:
        return [[None] * 7]
    x, y, c, _ = _place()
    rows = []
    for a in range(len(srcs)):
        row = []
        for k in range(7):
            r = k + 1
            to = (1 - x if r & 4 else x, 1 - y if r & 2 else y, 1 - c if r & 1 else c)
            row.append(pltpu.make_async_remote_copy(
                src_ref=srcs[a], dst_ref=lands[a].at[4 * x + 2 * y + c], send_sem=send_sems.at[7 * a + k], recv_sem=recv_sems.at[7 * a + k],
                device_id=to, device_id_type=MESH_ID))
        rows.append(row)
    return rows


def _scatter_copies(srcs, lands, send_sems, recv_sems):
    if send_sems is None:
        return [[None] * 7]
    x, y, c, _ = _place()
    rows = []
    for a in range(len(srcs)):
        row = []
        for k in range(7):
            r = k + 1
            to = (1 - x if r & 4 else x, 1 - y if r & 2 else y, 1 - c if r & 1 else c)
            row.append(pltpu.make_async_remote_copy(
                src_ref=srcs[a].at[4 * to[0] + 2 * to[1] + to[2]], dst_ref=lands[a].at[k], send_sem=send_sems.at[7 * a + k],
                recv_sem=recv_sems.at[7 * a + k], device_id=to, device_id_type=MESH_ID))
        rows.append(row)
    return rows


def _chip_copies(srcs, lands, send_sems, recv_sems):
    if send_sems is None:
        return [[None] * 3]
    x, y, c, chips = _place()
    return [[pltpu.make_async_remote_copy(
        src_ref=srcs[a].at[2 * chip[0] + chip[1]], dst_ref=lands[a].at[j], send_sem=send_sems.at[3 * a + j], recv_sem=recv_sems.at[3 * a + j],
        device_id=(*chip, c), device_id_type=MESH_ID) for j, chip in enumerate(chips)] for a in range(len(srcs))]


def _all_gather_vmem(block, name):
    def body(in_ref, out_ref, send_sems, recv_sems, local_sems):
        _all_gather_body(1, [in_ref], [out_ref], send_sems, recv_sems, local_sems)

    vmem = pl.BlockSpec(memory_space=pltpu.VMEM)
    return pl.pallas_call(
        body, name=name, in_specs=[vmem], out_specs=vmem,
        out_shape=jax.ShapeDtypeStruct((N_DEV,) + block.shape, block.dtype),
        scratch_shapes=[pltpu.SemaphoreType.DMA((1, 7)), pltpu.SemaphoreType.DMA((1, 7)), pltpu.SemaphoreType.DMA((1,))],
    )(block)


def _tile(rows, cols):
    if rows <= 256:
        return rows, cols
    tr = 256 if cols <= 512 else 128
    if rows % tr == 0:
        return tr, cols
    return rows, 256


def _pair_sum(core, own, got, name):
    _, rows, cols = own.shape
    tr, tc = _tile(rows, cols)

    def body(c_ref, own_ref, got_ref, o_ref):
        o_ref[0] = own_ref[0] + got_ref[0]

    return pl.pallas_call(
        body, name=name,
        grid_spec=pltpu.PrefetchScalarGridSpec(
            num_scalar_prefetch=1, grid=(4, rows // tr, cols // tc),
            in_specs=[pl.BlockSpec((1, tr, tc), lambda k, i, j, c: (2 * k + c[0], i, j)),
                      pl.BlockSpec((1, tr, tc), lambda k, i, j, c: (k, i, j))],
            out_specs=pl.BlockSpec((1, tr, tc), lambda k, i, j, c: (k, i, j))),
        out_shape=jax.ShapeDtypeStruct((4, rows, cols), F32),
        compiler_params=_cparams(("parallel", "parallel", "parallel")),
    )(core, own, got)


def _adamw(w, g, m, v):
    m_new = ADAM_B1 * m + (1.0 - ADAM_B1) * g
    v_new = ADAM_B2 * v + (1.0 - ADAM_B2) * (g * g)
    m_hat = m_new / (1.0 - ADAM_B1 ** ADAM_STEP)
    v_hat = v_new / (1.0 - ADAM_B2 ** ADAM_STEP)
    delta = -ADAM_LR * (m_hat / (jnp.sqrt(v_hat) + ADAM_EPS) + ADAM_WD * w)
    return delta, m_new, v_new


def _sum_adam(chip, sums, parts, w, m, v, name):
    n_parts, rows, cols = parts.shape
    tr, tc = _tile(rows, cols)

    def body(chip_ref, own_ref, p_ref, w_ref, m_ref, v_ref, g_ref, d_ref, mo_ref, vo_ref):
        g = own_ref[0]
        for k in range(n_parts):
            g = g + p_ref[k]
        g_ref[...] = g
        d_ref[...], mo_ref[...], vo_ref[...] = _adamw(w_ref[...], g, m_ref[...], v_ref[...])

    tile = pl.BlockSpec((tr, tc), lambda i, j, ch: (i, j))
    out = jax.ShapeDtypeStruct((rows, cols), F32)
    return pl.pallas_call(
        body, name=name,
        grid_spec=pltpu.PrefetchScalarGridSpec(
            num_scalar_prefetch=1, grid=(rows // tr, cols // tc),
            in_specs=[pl.BlockSpec((1, tr, tc), lambda i, j, ch: (ch[0], i, j)),
                      pl.BlockSpec((n_parts, tr, tc), lambda i, j, ch: (0, i, j)), tile, tile, tile],
            out_specs=[tile, tile, tile, tile]),
        out_shape=[out, out, out, out],
        compiler_params=_cparams(("parallel", "parallel")),
    )(chip, sums, parts, w, m, v)


SHARDED = ("w_in", "gdn_conv_w", "w_out", "w_cq", "w_ckv", "w_co", "w_mlp1", "w_mlp2")
TRANSPOSED = ("w_in",)
COLUMN_SHARDED = ("gdn_conv_w", "w_co", "w_mlp1")
REPLICATED = ("norm_mix_g", "fox_qnorm_g", "fox_knorm_g", "fox_f_bias", "fox_onorm_g", "gdn_A_log", "gdn_dt_bias", "gdn_onorm_g",
              "norm_xattn_g", "mem_norm_g", "xattn_qnorm_g", "xattn_knorm_g", "norm_mlp_g")
WEIGHTS = ("norm_mix_g", "w_in", "fox_qnorm_g", "fox_knorm_g", "fox_f_bias", "fox_onorm_g", "gdn_conv_w", "gdn_A_log", "gdn_dt_bias",
           "gdn_onorm_g", "w_out", "norm_xattn_g", "mem_norm_g", "w_cq", "w_ckv", "xattn_qnorm_g", "xattn_knorm_g", "w_co",
           "norm_mlp_g", "w_mlp1", "w_mlp2")
PACK_ROWS = 16
LOSS_ROW = len(REPLICATED)


def _whole(name, gathered):
    if name in COLUMN_SHARDED:
        return gathered.transpose(1, 0, 2).reshape(gathered.shape[1], N_DEV * gathered.shape[2])
    return gathered.reshape(N_DEV * gathered.shape[1], gathered.shape[2])


def _blocks(name, whole):
    if whole.ndim == 3:
        return whole
    if name in COLUMN_SHARDED:
        rows, cols = whole.shape
        return whole.reshape(rows, N_DEV, cols // N_DEV).transpose(1, 0, 2)
    return whole.reshape(N_DEV, whole.shape[0] // N_DEV, whole.shape[1])


def _adam_small(everyone, ws, ms, vs):
    n_par = len(ws)

    def body(*refs):
        ev_ref = refs[0]
        w_refs, m_refs, v_refs = (refs[1 + j * n_par:1 + (j + 1) * n_par] for j in range(3))
        outs = refs[1 + 3 * n_par:-1]
        sum_ref = refs[-1]
        total = ev_ref[0]
        for dev in range(1, N_DEV):
            total = total + ev_ref[dev]
        sum_ref[...] = total
        for i in range(n_par):
            n = w_refs[i].shape[1]
            g = sum_ref[i:i + 1, 0:n]
            outs[4 * i][...] = g
            outs[4 * i + 1][...], outs[4 * i + 2][...], outs[4 * i + 3][...] = _adamw(w_refs[i][...], g, m_refs[i][...], v_refs[i][...])
        outs[4 * n_par][...] = sum_ref[LOSS_ROW:LOSS_ROW + 1, 0:1]

    shapes = [jax.ShapeDtypeStruct(a.shape, F32) for a in ws for _ in range(4)] + [jax.ShapeDtypeStruct((1, 1), F32)]
    return pl.pallas_call(body, name="adam_small", out_shape=shapes,
                          scratch_shapes=[pltpu.VMEM((PACK_ROWS, D_MODEL), F32)])(everyone, *ws, *ms, *vs)


def kernel(x, mem, norm_mix_g, w_in, fox_qnorm_g, fox_knorm_g, fox_f_bias, fox_onorm_g, gdn_conv_w, gdn_A_log, gdn_dt_bias, gdn_onorm_g, w_out, norm_xattn_g, mem_norm_g, w_cq, w_ckv, xattn_qnorm_g, xattn_knorm_g, w_co, norm_mlp_g, w_mlp1, w_mlp2, loss_target, m_norm_mix_g, m_w_in, m_fox_qnorm_g, m_fox_knorm_g, m_fox_f_bias, m_fox_onorm_g, m_gdn_conv_w, m_gdn_A_log, m_gdn_dt_bias, m_gdn_onorm_g, m_w_out, m_norm_xattn_g, m_mem_norm_g, m_w_cq, m_w_ckv, m_xattn_qnorm_g, m_xattn_knorm_g, m_w_co, m_norm_mlp_g, m_w_mlp1, m_w_mlp2, v_norm_mix_g, v_w_in, v_fox_qnorm_g, v_fox_knorm_g, v_fox_f_bias, v_fox_onorm_g, v_gdn_conv_w, v_gdn_A_log, v_gdn_dt_bias, v_gdn_onorm_g, v_w_out, v_norm_xattn_g, v_mem_norm_g, v_w_cq, v_w_ckv, v_xattn_qnorm_g, v_xattn_knorm_g, v_w_co, v_norm_mlp_g, v_w_mlp1, v_w_mlp2):
    given = dict(locals())
    w = {k: given[k] for k in WEIGHTS}
    m = {k: given["m_" + k] for k in WEIGHTS}
    v = {k: given["v_" + k] for k in WEIGHTS}

    core = lax.axis_index("c").astype(jnp.int32).reshape(1)
    chip = (2 * lax.axis_index("x") + lax.axis_index("y")).astype(jnp.int32).reshape(1)
    me = 4 * lax.axis_index("x") + 2 * lax.axis_index("y") + lax.axis_index("c")

    local = lambda d: {k: jnp.transpose(d[k][0]) if k in TRANSPOSED else d[k][0] for k in SHARDED}
    w2, m2, v2 = local(w), local(m), local(v)
    shards = {k: w2[k] if k == "gdn_conv_w" else w2[k].astype(BF16) for k in SHARDED}
    early = [k for k in SHARDED if k not in LATE_WEIGHTS]
    gathered = _all_gather_hbm([shards[k] for k in early], "gather_early")
    whole = {k: _whole(k, g) for k, g in zip(early, gathered)}
    late_shards = [shards[k] for k in LATE_WEIGHTS]
    late_lands = [lax.empty((N_DEV,) + s.shape, s.dtype) for s in late_shards]
    gather = _copies_start("gather_late_start", late_shards, late_lands, _gather_copies, after=gathered[0])

    def late_weights(after):
        _, lands = _copies_wait("gather_late_wait", gather[0], gather[1], gather[2], gather[3], after, _gather_copies, own_block=True)
        return {k: _whole(k, land) for k, land in zip(LATE_WEIGHTS, lands)}

    pending = []

    def grads_ready(group):
        names = list(group)
        tag = str(len(pending))
        own = [_blocks(k, group[k]) for k in names]
        if "w_in" in names:
            got = _pair_exchange(own, "grad_pair_exchange_" + tag)
            srcs = [_pair_sum(core, o, g, "grad_pair_sum_" + k) for k, o, g in zip(names, own, got)]
            copies, index, n_parts = _chip_copies, chip, 3
        else:
            srcs, copies, index, n_parts = own, _scatter_copies, me.astype(jnp.int32).reshape(1), 7
        lands = [lax.empty((n_parts,) + s.shape[1:], s.dtype) for s in srcs]
        started = _copies_start("grad_exchange_start_" + tag, srcs, lands, copies, after=srcs[0])
        pending.append((names, started, copies, index))
        return started[4][0, 0]

    small = {k: w[k] for k in REPLICATED}
    packed, grad_x, _ = _local_step(x, mem, loss_target, **small, **whole, late_weights=late_weights,
                                    grads_ready=grads_ready, first_token=gather[4][0, 0])

    out_g, out_d, out_m, out_v = {}, {}, {}, {}
    after = grad_x
    for tag, (names, started, copies, index) in enumerate(pending):
        srcs, parts = _copies_wait("grad_exchange_wait_" + str(tag), started[0], started[1], started[2], started[3], after, copies)
        for k, s, p in zip(names, srcs, parts):
            res = _sum_adam(index, s, p, w2[k], m2[k], v2[k], "adam_" + k)
            out_g[k], out_d[k], out_m[k], out_v[k] = ((jnp.transpose(r) if k in TRANSPOSED else r)[None] for r in res)
            after = res[0]

    everyone = _all_gather_vmem(packed, "gather_small")
    res = _adam_small(everyone, [w[k] for k in REPLICATED], [m[k] for k in REPLICATED], [v[k] for k in REPLICATED])
    for i, k in enumerate(REPLICATED):
        out_g[k], out_d[k], out_m[k], out_v[k] = res[4 * i:4 * i + 4]
    loss = res[-1].reshape(())

    return (loss, grad_x, *[out_g[k] for k in WEIGHTS], *[out_d[k] for k in WEIGHTS], *[out_m[k] for k in WEIGHTS],
            *[out_v[k] for k in WEIGHTS])
```

```python
import functools

import jax
import jax.numpy as jnp
import numpy as np
from jax import lax
from jax.experimental import pallas as pl
from jax.experimental.pallas import tpu as pltpu

F32 = jnp.float32
BF16 = jnp.bfloat16

D_MODEL = 1024
FOX_HEADS = 8
FOX_HEAD_DIM = 64
FOX_WIDTH = 512
GDN_HEADS = 4
GDN_HEAD_DIM = 128
GDN_WIDTH = 512
CONV_WIDTH = 4
GDN_CHUNK = 128
GDN_GROUP = 4
FOX_BLOCK = 512
XATTN_HEADS = 4
XATTN_HEAD_DIM = 128
XATTN_WIDTH = 512
D_FF = 4096
EPS = 1e-6
NEG_INF = -1e30
N_DEV = 8

ADAM_LR = 0.001
ADAM_B1 = 0.9
ADAM_B2 = 0.999
ADAM_EPS = 1e-08
ADAM_WD = 0.01
ADAM_STEP = 10

P_FOX = 0
P_GDN = 1536
P_Z = 3072
P_SMALL = 3584
P_DIM = 3712
SM_F = 0
SM_B = 8
SM_A = 12
SM_ROWS = 16

LANES = 128
VMEM_LIMIT = 56 * 1024 * 1024

NN = (((1,), (0,)), ((), ()))
NT = (((1,), (1,)), ((), ()))
TN = (((0,), (0,)), ((), ()))


def _dot(a, b, dims=NN):
    return lax.dot_general(a.astype(BF16), b.astype(BF16), dims, preferred_element_type=F32)


def _cparams(sem=None):
    kw = dict(vmem_limit_bytes=VMEM_LIMIT)
    if sem is not None:
        kw["dimension_semantics"] = sem
    return pltpu.CompilerParams(**kw)


def _sigmoid(x):
    return 0.5 * (jnp.tanh(0.5 * x) + 1.0)


def _softplus(x):
    return jnp.maximum(x, 0.0) + jnp.log1p(jnp.exp(-jnp.abs(x)))


def _log_sigmoid(x):
    return -_softplus(-x)


def _rms(x, g):
    r = lax.rsqrt(jnp.mean(x * x, axis=-1, keepdims=True) + EPS)
    return x * r * g


def _rms_bwd(x, g, dy):
    r = lax.rsqrt(jnp.mean(x * x, axis=-1, keepdims=True) + EPS)
    xh = x * r
    dg = jnp.sum(dy * xh, axis=0, keepdims=True)
    dyg = dy * g
    dx = r * (dyg - xh * jnp.mean(dyg * xh, axis=-1, keepdims=True))
    return dx, dg


def _pair_stat(t, m0):
    s0 = jnp.sum(jnp.where(m0, t, 0.0), axis=-1, keepdims=True)
    s1 = jnp.sum(jnp.where(m0, 0.0, t), axis=-1, keepdims=True)
    return jnp.where(m0, s0, s1)


def _rms_pair(x, g, m0):
    r = lax.rsqrt(_pair_stat(x * x, m0) * (1.0 / FOX_HEAD_DIM) + EPS)
    return x * r * g


def _rms_pair_bwd(x, g, dy, m0):
    r = lax.rsqrt(_pair_stat(x * x, m0) * (1.0 / FOX_HEAD_DIM) + EPS)
    xh = x * r
    dg = jnp.sum(dy * xh, axis=0, keepdims=True)
    dyg = dy * g
    dx = r * (dyg - xh * (_pair_stat(dyg * xh, m0) * (1.0 / FOX_HEAD_DIM)))
    return dx, dg


@jax.custom_vjp
def _mm_nn(a, b):
    return _dot(a, b, NN)


_mm_nn.defvjp(lambda a, b: (_dot(a, b, NN), (a, b)),
              lambda r, g: (_dot(g, r[1], NT), _dot(r[0], g, TN)))


@jax.custom_vjp
def _mm_nt(a, b):
    return _dot(a, b, NT)


_mm_nt.defvjp(lambda a, b: (_dot(a, b, NT), (a, b)),
              lambda r, g: (_dot(g, r[1], NN), _dot(g, r[0], TN)))


@jax.custom_vjp
def _mm_tn(a, b):
    return _dot(a, b, TN)


_mm_tn.defvjp(lambda a, b: (_dot(a, b, TN), (a, b)),
              lambda r, g: (_dot(r[1], g, NT), _dot(r[0], g, NN)))


def _dot3(a, b, dims):
    ah = a.astype(BF16)
    al = (a - ah.astype(F32)).astype(BF16)
    bh = b.astype(BF16)
    bl = (b - bh.astype(F32)).astype(BF16)
    d = functools.partial(lax.dot_general, dimension_numbers=dims, preferred_element_type=F32)
    return d(ah, bh) + d(ah, bl) + d(al, bh)


def _neumann_inverses(mats):
    c = mats[0].shape[0]
    eye = (lax.broadcasted_iota(jnp.int32, (c, c), 0) == lax.broadcasted_iota(jnp.int32, (c, c), 1)).astype(F32)
    xs = [eye - a for a in mats]
    ps = list(mats)
    k = 2
    while k < c + 1:
        ps = [_dot3(p, p, NN) for p in ps]
        xs = [x + _dot3(x, p, NN) for x, p in zip(xs, ps)]
        k *= 2
    return xs


@jax.custom_vjp
def _unit_lower_inverses(mats):
    return _neumann_inverses(mats)


def _unit_lower_inverses_fwd(mats):
    ts = _neumann_inverses(mats)
    return ts, ts


def _unit_lower_inverses_bwd(ts, gs):
    left = [_dot3(t, g, TN) for t, g in zip(ts, gs)]
    return ([-_dot3(m, t, NT) for m, t in zip(left, ts)],)


_unit_lower_inverses.defvjp(_unit_lower_inverses_fwd, _unit_lower_inverses_bwd)


def _wgrad(a, b, name, bk=1024, bn=1024, bt=512, column_blocks=None):
    t_len, k_len = a.shape
    n_len = b.shape[1]
    bk, bn, bt = min(bk, k_len), min(bn, n_len), min(bt, t_len)
    nt = t_len // bt

    def body(a_ref, b_ref, o_ref, acc_ref):
        t = pl.program_id(2)

        @pl.when(t == 0)
        def _():
            acc_ref[...] = jnp.zeros_like(acc_ref)

        acc_ref[...] += _dot(a_ref[...], b_ref[...], TN)

        @pl.when(t == nt - 1)
        def _():
            if column_blocks:
                for jj in range(bn // column_blocks):
                    o_ref[jj] = acc_ref[:, jj * column_blocks:(jj + 1) * column_blocks]
            else:
                o_ref[...] = acc_ref[...]

    if column_blocks:
        out_spec = pl.BlockSpec((bn // column_blocks, bk, column_blocks), lambda i, j, t: (j, i, 0))
        out_shape = jax.ShapeDtypeStruct((n_len // column_blocks, k_len, column_blocks), F32)
    else:
        out_spec = pl.BlockSpec((bk, bn), lambda i, j, t: (i, j))
        out_shape = jax.ShapeDtypeStruct((k_len, n_len), F32)
    return pl.pallas_call(
        body, name=name, grid=(k_len // bk, n_len // bn, nt),
        in_specs=[pl.BlockSpec((bt, bk), lambda i, j, t: (t, i)), pl.BlockSpec((bt, bn), lambda i, j, t: (t, j))],
        out_specs=out_spec, out_shape=out_shape,
        scratch_shapes=[pltpu.VMEM((bk, bn), F32)],
        compiler_params=_cparams(("parallel", "parallel", "arbitrary")),
    )(a, b)


def _rows_matmul(a, b, name, bt=512):
    r_len, t_len = a.shape
    n_len = b.shape[1]
    bt = min(bt, t_len)
    nt = t_len // bt

    def body(a_ref, b_ref, o_ref):
        t = pl.program_id(0)

        @pl.when(t == 0)
        def _():
            o_ref[...] = jnp.zeros_like(o_ref)

        o_ref[...] += _dot(a_ref[...], b_ref[...], NN)

    return pl.pallas_call(
        body, name=name, grid=(nt,),
        in_specs=[pl.BlockSpec((r_len, bt), lambda t: (0, t)), pl.BlockSpec((bt, n_len), lambda t: (t, 0))],
        out_specs=pl.BlockSpec((r_len, n_len), lambda t: (0, 0)),
        out_shape=jax.ShapeDtypeStruct((r_len, n_len), F32),
        compiler_params=_cparams(("arbitrary",)),
    )(a, b)


def _in_proj(x, g, wp, wst, tm=256):
    t_len, d = x.shape
    tm = min(tm, t_len)

    def body(x_ref, g_ref, wp_ref, wst_ref, h_ref, fox_ref, gdn_ref, z_ref, sm_ref, smt_ref):
        h = _rms(x_ref[...], g_ref[...]).astype(BF16)
        h_ref[...] = h
        p = _dot(h, wp_ref[...], NT)
        fox_ref[...] = p[:, P_FOX:P_GDN]
        gdn_ref[...] = p[:, P_GDN:P_Z]
        z_ref[...] = p[:, P_Z:P_SMALL]
        sm_ref[...] = p[:, P_SMALL:P_DIM]
        smt_ref[...] = _dot(wst_ref[...], h, NT)

    row = lambda i: (i, 0)
    fixed = lambda i: (0, 0)
    return pl.pallas_call(
        body, name="in_proj", grid=(t_len // tm,),
        in_specs=[pl.BlockSpec((tm, d), row), pl.BlockSpec((1, d), fixed), pl.BlockSpec((P_DIM, d), fixed),
                  pl.BlockSpec((SM_ROWS, d), fixed)],
        out_specs=[pl.BlockSpec((tm, d), row), pl.BlockSpec((tm, 1536), row), pl.BlockSpec((tm, 1536), row),
                   pl.BlockSpec((tm, 512), row), pl.BlockSpec((tm, LANES), row), pl.BlockSpec((SM_ROWS, tm), lambda i: (0, i))],
        out_shape=[jax.ShapeDtypeStruct((t_len, d), BF16), jax.ShapeDtypeStruct((t_len, 1536), F32),
                   jax.ShapeDtypeStruct((t_len, 1536), F32), jax.ShapeDtypeStruct((t_len, 512), F32),
                   jax.ShapeDtypeStruct((t_len, LANES), F32), jax.ShapeDtypeStruct((SM_ROWS, t_len), F32)],
        compiler_params=_cparams(("parallel",)),
    )(x, g, wp, wst)


def _in_proj_bwd(dproj, dsmt, x, g, wp, wst, dx1, tm=256):
    t_len, d = x.shape
    tm = min(tm, t_len)

    def body(dp_ref, dst_ref, x_ref, g_ref, wp_ref, wst_ref, dx1_ref, dx_ref, dg_ref):
        i = pl.program_id(0)
        dh = _dot(dp_ref[...], wp_ref[...], NN) + _dot(dst_ref[...], wst_ref[...], TN)
        dxn, dg = _rms_bwd(x_ref[...], g_ref[...], dh)
        dx_ref[...] = dx1_ref[...] + dxn

        @pl.when(i == 0)
        def _():
            dg_ref[...] = jnp.zeros_like(dg_ref)

        dg_ref[...] += dg

    row = lambda i: (i, 0)
    fixed = lambda i: (0, 0)
    return pl.pallas_call(
        body, name="in_proj_bwd", grid=(t_len // tm,),
        in_specs=[pl.BlockSpec((tm, P_DIM), row), pl.BlockSpec((SM_ROWS, tm), lambda i: (0, i)), pl.BlockSpec((tm, d), row),
                  pl.BlockSpec((1, d), fixed), pl.BlockSpec((P_DIM, d), fixed), pl.BlockSpec((SM_ROWS, d), fixed),
                  pl.BlockSpec((tm, d), row)],
        out_specs=[pl.BlockSpec((tm, d), row), pl.BlockSpec((1, d), fixed)],
        out_shape=[jax.ShapeDtypeStruct((t_len, d), F32), jax.ShapeDtypeStruct((1, d), F32)],
        compiler_params=_cparams(("arbitrary",)),
    )(dproj, dsmt, x, g, wp, wst, dx1)


def _fox_cum(smt, bias_col, n_batch, s_len, ck=256):
    ck = min(ck, s_len)

    def body(s_ref, b_ref, c_ref):
        tri = (lax.broadcasted_iota(jnp.int32, (ck, ck), 0) <= lax.broadcasted_iota(jnp.int32, (ck, ck), 1)).astype(F32)
        carry = jnp.zeros((SM_ROWS, 1), F32)
        for r in range(s_len // ck):
            ls = _log_sigmoid(s_ref[:, r * ck:(r + 1) * ck] + b_ref[...])
            c = jnp.dot(ls, tri, precision=lax.Precision.HIGHEST, preferred_element_type=F32) + carry
            c_ref[:, r * ck:(r + 1) * ck] = c
            carry = c[:, ck - 1:ck]

    return pl.pallas_call(
        body, name="fox_cum", grid=(n_batch,),
        in_specs=[pl.BlockSpec((SM_ROWS, s_len), lambda b: (0, b)), pl.BlockSpec((SM_ROWS, 1), lambda b: (0, 0))],
        out_specs=pl.BlockSpec((SM_ROWS, s_len), lambda b: (0, b)),
        out_shape=jax.ShapeDtypeStruct(smt.shape, F32),
        compiler_params=_cparams(("parallel",)),
    )(smt, bias_col)


def _fox_cum_bwd(dc, smt, bias_col, n_batch, s_len, ck=256):
    ck = min(ck, s_len)
    nr = s_len // ck

    def body(dc_ref, s_ref, b_ref, dl_ref, db_ref):
        b = pl.program_id(0)
        tri = (lax.broadcasted_iota(jnp.int32, (ck, ck), 0) >= lax.broadcasted_iota(jnp.int32, (ck, ck), 1)).astype(F32)
        carry = jnp.zeros((SM_ROWS, 1), F32)
        tot = jnp.zeros((SM_ROWS, 1), F32)
        for r in reversed(range(nr)):
            sl = slice(r * ck, (r + 1) * ck)
            dls = jnp.dot(dc_ref[:, sl], tri, precision=lax.Precision.HIGHEST, preferred_element_type=F32) + carry
            carry = dls[:, 0:1]
            dl = dls * (1.0 - _sigmoid(s_ref[:, sl] + b_ref[...]))
            dl_ref[:, sl] = dl
            tot = tot + jnp.sum(dl, axis=1, keepdims=True)

        @pl.when(b == 0)
        def _():
            db_ref[...] = jnp.zeros_like(db_ref)

        db_ref[...] += jnp.broadcast_to(tot, db_ref.shape)

    return pl.pallas_call(
        body, name="fox_cum_bwd", grid=(n_batch,),
        in_specs=[pl.BlockSpec((SM_ROWS, s_len), lambda b: (0, b)), pl.BlockSpec((SM_ROWS, s_len), lambda b: (0, b)),
                  pl.BlockSpec((SM_ROWS, 1), lambda b: (0, 0))],
        out_specs=[pl.BlockSpec((SM_ROWS, s_len), lambda b: (0, b)), pl.BlockSpec((SM_ROWS, LANES), lambda b: (0, 0))],
        out_shape=[jax.ShapeDtypeStruct(smt.shape, F32), jax.ShapeDtypeStruct((SM_ROWS, LANES), F32)],
        compiler_params=_cparams(("arbitrary",)),
    )(dc, smt, bias_col)


def _fox_diagonal_mask(tq):
    return lax.broadcasted_iota(jnp.int32, (tq, tq), 1) <= lax.broadcasted_iota(jnp.int32, (tq, tq), 0)


def _fox_fwd(pf, cb, gq2, gk2, go2, tq=256):
    n_batch, s_len, _ = pf.shape
    tq = min(tq, s_len)
    nq = s_len // tq
    scale = FOX_HEAD_DIM ** -0.5

    def body(q_ref, k_ref, v_ref, c_ref, gq_ref, gk_ref, go_ref, o_ref, on_ref, lse_ref, kh_ref, vh_ref):
        j = pl.program_id(1)
        i = pl.program_id(2)
        m0 = lax.broadcasted_iota(jnp.int32, (1, LANES), 1) < FOX_HEAD_DIM

        @pl.when(i == 0)
        def _():
            kn = _rms_pair(k_ref[0], gk_ref[...], m0)
            kh_ref[0] = jnp.where(m0, kn, 0.0).astype(BF16)
            kh_ref[1] = jnp.where(m0, 0.0, kn).astype(BF16)
            v = v_ref[0]
            vh_ref[0] = jnp.where(m0, v, 0.0).astype(BF16)
            vh_ref[1] = jnp.where(m0, 0.0, v).astype(BF16)

        qb = (_rms_pair(q_ref[0], gq_ref[...], m0) * scale).astype(BF16)

        def step(kb, carry, diagonal=False):
            ms, ls, acc = carry
            off = pl.multiple_of(kb * tq, tq)
            new_m, new_l, alphas, pv = [], [], [], []
            for hh in range(2):
                s = _dot(qb, kh_ref[hh, pl.ds(off, tq), :], NT)
                s = s - c_ref[0, kb, pl.ds(2 * j + hh, 1), :]
                if diagonal:
                    s = jnp.where(_fox_diagonal_mask(tq), s, NEG_INF)
                m_new = jnp.maximum(ms[hh], jnp.max(s, axis=-1, keepdims=True))
                alpha = jnp.exp(ms[hh] - m_new)
                p = jnp.exp(s - m_new)
                new_l.append(alpha * ls[hh] + jnp.sum(p, axis=-1, keepdims=True))
                new_m.append(m_new)
                alphas.append(alpha)
                pv.append(_dot(p, vh_ref[hh, pl.ds(off, tq), :], NN))
            acc = jnp.where(m0, alphas[0], alphas[1]) * acc + pv[0] + pv[1]
            return tuple(new_m), tuple(new_l), acc

        init_m = (jnp.full((tq, 1), NEG_INF, F32),) * 2
        init_l = (jnp.zeros((tq, 1), F32),) * 2
        carry = lax.fori_loop(0, i, step, (init_m, init_l, jnp.zeros((tq, LANES), F32)))
        ms, ls, acc = step(i, carry, diagonal=True)
        o = acc / jnp.where(m0, ls[0], ls[1])
        o_ref[0] = o
        on_ref[0] = _rms_pair(o, go_ref[...], m0).astype(BF16)
        lse_ref[0] = jnp.where(m0, ms[0] + jnp.log(ls[0]), ms[1] + jnp.log(ls[1]))

    fixed = lambda b, j, i: (0, 0)
    tile = lambda b, j, i: (b, i, j)
    return pl.pallas_call(
        body, name="fox_fwd", grid=(n_batch, 4, nq),
        in_specs=[pl.BlockSpec((1, tq, LANES), tile), pl.BlockSpec((1, s_len, LANES), lambda b, j, i: (b, 0, 4 + j)),
                  pl.BlockSpec((1, s_len, LANES), lambda b, j, i: (b, 0, 8 + j)),
                  pl.BlockSpec((1, nq, SM_ROWS, tq), lambda b, j, i: (b, 0, 0, 0)),
                  pl.BlockSpec((1, LANES), fixed), pl.BlockSpec((1, LANES), fixed), pl.BlockSpec((1, LANES), fixed)],
        out_specs=[pl.BlockSpec((1, tq, LANES), tile), pl.BlockSpec((1, tq, LANES), tile), pl.BlockSpec((1, tq, LANES), tile)],
        out_shape=[jax.ShapeDtypeStruct((n_batch, s_len, FOX_WIDTH), F32), jax.ShapeDtypeStruct((n_batch, s_len, FOX_WIDTH), BF16),
                   jax.ShapeDtypeStruct((n_batch, s_len, FOX_WIDTH), F32)],
        scratch_shapes=[pltpu.VMEM((2, s_len, LANES), BF16), pltpu.VMEM((2, s_len, LANES), BF16)],
        compiler_params=_cparams(("parallel", "parallel", "arbitrary")),
    )(pf, pf, pf, cb, gq2, gk2, go2)


def _fox_bwd(pf, cb, gq2, gk2, go2, o, lse, don, tq=256):
    n_batch, s_len, _ = pf.shape
    tq = min(tq, s_len)
    nq = s_len // tq
    scale = FOX_HEAD_DIM ** -0.5

    def body(q_ref, k_ref, v_ref, c_ref, gq_ref, gk_ref, go_ref, o_ref, lse_ref, don_ref,
             dq_ref, dk_ref, dv_ref, dc_ref, dgq_ref, dgk_ref, dgo_ref, kh_ref, vh_ref, dka_ref, dva_ref, dca_ref):
        b = pl.program_id(0)
        j = pl.program_id(1)
        i = pl.program_id(2)
        m0 = lax.broadcasted_iota(jnp.int32, (1, LANES), 1) < FOX_HEAD_DIM

        @pl.when((b == 0) & (j == 0) & (i == 0))
        def _():
            dgq_ref[...] = jnp.zeros_like(dgq_ref)
            dgk_ref[...] = jnp.zeros_like(dgk_ref)
            dgo_ref[...] = jnp.zeros_like(dgo_ref)

        @pl.when(i == 0)
        def _():
            kn = _rms_pair(k_ref[0], gk_ref[...], m0)
            kh_ref[0] = jnp.where(m0, kn, 0.0).astype(BF16)
            kh_ref[1] = jnp.where(m0, 0.0, kn).astype(BF16)
            v = v_ref[0]
            vh_ref[0] = jnp.where(m0, v, 0.0).astype(BF16)
            vh_ref[1] = jnp.where(m0, 0.0, v).astype(BF16)
            dka_ref[...] = jnp.zeros_like(dka_ref)
            dva_ref[...] = jnp.zeros_like(dva_ref)
            dca_ref[...] = jnp.zeros_like(dca_ref)

        q = q_ref[0]
        qn = _rms_pair(q, gq_ref[...], m0)
        qs = qn * scale
        qb = qs.astype(BF16)
        qh = (jnp.where(m0, qs, 0.0).astype(BF16), jnp.where(m0, 0.0, qs).astype(BF16))
        ot = o_ref[0]
        do, dgo = _rms_pair_bwd(ot, go_ref[...], don_ref[0], m0)
        dgo_ref[...] += dgo
        dd = do * ot
        delta = (jnp.sum(jnp.where(m0, dd, 0.0), axis=-1, keepdims=True), jnp.sum(jnp.where(m0, 0.0, dd), axis=-1, keepdims=True))
        doh = (jnp.where(m0, do, 0.0).astype(BF16), jnp.where(m0, 0.0, do).astype(BF16))
        lse_t = lse_ref[0]
        lse_h = (lse_t[:, 0:1], lse_t[:, FOX_HEAD_DIM:FOX_HEAD_DIM + 1])

        def step(kb, carry, diagonal=False):
            dqn, rs = carry
            rs = list(rs)
            off = pl.multiple_of(kb * tq, tq)
            for hh in range(2):
                kblk = kh_ref[hh, pl.ds(off, tq), :]
                vblk = vh_ref[hh, pl.ds(off, tq), :]
                s = _dot(qb, kblk, NT)
                s = s - c_ref[0, kb, pl.ds(2 * j + hh, 1), :]
                if diagonal:
                    s = jnp.where(_fox_diagonal_mask(tq), s, NEG_INF)
                p = jnp.exp(s - lse_h[hh])
                dp = _dot(doh[hh], vblk, NT)
                ds = p * (dp - delta[hh])
                dva_ref[pl.ds(off, tq), :] += _dot(p, doh[hh], TN)
                dka_ref[pl.ds(off, tq), :] += _dot(ds, qh[hh], TN)
                dca_ref[kb, hh:hh + 1, :] += -jnp.sum(ds, axis=0, keepdims=True)
                rs[hh] = rs[hh] + jnp.sum(ds, axis=-1, keepdims=True)
                dqn = dqn + _dot(ds, kblk, NN)
            return dqn, tuple(rs)

        carry = lax.fori_loop(0, i, step, (jnp.zeros((tq, LANES), F32), (jnp.zeros((tq, 1), F32),) * 2))
        dqn, rs = step(i, carry, diagonal=True)
        dqn = dqn * scale
        rs_rows = jnp.where(m0, rs[0], rs[1]).T
        dca_ref[i, 0:1, :] += rs_rows[0:1, :]
        dca_ref[i, 1:2, :] += rs_rows[FOX_HEAD_DIM:FOX_HEAD_DIM + 1, :]
        dq, dgq = _rms_pair_bwd(q, gq_ref[...], dqn, m0)
        dq_ref[0] = dq.astype(BF16)
        dgq_ref[...] += dgq

        @pl.when(i == nq - 1)
        def _():
            dk, dgk = _rms_pair_bwd(k_ref[0], gk_ref[...], dka_ref[...], m0)
            dk_ref[0] = dk.astype(BF16)
            dgk_ref[...] += dgk
            dv_ref[0] = dva_ref[...].astype(BF16)
            dc_ref[0, 0] = dca_ref[...]

    fixed = lambda b, j, i: (0, 0)
    tile = lambda b, j, i: (b, i, j)
    full = lambda b, j, i: (b, 0, j)
    wide = jax.ShapeDtypeStruct((n_batch, s_len, FOX_WIDTH), BF16)
    gain = jax.ShapeDtypeStruct((1, LANES), F32)
    return pl.pallas_call(
        body, name="fox_bwd", grid=(n_batch, 4, nq),
        in_specs=[pl.BlockSpec((1, tq, LANES), tile), pl.BlockSpec((1, s_len, LANES), lambda b, j, i: (b, 0, 4 + j)),
                  pl.BlockSpec((1, s_len, LANES), lambda b, j, i: (b, 0, 8 + j)),
                  pl.BlockSpec((1, nq, SM_ROWS, tq), lambda b, j, i: (b, 0, 0, 0)),
                  pl.BlockSpec((1, LANES), fixed), pl.BlockSpec((1, LANES), fixed), pl.BlockSpec((1, LANES), fixed),
                  pl.BlockSpec((1, tq, LANES), tile), pl.BlockSpec((1, tq, LANES), tile), pl.BlockSpec((1, tq, LANES), tile)],
        out_specs=[pl.BlockSpec((1, tq, LANES), tile), pl.BlockSpec((1, s_len, LANES), full), pl.BlockSpec((1, s_len, LANES), full),
                   pl.BlockSpec((1, 1, nq, 8, tq), lambda b, j, i: (b, j, 0, 0, 0)),
                   pl.BlockSpec((1, LANES), fixed), pl.BlockSpec((1, LANES), fixed), pl.BlockSpec((1, LANES), fixed)],
        out_shape=[wide, wide, wide, jax.ShapeDtypeStruct((n_batch, 4, nq, 8, tq), F32), gain, gain, gain],
        scratch_shapes=[pltpu.VMEM((2, s_len, LANES), BF16), pltpu.VMEM((2, s_len, LANES), BF16),
                        pltpu.VMEM((s_len, LANES), F32), pltpu.VMEM((s_len, LANES), F32), pltpu.VMEM((nq, 8, tq), F32)],
        compiler_params=_cparams(("arbitrary", "arbitrary", "arbitrary")),
    )(pf, pf, pf, cb, gq2, gk2, go2, o, lse, don)


def _shift_down(x, k):
    row = lax.broadcasted_iota(jnp.int32, x.shape, 0)
    return jnp.where(row >= k, pltpu.roll(x, k, 0), 0.0)


def _shift_up(x, k):
    n = x.shape[0]
    row = lax.broadcasted_iota(jnp.int32, x.shape, 0)
    return jnp.where(row < n - k, pltpu.roll(x, n - k, 0), 0.0)


def _conv_silu(x, w):
    y = w[3:4] * x + w[2:3] * _shift_down(x, 1) + w[1:2] * _shift_down(x, 2) + w[0:1] * _shift_down(x, 3)
    return y, y * _sigmoid(y)


def _gdn_pre(pg, conv_w):
    n_batch, s_len, width = pg.shape
    ncb = width // LANES

    def body(x_ref, w_ref, o_ref):
        cb = pl.program_id(1)
        _, s = _conv_silu(x_ref[0], w_ref[...])
        sn = s * lax.rsqrt(jnp.sum(s * s, axis=-1, keepdims=True) + EPS)
        o_ref[0] = jnp.where(cb < 2 * GDN_HEADS, sn, s)

    return pl.pallas_call(
        body, name="gdn_pre", grid=(n_batch, ncb),
        in_specs=[pl.BlockSpec((1, s_len, LANES), lambda b, c: (b, 0, c)), pl.BlockSpec((8, LANES), lambda b, c: (0, c))],
        out_specs=pl.BlockSpec((1, s_len, LANES), lambda b, c: (b, 0, c)),
        out_shape=jax.ShapeDtypeStruct(pg.shape, F32),
        compiler_params=_cparams(("parallel", "parallel")),
    )(pg, conv_w)


def _gdn_pre_bwd(pg, conv_w, dout):
    n_batch, s_len, width = pg.shape
    ncb = width // LANES

    def body(x_ref, w_ref, d_ref, dx_ref, dw_ref):
        cb = pl.program_id(0)
        b = pl.program_id(1)
        x = x_ref[0]
        w = w_ref[...]
        d = d_ref[0]
        y, s = _conv_silu(x, w)
        rr = lax.rsqrt(jnp.sum(s * s, axis=-1, keepdims=True) + EPS)
        sn = s * rr
        ds_n = rr * (d - sn * jnp.sum(d * sn, axis=-1, keepdims=True))
        ds = jnp.where(cb < 2 * GDN_HEADS, ds_n, d)
        sig = _sigmoid(y)
        dy = ds * (sig * (1.0 + y * (1.0 - sig)))
        dx = w[3:4] * dy + w[2:3] * _shift_up(dy, 1) + w[1:2] * _shift_up(dy, 2) + w[0:1] * _shift_up(dy, 3)
        dx_ref[0] = dx.astype(BF16)
        dw = [jnp.sum(dy * _shift_down(x, 3 - jj), axis=0, keepdims=True) if jj < 3 else jnp.sum(dy * x, axis=0, keepdims=True)
              for jj in range(CONV_WIDTH)]
        rows = lax.broadcasted_iota(jnp.int32, (8, LANES), 0)
        dwb = jnp.zeros((8, LANES), F32)
        for jj in range(CONV_WIDTH):
            dwb = dwb + jnp.where(rows == jj, dw[jj], 0.0)

        @pl.when(b == 0)
        def _():
            dw_ref[...] = jnp.zeros_like(dw_ref)

        dw_ref[...] += dwb

    blk = lambda c, b: (b, 0, c)
    return pl.pallas_call(
        body, name="gdn_pre_bwd", grid=(ncb, n_batch),
        in_specs=[pl.BlockSpec((1, s_len, LANES), blk), pl.BlockSpec((8, LANES), lambda c, b: (0, c)), pl.BlockSpec((1, s_len, LANES), blk)],
        out_specs=[pl.BlockSpec((1, s_len, LANES), blk), pl.BlockSpec((8, LANES), lambda c, b: (0, c))],
        out_shape=[jax.ShapeDtypeStruct(pg.shape, BF16), jax.ShapeDtypeStruct((8, width), F32)],
        compiler_params=_cparams(("parallel", "arbitrary")),
    )(pg, conv_w, dout)


def _gdn_gates(smc, smr, a_c, dt_c, a_r, dt_r, h):
    lane = lax.broadcasted_iota(jnp.int32, (1, LANES), 1)
    sub = lax.broadcasted_iota(jnp.int32, (SM_ROWS, 1), 0)
    beta_c = jnp.sum(jnp.where(lane == SM_B + h, _sigmoid(smc), 0.0), axis=1, keepdims=True)
    g_all_c = -jnp.exp(a_c) * _softplus(smc + dt_c)
    g_c = jnp.sum(jnp.where(lane == SM_A + h, g_all_c, 0.0), axis=1, keepdims=True)
    g_all_r = -jnp.exp(a_r) * _softplus(smr + dt_r)
    g_r = jnp.sum(jnp.where(sub == SM_A + h, g_all_r, 0.0), axis=0, keepdims=True)
    return beta_c, g_c, g_r


def _gdn_group(qkv, z, smc, smr, a_c, dt_c, a_r, dt_r, go, states):
    n_grp = len(qkv)
    c = qkv[0].shape[0]
    hd = GDN_HEAD_DIM
    pairs = [(g, h) for g in range(n_grp) for h in range(GDN_HEADS)]
    ii = lax.broadcasted_iota(jnp.int32, (c, c), 0)
    jj = lax.broadcasted_iota(jnp.int32, (c, c), 1)
    incl = ii >= jj
    col = lambda arr, base, h: arr[:, base + h * hd:base + (h + 1) * hd]

    qs, ks, kbs, vbs, decays, gcs, g_lasts, amats = [], [], [], [], [], [], [], []
    for g, h in pairs:
        beta_c, g_c, g_r = _gdn_gates(smc[g], smr[g], a_c, dt_c, a_r, dt_r, h)
        gc_c = jnp.sum(jnp.where(incl, g_r, 0.0), axis=1, keepdims=True)
        gc_r = jnp.sum(jnp.where(ii <= jj, g_c, 0.0), axis=0, keepdims=True)
        decay = jnp.where(incl, jnp.exp(jnp.where(incl, gc_c - gc_r, 0.0)), 0.0)
        k = col(qkv[g], GDN_WIDTH, h)
        kb = k * beta_c
        qs.append(col(qkv[g], 0, h) * (hd ** -0.5))
        ks.append(k)
        kbs.append(kb)
        vbs.append(col(qkv[g], 2 * GDN_WIDTH, h) * beta_c)
        decays.append(decay)
        gcs.append(gc_c)
        g_lasts.append(jnp.sum(g_c, axis=0, keepdims=True))
        amats.append(jnp.where(ii > jj, _mm_nt(kb, k) * decay, 0.0))
    ts = _unit_lower_inverses(amats)
    egcs = [jnp.exp(gc) for gc in gcs]
    us = [_mm_nn(t, vb) for t, vb in zip(ts, vbs)]
    ws = [_mm_nn(t, kb * e) for t, kb, e in zip(ts, kbs, egcs)]
    intras = [_mm_nt(q, k) * d for q, k, d in zip(qs, ks, decays)]
    qes = [q * e for q, e in zip(qs, egcs)]
    kds = [k * jnp.exp(gl - gc) for k, gl, gc in zip(ks, g_lasts, gcs)]
    sdecs = [jnp.exp(gl) for gl in g_lasts]

    outs = []
    for g in range(n_grp):
        idx = [g * GDN_HEADS + h for h in range(GDN_HEADS)]
        v_new = [us[i] - _mm_nn(ws[i], states[h]) for h, i in enumerate(idx)]
        o_state = [_mm_nn(qes[i], states[h]) for h, i in enumerate(idx)]
        o_intra = [_mm_nn(intras[i], v_new[h]) for h, i in enumerate(idx)]
        states = [states[h] * sdecs[i] + _mm_tn(kds[i], v_new[h]) for h, i in enumerate(idx)]
        outs.append([_rms(o_state[h] + o_intra[h], go) * (col(z[g], 0, h) * _sigmoid(col(z[g], 0, h))) for h in range(GDN_HEADS)])
    return outs, states


def _gdn_group_size(n_chunks):
    return GDN_GROUP if n_chunks % GDN_GROUP == 0 else 1


def _gdn_fwd(qkvn, z, smc, smr, a_c, dt_c, a_r, dt_r, go):
    n_batch, s_len, _ = qkvn.shape
    c = GDN_CHUNK
    n = s_len // c
    grp = _gdn_group_size(n)
    ng = n // grp
    gc = grp * c
    hd = GDN_HEAD_DIM

    def body(qkv_ref, z_ref, smc_ref, smr_ref, ac_ref, dc_ref, ar_ref, dr_ref, go_ref, og_ref, st_ref, s_ref):
        @pl.when(pl.program_id(1) == 0)
        def _():
            s_ref[...] = jnp.zeros_like(s_ref)

        states = [s_ref[h] for h in range(GDN_HEADS)]
        for h in range(GDN_HEADS):
            st_ref[0, 0, h] = states[h]
        rows = lambda k: slice(k * c, (k + 1) * c)
        outs, nxt = _gdn_group([qkv_ref[0, rows(k), :] for k in range(grp)], [z_ref[0, rows(k), :] for k in range(grp)],
                               [smc_ref[0, rows(k), :] for k in range(grp)], [smr_ref[k] for k in range(grp)],
                               ac_ref[...], dc_ref[...], ar_ref[...], dr_ref[...], go_ref[...], states)
        for k in range(grp):
            for h in range(GDN_HEADS):
                og_ref[0, rows(k), h * hd:(h + 1) * hd] = outs[k][h].astype(BF16)
        for h in range(GDN_HEADS):
            s_ref[h] = nxt[h]

    tok = lambda b, i: (b, i, 0)
    fixed = lambda b, i: (0, 0)
    return pl.pallas_call(
        body, name="gdn_fwd", grid=(n_batch, ng),
        in_specs=[pl.BlockSpec((1, gc, 3 * GDN_WIDTH), tok), pl.BlockSpec((1, gc, GDN_WIDTH), tok), pl.BlockSpec((1, gc, LANES), tok),
                  pl.BlockSpec((grp, SM_ROWS, c), lambda b, i: (b * ng + i, 0, 0)),
                  pl.BlockSpec((1, LANES), fixed), pl.BlockSpec((1, LANES), fixed), pl.BlockSpec((SM_ROWS, 1), fixed),
                  pl.BlockSpec((SM_ROWS, 1), fixed), pl.BlockSpec((1, LANES), fixed)],
        out_specs=[pl.BlockSpec((1, gc, GDN_WIDTH), tok), pl.BlockSpec((1, 1, GDN_HEADS, hd, hd), lambda b, i: (b, i, 0, 0, 0))],
        out_shape=[jax.ShapeDtypeStruct((n_batch, s_len, GDN_WIDTH), BF16), jax.ShapeDtypeStruct((n_batch, ng, GDN_HEADS, hd, hd), F32)],
        scratch_shapes=[pltpu.VMEM((GDN_HEADS, hd, hd), F32)],
        compiler_params=_cparams(("parallel", "arbitrary")),
    )(qkvn, z, smc, smr, a_c, dt_c, a_r, dt_r, go)


def _gdn_bwd(qkvn, z, smc, smr, a_c, dt_c, a_r, dt_r, go, states, dog):
    n_batch, s_len, _ = qkvn.shape
    c = GDN_CHUNK
    n = s_len // c
    grp = _gdn_group_size(n)
    ng = n // grp
    gc = grp * c
    hd = GDN_HEAD_DIM

    def body(qkv_ref, z_ref, smc_ref, smr_ref, ac_ref, dc_ref, ar_ref, dr_ref, go_ref, st_ref, dog_ref,
             dqkv_ref, dz_ref, dsmc_ref, dsmr_ref, dac_ref, ddc_ref, dar_ref, ddr_ref, dgo_ref, ds_ref):
        first = (pl.program_id(0) == 0) & (pl.program_id(1) == 0)

        @pl.when(pl.program_id(1) == 0)
        def _():
            ds_ref[...] = jnp.zeros_like(ds_ref)

        @pl.when(first)
        def _():
            for r in (dac_ref, ddc_ref, dar_ref, ddr_ref, dgo_ref):
                r[...] = jnp.zeros_like(r)

        rows = lambda k: slice(k * c, (k + 1) * c)
        states = [st_ref[0, 0, h] for h in range(GDN_HEADS)]
        prim = ([qkv_ref[0, rows(k), :] for k in range(grp)], [z_ref[0, rows(k), :] for k in range(grp)],
                [smc_ref[0, rows(k), :] for k in range(grp)], [smr_ref[k] for k in range(grp)],
                ac_ref[...], dc_ref[...], ar_ref[...], dr_ref[...], go_ref[...], states)
        _, vjp = jax.vjp(_gdn_group, *prim)
        cot = ([[dog_ref[0, rows(k), h * hd:(h + 1) * hd] for h in range(GDN_HEADS)] for k in range(grp)],
               [ds_ref[h] for h in range(GDN_HEADS)])
        dqkv, dz, dsmc, dsmr, dac, ddc, dar, ddr, dgo, dstates = vjp(cot)
        for k in range(grp):
            dqkv_ref[0, rows(k), :] = dqkv[k]
            dz_ref[0, rows(k), :] = dz[k].astype(BF16)
            dsmc_ref[0, rows(k), :] = dsmc[k]
            dsmr_ref[k] = dsmr[k]
        dac_ref[...] += dac
        ddc_ref[...] += ddc
        dar_ref[...] += dar
        ddr_ref[...] += ddr
        dgo_ref[...] += dgo
        for h in range(GDN_HEADS):
            ds_ref[h] = dstates[h]

    tok = lambda b, i: (b, ng - 1 - i, 0)
    fixed = lambda b, i: (0, 0)
    lane_vec = jax.ShapeDtypeStruct((1, LANES), F32)
    row_vec = jax.ShapeDtypeStruct((SM_ROWS, 1), F32)
    return pl.pallas_call(
        body, name="gdn_bwd", grid=(n_batch, ng),
        in_specs=[pl.BlockSpec((1, gc, 3 * GDN_WIDTH), tok), pl.BlockSpec((1, gc, GDN_WIDTH), tok), pl.BlockSpec((1, gc, LANES), tok),
                  pl.BlockSpec((grp, SM_ROWS, c), lambda b, i: (b * ng + ng - 1 - i, 0, 0)),
                  pl.BlockSpec((1, LANES), fixed), pl.BlockSpec((1, LANES), fixed), pl.BlockSpec((SM_ROWS, 1), fixed),
                  pl.BlockSpec((SM_ROWS, 1), fixed), pl.BlockSpec((1, LANES), fixed),
                  pl.BlockSpec((1, 1, GDN_HEADS, hd, hd), lambda b, i: (b, ng - 1 - i, 0, 0, 0)),
                  pl.BlockSpec((1, gc, GDN_WIDTH), lambda b, i: (b, ng - 1 - i, 1))],
        out_specs=[pl.BlockSpec((1, gc, 3 * GDN_WIDTH), tok), pl.BlockSpec((1, gc, GDN_WIDTH), tok), pl.BlockSpec((1, gc, LANES), tok),
                   pl.BlockSpec((grp, SM_ROWS, c), lambda b, i: (b * ng + ng - 1 - i, 0, 0)),
                   pl.BlockSpec((1, LANES), fixed), pl.BlockSpec((1, LANES), fixed), pl.BlockSpec((SM_ROWS, 1), fixed),
                   pl.BlockSpec((SM_ROWS, 1), fixed), pl.BlockSpec((1, LANES), fixed)],
        out_shape=[jax.ShapeDtypeStruct((n_batch, s_len, 3 * GDN_WIDTH), F32), jax.ShapeDtypeStruct((n_batch, s_len, GDN_WIDTH), BF16),
                   jax.ShapeDtypeStruct((n_batch, s_len, LANES), F32), jax.ShapeDtypeStruct((n_batch * n, SM_ROWS, c), F32),
                   lane_vec, lane_vec, row_vec, row_vec, lane_vec],
        scratch_shapes=[pltpu.VMEM((GDN_HEADS, hd, hd), F32)],
        compiler_params=_cparams(("arbitrary", "arbitrary")),
    )(qkvn, z, smc, smr, a_c, dt_c, a_r, dt_r, go, states, dog)


def _out_proj(x, oa, ob, w_out, g_x, w_cq, tm=256):
    t_len, d = x.shape
    tm = min(tm, t_len)

    def body(x_ref, oa_ref, ob_ref, wo_ref, g_ref, wq_ref, x1_ref, hq_ref, cq_ref):
        x1 = x_ref[...] + _dot(oa_ref[...], wo_ref[0:FOX_WIDTH, :]) + _dot(ob_ref[...], wo_ref[FOX_WIDTH:2 * FOX_WIDTH, :])
        x1_ref[...] = x1
        hq = _rms(x1, g_ref[...]).astype(BF16)
        hq_ref[...] = hq
        cq_ref[...] = _dot(hq, wq_ref[...])

    row = lambda i: (i, 0)
    fixed = lambda i: (0, 0)
    return pl.pallas_call(
        body, name="out_proj", grid=(t_len // tm,),
        in_specs=[pl.BlockSpec((tm, d), row), pl.BlockSpec((tm, FOX_WIDTH), row), pl.BlockSpec((tm, GDN_WIDTH), row),
                  pl.BlockSpec((d, d), fixed), pl.BlockSpec((1, d), fixed), pl.BlockSpec((d, XATTN_WIDTH), fixed)],
        out_specs=[pl.BlockSpec((tm, d), row), pl.BlockSpec((tm, d), row), pl.BlockSpec((tm, XATTN_WIDTH), row)],
        out_shape=[jax.ShapeDtypeStruct((t_len, d), F32), jax.ShapeDtypeStruct((t_len, d), BF16), jax.ShapeDtypeStruct((t_len, XATTN_WIDTH), F32)],
        compiler_params=_cparams(("parallel",)),
    )(x, oa, ob, w_out, g_x, w_cq)


def _out_proj_bwd(dx1, w_out, tm=512):
    t_len, d = dx1.shape
    tm = min(tm, t_len)

    def body(dx_ref, w_ref, o_ref):
        o_ref[...] = _dot(dx_ref[...], w_ref[...], NT)

    return pl.pallas_call(
        body, name="out_proj_bwd", grid=(t_len // tm,),
        in_specs=[pl.BlockSpec((tm, d), lambda i: (i, 0)), pl.BlockSpec((d, d), lambda i: (0, 0))],
        out_specs=pl.BlockSpec((tm, d), lambda i: (i, 0)),
        out_shape=jax.ShapeDtypeStruct((t_len, d), F32),
        compiler_params=_cparams(("parallel",)),
    )(dx1, w_out)


def _mem_kv(mem, g, w_ckv, tm=256):
    t_len, d = mem.shape
    tm = min(tm, t_len)

    def body(x_ref, g_ref, w_ref, h_ref, o_ref):
        h = _rms(x_ref[...], g_ref[...]).astype(BF16)
        h_ref[...] = h
        o_ref[...] = _dot(h, w_ref[...])

    row = lambda i: (i, 0)
    fixed = lambda i: (0, 0)
    return pl.pallas_call(
        body, name="mem_kv", grid=(t_len // tm,),
        in_specs=[pl.BlockSpec((tm, d), row), pl.BlockSpec((1, d), fixed), pl.BlockSpec((d, 2 * XATTN_WIDTH), fixed)],
        out_specs=[pl.BlockSpec((tm, d), row), pl.BlockSpec((tm, 2 * XATTN_WIDTH), row)],
        out_shape=[jax.ShapeDtypeStruct((t_len, d), BF16), jax.ShapeDtypeStruct((t_len, 2 * XATTN_WIDTH), F32)],
        compiler_params=_cparams(("parallel",)),
    )(mem, g, w_ckv)


def _mem_kv_bwd(dckv, mem, g, w_ckv, tm=256):
    t_len, d = mem.shape
    tm = min(tm, t_len)

    def body(d_ref, x_ref, g_ref, w_ref, dg_ref):
        @pl.when(pl.program_id(0) == 0)
        def _():
            dg_ref[...] = jnp.zeros_like(dg_ref)

        dh = _dot(d_ref[...], w_ref[...], NT)
        _, dg = _rms_bwd(x_ref[...], g_ref[...], dh)
        dg_ref[...] += dg

    row = lambda i: (i, 0)
    fixed = lambda i: (0, 0)
    return pl.pallas_call(
        body, name="mem_kv_bwd", grid=(t_len // tm,),
        in_specs=[pl.BlockSpec((tm, 2 * XATTN_WIDTH), row), pl.BlockSpec((tm, d), row), pl.BlockSpec((1, d), fixed),
                  pl.BlockSpec((d, 2 * XATTN_WIDTH), fixed)],
        out_specs=pl.BlockSpec((1, d), fixed),
        out_shape=jax.ShapeDtypeStruct((1, d), F32),
        compiler_params=_cparams(("arbitrary",)),
    )(dckv, mem, g, w_ckv)


def _xattn_probs(qn, kn):
    s = _dot(qn, kn, NT) * (XATTN_HEAD_DIM ** -0.5)
    p = jnp.exp(s - jnp.max(s, axis=-1, keepdims=True))
    return p / jnp.sum(p, axis=-1, keepdims=True)


def _xattn_fwd(cq, ckv, x1, gq, gk, w_co, g_mlp, n_batch, s_len, m_len, tq=512):
    d = x1.shape[1]
    tq = min(tq, s_len)
    nq = s_len // tq
    hd = XATTN_HEAD_DIM

    def body(cq_ref, kv_ref, x1_ref, gq_ref, gk_ref, wo_ref, gm_ref, co_ref, x2_ref, hf_ref):
        outs = []
        for h in range(XATTN_HEADS):
            qn = _rms(cq_ref[:, h * hd:(h + 1) * hd], gq_ref[...])
            kn = _rms(kv_ref[:, h * hd:(h + 1) * hd], gk_ref[...])
            p = _xattn_probs(qn, kn)
            outs.append(_dot(p, kv_ref[:, XATTN_WIDTH + h * hd:XATTN_WIDTH + (h + 1) * hd]).astype(BF16))
        x2 = x1_ref[...]
        for h in range(XATTN_HEADS):
            co_ref[:, h * hd:(h + 1) * hd] = outs[h]
            x2 = x2 + _dot(outs[h], wo_ref[h * hd:(h + 1) * hd, :])
        x2_ref[...] = x2
        hf_ref[...] = _rms(x2, gm_ref[...]).astype(BF16)

    row = lambda b, i: (b * nq + i, 0)
    fixed = lambda b, i: (0, 0)
    t_len = n_batch * s_len
    return pl.pallas_call(
        body, name="xattn_fwd", grid=(n_batch, nq),
        in_specs=[pl.BlockSpec((tq, XATTN_WIDTH), row), pl.BlockSpec((m_len, 2 * XATTN_WIDTH), lambda b, i: (b, 0)),
                  pl.BlockSpec((tq, d), row), pl.BlockSpec((1, hd), fixed), pl.BlockSpec((1, hd), fixed),
                  pl.BlockSpec((XATTN_WIDTH, d), fixed), pl.BlockSpec((1, d), fixed)],
        out_specs=[pl.BlockSpec((tq, XATTN_WIDTH), row), pl.BlockSpec((tq, d), row), pl.BlockSpec((tq, d), row)],
        out_shape=[jax.ShapeDtypeStruct((t_len, XATTN_WIDTH), BF16), jax.ShapeDtypeStruct((t_len, d), F32),
                   jax.ShapeDtypeStruct((t_len, d), BF16)],
        compiler_params=_cparams(("parallel", "parallel")),
    )(cq, ckv, x1, gq, gk, w_co, g_mlp)


def _xattn_bwd(dx2, cq, ckv, x1, gq, gk, w_co, g_x, w_cq, n_batch, s_len, m_len, tq=512):
    d = x1.shape[1]
    tq = min(tq, s_len)
    nq = s_len // tq
    hd = XATTN_HEAD_DIM
    scale = XATTN_HEAD_DIM ** -0.5

    def body(dx2_ref, cq_ref, kv_ref, x1_ref, gq_ref, gk_ref, wo_ref, gx_ref, wq_ref,
             dx1_ref, dcq_ref, dkv_ref, dgq_ref, dgk_ref, dgx_ref, dk_acc, dv_acc):
        b = pl.program_id(0)
        i = pl.program_id(1)

        @pl.when((b == 0) & (i == 0))
        def _():
            dgq_ref[...] = jnp.zeros_like(dgq_ref)
            dgk_ref[...] = jnp.zeros_like(dgk_ref)
            dgx_ref[...] = jnp.zeros_like(dgx_ref)

        @pl.when(i == 0)
        def _():
            dk_acc[...] = jnp.zeros_like(dk_acc)
            dv_acc[...] = jnp.zeros_like(dv_acc)

        dx2 = dx2_ref[...]
        dhq = jnp.zeros((tq, d), F32)
        for h in range(XATTN_HEADS):
            sl = slice(h * hd, (h + 1) * hd)
            q = cq_ref[:, sl]
            qn = _rms(q, gq_ref[...])
            kn = _rms(kv_ref[:, sl], gk_ref[...])
            v = kv_ref[:, XATTN_WIDTH + h * hd:XATTN_WIDTH + (h + 1) * hd]
            p = _xattn_probs(qn, kn)
            dco = _dot(dx2, wo_ref[sl, :], NT)
            dv_acc[:, sl] += _dot(p, dco, TN)
            dp = _dot(dco, v, NT)
            ds = p * (dp - jnp.sum(dp * p, axis=-1, keepdims=True))
            dqn = _dot(ds, kn) * scale
            dk_acc[:, sl] += _dot(ds, qn, TN) * scale
            dq, dgq = _rms_bwd(q, gq_ref[...], dqn)
            dgq_ref[...] += dgq
            dqb = dq.astype(BF16)
            dcq_ref[:, sl] = dqb
            dhq = dhq + _dot(dqb, wq_ref[:, sl], NT)
        dxn, dgx = _rms_bwd(x1_ref[...], gx_ref[...], dhq)
        dgx_ref[...] += dgx
        dx1_ref[...] = dx2 + dxn

        @pl.when(i == nq - 1)
        def _():
            for h in range(XATTN_HEADS):
                sl = slice(h * hd, (h + 1) * hd)
                dk, dgk = _rms_bwd(kv_ref[:, sl], gk_ref[...], dk_acc[:, sl])
                dgk_ref[...] += dgk
                dkv_ref[:, sl] = dk.astype(BF16)
                dkv_ref[:, XATTN_WIDTH + h * hd:XATTN_WIDTH + (h + 1) * hd] = dv_acc[:, sl].astype(BF16)

    row = lambda b, i: (b * nq + i, 0)
    fixed = lambda b, i: (0, 0)
    t_len = n_batch * s_len
    return pl.pallas_call(
        body, name="xattn_bwd", grid=(n_batch, nq),
        in_specs=[pl.BlockSpec((tq, d), row), pl.BlockSpec((tq, XATTN_WIDTH), row), pl.BlockSpec((m_len, 2 * XATTN_WIDTH), lambda b, i: (b, 0)),
                  pl.BlockSpec((tq, d), row), pl.BlockSpec((1, hd), fixed), pl.BlockSpec((1, hd), fixed),
                  pl.BlockSpec((XATTN_WIDTH, d), fixed), pl.BlockSpec((1, d), fixed), pl.BlockSpec((d, XATTN_WIDTH), fixed)],
        out_specs=[pl.BlockSpec((tq, d), row), pl.BlockSpec((tq, XATTN_WIDTH), row), pl.BlockSpec((m_len, 2 * XATTN_WIDTH), lambda b, i: (b, 0)),
                   pl.BlockSpec((1, hd), fixed), pl.BlockSpec((1, hd), fixed), pl.BlockSpec((1, d), fixed)],
        out_shape=[jax.ShapeDtypeStruct((t_len, d), F32), jax.ShapeDtypeStruct((t_len, XATTN_WIDTH), BF16),
                   jax.ShapeDtypeStruct((n_batch * m_len, 2 * XATTN_WIDTH), BF16),
                   jax.ShapeDtypeStruct((1, hd), F32), jax.ShapeDtypeStruct((1, hd), F32), jax.ShapeDtypeStruct((1, d), F32)],
        scratch_shapes=[pltpu.VMEM((m_len, XATTN_WIDTH), F32), pltpu.VMEM((m_len, XATTN_WIDTH), F32)],
        compiler_params=_cparams(("arbitrary", "arbitrary")),
    )(dx2, cq, ckv, x1, gq, gk, w_co, g_x, w_cq)


def _resident(shape):
    return pl.BlockSpec(shape, lambda *_: (0,) * len(shape), pipeline_mode=pl.Buffered(1))


def _mlp_fwd(hf, x2, target, w1, w2, tm=256, tf=1024):
    t_len, d = x2.shape
    f = w1.shape[1]
    tm, tf = min(tm, t_len), min(tf, f)

    def body(hf_ref, x2_ref, tg_ref, w1_ref, w2_ref, u_ref, a_ref, dy_ref, ls_ref):
        hf_t = hf_ref[...]
        y = x2_ref[...]
        for k in range(f // tf):
            cols = slice(k * tf, (k + 1) * tf)
            u = _dot(hf_t, w1_ref[:, cols])
            u_ref[:, cols] = u
            r = jnp.maximum(u, 0.0)
            a = (r * r).astype(BF16)
            a_ref[:, cols] = a
            y = y + _dot(a, w2_ref[cols, :])
        err = y - tg_ref[...]
        dy_ref[...] = err * (1.0 / d)
        ls_ref[...] = jnp.broadcast_to(jnp.sum(jnp.sum(err * err, axis=-1, keepdims=True) * (1.0 / d), axis=0, keepdims=True), ls_ref.shape)

    row = lambda i: (i, 0)
    return pl.pallas_call(
        body, name="mlp_fwd", grid=(t_len // tm,),
        in_specs=[pl.BlockSpec((tm, d), row), pl.BlockSpec((tm, d), row), pl.BlockSpec((tm, d), row), _resident((d, f)), _resident((f, d))],
        out_specs=[pl.BlockSpec((tm, f), row), pl.BlockSpec((tm, f), row), pl.BlockSpec((tm, d), row),
                   pl.BlockSpec((1, 8, LANES), lambda i: (i, 0, 0))],
        out_shape=[jax.ShapeDtypeStruct((t_len, f), F32), jax.ShapeDtypeStruct((t_len, f), BF16), jax.ShapeDtypeStruct((t_len, d), F32),
                   jax.ShapeDtypeStruct((t_len // tm, 8, LANES), F32)],
        compiler_params=_cparams(("parallel",)),
    )(hf, x2, target, w1, w2)


def _mlp_bwd(dy, u, x2, g, w1, w2, tm=256, tf=1024):
    t_len, d = x2.shape
    f = w1.shape[1]
    tm, tf = min(tm, t_len), min(tf, f)

    def body(dy_ref, u_ref, x2_ref, g_ref, w1_ref, w2_ref, du_ref, dx2_ref, dg_ref):
        @pl.when(pl.program_id(0) == 0)
        def _():
            dg_ref[...] = jnp.zeros_like(dg_ref)

        dy_t = dy_ref[...]
        dyb = dy_t.astype(BF16)
        dhf = jnp.zeros((tm, d), F32)
        for k in range(f // tf):
            cols = slice(k * tf, (k + 1) * tf)
            da = _dot(dyb, w2_ref[cols, :], NT)
            du = (da * (2.0 * jnp.maximum(u_ref[:, cols], 0.0))).astype(BF16)
            du_ref[:, cols] = du
            dhf = dhf + _dot(du, w1_ref[:, cols], NT)
        dxn, dg = _rms_bwd(x2_ref[...], g_ref[...], dhf)
        dx2_ref[...] = dy_t + dxn
        dg_ref[...] += dg

    row = lambda i: (i, 0)
    fixed = lambda i: (0, 0)
    return pl.pallas_call(
        body, name="mlp_bwd", grid=(t_len // tm,),
        in_specs=[pl.BlockSpec((tm, d), row), pl.BlockSpec((tm, f), row), pl.BlockSpec((tm, d), row), pl.BlockSpec((1, d), fixed),
                  _resident((d, f)), _resident((f, d))],
        out_specs=[pl.BlockSpec((tm, f), row), pl.BlockSpec((tm, d), row), pl.BlockSpec((1, d), fixed)],
        out_shape=[jax.ShapeDtypeStruct((t_len, f), BF16), jax.ShapeDtypeStruct((t_len, d), F32), jax.ShapeDtypeStruct((1, d), F32)],
        compiler_params=_cparams(("arbitrary",)),
    )(dy, u, x2, g, w1, w2)


def _pad_lanes(v, offset=0, width=LANES):
    return jnp.zeros((1, width), F32).at[:, offset:offset + v.shape[1]].set(v)


def _col(v, offset=0, rows=SM_ROWS):
    return jnp.zeros((rows, 1), F32).at[offset:offset + v.shape[1], 0].set(v[0])


def _pack_small(g_mix, dgq, dgk, dbias, dgo, dac, dar, ddc, ddr, g_gdn_o, g_nx, g_mem, g_xq, g_xk, g_mlp, loss_tiles):
    def body(mix_ref, q_ref, k_ref, b_ref, o_ref, ac_ref, ar_ref, dc_ref, dr_ref, go_ref, nx_ref, mem_ref, xq_ref, xk_ref,
             mlp_ref, lt_ref, out_ref):
        lane = lax.broadcasted_iota(jnp.int32, (1, LANES), 1)
        diag = lax.broadcasted_iota(jnp.int32, (SM_ROWS, LANES), 0) == lax.broadcasted_iota(jnp.int32, (SM_ROWS, LANES), 1)

        def rolled(v, shift):
            return pltpu.roll(jnp.broadcast_to(v, (8, LANES)), shift, 1)[0:1, :]

        def rows_to_lanes(col):
            return jnp.sum(jnp.where(diag, col, 0.0), axis=0, keepdims=True)

        def put(row, v, n):
            out_ref[row:row + 1, 0:LANES] = jnp.where(lane < n, v, 0.0)

        out_ref[...] = jnp.zeros_like(out_ref)
        out_ref[0:1, :] = mix_ref[...]
        for row, ref in ((1, q_ref), (2, k_ref), (4, o_ref)):
            put(row, ref[...] + rolled(ref[...], FOX_HEAD_DIM), FOX_HEAD_DIM)
        put(3, rows_to_lanes(b_ref[...]), FOX_HEADS)
        for row, lane_ref, row_ref in ((5, ac_ref, ar_ref), (6, dc_ref, dr_ref)):
            put(row, rolled(lane_ref[...] + rows_to_lanes(row_ref[...]), LANES - SM_A), GDN_HEADS)
        put(7, go_ref[...], LANES)
        out_ref[8:9, :] = nx_ref[...]
        out_ref[9:10, :] = mem_ref[...]
        put(10, xq_ref[...], LANES)
        put(11, xk_ref[...], LANES)
        out_ref[12:13, :] = mlp_ref[...]
        put(LOSS_ROW, 0.5 * jnp.sum(lt_ref[...], axis=0)[0:1, :], 1)

    args = (g_mix, dgq, dgk, dbias, dgo, dac, dar, ddc, ddr, g_gdn_o, g_nx, g_mem, g_xq, g_xk, g_mlp, loss_tiles)
    return pl.pallas_call(body, name="pack_small", out_shape=jax.ShapeDtypeStruct((PACK_ROWS, D_MODEL), F32))(*args)


LATE_WEIGHTS = (("w_out", "w_cq", "w_ckv", "w_co"), ("w_mlp1", "w_mlp2"))
GRAD_GROUPS = (("w_mlp2", "w_mlp1"), ("w_co", "w_cq", "w_ckv", "w_out"), ("w_in", "gdn_conv_w"))


def _local_step(x, mem, target, norm_mix_g, w_in, fox_qnorm_g, fox_knorm_g, fox_f_bias, fox_onorm_g, gdn_conv_w, gdn_A_log,
                gdn_dt_bias, gdn_onorm_g, norm_xattn_g, mem_norm_g, xattn_qnorm_g, xattn_knorm_g, norm_mlp_g,
                late_weights, grads_ready=None, first_token=0.0):
    if grads_ready is None:
        grads_ready = lambda group: 0.0
    n_batch, s_len, d = x.shape
    m_len = mem.shape[1]
    t_len = n_batch * s_len
    tq = min(FOX_BLOCK, s_len)
    nq = s_len // tq
    n_chunks = s_len // GDN_CHUNK
    x2d = x.reshape(t_len, d)

    wp = jnp.concatenate([w_in[0:1536], w_in[1544:3080], w_in[3088:3600], w_in[1536:1544], w_in[3080:3088],
                          jnp.zeros((P_DIM - 3600, d), BF16)], axis=0)
    wst = jnp.concatenate([w_in[1536:1544], w_in[3080:3088]], axis=0)
    conv_w = jnp.concatenate([gdn_conv_w, jnp.zeros((8 - CONV_WIDTH, gdn_conv_w.shape[1]), F32)], axis=0)
    bias_col = _col(fox_f_bias, SM_F)
    gq2, gk2, go2 = (jnp.tile(g, (1, 2)) for g in (fox_qnorm_g, fox_knorm_g, fox_onorm_g))
    a_c, dt_c = _pad_lanes(gdn_A_log, SM_A), _pad_lanes(gdn_dt_bias, SM_A)
    a_r, dt_r = _col(gdn_A_log, SM_A), _col(gdn_dt_bias, SM_A)

    h1, pfox, pgdn, pz, sm, smt = _in_proj(x2d, norm_mix_g + first_token, wp, wst)
    c_rows = _fox_cum(smt, bias_col, n_batch, s_len)
    cb = c_rows.reshape(SM_ROWS, n_batch, nq, tq).transpose(1, 2, 0, 3)
    pf3 = pfox.reshape(n_batch, s_len, 1536)
    o_fox, oa, lse = _fox_fwd(pf3, cb, gq2, gk2, go2, tq)
    pg3 = pgdn.reshape(n_batch, s_len, 1536)
    qkvn = _gdn_pre(pg3, conv_w)
    z3 = pz.reshape(n_batch, s_len, GDN_WIDTH)
    smc = sm.reshape(n_batch, s_len, LANES)
    smr = smt.reshape(SM_ROWS, n_batch * n_chunks, GDN_CHUNK).transpose(1, 0, 2)
    ob, states = _gdn_fwd(qkvn, z3, smc, smr, a_c, dt_c, a_r, dt_r, gdn_onorm_g)
    oa2, ob2 = oa.reshape(t_len, FOX_WIDTH), ob.reshape(t_len, GDN_WIDTH)
    w_out, w_cq, w_ckv, w_co = late_weights(LATE_WEIGHTS[0], ob2)
    x1, hq, cq = _out_proj(x2d, oa2, ob2, w_out, norm_xattn_g, w_cq)
    mem2d = mem.reshape(n_batch * m_len, d)
    hm, ckv = _mem_kv(mem2d, mem_norm_g, w_ckv)
    co, x2, hf = _xattn_fwd(cq, ckv, x1, xattn_qnorm_g, xattn_knorm_g, w_co, norm_mlp_g, n_batch, s_len, m_len)
    w_mlp1, w_mlp2 = late_weights(LATE_WEIGHTS[1], hf)
    u, a_act, dy, loss_tiles = _mlp_fwd(hf, x2, target.reshape(t_len, d), w_mlp1, w_mlp2)

    grads = {}
    du, dx2, grads["norm_mlp_g"] = _mlp_bwd(dy, u, x2, norm_mlp_g, w_mlp1, w_mlp2)
    grads["w_mlp2"] = _wgrad(a_act, dy, "wgrad_mlp2")
    grads["w_mlp1"] = _wgrad(hf, du, "wgrad_mlp1", column_blocks=D_FF // N_DEV)
    token = grads_ready({k: grads[k] for k in GRAD_GROUPS[0]})
    grads["w_co"] = _wgrad(co, dx2, "wgrad_co", column_blocks=D_MODEL // N_DEV)
    dx1, dcq, dckv, grads["xattn_qnorm_g"], grads["xattn_knorm_g"], grads["norm_xattn_g"] = _xattn_bwd(
        dx2, cq, ckv, x1, xattn_qnorm_g + token, xattn_knorm_g, w_co, norm_xattn_g, w_cq, n_batch, s_len, m_len)
    grads["w_cq"] = _wgrad(hq, dcq, "wgrad_cq")
    grads["w_ckv"] = _wgrad(hm, dckv, "wgrad_ckv")
    grads["mem_norm_g"] = _mem_kv_bwd(dckv, mem2d, mem_norm_g, w_ckv)
    grads["w_out"] = _wgrad(jnp.concatenate([oa2, ob2], axis=1), dx1, "wgrad_out")
    token = grads_ready({k: grads[k] for k in GRAD_GROUPS[1]})
    dcat = _out_proj_bwd(dx1, w_out)
    dcat3 = dcat.reshape(n_batch, s_len, d)

    dqkvn, dz, dsmc, dsmr, dac, ddc, dar, ddr, grads["gdn_onorm_g"] = _gdn_bwd(
        qkvn, z3, smc, smr, a_c, dt_c, a_r, dt_r, gdn_onorm_g + token, states, dcat3)
    dpg, dconv = _gdn_pre_bwd(pg3, conv_w, dqkvn)
    grads["gdn_conv_w"] = dconv[0:CONV_WIDTH]

    dq, dk, dv, dcb, dgq, dgk, dgo = _fox_bwd(pf3, cb, gq2, gk2, go2, o_fox, lse, dcat3[:, :, 0:FOX_WIDTH], tq)
    dc8 = dcb[:, :, :, 0:2, :].transpose(1, 3, 0, 2, 4).reshape(FOX_HEADS, t_len)
    dc_rows = jnp.concatenate([dc8, jnp.zeros((SM_ROWS - FOX_HEADS, t_len), F32)], axis=0)
    dl_rows, dbias = _fox_cum_bwd(dc_rows, smt, bias_col, n_batch, s_len)
    dsm_rows = jnp.concatenate([dl_rows[0:SM_B], dsmr.transpose(1, 0, 2).reshape(SM_ROWS, t_len)[SM_B:SM_ROWS]], axis=0)

    dproj = jnp.concatenate([dq.reshape(t_len, FOX_WIDTH), dk.reshape(t_len, FOX_WIDTH), dv.reshape(t_len, FOX_WIDTH),
                             dpg.reshape(t_len, 1536), dz.reshape(t_len, GDN_WIDTH), dsmc.reshape(t_len, LANES).astype(BF16)], axis=1)
    dwp = _wgrad(dproj, h1, "wgrad_in", bk=P_DIM, bn=512)
    dwst = _rows_matmul(dsm_rows, h1, "wgrad_in_rows")
    dw_small = dwp[P_SMALL:P_SMALL + SM_ROWS] + dwst
    grads["w_in"] = jnp.concatenate([dwp[0:1536], dw_small[0:8], dwp[1536:3072], dw_small[8:16], dwp[3072:3584]], axis=0)
    token = grads_ready({k: grads[k] for k in GRAD_GROUPS[2]})
    grad_x, grads["norm_mix_g"] = _in_proj_bwd(dproj, dsm_rows, x2d, norm_mix_g + token, wp, wst, dx1)
    packed = _pack_small(grads["norm_mix_g"], dgq, dgk, dbias, dgo, dac, dar, ddc, ddr, grads["gdn_onorm_g"], grads["norm_xattn_g"],
                         grads["mem_norm_g"], grads["xattn_qnorm_g"], grads["xattn_knorm_g"], grads["norm_mlp_g"], loss_tiles)
    return packed, grad_x.reshape(n_batch, s_len, d), {k: grads[k] for k in SHARDED}


MESH_ID = pl.DeviceIdType.MESH
ANY_SPEC = pl.BlockSpec(memory_space=pl.ANY)


def _place():
    x, y, c = lax.axis_index("x"), lax.axis_index("y"), lax.axis_index("c")
    return x, y, c, [(1 - x, y), (x, 1 - y), (1 - x, 1 - y)]


def _place_own(src_ref, dst_ref):
    def staged(buf, sem):
        for a, b in ((src_ref, buf), (buf, dst_ref)):
            cp = pltpu.make_async_copy(a, b, sem)
            cp.start()
            cp.wait()

    pl.run_scoped(staged, pltpu.VMEM(src_ref.shape, src_ref.dtype), pltpu.SemaphoreType.DMA)


def _all_gather_body(n, ins, outs, send_sems, recv_sems, local_sems):
    x, y, c, chips = _place()
    me, sibling = (x, y, c), (x, y, 1 - c)

    def copy(a, k, block, to, src=None):
        dst = outs[a].at[4 * block[0] + 2 * block[1] + block[2]]
        return pltpu.make_async_remote_copy(src_ref=dst if src is None else src, dst_ref=dst, send_sem=send_sems.at[a, k],
                                            recv_sem=recv_sems.at[a, k], device_id=to, device_id_type=MESH_ID)

    mine = [] if local_sems is None else [pltpu.make_async_copy(ins[a], outs[a].at[4 * x + 2 * y + c], local_sems.at[a]) for a in range(n)]
    for cp in mine:
        cp.start()
    first = []
    for a in range(n):
        first.append(copy(a, 0, me, sibling, src=ins[a]))
        first += [copy(a, 1 + j, me, (*chip, c), src=ins[a]) for j, chip in enumerate(chips)]
    for cp in first:
        cp.start()
    if local_sems is None:
        for a in range(n):
            _place_own(ins[a], outs[a].at[4 * x + 2 * y + c])
    passed = []
    for j, chip in enumerate(chips):
        for a in range(n):
            copy(a, 1 + j, (*chip, c), me).wait_recv()
            fwd = copy(a, 4 + j, (*chip, c), sibling)
            fwd.start()
            passed.append(fwd)
    for a in range(n):
        copy(a, 0, sibling, me).wait_recv()
        for j, chip in enumerate(chips):
            copy(a, 4 + j, (*chip, 1 - c), me).wait_recv()
    for cp in first + passed:
        cp.wait_send()
    for cp in mine:
        cp.wait()


def _all_gather_hbm(arrs, name):
    n = len(arrs)

    def body(*refs):
        _all_gather_body(n, refs[:n], refs[n:2 * n], refs[2 * n], refs[2 * n + 1], None)

    return pl.pallas_call(
        body, name=name, in_specs=[ANY_SPEC] * n, out_specs=[ANY_SPEC] * n,
        out_shape=[jax.ShapeDtypeStruct((N_DEV,) + a.shape, a.dtype) for a in arrs],
        scratch_shapes=[pltpu.SemaphoreType.DMA((n, 7)), pltpu.SemaphoreType.DMA((n, 7))],
        compiler_params=pltpu.CompilerParams(vmem_limit_bytes=VMEM_LIMIT),
    )(*arrs)


def _pair_exchange(arrs, name):
    n = len(arrs)

    def body(*refs):
        ins, outs = refs[:n], refs[n:2 * n]
        send_sems, recv_sems = refs[2 * n:]
        x, y, c, _ = _place()
        copies = []
        for a in range(n):
            for chip in range(4):
                copies.append(pltpu.make_async_remote_copy(
                    src_ref=ins[a].at[2 * chip + (1 - c)], dst_ref=outs[a].at[chip], send_sem=send_sems.at[a, chip],
                    recv_sem=recv_sems.at[a, chip], device_id=(x, y, 1 - c), device_id_type=MESH_ID))
        for cp in copies:
            cp.start()
        for cp in copies:
            cp.wait()

    return pl.pallas_call(
        body, name=name, in_specs=[ANY_SPEC] * n, out_specs=[ANY_SPEC] * n,
        out_shape=[jax.ShapeDtypeStruct((4,) + a.shape[1:], a.dtype) for a in arrs],
        scratch_shapes=[pltpu.SemaphoreType.DMA((n, 4)), pltpu.SemaphoreType.DMA((n, 4))],
    )(*arrs)


HBM_SPEC = pl.BlockSpec(memory_space=pltpu.HBM)
SEM_SPEC = pl.BlockSpec(memory_space=pltpu.SEMAPHORE)
DATAFLOW = pltpu.SideEffectType.DATAFLOW_SIDE_EFFECTING


def _in_hbm(arrs):
    return [pltpu.with_memory_space_constraint(a, pltpu.HBM) for a in arrs]


def _copies_start(name, srcs, lands, make_copies, after):
    n = len(srcs)
    n_copies = len(make_copies(srcs, lands, None, None)[0])

    def body(*refs):
        send_sems, recv_sems = refs[2 * n + 1], refs[2 * n + 2]
        for row in make_copies(refs[:n], refs[n:2 * n], send_sems, recv_sems):
            for cp in row:
                cp.start()
        refs[-1][...] = jnp.zeros_like(refs[-1])

    sems = pltpu.SemaphoreType.DMA((n * n_copies,))
    thru = [pltpu.HBM(a.shape, a.dtype) for a in list(srcs) + list(lands)]
    res = pl.pallas_call(
        body, name=name, in_specs=[HBM_SPEC] * (2 * n) + [ANY_SPEC],
        out_specs=(SEM_SPEC, SEM_SPEC, *[HBM_SPEC] * (2 * n), pl.BlockSpec(memory_space=pltpu.VMEM)),
        out_shape=(sems, sems, *thru, jax.ShapeDtypeStruct((8, LANES), F32)),
        input_output_aliases={i: 2 + i for i in range(2 * n)},
        compiler_params=pltpu.CompilerParams(has_side_effects=DATAFLOW),
    )(*_in_hbm(list(srcs) + list(lands)), after)
    return res[0], res[1], list(res[2:2 + n]), list(res[2 + n:2 + 2 * n]), res[-1]


def _copies_wait(name, send_sems, recv_sems, srcs, lands, after, make_copies, own_block=False):
    n = len(srcs)

    def body(*refs):
        if own_block:
            me = 4 * lax.axis_index("x") + 2 * lax.axis_index("y") + lax.axis_index("c")
            for a in range(n):
                _place_own(refs[a], refs[3 * n + 3 + a].at[me])
        for row in make_copies(refs[:n], refs[n:2 * n], refs[2 * n], refs[2 * n + 1]):
            for cp in row:
                cp.wait_send()
                cp.wait_recv()

    res = pl.pallas_call(
        body, name=name, in_specs=[HBM_SPEC] * (2 * n) + [SEM_SPEC, SEM_SPEC, ANY_SPEC],
        out_specs=tuple([HBM_SPEC] * (2 * n)),
        out_shape=tuple(pltpu.HBM(a.shape, a.dtype) for a in list(srcs) + list(lands)),
        input_output_aliases={i: i for i in range(2 * n)},
        compiler_params=pltpu.CompilerParams(has_side_effects=DATAFLOW, vmem_limit_bytes=VMEM_LIMIT),
    )(*srcs, *lands, send_sems, recv_sems, after)
    return list(res[:n]), list(res[n:])


def _gather_copies(srcs, lands, send_sems, recv_sems):
    if send_sems is None:
        return [[None] * 7]
    x, y, c, _ = _place()
    rows = []
    for a in range(len(srcs)):
        row = []
        for k in range(7):
            r = k + 1
            to = (1 - x if r & 4 else x, 1 - y if r & 2 else y, 1 - c if r & 1 else c)
            row.append(pltpu.make_async_remote_copy(
                src_ref=srcs[a], dst_ref=lands[a].at[4 * x + 2 * y + c], send_sem=send_sems.at[7 * a + k], recv_sem=recv_sems.at[7 * a + k],
                device_id=to, device_id_type=MESH_ID))
        rows.append(row)
    return rows


def _scatter_copies(srcs, lands, send_sems, recv_sems):
    if send_sems is None:
        return [[None] * 7]
    x, y, c, _ = _place()
    rows = []
    for a in range(len(srcs)):
        row = []
        for k in range(7):
            r = k + 1
            to = (1 - x if r & 4 else x, 1 - y if r & 2 else y, 1 - c if r & 1 else c)
            row.append(pltpu.make_async_remote_copy(
                src_ref=srcs[a].at[4 * to[0] + 2 * to[1] + to[2]], dst_ref=lands[a].at[k], send_sem=send_sems.at[7 * a + k],
                recv_sem=recv_sems.at[7 * a + k], device_id=to, device_id_type=MESH_ID))
        rows.append(row)
    return rows


def _chip_copies(srcs, lands, send_sems, recv_sems):
    if send_sems is None:
        return [[None] * 3]
    x, y, c, chips = _place()
    return [[pltpu.make_async_remote_copy(
        src_ref=srcs[a].at[2 * chip[0] + chip[1]], dst_ref=lands[a].at[j], send_sem=send_sems.at[3 * a + j], recv_sem=recv_sems.at[3 * a + j],
        device_id=(*chip, c), device_id_type=MESH_ID) for j, chip in enumerate(chips)] for a in range(len(srcs))]


def _all_gather_vmem(block, name):
    def body(in_ref, out_ref, send_sems, recv_sems, local_sems):
        _all_gather_body(1, [in_ref], [out_ref], send_sems, recv_sems, local_sems)

    vmem = pl.BlockSpec(memory_space=pltpu.VMEM)
    return pl.pallas_call(
        body, name=name, in_specs=[vmem], out_specs=vmem,
        out_shape=jax.ShapeDtypeStruct((N_DEV,) + block.shape, block.dtype),
        scratch_shapes=[pltpu.SemaphoreType.DMA((1, 7)), pltpu.SemaphoreType.DMA((1, 7)), pltpu.SemaphoreType.DMA((1,))],
    )(block)


def _tile(rows, cols):
    if rows <= 256:
        return rows, cols
    tr = 256 if cols <= 512 else 128
    if rows % tr == 0:
        return tr, cols
    return rows, 256


def _pair_sum(core, own, got, name):
    _, rows, cols = own.shape
    tr, tc = _tile(rows, cols)

    def body(c_ref, own_ref, got_ref, o_ref):
        o_ref[0] = own_ref[0] + got_ref[0]

    return pl.pallas_call(
        body, name=name,
        grid_spec=pltpu.PrefetchScalarGridSpec(
            num_scalar_prefetch=1, grid=(4, rows // tr, cols // tc),
            in_specs=[pl.BlockSpec((1, tr, tc), lambda k, i, j, c: (2 * k + c[0], i, j)),
                      pl.BlockSpec((1, tr, tc), lambda k, i, j, c: (k, i, j))],
            out_specs=pl.BlockSpec((1, tr, tc), lambda k, i, j, c: (k, i, j))),
        out_shape=jax.ShapeDtypeStruct((4, rows, cols), F32),
        compiler_params=_cparams(("parallel", "parallel", "parallel")),
    )(core, own, got)


def _adamw(w, g, m, v):
    m_new = ADAM_B1 * m + (1.0 - ADAM_B1) * g
    v_new = ADAM_B2 * v + (1.0 - ADAM_B2) * (g * g)
    m_hat = m_new / (1.0 - ADAM_B1 ** ADAM_STEP)
    v_hat = v_new / (1.0 - ADAM_B2 ** ADAM_STEP)
    delta = -ADAM_LR * (m_hat / (jnp.sqrt(v_hat) + ADAM_EPS) + ADAM_WD * w)
    return delta, m_new, v_new


def _sum_adam(chip, sums, parts, w, m, v, name):
    n_parts, rows, cols = parts.shape
    tr, tc = _tile(rows, cols)

    def body(chip_ref, own_ref, p_ref, w_ref, m_ref, v_ref, g_ref, d_ref, mo_ref, vo_ref):
        g = own_ref[0]
        for k in range(n_parts):
            g = g + p_ref[k]
        g_ref[...] = g
        d_ref[...], mo_ref[...], vo_ref[...] = _adamw(w_ref[...], g, m_ref[...], v_ref[...])

    tile = pl.BlockSpec((tr, tc), lambda i, j, ch: (i, j))
    out = jax.ShapeDtypeStruct((rows, cols), F32)
    return pl.pallas_call(
        body, name=name,
        grid_spec=pltpu.PrefetchScalarGridSpec(
            num_scalar_prefetch=1, grid=(rows // tr, cols // tc),
            in_specs=[pl.BlockSpec((1, tr, tc), lambda i, j, ch: (ch[0], i, j)),
                      pl.BlockSpec((n_parts, tr, tc), lambda i, j, ch: (0, i, j)), tile, tile, tile],
            out_specs=[tile, tile, tile, tile]),
        out_shape=[out, out, out, out],
        compiler_params=_cparams(("parallel", "parallel")),
    )(chip, sums, parts, w, m, v)


SHARDED = ("w_in", "gdn_conv_w", "w_out", "w_cq", "w_ckv", "w_co", "w_mlp1", "w_mlp2")
TRANSPOSED = ("w_in",)
COLUMN_SHARDED = ("gdn_conv_w", "w_co", "w_mlp1")
REPLICATED = ("norm_mix_g", "fox_qnorm_g", "fox_knorm_g", "fox_f_bias", "fox_onorm_g", "gdn_A_log", "gdn_dt_bias", "gdn_onorm_g",
              "norm_xattn_g", "mem_norm_g", "xattn_qnorm_g", "xattn_knorm_g", "norm_mlp_g")
WEIGHTS = ("norm_mix_g", "w_in", "fox_qnorm_g", "fox_knorm_g", "fox_f_bias", "fox_onorm_g", "gdn_conv_w", "gdn_A_log", "gdn_dt_bias",
           "gdn_onorm_g", "w_out", "norm_xattn_g", "mem_norm_g", "w_cq", "w_ckv", "xattn_qnorm_g", "xattn_knorm_g", "w_co",
           "norm_mlp_g", "w_mlp1", "w_mlp2")
PACK_ROWS = 16
LOSS_ROW = len(REPLICATED)


def _whole(name, gathered):
    if name in COLUMN_SHARDED:
        return gathered.transpose(1, 0, 2).reshape(gathered.shape[1], N_DEV * gathered.shape[2])
    return gathered.reshape(N_DEV * gathered.shape[1], gathered.shape[2])


def _blocks(name, whole):
    if whole.ndim == 3:
        return whole
    if name in COLUMN_SHARDED:
        rows, cols = whole.shape
        return whole.reshape(rows, N_DEV, cols // N_DEV).transpose(1, 0, 2)
    return whole.reshape(N_DEV, whole.shape[0] // N_DEV, whole.shape[1])


def _adam_small(everyone, ws, ms, vs):
    n_par = len(ws)

    def body(*refs):
        ev_ref = refs[0]
        w_refs, m_refs, v_refs = (refs[1 + j * n_par:1 + (j + 1) * n_par] for j in range(3))
        outs = refs[1 + 3 * n_par:-1]
        sum_ref = refs[-1]
        total = ev_ref[0]
        for dev in range(1, N_DEV):
            total = total + ev_ref[dev]
        sum_ref[...] = total
        for i in range(n_par):
            n = w_refs[i].shape[1]
            g = sum_ref[i:i + 1, 0:n]
            outs[4 * i][...] = g
            outs[4 * i + 1][...], outs[4 * i + 2][...], outs[4 * i + 3][...] = _adamw(w_refs[i][...], g, m_refs[i][...], v_refs[i][...])
        outs[4 * n_par][...] = sum_ref[LOSS_ROW:LOSS_ROW + 1, 0:1]

    shapes = [jax.ShapeDtypeStruct(a.shape, F32) for a in ws for _ in range(4)] + [jax.ShapeDtypeStruct((1, 1), F32)]
    return pl.pallas_call(body, name="adam_small", out_shape=shapes,
                          scratch_shapes=[pltpu.VMEM((PACK_ROWS, D_MODEL), F32)])(everyone, *ws, *ms, *vs)


def kernel(x, mem, norm_mix_g, w_in, fox_qnorm_g, fox_knorm_g, fox_f_bias, fox_onorm_g, gdn_conv_w, gdn_A_log, gdn_dt_bias, gdn_onorm_g, w_out, norm_xattn_g, mem_norm_g, w_cq, w_ckv, xattn_qnorm_g, xattn_knorm_g, w_co, norm_mlp_g, w_mlp1, w_mlp2, loss_target, m_norm_mix_g, m_w_in, m_fox_qnorm_g, m_fox_knorm_g, m_fox_f_bias, m_fox_onorm_g, m_gdn_conv_w, m_gdn_A_log, m_gdn_dt_bias, m_gdn_onorm_g, m_w_out, m_norm_xattn_g, m_mem_norm_g, m_w_cq, m_w_ckv, m_xattn_qnorm_g, m_xattn_knorm_g, m_w_co, m_norm_mlp_g, m_w_mlp1, m_w_mlp2, v_norm_mix_g, v_w_in, v_fox_qnorm_g, v_fox_knorm_g, v_fox_f_bias, v_fox_onorm_g, v_gdn_conv_w, v_gdn_A_log, v_gdn_dt_bias, v_gdn_onorm_g, v_w_out, v_norm_xattn_g, v_mem_norm_g, v_w_cq, v_w_ckv, v_xattn_qnorm_g, v_xattn_knorm_g, v_w_co, v_norm_mlp_g, v_w_mlp1, v_w_mlp2):
    given = dict(locals())
    w = {k: given[k] for k in WEIGHTS}
    m = {k: given["m_" + k] for k in WEIGHTS}
    v = {k: given["v_" + k] for k in WEIGHTS}

    core = lax.axis_index("c").astype(jnp.int32).reshape(1)
    chip = (2 * lax.axis_index("x") + lax.axis_index("y")).astype(jnp.int32).reshape(1)
    me = 4 * lax.axis_index("x") + 2 * lax.axis_index("y") + lax.axis_index("c")

    local = lambda d: {k: jnp.transpose(d[k][0]) if k in TRANSPOSED else d[k][0] for k in SHARDED}
    w2, m2, v2 = local(w), local(m), local(v)
    shards = {k: w2[k] if k == "gdn_conv_w" else w2[k].astype(BF16) for k in SHARDED}
    early = [k for k in SHARDED if not any(k in group for group in LATE_WEIGHTS)]
    gathered = _all_gather_hbm([shards[k] for k in early], "gather_early")
    whole = {k: _whole(k, g) for k, g in zip(early, gathered)}
    gathers, after = {}, gathered[0]
    for i, group in enumerate(LATE_WEIGHTS):
        lands = [lax.empty((N_DEV,) + shards[k].shape, BF16) for k in group]
        gathers[group] = _copies_start("gather_late_start_" + str(i), [shards[k] for k in group], lands, _gather_copies, after=after)
        after = gathers[group][4]
    first_token = after[0, 0]

    def late_weights(group, after):
        gather = gathers[group]
        _, lands = _copies_wait("gather_late_wait_" + str(LATE_WEIGHTS.index(group)), gather[0], gather[1], gather[2], gather[3],
                                after, _gather_copies, own_block=True)
        return [_whole(k, land) for k, land in zip(group, lands)]

    pending = []

    def grads_ready(group):
        names = list(group)
        tag = str(len(pending))
        own = [_blocks(k, group[k]) for k in names]
        if "w_in" in names:
            got = _pair_exchange(own, "grad_pair_exchange_" + tag)
            srcs = [_pair_sum(core, o, g, "grad_pair_sum_" + k) for k, o, g in zip(names, own, got)]
            copies, index, n_parts = _chip_copies, chip, 3
        else:
            srcs, copies, index, n_parts = own, _scatter_copies, me.astype(jnp.int32).reshape(1), 7
        lands = [lax.empty((n_parts,) + s.shape[1:], s.dtype) for s in srcs]
        started = _copies_start("grad_exchange_start_" + tag, srcs, lands, copies, after=srcs[0])
        pending.append((names, started, copies, index))
        return started[4][0, 0]

    small = {k: w[k] for k in REPLICATED}
    packed, grad_x, _ = _local_step(x, mem, loss_target, **small, **whole, late_weights=late_weights,
                                    grads_ready=grads_ready, first_token=first_token)

    out_g, out_d, out_m, out_v = {}, {}, {}, {}
    after = grad_x
    for tag, (names, started, copies, index) in enumerate(pending):
        srcs, parts = _copies_wait("grad_exchange_wait_" + str(tag), started[0], started[1], started[2], started[3], after, copies)
        for k, s, p in zip(names, srcs, parts):
            res = _sum_adam(index, s, p, w2[k], m2[k], v2[k], "adam_" + k)
            out_g[k], out_d[k], out_m[k], out_v[k] = ((jnp.transpose(r) if k in TRANSPOSED else r)[None] for r in res)
            after = res[0]

    everyone = _all_gather_vmem(packed, "gather_small")
    res = _adam_small(everyone, [w[k] for k in REPLICATED], [m[k] for k in REPLICATED], [v[k] for k in REPLICATED])
    for i, k in enumerate(REPLICATED):
        out_g[k], out_d[k], out_m[k], out_v[k] = res[4 * i:4 * i + 4]
    loss = res[-1].reshape(())

    return (loss, grad_x, *[out_g[k] for k in WEIGHTS], *[out_d[k] for k in WEIGHTS], *[out_m[k] for k in WEIGHTS],
            *[out_v[k] for k in WEIGHTS])
```

```python
import functools

import jax
import jax.numpy as jnp
import numpy as np
from jax import lax
from jax.experimental import pallas as pl
from jax.experimental.pallas import tpu as pltpu

F32 = jnp.float32
BF16 = jnp.bfloat16

D_MODEL = 1024
FOX_HEADS = 8
FOX_HEAD_DIM = 64
FOX_WIDTH = 512
GDN_HEADS = 4
GDN_HEAD_DIM = 128
GDN_WIDTH = 512
CONV_WIDTH = 4
GDN_CHUNK = 128
GDN_GROUP = 4
FOX_BLOCK = 512
XATTN_HEADS = 4
XATTN_HEAD_DIM = 128
XATTN_WIDTH = 512
D_FF = 4096
EPS = 1e-6
NEG_INF = -1e30
N_DEV = 8

ADAM_LR = 0.001
ADAM_B1 = 0.9
ADAM_B2 = 0.999
ADAM_EPS = 1e-08
ADAM_WD = 0.01
ADAM_STEP = 10

P_FOX = 0
P_GDN = 1536
P_Z = 3072
P_SMALL = 3584
P_DIM = 3712
SM_F = 0
SM_B = 8
SM_A = 12
SM_ROWS = 16

LANES = 128
VMEM_LIMIT = 56 * 1024 * 1024

NN = (((1,), (0,)), ((), ()))
NT = (((1,), (1,)), ((), ()))
TN = (((0,), (0,)), ((), ()))


def _dot(a, b, dims=NN):
    return lax.dot_general(a.astype(BF16), b.astype(BF16), dims, preferred_element_type=F32)


def _cparams(sem=None):
    kw = dict(vmem_limit_bytes=VMEM_LIMIT)
    if sem is not None:
        kw["dimension_semantics"] = sem
    return pltpu.CompilerParams(**kw)


def _sigmoid(x):
    return 0.5 * (jnp.tanh(0.5 * x) + 1.0)


def _softplus(x):
    return jnp.maximum(x, 0.0) + jnp.log1p(jnp.exp(-jnp.abs(x)))


def _log_sigmoid(x):
    return -_softplus(-x)


def _rms(x, g):
    r = lax.rsqrt(jnp.mean(x * x, axis=-1, keepdims=True) + EPS)
    return x * r * g


def _rms_bwd(x, g, dy):
    r = lax.rsqrt(jnp.mean(x * x, axis=-1, keepdims=True) + EPS)
    xh = x * r
    dg = jnp.sum(dy * xh, axis=0, keepdims=True)
    dyg = dy * g
    dx = r * (dyg - xh * jnp.mean(dyg * xh, axis=-1, keepdims=True))
    return dx, dg


def _pair_stat(t, m0):
    s0 = jnp.sum(jnp.where(m0, t, 0.0), axis=-1, keepdims=True)
    s1 = jnp.sum(jnp.where(m0, 0.0, t), axis=-1, keepdims=True)
    return jnp.where(m0, s0, s1)


def _rms_pair(x, g, m0):
    r = lax.rsqrt(_pair_stat(x * x, m0) * (1.0 / FOX_HEAD_DIM) + EPS)
    return x * r * g


def _rms_pair_bwd(x, g, dy, m0):
    r = lax.rsqrt(_pair_stat(x * x, m0) * (1.0 / FOX_HEAD_DIM) + EPS)
    xh = x * r
    dg = jnp.sum(dy * xh, axis=0, keepdims=True)
    dyg = dy * g
    dx = r * (dyg - xh * (_pair_stat(dyg * xh, m0) * (1.0 / FOX_HEAD_DIM)))
    return dx, dg


@jax.custom_vjp
def _mm_nn(a, b):
    return _dot(a, b, NN)


_mm_nn.defvjp(lambda a, b: (_dot(a, b, NN), (a, b)),
              lambda r, g: (_dot(g, r[1], NT), _dot(r[0], g, TN)))


@jax.custom_vjp
def _mm_nt(a, b):
    return _dot(a, b, NT)


_mm_nt.defvjp(lambda a, b: (_dot(a, b, NT), (a, b)),
              lambda r, g: (_dot(g, r[1], NN), _dot(g, r[0], TN)))


@jax.custom_vjp
def _mm_tn(a, b):
    return _dot(a, b, TN)


_mm_tn.defvjp(lambda a, b: (_dot(a, b, TN), (a, b)),
              lambda r, g: (_dot(r[1], g, NT), _dot(r[0], g, NN)))


def _dot3(a, b, dims):
    ah = a.astype(BF16)
    al = (a - ah.astype(F32)).astype(BF16)
    bh = b.astype(BF16)
    bl = (b - bh.astype(F32)).astype(BF16)
    d = functools.partial(lax.dot_general, dimension_numbers=dims, preferred_element_type=F32)
    return d(ah, bh) + d(ah, bl) + d(al, bh)


def _neumann_inverses(mats):
    c = mats[0].shape[0]
    eye = (lax.broadcasted_iota(jnp.int32, (c, c), 0) == lax.broadcasted_iota(jnp.int32, (c, c), 1)).astype(F32)
    xs = [eye - a for a in mats]
    ps = list(mats)
    k = 2
    while k < c + 1:
        ps = [_dot3(p, p, NN) for p in ps]
        xs = [x + _dot3(x, p, NN) for x, p in zip(xs, ps)]
        k *= 2
    return xs


@jax.custom_vjp
def _unit_lower_inverses(mats):
    return _neumann_inverses(mats)


def _unit_lower_inverses_fwd(mats):
    ts = _neumann_inverses(mats)
    return ts, ts


def _unit_lower_inverses_bwd(ts, gs):
    left = [_dot3(t, g, TN) for t, g in zip(ts, gs)]
    return ([-_dot3(m, t, NT) for m, t in zip(left, ts)],)


_unit_lower_inverses.defvjp(_unit_lower_inverses_fwd, _unit_lower_inverses_bwd)


def _wgrad(a, b, name, bk=1024, bn=1024, bt=512, column_blocks=None):
    t_len, k_len = a.shape
    n_len = b.shape[1]
    bk, bn, bt = min(bk, k_len), min(bn, n_len), min(bt, t_len)
    nt = t_len // bt

    def body(a_ref, b_ref, o_ref, acc_ref):
        t = pl.program_id(2)

        @pl.when(t == 0)
        def _():
            acc_ref[...] = jnp.zeros_like(acc_ref)

        acc_ref[...] += _dot(a_ref[...], b_ref[...], TN)

        @pl.when(t == nt - 1)
        def _():
            if column_blocks:
                for jj in range(bn // column_blocks):
                    o_ref[jj] = acc_ref[:, jj * column_blocks:(jj + 1) * column_blocks]
            else:
                o_ref[...] = acc_ref[...]

    if column_blocks:
        out_spec = pl.BlockSpec((bn // column_blocks, bk, column_blocks), lambda i, j, t: (j, i, 0))
        out_shape = jax.ShapeDtypeStruct((n_len // column_blocks, k_len, column_blocks), F32)
    else:
        out_spec = pl.BlockSpec((bk, bn), lambda i, j, t: (i, j))
        out_shape = jax.ShapeDtypeStruct((k_len, n_len), F32)
    return pl.pallas_call(
        body, name=name, grid=(k_len // bk, n_len // bn, nt),
        in_specs=[pl.BlockSpec((bt, bk), lambda i, j, t: (t, i)), pl.BlockSpec((bt, bn), lambda i, j, t: (t, j))],
        out_specs=out_spec, out_shape=out_shape,
        scratch_shapes=[pltpu.VMEM((bk, bn), F32)],
        compiler_params=_cparams(("parallel", "parallel", "arbitrary")),
    )(a, b)


def _wgrad_stacked(pieces, b, name, bn=512, bt=512):
    t_len, n_len = b.shape
    n_p = len(pieces)
    starts = [int(s) for s in np.cumsum([0] + [p.shape[1] for p in pieces])]
    bn, bt = min(bn, n_len), min(bt, t_len)
    nt = t_len // bt

    def body(*refs):
        b_ref, o_ref, acc_ref = refs[n_p:]
        t = pl.program_id(1)

        @pl.when(t == 0)
        def _():
            acc_ref[...] = jnp.zeros_like(acc_ref)

        for k in range(n_p):
            acc_ref[starts[k]:starts[k + 1], :] += _dot(refs[k][...], b_ref[...], TN)

        @pl.when(t == nt - 1)
        def _():
            o_ref[...] = acc_ref[...]

    return pl.pallas_call(
        body, name=name, grid=(n_len // bn, nt),
        in_specs=[pl.BlockSpec((bt, p.shape[1]), lambda j, t: (t, 0)) for p in pieces] + [pl.BlockSpec((bt, bn), lambda j, t: (t, j))],
        out_specs=pl.BlockSpec((starts[-1], bn), lambda j, t: (0, j)),
        out_shape=jax.ShapeDtypeStruct((starts[-1], n_len), F32),
        scratch_shapes=[pltpu.VMEM((starts[-1], bn), F32)],
        compiler_params=_cparams(("parallel", "arbitrary")),
    )(*pieces, b)


def _rows_matmul(a, b, name, bt=512):
    r_len, t_len = a.shape
    n_len = b.shape[1]
    bt = min(bt, t_len)
    nt = t_len // bt

    def body(a_ref, b_ref, o_ref):
        t = pl.program_id(0)

        @pl.when(t == 0)
        def _():
            o_ref[...] = jnp.zeros_like(o_ref)

        o_ref[...] += _dot(a_ref[...], b_ref[...], NN)

    return pl.pallas_call(
        body, name=name, grid=(nt,),
        in_specs=[pl.BlockSpec((r_len, bt), lambda t: (0, t)), pl.BlockSpec((bt, n_len), lambda t: (t, 0))],
        out_specs=pl.BlockSpec((r_len, n_len), lambda t: (0, 0)),
        out_shape=jax.ShapeDtypeStruct((r_len, n_len), F32),
        compiler_params=_cparams(("arbitrary",)),
    )(a, b)


def _in_proj(x, g, wp, wst, tm=256):
    t_len, d = x.shape
    tm = min(tm, t_len)

    def body(x_ref, g_ref, wp_ref, wst_ref, h_ref, fox_ref, gdn_ref, z_ref, sm_ref, smt_ref):
        h = _rms(x_ref[...], g_ref[...]).astype(BF16)
        h_ref[...] = h
        p = _dot(h, wp_ref[...], NT)
        fox_ref[...] = p[:, P_FOX:P_GDN]
        gdn_ref[...] = p[:, P_GDN:P_Z]
        z_ref[...] = p[:, P_Z:P_SMALL]
        sm_ref[...] = p[:, P_SMALL:P_DIM]
        smt_ref[...] = _dot(wst_ref[...], h, NT)

    row = lambda i: (i, 0)
    fixed = lambda i: (0, 0)
    return pl.pallas_call(
        body, name="in_proj", grid=(t_len // tm,),
        in_specs=[pl.BlockSpec((tm, d), row), pl.BlockSpec((1, d), fixed), pl.BlockSpec((P_DIM, d), fixed),
                  pl.BlockSpec((SM_ROWS, d), fixed)],
        out_specs=[pl.BlockSpec((tm, d), row), pl.BlockSpec((tm, 1536), row), pl.BlockSpec((tm, 1536), row),
                   pl.BlockSpec((tm, 512), row), pl.BlockSpec((tm, LANES), row), pl.BlockSpec((SM_ROWS, tm), lambda i: (0, i))],
        out_shape=[jax.ShapeDtypeStruct((t_len, d), BF16), jax.ShapeDtypeStruct((t_len, 1536), F32),
                   jax.ShapeDtypeStruct((t_len, 1536), F32), jax.ShapeDtypeStruct((t_len, 512), F32),
                   jax.ShapeDtypeStruct((t_len, LANES), F32), jax.ShapeDtypeStruct((SM_ROWS, t_len), F32)],
        compiler_params=_cparams(("parallel",)),
    )(x, g, wp, wst)


def _in_proj_bwd(dprojs, dsmt, x, g, wp, wst, dx1, tm=256):
    t_len, d = x.shape
    tm = min(tm, t_len)
    n_p = len(dprojs)
    starts = np.cumsum([0] + [p.shape[1] for p in dprojs])

    def body(*refs):
        dp_refs = refs[:n_p]
        dst_ref, x_ref, g_ref, wp_ref, wst_ref, dx1_ref, dx_ref, dg_ref = refs[n_p:]
        i = pl.program_id(0)
        dh = _dot(dst_ref[...], wst_ref[...], TN)
        for k in range(n_p):
            dh = dh + _dot(dp_refs[k][...], wp_ref[int(starts[k]):int(starts[k + 1]), :], NN)
        dxn, dg = _rms_bwd(x_ref[...], g_ref[...], dh)
        dx_ref[...] = dx1_ref[...] + dxn

        @pl.when(i == 0)
        def _():
            dg_ref[...] = jnp.zeros_like(dg_ref)

        dg_ref[...] += dg

    row = lambda i: (i, 0)
    fixed = lambda i: (0, 0)
    return pl.pallas_call(
        body, name="in_proj_bwd", grid=(t_len // tm,),
        in_specs=[pl.BlockSpec((tm, p.shape[1]), row) for p in dprojs] + [
            pl.BlockSpec((SM_ROWS, tm), lambda i: (0, i)), pl.BlockSpec((tm, d), row),
            pl.BlockSpec((1, d), fixed), pl.BlockSpec((P_DIM, d), fixed), pl.BlockSpec((SM_ROWS, d), fixed),
            pl.BlockSpec((tm, d), row)],
        out_specs=[pl.BlockSpec((tm, d), row), pl.BlockSpec((1, d), fixed)],
        out_shape=[jax.ShapeDtypeStruct((t_len, d), F32), jax.ShapeDtypeStruct((1, d), F32)],
        compiler_params=_cparams(("arbitrary",)),
    )(*dprojs, dsmt, x, g, wp, wst, dx1)


def _fox_cum(smt, bias_col, n_batch, s_len, ck=256):
    ck = min(ck, s_len)

    def body(s_ref, b_ref, c_ref):
        tri = (lax.broadcasted_iota(jnp.int32, (ck, ck), 0) <= lax.broadcasted_iota(jnp.int32, (ck, ck), 1)).astype(F32)
        carry = jnp.zeros((SM_ROWS, 1), F32)
        for r in range(s_len // ck):
            ls = _log_sigmoid(s_ref[:, r * ck:(r + 1) * ck] + b_ref[...])
            c = jnp.dot(ls, tri, precision=lax.Precision.HIGHEST, preferred_element_type=F32) + carry
            c_ref[:, r * ck:(r + 1) * ck] = c
            carry = c[:, ck - 1:ck]

    return pl.pallas_call(
        body, name="fox_cum", grid=(n_batch,),
        in_specs=[pl.BlockSpec((SM_ROWS, s_len), lambda b: (0, b)), pl.BlockSpec((SM_ROWS, 1), lambda b: (0, 0))],
        out_specs=pl.BlockSpec((SM_ROWS, s_len), lambda b: (0, b)),
        out_shape=jax.ShapeDtypeStruct(smt.shape, F32),
        compiler_params=_cparams(("parallel",)),
    )(smt, bias_col)


def _fox_cum_bwd(dc, smt, bias_col, n_batch, s_len, ck=256):
    ck = min(ck, s_len)
    nr = s_len // ck

    def body(dc_ref, s_ref, b_ref, dl_ref, db_ref):
        b = pl.program_id(0)
        tri = (lax.broadcasted_iota(jnp.int32, (ck, ck), 0) >= lax.broadcasted_iota(jnp.int32, (ck, ck), 1)).astype(F32)
        carry = jnp.zeros((SM_ROWS, 1), F32)
        tot = jnp.zeros((SM_ROWS, 1), F32)
        for r in reversed(range(nr)):
            sl = slice(r * ck, (r + 1) * ck)
            dls = jnp.dot(dc_ref[:, sl], tri, precision=lax.Precision.HIGHEST, preferred_element_type=F32) + carry
            carry = dls[:, 0:1]
            dl = dls * (1.0 - _sigmoid(s_ref[:, sl] + b_ref[...]))
            dl_ref[:, sl] = dl
            tot = tot + jnp.sum(dl, axis=1, keepdims=True)

        @pl.when(b == 0)
        def _():
            db_ref[...] = jnp.zeros_like(db_ref)

        db_ref[...] += jnp.broadcast_to(tot, db_ref.shape)

    return pl.pallas_call(
        body, name="fox_cum_bwd", grid=(n_batch,),
        in_specs=[pl.BlockSpec((SM_ROWS, s_len), lambda b: (0, b)), pl.BlockSpec((SM_ROWS, s_len), lambda b: (0, b)),
                  pl.BlockSpec((SM_ROWS, 1), lambda b: (0, 0))],
        out_specs=[pl.BlockSpec((SM_ROWS, s_len), lambda b: (0, b)), pl.BlockSpec((SM_ROWS, LANES), lambda b: (0, 0))],
        out_shape=[jax.ShapeDtypeStruct(smt.shape, F32), jax.ShapeDtypeStruct((SM_ROWS, LANES), F32)],
        compiler_params=_cparams(("arbitrary",)),
    )(dc, smt, bias_col)


def _fox_diagonal_mask(tq):
    return lax.broadcasted_iota(jnp.int32, (tq, tq), 1) <= lax.broadcasted_iota(jnp.int32, (tq, tq), 0)


def _fox_fwd(pf, cb, gq2, gk2, go2, tq=256):
    n_batch, s_len, _ = pf.shape
    tq = min(tq, s_len)
    nq = s_len // tq
    scale = FOX_HEAD_DIM ** -0.5

    def body(q_ref, k_ref, v_ref, c_ref, gq_ref, gk_ref, go_ref, o_ref, on_ref, lse_ref, kh_ref, vh_ref):
        j = pl.program_id(1)
        i = pl.program_id(2)
        m0 = lax.broadcasted_iota(jnp.int32, (1, LANES), 1) < FOX_HEAD_DIM

        @pl.when(i == 0)
        def _():
            kn = _rms_pair(k_ref[0], gk_ref[...], m0)
            kh_ref[0] = jnp.where(m0, kn, 0.0).astype(BF16)
            kh_ref[1] = jnp.where(m0, 0.0, kn).astype(BF16)
            v = v_ref[0]
            vh_ref[0] = jnp.where(m0, v, 0.0).astype(BF16)
            vh_ref[1] = jnp.where(m0, 0.0, v).astype(BF16)

        qb = (_rms_pair(q_ref[0], gq_ref[...], m0) * scale).astype(BF16)

        def step(kb, carry, diagonal=False):
            ms, ls, acc = carry
            off = pl.multiple_of(kb * tq, tq)
            new_m, new_l, alphas, pv = [], [], [], []
            for hh in range(2):
                s = _dot(qb, kh_ref[hh, pl.ds(off, tq), :], NT)
                s = s - c_ref[0, kb, pl.ds(2 * j + hh, 1), :]
                if diagonal:
                    s = jnp.where(_fox_diagonal_mask(tq), s, NEG_INF)
                m_new = jnp.maximum(ms[hh], jnp.max(s, axis=-1, keepdims=True))
                alpha = jnp.exp(ms[hh] - m_new)
                p = jnp.exp(s - m_new)
                new_l.append(alpha * ls[hh] + jnp.sum(p, axis=-1, keepdims=True))
                new_m.append(m_new)
                alphas.append(alpha)
                pv.append(_dot(p, vh_ref[hh, pl.ds(off, tq), :], NN))
            acc = jnp.where(m0, alphas[0], alphas[1]) * acc + pv[0] + pv[1]
            return tuple(new_m), tuple(new_l), acc

        init_m = (jnp.full((tq, 1), NEG_INF, F32),) * 2
        init_l = (jnp.zeros((tq, 1), F32),) * 2
        carry = lax.fori_loop(0, i, step, (init_m, init_l, jnp.zeros((tq, LANES), F32)))
        ms, ls, acc = step(i, carry, diagonal=True)
        o = acc / jnp.where(m0, ls[0], ls[1])
        o_ref[0] = o
        on_ref[0] = _rms_pair(o, go_ref[...], m0).astype(BF16)
        lse_ref[0] = jnp.where(m0, ms[0] + jnp.log(ls[0]), ms[1] + jnp.log(ls[1]))

    fixed = lambda b, j, i: (0, 0)
    tile = lambda b, j, i: (b, i, j)
    return pl.pallas_call(
        body, name="fox_fwd", grid=(n_batch, 4, nq),
        in_specs=[pl.BlockSpec((1, tq, LANES), tile), pl.BlockSpec((1, s_len, LANES), lambda b, j, i: (b, 0, 4 + j)),
                  pl.BlockSpec((1, s_len, LANES), lambda b, j, i: (b, 0, 8 + j)),
                  pl.BlockSpec((1, nq, SM_ROWS, tq), lambda b, j, i: (b, 0, 0, 0)),
                  pl.BlockSpec((1, LANES), fixed), pl.BlockSpec((1, LANES), fixed), pl.BlockSpec((1, LANES), fixed)],
        out_specs=[pl.BlockSpec((1, tq, LANES), tile), pl.BlockSpec((1, tq, LANES), tile), pl.BlockSpec((1, tq, LANES), tile)],
        out_shape=[jax.ShapeDtypeStruct((n_batch, s_len, FOX_WIDTH), F32), jax.ShapeDtypeStruct((n_batch, s_len, FOX_WIDTH), BF16),
                   jax.ShapeDtypeStruct((n_batch, s_len, FOX_WIDTH), F32)],
        scratch_shapes=[pltpu.VMEM((2, s_len, LANES), BF16), pltpu.VMEM((2, s_len, LANES), BF16)],
        compiler_params=_cparams(("parallel", "parallel", "arbitrary")),
    )(pf, pf, pf, cb, gq2, gk2, go2)


def _fox_bwd(pf, cb, gq2, gk2, go2, o, lse, don, tq=256):
    n_batch, s_len, _ = pf.shape
    tq = min(tq, s_len)
    nq = s_len // tq
    scale = FOX_HEAD_DIM ** -0.5

    def body(q_ref, k_ref, v_ref, c_ref, gq_ref, gk_ref, go_ref, o_ref, lse_ref, don_ref,
             dq_ref, dk_ref, dv_ref, dc_ref, dgq_ref, dgk_ref, dgo_ref, kh_ref, vh_ref, dka_ref, dva_ref, dca_ref):
        b = pl.program_id(0)
        j = pl.program_id(1)
        i = pl.program_id(2)
        m0 = lax.broadcasted_iota(jnp.int32, (1, LANES), 1) < FOX_HEAD_DIM

        @pl.when((b == 0) & (j == 0) & (i == 0))
        def _():
            dgq_ref[...] = jnp.zeros_like(dgq_ref)
            dgk_ref[...] = jnp.zeros_like(dgk_ref)
            dgo_ref[...] = jnp.zeros_like(dgo_ref)

        @pl.when(i == 0)
        def _():
            kn = _rms_pair(k_ref[0], gk_ref[...], m0)
            kh_ref[0] = jnp.where(m0, kn, 0.0).astype(BF16)
            kh_ref[1] = jnp.where(m0, 0.0, kn).astype(BF16)
            v = v_ref[0]
            vh_ref[0] = jnp.where(m0, v, 0.0).astype(BF16)
            vh_ref[1] = jnp.where(m0, 0.0, v).astype(BF16)
            dka_ref[...] = jnp.zeros_like(dka_ref)
            dva_ref[...] = jnp.zeros_like(dva_ref)
            dca_ref[...] = jnp.zeros_like(dca_ref)

        q = q_ref[0]
        qn = _rms_pair(q, gq_ref[...], m0)
        qs = qn * scale
        qb = qs.astype(BF16)
        qh = (jnp.where(m0, qs, 0.0).astype(BF16), jnp.where(m0, 0.0, qs).astype(BF16))
        ot = o_ref[0]
        do, dgo = _rms_pair_bwd(ot, go_ref[...], don_ref[0], m0)
        dgo_ref[...] += dgo
        dd = do * ot
        delta = (jnp.sum(jnp.where(m0, dd, 0.0), axis=-1, keepdims=True), jnp.sum(jnp.where(m0, 0.0, dd), axis=-1, keepdims=True))
        doh = (jnp.where(m0, do, 0.0).astype(BF16), jnp.where(m0, 0.0, do).astype(BF16))
        lse_t = lse_ref[0]
        lse_h = (lse_t[:, 0:1], lse_t[:, FOX_HEAD_DIM:FOX_HEAD_DIM + 1])

        def step(kb, carry, diagonal=False):
            dqn, rs = carry
            rs = list(rs)
            off = pl.multiple_of(kb * tq, tq)
            for hh in range(2):
                kblk = kh_ref[hh, pl.ds(off, tq), :]
                vblk = vh_ref[hh, pl.ds(off, tq), :]
                s = _dot(qb, kblk, NT)
                s = s - c_ref[0, kb, pl.ds(2 * j + hh, 1), :]
                if diagonal:
                    s = jnp.where(_fox_diagonal_mask(tq), s, NEG_INF)
                p = jnp.exp(s - lse_h[hh])
                dp = _dot(doh[hh], vblk, NT)
                ds = p * (dp - delta[hh])
                dva_ref[pl.ds(off, tq), :] += _dot(p, doh[hh], TN)
                dka_ref[pl.ds(off, tq), :] += _dot(ds, qh[hh], TN)
                dca_ref[kb, hh:hh + 1, :] += -jnp.sum(ds, axis=0, keepdims=True)
                rs[hh] = rs[hh] + jnp.sum(ds, axis=-1, keepdims=True)
                dqn = dqn + _dot(ds, kblk, NN)
            return dqn, tuple(rs)

        carry = lax.fori_loop(0, i, step, (jnp.zeros((tq, LANES), F32), (jnp.zeros((tq, 1), F32),) * 2))
        dqn, rs = step(i, carry, diagonal=True)
        dqn = dqn * scale
        rs_rows = jnp.where(m0, rs[0], rs[1]).T
        dca_ref[i, 0:1, :] += rs_rows[0:1, :]
        dca_ref[i, 1:2, :] += rs_rows[FOX_HEAD_DIM:FOX_HEAD_DIM + 1, :]
        dq, dgq = _rms_pair_bwd(q, gq_ref[...], dqn, m0)
        dq_ref[0] = dq.astype(BF16)
        dgq_ref[...] += dgq

        @pl.when(i == nq - 1)
        def _():
            dk, dgk = _rms_pair_bwd(k_ref[0], gk_ref[...], dka_ref[...], m0)
            dk_ref[0] = dk.astype(BF16)
            dgk_ref[...] += dgk
            dv_ref[0] = dva_ref[...].astype(BF16)
            dc_ref[0, 0] = dca_ref[...]

    fixed = lambda b, j, i: (0, 0)
    tile = lambda b, j, i: (b, i, j)
    full = lambda b, j, i: (b, 0, j)
    wide = jax.ShapeDtypeStruct((n_batch, s_len, FOX_WIDTH), BF16)
    gain = jax.ShapeDtypeStruct((1, LANES), F32)
    return pl.pallas_call(
        body, name="fox_bwd", grid=(n_batch, 4, nq),
        in_specs=[pl.BlockSpec((1, tq, LANES), tile), pl.BlockSpec((1, s_len, LANES), lambda b, j, i: (b, 0, 4 + j)),
                  pl.BlockSpec((1, s_len, LANES), lambda b, j, i: (b, 0, 8 + j)),
                  pl.BlockSpec((1, nq, SM_ROWS, tq), lambda b, j, i: (b, 0, 0, 0)),
                  pl.BlockSpec((1, LANES), fixed), pl.BlockSpec((1, LANES), fixed), pl.BlockSpec((1, LANES), fixed),
                  pl.BlockSpec((1, tq, LANES), tile), pl.BlockSpec((1, tq, LANES), tile), pl.BlockSpec((1, tq, LANES), tile)],
        out_specs=[pl.BlockSpec((1, tq, LANES), tile), pl.BlockSpec((1, s_len, LANES), full), pl.BlockSpec((1, s_len, LANES), full),
                   pl.BlockSpec((1, 1, nq, 8, tq), lambda b, j, i: (b, j, 0, 0, 0)),
                   pl.BlockSpec((1, LANES), fixed), pl.BlockSpec((1, LANES), fixed), pl.BlockSpec((1, LANES), fixed)],
        out_shape=[wide, wide, wide, jax.ShapeDtypeStruct((n_batch, 4, nq, 8, tq), F32), gain, gain, gain],
        scratch_shapes=[pltpu.VMEM((2, s_len, LANES), BF16), pltpu.VMEM((2, s_len, LANES), BF16),
                        pltpu.VMEM((s_len, LANES), F32), pltpu.VMEM((s_len, LANES), F32), pltpu.VMEM((nq, 8, tq), F32)],
        compiler_params=_cparams(("arbitrary", "arbitrary", "arbitrary")),
    )(pf, pf, pf, cb, gq2, gk2, go2, o, lse, don)


def _shift_down(x, k):
    row = lax.broadcasted_iota(jnp.int32, x.shape, 0)
    return jnp.where(row >= k, pltpu.roll(x, k, 0), 0.0)


def _shift_up(x, k):
    n = x.shape[0]
    row = lax.broadcasted_iota(jnp.int32, x.shape, 0)
    return jnp.where(row < n - k, pltpu.roll(x, n - k, 0), 0.0)


def _conv_silu(x, w):
    y = w[3:4] * x + w[2:3] * _shift_down(x, 1) + w[1:2] * _shift_down(x, 2) + w[0:1] * _shift_down(x, 3)
    return y, y * _sigmoid(y)


def _gdn_pre(pg, conv_w):
    n_batch, s_len, width = pg.shape
    ncb = width // LANES

    def body(x_ref, w_ref, o_ref):
        cb = pl.program_id(1)
        _, s = _conv_silu(x_ref[0], w_ref[...])
        sn = s * lax.rsqrt(jnp.sum(s * s, axis=-1, keepdims=True) + EPS)
        o_ref[0] = jnp.where(cb < 2 * GDN_HEADS, sn, s)

    return pl.pallas_call(
        body, name="gdn_pre", grid=(n_batch, ncb),
        in_specs=[pl.BlockSpec((1, s_len, LANES), lambda b, c: (b, 0, c)), pl.BlockSpec((8, LANES), lambda b, c: (0, c))],
        out_specs=pl.BlockSpec((1, s_len, LANES), lambda b, c: (b, 0, c)),
        out_shape=jax.ShapeDtypeStruct(pg.shape, F32),
        compiler_params=_cparams(("parallel", "parallel")),
    )(pg, conv_w)


def _gdn_pre_bwd(pg, conv_w, dout):
    n_batch, s_len, width = pg.shape
    ncb = width // LANES

    def body(x_ref, w_ref, d_ref, dx_ref, dw_ref):
        cb = pl.program_id(0)
        b = pl.program_id(1)
        x = x_ref[0]
        w = w_ref[...]
        d = d_ref[0]
        y, s = _conv_silu(x, w)
        rr = lax.rsqrt(jnp.sum(s * s, axis=-1, keepdims=True) + EPS)
        sn = s * rr
        ds_n = rr * (d - sn * jnp.sum(d * sn, axis=-1, keepdims=True))
        ds = jnp.where(cb < 2 * GDN_HEADS, ds_n, d)
        sig = _sigmoid(y)
        dy = ds * (sig * (1.0 + y * (1.0 - sig)))
        dx = w[3:4] * dy + w[2:3] * _shift_up(dy, 1) + w[1:2] * _shift_up(dy, 2) + w[0:1] * _shift_up(dy, 3)
        dx_ref[0] = dx.astype(BF16)
        dw = [jnp.sum(dy * _shift_down(x, 3 - jj), axis=0, keepdims=True) if jj < 3 else jnp.sum(dy * x, axis=0, keepdims=True)
              for jj in range(CONV_WIDTH)]
        rows = lax.broadcasted_iota(jnp.int32, (8, LANES), 0)
        dwb = jnp.zeros((8, LANES), F32)
        for jj in range(CONV_WIDTH):
            dwb = dwb + jnp.where(rows == jj, dw[jj], 0.0)

        @pl.when(b == 0)
        def _():
            dw_ref[...] = jnp.zeros_like(dw_ref)

        dw_ref[...] += dwb

    blk = lambda c, b: (b, 0, c)
    return pl.pallas_call(
        body, name="gdn_pre_bwd", grid=(ncb, n_batch),
        in_specs=[pl.BlockSpec((1, s_len, LANES), blk), pl.BlockSpec((8, LANES), lambda c, b: (0, c)), pl.BlockSpec((1, s_len, LANES), blk)],
        out_specs=[pl.BlockSpec((1, s_len, LANES), blk), pl.BlockSpec((8, LANES), lambda c, b: (0, c))],
        out_shape=[jax.ShapeDtypeStruct(pg.shape, BF16), jax.ShapeDtypeStruct((8, width), F32)],
        compiler_params=_cparams(("parallel", "arbitrary")),
    )(pg, conv_w, dout)


def _gdn_gates(smc, smr, a_c, dt_c, a_r, dt_r, h):
    lane = lax.broadcasted_iota(jnp.int32, (1, LANES), 1)
    sub = lax.broadcasted_iota(jnp.int32, (SM_ROWS, 1), 0)
    beta_c = jnp.sum(jnp.where(lane == SM_B + h, _sigmoid(smc), 0.0), axis=1, keepdims=True)
    g_all_c = -jnp.exp(a_c) * _softplus(smc + dt_c)
    g_c = jnp.sum(jnp.where(lane == SM_A + h, g_all_c, 0.0), axis=1, keepdims=True)
    g_all_r = -jnp.exp(a_r) * _softplus(smr + dt_r)
    g_r = jnp.sum(jnp.where(sub == SM_A + h, g_all_r, 0.0), axis=0, keepdims=True)
    return beta_c, g_c, g_r


def _gdn_group(qkv, z, smc, smr, a_c, dt_c, a_r, dt_r, go, states):
    n_grp = len(qkv)
    c = qkv[0].shape[0]
    hd = GDN_HEAD_DIM
    pairs = [(g, h) for g in range(n_grp) for h in range(GDN_HEADS)]
    ii = lax.broadcasted_iota(jnp.int32, (c, c), 0)
    jj = lax.broadcasted_iota(jnp.int32, (c, c), 1)
    incl = ii >= jj
    col = lambda arr, base, h: arr[:, base + h * hd:base + (h + 1) * hd]

    qs, ks, kbs, vbs, decays, gcs, g_lasts, amats = [], [], [], [], [], [], [], []
    for g, h in pairs:
        beta_c, g_c, g_r = _gdn_gates(smc[g], smr[g], a_c, dt_c, a_r, dt_r, h)
        gc_c = jnp.sum(jnp.where(incl, g_r, 0.0), axis=1, keepdims=True)
        gc_r = jnp.sum(jnp.where(ii <= jj, g_c, 0.0), axis=0, keepdims=True)
        decay = jnp.where(incl, jnp.exp(jnp.where(incl, gc_c - gc_r, 0.0)), 0.0)
        k = col(qkv[g], GDN_WIDTH, h)
        kb = k * beta_c
        qs.append(col(qkv[g], 0, h) * (hd ** -0.5))
        ks.append(k)
        kbs.append(kb)
        vbs.append(col(qkv[g], 2 * GDN_WIDTH, h) * beta_c)
        decays.append(decay)
        gcs.append(gc_c)
        g_lasts.append(jnp.sum(g_c, axis=0, keepdims=True))
        amats.append(jnp.where(ii > jj, _mm_nt(kb, k) * decay, 0.0))
    ts = _unit_lower_inverses(amats)
    egcs = [jnp.exp(gc) for gc in gcs]
    us = [_mm_nn(t, vb) for t, vb in zip(ts, vbs)]
    ws = [_mm_nn(t, kb * e) for t, kb, e in zip(ts, kbs, egcs)]
    intras = [_mm_nt(q, k) * d for q, k, d in zip(qs, ks, decays)]
    qes = [q * e for q, e in zip(qs, egcs)]
    kds = [k * jnp.exp(gl - gc) for k, gl, gc in zip(ks, g_lasts, gcs)]
    sdecs = [jnp.exp(gl) for gl in g_lasts]

    outs = []
    for g in range(n_grp):
        idx = [g * GDN_HEADS + h for h in range(GDN_HEADS)]
        v_new = [us[i] - _mm_nn(ws[i], states[h]) for h, i in enumerate(idx)]
        o_state = [_mm_nn(qes[i], states[h]) for h, i in enumerate(idx)]
        o_intra = [_mm_nn(intras[i], v_new[h]) for h, i in enumerate(idx)]
        states = [states[h] * sdecs[i] + _mm_tn(kds[i], v_new[h]) for h, i in enumerate(idx)]
        outs.append([_rms(o_state[h] + o_intra[h], go) * (col(z[g], 0, h) * _sigmoid(col(z[g], 0, h))) for h in range(GDN_HEADS)])
    return outs, states


def _gdn_group_size(n_chunks):
    return GDN_GROUP if n_chunks % GDN_GROUP == 0 else 1


def _gdn_fwd(qkvn, z, smc, smr, a_c, dt_c, a_r, dt_r, go):
    n_batch, s_len, _ = qkvn.shape
    c = GDN_CHUNK
    n = s_len // c
    grp = _gdn_group_size(n)
    ng = n // grp
    gc = grp * c
    hd = GDN_HEAD_DIM

    def body(qkv_ref, z_ref, smc_ref, smr_ref, ac_ref, dc_ref, ar_ref, dr_ref, go_ref, og_ref, st_ref, s_ref):
        @pl.when(pl.program_id(1) == 0)
        def _():
            s_ref[...] = jnp.zeros_like(s_ref)

        states = [s_ref[h] for h in range(GDN_HEADS)]
        for h in range(GDN_HEADS):
            st_ref[0, 0, h] = states[h]
        rows = lambda k: slice(k * c, (k + 1) * c)
        outs, nxt = _gdn_group([qkv_ref[0, rows(k), :] for k in range(grp)], [z_ref[0, rows(k), :] for k in range(grp)],
                               [smc_ref[0, rows(k), :] for k in range(grp)], [smr_ref[k] for k in range(grp)],
                               ac_ref[...], dc_ref[...], ar_ref[...], dr_ref[...], go_ref[...], states)
        for k in range(grp):
            for h in range(GDN_HEADS):
                og_ref[0, rows(k), h * hd:(h + 1) * hd] = outs[k][h].astype(BF16)
        for h in range(GDN_HEADS):
            s_ref[h] = nxt[h]

    tok = lambda b, i: (b, i, 0)
    fixed = lambda b, i: (0, 0)
    return pl.pallas_call(
        body, name="gdn_fwd", grid=(n_batch, ng),
        in_specs=[pl.BlockSpec((1, gc, 3 * GDN_WIDTH), tok), pl.BlockSpec((1, gc, GDN_WIDTH), tok), pl.BlockSpec((1, gc, LANES), tok),
                  pl.BlockSpec((grp, SM_ROWS, c), lambda b, i: (b * ng + i, 0, 0)),
                  pl.BlockSpec((1, LANES), fixed), pl.BlockSpec((1, LANES), fixed), pl.BlockSpec((SM_ROWS, 1), fixed),
                  pl.BlockSpec((SM_ROWS, 1), fixed), pl.BlockSpec((1, LANES), fixed)],
        out_specs=[pl.BlockSpec((1, gc, GDN_WIDTH), tok), pl.BlockSpec((1, 1, GDN_HEADS, hd, hd), lambda b, i: (b, i, 0, 0, 0))],
        out_shape=[jax.ShapeDtypeStruct((n_batch, s_len, GDN_WIDTH), BF16), jax.ShapeDtypeStruct((n_batch, ng, GDN_HEADS, hd, hd), F32)],
        scratch_shapes=[pltpu.VMEM((GDN_HEADS, hd, hd), F32)],
        compiler_params=_cparams(("parallel", "arbitrary")),
    )(qkvn, z, smc, smr, a_c, dt_c, a_r, dt_r, go)


def _gdn_bwd(qkvn, z, smc, smr, a_c, dt_c, a_r, dt_r, go, states, dog):
    n_batch, s_len, _ = qkvn.shape
    c = GDN_CHUNK
    n = s_len // c
    grp = _gdn_group_size(n)
    ng = n // grp
    gc = grp * c
    hd = GDN_HEAD_DIM

    def body(qkv_ref, z_ref, smc_ref, smr_ref, ac_ref, dc_ref, ar_ref, dr_ref, go_ref, st_ref, dog_ref,
             dqkv_ref, dz_ref, dsmc_ref, dsmr_ref, dac_ref, ddc_ref, dar_ref, ddr_ref, dgo_ref, ds_ref):
        first = (pl.program_id(0) == 0) & (pl.program_id(1) == 0)

        @pl.when(pl.program_id(1) == 0)
        def _():
            ds_ref[...] = jnp.zeros_like(ds_ref)

        @pl.when(first)
        def _():
            for r in (dac_ref, ddc_ref, dar_ref, ddr_ref, dgo_ref):
                r[...] = jnp.zeros_like(r)

        rows = lambda k: slice(k * c, (k + 1) * c)
        states = [st_ref[0, 0, h] for h in range(GDN_HEADS)]
        prim = ([qkv_ref[0, rows(k), :] for k in range(grp)], [z_ref[0, rows(k), :] for k in range(grp)],
                [smc_ref[0, rows(k), :] for k in range(grp)], [smr_ref[k] for k in range(grp)],
                ac_ref[...], dc_ref[...], ar_ref[...], dr_ref[...], go_ref[...], states)
        _, vjp = jax.vjp(_gdn_group, *prim)
        cot = ([[dog_ref[0, rows(k), h * hd:(h + 1) * hd] for h in range(GDN_HEADS)] for k in range(grp)],
               [ds_ref[h] for h in range(GDN_HEADS)])
        dqkv, dz, dsmc, dsmr, dac, ddc, dar, ddr, dgo, dstates = vjp(cot)
        for k in range(grp):
            dqkv_ref[0, rows(k), :] = dqkv[k]
            dz_ref[0, rows(k), :] = dz[k].astype(BF16)
            dsmc_ref[0, rows(k), :] = dsmc[k]
            dsmr_ref[k] = dsmr[k]
        dac_ref[...] += dac
        ddc_ref[...] += ddc
        dar_ref[...] += dar
        ddr_ref[...] += ddr
        dgo_ref[...] += dgo
        for h in range(GDN_HEADS):
            ds_ref[h] = dstates[h]

    tok = lambda b, i: (b, ng - 1 - i, 0)
    fixed = lambda b, i: (0, 0)
    lane_vec = jax.ShapeDtypeStruct((1, LANES), F32)
    row_vec = jax.ShapeDtypeStruct((SM_ROWS, 1), F32)
    return pl.pallas_call(
        body, name="gdn_bwd", grid=(n_batch, ng),
        in_specs=[pl.BlockSpec((1, gc, 3 * GDN_WIDTH), tok), pl.BlockSpec((1, gc, GDN_WIDTH), tok), pl.BlockSpec((1, gc, LANES), tok),
                  pl.BlockSpec((grp, SM_ROWS, c), lambda b, i: (b * ng + ng - 1 - i, 0, 0)),
                  pl.BlockSpec((1, LANES), fixed), pl.BlockSpec((1, LANES), fixed), pl.BlockSpec((SM_ROWS, 1), fixed),
                  pl.BlockSpec((SM_ROWS, 1), fixed), pl.BlockSpec((1, LANES), fixed),
                  pl.BlockSpec((1, 1, GDN_HEADS, hd, hd), lambda b, i: (b, ng - 1 - i, 0, 0, 0)),
                  pl.BlockSpec((1, gc, GDN_WIDTH), lambda b, i: (b, ng - 1 - i, 1))],
        out_specs=[pl.BlockSpec((1, gc, 3 * GDN_WIDTH), tok), pl.BlockSpec((1, gc, GDN_WIDTH), tok), pl.BlockSpec((1, gc, LANES), tok),
                   pl.BlockSpec((grp, SM_ROWS, c), lambda b, i: (b * ng + ng - 1 - i, 0, 0)),
                   pl.BlockSpec((1, LANES), fixed), pl.BlockSpec((1, LANES), fixed), pl.BlockSpec((SM_ROWS, 1), fixed),
                   pl.BlockSpec((SM_ROWS, 1), fixed), pl.BlockSpec((1, LANES), fixed)],
        out_shape=[jax.ShapeDtypeStruct((n_batch, s_len, 3 * GDN_WIDTH), F32), jax.ShapeDtypeStruct((n_batch, s_len, GDN_WIDTH), BF16),
                   jax.ShapeDtypeStruct((n_batch, s_len, LANES), F32), jax.ShapeDtypeStruct((n_batch * n, SM_ROWS, c), F32),
                   lane_vec, lane_vec, row_vec, row_vec, lane_vec],
        scratch_shapes=[pltpu.VMEM((GDN_HEADS, hd, hd), F32)],
        compiler_params=_cparams(("arbitrary", "arbitrary")),
    )(qkvn, z, smc, smr, a_c, dt_c, a_r, dt_r, go, states, dog)


def _out_proj(x, oa, ob, w_out, g_x, w_cq, tm=256):
    t_len, d = x.shape
    tm = min(tm, t_len)

    def body(x_ref, oa_ref, ob_ref, wo_ref, g_ref, wq_ref, x1_ref, hq_ref, cq_ref):
        x1 = x_ref[...] + _dot(oa_ref[...], wo_ref[0:FOX_WIDTH, :]) + _dot(ob_ref[...], wo_ref[FOX_WIDTH:2 * FOX_WIDTH, :])
        x1_ref[...] = x1
        hq = _rms(x1, g_ref[...]).astype(BF16)
        hq_ref[...] = hq
        cq_ref[...] = _dot(hq, wq_ref[...])

    row = lambda i: (i, 0)
    fixed = lambda i: (0, 0)
    return pl.pallas_call(
        body, name="out_proj", grid=(t_len // tm,),
        in_specs=[pl.BlockSpec((tm, d), row), pl.BlockSpec((tm, FOX_WIDTH), row), pl.BlockSpec((tm, GDN_WIDTH), row),
                  pl.BlockSpec((d, d), fixed), pl.BlockSpec((1, d), fixed), pl.BlockSpec((d, XATTN_WIDTH), fixed)],
        out_specs=[pl.BlockSpec((tm, d), row), pl.BlockSpec((tm, d), row), pl.BlockSpec((tm, XATTN_WIDTH), row)],
        out_shape=[jax.ShapeDtypeStruct((t_len, d), F32), jax.ShapeDtypeStruct((t_len, d), BF16), jax.ShapeDtypeStruct((t_len, XATTN_WIDTH), F32)],
        compiler_params=_cparams(("parallel",)),
    )(x, oa, ob, w_out, g_x, w_cq)


def _out_proj_bwd(dx1, w_out, tm=512):
    t_len, d = dx1.shape
    tm = min(tm, t_len)

    def body(dx_ref, w_ref, o_ref):
        o_ref[...] = _dot(dx_ref[...], w_ref[...], NT)

    return pl.pallas_call(
        body, name="out_proj_bwd", grid=(t_len // tm,),
        in_specs=[pl.BlockSpec((tm, d), lambda i: (i, 0)), pl.BlockSpec((d, d), lambda i: (0, 0))],
        out_specs=pl.BlockSpec((tm, d), lambda i: (i, 0)),
        out_shape=jax.ShapeDtypeStruct((t_len, d), F32),
        compiler_params=_cparams(("parallel",)),
    )(dx1, w_out)


def _mem_kv(mem, g, w_ckv, tm=256):
    t_len, d = mem.shape
    tm = min(tm, t_len)

    def body(x_ref, g_ref, w_ref, h_ref, o_ref):
        h = _rms(x_ref[...], g_ref[...]).astype(BF16)
        h_ref[...] = h
        o_ref[...] = _dot(h, w_ref[...])

    row = lambda i: (i, 0)
    fixed = lambda i: (0, 0)
    return pl.pallas_call(
        body, name="mem_kv", grid=(t_len // tm,),
        in_specs=[pl.BlockSpec((tm, d), row), pl.BlockSpec((1, d), fixed), pl.BlockSpec((d, 2 * XATTN_WIDTH), fixed)],
        out_specs=[pl.BlockSpec((tm, d), row), pl.BlockSpec((tm, 2 * XATTN_WIDTH), row)],
        out_shape=[jax.ShapeDtypeStruct((t_len, d), BF16), jax.ShapeDtypeStruct((t_len, 2 * XATTN_WIDTH), F32)],
        compiler_params=_cparams(("parallel",)),
    )(mem, g, w_ckv)


def _mem_kv_bwd(dckv, mem, g, w_ckv, tm=256):
    t_len, d = mem.shape
    tm = min(tm, t_len)

    def body(d_ref, x_ref, g_ref, w_ref, dg_ref):
        @pl.when(pl.program_id(0) == 0)
        def _():
            dg_ref[...] = jnp.zeros_like(dg_ref)

        dh = _dot(d_ref[...], w_ref[...], NT)
        _, dg = _rms_bwd(x_ref[...], g_ref[...], dh)
        dg_ref[...] += dg

    row = lambda i: (i, 0)
    fixed = lambda i: (0, 0)
    return pl.pallas_call(
        body, name="mem_kv_bwd", grid=(t_len // tm,),
        in_specs=[pl.BlockSpec((tm, 2 * XATTN_WIDTH), row), pl.BlockSpec((tm, d), row), pl.BlockSpec((1, d), fixed),
                  pl.BlockSpec((d, 2 * XATTN_WIDTH), fixed)],
        out_specs=pl.BlockSpec((1, d), fixed),
        out_shape=jax.ShapeDtypeStruct((1, d), F32),
        compiler_params=_cparams(("arbitrary",)),
    )(dckv, mem, g, w_ckv)


def _xattn_probs(qn, kn):
    s = _dot(qn, kn, NT) * (XATTN_HEAD_DIM ** -0.5)
    p = jnp.exp(s - jnp.max(s, axis=-1, keepdims=True))
    return p / jnp.sum(p, axis=-1, keepdims=True)


def _xattn_fwd(cq, ckv, x1, gq, gk, w_co, g_mlp, n_batch, s_len, m_len, tq=512):
    d = x1.shape[1]
    tq = min(tq, s_len)
    nq = s_len // tq
    hd = XATTN_HEAD_DIM

    def body(cq_ref, kv_ref, x1_ref, gq_ref, gk_ref, wo_ref, gm_ref, co_ref, x2_ref, hf_ref):
        outs = []
        for h in range(XATTN_HEADS):
            qn = _rms(cq_ref[:, h * hd:(h + 1) * hd], gq_ref[...])
            kn = _rms(kv_ref[:, h * hd:(h + 1) * hd], gk_ref[...])
            p = _xattn_probs(qn, kn)
            outs.append(_dot(p, kv_ref[:, XATTN_WIDTH + h * hd:XATTN_WIDTH + (h + 1) * hd]).astype(BF16))
        x2 = x1_ref[...]
        for h in range(XATTN_HEADS):
            co_ref[:, h * hd:(h + 1) * hd] = outs[h]
            x2 = x2 + _dot(outs[h], wo_ref[h * hd:(h + 1) * hd, :])
        x2_ref[...] = x2
        hf_ref[...] = _rms(x2, gm_ref[...]).astype(BF16)

    row = lambda b, i: (b * nq + i, 0)
    fixed = lambda b, i: (0, 0)
    t_len = n_batch * s_len
    return pl.pallas_call(
        body, name="xattn_fwd", grid=(n_batch, nq),
        in_specs=[pl.BlockSpec((tq, XATTN_WIDTH), row), pl.BlockSpec((m_len, 2 * XATTN_WIDTH), lambda b, i: (b, 0)),
                  pl.BlockSpec((tq, d), row), pl.BlockSpec((1, hd), fixed), pl.BlockSpec((1, hd), fixed),
                  pl.BlockSpec((XATTN_WIDTH, d), fixed), pl.BlockSpec((1, d), fixed)],
        out_specs=[pl.BlockSpec((tq, XATTN_WIDTH), row), pl.BlockSpec((tq, d), row), pl.BlockSpec((tq, d), row)],
        out_shape=[jax.ShapeDtypeStruct((t_len, XATTN_WIDTH), BF16), jax.ShapeDtypeStruct((t_len, d), F32),
                   jax.ShapeDtypeStruct((t_len, d), BF16)],
        compiler_params=_cparams(("parallel", "parallel")),
    )(cq, ckv, x1, gq, gk, w_co, g_mlp)


def _xattn_bwd(dx2, cq, ckv, x1, gq, gk, w_co, g_x, w_cq, n_batch, s_len, m_len, tq=512):
    d = x1.shape[1]
    tq = min(tq, s_len)
    nq = s_len // tq
    hd = XATTN_HEAD_DIM
    scale = XATTN_HEAD_DIM ** -0.5

    def body(dx2_ref, cq_ref, kv_ref, x1_ref, gq_ref, gk_ref, wo_ref, gx_ref, wq_ref,
             dx1_ref, dcq_ref, dkv_ref, dgq_ref, dgk_ref, dgx_ref, dk_acc, dv_acc):
        b = pl.program_id(0)
        i = pl.program_id(1)

        @pl.when((b == 0) & (i == 0))
        def _():
            dgq_ref[...] = jnp.zeros_like(dgq_ref)
            dgk_ref[...] = jnp.zeros_like(dgk_ref)
            dgx_ref[...] = jnp.zeros_like(dgx_ref)

        @pl.when(i == 0)
        def _():
            dk_acc[...] = jnp.zeros_like(dk_acc)
            dv_acc[...] = jnp.zeros_like(dv_acc)

        dx2 = dx2_ref[...]
        dhq = jnp.zeros((tq, d), F32)
        for h in range(XATTN_HEADS):
            sl = slice(h * hd, (h + 1) * hd)
            q = cq_ref[:, sl]
            qn = _rms(q, gq_ref[...])
            kn = _rms(kv_ref[:, sl], gk_ref[...])
            v = kv_ref[:, XATTN_WIDTH + h * hd:XATTN_WIDTH + (h + 1) * hd]
            p = _xattn_probs(qn, kn)
            dco = _dot(dx2, wo_ref[sl, :], NT)
            dv_acc[:, sl] += _dot(p, dco, TN)
            dp = _dot(dco, v, NT)
            ds = p * (dp - jnp.sum(dp * p, axis=-1, keepdims=True))
            dqn = _dot(ds, kn) * scale
            dk_acc[:, sl] += _dot(ds, qn, TN) * scale
            dq, dgq = _rms_bwd(q, gq_ref[...], dqn)
            dgq_ref[...] += dgq
            dqb = dq.astype(BF16)
            dcq_ref[:, sl] = dqb
            dhq = dhq + _dot(dqb, wq_ref[:, sl], NT)
        dxn, dgx = _rms_bwd(x1_ref[...], gx_ref[...], dhq)
        dgx_ref[...] += dgx
        dx1_ref[...] = dx2 + dxn

        @pl.when(i == nq - 1)
        def _():
            for h in range(XATTN_HEADS):
                sl = slice(h * hd, (h + 1) * hd)
                dk, dgk = _rms_bwd(kv_ref[:, sl], gk_ref[...], dk_acc[:, sl])
                dgk_ref[...] += dgk
                dkv_ref[:, sl] = dk.astype(BF16)
                dkv_ref[:, XATTN_WIDTH + h * hd:XATTN_WIDTH + (h + 1) * hd] = dv_acc[:, sl].astype(BF16)

    row = lambda b, i: (b * nq + i, 0)
    fixed = lambda b, i: (0, 0)
    t_len = n_batch * s_len
    return pl.pallas_call(
        body, name="xattn_bwd", grid=(n_batch, nq),
        in_specs=[pl.BlockSpec((tq, d), row), pl.BlockSpec((tq, XATTN_WIDTH), row), pl.BlockSpec((m_len, 2 * XATTN_WIDTH), lambda b, i: (b, 0)),
                  pl.BlockSpec((tq, d), row), pl.BlockSpec((1, hd), fixed), pl.BlockSpec((1, hd), fixed),
                  pl.BlockSpec((XATTN_WIDTH, d), fixed), pl.BlockSpec((1, d), fixed), pl.BlockSpec((d, XATTN_WIDTH), fixed)],
        out_specs=[pl.BlockSpec((tq, d), row), pl.BlockSpec((tq, XATTN_WIDTH), row), pl.BlockSpec((m_len, 2 * XATTN_WIDTH), lambda b, i: (b, 0)),
                   pl.BlockSpec((1, hd), fixed), pl.BlockSpec((1, hd), fixed), pl.BlockSpec((1, d), fixed)],
        out_shape=[jax.ShapeDtypeStruct((t_len, d), F32), jax.ShapeDtypeStruct((t_len, XATTN_WIDTH), BF16),
                   jax.ShapeDtypeStruct((n_batch * m_len, 2 * XATTN_WIDTH), BF16),
                   jax.ShapeDtypeStruct((1, hd), F32), jax.ShapeDtypeStruct((1, hd), F32), jax.ShapeDtypeStruct((1, d), F32)],
        scratch_shapes=[pltpu.VMEM((m_len, XATTN_WIDTH), F32), pltpu.VMEM((m_len, XATTN_WIDTH), F32)],
        compiler_params=_cparams(("arbitrary", "arbitrary")),
    )(dx2, cq, ckv, x1, gq, gk, w_co, g_x, w_cq)


def _resident(shape):
    return pl.BlockSpec(shape, lambda *_: (0,) * len(shape), pipeline_mode=pl.Buffered(1))


def _mlp_fwd(hf, x2, target, w1, w2, tm=256, tf=1024):
    t_len, d = x2.shape
    f = w1.shape[1]
    tm, tf = min(tm, t_len), min(tf, f)

    def body(hf_ref, x2_ref, tg_ref, w1_ref, w2_ref, u_ref, a_ref, dy_ref, ls_ref):
        hf_t = hf_ref[...]
        y = x2_ref[...]
        for k in range(f // tf):
            cols = slice(k * tf, (k + 1) * tf)
            u = _dot(hf_t, w1_ref[:, cols])
            u_ref[:, cols] = u
            r = jnp.maximum(u, 0.0)
            a = (r * r).astype(BF16)
            a_ref[:, cols] = a
            y = y + _dot(a, w2_ref[cols, :])
        err = y - tg_ref[...]
        dy_ref[...] = err * (1.0 / d)
        ls_ref[...] = jnp.broadcast_to(jnp.sum(jnp.sum(err * err, axis=-1, keepdims=True) * (1.0 / d), axis=0, keepdims=True), ls_ref.shape)

    row = lambda i: (i, 0)
    return pl.pallas_call(
        body, name="mlp_fwd", grid=(t_len // tm,),
        in_specs=[pl.BlockSpec((tm, d), row), pl.BlockSpec((tm, d), row), pl.BlockSpec((tm, d), row), _resident((d, f)), _resident((f, d))],
        out_specs=[pl.BlockSpec((tm, f), row), pl.BlockSpec((tm, f), row), pl.BlockSpec((tm, d), row),
                   pl.BlockSpec((1, 8, LANES), lambda i: (i, 0, 0))],
        out_shape=[jax.ShapeDtypeStruct((t_len, f), F32), jax.ShapeDtypeStruct((t_len, f), BF16), jax.ShapeDtypeStruct((t_len, d), F32),
                   jax.ShapeDtypeStruct((t_len // tm, 8, LANES), F32)],
        compiler_params=_cparams(("parallel",)),
    )(hf, x2, target, w1, w2)


def _mlp_bwd(dy, u, x2, g, w1, w2, tm=256, tf=1024):
    t_len, d = x2.shape
    f = w1.shape[1]
    tm, tf = min(tm, t_len), min(tf, f)

    def body(dy_ref, u_ref, x2_ref, g_ref, w1_ref, w2_ref, du_ref, dx2_ref, dg_ref):
        @pl.when(pl.program_id(0) == 0)
        def _():
            dg_ref[...] = jnp.zeros_like(dg_ref)

        dy_t = dy_ref[...]
        dyb = dy_t.astype(BF16)
        dhf = jnp.zeros((tm, d), F32)
        for k in range(f // tf):
            cols = slice(k * tf, (k + 1) * tf)
            da = _dot(dyb, w2_ref[cols, :], NT)
            du = (da * (2.0 * jnp.maximum(u_ref[:, cols], 0.0))).astype(BF16)
            du_ref[:, cols] = du
            dhf = dhf + _dot(du, w1_ref[:, cols], NT)
        dxn, dg = _rms_bwd(x2_ref[...], g_ref[...], dhf)
        dx2_ref[...] = dy_t + dxn
        dg_ref[...] += dg

    row = lambda i: (i, 0)
    fixed = lambda i: (0, 0)
    return pl.pallas_call(
        body, name="mlp_bwd", grid=(t_len // tm,),
        in_specs=[pl.BlockSpec((tm, d), row), pl.BlockSpec((tm, f), row), pl.BlockSpec((tm, d), row), pl.BlockSpec((1, d), fixed),
                  _resident((d, f)), _resident((f, d))],
        out_specs=[pl.BlockSpec((tm, f), row), pl.BlockSpec((tm, d), row), pl.BlockSpec((1, d), fixed)],
        out_shape=[jax.ShapeDtypeStruct((t_len, f), BF16), jax.ShapeDtypeStruct((t_len, d), F32), jax.ShapeDtypeStruct((1, d), F32)],
        compiler_params=_cparams(("arbitrary",)),
    )(dy, u, x2, g, w1, w2)


def _pad_lanes(v, offset=0, width=LANES):
    return jnp.zeros((1, width), F32).at[:, offset:offset + v.shape[1]].set(v)


def _col(v, offset=0, rows=SM_ROWS):
    return jnp.zeros((rows, 1), F32).at[offset:offset + v.shape[1], 0].set(v[0])


def _pack_small(g_mix, dgq, dgk, dbias, dgo, dac, dar, ddc, ddr, g_gdn_o, g_nx, g_mem, g_xq, g_xk, g_mlp, loss_tiles):
    def body(mix_ref, q_ref, k_ref, b_ref, o_ref, ac_ref, ar_ref, dc_ref, dr_ref, go_ref, nx_ref, mem_ref, xq_ref, xk_ref,
             mlp_ref, lt_ref, out_ref):
        lane = lax.broadcasted_iota(jnp.int32, (1, LANES), 1)
        diag = lax.broadcasted_iota(jnp.int32, (SM_ROWS, LANES), 0) == lax.broadcasted_iota(jnp.int32, (SM_ROWS, LANES), 1)

        def rolled(v, shift):
            return pltpu.roll(jnp.broadcast_to(v, (8, LANES)), shift, 1)[0:1, :]

        def rows_to_lanes(col):
            return jnp.sum(jnp.where(diag, col, 0.0), axis=0, keepdims=True)

        def put(row, v, n):
            out_ref[row:row + 1, 0:LANES] = jnp.where(lane < n, v, 0.0)

        out_ref[...] = jnp.zeros_like(out_ref)
        out_ref[0:1, :] = mix_ref[...]
        for row, ref in ((1, q_ref), (2, k_ref), (4, o_ref)):
            put(row, ref[...] + rolled(ref[...], FOX_HEAD_DIM), FOX_HEAD_DIM)
        put(3, rows_to_lanes(b_ref[...]), FOX_HEADS)
        for row, lane_ref, row_ref in ((5, ac_ref, ar_ref), (6, dc_ref, dr_ref)):
            put(row, rolled(lane_ref[...] + rows_to_lanes(row_ref[...]), LANES - SM_A), GDN_HEADS)
        put(7, go_ref[...], LANES)
        out_ref[8:9, :] = nx_ref[...]
        out_ref[9:10, :] = mem_ref[...]
        put(10, xq_ref[...], LANES)
        put(11, xk_ref[...], LANES)
        out_ref[12:13, :] = mlp_ref[...]
        put(LOSS_ROW, 0.5 * jnp.sum(lt_ref[...], axis=0)[0:1, :], 1)

    args = (g_mix, dgq, dgk, dbias, dgo, dac, dar, ddc, ddr, g_gdn_o, g_nx, g_mem, g_xq, g_xk, g_mlp, loss_tiles)
    return pl.pallas_call(body, name="pack_small", out_shape=jax.ShapeDtypeStruct((PACK_ROWS, D_MODEL), F32))(*args)


LATE_WEIGHTS = (("w_out", "w_cq", "w_ckv", "w_co"), ("w_mlp1", "w_mlp2"))
GRAD_GROUPS = (("w_mlp2", "w_mlp1"), ("w_co", "w_cq", "w_ckv", "w_out"), ("w_in", "gdn_conv_w"))


def _local_step(x, mem, target, norm_mix_g, w_in, fox_qnorm_g, fox_knorm_g, fox_f_bias, fox_onorm_g, gdn_conv_w, gdn_A_log,
                gdn_dt_bias, gdn_onorm_g, norm_xattn_g, mem_norm_g, xattn_qnorm_g, xattn_knorm_g, norm_mlp_g,
                late_weights, grads_ready=None, first_token=0.0):
    if grads_ready is None:
        grads_ready = lambda group: 0.0
    n_batch, s_len, d = x.shape
    m_len = mem.shape[1]
    t_len = n_batch * s_len
    tq = min(FOX_BLOCK, s_len)
    nq = s_len // tq
    n_chunks = s_len // GDN_CHUNK
    x2d = x.reshape(t_len, d)

    wp = jnp.concatenate([w_in[0:1536], w_in[1544:3080], w_in[3088:3600], w_in[1536:1544], w_in[3080:3088],
                          jnp.zeros((P_DIM - 3600, d), BF16)], axis=0)
    wst = jnp.concatenate([w_in[1536:1544], w_in[3080:3088]], axis=0)
    conv_w = jnp.concatenate([gdn_conv_w, jnp.zeros((8 - CONV_WIDTH, gdn_conv_w.shape[1]), F32)], axis=0)
    bias_col = _col(fox_f_bias, SM_F)
    gq2, gk2, go2 = (jnp.tile(g, (1, 2)) for g in (fox_qnorm_g, fox_knorm_g, fox_onorm_g))
    a_c, dt_c = _pad_lanes(gdn_A_log, SM_A), _pad_lanes(gdn_dt_bias, SM_A)
    a_r, dt_r = _col(gdn_A_log, SM_A), _col(gdn_dt_bias, SM_A)

    h1, pfox, pgdn, pz, sm, smt = _in_proj(x2d, norm_mix_g + first_token, wp, wst)
    c_rows = _fox_cum(smt, bias_col, n_batch, s_len)
    cb = c_rows.reshape(SM_ROWS, n_batch, nq, tq).transpose(1, 2, 0, 3)
    pf3 = pfox.reshape(n_batch, s_len, 1536)
    o_fox, oa, lse = _fox_fwd(pf3, cb, gq2, gk2, go2, tq)
    pg3 = pgdn.reshape(n_batch, s_len, 1536)
    qkvn = _gdn_pre(pg3, conv_w)
    z3 = pz.reshape(n_batch, s_len, GDN_WIDTH)
    smc = sm.reshape(n_batch, s_len, LANES)
    smr = smt.reshape(SM_ROWS, n_batch * n_chunks, GDN_CHUNK).transpose(1, 0, 2)
    ob, states = _gdn_fwd(qkvn, z3, smc, smr, a_c, dt_c, a_r, dt_r, gdn_onorm_g)
    oa2, ob2 = oa.reshape(t_len, FOX_WIDTH), ob.reshape(t_len, GDN_WIDTH)
    w_out, w_cq, w_ckv, w_co = late_weights(LATE_WEIGHTS[0], ob2)
    x1, hq, cq = _out_proj(x2d, oa2, ob2, w_out, norm_xattn_g, w_cq)
    mem2d = mem.reshape(n_batch * m_len, d)
    hm, ckv = _mem_kv(mem2d, mem_norm_g, w_ckv)
    co, x2, hf = _xattn_fwd(cq, ckv, x1, xattn_qnorm_g, xattn_knorm_g, w_co, norm_mlp_g, n_batch, s_len, m_len)
    w_mlp1, w_mlp2 = late_weights(LATE_WEIGHTS[1], hf)
    u, a_act, dy, loss_tiles = _mlp_fwd(hf, x2, target.reshape(t_len, d), w_mlp1, w_mlp2)

    grads = {}
    du, dx2, grads["norm_mlp_g"] = _mlp_bwd(dy, u, x2, norm_mlp_g, w_mlp1, w_mlp2)
    grads["w_mlp2"] = _wgrad(a_act, dy, "wgrad_mlp2")
    grads["w_mlp1"] = _wgrad(hf, du, "wgrad_mlp1", column_blocks=D_FF // N_DEV)
    token = grads_ready({k: grads[k] for k in GRAD_GROUPS[0]})
    grads["w_co"] = _wgrad(co, dx2, "wgrad_co", column_blocks=D_MODEL // N_DEV)
    dx1, dcq, dckv, grads["xattn_qnorm_g"], grads["xattn_knorm_g"], grads["norm_xattn_g"] = _xattn_bwd(
        dx2, cq, ckv, x1, xattn_qnorm_g + token, xattn_knorm_g, w_co, norm_xattn_g, w_cq, n_batch, s_len, m_len)
    grads["w_cq"] = _wgrad(hq, dcq, "wgrad_cq")
    grads["w_ckv"] = _wgrad(hm, dckv, "wgrad_ckv")
    grads["mem_norm_g"] = _mem_kv_bwd(dckv, mem2d, mem_norm_g, w_ckv)
    grads["w_out"] = _wgrad_stacked([oa2, ob2], dx1, "wgrad_out", bn=1024)
    token = grads_ready({k: grads[k] for k in GRAD_GROUPS[1]})
    dcat = _out_proj_bwd(dx1, w_out)
    dcat3 = dcat.reshape(n_batch, s_len, d)

    dqkvn, dz, dsmc, dsmr, dac, ddc, dar, ddr, grads["gdn_onorm_g"] = _gdn_bwd(
        qkvn, z3, smc, smr, a_c, dt_c, a_r, dt_r, gdn_onorm_g + token, states, dcat3)
    dpg, dconv = _gdn_pre_bwd(pg3, conv_w, dqkvn)
    grads["gdn_conv_w"] = dconv[0:CONV_WIDTH]

    dq, dk, dv, dcb, dgq, dgk, dgo = _fox_bwd(pf3, cb, gq2, gk2, go2, o_fox, lse, dcat3, tq)
    dc8 = dcb[:, :, :, 0:2, :].transpose(1, 3, 0, 2, 4).reshape(FOX_HEADS, t_len)
    dc_rows = jnp.concatenate([dc8, jnp.zeros((SM_ROWS - FOX_HEADS, t_len), F32)], axis=0)
    dl_rows, dbias = _fox_cum_bwd(dc_rows, smt, bias_col, n_batch, s_len)
    dsm_rows = jnp.concatenate([dl_rows[0:SM_B], dsmr.transpose(1, 0, 2).reshape(SM_ROWS, t_len)[SM_B:SM_ROWS]], axis=0)

    dprojs = [dq.reshape(t_len, FOX_WIDTH), dk.reshape(t_len, FOX_WIDTH), dv.reshape(t_len, FOX_WIDTH),
              dpg.reshape(t_len, 1536), dz.reshape(t_len, GDN_WIDTH), dsmc.reshape(t_len, LANES)]
    dwp = _wgrad_stacked(dprojs, h1, "wgrad_in")
    dwst = _rows_matmul(dsm_rows, h1, "wgrad_in_rows")
    dw_small = dwp[P_SMALL:P_SMALL + SM_ROWS] + dwst
    grads["w_in"] = jnp.concatenate([dwp[0:1536], dw_small[0:8], dwp[1536:3072], dw_small[8:16], dwp[3072:3584]], axis=0)
    token = grads_ready({k: grads[k] for k in GRAD_GROUPS[2]})
    grad_x, grads["norm_mix_g"] = _in_proj_bwd(dprojs, dsm_rows, x2d, norm_mix_g + token, wp, wst, dx1)
    packed = _pack_small(grads["norm_mix_g"], dgq, dgk, dbias, dgo, dac, dar, ddc, ddr, grads["gdn_onorm_g"], grads["norm_xattn_g"],
                         grads["mem_norm_g"], grads["xattn_qnorm_g"], grads["xattn_knorm_g"], grads["norm_mlp_g"], loss_tiles)
    return packed, grad_x.reshape(n_batch, s_len, d), {k: grads[k] for k in SHARDED}


MESH_ID = pl.DeviceIdType.MESH
ANY_SPEC = pl.BlockSpec(memory_space=pl.ANY)


def _place():
    x, y, c = lax.axis_index("x"), lax.axis_index("y"), lax.axis_index("c")
    return x, y, c, [(1 - x, y), (x, 1 - y), (1 - x, 1 - y)]


def _place_own(src_ref, dst_ref):
    def staged(buf, sem):
        for a, b in ((src_ref, buf), (buf, dst_ref)):
            cp = pltpu.make_async_copy(a, b, sem)
            cp.start()
            cp.wait()

    pl.run_scoped(staged, pltpu.VMEM(src_ref.shape, src_ref.dtype), pltpu.SemaphoreType.DMA)


def _all_gather_body(n, ins, outs, send_sems, recv_sems):
    x, y, c, chips = _place()
    me, sibling = (x, y, c), (x, y, 1 - c)

    def copy(a, k, block, to, src=None):
        dst = outs[a].at[4 * block[0] + 2 * block[1] + block[2]]
        return pltpu.make_async_remote_copy(src_ref=dst if src is None else src, dst_ref=dst, send_sem=send_sems.at[a, k],
                                            recv_sem=recv_sems.at[a, k], device_id=to, device_id_type=MESH_ID)

    first = []
    for a in range(n):
        first.append(copy(a, 0, me, sibling, src=ins[a]))
        first += [copy(a, 1 + j, me, (*chip, c), src=ins[a]) for j, chip in enumerate(chips)]
    for cp in first:
        cp.start()
    for a in range(n):
        _place_own(ins[a], outs[a].at[4 * x + 2 * y + c])
    passed = []
    for j, chip in enumerate(chips):
        for a in range(n):
            copy(a, 1 + j, (*chip, c), me).wait_recv()
            fwd = copy(a, 4 + j, (*chip, c), sibling)
            fwd.start()
            passed.append(fwd)
    for a in range(n):
        copy(a, 0, sibling, me).wait_recv()
        for j, chip in enumerate(chips):
            copy(a, 4 + j, (*chip, 1 - c), me).wait_recv()
    for cp in first + passed:
        cp.wait_send()


def _all_gather_hbm(arrs, name):
    n = len(arrs)

    def body(*refs):
        _all_gather_body(n, refs[:n], refs[n:2 * n], refs[2 * n], refs[2 * n + 1])

    return pl.pallas_call(
        body, name=name, in_specs=[ANY_SPEC] * n, out_specs=[ANY_SPEC] * n,
        out_shape=[jax.ShapeDtypeStruct((N_DEV,) + a.shape, a.dtype) for a in arrs],
        scratch_shapes=[pltpu.SemaphoreType.DMA((n, 7)), pltpu.SemaphoreType.DMA((n, 7))],
        compiler_params=pltpu.CompilerParams(vmem_limit_bytes=VMEM_LIMIT),
    )(*arrs)


def _pair_exchange(arrs, name):
    n = len(arrs)

    def body(*refs):
        ins, outs = refs[:n], refs[n:2 * n]
        send_sems, recv_sems = refs[2 * n:]
        x, y, c, _ = _place()
        copies = []
        for a in range(n):
            for chip in range(4):
                copies.append(pltpu.make_async_remote_copy(
                    src_ref=ins[a].at[2 * chip + (1 - c)], dst_ref=outs[a].at[chip], send_sem=send_sems.at[a, chip],
                    recv_sem=recv_sems.at[a, chip], device_id=(x, y, 1 - c), device_id_type=MESH_ID))
        for cp in copies:
            cp.start()
        for cp in copies:
            cp.wait()

    return pl.pallas_call(
        body, name=name, in_specs=[ANY_SPEC] * n, out_specs=[ANY_SPEC] * n,
        out_shape=[jax.ShapeDtypeStruct((4,) + a.shape[1:], a.dtype) for a in arrs],
        scratch_shapes=[pltpu.SemaphoreType.DMA((n, 4)), pltpu.SemaphoreType.DMA((n, 4))],
    )(*arrs)


HBM_SPEC = pl.BlockSpec(memory_space=pltpu.HBM)
SEM_SPEC = pl.BlockSpec(memory_space=pltpu.SEMAPHORE)
DATAFLOW = pltpu.SideEffectType.DATAFLOW_SIDE_EFFECTING


def _in_hbm(arrs):
    return [pltpu.with_memory_space_constraint(a, pltpu.HBM) for a in arrs]


def _copies_start(name, srcs, lands, make_copies, after):
    n = len(srcs)
    n_copies = len(make_copies(srcs, lands, None, None)[0])

    def body(*refs):
        send_sems, recv_sems = refs[2 * n + 1], refs[2 * n + 2]
        for row in make_copies(refs[:n], refs[n:2 * n], send_sems, recv_sems):
            for cp in row:
                cp.start()
        refs[-1][...] = jnp.zeros_like(refs[-1])

    sems = pltpu.SemaphoreType.DMA((n * n_copies,))
    thru = [pltpu.HBM(a.shape, a.dtype) for a in list(srcs) + list(lands)]
    res = pl.pallas_call(
        body, name=name, in_specs=[HBM_SPEC] * (2 * n) + [ANY_SPEC],
        out_specs=(SEM_SPEC, SEM_SPEC, *[HBM_SPEC] * (2 * n), pl.BlockSpec(memory_space=pltpu.VMEM)),
        out_shape=(sems, sems, *thru, jax.ShapeDtypeStruct((8, LANES), F32)),
        input_output_aliases={i: 2 + i for i in range(2 * n)},
        compiler_params=pltpu.CompilerParams(has_side_effects=DATAFLOW),
    )(*_in_hbm(list(srcs) + list(lands)), after)
    return res[0], res[1], list(res[2:2 + n]), list(res[2 + n:2 + 2 * n]), res[-1]


def _copies_wait(name, send_sems, recv_sems, srcs, lands, after, make_copies, own_block=False):
    n = len(srcs)

    def body(*refs):
        if own_block:
            me = 4 * lax.axis_index("x") + 2 * lax.axis_index("y") + lax.axis_index("c")
            for a in range(n):
                _place_own(refs[a], refs[3 * n + 3 + a].at[me])
        for row in make_copies(refs[:n], refs[n:2 * n], refs[2 * n], refs[2 * n + 1]):
            for cp in row:
                cp.wait_send()
                cp.wait_recv()

    res = pl.pallas_call(
        body, name=name, in_specs=[HBM_SPEC] * (2 * n) + [SEM_SPEC, SEM_SPEC, ANY_SPEC],
        out_specs=tuple([HBM_SPEC] * (2 * n)),
        out_shape=tuple(pltpu.HBM(a.shape, a.dtype) for a in list(srcs) + list(lands)),
        input_output_aliases={i: i for i in range(2 * n)},
        compiler_params=pltpu.CompilerParams(has_side_effects=DATAFLOW, vmem_limit_bytes=VMEM_LIMIT),
    )(*srcs, *lands, send_sems, recv_sems, after)
    return list(res[:n]), list(res[n:])


def _gather_copies(srcs, lands, send_sems, recv_sems):
    if send_sems is None:
        return [[None] * 7]
    x, y, c, _ = _place()
    rows = []
    for a in range(len(srcs)):
        row = []
        for k in range(7):
            r = k + 1
            to = (1 - x if r & 4 else x, 1 - y if r & 2 else y, 1 - c if r & 1 else c)
            row.append(pltpu.make_async_remote_copy(
                src_ref=srcs[a], dst_ref=lands[a].at[4 * x + 2 * y + c], send_sem=send_sems.at[7 * a + k], recv_sem=recv_sems.at[7 * a + k],
                device_id=to, device_id_type=MESH_ID))
        rows.append(row)
    return rows


def _scatter_copies(srcs, lands, send_sems, recv_sems):
    if send_sems is None:
        return [[None] * 7]
    x, y, c, _ = _place()
    rows = []
    for a in range(len(srcs)):
        row = []
        for k in range(7):
            r = k + 1
            to = (1 - x if r & 4 else x, 1 - y if r & 2 else y, 1 - c if r & 1 else c)
            row.append(pltpu.make_async_remote_copy(
                src_ref=srcs[a].at[4 * to[0] + 2 * to[1] + to[2]], dst_ref=lands[a].at[k], send_sem=send_sems.at[7 * a + k],
                recv_sem=recv_sems.at[7 * a + k], device_id=to, device_id_type=MESH_ID))
        rows.append(row)
    return rows


def _chip_copies(srcs, lands, send_sems, recv_sems):
    if send_sems is None:
        return [[None] * 3]
    x, y, c, chips = _place()
    return [[pltpu.make_async_remote_copy(
        src_ref=srcs[a].at[2 * chip[0] + chip[1]], dst_ref=lands[a].at[j], send_sem=send_sems.at[3 * a + j], recv_sem=recv_sems.at[3 * a + j],
        device_id=(*chip, c), device_id_type=MESH_ID) for j, chip in enumerate(chips)] for a in range(len(srcs))]


def _tile(rows, cols):
    if rows <= 256:
        return rows, cols
    tr = 256 if cols <= 512 else 128
    if rows % tr == 0:
        return tr, cols
    return rows, 256


def _pair_sum(core, own, got, name):
    _, rows, cols = own.shape
    tr, tc = _tile(rows, cols)

    def body(c_ref, own_ref, got_ref, o_ref):
        o_ref[0] = own_ref[0] + got_ref[0]

    return pl.pallas_call(
        body, name=name,
        grid_spec=pltpu.PrefetchScalarGridSpec(
            num_scalar_prefetch=1, grid=(4, rows // tr, cols // tc),
            in_specs=[pl.BlockSpec((1, tr, tc), lambda k, i, j, c: (2 * k + c[0], i, j)),
                      pl.BlockSpec((1, tr, tc), lambda k, i, j, c: (k, i, j))],
            out_specs=pl.BlockSpec((1, tr, tc), lambda k, i, j, c: (k, i, j))),
        out_shape=jax.ShapeDtypeStruct((4, rows, cols), F32),
        compiler_params=_cparams(("parallel", "parallel", "parallel")),
    )(core, own, got)


def _adamw(w, g, m, v):
    m_new = ADAM_B1 * m + (1.0 - ADAM_B1) * g
    v_new = ADAM_B2 * v + (1.0 - ADAM_B2) * (g * g)
    m_hat = m_new / (1.0 - ADAM_B1 ** ADAM_STEP)
    v_hat = v_new / (1.0 - ADAM_B2 ** ADAM_STEP)
    delta = -ADAM_LR * (m_hat / (jnp.sqrt(v_hat) + ADAM_EPS) + ADAM_WD * w)
    return delta, m_new, v_new


def _sum_adam(chip, sums, parts, w, m, v, name):
    n_parts, rows, cols = parts.shape
    tr, tc = _tile(rows, cols)

    def body(chip_ref, own_ref, p_ref, w_ref, m_ref, v_ref, g_ref, d_ref, mo_ref, vo_ref):
        g = own_ref[0]
        for k in range(n_parts):
            g = g + p_ref[k]
        g_ref[...] = g
        d_ref[...], mo_ref[...], vo_ref[...] = _adamw(w_ref[...], g, m_ref[...], v_ref[...])

    tile = pl.BlockSpec((tr, tc), lambda i, j, ch: (i, j))
    out = jax.ShapeDtypeStruct((rows, cols), F32)
    return pl.pallas_call(
        body, name=name,
        grid_spec=pltpu.PrefetchScalarGridSpec(
            num_scalar_prefetch=1, grid=(rows // tr, cols // tc),
            in_specs=[pl.BlockSpec((1, tr, tc), lambda i, j, ch: (ch[0], i, j)),
                      pl.BlockSpec((n_parts, tr, tc), lambda i, j, ch: (0, i, j)), tile, tile, tile],
            out_specs=[tile, tile, tile, tile]),
        out_shape=[out, out, out, out],
        compiler_params=_cparams(("parallel", "parallel")),
    )(chip, sums, parts, w, m, v)


SHARDED = ("w_in", "gdn_conv_w", "w_out", "w_cq", "w_ckv", "w_co", "w_mlp1", "w_mlp2")
TRANSPOSED = ("w_in",)
COLUMN_SHARDED = ("gdn_conv_w", "w_co", "w_mlp1")
REPLICATED = ("norm_mix_g", "fox_qnorm_g", "fox_knorm_g", "fox_f_bias", "fox_onorm_g", "gdn_A_log", "gdn_dt_bias", "gdn_onorm_g",
              "norm_xattn_g", "mem_norm_g", "xattn_qnorm_g", "xattn_knorm_g", "norm_mlp_g")
WEIGHTS = ("norm_mix_g", "w_in", "fox_qnorm_g", "fox_knorm_g", "fox_f_bias", "fox_onorm_g", "gdn_conv_w", "gdn_A_log", "gdn_dt_bias",
           "gdn_onorm_g", "w_out", "norm_xattn_g", "mem_norm_g", "w_cq", "w_ckv", "xattn_qnorm_g", "xattn_knorm_g", "w_co",
           "norm_mlp_g", "w_mlp1", "w_mlp2")
PACK_ROWS = 16
LOSS_ROW = len(REPLICATED)


def _whole(name, gathered):
    if name in COLUMN_SHARDED:
        return gathered.transpose(1, 0, 2).reshape(gathered.shape[1], N_DEV * gathered.shape[2])
    return gathered.reshape(N_DEV * gathered.shape[1], gathered.shape[2])


def _blocks(name, whole):
    if whole.ndim == 3:
        return whole
    if name in COLUMN_SHARDED:
        rows, cols = whole.shape
        return whole.reshape(rows, N_DEV, cols // N_DEV).transpose(1, 0, 2)
    return whole.reshape(N_DEV, whole.shape[0] // N_DEV, whole.shape[1])


def _adam_small(everyone, ws, ms, vs):
    n_par = len(ws)

    def body(*refs):
        ev_ref = refs[0]
        w_refs, m_refs, v_refs = (refs[1 + j * n_par:1 + (j + 1) * n_par] for j in range(3))
        outs = refs[1 + 3 * n_par:-1]
        sum_ref = refs[-1]
        total = ev_ref[0]
        for dev in range(1, N_DEV):
            total = total + ev_ref[dev]
        sum_ref[...] = total
        for i in range(n_par):
            n = w_refs[i].shape[1]
            g = sum_ref[i:i + 1, 0:n]
            outs[4 * i][...] = g
            outs[4 * i + 1][...], outs[4 * i + 2][...], outs[4 * i + 3][...] = _adamw(w_refs[i][...], g, m_refs[i][...], v_refs[i][...])
        outs[4 * n_par][...] = sum_ref[LOSS_ROW:LOSS_ROW + 1, 0:1]

    shapes = [jax.ShapeDtypeStruct(a.shape, F32) for a in ws for _ in range(4)] + [jax.ShapeDtypeStruct((1, 1), F32)]
    return pl.pallas_call(body, name="adam_small", out_shape=shapes,
                          scratch_shapes=[pltpu.VMEM((PACK_ROWS, D_MODEL), F32)])(everyone, *ws, *ms, *vs)


def kernel(x, mem, norm_mix_g, w_in, fox_qnorm_g, fox_knorm_g, fox_f_bias, fox_onorm_g, gdn_conv_w, gdn_A_log, gdn_dt_bias, gdn_onorm_g, w_out, norm_xattn_g, mem_norm_g, w_cq, w_ckv, xattn_qnorm_g, xattn_knorm_g, w_co, norm_mlp_g, w_mlp1, w_mlp2, loss_target, m_norm_mix_g, m_w_in, m_fox_qnorm_g, m_fox_knorm_g, m_fox_f_bias, m_fox_onorm_g, m_gdn_conv_w, m_gdn_A_log, m_gdn_dt_bias, m_gdn_onorm_g, m_w_out, m_norm_xattn_g, m_mem_norm_g, m_w_cq, m_w_ckv, m_xattn_qnorm_g, m_xattn_knorm_g, m_w_co, m_norm_mlp_g, m_w_mlp1, m_w_mlp2, v_norm_mix_g, v_w_in, v_fox_qnorm_g, v_fox_knorm_g, v_fox_f_bias, v_fox_onorm_g, v_gdn_conv_w, v_gdn_A_log, v_gdn_dt_bias, v_gdn_onorm_g, v_w_out, v_norm_xattn_g, v_mem_norm_g, v_w_cq, v_w_ckv, v_xattn_qnorm_g, v_xattn_knorm_g, v_w_co, v_norm_mlp_g, v_w_mlp1, v_w_mlp2):
    given = dict(locals())
    w = {k: given[k] for k in WEIGHTS}
    m = {k: given["m_" + k] for k in WEIGHTS}
    v = {k: given["v_" + k] for k in WEIGHTS}

    core = lax.axis_index("c").astype(jnp.int32).reshape(1)
    chip = (2 * lax.axis_index("x") + lax.axis_index("y")).astype(jnp.int32).reshape(1)
    me = 4 * lax.axis_index("x") + 2 * lax.axis_index("y") + lax.axis_index("c")

    local = lambda d: {k: jnp.transpose(d[k][0]) if k in TRANSPOSED else d[k][0] for k in SHARDED}
    w2, m2, v2 = local(w), local(m), local(v)
    shards = {k: w2[k] if k == "gdn_conv_w" else w2[k].astype(BF16) for k in SHARDED}
    early = [k for k in SHARDED if not any(k in group for group in LATE_WEIGHTS)]
    gathered = _all_gather_hbm([shards[k] for k in early], "gather_early")
    whole = {k: _whole(k, g) for k, g in zip(early, gathered)}
    gathers, after = {}, gathered[0]
    for i, group in enumerate(LATE_WEIGHTS):
        lands = [lax.empty((N_DEV,) + shards[k].shape, BF16) for k in group]
        gathers[group] = _copies_start("gather_late_start_" + str(i), [shards[k] for k in group], lands, _gather_copies, after=after)
        after = gathers[group][4]
    first_token = after[0, 0]

    def late_weights(group, after):
        gather = gathers[group]
        _, lands = _copies_wait("gather_late_wait_" + str(LATE_WEIGHTS.index(group)), gather[0], gather[1], gather[2], gather[3],
                                after, _gather_copies, own_block=True)
        return [_whole(k, land) for k, land in zip(group, lands)]

    pending = []

    def grads_ready(group):
        names = list(group)
        tag = str(len(pending))
        own = [_blocks(k, group[k]) for k in names]
        if "w_in" in names:
            got = _pair_exchange(own, "grad_pair_exchange_" + tag)
            srcs = [_pair_sum(core, o, g, "grad_pair_sum_" + k) for k, o, g in zip(names, own, got)]
            copies, index, n_parts = _chip_copies, chip, 3
        else:
            srcs, copies, index, n_parts = own, _scatter_copies, me.astype(jnp.int32).reshape(1), 7
        lands = [lax.empty((n_parts,) + s.shape[1:], s.dtype) for s in srcs]
        started = _copies_start("grad_exchange_start_" + tag, srcs, lands, copies, after=srcs[0])
        pending.append((names, started, copies, index))
        return started[4][0, 0]

    small = {k: w[k] for k in REPLICATED}
    packed, grad_x, _ = _local_step(x, mem, loss_target, **small, **whole, late_weights=late_weights,
                                    grads_ready=grads_ready, first_token=first_token)

    small_lands = [lax.empty((N_DEV,) + packed.shape, F32)]
    small_gather = _copies_start("gather_small_start", [packed], small_lands, _gather_copies, after=grad_x)

    out_g, out_d, out_m, out_v = {}, {}, {}, {}
    after = small_gather[4]
    for tag, (names, started, copies, index) in enumerate(pending):
        srcs, parts = _copies_wait("grad_exchange_wait_" + str(tag), started[0], started[1], started[2], started[3], after, copies)
        for k, s, p in zip(names, srcs, parts):
            res = _sum_adam(index, s, p, w2[k], m2[k], v2[k], "adam_" + k)
            out_g[k], out_d[k], out_m[k], out_v[k] = ((jnp.transpose(r) if k in TRANSPOSED else r)[None] for r in res)
            after = res[0]

    _, (everyone,) = _copies_wait("gather_small_wait", small_gather[0], small_gather[1], small_gather[2], small_gather[3], after,
                                  _gather_copies, own_block=True)
    res = _adam_small(everyone, [w[k] for k in REPLICATED], [m[k] for k in REPLICATED], [v[k] for k in REPLICATED])
    for i, k in enumerate(REPLICATED):
        out_g[k], out_d[k], out_m[k], out_v[k] = res[4 * i:4 * i + 4]
    loss = res[-1].reshape(())

    return (loss, grad_x, *[out_g[k] for k in WEIGHTS], *[out_d[k] for k in WEIGHTS], *[out_m[k] for k in WEIGHTS],
            *[out_v[k] for k in WEIGHTS])
```

```python
import functools

import jax
import jax.numpy as jnp
import numpy as np
from jax import lax
from jax.experimental import pallas as pl
from jax.experimental.pallas import tpu as pltpu

F32 = jnp.float32
BF16 = jnp.bfloat16

D_MODEL = 1024
FOX_HEADS = 8
FOX_HEAD_DIM = 64
FOX_WIDTH = 512
GDN_HEADS = 4
GDN_HEAD_DIM = 128
GDN_WIDTH = 512
CONV_WIDTH = 4
GDN_CHUNK = 128
GDN_GROUP = 4
FOX_BLOCK = 512
XATTN_HEADS = 4
XATTN_HEAD_DIM = 128
XATTN_WIDTH = 512
D_FF = 4096
EPS = 1e-6
NEG_INF = -1e30
N_DEV = 8

ADAM_LR = 0.001
ADAM_B1 = 0.9
ADAM_B2 = 0.999
ADAM_EPS = 1e-08
ADAM_WD = 0.01
ADAM_STEP = 10

P_FOX = 0
P_GDN = 1536
P_Z = 3072
P_SMALL = 3584
P_DIM = 3712
SM_F = 0
SM_B = 8
SM_A = 12
SM_ROWS = 16

LANES = 128
VMEM_LIMIT = 56 * 1024 * 1024

NN = (((1,), (0,)), ((), ()))
NT = (((1,), (1,)), ((), ()))
TN = (((0,), (0,)), ((), ()))


def _dot(a, b, dims=NN):
    return lax.dot_general(a.astype(BF16), b.astype(BF16), dims, preferred_element_type=F32)


def _cparams(sem=None):
    kw = dict(vmem_limit_bytes=VMEM_LIMIT)
    if sem is not None:
        kw["dimension_semantics"] = sem
    return pltpu.CompilerParams(**kw)


def _sigmoid(x):
    return 0.5 * (jnp.tanh(0.5 * x) + 1.0)


def _softplus(x):
    return jnp.maximum(x, 0.0) + jnp.log1p(jnp.exp(-jnp.abs(x)))


def _log_sigmoid(x):
    return -_softplus(-x)


def _rms(x, g):
    r = lax.rsqrt(jnp.mean(x * x, axis=-1, keepdims=True) + EPS)
    return x * r * g


def _rms_bwd(x, g, dy):
    r = lax.rsqrt(jnp.mean(x * x, axis=-1, keepdims=True) + EPS)
    xh = x * r
    dg = jnp.sum(dy * xh, axis=0, keepdims=True)
    dyg = dy * g
    dx = r * (dyg - xh * jnp.mean(dyg * xh, axis=-1, keepdims=True))
    return dx, dg


def _pair_stat(t, m0):
    s0 = jnp.sum(jnp.where(m0, t, 0.0), axis=-1, keepdims=True)
    s1 = jnp.sum(jnp.where(m0, 0.0, t), axis=-1, keepdims=True)
    return jnp.where(m0, s0, s1)


def _rms_pair(x, g, m0):
    r = lax.rsqrt(_pair_stat(x * x, m0) * (1.0 / FOX_HEAD_DIM) + EPS)
    return x * r * g


def _rms_pair_bwd(x, g, dy, m0):
    r = lax.rsqrt(_pair_stat(x * x, m0) * (1.0 / FOX_HEAD_DIM) + EPS)
    xh = x * r
    dg = jnp.sum(dy * xh, axis=0, keepdims=True)
    dyg = dy * g
    dx = r * (dyg - xh * (_pair_stat(dyg * xh, m0) * (1.0 / FOX_HEAD_DIM)))
    return dx, dg


@jax.custom_vjp
def _mm_nn(a, b):
    return _dot(a, b, NN)


_mm_nn.defvjp(lambda a, b: (_dot(a, b, NN), (a, b)),
              lambda r, g: (_dot(g, r[1], NT), _dot(r[0], g, TN)))


@jax.custom_vjp
def _mm_nt(a, b):
    return _dot(a, b, NT)


_mm_nt.defvjp(lambda a, b: (_dot(a, b, NT), (a, b)),
              lambda r, g: (_dot(g, r[1], NN), _dot(g, r[0], TN)))


@jax.custom_vjp
def _mm_tn(a, b):
    return _dot(a, b, TN)


_mm_tn.defvjp(lambda a, b: (_dot(a, b, TN), (a, b)),
              lambda r, g: (_dot(r[1], g, NT), _dot(r[0], g, NN)))


def _dot3(a, b, dims):
    ah = a.astype(BF16)
    al = (a - ah.astype(F32)).astype(BF16)
    bh = b.astype(BF16)
    bl = (b - bh.astype(F32)).astype(BF16)
    d = functools.partial(lax.dot_general, dimension_numbers=dims, preferred_element_type=F32)
    return d(ah, bh) + d(ah, bl) + d(al, bh)


def _neumann_inverses(mats):
    c = mats[0].shape[0]
    eye = (lax.broadcasted_iota(jnp.int32, (c, c), 0) == lax.broadcasted_iota(jnp.int32, (c, c), 1)).astype(F32)
    xs = [eye - a for a in mats]
    ps = list(mats)
    k = 2
    while k < c + 1:
        ps = [_dot3(p, p, NN) for p in ps]
        xs = [x + _dot3(x, p, NN) for x, p in zip(xs, ps)]
        k *= 2
    return xs


@jax.custom_vjp
def _unit_lower_inverses(mats):
    return _neumann_inverses(mats)


def _unit_lower_inverses_fwd(mats):
    ts = _neumann_inverses(mats)
    return ts, ts


def _unit_lower_inverses_bwd(ts, gs):
    left = [_dot3(t, g, TN) for t, g in zip(ts, gs)]
    return ([-_dot3(m, t, NT) for m, t in zip(left, ts)],)


_unit_lower_inverses.defvjp(_unit_lower_inverses_fwd, _unit_lower_inverses_bwd)


def _wgrad(a, b, name, bk=1024, bn=1024, bt=512, column_blocks=None):
    t_len, k_len = a.shape
    n_len = b.shape[1]
    bk, bn, bt = min(bk, k_len), min(bn, n_len), min(bt, t_len)
    nt = t_len // bt

    def body(a_ref, b_ref, o_ref, acc_ref):
        t = pl.program_id(2)

        @pl.when(t == 0)
        def _():
            acc_ref[...] = jnp.zeros_like(acc_ref)

        acc_ref[...] += _dot(a_ref[...], b_ref[...], TN)

        @pl.when(t == nt - 1)
        def _():
            if column_blocks:
                for jj in range(bn // column_blocks):
                    o_ref[jj] = acc_ref[:, jj * column_blocks:(jj + 1) * column_blocks]
            else:
                o_ref[...] = acc_ref[...]

    if column_blocks:
        out_spec = pl.BlockSpec((bn // column_blocks, bk, column_blocks), lambda i, j, t: (j, i, 0))
        out_shape = jax.ShapeDtypeStruct((n_len // column_blocks, k_len, column_blocks), F32)
    else:
        out_spec = pl.BlockSpec((bk, bn), lambda i, j, t: (i, j))
        out_shape = jax.ShapeDtypeStruct((k_len, n_len), F32)
    return pl.pallas_call(
        body, name=name, grid=(k_len // bk, n_len // bn, nt),
        in_specs=[pl.BlockSpec((bt, bk), lambda i, j, t: (t, i)), pl.BlockSpec((bt, bn), lambda i, j, t: (t, j))],
        out_specs=out_spec, out_shape=out_shape,
        scratch_shapes=[pltpu.VMEM((bk, bn), F32)],
        compiler_params=_cparams(("parallel", "parallel", "arbitrary")),
    )(a, b)


def _wgrad_stacked(pieces, b, name, bn=512, bt=512):
    t_len, n_len = b.shape
    n_p = len(pieces)
    starts = [int(s) for s in np.cumsum([0] + [p.shape[1] for p in pieces])]
    bn, bt = min(bn, n_len), min(bt, t_len)
    nt = t_len // bt

    def body(*refs):
        b_ref, o_ref, acc_ref = refs[n_p:]
        t = pl.program_id(1)

        @pl.when(t == 0)
        def _():
            acc_ref[...] = jnp.zeros_like(acc_ref)

        for k in range(n_p):
            acc_ref[starts[k]:starts[k + 1], :] += _dot(refs[k][...], b_ref[...], TN)

        @pl.when(t == nt - 1)
        def _():
            o_ref[...] = acc_ref[...]

    return pl.pallas_call(
        body, name=name, grid=(n_len // bn, nt),
        in_specs=[pl.BlockSpec((bt, p.shape[1]), lambda j, t: (t, 0)) for p in pieces] + [pl.BlockSpec((bt, bn), lambda j, t: (t, j))],
        out_specs=pl.BlockSpec((starts[-1], bn), lambda j, t: (0, j)),
        out_shape=jax.ShapeDtypeStruct((starts[-1], n_len), F32),
        scratch_shapes=[pltpu.VMEM((starts[-1], bn), F32)],
        compiler_params=_cparams(("parallel", "arbitrary")),
    )(*pieces, b)


def _rows_matmul(a, b, name, bt=512):
    r_len, t_len = a.shape
    n_len = b.shape[1]
    bt = min(bt, t_len)
    nt = t_len // bt

    def body(a_ref, b_ref, o_ref):
        t = pl.program_id(0)

        @pl.when(t == 0)
        def _():
            o_ref[...] = jnp.zeros_like(o_ref)

        o_ref[...] += _dot(a_ref[...], b_ref[...], NN)

    return pl.pallas_call(
        body, name=name, grid=(nt,),
        in_specs=[pl.BlockSpec((r_len, bt), lambda t: (0, t)), pl.BlockSpec((bt, n_len), lambda t: (t, 0))],
        out_specs=pl.BlockSpec((r_len, n_len), lambda t: (0, 0)),
        out_shape=jax.ShapeDtypeStruct((r_len, n_len), F32),
        compiler_params=_cparams(("arbitrary",)),
    )(a, b)


def _in_proj(x, g, wp, wst, tm=256):
    t_len, d = x.shape
    tm = min(tm, t_len)

    def body(x_ref, g_ref, wp_ref, wst_ref, h_ref, fox_ref, gdn_ref, z_ref, sm_ref, smt_ref):
        h = _rms(x_ref[...], g_ref[...]).astype(BF16)
        h_ref[...] = h
        p = _dot(h, wp_ref[...], NT)
        fox_ref[...] = p[:, P_FOX:P_GDN]
        gdn_ref[...] = p[:, P_GDN:P_Z]
        z_ref[...] = p[:, P_Z:P_SMALL]
        sm_ref[...] = p[:, P_SMALL:P_DIM]
        smt_ref[...] = _dot(wst_ref[...], h, NT)

    row = lambda i: (i, 0)
    fixed = lambda i: (0, 0)
    return pl.pallas_call(
        body, name="in_proj", grid=(t_len // tm,),
        in_specs=[pl.BlockSpec((tm, d), row), pl.BlockSpec((1, d), fixed), pl.BlockSpec((P_DIM, d), fixed),
                  pl.BlockSpec((SM_ROWS, d), fixed)],
        out_specs=[pl.BlockSpec((tm, d), row), pl.BlockSpec((tm, 1536), row), pl.BlockSpec((tm, 1536), row),
                   pl.BlockSpec((tm, 512), row), pl.BlockSpec((tm, LANES), row), pl.BlockSpec((SM_ROWS, tm), lambda i: (0, i))],
        out_shape=[jax.ShapeDtypeStruct((t_len, d), BF16), jax.ShapeDtypeStruct((t_len, 1536), F32),
                   jax.ShapeDtypeStruct((t_len, 1536), F32), jax.ShapeDtypeStruct((t_len, 512), F32),
                   jax.ShapeDtypeStruct((t_len, LANES), F32), jax.ShapeDtypeStruct((SM_ROWS, t_len), F32)],
        compiler_params=_cparams(("parallel",)),
    )(x, g, wp, wst)


def _in_proj_bwd(dprojs, dsmt, x, g, wp, wst, dx1, tm=256):
    t_len, d = x.shape
    tm = min(tm, t_len)
    n_p = len(dprojs)
    starts = np.cumsum([0] + [p.shape[1] for p in dprojs])

    def body(*refs):
        dp_refs = refs[:n_p]
        dst_ref, x_ref, g_ref, wp_ref, wst_ref, dx1_ref, dx_ref, dg_ref = refs[n_p:]
        i = pl.program_id(0)
        dh = _dot(dst_ref[...], wst_ref[...], TN)
        for k in range(n_p):
            dh = dh + _dot(dp_refs[k][...], wp_ref[int(starts[k]):int(starts[k + 1]), :], NN)
        dxn, dg = _rms_bwd(x_ref[...], g_ref[...], dh)
        dx_ref[...] = dx1_ref[...] + dxn

        @pl.when(i == 0)
        def _():
            dg_ref[...] = jnp.zeros_like(dg_ref)

        dg_ref[...] += dg

    row = lambda i: (i, 0)
    fixed = lambda i: (0, 0)
    return pl.pallas_call(
        body, name="in_proj_bwd", grid=(t_len // tm,),
        in_specs=[pl.BlockSpec((tm, p.shape[1]), row) for p in dprojs] + [
            pl.BlockSpec((SM_ROWS, tm), lambda i: (0, i)), pl.BlockSpec((tm, d), row),
            pl.BlockSpec((1, d), fixed), pl.BlockSpec((P_DIM, d), fixed), pl.BlockSpec((SM_ROWS, d), fixed),
            pl.BlockSpec((tm, d), row)],
        out_specs=[pl.BlockSpec((tm, d), row), pl.BlockSpec((1, d), fixed)],
        out_shape=[jax.ShapeDtypeStruct((t_len, d), F32), jax.ShapeDtypeStruct((1, d), F32)],
        compiler_params=_cparams(("arbitrary",)),
    )(*dprojs, dsmt, x, g, wp, wst, dx1)


def _fox_cum(smt, bias_col, n_batch, s_len, ck=256):
    ck = min(ck, s_len)

    def body(s_ref, b_ref, c_ref):
        tri = (lax.broadcasted_iota(jnp.int32, (ck, ck), 0) <= lax.broadcasted_iota(jnp.int32, (ck, ck), 1)).astype(F32)
        carry = jnp.zeros((SM_ROWS, 1), F32)
        for r in range(s_len // ck):
            ls = _log_sigmoid(s_ref[:, r * ck:(r + 1) * ck] + b_ref[...])
            c = jnp.dot(ls, tri, precision=lax.Precision.HIGHEST, preferred_element_type=F32) + carry
            c_ref[:, r * ck:(r + 1) * ck] = c
            carry = c[:, ck - 1:ck]

    return pl.pallas_call(
        body, name="fox_cum", grid=(n_batch,),
        in_specs=[pl.BlockSpec((SM_ROWS, s_len), lambda b: (0, b)), pl.BlockSpec((SM_ROWS, 1), lambda b: (0, 0))],
        out_specs=pl.BlockSpec((SM_ROWS, s_len), lambda b: (0, b)),
        out_shape=jax.ShapeDtypeStruct(smt.shape, F32),
        compiler_params=_cparams(("parallel",)),
    )(smt, bias_col)


def _fox_cum_bwd(dc, smt, bias_col, n_batch, s_len, ck=256):
    ck = min(ck, s_len)
    nr = s_len // ck

    def body(dc_ref, s_ref, b_ref, dl_ref, db_ref):
        b = pl.program_id(0)
        tri = (lax.broadcasted_iota(jnp.int32, (ck, ck), 0) >= lax.broadcasted_iota(jnp.int32, (ck, ck), 1)).astype(F32)
        carry = jnp.zeros((SM_ROWS, 1), F32)
        tot = jnp.zeros((SM_ROWS, 1), F32)
        for r in reversed(range(nr)):
            sl = slice(r * ck, (r + 1) * ck)
            dls = jnp.dot(dc_ref[:, sl], tri, precision=lax.Precision.HIGHEST, preferred_element_type=F32) + carry
            carry = dls[:, 0:1]
            dl = dls * (1.0 - _sigmoid(s_ref[:, sl] + b_ref[...]))
            dl_ref[:, sl] = dl
            tot = tot + jnp.sum(dl, axis=1, keepdims=True)

        @pl.when(b == 0)
        def _():
            db_ref[...] = jnp.zeros_like(db_ref)

        db_ref[...] += jnp.broadcast_to(tot, db_ref.shape)

    return pl.pallas_call(
        body, name="fox_cum_bwd", grid=(n_batch,),
        in_specs=[pl.BlockSpec((SM_ROWS, s_len), lambda b: (0, b)), pl.BlockSpec((SM_ROWS, s_len), lambda b: (0, b)),
                  pl.BlockSpec((SM_ROWS, 1), lambda b: (0, 0))],
        out_specs=[pl.BlockSpec((SM_ROWS, s_len), lambda b: (0, b)), pl.BlockSpec((SM_ROWS, LANES), lambda b: (0, 0))],
        out_shape=[jax.ShapeDtypeStruct(smt.shape, F32), jax.ShapeDtypeStruct((SM_ROWS, LANES), F32)],
        compiler_params=_cparams(("arbitrary",)),
    )(dc, smt, bias_col)


def _fox_diagonal_mask(tq):
    return lax.broadcasted_iota(jnp.int32, (tq, tq), 1) <= lax.broadcasted_iota(jnp.int32, (tq, tq), 0)


def _fox_fwd(pf, cb, gq2, gk2, go2, tq=256):
    n_batch, s_len, _ = pf.shape
    tq = min(tq, s_len)
    nq = s_len // tq
    scale = FOX_HEAD_DIM ** -0.5

    def body(q_ref, k_ref, v_ref, c_ref, gq_ref, gk_ref, go_ref, o_ref, on_ref, lse_ref, kh_ref, vh_ref):
        j = pl.program_id(1)
        i = pl.program_id(2)
        m0 = lax.broadcasted_iota(jnp.int32, (1, LANES), 1) < FOX_HEAD_DIM

        @pl.when(i == 0)
        def _():
            kn = _rms_pair(k_ref[0], gk_ref[...], m0)
            kh_ref[0] = jnp.where(m0, kn, 0.0).astype(BF16)
            kh_ref[1] = jnp.where(m0, 0.0, kn).astype(BF16)
            v = v_ref[0]
            vh_ref[0] = jnp.where(m0, v, 0.0).astype(BF16)
            vh_ref[1] = jnp.where(m0, 0.0, v).astype(BF16)

        qb = (_rms_pair(q_ref[0], gq_ref[...], m0) * scale).astype(BF16)

        def step(kb, carry, diagonal=False):
            ms, ls, acc = carry
            off = pl.multiple_of(kb * tq, tq)
            new_m, new_l, alphas, pv = [], [], [], []
            for hh in range(2):
                s = _dot(qb, kh_ref[hh, pl.ds(off, tq), :], NT)
                s = s - c_ref[0, kb, pl.ds(2 * j + hh, 1), :]
                if diagonal:
                    s = jnp.where(_fox_diagonal_mask(tq), s, NEG_INF)
                m_new = jnp.maximum(ms[hh], jnp.max(s, axis=-1, keepdims=True))
                alpha = jnp.exp(ms[hh] - m_new)
                p = jnp.exp(s - m_new)
                new_l.append(alpha * ls[hh] + jnp.sum(p, axis=-1, keepdims=True))
                new_m.append(m_new)
                alphas.append(alpha)
                pv.append(_dot(p, vh_ref[hh, pl.ds(off, tq), :], NN))
            acc = jnp.where(m0, alphas[0], alphas[1]) * acc + pv[0] + pv[1]
            return tuple(new_m), tuple(new_l), acc

        init_m = (jnp.full((tq, 1), NEG_INF, F32),) * 2
        init_l = (jnp.zeros((tq, 1), F32),) * 2
        carry = lax.fori_loop(0, i, step, (init_m, init_l, jnp.zeros((tq, LANES), F32)))
        ms, ls, acc = step(i, carry, diagonal=True)
        o = acc / jnp.where(m0, ls[0], ls[1])
        o_ref[0] = o
        on_ref[0] = _rms_pair(o, go_ref[...], m0).astype(BF16)
        lse_ref[0] = jnp.where(m0, ms[0] + jnp.log(ls[0]), ms[1] + jnp.log(ls[1]))

    fixed = lambda b, j, i: (0, 0)
    tile = lambda b, j, i: (b, i, j)
    return pl.pallas_call(
        body, name="fox_fwd", grid=(n_batch, 4, nq),
        in_specs=[pl.BlockSpec((1, tq, LANES), tile), pl.BlockSpec((1, s_len, LANES), lambda b, j, i: (b, 0, 4 + j)),
                  pl.BlockSpec((1, s_len, LANES), lambda b, j, i: (b, 0, 8 + j)),
                  pl.BlockSpec((1, nq, SM_ROWS, tq), lambda b, j, i: (b, 0, 0, 0)),
                  pl.BlockSpec((1, LANES), fixed), pl.BlockSpec((1, LANES), fixed), pl.BlockSpec((1, LANES), fixed)],
        out_specs=[pl.BlockSpec((1, tq, LANES), tile), pl.BlockSpec((1, tq, LANES), tile), pl.BlockSpec((1, tq, LANES), tile)],
        out_shape=[jax.ShapeDtypeStruct((n_batch, s_len, FOX_WIDTH), F32), jax.ShapeDtypeStruct((n_batch, s_len, FOX_WIDTH), BF16),
                   jax.ShapeDtypeStruct((n_batch, s_len, FOX_WIDTH), F32)],
        scratch_shapes=[pltpu.VMEM((2, s_len, LANES), BF16), pltpu.VMEM((2, s_len, LANES), BF16)],
        compiler_params=_cparams(("parallel", "parallel", "arbitrary")),
    )(pf, pf, pf, cb, gq2, gk2, go2)


def _fox_bwd(pf, cb, gq2, gk2, go2, o, lse, don, tq=256):
    n_batch, s_len, _ = pf.shape
    tq = min(tq, s_len)
    nq = s_len // tq
    scale = FOX_HEAD_DIM ** -0.5

    def body(q_ref, k_ref, v_ref, c_ref, gq_ref, gk_ref, go_ref, o_ref, lse_ref, don_ref,
             dq_ref, dk_ref, dv_ref, dc_ref, dgq_ref, dgk_ref, dgo_ref, kh_ref, vh_ref, dka_ref, dva_ref, dca_ref):
        b = pl.program_id(0)
        j = pl.program_id(1)
        i = pl.program_id(2)
        m0 = lax.broadcasted_iota(jnp.int32, (1, LANES), 1) < FOX_HEAD_DIM

        @pl.when((b == 0) & (j == 0) & (i == 0))
        def _():
            dgq_ref[...] = jnp.zeros_like(dgq_ref)
            dgk_ref[...] = jnp.zeros_like(dgk_ref)
            dgo_ref[...] = jnp.zeros_like(dgo_ref)

        @pl.when(i == 0)
        def _():
            kn = _rms_pair(k_ref[0], gk_ref[...], m0)
            kh_ref[0] = jnp.where(m0, kn, 0.0).astype(BF16)
            kh_ref[1] = jnp.where(m0, 0.0, kn).astype(BF16)
            v = v_ref[0]
            vh_ref[0] = jnp.where(m0, v, 0.0).astype(BF16)
            vh_ref[1] = jnp.where(m0, 0.0, v).astype(BF16)
            dka_ref[...] = jnp.zeros_like(dka_ref)
            dva_ref[...] = jnp.zeros_like(dva_ref)
            dca_ref[...] = jnp.zeros_like(dca_ref)

        q = q_ref[0]
        qn = _rms_pair(q, gq_ref[...], m0)
        qs = qn * scale
        qb = qs.astype(BF16)
        qh = (jnp.where(m0, qs, 0.0).astype(BF16), jnp.where(m0, 0.0, qs).astype(BF16))
        ot = o_ref[0]
        do, dgo = _rms_pair_bwd(ot, go_ref[...], don_ref[0], m0)
        dgo_ref[...] += dgo
        dd = do * ot
        delta = (jnp.sum(jnp.where(m0, dd, 0.0), axis=-1, keepdims=True), jnp.sum(jnp.where(m0, 0.0, dd), axis=-1, keepdims=True))
        doh = (jnp.where(m0, do, 0.0).astype(BF16), jnp.where(m0, 0.0, do).astype(BF16))
        lse_t = lse_ref[0]
        lse_h = (lse_t[:, 0:1], lse_t[:, FOX_HEAD_DIM:FOX_HEAD_DIM + 1])

        def step(kb, carry, diagonal=False):
            dqn, rs = carry
            rs = list(rs)
            off = pl.multiple_of(kb * tq, tq)
            for hh in range(2):
                kblk = kh_ref[hh, pl.ds(off, tq), :]
                vblk = vh_ref[hh, pl.ds(off, tq), :]
                s = _dot(qb, kblk, NT)
                s = s - c_ref[0, kb, pl.ds(2 * j + hh, 1), :]
                if diagonal:
                    s = jnp.where(_fox_diagonal_mask(tq), s, NEG_INF)
                p = jnp.exp(s - lse_h[hh])
                dp = _dot(doh[hh], vblk, NT)
                ds = p * (dp - delta[hh])
                dva_ref[pl.ds(off, tq), :] += _dot(p, doh[hh], TN)
                dka_ref[pl.ds(off, tq), :] += _dot(ds, qh[hh], TN)
                dca_ref[kb, hh:hh + 1, :] += -jnp.sum(ds, axis=0, keepdims=True)
                rs[hh] = rs[hh] + jnp.sum(ds, axis=-1, keepdims=True)
                dqn = dqn + _dot(ds, kblk, NN)
            return dqn, tuple(rs)

        carry = lax.fori_loop(0, i, step, (jnp.zeros((tq, LANES), F32), (jnp.zeros((tq, 1), F32),) * 2))
        dqn, rs = step(i, carry, diagonal=True)
        dqn = dqn * scale
        rs_rows = jnp.where(m0, rs[0], rs[1]).T
        dca_ref[i, 0:1, :] += rs_rows[0:1, :]
        dca_ref[i, 1:2, :] += rs_rows[FOX_HEAD_DIM:FOX_HEAD_DIM + 1, :]
        dq, dgq = _rms_pair_bwd(q, gq_ref[...], dqn, m0)
        dq_ref[0] = dq.astype(BF16)
        dgq_ref[...] += dgq

        @pl.when(i == nq - 1)
        def _():
            dk, dgk = _rms_pair_bwd(k_ref[0], gk_ref[...], dka_ref[...], m0)
            dk_ref[0] = dk.astype(BF16)
            dgk_ref[...] += dgk
            dv_ref[0] = dva_ref[...].astype(BF16)
            dc_ref[0, 0] = dca_ref[...]

    fixed = lambda b, j, i: (0, 0)
    tile = lambda b, j, i: (b, i, j)
    full = lambda b, j, i: (b, 0, j)
    wide = jax.ShapeDtypeStruct((n_batch, s_len, FOX_WIDTH), BF16)
    gain = jax.ShapeDtypeStruct((1, LANES), F32)
    return pl.pallas_call(
        body, name="fox_bwd", grid=(n_batch, 4, nq),
        in_specs=[pl.BlockSpec((1, tq, LANES), tile), pl.BlockSpec((1, s_len, LANES), lambda b, j, i: (b, 0, 4 + j)),
                  pl.BlockSpec((1, s_len, LANES), lambda b, j, i: (b, 0, 8 + j)),
                  pl.BlockSpec((1, nq, SM_ROWS, tq), lambda b, j, i: (b, 0, 0, 0)),
                  pl.BlockSpec((1, LANES), fixed), pl.BlockSpec((1, LANES), fixed), pl.BlockSpec((1, LANES), fixed),
                  pl.BlockSpec((1, tq, LANES), tile), pl.BlockSpec((1, tq, LANES), tile), pl.BlockSpec((1, tq, LANES), tile)],
        out_specs=[pl.BlockSpec((1, tq, LANES), tile), pl.BlockSpec((1, s_len, LANES), full), pl.BlockSpec((1, s_len, LANES), full),
                   pl.BlockSpec((1, 1, nq, 8, tq), lambda b, j, i: (b, j, 0, 0, 0)),
                   pl.BlockSpec((1, LANES), fixed), pl.BlockSpec((1, LANES), fixed), pl.BlockSpec((1, LANES), fixed)],
        out_shape=[wide, wide, wide, jax.ShapeDtypeStruct((n_batch, 4, nq, 8, tq), F32), gain, gain, gain],
        scratch_shapes=[pltpu.VMEM((2, s_len, LANES), BF16), pltpu.VMEM((2, s_len, LANES), BF16),
                        pltpu.VMEM((s_len, LANES), F32), pltpu.VMEM((s_len, LANES), F32), pltpu.VMEM((nq, 8, tq), F32)],
        compiler_params=_cparams(("arbitrary", "arbitrary", "arbitrary")),
    )(pf, pf, pf, cb, gq2, gk2, go2, o, lse, don)


def _shift_down(x, k):
    row = lax.broadcasted_iota(jnp.int32, x.shape, 0)
    return jnp.where(row >= k, pltpu.roll(x, k, 0), 0.0)


def _shift_up(x, k):
    n = x.shape[0]
    row = lax.broadcasted_iota(jnp.int32, x.shape, 0)
    return jnp.where(row < n - k, pltpu.roll(x, n - k, 0), 0.0)


def _conv_silu(x, w):
    y = w[3:4] * x + w[2:3] * _shift_down(x, 1) + w[1:2] * _shift_down(x, 2) + w[0:1] * _shift_down(x, 3)
    return y, y * _sigmoid(y)


def _gdn_pre(pg, conv_w):
    n_batch, s_len, width = pg.shape
    ncb = width // LANES

    def body(x_ref, w_ref, o_ref):
        cb = pl.program_id(1)
        _, s = _conv_silu(x_ref[0], w_ref[...])
        sn = s * lax.rsqrt(jnp.sum(s * s, axis=-1, keepdims=True) + EPS)
        o_ref[0] = jnp.where(cb < 2 * GDN_HEADS, sn, s)

    return pl.pallas_call(
        body, name="gdn_pre", grid=(n_batch, ncb),
        in_specs=[pl.BlockSpec((1, s_len, LANES), lambda b, c: (b, 0, c)), pl.BlockSpec((8, LANES), lambda b, c: (0, c))],
        out_specs=pl.BlockSpec((1, s_len, LANES), lambda b, c: (b, 0, c)),
        out_shape=jax.ShapeDtypeStruct(pg.shape, F32),
        compiler_params=_cparams(("parallel", "parallel")),
    )(pg, conv_w)


def _gdn_pre_bwd(pg, conv_w, dout):
    n_batch, s_len, width = pg.shape
    ncb = width // LANES

    def body(x_ref, w_ref, d_ref, dx_ref, dw_ref):
        cb = pl.program_id(0)
        b = pl.program_id(1)
        x = x_ref[0]
        w = w_ref[...]
        d = d_ref[0]
        y, s = _conv_silu(x, w)
        rr = lax.rsqrt(jnp.sum(s * s, axis=-1, keepdims=True) + EPS)
        sn = s * rr
        ds_n = rr * (d - sn * jnp.sum(d * sn, axis=-1, keepdims=True))
        ds = jnp.where(cb < 2 * GDN_HEADS, ds_n, d)
        sig = _sigmoid(y)
        dy = ds * (sig * (1.0 + y * (1.0 - sig)))
        dx = w[3:4] * dy + w[2:3] * _shift_up(dy, 1) + w[1:2] * _shift_up(dy, 2) + w[0:1] * _shift_up(dy, 3)
        dx_ref[0] = dx.astype(BF16)
        dw = [jnp.sum(dy * _shift_down(x, 3 - jj), axis=0, keepdims=True) if jj < 3 else jnp.sum(dy * x, axis=0, keepdims=True)
              for jj in range(CONV_WIDTH)]
        rows = lax.broadcasted_iota(jnp.int32, (8, LANES), 0)
        dwb = jnp.zeros((8, LANES), F32)
        for jj in range(CONV_WIDTH):
            dwb = dwb + jnp.where(rows == jj, dw[jj], 0.0)

        @pl.when(b == 0)
        def _():
            dw_ref[...] = jnp.zeros_like(dw_ref)

        dw_ref[...] += dwb

    blk = lambda c, b: (b, 0, c)
    return pl.pallas_call(
        body, name="gdn_pre_bwd", grid=(ncb, n_batch),
        in_specs=[pl.BlockSpec((1, s_len, LANES), blk), pl.BlockSpec((8, LANES), lambda c, b: (0, c)), pl.BlockSpec((1, s_len, LANES), blk)],
        out_specs=[pl.BlockSpec((1, s_len, LANES), blk), pl.BlockSpec((8, LANES), lambda c, b: (0, c))],
        out_shape=[jax.ShapeDtypeStruct(pg.shape, BF16), jax.ShapeDtypeStruct((8, width), F32)],
        compiler_params=_cparams(("parallel", "arbitrary")),
    )(pg, conv_w, dout)


def _gdn_gates(smc, smr, a_c, dt_c, a_r, dt_r, h):
    lane = lax.broadcasted_iota(jnp.int32, (1, LANES), 1)
    sub = lax.broadcasted_iota(jnp.int32, (SM_ROWS, 1), 0)
    beta_c = jnp.sum(jnp.where(lane == SM_B + h, _sigmoid(smc), 0.0), axis=1, keepdims=True)
    g_all_c = -jnp.exp(a_c) * _softplus(smc + dt_c)
    g_c = jnp.sum(jnp.where(lane == SM_A + h, g_all_c, 0.0), axis=1, keepdims=True)
    g_all_r = -jnp.exp(a_r) * _softplus(smr + dt_r)
    g_r = jnp.sum(jnp.where(sub == SM_A + h, g_all_r, 0.0), axis=0, keepdims=True)
    return beta_c, g_c, g_r


def _gdn_group(qkv, z, smc, smr, a_c, dt_c, a_r, dt_r, go, states):
    n_grp = len(qkv)
    c = qkv[0].shape[0]
    hd = GDN_HEAD_DIM
    pairs = [(g, h) for g in range(n_grp) for h in range(GDN_HEADS)]
    ii = lax.broadcasted_iota(jnp.int32, (c, c), 0)
    jj = lax.broadcasted_iota(jnp.int32, (c, c), 1)
    incl = ii >= jj
    col = lambda arr, base, h: arr[:, base + h * hd:base + (h + 1) * hd]

    qs, ks, kbs, vbs, decays, gcs, g_lasts, amats = [], [], [], [], [], [], [], []
    for g, h in pairs:
        beta_c, g_c, g_r = _gdn_gates(smc[g], smr[g], a_c, dt_c, a_r, dt_r, h)
        gc_c = jnp.sum(jnp.where(incl, g_r, 0.0), axis=1, keepdims=True)
        gc_r = jnp.sum(jnp.where(ii <= jj, g_c, 0.0), axis=0, keepdims=True)
        decay = jnp.where(incl, jnp.exp(jnp.where(incl, gc_c - gc_r, 0.0)), 0.0)
        k = col(qkv[g], GDN_WIDTH, h)
        kb = k * beta_c
        qs.append(col(qkv[g], 0, h) * (hd ** -0.5))
        ks.append(k)
        kbs.append(kb)
        vbs.append(col(qkv[g], 2 * GDN_WIDTH, h) * beta_c)
        decays.append(decay)
        gcs.append(gc_c)
        g_lasts.append(jnp.sum(g_c, axis=0, keepdims=True))
        amats.append(jnp.where(ii > jj, _mm_nt(kb, k) * decay, 0.0))
    ts = _unit_lower_inverses(amats)
    egcs = [jnp.exp(gc) for gc in gcs]
    us = [_mm_nn(t, vb) for t, vb in zip(ts, vbs)]
    ws = [_mm_nn(t, kb * e) for t, kb, e in zip(ts, kbs, egcs)]
    intras = [_mm_nt(q, k) * d for q, k, d in zip(qs, ks, decays)]
    qes = [q * e for q, e in zip(qs, egcs)]
    kds = [k * jnp.exp(gl - gc) for k, gl, gc in zip(ks, g_lasts, gcs)]
    sdecs = [jnp.exp(gl) for gl in g_lasts]

    outs = []
    for g in range(n_grp):
        idx = [g * GDN_HEADS + h for h in range(GDN_HEADS)]
        v_new = [us[i] - _mm_nn(ws[i], states[h]) for h, i in enumerate(idx)]
        o_state = [_mm_nn(qes[i], states[h]) for h, i in enumerate(idx)]
        o_intra = [_mm_nn(intras[i], v_new[h]) for h, i in enumerate(idx)]
        states = [states[h] * sdecs[i] + _mm_tn(kds[i], v_new[h]) for h, i in enumerate(idx)]
        outs.append([_rms(o_state[h] + o_intra[h], go) * (col(z[g], 0, h) * _sigmoid(col(z[g], 0, h))) for h in range(GDN_HEADS)])
    return outs, states


def _gdn_group_size(n_chunks):
    return GDN_GROUP if n_chunks % GDN_GROUP == 0 else 1


def _gdn_fwd(qkvn, z, smc, smr, a_c, dt_c, a_r, dt_r, go):
    n_batch, s_len, _ = qkvn.shape
    c = GDN_CHUNK
    n = s_len // c
    grp = _gdn_group_size(n)
    ng = n // grp
    gc = grp * c
    hd = GDN_HEAD_DIM

    def body(qkv_ref, z_ref, smc_ref, smr_ref, ac_ref, dc_ref, ar_ref, dr_ref, go_ref, og_ref, st_ref, s_ref):
        @pl.when(pl.program_id(1) == 0)
        def _():
            s_ref[...] = jnp.zeros_like(s_ref)

        states = [s_ref[h] for h in range(GDN_HEADS)]
        for h in range(GDN_HEADS):
            st_ref[0, 0, h] = states[h]
        rows = lambda k: slice(k * c, (k + 1) * c)
        outs, nxt = _gdn_group([qkv_ref[0, rows(k), :] for k in range(grp)], [z_ref[0, rows(k), :] for k in range(grp)],
                               [smc_ref[0, rows(k), :] for k in range(grp)], [smr_ref[k] for k in range(grp)],
                               ac_ref[...], dc_ref[...], ar_ref[...], dr_ref[...], go_ref[...], states)
        for k in range(grp):
            for h in range(GDN_HEADS):
                og_ref[0, rows(k), h * hd:(h + 1) * hd] = outs[k][h].astype(BF16)
        for h in range(GDN_HEADS):
            s_ref[h] = nxt[h]

    tok = lambda b, i: (b, i, 0)
    fixed = lambda b, i: (0, 0)
    return pl.pallas_call(
        body, name="gdn_fwd", grid=(n_batch, ng),
        in_specs=[pl.BlockSpec((1, gc, 3 * GDN_WIDTH), tok), pl.BlockSpec((1, gc, GDN_WIDTH), tok), pl.BlockSpec((1, gc, LANES), tok),
                  pl.BlockSpec((grp, SM_ROWS, c), lambda b, i: (b * ng + i, 0, 0)),
                  pl.BlockSpec((1, LANES), fixed), pl.BlockSpec((1, LANES), fixed), pl.BlockSpec((SM_ROWS, 1), fixed),
                  pl.BlockSpec((SM_ROWS, 1), fixed), pl.BlockSpec((1, LANES), fixed)],
        out_specs=[pl.BlockSpec((1, gc, GDN_WIDTH), tok), pl.BlockSpec((1, 1, GDN_HEADS, hd, hd), lambda b, i: (b, i, 0, 0, 0))],
        out_shape=[jax.ShapeDtypeStruct((n_batch, s_len, GDN_WIDTH), BF16), jax.ShapeDtypeStruct((n_batch, ng, GDN_HEADS, hd, hd), F32)],
        scratch_shapes=[pltpu.VMEM((GDN_HEADS, hd, hd), F32)],
        compiler_params=_cparams(("parallel", "arbitrary")),
    )(qkvn, z, smc, smr, a_c, dt_c, a_r, dt_r, go)


def _gdn_bwd(qkvn, z, smc, smr, a_c, dt_c, a_r, dt_r, go, states, dog):
    n_batch, s_len, _ = qkvn.shape
    c = GDN_CHUNK
    n = s_len // c
    grp = _gdn_group_size(n)
    ng = n // grp
    gc = grp * c
    hd = GDN_HEAD_DIM

    def body(qkv_ref, z_ref, smc_ref, smr_ref, ac_ref, dc_ref, ar_ref, dr_ref, go_ref, st_ref, dog_ref,
             dqkv_ref, dz_ref, dsmc_ref, dsmr_ref, dac_ref, ddc_ref, dar_ref, ddr_ref, dgo_ref, ds_ref):
        first = (pl.program_id(0) == 0) & (pl.program_id(1) == 0)

        @pl.when(pl.program_id(1) == 0)
        def _():
            ds_ref[...] = jnp.zeros_like(ds_ref)

        @pl.when(first)
        def _():
            for r in (dac_ref, ddc_ref, dar_ref, ddr_ref, dgo_ref):
                r[...] = jnp.zeros_like(r)

        rows = lambda k: slice(k * c, (k + 1) * c)
        states = [st_ref[0, 0, h] for h in range(GDN_HEADS)]
        prim = ([qkv_ref[0, rows(k), :] for k in range(grp)], [z_ref[0, rows(k), :] for k in range(grp)],
                [smc_ref[0, rows(k), :] for k in range(grp)], [smr_ref[k] for k in range(grp)],
                ac_ref[...], dc_ref[...], ar_ref[...], dr_ref[...], go_ref[...], states)
        _, vjp = jax.vjp(_gdn_group, *prim)
        cot = ([[dog_ref[0, rows(k), h * hd:(h + 1) * hd] for h in range(GDN_HEADS)] for k in range(grp)],
               [ds_ref[h] for h in range(GDN_HEADS)])
        dqkv, dz, dsmc, dsmr, dac, ddc, dar, ddr, dgo, dstates = vjp(cot)
        for k in range(grp):
            dqkv_ref[0, rows(k), :] = dqkv[k]
            dz_ref[0, rows(k), :] = dz[k].astype(BF16)
            dsmc_ref[0, rows(k), :] = dsmc[k]
            dsmr_ref[k] = dsmr[k]
        dac_ref[...] += dac
        ddc_ref[...] += ddc
        dar_ref[...] += dar
        ddr_ref[...] += ddr
        dgo_ref[...] += dgo
        for h in range(GDN_HEADS):
            ds_ref[h] = dstates[h]

    tok = lambda b, i: (b, ng - 1 - i, 0)
    fixed = lambda b, i: (0, 0)
    lane_vec = jax.ShapeDtypeStruct((1, LANES), F32)
    row_vec = jax.ShapeDtypeStruct((SM_ROWS, 1), F32)
    return pl.pallas_call(
        body, name="gdn_bwd", grid=(n_batch, ng),
        in_specs=[pl.BlockSpec((1, gc, 3 * GDN_WIDTH), tok), pl.BlockSpec((1, gc, GDN_WIDTH), tok), pl.BlockSpec((1, gc, LANES), tok),
                  pl.BlockSpec((grp, SM_ROWS, c), lambda b, i: (b * ng + ng - 1 - i, 0, 0)),
                  pl.BlockSpec((1, LANES), fixed), pl.BlockSpec((1, LANES), fixed), pl.BlockSpec((SM_ROWS, 1), fixed),
                  pl.BlockSpec((SM_ROWS, 1), fixed), pl.BlockSpec((1, LANES), fixed),
                  pl.BlockSpec((1, 1, GDN_HEADS, hd, hd), lambda b, i: (b, ng - 1 - i, 0, 0, 0)),
                  pl.BlockSpec((1, gc, GDN_WIDTH), lambda b, i: (b, ng - 1 - i, 1))],
        out_specs=[pl.BlockSpec((1, gc, 3 * GDN_WIDTH), tok), pl.BlockSpec((1, gc, GDN_WIDTH), tok), pl.BlockSpec((1, gc, LANES), tok),
                   pl.BlockSpec((grp, SM_ROWS, c), lambda b, i: (b * ng + ng - 1 - i, 0, 0)),
                   pl.BlockSpec((1, LANES), fixed), pl.BlockSpec((1, LANES), fixed), pl.BlockSpec((SM_ROWS, 1), fixed),
                   pl.BlockSpec((SM_ROWS, 1), fixed), pl.BlockSpec((1, LANES), fixed)],
        out_shape=[jax.ShapeDtypeStruct((n_batch, s_len, 3 * GDN_WIDTH), F32), jax.ShapeDtypeStruct((n_batch, s_len, GDN_WIDTH), BF16),
                   jax.ShapeDtypeStruct((n_batch, s_len, LANES), F32), jax.ShapeDtypeStruct((n_batch * n, SM_ROWS, c), F32),
                   lane_vec, lane_vec, row_vec, row_vec, lane_vec],
        scratch_shapes=[pltpu.VMEM((GDN_HEADS, hd, hd), F32)],
        compiler_params=_cparams(("arbitrary", "arbitrary")),
    )(qkvn, z, smc, smr, a_c, dt_c, a_r, dt_r, go, states, dog)


def _out_proj(x, oa, ob, w_out, g_x, w_cq, tm=256):
    t_len, d = x.shape
    tm = min(tm, t_len)

    def body(x_ref, oa_ref, ob_ref, wo_ref, g_ref, wq_ref, x1_ref, hq_ref, cq_ref):
        x1 = x_ref[...] + _dot(oa_ref[...], wo_ref[0:FOX_WIDTH, :]) + _dot(ob_ref[...], wo_ref[FOX_WIDTH:2 * FOX_WIDTH, :])
        x1_ref[...] = x1
        hq = _rms(x1, g_ref[...]).astype(BF16)
        hq_ref[...] = hq
        cq_ref[...] = _dot(hq, wq_ref[...])

    row = lambda i: (i, 0)
    fixed = lambda i: (0, 0)
    return pl.pallas_call(
        body, name="out_proj", grid=(t_len // tm,),
        in_specs=[pl.BlockSpec((tm, d), row), pl.BlockSpec((tm, FOX_WIDTH), row), pl.BlockSpec((tm, GDN_WIDTH), row),
                  pl.BlockSpec((d, d), fixed), pl.BlockSpec((1, d), fixed), pl.BlockSpec((d, XATTN_WIDTH), fixed)],
        out_specs=[pl.BlockSpec((tm, d), row), pl.BlockSpec((tm, d), row), pl.BlockSpec((tm, XATTN_WIDTH), row)],
        out_shape=[jax.ShapeDtypeStruct((t_len, d), F32), jax.ShapeDtypeStruct((t_len, d), BF16), jax.ShapeDtypeStruct((t_len, XATTN_WIDTH), F32)],
        compiler_params=_cparams(("parallel",)),
    )(x, oa, ob, w_out, g_x, w_cq)


def _out_proj_bwd(dx1, w_out, tm=512):
    t_len, d = dx1.shape
    tm = min(tm, t_len)

    def body(dx_ref, w_ref, o_ref):
        o_ref[...] = _dot(dx_ref[...], w_ref[...], NT)

    return pl.pallas_call(
        body, name="out_proj_bwd", grid=(t_len // tm,),
        in_specs=[pl.BlockSpec((tm, d), lambda i: (i, 0)), pl.BlockSpec((d, d), lambda i: (0, 0))],
        out_specs=pl.BlockSpec((tm, d), lambda i: (i, 0)),
        out_shape=jax.ShapeDtypeStruct((t_len, d), F32),
        compiler_params=_cparams(("parallel",)),
    )(dx1, w_out)


def _mem_kv(mem, g, w_ckv, tm=256):
    t_len, d = mem.shape
    tm = min(tm, t_len)

    def body(x_ref, g_ref, w_ref, h_ref, o_ref):
        h = _rms(x_ref[...], g_ref[...]).astype(BF16)
        h_ref[...] = h
        o_ref[...] = _dot(h, w_ref[...])

    row = lambda i: (i, 0)
    fixed = lambda i: (0, 0)
    return pl.pallas_call(
        body, name="mem_kv", grid=(t_len // tm,),
        in_specs=[pl.BlockSpec((tm, d), row), pl.BlockSpec((1, d), fixed), pl.BlockSpec((d, 2 * XATTN_WIDTH), fixed)],
        out_specs=[pl.BlockSpec((tm, d), row), pl.BlockSpec((tm, 2 * XATTN_WIDTH), row)],
        out_shape=[jax.ShapeDtypeStruct((t_len, d), BF16), jax.ShapeDtypeStruct((t_len, 2 * XATTN_WIDTH), F32)],
        compiler_params=_cparams(("parallel",)),
    )(mem, g, w_ckv)


def _mem_kv_bwd(dckv, mem, g, w_ckv, tm=256):
    t_len, d = mem.shape
    tm = min(tm, t_len)

    def body(d_ref, x_ref, g_ref, w_ref, dg_ref):
        @pl.when(pl.program_id(0) == 0)
        def _():
            dg_ref[...] = jnp.zeros_like(dg_ref)

        dh = _dot(d_ref[...], w_ref[...], NT)
        _, dg = _rms_bwd(x_ref[...], g_ref[...], dh)
        dg_ref[...] += dg

    row = lambda i: (i, 0)
    fixed = lambda i: (0, 0)
    return pl.pallas_call(
        body, name="mem_kv_bwd", grid=(t_len // tm,),
        in_specs=[pl.BlockSpec((tm, 2 * XATTN_WIDTH), row), pl.BlockSpec((tm, d), row), pl.BlockSpec((1, d), fixed),
                  pl.BlockSpec((d, 2 * XATTN_WIDTH), fixed)],
        out_specs=pl.BlockSpec((1, d), fixed),
        out_shape=jax.ShapeDtypeStruct((1, d), F32),
        compiler_params=_cparams(("arbitrary",)),
    )(dckv, mem, g, w_ckv)


def _xattn_probs(qn, kn):
    s = _dot(qn, kn, NT) * (XATTN_HEAD_DIM ** -0.5)
    p = jnp.exp(s - jnp.max(s, axis=-1, keepdims=True))
    return p / jnp.sum(p, axis=-1, keepdims=True)


def _xattn_fwd(cq, ckv, x1, gq, gk, w_co, g_mlp, n_batch, s_len, m_len, tq=512):
    d = x1.shape[1]
    tq = min(tq, s_len)
    nq = s_len // tq
    hd = XATTN_HEAD_DIM

    def body(cq_ref, kv_ref, x1_ref, gq_ref, gk_ref, wo_ref, gm_ref, co_ref, x2_ref, hf_ref):
        outs = []
        for h in range(XATTN_HEADS):
            qn = _rms(cq_ref[:, h * hd:(h + 1) * hd], gq_ref[...])
            kn = _rms(kv_ref[:, h * hd:(h + 1) * hd], gk_ref[...])
            p = _xattn_probs(qn, kn)
            outs.append(_dot(p, kv_ref[:, XATTN_WIDTH + h * hd:XATTN_WIDTH + (h + 1) * hd]).astype(BF16))
        x2 = x1_ref[...]
        for h in range(XATTN_HEADS):
            co_ref[:, h * hd:(h + 1) * hd] = outs[h]
            x2 = x2 + _dot(outs[h], wo_ref[h * hd:(h + 1) * hd, :])
        x2_ref[...] = x2
        hf_ref[...] = _rms(x2, gm_ref[...]).astype(BF16)

    row = lambda b, i: (b * nq + i, 0)
    fixed = lambda b, i: (0, 0)
    t_len = n_batch * s_len
    return pl.pallas_call(
        body, name="xattn_fwd", grid=(n_batch, nq),
        in_specs=[pl.BlockSpec((tq, XATTN_WIDTH), row), pl.BlockSpec((m_len, 2 * XATTN_WIDTH), lambda b, i: (b, 0)),
                  pl.BlockSpec((tq, d), row), pl.BlockSpec((1, hd), fixed), pl.BlockSpec((1, hd), fixed),
                  pl.BlockSpec((XATTN_WIDTH, d), fixed), pl.BlockSpec((1, d), fixed)],
        out_specs=[pl.BlockSpec((tq, XATTN_WIDTH), row), pl.BlockSpec((tq, d), row), pl.BlockSpec((tq, d), row)],
        out_shape=[jax.ShapeDtypeStruct((t_len, XATTN_WIDTH), BF16), jax.ShapeDtypeStruct((t_len, d), F32),
                   jax.ShapeDtypeStruct((t_len, d), BF16)],
        compiler_params=_cparams(("parallel", "parallel")),
    )(cq, ckv, x1, gq, gk, w_co, g_mlp)


def _xattn_bwd(dx2, cq, ckv, x1, gq, gk, w_co, g_x, w_cq, n_batch, s_len, m_len, tq=512):
    d = x1.shape[1]
    tq = min(tq, s_len)
    nq = s_len // tq
    hd = XATTN_HEAD_DIM
    scale = XATTN_HEAD_DIM ** -0.5

    def body(dx2_ref, cq_ref, kv_ref, x1_ref, gq_ref, gk_ref, wo_ref, gx_ref, wq_ref,
             dx1_ref, dcq_ref, dkv_ref, dgq_ref, dgk_ref, dgx_ref, dk_acc, dv_acc):
        b = pl.program_id(0)
        i = pl.program_id(1)

        @pl.when((b == 0) & (i == 0))
        def _():
            dgq_ref[...] = jnp.zeros_like(dgq_ref)
            dgk_ref[...] = jnp.zeros_like(dgk_ref)
            dgx_ref[...] = jnp.zeros_like(dgx_ref)

        @pl.when(i == 0)
        def _():
            dk_acc[...] = jnp.zeros_like(dk_acc)
            dv_acc[...] = jnp.zeros_like(dv_acc)

        dx2 = dx2_ref[...]
        dhq = jnp.zeros((tq, d), F32)
        for h in range(XATTN_HEADS):
            sl = slice(h * hd, (h + 1) * hd)
            q = cq_ref[:, sl]
            qn = _rms(q, gq_ref[...])
            kn = _rms(kv_ref[:, sl], gk_ref[...])
            v = kv_ref[:, XATTN_WIDTH + h * hd:XATTN_WIDTH + (h + 1) * hd]
            p = _xattn_probs(qn, kn)
            dco = _dot(dx2, wo_ref[sl, :], NT)
            dv_acc[:, sl] += _dot(p, dco, TN)
            dp = _dot(dco, v, NT)
            ds = p * (dp - jnp.sum(dp * p, axis=-1, keepdims=True))
            dqn = _dot(ds, kn) * scale
            dk_acc[:, sl] += _dot(ds, qn, TN) * scale
            dq, dgq = _rms_bwd(q, gq_ref[...], dqn)
            dgq_ref[...] += dgq
            dqb = dq.astype(BF16)
            dcq_ref[:, sl] = dqb
            dhq = dhq + _dot(dqb, wq_ref[:, sl], NT)
        dxn, dgx = _rms_bwd(x1_ref[...], gx_ref[...], dhq)
        dgx_ref[...] += dgx
        dx1_ref[...] = dx2 + dxn

        @pl.when(i == nq - 1)
        def _():
            for h in range(XATTN_HEADS):
                sl = slice(h * hd, (h + 1) * hd)
                dk, dgk = _rms_bwd(kv_ref[:, sl], gk_ref[...], dk_acc[:, sl])
                dgk_ref[...] += dgk
                dkv_ref[:, sl] = dk.astype(BF16)
                dkv_ref[:, XATTN_WIDTH + h * hd:XATTN_WIDTH + (h + 1) * hd] = dv_acc[:, sl].astype(BF16)

    row = lambda b, i: (b * nq + i, 0)
    fixed = lambda b, i: (0, 0)
    t_len = n_batch * s_len
    return pl.pallas_call(
        body, name="xattn_bwd", grid=(n_batch, nq),
        in_specs=[pl.BlockSpec((tq, d), row), pl.BlockSpec((tq, XATTN_WIDTH), row), pl.BlockSpec((m_len, 2 * XATTN_WIDTH), lambda b, i: (b, 0)),
                  pl.BlockSpec((tq, d), row), pl.BlockSpec((1, hd), fixed), pl.BlockSpec((1, hd), fixed),
                  pl.BlockSpec((XATTN_WIDTH, d), fixed), pl.BlockSpec((1, d), fixed), pl.BlockSpec((d, XATTN_WIDTH), fixed)],
        out_specs=[pl.BlockSpec((tq, d), row), pl.BlockSpec((tq, XATTN_WIDTH), row), pl.BlockSpec((m_len, 2 * XATTN_WIDTH), lambda b, i: (b, 0)),
                   pl.BlockSpec((1, hd), fixed), pl.BlockSpec((1, hd), fixed), pl.BlockSpec((1, d), fixed)],
        out_shape=[jax.ShapeDtypeStruct((t_len, d), F32), jax.ShapeDtypeStruct((t_len, XATTN_WIDTH), BF16),
                   jax.ShapeDtypeStruct((n_batch * m_len, 2 * XATTN_WIDTH), BF16),
                   jax.ShapeDtypeStruct((1, hd), F32), jax.ShapeDtypeStruct((1, hd), F32), jax.ShapeDtypeStruct((1, d), F32)],
        scratch_shapes=[pltpu.VMEM((m_len, XATTN_WIDTH), F32), pltpu.VMEM((m_len, XATTN_WIDTH), F32)],
        compiler_params=_cparams(("arbitrary", "arbitrary")),
    )(dx2, cq, ckv, x1, gq, gk, w_co, g_x, w_cq)


def _resident(shape):
    return pl.BlockSpec(shape, lambda *_: (0,) * len(shape), pipeline_mode=pl.Buffered(1))


def _mlp_fwd(hf, x2, target, w1, w2, tm=256, tf=1024):
    t_len, d = x2.shape
    f = w1.shape[1]
    tm, tf = min(tm, t_len), min(tf, f)

    def body(hf_ref, x2_ref, tg_ref, w1_ref, w2_ref, u_ref, a_ref, dy_ref, ls_ref):
        hf_t = hf_ref[...]
        y = x2_ref[...]
        for k in range(f // tf):
            cols = slice(k * tf, (k + 1) * tf)
            u = _dot(hf_t, w1_ref[:, cols])
            u_ref[:, cols] = u
            r = jnp.maximum(u, 0.0)
            a = (r * r).astype(BF16)
            a_ref[:, cols] = a
            y = y + _dot(a, w2_ref[cols, :])
        err = y - tg_ref[...]
        dy_ref[...] = err * (1.0 / d)
        ls_ref[...] = jnp.broadcast_to(jnp.sum(jnp.sum(err * err, axis=-1, keepdims=True) * (1.0 / d), axis=0, keepdims=True), ls_ref.shape)

    row = lambda i: (i, 0)
    return pl.pallas_call(
        body, name="mlp_fwd", grid=(t_len // tm,),
        in_specs=[pl.BlockSpec((tm, d), row), pl.BlockSpec((tm, d), row), pl.BlockSpec((tm, d), row), _resident((d, f)), _resident((f, d))],
        out_specs=[pl.BlockSpec((tm, f), row), pl.BlockSpec((tm, f), row), pl.BlockSpec((tm, d), row),
                   pl.BlockSpec((1, 8, LANES), lambda i: (i, 0, 0))],
        out_shape=[jax.ShapeDtypeStruct((t_len, f), F32), jax.ShapeDtypeStruct((t_len, f), BF16), jax.ShapeDtypeStruct((t_len, d), F32),
                   jax.ShapeDtypeStruct((t_len // tm, 8, LANES), F32)],
        compiler_params=_cparams(("parallel",)),
    )(hf, x2, target, w1, w2)


def _mlp_bwd(dy, u, x2, g, w1, w2, tm=256, tf=1024):
    t_len, d = x2.shape
    f = w1.shape[1]
    tm, tf = min(tm, t_len), min(tf, f)

    def body(dy_ref, u_ref, x2_ref, g_ref, w1_ref, w2_ref, du_ref, dx2_ref, dg_ref):
        @pl.when(pl.program_id(0) == 0)
        def _():
            dg_ref[...] = jnp.zeros_like(dg_ref)

        dy_t = dy_ref[...]
        dyb = dy_t.astype(BF16)
        dhf = jnp.zeros((tm, d), F32)
        for k in range(f // tf):
            cols = slice(k * tf, (k + 1) * tf)
            da = _dot(dyb, w2_ref[cols, :], NT)
            du = (da * (2.0 * jnp.maximum(u_ref[:, cols], 0.0))).astype(BF16)
            du_ref[:, cols] = du
            dhf = dhf + _dot(du, w1_ref[:, cols], NT)
        dxn, dg = _rms_bwd(x2_ref[...], g_ref[...], dhf)
        dx2_ref[...] = dy_t + dxn
        dg_ref[...] += dg

    row = lambda i: (i, 0)
    fixed = lambda i: (0, 0)
    return pl.pallas_call(
        body, name="mlp_bwd", grid=(t_len // tm,),
        in_specs=[pl.BlockSpec((tm, d), row), pl.BlockSpec((tm, f), row), pl.BlockSpec((tm, d), row), pl.BlockSpec((1, d), fixed),
                  _resident((d, f)), _resident((f, d))],
        out_specs=[pl.BlockSpec((tm, f), row), pl.BlockSpec((tm, d), row), pl.BlockSpec((1, d), fixed)],
        out_shape=[jax.ShapeDtypeStruct((t_len, f), BF16), jax.ShapeDtypeStruct((t_len, d), F32), jax.ShapeDtypeStruct((1, d), F32)],
        compiler_params=_cparams(("arbitrary",)),
    )(dy, u, x2, g, w1, w2)


def _pad_lanes(v, offset=0, width=LANES):
    return jnp.zeros((1, width), F32).at[:, offset:offset + v.shape[1]].set(v)


def _col(v, offset=0, rows=SM_ROWS):
    return jnp.zeros((rows, 1), F32).at[offset:offset + v.shape[1], 0].set(v[0])


def _pack_small(g_mix, dgq, dgk, dbias, dgo, dac, dar, ddc, ddr, g_gdn_o, g_nx, g_mem, g_xq, g_xk, g_mlp, loss_tiles):
    def body(mix_ref, q_ref, k_ref, b_ref, o_ref, ac_ref, ar_ref, dc_ref, dr_ref, go_ref, nx_ref, mem_ref, xq_ref, xk_ref,
             mlp_ref, lt_ref, out_ref):
        lane = lax.broadcasted_iota(jnp.int32, (1, LANES), 1)
        diag = lax.broadcasted_iota(jnp.int32, (SM_ROWS, LANES), 0) == lax.broadcasted_iota(jnp.int32, (SM_ROWS, LANES), 1)

        def rolled(v, shift):
            return pltpu.roll(jnp.broadcast_to(v, (8, LANES)), shift, 1)[0:1, :]

        def rows_to_lanes(col):
            return jnp.sum(jnp.where(diag, col, 0.0), axis=0, keepdims=True)

        def put(row, v, n):
            out_ref[row:row + 1, 0:LANES] = jnp.where(lane < n, v, 0.0)

        out_ref[...] = jnp.zeros_like(out_ref)
        out_ref[0:1, :] = mix_ref[...]
        for row, ref in ((1, q_ref), (2, k_ref), (4, o_ref)):
            put(row, ref[...] + rolled(ref[...], FOX_HEAD_DIM), FOX_HEAD_DIM)
        put(3, rows_to_lanes(b_ref[...]), FOX_HEADS)
        for row, lane_ref, row_ref in ((5, ac_ref, ar_ref), (6, dc_ref, dr_ref)):
            put(row, rolled(lane_ref[...] + rows_to_lanes(row_ref[...]), LANES - SM_A), GDN_HEADS)
        put(7, go_ref[...], LANES)
        out_ref[8:9, :] = nx_ref[...]
        out_ref[9:10, :] = mem_ref[...]
        put(10, xq_ref[...], LANES)
        put(11, xk_ref[...], LANES)
        out_ref[12:13, :] = mlp_ref[...]
        put(LOSS_ROW, 0.5 * jnp.sum(lt_ref[...], axis=0)[0:1, :], 1)

    args = (g_mix, dgq, dgk, dbias, dgo, dac, dar, ddc, ddr, g_gdn_o, g_nx, g_mem, g_xq, g_xk, g_mlp, loss_tiles)
    return pl.pallas_call(body, name="pack_small", out_shape=jax.ShapeDtypeStruct((PACK_ROWS, D_MODEL), F32))(*args)


LATE_WEIGHTS = (("w_out", "w_cq", "w_ckv", "w_co"), ("w_mlp1", "w_mlp2"))
GRAD_GROUPS = (("w_mlp2", "w_mlp1"), ("w_co", "w_cq", "w_ckv", "w_out"), ("w_in", "gdn_conv_w"))


def _local_step(x, mem, target, norm_mix_g, w_in, fox_qnorm_g, fox_knorm_g, fox_f_bias, fox_onorm_g, gdn_conv_w, gdn_A_log,
                gdn_dt_bias, gdn_onorm_g, norm_xattn_g, mem_norm_g, xattn_qnorm_g, xattn_knorm_g, norm_mlp_g,
                late_weights, grads_ready=None, first_token=0.0):
    if grads_ready is None:
        grads_ready = lambda group: 0.0
    n_batch, s_len, d = x.shape
    m_len = mem.shape[1]
    t_len = n_batch * s_len
    tq = min(FOX_BLOCK, s_len)
    nq = s_len // tq
    n_chunks = s_len // GDN_CHUNK
    x2d = x.reshape(t_len, d)

    wp = jnp.concatenate([w_in[0:1536], w_in[1544:3080], w_in[3088:3600], w_in[1536:1544], w_in[3080:3088],
                          jnp.zeros((P_DIM - 3600, d), BF16)], axis=0)
    wst = jnp.concatenate([w_in[1536:1544], w_in[3080:3088]], axis=0)
    conv_w = jnp.concatenate([gdn_conv_w, jnp.zeros((8 - CONV_WIDTH, gdn_conv_w.shape[1]), F32)], axis=0)
    bias_col = _col(fox_f_bias, SM_F)
    gq2, gk2, go2 = (jnp.tile(g, (1, 2)) for g in (fox_qnorm_g, fox_knorm_g, fox_onorm_g))
    a_c, dt_c = _pad_lanes(gdn_A_log, SM_A), _pad_lanes(gdn_dt_bias, SM_A)
    a_r, dt_r = _col(gdn_A_log, SM_A), _col(gdn_dt_bias, SM_A)

    h1, pfox, pgdn, pz, sm, smt = _in_proj(x2d, norm_mix_g + first_token, wp, wst)
    c_rows = _fox_cum(smt, bias_col, n_batch, s_len)
    cb = c_rows.reshape(SM_ROWS, n_batch, nq, tq).transpose(1, 2, 0, 3)
    pf3 = pfox.reshape(n_batch, s_len, 1536)
    o_fox, oa, lse = _fox_fwd(pf3, cb, gq2, gk2, go2, tq)
    pg3 = pgdn.reshape(n_batch, s_len, 1536)
    qkvn = _gdn_pre(pg3, conv_w)
    z3 = pz.reshape(n_batch, s_len, GDN_WIDTH)
    smc = sm.reshape(n_batch, s_len, LANES)
    smr = smt.reshape(SM_ROWS, n_batch * n_chunks, GDN_CHUNK).transpose(1, 0, 2)
    ob, states = _gdn_fwd(qkvn, z3, smc, smr, a_c, dt_c, a_r, dt_r, gdn_onorm_g)
    oa2, ob2 = oa.reshape(t_len, FOX_WIDTH), ob.reshape(t_len, GDN_WIDTH)
    w_out, w_cq, w_ckv, w_co = late_weights(LATE_WEIGHTS[0], ob2)
    x1, hq, cq = _out_proj(x2d, oa2, ob2, w_out, norm_xattn_g, w_cq)
    mem2d = mem.reshape(n_batch * m_len, d)
    hm, ckv = _mem_kv(mem2d, mem_norm_g, w_ckv)
    co, x2, hf = _xattn_fwd(cq, ckv, x1, xattn_qnorm_g, xattn_knorm_g, w_co, norm_mlp_g, n_batch, s_len, m_len)
    w_mlp1, w_mlp2 = late_weights(LATE_WEIGHTS[1], hf)
    u, a_act, dy, loss_tiles = _mlp_fwd(hf, x2, target.reshape(t_len, d), w_mlp1, w_mlp2)

    grads = {}
    du, dx2, grads["norm_mlp_g"] = _mlp_bwd(dy, u, x2, norm_mlp_g, w_mlp1, w_mlp2)
    grads["w_mlp2"] = _wgrad(a_act, dy, "wgrad_mlp2")
    grads["w_mlp1"] = _wgrad(hf, du, "wgrad_mlp1", column_blocks=D_FF // N_DEV)
    token = grads_ready({k: grads[k] for k in GRAD_GROUPS[0]})
    grads["w_co"] = _wgrad(co, dx2, "wgrad_co", column_blocks=D_MODEL // N_DEV)
    dx1, dcq, dckv, grads["xattn_qnorm_g"], grads["xattn_knorm_g"], grads["norm_xattn_g"] = _xattn_bwd(
        dx2, cq, ckv, x1, xattn_qnorm_g + token, xattn_knorm_g, w_co, norm_xattn_g, w_cq, n_batch, s_len, m_len)
    grads["w_cq"] = _wgrad(hq, dcq, "wgrad_cq")
    grads["w_ckv"] = _wgrad(hm, dckv, "wgrad_ckv")
    grads["mem_norm_g"] = _mem_kv_bwd(dckv, mem2d, mem_norm_g, w_ckv)
    grads["w_out"] = _wgrad_stacked([oa2, ob2], dx1, "wgrad_out", bn=1024)
    token = grads_ready({k: grads[k] for k in GRAD_GROUPS[1]})
    dcat = _out_proj_bwd(dx1, w_out)
    dcat3 = dcat.reshape(n_batch, s_len, d)

    dqkvn, dz, dsmc, dsmr, dac, ddc, dar, ddr, grads["gdn_onorm_g"] = _gdn_bwd(
        qkvn, z3, smc, smr, a_c, dt_c, a_r, dt_r, gdn_onorm_g + token, states, dcat3)
    dpg, dconv = _gdn_pre_bwd(pg3, conv_w, dqkvn)
    grads["gdn_conv_w"] = dconv[0:CONV_WIDTH]

    dq, dk, dv, dcb, dgq, dgk, dgo = _fox_bwd(pf3, cb, gq2, gk2, go2, o_fox, lse, dcat3, tq)
    dc8 = dcb[:, :, :, 0:2, :].transpose(1, 3, 0, 2, 4).reshape(FOX_HEADS, t_len)
    dc_rows = jnp.concatenate([dc8, jnp.zeros((SM_ROWS - FOX_HEADS, t_len), F32)], axis=0)
    dl_rows, dbias = _fox_cum_bwd(dc_rows, smt, bias_col, n_batch, s_len)
    dsm_rows = jnp.concatenate([dl_rows[0:SM_B], dsmr.transpose(1, 0, 2).reshape(SM_ROWS, t_len)[SM_B:SM_ROWS]], axis=0)

    dprojs = [dq.reshape(t_len, FOX_WIDTH), dk.reshape(t_len, FOX_WIDTH), dv.reshape(t_len, FOX_WIDTH),
              dpg.reshape(t_len, 1536), dz.reshape(t_len, GDN_WIDTH), dsmc.reshape(t_len, LANES)]
    dwp = _wgrad_stacked(dprojs, h1, "wgrad_in")
    dwst = _rows_matmul(dsm_rows, h1, "wgrad_in_rows")
    dw_small = dwp[P_SMALL:P_SMALL + SM_ROWS] + dwst
    grads["w_in"] = jnp.concatenate([dwp[0:1536], dw_small[0:8], dwp[1536:3072], dw_small[8:16], dwp[3072:3584]], axis=0)
    token = grads_ready({k: grads[k] for k in GRAD_GROUPS[2]})
    grad_x, grads["norm_mix_g"] = _in_proj_bwd(dprojs, dsm_rows, x2d, norm_mix_g + token, wp, wst, dx1)
    packed = _pack_small(grads["norm_mix_g"], dgq, dgk, dbias, dgo, dac, dar, ddc, ddr, grads["gdn_onorm_g"], grads["norm_xattn_g"],
                         grads["mem_norm_g"], grads["xattn_qnorm_g"], grads["xattn_knorm_g"], grads["norm_mlp_g"], loss_tiles)
    return packed, grad_x.reshape(n_batch, s_len, d), {k: grads[k] for k in SHARDED}


MESH_ID = pl.DeviceIdType.MESH
ANY_SPEC = pl.BlockSpec(memory_space=pl.ANY)


def _place():
    x, y, c = lax.axis_index("x"), lax.axis_index("y"), lax.axis_index("c")
    return x, y, c, [(1 - x, y), (x, 1 - y), (1 - x, 1 - y)]


def _place_own(src_ref, dst_ref):
    def staged(buf, sem):
        for a, b in ((src_ref, buf), (buf, dst_ref)):
            cp = pltpu.make_async_copy(a, b, sem)
            cp.start()
            cp.wait()

    pl.run_scoped(staged, pltpu.VMEM(src_ref.shape, src_ref.dtype), pltpu.SemaphoreType.DMA)


def _all_gather_body(n, ins, outs, send_sems, recv_sems):
    x, y, c, chips = _place()
    me, sibling = (x, y, c), (x, y, 1 - c)

    def copy(a, k, block, to, src=None):
        dst = outs[a].at[4 * block[0] + 2 * block[1] + block[2]]
        return pltpu.make_async_remote_copy(src_ref=dst if src is None else src, dst_ref=dst, send_sem=send_sems.at[a, k],
                                            recv_sem=recv_sems.at[a, k], device_id=to, device_id_type=MESH_ID)

    first = []
    for a in range(n):
        first.append(copy(a, 0, me, sibling, src=ins[a]))
        first += [copy(a, 1 + j, me, (*chip, c), src=ins[a]) for j, chip in enumerate(chips)]
    for cp in first:
        cp.start()
    for a in range(n):
        _place_own(ins[a], outs[a].at[4 * x + 2 * y + c])
    passed = []
    for j, chip in enumerate(chips):
        for a in range(n):
            copy(a, 1 + j, (*chip, c), me).wait_recv()
            fwd = copy(a, 4 + j, (*chip, c), sibling)
            fwd.start()
            passed.append(fwd)
    for a in range(n):
        copy(a, 0, sibling, me).wait_recv()
        for j, chip in enumerate(chips):
            copy(a, 4 + j, (*chip, 1 - c), me).wait_recv()
    for cp in first + passed:
        cp.wait_send()


def _all_gather_hbm(arrs, name):
    n = len(arrs)

    def body(*refs):
        _all_gather_body(n, refs[:n], refs[n:2 * n], refs[2 * n], refs[2 * n + 1])

    return pl.pallas_call(
        body, name=name, in_specs=[ANY_SPEC] * n, out_specs=[ANY_SPEC] * n,
        out_shape=[jax.ShapeDtypeStruct((N_DEV,) + a.shape, a.dtype) for a in arrs],
        scratch_shapes=[pltpu.SemaphoreType.DMA((n, 7)), pltpu.SemaphoreType.DMA((n, 7))],
        compiler_params=pltpu.CompilerParams(vmem_limit_bytes=VMEM_LIMIT),
    )(*arrs)


def _pair_exchange(arrs, name):
    n = len(arrs)

    def body(*refs):
        ins, outs = refs[:n], refs[n:2 * n]
        send_sems, recv_sems = refs[2 * n:]
        x, y, c, _ = _place()
        copies = []
        for a in range(n):
            for chip in range(4):
                copies.append(pltpu.make_async_remote_copy(
                    src_ref=ins[a].at[2 * chip + (1 - c)], dst_ref=outs[a].at[chip], send_sem=send_sems.at[a, chip],
                    recv_sem=recv_sems.at[a, chip], device_id=(x, y, 1 - c), device_id_type=MESH_ID))
        for cp in copies:
            cp.start()
        for cp in copies:
            cp.wait()

    return pl.pallas_call(
        body, name=name, in_specs=[ANY_SPEC] * n, out_specs=[ANY_SPEC] * n,
        out_shape=[jax.ShapeDtypeStruct((4,) + a.shape[1:], a.dtype) for a in arrs],
        scratch_shapes=[pltpu.SemaphoreType.DMA((n, 4)), pltpu.SemaphoreType.DMA((n, 4))],
    )(*arrs)


HBM_SPEC = pl.BlockSpec(memory_space=pltpu.HBM)
SEM_SPEC = pl.BlockSpec(memory_space=pltpu.SEMAPHORE)
DATAFLOW = pltpu.SideEffectType.DATAFLOW_SIDE_EFFECTING


def _in_hbm(arrs):
    return [pltpu.with_memory_space_constraint(a, pltpu.HBM) for a in arrs]


def _copies_start(name, srcs, lands, make_copies, after):
    n = len(srcs)
    n_copies = len(make_copies(srcs, lands, None, None)[0])

    def body(*refs):
        send_sems, recv_sems = refs[2 * n + 1], refs[2 * n + 2]
        for row in make_copies(refs[:n], refs[n:2 * n], send_sems, recv_sems):
            for cp in row:
                cp.start()
        refs[-1][...] = jnp.zeros_like(refs[-1])

    sems = pltpu.SemaphoreType.DMA((n * n_copies,))
    thru = [pltpu.HBM(a.shape, a.dtype) for a in list(srcs) + list(lands)]
    res = pl.pallas_call(
        body, name=name, in_specs=[HBM_SPEC] * (2 * n) + [ANY_SPEC],
        out_specs=(SEM_SPEC, SEM_SPEC, *[HBM_SPEC] * (2 * n), pl.BlockSpec(memory_space=pltpu.VMEM)),
        out_shape=(sems, sems, *thru, jax.ShapeDtypeStruct((8, LANES), F32)),
        input_output_aliases={i: 2 + i for i in range(2 * n)},
        compiler_params=pltpu.CompilerParams(has_side_effects=DATAFLOW),
    )(*_in_hbm(list(srcs) + list(lands)), after)
    return res[0], res[1], list(res[2:2 + n]), list(res[2 + n:2 + 2 * n]), res[-1]


def _copies_wait(name, send_sems, recv_sems, srcs, lands, after, make_copies, own_block=False):
    n = len(srcs)

    def body(*refs):
        if own_block:
            for a in range(n):
                _place_own(refs[a], _own_part(refs[a], refs[3 * n + 3 + a]))
        for row in make_copies(refs[:n], refs[n:2 * n], refs[2 * n], refs[2 * n + 1]):
            for cp in row:
                cp.wait_send()
                cp.wait_recv()

    res = pl.pallas_call(
        body, name=name, in_specs=[HBM_SPEC] * (2 * n) + [SEM_SPEC, SEM_SPEC, ANY_SPEC],
        out_specs=tuple([HBM_SPEC] * (2 * n)),
        out_shape=tuple(pltpu.HBM(a.shape, a.dtype) for a in list(srcs) + list(lands)),
        input_output_aliases={i: i for i in range(2 * n)},
        compiler_params=pltpu.CompilerParams(has_side_effects=DATAFLOW, vmem_limit_bytes=VMEM_LIMIT),
    )(*srcs, *lands, send_sems, recv_sems, after)
    return list(res[:n]), list(res[n:])


def _own_part(src_ref, land_ref):
    me = 4 * lax.axis_index("x") + 2 * lax.axis_index("y") + lax.axis_index("c")
    rows, cols = src_ref.shape
    if land_ref.shape[0] == N_DEV * rows:
        return land_ref.at[pl.ds(pl.multiple_of(me * rows, rows), rows), :]
    return land_ref.at[:, pl.ds(pl.multiple_of(me * cols, cols), cols)]


def _gather_copies(srcs, lands, send_sems, recv_sems):
    if send_sems is None:
        return [[None] * 7]
    x, y, c, _ = _place()
    rows = []
    for a in range(len(srcs)):
        row = []
        for k in range(7):
            r = k + 1
            to = (1 - x if r & 4 else x, 1 - y if r & 2 else y, 1 - c if r & 1 else c)
            row.append(pltpu.make_async_remote_copy(
                src_ref=srcs[a], dst_ref=_own_part(srcs[a], lands[a]), send_sem=send_sems.at[7 * a + k], recv_sem=recv_sems.at[7 * a + k],
                device_id=to, device_id_type=MESH_ID))
        rows.append(row)
    return rows


def _scatter_copies(srcs, lands, send_sems, recv_sems):
    if send_sems is None:
        return [[None] * 7]
    x, y, c, _ = _place()
    rows = []
    for a in range(len(srcs)):
        row = []
        for k in range(7):
            r = k + 1
            to = (1 - x if r & 4 else x, 1 - y if r & 2 else y, 1 - c if r & 1 else c)
            row.append(pltpu.make_async_remote_copy(
                src_ref=srcs[a].at[4 * to[0] + 2 * to[1] + to[2]], dst_ref=lands[a].at[k], send_sem=send_sems.at[7 * a + k],
                recv_sem=recv_sems.at[7 * a + k], device_id=to, device_id_type=MESH_ID))
        rows.append(row)
    return rows


def _chip_copies(srcs, lands, send_sems, recv_sems):
    if send_sems is None:
        return [[None] * 3]
    x, y, c, chips = _place()
    return [[pltpu.make_async_remote_copy(
        src_ref=srcs[a].at[2 * chip[0] + chip[1]], dst_ref=lands[a].at[j], send_sem=send_sems.at[3 * a + j], recv_sem=recv_sems.at[3 * a + j],
        device_id=(*chip, c), device_id_type=MESH_ID) for j, chip in enumerate(chips)] for a in range(len(srcs))]


def _tile(rows, cols):
    if rows <= 256:
        return rows, cols
    tr = 256 if cols <= 512 else 128
    if rows % tr == 0:
        return tr, cols
    return rows, 256


def _pair_sum(core, own, got, name):
    _, rows, cols = own.shape
    tr, tc = _tile(rows, cols)

    def body(c_ref, own_ref, got_ref, o_ref):
        o_ref[0] = own_ref[0] + got_ref[0]

    return pl.pallas_call(
        body, name=name,
        grid_spec=pltpu.PrefetchScalarGridSpec(
            num_scalar_prefetch=1, grid=(4, rows // tr, cols // tc),
            in_specs=[pl.BlockSpec((1, tr, tc), lambda k, i, j, c: (2 * k + c[0], i, j)),
                      pl.BlockSpec((1, tr, tc), lambda k, i, j, c: (k, i, j))],
            out_specs=pl.BlockSpec((1, tr, tc), lambda k, i, j, c: (k, i, j))),
        out_shape=jax.ShapeDtypeStruct((4, rows, cols), F32),
        compiler_params=_cparams(("parallel", "parallel", "parallel")),
    )(core, own, got)


def _adamw(w, g, m, v):
    m_new = ADAM_B1 * m + (1.0 - ADAM_B1) * g
    v_new = ADAM_B2 * v + (1.0 - ADAM_B2) * (g * g)
    m_hat = m_new / (1.0 - ADAM_B1 ** ADAM_STEP)
    v_hat = v_new / (1.0 - ADAM_B2 ** ADAM_STEP)
    delta = -ADAM_LR * (m_hat / (jnp.sqrt(v_hat) + ADAM_EPS) + ADAM_WD * w)
    return delta, m_new, v_new


def _sum_adam(chip, sums, parts, w, m, v, name):
    n_parts, rows, cols = parts.shape
    tr, tc = _tile(rows, cols)

    def body(chip_ref, own_ref, p_ref, w_ref, m_ref, v_ref, g_ref, d_ref, mo_ref, vo_ref):
        g = own_ref[0]
        for k in range(n_parts):
            g = g + p_ref[k]
        g_ref[...] = g
        d_ref[...], mo_ref[...], vo_ref[...] = _adamw(w_ref[...], g, m_ref[...], v_ref[...])

    tile = pl.BlockSpec((tr, tc), lambda i, j, ch: (i, j))
    out = jax.ShapeDtypeStruct((rows, cols), F32)
    return pl.pallas_call(
        body, name=name,
        grid_spec=pltpu.PrefetchScalarGridSpec(
            num_scalar_prefetch=1, grid=(rows // tr, cols // tc),
            in_specs=[pl.BlockSpec((1, tr, tc), lambda i, j, ch: (ch[0], i, j)),
                      pl.BlockSpec((n_parts, tr, tc), lambda i, j, ch: (0, i, j)), tile, tile, tile],
            out_specs=[tile, tile, tile, tile]),
        out_shape=[out, out, out, out],
        compiler_params=_cparams(("parallel", "parallel")),
    )(chip, sums, parts, w, m, v)


SHARDED = ("w_in", "gdn_conv_w", "w_out", "w_cq", "w_ckv", "w_co", "w_mlp1", "w_mlp2")
TRANSPOSED = ("w_in",)
COLUMN_SHARDED = ("gdn_conv_w", "w_co", "w_mlp1")
REPLICATED = ("norm_mix_g", "fox_qnorm_g", "fox_knorm_g", "fox_f_bias", "fox_onorm_g", "gdn_A_log", "gdn_dt_bias", "gdn_onorm_g",
              "norm_xattn_g", "mem_norm_g", "xattn_qnorm_g", "xattn_knorm_g", "norm_mlp_g")
WEIGHTS = ("norm_mix_g", "w_in", "fox_qnorm_g", "fox_knorm_g", "fox_f_bias", "fox_onorm_g", "gdn_conv_w", "gdn_A_log", "gdn_dt_bias",
           "gdn_onorm_g", "w_out", "norm_xattn_g", "mem_norm_g", "w_cq", "w_ckv", "xattn_qnorm_g", "xattn_knorm_g", "w_co",
           "norm_mlp_g", "w_mlp1", "w_mlp2")
PACK_ROWS = 16
LOSS_ROW = len(REPLICATED)


def _whole(name, gathered):
    if name in COLUMN_SHARDED:
        return gathered.transpose(1, 0, 2).reshape(gathered.shape[1], N_DEV * gathered.shape[2])
    return gathered.reshape(N_DEV * gathered.shape[1], gathered.shape[2])


def _whole_shape(name, shard_shape):
    rows, cols = shard_shape
    return (rows, N_DEV * cols) if name in COLUMN_SHARDED else (N_DEV * rows, cols)


def _blocks(name, whole):
    if whole.ndim == 3:
        return whole
    if name in COLUMN_SHARDED:
        rows, cols = whole.shape
        return whole.reshape(rows, N_DEV, cols // N_DEV).transpose(1, 0, 2)
    return whole.reshape(N_DEV, whole.shape[0] // N_DEV, whole.shape[1])


def _adam_small(everyone, ws, ms, vs):
    n_par = len(ws)

    def body(*refs):
        ev_ref = refs[0]
        w_refs, m_refs, v_refs = (refs[1 + j * n_par:1 + (j + 1) * n_par] for j in range(3))
        outs = refs[1 + 3 * n_par:-1]
        sum_ref = refs[-1]
        total = ev_ref[0]
        for dev in range(1, N_DEV):
            total = total + ev_ref[dev]
        sum_ref[...] = total
        for i in range(n_par):
            n = w_refs[i].shape[1]
            g = sum_ref[i:i + 1, 0:n]
            outs[4 * i][...] = g
            outs[4 * i + 1][...], outs[4 * i + 2][...], outs[4 * i + 3][...] = _adamw(w_refs[i][...], g, m_refs[i][...], v_refs[i][...])
        outs[4 * n_par][...] = sum_ref[LOSS_ROW:LOSS_ROW + 1, 0:1]

    shapes = [jax.ShapeDtypeStruct(a.shape, F32) for a in ws for _ in range(4)] + [jax.ShapeDtypeStruct((1, 1), F32)]
    return pl.pallas_call(body, name="adam_small", out_shape=shapes,
                          scratch_shapes=[pltpu.VMEM((PACK_ROWS, D_MODEL), F32)])(everyone, *ws, *ms, *vs)


def kernel(x, mem, norm_mix_g, w_in, fox_qnorm_g, fox_knorm_g, fox_f_bias, fox_onorm_g, gdn_conv_w, gdn_A_log, gdn_dt_bias, gdn_onorm_g, w_out, norm_xattn_g, mem_norm_g, w_cq, w_ckv, xattn_qnorm_g, xattn_knorm_g, w_co, norm_mlp_g, w_mlp1, w_mlp2, loss_target, m_norm_mix_g, m_w_in, m_fox_qnorm_g, m_fox_knorm_g, m_fox_f_bias, m_fox_onorm_g, m_gdn_conv_w, m_gdn_A_log, m_gdn_dt_bias, m_gdn_onorm_g, m_w_out, m_norm_xattn_g, m_mem_norm_g, m_w_cq, m_w_ckv, m_xattn_qnorm_g, m_xattn_knorm_g, m_w_co, m_norm_mlp_g, m_w_mlp1, m_w_mlp2, v_norm_mix_g, v_w_in, v_fox_qnorm_g, v_fox_knorm_g, v_fox_f_bias, v_fox_onorm_g, v_gdn_conv_w, v_gdn_A_log, v_gdn_dt_bias, v_gdn_onorm_g, v_w_out, v_norm_xattn_g, v_mem_norm_g, v_w_cq, v_w_ckv, v_xattn_qnorm_g, v_xattn_knorm_g, v_w_co, v_norm_mlp_g, v_w_mlp1, v_w_mlp2):
    given = dict(locals())
    w = {k: given[k] for k in WEIGHTS}
    m = {k: given["m_" + k] for k in WEIGHTS}
    v = {k: given["v_" + k] for k in WEIGHTS}

    core = lax.axis_index("c").astype(jnp.int32).reshape(1)
    chip = (2 * lax.axis_index("x") + lax.axis_index("y")).astype(jnp.int32).reshape(1)
    me = 4 * lax.axis_index("x") + 2 * lax.axis_index("y") + lax.axis_index("c")

    local = lambda d: {k: jnp.transpose(d[k][0]) if k in TRANSPOSED else d[k][0] for k in SHARDED}
    w2, m2, v2 = local(w), local(m), local(v)
    shards = {k: w2[k] if k == "gdn_conv_w" else w2[k].astype(BF16) for k in SHARDED}
    early = [k for k in SHARDED if not any(k in group for group in LATE_WEIGHTS)]
    gathered = _all_gather_hbm([shards[k] for k in early], "gather_early")
    whole = {k: _whole(k, g) for k, g in zip(early, gathered)}
    gathers, after = {}, gathered[0]
    for i, group in enumerate(LATE_WEIGHTS):
        lands = [lax.empty(_whole_shape(k, shards[k].shape), BF16) for k in group]
        gathers[group] = _copies_start("gather_late_start_" + str(i), [shards[k] for k in group], lands, _gather_copies, after=after)
        after = gathers[group][4]
    first_token = after[0, 0]

    def late_weights(group, after):
        gather = gathers[group]
        _, lands = _copies_wait("gather_late_wait_" + str(LATE_WEIGHTS.index(group)), gather[0], gather[1], gather[2], gather[3],
                                after, _gather_copies, own_block=True)
        return lands

    pending = []

    def grads_ready(group):
        names = list(group)
        tag = str(len(pending))
        own = [_blocks(k, group[k]) for k in names]
        if "w_in" in names:
            got = _pair_exchange(own, "grad_pair_exchange_" + tag)
            srcs = [_pair_sum(core, o, g, "grad_pair_sum_" + k) for k, o, g in zip(names, own, got)]
            copies, index, n_parts = _chip_copies, chip, 3
        else:
            srcs, copies, index, n_parts = own, _scatter_copies, me.astype(jnp.int32).reshape(1), 7
        lands = [lax.empty((n_parts,) + s.shape[1:], s.dtype) for s in srcs]
        started = _copies_start("grad_exchange_start_" + tag, srcs, lands, copies, after=srcs[0])
        pending.append((names, started, copies, index))
        return started[4][0, 0]

    small = {k: w[k] for k in REPLICATED}
    packed, grad_x, _ = _local_step(x, mem, loss_target, **small, **whole, late_weights=late_weights,
                                    grads_ready=grads_ready, first_token=first_token)

    small_lands = [lax.empty((N_DEV * PACK_ROWS, D_MODEL), F32)]
    small_gather = _copies_start("gather_small_start", [packed], small_lands, _gather_copies, after=grad_x)

    out_g, out_d, out_m, out_v = {}, {}, {}, {}
    after = small_gather[4]
    for tag, (names, started, copies, index) in enumerate(pending):
        srcs, parts = _copies_wait("grad_exchange_wait_" + str(tag), started[0], started[1], started[2], started[3], after, copies)
        for k, s, p in zip(names, srcs, parts):
            res = _sum_adam(index, s, p, w2[k], m2[k], v2[k], "adam_" + k)
            out_g[k], out_d[k], out_m[k], out_v[k] = ((jnp.transpose(r) if k in TRANSPOSED else r)[None] for r in res)
            after = res[0]

    _, (everyone,) = _copies_wait("gather_small_wait", small_gather[0], small_gather[1], small_gather[2], small_gather[3], after,
                                  _gather_copies, own_block=True)
    res = _adam_small(everyone.reshape(N_DEV, PACK_ROWS, D_MODEL), [w[k] for k in REPLICATED], [m[k] for k in REPLICATED],
                      [v[k] for k in REPLICATED])
    for i, k in enumerate(REPLICATED):
        out_g[k], out_d[k], out_m[k], out_v[k] = res[4 * i:4 * i + 4]
    loss = res[-1].reshape(())

    return (loss, grad_x, *[out_g[k] for k in WEIGHTS], *[out_d[k] for k in WEIGHTS], *[out_m[k] for k in WEIGHTS],
            *[out_v[k] for k in WEIGHTS])
```

```python
import functools

import jax
import jax.numpy as jnp
import numpy as np
from jax import lax
from jax.experimental import pallas as pl
from jax.experimental.pallas import tpu as pltpu

F32 = jnp.float32
BF16 = jnp.bfloat16

D_MODEL = 1024
FOX_HEADS = 8
FOX_HEAD_DIM = 64
FOX_WIDTH = 512
GDN_HEADS = 4
GDN_HEAD_DIM = 128
GDN_WIDTH = 512
CONV_WIDTH = 4
GDN_CHUNK = 128
GDN_GROUP = 4
FOX_BLOCK = 512
XATTN_HEADS = 4
XATTN_HEAD_DIM = 128
XATTN_WIDTH = 512
D_FF = 4096
EPS = 1e-6
NEG_INF = -1e30
N_DEV = 8

ADAM_LR = 0.001
ADAM_B1 = 0.9
ADAM_B2 = 0.999
ADAM_EPS = 1e-08
ADAM_WD = 0.01
ADAM_STEP = 10

P_FOX = 0
P_GDN = 1536
P_Z = 3072
P_SMALL = 3584
P_DIM = 3712
SM_F = 0
SM_B = 8
SM_A = 12
SM_ROWS = 16

LANES = 128
VMEM_LIMIT = 56 * 1024 * 1024

NN = (((1,), (0,)), ((), ()))
NT = (((1,), (1,)), ((), ()))
TN = (((0,), (0,)), ((), ()))


def _dot(a, b, dims=NN):
    return lax.dot_general(a.astype(BF16), b.astype(BF16), dims, preferred_element_type=F32)


def _cparams(sem=None):
    kw = dict(vmem_limit_bytes=VMEM_LIMIT)
    if sem is not None:
        kw["dimension_semantics"] = sem
    return pltpu.CompilerParams(**kw)


def _sigmoid(x):
    return 0.5 * (jnp.tanh(0.5 * x) + 1.0)


def _softplus(x):
    return jnp.maximum(x, 0.0) + jnp.log1p(jnp.exp(-jnp.abs(x)))


def _log_sigmoid(x):
    return -_softplus(-x)


def _rms(x, g):
    r = lax.rsqrt(jnp.mean(x * x, axis=-1, keepdims=True) + EPS)
    return x * r * g


def _rms_bwd(x, g, dy):
    r = lax.rsqrt(jnp.mean(x * x, axis=-1, keepdims=True) + EPS)
    xh = x * r
    dg = jnp.sum(dy * xh, axis=0, keepdims=True)
    dyg = dy * g
    dx = r * (dyg - xh * jnp.mean(dyg * xh, axis=-1, keepdims=True))
    return dx, dg


def _pair_stat(t, m0):
    s0 = jnp.sum(jnp.where(m0, t, 0.0), axis=-1, keepdims=True)
    s1 = jnp.sum(jnp.where(m0, 0.0, t), axis=-1, keepdims=True)
    return jnp.where(m0, s0, s1)


def _rms_pair(x, g, m0):
    r = lax.rsqrt(_pair_stat(x * x, m0) * (1.0 / FOX_HEAD_DIM) + EPS)
    return x * r * g


def _rms_pair_bwd(x, g, dy, m0):
    r = lax.rsqrt(_pair_stat(x * x, m0) * (1.0 / FOX_HEAD_DIM) + EPS)
    xh = x * r
    dg = jnp.sum(dy * xh, axis=0, keepdims=True)
    dyg = dy * g
    dx = r * (dyg - xh * (_pair_stat(dyg * xh, m0) * (1.0 / FOX_HEAD_DIM)))
    return dx, dg


@jax.custom_vjp
def _mm_nn(a, b):
    return _dot(a, b, NN)


_mm_nn.defvjp(lambda a, b: (_dot(a, b, NN), (a, b)),
              lambda r, g: (_dot(g, r[1], NT), _dot(r[0], g, TN)))


@jax.custom_vjp
def _mm_nt(a, b):
    return _dot(a, b, NT)


_mm_nt.defvjp(lambda a, b: (_dot(a, b, NT), (a, b)),
              lambda r, g: (_dot(g, r[1], NN), _dot(g, r[0], TN)))


@jax.custom_vjp
def _mm_tn(a, b):
    return _dot(a, b, TN)


_mm_tn.defvjp(lambda a, b: (_dot(a, b, TN), (a, b)),
              lambda r, g: (_dot(r[1], g, NT), _dot(r[0], g, NN)))


def _dot3(a, b, dims):
    ah = a.astype(BF16)
    al = (a - ah.astype(F32)).astype(BF16)
    bh = b.astype(BF16)
    bl = (b - bh.astype(F32)).astype(BF16)
    d = functools.partial(lax.dot_general, dimension_numbers=dims, preferred_element_type=F32)
    return d(ah, bh) + d(ah, bl) + d(al, bh)


def _neumann_inverses(mats):
    c = mats[0].shape[0]
    eye = (lax.broadcasted_iota(jnp.int32, (c, c), 0) == lax.broadcasted_iota(jnp.int32, (c, c), 1)).astype(F32)
    xs = [eye - a for a in mats]
    ps = list(mats)
    k = 2
    while k < c + 1:
        ps = [_dot3(p, p, NN) for p in ps]
        xs = [x + _dot3(x, p, NN) for x, p in zip(xs, ps)]
        k *= 2
    return xs


@jax.custom_vjp
def _unit_lower_inverses(mats):
    return _neumann_inverses(mats)


def _unit_lower_inverses_fwd(mats):
    ts = _neumann_inverses(mats)
    return ts, ts


def _unit_lower_inverses_bwd(ts, gs):
    left = [_dot3(t, g, TN) for t, g in zip(ts, gs)]
    return ([-_dot3(m, t, NT) for m, t in zip(left, ts)],)


_unit_lower_inverses.defvjp(_unit_lower_inverses_fwd, _unit_lower_inverses_bwd)


def _wgrad(a, b, name, bk=1024, bn=1024, bt=512, column_blocks=None):
    t_len, k_len = a.shape
    n_len = b.shape[1]
    bk, bn, bt = min(bk, k_len), min(bn, n_len), min(bt, t_len)
    nt = t_len // bt

    def body(a_ref, b_ref, o_ref, acc_ref):
        t = pl.program_id(2)

        @pl.when(t == 0)
        def _():
            acc_ref[...] = jnp.zeros_like(acc_ref)

        acc_ref[...] += _dot(a_ref[...], b_ref[...], TN)

        @pl.when(t == nt - 1)
        def _():
            if column_blocks:
                for jj in range(bn // column_blocks):
                    o_ref[jj] = acc_ref[:, jj * column_blocks:(jj + 1) * column_blocks]
            else:
                o_ref[...] = acc_ref[...]

    if column_blocks:
        out_spec = pl.BlockSpec((bn // column_blocks, bk, column_blocks), lambda i, j, t: (j, i, 0))
        out_shape = jax.ShapeDtypeStruct((n_len // column_blocks, k_len, column_blocks), F32)
    else:
        out_spec = pl.BlockSpec((bk, bn), lambda i, j, t: (i, j))
        out_shape = jax.ShapeDtypeStruct((k_len, n_len), F32)
    return pl.pallas_call(
        body, name=name, grid=(k_len // bk, n_len // bn, nt),
        in_specs=[pl.BlockSpec((bt, bk), lambda i, j, t: (t, i)), pl.BlockSpec((bt, bn), lambda i, j, t: (t, j))],
        out_specs=out_spec, out_shape=out_shape,
        scratch_shapes=[pltpu.VMEM((bk, bn), F32)],
        compiler_params=_cparams(("parallel", "parallel", "arbitrary")),
    )(a, b)


def _wgrad_stacked(pieces, b, name, bn=512, bt=512):
    t_len, n_len = b.shape
    n_p = len(pieces)
    starts = [int(s) for s in np.cumsum([0] + [p.shape[1] for p in pieces])]
    bn, bt = min(bn, n_len), min(bt, t_len)
    nt = t_len // bt

    def body(*refs):
        b_ref, o_ref, acc_ref = refs[n_p:]
        t = pl.program_id(1)

        @pl.when(t == 0)
        def _():
            acc_ref[...] = jnp.zeros_like(acc_ref)

        for k in range(n_p):
            acc_ref[starts[k]:starts[k + 1], :] += _dot(refs[k][...], b_ref[...], TN)

        @pl.when(t == nt - 1)
        def _():
            o_ref[...] = acc_ref[...]

    return pl.pallas_call(
        body, name=name, grid=(n_len // bn, nt),
        in_specs=[pl.BlockSpec((bt, p.shape[1]), lambda j, t: (t, 0)) for p in pieces] + [pl.BlockSpec((bt, bn), lambda j, t: (t, j))],
        out_specs=pl.BlockSpec((starts[-1], bn), lambda j, t: (0, j)),
        out_shape=jax.ShapeDtypeStruct((starts[-1], n_len), F32),
        scratch_shapes=[pltpu.VMEM((starts[-1], bn), F32)],
        compiler_params=_cparams(("parallel", "arbitrary")),
    )(*pieces, b)


def _rows_matmul(a, b, name, bt=512):
    r_len, t_len = a.shape
    n_len = b.shape[1]
    bt = min(bt, t_len)
    nt = t_len // bt

    def body(a_ref, b_ref, o_ref):
        t = pl.program_id(0)

        @pl.when(t == 0)
        def _():
            o_ref[...] = jnp.zeros_like(o_ref)

        o_ref[...] += _dot(a_ref[...], b_ref[...], NN)

    return pl.pallas_call(
        body, name=name, grid=(nt,),
        in_specs=[pl.BlockSpec((r_len, bt), lambda t: (0, t)), pl.BlockSpec((bt, n_len), lambda t: (t, 0))],
        out_specs=pl.BlockSpec((r_len, n_len), lambda t: (0, 0)),
        out_shape=jax.ShapeDtypeStruct((r_len, n_len), F32),
        compiler_params=_cparams(("arbitrary",)),
    )(a, b)


def _in_proj(x, g, wp, wst, tm=256):
    t_len, d = x.shape
    tm = min(tm, t_len)

    def body(x_ref, g_ref, wp_ref, wst_ref, h_ref, fox_ref, gdn_ref, z_ref, sm_ref, smt_ref):
        h = _rms(x_ref[...], g_ref[...]).astype(BF16)
        h_ref[...] = h
        p = _dot(h, wp_ref[...], NT)
        fox_ref[...] = p[:, P_FOX:P_GDN]
        gdn_ref[...] = p[:, P_GDN:P_Z]
        z_ref[...] = p[:, P_Z:P_SMALL]
        sm_ref[...] = p[:, P_SMALL:P_DIM]
        smt_ref[...] = _dot(wst_ref[...], h, NT)

    row = lambda i: (i, 0)
    fixed = lambda i: (0, 0)
    return pl.pallas_call(
        body, name="in_proj", grid=(t_len // tm,),
        in_specs=[pl.BlockSpec((tm, d), row), pl.BlockSpec((1, d), fixed), pl.BlockSpec((P_DIM, d), fixed),
                  pl.BlockSpec((SM_ROWS, d), fixed)],
        out_specs=[pl.BlockSpec((tm, d), row), pl.BlockSpec((tm, 1536), row), pl.BlockSpec((tm, 1536), row),
                   pl.BlockSpec((tm, 512), row), pl.BlockSpec((tm, LANES), row), pl.BlockSpec((SM_ROWS, tm), lambda i: (0, i))],
        out_shape=[jax.ShapeDtypeStruct((t_len, d), BF16), jax.ShapeDtypeStruct((t_len, 1536), F32),
                   jax.ShapeDtypeStruct((t_len, 1536), F32), jax.ShapeDtypeStruct((t_len, 512), F32),
                   jax.ShapeDtypeStruct((t_len, LANES), F32), jax.ShapeDtypeStruct((SM_ROWS, t_len), F32)],
        compiler_params=_cparams(("parallel",)),
    )(x, g, wp, wst)


def _in_proj_bwd(dprojs, dsmt, x, g, wp, wst, dx1, tm=256):
    t_len, d = x.shape
    tm = min(tm, t_len)
    n_p = len(dprojs)
    starts = np.cumsum([0] + [p.shape[1] for p in dprojs])

    def body(*refs):
        dp_refs = refs[:n_p]
        dst_ref, x_ref, g_ref, wp_ref, wst_ref, dx1_ref, dx_ref, dg_ref = refs[n_p:]
        i = pl.program_id(0)
        dh = _dot(dst_ref[...], wst_ref[...], TN)
        for k in range(n_p):
            dh = dh + _dot(dp_refs[k][...], wp_ref[int(starts[k]):int(starts[k + 1]), :], NN)
        dxn, dg = _rms_bwd(x_ref[...], g_ref[...], dh)
        dx_ref[...] = dx1_ref[...] + dxn

        @pl.when(i == 0)
        def _():
            dg_ref[...] = jnp.zeros_like(dg_ref)

        dg_ref[...] += dg

    row = lambda i: (i, 0)
    fixed = lambda i: (0, 0)
    return pl.pallas_call(
        body, name="in_proj_bwd", grid=(t_len // tm,),
        in_specs=[pl.BlockSpec((tm, p.shape[1]), row) for p in dprojs] + [
            pl.BlockSpec((SM_ROWS, tm), lambda i: (0, i)), pl.BlockSpec((tm, d), row),
            pl.BlockSpec((1, d), fixed), pl.BlockSpec((P_DIM, d), fixed), pl.BlockSpec((SM_ROWS, d), fixed),
            pl.BlockSpec((tm, d), row)],
        out_specs=[pl.BlockSpec((tm, d), row), pl.BlockSpec((1, d), fixed)],
        out_shape=[jax.ShapeDtypeStruct((t_len, d), F32), jax.ShapeDtypeStruct((1, d), F32)],
        compiler_params=_cparams(("arbitrary",)),
    )(*dprojs, dsmt, x, g, wp, wst, dx1)


def _fox_cum(smt, bias_col, n_batch, s_len, ck=256):
    ck = min(ck, s_len)

    def body(s_ref, b_ref, c_ref):
        tri = (lax.broadcasted_iota(jnp.int32, (ck, ck), 0) <= lax.broadcasted_iota(jnp.int32, (ck, ck), 1)).astype(F32)
        carry = jnp.zeros((SM_ROWS, 1), F32)
        for r in range(s_len // ck):
            ls = _log_sigmoid(s_ref[:, r * ck:(r + 1) * ck] + b_ref[...])
            c = jnp.dot(ls, tri, precision=lax.Precision.HIGHEST, preferred_element_type=F32) + carry
            c_ref[:, r * ck:(r + 1) * ck] = c
            carry = c[:, ck - 1:ck]

    return pl.pallas_call(
        body, name="fox_cum", grid=(n_batch,),
        in_specs=[pl.BlockSpec((SM_ROWS, s_len), lambda b: (0, b)), pl.BlockSpec((SM_ROWS, 1), lambda b: (0, 0))],
        out_specs=pl.BlockSpec((SM_ROWS, s_len), lambda b: (0, b)),
        out_shape=jax.ShapeDtypeStruct(smt.shape, F32),
        compiler_params=_cparams(("parallel",)),
    )(smt, bias_col)


def _fox_cum_bwd(dc, smt, bias_col, n_batch, s_len, ck=256):
    ck = min(ck, s_len)
    nr = s_len // ck

    def body(dc_ref, s_ref, b_ref, dl_ref, db_ref):
        b = pl.program_id(0)
        tri = (lax.broadcasted_iota(jnp.int32, (ck, ck), 0) >= lax.broadcasted_iota(jnp.int32, (ck, ck), 1)).astype(F32)
        carry = jnp.zeros((SM_ROWS, 1), F32)
        tot = jnp.zeros((SM_ROWS, 1), F32)
        for r in reversed(range(nr)):
            sl = slice(r * ck, (r + 1) * ck)
            dls = jnp.dot(dc_ref[:, sl], tri, precision=lax.Precision.HIGHEST, preferred_element_type=F32) + carry
            carry = dls[:, 0:1]
            dl = dls * (1.0 - _sigmoid(s_ref[:, sl] + b_ref[...]))
            dl_ref[:, sl] = dl
            tot = tot + jnp.sum(dl, axis=1, keepdims=True)

        @pl.when(b == 0)
        def _():
            db_ref[...] = jnp.zeros_like(db_ref)

        db_ref[...] += jnp.broadcast_to(tot, db_ref.shape)

    return pl.pallas_call(
        body, name="fox_cum_bwd", grid=(n_batch,),
        in_specs=[pl.BlockSpec((SM_ROWS, s_len), lambda b: (0, b)), pl.BlockSpec((SM_ROWS, s_len), lambda b: (0, b)),
                  pl.BlockSpec((SM_ROWS, 1), lambda b: (0, 0))],
        out_specs=[pl.BlockSpec((SM_ROWS, s_len), lambda b: (0, b)), pl.BlockSpec((SM_ROWS, LANES), lambda b: (0, 0))],
        out_shape=[jax.ShapeDtypeStruct(smt.shape, F32), jax.ShapeDtypeStruct((SM_ROWS, LANES), F32)],
        compiler_params=_cparams(("arbitrary",)),
    )(dc, smt, bias_col)


def _fox_diagonal_mask(tq):
    return lax.broadcasted_iota(jnp.int32, (tq, tq), 1) <= lax.broadcasted_iota(jnp.int32, (tq, tq), 0)


def _fox_fwd(pf, cb, gq2, gk2, go2, tq=256):
    n_batch, s_len, _ = pf.shape
    tq = min(tq, s_len)
    nq = s_len // tq
    scale = FOX_HEAD_DIM ** -0.5

    def body(q_ref, k_ref, v_ref, c_ref, gq_ref, gk_ref, go_ref, o_ref, on_ref, lse_ref, kh_ref, vh_ref):
        j = pl.program_id(1)
        i = pl.program_id(2)
        m0 = lax.broadcasted_iota(jnp.int32, (1, LANES), 1) < FOX_HEAD_DIM

        @pl.when(i == 0)
        def _():
            kn = _rms_pair(k_ref[0], gk_ref[...], m0)
            kh_ref[0] = jnp.where(m0, kn, 0.0).astype(BF16)
            kh_ref[1] = jnp.where(m0, 0.0, kn).astype(BF16)
            v = v_ref[0]
            vh_ref[0] = jnp.where(m0, v, 0.0).astype(BF16)
            vh_ref[1] = jnp.where(m0, 0.0, v).astype(BF16)

        qb = (_rms_pair(q_ref[0], gq_ref[...], m0) * scale).astype(BF16)

        def step(kb, carry, diagonal=False):
            ms, ls, acc = carry
            off = pl.multiple_of(kb * tq, tq)
            new_m, new_l, alphas, pv = [], [], [], []
            for hh in range(2):
                s = _dot(qb, kh_ref[hh, pl.ds(off, tq), :], NT)
                s = s - c_ref[0, kb, pl.ds(2 * j + hh, 1), :]
                if diagonal:
                    s = jnp.where(_fox_diagonal_mask(tq), s, NEG_INF)
                m_new = jnp.maximum(ms[hh], jnp.max(s, axis=-1, keepdims=True))
                alpha = jnp.exp(ms[hh] - m_new)
                p = jnp.exp(s - m_new)
                new_l.append(alpha * ls[hh] + jnp.sum(p, axis=-1, keepdims=True))
                new_m.append(m_new)
                alphas.append(alpha)
                pv.append(_dot(p, vh_ref[hh, pl.ds(off, tq), :], NN))
            acc = jnp.where(m0, alphas[0], alphas[1]) * acc + pv[0] + pv[1]
            return tuple(new_m), tuple(new_l), acc

        init_m = (jnp.full((tq, 1), NEG_INF, F32),) * 2
        init_l = (jnp.zeros((tq, 1), F32),) * 2
        carry = lax.fori_loop(0, i, step, (init_m, init_l, jnp.zeros((tq, LANES), F32)))
        ms, ls, acc = step(i, carry, diagonal=True)
        o = acc / jnp.where(m0, ls[0], ls[1])
        o_ref[0] = o
        on_ref[0] = _rms_pair(o, go_ref[...], m0).astype(BF16)
        lse_ref[0] = jnp.where(m0, ms[0] + jnp.log(ls[0]), ms[1] + jnp.log(ls[1]))

    fixed = lambda b, j, i: (0, 0)
    tile = lambda b, j, i: (b, i, j)
    return pl.pallas_call(
        body, name="fox_fwd", grid=(n_batch, 4, nq),
        in_specs=[pl.BlockSpec((1, tq, LANES), tile), pl.BlockSpec((1, s_len, LANES), lambda b, j, i: (b, 0, 4 + j)),
                  pl.BlockSpec((1, s_len, LANES), lambda b, j, i: (b, 0, 8 + j)),
                  pl.BlockSpec((1, nq, SM_ROWS, tq), lambda b, j, i: (b, 0, 0, 0)),
                  pl.BlockSpec((1, LANES), fixed), pl.BlockSpec((1, LANES), fixed), pl.BlockSpec((1, LANES), fixed)],
        out_specs=[pl.BlockSpec((1, tq, LANES), tile), pl.BlockSpec((1, tq, LANES), tile), pl.BlockSpec((1, tq, LANES), tile)],
        out_shape=[jax.ShapeDtypeStruct((n_batch, s_len, FOX_WIDTH), F32), jax.ShapeDtypeStruct((n_batch, s_len, FOX_WIDTH), BF16),
                   jax.ShapeDtypeStruct((n_batch, s_len, FOX_WIDTH), F32)],
        scratch_shapes=[pltpu.VMEM((2, s_len, LANES), BF16), pltpu.VMEM((2, s_len, LANES), BF16)],
        compiler_params=_cparams(("parallel", "parallel", "arbitrary")),
    )(pf, pf, pf, cb, gq2, gk2, go2)


def _fox_bwd(pf, cb, gq2, gk2, go2, o, lse, don, tq=256):
    n_batch, s_len, _ = pf.shape
    tq = min(tq, s_len)
    nq = s_len // tq
    scale = FOX_HEAD_DIM ** -0.5

    def body(q_ref, k_ref, v_ref, c_ref, gq_ref, gk_ref, go_ref, o_ref, lse_ref, don_ref,
             dq_ref, dk_ref, dv_ref, dc_ref, dgq_ref, dgk_ref, dgo_ref, kh_ref, vh_ref, dka_ref, dva_ref, dca_ref):
        b = pl.program_id(0)
        j = pl.program_id(1)
        i = pl.program_id(2)
        m0 = lax.broadcasted_iota(jnp.int32, (1, LANES), 1) < FOX_HEAD_DIM

        @pl.when((b == 0) & (j == 0) & (i == 0))
        def _():
            dgq_ref[...] = jnp.zeros_like(dgq_ref)
            dgk_ref[...] = jnp.zeros_like(dgk_ref)
            dgo_ref[...] = jnp.zeros_like(dgo_ref)

        @pl.when(i == 0)
        def _():
            kn = _rms_pair(k_ref[0], gk_ref[...], m0)
            kh_ref[0] = jnp.where(m0, kn, 0.0).astype(BF16)
            kh_ref[1] = jnp.where(m0, 0.0, kn).astype(BF16)
            v = v_ref[0]
            vh_ref[0] = jnp.where(m0, v, 0.0).astype(BF16)
            vh_ref[1] = jnp.where(m0, 0.0, v).astype(BF16)
            dka_ref[...] = jnp.zeros_like(dka_ref)
            dva_ref[...] = jnp.zeros_like(dva_ref)
            dca_ref[...] = jnp.zeros_like(dca_ref)

        q = q_ref[0]
        qn = _rms_pair(q, gq_ref[...], m0)
        qs = qn * scale
        qb = qs.astype(BF16)
        qh = (jnp.where(m0, qs, 0.0).astype(BF16), jnp.where(m0, 0.0, qs).astype(BF16))
        ot = o_ref[0]
        do, dgo = _rms_pair_bwd(ot, go_ref[...], don_ref[0], m0)
        dgo_ref[...] += dgo
        dd = do * ot
        delta = (jnp.sum(jnp.where(m0, dd, 0.0), axis=-1, keepdims=True), jnp.sum(jnp.where(m0, 0.0, dd), axis=-1, keepdims=True))
        doh = (jnp.where(m0, do, 0.0).astype(BF16), jnp.where(m0, 0.0, do).astype(BF16))
        lse_t = lse_ref[0]
        lse_h = (lse_t[:, 0:1], lse_t[:, FOX_HEAD_DIM:FOX_HEAD_DIM + 1])

        def step(kb, carry, diagonal=False):
            dqn, rs = carry
            rs = list(rs)
            off = pl.multiple_of(kb * tq, tq)
            for hh in range(2):
                kblk = kh_ref[hh, pl.ds(off, tq), :]
                vblk = vh_ref[hh, pl.ds(off, tq), :]
                s = _dot(qb, kblk, NT)
                s = s - c_ref[0, kb, pl.ds(2 * j + hh, 1), :]
                if diagonal:
                    s = jnp.where(_fox_diagonal_mask(tq), s, NEG_INF)
                p = jnp.exp(s - lse_h[hh])
                dp = _dot(doh[hh], vblk, NT)
                ds = p * (dp - delta[hh])
                dva_ref[pl.ds(off, tq), :] += _dot(p, doh[hh], TN)
                dka_ref[pl.ds(off, tq), :] += _dot(ds, qh[hh], TN)
                dca_ref[kb, hh:hh + 1, :] += -jnp.sum(ds, axis=0, keepdims=True)
                rs[hh] = rs[hh] + jnp.sum(ds, axis=-1, keepdims=True)
                dqn = dqn + _dot(ds, kblk, NN)
            return dqn, tuple(rs)

        carry = lax.fori_loop(0, i, step, (jnp.zeros((tq, LANES), F32), (jnp.zeros((tq, 1), F32),) * 2))
        dqn, rs = step(i, carry, diagonal=True)
        dqn = dqn * scale
        rs_rows = jnp.where(m0, rs[0], rs[1]).T
        dca_ref[i, 0:1, :] += rs_rows[0:1, :]
        dca_ref[i, 1:2, :] += rs_rows[FOX_HEAD_DIM:FOX_HEAD_DIM + 1, :]
        dq, dgq = _rms_pair_bwd(q, gq_ref[...], dqn, m0)
        dq_ref[0] = dq.astype(BF16)
        dgq_ref[...] += dgq

        @pl.when(i == nq - 1)
        def _():
            dk, dgk = _rms_pair_bwd(k_ref[0], gk_ref[...], dka_ref[...], m0)
            dk_ref[0] = dk.astype(BF16)
            dgk_ref[...] += dgk
            dv_ref[0] = dva_ref[...].astype(BF16)
            dc_ref[0, 0] = dca_ref[...]

    fixed = lambda b, j, i: (0, 0)
    tile = lambda b, j, i: (b, i, j)
    full = lambda b, j, i: (b, 0, j)
    wide = jax.ShapeDtypeStruct((n_batch, s_len, FOX_WIDTH), BF16)
    gain = jax.ShapeDtypeStruct((1, LANES), F32)
    return pl.pallas_call(
        body, name="fox_bwd", grid=(n_batch, 4, nq),
        in_specs=[pl.BlockSpec((1, tq, LANES), tile), pl.BlockSpec((1, s_len, LANES), lambda b, j, i: (b, 0, 4 + j)),
                  pl.BlockSpec((1, s_len, LANES), lambda b, j, i: (b, 0, 8 + j)),
                  pl.BlockSpec((1, nq, SM_ROWS, tq), lambda b, j, i: (b, 0, 0, 0)),
                  pl.BlockSpec((1, LANES), fixed), pl.BlockSpec((1, LANES), fixed), pl.BlockSpec((1, LANES), fixed),
                  pl.BlockSpec((1, tq, LANES), tile), pl.BlockSpec((1, tq, LANES), tile), pl.BlockSpec((1, tq, LANES), tile)],
        out_specs=[pl.BlockSpec((1, tq, LANES), tile), pl.BlockSpec((1, s_len, LANES), full), pl.BlockSpec((1, s_len, LANES), full),
                   pl.BlockSpec((1, 1, nq, 8, tq), lambda b, j, i: (b, j, 0, 0, 0)),
                   pl.BlockSpec((1, LANES), fixed), pl.BlockSpec((1, LANES), fixed), pl.BlockSpec((1, LANES), fixed)],
        out_shape=[wide, wide, wide, jax.ShapeDtypeStruct((n_batch, 4, nq, 8, tq), F32), gain, gain, gain],
        scratch_shapes=[pltpu.VMEM((2, s_len, LANES), BF16), pltpu.VMEM((2, s_len, LANES), BF16),
                        pltpu.VMEM((s_len, LANES), F32), pltpu.VMEM((s_len, LANES), F32), pltpu.VMEM((nq, 8, tq), F32)],
        compiler_params=_cparams(("arbitrary", "arbitrary", "arbitrary")),
    )(pf, pf, pf, cb, gq2, gk2, go2, o, lse, don)


def _shift_down(x, k):
    row = lax.broadcasted_iota(jnp.int32, x.shape, 0)
    return jnp.where(row >= k, pltpu.roll(x, k, 0), 0.0)


def _shift_up(x, k):
    n = x.shape[0]
    row = lax.broadcasted_iota(jnp.int32, x.shape, 0)
    return jnp.where(row < n - k, pltpu.roll(x, n - k, 0), 0.0)


def _conv_silu(x, w):
    y = w[3:4] * x + w[2:3] * _shift_down(x, 1) + w[1:2] * _shift_down(x, 2) + w[0:1] * _shift_down(x, 3)
    return y, y * _sigmoid(y)


def _gdn_pre(pg, conv_w):
    n_batch, s_len, width = pg.shape
    ncb = width // LANES

    def body(x_ref, w_ref, o_ref):
        cb = pl.program_id(1)
        _, s = _conv_silu(x_ref[0], w_ref[...])
        sn = s * lax.rsqrt(jnp.sum(s * s, axis=-1, keepdims=True) + EPS)
        o_ref[0] = jnp.where(cb < 2 * GDN_HEADS, sn, s)

    return pl.pallas_call(
        body, name="gdn_pre", grid=(n_batch, ncb),
        in_specs=[pl.BlockSpec((1, s_len, LANES), lambda b, c: (b, 0, c)), pl.BlockSpec((8, LANES), lambda b, c: (0, c))],
        out_specs=pl.BlockSpec((1, s_len, LANES), lambda b, c: (b, 0, c)),
        out_shape=jax.ShapeDtypeStruct(pg.shape, F32),
        compiler_params=_cparams(("parallel", "parallel")),
    )(pg, conv_w)


def _gdn_pre_bwd(pg, conv_w, dout):
    n_batch, s_len, width = pg.shape
    ncb = width // LANES

    def body(x_ref, w_ref, d_ref, dx_ref, dw_ref):
        cb = pl.program_id(0)
        b = pl.program_id(1)
        x = x_ref[0]
        w = w_ref[...]
        d = d_ref[0]
        y, s = _conv_silu(x, w)
        rr = lax.rsqrt(jnp.sum(s * s, axis=-1, keepdims=True) + EPS)
        sn = s * rr
        ds_n = rr * (d - sn * jnp.sum(d * sn, axis=-1, keepdims=True))
        ds = jnp.where(cb < 2 * GDN_HEADS, ds_n, d)
        sig = _sigmoid(y)
        dy = ds * (sig * (1.0 + y * (1.0 - sig)))
        dx = w[3:4] * dy + w[2:3] * _shift_up(dy, 1) + w[1:2] * _shift_up(dy, 2) + w[0:1] * _shift_up(dy, 3)
        dx_ref[0] = dx.astype(BF16)
        dw = [jnp.sum(dy * _shift_down(x, 3 - jj), axis=0, keepdims=True) if jj < 3 else jnp.sum(dy * x, axis=0, keepdims=True)
              for jj in range(CONV_WIDTH)]
        rows = lax.broadcasted_iota(jnp.int32, (8, LANES), 0)
        dwb = jnp.zeros((8, LANES), F32)
        for jj in range(CONV_WIDTH):
            dwb = dwb + jnp.where(rows == jj, dw[jj], 0.0)

        @pl.when(b == 0)
        def _():
            dw_ref[...] = jnp.zeros_like(dw_ref)

        dw_ref[...] += dwb

    blk = lambda c, b: (b, 0, c)
    return pl.pallas_call(
        body, name="gdn_pre_bwd", grid=(ncb, n_batch),
        in_specs=[pl.BlockSpec((1, s_len, LANES), blk), pl.BlockSpec((8, LANES), lambda c, b: (0, c)), pl.BlockSpec((1, s_len, LANES), blk)],
        out_specs=[pl.BlockSpec((1, s_len, LANES), blk), pl.BlockSpec((8, LANES), lambda c, b: (0, c))],
        out_shape=[jax.ShapeDtypeStruct(pg.shape, BF16), jax.ShapeDtypeStruct((8, width), F32)],
        compiler_params=_cparams(("parallel", "arbitrary")),
    )(pg, conv_w, dout)


def _gdn_gates(smc, smr, a_c, dt_c, a_r, dt_r, h):
    lane = lax.broadcasted_iota(jnp.int32, (1, LANES), 1)
    sub = lax.broadcasted_iota(jnp.int32, (SM_ROWS, 1), 0)
    beta_c = jnp.sum(jnp.where(lane == SM_B + h, _sigmoid(smc), 0.0), axis=1, keepdims=True)
    g_all_c = -jnp.exp(a_c) * _softplus(smc + dt_c)
    g_c = jnp.sum(jnp.where(lane == SM_A + h, g_all_c, 0.0), axis=1, keepdims=True)
    g_all_r = -jnp.exp(a_r) * _softplus(smr + dt_r)
    g_r = jnp.sum(jnp.where(sub == SM_A + h, g_all_r, 0.0), axis=0, keepdims=True)
    return beta_c, g_c, g_r


def _gdn_group(qkv, z, smc, smr, a_c, dt_c, a_r, dt_r, go, states):
    n_grp = len(qkv)
    c = qkv[0].shape[0]
    hd = GDN_HEAD_DIM
    pairs = [(g, h) for g in range(n_grp) for h in range(GDN_HEADS)]
    ii = lax.broadcasted_iota(jnp.int32, (c, c), 0)
    jj = lax.broadcasted_iota(jnp.int32, (c, c), 1)
    incl = ii >= jj
    col = lambda arr, base, h: arr[:, base + h * hd:base + (h + 1) * hd]

    qs, ks, kbs, vbs, decays, gcs, g_lasts, amats = [], [], [], [], [], [], [], []
    for g, h in pairs:
        beta_c, g_c, g_r = _gdn_gates(smc[g], smr[g], a_c, dt_c, a_r, dt_r, h)
        gc_c = jnp.sum(jnp.where(incl, g_r, 0.0), axis=1, keepdims=True)
        gc_r = jnp.sum(jnp.where(ii <= jj, g_c, 0.0), axis=0, keepdims=True)
        decay = jnp.where(incl, jnp.exp(jnp.where(incl, gc_c - gc_r, 0.0)), 0.0)
        k = col(qkv[g], GDN_WIDTH, h)
        kb = k * beta_c
        qs.append(col(qkv[g], 0, h) * (hd ** -0.5))
        ks.append(k)
        kbs.append(kb)
        vbs.append(col(qkv[g], 2 * GDN_WIDTH, h) * beta_c)
        decays.append(decay)
        gcs.append(gc_c)
        g_lasts.append(jnp.sum(g_c, axis=0, keepdims=True))
        amats.append(jnp.where(ii > jj, _mm_nt(kb, k) * decay, 0.0))
    ts = _unit_lower_inverses(amats)
    egcs = [jnp.exp(gc) for gc in gcs]
    us = [_mm_nn(t, vb) for t, vb in zip(ts, vbs)]
    ws = [_mm_nn(t, kb * e) for t, kb, e in zip(ts, kbs, egcs)]
    intras = [_mm_nt(q, k) * d for q, k, d in zip(qs, ks, decays)]
    qes = [q * e for q, e in zip(qs, egcs)]
    kds = [k * jnp.exp(gl - gc) for k, gl, gc in zip(ks, g_lasts, gcs)]
    sdecs = [jnp.exp(gl) for gl in g_lasts]

    outs = []
    for g in range(n_grp):
        idx = [g * GDN_HEADS + h for h in range(GDN_HEADS)]
        v_new = [us[i] - _mm_nn(ws[i], states[h]) for h, i in enumerate(idx)]
        o_state = [_mm_nn(qes[i], states[h]) for h, i in enumerate(idx)]
        o_intra = [_mm_nn(intras[i], v_new[h]) for h, i in enumerate(idx)]
        states = [states[h] * sdecs[i] + _mm_tn(kds[i], v_new[h]) for h, i in enumerate(idx)]
        outs.append([_rms(o_state[h] + o_intra[h], go) * (col(z[g], 0, h) * _sigmoid(col(z[g], 0, h))) for h in range(GDN_HEADS)])
    return outs, states


def _gdn_group_size(n_chunks):
    return GDN_GROUP if n_chunks % GDN_GROUP == 0 else 1


def _gdn_fwd(qkvn, z, smc, smr, a_c, dt_c, a_r, dt_r, go):
    n_batch, s_len, _ = qkvn.shape
    c = GDN_CHUNK
    n = s_len // c
    grp = _gdn_group_size(n)
    ng = n // grp
    gc = grp * c
    hd = GDN_HEAD_DIM

    def body(qkv_ref, z_ref, smc_ref, smr_ref, ac_ref, dc_ref, ar_ref, dr_ref, go_ref, og_ref, st_ref, s_ref):
        @pl.when(pl.program_id(1) == 0)
        def _():
            s_ref[...] = jnp.zeros_like(s_ref)

        states = [s_ref[h] for h in range(GDN_HEADS)]
        for h in range(GDN_HEADS):
            st_ref[0, 0, h] = states[h]
        rows = lambda k: slice(k * c, (k + 1) * c)
        outs, nxt = _gdn_group([qkv_ref[0, rows(k), :] for k in range(grp)], [z_ref[0, rows(k), :] for k in range(grp)],
                               [smc_ref[0, rows(k), :] for k in range(grp)], [smr_ref[k] for k in range(grp)],
                               ac_ref[...], dc_ref[...], ar_ref[...], dr_ref[...], go_ref[...], states)
        for k in range(grp):
            for h in range(GDN_HEADS):
                og_ref[0, rows(k), h * hd:(h + 1) * hd] = outs[k][h].astype(BF16)
        for h in range(GDN_HEADS):
            s_ref[h] = nxt[h]

    tok = lambda b, i: (b, i, 0)
    fixed = lambda b, i: (0, 0)
    return pl.pallas_call(
        body, name="gdn_fwd", grid=(n_batch, ng),
        in_specs=[pl.BlockSpec((1, gc, 3 * GDN_WIDTH), tok), pl.BlockSpec((1, gc, GDN_WIDTH), tok), pl.BlockSpec((1, gc, LANES), tok),
                  pl.BlockSpec((grp, SM_ROWS, c), lambda b, i: (b * ng + i, 0, 0)),
                  pl.BlockSpec((1, LANES), fixed), pl.BlockSpec((1, LANES), fixed), pl.BlockSpec((SM_ROWS, 1), fixed),
                  pl.BlockSpec((SM_ROWS, 1), fixed), pl.BlockSpec((1, LANES), fixed)],
        out_specs=[pl.BlockSpec((1, gc, GDN_WIDTH), tok), pl.BlockSpec((1, 1, GDN_HEADS, hd, hd), lambda b, i: (b, i, 0, 0, 0))],
        out_shape=[jax.ShapeDtypeStruct((n_batch, s_len, GDN_WIDTH), BF16), jax.ShapeDtypeStruct((n_batch, ng, GDN_HEADS, hd, hd), F32)],
        scratch_shapes=[pltpu.VMEM((GDN_HEADS, hd, hd), F32)],
        compiler_params=_cparams(("parallel", "arbitrary")),
    )(qkvn, z, smc, smr, a_c, dt_c, a_r, dt_r, go)


def _gdn_bwd(qkvn, z, smc, smr, a_c, dt_c, a_r, dt_r, go, states, dog):
    n_batch, s_len, _ = qkvn.shape
    c = GDN_CHUNK
    n = s_len // c
    grp = _gdn_group_size(n)
    ng = n // grp
    gc = grp * c
    hd = GDN_HEAD_DIM

    def body(qkv_ref, z_ref, smc_ref, smr_ref, ac_ref, dc_ref, ar_ref, dr_ref, go_ref, st_ref, dog_ref,
             dqkv_ref, dz_ref, dsmc_ref, dsmr_ref, dac_ref, ddc_ref, dar_ref, ddr_ref, dgo_ref, ds_ref):
        first = (pl.program_id(0) == 0) & (pl.program_id(1) == 0)

        @pl.when(pl.program_id(1) == 0)
        def _():
            ds_ref[...] = jnp.zeros_like(ds_ref)

        @pl.when(first)
        def _():
            for r in (dac_ref, ddc_ref, dar_ref, ddr_ref, dgo_ref):
                r[...] = jnp.zeros_like(r)

        rows = lambda k: slice(k * c, (k + 1) * c)
        states = [st_ref[0, 0, h] for h in range(GDN_HEADS)]
        prim = ([qkv_ref[0, rows(k), :] for k in range(grp)], [z_ref[0, rows(k), :] for k in range(grp)],
                [smc_ref[0, rows(k), :] for k in range(grp)], [smr_ref[k] for k in range(grp)],
                ac_ref[...], dc_ref[...], ar_ref[...], dr_ref[...], go_ref[...], states)
        _, vjp = jax.vjp(_gdn_group, *prim)
        cot = ([[dog_ref[0, rows(k), h * hd:(h + 1) * hd] for h in range(GDN_HEADS)] for k in range(grp)],
               [ds_ref[h] for h in range(GDN_HEADS)])
        dqkv, dz, dsmc, dsmr, dac, ddc, dar, ddr, dgo, dstates = vjp(cot)
        for k in range(grp):
            dqkv_ref[0, rows(k), :] = dqkv[k]
            dz_ref[0, rows(k), :] = dz[k].astype(BF16)
            dsmc_ref[0, rows(k), :] = dsmc[k]
            dsmr_ref[k] = dsmr[k]
        dac_ref[...] += dac
        ddc_ref[...] += ddc
        dar_ref[...] += dar
        ddr_ref[...] += ddr
        dgo_ref[...] += dgo
        for h in range(GDN_HEADS):
            ds_ref[h] = dstates[h]

    tok = lambda b, i: (b, ng - 1 - i, 0)
    fixed = lambda b, i: (0, 0)
    lane_vec = jax.ShapeDtypeStruct((1, LANES), F32)
    row_vec = jax.ShapeDtypeStruct((SM_ROWS, 1), F32)
    return pl.pallas_call(
        body, name="gdn_bwd", grid=(n_batch, ng),
        in_specs=[pl.BlockSpec((1, gc, 3 * GDN_WIDTH), tok), pl.BlockSpec((1, gc, GDN_WIDTH), tok), pl.BlockSpec((1, gc, LANES), tok),
                  pl.BlockSpec((grp, SM_ROWS, c), lambda b, i: (b * ng + ng - 1 - i, 0, 0)),
                  pl.BlockSpec((1, LANES), fixed), pl.BlockSpec((1, LANES), fixed), pl.BlockSpec((SM_ROWS, 1), fixed),
                  pl.BlockSpec((SM_ROWS, 1), fixed), pl.BlockSpec((1, LANES), fixed),
                  pl.BlockSpec((1, 1, GDN_HEADS, hd, hd), lambda b, i: (b, ng - 1 - i, 0, 0, 0)),
                  pl.BlockSpec((1, gc, GDN_WIDTH), lambda b, i: (b, ng - 1 - i, 1))],
        out_specs=[pl.BlockSpec((1, gc, 3 * GDN_WIDTH), tok), pl.BlockSpec((1, gc, GDN_WIDTH), tok), pl.BlockSpec((1, gc, LANES), tok),
                   pl.BlockSpec((grp, SM_ROWS, c), lambda b, i: (b * ng + ng - 1 - i, 0, 0)),
                   pl.BlockSpec((1, LANES), fixed), pl.BlockSpec((1, LANES), fixed), pl.BlockSpec((SM_ROWS, 1), fixed),
                   pl.BlockSpec((SM_ROWS, 1), fixed), pl.BlockSpec((1, LANES), fixed)],
        out_shape=[jax.ShapeDtypeStruct((n_batch, s_len, 3 * GDN_WIDTH), F32), jax.ShapeDtypeStruct((n_batch, s_len, GDN_WIDTH), BF16),
                   jax.ShapeDtypeStruct((n_batch, s_len, LANES), F32), jax.ShapeDtypeStruct((n_batch * n, SM_ROWS, c), F32),
                   lane_vec, lane_vec, row_vec, row_vec, lane_vec],
        scratch_shapes=[pltpu.VMEM((GDN_HEADS, hd, hd), F32)],
        compiler_params=_cparams(("arbitrary", "arbitrary")),
    )(qkvn, z, smc, smr, a_c, dt_c, a_r, dt_r, go, states, dog)


def _out_proj(x, oa, ob, w_out, g_x, w_cq, tm=256):
    t_len, d = x.shape
    tm = min(tm, t_len)

    def body(x_ref, oa_ref, ob_ref, wo_ref, g_ref, wq_ref, x1_ref, hq_ref, cq_ref):
        x1 = x_ref[...] + _dot(oa_ref[...], wo_ref[0:FOX_WIDTH, :]) + _dot(ob_ref[...], wo_ref[FOX_WIDTH:2 * FOX_WIDTH, :])
        x1_ref[...] = x1
        hq = _rms(x1, g_ref[...]).astype(BF16)
        hq_ref[...] = hq
        cq_ref[...] = _dot(hq, wq_ref[...])

    row = lambda i: (i, 0)
    fixed = lambda i: (0, 0)
    return pl.pallas_call(
        body, name="out_proj", grid=(t_len // tm,),
        in_specs=[pl.BlockSpec((tm, d), row), pl.BlockSpec((tm, FOX_WIDTH), row), pl.BlockSpec((tm, GDN_WIDTH), row),
                  pl.BlockSpec((d, d), fixed), pl.BlockSpec((1, d), fixed), pl.BlockSpec((d, XATTN_WIDTH), fixed)],
        out_specs=[pl.BlockSpec((tm, d), row), pl.BlockSpec((tm, d), row), pl.BlockSpec((tm, XATTN_WIDTH), row)],
        out_shape=[jax.ShapeDtypeStruct((t_len, d), F32), jax.ShapeDtypeStruct((t_len, d), BF16), jax.ShapeDtypeStruct((t_len, XATTN_WIDTH), F32)],
        compiler_params=_cparams(("parallel",)),
    )(x, oa, ob, w_out, g_x, w_cq)


def _out_proj_bwd(dx1, w_out, tm=512):
    t_len, d = dx1.shape
    tm = min(tm, t_len)

    def body(dx_ref, w_ref, o_ref):
        o_ref[...] = _dot(dx_ref[...], w_ref[...], NT)

    return pl.pallas_call(
        body, name="out_proj_bwd", grid=(t_len // tm,),
        in_specs=[pl.BlockSpec((tm, d), lambda i: (i, 0)), pl.BlockSpec((d, d), lambda i: (0, 0))],
        out_specs=pl.BlockSpec((tm, d), lambda i: (i, 0)),
        out_shape=jax.ShapeDtypeStruct((t_len, d), F32),
        compiler_params=_cparams(("parallel",)),
    )(dx1, w_out)


def _mem_kv(mem, g, w_ckv, tm=256):
    t_len, d = mem.shape
    tm = min(tm, t_len)

    def body(x_ref, g_ref, w_ref, h_ref, o_ref):
        h = _rms(x_ref[...], g_ref[...]).astype(BF16)
        h_ref[...] = h
        o_ref[...] = _dot(h, w_ref[...])

    row = lambda i: (i, 0)
    fixed = lambda i: (0, 0)
    return pl.pallas_call(
        body, name="mem_kv", grid=(t_len // tm,),
        in_specs=[pl.BlockSpec((tm, d), row), pl.BlockSpec((1, d), fixed), pl.BlockSpec((d, 2 * XATTN_WIDTH), fixed)],
        out_specs=[pl.BlockSpec((tm, d), row), pl.BlockSpec((tm, 2 * XATTN_WIDTH), row)],
        out_shape=[jax.ShapeDtypeStruct((t_len, d), BF16), jax.ShapeDtypeStruct((t_len, 2 * XATTN_WIDTH), F32)],
        compiler_params=_cparams(("parallel",)),
    )(mem, g, w_ckv)


def _mem_kv_bwd(dckv, mem, g, w_ckv, tm=256):
    t_len, d = mem.shape
    tm = min(tm, t_len)

    def body(d_ref, x_ref, g_ref, w_ref, dg_ref):
        @pl.when(pl.program_id(0) == 0)
        def _():
            dg_ref[...] = jnp.zeros_like(dg_ref)

        dh = _dot(d_ref[...], w_ref[...], NT)
        _, dg = _rms_bwd(x_ref[...], g_ref[...], dh)
        dg_ref[...] += dg

    row = lambda i: (i, 0)
    fixed = lambda i: (0, 0)
    return pl.pallas_call(
        body, name="mem_kv_bwd", grid=(t_len // tm,),
        in_specs=[pl.BlockSpec((tm, 2 * XATTN_WIDTH), row), pl.BlockSpec((tm, d), row), pl.BlockSpec((1, d), fixed),
                  pl.BlockSpec((d, 2 * XATTN_WIDTH), fixed)],
        out_specs=pl.BlockSpec((1, d), fixed),
        out_shape=jax.ShapeDtypeStruct((1, d), F32),
        compiler_params=_cparams(("arbitrary",)),
    )(dckv, mem, g, w_ckv)


def _xattn_probs(qn, kn):
    s = _dot(qn, kn, NT) * (XATTN_HEAD_DIM ** -0.5)
    p = jnp.exp(s - jnp.max(s, axis=-1, keepdims=True))
    return p / jnp.sum(p, axis=-1, keepdims=True)


def _xattn_fwd(cq, ckv, x1, gq, gk, w_co, g_mlp, n_batch, s_len, m_len, tq=512):
    d = x1.shape[1]
    tq = min(tq, s_len)
    nq = s_len // tq
    hd = XATTN_HEAD_DIM

    def body(cq_ref, kv_ref, x1_ref, gq_ref, gk_ref, wo_ref, gm_ref, co_ref, x2_ref, hf_ref):
        outs = []
        for h in range(XATTN_HEADS):
            qn = _rms(cq_ref[:, h * hd:(h + 1) * hd], gq_ref[...])
            kn = _rms(kv_ref[:, h * hd:(h + 1) * hd], gk_ref[...])
            p = _xattn_probs(qn, kn)
            outs.append(_dot(p, kv_ref[:, XATTN_WIDTH + h * hd:XATTN_WIDTH + (h + 1) * hd]).astype(BF16))
        x2 = x1_ref[...]
        for h in range(XATTN_HEADS):
            co_ref[:, h * hd:(h + 1) * hd] = outs[h]
            x2 = x2 + _dot(outs[h], wo_ref[h * hd:(h + 1) * hd, :])
        x2_ref[...] = x2
        hf_ref[...] = _rms(x2, gm_ref[...]).astype(BF16)

    row = lambda b, i: (b * nq + i, 0)
    fixed = lambda b, i: (0, 0)
    t_len = n_batch * s_len
    return pl.pallas_call(
        body, name="xattn_fwd", grid=(n_batch, nq),
        in_specs=[pl.BlockSpec((tq, XATTN_WIDTH), row), pl.BlockSpec((m_len, 2 * XATTN_WIDTH), lambda b, i: (b, 0)),
                  pl.BlockSpec((tq, d), row), pl.BlockSpec((1, hd), fixed), pl.BlockSpec((1, hd), fixed),
                  pl.BlockSpec((XATTN_WIDTH, d), fixed), pl.BlockSpec((1, d), fixed)],
        out_specs=[pl.BlockSpec((tq, XATTN_WIDTH), row), pl.BlockSpec((tq, d), row), pl.BlockSpec((tq, d), row)],
        out_shape=[jax.ShapeDtypeStruct((t_len, XATTN_WIDTH), BF16), jax.ShapeDtypeStruct((t_len, d), F32),
                   jax.ShapeDtypeStruct((t_len, d), BF16)],
        compiler_params=_cparams(("parallel", "parallel")),
    )(cq, ckv, x1, gq, gk, w_co, g_mlp)


def _xattn_bwd(dx2, cq, ckv, x1, gq, gk, w_co, g_x, w_cq, n_batch, s_len, m_len, tq=512):
    d = x1.shape[1]
    tq = min(tq, s_len)
    nq = s_len // tq
    hd = XATTN_HEAD_DIM
    scale = XATTN_HEAD_DIM ** -0.5

    def body(dx2_ref, cq_ref, kv_ref, x1_ref, gq_ref, gk_ref, wo_ref, gx_ref, wq_ref,
             dx1_ref, dcq_ref, dkv_ref, dgq_ref, dgk_ref, dgx_ref, dk_acc, dv_acc):
        b = pl.program_id(0)
        i = pl.program_id(1)

        @pl.when((b == 0) & (i == 0))
        def _():
            dgq_ref[...] = jnp.zeros_like(dgq_ref)
            dgk_ref[...] = jnp.zeros_like(dgk_ref)
            dgx_ref[...] = jnp.zeros_like(dgx_ref)

        @pl.when(i == 0)
        def _():
            dk_acc[...] = jnp.zeros_like(dk_acc)
            dv_acc[...] = jnp.zeros_like(dv_acc)

        dx2 = dx2_ref[...]
        dhq = jnp.zeros((tq, d), F32)
        for h in range(XATTN_HEADS):
            sl = slice(h * hd, (h + 1) * hd)
            q = cq_ref[:, sl]
            qn = _rms(q, gq_ref[...])
            kn = _rms(kv_ref[:, sl], gk_ref[...])
            v = kv_ref[:, XATTN_WIDTH + h * hd:XATTN_WIDTH + (h + 1) * hd]
            p = _xattn_probs(qn, kn)
            dco = _dot(dx2, wo_ref[sl, :], NT)
            dv_acc[:, sl] += _dot(p, dco, TN)
            dp = _dot(dco, v, NT)
            ds = p * (dp - jnp.sum(dp * p, axis=-1, keepdims=True))
            dqn = _dot(ds, kn) * scale
            dk_acc[:, sl] += _dot(ds, qn, TN) * scale
            dq, dgq = _rms_bwd(q, gq_ref[...], dqn)
            dgq_ref[...] += dgq
            dqb = dq.astype(BF16)
            dcq_ref[:, sl] = dqb
            dhq = dhq + _dot(dqb, wq_ref[:, sl], NT)
        dxn, dgx = _rms_bwd(x1_ref[...], gx_ref[...], dhq)
        dgx_ref[...] += dgx
        dx1_ref[...] = dx2 + dxn

        @pl.when(i == nq - 1)
        def _():
            for h in range(XATTN_HEADS):
                sl = slice(h * hd, (h + 1) * hd)
                dk, dgk = _rms_bwd(kv_ref[:, sl], gk_ref[...], dk_acc[:, sl])
                dgk_ref[...] += dgk
                dkv_ref[:, sl] = dk.astype(BF16)
                dkv_ref[:, XATTN_WIDTH + h * hd:XATTN_WIDTH + (h + 1) * hd] = dv_acc[:, sl].astype(BF16)

    row = lambda b, i: (b * nq + i, 0)
    fixed = lambda b, i: (0, 0)
    t_len = n_batch * s_len
    return pl.pallas_call(
        body, name="xattn_bwd", grid=(n_batch, nq),
        in_specs=[pl.BlockSpec((tq, d), row), pl.BlockSpec((tq, XATTN_WIDTH), row), pl.BlockSpec((m_len, 2 * XATTN_WIDTH), lambda b, i: (b, 0)),
                  pl.BlockSpec((tq, d), row), pl.BlockSpec((1, hd), fixed), pl.BlockSpec((1, hd), fixed),
                  pl.BlockSpec((XATTN_WIDTH, d), fixed), pl.BlockSpec((1, d), fixed), pl.BlockSpec((d, XATTN_WIDTH), fixed)],
        out_specs=[pl.BlockSpec((tq, d), row), pl.BlockSpec((tq, XATTN_WIDTH), row), pl.BlockSpec((m_len, 2 * XATTN_WIDTH), lambda b, i: (b, 0)),
                   pl.BlockSpec((1, hd), fixed), pl.BlockSpec((1, hd), fixed), pl.BlockSpec((1, d), fixed)],
        out_shape=[jax.ShapeDtypeStruct((t_len, d), F32), jax.ShapeDtypeStruct((t_len, XATTN_WIDTH), BF16),
                   jax.ShapeDtypeStruct((n_batch * m_len, 2 * XATTN_WIDTH), BF16),
                   jax.ShapeDtypeStruct((1, hd), F32), jax.ShapeDtypeStruct((1, hd), F32), jax.ShapeDtypeStruct((1, d), F32)],
        scratch_shapes=[pltpu.VMEM((m_len, XATTN_WIDTH), F32), pltpu.VMEM((m_len, XATTN_WIDTH), F32)],
        compiler_params=_cparams(("arbitrary", "arbitrary")),
    )(dx2, cq, ckv, x1, gq, gk, w_co, g_x, w_cq)


def _resident(shape):
    return pl.BlockSpec(shape, lambda *_: (0,) * len(shape), pipeline_mode=pl.Buffered(1))


def _mlp_fwd(hf, x2, target, w1, w2, tm=256, tf=1024):
    t_len, d = x2.shape
    f = w1.shape[1]
    tm, tf = min(tm, t_len), min(tf, f)

    def body(hf_ref, x2_ref, tg_ref, w1_ref, w2_ref, u_ref, a_ref, dy_ref, ls_ref):
        hf_t = hf_ref[...]
        y = x2_ref[...]
        for k in range(f // tf):
            cols = slice(k * tf, (k + 1) * tf)
            u = _dot(hf_t, w1_ref[:, cols])
            u_ref[:, cols] = u
            r = jnp.maximum(u, 0.0)
            a = (r * r).astype(BF16)
            a_ref[:, cols] = a
            y = y + _dot(a, w2_ref[cols, :])
        err = y - tg_ref[...]
        dy_ref[...] = err * (1.0 / d)
        ls_ref[...] = jnp.broadcast_to(jnp.sum(jnp.sum(err * err, axis=-1, keepdims=True) * (1.0 / d), axis=0, keepdims=True), ls_ref.shape)

    row = lambda i: (i, 0)
    return pl.pallas_call(
        body, name="mlp_fwd", grid=(t_len // tm,),
        in_specs=[pl.BlockSpec((tm, d), row), pl.BlockSpec((tm, d), row), pl.BlockSpec((tm, d), row), _resident((d, f)), _resident((f, d))],
        out_specs=[pl.BlockSpec((tm, f), row), pl.BlockSpec((tm, f), row), pl.BlockSpec((tm, d), row),
                   pl.BlockSpec((1, 8, LANES), lambda i: (i, 0, 0))],
        out_shape=[jax.ShapeDtypeStruct((t_len, f), F32), jax.ShapeDtypeStruct((t_len, f), BF16), jax.ShapeDtypeStruct((t_len, d), F32),
                   jax.ShapeDtypeStruct((t_len // tm, 8, LANES), F32)],
        compiler_params=_cparams(("parallel",)),
    )(hf, x2, target, w1, w2)


def _mlp_bwd(dy, u, x2, g, w1, w2, tm=256, tf=1024):
    t_len, d = x2.shape
    f = w1.shape[1]
    tm, tf = min(tm, t_len), min(tf, f)

    def body(dy_ref, u_ref, x2_ref, g_ref, w1_ref, w2_ref, du_ref, dx2_ref, dg_ref):
        @pl.when(pl.program_id(0) == 0)
        def _():
            dg_ref[...] = jnp.zeros_like(dg_ref)

        dy_t = dy_ref[...]
        dyb = dy_t.astype(BF16)
        dhf = jnp.zeros((tm, d), F32)
        for k in range(f // tf):
            cols = slice(k * tf, (k + 1) * tf)
            da = _dot(dyb, w2_ref[cols, :], NT)
            du = (da * (2.0 * jnp.maximum(u_ref[:, cols], 0.0))).astype(BF16)
            du_ref[:, cols] = du
            dhf = dhf + _dot(du, w1_ref[:, cols], NT)
        dxn, dg = _rms_bwd(x2_ref[...], g_ref[...], dhf)
        dx2_ref[...] = dy_t + dxn
        dg_ref[...] += dg

    row = lambda i: (i, 0)
    fixed = lambda i: (0, 0)
    return pl.pallas_call(
        body, name="mlp_bwd", grid=(t_len // tm,),
        in_specs=[pl.BlockSpec((tm, d), row), pl.BlockSpec((tm, f), row), pl.BlockSpec((tm, d), row), pl.BlockSpec((1, d), fixed),
                  _resident((d, f)), _resident((f, d))],
        out_specs=[pl.BlockSpec((tm, f), row), pl.BlockSpec((tm, d), row), pl.BlockSpec((1, d), fixed)],
        out_shape=[jax.ShapeDtypeStruct((t_len, f), BF16), jax.ShapeDtypeStruct((t_len, d), F32), jax.ShapeDtypeStruct((1, d), F32)],
        compiler_params=_cparams(("arbitrary",)),
    )(dy, u, x2, g, w1, w2)


def _pad_lanes(v, offset=0, width=LANES):
    return jnp.zeros((1, width), F32).at[:, offset:offset + v.shape[1]].set(v)


def _col(v, offset=0, rows=SM_ROWS):
    return jnp.zeros((rows, 1), F32).at[offset:offset + v.shape[1], 0].set(v[0])


def _pack_small(g_mix, dgq, dgk, dbias, dgo, dac, dar, ddc, ddr, g_gdn_o, g_nx, g_mem, g_xq, g_xk, g_mlp, loss_tiles):
    def body(mix_ref, q_ref, k_ref, b_ref, o_ref, ac_ref, ar_ref, dc_ref, dr_ref, go_ref, nx_ref, mem_ref, xq_ref, xk_ref,
             mlp_ref, lt_ref, out_ref):
        lane = lax.broadcasted_iota(jnp.int32, (1, LANES), 1)
        diag = lax.broadcasted_iota(jnp.int32, (SM_ROWS, LANES), 0) == lax.broadcasted_iota(jnp.int32, (SM_ROWS, LANES), 1)

        def rolled(v, shift):
            return pltpu.roll(jnp.broadcast_to(v, (8, LANES)), shift, 1)[0:1, :]

        def rows_to_lanes(col):
            return jnp.sum(jnp.where(diag, col, 0.0), axis=0, keepdims=True)

        def put(row, v, n):
            out_ref[row:row + 1, 0:LANES] = jnp.where(lane < n, v, 0.0)

        out_ref[...] = jnp.zeros_like(out_ref)
        out_ref[0:1, :] = mix_ref[...]
        for row, ref in ((1, q_ref), (2, k_ref), (4, o_ref)):
            put(row, ref[...] + rolled(ref[...], FOX_HEAD_DIM), FOX_HEAD_DIM)
        put(3, rows_to_lanes(b_ref[...]), FOX_HEADS)
        for row, lane_ref, row_ref in ((5, ac_ref, ar_ref), (6, dc_ref, dr_ref)):
            put(row, rolled(lane_ref[...] + rows_to_lanes(row_ref[...]), LANES - SM_A), GDN_HEADS)
        put(7, go_ref[...], LANES)
        out_ref[8:9, :] = nx_ref[...]
        out_ref[9:10, :] = mem_ref[...]
        put(10, xq_ref[...], LANES)
        put(11, xk_ref[...], LANES)
        out_ref[12:13, :] = mlp_ref[...]
        put(LOSS_ROW, 0.5 * jnp.sum(lt_ref[...], axis=0)[0:1, :], 1)

    args = (g_mix, dgq, dgk, dbias, dgo, dac, dar, ddc, ddr, g_gdn_o, g_nx, g_mem, g_xq, g_xk, g_mlp, loss_tiles)
    return pl.pallas_call(body, name="pack_small", out_shape=jax.ShapeDtypeStruct((PACK_ROWS, D_MODEL), F32))(*args)


LATE_WEIGHTS = (("w_out", "w_cq", "w_ckv", "w_co"), ("w_mlp1", "w_mlp2"))
GRAD_GROUPS = (("w_mlp2", "w_mlp1"), ("w_co", "w_cq", "w_ckv", "w_out"), ("w_in", "gdn_conv_w"))


def _local_step(x, mem, target, norm_mix_g, w_in, fox_qnorm_g, fox_knorm_g, fox_f_bias, fox_onorm_g, gdn_conv_w, gdn_A_log,
                gdn_dt_bias, gdn_onorm_g, norm_xattn_g, mem_norm_g, xattn_qnorm_g, xattn_knorm_g, norm_mlp_g,
                late_weights, grads_ready=None, first_token=0.0):
    if grads_ready is None:
        grads_ready = lambda group: 0.0
    n_batch, s_len, d = x.shape
    m_len = mem.shape[1]
    t_len = n_batch * s_len
    tq = min(FOX_BLOCK, s_len)
    nq = s_len // tq
    n_chunks = s_len // GDN_CHUNK
    x2d = x.reshape(t_len, d)

    wp = jnp.concatenate([w_in[0:1536], w_in[1544:3080], w_in[3088:3600], w_in[1536:1544], w_in[3080:3088],
                          jnp.zeros((P_DIM - 3600, d), BF16)], axis=0)
    wst = jnp.concatenate([w_in[1536:1544], w_in[3080:3088]], axis=0)
    conv_w = jnp.concatenate([gdn_conv_w, jnp.zeros((8 - CONV_WIDTH, gdn_conv_w.shape[1]), F32)], axis=0)
    bias_col = _col(fox_f_bias, SM_F)
    gq2, gk2, go2 = (jnp.tile(g, (1, 2)) for g in (fox_qnorm_g, fox_knorm_g, fox_onorm_g))
    a_c, dt_c = _pad_lanes(gdn_A_log, SM_A), _pad_lanes(gdn_dt_bias, SM_A)
    a_r, dt_r = _col(gdn_A_log, SM_A), _col(gdn_dt_bias, SM_A)

    h1, pfox, pgdn, pz, sm, smt = _in_proj(x2d, norm_mix_g + first_token, wp, wst)
    c_rows = _fox_cum(smt, bias_col, n_batch, s_len)
    cb = c_rows.reshape(SM_ROWS, n_batch, nq, tq).transpose(1, 2, 0, 3)
    pf3 = pfox.reshape(n_batch, s_len, 1536)
    o_fox, oa, lse = _fox_fwd(pf3, cb, gq2, gk2, go2, tq)
    pg3 = pgdn.reshape(n_batch, s_len, 1536)
    qkvn = _gdn_pre(pg3, conv_w)
    z3 = pz.reshape(n_batch, s_len, GDN_WIDTH)
    smc = sm.reshape(n_batch, s_len, LANES)
    smr = smt.reshape(SM_ROWS, n_batch * n_chunks, GDN_CHUNK).transpose(1, 0, 2)
    ob, states = _gdn_fwd(qkvn, z3, smc, smr, a_c, dt_c, a_r, dt_r, gdn_onorm_g)
    oa2, ob2 = oa.reshape(t_len, FOX_WIDTH), ob.reshape(t_len, GDN_WIDTH)
    w_out, w_cq, w_ckv, w_co = late_weights(LATE_WEIGHTS[0], ob2)
    x1, hq, cq = _out_proj(x2d, oa2, ob2, w_out, norm_xattn_g, w_cq)
    mem2d = mem.reshape(n_batch * m_len, d)
    hm, ckv = _mem_kv(mem2d, mem_norm_g, w_ckv)
    co, x2, hf = _xattn_fwd(cq, ckv, x1, xattn_qnorm_g, xattn_knorm_g, w_co, norm_mlp_g, n_batch, s_len, m_len)
    w_mlp1, w_mlp2 = late_weights(LATE_WEIGHTS[1], hf)
    u, a_act, dy, loss_tiles = _mlp_fwd(hf, x2, target.reshape(t_len, d), w_mlp1, w_mlp2)

    grads = {}
    du, dx2, grads["norm_mlp_g"] = _mlp_bwd(dy, u, x2, norm_mlp_g, w_mlp1, w_mlp2)
    grads["w_mlp2"] = _wgrad(a_act, dy, "wgrad_mlp2")
    grads["w_mlp1"] = _wgrad(hf, du, "wgrad_mlp1", column_blocks=D_FF // N_DEV)
    token = grads_ready({k: grads[k] for k in GRAD_GROUPS[0]})
    grads["w_co"] = _wgrad(co, dx2, "wgrad_co", column_blocks=D_MODEL // N_DEV)
    dx1, dcq, dckv, grads["xattn_qnorm_g"], grads["xattn_knorm_g"], grads["norm_xattn_g"] = _xattn_bwd(
        dx2, cq, ckv, x1, xattn_qnorm_g + token, xattn_knorm_g, w_co, norm_xattn_g, w_cq, n_batch, s_len, m_len)
    grads["w_cq"] = _wgrad(hq, dcq, "wgrad_cq")
    grads["w_ckv"] = _wgrad(hm, dckv, "wgrad_ckv")
    grads["mem_norm_g"] = _mem_kv_bwd(dckv, mem2d, mem_norm_g, w_ckv)
    grads["w_out"] = _wgrad_stacked([oa2, ob2], dx1, "wgrad_out", bn=1024)
    token = grads_ready({k: grads[k] for k in GRAD_GROUPS[1]})
    dcat = _out_proj_bwd(dx1, w_out)
    dcat3 = dcat.reshape(n_batch, s_len, d)

    dqkvn, dz, dsmc, dsmr, dac, ddc, dar, ddr, grads["gdn_onorm_g"] = _gdn_bwd(
        qkvn, z3, smc, smr, a_c, dt_c, a_r, dt_r, gdn_onorm_g + token, states, dcat3)
    dpg, dconv = _gdn_pre_bwd(pg3, conv_w, dqkvn)
    grads["gdn_conv_w"] = dconv[0:CONV_WIDTH]

    dq, dk, dv, dcb, dgq, dgk, dgo = _fox_bwd(pf3, cb, gq2, gk2, go2, o_fox, lse, dcat3, tq)
    dc8 = dcb[:, :, :, 0:2, :].transpose(1, 3, 0, 2, 4).reshape(FOX_HEADS, t_len)
    dc_rows = jnp.concatenate([dc8, jnp.zeros((SM_ROWS - FOX_HEADS, t_len), F32)], axis=0)
    dl_rows, dbias = _fox_cum_bwd(dc_rows, smt, bias_col, n_batch, s_len)
    dsm_rows = jnp.concatenate([dl_rows[0:SM_B], dsmr.transpose(1, 0, 2).reshape(SM_ROWS, t_len)[SM_B:SM_ROWS]], axis=0)

    dprojs = [dq.reshape(t_len, FOX_WIDTH), dk.reshape(t_len, FOX_WIDTH), dv.reshape(t_len, FOX_WIDTH),
              dpg.reshape(t_len, 1536), dz.reshape(t_len, GDN_WIDTH), dsmc.reshape(t_len, LANES)]
    dwp = _wgrad_stacked(dprojs, h1, "wgrad_in")
    dwst = _rows_matmul(dsm_rows, h1, "wgrad_in_rows")
    dw_small = dwp[P_SMALL:P_SMALL + SM_ROWS] + dwst
    grads["w_in"] = jnp.concatenate([dwp[0:1536], dw_small[0:8], dwp[1536:3072], dw_small[8:16], dwp[3072:3584]], axis=0)
    token = grads_ready({k: grads[k] for k in GRAD_GROUPS[2]})
    grad_x, grads["norm_mix_g"] = _in_proj_bwd(dprojs, dsm_rows, x2d, norm_mix_g + token, wp, wst, dx1)
    packed = _pack_small(grads["norm_mix_g"], dgq, dgk, dbias, dgo, dac, dar, ddc, ddr, grads["gdn_onorm_g"], grads["norm_xattn_g"],
                         grads["mem_norm_g"], grads["xattn_qnorm_g"], grads["xattn_knorm_g"], grads["norm_mlp_g"], loss_tiles)
    return packed, grad_x.reshape(n_batch, s_len, d), {k: grads[k] for k in SHARDED}


MESH_ID = pl.DeviceIdType.MESH
ANY_SPEC = pl.BlockSpec(memory_space=pl.ANY)


def _place():
    x, y, c = lax.axis_index("x"), lax.axis_index("y"), lax.axis_index("c")
    return x, y, c, [(1 - x, y), (x, 1 - y), (1 - x, 1 - y)]


def _place_own(src_ref, dst_ref):
    def staged(buf, sem):
        for a, b in ((src_ref, buf), (buf, dst_ref)):
            cp = pltpu.make_async_copy(a, b, sem)
            cp.start()
            cp.wait()

    pl.run_scoped(staged, pltpu.VMEM(src_ref.shape, src_ref.dtype), pltpu.SemaphoreType.DMA)


def _all_gather_body(n, ins, outs, send_sems, recv_sems):
    x, y, c, chips = _place()
    me, sibling = (x, y, c), (x, y, 1 - c)

    def copy(a, k, block, to, src=None):
        dst = outs[a].at[4 * block[0] + 2 * block[1] + block[2]]
        return pltpu.make_async_remote_copy(src_ref=dst if src is None else src, dst_ref=dst, send_sem=send_sems.at[a, k],
                                            recv_sem=recv_sems.at[a, k], device_id=to, device_id_type=MESH_ID)

    first = []
    for a in range(n):
        first.append(copy(a, 0, me, sibling, src=ins[a]))
        first += [copy(a, 1 + j, me, (*chip, c), src=ins[a]) for j, chip in enumerate(chips)]
    for cp in first:
        cp.start()
    for a in range(n):
        _place_own(ins[a], outs[a].at[4 * x + 2 * y + c])
    passed = []
    for j, chip in enumerate(chips):
        for a in range(n):
            copy(a, 1 + j, (*chip, c), me).wait_recv()
            fwd = copy(a, 4 + j, (*chip, c), sibling)
            fwd.start()
            passed.append(fwd)
    for a in range(n):
        copy(a, 0, sibling, me).wait_recv()
        for j, chip in enumerate(chips):
            copy(a, 4 + j, (*chip, 1 - c), me).wait_recv()
    for cp in first + passed:
        cp.wait_send()


def _all_gather_hbm(arrs, name):
    n = len(arrs)

    def body(*refs):
        _all_gather_body(n, refs[:n], refs[n:2 * n], refs[2 * n], refs[2 * n + 1])

    return pl.pallas_call(
        body, name=name, in_specs=[ANY_SPEC] * n, out_specs=[ANY_SPEC] * n,
        out_shape=[jax.ShapeDtypeStruct((N_DEV,) + a.shape, a.dtype) for a in arrs],
        scratch_shapes=[pltpu.SemaphoreType.DMA((n, 7)), pltpu.SemaphoreType.DMA((n, 7))],
        compiler_params=pltpu.CompilerParams(vmem_limit_bytes=VMEM_LIMIT),
    )(*arrs)


def _pair_exchange(arrs, name):
    n = len(arrs)

    def body(*refs):
        ins, outs = refs[:n], refs[n:2 * n]
        send_sems, recv_sems = refs[2 * n:]
        x, y, c, _ = _place()
        copies = []
        for a in range(n):
            for chip in range(4):
                copies.append(pltpu.make_async_remote_copy(
                    src_ref=ins[a].at[2 * chip + (1 - c)], dst_ref=outs[a].at[chip], send_sem=send_sems.at[a, chip],
                    recv_sem=recv_sems.at[a, chip], device_id=(x, y, 1 - c), device_id_type=MESH_ID))
        for cp in copies:
            cp.start()
        for cp in copies:
            cp.wait()

    return pl.pallas_call(
        body, name=name, in_specs=[ANY_SPEC] * n, out_specs=[ANY_SPEC] * n,
        out_shape=[jax.ShapeDtypeStruct((4,) + a.shape[1:], a.dtype) for a in arrs],
        scratch_shapes=[pltpu.SemaphoreType.DMA((n, 4)), pltpu.SemaphoreType.DMA((n, 4))],
    )(*arrs)


HBM_SPEC = pl.BlockSpec(memory_space=pltpu.HBM)
SEM_SPEC = pl.BlockSpec(memory_space=pltpu.SEMAPHORE)
DATAFLOW = pltpu.SideEffectType.DATAFLOW_SIDE_EFFECTING


def _in_hbm(arrs):
    return [pltpu.with_memory_space_constraint(a, pltpu.HBM) for a in arrs]


def _copies_start(name, srcs, lands, make_copies, after):
    n = len(srcs)
    n_copies = len(make_copies(srcs, lands, None, None)[0])

    def body(*refs):
        send_sems, recv_sems = refs[2 * n + 1], refs[2 * n + 2]
        for row in make_copies(refs[:n], refs[n:2 * n], send_sems, recv_sems):
            for cp in row:
                cp.start()
        refs[-1][...] = jnp.zeros_like(refs[-1])

    sems = pltpu.SemaphoreType.DMA((n * n_copies,))
    thru = [pltpu.HBM(a.shape, a.dtype) for a in list(srcs) + list(lands)]
    res = pl.pallas_call(
        body, name=name, in_specs=[HBM_SPEC] * (2 * n) + [ANY_SPEC],
        out_specs=(SEM_SPEC, SEM_SPEC, *[HBM_SPEC] * (2 * n), pl.BlockSpec(memory_space=pltpu.VMEM)),
        out_shape=(sems, sems, *thru, jax.ShapeDtypeStruct((8, LANES), F32)),
        input_output_aliases={i: 2 + i for i in range(2 * n)},
        compiler_params=pltpu.CompilerParams(has_side_effects=DATAFLOW),
    )(*_in_hbm(list(srcs) + list(lands)), after)
    return res[0], res[1], list(res[2:2 + n]), list(res[2 + n:2 + 2 * n]), res[-1]


def _copies_wait(name, send_sems, recv_sems, srcs, lands, after, make_copies, own_block=False):
    n = len(srcs)

    def body(*refs):
        if own_block:
            for a in range(n):
                _place_own(refs[a], _own_part(refs[a], refs[3 * n + 3 + a]))
        for row in make_copies(refs[:n], refs[n:2 * n], refs[2 * n], refs[2 * n + 1]):
            for cp in row:
                cp.wait_send()
                cp.wait_recv()

    res = pl.pallas_call(
        body, name=name, in_specs=[HBM_SPEC] * (2 * n) + [SEM_SPEC, SEM_SPEC, ANY_SPEC],
        out_specs=tuple([HBM_SPEC] * (2 * n)),
        out_shape=tuple(pltpu.HBM(a.shape, a.dtype) for a in list(srcs) + list(lands)),
        input_output_aliases={i: i for i in range(2 * n)},
        compiler_params=pltpu.CompilerParams(has_side_effects=DATAFLOW, vmem_limit_bytes=VMEM_LIMIT),
    )(*srcs, *lands, send_sems, recv_sems, after)
    return list(res[:n]), list(res[n:])


def _own_part(src_ref, land_ref):
    me = 4 * lax.axis_index("x") + 2 * lax.axis_index("y") + lax.axis_index("c")
    rows, cols = src_ref.shape
    if land_ref.shape[0] == N_DEV * rows:
        return land_ref.at[pl.ds(pl.multiple_of(me * rows, rows), rows), :]
    return land_ref.at[:, pl.ds(pl.multiple_of(me * cols, cols), cols)]


def _gather_copies(srcs, lands, send_sems, recv_sems):
    if send_sems is None:
        return [[None] * 7]
    x, y, c, _ = _place()
    rows = []
    for a in range(len(srcs)):
        row = []
        for k in range(7):
            r = k + 1
            to = (1 - x if r & 4 else x, 1 - y if r & 2 else y, 1 - c if r & 1 else c)
            row.append(pltpu.make_async_remote_copy(
                src_ref=srcs[a], dst_ref=_own_part(srcs[a], lands[a]), send_sem=send_sems.at[7 * a + k], recv_sem=recv_sems.at[7 * a + k],
                device_id=to, device_id_type=MESH_ID))
        rows.append(row)
    return rows


def _scatter_copies(srcs, lands, send_sems, recv_sems):
    if send_sems is None:
        return [[None] * 7]
    x, y, c, _ = _place()
    rows = []
    for a in range(len(srcs)):
        row = []
        for k in range(7):
            r = k + 1
            to = (1 - x if r & 4 else x, 1 - y if r & 2 else y, 1 - c if r & 1 else c)
            row.append(pltpu.make_async_remote_copy(
                src_ref=srcs[a].at[4 * to[0] + 2 * to[1] + to[2]], dst_ref=lands[a].at[k], send_sem=send_sems.at[7 * a + k],
                recv_sem=recv_sems.at[7 * a + k], device_id=to, device_id_type=MESH_ID))
        rows.append(row)
    return rows


def _chip_copies(srcs, lands, send_sems, recv_sems):
    if send_sems is None:
        return [[None] * 3]
    x, y, c, chips = _place()
    return [[pltpu.make_async_remote_copy(
        src_ref=srcs[a].at[2 * chip[0] + chip[1]], dst_ref=lands[a].at[j], send_sem=send_sems.at[3 * a + j], recv_sem=recv_sems.at[3 * a + j],
        device_id=(*chip, c), device_id_type=MESH_ID) for j, chip in enumerate(chips)] for a in range(len(srcs))]


def _tile(rows, cols):
    if rows <= 256:
        return rows, cols
    tr = 256 if cols <= 512 else 128
    if rows % tr == 0:
        return tr, cols
    return rows, 256


def _pair_sum(core, own, got, name):
    _, rows, cols = own.shape
    tr, tc = _tile(rows, cols)

    def body(c_ref, own_ref, got_ref, o_ref):
        o_ref[0] = own_ref[0] + got_ref[0]

    return pl.pallas_call(
        body, name=name,
        grid_spec=pltpu.PrefetchScalarGridSpec(
            num_scalar_prefetch=1, grid=(4, rows // tr, cols // tc),
            in_specs=[pl.BlockSpec((1, tr, tc), lambda k, i, j, c: (2 * k + c[0], i, j)),
                      pl.BlockSpec((1, tr, tc), lambda k, i, j, c: (k, i, j))],
            out_specs=pl.BlockSpec((1, tr, tc), lambda k, i, j, c: (k, i, j))),
        out_shape=jax.ShapeDtypeStruct((4, rows, cols), F32),
        compiler_params=_cparams(("parallel", "parallel", "parallel")),
    )(core, own, got)


def _adamw(w, g, m, v):
    m_new = ADAM_B1 * m + (1.0 - ADAM_B1) * g
    v_new = ADAM_B2 * v + (1.0 - ADAM_B2) * (g * g)
    m_hat = m_new / (1.0 - ADAM_B1 ** ADAM_STEP)
    v_hat = v_new / (1.0 - ADAM_B2 ** ADAM_STEP)
    delta = -ADAM_LR * (m_hat / (jnp.sqrt(v_hat) + ADAM_EPS) + ADAM_WD * w)
    return delta, m_new, v_new


def _sum_adam(chip, sums, parts, w, m, v, name):
    n_parts, rows, cols = parts.shape
    tr, tc = _tile(rows, cols)

    def body(chip_ref, own_ref, p_ref, w_ref, m_ref, v_ref, g_ref, d_ref, mo_ref, vo_ref):
        g = own_ref[0]
        for k in range(n_parts):
            g = g + p_ref[k]
        g_ref[...] = g
        d_ref[...], mo_ref[...], vo_ref[...] = _adamw(w_ref[...], g, m_ref[...], v_ref[...])

    tile = pl.BlockSpec((tr, tc), lambda i, j, ch: (i, j))
    out = jax.ShapeDtypeStruct((rows, cols), F32)
    return pl.pallas_call(
        body, name=name,
        grid_spec=pltpu.PrefetchScalarGridSpec(
            num_scalar_prefetch=1, grid=(rows // tr, cols // tc),
            in_specs=[pl.BlockSpec((1, tr, tc), lambda i, j, ch: (ch[0], i, j)),
                      pl.BlockSpec((n_parts, tr, tc), lambda i, j, ch: (0, i, j)), tile, tile, tile],
            out_specs=[tile, tile, tile, tile]),
        out_shape=[out, out, out, out],
        compiler_params=_cparams(("parallel", "parallel")),
    )(chip, sums, parts, w, m, v)


SHARDED = ("w_in", "gdn_conv_w", "w_out", "w_cq", "w_ckv", "w_co", "w_mlp1", "w_mlp2")
TRANSPOSED = ("w_in",)
COLUMN_SHARDED = ("gdn_conv_w", "w_co", "w_mlp1")
REPLICATED = ("norm_mix_g", "fox_qnorm_g", "fox_knorm_g", "fox_f_bias", "fox_onorm_g", "gdn_A_log", "gdn_dt_bias", "gdn_onorm_g",
              "norm_xattn_g", "mem_norm_g", "xattn_qnorm_g", "xattn_knorm_g", "norm_mlp_g")
WEIGHTS = ("norm_mix_g", "w_in", "fox_qnorm_g", "fox_knorm_g", "fox_f_bias", "fox_onorm_g", "gdn_conv_w", "gdn_A_log", "gdn_dt_bias",
           "gdn_onorm_g", "w_out", "norm_xattn_g", "mem_norm_g", "w_cq", "w_ckv", "xattn_qnorm_g", "xattn_knorm_g", "w_co",
           "norm_mlp_g", "w_mlp1", "w_mlp2")
PACK_ROWS = 16
LOSS_ROW = len(REPLICATED)


def _whole(name, gathered):
    if name in COLUMN_SHARDED:
        return gathered.transpose(1, 0, 2).reshape(gathered.shape[1], N_DEV * gathered.shape[2])
    return gathered.reshape(N_DEV * gathered.shape[1], gathered.shape[2])


def _whole_shape(name, shard_shape):
    rows, cols = shard_shape
    return (rows, N_DEV * cols) if name in COLUMN_SHARDED else (N_DEV * rows, cols)


def _blocks(name, whole):
    if whole.ndim == 3:
        return whole
    if name in COLUMN_SHARDED:
        rows, cols = whole.shape
        return whole.reshape(rows, N_DEV, cols // N_DEV).transpose(1, 0, 2)
    return whole.reshape(N_DEV, whole.shape[0] // N_DEV, whole.shape[1])


def _adam_small(everyone, ws, ms, vs):
    n_par = len(ws)

    def body(*refs):
        ev_ref = refs[0]
        w_refs, m_refs, v_refs = (refs[1 + j * n_par:1 + (j + 1) * n_par] for j in range(3))
        outs = refs[1 + 3 * n_par:-1]
        sum_ref = refs[-1]
        total = ev_ref[0]
        for dev in range(1, N_DEV):
            total = total + ev_ref[dev]
        sum_ref[...] = total
        for i in range(n_par):
            n = w_refs[i].shape[1]
            g = sum_ref[i:i + 1, 0:n]
            outs[4 * i][...] = g
            outs[4 * i + 1][...], outs[4 * i + 2][...], outs[4 * i + 3][...] = _adamw(w_refs[i][...], g, m_refs[i][...], v_refs[i][...])
        outs[4 * n_par][...] = sum_ref[LOSS_ROW:LOSS_ROW + 1, 0:1]

    shapes = [jax.ShapeDtypeStruct(a.shape, F32) for a in ws for _ in range(4)] + [jax.ShapeDtypeStruct((1, 1), F32)]
    return pl.pallas_call(body, name="adam_small", out_shape=shapes,
                          scratch_shapes=[pltpu.VMEM((PACK_ROWS, D_MODEL), F32)])(everyone, *ws, *ms, *vs)


def kernel(x, mem, norm_mix_g, w_in, fox_qnorm_g, fox_knorm_g, fox_f_bias, fox_onorm_g, gdn_conv_w, gdn_A_log, gdn_dt_bias, gdn_onorm_g, w_out, norm_xattn_g, mem_norm_g, w_cq, w_ckv, xattn_qnorm_g, xattn_knorm_g, w_co, norm_mlp_g, w_mlp1, w_mlp2, loss_target, m_norm_mix_g, m_w_in, m_fox_qnorm_g, m_fox_knorm_g, m_fox_f_bias, m_fox_onorm_g, m_gdn_conv_w, m_gdn_A_log, m_gdn_dt_bias, m_gdn_onorm_g, m_w_out, m_norm_xattn_g, m_mem_norm_g, m_w_cq, m_w_ckv, m_xattn_qnorm_g, m_xattn_knorm_g, m_w_co, m_norm_mlp_g, m_w_mlp1, m_w_mlp2, v_norm_mix_g, v_w_in, v_fox_qnorm_g, v_fox_knorm_g, v_fox_f_bias, v_fox_onorm_g, v_gdn_conv_w, v_gdn_A_log, v_gdn_dt_bias, v_gdn_onorm_g, v_w_out, v_norm_xattn_g, v_mem_norm_g, v_w_cq, v_w_ckv, v_xattn_qnorm_g, v_xattn_knorm_g, v_w_co, v_norm_mlp_g, v_w_mlp1, v_w_mlp2):
    given = dict(locals())
    w = {k: given[k] for k in WEIGHTS}
    m = {k: given["m_" + k] for k in WEIGHTS}
    v = {k: given["v_" + k] for k in WEIGHTS}

    core = lax.axis_index("c").astype(jnp.int32).reshape(1)
    chip = (2 * lax.axis_index("x") + lax.axis_index("y")).astype(jnp.int32).reshape(1)
    me = 4 * lax.axis_index("x") + 2 * lax.axis_index("y") + lax.axis_index("c")

    local = lambda d: {k: jnp.transpose(d[k][0]) if k in TRANSPOSED else d[k][0] for k in SHARDED}
    w2, m2, v2 = local(w), local(m), local(v)
    shards = {k: w2[k] if k == "gdn_conv_w" else w2[k].astype(BF16) for k in SHARDED}
    early = [k for k in SHARDED if not any(k in group for group in LATE_WEIGHTS)]
    gathered = _all_gather_hbm([shards[k] for k in early], "gather_early")
    whole = {k: _whole(k, g) for k, g in zip(early, gathered)}
    gathers, after = {}, gathered[0]
    for i, group in enumerate(LATE_WEIGHTS):
        lands = [lax.empty(_whole_shape(k, shards[k].shape), BF16) for k in group]
        gathers[group] = _copies_start("gather_late_start_" + str(i), [shards[k] for k in group], lands, _gather_copies, after=after)
        after = gathers[group][4]
    first_token = after[0, 0]

    def late_weights(group, after):
        gather = gathers[group]
        _, lands = _copies_wait("gather_late_wait_" + str(LATE_WEIGHTS.index(group)), gather[0], gather[1], gather[2], gather[3],
                                after, _gather_copies, own_block=True)
        return lands

    pending = []

    def grads_ready(group):
        names = list(group)
        tag = str(len(pending))
        own = [_blocks(k, group[k]) for k in names]
        if "w_in" in names:
            got = _pair_exchange(own, "grad_pair_exchange_" + tag)
            srcs = [_pair_sum(core, o, g, "grad_pair_sum_" + k) for k, o, g in zip(names, own, got)]
            copies, index, n_parts = _chip_copies, chip, 3
        else:
            srcs, copies, index, n_parts = own, _scatter_copies, me.astype(jnp.int32).reshape(1), 7
        lands = [lax.empty((n_parts,) + s.shape[1:], s.dtype) for s in srcs]
        started = _copies_start("grad_exchange_start_" + tag, srcs, lands, copies, after=core)
        pending.append((names, started, copies, index))
        return started[4][0, 0]

    small = {k: w[k] for k in REPLICATED}
    packed, grad_x, _ = _local_step(x, mem, loss_target, **small, **whole, late_weights=late_weights,
                                    grads_ready=grads_ready, first_token=first_token)

    small_lands = [lax.empty((N_DEV * PACK_ROWS, D_MODEL), F32)]
    small_gather = _copies_start("gather_small_start", [packed], small_lands, _gather_copies, after=grad_x)

    out_g, out_d, out_m, out_v = {}, {}, {}, {}
    after = small_gather[4]
    for tag, (names, started, copies, index) in enumerate(pending):
        srcs, parts = _copies_wait("grad_exchange_wait_" + str(tag), started[0], started[1], started[2], started[3], after, copies)
        for k, s, p in zip(names, srcs, parts):
            res = _sum_adam(index, s, p, w2[k], m2[k], v2[k], "adam_" + k)
            out_g[k], out_d[k], out_m[k], out_v[k] = ((jnp.transpose(r) if k in TRANSPOSED else r)[None] for r in res)
            after = res[0]

    _, (everyone,) = _copies_wait("gather_small_wait", small_gather[0], small_gather[1], small_gather[2], small_gather[3], after,
                                  _gather_copies, own_block=True)
    res = _adam_small(everyone.reshape(N_DEV, PACK_ROWS, D_MODEL), [w[k] for k in REPLICATED], [m[k] for k in REPLICATED],
                      [v[k] for k in REPLICATED])
    for i, k in enumerate(REPLICATED):
        out_g[k], out_d[k], out_m[k], out_v[k] = res[4 * i:4 * i + 4]
    loss = res[-1].reshape(())

    return (loss, grad_x, *[out_g[k] for k in WEIGHTS], *[out_d[k] for k in WEIGHTS], *[out_m[k] for k in WEIGHTS],
            *[out_v[k] for k in WEIGHTS])
```

```python
import functools

import jax
import jax.numpy as jnp
import numpy as np
from jax import lax
from jax.experimental import pallas as pl
from jax.experimental.pallas import tpu as pltpu

F32 = jnp.float32
BF16 = jnp.bfloat16

D_MODEL = 1024
FOX_HEADS = 8
FOX_HEAD_DIM = 64
FOX_WIDTH = 512
GDN_HEADS = 4
GDN_HEAD_DIM = 128
GDN_WIDTH = 512
CONV_WIDTH = 4
GDN_CHUNK = 128
GDN_GROUP = 4
FOX_BLOCK = 512
XATTN_HEADS = 4
XATTN_HEAD_DIM = 128
XATTN_WIDTH = 512
D_FF = 4096
EPS = 1e-6
NEG_INF = -1e30
N_DEV = 8

ADAM_LR = 0.001
ADAM_B1 = 0.9
ADAM_B2 = 0.999
ADAM_EPS = 1e-08
ADAM_WD = 0.01
ADAM_STEP = 10

P_FOX = 0
P_GDN = 1536
P_Z = 3072
P_SMALL = 3584
P_DIM = 3712
SM_F = 0
SM_B = 8
SM_A = 12
SM_ROWS = 16

LANES = 128
VMEM_LIMIT = 56 * 1024 * 1024

NN = (((1,), (0,)), ((), ()))
NT = (((1,), (1,)), ((), ()))
TN = (((0,), (0,)), ((), ()))


def _dot(a, b, dims=NN):
    return lax.dot_general(a.astype(BF16), b.astype(BF16), dims, preferred_element_type=F32)


def _cparams(sem=None):
    kw = dict(vmem_limit_bytes=VMEM_LIMIT)
    if sem is not None:
        kw["dimension_semantics"] = sem
    return pltpu.CompilerParams(**kw)


def _sigmoid(x):
    return 0.5 * (jnp.tanh(0.5 * x) + 1.0)


def _softplus(x):
    return jnp.maximum(x, 0.0) + jnp.log1p(jnp.exp(-jnp.abs(x)))


def _log_sigmoid(x):
    return -_softplus(-x)


def _rms(x, g):
    r = lax.rsqrt(jnp.mean(x * x, axis=-1, keepdims=True) + EPS)
    return x * r * g


def _rms_bwd(x, g, dy):
    r = lax.rsqrt(jnp.mean(x * x, axis=-1, keepdims=True) + EPS)
    xh = x * r
    dg = jnp.sum(dy * xh, axis=0, keepdims=True)
    dyg = dy * g
    dx = r * (dyg - xh * jnp.mean(dyg * xh, axis=-1, keepdims=True))
    return dx, dg


def _pair_stat(t, m0):
    s0 = jnp.sum(jnp.where(m0, t, 0.0), axis=-1, keepdims=True)
    s1 = jnp.sum(jnp.where(m0, 0.0, t), axis=-1, keepdims=True)
    return jnp.where(m0, s0, s1)


def _rms_pair(x, g, m0):
    r = lax.rsqrt(_pair_stat(x * x, m0) * (1.0 / FOX_HEAD_DIM) + EPS)
    return x * r * g


def _rms_pair_bwd(x, g, dy, m0):
    r = lax.rsqrt(_pair_stat(x * x, m0) * (1.0 / FOX_HEAD_DIM) + EPS)
    xh = x * r
    dg = jnp.sum(dy * xh, axis=0, keepdims=True)
    dyg = dy * g
    dx = r * (dyg - xh * (_pair_stat(dyg * xh, m0) * (1.0 / FOX_HEAD_DIM)))
    return dx, dg


@jax.custom_vjp
def _mm_nn(a, b):
    return _dot(a, b, NN)


_mm_nn.defvjp(lambda a, b: (_dot(a, b, NN), (a, b)),
              lambda r, g: (_dot(g, r[1], NT), _dot(r[0], g, TN)))


@jax.custom_vjp
def _mm_nt(a, b):
    return _dot(a, b, NT)


_mm_nt.defvjp(lambda a, b: (_dot(a, b, NT), (a, b)),
              lambda r, g: (_dot(g, r[1], NN), _dot(g, r[0], TN)))


@jax.custom_vjp
def _mm_tn(a, b):
    return _dot(a, b, TN)


_mm_tn.defvjp(lambda a, b: (_dot(a, b, TN), (a, b)),
              lambda r, g: (_dot(r[1], g, NT), _dot(r[0], g, NN)))


def _dot3(a, b, dims):
    ah = a.astype(BF16)
    al = (a - ah.astype(F32)).astype(BF16)
    bh = b.astype(BF16)
    bl = (b - bh.astype(F32)).astype(BF16)
    d = functools.partial(lax.dot_general, dimension_numbers=dims, preferred_element_type=F32)
    return d(ah, bh) + d(ah, bl) + d(al, bh)


def _neumann_inverses(mats):
    c = mats[0].shape[0]
    eye = (lax.broadcasted_iota(jnp.int32, (c, c), 0) == lax.broadcasted_iota(jnp.int32, (c, c), 1)).astype(F32)
    xs = [eye - a for a in mats]
    ps = list(mats)
    k = 2
    while k < c + 1:
        ps = [_dot3(p, p, NN) for p in ps]
        xs = [x + _dot3(x, p, NN) for x, p in zip(xs, ps)]
        k *= 2
    return xs


@jax.custom_vjp
def _unit_lower_inverses(mats):
    return _neumann_inverses(mats)


def _unit_lower_inverses_fwd(mats):
    ts = _neumann_inverses(mats)
    return ts, ts


def _unit_lower_inverses_bwd(ts, gs):
    left = [_dot3(t, g, TN) for t, g in zip(ts, gs)]
    return ([-_dot3(m, t, NT) for m, t in zip(left, ts)],)


_unit_lower_inverses.defvjp(_unit_lower_inverses_fwd, _unit_lower_inverses_bwd)


def _wgrad(a, b, name, bk=1024, bn=1024, bt=1024, column_blocks=None):
    t_len, k_len = a.shape
    n_len = b.shape[1]
    bk, bn, bt = min(bk, k_len), min(bn, n_len), min(bt, t_len)
    nt = t_len // bt

    def body(a_ref, b_ref, o_ref, acc_ref):
        t = pl.program_id(2)

        @pl.when(t == 0)
        def _():
            acc_ref[...] = jnp.zeros_like(acc_ref)

        acc_ref[...] += _dot(a_ref[...], b_ref[...], TN)

        @pl.when(t == nt - 1)
        def _():
            if column_blocks:
                for jj in range(bn // column_blocks):
                    o_ref[jj] = acc_ref[:, jj * column_blocks:(jj + 1) * column_blocks]
            else:
                o_ref[...] = acc_ref[...]

    if column_blocks:
        out_spec = pl.BlockSpec((bn // column_blocks, bk, column_blocks), lambda i, j, t: (j, i, 0))
        out_shape = jax.ShapeDtypeStruct((n_len // column_blocks, k_len, column_blocks), F32)
    else:
        out_spec = pl.BlockSpec((bk, bn), lambda i, j, t: (i, j))
        out_shape = jax.ShapeDtypeStruct((k_len, n_len), F32)
    return pl.pallas_call(
        body, name=name, grid=(k_len // bk, n_len // bn, nt),
        in_specs=[pl.BlockSpec((bt, bk), lambda i, j, t: (t, i)), pl.BlockSpec((bt, bn), lambda i, j, t: (t, j))],
        out_specs=out_spec, out_shape=out_shape,
        scratch_shapes=[pltpu.VMEM((bk, bn), F32)],
        compiler_params=_cparams(("parallel", "parallel", "arbitrary")),
    )(a, b)


def _wgrad_stacked(pieces, b, name, bn=512, bt=1024):
    t_len, n_len = b.shape
    n_p = len(pieces)
    starts = [int(s) for s in np.cumsum([0] + [p.shape[1] for p in pieces])]
    bn, bt = min(bn, n_len), min(bt, t_len)
    nt = t_len // bt

    def body(*refs):
        b_ref, o_ref, acc_ref = refs[n_p:]
        t = pl.program_id(1)

        @pl.when(t == 0)
        def _():
            acc_ref[...] = jnp.zeros_like(acc_ref)

        for k in range(n_p):
            acc_ref[starts[k]:starts[k + 1], :] += _dot(refs[k][...], b_ref[...], TN)

        @pl.when(t == nt - 1)
        def _():
            o_ref[...] = acc_ref[...]

    return pl.pallas_call(
        body, name=name, grid=(n_len // bn, nt),
        in_specs=[pl.BlockSpec((bt, p.shape[1]), lambda j, t: (t, 0)) for p in pieces] + [pl.BlockSpec((bt, bn), lambda j, t: (t, j))],
        out_specs=pl.BlockSpec((starts[-1], bn), lambda j, t: (0, j)),
        out_shape=jax.ShapeDtypeStruct((starts[-1], n_len), F32),
        scratch_shapes=[pltpu.VMEM((starts[-1], bn), F32)],
        compiler_params=_cparams(("parallel", "arbitrary")),
    )(*pieces, b)


def _rows_matmul(a, b, name, bt=512):
    r_len, t_len = a.shape
    n_len = b.shape[1]
    bt = min(bt, t_len)
    nt = t_len // bt

    def body(a_ref, b_ref, o_ref):
        t = pl.program_id(0)

        @pl.when(t == 0)
        def _():
            o_ref[...] = jnp.zeros_like(o_ref)

        o_ref[...] += _dot(a_ref[...], b_ref[...], NN)

    return pl.pallas_call(
        body, name=name, grid=(nt,),
        in_specs=[pl.BlockSpec((r_len, bt), lambda t: (0, t)), pl.BlockSpec((bt, n_len), lambda t: (t, 0))],
        out_specs=pl.BlockSpec((r_len, n_len), lambda t: (0, 0)),
        out_shape=jax.ShapeDtypeStruct((r_len, n_len), F32),
        compiler_params=_cparams(("arbitrary",)),
    )(a, b)


def _in_proj(x, g, wp, wst, tm=256):
    t_len, d = x.shape
    tm = min(tm, t_len)

    def body(x_ref, g_ref, wp_ref, wst_ref, h_ref, fox_ref, gdn_ref, z_ref, sm_ref, smt_ref):
        h = _rms(x_ref[...], g_ref[...]).astype(BF16)
        h_ref[...] = h
        p = _dot(h, wp_ref[...], NT)
        fox_ref[...] = p[:, P_FOX:P_GDN]
        gdn_ref[...] = p[:, P_GDN:P_Z]
        z_ref[...] = p[:, P_Z:P_SMALL]
        sm_ref[...] = p[:, P_SMALL:P_DIM]
        smt_ref[...] = _dot(wst_ref[...], h, NT)

    row = lambda i: (i, 0)
    fixed = lambda i: (0, 0)
    return pl.pallas_call(
        body, name="in_proj", grid=(t_len // tm,),
        in_specs=[pl.BlockSpec((tm, d), row), pl.BlockSpec((1, d), fixed), pl.BlockSpec((P_DIM, d), fixed),
                  pl.BlockSpec((SM_ROWS, d), fixed)],
        out_specs=[pl.BlockSpec((tm, d), row), pl.BlockSpec((tm, 1536), row), pl.BlockSpec((tm, 1536), row),
                   pl.BlockSpec((tm, 512), row), pl.BlockSpec((tm, LANES), row), pl.BlockSpec((SM_ROWS, tm), lambda i: (0, i))],
        out_shape=[jax.ShapeDtypeStruct((t_len, d), BF16), jax.ShapeDtypeStruct((t_len, 1536), F32),
                   jax.ShapeDtypeStruct((t_len, 1536), F32), jax.ShapeDtypeStruct((t_len, 512), F32),
                   jax.ShapeDtypeStruct((t_len, LANES), F32), jax.ShapeDtypeStruct((SM_ROWS, t_len), F32)],
        compiler_params=_cparams(("parallel",)),
    )(x, g, wp, wst)


def _in_proj_bwd(dprojs, dsmt, x, g, wp, wst, dx1, tm=256):
    t_len, d = x.shape
    tm = min(tm, t_len)
    n_p = len(dprojs)
    starts = np.cumsum([0] + [p.shape[1] for p in dprojs])

    def body(*refs):
        dp_refs = refs[:n_p]
        dst_ref, x_ref, g_ref, wp_ref, wst_ref, dx1_ref, dx_ref, dg_ref = refs[n_p:]
        i = pl.program_id(0)
        dh = _dot(dst_ref[...], wst_ref[...], TN)
        for k in range(n_p):
            dh = dh + _dot(dp_refs[k][...], wp_ref[int(starts[k]):int(starts[k + 1]), :], NN)
        dxn, dg = _rms_bwd(x_ref[...], g_ref[...], dh)
        dx_ref[...] = dx1_ref[...] + dxn

        @pl.when(i == 0)
        def _():
            dg_ref[...] = jnp.zeros_like(dg_ref)

        dg_ref[...] += dg

    row = lambda i: (i, 0)
    fixed = lambda i: (0, 0)
    return pl.pallas_call(
        body, name="in_proj_bwd", grid=(t_len // tm,),
        in_specs=[pl.BlockSpec((tm, p.shape[1]), row) for p in dprojs] + [
            pl.BlockSpec((SM_ROWS, tm), lambda i: (0, i)), pl.BlockSpec((tm, d), row),
            pl.BlockSpec((1, d), fixed), pl.BlockSpec((P_DIM, d), fixed), pl.BlockSpec((SM_ROWS, d), fixed),
            pl.BlockSpec((tm, d), row)],
        out_specs=[pl.BlockSpec((tm, d), row), pl.BlockSpec((1, d), fixed)],
        out_shape=[jax.ShapeDtypeStruct((t_len, d), F32), jax.ShapeDtypeStruct((1, d), F32)],
        compiler_params=_cparams(("arbitrary",)),
    )(*dprojs, dsmt, x, g, wp, wst, dx1)


def _fox_cum(smt, bias_col, n_batch, s_len, ck=256):
    ck = min(ck, s_len)

    def body(s_ref, b_ref, c_ref):
        tri = (lax.broadcasted_iota(jnp.int32, (ck, ck), 0) <= lax.broadcasted_iota(jnp.int32, (ck, ck), 1)).astype(F32)
        carry = jnp.zeros((SM_ROWS, 1), F32)
        for r in range(s_len // ck):
            ls = _log_sigmoid(s_ref[:, r * ck:(r + 1) * ck] + b_ref[...])
            c = jnp.dot(ls, tri, precision=lax.Precision.HIGHEST, preferred_element_type=F32) + carry
            c_ref[:, r * ck:(r + 1) * ck] = c
            carry = c[:, ck - 1:ck]

    return pl.pallas_call(
        body, name="fox_cum", grid=(n_batch,),
        in_specs=[pl.BlockSpec((SM_ROWS, s_len), lambda b: (0, b)), pl.BlockSpec((SM_ROWS, 1), lambda b: (0, 0))],
        out_specs=pl.BlockSpec((SM_ROWS, s_len), lambda b: (0, b)),
        out_shape=jax.ShapeDtypeStruct(smt.shape, F32),
        compiler_params=_cparams(("parallel",)),
    )(smt, bias_col)


def _fox_cum_bwd(dc, smt, bias_col, n_batch, s_len, ck=256):
    ck = min(ck, s_len)
    nr = s_len // ck

    def body(dc_ref, s_ref, b_ref, dl_ref, db_ref):
        b = pl.program_id(0)
        tri = (lax.broadcasted_iota(jnp.int32, (ck, ck), 0) >= lax.broadcasted_iota(jnp.int32, (ck, ck), 1)).astype(F32)
        carry = jnp.zeros((SM_ROWS, 1), F32)
        tot = jnp.zeros((SM_ROWS, 1), F32)
        for r in reversed(range(nr)):
            sl = slice(r * ck, (r + 1) * ck)
            dls = jnp.dot(dc_ref[:, sl], tri, precision=lax.Precision.HIGHEST, preferred_element_type=F32) + carry
            carry = dls[:, 0:1]
            dl = dls * (1.0 - _sigmoid(s_ref[:, sl] + b_ref[...]))
            dl_ref[:, sl] = dl
            tot = tot + jnp.sum(dl, axis=1, keepdims=True)

        @pl.when(b == 0)
        def _():
            db_ref[...] = jnp.zeros_like(db_ref)

        db_ref[...] += jnp.broadcast_to(tot, db_ref.shape)

    return pl.pallas_call(
        body, name="fox_cum_bwd", grid=(n_batch,),
        in_specs=[pl.BlockSpec((SM_ROWS, s_len), lambda b: (0, b)), pl.BlockSpec((SM_ROWS, s_len), lambda b: (0, b)),
                  pl.BlockSpec((SM_ROWS, 1), lambda b: (0, 0))],
        out_specs=[pl.BlockSpec((SM_ROWS, s_len), lambda b: (0, b)), pl.BlockSpec((SM_ROWS, LANES), lambda b: (0, 0))],
        out_shape=[jax.ShapeDtypeStruct(smt.shape, F32), jax.ShapeDtypeStruct((SM_ROWS, LANES), F32)],
        compiler_params=_cparams(("arbitrary",)),
    )(dc, smt, bias_col)


def _fox_diagonal_mask(tq):
    return lax.broadcasted_iota(jnp.int32, (tq, tq), 1) <= lax.broadcasted_iota(jnp.int32, (tq, tq), 0)


def _fox_fwd(pf, cb, gq2, gk2, go2, tq=256):
    n_batch, s_len, _ = pf.shape
    tq = min(tq, s_len)
    nq = s_len // tq
    scale = FOX_HEAD_DIM ** -0.5

    def body(q_ref, k_ref, v_ref, c_ref, gq_ref, gk_ref, go_ref, o_ref, on_ref, lse_ref, kh_ref, vh_ref):
        j = pl.program_id(1)
        i = pl.program_id(2)
        m0 = lax.broadcasted_iota(jnp.int32, (1, LANES), 1) < FOX_HEAD_DIM

        @pl.when(i == 0)
        def _():
            kn = _rms_pair(k_ref[0], gk_ref[...], m0)
            kh_ref[0] = jnp.where(m0, kn, 0.0).astype(BF16)
            kh_ref[1] = jnp.where(m0, 0.0, kn).astype(BF16)
            v = v_ref[0]
            vh_ref[0] = jnp.where(m0, v, 0.0).astype(BF16)
            vh_ref[1] = jnp.where(m0, 0.0, v).astype(BF16)

        qb = (_rms_pair(q_ref[0], gq_ref[...], m0) * scale).astype(BF16)

        def step(kb, carry, diagonal=False):
            ms, ls, acc = carry
            off = pl.multiple_of(kb * tq, tq)
            new_m, new_l, alphas, pv = [], [], [], []
            for hh in range(2):
                s = _dot(qb, kh_ref[hh, pl.ds(off, tq), :], NT)
                s = s - c_ref[0, kb, pl.ds(2 * j + hh, 1), :]
                if diagonal:
                    s = jnp.where(_fox_diagonal_mask(tq), s, NEG_INF)
                m_new = jnp.maximum(ms[hh], jnp.max(s, axis=-1, keepdims=True))
                alpha = jnp.exp(ms[hh] - m_new)
                p = jnp.exp(s - m_new)
                new_l.append(alpha * ls[hh] + jnp.sum(p, axis=-1, keepdims=True))
                new_m.append(m_new)
                alphas.append(alpha)
                pv.append(_dot(p, vh_ref[hh, pl.ds(off, tq), :], NN))
            acc = jnp.where(m0, alphas[0], alphas[1]) * acc + pv[0] + pv[1]
            return tuple(new_m), tuple(new_l), acc

        init_m = (jnp.full((tq, 1), NEG_INF, F32),) * 2
        init_l = (jnp.zeros((tq, 1), F32),) * 2
        carry = lax.fori_loop(0, i, step, (init_m, init_l, jnp.zeros((tq, LANES), F32)))
        ms, ls, acc = step(i, carry, diagonal=True)
        o = acc / jnp.where(m0, ls[0], ls[1])
        o_ref[0] = o
        on_ref[0] = _rms_pair(o, go_ref[...], m0).astype(BF16)
        lse_ref[0] = jnp.where(m0, ms[0] + jnp.log(ls[0]), ms[1] + jnp.log(ls[1]))

    fixed = lambda b, j, i: (0, 0)
    tile = lambda b, j, i: (b, i, j)
    return pl.pallas_call(
        body, name="fox_fwd", grid=(n_batch, 4, nq),
        in_specs=[pl.BlockSpec((1, tq, LANES), tile), pl.BlockSpec((1, s_len, LANES), lambda b, j, i: (b, 0, 4 + j)),
                  pl.BlockSpec((1, s_len, LANES), lambda b, j, i: (b, 0, 8 + j)),
                  pl.BlockSpec((1, nq, SM_ROWS, tq), lambda b, j, i: (b, 0, 0, 0)),
                  pl.BlockSpec((1, LANES), fixed), pl.BlockSpec((1, LANES), fixed), pl.BlockSpec((1, LANES), fixed)],
        out_specs=[pl.BlockSpec((1, tq, LANES), tile), pl.BlockSpec((1, tq, LANES), tile), pl.BlockSpec((1, tq, LANES), tile)],
        out_shape=[jax.ShapeDtypeStruct((n_batch, s_len, FOX_WIDTH), F32), jax.ShapeDtypeStruct((n_batch, s_len, FOX_WIDTH), BF16),
                   jax.ShapeDtypeStruct((n_batch, s_len, FOX_WIDTH), F32)],
        scratch_shapes=[pltpu.VMEM((2, s_len, LANES), BF16), pltpu.VMEM((2, s_len, LANES), BF16)],
        compiler_params=_cparams(("parallel", "parallel", "arbitrary")),
    )(pf, pf, pf, cb, gq2, gk2, go2)


def _fox_bwd(pf, cb, gq2, gk2, go2, o, lse, don, tq=256):
    n_batch, s_len, _ = pf.shape
    tq = min(tq, s_len)
    nq = s_len // tq
    scale = FOX_HEAD_DIM ** -0.5

    def body(q_ref, k_ref, v_ref, c_ref, gq_ref, gk_ref, go_ref, o_ref, lse_ref, don_ref,
             dq_ref, dk_ref, dv_ref, dc_ref, dgq_ref, dgk_ref, dgo_ref, kh_ref, vh_ref, dka_ref, dva_ref, dca_ref):
        b = pl.program_id(0)
        j = pl.program_id(1)
        i = pl.program_id(2)
        m0 = lax.broadcasted_iota(jnp.int32, (1, LANES), 1) < FOX_HEAD_DIM

        @pl.when((b == 0) & (j == 0) & (i == 0))
        def _():
            dgq_ref[...] = jnp.zeros_like(dgq_ref)
            dgk_ref[...] = jnp.zeros_like(dgk_ref)
            dgo_ref[...] = jnp.zeros_like(dgo_ref)

        @pl.when(i == 0)
        def _():
            kn = _rms_pair(k_ref[0], gk_ref[...], m0)
            kh_ref[0] = jnp.where(m0, kn, 0.0).astype(BF16)
            kh_ref[1] = jnp.where(m0, 0.0, kn).astype(BF16)
            v = v_ref[0]
            vh_ref[0] = jnp.where(m0, v, 0.0).astype(BF16)
            vh_ref[1] = jnp.where(m0, 0.0, v).astype(BF16)
            dka_ref[...] = jnp.zeros_like(dka_ref)
            dva_ref[...] = jnp.zeros_like(dva_ref)
            dca_ref[...] = jnp.zeros_like(dca_ref)

        q = q_ref[0]
        qn = _rms_pair(q, gq_ref[...], m0)
        qs = qn * scale
        qb = qs.astype(BF16)
        qh = (jnp.where(m0, qs, 0.0).astype(BF16), jnp.where(m0, 0.0, qs).astype(BF16))
        ot = o_ref[0]
        do, dgo = _rms_pair_bwd(ot, go_ref[...], don_ref[0], m0)
        dgo_ref[...] += dgo
        dd = do * ot
        delta = (jnp.sum(jnp.where(m0, dd, 0.0), axis=-1, keepdims=True), jnp.sum(jnp.where(m0, 0.0, dd), axis=-1, keepdims=True))
        doh = (jnp.where(m0, do, 0.0).astype(BF16), jnp.where(m0, 0.0, do).astype(BF16))
        lse_t = lse_ref[0]
        lse_h = (lse_t[:, 0:1], lse_t[:, FOX_HEAD_DIM:FOX_HEAD_DIM + 1])

        def step(kb, carry, diagonal=False):
            dqn, rs = carry
            rs = list(rs)
            off = pl.multiple_of(kb * tq, tq)
            for hh in range(2):
                kblk = kh_ref[hh, pl.ds(off, tq), :]
                vblk = vh_ref[hh, pl.ds(off, tq), :]
                s = _dot(qb, kblk, NT)
                s = s - c_ref[0, kb, pl.ds(2 * j + hh, 1), :]
                if diagonal:
                    s = jnp.where(_fox_diagonal_mask(tq), s, NEG_INF)
                p = jnp.exp(s - lse_h[hh])
                dp = _dot(doh[hh], vblk, NT)
                ds = p * (dp - delta[hh])
                dva_ref[pl.ds(off, tq), :] += _dot(p, doh[hh], TN)
                dka_ref[pl.ds(off, tq), :] += _dot(ds, qh[hh], TN)
                dca_ref[kb, hh:hh + 1, :] += -jnp.sum(ds, axis=0, keepdims=True)
                rs[hh] = rs[hh] + jnp.sum(ds, axis=-1, keepdims=True)
                dqn = dqn + _dot(ds, kblk, NN)
            return dqn, tuple(rs)

        carry = lax.fori_loop(0, i, step, (jnp.zeros((tq, LANES), F32), (jnp.zeros((tq, 1), F32),) * 2))
        dqn, rs = step(i, carry, diagonal=True)
        dqn = dqn * scale
        rs_rows = jnp.where(m0, rs[0], rs[1]).T
        dca_ref[i, 0:1, :] += rs_rows[0:1, :]
        dca_ref[i, 1:2, :] += rs_rows[FOX_HEAD_DIM:FOX_HEAD_DIM + 1, :]
        dq, dgq = _rms_pair_bwd(q, gq_ref[...], dqn, m0)
        dq_ref[0] = dq.astype(BF16)
        dgq_ref[...] += dgq

        @pl.when(i == nq - 1)
        def _():
            dk, dgk = _rms_pair_bwd(k_ref[0], gk_ref[...], dka_ref[...], m0)
            dk_ref[0] = dk.astype(BF16)
            dgk_ref[...] += dgk
            dv_ref[0] = dva_ref[...].astype(BF16)
            dc_ref[0, 0] = dca_ref[...]

    fixed = lambda b, j, i: (0, 0)
    tile = lambda b, j, i: (b, i, j)
    full = lambda b, j, i: (b, 0, j)
    wide = jax.ShapeDtypeStruct((n_batch, s_len, FOX_WIDTH), BF16)
    gain = jax.ShapeDtypeStruct((1, LANES), F32)
    return pl.pallas_call(
        body, name="fox_bwd", grid=(n_batch, 4, nq),
        in_specs=[pl.BlockSpec((1, tq, LANES), tile), pl.BlockSpec((1, s_len, LANES), lambda b, j, i: (b, 0, 4 + j)),
                  pl.BlockSpec((1, s_len, LANES), lambda b, j, i: (b, 0, 8 + j)),
                  pl.BlockSpec((1, nq, SM_ROWS, tq), lambda b, j, i: (b, 0, 0, 0)),
                  pl.BlockSpec((1, LANES), fixed), pl.BlockSpec((1, LANES), fixed), pl.BlockSpec((1, LANES), fixed),
                  pl.BlockSpec((1, tq, LANES), tile), pl.BlockSpec((1, tq, LANES), tile), pl.BlockSpec((1, tq, LANES), tile)],
        out_specs=[pl.BlockSpec((1, tq, LANES), tile), pl.BlockSpec((1, s_len, LANES), full), pl.BlockSpec((1, s_len, LANES), full),
                   pl.BlockSpec((1, 1, nq, 8, tq), lambda b, j, i: (b, j, 0, 0, 0)),
                   pl.BlockSpec((1, LANES), fixed), pl.BlockSpec((1, LANES), fixed), pl.BlockSpec((1, LANES), fixed)],
        out_shape=[wide, wide, wide, jax.ShapeDtypeStruct((n_batch, 4, nq, 8, tq), F32), gain, gain, gain],
        scratch_shapes=[pltpu.VMEM((2, s_len, LANES), BF16), pltpu.VMEM((2, s_len, LANES), BF16),
                        pltpu.VMEM((s_len, LANES), F32), pltpu.VMEM((s_len, LANES), F32), pltpu.VMEM((nq, 8, tq), F32)],
        compiler_params=_cparams(("arbitrary", "arbitrary", "arbitrary")),
    )(pf, pf, pf, cb, gq2, gk2, go2, o, lse, don)


def _shift_down(x, k):
    row = lax.broadcasted_iota(jnp.int32, x.shape, 0)
    return jnp.where(row >= k, pltpu.roll(x, k, 0), 0.0)


def _shift_up(x, k):
    n = x.shape[0]
    row = lax.broadcasted_iota(jnp.int32, x.shape, 0)
    return jnp.where(row < n - k, pltpu.roll(x, n - k, 0), 0.0)


def _conv_silu(x, w):
    y = w[3:4] * x + w[2:3] * _shift_down(x, 1) + w[1:2] * _shift_down(x, 2) + w[0:1] * _shift_down(x, 3)
    return y, y * _sigmoid(y)


def _gdn_pre(pg, conv_w):
    n_batch, s_len, width = pg.shape
    ncb = width // LANES

    def body(x_ref, w_ref, o_ref):
        cb = pl.program_id(1)
        _, s = _conv_silu(x_ref[0], w_ref[...])
        sn = s * lax.rsqrt(jnp.sum(s * s, axis=-1, keepdims=True) + EPS)
        o_ref[0] = jnp.where(cb < 2 * GDN_HEADS, sn, s)

    return pl.pallas_call(
        body, name="gdn_pre", grid=(n_batch, ncb),
        in_specs=[pl.BlockSpec((1, s_len, LANES), lambda b, c: (b, 0, c)), pl.BlockSpec((8, LANES), lambda b, c: (0, c))],
        out_specs=pl.BlockSpec((1, s_len, LANES), lambda b, c: (b, 0, c)),
        out_shape=jax.ShapeDtypeStruct(pg.shape, F32),
        compiler_params=_cparams(("parallel", "parallel")),
    )(pg, conv_w)


def _gdn_pre_bwd(pg, conv_w, dout):
    n_batch, s_len, width = pg.shape
    ncb = width // LANES

    def body(x_ref, w_ref, d_ref, dx_ref, dw_ref):
        cb = pl.program_id(0)
        b = pl.program_id(1)
        x = x_ref[0]
        w = w_ref[...]
        d = d_ref[0]
        y, s = _conv_silu(x, w)
        rr = lax.rsqrt(jnp.sum(s * s, axis=-1, keepdims=True) + EPS)
        sn = s * rr
        ds_n = rr * (d - sn * jnp.sum(d * sn, axis=-1, keepdims=True))
        ds = jnp.where(cb < 2 * GDN_HEADS, ds_n, d)
        sig = _sigmoid(y)
        dy = ds * (sig * (1.0 + y * (1.0 - sig)))
        dx = w[3:4] * dy + w[2:3] * _shift_up(dy, 1) + w[1:2] * _shift_up(dy, 2) + w[0:1] * _shift_up(dy, 3)
        dx_ref[0] = dx.astype(BF16)
        dw = [jnp.sum(dy * _shift_down(x, 3 - jj), axis=0, keepdims=True) if jj < 3 else jnp.sum(dy * x, axis=0, keepdims=True)
              for jj in range(CONV_WIDTH)]
        rows = lax.broadcasted_iota(jnp.int32, (8, LANES), 0)
        dwb = jnp.zeros((8, LANES), F32)
        for jj in range(CONV_WIDTH):
            dwb = dwb + jnp.where(rows == jj, dw[jj], 0.0)

        @pl.when(b == 0)
        def _():
            dw_ref[...] = jnp.zeros_like(dw_ref)

        dw_ref[...] += dwb

    blk = lambda c, b: (b, 0, c)
    return pl.pallas_call(
        body, name="gdn_pre_bwd", grid=(ncb, n_batch),
        in_specs=[pl.BlockSpec((1, s_len, LANES), blk), pl.BlockSpec((8, LANES), lambda c, b: (0, c)), pl.BlockSpec((1, s_len, LANES), blk)],
        out_specs=[pl.BlockSpec((1, s_len, LANES), blk), pl.BlockSpec((8, LANES), lambda c, b: (0, c))],
        out_shape=[jax.ShapeDtypeStruct(pg.shape, BF16), jax.ShapeDtypeStruct((8, width), F32)],
        compiler_params=_cparams(("parallel", "arbitrary")),
    )(pg, conv_w, dout)


def _gdn_gates(smc, smr, a_c, dt_c, a_r, dt_r, h):
    lane = lax.broadcasted_iota(jnp.int32, (1, LANES), 1)
    sub = lax.broadcasted_iota(jnp.int32, (SM_ROWS, 1), 0)
    beta_c = jnp.sum(jnp.where(lane == SM_B + h, _sigmoid(smc), 0.0), axis=1, keepdims=True)
    g_all_c = -jnp.exp(a_c) * _softplus(smc + dt_c)
    g_c = jnp.sum(jnp.where(lane == SM_A + h, g_all_c, 0.0), axis=1, keepdims=True)
    g_all_r = -jnp.exp(a_r) * _softplus(smr + dt_r)
    g_r = jnp.sum(jnp.where(sub == SM_A + h, g_all_r, 0.0), axis=0, keepdims=True)
    return beta_c, g_c, g_r


def _gdn_group(qkv, z, smc, smr, a_c, dt_c, a_r, dt_r, go, states):
    n_grp = len(qkv)
    c = qkv[0].shape[0]
    hd = GDN_HEAD_DIM
    pairs = [(g, h) for g in range(n_grp) for h in range(GDN_HEADS)]
    ii = lax.broadcasted_iota(jnp.int32, (c, c), 0)
    jj = lax.broadcasted_iota(jnp.int32, (c, c), 1)
    incl = ii >= jj
    col = lambda arr, base, h: arr[:, base + h * hd:base + (h + 1) * hd]

    qs, ks, kbs, vbs, decays, gcs, g_lasts, amats = [], [], [], [], [], [], [], []
    for g, h in pairs:
        beta_c, g_c, g_r = _gdn_gates(smc[g], smr[g], a_c, dt_c, a_r, dt_r, h)
        gc_c = jnp.sum(jnp.where(incl, g_r, 0.0), axis=1, keepdims=True)
        gc_r = jnp.sum(jnp.where(ii <= jj, g_c, 0.0), axis=0, keepdims=True)
        decay = jnp.where(incl, jnp.exp(jnp.where(incl, gc_c - gc_r, 0.0)), 0.0)
        k = col(qkv[g], GDN_WIDTH, h)
        kb = k * beta_c
        qs.append(col(qkv[g], 0, h) * (hd ** -0.5))
        ks.append(k)
        kbs.append(kb)
        vbs.append(col(qkv[g], 2 * GDN_WIDTH, h) * beta_c)
        decays.append(decay)
        gcs.append(gc_c)
        g_lasts.append(jnp.sum(g_c, axis=0, keepdims=True))
        amats.append(jnp.where(ii > jj, _mm_nt(kb, k) * decay, 0.0))
    ts = _unit_lower_inverses(amats)
    egcs = [jnp.exp(gc) for gc in gcs]
    us = [_mm_nn(t, vb) for t, vb in zip(ts, vbs)]
    ws = [_mm_nn(t, kb * e) for t, kb, e in zip(ts, kbs, egcs)]
    intras = [_mm_nt(q, k) * d for q, k, d in zip(qs, ks, decays)]
    qes = [q * e for q, e in zip(qs, egcs)]
    kds = [k * jnp.exp(gl - gc) for k, gl, gc in zip(ks, g_lasts, gcs)]
    sdecs = [jnp.exp(gl) for gl in g_lasts]

    outs = []
    for g in range(n_grp):
        idx = [g * GDN_HEADS + h for h in range(GDN_HEADS)]
        v_new = [us[i] - _mm_nn(ws[i], states[h]) for h, i in enumerate(idx)]
        o_state = [_mm_nn(qes[i], states[h]) for h, i in enumerate(idx)]
        o_intra = [_mm_nn(intras[i], v_new[h]) for h, i in enumerate(idx)]
        states = [states[h] * sdecs[i] + _mm_tn(kds[i], v_new[h]) for h, i in enumerate(idx)]
        outs.append([_rms(o_state[h] + o_intra[h], go) * (col(z[g], 0, h) * _sigmoid(col(z[g], 0, h))) for h in range(GDN_HEADS)])
    return outs, states


def _gdn_group_size(n_chunks):
    return GDN_GROUP if n_chunks % GDN_GROUP == 0 else 1


def _gdn_fwd(qkvn, z, smc, smr, a_c, dt_c, a_r, dt_r, go):
    n_batch, s_len, _ = qkvn.shape
    c = GDN_CHUNK
    n = s_len // c
    grp = _gdn_group_size(n)
    ng = n // grp
    gc = grp * c
    hd = GDN_HEAD_DIM

    def body(qkv_ref, z_ref, smc_ref, smr_ref, ac_ref, dc_ref, ar_ref, dr_ref, go_ref, og_ref, st_ref, s_ref):
        @pl.when(pl.program_id(1) == 0)
        def _():
            s_ref[...] = jnp.zeros_like(s_ref)

        states = [s_ref[h] for h in range(GDN_HEADS)]
        for h in range(GDN_HEADS):
            st_ref[0, 0, h] = states[h]
        rows = lambda k: slice(k * c, (k + 1) * c)
        outs, nxt = _gdn_group([qkv_ref[0, rows(k), :] for k in range(grp)], [z_ref[0, rows(k), :] for k in range(grp)],
                               [smc_ref[0, rows(k), :] for k in range(grp)], [smr_ref[k] for k in range(grp)],
                               ac_ref[...], dc_ref[...], ar_ref[...], dr_ref[...], go_ref[...], states)
        for k in range(grp):
            for h in range(GDN_HEADS):
                og_ref[0, rows(k), h * hd:(h + 1) * hd] = outs[k][h].astype(BF16)
        for h in range(GDN_HEADS):
            s_ref[h] = nxt[h]

    tok = lambda b, i: (b, i, 0)
    fixed = lambda b, i: (0, 0)
    return pl.pallas_call(
        body, name="gdn_fwd", grid=(n_batch, ng),
        in_specs=[pl.BlockSpec((1, gc, 3 * GDN_WIDTH), tok), pl.BlockSpec((1, gc, GDN_WIDTH), tok), pl.BlockSpec((1, gc, LANES), tok),
                  pl.BlockSpec((grp, SM_ROWS, c), lambda b, i: (b * ng + i, 0, 0)),
                  pl.BlockSpec((1, LANES), fixed), pl.BlockSpec((1, LANES), fixed), pl.BlockSpec((SM_ROWS, 1), fixed),
                  pl.BlockSpec((SM_ROWS, 1), fixed), pl.BlockSpec((1, LANES), fixed)],
        out_specs=[pl.BlockSpec((1, gc, GDN_WIDTH), tok), pl.BlockSpec((1, 1, GDN_HEADS, hd, hd), lambda b, i: (b, i, 0, 0, 0))],
        out_shape=[jax.ShapeDtypeStruct((n_batch, s_len, GDN_WIDTH), BF16), jax.ShapeDtypeStruct((n_batch, ng, GDN_HEADS, hd, hd), F32)],
        scratch_shapes=[pltpu.VMEM((GDN_HEADS, hd, hd), F32)],
        compiler_params=_cparams(("parallel", "arbitrary")),
    )(qkvn, z, smc, smr, a_c, dt_c, a_r, dt_r, go)


def _gdn_bwd(qkvn, z, smc, smr, a_c, dt_c, a_r, dt_r, go, states, dog):
    n_batch, s_len, _ = qkvn.shape
    c = GDN_CHUNK
    n = s_len // c
    grp = _gdn_group_size(n)
    ng = n // grp
    gc = grp * c
    hd = GDN_HEAD_DIM

    def body(qkv_ref, z_ref, smc_ref, smr_ref, ac_ref, dc_ref, ar_ref, dr_ref, go_ref, st_ref, dog_ref,
             dqkv_ref, dz_ref, dsmc_ref, dsmr_ref, dac_ref, ddc_ref, dar_ref, ddr_ref, dgo_ref, ds_ref):
        first = (pl.program_id(0) == 0) & (pl.program_id(1) == 0)

        @pl.when(pl.program_id(1) == 0)
        def _():
            ds_ref[...] = jnp.zeros_like(ds_ref)

        @pl.when(first)
        def _():
            for r in (dac_ref, ddc_ref, dar_ref, ddr_ref, dgo_ref):
                r[...] = jnp.zeros_like(r)

        rows = lambda k: slice(k * c, (k + 1) * c)
        states = [st_ref[0, 0, h] for h in range(GDN_HEADS)]
        prim = ([qkv_ref[0, rows(k), :] for k in range(grp)], [z_ref[0, rows(k), :] for k in range(grp)],
                [smc_ref[0, rows(k), :] for k in range(grp)], [smr_ref[k] for k in range(grp)],
                ac_ref[...], dc_ref[...], ar_ref[...], dr_ref[...], go_ref[...], states)
        _, vjp = jax.vjp(_gdn_group, *prim)
        cot = ([[dog_ref[0, rows(k), h * hd:(h + 1) * hd] for h in range(GDN_HEADS)] for k in range(grp)],
               [ds_ref[h] for h in range(GDN_HEADS)])
        dqkv, dz, dsmc, dsmr, dac, ddc, dar, ddr, dgo, dstates = vjp(cot)
        for k in range(grp):
            dqkv_ref[0, rows(k), :] = dqkv[k]
            dz_ref[0, rows(k), :] = dz[k].astype(BF16)
            dsmc_ref[0, rows(k), :] = dsmc[k]
            dsmr_ref[k] = dsmr[k]
        dac_ref[...] += dac
        ddc_ref[...] += ddc
        dar_ref[...] += dar
        ddr_ref[...] += ddr
        dgo_ref[...] += dgo
        for h in range(GDN_HEADS):
            ds_ref[h] = dstates[h]

    tok = lambda b, i: (b, ng - 1 - i, 0)
    fixed = lambda b, i: (0, 0)
    lane_vec = jax.ShapeDtypeStruct((1, LANES), F32)
    row_vec = jax.ShapeDtypeStruct((SM_ROWS, 1), F32)
    return pl.pallas_call(
        body, name="gdn_bwd", grid=(n_batch, ng),
        in_specs=[pl.BlockSpec((1, gc, 3 * GDN_WIDTH), tok), pl.BlockSpec((1, gc, GDN_WIDTH), tok), pl.BlockSpec((1, gc, LANES), tok),
                  pl.BlockSpec((grp, SM_ROWS, c), lambda b, i: (b * ng + ng - 1 - i, 0, 0)),
                  pl.BlockSpec((1, LANES), fixed), pl.BlockSpec((1, LANES), fixed), pl.BlockSpec((SM_ROWS, 1), fixed),
                  pl.BlockSpec((SM_ROWS, 1), fixed), pl.BlockSpec((1, LANES), fixed),
                  pl.BlockSpec((1, 1, GDN_HEADS, hd, hd), lambda b, i: (b, ng - 1 - i, 0, 0, 0)),
                  pl.BlockSpec((1, gc, GDN_WIDTH), lambda b, i: (b, ng - 1 - i, 1))],
        out_specs=[pl.BlockSpec((1, gc, 3 * GDN_WIDTH), tok), pl.BlockSpec((1, gc, GDN_WIDTH), tok), pl.BlockSpec((1, gc, LANES), tok),
                   pl.BlockSpec((grp, SM_ROWS, c), lambda b, i: (b * ng + ng - 1 - i, 0, 0)),
                   pl.BlockSpec((1, LANES), fixed), pl.BlockSpec((1, LANES), fixed), pl.BlockSpec((SM_ROWS, 1), fixed),
                   pl.BlockSpec((SM_ROWS, 1), fixed), pl.BlockSpec((1, LANES), fixed)],
        out_shape=[jax.ShapeDtypeStruct((n_batch, s_len, 3 * GDN_WIDTH), F32), jax.ShapeDtypeStruct((n_batch, s_len, GDN_WIDTH), BF16),
                   jax.ShapeDtypeStruct((n_batch, s_len, LANES), F32), jax.ShapeDtypeStruct((n_batch * n, SM_ROWS, c), F32),
                   lane_vec, lane_vec, row_vec, row_vec, lane_vec],
        scratch_shapes=[pltpu.VMEM((GDN_HEADS, hd, hd), F32)],
        compiler_params=_cparams(("arbitrary", "arbitrary")),
    )(qkvn, z, smc, smr, a_c, dt_c, a_r, dt_r, go, states, dog)


def _out_proj(x, oa, ob, w_out, g_x, w_cq, tm=256):
    t_len, d = x.shape
    tm = min(tm, t_len)

    def body(x_ref, oa_ref, ob_ref, wo_ref, g_ref, wq_ref, x1_ref, hq_ref, cq_ref):
        x1 = x_ref[...] + _dot(oa_ref[...], wo_ref[0:FOX_WIDTH, :]) + _dot(ob_ref[...], wo_ref[FOX_WIDTH:2 * FOX_WIDTH, :])
        x1_ref[...] = x1
        hq = _rms(x1, g_ref[...]).astype(BF16)
        hq_ref[...] = hq
        cq_ref[...] = _dot(hq, wq_ref[...])

    row = lambda i: (i, 0)
    fixed = lambda i: (0, 0)
    return pl.pallas_call(
        body, name="out_proj", grid=(t_len // tm,),
        in_specs=[pl.BlockSpec((tm, d), row), pl.BlockSpec((tm, FOX_WIDTH), row), pl.BlockSpec((tm, GDN_WIDTH), row),
                  pl.BlockSpec((d, d), fixed), pl.BlockSpec((1, d), fixed), pl.BlockSpec((d, XATTN_WIDTH), fixed)],
        out_specs=[pl.BlockSpec((tm, d), row), pl.BlockSpec((tm, d), row), pl.BlockSpec((tm, XATTN_WIDTH), row)],
        out_shape=[jax.ShapeDtypeStruct((t_len, d), F32), jax.ShapeDtypeStruct((t_len, d), BF16), jax.ShapeDtypeStruct((t_len, XATTN_WIDTH), F32)],
        compiler_params=_cparams(("parallel",)),
    )(x, oa, ob, w_out, g_x, w_cq)


def _out_proj_bwd(dx1, w_out, tm=512):
    t_len, d = dx1.shape
    tm = min(tm, t_len)

    def body(dx_ref, w_ref, o_ref):
        o_ref[...] = _dot(dx_ref[...], w_ref[...], NT)

    return pl.pallas_call(
        body, name="out_proj_bwd", grid=(t_len // tm,),
        in_specs=[pl.BlockSpec((tm, d), lambda i: (i, 0)), pl.BlockSpec((d, d), lambda i: (0, 0))],
        out_specs=pl.BlockSpec((tm, d), lambda i: (i, 0)),
        out_shape=jax.ShapeDtypeStruct((t_len, d), F32),
        compiler_params=_cparams(("parallel",)),
    )(dx1, w_out)


def _mem_kv(mem, g, w_ckv, tm=256):
    t_len, d = mem.shape
    tm = min(tm, t_len)

    def body(x_ref, g_ref, w_ref, h_ref, o_ref):
        h = _rms(x_ref[...], g_ref[...]).astype(BF16)
        h_ref[...] = h
        o_ref[...] = _dot(h, w_ref[...])

    row = lambda i: (i, 0)
    fixed = lambda i: (0, 0)
    return pl.pallas_call(
        body, name="mem_kv", grid=(t_len // tm,),
        in_specs=[pl.BlockSpec((tm, d), row), pl.BlockSpec((1, d), fixed), pl.BlockSpec((d, 2 * XATTN_WIDTH), fixed)],
        out_specs=[pl.BlockSpec((tm, d), row), pl.BlockSpec((tm, 2 * XATTN_WIDTH), row)],
        out_shape=[jax.ShapeDtypeStruct((t_len, d), BF16), jax.ShapeDtypeStruct((t_len, 2 * XATTN_WIDTH), F32)],
        compiler_params=_cparams(("parallel",)),
    )(mem, g, w_ckv)


def _mem_kv_bwd(dckv, mem, g, w_ckv, tm=256):
    t_len, d = mem.shape
    tm = min(tm, t_len)

    def body(d_ref, x_ref, g_ref, w_ref, dg_ref):
        @pl.when(pl.program_id(0) == 0)
        def _():
            dg_ref[...] = jnp.zeros_like(dg_ref)

        dh = _dot(d_ref[...], w_ref[...], NT)
        _, dg = _rms_bwd(x_ref[...], g_ref[...], dh)
        dg_ref[...] += dg

    row = lambda i: (i, 0)
    fixed = lambda i: (0, 0)
    return pl.pallas_call(
        body, name="mem_kv_bwd", grid=(t_len // tm,),
        in_specs=[pl.BlockSpec((tm, 2 * XATTN_WIDTH), row), pl.BlockSpec((tm, d), row), pl.BlockSpec((1, d), fixed),
                  pl.BlockSpec((d, 2 * XATTN_WIDTH), fixed)],
        out_specs=pl.BlockSpec((1, d), fixed),
        out_shape=jax.ShapeDtypeStruct((1, d), F32),
        compiler_params=_cparams(("arbitrary",)),
    )(dckv, mem, g, w_ckv)


def _xattn_probs(qn, kn):
    s = _dot(qn, kn, NT) * (XATTN_HEAD_DIM ** -0.5)
    p = jnp.exp(s - jnp.max(s, axis=-1, keepdims=True))
    return p / jnp.sum(p, axis=-1, keepdims=True)


def _xattn_fwd(cq, ckv, x1, gq, gk, w_co, g_mlp, n_batch, s_len, m_len, tq=512):
    d = x1.shape[1]
    tq = min(tq, s_len)
    nq = s_len // tq
    hd = XATTN_HEAD_DIM

    def body(cq_ref, kv_ref, x1_ref, gq_ref, gk_ref, wo_ref, gm_ref, co_ref, x2_ref, hf_ref):
        outs = []
        for h in range(XATTN_HEADS):
            qn = _rms(cq_ref[:, h * hd:(h + 1) * hd], gq_ref[...])
            kn = _rms(kv_ref[:, h * hd:(h + 1) * hd], gk_ref[...])
            p = _xattn_probs(qn, kn)
            outs.append(_dot(p, kv_ref[:, XATTN_WIDTH + h * hd:XATTN_WIDTH + (h + 1) * hd]).astype(BF16))
        x2 = x1_ref[...]
        for h in range(XATTN_HEADS):
            co_ref[:, h * hd:(h + 1) * hd] = outs[h]
            x2 = x2 + _dot(outs[h], wo_ref[h * hd:(h + 1) * hd, :])
        x2_ref[...] = x2
        hf_ref[...] = _rms(x2, gm_ref[...]).astype(BF16)

    row = lambda b, i: (b * nq + i, 0)
    fixed = lambda b, i: (0, 0)
    t_len = n_batch * s_len
    return pl.pallas_call(
        body, name="xattn_fwd", grid=(n_batch, nq),
        in_specs=[pl.BlockSpec((tq, XATTN_WIDTH), row), pl.BlockSpec((m_len, 2 * XATTN_WIDTH), lambda b, i: (b, 0)),
                  pl.BlockSpec((tq, d), row), pl.BlockSpec((1, hd), fixed), pl.BlockSpec((1, hd), fixed),
                  pl.BlockSpec((XATTN_WIDTH, d), fixed), pl.BlockSpec((1, d), fixed)],
        out_specs=[pl.BlockSpec((tq, XATTN_WIDTH), row), pl.BlockSpec((tq, d), row), pl.BlockSpec((tq, d), row)],
        out_shape=[jax.ShapeDtypeStruct((t_len, XATTN_WIDTH), BF16), jax.ShapeDtypeStruct((t_len, d), F32),
                   jax.ShapeDtypeStruct((t_len, d), BF16)],
        compiler_params=_cparams(("parallel", "parallel")),
    )(cq, ckv, x1, gq, gk, w_co, g_mlp)


def _xattn_bwd(dx2, cq, ckv, x1, gq, gk, w_co, g_x, w_cq, n_batch, s_len, m_len, tq=512):
    d = x1.shape[1]
    tq = min(tq, s_len)
    nq = s_len // tq
    hd = XATTN_HEAD_DIM
    scale = XATTN_HEAD_DIM ** -0.5

    def body(dx2_ref, cq_ref, kv_ref, x1_ref, gq_ref, gk_ref, wo_ref, gx_ref, wq_ref,
             dx1_ref, dcq_ref, dkv_ref, dgq_ref, dgk_ref, dgx_ref, dk_acc, dv_acc):
        b = pl.program_id(0)
        i = pl.program_id(1)

        @pl.when((b == 0) & (i == 0))
        def _():
            dgq_ref[...] = jnp.zeros_like(dgq_ref)
            dgk_ref[...] = jnp.zeros_like(dgk_ref)
            dgx_ref[...] = jnp.zeros_like(dgx_ref)

        @pl.when(i == 0)
        def _():
            dk_acc[...] = jnp.zeros_like(dk_acc)
            dv_acc[...] = jnp.zeros_like(dv_acc)

        dx2 = dx2_ref[...]
        dhq = jnp.zeros((tq, d), F32)
        for h in range(XATTN_HEADS):
            sl = slice(h * hd, (h + 1) * hd)
            q = cq_ref[:, sl]
            qn = _rms(q, gq_ref[...])
            kn = _rms(kv_ref[:, sl], gk_ref[...])
            v = kv_ref[:, XATTN_WIDTH + h * hd:XATTN_WIDTH + (h + 1) * hd]
            p = _xattn_probs(qn, kn)
            dco = _dot(dx2, wo_ref[sl, :], NT)
            dv_acc[:, sl] += _dot(p, dco, TN)
            dp = _dot(dco, v, NT)
            ds = p * (dp - jnp.sum(dp * p, axis=-1, keepdims=True))
            dqn = _dot(ds, kn) * scale
            dk_acc[:, sl] += _dot(ds, qn, TN) * scale
            dq, dgq = _rms_bwd(q, gq_ref[...], dqn)
            dgq_ref[...] += dgq
            dqb = dq.astype(BF16)
            dcq_ref[:, sl] = dqb
            dhq = dhq + _dot(dqb, wq_ref[:, sl], NT)
        dxn, dgx = _rms_bwd(x1_ref[...], gx_ref[...], dhq)
        dgx_ref[...] += dgx
        dx1_ref[...] = dx2 + dxn

        @pl.when(i == nq - 1)
        def _():
            for h in range(XATTN_HEADS):
                sl = slice(h * hd, (h + 1) * hd)
                dk, dgk = _rms_bwd(kv_ref[:, sl], gk_ref[...], dk_acc[:, sl])
                dgk_ref[...] += dgk
                dkv_ref[:, sl] = dk.astype(BF16)
                dkv_ref[:, XATTN_WIDTH + h * hd:XATTN_WIDTH + (h + 1) * hd] = dv_acc[:, sl].astype(BF16)

    row = lambda b, i: (b * nq + i, 0)
    fixed = lambda b, i: (0, 0)
    t_len = n_batch * s_len
    return pl.pallas_call(
        body, name="xattn_bwd", grid=(n_batch, nq),
        in_specs=[pl.BlockSpec((tq, d), row), pl.BlockSpec((tq, XATTN_WIDTH), row), pl.BlockSpec((m_len, 2 * XATTN_WIDTH), lambda b, i: (b, 0)),
                  pl.BlockSpec((tq, d), row), pl.BlockSpec((1, hd), fixed), pl.BlockSpec((1, hd), fixed),
                  pl.BlockSpec((XATTN_WIDTH, d), fixed), pl.BlockSpec((1, d), fixed), pl.BlockSpec((d, XATTN_WIDTH), fixed)],
        out_specs=[pl.BlockSpec((tq, d), row), pl.BlockSpec((tq, XATTN_WIDTH), row), pl.BlockSpec((m_len, 2 * XATTN_WIDTH), lambda b, i: (b, 0)),
                   pl.BlockSpec((1, hd), fixed), pl.BlockSpec((1, hd), fixed), pl.BlockSpec((1, d), fixed)],
        out_shape=[jax.ShapeDtypeStruct((t_len, d), F32), jax.ShapeDtypeStruct((t_len, XATTN_WIDTH), BF16),
                   jax.ShapeDtypeStruct((n_batch * m_len, 2 * XATTN_WIDTH), BF16),
                   jax.ShapeDtypeStruct((1, hd), F32), jax.ShapeDtypeStruct((1, hd), F32), jax.ShapeDtypeStruct((1, d), F32)],
        scratch_shapes=[pltpu.VMEM((m_len, XATTN_WIDTH), F32), pltpu.VMEM((m_len, XATTN_WIDTH), F32)],
        compiler_params=_cparams(("arbitrary", "arbitrary")),
    )(dx2, cq, ckv, x1, gq, gk, w_co, g_x, w_cq)


def _resident(shape):
    return pl.BlockSpec(shape, lambda *_: (0,) * len(shape), pipeline_mode=pl.Buffered(1))


def _mlp_fwd(hf, x2, target, w1, w2, tm=256, tf=1024):
    t_len, d = x2.shape
    f = w1.shape[1]
    tm, tf = min(tm, t_len), min(tf, f)

    def body(hf_ref, x2_ref, tg_ref, w1_ref, w2_ref, u_ref, a_ref, dy_ref, ls_ref):
        hf_t = hf_ref[...]
        y = x2_ref[...]
        for k in range(f // tf):
            cols = slice(k * tf, (k + 1) * tf)
            u = _dot(hf_t, w1_ref[:, cols])
            u_ref[:, cols] = u
            r = jnp.maximum(u, 0.0)
            a = (r * r).astype(BF16)
            a_ref[:, cols] = a
            y = y + _dot(a, w2_ref[cols, :])
        err = y - tg_ref[...]
        dy_ref[...] = err * (1.0 / d)
        ls_ref[...] = jnp.broadcast_to(jnp.sum(jnp.sum(err * err, axis=-1, keepdims=True) * (1.0 / d), axis=0, keepdims=True), ls_ref.shape)

    row = lambda i: (i, 0)
    return pl.pallas_call(
        body, name="mlp_fwd", grid=(t_len // tm,),
        in_specs=[pl.BlockSpec((tm, d), row), pl.BlockSpec((tm, d), row), pl.BlockSpec((tm, d), row), _resident((d, f)), _resident((f, d))],
        out_specs=[pl.BlockSpec((tm, f), row), pl.BlockSpec((tm, f), row), pl.BlockSpec((tm, d), row),
                   pl.BlockSpec((1, 8, LANES), lambda i: (i, 0, 0))],
        out_shape=[jax.ShapeDtypeStruct((t_len, f), F32), jax.ShapeDtypeStruct((t_len, f), BF16), jax.ShapeDtypeStruct((t_len, d), F32),
                   jax.ShapeDtypeStruct((t_len // tm, 8, LANES), F32)],
        compiler_params=_cparams(("parallel",)),
    )(hf, x2, target, w1, w2)


def _mlp_bwd(dy, u, x2, g, w1, w2, tm=256, tf=1024):
    t_len, d = x2.shape
    f = w1.shape[1]
    tm, tf = min(tm, t_len), min(tf, f)

    def body(dy_ref, u_ref, x2_ref, g_ref, w1_ref, w2_ref, du_ref, dx2_ref, dg_ref):
        @pl.when(pl.program_id(0) == 0)
        def _():
            dg_ref[...] = jnp.zeros_like(dg_ref)

        dy_t = dy_ref[...]
        dyb = dy_t.astype(BF16)
        dhf = jnp.zeros((tm, d), F32)
        for k in range(f // tf):
            cols = slice(k * tf, (k + 1) * tf)
            da = _dot(dyb, w2_ref[cols, :], NT)
            du = (da * (2.0 * jnp.maximum(u_ref[:, cols], 0.0))).astype(BF16)
            du_ref[:, cols] = du
            dhf = dhf + _dot(du, w1_ref[:, cols], NT)
        dxn, dg = _rms_bwd(x2_ref[...], g_ref[...], dhf)
        dx2_ref[...] = dy_t + dxn
        dg_ref[...] += dg

    row = lambda i: (i, 0)
    fixed = lambda i: (0, 0)
    return pl.pallas_call(
        body, name="mlp_bwd", grid=(t_len // tm,),
        in_specs=[pl.BlockSpec((tm, d), row), pl.BlockSpec((tm, f), row), pl.BlockSpec((tm, d), row), pl.BlockSpec((1, d), fixed),
                  _resident((d, f)), _resident((f, d))],
        out_specs=[pl.BlockSpec((tm, f), row), pl.BlockSpec((tm, d), row), pl.BlockSpec((1, d), fixed)],
        out_shape=[jax.ShapeDtypeStruct((t_len, f), BF16), jax.ShapeDtypeStruct((t_len, d), F32), jax.ShapeDtypeStruct((1, d), F32)],
        compiler_params=_cparams(("arbitrary",)),
    )(dy, u, x2, g, w1, w2)


def _pad_lanes(v, offset=0, width=LANES):
    return jnp.zeros((1, width), F32).at[:, offset:offset + v.shape[1]].set(v)


def _col(v, offset=0, rows=SM_ROWS):
    return jnp.zeros((rows, 1), F32).at[offset:offset + v.shape[1], 0].set(v[0])


def _pack_small(g_mix, dgq, dgk, dbias, dgo, dac, dar, ddc, ddr, g_gdn_o, g_nx, g_mem, g_xq, g_xk, g_mlp, loss_tiles):
    def body(mix_ref, q_ref, k_ref, b_ref, o_ref, ac_ref, ar_ref, dc_ref, dr_ref, go_ref, nx_ref, mem_ref, xq_ref, xk_ref,
             mlp_ref, lt_ref, out_ref):
        lane = lax.broadcasted_iota(jnp.int32, (1, LANES), 1)
        diag = lax.broadcasted_iota(jnp.int32, (SM_ROWS, LANES), 0) == lax.broadcasted_iota(jnp.int32, (SM_ROWS, LANES), 1)

        def rolled(v, shift):
            return pltpu.roll(jnp.broadcast_to(v, (8, LANES)), shift, 1)[0:1, :]

        def rows_to_lanes(col):
            return jnp.sum(jnp.where(diag, col, 0.0), axis=0, keepdims=True)

        def put(row, v, n):
            out_ref[row:row + 1, 0:LANES] = jnp.where(lane < n, v, 0.0)

        out_ref[...] = jnp.zeros_like(out_ref)
        out_ref[0:1, :] = mix_ref[...]
        for row, ref in ((1, q_ref), (2, k_ref), (4, o_ref)):
            put(row, ref[...] + rolled(ref[...], FOX_HEAD_DIM), FOX_HEAD_DIM)
        put(3, rows_to_lanes(b_ref[...]), FOX_HEADS)
        for row, lane_ref, row_ref in ((5, ac_ref, ar_ref), (6, dc_ref, dr_ref)):
            put(row, rolled(lane_ref[...] + rows_to_lanes(row_ref[...]), LANES - SM_A), GDN_HEADS)
        put(7, go_ref[...], LANES)
        out_ref[8:9, :] = nx_ref[...]
        out_ref[9:10, :] = mem_ref[...]
        put(10, xq_ref[...], LANES)
        put(11, xk_ref[...], LANES)
        out_ref[12:13, :] = mlp_ref[...]
        put(LOSS_ROW, 0.5 * jnp.sum(lt_ref[...], axis=0)[0:1, :], 1)

    args = (g_mix, dgq, dgk, dbias, dgo, dac, dar, ddc, ddr, g_gdn_o, g_nx, g_mem, g_xq, g_xk, g_mlp, loss_tiles)
    return pl.pallas_call(body, name="pack_small", out_shape=jax.ShapeDtypeStruct((PACK_ROWS, D_MODEL), F32))(*args)


LATE_WEIGHTS = (("w_out", "w_cq", "w_ckv", "w_co"), ("w_mlp1", "w_mlp2"))
GRAD_GROUPS = (("w_mlp2", "w_mlp1"), ("w_co", "w_cq", "w_ckv", "w_out"), ("w_in", "gdn_conv_w"))


def _local_step(x, mem, target, norm_mix_g, w_in, fox_qnorm_g, fox_knorm_g, fox_f_bias, fox_onorm_g, gdn_conv_w, gdn_A_log,
                gdn_dt_bias, gdn_onorm_g, norm_xattn_g, mem_norm_g, xattn_qnorm_g, xattn_knorm_g, norm_mlp_g,
                late_weights, grads_ready=None, first_token=0.0):
    if grads_ready is None:
        grads_ready = lambda group: 0.0
    n_batch, s_len, d = x.shape
    m_len = mem.shape[1]
    t_len = n_batch * s_len
    tq = min(FOX_BLOCK, s_len)
    nq = s_len // tq
    n_chunks = s_len // GDN_CHUNK
    x2d = x.reshape(t_len, d)

    wp = jnp.concatenate([w_in[0:1536], w_in[1544:3080], w_in[3088:3600], w_in[1536:1544], w_in[3080:3088],
                          jnp.zeros((P_DIM - 3600, d), BF16)], axis=0)
    wst = jnp.concatenate([w_in[1536:1544], w_in[3080:3088]], axis=0)
    conv_w = jnp.concatenate([gdn_conv_w, jnp.zeros((8 - CONV_WIDTH, gdn_conv_w.shape[1]), F32)], axis=0)
    bias_col = _col(fox_f_bias, SM_F)
    gq2, gk2, go2 = (jnp.tile(g, (1, 2)) for g in (fox_qnorm_g, fox_knorm_g, fox_onorm_g))
    a_c, dt_c = _pad_lanes(gdn_A_log, SM_A), _pad_lanes(gdn_dt_bias, SM_A)
    a_r, dt_r = _col(gdn_A_log, SM_A), _col(gdn_dt_bias, SM_A)

    h1, pfox, pgdn, pz, sm, smt = _in_proj(x2d, norm_mix_g + first_token, wp, wst)
    c_rows = _fox_cum(smt, bias_col, n_batch, s_len)
    cb = c_rows.reshape(SM_ROWS, n_batch, nq, tq).transpose(1, 2, 0, 3)
    pf3 = pfox.reshape(n_batch, s_len, 1536)
    o_fox, oa, lse = _fox_fwd(pf3, cb, gq2, gk2, go2, tq)
    pg3 = pgdn.reshape(n_batch, s_len, 1536)
    qkvn = _gdn_pre(pg3, conv_w)
    z3 = pz.reshape(n_batch, s_len, GDN_WIDTH)
    smc = sm.reshape(n_batch, s_len, LANES)
    smr = smt.reshape(SM_ROWS, n_batch * n_chunks, GDN_CHUNK).transpose(1, 0, 2)
    ob, states = _gdn_fwd(qkvn, z3, smc, smr, a_c, dt_c, a_r, dt_r, gdn_onorm_g)
    oa2, ob2 = oa.reshape(t_len, FOX_WIDTH), ob.reshape(t_len, GDN_WIDTH)
    w_out, w_cq, w_ckv, w_co = late_weights(LATE_WEIGHTS[0], ob2)
    x1, hq, cq = _out_proj(x2d, oa2, ob2, w_out, norm_xattn_g, w_cq)
    mem2d = mem.reshape(n_batch * m_len, d)
    hm, ckv = _mem_kv(mem2d, mem_norm_g, w_ckv)
    co, x2, hf = _xattn_fwd(cq, ckv, x1, xattn_qnorm_g, xattn_knorm_g, w_co, norm_mlp_g, n_batch, s_len, m_len)
    w_mlp1, w_mlp2 = late_weights(LATE_WEIGHTS[1], hf)
    u, a_act, dy, loss_tiles = _mlp_fwd(hf, x2, target.reshape(t_len, d), w_mlp1, w_mlp2)

    grads = {}
    du, dx2, grads["norm_mlp_g"] = _mlp_bwd(dy, u, x2, norm_mlp_g, w_mlp1, w_mlp2)
    grads["w_mlp2"] = _wgrad(a_act, dy, "wgrad_mlp2", bt=2048)
    grads["w_mlp1"] = _wgrad(hf, du, "wgrad_mlp1", bt=2048, column_blocks=D_FF // N_DEV)
    token = grads_ready({k: grads[k] for k in GRAD_GROUPS[0]})
    grads["w_co"] = _wgrad(co, dx2, "wgrad_co", column_blocks=D_MODEL // N_DEV)
    dx1, dcq, dckv, grads["xattn_qnorm_g"], grads["xattn_knorm_g"], grads["norm_xattn_g"] = _xattn_bwd(
        dx2, cq, ckv, x1, xattn_qnorm_g + token, xattn_knorm_g, w_co, norm_xattn_g, w_cq, n_batch, s_len, m_len)
    grads["w_cq"] = _wgrad(hq, dcq, "wgrad_cq")
    grads["w_ckv"] = _wgrad(hm, dckv, "wgrad_ckv")
    grads["mem_norm_g"] = _mem_kv_bwd(dckv, mem2d, mem_norm_g, w_ckv)
    grads["w_out"] = _wgrad_stacked([oa2, ob2], dx1, "wgrad_out", bn=1024)
    token = grads_ready({k: grads[k] for k in GRAD_GROUPS[1]})
    dcat = _out_proj_bwd(dx1, w_out)
    dcat3 = dcat.reshape(n_batch, s_len, d)

    dqkvn, dz, dsmc, dsmr, dac, ddc, dar, ddr, grads["gdn_onorm_g"] = _gdn_bwd(
        qkvn, z3, smc, smr, a_c, dt_c, a_r, dt_r, gdn_onorm_g + token, states, dcat3)
    dpg, dconv = _gdn_pre_bwd(pg3, conv_w, dqkvn)
    grads["gdn_conv_w"] = dconv[0:CONV_WIDTH]

    dq, dk, dv, dcb, dgq, dgk, dgo = _fox_bwd(pf3, cb, gq2, gk2, go2, o_fox, lse, dcat3, tq)
    dc8 = dcb[:, :, :, 0:2, :].transpose(1, 3, 0, 2, 4).reshape(FOX_HEADS, t_len)
    dc_rows = jnp.concatenate([dc8, jnp.zeros((SM_ROWS - FOX_HEADS, t_len), F32)], axis=0)
    dl_rows, dbias = _fox_cum_bwd(dc_rows, smt, bias_col, n_batch, s_len)
    dsm_rows = jnp.concatenate([dl_rows[0:SM_B], dsmr.transpose(1, 0, 2).reshape(SM_ROWS, t_len)[SM_B:SM_ROWS]], axis=0)

    dprojs = [dq.reshape(t_len, FOX_WIDTH), dk.reshape(t_len, FOX_WIDTH), dv.reshape(t_len, FOX_WIDTH),
              dpg.reshape(t_len, 1536), dz.reshape(t_len, GDN_WIDTH), dsmc.reshape(t_len, LANES)]
    dwp = _wgrad_stacked(dprojs, h1, "wgrad_in")
    dwst = _rows_matmul(dsm_rows, h1, "wgrad_in_rows")
    dw_small = dwp[P_SMALL:P_SMALL + SM_ROWS] + dwst
    grads["w_in"] = jnp.concatenate([dwp[0:1536], dw_small[0:8], dwp[1536:3072], dw_small[8:16], dwp[3072:3584]], axis=0)
    token = grads_ready({k: grads[k] for k in GRAD_GROUPS[2]})
    grad_x, grads["norm_mix_g"] = _in_proj_bwd(dprojs, dsm_rows, x2d, norm_mix_g + token, wp, wst, dx1)
    packed = _pack_small(grads["norm_mix_g"], dgq, dgk, dbias, dgo, dac, dar, ddc, ddr, grads["gdn_onorm_g"], grads["norm_xattn_g"],
                         grads["mem_norm_g"], grads["xattn_qnorm_g"], grads["xattn_knorm_g"], grads["norm_mlp_g"], loss_tiles)
    return packed, grad_x.reshape(n_batch, s_len, d), {k: grads[k] for k in SHARDED}


MESH_ID = pl.DeviceIdType.MESH
ANY_SPEC = pl.BlockSpec(memory_space=pl.ANY)


def _place():
    x, y, c = lax.axis_index("x"), lax.axis_index("y"), lax.axis_index("c")
    return x, y, c, [(1 - x, y), (x, 1 - y), (1 - x, 1 - y)]


def _place_own(src_ref, dst_ref):
    def staged(buf, sem):
        for a, b in ((src_ref, buf), (buf, dst_ref)):
            cp = pltpu.make_async_copy(a, b, sem)
            cp.start()
            cp.wait()

    pl.run_scoped(staged, pltpu.VMEM(src_ref.shape, src_ref.dtype), pltpu.SemaphoreType.DMA)


def _all_gather_body(n, ins, outs, send_sems, recv_sems):
    x, y, c, chips = _place()
    me, sibling = (x, y, c), (x, y, 1 - c)

    def copy(a, k, block, to, src=None):
        dst = outs[a].at[4 * block[0] + 2 * block[1] + block[2]]
        return pltpu.make_async_remote_copy(src_ref=dst if src is None else src, dst_ref=dst, send_sem=send_sems.at[a, k],
                                            recv_sem=recv_sems.at[a, k], device_id=to, device_id_type=MESH_ID)

    first = []
    for a in range(n):
        first.append(copy(a, 0, me, sibling, src=ins[a]))
        first += [copy(a, 1 + j, me, (*chip, c), src=ins[a]) for j, chip in enumerate(chips)]
    for cp in first:
        cp.start()
    for a in range(n):
        _place_own(ins[a], outs[a].at[4 * x + 2 * y + c])
    passed = []
    for j, chip in enumerate(chips):
        for a in range(n):
            copy(a, 1 + j, (*chip, c), me).wait_recv()
            fwd = copy(a, 4 + j, (*chip, c), sibling)
            fwd.start()
            passed.append(fwd)
    for a in range(n):
        copy(a, 0, sibling, me).wait_recv()
        for j, chip in enumerate(chips):
            copy(a, 4 + j, (*chip, 1 - c), me).wait_recv()
    for cp in first + passed:
        cp.wait_send()


def _all_gather_hbm(arrs, name):
    n = len(arrs)

    def body(*refs):
        _all_gather_body(n, refs[:n], refs[n:2 * n], refs[2 * n], refs[2 * n + 1])

    return pl.pallas_call(
        body, name=name, in_specs=[ANY_SPEC] * n, out_specs=[ANY_SPEC] * n,
        out_shape=[jax.ShapeDtypeStruct((N_DEV,) + a.shape, a.dtype) for a in arrs],
        scratch_shapes=[pltpu.SemaphoreType.DMA((n, 7)), pltpu.SemaphoreType.DMA((n, 7))],
        compiler_params=pltpu.CompilerParams(vmem_limit_bytes=VMEM_LIMIT),
    )(*arrs)


def _pair_exchange(arrs, name):
    n = len(arrs)

    def body(*refs):
        ins, outs = refs[:n], refs[n:2 * n]
        send_sems, recv_sems = refs[2 * n:]
        x, y, c, _ = _place()
        copies = []
        for a in range(n):
            for chip in range(4):
                copies.append(pltpu.make_async_remote_copy(
                    src_ref=ins[a].at[2 * chip + (1 - c)], dst_ref=outs[a].at[chip], send_sem=send_sems.at[a, chip],
                    recv_sem=recv_sems.at[a, chip], device_id=(x, y, 1 - c), device_id_type=MESH_ID))
        for cp in copies:
            cp.start()
        for cp in copies:
            cp.wait()

    return pl.pallas_call(
        body, name=name, in_specs=[ANY_SPEC] * n, out_specs=[ANY_SPEC] * n,
        out_shape=[jax.ShapeDtypeStruct((4,) + a.shape[1:], a.dtype) for a in arrs],
        scratch_shapes=[pltpu.SemaphoreType.DMA((n, 4)), pltpu.SemaphoreType.DMA((n, 4))],
    )(*arrs)


HBM_SPEC = pl.BlockSpec(memory_space=pltpu.HBM)
SEM_SPEC = pl.BlockSpec(memory_space=pltpu.SEMAPHORE)
DATAFLOW = pltpu.SideEffectType.DATAFLOW_SIDE_EFFECTING


def _in_hbm(arrs):
    return [pltpu.with_memory_space_constraint(a, pltpu.HBM) for a in arrs]


def _copies_start(name, srcs, lands, make_copies, after):
    n = len(srcs)
    n_copies = len(make_copies(srcs, lands, None, None)[0])

    def body(*refs):
        send_sems, recv_sems = refs[2 * n + 1], refs[2 * n + 2]
        for row in make_copies(refs[:n], refs[n:2 * n], send_sems, recv_sems):
            for cp in row:
                cp.start()
        refs[-1][...] = jnp.zeros_like(refs[-1])

    sems = pltpu.SemaphoreType.DMA((n * n_copies,))
    thru = [pltpu.HBM(a.shape, a.dtype) for a in list(srcs) + list(lands)]
    res = pl.pallas_call(
        body, name=name, in_specs=[HBM_SPEC] * (2 * n) + [ANY_SPEC],
        out_specs=(SEM_SPEC, SEM_SPEC, *[HBM_SPEC] * (2 * n), pl.BlockSpec(memory_space=pltpu.VMEM)),
        out_shape=(sems, sems, *thru, jax.ShapeDtypeStruct((8, LANES), F32)),
        input_output_aliases={i: 2 + i for i in range(2 * n)},
        compiler_params=pltpu.CompilerParams(has_side_effects=DATAFLOW),
    )(*_in_hbm(list(srcs) + list(lands)), after)
    return res[0], res[1], list(res[2:2 + n]), list(res[2 + n:2 + 2 * n]), res[-1]


def _copies_wait(name, send_sems, recv_sems, srcs, lands, after, make_copies, own_block=False):
    n = len(srcs)

    def body(*refs):
        if own_block:
            for a in range(n):
                _place_own(refs[a], _own_part(refs[a], refs[3 * n + 3 + a]))
        for row in make_copies(refs[:n], refs[n:2 * n], refs[2 * n], refs[2 * n + 1]):
            for cp in row:
                cp.wait_send()
                cp.wait_recv()

    res = pl.pallas_call(
        body, name=name, in_specs=[HBM_SPEC] * (2 * n) + [SEM_SPEC, SEM_SPEC, ANY_SPEC],
        out_specs=tuple([HBM_SPEC] * (2 * n)),
        out_shape=tuple(pltpu.HBM(a.shape, a.dtype) for a in list(srcs) + list(lands)),
        input_output_aliases={i: i for i in range(2 * n)},
        compiler_params=pltpu.CompilerParams(has_side_effects=DATAFLOW, vmem_limit_bytes=VMEM_LIMIT),
    )(*srcs, *lands, send_sems, recv_sems, after)
    return list(res[:n]), list(res[n:])


def _own_part(src_ref, land_ref):
    me = 4 * lax.axis_index("x") + 2 * lax.axis_index("y") + lax.axis_index("c")
    rows, cols = src_ref.shape
    if land_ref.shape[0] == N_DEV * rows:
        return land_ref.at[pl.ds(pl.multiple_of(me * rows, rows), rows), :]
    return land_ref.at[:, pl.ds(pl.multiple_of(me * cols, cols), cols)]


def _gather_copies(srcs, lands, send_sems, recv_sems):
    if send_sems is None:
        return [[None] * 7]
    x, y, c, _ = _place()
    rows = []
    for a in range(len(srcs)):
        row = []
        for k in range(7):
            r = k + 1
            to = (1 - x if r & 4 else x, 1 - y if r & 2 else y, 1 - c if r & 1 else c)
            row.append(pltpu.make_async_remote_copy(
                src_ref=srcs[a], dst_ref=_own_part(srcs[a], lands[a]), send_sem=send_sems.at[7 * a + k], recv_sem=recv_sems.at[7 * a + k],
                device_id=to, device_id_type=MESH_ID))
        rows.append(row)
    return rows


def _scatter_copies(srcs, lands, send_sems, recv_sems):
    if send_sems is None:
        return [[None] * 7]
    x, y, c, _ = _place()
    rows = []
    for a in range(len(srcs)):
        row = []
        for k in range(7):
            r = k + 1
            to = (1 - x if r & 4 else x, 1 - y if r & 2 else y, 1 - c if r & 1 else c)
            row.append(pltpu.make_async_remote_copy(
                src_ref=srcs[a].at[4 * to[0] + 2 * to[1] + to[2]], dst_ref=lands[a].at[k], send_sem=send_sems.at[7 * a + k],
                recv_sem=recv_sems.at[7 * a + k], device_id=to, device_id_type=MESH_ID))
        rows.append(row)
    return rows


def _chip_copies(srcs, lands, send_sems, recv_sems):
    if send_sems is None:
        return [[None] * 3]
    x, y, c, chips = _place()
    return [[pltpu.make_async_remote_copy(
        src_ref=srcs[a].at[2 * chip[0] + chip[1]], dst_ref=lands[a].at[j], send_sem=send_sems.at[3 * a + j], recv_sem=recv_sems.at[3 * a + j],
        device_id=(*chip, c), device_id_type=MESH_ID) for j, chip in enumerate(chips)] for a in range(len(srcs))]


def _tile(rows, cols):
    if rows <= 256:
        return rows, cols
    tr = 256 if cols <= 512 else 128
    if rows % tr == 0:
        return tr, cols
    return rows, 256


def _pair_sum(core, own, got, name):
    _, rows, cols = own.shape
    tr, tc = _tile(rows, cols)

    def body(c_ref, own_ref, got_ref, o_ref):
        o_ref[0] = own_ref[0] + got_ref[0]

    return pl.pallas_call(
        body, name=name,
        grid_spec=pltpu.PrefetchScalarGridSpec(
            num_scalar_prefetch=1, grid=(4, rows // tr, cols // tc),
            in_specs=[pl.BlockSpec((1, tr, tc), lambda k, i, j, c: (2 * k + c[0], i, j)),
                      pl.BlockSpec((1, tr, tc), lambda k, i, j, c: (k, i, j))],
            out_specs=pl.BlockSpec((1, tr, tc), lambda k, i, j, c: (k, i, j))),
        out_shape=jax.ShapeDtypeStruct((4, rows, cols), F32),
        compiler_params=_cparams(("parallel", "parallel", "parallel")),
    )(core, own, got)


def _adamw(w, g, m, v):
    m_new = ADAM_B1 * m + (1.0 - ADAM_B1) * g
    v_new = ADAM_B2 * v + (1.0 - ADAM_B2) * (g * g)
    m_hat = m_new / (1.0 - ADAM_B1 ** ADAM_STEP)
    v_hat = v_new / (1.0 - ADAM_B2 ** ADAM_STEP)
    delta = -ADAM_LR * (m_hat / (jnp.sqrt(v_hat) + ADAM_EPS) + ADAM_WD * w)
    return delta, m_new, v_new


def _sum_adam(chip, sums, parts, w, m, v, name):
    n_parts, rows, cols = parts.shape
    tr, tc = _tile(rows, cols)

    def body(chip_ref, own_ref, p_ref, w_ref, m_ref, v_ref, g_ref, d_ref, mo_ref, vo_ref):
        g = own_ref[0]
        for k in range(n_parts):
            g = g + p_ref[k]
        g_ref[...] = g
        d_ref[...], mo_ref[...], vo_ref[...] = _adamw(w_ref[...], g, m_ref[...], v_ref[...])

    tile = pl.BlockSpec((tr, tc), lambda i, j, ch: (i, j))
    out = jax.ShapeDtypeStruct((rows, cols), F32)
    return pl.pallas_call(
        body, name=name,
        grid_spec=pltpu.PrefetchScalarGridSpec(
            num_scalar_prefetch=1, grid=(rows // tr, cols // tc),
            in_specs=[pl.BlockSpec((1, tr, tc), lambda i, j, ch: (ch[0], i, j)),
                      pl.BlockSpec((n_parts, tr, tc), lambda i, j, ch: (0, i, j)), tile, tile, tile],
            out_specs=[tile, tile, tile, tile]),
        out_shape=[out, out, out, out],
        compiler_params=_cparams(("parallel", "parallel")),
    )(chip, sums, parts, w, m, v)


SHARDED = ("w_in", "gdn_conv_w", "w_out", "w_cq", "w_ckv", "w_co", "w_mlp1", "w_mlp2")
TRANSPOSED = ("w_in",)
COLUMN_SHARDED = ("gdn_conv_w", "w_co", "w_mlp1")
REPLICATED = ("norm_mix_g", "fox_qnorm_g", "fox_knorm_g", "fox_f_bias", "fox_onorm_g", "gdn_A_log", "gdn_dt_bias", "gdn_onorm_g",
              "norm_xattn_g", "mem_norm_g", "xattn_qnorm_g", "xattn_knorm_g", "norm_mlp_g")
WEIGHTS = ("norm_mix_g", "w_in", "fox_qnorm_g", "fox_knorm_g", "fox_f_bias", "fox_onorm_g", "gdn_conv_w", "gdn_A_log", "gdn_dt_bias",
           "gdn_onorm_g", "w_out", "norm_xattn_g", "mem_norm_g", "w_cq", "w_ckv", "xattn_qnorm_g", "xattn_knorm_g", "w_co",
           "norm_mlp_g", "w_mlp1", "w_mlp2")
PACK_ROWS = 16
LOSS_ROW = len(REPLICATED)


def _whole(name, gathered):
    if name in COLUMN_SHARDED:
        return gathered.transpose(1, 0, 2).reshape(gathered.shape[1], N_DEV * gathered.shape[2])
    return gathered.reshape(N_DEV * gathered.shape[1], gathered.shape[2])


def _whole_shape(name, shard_shape):
    rows, cols = shard_shape
    return (rows, N_DEV * cols) if name in COLUMN_SHARDED else (N_DEV * rows, cols)


def _blocks(name, whole):
    if whole.ndim == 3:
        return whole
    if name in COLUMN_SHARDED:
        rows, cols = whole.shape
        return whole.reshape(rows, N_DEV, cols // N_DEV).transpose(1, 0, 2)
    return whole.reshape(N_DEV, whole.shape[0] // N_DEV, whole.shape[1])


def _adam_small(everyone, ws, ms, vs):
    n_par = len(ws)

    def body(*refs):
        ev_ref = refs[0]
        w_refs, m_refs, v_refs = (refs[1 + j * n_par:1 + (j + 1) * n_par] for j in range(3))
        outs = refs[1 + 3 * n_par:-1]
        sum_ref = refs[-1]
        total = ev_ref[0]
        for dev in range(1, N_DEV):
            total = total + ev_ref[dev]
        sum_ref[...] = total
        for i in range(n_par):
            n = w_refs[i].shape[1]
            g = sum_ref[i:i + 1, 0:n]
            outs[4 * i][...] = g
            outs[4 * i + 1][...], outs[4 * i + 2][...], outs[4 * i + 3][...] = _adamw(w_refs[i][...], g, m_refs[i][...], v_refs[i][...])
        outs[4 * n_par][...] = sum_ref[LOSS_ROW:LOSS_ROW + 1, 0:1]

    shapes = [jax.ShapeDtypeStruct(a.shape, F32) for a in ws for _ in range(4)] + [jax.ShapeDtypeStruct((1, 1), F32)]
    return pl.pallas_call(body, name="adam_small", out_shape=shapes,
                          scratch_shapes=[pltpu.VMEM((PACK_ROWS, D_MODEL), F32)])(everyone, *ws, *ms, *vs)


def kernel(x, mem, norm_mix_g, w_in, fox_qnorm_g, fox_knorm_g, fox_f_bias, fox_onorm_g, gdn_conv_w, gdn_A_log, gdn_dt_bias, gdn_onorm_g, w_out, norm_xattn_g, mem_norm_g, w_cq, w_ckv, xattn_qnorm_g, xattn_knorm_g, w_co, norm_mlp_g, w_mlp1, w_mlp2, loss_target, m_norm_mix_g, m_w_in, m_fox_qnorm_g, m_fox_knorm_g, m_fox_f_bias, m_fox_onorm_g, m_gdn_conv_w, m_gdn_A_log, m_gdn_dt_bias, m_gdn_onorm_g, m_w_out, m_norm_xattn_g, m_mem_norm_g, m_w_cq, m_w_ckv, m_xattn_qnorm_g, m_xattn_knorm_g, m_w_co, m_norm_mlp_g, m_w_mlp1, m_w_mlp2, v_norm_mix_g, v_w_in, v_fox_qnorm_g, v_fox_knorm_g, v_fox_f_bias, v_fox_onorm_g, v_gdn_conv_w, v_gdn_A_log, v_gdn_dt_bias, v_gdn_onorm_g, v_w_out, v_norm_xattn_g, v_mem_norm_g, v_w_cq, v_w_ckv, v_xattn_qnorm_g, v_xattn_knorm_g, v_w_co, v_norm_mlp_g, v_w_mlp1, v_w_mlp2):
    given = dict(locals())
    w = {k: given[k] for k in WEIGHTS}
    m = {k: given["m_" + k] for k in WEIGHTS}
    v = {k: given["v_" + k] for k in WEIGHTS}

    core = lax.axis_index("c").astype(jnp.int32).reshape(1)
    chip = (2 * lax.axis_index("x") + lax.axis_index("y")).astype(jnp.int32).reshape(1)
    me = 4 * lax.axis_index("x") + 2 * lax.axis_index("y") + lax.axis_index("c")

    local = lambda d: {k: jnp.transpose(d[k][0]) if k in TRANSPOSED else d[k][0] for k in SHARDED}
    w2, m2, v2 = local(w), local(m), local(v)
    shards = {k: w2[k] if k == "gdn_conv_w" else w2[k].astype(BF16) for k in SHARDED}
    early = [k for k in SHARDED if not any(k in group for group in LATE_WEIGHTS)]
    gathered = _all_gather_hbm([shards[k] for k in early], "gather_early")
    whole = {k: _whole(k, g) for k, g in zip(early, gathered)}
    gathers, after = {}, gathered[0]
    for i, group in enumerate(LATE_WEIGHTS):
        lands = [lax.empty(_whole_shape(k, shards[k].shape), BF16) for k in group]
        gathers[group] = _copies_start("gather_late_start_" + str(i), [shards[k] for k in group], lands, _gather_copies, after=after)
        after = gathers[group][4]
    first_token = after[0, 0]

    def late_weights(group, after):
        gather = gathers[group]
        _, lands = _copies_wait("gather_late_wait_" + str(LATE_WEIGHTS.index(group)), gather[0], gather[1], gather[2], gather[3],
                                after, _gather_copies, own_block=True)
        return lands

    pending = []

    def grads_ready(group):
        names = list(group)
        tag = str(len(pending))
        own = [_blocks(k, group[k]) for k in names]
        if "w_in" in names:
            got = _pair_exchange(own, "grad_pair_exchange_" + tag)
            srcs = [_pair_sum(core, o, g, "grad_pair_sum_" + k) for k, o, g in zip(names, own, got)]
            copies, index, n_parts = _chip_copies, chip, 3
        else:
            srcs, copies, index, n_parts = own, _scatter_copies, me.astype(jnp.int32).reshape(1), 7
        lands = [lax.empty((n_parts,) + s.shape[1:], s.dtype) for s in srcs]
        started = _copies_start("grad_exchange_start_" + tag, srcs, lands, copies, after=core)
        pending.append((names, started, copies, index))
        return started[4][0, 0]

    small = {k: w[k] for k in REPLICATED}
    packed, grad_x, _ = _local_step(x, mem, loss_target, **small, **whole, late_weights=late_weights,
                                    grads_ready=grads_ready, first_token=first_token)

    small_lands = [lax.empty((N_DEV * PACK_ROWS, D_MODEL), F32)]
    small_gather = _copies_start("gather_small_start", [packed], small_lands, _gather_copies, after=grad_x)

    out_g, out_d, out_m, out_v = {}, {}, {}, {}
    after = small_gather[4]
    for tag, (names, started, copies, index) in enumerate(pending):
        srcs, parts = _copies_wait("grad_exchange_wait_" + str(tag), started[0], started[1], started[2], started[3], after, copies)
        for k, s, p in zip(names, srcs, parts):
            res = _sum_adam(index, s, p, w2[k], m2[k], v2[k], "adam_" + k)
            out_g[k], out_d[k], out_m[k], out_v[k] = ((jnp.transpose(r) if k in TRANSPOSED else r)[None] for r in res)
            after = res[0]

    _, (everyone,) = _copies_wait("gather_small_wait", small_gather[0], small_gather[1], small_gather[2], small_gather[3], after,
                                  _gather_copies, own_block=True)
    res = _adam_small(everyone.reshape(N_DEV, PACK_ROWS, D_MODEL), [w[k] for k in REPLICATED], [m[k] for k in REPLICATED],
                      [v[k] for k in REPLICATED])
    for i, k in enumerate(REPLICATED):
        out_g[k], out_d[k], out_m[k], out_v[k] = res[4 * i:4 * i + 4]
    loss = res[-1].reshape(())

    return (loss, grad_x, *[out_g[k] for k in WEIGHTS], *[out_d[k] for k in WEIGHTS], *[out_m[k] for k in WEIGHTS],
            *[out_v[k] for k in WEIGHTS])
```

```python
import functools

import jax
import jax.numpy as jnp
import numpy as np
from jax import lax
from jax.experimental import pallas as pl
from jax.experimental.pallas import tpu as pltpu

F32 = jnp.float32
BF16 = jnp.bfloat16

D_MODEL = 1024
FOX_HEADS = 8
FOX_HEAD_DIM = 64
FOX_WIDTH = 512
GDN_HEADS = 4
GDN_HEAD_DIM = 128
GDN_WIDTH = 512
CONV_WIDTH = 4
GDN_CHUNK = 128
GDN_GROUP = 4
FOX_BLOCK = 512
XATTN_HEADS = 4
XATTN_HEAD_DIM = 128
XATTN_WIDTH = 512
D_FF = 4096
EPS = 1e-6
NEG_INF = -1e30
N_DEV = 8

ADAM_LR = 0.001
ADAM_B1 = 0.9
ADAM_B2 = 0.999
ADAM_EPS = 1e-08
ADAM_WD = 0.01
ADAM_STEP = 10

P_FOX = 0
P_GDN = 1536
P_Z = 3072
P_SMALL = 3584
P_DIM = 3712
SM_F = 0
SM_B = 8
SM_A = 12
SM_ROWS = 16

LANES = 128
VMEM_LIMIT = 56 * 1024 * 1024

NN = (((1,), (0,)), ((), ()))
NT = (((1,), (1,)), ((), ()))
TN = (((0,), (0,)), ((), ()))


def _dot(a, b, dims=NN):
    return lax.dot_general(a.astype(BF16), b.astype(BF16), dims, preferred_element_type=F32)


def _cparams(sem=None):
    kw = dict(vmem_limit_bytes=VMEM_LIMIT)
    if sem is not None:
        kw["dimension_semantics"] = sem
    return pltpu.CompilerParams(**kw)


def _sigmoid(x):
    return 0.5 * (jnp.tanh(0.5 * x) + 1.0)


def _softplus(x):
    return jnp.maximum(x, 0.0) + jnp.log1p(jnp.exp(-jnp.abs(x)))


def _log_sigmoid(x):
    return -_softplus(-x)


def _rms(x, g):
    r = lax.rsqrt(jnp.mean(x * x, axis=-1, keepdims=True) + EPS)
    return x * r * g


def _rms_bwd(x, g, dy):
    r = lax.rsqrt(jnp.mean(x * x, axis=-1, keepdims=True) + EPS)
    xh = x * r
    dg = jnp.sum(dy * xh, axis=0, keepdims=True)
    dyg = dy * g
    dx = r * (dyg - xh * jnp.mean(dyg * xh, axis=-1, keepdims=True))
    return dx, dg


def _pair_stat(t, m0):
    s0 = jnp.sum(jnp.where(m0, t, 0.0), axis=-1, keepdims=True)
    s1 = jnp.sum(jnp.where(m0, 0.0, t), axis=-1, keepdims=True)
    return jnp.where(m0, s0, s1)


def _rms_pair(x, g, m0):
    r = lax.rsqrt(_pair_stat(x * x, m0) * (1.0 / FOX_HEAD_DIM) + EPS)
    return x * r * g


def _rms_pair_bwd(x, g, dy, m0):
    r = lax.rsqrt(_pair_stat(x * x, m0) * (1.0 / FOX_HEAD_DIM) + EPS)
    xh = x * r
    dg = jnp.sum(dy * xh, axis=0, keepdims=True)
    dyg = dy * g
    dx = r * (dyg - xh * (_pair_stat(dyg * xh, m0) * (1.0 / FOX_HEAD_DIM)))
    return dx, dg


@jax.custom_vjp
def _mm_nn(a, b):
    return _dot(a, b, NN)


_mm_nn.defvjp(lambda a, b: (_dot(a, b, NN), (a, b)),
              lambda r, g: (_dot(g, r[1], NT), _dot(r[0], g, TN)))


@jax.custom_vjp
def _mm_nt(a, b):
    return _dot(a, b, NT)


_mm_nt.defvjp(lambda a, b: (_dot(a, b, NT), (a, b)),
              lambda r, g: (_dot(g, r[1], NN), _dot(g, r[0], TN)))


@jax.custom_vjp
def _mm_tn(a, b):
    return _dot(a, b, TN)


_mm_tn.defvjp(lambda a, b: (_dot(a, b, TN), (a, b)),
              lambda r, g: (_dot(r[1], g, NT), _dot(r[0], g, NN)))


def _dot3(a, b, dims):
    ah = a.astype(BF16)
    al = (a - ah.astype(F32)).astype(BF16)
    bh = b.astype(BF16)
    bl = (b - bh.astype(F32)).astype(BF16)
    d = functools.partial(lax.dot_general, dimension_numbers=dims, preferred_element_type=F32)
    return d(ah, bh) + d(ah, bl) + d(al, bh)


def _neumann_inverses(mats):
    c = mats[0].shape[0]
    eye = (lax.broadcasted_iota(jnp.int32, (c, c), 0) == lax.broadcasted_iota(jnp.int32, (c, c), 1)).astype(F32)
    xs = [eye - a for a in mats]
    ps = list(mats)
    k = 2
    while k < c + 1:
        ps = [_dot3(p, p, NN) for p in ps]
        xs = [x + _dot3(x, p, NN) for x, p in zip(xs, ps)]
        k *= 2
    return xs


@jax.custom_vjp
def _unit_lower_inverses(mats):
    return _neumann_inverses(mats)


def _unit_lower_inverses_fwd(mats):
    ts = _neumann_inverses(mats)
    return ts, ts


def _unit_lower_inverses_bwd(ts, gs):
    left = [_dot3(t, g, TN) for t, g in zip(ts, gs)]
    return ([-_dot3(m, t, NT) for m, t in zip(left, ts)],)


_unit_lower_inverses.defvjp(_unit_lower_inverses_fwd, _unit_lower_inverses_bwd)


def _wgrad(a, b, name, bk=1024, bn=1024, bt=1024, column_blocks=None):
    t_len, k_len = a.shape
    n_len = b.shape[1]
    bk, bn, bt = min(bk, k_len), min(bn, n_len), min(bt, t_len)
    nt = t_len // bt

    def body(a_ref, b_ref, o_ref, acc_ref):
        t = pl.program_id(2)

        @pl.when(t == 0)
        def _():
            acc_ref[...] = jnp.zeros_like(acc_ref)

        acc_ref[...] += _dot(a_ref[...], b_ref[...], TN)

        @pl.when(t == nt - 1)
        def _():
            if column_blocks:
                for jj in range(bn // column_blocks):
                    o_ref[jj] = acc_ref[:, jj * column_blocks:(jj + 1) * column_blocks]
            else:
                o_ref[...] = acc_ref[...]

    if column_blocks:
        out_spec = pl.BlockSpec((bn // column_blocks, bk, column_blocks), lambda i, j, t: (j, i, 0))
        out_shape = jax.ShapeDtypeStruct((n_len // column_blocks, k_len, column_blocks), F32)
    else:
        out_spec = pl.BlockSpec((bk, bn), lambda i, j, t: (i, j))
        out_shape = jax.ShapeDtypeStruct((k_len, n_len), F32)
    return pl.pallas_call(
        body, name=name, grid=(k_len // bk, n_len // bn, nt),
        in_specs=[pl.BlockSpec((bt, bk), lambda i, j, t: (t, i)), pl.BlockSpec((bt, bn), lambda i, j, t: (t, j))],
        out_specs=out_spec, out_shape=out_shape,
        scratch_shapes=[pltpu.VMEM((bk, bn), F32)],
        compiler_params=_cparams(("parallel", "parallel", "arbitrary")),
    )(a, b)


def _wgrad_stacked(pieces, b, name, bn=512, bt=1024):
    t_len, n_len = b.shape
    n_p = len(pieces)
    starts = [int(s) for s in np.cumsum([0] + [p.shape[1] for p in pieces])]
    bn, bt = min(bn, n_len), min(bt, t_len)
    nt = t_len // bt

    def body(*refs):
        b_ref, o_ref, acc_ref = refs[n_p:]
        t = pl.program_id(1)

        @pl.when(t == 0)
        def _():
            acc_ref[...] = jnp.zeros_like(acc_ref)

        for k in range(n_p):
            acc_ref[starts[k]:starts[k + 1], :] += _dot(refs[k][...], b_ref[...], TN)

        @pl.when(t == nt - 1)
        def _():
            o_ref[...] = acc_ref[...]

    return pl.pallas_call(
        body, name=name, grid=(n_len // bn, nt),
        in_specs=[pl.BlockSpec((bt, p.shape[1]), lambda j, t: (t, 0)) for p in pieces] + [pl.BlockSpec((bt, bn), lambda j, t: (t, j))],
        out_specs=pl.BlockSpec((starts[-1], bn), lambda j, t: (0, j)),
        out_shape=jax.ShapeDtypeStruct((starts[-1], n_len), F32),
        scratch_shapes=[pltpu.VMEM((starts[-1], bn), F32)],
        compiler_params=_cparams(("parallel", "arbitrary")),
    )(*pieces, b)


def _rows_matmul(a, b, name, bt=512):
    r_len, t_len = a.shape
    n_len = b.shape[1]
    bt = min(bt, t_len)
    nt = t_len // bt

    def body(a_ref, b_ref, o_ref):
        t = pl.program_id(0)

        @pl.when(t == 0)
        def _():
            o_ref[...] = jnp.zeros_like(o_ref)

        o_ref[...] += _dot(a_ref[...], b_ref[...], NN)

    return pl.pallas_call(
        body, name=name, grid=(nt,),
        in_specs=[pl.BlockSpec((r_len, bt), lambda t: (0, t)), pl.BlockSpec((bt, n_len), lambda t: (t, 0))],
        out_specs=pl.BlockSpec((r_len, n_len), lambda t: (0, 0)),
        out_shape=jax.ShapeDtypeStruct((r_len, n_len), F32),
        compiler_params=_cparams(("arbitrary",)),
    )(a, b)


def _in_proj(x, g, wp, wst, tm=256):
    t_len, d = x.shape
    tm = min(tm, t_len)

    def body(x_ref, g_ref, wp_ref, wst_ref, h_ref, fox_ref, gdn_ref, z_ref, sm_ref, smt_ref):
        h = _rms(x_ref[...], g_ref[...]).astype(BF16)
        h_ref[...] = h
        p = _dot(h, wp_ref[...], NT)
        fox_ref[...] = p[:, P_FOX:P_GDN]
        gdn_ref[...] = p[:, P_GDN:P_Z]
        z_ref[...] = p[:, P_Z:P_SMALL]
        sm_ref[...] = p[:, P_SMALL:P_DIM]
        smt_ref[...] = _dot(wst_ref[...], h, NT)

    row = lambda i: (i, 0)
    fixed = lambda i: (0, 0)
    return pl.pallas_call(
        body, name="in_proj", grid=(t_len // tm,),
        in_specs=[pl.BlockSpec((tm, d), row), pl.BlockSpec((1, d), fixed), pl.BlockSpec((P_DIM, d), fixed),
                  pl.BlockSpec((SM_ROWS, d), fixed)],
        out_specs=[pl.BlockSpec((tm, d), row), pl.BlockSpec((tm, 1536), row), pl.BlockSpec((tm, 1536), row),
                   pl.BlockSpec((tm, 512), row), pl.BlockSpec((tm, LANES), row), pl.BlockSpec((SM_ROWS, tm), lambda i: (0, i))],
        out_shape=[jax.ShapeDtypeStruct((t_len, d), BF16), jax.ShapeDtypeStruct((t_len, 1536), F32),
                   jax.ShapeDtypeStruct((t_len, 1536), F32), jax.ShapeDtypeStruct((t_len, 512), F32),
                   jax.ShapeDtypeStruct((t_len, LANES), F32), jax.ShapeDtypeStruct((SM_ROWS, t_len), F32)],
        compiler_params=_cparams(("parallel",)),
    )(x, g, wp, wst)


def _in_proj_bwd(dprojs, dsmt, x, g, wp, wst, dx1, tm=256):
    t_len, d = x.shape
    tm = min(tm, t_len)
    n_p = len(dprojs)
    starts = np.cumsum([0] + [p.shape[1] for p in dprojs])

    def body(*refs):
        dp_refs = refs[:n_p]
        dst_ref, x_ref, g_ref, wp_ref, wst_ref, dx1_ref, dx_ref, dg_ref = refs[n_p:]
        i = pl.program_id(0)
        dh = _dot(dst_ref[...], wst_ref[...], TN)
        for k in range(n_p):
            dh = dh + _dot(dp_refs[k][...], wp_ref[int(starts[k]):int(starts[k + 1]), :], NN)
        dxn, dg = _rms_bwd(x_ref[...], g_ref[...], dh)
        dx_ref[...] = dx1_ref[...] + dxn

        @pl.when(i == 0)
        def _():
            dg_ref[...] = jnp.zeros_like(dg_ref)

        dg_ref[...] += dg

    row = lambda i: (i, 0)
    fixed = lambda i: (0, 0)
    return pl.pallas_call(
        body, name="in_proj_bwd", grid=(t_len // tm,),
        in_specs=[pl.BlockSpec((tm, p.shape[1]), row) for p in dprojs] + [
            pl.BlockSpec((SM_ROWS, tm), lambda i: (0, i)), pl.BlockSpec((tm, d), row),
            pl.BlockSpec((1, d), fixed), pl.BlockSpec((P_DIM, d), fixed), pl.BlockSpec((SM_ROWS, d), fixed),
            pl.BlockSpec((tm, d), row)],
        out_specs=[pl.BlockSpec((tm, d), row), pl.BlockSpec((1, d), fixed)],
        out_shape=[jax.ShapeDtypeStruct((t_len, d), F32), jax.ShapeDtypeStruct((1, d), F32)],
        compiler_params=_cparams(("arbitrary",)),
    )(*dprojs, dsmt, x, g, wp, wst, dx1)


def _fox_cum(smt, bias_col, n_batch, s_len, ck=256):
    ck = min(ck, s_len)

    def body(s_ref, b_ref, c_ref):
        tri = (lax.broadcasted_iota(jnp.int32, (ck, ck), 0) <= lax.broadcasted_iota(jnp.int32, (ck, ck), 1)).astype(F32)
        carry = jnp.zeros((SM_ROWS, 1), F32)
        for r in range(s_len // ck):
            ls = _log_sigmoid(s_ref[:, r * ck:(r + 1) * ck] + b_ref[...])
            c = jnp.dot(ls, tri, precision=lax.Precision.HIGHEST, preferred_element_type=F32) + carry
            c_ref[:, r * ck:(r + 1) * ck] = c
            carry = c[:, ck - 1:ck]

    return pl.pallas_call(
        body, name="fox_cum", grid=(n_batch,),
        in_specs=[pl.BlockSpec((SM_ROWS, s_len), lambda b: (0, b)), pl.BlockSpec((SM_ROWS, 1), lambda b: (0, 0))],
        out_specs=pl.BlockSpec((SM_ROWS, s_len), lambda b: (0, b)),
        out_shape=jax.ShapeDtypeStruct(smt.shape, F32),
        compiler_params=_cparams(("parallel",)),
    )(smt, bias_col)


def _fox_cum_bwd(dc, smt, bias_col, n_batch, s_len, ck=256):
    ck = min(ck, s_len)
    nr = s_len // ck

    def body(dc_ref, s_ref, b_ref, dl_ref, db_ref):
        b = pl.program_id(0)
        tri = (lax.broadcasted_iota(jnp.int32, (ck, ck), 0) >= lax.broadcasted_iota(jnp.int32, (ck, ck), 1)).astype(F32)
        carry = jnp.zeros((SM_ROWS, 1), F32)
        tot = jnp.zeros((SM_ROWS, 1), F32)
        for r in reversed(range(nr)):
            sl = slice(r * ck, (r + 1) * ck)
            dls = jnp.dot(dc_ref[:, sl], tri, precision=lax.Precision.HIGHEST, preferred_element_type=F32) + carry
            carry = dls[:, 0:1]
            dl = dls * (1.0 - _sigmoid(s_ref[:, sl] + b_ref[...]))
            dl_ref[:, sl] = dl
            tot = tot + jnp.sum(dl, axis=1, keepdims=True)

        @pl.when(b == 0)
        def _():
            db_ref[...] = jnp.zeros_like(db_ref)

        db_ref[...] += jnp.broadcast_to(tot, db_ref.shape)

    return pl.pallas_call(
        body, name="fox_cum_bwd", grid=(n_batch,),
        in_specs=[pl.BlockSpec((SM_ROWS, s_len), lambda b: (0, b)), pl.BlockSpec((SM_ROWS, s_len), lambda b: (0, b)),
                  pl.BlockSpec((SM_ROWS, 1), lambda b: (0, 0))],
        out_specs=[pl.BlockSpec((SM_ROWS, s_len), lambda b: (0, b)), pl.BlockSpec((SM_ROWS, LANES), lambda b: (0, 0))],
        out_shape=[jax.ShapeDtypeStruct(smt.shape, F32), jax.ShapeDtypeStruct((SM_ROWS, LANES), F32)],
        compiler_params=_cparams(("arbitrary",)),
    )(dc, smt, bias_col)


def _fox_diagonal_mask(tq):
    return lax.broadcasted_iota(jnp.int32, (tq, tq), 1) <= lax.broadcasted_iota(jnp.int32, (tq, tq), 0)


def _fox_fwd(pf, cb, gq2, gk2, go2, tq=256):
    n_batch, s_len, _ = pf.shape
    tq = min(tq, s_len)
    nq = s_len // tq
    scale = FOX_HEAD_DIM ** -0.5

    def body(q_ref, k_ref, v_ref, c_ref, gq_ref, gk_ref, go_ref, o_ref, on_ref, lse_ref, kh_ref, vh_ref):
        j = pl.program_id(1)
        i = pl.program_id(2)
        m0 = lax.broadcasted_iota(jnp.int32, (1, LANES), 1) < FOX_HEAD_DIM

        @pl.when(i == 0)
        def _():
            kn = _rms_pair(k_ref[0], gk_ref[...], m0)
            kh_ref[0] = jnp.where(m0, kn, 0.0).astype(BF16)
            kh_ref[1] = jnp.where(m0, 0.0, kn).astype(BF16)
            v = v_ref[0]
            vh_ref[0] = jnp.where(m0, v, 0.0).astype(BF16)
            vh_ref[1] = jnp.where(m0, 0.0, v).astype(BF16)

        qb = (_rms_pair(q_ref[0], gq_ref[...], m0) * scale).astype(BF16)

        def step(kb, carry, diagonal=False):
            ms, ls, acc = carry
            off = pl.multiple_of(kb * tq, tq)
            new_m, new_l, alphas, pv = [], [], [], []
            for hh in range(2):
                s = _dot(qb, kh_ref[hh, pl.ds(off, tq), :], NT)
                s = s - c_ref[0, kb, pl.ds(2 * j + hh, 1), :]
                if diagonal:
                    s = jnp.where(_fox_diagonal_mask(tq), s, NEG_INF)
                m_new = jnp.maximum(ms[hh], jnp.max(s, axis=-1, keepdims=True))
                alpha = jnp.exp(ms[hh] - m_new)
                p = jnp.exp(s - m_new)
                new_l.append(alpha * ls[hh] + jnp.sum(p, axis=-1, keepdims=True))
                new_m.append(m_new)
                alphas.append(alpha)
                pv.append(_dot(p, vh_ref[hh, pl.ds(off, tq), :], NN))
            acc = jnp.where(m0, alphas[0], alphas[1]) * acc + pv[0] + pv[1]
            return tuple(new_m), tuple(new_l), acc

        init_m = (jnp.full((tq, 1), NEG_INF, F32),) * 2
        init_l = (jnp.zeros((tq, 1), F32),) * 2
        carry = lax.fori_loop(0, i, step, (init_m, init_l, jnp.zeros((tq, LANES), F32)))
        ms, ls, acc = step(i, carry, diagonal=True)
        o = acc / jnp.where(m0, ls[0], ls[1])
        o_ref[0] = o
        on_ref[0] = _rms_pair(o, go_ref[...], m0).astype(BF16)
        lse_ref[0] = jnp.where(m0, ms[0] + jnp.log(ls[0]), ms[1] + jnp.log(ls[1]))

    fixed = lambda b, j, i: (0, 0)
    tile = lambda b, j, i: (b, i, j)
    return pl.pallas_call(
        body, name="fox_fwd", grid=(n_batch, 4, nq),
        in_specs=[pl.BlockSpec((1, tq, LANES), tile), pl.BlockSpec((1, s_len, LANES), lambda b, j, i: (b, 0, 4 + j)),
                  pl.BlockSpec((1, s_len, LANES), lambda b, j, i: (b, 0, 8 + j)),
                  pl.BlockSpec((1, nq, SM_ROWS, tq), lambda b, j, i: (b, 0, 0, 0)),
                  pl.BlockSpec((1, LANES), fixed), pl.BlockSpec((1, LANES), fixed), pl.BlockSpec((1, LANES), fixed)],
        out_specs=[pl.BlockSpec((1, tq, LANES), tile), pl.BlockSpec((1, tq, LANES), tile), pl.BlockSpec((1, tq, LANES), tile)],
        out_shape=[jax.ShapeDtypeStruct((n_batch, s_len, FOX_WIDTH), F32), jax.ShapeDtypeStruct((n_batch, s_len, FOX_WIDTH), BF16),
                   jax.ShapeDtypeStruct((n_batch, s_len, FOX_WIDTH), F32)],
        scratch_shapes=[pltpu.VMEM((2, s_len, LANES), BF16), pltpu.VMEM((2, s_len, LANES), BF16)],
        compiler_params=_cparams(("parallel", "parallel", "arbitrary")),
    )(pf, pf, pf, cb, gq2, gk2, go2)


def _fox_bwd(pf, cb, gq2, gk2, go2, o, lse, don, tq=256):
    n_batch, s_len, _ = pf.shape
    tq = min(tq, s_len)
    nq = s_len // tq
    scale = FOX_HEAD_DIM ** -0.5

    def body(q_ref, k_ref, v_ref, c_ref, gq_ref, gk_ref, go_ref, o_ref, lse_ref, don_ref,
             dq_ref, dk_ref, dv_ref, dc_ref, dgq_ref, dgk_ref, dgo_ref, kh_ref, vh_ref, dka_ref, dva_ref, dca_ref):
        b = pl.program_id(0)
        j = pl.program_id(1)
        i = pl.program_id(2)
        m0 = lax.broadcasted_iota(jnp.int32, (1, LANES), 1) < FOX_HEAD_DIM

        @pl.when((b == 0) & (j == 0) & (i == 0))
        def _():
            dgq_ref[...] = jnp.zeros_like(dgq_ref)
            dgk_ref[...] = jnp.zeros_like(dgk_ref)
            dgo_ref[...] = jnp.zeros_like(dgo_ref)

        @pl.when(i == 0)
        def _():
            kn = _rms_pair(k_ref[0], gk_ref[...], m0)
            kh_ref[0] = jnp.where(m0, kn, 0.0).astype(BF16)
            kh_ref[1] = jnp.where(m0, 0.0, kn).astype(BF16)
            v = v_ref[0]
            vh_ref[0] = jnp.where(m0, v, 0.0).astype(BF16)
            vh_ref[1] = jnp.where(m0, 0.0, v).astype(BF16)
            dka_ref[...] = jnp.zeros_like(dka_ref)
            dva_ref[...] = jnp.zeros_like(dva_ref)
            dca_ref[...] = jnp.zeros_like(dca_ref)

        q = q_ref[0]
        qn = _rms_pair(q, gq_ref[...], m0)
        qs = qn * scale
        qb = qs.astype(BF16)
        qh = (jnp.where(m0, qs, 0.0).astype(BF16), jnp.where(m0, 0.0, qs).astype(BF16))
        ot = o_ref[0]
        do, dgo = _rms_pair_bwd(ot, go_ref[...], don_ref[0], m0)
        dgo_ref[...] += dgo
        dd = do * ot
        delta = (jnp.sum(jnp.where(m0, dd, 0.0), axis=-1, keepdims=True), jnp.sum(jnp.where(m0, 0.0, dd), axis=-1, keepdims=True))
        doh = (jnp.where(m0, do, 0.0).astype(BF16), jnp.where(m0, 0.0, do).astype(BF16))
        lse_t = lse_ref[0]
        lse_h = (lse_t[:, 0:1], lse_t[:, FOX_HEAD_DIM:FOX_HEAD_DIM + 1])

        def step(kb, carry, diagonal=False):
            dqn, rs = carry
            rs = list(rs)
            off = pl.multiple_of(kb * tq, tq)
            for hh in range(2):
                kblk = kh_ref[hh, pl.ds(off, tq), :]
                vblk = vh_ref[hh, pl.ds(off, tq), :]
                s = _dot(qb, kblk, NT)
                s = s - c_ref[0, kb, pl.ds(2 * j + hh, 1), :]
                if diagonal:
                    s = jnp.where(_fox_diagonal_mask(tq), s, NEG_INF)
                p = jnp.exp(s - lse_h[hh])
                dp = _dot(doh[hh], vblk, NT)
                ds = p * (dp - delta[hh])
                dva_ref[pl.ds(off, tq), :] += _dot(p, doh[hh], TN)
                dka_ref[pl.ds(off, tq), :] += _dot(ds, qh[hh], TN)
                dca_ref[kb, hh:hh + 1, :] += -jnp.sum(ds, axis=0, keepdims=True)
                rs[hh] = rs[hh] + jnp.sum(ds, axis=-1, keepdims=True)
                dqn = dqn + _dot(ds, kblk, NN)
            return dqn, tuple(rs)

        carry = lax.fori_loop(0, i, step, (jnp.zeros((tq, LANES), F32), (jnp.zeros((tq, 1), F32),) * 2))
        dqn, rs = step(i, carry, diagonal=True)
        dqn = dqn * scale
        rs_rows = jnp.where(m0, rs[0], rs[1]).T
        dca_ref[i, 0:1, :] += rs_rows[0:1, :]
        dca_ref[i, 1:2, :] += rs_rows[FOX_HEAD_DIM:FOX_HEAD_DIM + 1, :]
        dq, dgq = _rms_pair_bwd(q, gq_ref[...], dqn, m0)
        dq_ref[0] = dq.astype(BF16)
        dgq_ref[...] += dgq

        @pl.when(i == nq - 1)
        def _():
            dk, dgk = _rms_pair_bwd(k_ref[0], gk_ref[...], dka_ref[...], m0)
            dk_ref[0] = dk.astype(BF16)
            dgk_ref[...] += dgk
            dv_ref[0] = dva_ref[...].astype(BF16)
            dc_ref[0, 0] = dca_ref[...]

    fixed = lambda b, j, i: (0, 0)
    tile = lambda b, j, i: (b, i, j)
    full = lambda b, j, i: (b, 0, j)
    wide = jax.ShapeDtypeStruct((n_batch, s_len, FOX_WIDTH), BF16)
    gain = jax.ShapeDtypeStruct((1, LANES), F32)
    return pl.pallas_call(
        body, name="fox_bwd", grid=(n_batch, 4, nq),
        in_specs=[pl.BlockSpec((1, tq, LANES), tile), pl.BlockSpec((1, s_len, LANES), lambda b, j, i: (b, 0, 4 + j)),
                  pl.BlockSpec((1, s_len, LANES), lambda b, j, i: (b, 0, 8 + j)),
                  pl.BlockSpec((1, nq, SM_ROWS, tq), lambda b, j, i: (b, 0, 0, 0)),
                  pl.BlockSpec((1, LANES), fixed), pl.BlockSpec((1, LANES), fixed), pl.BlockSpec((1, LANES), fixed),
                  pl.BlockSpec((1, tq, LANES), tile), pl.BlockSpec((1, tq, LANES), tile), pl.BlockSpec((1, tq, LANES), tile)],
        out_specs=[pl.BlockSpec((1, tq, LANES), tile), pl.BlockSpec((1, s_len, LANES), full), pl.BlockSpec((1, s_len, LANES), full),
                   pl.BlockSpec((1, 1, nq, 8, tq), lambda b, j, i: (b, j, 0, 0, 0)),
                   pl.BlockSpec((1, LANES), fixed), pl.BlockSpec((1, LANES), fixed), pl.BlockSpec((1, LANES), fixed)],
        out_shape=[wide, wide, wide, jax.ShapeDtypeStruct((n_batch, 4, nq, 8, tq), F32), gain, gain, gain],
        scratch_shapes=[pltpu.VMEM((2, s_len, LANES), BF16), pltpu.VMEM((2, s_len, LANES), BF16),
                        pltpu.VMEM((s_len, LANES), F32), pltpu.VMEM((s_len, LANES), F32), pltpu.VMEM((nq, 8, tq), F32)],
        compiler_params=_cparams(("arbitrary", "arbitrary", "arbitrary")),
    )(pf, pf, pf, cb, gq2, gk2, go2, o, lse, don)


def _shift_down(x, k):
    row = lax.broadcasted_iota(jnp.int32, x.shape, 0)
    return jnp.where(row >= k, pltpu.roll(x, k, 0), 0.0)


def _shift_up(x, k):
    n = x.shape[0]
    row = lax.broadcasted_iota(jnp.int32, x.shape, 0)
    return jnp.where(row < n - k, pltpu.roll(x, n - k, 0), 0.0)


def _conv_silu(x, w):
    y = w[3:4] * x + w[2:3] * _shift_down(x, 1) + w[1:2] * _shift_down(x, 2) + w[0:1] * _shift_down(x, 3)
    return y, y * _sigmoid(y)


def _gdn_pre(pg, conv_w):
    n_batch, s_len, width = pg.shape
    ncb = width // LANES

    def body(x_ref, w_ref, o_ref):
        cb = pl.program_id(1)
        _, s = _conv_silu(x_ref[0], w_ref[...])
        sn = s * lax.rsqrt(jnp.sum(s * s, axis=-1, keepdims=True) + EPS)
        o_ref[0] = jnp.where(cb < 2 * GDN_HEADS, sn, s)

    return pl.pallas_call(
        body, name="gdn_pre", grid=(n_batch, ncb),
        in_specs=[pl.BlockSpec((1, s_len, LANES), lambda b, c: (b, 0, c)), pl.BlockSpec((8, LANES), lambda b, c: (0, c))],
        out_specs=pl.BlockSpec((1, s_len, LANES), lambda b, c: (b, 0, c)),
        out_shape=jax.ShapeDtypeStruct(pg.shape, F32),
        compiler_params=_cparams(("parallel", "parallel")),
    )(pg, conv_w)


def _gdn_pre_bwd(pg, conv_w, dout):
    n_batch, s_len, width = pg.shape
    ncb = width // LANES

    def body(x_ref, w_ref, d_ref, dx_ref, dw_ref):
        cb = pl.program_id(0)
        b = pl.program_id(1)
        x = x_ref[0]
        w = w_ref[...]
        d = d_ref[0]
        y, s = _conv_silu(x, w)
        rr = lax.rsqrt(jnp.sum(s * s, axis=-1, keepdims=True) + EPS)
        sn = s * rr
        ds_n = rr * (d - sn * jnp.sum(d * sn, axis=-1, keepdims=True))
        ds = jnp.where(cb < 2 * GDN_HEADS, ds_n, d)
        sig = _sigmoid(y)
        dy = ds * (sig * (1.0 + y * (1.0 - sig)))
        dx = w[3:4] * dy + w[2:3] * _shift_up(dy, 1) + w[1:2] * _shift_up(dy, 2) + w[0:1] * _shift_up(dy, 3)
        dx_ref[0] = dx.astype(BF16)
        dw = [jnp.sum(dy * _shift_down(x, 3 - jj), axis=0, keepdims=True) if jj < 3 else jnp.sum(dy * x, axis=0, keepdims=True)
              for jj in range(CONV_WIDTH)]
        rows = lax.broadcasted_iota(jnp.int32, (8, LANES), 0)
        dwb = jnp.zeros((8, LANES), F32)
        for jj in range(CONV_WIDTH):
            dwb = dwb + jnp.where(rows == jj, dw[jj], 0.0)

        @pl.when(b == 0)
        def _():
            dw_ref[...] = jnp.zeros_like(dw_ref)

        dw_ref[...] += dwb

    blk = lambda c, b: (b, 0, c)
    return pl.pallas_call(
        body, name="gdn_pre_bwd", grid=(ncb, n_batch),
        in_specs=[pl.BlockSpec((1, s_len, LANES), blk), pl.BlockSpec((8, LANES), lambda c, b: (0, c)), pl.BlockSpec((1, s_len, LANES), blk)],
        out_specs=[pl.BlockSpec((1, s_len, LANES), blk), pl.BlockSpec((8, LANES), lambda c, b: (0, c))],
        out_shape=[jax.ShapeDtypeStruct(pg.shape, BF16), jax.ShapeDtypeStruct((8, width), F32)],
        compiler_params=_cparams(("parallel", "arbitrary")),
    )(pg, conv_w, dout)


def _gdn_gates(smc, smr, a_c, dt_c, a_r, dt_r, h):
    lane = lax.broadcasted_iota(jnp.int32, (1, LANES), 1)
    sub = lax.broadcasted_iota(jnp.int32, (SM_ROWS, 1), 0)
    beta_c = jnp.sum(jnp.where(lane == SM_B + h, _sigmoid(smc), 0.0), axis=1, keepdims=True)
    g_all_c = -jnp.exp(a_c) * _softplus(smc + dt_c)
    g_c = jnp.sum(jnp.where(lane == SM_A + h, g_all_c, 0.0), axis=1, keepdims=True)
    g_all_r = -jnp.exp(a_r) * _softplus(smr + dt_r)
    g_r = jnp.sum(jnp.where(sub == SM_A + h, g_all_r, 0.0), axis=0, keepdims=True)
    return beta_c, g_c, g_r


def _gdn_group(qkv, z, smc, smr, a_c, dt_c, a_r, dt_r, go, states):
    n_grp = len(qkv)
    c = qkv[0].shape[0]
    hd = GDN_HEAD_DIM
    pairs = [(g, h) for g in range(n_grp) for h in range(GDN_HEADS)]
    ii = lax.broadcasted_iota(jnp.int32, (c, c), 0)
    jj = lax.broadcasted_iota(jnp.int32, (c, c), 1)
    incl = ii >= jj
    col = lambda arr, base, h: arr[:, base + h * hd:base + (h + 1) * hd]

    qs, ks, kbs, vbs, decays, gcs, g_lasts, amats = [], [], [], [], [], [], [], []
    for g, h in pairs:
        beta_c, g_c, g_r = _gdn_gates(smc[g], smr[g], a_c, dt_c, a_r, dt_r, h)
        gc_c = jnp.sum(jnp.where(incl, g_r, 0.0), axis=1, keepdims=True)
        gc_r = jnp.sum(jnp.where(ii <= jj, g_c, 0.0), axis=0, keepdims=True)
        decay = jnp.where(incl, jnp.exp(jnp.where(incl, gc_c - gc_r, 0.0)), 0.0)
        k = col(qkv[g], GDN_WIDTH, h)
        kb = k * beta_c
        qs.append(col(qkv[g], 0, h) * (hd ** -0.5))
        ks.append(k)
        kbs.append(kb)
        vbs.append(col(qkv[g], 2 * GDN_WIDTH, h) * beta_c)
        decays.append(decay)
        gcs.append(gc_c)
        g_lasts.append(jnp.sum(g_c, axis=0, keepdims=True))
        amats.append(jnp.where(ii > jj, _mm_nt(kb, k) * decay, 0.0))
    ts = _unit_lower_inverses(amats)
    egcs = [jnp.exp(gc) for gc in gcs]
    us = [_mm_nn(t, vb) for t, vb in zip(ts, vbs)]
    ws = [_mm_nn(t, kb * e) for t, kb, e in zip(ts, kbs, egcs)]
    intras = [_mm_nt(q, k) * d for q, k, d in zip(qs, ks, decays)]
    qes = [q * e for q, e in zip(qs, egcs)]
    kds = [k * jnp.exp(gl - gc) for k, gl, gc in zip(ks, g_lasts, gcs)]
    sdecs = [jnp.exp(gl) for gl in g_lasts]

    outs = []
    for g in range(n_grp):
        idx = [g * GDN_HEADS + h for h in range(GDN_HEADS)]
        v_new = [us[i] - _mm_nn(ws[i], states[h]) for h, i in enumerate(idx)]
        o_state = [_mm_nn(qes[i], states[h]) for h, i in enumerate(idx)]
        o_intra = [_mm_nn(intras[i], v_new[h]) for h, i in enumerate(idx)]
        states = [states[h] * sdecs[i] + _mm_tn(kds[i], v_new[h]) for h, i in enumerate(idx)]
        outs.append([_rms(o_state[h] + o_intra[h], go) * (col(z[g], 0, h) * _sigmoid(col(z[g], 0, h))) for h in range(GDN_HEADS)])
    return outs, states


def _gdn_group_size(n_chunks):
    return GDN_GROUP if n_chunks % GDN_GROUP == 0 else 1


def _gdn_fwd(qkvn, z, smc, smr, a_c, dt_c, a_r, dt_r, go):
    n_batch, s_len, _ = qkvn.shape
    c = GDN_CHUNK
    n = s_len // c
    grp = _gdn_group_size(n)
    ng = n // grp
    gc = grp * c
    hd = GDN_HEAD_DIM

    def body(qkv_ref, z_ref, smc_ref, smr_ref, ac_ref, dc_ref, ar_ref, dr_ref, go_ref, og_ref, st_ref, s_ref):
        @pl.when(pl.program_id(1) == 0)
        def _():
            s_ref[...] = jnp.zeros_like(s_ref)

        states = [s_ref[h] for h in range(GDN_HEADS)]
        for h in range(GDN_HEADS):
            st_ref[0, 0, h] = states[h]
        rows = lambda k: slice(k * c, (k + 1) * c)
        outs, nxt = _gdn_group([qkv_ref[0, rows(k), :] for k in range(grp)], [z_ref[0, rows(k), :] for k in range(grp)],
                               [smc_ref[0, rows(k), :] for k in range(grp)], [smr_ref[k] for k in range(grp)],
                               ac_ref[...], dc_ref[...], ar_ref[...], dr_ref[...], go_ref[...], states)
        for k in range(grp):
            for h in range(GDN_HEADS):
                og_ref[0, rows(k), h * hd:(h + 1) * hd] = outs[k][h].astype(BF16)
        for h in range(GDN_HEADS):
            s_ref[h] = nxt[h]

    tok = lambda b, i: (b, i, 0)
    fixed = lambda b, i: (0, 0)
    return pl.pallas_call(
        body, name="gdn_fwd", grid=(n_batch, ng),
        in_specs=[pl.BlockSpec((1, gc, 3 * GDN_WIDTH), tok), pl.BlockSpec((1, gc, GDN_WIDTH), tok), pl.BlockSpec((1, gc, LANES), tok),
                  pl.BlockSpec((grp, SM_ROWS, c), lambda b, i: (b * ng + i, 0, 0)),
                  pl.BlockSpec((1, LANES), fixed), pl.BlockSpec((1, LANES), fixed), pl.BlockSpec((SM_ROWS, 1), fixed),
                  pl.BlockSpec((SM_ROWS, 1), fixed), pl.BlockSpec((1, LANES), fixed)],
        out_specs=[pl.BlockSpec((1, gc, GDN_WIDTH), tok), pl.BlockSpec((1, 1, GDN_HEADS, hd, hd), lambda b, i: (b, i, 0, 0, 0))],
        out_shape=[jax.ShapeDtypeStruct((n_batch, s_len, GDN_WIDTH), BF16), jax.ShapeDtypeStruct((n_batch, ng, GDN_HEADS, hd, hd), F32)],
        scratch_shapes=[pltpu.VMEM((GDN_HEADS, hd, hd), F32)],
        compiler_params=_cparams(("parallel", "arbitrary")),
    )(qkvn, z, smc, smr, a_c, dt_c, a_r, dt_r, go)


def _gdn_bwd(qkvn, z, smc, smr, a_c, dt_c, a_r, dt_r, go, states, dog):
    n_batch, s_len, _ = qkvn.shape
    c = GDN_CHUNK
    n = s_len // c
    grp = _gdn_group_size(n)
    ng = n // grp
    gc = grp * c
    hd = GDN_HEAD_DIM

    def body(qkv_ref, z_ref, smc_ref, smr_ref, ac_ref, dc_ref, ar_ref, dr_ref, go_ref, st_ref, dog_ref,
             dqkv_ref, dz_ref, dsmc_ref, dsmr_ref, dac_ref, ddc_ref, dar_ref, ddr_ref, dgo_ref, ds_ref):
        first = (pl.program_id(0) == 0) & (pl.program_id(1) == 0)

        @pl.when(pl.program_id(1) == 0)
        def _():
            ds_ref[...] = jnp.zeros_like(ds_ref)

        @pl.when(first)
        def _():
            for r in (dac_ref, ddc_ref, dar_ref, ddr_ref, dgo_ref):
                r[...] = jnp.zeros_like(r)

        rows = lambda k: slice(k * c, (k + 1) * c)
        states = [st_ref[0, 0, h] for h in range(GDN_HEADS)]
        prim = ([qkv_ref[0, rows(k), :] for k in range(grp)], [z_ref[0, rows(k), :] for k in range(grp)],
                [smc_ref[0, rows(k), :] for k in range(grp)], [smr_ref[k] for k in range(grp)],
                ac_ref[...], dc_ref[...], ar_ref[...], dr_ref[...], go_ref[...], states)
        _, vjp = jax.vjp(_gdn_group, *prim)
        cot = ([[dog_ref[0, rows(k), h * hd:(h + 1) * hd] for h in range(GDN_HEADS)] for k in range(grp)],
               [ds_ref[h] for h in range(GDN_HEADS)])
        dqkv, dz, dsmc, dsmr, dac, ddc, dar, ddr, dgo, dstates = vjp(cot)
        for k in range(grp):
            dqkv_ref[0, rows(k), :] = dqkv[k]
            dz_ref[0, rows(k), :] = dz[k].astype(BF16)
            dsmc_ref[0, rows(k), :] = dsmc[k]
            dsmr_ref[k] = dsmr[k]
        dac_ref[...] += dac
        ddc_ref[...] += ddc
        dar_ref[...] += dar
        ddr_ref[...] += ddr
        dgo_ref[...] += dgo
        for h in range(GDN_HEADS):
            ds_ref[h] = dstates[h]

    tok = lambda b, i: (b, ng - 1 - i, 0)
    fixed = lambda b, i: (0, 0)
    lane_vec = jax.ShapeDtypeStruct((1, LANES), F32)
    row_vec = jax.ShapeDtypeStruct((SM_ROWS, 1), F32)
    return pl.pallas_call(
        body, name="gdn_bwd", grid=(n_batch, ng),
        in_specs=[pl.BlockSpec((1, gc, 3 * GDN_WIDTH), tok), pl.BlockSpec((1, gc, GDN_WIDTH), tok), pl.BlockSpec((1, gc, LANES), tok),
                  pl.BlockSpec((grp, SM_ROWS, c), lambda b, i: (b * ng + ng - 1 - i, 0, 0)),
                  pl.BlockSpec((1, LANES), fixed), pl.BlockSpec((1, LANES), fixed), pl.BlockSpec((SM_ROWS, 1), fixed),
                  pl.BlockSpec((SM_ROWS, 1), fixed), pl.BlockSpec((1, LANES), fixed),
                  pl.BlockSpec((1, 1, GDN_HEADS, hd, hd), lambda b, i: (b, ng - 1 - i, 0, 0, 0)),
                  pl.BlockSpec((1, gc, GDN_WIDTH), lambda b, i: (b, ng - 1 - i, 1))],
        out_specs=[pl.BlockSpec((1, gc, 3 * GDN_WIDTH), tok), pl.BlockSpec((1, gc, GDN_WIDTH), tok), pl.BlockSpec((1, gc, LANES), tok),
                   pl.BlockSpec((grp, SM_ROWS, c), lambda b, i: (b * ng + ng - 1 - i, 0, 0)),
                   pl.BlockSpec((1, LANES), fixed), pl.BlockSpec((1, LANES), fixed), pl.BlockSpec((SM_ROWS, 1), fixed),
                   pl.BlockSpec((SM_ROWS, 1), fixed), pl.BlockSpec((1, LANES), fixed)],
        out_shape=[jax.ShapeDtypeStruct((n_batch, s_len, 3 * GDN_WIDTH), F32), jax.ShapeDtypeStruct((n_batch, s_len, GDN_WIDTH), BF16),
                   jax.ShapeDtypeStruct((n_batch, s_len, LANES), F32), jax.ShapeDtypeStruct((n_batch * n, SM_ROWS, c), F32),
                   lane_vec, lane_vec, row_vec, row_vec, lane_vec],
        scratch_shapes=[pltpu.VMEM((GDN_HEADS, hd, hd), F32)],
        compiler_params=_cparams(("arbitrary", "arbitrary")),
    )(qkvn, z, smc, smr, a_c, dt_c, a_r, dt_r, go, states, dog)


def _out_proj(x, oa, ob, w_out, g_x, w_cq, tm=256):
    t_len, d = x.shape
    tm = min(tm, t_len)

    def body(x_ref, oa_ref, ob_ref, wo_ref, g_ref, wq_ref, x1_ref, hq_ref, cq_ref):
        x1 = x_ref[...] + _dot(oa_ref[...], wo_ref[0:FOX_WIDTH, :]) + _dot(ob_ref[...], wo_ref[FOX_WIDTH:2 * FOX_WIDTH, :])
        x1_ref[...] = x1
        hq = _rms(x1, g_ref[...]).astype(BF16)
        hq_ref[...] = hq
        cq_ref[...] = _dot(hq, wq_ref[...])

    row = lambda i: (i, 0)
    fixed = lambda i: (0, 0)
    return pl.pallas_call(
        body, name="out_proj", grid=(t_len // tm,),
        in_specs=[pl.BlockSpec((tm, d), row), pl.BlockSpec((tm, FOX_WIDTH), row), pl.BlockSpec((tm, GDN_WIDTH), row),
                  pl.BlockSpec((d, d), fixed), pl.BlockSpec((1, d), fixed), pl.BlockSpec((d, XATTN_WIDTH), fixed)],
        out_specs=[pl.BlockSpec((tm, d), row), pl.BlockSpec((tm, d), row), pl.BlockSpec((tm, XATTN_WIDTH), row)],
        out_shape=[jax.ShapeDtypeStruct((t_len, d), F32), jax.ShapeDtypeStruct((t_len, d), BF16), jax.ShapeDtypeStruct((t_len, XATTN_WIDTH), F32)],
        compiler_params=_cparams(("parallel",)),
    )(x, oa, ob, w_out, g_x, w_cq)


def _out_proj_bwd(dx1, w_out, tm=512):
    t_len, d = dx1.shape
    tm = min(tm, t_len)

    def body(dx_ref, w_ref, o_ref):
        o_ref[...] = _dot(dx_ref[...], w_ref[...], NT)

    return pl.pallas_call(
        body, name="out_proj_bwd", grid=(t_len // tm,),
        in_specs=[pl.BlockSpec((tm, d), lambda i: (i, 0)), pl.BlockSpec((d, d), lambda i: (0, 0))],
        out_specs=pl.BlockSpec((tm, d), lambda i: (i, 0)),
        out_shape=jax.ShapeDtypeStruct((t_len, d), F32),
        compiler_params=_cparams(("parallel",)),
    )(dx1, w_out)


def _mem_kv(mem, g, w_ckv, tm=256):
    t_len, d = mem.shape
    tm = min(tm, t_len)

    def body(x_ref, g_ref, w_ref, h_ref, o_ref):
        h = _rms(x_ref[...], g_ref[...]).astype(BF16)
        h_ref[...] = h
        o_ref[...] = _dot(h, w_ref[...])

    row = lambda i: (i, 0)
    fixed = lambda i: (0, 0)
    return pl.pallas_call(
        body, name="mem_kv", grid=(t_len // tm,),
        in_specs=[pl.BlockSpec((tm, d), row), pl.BlockSpec((1, d), fixed), pl.BlockSpec((d, 2 * XATTN_WIDTH), fixed)],
        out_specs=[pl.BlockSpec((tm, d), row), pl.BlockSpec((tm, 2 * XATTN_WIDTH), row)],
        out_shape=[jax.ShapeDtypeStruct((t_len, d), BF16), jax.ShapeDtypeStruct((t_len, 2 * XATTN_WIDTH), F32)],
        compiler_params=_cparams(("parallel",)),
    )(mem, g, w_ckv)


def _mem_kv_bwd(dckv, mem, g, w_ckv, tm=256):
    t_len, d = mem.shape
    tm = min(tm, t_len)

    def body(d_ref, x_ref, g_ref, w_ref, dg_ref):
        @pl.when(pl.program_id(0) == 0)
        def _():
            dg_ref[...] = jnp.zeros_like(dg_ref)

        dh = _dot(d_ref[...], w_ref[...], NT)
        _, dg = _rms_bwd(x_ref[...], g_ref[...], dh)
        dg_ref[...] += dg

    row = lambda i: (i, 0)
    fixed = lambda i: (0, 0)
    return pl.pallas_call(
        body, name="mem_kv_bwd", grid=(t_len // tm,),
        in_specs=[pl.BlockSpec((tm, 2 * XATTN_WIDTH), row), pl.BlockSpec((tm, d), row), pl.BlockSpec((1, d), fixed),
                  pl.BlockSpec((d, 2 * XATTN_WIDTH), fixed)],
        out_specs=pl.BlockSpec((1, d), fixed),
        out_shape=jax.ShapeDtypeStruct((1, d), F32),
        compiler_params=_cparams(("arbitrary",)),
    )(dckv, mem, g, w_ckv)


def _xattn_probs(qn, kn):
    s = _dot(qn, kn, NT) * (XATTN_HEAD_DIM ** -0.5)
    p = jnp.exp(s - jnp.max(s, axis=-1, keepdims=True))
    return p / jnp.sum(p, axis=-1, keepdims=True)


def _xattn_fwd(cq, ckv, x1, gq, gk, w_co, g_mlp, n_batch, s_len, m_len, tq=512):
    d = x1.shape[1]
    tq = min(tq, s_len)
    nq = s_len // tq
    hd = XATTN_HEAD_DIM

    def body(cq_ref, kv_ref, x1_ref, gq_ref, gk_ref, wo_ref, gm_ref, co_ref, x2_ref, hf_ref):
        outs = []
        for h in range(XATTN_HEADS):
            qn = _rms(cq_ref[:, h * hd:(h + 1) * hd], gq_ref[...])
            kn = _rms(kv_ref[:, h * hd:(h + 1) * hd], gk_ref[...])
            p = _xattn_probs(qn, kn)
            outs.append(_dot(p, kv_ref[:, XATTN_WIDTH + h * hd:XATTN_WIDTH + (h + 1) * hd]).astype(BF16))
        for h in range(XATTN_HEADS):
            co_ref[:, h * hd:(h + 1) * hd] = outs[h]
        x2 = x1_ref[...] + _dot(co_ref[...], wo_ref[...])
        x2_ref[...] = x2
        hf_ref[...] = _rms(x2, gm_ref[...]).astype(BF16)

    row = lambda b, i: (b * nq + i, 0)
    fixed = lambda b, i: (0, 0)
    t_len = n_batch * s_len
    return pl.pallas_call(
        body, name="xattn_fwd", grid=(n_batch, nq),
        in_specs=[pl.BlockSpec((tq, XATTN_WIDTH), row), pl.BlockSpec((m_len, 2 * XATTN_WIDTH), lambda b, i: (b, 0)),
                  pl.BlockSpec((tq, d), row), pl.BlockSpec((1, hd), fixed), pl.BlockSpec((1, hd), fixed),
                  pl.BlockSpec((XATTN_WIDTH, d), fixed), pl.BlockSpec((1, d), fixed)],
        out_specs=[pl.BlockSpec((tq, XATTN_WIDTH), row), pl.BlockSpec((tq, d), row), pl.BlockSpec((tq, d), row)],
        out_shape=[jax.ShapeDtypeStruct((t_len, XATTN_WIDTH), BF16), jax.ShapeDtypeStruct((t_len, d), F32),
                   jax.ShapeDtypeStruct((t_len, d), BF16)],
        compiler_params=_cparams(("parallel", "parallel")),
    )(cq, ckv, x1, gq, gk, w_co, g_mlp)


def _xattn_bwd(dx2, cq, ckv, x1, gq, gk, w_co, g_x, w_cq, n_batch, s_len, m_len, tq=512):
    d = x1.shape[1]
    tq = min(tq, s_len)
    nq = s_len // tq
    hd = XATTN_HEAD_DIM
    scale = XATTN_HEAD_DIM ** -0.5

    def body(dx2_ref, cq_ref, kv_ref, x1_ref, gq_ref, gk_ref, wo_ref, gx_ref, wq_ref,
             dx1_ref, dcq_ref, dkv_ref, dgq_ref, dgk_ref, dgx_ref, dk_acc, dv_acc):
        b = pl.program_id(0)
        i = pl.program_id(1)

        @pl.when((b == 0) & (i == 0))
        def _():
            dgq_ref[...] = jnp.zeros_like(dgq_ref)
            dgk_ref[...] = jnp.zeros_like(dgk_ref)
            dgx_ref[...] = jnp.zeros_like(dgx_ref)

        @pl.when(i == 0)
        def _():
            dk_acc[...] = jnp.zeros_like(dk_acc)
            dv_acc[...] = jnp.zeros_like(dv_acc)

        dx2 = dx2_ref[...]
        dco_all = _dot(dx2, wo_ref[...], NT)
        for h in range(XATTN_HEADS):
            sl = slice(h * hd, (h + 1) * hd)
            q = cq_ref[:, sl]
            qn = _rms(q, gq_ref[...])
            kn = _rms(kv_ref[:, sl], gk_ref[...])
            v = kv_ref[:, XATTN_WIDTH + h * hd:XATTN_WIDTH + (h + 1) * hd]
            p = _xattn_probs(qn, kn)
            dco = dco_all[:, sl]
            dv_acc[:, sl] += _dot(p, dco, TN)
            dp = _dot(dco, v, NT)
            ds = p * (dp - jnp.sum(dp * p, axis=-1, keepdims=True))
            dqn = _dot(ds, kn) * scale
            dk_acc[:, sl] += _dot(ds, qn, TN) * scale
            dq, dgq = _rms_bwd(q, gq_ref[...], dqn)
            dgq_ref[...] += dgq
            dcq_ref[:, sl] = dq.astype(BF16)
        dhq = _dot(dcq_ref[...], wq_ref[...], NT)
        dxn, dgx = _rms_bwd(x1_ref[...], gx_ref[...], dhq)
        dgx_ref[...] += dgx
        dx1_ref[...] = dx2 + dxn

        @pl.when(i == nq - 1)
        def _():
            for h in range(XATTN_HEADS):
                sl = slice(h * hd, (h + 1) * hd)
                dk, dgk = _rms_bwd(kv_ref[:, sl], gk_ref[...], dk_acc[:, sl])
                dgk_ref[...] += dgk
                dkv_ref[:, sl] = dk.astype(BF16)
                dkv_ref[:, XATTN_WIDTH + h * hd:XATTN_WIDTH + (h + 1) * hd] = dv_acc[:, sl].astype(BF16)

    row = lambda b, i: (b * nq + i, 0)
    fixed = lambda b, i: (0, 0)
    t_len = n_batch * s_len
    return pl.pallas_call(
        body, name="xattn_bwd", grid=(n_batch, nq),
        in_specs=[pl.BlockSpec((tq, d), row), pl.BlockSpec((tq, XATTN_WIDTH), row), pl.BlockSpec((m_len, 2 * XATTN_WIDTH), lambda b, i: (b, 0)),
                  pl.BlockSpec((tq, d), row), pl.BlockSpec((1, hd), fixed), pl.BlockSpec((1, hd), fixed),
                  pl.BlockSpec((XATTN_WIDTH, d), fixed), pl.BlockSpec((1, d), fixed), pl.BlockSpec((d, XATTN_WIDTH), fixed)],
        out_specs=[pl.BlockSpec((tq, d), row), pl.BlockSpec((tq, XATTN_WIDTH), row), pl.BlockSpec((m_len, 2 * XATTN_WIDTH), lambda b, i: (b, 0)),
                   pl.BlockSpec((1, hd), fixed), pl.BlockSpec((1, hd), fixed), pl.BlockSpec((1, d), fixed)],
        out_shape=[jax.ShapeDtypeStruct((t_len, d), F32), jax.ShapeDtypeStruct((t_len, XATTN_WIDTH), BF16),
                   jax.ShapeDtypeStruct((n_batch * m_len, 2 * XATTN_WIDTH), BF16),
                   jax.ShapeDtypeStruct((1, hd), F32), jax.ShapeDtypeStruct((1, hd), F32), jax.ShapeDtypeStruct((1, d), F32)],
        scratch_shapes=[pltpu.VMEM((m_len, XATTN_WIDTH), F32), pltpu.VMEM((m_len, XATTN_WIDTH), F32)],
        compiler_params=_cparams(("arbitrary", "arbitrary")),
    )(dx2, cq, ckv, x1, gq, gk, w_co, g_x, w_cq)


def _resident(shape):
    return pl.BlockSpec(shape, lambda *_: (0,) * len(shape), pipeline_mode=pl.Buffered(1))


def _mlp_fwd(hf, x2, target, w1, w2, tm=256, tf=1024):
    t_len, d = x2.shape
    f = w1.shape[1]
    tm, tf = min(tm, t_len), min(tf, f)

    def body(hf_ref, x2_ref, tg_ref, w1_ref, w2_ref, u_ref, a_ref, dy_ref, ls_ref):
        hf_t = hf_ref[...]
        for k in range(f // tf):
            cols = slice(k * tf, (k + 1) * tf)
            u = _dot(hf_t, w1_ref[:, cols])
            u_ref[:, cols] = u
            r = jnp.maximum(u, 0.0)
            a_ref[:, cols] = (r * r).astype(BF16)
        y = x2_ref[...] + _dot(a_ref[...], w2_ref[...])
        err = y - tg_ref[...]
        dy_ref[...] = err * (1.0 / d)
        ls_ref[...] = jnp.broadcast_to(jnp.sum(jnp.sum(err * err, axis=-1, keepdims=True) * (1.0 / d), axis=0, keepdims=True), ls_ref.shape)

    row = lambda i: (i, 0)
    return pl.pallas_call(
        body, name="mlp_fwd", grid=(t_len // tm,),
        in_specs=[pl.BlockSpec((tm, d), row), pl.BlockSpec((tm, d), row), pl.BlockSpec((tm, d), row), _resident((d, f)), _resident((f, d))],
        out_specs=[pl.BlockSpec((tm, f), row), pl.BlockSpec((tm, f), row), pl.BlockSpec((tm, d), row),
                   pl.BlockSpec((1, 8, LANES), lambda i: (i, 0, 0))],
        out_shape=[jax.ShapeDtypeStruct((t_len, f), F32), jax.ShapeDtypeStruct((t_len, f), BF16), jax.ShapeDtypeStruct((t_len, d), F32),
                   jax.ShapeDtypeStruct((t_len // tm, 8, LANES), F32)],
        compiler_params=_cparams(("parallel",)),
    )(hf, x2, target, w1, w2)


def _mlp_bwd(dy, u, x2, g, w1, w2, tm=256, tf=1024):
    t_len, d = x2.shape
    f = w1.shape[1]
    tm, tf = min(tm, t_len), min(tf, f)

    def body(dy_ref, u_ref, x2_ref, g_ref, w1_ref, w2_ref, du_ref, dx2_ref, dg_ref):
        @pl.when(pl.program_id(0) == 0)
        def _():
            dg_ref[...] = jnp.zeros_like(dg_ref)

        dy_t = dy_ref[...]
        dyb = dy_t.astype(BF16)
        for k in range(f // tf):
            cols = slice(k * tf, (k + 1) * tf)
            da = _dot(dyb, w2_ref[cols, :], NT)
            du_ref[:, cols] = (da * (2.0 * jnp.maximum(u_ref[:, cols], 0.0))).astype(BF16)
        dhf = _dot(du_ref[...], w1_ref[...], NT)
        dxn, dg = _rms_bwd(x2_ref[...], g_ref[...], dhf)
        dx2_ref[...] = dy_t + dxn
        dg_ref[...] += dg

    row = lambda i: (i, 0)
    fixed = lambda i: (0, 0)
    return pl.pallas_call(
        body, name="mlp_bwd", grid=(t_len // tm,),
        in_specs=[pl.BlockSpec((tm, d), row), pl.BlockSpec((tm, f), row), pl.BlockSpec((tm, d), row), pl.BlockSpec((1, d), fixed),
                  _resident((d, f)), _resident((f, d))],
        out_specs=[pl.BlockSpec((tm, f), row), pl.BlockSpec((tm, d), row), pl.BlockSpec((1, d), fixed)],
        out_shape=[jax.ShapeDtypeStruct((t_len, f), BF16), jax.ShapeDtypeStruct((t_len, d), F32), jax.ShapeDtypeStruct((1, d), F32)],
        compiler_params=_cparams(("arbitrary",)),
    )(dy, u, x2, g, w1, w2)


def _pad_lanes(v, offset=0, width=LANES):
    return jnp.zeros((1, width), F32).at[:, offset:offset + v.shape[1]].set(v)


def _col(v, offset=0, rows=SM_ROWS):
    return jnp.zeros((rows, 1), F32).at[offset:offset + v.shape[1], 0].set(v[0])


def _pack_small(g_mix, dgq, dgk, dbias, dgo, dac, dar, ddc, ddr, g_gdn_o, g_nx, g_mem, g_xq, g_xk, g_mlp, loss_tiles):
    def body(mix_ref, q_ref, k_ref, b_ref, o_ref, ac_ref, ar_ref, dc_ref, dr_ref, go_ref, nx_ref, mem_ref, xq_ref, xk_ref,
             mlp_ref, lt_ref, out_ref):
        lane = lax.broadcasted_iota(jnp.int32, (1, LANES), 1)
        diag = lax.broadcasted_iota(jnp.int32, (SM_ROWS, LANES), 0) == lax.broadcasted_iota(jnp.int32, (SM_ROWS, LANES), 1)

        def rolled(v, shift):
            return pltpu.roll(jnp.broadcast_to(v, (8, LANES)), shift, 1)[0:1, :]

        def rows_to_lanes(col):
            return jnp.sum(jnp.where(diag, col, 0.0), axis=0, keepdims=True)

        def put(row, v, n):
            out_ref[row:row + 1, 0:LANES] = jnp.where(lane < n, v, 0.0)

        out_ref[...] = jnp.zeros_like(out_ref)
        out_ref[0:1, :] = mix_ref[...]
        for row, ref in ((1, q_ref), (2, k_ref), (4, o_ref)):
            put(row, ref[...] + rolled(ref[...], FOX_HEAD_DIM), FOX_HEAD_DIM)
        put(3, rows_to_lanes(b_ref[...]), FOX_HEADS)
        for row, lane_ref, row_ref in ((5, ac_ref, ar_ref), (6, dc_ref, dr_ref)):
            put(row, rolled(lane_ref[...] + rows_to_lanes(row_ref[...]), LANES - SM_A), GDN_HEADS)
        put(7, go_ref[...], LANES)
        out_ref[8:9, :] = nx_ref[...]
        out_ref[9:10, :] = mem_ref[...]
        put(10, xq_ref[...], LANES)
        put(11, xk_ref[...], LANES)
        out_ref[12:13, :] = mlp_ref[...]
        put(LOSS_ROW, 0.5 * jnp.sum(lt_ref[...], axis=0)[0:1, :], 1)

    args = (g_mix, dgq, dgk, dbias, dgo, dac, dar, ddc, ddr, g_gdn_o, g_nx, g_mem, g_xq, g_xk, g_mlp, loss_tiles)
    return pl.pallas_call(body, name="pack_small", out_shape=jax.ShapeDtypeStruct((PACK_ROWS, D_MODEL), F32))(*args)


LATE_WEIGHTS = (("w_out", "w_cq", "w_ckv", "w_co"), ("w_mlp1", "w_mlp2"))
GRAD_GROUPS = (("w_mlp2", "w_mlp1"), ("w_co", "w_cq", "w_ckv", "w_out"), ("w_in", "gdn_conv_w"))


def _local_step(x, mem, target, norm_mix_g, w_in, fox_qnorm_g, fox_knorm_g, fox_f_bias, fox_onorm_g, gdn_conv_w, gdn_A_log,
                gdn_dt_bias, gdn_onorm_g, norm_xattn_g, mem_norm_g, xattn_qnorm_g, xattn_knorm_g, norm_mlp_g,
                late_weights, grads_ready=None, first_token=0.0):
    if grads_ready is None:
        grads_ready = lambda group: 0.0
    n_batch, s_len, d = x.shape
    m_len = mem.shape[1]
    t_len = n_batch * s_len
    tq = min(FOX_BLOCK, s_len)
    nq = s_len // tq
    n_chunks = s_len // GDN_CHUNK
    x2d = x.reshape(t_len, d)

    wp = jnp.concatenate([w_in[0:1536], w_in[1544:3080], w_in[3088:3600], w_in[1536:1544], w_in[3080:3088],
                          jnp.zeros((P_DIM - 3600, d), BF16)], axis=0)
    wst = jnp.concatenate([w_in[1536:1544], w_in[3080:3088]], axis=0)
    conv_w = jnp.concatenate([gdn_conv_w, jnp.zeros((8 - CONV_WIDTH, gdn_conv_w.shape[1]), F32)], axis=0)
    bias_col = _col(fox_f_bias, SM_F)
    gq2, gk2, go2 = (jnp.tile(g, (1, 2)) for g in (fox_qnorm_g, fox_knorm_g, fox_onorm_g))
    a_c, dt_c = _pad_lanes(gdn_A_log, SM_A), _pad_lanes(gdn_dt_bias, SM_A)
    a_r, dt_r = _col(gdn_A_log, SM_A), _col(gdn_dt_bias, SM_A)

    h1, pfox, pgdn, pz, sm, smt = _in_proj(x2d, norm_mix_g + first_token, wp, wst)
    c_rows = _fox_cum(smt, bias_col, n_batch, s_len)
    cb = c_rows.reshape(SM_ROWS, n_batch, nq, tq).transpose(1, 2, 0, 3)
    pf3 = pfox.reshape(n_batch, s_len, 1536)
    o_fox, oa, lse = _fox_fwd(pf3, cb, gq2, gk2, go2, tq)
    pg3 = pgdn.reshape(n_batch, s_len, 1536)
    qkvn = _gdn_pre(pg3, conv_w)
    z3 = pz.reshape(n_batch, s_len, GDN_WIDTH)
    smc = sm.reshape(n_batch, s_len, LANES)
    smr = smt.reshape(SM_ROWS, n_batch * n_chunks, GDN_CHUNK).transpose(1, 0, 2)
    ob, states = _gdn_fwd(qkvn, z3, smc, smr, a_c, dt_c, a_r, dt_r, gdn_onorm_g)
    oa2, ob2 = oa.reshape(t_len, FOX_WIDTH), ob.reshape(t_len, GDN_WIDTH)
    w_out, w_cq, w_ckv, w_co = late_weights(LATE_WEIGHTS[0], ob2)
    x1, hq, cq = _out_proj(x2d, oa2, ob2, w_out, norm_xattn_g, w_cq)
    mem2d = mem.reshape(n_batch * m_len, d)
    hm, ckv = _mem_kv(mem2d, mem_norm_g, w_ckv)
    co, x2, hf = _xattn_fwd(cq, ckv, x1, xattn_qnorm_g, xattn_knorm_g, w_co, norm_mlp_g, n_batch, s_len, m_len)
    w_mlp1, w_mlp2 = late_weights(LATE_WEIGHTS[1], hf)
    u, a_act, dy, loss_tiles = _mlp_fwd(hf, x2, target.reshape(t_len, d), w_mlp1, w_mlp2)

    grads = {}
    du, dx2, grads["norm_mlp_g"] = _mlp_bwd(dy, u, x2, norm_mlp_g, w_mlp1, w_mlp2)
    grads["w_mlp2"] = _wgrad(a_act, dy, "wgrad_mlp2", bt=2048)
    grads["w_mlp1"] = _wgrad(hf, du, "wgrad_mlp1", bt=2048, column_blocks=D_FF // N_DEV)
    token = grads_ready({k: grads[k] for k in GRAD_GROUPS[0]})
    grads["w_co"] = _wgrad(co, dx2, "wgrad_co", column_blocks=D_MODEL // N_DEV)
    dx1, dcq, dckv, grads["xattn_qnorm_g"], grads["xattn_knorm_g"], grads["norm_xattn_g"] = _xattn_bwd(
        dx2, cq, ckv, x1, xattn_qnorm_g + token, xattn_knorm_g, w_co, norm_xattn_g, w_cq, n_batch, s_len, m_len)
    grads["w_cq"] = _wgrad(hq, dcq, "wgrad_cq")
    grads["w_ckv"] = _wgrad(hm, dckv, "wgrad_ckv")
    grads["mem_norm_g"] = _mem_kv_bwd(dckv, mem2d, mem_norm_g, w_ckv)
    grads["w_out"] = _wgrad_stacked([oa2, ob2], dx1, "wgrad_out", bn=1024)
    token = grads_ready({k: grads[k] for k in GRAD_GROUPS[1]})
    dcat = _out_proj_bwd(dx1, w_out)
    dcat3 = dcat.reshape(n_batch, s_len, d)

    dqkvn, dz, dsmc, dsmr, dac, ddc, dar, ddr, grads["gdn_onorm_g"] = _gdn_bwd(
        qkvn, z3, smc, smr, a_c, dt_c, a_r, dt_r, gdn_onorm_g + token, states, dcat3)
    dpg, dconv = _gdn_pre_bwd(pg3, conv_w, dqkvn)
    grads["gdn_conv_w"] = dconv[0:CONV_WIDTH]

    dq, dk, dv, dcb, dgq, dgk, dgo = _fox_bwd(pf3, cb, gq2, gk2, go2, o_fox, lse, dcat3, tq)
    dc8 = dcb[:, :, :, 0:2, :].transpose(1, 3, 0, 2, 4).reshape(FOX_HEADS, t_len)
    dc_rows = jnp.concatenate([dc8, jnp.zeros((SM_ROWS - FOX_HEADS, t_len), F32)], axis=0)
    dl_rows, dbias = _fox_cum_bwd(dc_rows, smt, bias_col, n_batch, s_len)
    dsm_rows = jnp.concatenate([dl_rows[0:SM_B], dsmr.transpose(1, 0, 2).reshape(SM_ROWS, t_len)[SM_B:SM_ROWS]], axis=0)

    dprojs = [dq.reshape(t_len, FOX_WIDTH), dk.reshape(t_len, FOX_WIDTH), dv.reshape(t_len, FOX_WIDTH),
              dpg.reshape(t_len, 1536), dz.reshape(t_len, GDN_WIDTH), dsmc.reshape(t_len, LANES)]
    dwp = _wgrad_stacked(dprojs, h1, "wgrad_in")
    dwst = _rows_matmul(dsm_rows, h1, "wgrad_in_rows")
    dw_small = dwp[P_SMALL:P_SMALL + SM_ROWS] + dwst
    grads["w_in"] = jnp.concatenate([dwp[0:1536], dw_small[0:8], dwp[1536:3072], dw_small[8:16], dwp[3072:3584]], axis=0)
    token = grads_ready({k: grads[k] for k in GRAD_GROUPS[2]})
    grad_x, grads["norm_mix_g"] = _in_proj_bwd(dprojs, dsm_rows, x2d, norm_mix_g + token, wp, wst, dx1)
    packed = _pack_small(grads["norm_mix_g"], dgq, dgk, dbias, dgo, dac, dar, ddc, ddr, grads["gdn_onorm_g"], grads["norm_xattn_g"],
                         grads["mem_norm_g"], grads["xattn_qnorm_g"], grads["xattn_knorm_g"], grads["norm_mlp_g"], loss_tiles)
    return packed, grad_x.reshape(n_batch, s_len, d), {k: grads[k] for k in SHARDED}


MESH_ID = pl.DeviceIdType.MESH
ANY_SPEC = pl.BlockSpec(memory_space=pl.ANY)


def _place():
    x, y, c = lax.axis_index("x"), lax.axis_index("y"), lax.axis_index("c")
    return x, y, c, [(1 - x, y), (x, 1 - y), (1 - x, 1 - y)]


def _place_own(src_ref, dst_ref):
    def staged(buf, sem):
        for a, b in ((src_ref, buf), (buf, dst_ref)):
            cp = pltpu.make_async_copy(a, b, sem)
            cp.start()
            cp.wait()

    pl.run_scoped(staged, pltpu.VMEM(src_ref.shape, src_ref.dtype), pltpu.SemaphoreType.DMA)


def _all_gather_body(n, ins, outs, send_sems, recv_sems):
    x, y, c, chips = _place()
    me, sibling = (x, y, c), (x, y, 1 - c)

    def copy(a, k, block, to, src=None):
        dst = outs[a].at[4 * block[0] + 2 * block[1] + block[2]]
        return pltpu.make_async_remote_copy(src_ref=dst if src is None else src, dst_ref=dst, send_sem=send_sems.at[a, k],
                                            recv_sem=recv_sems.at[a, k], device_id=to, device_id_type=MESH_ID)

    first = []
    for a in range(n):
        first.append(copy(a, 0, me, sibling, src=ins[a]))
        first += [copy(a, 1 + j, me, (*chip, c), src=ins[a]) for j, chip in enumerate(chips)]
    for cp in first:
        cp.start()
    for a in range(n):
        _place_own(ins[a], outs[a].at[4 * x + 2 * y + c])
    passed = []
    for j, chip in enumerate(chips):
        for a in range(n):
            copy(a, 1 + j, (*chip, c), me).wait_recv()
            fwd = copy(a, 4 + j, (*chip, c), sibling)
            fwd.start()
            passed.append(fwd)
    for a in range(n):
        copy(a, 0, sibling, me).wait_recv()
        for j, chip in enumerate(chips):
            copy(a, 4 + j, (*chip, 1 - c), me).wait_recv()
    for cp in first + passed:
        cp.wait_send()


def _all_gather_hbm(arrs, name):
    n = len(arrs)

    def body(*refs):
        _all_gather_body(n, refs[:n], refs[n:2 * n], refs[2 * n], refs[2 * n + 1])

    return pl.pallas_call(
        body, name=name, in_specs=[ANY_SPEC] * n, out_specs=[ANY_SPEC] * n,
        out_shape=[jax.ShapeDtypeStruct((N_DEV,) + a.shape, a.dtype) for a in arrs],
        scratch_shapes=[pltpu.SemaphoreType.DMA((n, 7)), pltpu.SemaphoreType.DMA((n, 7))],
        compiler_params=pltpu.CompilerParams(vmem_limit_bytes=VMEM_LIMIT),
    )(*arrs)


def _pair_exchange(arrs, name):
    n = len(arrs)

    def body(*refs):
        ins, outs = refs[:n], refs[n:2 * n]
        send_sems, recv_sems = refs[2 * n:]
        x, y, c, _ = _place()
        copies = []
        for a in range(n):
            for chip in range(4):
                copies.append(pltpu.make_async_remote_copy(
                    src_ref=ins[a].at[2 * chip + (1 - c)], dst_ref=outs[a].at[chip], send_sem=send_sems.at[a, chip],
                    recv_sem=recv_sems.at[a, chip], device_id=(x, y, 1 - c), device_id_type=MESH_ID))
        for cp in copies:
            cp.start()
        for cp in copies:
            cp.wait()

    return pl.pallas_call(
        body, name=name, in_specs=[ANY_SPEC] * n, out_specs=[ANY_SPEC] * n,
        out_shape=[jax.ShapeDtypeStruct((4,) + a.shape[1:], a.dtype) for a in arrs],
        scratch_shapes=[pltpu.SemaphoreType.DMA((n, 4)), pltpu.SemaphoreType.DMA((n, 4))],
    )(*arrs)


HBM_SPEC = pl.BlockSpec(memory_space=pltpu.HBM)
SEM_SPEC = pl.BlockSpec(memory_space=pltpu.SEMAPHORE)
DATAFLOW = pltpu.SideEffectType.DATAFLOW_SIDE_EFFECTING


def _in_hbm(arrs):
    return [pltpu.with_memory_space_constraint(a, pltpu.HBM) for a in arrs]


def _copies_start(name, srcs, lands, make_copies, after):
    n = len(srcs)
    n_copies = len(make_copies(srcs, lands, None, None)[0])

    def body(*refs):
        send_sems, recv_sems = refs[2 * n + 1], refs[2 * n + 2]
        for row in make_copies(refs[:n], refs[n:2 * n], send_sems, recv_sems):
            for cp in row:
                cp.start()
        refs[-1][...] = jnp.zeros_like(refs[-1])

    sems = pltpu.SemaphoreType.DMA((n * n_copies,))
    thru = [pltpu.HBM(a.shape, a.dtype) for a in list(srcs) + list(lands)]
    res = pl.pallas_call(
        body, name=name, in_specs=[HBM_SPEC] * (2 * n) + [ANY_SPEC],
        out_specs=(SEM_SPEC, SEM_SPEC, *[HBM_SPEC] * (2 * n), pl.BlockSpec(memory_space=pltpu.VMEM)),
        out_shape=(sems, sems, *thru, jax.ShapeDtypeStruct((8, LANES), F32)),
        input_output_aliases={i: 2 + i for i in range(2 * n)},
        compiler_params=pltpu.CompilerParams(has_side_effects=DATAFLOW),
    )(*_in_hbm(list(srcs) + list(lands)), after)
    return res[0], res[1], list(res[2:2 + n]), list(res[2 + n:2 + 2 * n]), res[-1]


def _copies_wait(name, send_sems, recv_sems, srcs, lands, after, make_copies, own_block=False):
    n = len(srcs)

    def body(*refs):
        if own_block:
            for a in range(n):
                _place_own(refs[a], _own_part(refs[a], refs[3 * n + 3 + a]))
        for row in make_copies(refs[:n], refs[n:2 * n], refs[2 * n], refs[2 * n + 1]):
            for cp in row:
                cp.wait_send()
                cp.wait_recv()

    res = pl.pallas_call(
        body, name=name, in_specs=[HBM_SPEC] * (2 * n) + [SEM_SPEC, SEM_SPEC, ANY_SPEC],
        out_specs=tuple([HBM_SPEC] * (2 * n)),
        out_shape=tuple(pltpu.HBM(a.shape, a.dtype) for a in list(srcs) + list(lands)),
        input_output_aliases={i: i for i in range(2 * n)},
        compiler_params=pltpu.CompilerParams(has_side_effects=DATAFLOW, vmem_limit_bytes=VMEM_LIMIT),
    )(*srcs, *lands, send_sems, recv_sems, after)
    return list(res[:n]), list(res[n:])


def _own_part(src_ref, land_ref):
    me = 4 * lax.axis_index("x") + 2 * lax.axis_index("y") + lax.axis_index("c")
    rows, cols = src_ref.shape
    if land_ref.shape[0] == N_DEV * rows:
        return land_ref.at[pl.ds(pl.multiple_of(me * rows, rows), rows), :]
    return land_ref.at[:, pl.ds(pl.multiple_of(me * cols, cols), cols)]


def _gather_copies(srcs, lands, send_sems, recv_sems):
    if send_sems is None:
        return [[None] * 7]
    x, y, c, _ = _place()
    rows = []
    for a in range(len(srcs)):
        row = []
        for k in range(7):
            r = k + 1
            to = (1 - x if r & 4 else x, 1 - y if r & 2 else y, 1 - c if r & 1 else c)
            row.append(pltpu.make_async_remote_copy(
                src_ref=srcs[a], dst_ref=_own_part(srcs[a], lands[a]), send_sem=send_sems.at[7 * a + k], recv_sem=recv_sems.at[7 * a + k],
                device_id=to, device_id_type=MESH_ID))
        rows.append(row)
    return rows


def _scatter_copies(srcs, lands, send_sems, recv_sems):
    if send_sems is None:
        return [[None] * 7]
    x, y, c, _ = _place()
    rows = []
    for a in range(len(srcs)):
        row = []
        for k in range(7):
            r = k + 1
            to = (1 - x if r & 4 else x, 1 - y if r & 2 else y, 1 - c if r & 1 else c)
            row.append(pltpu.make_async_remote_copy(
                src_ref=srcs[a].at[4 * to[0] + 2 * to[1] + to[2]], dst_ref=lands[a].at[k], send_sem=send_sems.at[7 * a + k],
                recv_sem=recv_sems.at[7 * a + k], device_id=to, device_id_type=MESH_ID))
        rows.append(row)
    return rows


def _chip_copies(srcs, lands, send_sems, recv_sems):
    if send_sems is None:
        return [[None] * 3]
    x, y, c, chips = _place()
    return [[pltpu.make_async_remote_copy(
        src_ref=srcs[a].at[2 * chip[0] + chip[1]], dst_ref=lands[a].at[j], send_sem=send_sems.at[3 * a + j], recv_sem=recv_sems.at[3 * a + j],
        device_id=(*chip, c), device_id_type=MESH_ID) for j, chip in enumerate(chips)] for a in range(len(srcs))]


def _tile(rows, cols):
    if rows <= 256:
        return rows, cols
    tr = 256 if cols <= 512 else 128
    if rows % tr == 0:
        return tr, cols
    return rows, 256


def _pair_sum(core, own, got, name):
    _, rows, cols = own.shape
    tr, tc = _tile(rows, cols)

    def body(c_ref, own_ref, got_ref, o_ref):
        o_ref[0] = own_ref[0] + got_ref[0]

    return pl.pallas_call(
        body, name=name,
        grid_spec=pltpu.PrefetchScalarGridSpec(
            num_scalar_prefetch=1, grid=(4, rows // tr, cols // tc),
            in_specs=[pl.BlockSpec((1, tr, tc), lambda k, i, j, c: (2 * k + c[0], i, j)),
                      pl.BlockSpec((1, tr, tc), lambda k, i, j, c: (k, i, j))],
            out_specs=pl.BlockSpec((1, tr, tc), lambda k, i, j, c: (k, i, j))),
        out_shape=jax.ShapeDtypeStruct((4, rows, cols), F32),
        compiler_params=_cparams(("parallel", "parallel", "parallel")),
    )(core, own, got)


def _adamw(w, g, m, v):
    m_new = ADAM_B1 * m + (1.0 - ADAM_B1) * g
    v_new = ADAM_B2 * v + (1.0 - ADAM_B2) * (g * g)
    m_hat = m_new / (1.0 - ADAM_B1 ** ADAM_STEP)
    v_hat = v_new / (1.0 - ADAM_B2 ** ADAM_STEP)
    delta = -ADAM_LR * (m_hat / (jnp.sqrt(v_hat) + ADAM_EPS) + ADAM_WD * w)
    return delta, m_new, v_new


def _sum_adam(chip, sums, parts, w, m, v, name):
    n_parts, rows, cols = parts.shape
    tr, tc = _tile(rows, cols)

    def body(chip_ref, own_ref, p_ref, w_ref, m_ref, v_ref, g_ref, d_ref, mo_ref, vo_ref):
        g = own_ref[0]
        for k in range(n_parts):
            g = g + p_ref[k]
        g_ref[...] = g
        d_ref[...], mo_ref[...], vo_ref[...] = _adamw(w_ref[...], g, m_ref[...], v_ref[...])

    tile = pl.BlockSpec((tr, tc), lambda i, j, ch: (i, j))
    out = jax.ShapeDtypeStruct((rows, cols), F32)
    return pl.pallas_call(
        body, name=name,
        grid_spec=pltpu.PrefetchScalarGridSpec(
            num_scalar_prefetch=1, grid=(rows // tr, cols // tc),
            in_specs=[pl.BlockSpec((1, tr, tc), lambda i, j, ch: (ch[0], i, j)),
                      pl.BlockSpec((n_parts, tr, tc), lambda i, j, ch: (0, i, j)), tile, tile, tile],
            out_specs=[tile, tile, tile, tile]),
        out_shape=[out, out, out, out],
        compiler_params=_cparams(("parallel", "parallel")),
    )(chip, sums, parts, w, m, v)


SHARDED = ("w_in", "gdn_conv_w", "w_out", "w_cq", "w_ckv", "w_co", "w_mlp1", "w_mlp2")
TRANSPOSED = ("w_in",)
COLUMN_SHARDED = ("gdn_conv_w", "w_co", "w_mlp1")
REPLICATED = ("norm_mix_g", "fox_qnorm_g", "fox_knorm_g", "fox_f_bias", "fox_onorm_g", "gdn_A_log", "gdn_dt_bias", "gdn_onorm_g",
              "norm_xattn_g", "mem_norm_g", "xattn_qnorm_g", "xattn_knorm_g", "norm_mlp_g")
WEIGHTS = ("norm_mix_g", "w_in", "fox_qnorm_g", "fox_knorm_g", "fox_f_bias", "fox_onorm_g", "gdn_conv_w", "gdn_A_log", "gdn_dt_bias",
           "gdn_onorm_g", "w_out", "norm_xattn_g", "mem_norm_g", "w_cq", "w_ckv", "xattn_qnorm_g", "xattn_knorm_g", "w_co",
           "norm_mlp_g", "w_mlp1", "w_mlp2")
PACK_ROWS = 16
LOSS_ROW = len(REPLICATED)


def _whole(name, gathered):
    if name in COLUMN_SHARDED:
        return gathered.transpose(1, 0, 2).reshape(gathered.shape[1], N_DEV * gathered.shape[2])
    return gathered.reshape(N_DEV * gathered.shape[1], gathered.shape[2])


def _whole_shape(name, shard_shape):
    rows, cols = shard_shape
    return (rows, N_DEV * cols) if name in COLUMN_SHARDED else (N_DEV * rows, cols)


def _blocks(name, whole):
    if whole.ndim == 3:
        return whole
    if name in COLUMN_SHARDED:
        rows, cols = whole.shape
        return whole.reshape(rows, N_DEV, cols // N_DEV).transpose(1, 0, 2)
    return whole.reshape(N_DEV, whole.shape[0] // N_DEV, whole.shape[1])


def _adam_small(everyone, ws, ms, vs):
    n_par = len(ws)

    def body(*refs):
        ev_ref = refs[0]
        w_refs, m_refs, v_refs = (refs[1 + j * n_par:1 + (j + 1) * n_par] for j in range(3))
        outs = refs[1 + 3 * n_par:-1]
        sum_ref = refs[-1]
        total = ev_ref[0]
        for dev in range(1, N_DEV):
            total = total + ev_ref[dev]
        sum_ref[...] = total
        for i in range(n_par):
            n = w_refs[i].shape[1]
            g = sum_ref[i:i + 1, 0:n]
            outs[4 * i][...] = g
            outs[4 * i + 1][...], outs[4 * i + 2][...], outs[4 * i + 3][...] = _adamw(w_refs[i][...], g, m_refs[i][...], v_refs[i][...])
        outs[4 * n_par][...] = sum_ref[LOSS_ROW:LOSS_ROW + 1, 0:1]

    shapes = [jax.ShapeDtypeStruct(a.shape, F32) for a in ws for _ in range(4)] + [jax.ShapeDtypeStruct((1, 1), F32)]
    return pl.pallas_call(body, name="adam_small", out_shape=shapes,
                          scratch_shapes=[pltpu.VMEM((PACK_ROWS, D_MODEL), F32)])(everyone, *ws, *ms, *vs)


def kernel(x, mem, norm_mix_g, w_in, fox_qnorm_g, fox_knorm_g, fox_f_bias, fox_onorm_g, gdn_conv_w, gdn_A_log, gdn_dt_bias, gdn_onorm_g, w_out, norm_xattn_g, mem_norm_g, w_cq, w_ckv, xattn_qnorm_g, xattn_knorm_g, w_co, norm_mlp_g, w_mlp1, w_mlp2, loss_target, m_norm_mix_g, m_w_in, m_fox_qnorm_g, m_fox_knorm_g, m_fox_f_bias, m_fox_onorm_g, m_gdn_conv_w, m_gdn_A_log, m_gdn_dt_bias, m_gdn_onorm_g, m_w_out, m_norm_xattn_g, m_mem_norm_g, m_w_cq, m_w_ckv, m_xattn_qnorm_g, m_xattn_knorm_g, m_w_co, m_norm_mlp_g, m_w_mlp1, m_w_mlp2, v_norm_mix_g, v_w_in, v_fox_qnorm_g, v_fox_knorm_g, v_fox_f_bias, v_fox_onorm_g, v_gdn_conv_w, v_gdn_A_log, v_gdn_dt_bias, v_gdn_onorm_g, v_w_out, v_norm_xattn_g, v_mem_norm_g, v_w_cq, v_w_ckv, v_xattn_qnorm_g, v_xattn_knorm_g, v_w_co, v_norm_mlp_g, v_w_mlp1, v_w_mlp2):
    given = dict(locals())
    w = {k: given[k] for k in WEIGHTS}
    m = {k: given["m_" + k] for k in WEIGHTS}
    v = {k: given["v_" + k] for k in WEIGHTS}

    core = lax.axis_index("c").astype(jnp.int32).reshape(1)
    chip = (2 * lax.axis_index("x") + lax.axis_index("y")).astype(jnp.int32).reshape(1)
    me = 4 * lax.axis_index("x") + 2 * lax.axis_index("y") + lax.axis_index("c")

    local = lambda d: {k: jnp.transpose(d[k][0]) if k in TRANSPOSED else d[k][0] for k in SHARDED}
    w2, m2, v2 = local(w), local(m), local(v)
    shards = {k: w2[k] if k == "gdn_conv_w" else w2[k].astype(BF16) for k in SHARDED}
    early = [k for k in SHARDED if not any(k in group for group in LATE_WEIGHTS)]
    gathered = _all_gather_hbm([shards[k] for k in early], "gather_early")
    whole = {k: _whole(k, g) for k, g in zip(early, gathered)}
    gathers, after = {}, gathered[0]
    for i, group in enumerate(LATE_WEIGHTS):
        lands = [lax.empty(_whole_shape(k, shards[k].shape), BF16) for k in group]
        gathers[group] = _copies_start("gather_late_start_" + str(i), [shards[k] for k in group], lands, _gather_copies, after=after)
        after = gathers[group][4]
    first_token = after[0, 0]

    def late_weights(group, after):
        gather = gathers[group]
        _, lands = _copies_wait("gather_late_wait_" + str(LATE_WEIGHTS.index(group)), gather[0], gather[1], gather[2], gather[3],
                                after, _gather_copies, own_block=True)
        return lands

    pending = []

    def grads_ready(group):
        names = list(group)
        tag = str(len(pending))
        own = [_blocks(k, group[k]) for k in names]
        if "w_in" in names:
            got = _pair_exchange(own, "grad_pair_exchange_" + tag)
            srcs = [_pair_sum(core, o, g, "grad_pair_sum_" + k) for k, o, g in zip(names, own, got)]
            copies, index, n_parts = _chip_copies, chip, 3
        else:
            srcs, copies, index, n_parts = own, _scatter_copies, me.astype(jnp.int32).reshape(1), 7
        lands = [lax.empty((n_parts,) + s.shape[1:], s.dtype) for s in srcs]
        started = _copies_start("grad_exchange_start_" + tag, srcs, lands, copies, after=core)
        pending.append((names, started, copies, index))
        return started[4][0, 0]

    small = {k: w[k] for k in REPLICATED}
    packed, grad_x, _ = _local_step(x, mem, loss_target, **small, **whole, late_weights=late_weights,
                                    grads_ready=grads_ready, first_token=first_token)

    small_lands = [lax.empty((N_DEV * PACK_ROWS, D_MODEL), F32)]
    small_gather = _copies_start("gather_small_start", [packed], small_lands, _gather_copies, after=grad_x)

    out_g, out_d, out_m, out_v = {}, {}, {}, {}
    after = small_gather[4]
    for tag, (names, started, copies, index) in enumerate(pending):
        srcs, parts = _copies_wait("grad_exchange_wait_" + str(tag), started[0], started[1], started[2], started[3], after, copies)
        for k, s, p in zip(names, srcs, parts):
            res = _sum_adam(index, s, p, w2[k], m2[k], v2[k], "adam_" + k)
            out_g[k], out_d[k], out_m[k], out_v[k] = ((jnp.transpose(r) if k in TRANSPOSED else r)[None] for r in res)
            after = res[0]

    _, (everyone,) = _copies_wait("gather_small_wait", small_gather[0], small_gather[1], small_gather[2], small_gather[3], after,
                                  _gather_copies, own_block=True)
    res = _adam_small(everyone.reshape(N_DEV, PACK_ROWS, D_MODEL), [w[k] for k in REPLICATED], [m[k] for k in REPLICATED],
                      [v[k] for k in REPLICATED])
    for i, k in enumerate(REPLICATED):
        out_g[k], out_d[k], out_m[k], out_v[k] = res[4 * i:4 * i + 4]
    loss = res[-1].reshape(())

    return (loss, grad_x, *[out_g[k] for k in WEIGHTS], *[out_d[k] for k in WEIGHTS], *[out_m[k] for k in WEIGHTS],
            *[out_v[k] for k in WEIGHTS])
```

```python
import functools

import jax
import jax.numpy as jnp
import numpy as np
from jax import lax
from jax.experimental import pallas as pl
from jax.experimental.pallas import tpu as pltpu

F32 = jnp.float32
BF16 = jnp.bfloat16

D_MODEL = 1024
FOX_HEADS = 8
FOX_HEAD_DIM = 64
FOX_WIDTH = 512
GDN_HEADS = 4
GDN_HEAD_DIM = 128
GDN_WIDTH = 512
CONV_WIDTH = 4
GDN_CHUNK = 128
GDN_GROUP = 4
FOX_BLOCK = 512
XATTN_HEADS = 4
XATTN_HEAD_DIM = 128
XATTN_WIDTH = 512
D_FF = 4096
EPS = 1e-6
NEG_INF = -1e30
N_DEV = 8

ADAM_LR = 0.001
ADAM_B1 = 0.9
ADAM_B2 = 0.999
ADAM_EPS = 1e-08
ADAM_WD = 0.01
ADAM_STEP = 10

P_FOX = 0
P_GDN = 1536
P_Z = 3072
P_SMALL = 3584
P_DIM = 3712
SM_F = 0
SM_B = 8
SM_A = 12
SM_ROWS = 16

LANES = 128
VMEM_LIMIT = 56 * 1024 * 1024

NN = (((1,), (0,)), ((), ()))
NT = (((1,), (1,)), ((), ()))
TN = (((0,), (0,)), ((), ()))


def _dot(a, b, dims=NN):
    return lax.dot_general(a.astype(BF16), b.astype(BF16), dims, preferred_element_type=F32)


def _cparams(sem=None):
    kw = dict(vmem_limit_bytes=VMEM_LIMIT)
    if sem is not None:
        kw["dimension_semantics"] = sem
    return pltpu.CompilerParams(**kw)


def _sigmoid(x):
    return 0.5 * (jnp.tanh(0.5 * x) + 1.0)


def _softplus(x):
    return jnp.maximum(x, 0.0) + jnp.log1p(jnp.exp(-jnp.abs(x)))


def _log_sigmoid(x):
    return -_softplus(-x)


def _rms(x, g):
    r = lax.rsqrt(jnp.mean(x * x, axis=-1, keepdims=True) + EPS)
    return x * r * g


def _rms_bwd(x, g, dy):
    r = lax.rsqrt(jnp.mean(x * x, axis=-1, keepdims=True) + EPS)
    xh = x * r
    dg = jnp.sum(dy * xh, axis=0, keepdims=True)
    dyg = dy * g
    dx = r * (dyg - xh * jnp.mean(dyg * xh, axis=-1, keepdims=True))
    return dx, dg


def _pair_stat(t, m0):
    s0 = jnp.sum(jnp.where(m0, t, 0.0), axis=-1, keepdims=True)
    s1 = jnp.sum(jnp.where(m0, 0.0, t), axis=-1, keepdims=True)
    return jnp.where(m0, s0, s1)


def _rms_pair(x, g, m0):
    r = lax.rsqrt(_pair_stat(x * x, m0) * (1.0 / FOX_HEAD_DIM) + EPS)
    return x * r * g


def _rms_pair_bwd(x, g, dy, m0):
    r = lax.rsqrt(_pair_stat(x * x, m0) * (1.0 / FOX_HEAD_DIM) + EPS)
    xh = x * r
    dg = jnp.sum(dy * xh, axis=0, keepdims=True)
    dyg = dy * g
    dx = r * (dyg - xh * (_pair_stat(dyg * xh, m0) * (1.0 / FOX_HEAD_DIM)))
    return dx, dg


@jax.custom_vjp
def _mm_nn(a, b):
    return _dot(a, b, NN)


_mm_nn.defvjp(lambda a, b: (_dot(a, b, NN), (a, b)),
              lambda r, g: (_dot(g, r[1], NT), _dot(r[0], g, TN)))


@jax.custom_vjp
def _mm_nt(a, b):
    return _dot(a, b, NT)


_mm_nt.defvjp(lambda a, b: (_dot(a, b, NT), (a, b)),
              lambda r, g: (_dot(g, r[1], NN), _dot(g, r[0], TN)))


@jax.custom_vjp
def _mm_tn(a, b):
    return _dot(a, b, TN)


_mm_tn.defvjp(lambda a, b: (_dot(a, b, TN), (a, b)),
              lambda r, g: (_dot(r[1], g, NT), _dot(r[0], g, NN)))


def _dot3(a, b, dims):
    ah = a.astype(BF16)
    al = (a - ah.astype(F32)).astype(BF16)
    bh = b.astype(BF16)
    bl = (b - bh.astype(F32)).astype(BF16)
    d = functools.partial(lax.dot_general, dimension_numbers=dims, preferred_element_type=F32)
    return d(ah, bh) + d(ah, bl) + d(al, bh)


def _neumann_inverses(mats):
    c = mats[0].shape[0]
    eye = (lax.broadcasted_iota(jnp.int32, (c, c), 0) == lax.broadcasted_iota(jnp.int32, (c, c), 1)).astype(F32)
    xs = [eye - a for a in mats]
    ps = list(mats)
    k = 2
    while k < c + 1:
        mm = functools.partial(_dot3, dims=NN) if k == 2 else _dot
        ps = [mm(p, p) for p in ps]
        xs = [x + mm(x, p) for x, p in zip(xs, ps)]
        k *= 2
    return xs


@jax.custom_vjp
def _unit_lower_inverses(mats):
    return _neumann_inverses(mats)


def _unit_lower_inverses_fwd(mats):
    ts = _neumann_inverses(mats)
    return ts, ts


def _unit_lower_inverses_bwd(ts, gs):
    left = [_dot3(t, g, TN) for t, g in zip(ts, gs)]
    return ([-_dot3(m, t, NT) for m, t in zip(left, ts)],)


_unit_lower_inverses.defvjp(_unit_lower_inverses_fwd, _unit_lower_inverses_bwd)


def _wgrad(a, b, name, bk=1024, bn=1024, bt=1024, column_blocks=None):
    t_len, k_len = a.shape
    n_len = b.shape[1]
    bk, bn, bt = min(bk, k_len), min(bn, n_len), min(bt, t_len)
    nt = t_len // bt

    def body(a_ref, b_ref, o_ref, acc_ref):
        t = pl.program_id(2)

        @pl.when(t == 0)
        def _():
            acc_ref[...] = jnp.zeros_like(acc_ref)

        acc_ref[...] += _dot(a_ref[...], b_ref[...], TN)

        @pl.when(t == nt - 1)
        def _():
            if column_blocks:
                for jj in range(bn // column_blocks):
                    o_ref[jj] = acc_ref[:, jj * column_blocks:(jj + 1) * column_blocks]
            else:
                o_ref[...] = acc_ref[...]

    if column_blocks:
        out_spec = pl.BlockSpec((bn // column_blocks, bk, column_blocks), lambda i, j, t: (j, i, 0))
        out_shape = jax.ShapeDtypeStruct((n_len // column_blocks, k_len, column_blocks), F32)
    else:
        out_spec = pl.BlockSpec((bk, bn), lambda i, j, t: (i, j))
        out_shape = jax.ShapeDtypeStruct((k_len, n_len), F32)
    return pl.pallas_call(
        body, name=name, grid=(k_len // bk, n_len // bn, nt),
        in_specs=[pl.BlockSpec((bt, bk), lambda i, j, t: (t, i)), pl.BlockSpec((bt, bn), lambda i, j, t: (t, j))],
        out_specs=out_spec, out_shape=out_shape,
        scratch_shapes=[pltpu.VMEM((bk, bn), F32)],
        compiler_params=_cparams(("parallel", "parallel", "arbitrary")),
    )(a, b)


def _wgrad_stacked(pieces, b, name, bn=512, bt=1024):
    t_len, n_len = b.shape
    n_p = len(pieces)
    starts = [int(s) for s in np.cumsum([0] + [p.shape[1] for p in pieces])]
    bn, bt = min(bn, n_len), min(bt, t_len)
    nt = t_len // bt

    def body(*refs):
        b_ref, o_ref, acc_ref = refs[n_p:]
        t = pl.program_id(1)

        @pl.when(t == 0)
        def _():
            acc_ref[...] = jnp.zeros_like(acc_ref)

        for k in range(n_p):
            acc_ref[starts[k]:starts[k + 1], :] += _dot(refs[k][...], b_ref[...], TN)

        @pl.when(t == nt - 1)
        def _():
            o_ref[...] = acc_ref[...]

    return pl.pallas_call(
        body, name=name, grid=(n_len // bn, nt),
        in_specs=[pl.BlockSpec((bt, p.shape[1]), lambda j, t: (t, 0)) for p in pieces] + [pl.BlockSpec((bt, bn), lambda j, t: (t, j))],
        out_specs=pl.BlockSpec((starts[-1], bn), lambda j, t: (0, j)),
        out_shape=jax.ShapeDtypeStruct((starts[-1], n_len), F32),
        scratch_shapes=[pltpu.VMEM((starts[-1], bn), F32)],
        compiler_params=_cparams(("parallel", "arbitrary")),
    )(*pieces, b)


def _rows_matmul(a, b, name, bt=512):
    r_len, t_len = a.shape
    n_len = b.shape[1]
    bt = min(bt, t_len)
    nt = t_len // bt

    def body(a_ref, b_ref, o_ref):
        t = pl.program_id(0)

        @pl.when(t == 0)
        def _():
            o_ref[...] = jnp.zeros_like(o_ref)

        o_ref[...] += _dot(a_ref[...], b_ref[...], NN)

    return pl.pallas_call(
        body, name=name, grid=(nt,),
        in_specs=[pl.BlockSpec((r_len, bt), lambda t: (0, t)), pl.BlockSpec((bt, n_len), lambda t: (t, 0))],
        out_specs=pl.BlockSpec((r_len, n_len), lambda t: (0, 0)),
        out_shape=jax.ShapeDtypeStruct((r_len, n_len), F32),
        compiler_params=_cparams(("arbitrary",)),
    )(a, b)


def _in_proj(x, g, wp, wst, tm=256):
    t_len, d = x.shape
    tm = min(tm, t_len)

    def body(x_ref, g_ref, wp_ref, wst_ref, h_ref, fox_ref, gdn_ref, z_ref, sm_ref, smt_ref):
        h = _rms(x_ref[...], g_ref[...]).astype(BF16)
        h_ref[...] = h
        p = _dot(h, wp_ref[...], NT)
        fox_ref[...] = p[:, P_FOX:P_GDN]
        gdn_ref[...] = p[:, P_GDN:P_Z]
        z_ref[...] = p[:, P_Z:P_SMALL]
        sm_ref[...] = p[:, P_SMALL:P_DIM]
        smt_ref[...] = _dot(wst_ref[...], h, NT)

    row = lambda i: (i, 0)
    fixed = lambda i: (0, 0)
    return pl.pallas_call(
        body, name="in_proj", grid=(t_len // tm,),
        in_specs=[pl.BlockSpec((tm, d), row), pl.BlockSpec((1, d), fixed), pl.BlockSpec((P_DIM, d), fixed),
                  pl.BlockSpec((SM_ROWS, d), fixed)],
        out_specs=[pl.BlockSpec((tm, d), row), pl.BlockSpec((tm, 1536), row), pl.BlockSpec((tm, 1536), row),
                   pl.BlockSpec((tm, 512), row), pl.BlockSpec((tm, LANES), row), pl.BlockSpec((SM_ROWS, tm), lambda i: (0, i))],
        out_shape=[jax.ShapeDtypeStruct((t_len, d), BF16), jax.ShapeDtypeStruct((t_len, 1536), F32),
                   jax.ShapeDtypeStruct((t_len, 1536), F32), jax.ShapeDtypeStruct((t_len, 512), F32),
                   jax.ShapeDtypeStruct((t_len, LANES), F32), jax.ShapeDtypeStruct((SM_ROWS, t_len), F32)],
        compiler_params=_cparams(("parallel",)),
    )(x, g, wp, wst)


def _in_proj_bwd(dprojs, dsmt, x, g, wp, wst, dx1, tm=256):
    t_len, d = x.shape
    tm = min(tm, t_len)
    n_p = len(dprojs)
    starts = np.cumsum([0] + [p.shape[1] for p in dprojs])

    def body(*refs):
        dp_refs = refs[:n_p]
        dst_ref, x_ref, g_ref, wp_ref, wst_ref, dx1_ref, dx_ref, dg_ref = refs[n_p:]
        i = pl.program_id(0)
        dh = _dot(dst_ref[...], wst_ref[...], TN)
        for k in range(n_p):
            dh = dh + _dot(dp_refs[k][...], wp_ref[int(starts[k]):int(starts[k + 1]), :], NN)
        dxn, dg = _rms_bwd(x_ref[...], g_ref[...], dh)
        dx_ref[...] = dx1_ref[...] + dxn

        @pl.when(i == 0)
        def _():
            dg_ref[...] = jnp.zeros_like(dg_ref)

        dg_ref[...] += dg

    row = lambda i: (i, 0)
    fixed = lambda i: (0, 0)
    return pl.pallas_call(
        body, name="in_proj_bwd", grid=(t_len // tm,),
        in_specs=[pl.BlockSpec((tm, p.shape[1]), row) for p in dprojs] + [
            pl.BlockSpec((SM_ROWS, tm), lambda i: (0, i)), pl.BlockSpec((tm, d), row),
            pl.BlockSpec((1, d), fixed), pl.BlockSpec((P_DIM, d), fixed), pl.BlockSpec((SM_ROWS, d), fixed),
            pl.BlockSpec((tm, d), row)],
        out_specs=[pl.BlockSpec((tm, d), row), pl.BlockSpec((1, d), fixed)],
        out_shape=[jax.ShapeDtypeStruct((t_len, d), F32), jax.ShapeDtypeStruct((1, d), F32)],
        compiler_params=_cparams(("arbitrary",)),
    )(*dprojs, dsmt, x, g, wp, wst, dx1)


def _fox_cum(smt, bias_col, n_batch, s_len, ck=256):
    ck = min(ck, s_len)

    def body(s_ref, b_ref, c_ref):
        tri = (lax.broadcasted_iota(jnp.int32, (ck, ck), 0) <= lax.broadcasted_iota(jnp.int32, (ck, ck), 1)).astype(F32)
        carry = jnp.zeros((SM_ROWS, 1), F32)
        for r in range(s_len // ck):
            ls = _log_sigmoid(s_ref[:, r * ck:(r + 1) * ck] + b_ref[...])
            c = jnp.dot(ls, tri, precision=lax.Precision.HIGHEST, preferred_element_type=F32) + carry
            c_ref[:, r * ck:(r + 1) * ck] = c
            carry = c[:, ck - 1:ck]

    return pl.pallas_call(
        body, name="fox_cum", grid=(n_batch,),
        in_specs=[pl.BlockSpec((SM_ROWS, s_len), lambda b: (0, b)), pl.BlockSpec((SM_ROWS, 1), lambda b: (0, 0))],
        out_specs=pl.BlockSpec((SM_ROWS, s_len), lambda b: (0, b)),
        out_shape=jax.ShapeDtypeStruct(smt.shape, F32),
        compiler_params=_cparams(("parallel",)),
    )(smt, bias_col)


def _fox_cum_bwd(dc, smt, bias_col, n_batch, s_len, ck=256):
    ck = min(ck, s_len)
    nr = s_len // ck

    def body(dc_ref, s_ref, b_ref, dl_ref, db_ref):
        b = pl.program_id(0)
        tri = (lax.broadcasted_iota(jnp.int32, (ck, ck), 0) >= lax.broadcasted_iota(jnp.int32, (ck, ck), 1)).astype(F32)
        carry = jnp.zeros((SM_ROWS, 1), F32)
        tot = jnp.zeros((SM_ROWS, 1), F32)
        for r in reversed(range(nr)):
            sl = slice(r * ck, (r + 1) * ck)
            dls = jnp.dot(dc_ref[:, sl], tri, precision=lax.Precision.HIGHEST, preferred_element_type=F32) + carry
            carry = dls[:, 0:1]
            dl = dls * (1.0 - _sigmoid(s_ref[:, sl] + b_ref[...]))
            dl_ref[:, sl] = dl
            tot = tot + jnp.sum(dl, axis=1, keepdims=True)

        @pl.when(b == 0)
        def _():
            db_ref[...] = jnp.zeros_like(db_ref)

        db_ref[...] += jnp.broadcast_to(tot, db_ref.shape)

    return pl.pallas_call(
        body, name="fox_cum_bwd", grid=(n_batch,),
        in_specs=[pl.BlockSpec((SM_ROWS, s_len), lambda b: (0, b)), pl.BlockSpec((SM_ROWS, s_len), lambda b: (0, b)),
                  pl.BlockSpec((SM_ROWS, 1), lambda b: (0, 0))],
        out_specs=[pl.BlockSpec((SM_ROWS, s_len), lambda b: (0, b)), pl.BlockSpec((SM_ROWS, LANES), lambda b: (0, 0))],
        out_shape=[jax.ShapeDtypeStruct(smt.shape, F32), jax.ShapeDtypeStruct((SM_ROWS, LANES), F32)],
        compiler_params=_cparams(("arbitrary",)),
    )(dc, smt, bias_col)


def _fox_diagonal_mask(tq):
    return lax.broadcasted_iota(jnp.int32, (tq, tq), 1) <= lax.broadcasted_iota(jnp.int32, (tq, tq), 0)


def _fox_fwd(pf, cb, gq2, gk2, go2, tq=256):
    n_batch, s_len, _ = pf.shape
    tq = min(tq, s_len)
    nq = s_len // tq
    scale = FOX_HEAD_DIM ** -0.5

    def body(q_ref, k_ref, v_ref, c_ref, gq_ref, gk_ref, go_ref, o_ref, on_ref, lse_ref, kh_ref, vh_ref):
        j = pl.program_id(1)
        i = pl.program_id(2)
        m0 = lax.broadcasted_iota(jnp.int32, (1, LANES), 1) < FOX_HEAD_DIM

        @pl.when(i == 0)
        def _():
            kn = _rms_pair(k_ref[0], gk_ref[...], m0)
            kh_ref[0] = jnp.where(m0, kn, 0.0).astype(BF16)
            kh_ref[1] = jnp.where(m0, 0.0, kn).astype(BF16)
            v = v_ref[0]
            vh_ref[0] = jnp.where(m0, v, 0.0).astype(BF16)
            vh_ref[1] = jnp.where(m0, 0.0, v).astype(BF16)

        qb = (_rms_pair(q_ref[0], gq_ref[...], m0) * scale).astype(BF16)

        def step(kb, carry, diagonal=False):
            ms, ls, acc = carry
            off = pl.multiple_of(kb * tq, tq)
            new_m, new_l, alphas, pv = [], [], [], []
            for hh in range(2):
                s = _dot(qb, kh_ref[hh, pl.ds(off, tq), :], NT)
                s = s - c_ref[0, kb, pl.ds(2 * j + hh, 1), :]
                if diagonal:
                    s = jnp.where(_fox_diagonal_mask(tq), s, NEG_INF)
                m_new = jnp.maximum(ms[hh], jnp.max(s, axis=-1, keepdims=True))
                alpha = jnp.exp(ms[hh] - m_new)
                p = jnp.exp(s - m_new)
                new_l.append(alpha * ls[hh] + jnp.sum(p, axis=-1, keepdims=True))
                new_m.append(m_new)
                alphas.append(alpha)
                pv.append(_dot(p, vh_ref[hh, pl.ds(off, tq), :], NN))
            acc = jnp.where(m0, alphas[0], alphas[1]) * acc + pv[0] + pv[1]
            return tuple(new_m), tuple(new_l), acc

        init_m = (jnp.full((tq, 1), NEG_INF, F32),) * 2
        init_l = (jnp.zeros((tq, 1), F32),) * 2
        carry = lax.fori_loop(0, i, step, (init_m, init_l, jnp.zeros((tq, LANES), F32)))
        ms, ls, acc = step(i, carry, diagonal=True)
        o = acc / jnp.where(m0, ls[0], ls[1])
        o_ref[0] = o
        on_ref[0] = _rms_pair(o, go_ref[...], m0).astype(BF16)
        lse_ref[0] = jnp.where(m0, ms[0] + jnp.log(ls[0]), ms[1] + jnp.log(ls[1]))

    fixed = lambda b, j, i: (0, 0)
    tile = lambda b, j, i: (b, i, j)
    return pl.pallas_call(
        body, name="fox_fwd", grid=(n_batch, 4, nq),
        in_specs=[pl.BlockSpec((1, tq, LANES), tile), pl.BlockSpec((1, s_len, LANES), lambda b, j, i: (b, 0, 4 + j)),
                  pl.BlockSpec((1, s_len, LANES), lambda b, j, i: (b, 0, 8 + j)),
                  pl.BlockSpec((1, nq, SM_ROWS, tq), lambda b, j, i: (b, 0, 0, 0)),
                  pl.BlockSpec((1, LANES), fixed), pl.BlockSpec((1, LANES), fixed), pl.BlockSpec((1, LANES), fixed)],
        out_specs=[pl.BlockSpec((1, tq, LANES), tile), pl.BlockSpec((1, tq, LANES), tile), pl.BlockSpec((1, tq, LANES), tile)],
        out_shape=[jax.ShapeDtypeStruct((n_batch, s_len, FOX_WIDTH), F32), jax.ShapeDtypeStruct((n_batch, s_len, FOX_WIDTH), BF16),
                   jax.ShapeDtypeStruct((n_batch, s_len, FOX_WIDTH), F32)],
        scratch_shapes=[pltpu.VMEM((2, s_len, LANES), BF16), pltpu.VMEM((2, s_len, LANES), BF16)],
        compiler_params=_cparams(("parallel", "parallel", "arbitrary")),
    )(pf, pf, pf, cb, gq2, gk2, go2)


def _fox_bwd(pf, cb, gq2, gk2, go2, o, lse, don, tq=256):
    n_batch, s_len, _ = pf.shape
    tq = min(tq, s_len)
    nq = s_len // tq
    scale = FOX_HEAD_DIM ** -0.5

    def body(q_ref, k_ref, v_ref, c_ref, gq_ref, gk_ref, go_ref, o_ref, lse_ref, don_ref,
             dq_ref, dk_ref, dv_ref, dc_ref, dgq_ref, dgk_ref, dgo_ref, kh_ref, vh_ref, dka_ref, dva_ref, dca_ref):
        b = pl.program_id(0)
        j = pl.program_id(1)
        i = pl.program_id(2)
        m0 = lax.broadcasted_iota(jnp.int32, (1, LANES), 1) < FOX_HEAD_DIM

        @pl.when((b == 0) & (j == 0) & (i == 0))
        def _():
            dgq_ref[...] = jnp.zeros_like(dgq_ref)
            dgk_ref[...] = jnp.zeros_like(dgk_ref)
            dgo_ref[...] = jnp.zeros_like(dgo_ref)

        @pl.when(i == 0)
        def _():
            kn = _rms_pair(k_ref[0], gk_ref[...], m0)
            kh_ref[0] = jnp.where(m0, kn, 0.0).astype(BF16)
            kh_ref[1] = jnp.where(m0, 0.0, kn).astype(BF16)
            v = v_ref[0]
            vh_ref[0] = jnp.where(m0, v, 0.0).astype(BF16)
            vh_ref[1] = jnp.where(m0, 0.0, v).astype(BF16)
            dka_ref[...] = jnp.zeros_like(dka_ref)
            dva_ref[...] = jnp.zeros_like(dva_ref)
            dca_ref[...] = jnp.zeros_like(dca_ref)

        q = q_ref[0]
        qn = _rms_pair(q, gq_ref[...], m0)
        qs = qn * scale
        qb = qs.astype(BF16)
        qh = (jnp.where(m0, qs, 0.0).astype(BF16), jnp.where(m0, 0.0, qs).astype(BF16))
        ot = o_ref[0]
        do, dgo = _rms_pair_bwd(ot, go_ref[...], don_ref[0], m0)
        dgo_ref[...] += dgo
        dd = do * ot
        delta = (jnp.sum(jnp.where(m0, dd, 0.0), axis=-1, keepdims=True), jnp.sum(jnp.where(m0, 0.0, dd), axis=-1, keepdims=True))
        doh = (jnp.where(m0, do, 0.0).astype(BF16), jnp.where(m0, 0.0, do).astype(BF16))
        lse_t = lse_ref[0]
        lse_h = (lse_t[:, 0:1], lse_t[:, FOX_HEAD_DIM:FOX_HEAD_DIM + 1])

        def step(kb, carry, diagonal=False):
            dqn, rs = carry
            rs = list(rs)
            off = pl.multiple_of(kb * tq, tq)
            for hh in range(2):
                kblk = kh_ref[hh, pl.ds(off, tq), :]
                vblk = vh_ref[hh, pl.ds(off, tq), :]
                s = _dot(qb, kblk, NT)
                s = s - c_ref[0, kb, pl.ds(2 * j + hh, 1), :]
                if diagonal:
                    s = jnp.where(_fox_diagonal_mask(tq), s, NEG_INF)
                p = jnp.exp(s - lse_h[hh])
                dp = _dot(doh[hh], vblk, NT)
                ds = p * (dp - delta[hh])
                dva_ref[pl.ds(off, tq), :] += _dot(p, doh[hh], TN)
                dka_ref[pl.ds(off, tq), :] += _dot(ds, qh[hh], TN)
                dca_ref[kb, hh:hh + 1, :] += -jnp.sum(ds, axis=0, keepdims=True)
                rs[hh] = rs[hh] + jnp.sum(ds, axis=-1, keepdims=True)
                dqn = dqn + _dot(ds, kblk, NN)
            return dqn, tuple(rs)

        carry = lax.fori_loop(0, i, step, (jnp.zeros((tq, LANES), F32), (jnp.zeros((tq, 1), F32),) * 2))
        dqn, rs = step(i, carry, diagonal=True)
        dqn = dqn * scale
        rs_rows = jnp.where(m0, rs[0], rs[1]).T
        dca_ref[i, 0:1, :] += rs_rows[0:1, :]
        dca_ref[i, 1:2, :] += rs_rows[FOX_HEAD_DIM:FOX_HEAD_DIM + 1, :]
        dq, dgq = _rms_pair_bwd(q, gq_ref[...], dqn, m0)
        dq_ref[0] = dq.astype(BF16)
        dgq_ref[...] += dgq

        @pl.when(i == nq - 1)
        def _():
            dk, dgk = _rms_pair_bwd(k_ref[0], gk_ref[...], dka_ref[...], m0)
            dk_ref[0] = dk.astype(BF16)
            dgk_ref[...] += dgk
            dv_ref[0] = dva_ref[...].astype(BF16)
            dc_ref[0, 0] = dca_ref[...]

    fixed = lambda b, j, i: (0, 0)
    tile = lambda b, j, i: (b, i, j)
    full = lambda b, j, i: (b, 0, j)
    wide = jax.ShapeDtypeStruct((n_batch, s_len, FOX_WIDTH), BF16)
    gain = jax.ShapeDtypeStruct((1, LANES), F32)
    return pl.pallas_call(
        body, name="fox_bwd", grid=(n_batch, 4, nq),
        in_specs=[pl.BlockSpec((1, tq, LANES), tile), pl.BlockSpec((1, s_len, LANES), lambda b, j, i: (b, 0, 4 + j)),
                  pl.BlockSpec((1, s_len, LANES), lambda b, j, i: (b, 0, 8 + j)),
                  pl.BlockSpec((1, nq, SM_ROWS, tq), lambda b, j, i: (b, 0, 0, 0)),
                  pl.BlockSpec((1, LANES), fixed), pl.BlockSpec((1, LANES), fixed), pl.BlockSpec((1, LANES), fixed),
                  pl.BlockSpec((1, tq, LANES), tile), pl.BlockSpec((1, tq, LANES), tile), pl.BlockSpec((1, tq, LANES), tile)],
        out_specs=[pl.BlockSpec((1, tq, LANES), tile), pl.BlockSpec((1, s_len, LANES), full), pl.BlockSpec((1, s_len, LANES), full),
                   pl.BlockSpec((1, 1, nq, 8, tq), lambda b, j, i: (b, j, 0, 0, 0)),
                   pl.BlockSpec((1, LANES), fixed), pl.BlockSpec((1, LANES), fixed), pl.BlockSpec((1, LANES), fixed)],
        out_shape=[wide, wide, wide, jax.ShapeDtypeStruct((n_batch, 4, nq, 8, tq), F32), gain, gain, gain],
        scratch_shapes=[pltpu.VMEM((2, s_len, LANES), BF16), pltpu.VMEM((2, s_len, LANES), BF16),
                        pltpu.VMEM((s_len, LANES), F32), pltpu.VMEM((s_len, LANES), F32), pltpu.VMEM((nq, 8, tq), F32)],
        compiler_params=_cparams(("arbitrary", "arbitrary", "arbitrary")),
    )(pf, pf, pf, cb, gq2, gk2, go2, o, lse, don)


def _shift_down(x, k):
    row = lax.broadcasted_iota(jnp.int32, x.shape, 0)
    return jnp.where(row >= k, pltpu.roll(x, k, 0), 0.0)


def _shift_up(x, k):
    n = x.shape[0]
    row = lax.broadcasted_iota(jnp.int32, x.shape, 0)
    return jnp.where(row < n - k, pltpu.roll(x, n - k, 0), 0.0)


def _conv_silu(x, w):
    y = w[3:4] * x + w[2:3] * _shift_down(x, 1) + w[1:2] * _shift_down(x, 2) + w[0:1] * _shift_down(x, 3)
    return y, y * _sigmoid(y)


def _gdn_pre(pg, conv_w):
    n_batch, s_len, width = pg.shape
    ncb = width // LANES

    def body(x_ref, w_ref, o_ref):
        cb = pl.program_id(1)
        _, s = _conv_silu(x_ref[0], w_ref[...])
        sn = s * lax.rsqrt(jnp.sum(s * s, axis=-1, keepdims=True) + EPS)
        o_ref[0] = jnp.where(cb < 2 * GDN_HEADS, sn, s)

    return pl.pallas_call(
        body, name="gdn_pre", grid=(n_batch, ncb),
        in_specs=[pl.BlockSpec((1, s_len, LANES), lambda b, c: (b, 0, c)), pl.BlockSpec((8, LANES), lambda b, c: (0, c))],
        out_specs=pl.BlockSpec((1, s_len, LANES), lambda b, c: (b, 0, c)),
        out_shape=jax.ShapeDtypeStruct(pg.shape, F32),
        compiler_params=_cparams(("parallel", "parallel")),
    )(pg, conv_w)


def _gdn_pre_bwd(pg, conv_w, dout):
    n_batch, s_len, width = pg.shape
    ncb = width // LANES

    def body(x_ref, w_ref, d_ref, dx_ref, dw_ref):
        cb = pl.program_id(0)
        b = pl.program_id(1)
        x = x_ref[0]
        w = w_ref[...]
        d = d_ref[0]
        y, s = _conv_silu(x, w)
        rr = lax.rsqrt(jnp.sum(s * s, axis=-1, keepdims=True) + EPS)
        sn = s * rr
        ds_n = rr * (d - sn * jnp.sum(d * sn, axis=-1, keepdims=True))
        ds = jnp.where(cb < 2 * GDN_HEADS, ds_n, d)
        sig = _sigmoid(y)
        dy = ds * (sig * (1.0 + y * (1.0 - sig)))
        dx = w[3:4] * dy + w[2:3] * _shift_up(dy, 1) + w[1:2] * _shift_up(dy, 2) + w[0:1] * _shift_up(dy, 3)
        dx_ref[0] = dx.astype(BF16)
        dw = [jnp.sum(dy * _shift_down(x, 3 - jj), axis=0, keepdims=True) if jj < 3 else jnp.sum(dy * x, axis=0, keepdims=True)
              for jj in range(CONV_WIDTH)]
        rows = lax.broadcasted_iota(jnp.int32, (8, LANES), 0)
        dwb = jnp.zeros((8, LANES), F32)
        for jj in range(CONV_WIDTH):
            dwb = dwb + jnp.where(rows == jj, dw[jj], 0.0)

        @pl.when(b == 0)
        def _():
            dw_ref[...] = jnp.zeros_like(dw_ref)

        dw_ref[...] += dwb

    blk = lambda c, b: (b, 0, c)
    return pl.pallas_call(
        body, name="gdn_pre_bwd", grid=(ncb, n_batch),
        in_specs=[pl.BlockSpec((1, s_len, LANES), blk), pl.BlockSpec((8, LANES), lambda c, b: (0, c)), pl.BlockSpec((1, s_len, LANES), blk)],
        out_specs=[pl.BlockSpec((1, s_len, LANES), blk), pl.BlockSpec((8, LANES), lambda c, b: (0, c))],
        out_shape=[jax.ShapeDtypeStruct(pg.shape, BF16), jax.ShapeDtypeStruct((8, width), F32)],
        compiler_params=_cparams(("parallel", "arbitrary")),
    )(pg, conv_w, dout)


def _gdn_gates(smc, smr, a_c, dt_c, a_r, dt_r, h):
    lane = lax.broadcasted_iota(jnp.int32, (1, LANES), 1)
    sub = lax.broadcasted_iota(jnp.int32, (SM_ROWS, 1), 0)
    beta_c = jnp.sum(jnp.where(lane == SM_B + h, _sigmoid(smc), 0.0), axis=1, keepdims=True)
    g_all_c = -jnp.exp(a_c) * _softplus(smc + dt_c)
    g_c = jnp.sum(jnp.where(lane == SM_A + h, g_all_c, 0.0), axis=1, keepdims=True)
    g_all_r = -jnp.exp(a_r) * _softplus(smr + dt_r)
    g_r = jnp.sum(jnp.where(sub == SM_A + h, g_all_r, 0.0), axis=0, keepdims=True)
    return beta_c, g_c, g_r


def _gdn_group(qkv, z, smc, smr, a_c, dt_c, a_r, dt_r, go, states):
    n_grp = len(qkv)
    c = qkv[0].shape[0]
    hd = GDN_HEAD_DIM
    pairs = [(g, h) for g in range(n_grp) for h in range(GDN_HEADS)]
    ii = lax.broadcasted_iota(jnp.int32, (c, c), 0)
    jj = lax.broadcasted_iota(jnp.int32, (c, c), 1)
    incl = ii >= jj
    col = lambda arr, base, h: arr[:, base + h * hd:base + (h + 1) * hd]

    qs, ks, kbs, vbs, decays, gcs, g_lasts, amats = [], [], [], [], [], [], [], []
    for g, h in pairs:
        beta_c, g_c, g_r = _gdn_gates(smc[g], smr[g], a_c, dt_c, a_r, dt_r, h)
        gc_c = jnp.sum(jnp.where(incl, g_r, 0.0), axis=1, keepdims=True)
        gc_r = jnp.sum(jnp.where(ii <= jj, g_c, 0.0), axis=0, keepdims=True)
        decay = jnp.where(incl, jnp.exp(jnp.where(incl, gc_c - gc_r, 0.0)), 0.0)
        k = col(qkv[g], GDN_WIDTH, h)
        kb = k * beta_c
        qs.append(col(qkv[g], 0, h) * (hd ** -0.5))
        ks.append(k)
        kbs.append(kb)
        vbs.append(col(qkv[g], 2 * GDN_WIDTH, h) * beta_c)
        decays.append(decay)
        gcs.append(gc_c)
        g_lasts.append(jnp.sum(g_c, axis=0, keepdims=True))
        amats.append(jnp.where(ii > jj, _mm_nt(kb, k) * decay, 0.0))
    ts = _unit_lower_inverses(amats)
    egcs = [jnp.exp(gc) for gc in gcs]
    us = [_mm_nn(t, vb) for t, vb in zip(ts, vbs)]
    ws = [_mm_nn(t, kb * e) for t, kb, e in zip(ts, kbs, egcs)]
    intras = [_mm_nt(q, k) * d for q, k, d in zip(qs, ks, decays)]
    qes = [q * e for q, e in zip(qs, egcs)]
    kds = [k * jnp.exp(gl - gc) for k, gl, gc in zip(ks, g_lasts, gcs)]
    sdecs = [jnp.exp(gl) for gl in g_lasts]

    outs = []
    for g in range(n_grp):
        idx = [g * GDN_HEADS + h for h in range(GDN_HEADS)]
        v_new = [us[i] - _mm_nn(ws[i], states[h]) for h, i in enumerate(idx)]
        o_state = [_mm_nn(qes[i], states[h]) for h, i in enumerate(idx)]
        o_intra = [_mm_nn(intras[i], v_new[h]) for h, i in enumerate(idx)]
        states = [states[h] * sdecs[i] + _mm_tn(kds[i], v_new[h]) for h, i in enumerate(idx)]
        outs.append([_rms(o_state[h] + o_intra[h], go) * (col(z[g], 0, h) * _sigmoid(col(z[g], 0, h))) for h in range(GDN_HEADS)])
    return outs, states


def _gdn_group_size(n_chunks):
    return GDN_GROUP if n_chunks % GDN_GROUP == 0 else 1


def _gdn_fwd(qkvn, z, smc, smr, a_c, dt_c, a_r, dt_r, go):
    n_batch, s_len, _ = qkvn.shape
    c = GDN_CHUNK
    n = s_len // c
    grp = _gdn_group_size(n)
    ng = n // grp
    gc = grp * c
    hd = GDN_HEAD_DIM

    def body(qkv_ref, z_ref, smc_ref, smr_ref, ac_ref, dc_ref, ar_ref, dr_ref, go_ref, og_ref, st_ref, s_ref):
        @pl.when(pl.program_id(1) == 0)
        def _():
            s_ref[...] = jnp.zeros_like(s_ref)

        states = [s_ref[h] for h in range(GDN_HEADS)]
        for h in range(GDN_HEADS):
            st_ref[0, 0, h] = states[h]
        rows = lambda k: slice(k * c, (k + 1) * c)
        outs, nxt = _gdn_group([qkv_ref[0, rows(k), :] for k in range(grp)], [z_ref[0, rows(k), :] for k in range(grp)],
                               [smc_ref[0, rows(k), :] for k in range(grp)], [smr_ref[k] for k in range(grp)],
                               ac_ref[...], dc_ref[...], ar_ref[...], dr_ref[...], go_ref[...], states)
        for k in range(grp):
            for h in range(GDN_HEADS):
                og_ref[0, rows(k), h * hd:(h + 1) * hd] = outs[k][h].astype(BF16)
        for h in range(GDN_HEADS):
            s_ref[h] = nxt[h]

    tok = lambda b, i: (b, i, 0)
    fixed = lambda b, i: (0, 0)
    return pl.pallas_call(
        body, name="gdn_fwd", grid=(n_batch, ng),
        in_specs=[pl.BlockSpec((1, gc, 3 * GDN_WIDTH), tok), pl.BlockSpec((1, gc, GDN_WIDTH), tok), pl.BlockSpec((1, gc, LANES), tok),
                  pl.BlockSpec((grp, SM_ROWS, c), lambda b, i: (b * ng + i, 0, 0)),
                  pl.BlockSpec((1, LANES), fixed), pl.BlockSpec((1, LANES), fixed), pl.BlockSpec((SM_ROWS, 1), fixed),
                  pl.BlockSpec((SM_ROWS, 1), fixed), pl.BlockSpec((1, LANES), fixed)],
        out_specs=[pl.BlockSpec((1, gc, GDN_WIDTH), tok), pl.BlockSpec((1, 1, GDN_HEADS, hd, hd), lambda b, i: (b, i, 0, 0, 0))],
        out_shape=[jax.ShapeDtypeStruct((n_batch, s_len, GDN_WIDTH), BF16), jax.ShapeDtypeStruct((n_batch, ng, GDN_HEADS, hd, hd), F32)],
        scratch_shapes=[pltpu.VMEM((GDN_HEADS, hd, hd), F32)],
        compiler_params=_cparams(("parallel", "arbitrary")),
    )(qkvn, z, smc, smr, a_c, dt_c, a_r, dt_r, go)


def _gdn_bwd(qkvn, z, smc, smr, a_c, dt_c, a_r, dt_r, go, states, dog):
    n_batch, s_len, _ = qkvn.shape
    c = GDN_CHUNK
    n = s_len // c
    grp = _gdn_group_size(n)
    ng = n // grp
    gc = grp * c
    hd = GDN_HEAD_DIM

    def body(qkv_ref, z_ref, smc_ref, smr_ref, ac_ref, dc_ref, ar_ref, dr_ref, go_ref, st_ref, dog_ref,
             dqkv_ref, dz_ref, dsmc_ref, dsmr_ref, dac_ref, ddc_ref, dar_ref, ddr_ref, dgo_ref, ds_ref):
        first = (pl.program_id(0) == 0) & (pl.program_id(1) == 0)

        @pl.when(pl.program_id(1) == 0)
        def _():
            ds_ref[...] = jnp.zeros_like(ds_ref)

        @pl.when(first)
        def _():
            for r in (dac_ref, ddc_ref, dar_ref, ddr_ref, dgo_ref):
                r[...] = jnp.zeros_like(r)

        rows = lambda k: slice(k * c, (k + 1) * c)
        states = [st_ref[0, 0, h] for h in range(GDN_HEADS)]
        prim = ([qkv_ref[0, rows(k), :] for k in range(grp)], [z_ref[0, rows(k), :] for k in range(grp)],
                [smc_ref[0, rows(k), :] for k in range(grp)], [smr_ref[k] for k in range(grp)],
                ac_ref[...], dc_ref[...], ar_ref[...], dr_ref[...], go_ref[...], states)
        _, vjp = jax.vjp(_gdn_group, *prim)
        cot = ([[dog_ref[0, rows(k), h * hd:(h + 1) * hd] for h in range(GDN_HEADS)] for k in range(grp)],
               [ds_ref[h] for h in range(GDN_HEADS)])
        dqkv, dz, dsmc, dsmr, dac, ddc, dar, ddr, dgo, dstates = vjp(cot)
        for k in range(grp):
            dqkv_ref[0, rows(k), :] = dqkv[k]
            dz_ref[0, rows(k), :] = dz[k].astype(BF16)
            dsmc_ref[0, rows(k), :] = dsmc[k]
            dsmr_ref[k] = dsmr[k]
        dac_ref[...] += dac
        ddc_ref[...] += ddc
        dar_ref[...] += dar
        ddr_ref[...] += ddr
        dgo_ref[...] += dgo
        for h in range(GDN_HEADS):
            ds_ref[h] = dstates[h]

    tok = lambda b, i: (b, ng - 1 - i, 0)
    fixed = lambda b, i: (0, 0)
    lane_vec = jax.ShapeDtypeStruct((1, LANES), F32)
    row_vec = jax.ShapeDtypeStruct((SM_ROWS, 1), F32)
    return pl.pallas_call(
        body, name="gdn_bwd", grid=(n_batch, ng),
        in_specs=[pl.BlockSpec((1, gc, 3 * GDN_WIDTH), tok), pl.BlockSpec((1, gc, GDN_WIDTH), tok), pl.BlockSpec((1, gc, LANES), tok),
                  pl.BlockSpec((grp, SM_ROWS, c), lambda b, i: (b * ng + ng - 1 - i, 0, 0)),
                  pl.BlockSpec((1, LANES), fixed), pl.BlockSpec((1, LANES), fixed), pl.BlockSpec((SM_ROWS, 1), fixed),
                  pl.BlockSpec((SM_ROWS, 1), fixed), pl.BlockSpec((1, LANES), fixed),
                  pl.BlockSpec((1, 1, GDN_HEADS, hd, hd), lambda b, i: (b, ng - 1 - i, 0, 0, 0)),
                  pl.BlockSpec((1, gc, GDN_WIDTH), lambda b, i: (b, ng - 1 - i, 1))],
        out_specs=[pl.BlockSpec((1, gc, 3 * GDN_WIDTH), tok), pl.BlockSpec((1, gc, GDN_WIDTH), tok), pl.BlockSpec((1, gc, LANES), tok),
                   pl.BlockSpec((grp, SM_ROWS, c), lambda b, i: (b * ng + ng - 1 - i, 0, 0)),
                   pl.BlockSpec((1, LANES), fixed), pl.BlockSpec((1, LANES), fixed), pl.BlockSpec((SM_ROWS, 1), fixed),
                   pl.BlockSpec((SM_ROWS, 1), fixed), pl.BlockSpec((1, LANES), fixed)],
        out_shape=[jax.ShapeDtypeStruct((n_batch, s_len, 3 * GDN_WIDTH), F32), jax.ShapeDtypeStruct((n_batch, s_len, GDN_WIDTH), BF16),
                   jax.ShapeDtypeStruct((n_batch, s_len, LANES), F32), jax.ShapeDtypeStruct((n_batch * n, SM_ROWS, c), F32),
                   lane_vec, lane_vec, row_vec, row_vec, lane_vec],
        scratch_shapes=[pltpu.VMEM((GDN_HEADS, hd, hd), F32)],
        compiler_params=_cparams(("arbitrary", "arbitrary")),
    )(qkvn, z, smc, smr, a_c, dt_c, a_r, dt_r, go, states, dog)


def _out_proj(x, oa, ob, w_out, g_x, w_cq, tm=256):
    t_len, d = x.shape
    tm = min(tm, t_len)

    def body(x_ref, oa_ref, ob_ref, wo_ref, g_ref, wq_ref, x1_ref, hq_ref, cq_ref):
        x1 = x_ref[...] + _dot(oa_ref[...], wo_ref[0:FOX_WIDTH, :]) + _dot(ob_ref[...], wo_ref[FOX_WIDTH:2 * FOX_WIDTH, :])
        x1_ref[...] = x1
        hq = _rms(x1, g_ref[...]).astype(BF16)
        hq_ref[...] = hq
        cq_ref[...] = _dot(hq, wq_ref[...])

    row = lambda i: (i, 0)
    fixed = lambda i: (0, 0)
    return pl.pallas_call(
        body, name="out_proj", grid=(t_len // tm,),
        in_specs=[pl.BlockSpec((tm, d), row), pl.BlockSpec((tm, FOX_WIDTH), row), pl.BlockSpec((tm, GDN_WIDTH), row),
                  pl.BlockSpec((d, d), fixed), pl.BlockSpec((1, d), fixed), pl.BlockSpec((d, XATTN_WIDTH), fixed)],
        out_specs=[pl.BlockSpec((tm, d), row), pl.BlockSpec((tm, d), row), pl.BlockSpec((tm, XATTN_WIDTH), row)],
        out_shape=[jax.ShapeDtypeStruct((t_len, d), F32), jax.ShapeDtypeStruct((t_len, d), BF16), jax.ShapeDtypeStruct((t_len, XATTN_WIDTH), F32)],
        compiler_params=_cparams(("parallel",)),
    )(x, oa, ob, w_out, g_x, w_cq)


def _out_proj_bwd(dx1, w_out, tm=512):
    t_len, d = dx1.shape
    tm = min(tm, t_len)

    def body(dx_ref, w_ref, o_ref):
        o_ref[...] = _dot(dx_ref[...], w_ref[...], NT)

    return pl.pallas_call(
        body, name="out_proj_bwd", grid=(t_len // tm,),
        in_specs=[pl.BlockSpec((tm, d), lambda i: (i, 0)), pl.BlockSpec((d, d), lambda i: (0, 0))],
        out_specs=pl.BlockSpec((tm, d), lambda i: (i, 0)),
        out_shape=jax.ShapeDtypeStruct((t_len, d), F32),
        compiler_params=_cparams(("parallel",)),
    )(dx1, w_out)


def _mem_kv(mem, g, w_ckv, tm=256):
    t_len, d = mem.shape
    tm = min(tm, t_len)

    def body(x_ref, g_ref, w_ref, h_ref, o_ref):
        h = _rms(x_ref[...], g_ref[...]).astype(BF16)
        h_ref[...] = h
        o_ref[...] = _dot(h, w_ref[...])

    row = lambda i: (i, 0)
    fixed = lambda i: (0, 0)
    return pl.pallas_call(
        body, name="mem_kv", grid=(t_len // tm,),
        in_specs=[pl.BlockSpec((tm, d), row), pl.BlockSpec((1, d), fixed), pl.BlockSpec((d, 2 * XATTN_WIDTH), fixed)],
        out_specs=[pl.BlockSpec((tm, d), row), pl.BlockSpec((tm, 2 * XATTN_WIDTH), row)],
        out_shape=[jax.ShapeDtypeStruct((t_len, d), BF16), jax.ShapeDtypeStruct((t_len, 2 * XATTN_WIDTH), F32)],
        compiler_params=_cparams(("parallel",)),
    )(mem, g, w_ckv)


def _mem_kv_bwd(dckv, mem, g, w_ckv, tm=256):
    t_len, d = mem.shape
    tm = min(tm, t_len)

    def body(d_ref, x_ref, g_ref, w_ref, dg_ref):
        @pl.when(pl.program_id(0) == 0)
        def _():
            dg_ref[...] = jnp.zeros_like(dg_ref)

        dh = _dot(d_ref[...], w_ref[...], NT)
        _, dg = _rms_bwd(x_ref[...], g_ref[...], dh)
        dg_ref[...] += dg

    row = lambda i: (i, 0)
    fixed = lambda i: (0, 0)
    return pl.pallas_call(
        body, name="mem_kv_bwd", grid=(t_len // tm,),
        in_specs=[pl.BlockSpec((tm, 2 * XATTN_WIDTH), row), pl.BlockSpec((tm, d), row), pl.BlockSpec((1, d), fixed),
                  pl.BlockSpec((d, 2 * XATTN_WIDTH), fixed)],
        out_specs=pl.BlockSpec((1, d), fixed),
        out_shape=jax.ShapeDtypeStruct((1, d), F32),
        compiler_params=_cparams(("arbitrary",)),
    )(dckv, mem, g, w_ckv)


def _xattn_probs(qn, kn):
    s = _dot(qn, kn, NT) * (XATTN_HEAD_DIM ** -0.5)
    p = jnp.exp(s - jnp.max(s, axis=-1, keepdims=True))
    return p / jnp.sum(p, axis=-1, keepdims=True)


def _xattn_fwd(cq, ckv, x1, gq, gk, w_co, g_mlp, n_batch, s_len, m_len, tq=512):
    d = x1.shape[1]
    tq = min(tq, s_len)
    nq = s_len // tq
    hd = XATTN_HEAD_DIM

    def body(cq_ref, kv_ref, x1_ref, gq_ref, gk_ref, wo_ref, gm_ref, co_ref, x2_ref, hf_ref):
        outs = []
        for h in range(XATTN_HEADS):
            qn = _rms(cq_ref[:, h * hd:(h + 1) * hd], gq_ref[...])
            kn = _rms(kv_ref[:, h * hd:(h + 1) * hd], gk_ref[...])
            p = _xattn_probs(qn, kn)
            outs.append(_dot(p, kv_ref[:, XATTN_WIDTH + h * hd:XATTN_WIDTH + (h + 1) * hd]).astype(BF16))
        for h in range(XATTN_HEADS):
            co_ref[:, h * hd:(h + 1) * hd] = outs[h]
        x2 = x1_ref[...] + _dot(co_ref[...], wo_ref[...])
        x2_ref[...] = x2
        hf_ref[...] = _rms(x2, gm_ref[...]).astype(BF16)

    row = lambda b, i: (b * nq + i, 0)
    fixed = lambda b, i: (0, 0)
    t_len = n_batch * s_len
    return pl.pallas_call(
        body, name="xattn_fwd", grid=(n_batch, nq),
        in_specs=[pl.BlockSpec((tq, XATTN_WIDTH), row), pl.BlockSpec((m_len, 2 * XATTN_WIDTH), lambda b, i: (b, 0)),
                  pl.BlockSpec((tq, d), row), pl.BlockSpec((1, hd), fixed), pl.BlockSpec((1, hd), fixed),
                  pl.BlockSpec((XATTN_WIDTH, d), fixed), pl.BlockSpec((1, d), fixed)],
        out_specs=[pl.BlockSpec((tq, XATTN_WIDTH), row), pl.BlockSpec((tq, d), row), pl.BlockSpec((tq, d), row)],
        out_shape=[jax.ShapeDtypeStruct((t_len, XATTN_WIDTH), BF16), jax.ShapeDtypeStruct((t_len, d), F32),
                   jax.ShapeDtypeStruct((t_len, d), BF16)],
        compiler_params=_cparams(("parallel", "parallel")),
    )(cq, ckv, x1, gq, gk, w_co, g_mlp)


def _xattn_bwd(dx2, cq, ckv, x1, gq, gk, w_co, g_x, w_cq, n_batch, s_len, m_len, tq=512):
    d = x1.shape[1]
    tq = min(tq, s_len)
    nq = s_len // tq
    hd = XATTN_HEAD_DIM
    scale = XATTN_HEAD_DIM ** -0.5

    def body(dx2_ref, cq_ref, kv_ref, x1_ref, gq_ref, gk_ref, wo_ref, gx_ref, wq_ref,
             dx1_ref, dcq_ref, dkv_ref, dgq_ref, dgk_ref, dgx_ref, dk_acc, dv_acc):
        b = pl.program_id(0)
        i = pl.program_id(1)

        @pl.when((b == 0) & (i == 0))
        def _():
            dgq_ref[...] = jnp.zeros_like(dgq_ref)
            dgk_ref[...] = jnp.zeros_like(dgk_ref)
            dgx_ref[...] = jnp.zeros_like(dgx_ref)

        @pl.when(i == 0)
        def _():
            dk_acc[...] = jnp.zeros_like(dk_acc)
            dv_acc[...] = jnp.zeros_like(dv_acc)

        dx2 = dx2_ref[...]
        dco_all = _dot(dx2, wo_ref[...], NT)
        for h in range(XATTN_HEADS):
            sl = slice(h * hd, (h + 1) * hd)
            q = cq_ref[:, sl]
            qn = _rms(q, gq_ref[...])
            kn = _rms(kv_ref[:, sl], gk_ref[...])
            v = kv_ref[:, XATTN_WIDTH + h * hd:XATTN_WIDTH + (h + 1) * hd]
            p = _xattn_probs(qn, kn)
            dco = dco_all[:, sl]
            dv_acc[:, sl] += _dot(p, dco, TN)
            dp = _dot(dco, v, NT)
            ds = p * (dp - jnp.sum(dp * p, axis=-1, keepdims=True))
            dqn = _dot(ds, kn) * scale
            dk_acc[:, sl] += _dot(ds, qn, TN) * scale
            dq, dgq = _rms_bwd(q, gq_ref[...], dqn)
            dgq_ref[...] += dgq
            dcq_ref[:, sl] = dq.astype(BF16)
        dhq = _dot(dcq_ref[...], wq_ref[...], NT)
        dxn, dgx = _rms_bwd(x1_ref[...], gx_ref[...], dhq)
        dgx_ref[...] += dgx
        dx1_ref[...] = dx2 + dxn

        @pl.when(i == nq - 1)
        def _():
            for h in range(XATTN_HEADS):
                sl = slice(h * hd, (h + 1) * hd)
                dk, dgk = _rms_bwd(kv_ref[:, sl], gk_ref[...], dk_acc[:, sl])
                dgk_ref[...] += dgk
                dkv_ref[:, sl] = dk.astype(BF16)
                dkv_ref[:, XATTN_WIDTH + h * hd:XATTN_WIDTH + (h + 1) * hd] = dv_acc[:, sl].astype(BF16)

    row = lambda b, i: (b * nq + i, 0)
    fixed = lambda b, i: (0, 0)
    t_len = n_batch * s_len
    return pl.pallas_call(
        body, name="xattn_bwd", grid=(n_batch, nq),
        in_specs=[pl.BlockSpec((tq, d), row), pl.BlockSpec((tq, XATTN_WIDTH), row), pl.BlockSpec((m_len, 2 * XATTN_WIDTH), lambda b, i: (b, 0)),
                  pl.BlockSpec((tq, d), row), pl.BlockSpec((1, hd), fixed), pl.BlockSpec((1, hd), fixed),
                  pl.BlockSpec((XATTN_WIDTH, d), fixed), pl.BlockSpec((1, d), fixed), pl.BlockSpec((d, XATTN_WIDTH), fixed)],
        out_specs=[pl.BlockSpec((tq, d), row), pl.BlockSpec((tq, XATTN_WIDTH), row), pl.BlockSpec((m_len, 2 * XATTN_WIDTH), lambda b, i: (b, 0)),
                   pl.BlockSpec((1, hd), fixed), pl.BlockSpec((1, hd), fixed), pl.BlockSpec((1, d), fixed)],
        out_shape=[jax.ShapeDtypeStruct((t_len, d), F32), jax.ShapeDtypeStruct((t_len, XATTN_WIDTH), BF16),
                   jax.ShapeDtypeStruct((n_batch * m_len, 2 * XATTN_WIDTH), BF16),
                   jax.ShapeDtypeStruct((1, hd), F32), jax.ShapeDtypeStruct((1, hd), F32), jax.ShapeDtypeStruct((1, d), F32)],
        scratch_shapes=[pltpu.VMEM((m_len, XATTN_WIDTH), F32), pltpu.VMEM((m_len, XATTN_WIDTH), F32)],
        compiler_params=_cparams(("arbitrary", "arbitrary")),
    )(dx2, cq, ckv, x1, gq, gk, w_co, g_x, w_cq)


def _resident(shape):
    return pl.BlockSpec(shape, lambda *_: (0,) * len(shape), pipeline_mode=pl.Buffered(1))


def _mlp_fwd(hf, x2, target, w1, w2, tm=256, tf=1024):
    t_len, d = x2.shape
    f = w1.shape[1]
    tm, tf = min(tm, t_len), min(tf, f)

    def body(hf_ref, x2_ref, tg_ref, w1_ref, w2_ref, u_ref, a_ref, dy_ref, ls_ref):
        hf_t = hf_ref[...]
        for k in range(f // tf):
            cols = slice(k * tf, (k + 1) * tf)
            u = _dot(hf_t, w1_ref[:, cols])
            u_ref[:, cols] = u
            r = jnp.maximum(u, 0.0)
            a_ref[:, cols] = (r * r).astype(BF16)
        y = x2_ref[...] + _dot(a_ref[...], w2_ref[...])
        err = y - tg_ref[...]
        dy_ref[...] = err * (1.0 / d)
        ls_ref[...] = jnp.broadcast_to(jnp.sum(jnp.sum(err * err, axis=-1, keepdims=True) * (1.0 / d), axis=0, keepdims=True), ls_ref.shape)

    row = lambda i: (i, 0)
    return pl.pallas_call(
        body, name="mlp_fwd", grid=(t_len // tm,),
        in_specs=[pl.BlockSpec((tm, d), row), pl.BlockSpec((tm, d), row), pl.BlockSpec((tm, d), row), _resident((d, f)), _resident((f, d))],
        out_specs=[pl.BlockSpec((tm, f), row), pl.BlockSpec((tm, f), row), pl.BlockSpec((tm, d), row),
                   pl.BlockSpec((1, 8, LANES), lambda i: (i, 0, 0))],
        out_shape=[jax.ShapeDtypeStruct((t_len, f), F32), jax.ShapeDtypeStruct((t_len, f), BF16), jax.ShapeDtypeStruct((t_len, d), F32),
                   jax.ShapeDtypeStruct((t_len // tm, 8, LANES), F32)],
        compiler_params=_cparams(("parallel",)),
    )(hf, x2, target, w1, w2)


def _mlp_bwd(dy, u, x2, g, w1, w2, tm=256, tf=1024):
    t_len, d = x2.shape
    f = w1.shape[1]
    tm, tf = min(tm, t_len), min(tf, f)

    def body(dy_ref, u_ref, x2_ref, g_ref, w1_ref, w2_ref, du_ref, dx2_ref, dg_ref):
        @pl.when(pl.program_id(0) == 0)
        def _():
            dg_ref[...] = jnp.zeros_like(dg_ref)

        dy_t = dy_ref[...]
        dyb = dy_t.astype(BF16)
        for k in range(f // tf):
            cols = slice(k * tf, (k + 1) * tf)
            da = _dot(dyb, w2_ref[cols, :], NT)
            du_ref[:, cols] = (da * (2.0 * jnp.maximum(u_ref[:, cols], 0.0))).astype(BF16)
        dhf = _dot(du_ref[...], w1_ref[...], NT)
        dxn, dg = _rms_bwd(x2_ref[...], g_ref[...], dhf)
        dx2_ref[...] = dy_t + dxn
        dg_ref[...] += dg

    row = lambda i: (i, 0)
    fixed = lambda i: (0, 0)
    return pl.pallas_call(
        body, name="mlp_bwd", grid=(t_len // tm,),
        in_specs=[pl.BlockSpec((tm, d), row), pl.BlockSpec((tm, f), row), pl.BlockSpec((tm, d), row), pl.BlockSpec((1, d), fixed),
                  _resident((d, f)), _resident((f, d))],
        out_specs=[pl.BlockSpec((tm, f), row), pl.BlockSpec((tm, d), row), pl.BlockSpec((1, d), fixed)],
        out_shape=[jax.ShapeDtypeStruct((t_len, f), BF16), jax.ShapeDtypeStruct((t_len, d), F32), jax.ShapeDtypeStruct((1, d), F32)],
        compiler_params=_cparams(("arbitrary",)),
    )(dy, u, x2, g, w1, w2)


def _pad_lanes(v, offset=0, width=LANES):
    return jnp.zeros((1, width), F32).at[:, offset:offset + v.shape[1]].set(v)


def _col(v, offset=0, rows=SM_ROWS):
    return jnp.zeros((rows, 1), F32).at[offset:offset + v.shape[1], 0].set(v[0])


def _pack_small(g_mix, dgq, dgk, dbias, dgo, dac, dar, ddc, ddr, g_gdn_o, g_nx, g_mem, g_xq, g_xk, g_mlp, loss_tiles):
    def body(mix_ref, q_ref, k_ref, b_ref, o_ref, ac_ref, ar_ref, dc_ref, dr_ref, go_ref, nx_ref, mem_ref, xq_ref, xk_ref,
             mlp_ref, lt_ref, out_ref):
        lane = lax.broadcasted_iota(jnp.int32, (1, LANES), 1)
        diag = lax.broadcasted_iota(jnp.int32, (SM_ROWS, LANES), 0) == lax.broadcasted_iota(jnp.int32, (SM_ROWS, LANES), 1)

        def rolled(v, shift):
            return pltpu.roll(jnp.broadcast_to(v, (8, LANES)), shift, 1)[0:1, :]

        def rows_to_lanes(col):
            return jnp.sum(jnp.where(diag, col, 0.0), axis=0, keepdims=True)

        def put(row, v, n):
            out_ref[row:row + 1, 0:LANES] = jnp.where(lane < n, v, 0.0)

        out_ref[...] = jnp.zeros_like(out_ref)
        out_ref[0:1, :] = mix_ref[...]
        for row, ref in ((1, q_ref), (2, k_ref), (4, o_ref)):
            put(row, ref[...] + rolled(ref[...], FOX_HEAD_DIM), FOX_HEAD_DIM)
        put(3, rows_to_lanes(b_ref[...]), FOX_HEADS)
        for row, lane_ref, row_ref in ((5, ac_ref, ar_ref), (6, dc_ref, dr_ref)):
            put(row, rolled(lane_ref[...] + rows_to_lanes(row_ref[...]), LANES - SM_A), GDN_HEADS)
        put(7, go_ref[...], LANES)
        out_ref[8:9, :] = nx_ref[...]
        out_ref[9:10, :] = mem_ref[...]
        put(10, xq_ref[...], LANES)
        put(11, xk_ref[...], LANES)
        out_ref[12:13, :] = mlp_ref[...]
        put(LOSS_ROW, 0.5 * jnp.sum(lt_ref[...], axis=0)[0:1, :], 1)

    args = (g_mix, dgq, dgk, dbias, dgo, dac, dar, ddc, ddr, g_gdn_o, g_nx, g_mem, g_xq, g_xk, g_mlp, loss_tiles)
    return pl.pallas_call(body, name="pack_small", out_shape=jax.ShapeDtypeStruct((PACK_ROWS, D_MODEL), F32))(*args)


LATE_WEIGHTS = (("w_out", "w_cq", "w_ckv", "w_co"), ("w_mlp1", "w_mlp2"))
GRAD_GROUPS = (("w_mlp2", "w_mlp1"), ("w_co", "w_cq", "w_ckv", "w_out"), ("w_in", "gdn_conv_w"))


def _local_step(x, mem, target, norm_mix_g, w_in, fox_qnorm_g, fox_knorm_g, fox_f_bias, fox_onorm_g, gdn_conv_w, gdn_A_log,
                gdn_dt_bias, gdn_onorm_g, norm_xattn_g, mem_norm_g, xattn_qnorm_g, xattn_knorm_g, norm_mlp_g,
                late_weights, grads_ready=None, first_token=0.0):
    if grads_ready is None:
        grads_ready = lambda group: 0.0
    n_batch, s_len, d = x.shape
    m_len = mem.shape[1]
    t_len = n_batch * s_len
    tq = min(FOX_BLOCK, s_len)
    nq = s_len // tq
    n_chunks = s_len // GDN_CHUNK
    x2d = x.reshape(t_len, d)

    wp = jnp.concatenate([w_in[0:1536], w_in[1544:3080], w_in[3088:3600], w_in[1536:1544], w_in[3080:3088],
                          jnp.zeros((P_DIM - 3600, d), BF16)], axis=0)
    wst = jnp.concatenate([w_in[1536:1544], w_in[3080:3088]], axis=0)
    conv_w = jnp.concatenate([gdn_conv_w, jnp.zeros((8 - CONV_WIDTH, gdn_conv_w.shape[1]), F32)], axis=0)
    bias_col = _col(fox_f_bias, SM_F)
    gq2, gk2, go2 = (jnp.tile(g, (1, 2)) for g in (fox_qnorm_g, fox_knorm_g, fox_onorm_g))
    a_c, dt_c = _pad_lanes(gdn_A_log, SM_A), _pad_lanes(gdn_dt_bias, SM_A)
    a_r, dt_r = _col(gdn_A_log, SM_A), _col(gdn_dt_bias, SM_A)

    h1, pfox, pgdn, pz, sm, smt = _in_proj(x2d, norm_mix_g + first_token, wp, wst)
    c_rows = _fox_cum(smt, bias_col, n_batch, s_len)
    cb = c_rows.reshape(SM_ROWS, n_batch, nq, tq).transpose(1, 2, 0, 3)
    pf3 = pfox.reshape(n_batch, s_len, 1536)
    o_fox, oa, lse = _fox_fwd(pf3, cb, gq2, gk2, go2, tq)
    pg3 = pgdn.reshape(n_batch, s_len, 1536)
    qkvn = _gdn_pre(pg3, conv_w)
    z3 = pz.reshape(n_batch, s_len, GDN_WIDTH)
    smc = sm.reshape(n_batch, s_len, LANES)
    smr = smt.reshape(SM_ROWS, n_batch * n_chunks, GDN_CHUNK).transpose(1, 0, 2)
    ob, states = _gdn_fwd(qkvn, z3, smc, smr, a_c, dt_c, a_r, dt_r, gdn_onorm_g)
    oa2, ob2 = oa.reshape(t_len, FOX_WIDTH), ob.reshape(t_len, GDN_WIDTH)
    w_out, w_cq, w_ckv, w_co = late_weights(LATE_WEIGHTS[0], ob2)
    x1, hq, cq = _out_proj(x2d, oa2, ob2, w_out, norm_xattn_g, w_cq)
    mem2d = mem.reshape(n_batch * m_len, d)
    hm, ckv = _mem_kv(mem2d, mem_norm_g, w_ckv)
    co, x2, hf = _xattn_fwd(cq, ckv, x1, xattn_qnorm_g, xattn_knorm_g, w_co, norm_mlp_g, n_batch, s_len, m_len)
    w_mlp1, w_mlp2 = late_weights(LATE_WEIGHTS[1], hf)
    u, a_act, dy, loss_tiles = _mlp_fwd(hf, x2, target.reshape(t_len, d), w_mlp1, w_mlp2)

    grads = {}
    du, dx2, grads["norm_mlp_g"] = _mlp_bwd(dy, u, x2, norm_mlp_g, w_mlp1, w_mlp2)
    grads["w_mlp2"] = _wgrad(a_act, dy, "wgrad_mlp2", bt=2048)
    grads["w_mlp1"] = _wgrad(hf, du, "wgrad_mlp1", bt=2048, column_blocks=D_FF // N_DEV)
    token = grads_ready({k: grads[k] for k in GRAD_GROUPS[0]})
    grads["w_co"] = _wgrad(co, dx2, "wgrad_co", column_blocks=D_MODEL // N_DEV)
    dx1, dcq, dckv, grads["xattn_qnorm_g"], grads["xattn_knorm_g"], grads["norm_xattn_g"] = _xattn_bwd(
        dx2, cq, ckv, x1, xattn_qnorm_g + token, xattn_knorm_g, w_co, norm_xattn_g, w_cq, n_batch, s_len, m_len)
    grads["w_cq"] = _wgrad(hq, dcq, "wgrad_cq")
    grads["w_ckv"] = _wgrad(hm, dckv, "wgrad_ckv")
    grads["mem_norm_g"] = _mem_kv_bwd(dckv, mem2d, mem_norm_g, w_ckv)
    grads["w_out"] = _wgrad_stacked([oa2, ob2], dx1, "wgrad_out", bn=1024)
    token = grads_ready({k: grads[k] for k in GRAD_GROUPS[1]})
    dcat = _out_proj_bwd(dx1, w_out)
    dcat3 = dcat.reshape(n_batch, s_len, d)

    dqkvn, dz, dsmc, dsmr, dac, ddc, dar, ddr, grads["gdn_onorm_g"] = _gdn_bwd(
        qkvn, z3, smc, smr, a_c, dt_c, a_r, dt_r, gdn_onorm_g + token, states, dcat3)
    dpg, dconv = _gdn_pre_bwd(pg3, conv_w, dqkvn)
    grads["gdn_conv_w"] = dconv[0:CONV_WIDTH]

    dq, dk, dv, dcb, dgq, dgk, dgo = _fox_bwd(pf3, cb, gq2, gk2, go2, o_fox, lse, dcat3, tq)
    dc8 = dcb[:, :, :, 0:2, :].transpose(1, 3, 0, 2, 4).reshape(FOX_HEADS, t_len)
    dc_rows = jnp.concatenate([dc8, jnp.zeros((SM_ROWS - FOX_HEADS, t_len), F32)], axis=0)
    dl_rows, dbias = _fox_cum_bwd(dc_rows, smt, bias_col, n_batch, s_len)
    dsm_rows = jnp.concatenate([dl_rows[0:SM_B], dsmr.transpose(1, 0, 2).reshape(SM_ROWS, t_len)[SM_B:SM_ROWS]], axis=0)

    dprojs = [dq.reshape(t_len, FOX_WIDTH), dk.reshape(t_len, FOX_WIDTH), dv.reshape(t_len, FOX_WIDTH),
              dpg.reshape(t_len, 1536), dz.reshape(t_len, GDN_WIDTH), dsmc.reshape(t_len, LANES)]
    dwp = _wgrad_stacked(dprojs, h1, "wgrad_in")
    dwst = _rows_matmul(dsm_rows, h1, "wgrad_in_rows")
    dw_small = dwp[P_SMALL:P_SMALL + SM_ROWS] + dwst
    grads["w_in"] = jnp.concatenate([dwp[0:1536], dw_small[0:8], dwp[1536:3072], dw_small[8:16], dwp[3072:3584]], axis=0)
    token = grads_ready({k: grads[k] for k in GRAD_GROUPS[2]})
    grad_x, grads["norm_mix_g"] = _in_proj_bwd(dprojs, dsm_rows, x2d, norm_mix_g + token, wp, wst, dx1)
    packed = _pack_small(grads["norm_mix_g"], dgq, dgk, dbias, dgo, dac, dar, ddc, ddr, grads["gdn_onorm_g"], grads["norm_xattn_g"],
                         grads["mem_norm_g"], grads["xattn_qnorm_g"], grads["xattn_knorm_g"], grads["norm_mlp_g"], loss_tiles)
    return packed, grad_x.reshape(n_batch, s_len, d), {k: grads[k] for k in SHARDED}


MESH_ID = pl.DeviceIdType.MESH
ANY_SPEC = pl.BlockSpec(memory_space=pl.ANY)


def _place():
    x, y, c = lax.axis_index("x"), lax.axis_index("y"), lax.axis_index("c")
    return x, y, c, [(1 - x, y), (x, 1 - y), (1 - x, 1 - y)]


def _place_own(src_ref, dst_ref):
    def staged(buf, sem):
        for a, b in ((src_ref, buf), (buf, dst_ref)):
            cp = pltpu.make_async_copy(a, b, sem)
            cp.start()
            cp.wait()

    pl.run_scoped(staged, pltpu.VMEM(src_ref.shape, src_ref.dtype), pltpu.SemaphoreType.DMA)


def _all_gather_body(n, ins, outs, send_sems, recv_sems):
    x, y, c, chips = _place()
    me, sibling = (x, y, c), (x, y, 1 - c)

    def copy(a, k, block, to, src=None):
        dst = outs[a].at[4 * block[0] + 2 * block[1] + block[2]]
        return pltpu.make_async_remote_copy(src_ref=dst if src is None else src, dst_ref=dst, send_sem=send_sems.at[a, k],
                                            recv_sem=recv_sems.at[a, k], device_id=to, device_id_type=MESH_ID)

    first = []
    for a in range(n):
        first.append(copy(a, 0, me, sibling, src=ins[a]))
        first += [copy(a, 1 + j, me, (*chip, c), src=ins[a]) for j, chip in enumerate(chips)]
    for cp in first:
        cp.start()
    for a in range(n):
        _place_own(ins[a], outs[a].at[4 * x + 2 * y + c])
    passed = []
    for j, chip in enumerate(chips):
        for a in range(n):
            copy(a, 1 + j, (*chip, c), me).wait_recv()
            fwd = copy(a, 4 + j, (*chip, c), sibling)
            fwd.start()
            passed.append(fwd)
    for a in range(n):
        copy(a, 0, sibling, me).wait_recv()
        for j, chip in enumerate(chips):
            copy(a, 4 + j, (*chip, 1 - c), me).wait_recv()
    for cp in first + passed:
        cp.wait_send()


def _all_gather_hbm(arrs, name):
    n = len(arrs)

    def body(*refs):
        _all_gather_body(n, refs[:n], refs[n:2 * n], refs[2 * n], refs[2 * n + 1])

    return pl.pallas_call(
        body, name=name, in_specs=[ANY_SPEC] * n, out_specs=[ANY_SPEC] * n,
        out_shape=[jax.ShapeDtypeStruct((N_DEV,) + a.shape, a.dtype) for a in arrs],
        scratch_shapes=[pltpu.SemaphoreType.DMA((n, 7)), pltpu.SemaphoreType.DMA((n, 7))],
        compiler_params=pltpu.CompilerParams(vmem_limit_bytes=VMEM_LIMIT),
    )(*arrs)


def _pair_exchange(arrs, name):
    n = len(arrs)

    def body(*refs):
        ins, outs = refs[:n], refs[n:2 * n]
        send_sems, recv_sems = refs[2 * n:]
        x, y, c, _ = _place()
        copies = []
        for a in range(n):
            for chip in range(4):
                copies.append(pltpu.make_async_remote_copy(
                    src_ref=ins[a].at[2 * chip + (1 - c)], dst_ref=outs[a].at[chip], send_sem=send_sems.at[a, chip],
                    recv_sem=recv_sems.at[a, chip], device_id=(x, y, 1 - c), device_id_type=MESH_ID))
        for cp in copies:
            cp.start()
        for cp in copies:
            cp.wait()

    return pl.pallas_call(
        body, name=name, in_specs=[ANY_SPEC] * n, out_specs=[ANY_SPEC] * n,
        out_shape=[jax.ShapeDtypeStruct((4,) + a.shape[1:], a.dtype) for a in arrs],
        scratch_shapes=[pltpu.SemaphoreType.DMA((n, 4)), pltpu.SemaphoreType.DMA((n, 4))],
    )(*arrs)


HBM_SPEC = pl.BlockSpec(memory_space=pltpu.HBM)
SEM_SPEC = pl.BlockSpec(memory_space=pltpu.SEMAPHORE)
DATAFLOW = pltpu.SideEffectType.DATAFLOW_SIDE_EFFECTING


def _in_hbm(arrs):
    return [pltpu.with_memory_space_constraint(a, pltpu.HBM) for a in arrs]


def _copies_start(name, srcs, lands, make_copies, after):
    n = len(srcs)
    n_copies = len(make_copies(srcs, lands, None, None)[0])

    def body(*refs):
        send_sems, recv_sems = refs[2 * n + 1], refs[2 * n + 2]
        for row in make_copies(refs[:n], refs[n:2 * n], send_sems, recv_sems):
            for cp in row:
                cp.start()
        refs[-1][...] = jnp.zeros_like(refs[-1])

    sems = pltpu.SemaphoreType.DMA((n * n_copies,))
    thru = [pltpu.HBM(a.shape, a.dtype) for a in list(srcs) + list(lands)]
    res = pl.pallas_call(
        body, name=name, in_specs=[HBM_SPEC] * (2 * n) + [ANY_SPEC],
        out_specs=(SEM_SPEC, SEM_SPEC, *[HBM_SPEC] * (2 * n), pl.BlockSpec(memory_space=pltpu.VMEM)),
        out_shape=(sems, sems, *thru, jax.ShapeDtypeStruct((8, LANES), F32)),
        input_output_aliases={i: 2 + i for i in range(2 * n)},
        compiler_params=pltpu.CompilerParams(has_side_effects=DATAFLOW),
    )(*_in_hbm(list(srcs) + list(lands)), after)
    return res[0], res[1], list(res[2:2 + n]), list(res[2 + n:2 + 2 * n]), res[-1]


def _copies_wait(name, send_sems, recv_sems, srcs, lands, after, make_copies, own_block=False):
    n = len(srcs)

    def body(*refs):
        if own_block:
            for a in range(n):
                _place_own(refs[a], _own_part(refs[a], refs[3 * n + 3 + a]))
        for row in make_copies(refs[:n], refs[n:2 * n], refs[2 * n], refs[2 * n + 1]):
            for cp in row:
                cp.wait_send()
                cp.wait_recv()

    res = pl.pallas_call(
        body, name=name, in_specs=[HBM_SPEC] * (2 * n) + [SEM_SPEC, SEM_SPEC, ANY_SPEC],
        out_specs=tuple([HBM_SPEC] * (2 * n)),
        out_shape=tuple(pltpu.HBM(a.shape, a.dtype) for a in list(srcs) + list(lands)),
        input_output_aliases={i: i for i in range(2 * n)},
        compiler_params=pltpu.CompilerParams(has_side_effects=DATAFLOW, vmem_limit_bytes=VMEM_LIMIT),
    )(*srcs, *lands, send_sems, recv_sems, after)
    return list(res[:n]), list(res[n:])


def _own_part(src_ref, land_ref):
    me = 4 * lax.axis_index("x") + 2 * lax.axis_index("y") + lax.axis_index("c")
    rows, cols = src_ref.shape
    if land_ref.shape[0] == N_DEV * rows:
        return land_ref.at[pl.ds(pl.multiple_of(me * rows, rows), rows), :]
    return land_ref.at[:, pl.ds(pl.multiple_of(me * cols, cols), cols)]


def _gather_copies(srcs, lands, send_sems, recv_sems):
    if send_sems is None:
        return [[None] * 7]
    x, y, c, _ = _place()
    rows = []
    for a in range(len(srcs)):
        row = []
        for k in range(7):
            r = k + 1
            to = (1 - x if r & 4 else x, 1 - y if r & 2 else y, 1 - c if r & 1 else c)
            row.append(pltpu.make_async_remote_copy(
                src_ref=srcs[a], dst_ref=_own_part(srcs[a], lands[a]), send_sem=send_sems.at[7 * a + k], recv_sem=recv_sems.at[7 * a + k],
                device_id=to, device_id_type=MESH_ID))
        rows.append(row)
    return rows


def _scatter_copies(srcs, lands, send_sems, recv_sems):
    if send_sems is None:
        return [[None] * 7]
    x, y, c, _ = _place()
    rows = []
    for a in range(len(srcs)):
        row = []
        for k in range(7):
            r = k + 1
            to = (1 - x if r & 4 else x, 1 - y if r & 2 else y, 1 - c if r & 1 else c)
            row.append(pltpu.make_async_remote_copy(
                src_ref=srcs[a].at[4 * to[0] + 2 * to[1] + to[2]], dst_ref=lands[a].at[k], send_sem=send_sems.at[7 * a + k],
                recv_sem=recv_sems.at[7 * a + k], device_id=to, device_id_type=MESH_ID))
        rows.append(row)
    return rows


def _chip_copies(srcs, lands, send_sems, recv_sems):
    if send_sems is None:
        return [[None] * 3]
    x, y, c, chips = _place()
    return [[pltpu.make_async_remote_copy(
        src_ref=srcs[a].at[2 * chip[0] + chip[1]], dst_ref=lands[a].at[j], send_sem=send_sems.at[3 * a + j], recv_sem=recv_sems.at[3 * a + j],
        device_id=(*chip, c), device_id_type=MESH_ID) for j, chip in enumerate(chips)] for a in range(len(srcs))]


def _tile(rows, cols):
    if rows <= 256:
        return rows, cols
    tr = 256 if cols <= 512 else 128
    if rows % tr == 0:
        return tr, cols
    return rows, 256


def _pair_sum(core, own, got, name):
    _, rows, cols = own.shape
    tr, tc = _tile(rows, cols)

    def body(c_ref, own_ref, got_ref, o_ref):
        o_ref[0] = own_ref[0] + got_ref[0]

    return pl.pallas_call(
        body, name=name,
        grid_spec=pltpu.PrefetchScalarGridSpec(
            num_scalar_prefetch=1, grid=(4, rows // tr, cols // tc),
            in_specs=[pl.BlockSpec((1, tr, tc), lambda k, i, j, c: (2 * k + c[0], i, j)),
                      pl.BlockSpec((1, tr, tc), lambda k, i, j, c: (k, i, j))],
            out_specs=pl.BlockSpec((1, tr, tc), lambda k, i, j, c: (k, i, j))),
        out_shape=jax.ShapeDtypeStruct((4, rows, cols), F32),
        compiler_params=_cparams(("parallel", "parallel", "parallel")),
    )(core, own, got)


def _adamw(w, g, m, v):
    m_new = ADAM_B1 * m + (1.0 - ADAM_B1) * g
    v_new = ADAM_B2 * v + (1.0 - ADAM_B2) * (g * g)
    m_hat = m_new / (1.0 - ADAM_B1 ** ADAM_STEP)
    v_hat = v_new / (1.0 - ADAM_B2 ** ADAM_STEP)
    delta = -ADAM_LR * (m_hat / (jnp.sqrt(v_hat) + ADAM_EPS) + ADAM_WD * w)
    return delta, m_new, v_new


def _sum_adam(chip, sums, parts, w, m, v, name):
    n_parts, rows, cols = parts.shape
    tr, tc = _tile(rows, cols)

    def body(chip_ref, own_ref, p_ref, w_ref, m_ref, v_ref, g_ref, d_ref, mo_ref, vo_ref):
        g = own_ref[0]
        for k in range(n_parts):
            g = g + p_ref[k]
        g_ref[...] = g
        d_ref[...], mo_ref[...], vo_ref[...] = _adamw(w_ref[...], g, m_ref[...], v_ref[...])

    tile = pl.BlockSpec((tr, tc), lambda i, j, ch: (i, j))
    out = jax.ShapeDtypeStruct((rows, cols), F32)
    return pl.pallas_call(
        body, name=name,
        grid_spec=pltpu.PrefetchScalarGridSpec(
            num_scalar_prefetch=1, grid=(rows // tr, cols // tc),
            in_specs=[pl.BlockSpec((1, tr, tc), lambda i, j, ch: (ch[0], i, j)),
                      pl.BlockSpec((n_parts, tr, tc), lambda i, j, ch: (0, i, j)), tile, tile, tile],
            out_specs=[tile, tile, tile, tile]),
        out_shape=[out, out, out, out],
        compiler_params=_cparams(("parallel", "parallel")),
    )(chip, sums, parts, w, m, v)


SHARDED = ("w_in", "gdn_conv_w", "w_out", "w_cq", "w_ckv", "w_co", "w_mlp1", "w_mlp2")
TRANSPOSED = ("w_in",)
COLUMN_SHARDED = ("gdn_conv_w", "w_co", "w_mlp1")
REPLICATED = ("norm_mix_g", "fox_qnorm_g", "fox_knorm_g", "fox_f_bias", "fox_onorm_g", "gdn_A_log", "gdn_dt_bias", "gdn_onorm_g",
              "norm_xattn_g", "mem_norm_g", "xattn_qnorm_g", "xattn_knorm_g", "norm_mlp_g")
WEIGHTS = ("norm_mix_g", "w_in", "fox_qnorm_g", "fox_knorm_g", "fox_f_bias", "fox_onorm_g", "gdn_conv_w", "gdn_A_log", "gdn_dt_bias",
           "gdn_onorm_g", "w_out", "norm_xattn_g", "mem_norm_g", "w_cq", "w_ckv", "xattn_qnorm_g", "xattn_knorm_g", "w_co",
           "norm_mlp_g", "w_mlp1", "w_mlp2")
PACK_ROWS = 16
LOSS_ROW = len(REPLICATED)


def _whole(name, gathered):
    if name in COLUMN_SHARDED:
        return gathered.transpose(1, 0, 2).reshape(gathered.shape[1], N_DEV * gathered.shape[2])
    return gathered.reshape(N_DEV * gathered.shape[1], gathered.shape[2])


def _whole_shape(name, shard_shape):
    rows, cols = shard_shape
    return (rows, N_DEV * cols) if name in COLUMN_SHARDED else (N_DEV * rows, cols)


def _blocks(name, whole):
    if whole.ndim == 3:
        return whole
    if name in COLUMN_SHARDED:
        rows, cols = whole.shape
        return whole.reshape(rows, N_DEV, cols // N_DEV).transpose(1, 0, 2)
    return whole.reshape(N_DEV, whole.shape[0] // N_DEV, whole.shape[1])


def _adam_small(everyone, ws, ms, vs):
    n_par = len(ws)

    def body(*refs):
        ev_ref = refs[0]
        w_refs, m_refs, v_refs = (refs[1 + j * n_par:1 + (j + 1) * n_par] for j in range(3))
        outs = refs[1 + 3 * n_par:-1]
        sum_ref = refs[-1]
        total = ev_ref[0]
        for dev in range(1, N_DEV):
            total = total + ev_ref[dev]
        sum_ref[...] = total
        for i in range(n_par):
            n = w_refs[i].shape[1]
            g = sum_ref[i:i + 1, 0:n]
            outs[4 * i][...] = g
            outs[4 * i + 1][...], outs[4 * i + 2][...], outs[4 * i + 3][...] = _adamw(w_refs[i][...], g, m_refs[i][...], v_refs[i][...])
        outs[4 * n_par][...] = sum_ref[LOSS_ROW:LOSS_ROW + 1, 0:1]

    shapes = [jax.ShapeDtypeStruct(a.shape, F32) for a in ws for _ in range(4)] + [jax.ShapeDtypeStruct((1, 1), F32)]
    return pl.pallas_call(body, name="adam_small", out_shape=shapes,
                          scratch_shapes=[pltpu.VMEM((PACK_ROWS, D_MODEL), F32)])(everyone, *ws, *ms, *vs)


def kernel(x, mem, norm_mix_g, w_in, fox_qnorm_g, fox_knorm_g, fox_f_bias, fox_onorm_g, gdn_conv_w, gdn_A_log, gdn_dt_bias, gdn_onorm_g, w_out, norm_xattn_g, mem_norm_g, w_cq, w_ckv, xattn_qnorm_g, xattn_knorm_g, w_co, norm_mlp_g, w_mlp1, w_mlp2, loss_target, m_norm_mix_g, m_w_in, m_fox_qnorm_g, m_fox_knorm_g, m_fox_f_bias, m_fox_onorm_g, m_gdn_conv_w, m_gdn_A_log, m_gdn_dt_bias, m_gdn_onorm_g, m_w_out, m_norm_xattn_g, m_mem_norm_g, m_w_cq, m_w_ckv, m_xattn_qnorm_g, m_xattn_knorm_g, m_w_co, m_norm_mlp_g, m_w_mlp1, m_w_mlp2, v_norm_mix_g, v_w_in, v_fox_qnorm_g, v_fox_knorm_g, v_fox_f_bias, v_fox_onorm_g, v_gdn_conv_w, v_gdn_A_log, v_gdn_dt_bias, v_gdn_onorm_g, v_w_out, v_norm_xattn_g, v_mem_norm_g, v_w_cq, v_w_ckv, v_xattn_qnorm_g, v_xattn_knorm_g, v_w_co, v_norm_mlp_g, v_w_mlp1, v_w_mlp2):
    given = dict(locals())
    w = {k: given[k] for k in WEIGHTS}
    m = {k: given["m_" + k] for k in WEIGHTS}
    v = {k: given["v_" + k] for k in WEIGHTS}

    core = lax.axis_index("c").astype(jnp.int32).reshape(1)
    chip = (2 * lax.axis_index("x") + lax.axis_index("y")).astype(jnp.int32).reshape(1)
    me = 4 * lax.axis_index("x") + 2 * lax.axis_index("y") + lax.axis_index("c")

    local = lambda d: {k: jnp.transpose(d[k][0]) if k in TRANSPOSED else d[k][0] for k in SHARDED}
    w2, m2, v2 = local(w), local(m), local(v)
    shards = {k: w2[k] if k == "gdn_conv_w" else w2[k].astype(BF16) for k in SHARDED}
    early = [k for k in SHARDED if not any(k in group for group in LATE_WEIGHTS)]
    gathered = _all_gather_hbm([shards[k] for k in early], "gather_early")
    whole = {k: _whole(k, g) for k, g in zip(early, gathered)}
    gathers, after = {}, gathered[0]
    for i, group in enumerate(LATE_WEIGHTS):
        lands = [lax.empty(_whole_shape(k, shards[k].shape), BF16) for k in group]
        gathers[group] = _copies_start("gather_late_start_" + str(i), [shards[k] for k in group], lands, _gather_copies, after=after)
        after = gathers[group][4]
    first_token = after[0, 0]

    def late_weights(group, after):
        gather = gathers[group]
        _, lands = _copies_wait("gather_late_wait_" + str(LATE_WEIGHTS.index(group)), gather[0], gather[1], gather[2], gather[3],
                                after, _gather_copies, own_block=True)
        return lands

    pending = []

    def grads_ready(group):
        names = list(group)
        tag = str(len(pending))
        own = [_blocks(k, group[k]) for k in names]
        if "w_in" in names:
            got = _pair_exchange(own, "grad_pair_exchange_" + tag)
            srcs = [_pair_sum(core, o, g, "grad_pair_sum_" + k) for k, o, g in zip(names, own, got)]
            copies, index, n_parts = _chip_copies, chip, 3
        else:
            srcs, copies, index, n_parts = own, _scatter_copies, me.astype(jnp.int32).reshape(1), 7
        lands = [lax.empty((n_parts,) + s.shape[1:], s.dtype) for s in srcs]
        started = _copies_start("grad_exchange_start_" + tag, srcs, lands, copies, after=core)
        pending.append((names, started, copies, index))
        return started[4][0, 0]

    small = {k: w[k] for k in REPLICATED}
    packed, grad_x, _ = _local_step(x, mem, loss_target, **small, **whole, late_weights=late_weights,
                                    grads_ready=grads_ready, first_token=first_token)

    small_lands = [lax.empty((N_DEV * PACK_ROWS, D_MODEL), F32)]
    small_gather = _copies_start("gather_small_start", [packed], small_lands, _gather_copies, after=grad_x)

    out_g, out_d, out_m, out_v = {}, {}, {}, {}
    after = small_gather[4]
    for tag, (names, started, copies, index) in enumerate(pending):
        srcs, parts = _copies_wait("grad_exchange_wait_" + str(tag), started[0], started[1], started[2], started[3], after, copies)
        for k, s, p in zip(names, srcs, parts):
            res = _sum_adam(index, s, p, w2[k], m2[k], v2[k], "adam_" + k)
            out_g[k], out_d[k], out_m[k], out_v[k] = ((jnp.transpose(r) if k in TRANSPOSED else r)[None] for r in res)
            after = res[0]

    _, (everyone,) = _copies_wait("gather_small_wait", small_gather[0], small_gather[1], small_gather[2], small_gather[3], after,
                                  _gather_copies, own_block=True)
    res = _adam_small(everyone.reshape(N_DEV, PACK_ROWS, D_MODEL), [w[k] for k in REPLICATED], [m[k] for k in REPLICATED],
                      [v[k] for k in REPLICATED])
    for i, k in enumerate(REPLICATED):
        out_g[k], out_d[k], out_m[k], out_v[k] = res[4 * i:4 * i + 4]
    loss = res[-1].reshape(())

    return (loss, grad_x, *[out_g[k] for k in WEIGHTS], *[out_d[k] for k in WEIGHTS], *[out_m[k] for k in WEIGHTS],
            *[out_v[k] for k in WEIGHTS])
```

```python
import functools

import jax
import jax.numpy as jnp
import numpy as np
from jax import lax
from jax.experimental import pallas as pl
from jax.experimental.pallas import tpu as pltpu

F32 = jnp.float32
BF16 = jnp.bfloat16

D_MODEL = 1024
FOX_HEADS = 8
FOX_HEAD_DIM = 64
FOX_WIDTH = 512
GDN_HEADS = 4
GDN_HEAD_DIM = 128
GDN_WIDTH = 512
CONV_WIDTH = 4
GDN_CHUNK = 128
GDN_GROUP = 4
FOX_BLOCK = 512
XATTN_HEADS = 4
XATTN_HEAD_DIM = 128
XATTN_WIDTH = 512
D_FF = 4096
EPS = 1e-6
NEG_INF = -1e30
N_DEV = 8

ADAM_LR = 0.001
ADAM_B1 = 0.9
ADAM_B2 = 0.999
ADAM_EPS = 1e-08
ADAM_WD = 0.01
ADAM_STEP = 10

P_FOX = 0
P_GDN = 1536
P_Z = 3072
P_SMALL = 3584
P_DIM = 3712
SM_F = 0
SM_B = 8
SM_A = 12
SM_ROWS = 16

LANES = 128
VMEM_LIMIT = 56 * 1024 * 1024

NN = (((1,), (0,)), ((), ()))
NT = (((1,), (1,)), ((), ()))
TN = (((0,), (0,)), ((), ()))


def _dot(a, b, dims=NN):
    return lax.dot_general(a.astype(BF16), b.astype(BF16), dims, preferred_element_type=F32)


def _cparams(sem=None):
    kw = dict(vmem_limit_bytes=VMEM_LIMIT)
    if sem is not None:
        kw["dimension_semantics"] = sem
    return pltpu.CompilerParams(**kw)


def _sigmoid(x):
    return 0.5 * (jnp.tanh(0.5 * x) + 1.0)


def _softplus(x):
    return jnp.maximum(x, 0.0) + jnp.log1p(jnp.exp(-jnp.abs(x)))


def _log_sigmoid(x):
    return -_softplus(-x)


def _rms(x, g):
    r = lax.rsqrt(jnp.mean(x * x, axis=-1, keepdims=True) + EPS)
    return x * r * g


def _rms_bwd(x, g, dy):
    r = lax.rsqrt(jnp.mean(x * x, axis=-1, keepdims=True) + EPS)
    xh = x * r
    dg = jnp.sum(dy * xh, axis=0, keepdims=True)
    dyg = dy * g
    dx = r * (dyg - xh * jnp.mean(dyg * xh, axis=-1, keepdims=True))
    return dx, dg


def _pair_stat(t, m0):
    s0 = jnp.sum(jnp.where(m0, t, 0.0), axis=-1, keepdims=True)
    s1 = jnp.sum(jnp.where(m0, 0.0, t), axis=-1, keepdims=True)
    return jnp.where(m0, s0, s1)


def _rms_pair(x, g, m0):
    r = lax.rsqrt(_pair_stat(x * x, m0) * (1.0 / FOX_HEAD_DIM) + EPS)
    return x * r * g


def _rms_pair_bwd(x, g, dy, m0):
    r = lax.rsqrt(_pair_stat(x * x, m0) * (1.0 / FOX_HEAD_DIM) + EPS)
    xh = x * r
    dg = jnp.sum(dy * xh, axis=0, keepdims=True)
    dyg = dy * g
    dx = r * (dyg - xh * (_pair_stat(dyg * xh, m0) * (1.0 / FOX_HEAD_DIM)))
    return dx, dg


@jax.custom_vjp
def _mm_nn(a, b):
    return _dot(a, b, NN)


_mm_nn.defvjp(lambda a, b: (_dot(a, b, NN), (a, b)),
              lambda r, g: (_dot(g, r[1], NT), _dot(r[0], g, TN)))


@jax.custom_vjp
def _mm_nt(a, b):
    return _dot(a, b, NT)


_mm_nt.defvjp(lambda a, b: (_dot(a, b, NT), (a, b)),
              lambda r, g: (_dot(g, r[1], NN), _dot(g, r[0], TN)))


@jax.custom_vjp
def _mm_tn(a, b):
    return _dot(a, b, TN)


_mm_tn.defvjp(lambda a, b: (_dot(a, b, TN), (a, b)),
              lambda r, g: (_dot(r[1], g, NT), _dot(r[0], g, NN)))


def _dot3(a, b, dims):
    ah = a.astype(BF16)
    al = (a - ah.astype(F32)).astype(BF16)
    bh = b.astype(BF16)
    bl = (b - bh.astype(F32)).astype(BF16)
    d = functools.partial(lax.dot_general, dimension_numbers=dims, preferred_element_type=F32)
    return d(ah, bh) + d(ah, bl) + d(al, bh)


def _neumann_inverses(mats):
    c = mats[0].shape[0]
    eye = (lax.broadcasted_iota(jnp.int32, (c, c), 0) == lax.broadcasted_iota(jnp.int32, (c, c), 1)).astype(F32)
    xs = [eye - a for a in mats]
    ps = list(mats)
    k = 2
    while k < c + 1:
        ps = [_dot3(p, p, NN) for p in ps]
        xs = [x + _dot3(x, p, NN) for x, p in zip(xs, ps)]
        k *= 2
    return xs


@jax.custom_vjp
def _unit_lower_inverses(mats):
    return _neumann_inverses(mats)


def _unit_lower_inverses_fwd(mats):
    ts = _neumann_inverses(mats)
    return ts, ts


def _unit_lower_inverses_bwd(ts, gs):
    left = [_dot3(t, g, TN) for t, g in zip(ts, gs)]
    return ([-_dot3(m, t, NT) for m, t in zip(left, ts)],)


_unit_lower_inverses.defvjp(_unit_lower_inverses_fwd, _unit_lower_inverses_bwd)


def _wgrad(a, b, name, bk=1024, bn=1024, bt=1024, column_blocks=None):
    t_len, k_len = a.shape
    n_len = b.shape[1]
    bk, bn, bt = min(bk, k_len), min(bn, n_len), min(bt, t_len)
    nt = t_len // bt

    def body(a_ref, b_ref, o_ref, acc_ref):
        t = pl.program_id(2)

        @pl.when(t == 0)
        def _():
            acc_ref[...] = jnp.zeros_like(acc_ref)

        acc_ref[...] += _dot(a_ref[...], b_ref[...], TN)

        @pl.when(t == nt - 1)
        def _():
            if column_blocks:
                for jj in range(bn // column_blocks):
                    o_ref[jj] = acc_ref[:, jj * column_blocks:(jj + 1) * column_blocks]
            else:
                o_ref[...] = acc_ref[...]

    if column_blocks:
        out_spec = pl.BlockSpec((bn // column_blocks, bk, column_blocks), lambda i, j, t: (j, i, 0))
        out_shape = jax.ShapeDtypeStruct((n_len // column_blocks, k_len, column_blocks), F32)
    else:
        out_spec = pl.BlockSpec((bk, bn), lambda i, j, t: (i, j))
        out_shape = jax.ShapeDtypeStruct((k_len, n_len), F32)
    return pl.pallas_call(
        body, name=name, grid=(k_len // bk, n_len // bn, nt),
        in_specs=[pl.BlockSpec((bt, bk), lambda i, j, t: (t, i)), pl.BlockSpec((bt, bn), lambda i, j, t: (t, j))],
        out_specs=out_spec, out_shape=out_shape,
        scratch_shapes=[pltpu.VMEM((bk, bn), F32)],
        compiler_params=_cparams(("parallel", "parallel", "arbitrary")),
    )(a, b)


def _wgrad_stacked(pieces, b, name, bn=512, bt=1024):
    t_len, n_len = b.shape
    n_p = len(pieces)
    starts = [int(s) for s in np.cumsum([0] + [p.shape[1] for p in pieces])]
    bn, bt = min(bn, n_len), min(bt, t_len)
    nt = t_len // bt

    def body(*refs):
        b_ref, o_ref, acc_ref = refs[n_p:]
        t = pl.program_id(1)

        @pl.when(t == 0)
        def _():
            acc_ref[...] = jnp.zeros_like(acc_ref)

        for k in range(n_p):
            acc_ref[starts[k]:starts[k + 1], :] += _dot(refs[k][...], b_ref[...], TN)

        @pl.when(t == nt - 1)
        def _():
            o_ref[...] = acc_ref[...]

    return pl.pallas_call(
        body, name=name, grid=(n_len // bn, nt),
        in_specs=[pl.BlockSpec((bt, p.shape[1]), lambda j, t: (t, 0)) for p in pieces] + [pl.BlockSpec((bt, bn), lambda j, t: (t, j))],
        out_specs=pl.BlockSpec((starts[-1], bn), lambda j, t: (0, j)),
        out_shape=jax.ShapeDtypeStruct((starts[-1], n_len), F32),
        scratch_shapes=[pltpu.VMEM((starts[-1], bn), F32)],
        compiler_params=_cparams(("parallel", "arbitrary")),
    )(*pieces, b)


def _rows_matmul(a, b, name, bt=512):
    r_len, t_len = a.shape
    n_len = b.shape[1]
    bt = min(bt, t_len)
    nt = t_len // bt

    def body(a_ref, b_ref, o_ref):
        t = pl.program_id(0)

        @pl.when(t == 0)
        def _():
            o_ref[...] = jnp.zeros_like(o_ref)

        o_ref[...] += _dot(a_ref[...], b_ref[...], NN)

    return pl.pallas_call(
        body, name=name, grid=(nt,),
        in_specs=[pl.BlockSpec((r_len, bt), lambda t: (0, t)), pl.BlockSpec((bt, n_len), lambda t: (t, 0))],
        out_specs=pl.BlockSpec((r_len, n_len), lambda t: (0, 0)),
        out_shape=jax.ShapeDtypeStruct((r_len, n_len), F32),
        compiler_params=_cparams(("arbitrary",)),
    )(a, b)


def _in_proj(x, g, wp, wst, tm=512):
    t_len, d = x.shape
    tm = min(tm, t_len)

    def body(x_ref, g_ref, wp_ref, wst_ref, h_ref, fox_ref, gdn_ref, z_ref, sm_ref, smt_ref):
        h = _rms(x_ref[...], g_ref[...]).astype(BF16)
        h_ref[...] = h
        p = _dot(h, wp_ref[...], NT)
        fox_ref[...] = p[:, P_FOX:P_GDN]
        gdn_ref[...] = p[:, P_GDN:P_Z]
        z_ref[...] = p[:, P_Z:P_SMALL]
        sm_ref[...] = p[:, P_SMALL:P_DIM]
        smt_ref[...] = _dot(wst_ref[...], h, NT)

    row = lambda i: (i, 0)
    fixed = lambda i: (0, 0)
    return pl.pallas_call(
        body, name="in_proj", grid=(t_len // tm,),
        in_specs=[pl.BlockSpec((tm, d), row), pl.BlockSpec((1, d), fixed), _resident((P_DIM, d)),
                  pl.BlockSpec((SM_ROWS, d), fixed)],
        out_specs=[pl.BlockSpec((tm, d), row), pl.BlockSpec((tm, 1536), row), pl.BlockSpec((tm, 1536), row),
                   pl.BlockSpec((tm, 512), row), pl.BlockSpec((tm, LANES), row), pl.BlockSpec((SM_ROWS, tm), lambda i: (0, i))],
        out_shape=[jax.ShapeDtypeStruct((t_len, d), BF16), jax.ShapeDtypeStruct((t_len, 1536), F32),
                   jax.ShapeDtypeStruct((t_len, 1536), F32), jax.ShapeDtypeStruct((t_len, 512), F32),
                   jax.ShapeDtypeStruct((t_len, LANES), F32), jax.ShapeDtypeStruct((SM_ROWS, t_len), F32)],
        compiler_params=_cparams(("parallel",)),
    )(x, g, wp, wst)


def _in_proj_bwd(dprojs, dsmt, x, g, wp, wst, dx1, tm=512):
    t_len, d = x.shape
    tm = min(tm, t_len)
    n_p = len(dprojs)
    starts = np.cumsum([0] + [p.shape[1] for p in dprojs])

    def body(*refs):
        dp_refs = refs[:n_p]
        dst_ref, x_ref, g_ref, wp_ref, wst_ref, dx1_ref, dx_ref, dg_ref = refs[n_p:]
        i = pl.program_id(0)
        dh = _dot(dst_ref[...], wst_ref[...], TN)
        for k in range(n_p):
            dh = dh + _dot(dp_refs[k][...], wp_ref[int(starts[k]):int(starts[k + 1]), :], NN)
        dxn, dg = _rms_bwd(x_ref[...], g_ref[...], dh)
        dx_ref[...] = dx1_ref[...] + dxn

        @pl.when(i == 0)
        def _():
            dg_ref[...] = jnp.zeros_like(dg_ref)

        dg_ref[...] += dg

    row = lambda i: (i, 0)
    fixed = lambda i: (0, 0)
    return pl.pallas_call(
        body, name="in_proj_bwd", grid=(t_len // tm,),
        in_specs=[pl.BlockSpec((tm, p.shape[1]), row) for p in dprojs] + [
            pl.BlockSpec((SM_ROWS, tm), lambda i: (0, i)), pl.BlockSpec((tm, d), row),
            pl.BlockSpec((1, d), fixed), _resident((P_DIM, d)), pl.BlockSpec((SM_ROWS, d), fixed),
            pl.BlockSpec((tm, d), row)],
        out_specs=[pl.BlockSpec((tm, d), row), pl.BlockSpec((1, d), fixed)],
        out_shape=[jax.ShapeDtypeStruct((t_len, d), F32), jax.ShapeDtypeStruct((1, d), F32)],
        compiler_params=_cparams(("arbitrary",)),
    )(*dprojs, dsmt, x, g, wp, wst, dx1)


def _fox_cum(smt, bias_col, n_batch, s_len, ck=256):
    ck = min(ck, s_len)

    def body(s_ref, b_ref, c_ref):
        tri = (lax.broadcasted_iota(jnp.int32, (ck, ck), 0) <= lax.broadcasted_iota(jnp.int32, (ck, ck), 1)).astype(F32)
        carry = jnp.zeros((SM_ROWS, 1), F32)
        for r in range(s_len // ck):
            ls = _log_sigmoid(s_ref[:, r * ck:(r + 1) * ck] + b_ref[...])
            c = jnp.dot(ls, tri, precision=lax.Precision.HIGHEST, preferred_element_type=F32) + carry
            c_ref[:, r * ck:(r + 1) * ck] = c
            carry = c[:, ck - 1:ck]

    return pl.pallas_call(
        body, name="fox_cum", grid=(n_batch,),
        in_specs=[pl.BlockSpec((SM_ROWS, s_len), lambda b: (0, b)), pl.BlockSpec((SM_ROWS, 1), lambda b: (0, 0))],
        out_specs=pl.BlockSpec((SM_ROWS, s_len), lambda b: (0, b)),
        out_shape=jax.ShapeDtypeStruct(smt.shape, F32),
        compiler_params=_cparams(("parallel",)),
    )(smt, bias_col)


def _fox_cum_bwd(dc, smt, bias_col, n_batch, s_len, ck=256):
    ck = min(ck, s_len)
    nr = s_len // ck

    def body(dc_ref, s_ref, b_ref, dl_ref, db_ref):
        b = pl.program_id(0)
        tri = (lax.broadcasted_iota(jnp.int32, (ck, ck), 0) >= lax.broadcasted_iota(jnp.int32, (ck, ck), 1)).astype(F32)
        carry = jnp.zeros((SM_ROWS, 1), F32)
        tot = jnp.zeros((SM_ROWS, 1), F32)
        for r in reversed(range(nr)):
            sl = slice(r * ck, (r + 1) * ck)
            dls = jnp.dot(dc_ref[:, sl], tri, precision=lax.Precision.HIGHEST, preferred_element_type=F32) + carry
            carry = dls[:, 0:1]
            dl = dls * (1.0 - _sigmoid(s_ref[:, sl] + b_ref[...]))
            dl_ref[:, sl] = dl
            tot = tot + jnp.sum(dl, axis=1, keepdims=True)

        @pl.when(b == 0)
        def _():
            db_ref[...] = jnp.zeros_like(db_ref)

        db_ref[...] += jnp.broadcast_to(tot, db_ref.shape)

    return pl.pallas_call(
        body, name="fox_cum_bwd", grid=(n_batch,),
        in_specs=[pl.BlockSpec((SM_ROWS, s_len), lambda b: (0, b)), pl.BlockSpec((SM_ROWS, s_len), lambda b: (0, b)),
                  pl.BlockSpec((SM_ROWS, 1), lambda b: (0, 0))],
        out_specs=[pl.BlockSpec((SM_ROWS, s_len), lambda b: (0, b)), pl.BlockSpec((SM_ROWS, LANES), lambda b: (0, 0))],
        out_shape=[jax.ShapeDtypeStruct(smt.shape, F32), jax.ShapeDtypeStruct((SM_ROWS, LANES), F32)],
        compiler_params=_cparams(("arbitrary",)),
    )(dc, smt, bias_col)


def _fox_diagonal_mask(tq):
    return lax.broadcasted_iota(jnp.int32, (tq, tq), 1) <= lax.broadcasted_iota(jnp.int32, (tq, tq), 0)


def _fox_fwd(pf, cb, gq2, gk2, go2, tq=256):
    n_batch, s_len, _ = pf.shape
    tq = min(tq, s_len)
    nq = s_len // tq
    scale = FOX_HEAD_DIM ** -0.5

    def body(q_ref, k_ref, v_ref, c_ref, gq_ref, gk_ref, go_ref, o_ref, on_ref, lse_ref, kh_ref, vh_ref):
        j = pl.program_id(1)
        i = pl.program_id(2)
        m0 = lax.broadcasted_iota(jnp.int32, (1, LANES), 1) < FOX_HEAD_DIM

        @pl.when(i == 0)
        def _():
            kn = _rms_pair(k_ref[0], gk_ref[...], m0)
            kh_ref[0] = jnp.where(m0, kn, 0.0).astype(BF16)
            kh_ref[1] = jnp.where(m0, 0.0, kn).astype(BF16)
            v = v_ref[0]
            vh_ref[0] = jnp.where(m0, v, 0.0).astype(BF16)
            vh_ref[1] = jnp.where(m0, 0.0, v).astype(BF16)

        qb = (_rms_pair(q_ref[0], gq_ref[...], m0) * scale).astype(BF16)

        def step(kb, carry, diagonal=False):
            ms, ls, acc = carry
            off = pl.multiple_of(kb * tq, tq)
            new_m, new_l, alphas, pv = [], [], [], []
            for hh in range(2):
                s = _dot(qb, kh_ref[hh, pl.ds(off, tq), :], NT)
                s = s - c_ref[0, kb, pl.ds(2 * j + hh, 1), :]
                if diagonal:
                    s = jnp.where(_fox_diagonal_mask(tq), s, NEG_INF)
                m_new = jnp.maximum(ms[hh], jnp.max(s, axis=-1, keepdims=True))
                alpha = jnp.exp(ms[hh] - m_new)
                p = jnp.exp(s - m_new)
                new_l.append(alpha * ls[hh] + jnp.sum(p, axis=-1, keepdims=True))
                new_m.append(m_new)
                alphas.append(alpha)
                pv.append(_dot(p, vh_ref[hh, pl.ds(off, tq), :], NN))
            acc = jnp.where(m0, alphas[0], alphas[1]) * acc + pv[0] + pv[1]
            return tuple(new_m), tuple(new_l), acc

        init_m = (jnp.full((tq, 1), NEG_INF, F32),) * 2
        init_l = (jnp.zeros((tq, 1), F32),) * 2
        carry = lax.fori_loop(0, i, step, (init_m, init_l, jnp.zeros((tq, LANES), F32)))
        ms, ls, acc = step(i, carry, diagonal=True)
        o = acc / jnp.where(m0, ls[0], ls[1])
        o_ref[0] = o
        on_ref[0] = _rms_pair(o, go_ref[...], m0).astype(BF16)
        lse_ref[0] = jnp.where(m0, ms[0] + jnp.log(ls[0]), ms[1] + jnp.log(ls[1]))

    fixed = lambda b, j, i: (0, 0)
    tile = lambda b, j, i: (b, i, j)
    return pl.pallas_call(
        body, name="fox_fwd", grid=(n_batch, 4, nq),
        in_specs=[pl.BlockSpec((1, tq, LANES), tile), pl.BlockSpec((1, s_len, LANES), lambda b, j, i: (b, 0, 4 + j)),
                  pl.BlockSpec((1, s_len, LANES), lambda b, j, i: (b, 0, 8 + j)),
                  pl.BlockSpec((1, nq, SM_ROWS, tq), lambda b, j, i: (b, 0, 0, 0)),
                  pl.BlockSpec((1, LANES), fixed), pl.BlockSpec((1, LANES), fixed), pl.BlockSpec((1, LANES), fixed)],
        out_specs=[pl.BlockSpec((1, tq, LANES), tile), pl.BlockSpec((1, tq, LANES), tile), pl.BlockSpec((1, tq, LANES), tile)],
        out_shape=[jax.ShapeDtypeStruct((n_batch, s_len, FOX_WIDTH), F32), jax.ShapeDtypeStruct((n_batch, s_len, FOX_WIDTH), BF16),
                   jax.ShapeDtypeStruct((n_batch, s_len, FOX_WIDTH), F32)],
        scratch_shapes=[pltpu.VMEM((2, s_len, LANES), BF16), pltpu.VMEM((2, s_len, LANES), BF16)],
        compiler_params=_cparams(("parallel", "parallel", "arbitrary")),
    )(pf, pf, pf, cb, gq2, gk2, go2)


def _fox_bwd(pf, cb, gq2, gk2, go2, o, lse, don, tq=256):
    n_batch, s_len, _ = pf.shape
    tq = min(tq, s_len)
    nq = s_len // tq
    scale = FOX_HEAD_DIM ** -0.5

    def body(q_ref, k_ref, v_ref, c_ref, gq_ref, gk_ref, go_ref, o_ref, lse_ref, don_ref,
             dq_ref, dk_ref, dv_ref, dc_ref, dgq_ref, dgk_ref, dgo_ref, kh_ref, vh_ref, dka_ref, dva_ref, dca_ref):
        b = pl.program_id(0)
        j = pl.program_id(1)
        i = pl.program_id(2)
        m0 = lax.broadcasted_iota(jnp.int32, (1, LANES), 1) < FOX_HEAD_DIM

        @pl.when((b == 0) & (j == 0) & (i == 0))
        def _():
            dgq_ref[...] = jnp.zeros_like(dgq_ref)
            dgk_ref[...] = jnp.zeros_like(dgk_ref)
            dgo_ref[...] = jnp.zeros_like(dgo_ref)

        @pl.when(i == 0)
        def _():
            kn = _rms_pair(k_ref[0], gk_ref[...], m0)
            kh_ref[0] = jnp.where(m0, kn, 0.0).astype(BF16)
            kh_ref[1] = jnp.where(m0, 0.0, kn).astype(BF16)
            v = v_ref[0]
            vh_ref[0] = jnp.where(m0, v, 0.0).astype(BF16)
            vh_ref[1] = jnp.where(m0, 0.0, v).astype(BF16)
            dka_ref[...] = jnp.zeros_like(dka_ref)
            dva_ref[...] = jnp.zeros_like(dva_ref)
            dca_ref[...] = jnp.zeros_like(dca_ref)

        q = q_ref[0]
        qn = _rms_pair(q, gq_ref[...], m0)
        qs = qn * scale
        qb = qs.astype(BF16)
        qh = (jnp.where(m0, qs, 0.0).astype(BF16), jnp.where(m0, 0.0, qs).astype(BF16))
        ot = o_ref[0]
        do, dgo = _rms_pair_bwd(ot, go_ref[...], don_ref[0], m0)
        dgo_ref[...] += dgo
        dd = do * ot
        delta = (jnp.sum(jnp.where(m0, dd, 0.0), axis=-1, keepdims=True), jnp.sum(jnp.where(m0, 0.0, dd), axis=-1, keepdims=True))
        doh = (jnp.where(m0, do, 0.0).astype(BF16), jnp.where(m0, 0.0, do).astype(BF16))
        lse_t = lse_ref[0]
        lse_h = (lse_t[:, 0:1], lse_t[:, FOX_HEAD_DIM:FOX_HEAD_DIM + 1])

        def step(kb, carry, diagonal=False):
            dqn, rs = carry
            rs = list(rs)
            off = pl.multiple_of(kb * tq, tq)
            for hh in range(2):
                kblk = kh_ref[hh, pl.ds(off, tq), :]
                vblk = vh_ref[hh, pl.ds(off, tq), :]
                s = _dot(qb, kblk, NT)
                s = s - c_ref[0, kb, pl.ds(2 * j + hh, 1), :]
                if diagonal:
                    s = jnp.where(_fox_diagonal_mask(tq), s, NEG_INF)
                p = jnp.exp(s - lse_h[hh])
                dp = _dot(doh[hh], vblk, NT)
                ds = p * (dp - delta[hh])
                dva_ref[pl.ds(off, tq), :] += _dot(p, doh[hh], TN)
                dka_ref[pl.ds(off, tq), :] += _dot(ds, qh[hh], TN)
                dca_ref[kb, hh:hh + 1, :] += -jnp.sum(ds, axis=0, keepdims=True)
                rs[hh] = rs[hh] + jnp.sum(ds, axis=-1, keepdims=True)
                dqn = dqn + _dot(ds, kblk, NN)
            return dqn, tuple(rs)

        carry = lax.fori_loop(0, i, step, (jnp.zeros((tq, LANES), F32), (jnp.zeros((tq, 1), F32),) * 2))
        dqn, rs = step(i, carry, diagonal=True)
        dqn = dqn * scale
        rs_rows = jnp.where(m0, rs[0], rs[1]).T
        dca_ref[i, 0:1, :] += rs_rows[0:1, :]
        dca_ref[i, 1:2, :] += rs_rows[FOX_HEAD_DIM:FOX_HEAD_DIM + 1, :]
        dq, dgq = _rms_pair_bwd(q, gq_ref[...], dqn, m0)
        dq_ref[0] = dq.astype(BF16)
        dgq_ref[...] += dgq

        @pl.when(i == nq - 1)
        def _():
            dk, dgk = _rms_pair_bwd(k_ref[0], gk_ref[...], dka_ref[...], m0)
            dk_ref[0] = dk.astype(BF16)
            dgk_ref[...] += dgk
            dv_ref[0] = dva_ref[...].astype(BF16)
            dc_ref[0, 0] = dca_ref[...]

    fixed = lambda b, j, i: (0, 0)
    tile = lambda b, j, i: (b, i, j)
    full = lambda b, j, i: (b, 0, j)
    wide = jax.ShapeDtypeStruct((n_batch, s_len, FOX_WIDTH), BF16)
    gain = jax.ShapeDtypeStruct((1, LANES), F32)
    return pl.pallas_call(
        body, name="fox_bwd", grid=(n_batch, 4, nq),
        in_specs=[pl.BlockSpec((1, tq, LANES), tile), pl.BlockSpec((1, s_len, LANES), lambda b, j, i: (b, 0, 4 + j)),
                  pl.BlockSpec((1, s_len, LANES), lambda b, j, i: (b, 0, 8 + j)),
                  pl.BlockSpec((1, nq, SM_ROWS, tq), lambda b, j, i: (b, 0, 0, 0)),
                  pl.BlockSpec((1, LANES), fixed), pl.BlockSpec((1, LANES), fixed), pl.BlockSpec((1, LANES), fixed),
                  pl.BlockSpec((1, tq, LANES), tile), pl.BlockSpec((1, tq, LANES), tile), pl.BlockSpec((1, tq, LANES), tile)],
        out_specs=[pl.BlockSpec((1, tq, LANES), tile), pl.BlockSpec((1, s_len, LANES), full), pl.BlockSpec((1, s_len, LANES), full),
                   pl.BlockSpec((1, 1, nq, 8, tq), lambda b, j, i: (b, j, 0, 0, 0)),
                   pl.BlockSpec((1, LANES), fixed), pl.BlockSpec((1, LANES), fixed), pl.BlockSpec((1, LANES), fixed)],
        out_shape=[wide, wide, wide, jax.ShapeDtypeStruct((n_batch, 4, nq, 8, tq), F32), gain, gain, gain],
        scratch_shapes=[pltpu.VMEM((2, s_len, LANES), BF16), pltpu.VMEM((2, s_len, LANES), BF16),
                        pltpu.VMEM((s_len, LANES), F32), pltpu.VMEM((s_len, LANES), F32), pltpu.VMEM((nq, 8, tq), F32)],
        compiler_params=_cparams(("arbitrary", "arbitrary", "arbitrary")),
    )(pf, pf, pf, cb, gq2, gk2, go2, o, lse, don)


def _shift_down(x, k):
    row = lax.broadcasted_iota(jnp.int32, x.shape, 0)
    return jnp.where(row >= k, pltpu.roll(x, k, 0), 0.0)


def _shift_up(x, k):
    n = x.shape[0]
    row = lax.broadcasted_iota(jnp.int32, x.shape, 0)
    return jnp.where(row < n - k, pltpu.roll(x, n - k, 0), 0.0)


def _conv_silu(x, w):
    y = w[3:4] * x + w[2:3] * _shift_down(x, 1) + w[1:2] * _shift_down(x, 2) + w[0:1] * _shift_down(x, 3)
    sig = _sigmoid(y)
    return y, sig, y * sig


def _gdn_pre(pg, conv_w):
    n_batch, s_len, width = pg.shape
    ncb = width // LANES

    def body(x_ref, w_ref, o_ref):
        cb = pl.program_id(1)
        _, _, s = _conv_silu(x_ref[0], w_ref[...])
        sn = s * lax.rsqrt(jnp.sum(s * s, axis=-1, keepdims=True) + EPS)
        o_ref[0] = jnp.where(cb < 2 * GDN_HEADS, sn, s)

    return pl.pallas_call(
        body, name="gdn_pre", grid=(n_batch, ncb),
        in_specs=[pl.BlockSpec((1, s_len, LANES), lambda b, c: (b, 0, c)), pl.BlockSpec((8, LANES), lambda b, c: (0, c))],
        out_specs=pl.BlockSpec((1, s_len, LANES), lambda b, c: (b, 0, c)),
        out_shape=jax.ShapeDtypeStruct(pg.shape, F32),
        compiler_params=_cparams(("parallel", "parallel")),
    )(pg, conv_w)


def _gdn_pre_bwd(pg, conv_w, dout):
    n_batch, s_len, width = pg.shape
    ncb = width // LANES

    def body(x_ref, w_ref, d_ref, dx_ref, dw_ref):
        cb = pl.program_id(0)
        b = pl.program_id(1)
        x = x_ref[0]
        w = w_ref[...]
        d = d_ref[0]
        y, sig, s = _conv_silu(x, w)
        rr = lax.rsqrt(jnp.sum(s * s, axis=-1, keepdims=True) + EPS)
        sn = s * rr
        ds_n = rr * (d - sn * jnp.sum(d * sn, axis=-1, keepdims=True))
        ds = jnp.where(cb < 2 * GDN_HEADS, ds_n, d)
        dy = ds * (sig * (1.0 + y * (1.0 - sig)))
        dyu = [_shift_up(dy, 3 - jj) if jj < 3 else dy for jj in range(CONV_WIDTH)]
        dx = w[0:1] * dyu[0] + w[1:2] * dyu[1] + w[2:3] * dyu[2] + w[3:4] * dyu[3]
        dx_ref[0] = dx.astype(BF16)
        dw = [jnp.sum(dyu[jj] * x, axis=0, keepdims=True) for jj in range(CONV_WIDTH)]
        rows = lax.broadcasted_iota(jnp.int32, (8, LANES), 0)
        dwb = jnp.zeros((8, LANES), F32)
        for jj in range(CONV_WIDTH):
            dwb = dwb + jnp.where(rows == jj, dw[jj], 0.0)

        @pl.when(b == 0)
        def _():
            dw_ref[...] = jnp.zeros_like(dw_ref)

        dw_ref[...] += dwb

    blk = lambda c, b: (b, 0, c)
    return pl.pallas_call(
        body, name="gdn_pre_bwd", grid=(ncb, n_batch),
        in_specs=[pl.BlockSpec((1, s_len, LANES), blk), pl.BlockSpec((8, LANES), lambda c, b: (0, c)), pl.BlockSpec((1, s_len, LANES), blk)],
        out_specs=[pl.BlockSpec((1, s_len, LANES), blk), pl.BlockSpec((8, LANES), lambda c, b: (0, c))],
        out_shape=[jax.ShapeDtypeStruct(pg.shape, BF16), jax.ShapeDtypeStruct((8, width), F32)],
        compiler_params=_cparams(("parallel", "arbitrary")),
    )(pg, conv_w, dout)


def _gdn_gates(smc, smr, a_c, dt_c, a_r, dt_r, h):
    lane = lax.broadcasted_iota(jnp.int32, (1, LANES), 1)
    sub = lax.broadcasted_iota(jnp.int32, (SM_ROWS, 1), 0)
    beta_c = jnp.sum(jnp.where(lane == SM_B + h, _sigmoid(smc), 0.0), axis=1, keepdims=True)
    g_all_c = -jnp.exp(a_c) * _softplus(smc + dt_c)
    g_c = jnp.sum(jnp.where(lane == SM_A + h, g_all_c, 0.0), axis=1, keepdims=True)
    g_all_r = -jnp.exp(a_r) * _softplus(smr + dt_r)
    g_r = jnp.sum(jnp.where(sub == SM_A + h, g_all_r, 0.0), axis=0, keepdims=True)
    return beta_c, g_c, g_r


def _gdn_group(qkv, z, smc, smr, a_c, dt_c, a_r, dt_r, go, states):
    n_grp = len(qkv)
    c = qkv[0].shape[0]
    hd = GDN_HEAD_DIM
    pairs = [(g, h) for g in range(n_grp) for h in range(GDN_HEADS)]
    ii = lax.broadcasted_iota(jnp.int32, (c, c), 0)
    jj = lax.broadcasted_iota(jnp.int32, (c, c), 1)
    incl = ii >= jj
    col = lambda arr, base, h: arr[:, base + h * hd:base + (h + 1) * hd]

    qs, ks, kbs, vbs, gcs, g_lasts, amats, intras = [], [], [], [], [], [], [], []
    for g, h in pairs:
        beta_c, g_c, g_r = _gdn_gates(smc[g], smr[g], a_c, dt_c, a_r, dt_r, h)
        gc_c = jnp.sum(jnp.where(incl, g_r, 0.0), axis=1, keepdims=True)
        gc_r = jnp.sum(jnp.where(ii <= jj, g_c, 0.0), axis=0, keepdims=True)
        decay = jnp.where(incl, jnp.exp(jnp.where(incl, gc_c - gc_r, 0.0)), 0.0)
        k = col(qkv[g], GDN_WIDTH, h)
        kb = k * beta_c
        qs.append(col(qkv[g], 0, h) * (hd ** -0.5))
        ks.append(k)
        kbs.append(kb)
        vbs.append(col(qkv[g], 2 * GDN_WIDTH, h) * beta_c)
        gcs.append(gc_c)
        g_lasts.append(jnp.sum(g_c, axis=0, keepdims=True))
        both = _mm_nt(jnp.concatenate([kb, qs[-1]], axis=0), k)
        amats.append(jnp.where(ii > jj, both[0:c] * decay, 0.0))
        intras.append(both[c:2 * c] * decay)
    ts = _unit_lower_inverses(amats)
    egcs = [jnp.exp(gc) for gc in gcs]
    uws = [_mm_nn(t, jnp.concatenate([vb, kb * e], axis=1)) for t, vb, kb, e in zip(ts, vbs, kbs, egcs)]
    us = [uw[:, 0:hd] for uw in uws]
    ws = [uw[:, hd:2 * hd] for uw in uws]
    qes = [q * e for q, e in zip(qs, egcs)]
    kds = [k * jnp.exp(gl - gc) for k, gl, gc in zip(ks, g_lasts, gcs)]
    sdecs = [jnp.exp(gl) for gl in g_lasts]

    outs = []
    for g in range(n_grp):
        idx = [g * GDN_HEADS + h for h in range(GDN_HEADS)]
        v_new = [us[i] - _mm_nn(ws[i], states[h]) for h, i in enumerate(idx)]
        o = [_mm_nn(jnp.concatenate([qes[i], intras[i]], axis=1), jnp.concatenate([states[h], v_new[h]], axis=0))
             for h, i in enumerate(idx)]
        states = [states[h] * sdecs[i] + _mm_tn(kds[i], v_new[h]) for h, i in enumerate(idx)]
        outs.append([_rms(o[h], go) * (col(z[g], 0, h) * _sigmoid(col(z[g], 0, h))) for h in range(GDN_HEADS)])
    return outs, states


def _gdn_group_size(n_chunks):
    return GDN_GROUP if n_chunks % GDN_GROUP == 0 else 1


def _gdn_fwd(qkvn, z, smc, smr, a_c, dt_c, a_r, dt_r, go):
    n_batch, s_len, _ = qkvn.shape
    c = GDN_CHUNK
    n = s_len // c
    grp = _gdn_group_size(n)
    ng = n // grp
    gc = grp * c
    hd = GDN_HEAD_DIM

    def body(qkv_ref, z_ref, smc_ref, smr_ref, ac_ref, dc_ref, ar_ref, dr_ref, go_ref, og_ref, st_ref, s_ref):
        @pl.when(pl.program_id(1) == 0)
        def _():
            s_ref[...] = jnp.zeros_like(s_ref)

        states = [s_ref[h] for h in range(GDN_HEADS)]
        for h in range(GDN_HEADS):
            st_ref[0, 0, h] = states[h]
        rows = lambda k: slice(k * c, (k + 1) * c)
        outs, nxt = _gdn_group([qkv_ref[0, rows(k), :] for k in range(grp)], [z_ref[0, rows(k), :] for k in range(grp)],
                               [smc_ref[0, rows(k), :] for k in range(grp)], [smr_ref[k] for k in range(grp)],
                               ac_ref[...], dc_ref[...], ar_ref[...], dr_ref[...], go_ref[...], states)
        for k in range(grp):
            for h in range(GDN_HEADS):
                og_ref[0, rows(k), h * hd:(h + 1) * hd] = outs[k][h].astype(BF16)
        for h in range(GDN_HEADS):
            s_ref[h] = nxt[h]

    tok = lambda b, i: (b, i, 0)
    fixed = lambda b, i: (0, 0)
    return pl.pallas_call(
        body, name="gdn_fwd", grid=(n_batch, ng),
        in_specs=[pl.BlockSpec((1, gc, 3 * GDN_WIDTH), tok), pl.BlockSpec((1, gc, GDN_WIDTH), tok), pl.BlockSpec((1, gc, LANES), tok),
                  pl.BlockSpec((grp, SM_ROWS, c), lambda b, i: (b * ng + i, 0, 0)),
                  pl.BlockSpec((1, LANES), fixed), pl.BlockSpec((1, LANES), fixed), pl.BlockSpec((SM_ROWS, 1), fixed),
                  pl.BlockSpec((SM_ROWS, 1), fixed), pl.BlockSpec((1, LANES), fixed)],
        out_specs=[pl.BlockSpec((1, gc, GDN_WIDTH), tok), pl.BlockSpec((1, 1, GDN_HEADS, hd, hd), lambda b, i: (b, i, 0, 0, 0))],
        out_shape=[jax.ShapeDtypeStruct((n_batch, s_len, GDN_WIDTH), BF16), jax.ShapeDtypeStruct((n_batch, ng, GDN_HEADS, hd, hd), F32)],
        scratch_shapes=[pltpu.VMEM((GDN_HEADS, hd, hd), F32)],
        compiler_params=_cparams(("parallel", "arbitrary")),
    )(qkvn, z, smc, smr, a_c, dt_c, a_r, dt_r, go)


def _gdn_bwd(qkvn, z, smc, smr, a_c, dt_c, a_r, dt_r, go, states, dog):
    n_batch, s_len, _ = qkvn.shape
    c = GDN_CHUNK
    n = s_len // c
    grp = _gdn_group_size(n)
    ng = n // grp
    gc = grp * c
    hd = GDN_HEAD_DIM

    def body(qkv_ref, z_ref, smc_ref, smr_ref, ac_ref, dc_ref, ar_ref, dr_ref, go_ref, st_ref, dog_ref,
             dqkv_ref, dz_ref, dsmc_ref, dsmr_ref, dac_ref, ddc_ref, dar_ref, ddr_ref, dgo_ref, ds_ref):
        first = (pl.program_id(0) == 0) & (pl.program_id(1) == 0)

        @pl.when(pl.program_id(1) == 0)
        def _():
            ds_ref[...] = jnp.zeros_like(ds_ref)

        @pl.when(first)
        def _():
            for r in (dac_ref, ddc_ref, dar_ref, ddr_ref, dgo_ref):
                r[...] = jnp.zeros_like(r)

        rows = lambda k: slice(k * c, (k + 1) * c)
        states = [st_ref[0, 0, h] for h in range(GDN_HEADS)]
        prim = ([qkv_ref[0, rows(k), :] for k in range(grp)], [z_ref[0, rows(k), :] for k in range(grp)],
                [smc_ref[0, rows(k), :] for k in range(grp)], [smr_ref[k] for k in range(grp)],
                ac_ref[...], dc_ref[...], ar_ref[...], dr_ref[...], go_ref[...], states)
        _, vjp = jax.vjp(_gdn_group, *prim)
        cot = ([[dog_ref[0, rows(k), h * hd:(h + 1) * hd] for h in range(GDN_HEADS)] for k in range(grp)],
               [ds_ref[h] for h in range(GDN_HEADS)])
        dqkv, dz, dsmc, dsmr, dac, ddc, dar, ddr, dgo, dstates = vjp(cot)
        for k in range(grp):
            dqkv_ref[0, rows(k), :] = dqkv[k]
            dz_ref[0, rows(k), :] = dz[k].astype(BF16)
            dsmc_ref[0, rows(k), :] = dsmc[k]
            dsmr_ref[k] = dsmr[k]
        dac_ref[...] += dac
        ddc_ref[...] += ddc
        dar_ref[...] += dar
        ddr_ref[...] += ddr
        dgo_ref[...] += dgo
        for h in range(GDN_HEADS):
            ds_ref[h] = dstates[h]

    tok = lambda b, i: (b, ng - 1 - i, 0)
    fixed = lambda b, i: (0, 0)
    lane_vec = jax.ShapeDtypeStruct((1, LANES), F32)
    row_vec = jax.ShapeDtypeStruct((SM_ROWS, 1), F32)
    return pl.pallas_call(
        body, name="gdn_bwd", grid=(n_batch, ng),
        in_specs=[pl.BlockSpec((1, gc, 3 * GDN_WIDTH), tok), pl.BlockSpec((1, gc, GDN_WIDTH), tok), pl.BlockSpec((1, gc, LANES), tok),
                  pl.BlockSpec((grp, SM_ROWS, c), lambda b, i: (b * ng + ng - 1 - i, 0, 0)),
                  pl.BlockSpec((1, LANES), fixed), pl.BlockSpec((1, LANES), fixed), pl.BlockSpec((SM_ROWS, 1), fixed),
                  pl.BlockSpec((SM_ROWS, 1), fixed), pl.BlockSpec((1, LANES), fixed),
                  pl.BlockSpec((1, 1, GDN_HEADS, hd, hd), lambda b, i: (b, ng - 1 - i, 0, 0, 0)),
                  pl.BlockSpec((1, gc, GDN_WIDTH), lambda b, i: (b, ng - 1 - i, 1))],
        out_specs=[pl.BlockSpec((1, gc, 3 * GDN_WIDTH), tok), pl.BlockSpec((1, gc, GDN_WIDTH), tok), pl.BlockSpec((1, gc, LANES), tok),
                   pl.BlockSpec((grp, SM_ROWS, c), lambda b, i: (b * ng + ng - 1 - i, 0, 0)),
                   pl.BlockSpec((1, LANES), fixed), pl.BlockSpec((1, LANES), fixed), pl.BlockSpec((SM_ROWS, 1), fixed),
                   pl.BlockSpec((SM_ROWS, 1), fixed), pl.BlockSpec((1, LANES), fixed)],
        out_shape=[jax.ShapeDtypeStruct((n_batch, s_len, 3 * GDN_WIDTH), F32), jax.ShapeDtypeStruct((n_batch, s_len, GDN_WIDTH), BF16),
                   jax.ShapeDtypeStruct((n_batch, s_len, LANES), F32), jax.ShapeDtypeStruct((n_batch * n, SM_ROWS, c), F32),
                   lane_vec, lane_vec, row_vec, row_vec, lane_vec],
        scratch_shapes=[pltpu.VMEM((GDN_HEADS, hd, hd), F32)],
        compiler_params=_cparams(("arbitrary", "arbitrary")),
    )(qkvn, z, smc, smr, a_c, dt_c, a_r, dt_r, go, states, dog)


def _out_proj(x, oa, ob, w_out, g_x, w_cq, tm=256):
    t_len, d = x.shape
    tm = min(tm, t_len)

    def body(x_ref, oa_ref, ob_ref, wo_ref, g_ref, wq_ref, x1_ref, hq_ref, cq_ref):
        x1 = x_ref[...] + _dot(oa_ref[...], wo_ref[0:FOX_WIDTH, :]) + _dot(ob_ref[...], wo_ref[FOX_WIDTH:2 * FOX_WIDTH, :])
        x1_ref[...] = x1
        hq = _rms(x1, g_ref[...]).astype(BF16)
        hq_ref[...] = hq
        cq_ref[...] = _dot(hq, wq_ref[...])

    row = lambda i: (i, 0)
    fixed = lambda i: (0, 0)
    return pl.pallas_call(
        body, name="out_proj", grid=(t_len // tm,),
        in_specs=[pl.BlockSpec((tm, d), row), pl.BlockSpec((tm, FOX_WIDTH), row), pl.BlockSpec((tm, GDN_WIDTH), row),
                  pl.BlockSpec((d, d), fixed), pl.BlockSpec((1, d), fixed), pl.BlockSpec((d, XATTN_WIDTH), fixed)],
        out_specs=[pl.BlockSpec((tm, d), row), pl.BlockSpec((tm, d), row), pl.BlockSpec((tm, XATTN_WIDTH), row)],
        out_shape=[jax.ShapeDtypeStruct((t_len, d), F32), jax.ShapeDtypeStruct((t_len, d), BF16), jax.ShapeDtypeStruct((t_len, XATTN_WIDTH), F32)],
        compiler_params=_cparams(("parallel",)),
    )(x, oa, ob, w_out, g_x, w_cq)


def _out_proj_bwd(dx1, w_out, tm=512):
    t_len, d = dx1.shape
    tm = min(tm, t_len)

    def body(dx_ref, w_ref, o_ref):
        o_ref[...] = _dot(dx_ref[...], w_ref[...], NT)

    return pl.pallas_call(
        body, name="out_proj_bwd", grid=(t_len // tm,),
        in_specs=[pl.BlockSpec((tm, d), lambda i: (i, 0)), pl.BlockSpec((d, d), lambda i: (0, 0))],
        out_specs=pl.BlockSpec((tm, d), lambda i: (i, 0)),
        out_shape=jax.ShapeDtypeStruct((t_len, d), F32),
        compiler_params=_cparams(("parallel",)),
    )(dx1, w_out)


def _mem_kv(mem, g, w_ckv, tm=256):
    t_len, d = mem.shape
    tm = min(tm, t_len)

    def body(x_ref, g_ref, w_ref, h_ref, o_ref):
        h = _rms(x_ref[...], g_ref[...]).astype(BF16)
        h_ref[...] = h
        o_ref[...] = _dot(h, w_ref[...])

    row = lambda i: (i, 0)
    fixed = lambda i: (0, 0)
    return pl.pallas_call(
        body, name="mem_kv", grid=(t_len // tm,),
        in_specs=[pl.BlockSpec((tm, d), row), pl.BlockSpec((1, d), fixed), pl.BlockSpec((d, 2 * XATTN_WIDTH), fixed)],
        out_specs=[pl.BlockSpec((tm, d), row), pl.BlockSpec((tm, 2 * XATTN_WIDTH), row)],
        out_shape=[jax.ShapeDtypeStruct((t_len, d), BF16), jax.ShapeDtypeStruct((t_len, 2 * XATTN_WIDTH), F32)],
        compiler_params=_cparams(("parallel",)),
    )(mem, g, w_ckv)


def _mem_kv_bwd(dckv, mem, g, w_ckv, tm=256):
    t_len, d = mem.shape
    tm = min(tm, t_len)

    def body(d_ref, x_ref, g_ref, w_ref, dg_ref):
        @pl.when(pl.program_id(0) == 0)
        def _():
            dg_ref[...] = jnp.zeros_like(dg_ref)

        dh = _dot(d_ref[...], w_ref[...], NT)
        _, dg = _rms_bwd(x_ref[...], g_ref[...], dh)
        dg_ref[...] += dg

    row = lambda i: (i, 0)
    fixed = lambda i: (0, 0)
    return pl.pallas_call(
        body, name="mem_kv_bwd", grid=(t_len // tm,),
        in_specs=[pl.BlockSpec((tm, 2 * XATTN_WIDTH), row), pl.BlockSpec((tm, d), row), pl.BlockSpec((1, d), fixed),
                  pl.BlockSpec((d, 2 * XATTN_WIDTH), fixed)],
        out_specs=pl.BlockSpec((1, d), fixed),
        out_shape=jax.ShapeDtypeStruct((1, d), F32),
        compiler_params=_cparams(("arbitrary",)),
    )(dckv, mem, g, w_ckv)


def _xattn_probs(qn, kn):
    s = _dot(qn, kn, NT) * (XATTN_HEAD_DIM ** -0.5)
    p = jnp.exp(s - jnp.max(s, axis=-1, keepdims=True))
    return p / jnp.sum(p, axis=-1, keepdims=True)


def _xattn_fwd(cq, ckv, x1, gq, gk, w_co, g_mlp, n_batch, s_len, m_len, tq=512):
    d = x1.shape[1]
    tq = min(tq, s_len)
    nq = s_len // tq
    hd = XATTN_HEAD_DIM

    def body(cq_ref, kv_ref, x1_ref, gq_ref, gk_ref, wo_ref, gm_ref, co_ref, x2_ref, hf_ref):
        outs = []
        for h in range(XATTN_HEADS):
            qn = _rms(cq_ref[:, h * hd:(h + 1) * hd], gq_ref[...])
            kn = _rms(kv_ref[:, h * hd:(h + 1) * hd], gk_ref[...])
            p = _xattn_probs(qn, kn)
            outs.append(_dot(p, kv_ref[:, XATTN_WIDTH + h * hd:XATTN_WIDTH + (h + 1) * hd]).astype(BF16))
        for h in range(XATTN_HEADS):
            co_ref[:, h * hd:(h + 1) * hd] = outs[h]
        x2 = x1_ref[...] + _dot(co_ref[...], wo_ref[...])
        x2_ref[...] = x2
        hf_ref[...] = _rms(x2, gm_ref[...]).astype(BF16)

    row = lambda b, i: (b * nq + i, 0)
    fixed = lambda b, i: (0, 0)
    t_len = n_batch * s_len
    return pl.pallas_call(
        body, name="xattn_fwd", grid=(n_batch, nq),
        in_specs=[pl.BlockSpec((tq, XATTN_WIDTH), row), pl.BlockSpec((m_len, 2 * XATTN_WIDTH), lambda b, i: (b, 0)),
                  pl.BlockSpec((tq, d), row), pl.BlockSpec((1, hd), fixed), pl.BlockSpec((1, hd), fixed),
                  pl.BlockSpec((XATTN_WIDTH, d), fixed), pl.BlockSpec((1, d), fixed)],
        out_specs=[pl.BlockSpec((tq, XATTN_WIDTH), row), pl.BlockSpec((tq, d), row), pl.BlockSpec((tq, d), row)],
        out_shape=[jax.ShapeDtypeStruct((t_len, XATTN_WIDTH), BF16), jax.ShapeDtypeStruct((t_len, d), F32),
                   jax.ShapeDtypeStruct((t_len, d), BF16)],
        compiler_params=_cparams(("parallel", "parallel")),
    )(cq, ckv, x1, gq, gk, w_co, g_mlp)


def _xattn_bwd(dx2, cq, ckv, x1, gq, gk, w_co, g_x, w_cq, n_batch, s_len, m_len, tq=512):
    d = x1.shape[1]
    tq = min(tq, s_len)
    nq = s_len // tq
    hd = XATTN_HEAD_DIM
    scale = XATTN_HEAD_DIM ** -0.5

    def body(dx2_ref, cq_ref, kv_ref, x1_ref, gq_ref, gk_ref, wo_ref, gx_ref, wq_ref,
             dx1_ref, dcq_ref, dkv_ref, dgq_ref, dgk_ref, dgx_ref, dk_acc, dv_acc):
        b = pl.program_id(0)
        i = pl.program_id(1)

        @pl.when((b == 0) & (i == 0))
        def _():
            dgq_ref[...] = jnp.zeros_like(dgq_ref)
            dgk_ref[...] = jnp.zeros_like(dgk_ref)
            dgx_ref[...] = jnp.zeros_like(dgx_ref)

        @pl.when(i == 0)
        def _():
            dk_acc[...] = jnp.zeros_like(dk_acc)
            dv_acc[...] = jnp.zeros_like(dv_acc)

        dx2 = dx2_ref[...]
        dco_all = _dot(dx2, wo_ref[...], NT)
        for h in range(XATTN_HEADS):
            sl = slice(h * hd, (h + 1) * hd)
            q = cq_ref[:, sl]
            qn = _rms(q, gq_ref[...])
            kn = _rms(kv_ref[:, sl], gk_ref[...])
            v = kv_ref[:, XATTN_WIDTH + h * hd:XATTN_WIDTH + (h + 1) * hd]
            p = _xattn_probs(qn, kn)
            dco = dco_all[:, sl]
            dv_acc[:, sl] += _dot(p, dco, TN)
            dp = _dot(dco, v, NT)
            ds = p * (dp - jnp.sum(dp * p, axis=-1, keepdims=True))
            dqn = _dot(ds, kn) * scale
            dk_acc[:, sl] += _dot(ds, qn, TN) * scale
            dq, dgq = _rms_bwd(q, gq_ref[...], dqn)
            dgq_ref[...] += dgq
            dcq_ref[:, sl] = dq.astype(BF16)
        dhq = _dot(dcq_ref[...], wq_ref[...], NT)
        dxn, dgx = _rms_bwd(x1_ref[...], gx_ref[...], dhq)
        dgx_ref[...] += dgx
        dx1_ref[...] = dx2 + dxn

        @pl.when(i == nq - 1)
        def _():
            for h in range(XATTN_HEADS):
                sl = slice(h * hd, (h + 1) * hd)
                dk, dgk = _rms_bwd(kv_ref[:, sl], gk_ref[...], dk_acc[:, sl])
                dgk_ref[...] += dgk
                dkv_ref[:, sl] = dk.astype(BF16)
                dkv_ref[:, XATTN_WIDTH + h * hd:XATTN_WIDTH + (h + 1) * hd] = dv_acc[:, sl].astype(BF16)

    row = lambda b, i: (b * nq + i, 0)
    fixed = lambda b, i: (0, 0)
    t_len = n_batch * s_len
    return pl.pallas_call(
        body, name="xattn_bwd", grid=(n_batch, nq),
        in_specs=[pl.BlockSpec((tq, d), row), pl.BlockSpec((tq, XATTN_WIDTH), row), pl.BlockSpec((m_len, 2 * XATTN_WIDTH), lambda b, i: (b, 0)),
                  pl.BlockSpec((tq, d), row), pl.BlockSpec((1, hd), fixed), pl.BlockSpec((1, hd), fixed),
                  pl.BlockSpec((XATTN_WIDTH, d), fixed), pl.BlockSpec((1, d), fixed), pl.BlockSpec((d, XATTN_WIDTH), fixed)],
        out_specs=[pl.BlockSpec((tq, d), row), pl.BlockSpec((tq, XATTN_WIDTH), row), pl.BlockSpec((m_len, 2 * XATTN_WIDTH), lambda b, i: (b, 0)),
                   pl.BlockSpec((1, hd), fixed), pl.BlockSpec((1, hd), fixed), pl.BlockSpec((1, d), fixed)],
        out_shape=[jax.ShapeDtypeStruct((t_len, d), F32), jax.ShapeDtypeStruct((t_len, XATTN_WIDTH), BF16),
                   jax.ShapeDtypeStruct((n_batch * m_len, 2 * XATTN_WIDTH), BF16),
                   jax.ShapeDtypeStruct((1, hd), F32), jax.ShapeDtypeStruct((1, hd), F32), jax.ShapeDtypeStruct((1, d), F32)],
        scratch_shapes=[pltpu.VMEM((m_len, XATTN_WIDTH), F32), pltpu.VMEM((m_len, XATTN_WIDTH), F32)],
        compiler_params=_cparams(("arbitrary", "arbitrary")),
    )(dx2, cq, ckv, x1, gq, gk, w_co, g_x, w_cq)


def _resident(shape):
    return pl.BlockSpec(shape, lambda *_: (0,) * len(shape), pipeline_mode=pl.Buffered(1))


def _mlp_fwd(hf, x2, target, w1, w2, tm=256, tf=1024):
    t_len, d = x2.shape
    f = w1.shape[1]
    tm, tf = min(tm, t_len), min(tf, f)

    def body(hf_ref, x2_ref, tg_ref, w1_ref, w2_ref, u_ref, a_ref, dy_ref, ls_ref):
        hf_t = hf_ref[...]
        for k in range(f // tf):
            cols = slice(k * tf, (k + 1) * tf)
            u = _dot(hf_t, w1_ref[:, cols])
            u_ref[:, cols] = u
            r = jnp.maximum(u, 0.0)
            a_ref[:, cols] = (r * r).astype(BF16)
        y = x2_ref[...] + _dot(a_ref[...], w2_ref[...])
        err = y - tg_ref[...]
        dy_ref[...] = err * (1.0 / d)
        ls_ref[...] = jnp.broadcast_to(jnp.sum(jnp.sum(err * err, axis=-1, keepdims=True) * (1.0 / d), axis=0, keepdims=True), ls_ref.shape)

    row = lambda i: (i, 0)
    return pl.pallas_call(
        body, name="mlp_fwd", grid=(t_len // tm,),
        in_specs=[pl.BlockSpec((tm, d), row), pl.BlockSpec((tm, d), row), pl.BlockSpec((tm, d), row), _resident((d, f)), _resident((f, d))],
        out_specs=[pl.BlockSpec((tm, f), row), pl.BlockSpec((tm, f), row), pl.BlockSpec((tm, d), row),
                   pl.BlockSpec((1, 8, LANES), lambda i: (i, 0, 0))],
        out_shape=[jax.ShapeDtypeStruct((t_len, f), F32), jax.ShapeDtypeStruct((t_len, f), BF16), jax.ShapeDtypeStruct((t_len, d), F32),
                   jax.ShapeDtypeStruct((t_len // tm, 8, LANES), F32)],
        compiler_params=_cparams(("parallel",)),
    )(hf, x2, target, w1, w2)


def _mlp_bwd(dy, u, x2, g, w1, w2, tm=256, tf=1024):
    t_len, d = x2.shape
    f = w1.shape[1]
    tm, tf = min(tm, t_len), min(tf, f)

    def body(dy_ref, u_ref, x2_ref, g_ref, w1_ref, w2_ref, du_ref, dx2_ref, dg_ref):
        @pl.when(pl.program_id(0) == 0)
        def _():
            dg_ref[...] = jnp.zeros_like(dg_ref)

        dy_t = dy_ref[...]
        dyb = dy_t.astype(BF16)
        for k in range(f // tf):
            cols = slice(k * tf, (k + 1) * tf)
            da = _dot(dyb, w2_ref[cols, :], NT)
            du_ref[:, cols] = (da * (2.0 * jnp.maximum(u_ref[:, cols], 0.0))).astype(BF16)
        dhf = _dot(du_ref[...], w1_ref[...], NT)
        dxn, dg = _rms_bwd(x2_ref[...], g_ref[...], dhf)
        dx2_ref[...] = dy_t + dxn
        dg_ref[...] += dg

    row = lambda i: (i, 0)
    fixed = lambda i: (0, 0)
    return pl.pallas_call(
        body, name="mlp_bwd", grid=(t_len // tm,),
        in_specs=[pl.BlockSpec((tm, d), row), pl.BlockSpec((tm, f), row), pl.BlockSpec((tm, d), row), pl.BlockSpec((1, d), fixed),
                  _resident((d, f)), _resident((f, d))],
        out_specs=[pl.BlockSpec((tm, f), row), pl.BlockSpec((tm, d), row), pl.BlockSpec((1, d), fixed)],
        out_shape=[jax.ShapeDtypeStruct((t_len, f), BF16), jax.ShapeDtypeStruct((t_len, d), F32), jax.ShapeDtypeStruct((1, d), F32)],
        compiler_params=_cparams(("arbitrary",)),
    )(dy, u, x2, g, w1, w2)


def _pad_lanes(v, offset=0, width=LANES):
    return jnp.zeros((1, width), F32).at[:, offset:offset + v.shape[1]].set(v)


def _col(v, offset=0, rows=SM_ROWS):
    return jnp.zeros((rows, 1), F32).at[offset:offset + v.shape[1], 0].set(v[0])


def _pack_small(g_mix, dgq, dgk, dbias, dgo, dac, dar, ddc, ddr, g_gdn_o, g_nx, g_mem, g_xq, g_xk, g_mlp, loss_tiles):
    def body(mix_ref, q_ref, k_ref, b_ref, o_ref, ac_ref, ar_ref, dc_ref, dr_ref, go_ref, nx_ref, mem_ref, xq_ref, xk_ref,
             mlp_ref, lt_ref, out_ref):
        lane = lax.broadcasted_iota(jnp.int32, (1, LANES), 1)
        diag = lax.broadcasted_iota(jnp.int32, (SM_ROWS, LANES), 0) == lax.broadcasted_iota(jnp.int32, (SM_ROWS, LANES), 1)

        def rolled(v, shift):
            return pltpu.roll(jnp.broadcast_to(v, (8, LANES)), shift, 1)[0:1, :]

        def rows_to_lanes(col):
            return jnp.sum(jnp.where(diag, col, 0.0), axis=0, keepdims=True)

        def put(row, v, n):
            out_ref[row:row + 1, 0:LANES] = jnp.where(lane < n, v, 0.0)

        out_ref[...] = jnp.zeros_like(out_ref)
        out_ref[0:1, :] = mix_ref[...]
        for row, ref in ((1, q_ref), (2, k_ref), (4, o_ref)):
            put(row, ref[...] + rolled(ref[...], FOX_HEAD_DIM), FOX_HEAD_DIM)
        put(3, rows_to_lanes(b_ref[...]), FOX_HEADS)
        for row, lane_ref, row_ref in ((5, ac_ref, ar_ref), (6, dc_ref, dr_ref)):
            put(row, rolled(lane_ref[...] + rows_to_lanes(row_ref[...]), LANES - SM_A), GDN_HEADS)
        put(7, go_ref[...], LANES)
        out_ref[8:9, :] = nx_ref[...]
        out_ref[9:10, :] = mem_ref[...]
        put(10, xq_ref[...], LANES)
        put(11, xk_ref[...], LANES)
        out_ref[12:13, :] = mlp_ref[...]
        put(LOSS_ROW, 0.5 * jnp.sum(lt_ref[...], axis=0)[0:1, :], 1)

    args = (g_mix, dgq, dgk, dbias, dgo, dac, dar, ddc, ddr, g_gdn_o, g_nx, g_mem, g_xq, g_xk, g_mlp, loss_tiles)
    return pl.pallas_call(body, name="pack_small", out_shape=jax.ShapeDtypeStruct((PACK_ROWS, D_MODEL), F32))(*args)


LATE_WEIGHTS = (("w_out", "w_cq", "w_ckv", "w_co"), ("w_mlp1", "w_mlp2"))
GRAD_GROUPS = (("w_mlp2", "w_mlp1"), ("w_co", "w_cq", "w_ckv", "w_out"), ("w_in", "gdn_conv_w"))


def _local_step(x, mem, target, norm_mix_g, w_in, fox_qnorm_g, fox_knorm_g, fox_f_bias, fox_onorm_g, gdn_conv_w, gdn_A_log,
                gdn_dt_bias, gdn_onorm_g, norm_xattn_g, mem_norm_g, xattn_qnorm_g, xattn_knorm_g, norm_mlp_g,
                late_weights, grads_ready=None, first_token=0.0):
    if grads_ready is None:
        grads_ready = lambda group: 0.0
    n_batch, s_len, d = x.shape
    m_len = mem.shape[1]
    t_len = n_batch * s_len
    tq = min(FOX_BLOCK, s_len)
    nq = s_len // tq
    n_chunks = s_len // GDN_CHUNK
    x2d = x.reshape(t_len, d)

    wp = jnp.concatenate([w_in[0:1536], w_in[1544:3080], w_in[3088:3600], w_in[1536:1544], w_in[3080:3088],
                          jnp.zeros((P_DIM - 3600, d), BF16)], axis=0)
    wst = jnp.concatenate([w_in[1536:1544], w_in[3080:3088]], axis=0)
    conv_w = jnp.concatenate([gdn_conv_w, jnp.zeros((8 - CONV_WIDTH, gdn_conv_w.shape[1]), F32)], axis=0)
    bias_col = _col(fox_f_bias, SM_F)
    gq2, gk2, go2 = (jnp.tile(g, (1, 2)) for g in (fox_qnorm_g, fox_knorm_g, fox_onorm_g))
    a_c, dt_c = _pad_lanes(gdn_A_log, SM_A), _pad_lanes(gdn_dt_bias, SM_A)
    a_r, dt_r = _col(gdn_A_log, SM_A), _col(gdn_dt_bias, SM_A)

    h1, pfox, pgdn, pz, sm, smt = _in_proj(x2d, norm_mix_g + first_token, wp, wst)
    c_rows = _fox_cum(smt, bias_col, n_batch, s_len)
    cb = c_rows.reshape(SM_ROWS, n_batch, nq, tq).transpose(1, 2, 0, 3)
    pf3 = pfox.reshape(n_batch, s_len, 1536)
    o_fox, oa, lse = _fox_fwd(pf3, cb, gq2, gk2, go2, tq)
    pg3 = pgdn.reshape(n_batch, s_len, 1536)
    qkvn = _gdn_pre(pg3, conv_w)
    z3 = pz.reshape(n_batch, s_len, GDN_WIDTH)
    smc = sm.reshape(n_batch, s_len, LANES)
    smr = smt.reshape(SM_ROWS, n_batch * n_chunks, GDN_CHUNK).transpose(1, 0, 2)
    ob, states = _gdn_fwd(qkvn, z3, smc, smr, a_c, dt_c, a_r, dt_r, gdn_onorm_g)
    oa2, ob2 = oa.reshape(t_len, FOX_WIDTH), ob.reshape(t_len, GDN_WIDTH)
    w_out, w_cq, w_ckv, w_co = late_weights(LATE_WEIGHTS[0], ob2)
    x1, hq, cq = _out_proj(x2d, oa2, ob2, w_out, norm_xattn_g, w_cq)
    mem2d = mem.reshape(n_batch * m_len, d)
    hm, ckv = _mem_kv(mem2d, mem_norm_g, w_ckv)
    co, x2, hf = _xattn_fwd(cq, ckv, x1, xattn_qnorm_g, xattn_knorm_g, w_co, norm_mlp_g, n_batch, s_len, m_len)
    w_mlp1, w_mlp2 = late_weights(LATE_WEIGHTS[1], hf)
    u, a_act, dy, loss_tiles = _mlp_fwd(hf, x2, target.reshape(t_len, d), w_mlp1, w_mlp2)

    grads = {}
    du, dx2, grads["norm_mlp_g"] = _mlp_bwd(dy, u, x2, norm_mlp_g, w_mlp1, w_mlp2)
    grads["w_mlp2"] = _wgrad(a_act, dy, "wgrad_mlp2", bt=2048)
    grads["w_mlp1"] = _wgrad(hf, du, "wgrad_mlp1", bt=2048, column_blocks=D_FF // N_DEV)
    token = grads_ready({k: grads[k] for k in GRAD_GROUPS[0]})
    grads["w_co"] = _wgrad(co, dx2, "wgrad_co", column_blocks=D_MODEL // N_DEV)
    dx1, dcq, dckv, grads["xattn_qnorm_g"], grads["xattn_knorm_g"], grads["norm_xattn_g"] = _xattn_bwd(
        dx2, cq, ckv, x1, xattn_qnorm_g + token, xattn_knorm_g, w_co, norm_xattn_g, w_cq, n_batch, s_len, m_len)
    grads["w_cq"] = _wgrad(hq, dcq, "wgrad_cq")
    grads["w_ckv"] = _wgrad(hm, dckv, "wgrad_ckv")
    grads["mem_norm_g"] = _mem_kv_bwd(dckv, mem2d, mem_norm_g, w_ckv)
    grads["w_out"] = _wgrad_stacked([oa2, ob2], dx1, "wgrad_out", bn=1024)
    token = grads_ready({k: grads[k] for k in GRAD_GROUPS[1]})
    dcat = _out_proj_bwd(dx1, w_out)
    dcat3 = dcat.reshape(n_batch, s_len, d)

    dqkvn, dz, dsmc, dsmr, dac, ddc, dar, ddr, grads["gdn_onorm_g"] = _gdn_bwd(
        qkvn, z3, smc, smr, a_c, dt_c, a_r, dt_r, gdn_onorm_g + token, states, dcat3)
    dpg, dconv = _gdn_pre_bwd(pg3, conv_w, dqkvn)
    grads["gdn_conv_w"] = dconv[0:CONV_WIDTH]

    dq, dk, dv, dcb, dgq, dgk, dgo = _fox_bwd(pf3, cb, gq2, gk2, go2, o_fox, lse, dcat3, tq)
    dc8 = dcb[:, :, :, 0:2, :].transpose(1, 3, 0, 2, 4).reshape(FOX_HEADS, t_len)
    dc_rows = jnp.concatenate([dc8, jnp.zeros((SM_ROWS - FOX_HEADS, t_len), F32)], axis=0)
    dl_rows, dbias = _fox_cum_bwd(dc_rows, smt, bias_col, n_batch, s_len)
    dsm_rows = jnp.concatenate([dl_rows[0:SM_B], dsmr.transpose(1, 0, 2).reshape(SM_ROWS, t_len)[SM_B:SM_ROWS]], axis=0)

    dprojs = [dq.reshape(t_len, FOX_WIDTH), dk.reshape(t_len, FOX_WIDTH), dv.reshape(t_len, FOX_WIDTH),
              dpg.reshape(t_len, 1536), dz.reshape(t_len, GDN_WIDTH), dsmc.reshape(t_len, LANES)]
    dwp = _wgrad_stacked(dprojs, h1, "wgrad_in")
    dwst = _rows_matmul(dsm_rows, h1, "wgrad_in_rows")
    dw_small = dwp[P_SMALL:P_SMALL + SM_ROWS] + dwst
    grads["w_in"] = jnp.concatenate([dwp[0:1536], dw_small[0:8], dwp[1536:3072], dw_small[8:16], dwp[3072:3584]], axis=0)
    token = grads_ready({k: grads[k] for k in GRAD_GROUPS[2]})
    grad_x, grads["norm_mix_g"] = _in_proj_bwd(dprojs, dsm_rows, x2d, norm_mix_g + token, wp, wst, dx1)
    packed = _pack_small(grads["norm_mix_g"], dgq, dgk, dbias, dgo, dac, dar, ddc, ddr, grads["gdn_onorm_g"], grads["norm_xattn_g"],
                         grads["mem_norm_g"], grads["xattn_qnorm_g"], grads["xattn_knorm_g"], grads["norm_mlp_g"], loss_tiles)
    return packed, grad_x.reshape(n_batch, s_len, d), {k: grads[k] for k in SHARDED}


MESH_ID = pl.DeviceIdType.MESH
ANY_SPEC = pl.BlockSpec(memory_space=pl.ANY)


def _place():
    x, y, c = lax.axis_index("x"), lax.axis_index("y"), lax.axis_index("c")
    return x, y, c, [(1 - x, y), (x, 1 - y), (1 - x, 1 - y)]


def _place_own(src_ref, dst_ref):
    def staged(buf, sem):
        for a, b in ((src_ref, buf), (buf, dst_ref)):
            cp = pltpu.make_async_copy(a, b, sem)
            cp.start()
            cp.wait()

    pl.run_scoped(staged, pltpu.VMEM(src_ref.shape, src_ref.dtype), pltpu.SemaphoreType.DMA)


def _all_gather_body(n, ins, outs, send_sems, recv_sems):
    x, y, c, chips = _place()
    me, sibling = (x, y, c), (x, y, 1 - c)

    def copy(a, k, block, to, src=None):
        dst = outs[a].at[4 * block[0] + 2 * block[1] + block[2]]
        return pltpu.make_async_remote_copy(src_ref=dst if src is None else src, dst_ref=dst, send_sem=send_sems.at[a, k],
                                            recv_sem=recv_sems.at[a, k], device_id=to, device_id_type=MESH_ID)

    first = []
    for a in range(n):
        first.append(copy(a, 0, me, sibling, src=ins[a]))
        first += [copy(a, 1 + j, me, (*chip, c), src=ins[a]) for j, chip in enumerate(chips)]
    for cp in first:
        cp.start()
    for a in range(n):
        _place_own(ins[a], outs[a].at[4 * x + 2 * y + c])
    passed = []
    for j, chip in enumerate(chips):
        for a in range(n):
            copy(a, 1 + j, (*chip, c), me).wait_recv()
            fwd = copy(a, 4 + j, (*chip, c), sibling)
            fwd.start()
            passed.append(fwd)
    for a in range(n):
        copy(a, 0, sibling, me).wait_recv()
        for j, chip in enumerate(chips):
            copy(a, 4 + j, (*chip, 1 - c), me).wait_recv()
    for cp in first + passed:
        cp.wait_send()


def _all_gather_hbm(arrs, name):
    n = len(arrs)

    def body(*refs):
        _all_gather_body(n, refs[:n], refs[n:2 * n], refs[2 * n], refs[2 * n + 1])

    return pl.pallas_call(
        body, name=name, in_specs=[ANY_SPEC] * n, out_specs=[ANY_SPEC] * n,
        out_shape=[jax.ShapeDtypeStruct((N_DEV,) + a.shape, a.dtype) for a in arrs],
        scratch_shapes=[pltpu.SemaphoreType.DMA((n, 7)), pltpu.SemaphoreType.DMA((n, 7))],
        compiler_params=pltpu.CompilerParams(vmem_limit_bytes=VMEM_LIMIT),
    )(*arrs)


def _pair_exchange(arrs, name):
    n = len(arrs)

    def body(*refs):
        ins, outs = refs[:n], refs[n:2 * n]
        send_sems, recv_sems = refs[2 * n:]
        x, y, c, _ = _place()
        copies = []
        for a in range(n):
            for chip in range(4):
                copies.append(pltpu.make_async_remote_copy(
                    src_ref=ins[a].at[2 * chip + (1 - c)], dst_ref=outs[a].at[chip], send_sem=send_sems.at[a, chip],
                    recv_sem=recv_sems.at[a, chip], device_id=(x, y, 1 - c), device_id_type=MESH_ID))
        for cp in copies:
            cp.start()
        for cp in copies:
            cp.wait()

    return pl.pallas_call(
        body, name=name, in_specs=[ANY_SPEC] * n, out_specs=[ANY_SPEC] * n,
        out_shape=[jax.ShapeDtypeStruct((4,) + a.shape[1:], a.dtype) for a in arrs],
        scratch_shapes=[pltpu.SemaphoreType.DMA((n, 4)), pltpu.SemaphoreType.DMA((n, 4))],
    )(*arrs)


HBM_SPEC = pl.BlockSpec(memory_space=pltpu.HBM)
SEM_SPEC = pl.BlockSpec(memory_space=pltpu.SEMAPHORE)
DATAFLOW = pltpu.SideEffectType.DATAFLOW_SIDE_EFFECTING


def _in_hbm(arrs):
    return [pltpu.with_memory_space_constraint(a, pltpu.HBM) for a in arrs]


def _copies_start(name, srcs, lands, make_copies, after):
    n = len(srcs)
    n_copies = len(make_copies(srcs, lands, None, None)[0])

    def body(*refs):
        send_sems, recv_sems = refs[2 * n + 1], refs[2 * n + 2]
        for row in make_copies(refs[:n], refs[n:2 * n], send_sems, recv_sems):
            for cp in row:
                cp.start()
        refs[-1][...] = jnp.zeros_like(refs[-1])

    sems = pltpu.SemaphoreType.DMA((n * n_copies,))
    thru = [pltpu.HBM(a.shape, a.dtype) for a in list(srcs) + list(lands)]
    res = pl.pallas_call(
        body, name=name, in_specs=[HBM_SPEC] * (2 * n) + [ANY_SPEC],
        out_specs=(SEM_SPEC, SEM_SPEC, *[HBM_SPEC] * (2 * n), pl.BlockSpec(memory_space=pltpu.VMEM)),
        out_shape=(sems, sems, *thru, jax.ShapeDtypeStruct((8, LANES), F32)),
        input_output_aliases={i: 2 + i for i in range(2 * n)},
        compiler_params=pltpu.CompilerParams(has_side_effects=DATAFLOW),
    )(*_in_hbm(list(srcs) + list(lands)), after)
    return res[0], res[1], list(res[2:2 + n]), list(res[2 + n:2 + 2 * n]), res[-1]


def _copies_wait(name, send_sems, recv_sems, srcs, lands, after, make_copies, own_block=False):
    n = len(srcs)

    def body(*refs):
        if own_block:
            for a in range(n):
                _place_own(refs[a], _own_part(refs[a], refs[3 * n + 3 + a]))
        for row in make_copies(refs[:n], refs[n:2 * n], refs[2 * n], refs[2 * n + 1]):
            for cp in row:
                cp.wait_send()
                cp.wait_recv()

    res = pl.pallas_call(
        body, name=name, in_specs=[HBM_SPEC] * (2 * n) + [SEM_SPEC, SEM_SPEC, ANY_SPEC],
        out_specs=tuple([HBM_SPEC] * (2 * n)),
        out_shape=tuple(pltpu.HBM(a.shape, a.dtype) for a in list(srcs) + list(lands)),
        input_output_aliases={i: i for i in range(2 * n)},
        compiler_params=pltpu.CompilerParams(has_side_effects=DATAFLOW, vmem_limit_bytes=VMEM_LIMIT),
    )(*srcs, *lands, send_sems, recv_sems, after)
    return list(res[:n]), list(res[n:])


def _own_part(src_ref, land_ref):
    me = 4 * lax.axis_index("x") + 2 * lax.axis_index("y") + lax.axis_index("c")
    rows, cols = src_ref.shape
    if land_ref.shape[0] == N_DEV * rows:
        return land_ref.at[pl.ds(pl.multiple_of(me * rows, rows), rows), :]
    return land_ref.at[:, pl.ds(pl.multiple_of(me * cols, cols), cols)]


def _gather_copies(srcs, lands, send_sems, recv_sems):
    if send_sems is None:
        return [[None] * 7]
    x, y, c, _ = _place()
    rows = []
    for a in range(len(srcs)):
        row = []
        for k in range(7):
            r = k + 1
            to = (1 - x if r & 4 else x, 1 - y if r & 2 else y, 1 - c if r & 1 else c)
            row.append(pltpu.make_async_remote_copy(
                src_ref=srcs[a], dst_ref=_own_part(srcs[a], lands[a]), send_sem=send_sems.at[7 * a + k], recv_sem=recv_sems.at[7 * a + k],
                device_id=to, device_id_type=MESH_ID))
        rows.append(row)
    return rows


def _scatter_copies(srcs, lands, send_sems, recv_sems):
    if send_sems is None:
        return [[None] * 7]
    x, y, c, _ = _place()
    rows = []
    for a in range(len(srcs)):
        row = []
        for k in range(7):
            r = k + 1
            to = (1 - x if r & 4 else x, 1 - y if r & 2 else y, 1 - c if r & 1 else c)
            row.append(pltpu.make_async_remote_copy(
                src_ref=srcs[a].at[4 * to[0] + 2 * to[1] + to[2]], dst_ref=lands[a].at[k], send_sem=send_sems.at[7 * a + k],
                recv_sem=recv_sems.at[7 * a + k], device_id=to, device_id_type=MESH_ID))
        rows.append(row)
    return rows


def _chip_copies(srcs, lands, send_sems, recv_sems):
    if send_sems is None:
        return [[None] * 3]
    x, y, c, chips = _place()
    return [[pltpu.make_async_remote_copy(
        src_ref=srcs[a].at[2 * chip[0] + chip[1]], dst_ref=lands[a].at[j], send_sem=send_sems.at[3 * a + j], recv_sem=recv_sems.at[3 * a + j],
        device_id=(*chip, c), device_id_type=MESH_ID) for j, chip in enumerate(chips)] for a in range(len(srcs))]


def _tile(rows, cols):
    if rows <= 256:
        return rows, cols
    tr = 256 if cols <= 512 else 128
    if rows % tr == 0:
        return tr, cols
    return rows, 256


def _pair_sum(core, own, got, name):
    _, rows, cols = own.shape
    tr, tc = _tile(rows, cols)

    def body(c_ref, own_ref, got_ref, o_ref):
        o_ref[0] = own_ref[0] + got_ref[0]

    return pl.pallas_call(
        body, name=name,
        grid_spec=pltpu.PrefetchScalarGridSpec(
            num_scalar_prefetch=1, grid=(4, rows // tr, cols // tc),
            in_specs=[pl.BlockSpec((1, tr, tc), lambda k, i, j, c: (2 * k + c[0], i, j)),
                      pl.BlockSpec((1, tr, tc), lambda k, i, j, c: (k, i, j))],
            out_specs=pl.BlockSpec((1, tr, tc), lambda k, i, j, c: (k, i, j))),
        out_shape=jax.ShapeDtypeStruct((4, rows, cols), F32),
        compiler_params=_cparams(("parallel", "parallel", "parallel")),
    )(core, own, got)


def _adamw(w, g, m, v):
    m_new = ADAM_B1 * m + (1.0 - ADAM_B1) * g
    v_new = ADAM_B2 * v + (1.0 - ADAM_B2) * (g * g)
    m_hat = m_new / (1.0 - ADAM_B1 ** ADAM_STEP)
    v_hat = v_new / (1.0 - ADAM_B2 ** ADAM_STEP)
    delta = -ADAM_LR * (m_hat / (jnp.sqrt(v_hat) + ADAM_EPS) + ADAM_WD * w)
    return delta, m_new, v_new


def _sum_adam(chip, sums, parts, w, m, v, name):
    n_parts, rows, cols = parts.shape
    tr, tc = _tile(rows, cols)

    def body(chip_ref, own_ref, p_ref, w_ref, m_ref, v_ref, g_ref, d_ref, mo_ref, vo_ref):
        g = own_ref[0]
        for k in range(n_parts):
            g = g + p_ref[k]
        g_ref[...] = g
        d_ref[...], mo_ref[...], vo_ref[...] = _adamw(w_ref[...], g, m_ref[...], v_ref[...])

    tile = pl.BlockSpec((tr, tc), lambda i, j, ch: (i, j))
    out = jax.ShapeDtypeStruct((rows, cols), F32)
    return pl.pallas_call(
        body, name=name,
        grid_spec=pltpu.PrefetchScalarGridSpec(
            num_scalar_prefetch=1, grid=(rows // tr, cols // tc),
            in_specs=[pl.BlockSpec((1, tr, tc), lambda i, j, ch: (ch[0], i, j)),
                      pl.BlockSpec((n_parts, tr, tc), lambda i, j, ch: (0, i, j)), tile, tile, tile],
            out_specs=[tile, tile, tile, tile]),
        out_shape=[out, out, out, out],
        compiler_params=_cparams(("parallel", "parallel")),
    )(chip, sums, parts, w, m, v)


SHARDED = ("w_in", "gdn_conv_w", "w_out", "w_cq", "w_ckv", "w_co", "w_mlp1", "w_mlp2")
TRANSPOSED = ("w_in",)
COLUMN_SHARDED = ("gdn_conv_w", "w_co", "w_mlp1")
REPLICATED = ("norm_mix_g", "fox_qnorm_g", "fox_knorm_g", "fox_f_bias", "fox_onorm_g", "gdn_A_log", "gdn_dt_bias", "gdn_onorm_g",
              "norm_xattn_g", "mem_norm_g", "xattn_qnorm_g", "xattn_knorm_g", "norm_mlp_g")
WEIGHTS = ("norm_mix_g", "w_in", "fox_qnorm_g", "fox_knorm_g", "fox_f_bias", "fox_onorm_g", "gdn_conv_w", "gdn_A_log", "gdn_dt_bias",
           "gdn_onorm_g", "w_out", "norm_xattn_g", "mem_norm_g", "w_cq", "w_ckv", "xattn_qnorm_g", "xattn_knorm_g", "w_co",
           "norm_mlp_g", "w_mlp1", "w_mlp2")
PACK_ROWS = 16
LOSS_ROW = len(REPLICATED)


def _whole(name, gathered):
    if name in COLUMN_SHARDED:
        return gathered.transpose(1, 0, 2).reshape(gathered.shape[1], N_DEV * gathered.shape[2])
    return gathered.reshape(N_DEV * gathered.shape[1], gathered.shape[2])


def _whole_shape(name, shard_shape):
    rows, cols = shard_shape
    return (rows, N_DEV * cols) if name in COLUMN_SHARDED else (N_DEV * rows, cols)


def _blocks(name, whole):
    if whole.ndim == 3:
        return whole
    if name in COLUMN_SHARDED:
        rows, cols = whole.shape
        return whole.reshape(rows, N_DEV, cols // N_DEV).transpose(1, 0, 2)
    return whole.reshape(N_DEV, whole.shape[0] // N_DEV, whole.shape[1])


def _adam_small(everyone, ws, ms, vs):
    n_par = len(ws)

    def body(*refs):
        ev_ref = refs[0]
        w_refs, m_refs, v_refs = (refs[1 + j * n_par:1 + (j + 1) * n_par] for j in range(3))
        outs = refs[1 + 3 * n_par:-1]
        sum_ref = refs[-1]
        total = ev_ref[0]
        for dev in range(1, N_DEV):
            total = total + ev_ref[dev]
        sum_ref[...] = total
        for i in range(n_par):
            n = w_refs[i].shape[1]
            g = sum_ref[i:i + 1, 0:n]
            outs[4 * i][...] = g
            outs[4 * i + 1][...], outs[4 * i + 2][...], outs[4 * i + 3][...] = _adamw(w_refs[i][...], g, m_refs[i][...], v_refs[i][...])
        outs[4 * n_par][...] = sum_ref[LOSS_ROW:LOSS_ROW + 1, 0:1]

    shapes = [jax.ShapeDtypeStruct(a.shape, F32) for a in ws for _ in range(4)] + [jax.ShapeDtypeStruct((1, 1), F32)]
    return pl.pallas_call(body, name="adam_small", out_shape=shapes,
                          scratch_shapes=[pltpu.VMEM((PACK_ROWS, D_MODEL), F32)])(everyone, *ws, *ms, *vs)


def kernel(x, mem, norm_mix_g, w_in, fox_qnorm_g, fox_knorm_g, fox_f_bias, fox_onorm_g, gdn_conv_w, gdn_A_log, gdn_dt_bias, gdn_onorm_g, w_out, norm_xattn_g, mem_norm_g, w_cq, w_ckv, xattn_qnorm_g, xattn_knorm_g, w_co, norm_mlp_g, w_mlp1, w_mlp2, loss_target, m_norm_mix_g, m_w_in, m_fox_qnorm_g, m_fox_knorm_g, m_fox_f_bias, m_fox_onorm_g, m_gdn_conv_w, m_gdn_A_log, m_gdn_dt_bias, m_gdn_onorm_g, m_w_out, m_norm_xattn_g, m_mem_norm_g, m_w_cq, m_w_ckv, m_xattn_qnorm_g, m_xattn_knorm_g, m_w_co, m_norm_mlp_g, m_w_mlp1, m_w_mlp2, v_norm_mix_g, v_w_in, v_fox_qnorm_g, v_fox_knorm_g, v_fox_f_bias, v_fox_onorm_g, v_gdn_conv_w, v_gdn_A_log, v_gdn_dt_bias, v_gdn_onorm_g, v_w_out, v_norm_xattn_g, v_mem_norm_g, v_w_cq, v_w_ckv, v_xattn_qnorm_g, v_xattn_knorm_g, v_w_co, v_norm_mlp_g, v_w_mlp1, v_w_mlp2):
    given = dict(locals())
    w = {k: given[k] for k in WEIGHTS}
    m = {k: given["m_" + k] for k in WEIGHTS}
    v = {k: given["v_" + k] for k in WEIGHTS}

    core = lax.axis_index("c").astype(jnp.int32).reshape(1)
    chip = (2 * lax.axis_index("x") + lax.axis_index("y")).astype(jnp.int32).reshape(1)
    me = 4 * lax.axis_index("x") + 2 * lax.axis_index("y") + lax.axis_index("c")

    local = lambda d: {k: jnp.transpose(d[k][0]) if k in TRANSPOSED else d[k][0] for k in SHARDED}
    w2, m2, v2 = local(w), local(m), local(v)
    shards = {k: w2[k] if k == "gdn_conv_w" else w2[k].astype(BF16) for k in SHARDED}
    early = [k for k in SHARDED if not any(k in group for group in LATE_WEIGHTS)]
    gathered = _all_gather_hbm([shards[k] for k in early], "gather_early")
    whole = {k: _whole(k, g) for k, g in zip(early, gathered)}
    gathers, after = {}, gathered[0]
    for i, group in enumerate(LATE_WEIGHTS):
        lands = [lax.empty(_whole_shape(k, shards[k].shape), BF16) for k in group]
        gathers[group] = _copies_start("gather_late_start_" + str(i), [shards[k] for k in group], lands, _gather_copies, after=after)
        after = gathers[group][4]
    first_token = after[0, 0]

    def late_weights(group, after):
        gather = gathers[group]
        _, lands = _copies_wait("gather_late_wait_" + str(LATE_WEIGHTS.index(group)), gather[0], gather[1], gather[2], gather[3],
                                after, _gather_copies, own_block=True)
        return lands

    pending = []

    def grads_ready(group):
        names = list(group)
        tag = str(len(pending))
        own = [_blocks(k, group[k]) for k in names]
        if "w_in" in names:
            got = _pair_exchange(own, "grad_pair_exchange_" + tag)
            srcs = [_pair_sum(core, o, g, "grad_pair_sum_" + k) for k, o, g in zip(names, own, got)]
            copies, index, n_parts = _chip_copies, chip, 3
        else:
            srcs, copies, index, n_parts = own, _scatter_copies, me.astype(jnp.int32).reshape(1), 7
        lands = [lax.empty((n_parts,) + s.shape[1:], s.dtype) for s in srcs]
        started = _copies_start("grad_exchange_start_" + tag, srcs, lands, copies, after=core)
        pending.append((names, started, copies, index))
        return started[4][0, 0]

    small = {k: w[k] for k in REPLICATED}
    packed, grad_x, _ = _local_step(x, mem, loss_target, **small, **whole, late_weights=late_weights,
                                    grads_ready=grads_ready, first_token=first_token)

    small_lands = [lax.empty((N_DEV * PACK_ROWS, D_MODEL), F32)]
    small_gather = _copies_start("gather_small_start", [packed], small_lands, _gather_copies, after=grad_x)

    out_g, out_d, out_m, out_v = {}, {}, {}, {}
    after = small_gather[4]
    for tag, (names, started, copies, index) in enumerate(pending):
        srcs, parts = _copies_wait("grad_exchange_wait_" + str(tag), started[0], started[1], started[2], started[3], after, copies)
        for k, s, p in zip(names, srcs, parts):
            res = _sum_adam(index, s, p, w2[k], m2[k], v2[k], "adam_" + k)
            out_g[k], out_d[k], out_m[k], out_v[k] = ((jnp.transpose(r) if k in TRANSPOSED else r)[None] for r in res)
            after = res[0]

    _, (everyone,) = _copies_wait("gather_small_wait", small_gather[0], small_gather[1], small_gather[2], small_gather[3], after,
                                  _gather_copies, own_block=True)
    res = _adam_small(everyone.reshape(N_DEV, PACK_ROWS, D_MODEL), [w[k] for k in REPLICATED], [m[k] for k in REPLICATED],
                      [v[k] for k in REPLICATED])
    for i, k in enumerate(REPLICATED):
        out_g[k], out_d[k], out_m[k], out_v[k] = res[4 * i:4 * i + 4]
    loss = res[-1].reshape(())

    return (loss, grad_x, *[out_g[k] for k in WEIGHTS], *[out_d[k] for k in WEIGHTS], *[out_m[k] for k in WEIGHTS],
            *[out_v[k] for k in WEIGHTS])
```

```python
import functools

import jax
import jax.numpy as jnp
import numpy as np
from jax import lax
from jax.experimental import pallas as pl
from jax.experimental.pallas import tpu as pltpu

F32 = jnp.float32
BF16 = jnp.bfloat16

D_MODEL = 1024
FOX_HEADS = 8
FOX_HEAD_DIM = 64
FOX_WIDTH = 512
GDN_HEADS = 4
GDN_HEAD_DIM = 128
GDN_WIDTH = 512
CONV_WIDTH = 4
GDN_CHUNK = 128
GDN_GROUP = 4
FOX_BLOCK = 512
XATTN_HEADS = 4
XATTN_HEAD_DIM = 128
XATTN_WIDTH = 512
D_FF = 4096
EPS = 1e-6
NEG_INF = -1e30
N_DEV = 8

ADAM_LR = 0.001
ADAM_B1 = 0.9
ADAM_B2 = 0.999
ADAM_EPS = 1e-08
ADAM_WD = 0.01
ADAM_STEP = 10

P_FOX = 0
P_GDN = 1536
P_Z = 3072
P_SMALL = 3584
P_DIM = 3712
SM_F = 0
SM_B = 8
SM_A = 12
SM_ROWS = 16

LANES = 128
VMEM_LIMIT = 56 * 1024 * 1024

NN = (((1,), (0,)), ((), ()))
NT = (((1,), (1,)), ((), ()))
TN = (((0,), (0,)), ((), ()))


def _dot(a, b, dims=NN):
    return lax.dot_general(a.astype(BF16), b.astype(BF16), dims, preferred_element_type=F32)


def _cparams(sem=None):
    kw = dict(vmem_limit_bytes=VMEM_LIMIT)
    if sem is not None:
        kw["dimension_semantics"] = sem
    return pltpu.CompilerParams(**kw)


def _sigmoid(x):
    return 0.5 * (jnp.tanh(0.5 * x) + 1.0)


def _softplus(x):
    return jnp.maximum(x, 0.0) + jnp.log1p(jnp.exp(-jnp.abs(x)))


def _log_sigmoid(x):
    return -_softplus(-x)


def _rms(x, g):
    r = lax.rsqrt(jnp.mean(x * x, axis=-1, keepdims=True) + EPS)
    return x * r * g


def _rms_bwd(x, g, dy):
    r = lax.rsqrt(jnp.mean(x * x, axis=-1, keepdims=True) + EPS)
    xh = x * r
    dg = jnp.sum(dy * xh, axis=0, keepdims=True)
    dyg = dy * g
    dx = r * (dyg - xh * jnp.mean(dyg * xh, axis=-1, keepdims=True))
    return dx, dg


def _pair_stat(t, m0):
    s0 = jnp.sum(jnp.where(m0, t, 0.0), axis=-1, keepdims=True)
    s1 = jnp.sum(jnp.where(m0, 0.0, t), axis=-1, keepdims=True)
    return jnp.where(m0, s0, s1)


def _rms_pair(x, g, m0):
    r = lax.rsqrt(_pair_stat(x * x, m0) * (1.0 / FOX_HEAD_DIM) + EPS)
    return x * r * g


def _rms_pair_bwd(x, g, dy, m0):
    r = lax.rsqrt(_pair_stat(x * x, m0) * (1.0 / FOX_HEAD_DIM) + EPS)
    xh = x * r
    dg = jnp.sum(dy * xh, axis=0, keepdims=True)
    dyg = dy * g
    dx = r * (dyg - xh * (_pair_stat(dyg * xh, m0) * (1.0 / FOX_HEAD_DIM)))
    return dx, dg


@jax.custom_vjp
def _mm_nn(a, b):
    return _dot(a, b, NN)


_mm_nn.defvjp(lambda a, b: (_dot(a, b, NN), (a, b)),
              lambda r, g: (_dot(g, r[1], NT), _dot(r[0], g, TN)))


@jax.custom_vjp
def _mm_nt(a, b):
    return _dot(a, b, NT)


_mm_nt.defvjp(lambda a, b: (_dot(a, b, NT), (a, b)),
              lambda r, g: (_dot(g, r[1], NN), _dot(g, r[0], TN)))


@jax.custom_vjp
def _mm_tn(a, b):
    return _dot(a, b, TN)


_mm_tn.defvjp(lambda a, b: (_dot(a, b, TN), (a, b)),
              lambda r, g: (_dot(r[1], g, NT), _dot(r[0], g, NN)))


def _dot3(a, b, dims):
    ah = a.astype(BF16)
    al = (a - ah.astype(F32)).astype(BF16)
    bh = b.astype(BF16)
    bl = (b - bh.astype(F32)).astype(BF16)
    d = functools.partial(lax.dot_general, dimension_numbers=dims, preferred_element_type=F32)
    return d(ah, bh) + d(ah, bl) + d(al, bh)


def _neumann_inverses(mats):
    c = mats[0].shape[0]
    eye = (lax.broadcasted_iota(jnp.int32, (c, c), 0) == lax.broadcasted_iota(jnp.int32, (c, c), 1)).astype(F32)
    xs = [eye - a for a in mats]
    ps = list(mats)
    k = 2
    while k < c + 1:
        ps = [_dot3(p, p, NN) for p in ps]
        xs = [x + _dot3(x, p, NN) for x, p in zip(xs, ps)]
        k *= 2
    return xs


@jax.custom_vjp
def _unit_lower_inverses(mats):
    return _neumann_inverses(mats)


def _unit_lower_inverses_fwd(mats):
    ts = _neumann_inverses(mats)
    return ts, ts


def _unit_lower_inverses_bwd(ts, gs):
    left = [_dot3(t, g, TN) for t, g in zip(ts, gs)]
    return ([-_dot3(m, t, NT) for m, t in zip(left, ts)],)


_unit_lower_inverses.defvjp(_unit_lower_inverses_fwd, _unit_lower_inverses_bwd)


def _wgrad(a, b, name, bk=1024, bn=1024, bt=1024, column_blocks=None):
    t_len, k_len = a.shape
    n_len = b.shape[1]
    bk, bn, bt = min(bk, k_len), min(bn, n_len), min(bt, t_len)
    nt = t_len // bt

    def body(a_ref, b_ref, o_ref, acc_ref):
        t = pl.program_id(2)

        @pl.when(t == 0)
        def _():
            acc_ref[...] = jnp.zeros_like(acc_ref)

        acc_ref[...] += _dot(a_ref[...], b_ref[...], TN)

        @pl.when(t == nt - 1)
        def _():
            if column_blocks:
                for jj in range(bn // column_blocks):
                    o_ref[jj] = acc_ref[:, jj * column_blocks:(jj + 1) * column_blocks]
            else:
                o_ref[...] = acc_ref[...]

    if column_blocks:
        out_spec = pl.BlockSpec((bn // column_blocks, bk, column_blocks), lambda i, j, t: (j, i, 0))
        out_shape = jax.ShapeDtypeStruct((n_len // column_blocks, k_len, column_blocks), F32)
    else:
        out_spec = pl.BlockSpec((bk, bn), lambda i, j, t: (i, j))
        out_shape = jax.ShapeDtypeStruct((k_len, n_len), F32)
    return pl.pallas_call(
        body, name=name, grid=(k_len // bk, n_len // bn, nt),
        in_specs=[pl.BlockSpec((bt, bk), lambda i, j, t: (t, i)), pl.BlockSpec((bt, bn), lambda i, j, t: (t, j))],
        out_specs=out_spec, out_shape=out_shape,
        scratch_shapes=[pltpu.VMEM((bk, bn), F32)],
        compiler_params=_cparams(("parallel", "parallel", "arbitrary")),
    )(a, b)


def _wgrad_stacked(pieces, b, name, bn=512, bt=1024):
    t_len, n_len = b.shape
    n_p = len(pieces)
    starts = [int(s) for s in np.cumsum([0] + [p.shape[1] for p in pieces])]
    bn, bt = min(bn, n_len), min(bt, t_len)
    nt = t_len // bt

    def body(*refs):
        b_ref, o_ref, acc_ref = refs[n_p:]
        t = pl.program_id(1)

        @pl.when(t == 0)
        def _():
            acc_ref[...] = jnp.zeros_like(acc_ref)

        for k in range(n_p):
            acc_ref[starts[k]:starts[k + 1], :] += _dot(refs[k][...], b_ref[...], TN)

        @pl.when(t == nt - 1)
        def _():
            o_ref[...] = acc_ref[...]

    return pl.pallas_call(
        body, name=name, grid=(n_len // bn, nt),
        in_specs=[pl.BlockSpec((bt, p.shape[1]), lambda j, t: (t, 0)) for p in pieces] + [pl.BlockSpec((bt, bn), lambda j, t: (t, j))],
        out_specs=pl.BlockSpec((starts[-1], bn), lambda j, t: (0, j)),
        out_shape=jax.ShapeDtypeStruct((starts[-1], n_len), F32),
        scratch_shapes=[pltpu.VMEM((starts[-1], bn), F32)],
        compiler_params=_cparams(("parallel", "arbitrary")),
    )(*pieces, b)


def _rows_matmul(a, b, name, bt=512):
    r_len, t_len = a.shape
    n_len = b.shape[1]
    bt = min(bt, t_len)
    nt = t_len // bt

    def body(a_ref, b_ref, o_ref):
        t = pl.program_id(0)

        @pl.when(t == 0)
        def _():
            o_ref[...] = jnp.zeros_like(o_ref)

        o_ref[...] += _dot(a_ref[...], b_ref[...], NN)

    return pl.pallas_call(
        body, name=name, grid=(nt,),
        in_specs=[pl.BlockSpec((r_len, bt), lambda t: (0, t)), pl.BlockSpec((bt, n_len), lambda t: (t, 0))],
        out_specs=pl.BlockSpec((r_len, n_len), lambda t: (0, 0)),
        out_shape=jax.ShapeDtypeStruct((r_len, n_len), F32),
        compiler_params=_cparams(("arbitrary",)),
    )(a, b)


def _in_proj(x, g, wp, wst, tm=512):
    t_len, d = x.shape
    tm = min(tm, t_len)

    def body(x_ref, g_ref, wp_ref, wst_ref, h_ref, fox_ref, gdn_ref, z_ref, sm_ref, smt_ref):
        h = _rms(x_ref[...], g_ref[...]).astype(BF16)
        h_ref[...] = h
        p = _dot(h, wp_ref[...], NT)
        fox_ref[...] = p[:, P_FOX:P_GDN]
        gdn_ref[...] = p[:, P_GDN:P_Z]
        z_ref[...] = p[:, P_Z:P_SMALL]
        sm_ref[...] = p[:, P_SMALL:P_DIM]
        smt_ref[...] = _dot(wst_ref[...], h, NT)

    row = lambda i: (i, 0)
    fixed = lambda i: (0, 0)
    return pl.pallas_call(
        body, name="in_proj", grid=(t_len // tm,),
        in_specs=[pl.BlockSpec((tm, d), row), pl.BlockSpec((1, d), fixed), _resident((P_DIM, d)),
                  pl.BlockSpec((SM_ROWS, d), fixed)],
        out_specs=[pl.BlockSpec((tm, d), row), pl.BlockSpec((tm, 1536), row), pl.BlockSpec((tm, 1536), row),
                   pl.BlockSpec((tm, 512), row), pl.BlockSpec((tm, LANES), row), pl.BlockSpec((SM_ROWS, tm), lambda i: (0, i))],
        out_shape=[jax.ShapeDtypeStruct((t_len, d), BF16), jax.ShapeDtypeStruct((t_len, 1536), F32),
                   jax.ShapeDtypeStruct((t_len, 1536), F32), jax.ShapeDtypeStruct((t_len, 512), F32),
                   jax.ShapeDtypeStruct((t_len, LANES), F32), jax.ShapeDtypeStruct((SM_ROWS, t_len), F32)],
        compiler_params=_cparams(("parallel",)),
    )(x, g, wp, wst)


def _in_proj_bwd(dprojs, dsmt, x, g, wp, wst, dx1, tm=512):
    t_len, d = x.shape
    tm = min(tm, t_len)
    n_p = len(dprojs)
    starts = np.cumsum([0] + [p.shape[1] for p in dprojs])

    def body(*refs):
        dp_refs = refs[:n_p]
        dst_ref, x_ref, g_ref, wp_ref, wst_ref, dx1_ref, dx_ref, dg_ref = refs[n_p:]
        i = pl.program_id(0)
        dh = _dot(dst_ref[...], wst_ref[...], TN)
        for k in range(n_p):
            dh = dh + _dot(dp_refs[k][...], wp_ref[int(starts[k]):int(starts[k + 1]), :], NN)
        dxn, dg = _rms_bwd(x_ref[...], g_ref[...], dh)
        dx_ref[...] = dx1_ref[...] + dxn

        @pl.when(i == 0)
        def _():
            dg_ref[...] = jnp.zeros_like(dg_ref)

        dg_ref[...] += dg

    row = lambda i: (i, 0)
    fixed = lambda i: (0, 0)
    return pl.pallas_call(
        body, name="in_proj_bwd", grid=(t_len // tm,),
        in_specs=[pl.BlockSpec((tm, p.shape[1]), row) for p in dprojs] + [
            pl.BlockSpec((SM_ROWS, tm), lambda i: (0, i)), pl.BlockSpec((tm, d), row),
            pl.BlockSpec((1, d), fixed), _resident((P_DIM, d)), pl.BlockSpec((SM_ROWS, d), fixed),
            pl.BlockSpec((tm, d), row)],
        out_specs=[pl.BlockSpec((tm, d), row), pl.BlockSpec((1, d), fixed)],
        out_shape=[jax.ShapeDtypeStruct((t_len, d), F32), jax.ShapeDtypeStruct((1, d), F32)],
        compiler_params=_cparams(("arbitrary",)),
    )(*dprojs, dsmt, x, g, wp, wst, dx1)


def _fox_cum(smt, bias_col, n_batch, s_len, ck=256):
    ck = min(ck, s_len)

    def body(s_ref, b_ref, c_ref):
        tri = (lax.broadcasted_iota(jnp.int32, (ck, ck), 0) <= lax.broadcasted_iota(jnp.int32, (ck, ck), 1)).astype(F32)
        carry = jnp.zeros((SM_ROWS, 1), F32)
        for r in range(s_len // ck):
            ls = _log_sigmoid(s_ref[:, r * ck:(r + 1) * ck] + b_ref[...])
            c = jnp.dot(ls, tri, precision=lax.Precision.HIGHEST, preferred_element_type=F32) + carry
            c_ref[:, r * ck:(r + 1) * ck] = c
            carry = c[:, ck - 1:ck]

    return pl.pallas_call(
        body, name="fox_cum", grid=(n_batch,),
        in_specs=[pl.BlockSpec((SM_ROWS, s_len), lambda b: (0, b)), pl.BlockSpec((SM_ROWS, 1), lambda b: (0, 0))],
        out_specs=pl.BlockSpec((SM_ROWS, s_len), lambda b: (0, b)),
        out_shape=jax.ShapeDtypeStruct(smt.shape, F32),
        compiler_params=_cparams(("parallel",)),
    )(smt, bias_col)


def _fox_cum_bwd(dc, smt, bias_col, n_batch, s_len, ck=256):
    ck = min(ck, s_len)
    nr = s_len // ck

    def body(dc_ref, s_ref, b_ref, dl_ref, db_ref):
        b = pl.program_id(0)
        tri = (lax.broadcasted_iota(jnp.int32, (ck, ck), 0) >= lax.broadcasted_iota(jnp.int32, (ck, ck), 1)).astype(F32)
        carry = jnp.zeros((SM_ROWS, 1), F32)
        tot = jnp.zeros((SM_ROWS, 1), F32)
        for r in reversed(range(nr)):
            sl = slice(r * ck, (r + 1) * ck)
            dls = jnp.dot(dc_ref[:, sl], tri, precision=lax.Precision.HIGHEST, preferred_element_type=F32) + carry
            carry = dls[:, 0:1]
            dl = dls * (1.0 - _sigmoid(s_ref[:, sl] + b_ref[...]))
            dl_ref[:, sl] = dl
            tot = tot + jnp.sum(dl, axis=1, keepdims=True)

        @pl.when(b == 0)
        def _():
            db_ref[...] = jnp.zeros_like(db_ref)

        db_ref[...] += jnp.broadcast_to(tot, db_ref.shape)

    return pl.pallas_call(
        body, name="fox_cum_bwd", grid=(n_batch,),
        in_specs=[pl.BlockSpec((SM_ROWS, s_len), lambda b: (0, b)), pl.BlockSpec((SM_ROWS, s_len), lambda b: (0, b)),
                  pl.BlockSpec((SM_ROWS, 1), lambda b: (0, 0))],
        out_specs=[pl.BlockSpec((SM_ROWS, s_len), lambda b: (0, b)), pl.BlockSpec((SM_ROWS, LANES), lambda b: (0, 0))],
        out_shape=[jax.ShapeDtypeStruct(smt.shape, F32), jax.ShapeDtypeStruct((SM_ROWS, LANES), F32)],
        compiler_params=_cparams(("arbitrary",)),
    )(dc, smt, bias_col)


def _fox_diagonal_mask(tq):
    return lax.broadcasted_iota(jnp.int32, (tq, tq), 1) <= lax.broadcasted_iota(jnp.int32, (tq, tq), 0)


def _fox_fwd(pf, cb, gq2, gk2, go2, tq=256):
    n_batch, s_len, _ = pf.shape
    tq = min(tq, s_len)
    nq = s_len // tq
    scale = FOX_HEAD_DIM ** -0.5

    def body(q_ref, k_ref, v_ref, c_ref, gq_ref, gk_ref, go_ref, o_ref, on_ref, lse_ref, kh_ref, vh_ref):
        j = pl.program_id(1)
        i = pl.program_id(2)
        m0 = lax.broadcasted_iota(jnp.int32, (1, LANES), 1) < FOX_HEAD_DIM

        @pl.when(i == 0)
        def _():
            kn = _rms_pair(k_ref[0], gk_ref[...], m0)
            kh_ref[0] = jnp.where(m0, kn, 0.0).astype(BF16)
            kh_ref[1] = jnp.where(m0, 0.0, kn).astype(BF16)
            v = v_ref[0]
            vh_ref[0] = jnp.where(m0, v, 0.0).astype(BF16)
            vh_ref[1] = jnp.where(m0, 0.0, v).astype(BF16)

        qb = (_rms_pair(q_ref[0], gq_ref[...], m0) * scale).astype(BF16)

        def step(kb, carry, diagonal=False):
            ms, ls, acc = carry
            off = pl.multiple_of(kb * tq, tq)
            new_m, new_l, alphas, pv = [], [], [], []
            for hh in range(2):
                s = _dot(qb, kh_ref[hh, pl.ds(off, tq), :], NT)
                s = s - c_ref[0, kb, pl.ds(2 * j + hh, 1), :]
                if diagonal:
                    s = jnp.where(_fox_diagonal_mask(tq), s, NEG_INF)
                m_new = jnp.maximum(ms[hh], jnp.max(s, axis=-1, keepdims=True))
                alpha = jnp.exp(ms[hh] - m_new)
                p = jnp.exp(s - m_new)
                new_l.append(alpha * ls[hh] + jnp.sum(p, axis=-1, keepdims=True))
                new_m.append(m_new)
                alphas.append(alpha)
                pv.append(_dot(p, vh_ref[hh, pl.ds(off, tq), :], NN))
            acc = jnp.where(m0, alphas[0], alphas[1]) * acc + pv[0] + pv[1]
            return tuple(new_m), tuple(new_l), acc

        init_m = (jnp.full((tq, 1), NEG_INF, F32),) * 2
        init_l = (jnp.zeros((tq, 1), F32),) * 2
        carry = lax.fori_loop(0, i, step, (init_m, init_l, jnp.zeros((tq, LANES), F32)))
        ms, ls, acc = step(i, carry, diagonal=True)
        o = acc / jnp.where(m0, ls[0], ls[1])
        o_ref[0] = o
        on_ref[0] = _rms_pair(o, go_ref[...], m0).astype(BF16)
        lse_ref[0] = jnp.where(m0, ms[0] + jnp.log(ls[0]), ms[1] + jnp.log(ls[1]))

    fixed = lambda b, j, i: (0, 0)
    tile = lambda b, j, i: (b, i, j)
    return pl.pallas_call(
        body, name="fox_fwd", grid=(n_batch, 4, nq),
        in_specs=[pl.BlockSpec((1, tq, LANES), tile), pl.BlockSpec((1, s_len, LANES), lambda b, j, i: (b, 0, 4 + j)),
                  pl.BlockSpec((1, s_len, LANES), lambda b, j, i: (b, 0, 8 + j)),
                  pl.BlockSpec((1, nq, SM_ROWS, tq), lambda b, j, i: (b, 0, 0, 0)),
                  pl.BlockSpec((1, LANES), fixed), pl.BlockSpec((1, LANES), fixed), pl.BlockSpec((1, LANES), fixed)],
        out_specs=[pl.BlockSpec((1, tq, LANES), tile), pl.BlockSpec((1, tq, LANES), tile), pl.BlockSpec((1, tq, LANES), tile)],
        out_shape=[jax.ShapeDtypeStruct((n_batch, s_len, FOX_WIDTH), F32), jax.ShapeDtypeStruct((n_batch, s_len, FOX_WIDTH), BF16),
                   jax.ShapeDtypeStruct((n_batch, s_len, FOX_WIDTH), F32)],
        scratch_shapes=[pltpu.VMEM((2, s_len, LANES), BF16), pltpu.VMEM((2, s_len, LANES), BF16)],
        compiler_params=_cparams(("parallel", "parallel", "arbitrary")),
    )(pf, pf, pf, cb, gq2, gk2, go2)


def _fox_bwd(pf, cb, gq2, gk2, go2, o, lse, don, tq=256):
    n_batch, s_len, _ = pf.shape
    tq = min(tq, s_len)
    nq = s_len // tq
    scale = FOX_HEAD_DIM ** -0.5

    def body(q_ref, k_ref, v_ref, c_ref, gq_ref, gk_ref, go_ref, o_ref, lse_ref, don_ref,
             dq_ref, dk_ref, dv_ref, dc_ref, dgq_ref, dgk_ref, dgo_ref, kh_ref, vh_ref, dka_ref, dva_ref, dca_ref):
        b = pl.program_id(0)
        j = pl.program_id(1)
        i = pl.program_id(2)
        m0 = lax.broadcasted_iota(jnp.int32, (1, LANES), 1) < FOX_HEAD_DIM

        @pl.when((b == 0) & (j == 0) & (i == 0))
        def _():
            dgq_ref[...] = jnp.zeros_like(dgq_ref)
            dgk_ref[...] = jnp.zeros_like(dgk_ref)
            dgo_ref[...] = jnp.zeros_like(dgo_ref)

        @pl.when(i == 0)
        def _():
            kn = _rms_pair(k_ref[0], gk_ref[...], m0)
            kh_ref[0] = jnp.where(m0, kn, 0.0).astype(BF16)
            kh_ref[1] = jnp.where(m0, 0.0, kn).astype(BF16)
            v = v_ref[0]
            vh_ref[0] = jnp.where(m0, v, 0.0).astype(BF16)
            vh_ref[1] = jnp.where(m0, 0.0, v).astype(BF16)
            dka_ref[...] = jnp.zeros_like(dka_ref)
            dva_ref[...] = jnp.zeros_like(dva_ref)
            dca_ref[...] = jnp.zeros_like(dca_ref)

        q = q_ref[0]
        qn = _rms_pair(q, gq_ref[...], m0)
        qs = qn * scale
        qb = qs.astype(BF16)
        qh = (jnp.where(m0, qs, 0.0).astype(BF16), jnp.where(m0, 0.0, qs).astype(BF16))
        ot = o_ref[0]
        do, dgo = _rms_pair_bwd(ot, go_ref[...], don_ref[0], m0)
        dgo_ref[...] += dgo
        dd = do * ot
        delta = (jnp.sum(jnp.where(m0, dd, 0.0), axis=-1, keepdims=True), jnp.sum(jnp.where(m0, 0.0, dd), axis=-1, keepdims=True))
        doh = (jnp.where(m0, do, 0.0).astype(BF16), jnp.where(m0, 0.0, do).astype(BF16))
        lse_t = lse_ref[0]
        lse_h = (lse_t[:, 0:1], lse_t[:, FOX_HEAD_DIM:FOX_HEAD_DIM + 1])

        def step(kb, carry, diagonal=False):
            dqn, rs = carry
            rs = list(rs)
            off = pl.multiple_of(kb * tq, tq)
            for hh in range(2):
                kblk = kh_ref[hh, pl.ds(off, tq), :]
                vblk = vh_ref[hh, pl.ds(off, tq), :]
                s = _dot(qb, kblk, NT)
                s = s - c_ref[0, kb, pl.ds(2 * j + hh, 1), :]
                if diagonal:
                    s = jnp.where(_fox_diagonal_mask(tq), s, NEG_INF)
                p = jnp.exp(s - lse_h[hh])
                dp = _dot(doh[hh], vblk, NT)
                ds = p * (dp - delta[hh])
                dva_ref[pl.ds(off, tq), :] += _dot(p, doh[hh], TN)
                dka_ref[pl.ds(off, tq), :] += _dot(ds, qh[hh], TN)
                dca_ref[kb, hh:hh + 1, :] += -jnp.sum(ds, axis=0, keepdims=True)
                rs[hh] = rs[hh] + jnp.sum(ds, axis=-1, keepdims=True)
                dqn = dqn + _dot(ds, kblk, NN)
            return dqn, tuple(rs)

        carry = lax.fori_loop(0, i, step, (jnp.zeros((tq, LANES), F32), (jnp.zeros((tq, 1), F32),) * 2))
        dqn, rs = step(i, carry, diagonal=True)
        dqn = dqn * scale
        rs_rows = jnp.where(m0, rs[0], rs[1]).T
        dca_ref[i, 0:1, :] += rs_rows[0:1, :]
        dca_ref[i, 1:2, :] += rs_rows[FOX_HEAD_DIM:FOX_HEAD_DIM + 1, :]
        dq, dgq = _rms_pair_bwd(q, gq_ref[...], dqn, m0)
        dq_ref[0] = dq.astype(BF16)
        dgq_ref[...] += dgq

        @pl.when(i == nq - 1)
        def _():
            dk, dgk = _rms_pair_bwd(k_ref[0], gk_ref[...], dka_ref[...], m0)
            dk_ref[0] = dk.astype(BF16)
            dgk_ref[...] += dgk
            dv_ref[0] = dva_ref[...].astype(BF16)
            dc_ref[0, 0] = dca_ref[...]

    fixed = lambda b, j, i: (0, 0)
    tile = lambda b, j, i: (b, i, j)
    full = lambda b, j, i: (b, 0, j)
    wide = jax.ShapeDtypeStruct((n_batch, s_len, FOX_WIDTH), BF16)
    gain = jax.ShapeDtypeStruct((1, LANES), F32)
    return pl.pallas_call(
        body, name="fox_bwd", grid=(n_batch, 4, nq),
        in_specs=[pl.BlockSpec((1, tq, LANES), tile), pl.BlockSpec((1, s_len, LANES), lambda b, j, i: (b, 0, 4 + j)),
                  pl.BlockSpec((1, s_len, LANES), lambda b, j, i: (b, 0, 8 + j)),
                  pl.BlockSpec((1, nq, SM_ROWS, tq), lambda b, j, i: (b, 0, 0, 0)),
                  pl.BlockSpec((1, LANES), fixed), pl.BlockSpec((1, LANES), fixed), pl.BlockSpec((1, LANES), fixed),
                  pl.BlockSpec((1, tq, LANES), tile), pl.BlockSpec((1, tq, LANES), tile), pl.BlockSpec((1, tq, LANES), tile)],
        out_specs=[pl.BlockSpec((1, tq, LANES), tile), pl.BlockSpec((1, s_len, LANES), full), pl.BlockSpec((1, s_len, LANES), full),
                   pl.BlockSpec((1, 1, nq, 8, tq), lambda b, j, i: (b, j, 0, 0, 0)),
                   pl.BlockSpec((1, LANES), fixed), pl.BlockSpec((1, LANES), fixed), pl.BlockSpec((1, LANES), fixed)],
        out_shape=[wide, wide, wide, jax.ShapeDtypeStruct((n_batch, 4, nq, 8, tq), F32), gain, gain, gain],
        scratch_shapes=[pltpu.VMEM((2, s_len, LANES), BF16), pltpu.VMEM((2, s_len, LANES), BF16),
                        pltpu.VMEM((s_len, LANES), F32), pltpu.VMEM((s_len, LANES), F32), pltpu.VMEM((nq, 8, tq), F32)],
        compiler_params=_cparams(("arbitrary", "arbitrary", "arbitrary")),
    )(pf, pf, pf, cb, gq2, gk2, go2, o, lse, don)


def _shift_down(x, k):
    row = lax.broadcasted_iota(jnp.int32, x.shape, 0)
    return jnp.where(row >= k, pltpu.roll(x, k, 0), 0.0)


def _shift_up(x, k):
    n = x.shape[0]
    row = lax.broadcasted_iota(jnp.int32, x.shape, 0)
    return jnp.where(row < n - k, pltpu.roll(x, n - k, 0), 0.0)


def _conv_silu(x, w):
    y = w[3:4] * x + w[2:3] * _shift_down(x, 1) + w[1:2] * _shift_down(x, 2) + w[0:1] * _shift_down(x, 3)
    sig = _sigmoid(y)
    return y, sig, y * sig


def _gdn_pre(pg, conv_w):
    n_batch, s_len, width = pg.shape
    ncb = width // LANES

    def body(x_ref, w_ref, o_ref):
        cb = pl.program_id(1)
        _, _, s = _conv_silu(x_ref[0], w_ref[...])
        sn = s * lax.rsqrt(jnp.sum(s * s, axis=-1, keepdims=True) + EPS)
        o_ref[0] = jnp.where(cb < 2 * GDN_HEADS, sn, s)

    return pl.pallas_call(
        body, name="gdn_pre", grid=(n_batch, ncb),
        in_specs=[pl.BlockSpec((1, s_len, LANES), lambda b, c: (b, 0, c)), pl.BlockSpec((8, LANES), lambda b, c: (0, c))],
        out_specs=pl.BlockSpec((1, s_len, LANES), lambda b, c: (b, 0, c)),
        out_shape=jax.ShapeDtypeStruct(pg.shape, F32),
        compiler_params=_cparams(("parallel", "parallel")),
    )(pg, conv_w)


def _gdn_pre_bwd(pg, conv_w, dout):
    n_batch, s_len, width = pg.shape
    ncb = width // LANES

    def body(x_ref, w_ref, d_ref, dx_ref, dw_ref):
        cb = pl.program_id(0)
        b = pl.program_id(1)
        x = x_ref[0]
        w = w_ref[...]
        d = d_ref[0]
        y, sig, s = _conv_silu(x, w)
        rr = lax.rsqrt(jnp.sum(s * s, axis=-1, keepdims=True) + EPS)
        sn = s * rr
        ds_n = rr * (d - sn * jnp.sum(d * sn, axis=-1, keepdims=True))
        ds = jnp.where(cb < 2 * GDN_HEADS, ds_n, d)
        dy = ds * (sig * (1.0 + y * (1.0 - sig)))
        dyu = [_shift_up(dy, 3 - jj) if jj < 3 else dy for jj in range(CONV_WIDTH)]
        dx = w[0:1] * dyu[0] + w[1:2] * dyu[1] + w[2:3] * dyu[2] + w[3:4] * dyu[3]
        dx_ref[0] = dx.astype(BF16)
        dw = [jnp.sum(dyu[jj] * x, axis=0, keepdims=True) for jj in range(CONV_WIDTH)]
        rows = lax.broadcasted_iota(jnp.int32, (8, LANES), 0)
        dwb = jnp.zeros((8, LANES), F32)
        for jj in range(CONV_WIDTH):
            dwb = dwb + jnp.where(rows == jj, dw[jj], 0.0)

        @pl.when(b == 0)
        def _():
            dw_ref[...] = jnp.zeros_like(dw_ref)

        dw_ref[...] += dwb

    blk = lambda c, b: (b, 0, c)
    return pl.pallas_call(
        body, name="gdn_pre_bwd", grid=(ncb, n_batch),
        in_specs=[pl.BlockSpec((1, s_len, LANES), blk), pl.BlockSpec((8, LANES), lambda c, b: (0, c)), pl.BlockSpec((1, s_len, LANES), blk)],
        out_specs=[pl.BlockSpec((1, s_len, LANES), blk), pl.BlockSpec((8, LANES), lambda c, b: (0, c))],
        out_shape=[jax.ShapeDtypeStruct(pg.shape, BF16), jax.ShapeDtypeStruct((8, width), F32)],
        compiler_params=_cparams(("parallel", "arbitrary")),
    )(pg, conv_w, dout)


def _gdn_gates(smc, smr, a_c, dt_c, a_r, dt_r, h):
    lane = lax.broadcasted_iota(jnp.int32, (1, LANES), 1)
    sub = lax.broadcasted_iota(jnp.int32, (SM_ROWS, 1), 0)
    beta_c = jnp.sum(jnp.where(lane == SM_B + h, _sigmoid(smc), 0.0), axis=1, keepdims=True)
    g_all_c = -jnp.exp(a_c) * _softplus(smc + dt_c)
    g_c = jnp.sum(jnp.where(lane == SM_A + h, g_all_c, 0.0), axis=1, keepdims=True)
    g_all_r = -jnp.exp(a_r) * _softplus(smr + dt_r)
    g_r = jnp.sum(jnp.where(sub == SM_A + h, g_all_r, 0.0), axis=0, keepdims=True)
    return beta_c, g_c, g_r


@jax.custom_vjp
def _known_inverse(a, t):
    return t


_known_inverse.defvjp(lambda a, t: (t, t),
                      lambda t, g: (-_dot3(_dot3(t, g, TN), t, NT), jnp.zeros_like(t)))


def _gdn_group(qkv, z, smc, smr, a_c, dt_c, a_r, dt_r, go, states, inverses=None):
    n_grp = len(qkv)
    c = qkv[0].shape[0]
    hd = GDN_HEAD_DIM
    pairs = [(g, h) for g in range(n_grp) for h in range(GDN_HEADS)]
    ii = lax.broadcasted_iota(jnp.int32, (c, c), 0)
    jj = lax.broadcasted_iota(jnp.int32, (c, c), 1)
    incl = ii >= jj
    col = lambda arr, base, h: arr[:, base + h * hd:base + (h + 1) * hd]

    qs, ks, kbs, vbs, gcs, g_lasts, amats, intras = [], [], [], [], [], [], [], []
    for g, h in pairs:
        beta_c, g_c, g_r = _gdn_gates(smc[g], smr[g], a_c, dt_c, a_r, dt_r, h)
        gc_c = jnp.sum(jnp.where(incl, g_r, 0.0), axis=1, keepdims=True)
        gc_r = jnp.sum(jnp.where(ii <= jj, g_c, 0.0), axis=0, keepdims=True)
        decay = jnp.where(incl, jnp.exp(jnp.where(incl, gc_c - gc_r, 0.0)), 0.0)
        k = col(qkv[g], GDN_WIDTH, h)
        kb = k * beta_c
        qs.append(col(qkv[g], 0, h) * (hd ** -0.5))
        ks.append(k)
        kbs.append(kb)
        vbs.append(col(qkv[g], 2 * GDN_WIDTH, h) * beta_c)
        gcs.append(gc_c)
        g_lasts.append(jnp.sum(g_c, axis=0, keepdims=True))
        both = _mm_nt(jnp.concatenate([kb, qs[-1]], axis=0), k)
        amats.append(jnp.where(ii > jj, both[0:c] * decay, 0.0))
        intras.append(both[c:2 * c] * decay)
    ts = _unit_lower_inverses(amats) if inverses is None else [_known_inverse(a, t) for a, t in zip(amats, inverses)]
    egcs = [jnp.exp(gc) for gc in gcs]
    uws = [_mm_nn(t, jnp.concatenate([vb, kb * e], axis=1)) for t, vb, kb, e in zip(ts, vbs, kbs, egcs)]
    us = [uw[:, 0:hd] for uw in uws]
    ws = [uw[:, hd:2 * hd] for uw in uws]
    qes = [q * e for q, e in zip(qs, egcs)]
    kds = [k * jnp.exp(gl - gc) for k, gl, gc in zip(ks, g_lasts, gcs)]
    sdecs = [jnp.exp(gl) for gl in g_lasts]

    outs = []
    for g in range(n_grp):
        idx = [g * GDN_HEADS + h for h in range(GDN_HEADS)]
        v_new = [us[i] - _mm_nn(ws[i], states[h]) for h, i in enumerate(idx)]
        o = [_mm_nn(jnp.concatenate([qes[i], intras[i]], axis=1), jnp.concatenate([states[h], v_new[h]], axis=0))
             for h, i in enumerate(idx)]
        states = [states[h] * sdecs[i] + _mm_tn(kds[i], v_new[h]) for h, i in enumerate(idx)]
        outs.append([_rms(o[h], go) * (col(z[g], 0, h) * _sigmoid(col(z[g], 0, h))) for h in range(GDN_HEADS)])
    return outs, states, ts


def _gdn_group_size(n_chunks):
    return GDN_GROUP if n_chunks % GDN_GROUP == 0 else 1


def _gdn_fwd(qkvn, z, smc, smr, a_c, dt_c, a_r, dt_r, go):
    n_batch, s_len, _ = qkvn.shape
    c = GDN_CHUNK
    n = s_len // c
    grp = _gdn_group_size(n)
    ng = n // grp
    gc = grp * c
    hd = GDN_HEAD_DIM

    def body(qkv_ref, z_ref, smc_ref, smr_ref, ac_ref, dc_ref, ar_ref, dr_ref, go_ref, og_ref, st_ref, inv_ref, s_ref):
        @pl.when(pl.program_id(1) == 0)
        def _():
            s_ref[...] = jnp.zeros_like(s_ref)

        states = [s_ref[h] for h in range(GDN_HEADS)]
        for h in range(GDN_HEADS):
            st_ref[0, 0, h] = states[h]
        rows = lambda k: slice(k * c, (k + 1) * c)
        outs, nxt, invs = _gdn_group([qkv_ref[0, rows(k), :] for k in range(grp)], [z_ref[0, rows(k), :] for k in range(grp)],
                                     [smc_ref[0, rows(k), :] for k in range(grp)], [smr_ref[k] for k in range(grp)],
                                     ac_ref[...], dc_ref[...], ar_ref[...], dr_ref[...], go_ref[...], states)
        for k in range(grp):
            for h in range(GDN_HEADS):
                og_ref[0, rows(k), h * hd:(h + 1) * hd] = outs[k][h].astype(BF16)
        for p, inv in enumerate(invs):
            inv_ref[0, 0, p] = inv
        for h in range(GDN_HEADS):
            s_ref[h] = nxt[h]

    tok = lambda b, i: (b, i, 0)
    fixed = lambda b, i: (0, 0)
    return pl.pallas_call(
        body, name="gdn_fwd", grid=(n_batch, ng),
        in_specs=[pl.BlockSpec((1, gc, 3 * GDN_WIDTH), tok), pl.BlockSpec((1, gc, GDN_WIDTH), tok), pl.BlockSpec((1, gc, LANES), tok),
                  pl.BlockSpec((grp, SM_ROWS, c), lambda b, i: (b * ng + i, 0, 0)),
                  pl.BlockSpec((1, LANES), fixed), pl.BlockSpec((1, LANES), fixed), pl.BlockSpec((SM_ROWS, 1), fixed),
                  pl.BlockSpec((SM_ROWS, 1), fixed), pl.BlockSpec((1, LANES), fixed)],
        out_specs=[pl.BlockSpec((1, gc, GDN_WIDTH), tok), pl.BlockSpec((1, 1, GDN_HEADS, hd, hd), lambda b, i: (b, i, 0, 0, 0)),
                   pl.BlockSpec((1, 1, grp * GDN_HEADS, c, c), lambda b, i: (b, i, 0, 0, 0))],
        out_shape=[jax.ShapeDtypeStruct((n_batch, s_len, GDN_WIDTH), BF16), jax.ShapeDtypeStruct((n_batch, ng, GDN_HEADS, hd, hd), F32),
                   jax.ShapeDtypeStruct((n_batch, ng, grp * GDN_HEADS, c, c), F32)],
        scratch_shapes=[pltpu.VMEM((GDN_HEADS, hd, hd), F32)],
        compiler_params=_cparams(("parallel", "arbitrary")),
    )(qkvn, z, smc, smr, a_c, dt_c, a_r, dt_r, go)


def _gdn_bwd(qkvn, z, smc, smr, a_c, dt_c, a_r, dt_r, go, states, inverses, dog):
    n_batch, s_len, _ = qkvn.shape
    c = GDN_CHUNK
    n = s_len // c
    grp = _gdn_group_size(n)
    ng = n // grp
    gc = grp * c
    hd = GDN_HEAD_DIM

    def body(qkv_ref, z_ref, smc_ref, smr_ref, ac_ref, dc_ref, ar_ref, dr_ref, go_ref, st_ref, inv_ref, dog_ref,
             dqkv_ref, dz_ref, dsmc_ref, dsmr_ref, dac_ref, ddc_ref, dar_ref, ddr_ref, dgo_ref, ds_ref):
        first = (pl.program_id(0) == 0) & (pl.program_id(1) == 0)

        @pl.when(pl.program_id(1) == 0)
        def _():
            ds_ref[...] = jnp.zeros_like(ds_ref)

        @pl.when(first)
        def _():
            for r in (dac_ref, ddc_ref, dar_ref, ddr_ref, dgo_ref):
                r[...] = jnp.zeros_like(r)

        rows = lambda k: slice(k * c, (k + 1) * c)
        states = [st_ref[0, 0, h] for h in range(GDN_HEADS)]
        prim = ([qkv_ref[0, rows(k), :] for k in range(grp)], [z_ref[0, rows(k), :] for k in range(grp)],
                [smc_ref[0, rows(k), :] for k in range(grp)], [smr_ref[k] for k in range(grp)],
                ac_ref[...], dc_ref[...], ar_ref[...], dr_ref[...], go_ref[...], states)
        invs = [inv_ref[0, 0, p] for p in range(grp * GDN_HEADS)]
        _, vjp = jax.vjp(functools.partial(_gdn_group, inverses=invs), *prim)
        cot = ([[dog_ref[0, rows(k), h * hd:(h + 1) * hd] for h in range(GDN_HEADS)] for k in range(grp)],
               [ds_ref[h] for h in range(GDN_HEADS)], [jnp.zeros((c, c), F32)] * (grp * GDN_HEADS))
        dqkv, dz, dsmc, dsmr, dac, ddc, dar, ddr, dgo, dstates = vjp(cot)
        for k in range(grp):
            dqkv_ref[0, rows(k), :] = dqkv[k]
            dz_ref[0, rows(k), :] = dz[k].astype(BF16)
            dsmc_ref[0, rows(k), :] = dsmc[k]
            dsmr_ref[k] = dsmr[k]
        dac_ref[...] += dac
        ddc_ref[...] += ddc
        dar_ref[...] += dar
        ddr_ref[...] += ddr
        dgo_ref[...] += dgo
        for h in range(GDN_HEADS):
            ds_ref[h] = dstates[h]

    tok = lambda b, i: (b, ng - 1 - i, 0)
    fixed = lambda b, i: (0, 0)
    lane_vec = jax.ShapeDtypeStruct((1, LANES), F32)
    row_vec = jax.ShapeDtypeStruct((SM_ROWS, 1), F32)
    return pl.pallas_call(
        body, name="gdn_bwd", grid=(n_batch, ng),
        in_specs=[pl.BlockSpec((1, gc, 3 * GDN_WIDTH), tok), pl.BlockSpec((1, gc, GDN_WIDTH), tok), pl.BlockSpec((1, gc, LANES), tok),
                  pl.BlockSpec((grp, SM_ROWS, c), lambda b, i: (b * ng + ng - 1 - i, 0, 0)),
                  pl.BlockSpec((1, LANES), fixed), pl.BlockSpec((1, LANES), fixed), pl.BlockSpec((SM_ROWS, 1), fixed),
                  pl.BlockSpec((SM_ROWS, 1), fixed), pl.BlockSpec((1, LANES), fixed),
                  pl.BlockSpec((1, 1, GDN_HEADS, hd, hd), lambda b, i: (b, ng - 1 - i, 0, 0, 0)),
                  pl.BlockSpec((1, 1, grp * GDN_HEADS, c, c), lambda b, i: (b, ng - 1 - i, 0, 0, 0)),
                  pl.BlockSpec((1, gc, GDN_WIDTH), lambda b, i: (b, ng - 1 - i, 1))],
        out_specs=[pl.BlockSpec((1, gc, 3 * GDN_WIDTH), tok), pl.BlockSpec((1, gc, GDN_WIDTH), tok), pl.BlockSpec((1, gc, LANES), tok),
                   pl.BlockSpec((grp, SM_ROWS, c), lambda b, i: (b * ng + ng - 1 - i, 0, 0)),
                   pl.BlockSpec((1, LANES), fixed), pl.BlockSpec((1, LANES), fixed), pl.BlockSpec((SM_ROWS, 1), fixed),
                   pl.BlockSpec((SM_ROWS, 1), fixed), pl.BlockSpec((1, LANES), fixed)],
        out_shape=[jax.ShapeDtypeStruct((n_batch, s_len, 3 * GDN_WIDTH), F32), jax.ShapeDtypeStruct((n_batch, s_len, GDN_WIDTH), BF16),
                   jax.ShapeDtypeStruct((n_batch, s_len, LANES), F32), jax.ShapeDtypeStruct((n_batch * n, SM_ROWS, c), F32),
                   lane_vec, lane_vec, row_vec, row_vec, lane_vec],
        scratch_shapes=[pltpu.VMEM((GDN_HEADS, hd, hd), F32)],
        compiler_params=_cparams(("arbitrary", "arbitrary")),
    )(qkvn, z, smc, smr, a_c, dt_c, a_r, dt_r, go, states, inverses, dog)


def _out_proj(x, oa, ob, w_out, g_x, w_cq, tm=256):
    t_len, d = x.shape
    tm = min(tm, t_len)

    def body(x_ref, oa_ref, ob_ref, wo_ref, g_ref, wq_ref, x1_ref, hq_ref, cq_ref):
        x1 = x_ref[...] + _dot(oa_ref[...], wo_ref[0:FOX_WIDTH, :]) + _dot(ob_ref[...], wo_ref[FOX_WIDTH:2 * FOX_WIDTH, :])
        x1_ref[...] = x1
        hq = _rms(x1, g_ref[...]).astype(BF16)
        hq_ref[...] = hq
        cq_ref[...] = _dot(hq, wq_ref[...])

    row = lambda i: (i, 0)
    fixed = lambda i: (0, 0)
    return pl.pallas_call(
        body, name="out_proj", grid=(t_len // tm,),
        in_specs=[pl.BlockSpec((tm, d), row), pl.BlockSpec((tm, FOX_WIDTH), row), pl.BlockSpec((tm, GDN_WIDTH), row),
                  pl.BlockSpec((d, d), fixed), pl.BlockSpec((1, d), fixed), pl.BlockSpec((d, XATTN_WIDTH), fixed)],
        out_specs=[pl.BlockSpec((tm, d), row), pl.BlockSpec((tm, d), row), pl.BlockSpec((tm, XATTN_WIDTH), row)],
        out_shape=[jax.ShapeDtypeStruct((t_len, d), F32), jax.ShapeDtypeStruct((t_len, d), BF16), jax.ShapeDtypeStruct((t_len, XATTN_WIDTH), F32)],
        compiler_params=_cparams(("parallel",)),
    )(x, oa, ob, w_out, g_x, w_cq)


def _out_proj_bwd(dx1, w_out, tm=512):
    t_len, d = dx1.shape
    tm = min(tm, t_len)

    def body(dx_ref, w_ref, o_ref):
        o_ref[...] = _dot(dx_ref[...], w_ref[...], NT)

    return pl.pallas_call(
        body, name="out_proj_bwd", grid=(t_len // tm,),
        in_specs=[pl.BlockSpec((tm, d), lambda i: (i, 0)), pl.BlockSpec((d, d), lambda i: (0, 0))],
        out_specs=pl.BlockSpec((tm, d), lambda i: (i, 0)),
        out_shape=jax.ShapeDtypeStruct((t_len, d), F32),
        compiler_params=_cparams(("parallel",)),
    )(dx1, w_out)


def _mem_kv(mem, g, w_ckv, tm=256):
    t_len, d = mem.shape
    tm = min(tm, t_len)

    def body(x_ref, g_ref, w_ref, h_ref, o_ref):
        h = _rms(x_ref[...], g_ref[...]).astype(BF16)
        h_ref[...] = h
        o_ref[...] = _dot(h, w_ref[...])

    row = lambda i: (i, 0)
    fixed = lambda i: (0, 0)
    return pl.pallas_call(
        body, name="mem_kv", grid=(t_len // tm,),
        in_specs=[pl.BlockSpec((tm, d), row), pl.BlockSpec((1, d), fixed), pl.BlockSpec((d, 2 * XATTN_WIDTH), fixed)],
        out_specs=[pl.BlockSpec((tm, d), row), pl.BlockSpec((tm, 2 * XATTN_WIDTH), row)],
        out_shape=[jax.ShapeDtypeStruct((t_len, d), BF16), jax.ShapeDtypeStruct((t_len, 2 * XATTN_WIDTH), F32)],
        compiler_params=_cparams(("parallel",)),
    )(mem, g, w_ckv)


def _mem_kv_bwd(dckv, mem, g, w_ckv, tm=256):
    t_len, d = mem.shape
    tm = min(tm, t_len)

    def body(d_ref, x_ref, g_ref, w_ref, dg_ref):
        @pl.when(pl.program_id(0) == 0)
        def _():
            dg_ref[...] = jnp.zeros_like(dg_ref)

        dh = _dot(d_ref[...], w_ref[...], NT)
        _, dg = _rms_bwd(x_ref[...], g_ref[...], dh)
        dg_ref[...] += dg

    row = lambda i: (i, 0)
    fixed = lambda i: (0, 0)
    return pl.pallas_call(
        body, name="mem_kv_bwd", grid=(t_len // tm,),
        in_specs=[pl.BlockSpec((tm, 2 * XATTN_WIDTH), row), pl.BlockSpec((tm, d), row), pl.BlockSpec((1, d), fixed),
                  pl.BlockSpec((d, 2 * XATTN_WIDTH), fixed)],
        out_specs=pl.BlockSpec((1, d), fixed),
        out_shape=jax.ShapeDtypeStruct((1, d), F32),
        compiler_params=_cparams(("arbitrary",)),
    )(dckv, mem, g, w_ckv)


def _xattn_probs(qn, kn):
    s = _dot(qn, kn, NT) * (XATTN_HEAD_DIM ** -0.5)
    p = jnp.exp(s - jnp.max(s, axis=-1, keepdims=True))
    return p / jnp.sum(p, axis=-1, keepdims=True)


def _xattn_fwd(cq, ckv, x1, gq, gk, w_co, g_mlp, n_batch, s_len, m_len, tq=512):
    d = x1.shape[1]
    tq = min(tq, s_len)
    nq = s_len // tq
    hd = XATTN_HEAD_DIM

    def body(cq_ref, kv_ref, x1_ref, gq_ref, gk_ref, wo_ref, gm_ref, co_ref, x2_ref, hf_ref):
        outs = []
        for h in range(XATTN_HEADS):
            qn = _rms(cq_ref[:, h * hd:(h + 1) * hd], gq_ref[...])
            kn = _rms(kv_ref[:, h * hd:(h + 1) * hd], gk_ref[...])
            p = _xattn_probs(qn, kn)
            outs.append(_dot(p, kv_ref[:, XATTN_WIDTH + h * hd:XATTN_WIDTH + (h + 1) * hd]).astype(BF16))
        for h in range(XATTN_HEADS):
            co_ref[:, h * hd:(h + 1) * hd] = outs[h]
        x2 = x1_ref[...] + _dot(co_ref[...], wo_ref[...])
        x2_ref[...] = x2
        hf_ref[...] = _rms(x2, gm_ref[...]).astype(BF16)

    row = lambda b, i: (b * nq + i, 0)
    fixed = lambda b, i: (0, 0)
    t_len = n_batch * s_len
    return pl.pallas_call(
        body, name="xattn_fwd", grid=(n_batch, nq),
        in_specs=[pl.BlockSpec((tq, XATTN_WIDTH), row), pl.BlockSpec((m_len, 2 * XATTN_WIDTH), lambda b, i: (b, 0)),
                  pl.BlockSpec((tq, d), row), pl.BlockSpec((1, hd), fixed), pl.BlockSpec((1, hd), fixed),
                  pl.BlockSpec((XATTN_WIDTH, d), fixed), pl.BlockSpec((1, d), fixed)],
        out_specs=[pl.BlockSpec((tq, XATTN_WIDTH), row), pl.BlockSpec((tq, d), row), pl.BlockSpec((tq, d), row)],
        out_shape=[jax.ShapeDtypeStruct((t_len, XATTN_WIDTH), BF16), jax.ShapeDtypeStruct((t_len, d), F32),
                   jax.ShapeDtypeStruct((t_len, d), BF16)],
        compiler_params=_cparams(("parallel", "parallel")),
    )(cq, ckv, x1, gq, gk, w_co, g_mlp)


def _xattn_bwd(dx2, cq, ckv, x1, gq, gk, w_co, g_x, w_cq, n_batch, s_len, m_len, tq=512):
    d = x1.shape[1]
    tq = min(tq, s_len)
    nq = s_len // tq
    hd = XATTN_HEAD_DIM
    scale = XATTN_HEAD_DIM ** -0.5

    def body(dx2_ref, cq_ref, kv_ref, x1_ref, gq_ref, gk_ref, wo_ref, gx_ref, wq_ref,
             dx1_ref, dcq_ref, dkv_ref, dgq_ref, dgk_ref, dgx_ref, dk_acc, dv_acc):
        b = pl.program_id(0)
        i = pl.program_id(1)

        @pl.when((b == 0) & (i == 0))
        def _():
            dgq_ref[...] = jnp.zeros_like(dgq_ref)
            dgk_ref[...] = jnp.zeros_like(dgk_ref)
            dgx_ref[...] = jnp.zeros_like(dgx_ref)

        @pl.when(i == 0)
        def _():
            dk_acc[...] = jnp.zeros_like(dk_acc)
            dv_acc[...] = jnp.zeros_like(dv_acc)

        dx2 = dx2_ref[...]
        dco_all = _dot(dx2, wo_ref[...], NT)
        for h in range(XATTN_HEADS):
            sl = slice(h * hd, (h + 1) * hd)
            q = cq_ref[:, sl]
            qn = _rms(q, gq_ref[...])
            kn = _rms(kv_ref[:, sl], gk_ref[...])
            v = kv_ref[:, XATTN_WIDTH + h * hd:XATTN_WIDTH + (h + 1) * hd]
            p = _xattn_probs(qn, kn)
            dco = dco_all[:, sl]
            dv_acc[:, sl] += _dot(p, dco, TN)
            dp = _dot(dco, v, NT)
            ds = p * (dp - jnp.sum(dp * p, axis=-1, keepdims=True))
            dqn = _dot(ds, kn) * scale
            dk_acc[:, sl] += _dot(ds, qn, TN) * scale
            dq, dgq = _rms_bwd(q, gq_ref[...], dqn)
            dgq_ref[...] += dgq
            dcq_ref[:, sl] = dq.astype(BF16)
        dhq = _dot(dcq_ref[...], wq_ref[...], NT)
        dxn, dgx = _rms_bwd(x1_ref[...], gx_ref[...], dhq)
        dgx_ref[...] += dgx
        dx1_ref[...] = dx2 + dxn

        @pl.when(i == nq - 1)
        def _():
            for h in range(XATTN_HEADS):
                sl = slice(h * hd, (h + 1) * hd)
                dk, dgk = _rms_bwd(kv_ref[:, sl], gk_ref[...], dk_acc[:, sl])
                dgk_ref[...] += dgk
                dkv_ref[:, sl] = dk.astype(BF16)
                dkv_ref[:, XATTN_WIDTH + h * hd:XATTN_WIDTH + (h + 1) * hd] = dv_acc[:, sl].astype(BF16)

    row = lambda b, i: (b * nq + i, 0)
    fixed = lambda b, i: (0, 0)
    t_len = n_batch * s_len
    return pl.pallas_call(
        body, name="xattn_bwd", grid=(n_batch, nq),
        in_specs=[pl.BlockSpec((tq, d), row), pl.BlockSpec((tq, XATTN_WIDTH), row), pl.BlockSpec((m_len, 2 * XATTN_WIDTH), lambda b, i: (b, 0)),
                  pl.BlockSpec((tq, d), row), pl.BlockSpec((1, hd), fixed), pl.BlockSpec((1, hd), fixed),
                  pl.BlockSpec((XATTN_WIDTH, d), fixed), pl.BlockSpec((1, d), fixed), pl.BlockSpec((d, XATTN_WIDTH), fixed)],
        out_specs=[pl.BlockSpec((tq, d), row), pl.BlockSpec((tq, XATTN_WIDTH), row), pl.BlockSpec((m_len, 2 * XATTN_WIDTH), lambda b, i: (b, 0)),
                   pl.BlockSpec((1, hd), fixed), pl.BlockSpec((1, hd), fixed), pl.BlockSpec((1, d), fixed)],
        out_shape=[jax.ShapeDtypeStruct((t_len, d), F32), jax.ShapeDtypeStruct((t_len, XATTN_WIDTH), BF16),
                   jax.ShapeDtypeStruct((n_batch * m_len, 2 * XATTN_WIDTH), BF16),
                   jax.ShapeDtypeStruct((1, hd), F32), jax.ShapeDtypeStruct((1, hd), F32), jax.ShapeDtypeStruct((1, d), F32)],
        scratch_shapes=[pltpu.VMEM((m_len, XATTN_WIDTH), F32), pltpu.VMEM((m_len, XATTN_WIDTH), F32)],
        compiler_params=_cparams(("arbitrary", "arbitrary")),
    )(dx2, cq, ckv, x1, gq, gk, w_co, g_x, w_cq)


def _resident(shape):
    return pl.BlockSpec(shape, lambda *_: (0,) * len(shape), pipeline_mode=pl.Buffered(1))


def _mlp_fwd(hf, x2, target, w1, w2, tm=256, tf=1024):
    t_len, d = x2.shape
    f = w1.shape[1]
    tm, tf = min(tm, t_len), min(tf, f)

    def body(hf_ref, x2_ref, tg_ref, w1_ref, w2_ref, u_ref, a_ref, dy_ref, ls_ref):
        hf_t = hf_ref[...]
        for k in range(f // tf):
            cols = slice(k * tf, (k + 1) * tf)
            u = _dot(hf_t, w1_ref[:, cols])
            u_ref[:, cols] = u
            r = jnp.maximum(u, 0.0)
            a_ref[:, cols] = (r * r).astype(BF16)
        y = x2_ref[...] + _dot(a_ref[...], w2_ref[...])
        err = y - tg_ref[...]
        dy_ref[...] = err * (1.0 / d)
        ls_ref[...] = jnp.broadcast_to(jnp.sum(jnp.sum(err * err, axis=-1, keepdims=True) * (1.0 / d), axis=0, keepdims=True), ls_ref.shape)

    row = lambda i: (i, 0)
    return pl.pallas_call(
        body, name="mlp_fwd", grid=(t_len // tm,),
        in_specs=[pl.BlockSpec((tm, d), row), pl.BlockSpec((tm, d), row), pl.BlockSpec((tm, d), row), _resident((d, f)), _resident((f, d))],
        out_specs=[pl.BlockSpec((tm, f), row), pl.BlockSpec((tm, f), row), pl.BlockSpec((tm, d), row),
                   pl.BlockSpec((1, 8, LANES), lambda i: (i, 0, 0))],
        out_shape=[jax.ShapeDtypeStruct((t_len, f), F32), jax.ShapeDtypeStruct((t_len, f), BF16), jax.ShapeDtypeStruct((t_len, d), F32),
                   jax.ShapeDtypeStruct((t_len // tm, 8, LANES), F32)],
        compiler_params=_cparams(("parallel",)),
    )(hf, x2, target, w1, w2)


def _mlp_bwd(dy, u, x2, g, w1, w2, tm=256, tf=1024):
    t_len, d = x2.shape
    f = w1.shape[1]
    tm, tf = min(tm, t_len), min(tf, f)

    def body(dy_ref, u_ref, x2_ref, g_ref, w1_ref, w2_ref, du_ref, dx2_ref, dg_ref):
        @pl.when(pl.program_id(0) == 0)
        def _():
            dg_ref[...] = jnp.zeros_like(dg_ref)

        dy_t = dy_ref[...]
        dyb = dy_t.astype(BF16)
        for k in range(f // tf):
            cols = slice(k * tf, (k + 1) * tf)
            da = _dot(dyb, w2_ref[cols, :], NT)
            du_ref[:, cols] = (da * (2.0 * jnp.maximum(u_ref[:, cols], 0.0))).astype(BF16)
        dhf = _dot(du_ref[...], w1_ref[...], NT)
        dxn, dg = _rms_bwd(x2_ref[...], g_ref[...], dhf)
        dx2_ref[...] = dy_t + dxn
        dg_ref[...] += dg

    row = lambda i: (i, 0)
    fixed = lambda i: (0, 0)
    return pl.pallas_call(
        body, name="mlp_bwd", grid=(t_len // tm,),
        in_specs=[pl.BlockSpec((tm, d), row), pl.BlockSpec((tm, f), row), pl.BlockSpec((tm, d), row), pl.BlockSpec((1, d), fixed),
                  _resident((d, f)), _resident((f, d))],
        out_specs=[pl.BlockSpec((tm, f), row), pl.BlockSpec((tm, d), row), pl.BlockSpec((1, d), fixed)],
        out_shape=[jax.ShapeDtypeStruct((t_len, f), BF16), jax.ShapeDtypeStruct((t_len, d), F32), jax.ShapeDtypeStruct((1, d), F32)],
        compiler_params=_cparams(("arbitrary",)),
    )(dy, u, x2, g, w1, w2)


def _pad_lanes(v, offset=0, width=LANES):
    return jnp.zeros((1, width), F32).at[:, offset:offset + v.shape[1]].set(v)


def _col(v, offset=0, rows=SM_ROWS):
    return jnp.zeros((rows, 1), F32).at[offset:offset + v.shape[1], 0].set(v[0])


def _pack_small(g_mix, dgq, dgk, dbias, dgo, dac, dar, ddc, ddr, g_gdn_o, g_nx, g_mem, g_xq, g_xk, g_mlp, loss_tiles):
    def body(mix_ref, q_ref, k_ref, b_ref, o_ref, ac_ref, ar_ref, dc_ref, dr_ref, go_ref, nx_ref, mem_ref, xq_ref, xk_ref,
             mlp_ref, lt_ref, out_ref):
        lane = lax.broadcasted_iota(jnp.int32, (1, LANES), 1)
        diag = lax.broadcasted_iota(jnp.int32, (SM_ROWS, LANES), 0) == lax.broadcasted_iota(jnp.int32, (SM_ROWS, LANES), 1)

        def rolled(v, shift):
            return pltpu.roll(jnp.broadcast_to(v, (8, LANES)), shift, 1)[0:1, :]

        def rows_to_lanes(col):
            return jnp.sum(jnp.where(diag, col, 0.0), axis=0, keepdims=True)

        def put(row, v, n):
            out_ref[row:row + 1, 0:LANES] = jnp.where(lane < n, v, 0.0)

        out_ref[...] = jnp.zeros_like(out_ref)
        out_ref[0:1, :] = mix_ref[...]
        for row, ref in ((1, q_ref), (2, k_ref), (4, o_ref)):
            put(row, ref[...] + rolled(ref[...], FOX_HEAD_DIM), FOX_HEAD_DIM)
        put(3, rows_to_lanes(b_ref[...]), FOX_HEADS)
        for row, lane_ref, row_ref in ((5, ac_ref, ar_ref), (6, dc_ref, dr_ref)):
            put(row, rolled(lane_ref[...] + rows_to_lanes(row_ref[...]), LANES - SM_A), GDN_HEADS)
        put(7, go_ref[...], LANES)
        out_ref[8:9, :] = nx_ref[...]
        out_ref[9:10, :] = mem_ref[...]
        put(10, xq_ref[...], LANES)
        put(11, xk_ref[...], LANES)
        out_ref[12:13, :] = mlp_ref[...]
        put(LOSS_ROW, 0.5 * jnp.sum(lt_ref[...], axis=0)[0:1, :], 1)

    args = (g_mix, dgq, dgk, dbias, dgo, dac, dar, ddc, ddr, g_gdn_o, g_nx, g_mem, g_xq, g_xk, g_mlp, loss_tiles)
    return pl.pallas_call(body, name="pack_small", out_shape=jax.ShapeDtypeStruct((PACK_ROWS, D_MODEL), F32))(*args)


LATE_WEIGHTS = (("w_out", "w_cq", "w_ckv", "w_co"), ("w_mlp1", "w_mlp2"))
GRAD_GROUPS = (("w_mlp2", "w_mlp1"), ("w_co", "w_cq", "w_ckv", "w_out"), ("w_in", "gdn_conv_w"))


def _local_step(x, mem, target, norm_mix_g, w_in, fox_qnorm_g, fox_knorm_g, fox_f_bias, fox_onorm_g, gdn_conv_w, gdn_A_log,
                gdn_dt_bias, gdn_onorm_g, norm_xattn_g, mem_norm_g, xattn_qnorm_g, xattn_knorm_g, norm_mlp_g,
                late_weights, grads_ready=None, first_token=0.0):
    if grads_ready is None:
        grads_ready = lambda group: 0.0
    n_batch, s_len, d = x.shape
    m_len = mem.shape[1]
    t_len = n_batch * s_len
    tq = min(FOX_BLOCK, s_len)
    nq = s_len // tq
    n_chunks = s_len // GDN_CHUNK
    x2d = x.reshape(t_len, d)

    wp = jnp.concatenate([w_in[0:1536], w_in[1544:3080], w_in[3088:3600], w_in[1536:1544], w_in[3080:3088],
                          jnp.zeros((P_DIM - 3600, d), BF16)], axis=0)
    wst = jnp.concatenate([w_in[1536:1544], w_in[3080:3088]], axis=0)
    conv_w = jnp.concatenate([gdn_conv_w, jnp.zeros((8 - CONV_WIDTH, gdn_conv_w.shape[1]), F32)], axis=0)
    bias_col = _col(fox_f_bias, SM_F)
    gq2, gk2, go2 = (jnp.tile(g, (1, 2)) for g in (fox_qnorm_g, fox_knorm_g, fox_onorm_g))
    a_c, dt_c = _pad_lanes(gdn_A_log, SM_A), _pad_lanes(gdn_dt_bias, SM_A)
    a_r, dt_r = _col(gdn_A_log, SM_A), _col(gdn_dt_bias, SM_A)

    h1, pfox, pgdn, pz, sm, smt = _in_proj(x2d, norm_mix_g + first_token, wp, wst)
    c_rows = _fox_cum(smt, bias_col, n_batch, s_len)
    cb = c_rows.reshape(SM_ROWS, n_batch, nq, tq).transpose(1, 2, 0, 3)
    pf3 = pfox.reshape(n_batch, s_len, 1536)
    o_fox, oa, lse = _fox_fwd(pf3, cb, gq2, gk2, go2, tq)
    pg3 = pgdn.reshape(n_batch, s_len, 1536)
    qkvn = _gdn_pre(pg3, conv_w)
    z3 = pz.reshape(n_batch, s_len, GDN_WIDTH)
    smc = sm.reshape(n_batch, s_len, LANES)
    smr = smt.reshape(SM_ROWS, n_batch * n_chunks, GDN_CHUNK).transpose(1, 0, 2)
    ob, states, inverses = _gdn_fwd(qkvn, z3, smc, smr, a_c, dt_c, a_r, dt_r, gdn_onorm_g)
    oa2, ob2 = oa.reshape(t_len, FOX_WIDTH), ob.reshape(t_len, GDN_WIDTH)
    w_out, w_cq, w_ckv, w_co = late_weights(LATE_WEIGHTS[0], ob2)
    x1, hq, cq = _out_proj(x2d, oa2, ob2, w_out, norm_xattn_g, w_cq)
    mem2d = mem.reshape(n_batch * m_len, d)
    hm, ckv = _mem_kv(mem2d, mem_norm_g, w_ckv)
    co, x2, hf = _xattn_fwd(cq, ckv, x1, xattn_qnorm_g, xattn_knorm_g, w_co, norm_mlp_g, n_batch, s_len, m_len)
    w_mlp1, w_mlp2 = late_weights(LATE_WEIGHTS[1], hf)
    u, a_act, dy, loss_tiles = _mlp_fwd(hf, x2, target.reshape(t_len, d), w_mlp1, w_mlp2)

    grads = {}
    du, dx2, grads["norm_mlp_g"] = _mlp_bwd(dy, u, x2, norm_mlp_g, w_mlp1, w_mlp2)
    grads["w_mlp2"] = _wgrad(a_act, dy, "wgrad_mlp2", bt=2048)
    grads["w_mlp1"] = _wgrad(hf, du, "wgrad_mlp1", bt=2048, column_blocks=D_FF // N_DEV)
    token = grads_ready({k: grads[k] for k in GRAD_GROUPS[0]})
    grads["w_co"] = _wgrad(co, dx2, "wgrad_co", column_blocks=D_MODEL // N_DEV)
    dx1, dcq, dckv, grads["xattn_qnorm_g"], grads["xattn_knorm_g"], grads["norm_xattn_g"] = _xattn_bwd(
        dx2, cq, ckv, x1, xattn_qnorm_g + token, xattn_knorm_g, w_co, norm_xattn_g, w_cq, n_batch, s_len, m_len)
    grads["w_cq"] = _wgrad(hq, dcq, "wgrad_cq")
    grads["w_ckv"] = _wgrad(hm, dckv, "wgrad_ckv")
    grads["mem_norm_g"] = _mem_kv_bwd(dckv, mem2d, mem_norm_g, w_ckv)
    grads["w_out"] = _wgrad_stacked([oa2, ob2], dx1, "wgrad_out", bn=1024)
    token = grads_ready({k: grads[k] for k in GRAD_GROUPS[1]})
    dcat = _out_proj_bwd(dx1, w_out)
    dcat3 = dcat.reshape(n_batch, s_len, d)

    dqkvn, dz, dsmc, dsmr, dac, ddc, dar, ddr, grads["gdn_onorm_g"] = _gdn_bwd(
        qkvn, z3, smc, smr, a_c, dt_c, a_r, dt_r, gdn_onorm_g + token, states, inverses, dcat3)
    dpg, dconv = _gdn_pre_bwd(pg3, conv_w, dqkvn)
    grads["gdn_conv_w"] = dconv[0:CONV_WIDTH]

    dq, dk, dv, dcb, dgq, dgk, dgo = _fox_bwd(pf3, cb, gq2, gk2, go2, o_fox, lse, dcat3, tq)
    dc8 = dcb[:, :, :, 0:2, :].transpose(1, 3, 0, 2, 4).reshape(FOX_HEADS, t_len)
    dc_rows = jnp.concatenate([dc8, jnp.zeros((SM_ROWS - FOX_HEADS, t_len), F32)], axis=0)
    dl_rows, dbias = _fox_cum_bwd(dc_rows, smt, bias_col, n_batch, s_len)
    dsm_rows = jnp.concatenate([dl_rows[0:SM_B], dsmr.transpose(1, 0, 2).reshape(SM_ROWS, t_len)[SM_B:SM_ROWS]], axis=0)

    dprojs = [dq.reshape(t_len, FOX_WIDTH), dk.reshape(t_len, FOX_WIDTH), dv.reshape(t_len, FOX_WIDTH),
              dpg.reshape(t_len, 1536), dz.reshape(t_len, GDN_WIDTH), dsmc.reshape(t_len, LANES)]
    dwp = _wgrad_stacked(dprojs, h1, "wgrad_in")
    dwst = _rows_matmul(dsm_rows, h1, "wgrad_in_rows")
    dw_small = dwp[P_SMALL:P_SMALL + SM_ROWS] + dwst
    grads["w_in"] = jnp.concatenate([dwp[0:1536], dw_small[0:8], dwp[1536:3072], dw_small[8:16], dwp[3072:3584]], axis=0)
    token = grads_ready({k: grads[k] for k in GRAD_GROUPS[2]})
    grad_x, grads["norm_mix_g"] = _in_proj_bwd(dprojs, dsm_rows, x2d, norm_mix_g + token, wp, wst, dx1)
    packed = _pack_small(grads["norm_mix_g"], dgq, dgk, dbias, dgo, dac, dar, ddc, ddr, grads["gdn_onorm_g"], grads["norm_xattn_g"],
                         grads["mem_norm_g"], grads["xattn_qnorm_g"], grads["xattn_knorm_g"], grads["norm_mlp_g"], loss_tiles)
    return packed, grad_x.reshape(n_batch, s_len, d), {k: grads[k] for k in SHARDED}


MESH_ID = pl.DeviceIdType.MESH
ANY_SPEC = pl.BlockSpec(memory_space=pl.ANY)


def _place():
    x, y, c = lax.axis_index("x"), lax.axis_index("y"), lax.axis_index("c")
    return x, y, c, [(1 - x, y), (x, 1 - y), (1 - x, 1 - y)]


def _place_own(src_ref, dst_ref):
    def staged(buf, sem):
        for a, b in ((src_ref, buf), (buf, dst_ref)):
            cp = pltpu.make_async_copy(a, b, sem)
            cp.start()
            cp.wait()

    pl.run_scoped(staged, pltpu.VMEM(src_ref.shape, src_ref.dtype), pltpu.SemaphoreType.DMA)


def _all_gather_body(n, ins, outs, send_sems, recv_sems):
    x, y, c, chips = _place()
    me, sibling = (x, y, c), (x, y, 1 - c)

    def copy(a, k, block, to, src=None):
        dst = outs[a].at[4 * block[0] + 2 * block[1] + block[2]]
        return pltpu.make_async_remote_copy(src_ref=dst if src is None else src, dst_ref=dst, send_sem=send_sems.at[a, k],
                                            recv_sem=recv_sems.at[a, k], device_id=to, device_id_type=MESH_ID)

    first = []
    for a in range(n):
        first.append(copy(a, 0, me, sibling, src=ins[a]))
        first += [copy(a, 1 + j, me, (*chip, c), src=ins[a]) for j, chip in enumerate(chips)]
    for cp in first:
        cp.start()
    for a in range(n):
        _place_own(ins[a], outs[a].at[4 * x + 2 * y + c])
    passed = []
    for j, chip in enumerate(chips):
        for a in range(n):
            copy(a, 1 + j, (*chip, c), me).wait_recv()
            fwd = copy(a, 4 + j, (*chip, c), sibling)
            fwd.start()
            passed.append(fwd)
    for a in range(n):
        copy(a, 0, sibling, me).wait_recv()
        for j, chip in enumerate(chips):
            copy(a, 4 + j, (*chip, 1 - c), me).wait_recv()
    for cp in first + passed:
        cp.wait_send()


def _all_gather_hbm(arrs, name):
    n = len(arrs)

    def body(*refs):
        _all_gather_body(n, refs[:n], refs[n:2 * n], refs[2 * n], refs[2 * n + 1])

    return pl.pallas_call(
        body, name=name, in_specs=[ANY_SPEC] * n, out_specs=[ANY_SPEC] * n,
        out_shape=[jax.ShapeDtypeStruct((N_DEV,) + a.shape, a.dtype) for a in arrs],
        scratch_shapes=[pltpu.SemaphoreType.DMA((n, 7)), pltpu.SemaphoreType.DMA((n, 7))],
        compiler_params=pltpu.CompilerParams(vmem_limit_bytes=VMEM_LIMIT),
    )(*arrs)


def _pair_exchange(arrs, name):
    n = len(arrs)

    def body(*refs):
        ins, outs = refs[:n], refs[n:2 * n]
        send_sems, recv_sems = refs[2 * n:]
        x, y, c, _ = _place()
        copies = []
        for a in range(n):
            for chip in range(4):
                copies.append(pltpu.make_async_remote_copy(
                    src_ref=ins[a].at[2 * chip + (1 - c)], dst_ref=outs[a].at[chip], send_sem=send_sems.at[a, chip],
                    recv_sem=recv_sems.at[a, chip], device_id=(x, y, 1 - c), device_id_type=MESH_ID))
        for cp in copies:
            cp.start()
        for cp in copies:
            cp.wait()

    return pl.pallas_call(
        body, name=name, in_specs=[ANY_SPEC] * n, out_specs=[ANY_SPEC] * n,
        out_shape=[jax.ShapeDtypeStruct((4,) + a.shape[1:], a.dtype) for a in arrs],
        scratch_shapes=[pltpu.SemaphoreType.DMA((n, 4)), pltpu.SemaphoreType.DMA((n, 4))],
    )(*arrs)


HBM_SPEC = pl.BlockSpec(memory_space=pltpu.HBM)
SEM_SPEC = pl.BlockSpec(memory_space=pltpu.SEMAPHORE)
DATAFLOW = pltpu.SideEffectType.DATAFLOW_SIDE_EFFECTING


def _in_hbm(arrs):
    return [pltpu.with_memory_space_constraint(a, pltpu.HBM) for a in arrs]


def _copies_start(name, srcs, lands, make_copies, after):
    n = len(srcs)
    n_copies = len(make_copies(srcs, lands, None, None)[0])

    def body(*refs):
        send_sems, recv_sems = refs[2 * n + 1], refs[2 * n + 2]
        for row in make_copies(refs[:n], refs[n:2 * n], send_sems, recv_sems):
            for cp in row:
                cp.start()
        refs[-1][...] = jnp.zeros_like(refs[-1])

    sems = pltpu.SemaphoreType.DMA((n * n_copies,))
    thru = [pltpu.HBM(a.shape, a.dtype) for a in list(srcs) + list(lands)]
    res = pl.pallas_call(
        body, name=name, in_specs=[HBM_SPEC] * (2 * n) + [ANY_SPEC],
        out_specs=(SEM_SPEC, SEM_SPEC, *[HBM_SPEC] * (2 * n), pl.BlockSpec(memory_space=pltpu.VMEM)),
        out_shape=(sems, sems, *thru, jax.ShapeDtypeStruct((8, LANES), F32)),
        input_output_aliases={i: 2 + i for i in range(2 * n)},
        compiler_params=pltpu.CompilerParams(has_side_effects=DATAFLOW),
    )(*_in_hbm(list(srcs) + list(lands)), after)
    return res[0], res[1], list(res[2:2 + n]), list(res[2 + n:2 + 2 * n]), res[-1]


def _copies_wait(name, send_sems, recv_sems, srcs, lands, after, make_copies, own_block=False):
    n = len(srcs)

    def body(*refs):
        if own_block:
            for a in range(n):
                _place_own(refs[a], _own_part(refs[a], refs[3 * n + 3 + a]))
        for row in make_copies(refs[:n], refs[n:2 * n], refs[2 * n], refs[2 * n + 1]):
            for cp in row:
                cp.wait_send()
                cp.wait_recv()

    res = pl.pallas_call(
        body, name=name, in_specs=[HBM_SPEC] * (2 * n) + [SEM_SPEC, SEM_SPEC, ANY_SPEC],
        out_specs=tuple([HBM_SPEC] * (2 * n)),
        out_shape=tuple(pltpu.HBM(a.shape, a.dtype) for a in list(srcs) + list(lands)),
        input_output_aliases={i: i for i in range(2 * n)},
        compiler_params=pltpu.CompilerParams(has_side_effects=DATAFLOW, vmem_limit_bytes=VMEM_LIMIT),
    )(*srcs, *lands, send_sems, recv_sems, after)
    return list(res[:n]), list(res[n:])


def _own_part(src_ref, land_ref):
    me = 4 * lax.axis_index("x") + 2 * lax.axis_index("y") + lax.axis_index("c")
    rows, cols = src_ref.shape
    if land_ref.shape[0] == N_DEV * rows:
        return land_ref.at[pl.ds(pl.multiple_of(me * rows, rows), rows), :]
    return land_ref.at[:, pl.ds(pl.multiple_of(me * cols, cols), cols)]


def _gather_copies(srcs, lands, send_sems, recv_sems):
    if send_sems is None:
        return [[None] * 7]
    x, y, c, _ = _place()
    rows = []
    for a in range(len(srcs)):
        row = []
        for k in range(7):
            r = k + 1
            to = (1 - x if r & 4 else x, 1 - y if r & 2 else y, 1 - c if r & 1 else c)
            row.append(pltpu.make_async_remote_copy(
                src_ref=srcs[a], dst_ref=_own_part(srcs[a], lands[a]), send_sem=send_sems.at[7 * a + k], recv_sem=recv_sems.at[7 * a + k],
                device_id=to, device_id_type=MESH_ID))
        rows.append(row)
    return rows


def _scatter_copies(srcs, lands, send_sems, recv_sems):
    if send_sems is None:
        return [[None] * 7]
    x, y, c, _ = _place()
    rows = []
    for a in range(len(srcs)):
        row = []
        for k in range(7):
            r = k + 1
            to = (1 - x if r & 4 else x, 1 - y if r & 2 else y, 1 - c if r & 1 else c)
            row.append(pltpu.make_async_remote_copy(
                src_ref=srcs[a].at[4 * to[0] + 2 * to[1] + to[2]], dst_ref=lands[a].at[k], send_sem=send_sems.at[7 * a + k],
                recv_sem=recv_sems.at[7 * a + k], device_id=to, device_id_type=MESH_ID))
        rows.append(row)
    return rows


def _chip_copies(srcs, lands, send_sems, recv_sems):
    if send_sems is None:
        return [[None] * 3]
    x, y, c, chips = _place()
    return [[pltpu.make_async_remote_copy(
        src_ref=srcs[a].at[2 * chip[0] + chip[1]], dst_ref=lands[a].at[j], send_sem=send_sems.at[3 * a + j], recv_sem=recv_sems.at[3 * a + j],
        device_id=(*chip, c), device_id_type=MESH_ID) for j, chip in enumerate(chips)] for a in range(len(srcs))]


def _tile(rows, cols):
    if rows <= 256:
        return rows, cols
    tr = 256 if cols <= 512 else 128
    if rows % tr == 0:
        return tr, cols
    return rows, 256


def _pair_sum(core, own, got, name):
    _, rows, cols = own.shape
    tr, tc = _tile(rows, cols)

    def body(c_ref, own_ref, got_ref, o_ref):
        o_ref[0] = own_ref[0] + got_ref[0]

    return pl.pallas_call(
        body, name=name,
        grid_spec=pltpu.PrefetchScalarGridSpec(
            num_scalar_prefetch=1, grid=(4, rows // tr, cols // tc),
            in_specs=[pl.BlockSpec((1, tr, tc), lambda k, i, j, c: (2 * k + c[0], i, j)),
                      pl.BlockSpec((1, tr, tc), lambda k, i, j, c: (k, i, j))],
            out_specs=pl.BlockSpec((1, tr, tc), lambda k, i, j, c: (k, i, j))),
        out_shape=jax.ShapeDtypeStruct((4, rows, cols), F32),
        compiler_params=_cparams(("parallel", "parallel", "parallel")),
    )(core, own, got)


def _adamw(w, g, m, v):
    m_new = ADAM_B1 * m + (1.0 - ADAM_B1) * g
    v_new = ADAM_B2 * v + (1.0 - ADAM_B2) * (g * g)
    m_hat = m_new / (1.0 - ADAM_B1 ** ADAM_STEP)
    v_hat = v_new / (1.0 - ADAM_B2 ** ADAM_STEP)
    delta = -ADAM_LR * (m_hat / (jnp.sqrt(v_hat) + ADAM_EPS) + ADAM_WD * w)
    return delta, m_new, v_new


def _sum_adam(chip, sums, parts, w, m, v, name):
    n_parts, rows, cols = parts.shape
    tr, tc = _tile(rows, cols)

    def body(chip_ref, own_ref, p_ref, w_ref, m_ref, v_ref, g_ref, d_ref, mo_ref, vo_ref):
        g = own_ref[0]
        for k in range(n_parts):
            g = g + p_ref[k]
        g_ref[...] = g
        d_ref[...], mo_ref[...], vo_ref[...] = _adamw(w_ref[...], g, m_ref[...], v_ref[...])

    tile = pl.BlockSpec((tr, tc), lambda i, j, ch: (i, j))
    out = jax.ShapeDtypeStruct((rows, cols), F32)
    return pl.pallas_call(
        body, name=name,
        grid_spec=pltpu.PrefetchScalarGridSpec(
            num_scalar_prefetch=1, grid=(rows // tr, cols // tc),
            in_specs=[pl.BlockSpec((1, tr, tc), lambda i, j, ch: (ch[0], i, j)),
                      pl.BlockSpec((n_parts, tr, tc), lambda i, j, ch: (0, i, j)), tile, tile, tile],
            out_specs=[tile, tile, tile, tile]),
        out_shape=[out, out, out, out],
        compiler_params=_cparams(("parallel", "parallel")),
    )(chip, sums, parts, w, m, v)


SHARDED = ("w_in", "gdn_conv_w", "w_out", "w_cq", "w_ckv", "w_co", "w_mlp1", "w_mlp2")
TRANSPOSED = ("w_in",)
COLUMN_SHARDED = ("gdn_conv_w", "w_co", "w_mlp1")
REPLICATED = ("norm_mix_g", "fox_qnorm_g", "fox_knorm_g", "fox_f_bias", "fox_onorm_g", "gdn_A_log", "gdn_dt_bias", "gdn_onorm_g",
              "norm_xattn_g", "mem_norm_g", "xattn_qnorm_g", "xattn_knorm_g", "norm_mlp_g")
WEIGHTS = ("norm_mix_g", "w_in", "fox_qnorm_g", "fox_knorm_g", "fox_f_bias", "fox_onorm_g", "gdn_conv_w", "gdn_A_log", "gdn_dt_bias",
           "gdn_onorm_g", "w_out", "norm_xattn_g", "mem_norm_g", "w_cq", "w_ckv", "xattn_qnorm_g", "xattn_knorm_g", "w_co",
           "norm_mlp_g", "w_mlp1", "w_mlp2")
PACK_ROWS = 16
LOSS_ROW = len(REPLICATED)


def _whole(name, gathered):
    if name in COLUMN_SHARDED:
        return gathered.transpose(1, 0, 2).reshape(gathered.shape[1], N_DEV * gathered.shape[2])
    return gathered.reshape(N_DEV * gathered.shape[1], gathered.shape[2])


def _whole_shape(name, shard_shape):
    rows, cols = shard_shape
    return (rows, N_DEV * cols) if name in COLUMN_SHARDED else (N_DEV * rows, cols)


def _blocks(name, whole):
    if whole.ndim == 3:
        return whole
    if name in COLUMN_SHARDED:
        rows, cols = whole.shape
        return whole.reshape(rows, N_DEV, cols // N_DEV).transpose(1, 0, 2)
    return whole.reshape(N_DEV, whole.shape[0] // N_DEV, whole.shape[1])


def _adam_small(everyone, ws, ms, vs):
    n_par = len(ws)

    def body(*refs):
        ev_ref = refs[0]
        w_refs, m_refs, v_refs = (refs[1 + j * n_par:1 + (j + 1) * n_par] for j in range(3))
        outs = refs[1 + 3 * n_par:-1]
        sum_ref = refs[-1]
        total = ev_ref[0]
        for dev in range(1, N_DEV):
            total = total + ev_ref[dev]
        sum_ref[...] = total
        for i in range(n_par):
            n = w_refs[i].shape[1]
            g = sum_ref[i:i + 1, 0:n]
            outs[4 * i][...] = g
            outs[4 * i + 1][...], outs[4 * i + 2][...], outs[4 * i + 3][...] = _adamw(w_refs[i][...], g, m_refs[i][...], v_refs[i][...])
        outs[4 * n_par][...] = sum_ref[LOSS_ROW:LOSS_ROW + 1, 0:1]

    shapes = [jax.ShapeDtypeStruct(a.shape, F32) for a in ws for _ in range(4)] + [jax.ShapeDtypeStruct((1, 1), F32)]
    return pl.pallas_call(body, name="adam_small", out_shape=shapes,
                          scratch_shapes=[pltpu.VMEM((PACK_ROWS, D_MODEL), F32)])(everyone, *ws, *ms, *vs)


def kernel(x, mem, norm_mix_g, w_in, fox_qnorm_g, fox_knorm_g, fox_f_bias, fox_onorm_g, gdn_conv_w, gdn_A_log, gdn_dt_bias, gdn_onorm_g, w_out, norm_xattn_g, mem_norm_g, w_cq, w_ckv, xattn_qnorm_g, xattn_knorm_g, w_co, norm_mlp_g, w_mlp1, w_mlp2, loss_target, m_norm_mix_g, m_w_in, m_fox_qnorm_g, m_fox_knorm_g, m_fox_f_bias, m_fox_onorm_g, m_gdn_conv_w, m_gdn_A_log, m_gdn_dt_bias, m_gdn_onorm_g, m_w_out, m_norm_xattn_g, m_mem_norm_g, m_w_cq, m_w_ckv, m_xattn_qnorm_g, m_xattn_knorm_g, m_w_co, m_norm_mlp_g, m_w_mlp1, m_w_mlp2, v_norm_mix_g, v_w_in, v_fox_qnorm_g, v_fox_knorm_g, v_fox_f_bias, v_fox_onorm_g, v_gdn_conv_w, v_gdn_A_log, v_gdn_dt_bias, v_gdn_onorm_g, v_w_out, v_norm_xattn_g, v_mem_norm_g, v_w_cq, v_w_ckv, v_xattn_qnorm_g, v_xattn_knorm_g, v_w_co, v_norm_mlp_g, v_w_mlp1, v_w_mlp2):
    given = dict(locals())
    w = {k: given[k] for k in WEIGHTS}
    m = {k: given["m_" + k] for k in WEIGHTS}
    v = {k: given["v_" + k] for k in WEIGHTS}

    core = lax.axis_index("c").astype(jnp.int32).reshape(1)
    chip = (2 * lax.axis_index("x") + lax.axis_index("y")).astype(jnp.int32).reshape(1)
    me = 4 * lax.axis_index("x") + 2 * lax.axis_index("y") + lax.axis_index("c")

    local = lambda d: {k: jnp.transpose(d[k][0]) if k in TRANSPOSED else d[k][0] for k in SHARDED}
    w2, m2, v2 = local(w), local(m), local(v)
    shards = {k: w2[k] if k == "gdn_conv_w" else w2[k].astype(BF16) for k in SHARDED}
    early = [k for k in SHARDED if not any(k in group for group in LATE_WEIGHTS)]
    gathered = _all_gather_hbm([shards[k] for k in early], "gather_early")
    whole = {k: _whole(k, g) for k, g in zip(early, gathered)}
    gathers, after = {}, gathered[0]
    for i, group in enumerate(LATE_WEIGHTS):
        lands = [lax.empty(_whole_shape(k, shards[k].shape), BF16) for k in group]
        gathers[group] = _copies_start("gather_late_start_" + str(i), [shards[k] for k in group], lands, _gather_copies, after=after)
        after = gathers[group][4]
    first_token = after[0, 0]

    def late_weights(group, after):
        gather = gathers[group]
        _, lands = _copies_wait("gather_late_wait_" + str(LATE_WEIGHTS.index(group)), gather[0], gather[1], gather[2], gather[3],
                                after, _gather_copies, own_block=True)
        return lands

    pending = []

    def grads_ready(group):
        names = list(group)
        tag = str(len(pending))
        own = [_blocks(k, group[k]) for k in names]
        if "w_in" in names:
            got = _pair_exchange(own, "grad_pair_exchange_" + tag)
            srcs = [_pair_sum(core, o, g, "grad_pair_sum_" + k) for k, o, g in zip(names, own, got)]
            copies, index, n_parts = _chip_copies, chip, 3
        else:
            srcs, copies, index, n_parts = own, _scatter_copies, me.astype(jnp.int32).reshape(1), 7
        lands = [lax.empty((n_parts,) + s.shape[1:], s.dtype) for s in srcs]
        started = _copies_start("grad_exchange_start_" + tag, srcs, lands, copies, after=core)
        pending.append((names, started, copies, index))
        return started[4][0, 0]

    small = {k: w[k] for k in REPLICATED}
    packed, grad_x, _ = _local_step(x, mem, loss_target, **small, **whole, late_weights=late_weights,
                                    grads_ready=grads_ready, first_token=first_token)

    small_lands = [lax.empty((N_DEV * PACK_ROWS, D_MODEL), F32)]
    small_gather = _copies_start("gather_small_start", [packed], small_lands, _gather_copies, after=grad_x)

    out_g, out_d, out_m, out_v = {}, {}, {}, {}
    after = small_gather[4]
    for tag, (names, started, copies, index) in enumerate(pending):
        srcs, parts = _copies_wait("grad_exchange_wait_" + str(tag), started[0], started[1], started[2], started[3], after, copies)
        for k, s, p in zip(names, srcs, parts):
            res = _sum_adam(index, s, p, w2[k], m2[k], v2[k], "adam_" + k)
            out_g[k], out_d[k], out_m[k], out_v[k] = ((jnp.transpose(r) if k in TRANSPOSED else r)[None] for r in res)
            after = res[0]

    _, (everyone,) = _copies_wait("gather_small_wait", small_gather[0], small_gather[1], small_gather[2], small_gather[3], after,
                                  _gather_copies, own_block=True)
    res = _adam_small(everyone.reshape(N_DEV, PACK_ROWS, D_MODEL), [w[k] for k in REPLICATED], [m[k] for k in REPLICATED],
                      [v[k] for k in REPLICATED])
    for i, k in enumerate(REPLICATED):
        out_g[k], out_d[k], out_m[k], out_v[k] = res[4 * i:4 * i + 4]
    loss = res[-1].reshape(())

    return (loss, grad_x, *[out_g[k] for k in WEIGHTS], *[out_d[k] for k in WEIGHTS], *[out_m[k] for k in WEIGHTS],
            *[out_v[k] for k in WEIGHTS])
```

```python
import functools

import jax
import jax.numpy as jnp
import numpy as np
from jax import lax
from jax.experimental import pallas as pl
from jax.experimental.pallas import tpu as pltpu

F32 = jnp.float32
BF16 = jnp.bfloat16

D_MODEL = 1024
FOX_HEADS = 8
FOX_HEAD_DIM = 64
FOX_WIDTH = 512
GDN_HEADS = 4
GDN_HEAD_DIM = 128
GDN_WIDTH = 512
CONV_WIDTH = 4
GDN_CHUNK = 128
GDN_GROUP = 4
FOX_BLOCK = 512
XATTN_HEADS = 4
XATTN_HEAD_DIM = 128
XATTN_WIDTH = 512
D_FF = 4096
EPS = 1e-6
NEG_INF = -1e30
N_DEV = 8

ADAM_LR = 0.001
ADAM_B1 = 0.9
ADAM_B2 = 0.999
ADAM_EPS = 1e-08
ADAM_WD = 0.01
ADAM_STEP = 10

P_FOX = 0
P_GDN = 1536
P_Z = 3072
P_SMALL = 3584
P_DIM = 3712
SM_F = 0
SM_B = 8
SM_A = 12
SM_ROWS = 16

LANES = 128
VMEM_LIMIT = 56 * 1024 * 1024

NN = (((1,), (0,)), ((), ()))
NT = (((1,), (1,)), ((), ()))
TN = (((0,), (0,)), ((), ()))


def _dot(a, b, dims=NN):
    return lax.dot_general(a.astype(BF16), b.astype(BF16), dims, preferred_element_type=F32)


def _cparams(sem=None):
    kw = dict(vmem_limit_bytes=VMEM_LIMIT)
    if sem is not None:
        kw["dimension_semantics"] = sem
    return pltpu.CompilerParams(**kw)


def _sigmoid(x):
    return 0.5 * (jnp.tanh(0.5 * x) + 1.0)


def _softplus(x):
    return jnp.maximum(x, 0.0) + jnp.log1p(jnp.exp(-jnp.abs(x)))


def _log_sigmoid(x):
    return -_softplus(-x)


def _rms(x, g):
    r = lax.rsqrt(jnp.mean(x * x, axis=-1, keepdims=True) + EPS)
    return x * r * g


def _rms_bwd(x, g, dy):
    r = lax.rsqrt(jnp.mean(x * x, axis=-1, keepdims=True) + EPS)
    xh = x * r
    dg = jnp.sum(dy * xh, axis=0, keepdims=True)
    dyg = dy * g
    dx = r * (dyg - xh * jnp.mean(dyg * xh, axis=-1, keepdims=True))
    return dx, dg


def _pair_stat(t, m0):
    s0 = jnp.sum(jnp.where(m0, t, 0.0), axis=-1, keepdims=True)
    s1 = jnp.sum(jnp.where(m0, 0.0, t), axis=-1, keepdims=True)
    return jnp.where(m0, s0, s1)


def _rms_pair(x, g, m0):
    r = lax.rsqrt(_pair_stat(x * x, m0) * (1.0 / FOX_HEAD_DIM) + EPS)
    return x * r * g


def _rms_pair_bwd(x, g, dy, m0):
    r = lax.rsqrt(_pair_stat(x * x, m0) * (1.0 / FOX_HEAD_DIM) + EPS)
    xh = x * r
    dg = jnp.sum(dy * xh, axis=0, keepdims=True)
    dyg = dy * g
    dx = r * (dyg - xh * (_pair_stat(dyg * xh, m0) * (1.0 / FOX_HEAD_DIM)))
    return dx, dg


@jax.custom_vjp
def _mm_nn(a, b):
    return _dot(a, b, NN)


_mm_nn.defvjp(lambda a, b: (_dot(a, b, NN), (a, b)),
              lambda r, g: (_dot(g, r[1], NT), _dot(r[0], g, TN)))


@jax.custom_vjp
def _mm_nt(a, b):
    return _dot(a, b, NT)


_mm_nt.defvjp(lambda a, b: (_dot(a, b, NT), (a, b)),
              lambda r, g: (_dot(g, r[1], NN), _dot(g, r[0], TN)))


@jax.custom_vjp
def _mm_tn(a, b):
    return _dot(a, b, TN)


_mm_tn.defvjp(lambda a, b: (_dot(a, b, TN), (a, b)),
              lambda r, g: (_dot(r[1], g, NT), _dot(r[0], g, NN)))


def _dot3(a, b, dims):
    ah = a.astype(BF16)
    al = (a - ah.astype(F32)).astype(BF16)
    bh = b.astype(BF16)
    bl = (b - bh.astype(F32)).astype(BF16)
    d = functools.partial(lax.dot_general, dimension_numbers=dims, preferred_element_type=F32)
    return d(ah, bh) + d(ah, bl) + d(al, bh)


def _neumann_inverses(mats):
    c = mats[0].shape[0]
    eye = (lax.broadcasted_iota(jnp.int32, (c, c), 0) == lax.broadcasted_iota(jnp.int32, (c, c), 1)).astype(F32)
    xs = [eye - a for a in mats]
    ps = list(mats)
    k = 2
    while k < c + 1:
        ps = [_dot3(p, p, NN) for p in ps]
        xs = [x + _dot3(x, p, NN) for x, p in zip(xs, ps)]
        k *= 2
    return xs


@jax.custom_vjp
def _unit_lower_inverses(mats):
    return _neumann_inverses(mats)


def _unit_lower_inverses_fwd(mats):
    ts = _neumann_inverses(mats)
    return ts, ts


def _unit_lower_inverses_bwd(ts, gs):
    left = [_dot3(t, g, TN) for t, g in zip(ts, gs)]
    return ([-_dot3(m, t, NT) for m, t in zip(left, ts)],)


_unit_lower_inverses.defvjp(_unit_lower_inverses_fwd, _unit_lower_inverses_bwd)


def _wgrad(a, b, name, bk=1024, bn=1024, bt=1024, column_blocks=None):
    t_len, k_len = a.shape
    n_len = b.shape[1]
    bk, bn, bt = min(bk, k_len), min(bn, n_len), min(bt, t_len)
    nt = t_len // bt

    def body(a_ref, b_ref, o_ref, acc_ref):
        t = pl.program_id(2)

        @pl.when(t == 0)
        def _():
            acc_ref[...] = jnp.zeros_like(acc_ref)

        acc_ref[...] += _dot(a_ref[...], b_ref[...], TN)

        @pl.when(t == nt - 1)
        def _():
            if column_blocks:
                for jj in range(bn // column_blocks):
                    o_ref[jj] = acc_ref[:, jj * column_blocks:(jj + 1) * column_blocks]
            else:
                o_ref[...] = acc_ref[...]

    if column_blocks:
        out_spec = pl.BlockSpec((bn // column_blocks, bk, column_blocks), lambda i, j, t: (j, i, 0))
        out_shape = jax.ShapeDtypeStruct((n_len // column_blocks, k_len, column_blocks), F32)
    else:
        out_spec = pl.BlockSpec((bk, bn), lambda i, j, t: (i, j))
        out_shape = jax.ShapeDtypeStruct((k_len, n_len), F32)
    return pl.pallas_call(
        body, name=name, grid=(k_len // bk, n_len // bn, nt),
        in_specs=[pl.BlockSpec((bt, bk), lambda i, j, t: (t, i)), pl.BlockSpec((bt, bn), lambda i, j, t: (t, j))],
        out_specs=out_spec, out_shape=out_shape,
        scratch_shapes=[pltpu.VMEM((bk, bn), F32)],
        compiler_params=_cparams(("parallel", "parallel", "arbitrary")),
    )(a, b)


def _wgrad_stacked(pieces, b, name, bn=512, bt=1024):
    t_len, n_len = b.shape
    n_p = len(pieces)
    starts = [int(s) for s in np.cumsum([0] + [p.shape[1] for p in pieces])]
    bn, bt = min(bn, n_len), min(bt, t_len)
    nt = t_len // bt

    def body(*refs):
        b_ref, o_ref, acc_ref = refs[n_p:]
        t = pl.program_id(1)

        @pl.when(t == 0)
        def _():
            acc_ref[...] = jnp.zeros_like(acc_ref)

        for k in range(n_p):
            acc_ref[starts[k]:starts[k + 1], :] += _dot(refs[k][...], b_ref[...], TN)

        @pl.when(t == nt - 1)
        def _():
            o_ref[...] = acc_ref[...]

    return pl.pallas_call(
        body, name=name, grid=(n_len // bn, nt),
        in_specs=[pl.BlockSpec((bt, p.shape[1]), lambda j, t: (t, 0)) for p in pieces] + [pl.BlockSpec((bt, bn), lambda j, t: (t, j))],
        out_specs=pl.BlockSpec((starts[-1], bn), lambda j, t: (0, j)),
        out_shape=jax.ShapeDtypeStruct((starts[-1], n_len), F32),
        scratch_shapes=[pltpu.VMEM((starts[-1], bn), F32)],
        compiler_params=_cparams(("parallel", "arbitrary")),
    )(*pieces, b)


def _rows_matmul(a, b, name, bt=512):
    r_len, t_len = a.shape
    n_len = b.shape[1]
    bt = min(bt, t_len)
    nt = t_len // bt

    def body(a_ref, b_ref, o_ref):
        t = pl.program_id(0)

        @pl.when(t == 0)
        def _():
            o_ref[...] = jnp.zeros_like(o_ref)

        o_ref[...] += _dot(a_ref[...], b_ref[...], NN)

    return pl.pallas_call(
        body, name=name, grid=(nt,),
        in_specs=[pl.BlockSpec((r_len, bt), lambda t: (0, t)), pl.BlockSpec((bt, n_len), lambda t: (t, 0))],
        out_specs=pl.BlockSpec((r_len, n_len), lambda t: (0, 0)),
        out_shape=jax.ShapeDtypeStruct((r_len, n_len), F32),
        compiler_params=_cparams(("arbitrary",)),
    )(a, b)


def _in_proj(x, g, wp, wst, tm=512):
    t_len, d = x.shape
    tm = min(tm, t_len)

    def body(x_ref, g_ref, wp_ref, wst_ref, h_ref, fox_ref, gdn_ref, z_ref, sm_ref, smt_ref):
        h = _rms(x_ref[...], g_ref[...]).astype(BF16)
        h_ref[...] = h
        p = _dot(h, wp_ref[...], NT)
        fox_ref[...] = p[:, P_FOX:P_GDN]
        gdn_ref[...] = p[:, P_GDN:P_Z]
        z_ref[...] = p[:, P_Z:P_SMALL]
        sm_ref[...] = p[:, P_SMALL:P_DIM]
        smt_ref[...] = _dot(wst_ref[...], h, NT)

    row = lambda i: (i, 0)
    fixed = lambda i: (0, 0)
    return pl.pallas_call(
        body, name="in_proj", grid=(t_len // tm,),
        in_specs=[pl.BlockSpec((tm, d), row), pl.BlockSpec((1, d), fixed), _resident((P_DIM, d)),
                  pl.BlockSpec((SM_ROWS, d), fixed)],
        out_specs=[pl.BlockSpec((tm, d), row), pl.BlockSpec((tm, 1536), row), pl.BlockSpec((tm, 1536), row),
                   pl.BlockSpec((tm, 512), row), pl.BlockSpec((tm, LANES), row), pl.BlockSpec((SM_ROWS, tm), lambda i: (0, i))],
        out_shape=[jax.ShapeDtypeStruct((t_len, d), BF16), jax.ShapeDtypeStruct((t_len, 1536), F32),
                   jax.ShapeDtypeStruct((t_len, 1536), F32), jax.ShapeDtypeStruct((t_len, 512), F32),
                   jax.ShapeDtypeStruct((t_len, LANES), F32), jax.ShapeDtypeStruct((SM_ROWS, t_len), F32)],
        compiler_params=_cparams(("parallel",)),
    )(x, g, wp, wst)


def _in_proj_bwd(dprojs, dsmt, x, g, wp, wst, dx1, tm=512):
    t_len, d = x.shape
    tm = min(tm, t_len)
    n_p = len(dprojs)
    starts = np.cumsum([0] + [p.shape[1] for p in dprojs])

    def body(*refs):
        dp_refs = refs[:n_p]
        dst_ref, x_ref, g_ref, wp_ref, wst_ref, dx1_ref, dx_ref, dg_ref = refs[n_p:]
        i = pl.program_id(0)
        dh = _dot(dst_ref[...], wst_ref[...], TN)
        for k in range(n_p):
            dh = dh + _dot(dp_refs[k][...], wp_ref[int(starts[k]):int(starts[k + 1]), :], NN)
        dxn, dg = _rms_bwd(x_ref[...], g_ref[...], dh)
        dx_ref[...] = dx1_ref[...] + dxn

        @pl.when(i == 0)
        def _():
            dg_ref[...] = jnp.zeros_like(dg_ref)

        dg_ref[...] += dg

    row = lambda i: (i, 0)
    fixed = lambda i: (0, 0)
    return pl.pallas_call(
        body, name="in_proj_bwd", grid=(t_len // tm,),
        in_specs=[pl.BlockSpec((tm, p.shape[1]), row) for p in dprojs] + [
            pl.BlockSpec((SM_ROWS, tm), lambda i: (0, i)), pl.BlockSpec((tm, d), row),
            pl.BlockSpec((1, d), fixed), _resident((P_DIM, d)), pl.BlockSpec((SM_ROWS, d), fixed),
            pl.BlockSpec((tm, d), row)],
        out_specs=[pl.BlockSpec((tm, d), row), pl.BlockSpec((1, d), fixed)],
        out_shape=[jax.ShapeDtypeStruct((t_len, d), F32), jax.ShapeDtypeStruct((1, d), F32)],
        compiler_params=_cparams(("arbitrary",)),
    )(*dprojs, dsmt, x, g, wp, wst, dx1)


def _fox_cum(smt, bias_col, n_batch, s_len, ck=256):
    ck = min(ck, s_len)

    def body(s_ref, b_ref, c_ref):
        tri = (lax.broadcasted_iota(jnp.int32, (ck, ck), 0) <= lax.broadcasted_iota(jnp.int32, (ck, ck), 1)).astype(F32)
        carry = jnp.zeros((SM_ROWS, 1), F32)
        for r in range(s_len // ck):
            ls = _log_sigmoid(s_ref[:, r * ck:(r + 1) * ck] + b_ref[...])
            c = jnp.dot(ls, tri, precision=lax.Precision.HIGHEST, preferred_element_type=F32) + carry
            c_ref[:, r * ck:(r + 1) * ck] = c
            carry = c[:, ck - 1:ck]

    return pl.pallas_call(
        body, name="fox_cum", grid=(n_batch,),
        in_specs=[pl.BlockSpec((SM_ROWS, s_len), lambda b: (0, b)), pl.BlockSpec((SM_ROWS, 1), lambda b: (0, 0))],
        out_specs=pl.BlockSpec((SM_ROWS, s_len), lambda b: (0, b)),
        out_shape=jax.ShapeDtypeStruct(smt.shape, F32),
        compiler_params=_cparams(("parallel",)),
    )(smt, bias_col)


def _fox_cum_bwd(dc, smt, bias_col, n_batch, s_len, ck=256):
    ck = min(ck, s_len)
    nr = s_len // ck

    def body(dc_ref, s_ref, b_ref, dl_ref, db_ref):
        b = pl.program_id(0)
        tri = (lax.broadcasted_iota(jnp.int32, (ck, ck), 0) >= lax.broadcasted_iota(jnp.int32, (ck, ck), 1)).astype(F32)
        carry = jnp.zeros((SM_ROWS, 1), F32)
        tot = jnp.zeros((SM_ROWS, 1), F32)
        for r in reversed(range(nr)):
            sl = slice(r * ck, (r + 1) * ck)
            dls = jnp.dot(dc_ref[:, sl], tri, precision=lax.Precision.HIGHEST, preferred_element_type=F32) + carry
            carry = dls[:, 0:1]
            dl = dls * (1.0 - _sigmoid(s_ref[:, sl] + b_ref[...]))
            dl_ref[:, sl] = dl
            tot = tot + jnp.sum(dl, axis=1, keepdims=True)

        @pl.when(b == 0)
        def _():
            db_ref[...] = jnp.zeros_like(db_ref)

        db_ref[...] += jnp.broadcast_to(tot, db_ref.shape)

    return pl.pallas_call(
        body, name="fox_cum_bwd", grid=(n_batch,),
        in_specs=[pl.BlockSpec((SM_ROWS, s_len), lambda b: (0, b)), pl.BlockSpec((SM_ROWS, s_len), lambda b: (0, b)),
                  pl.BlockSpec((SM_ROWS, 1), lambda b: (0, 0))],
        out_specs=[pl.BlockSpec((SM_ROWS, s_len), lambda b: (0, b)), pl.BlockSpec((SM_ROWS, LANES), lambda b: (0, 0))],
        out_shape=[jax.ShapeDtypeStruct(smt.shape, F32), jax.ShapeDtypeStruct((SM_ROWS, LANES), F32)],
        compiler_params=_cparams(("arbitrary",)),
    )(dc, smt, bias_col)


def _fox_diagonal_mask(tq):
    return lax.broadcasted_iota(jnp.int32, (tq, tq), 1) <= lax.broadcasted_iota(jnp.int32, (tq, tq), 0)


def _fox_fwd(pf, cb, gq2, gk2, go2, tq=256):
    n_batch, s_len, _ = pf.shape
    tq = min(tq, s_len)
    nq = s_len // tq
    scale = FOX_HEAD_DIM ** -0.5

    def body(q_ref, k_ref, v_ref, c_ref, gq_ref, gk_ref, go_ref, o_ref, on_ref, lse_ref, kh_ref, vh_ref):
        j = pl.program_id(1)
        i = pl.program_id(2)
        m0 = lax.broadcasted_iota(jnp.int32, (1, LANES), 1) < FOX_HEAD_DIM

        @pl.when(i == 0)
        def _():
            kn = _rms_pair(k_ref[0], gk_ref[...], m0)
            kh_ref[0] = jnp.where(m0, kn, 0.0).astype(BF16)
            kh_ref[1] = jnp.where(m0, 0.0, kn).astype(BF16)
            v = v_ref[0]
            vh_ref[0] = jnp.where(m0, v, 0.0).astype(BF16)
            vh_ref[1] = jnp.where(m0, 0.0, v).astype(BF16)

        qb = (_rms_pair(q_ref[0], gq_ref[...], m0) * scale).astype(BF16)

        def step(kb, carry, diagonal=False):
            ms, ls, acc = carry
            off = pl.multiple_of(kb * tq, tq)
            new_m, new_l, alphas, pv = [], [], [], []
            for hh in range(2):
                s = _dot(qb, kh_ref[hh, pl.ds(off, tq), :], NT)
                s = s - c_ref[0, kb, pl.ds(2 * j + hh, 1), :]
                if diagonal:
                    s = jnp.where(_fox_diagonal_mask(tq), s, NEG_INF)
                m_new = jnp.maximum(ms[hh], jnp.max(s, axis=-1, keepdims=True))
                alpha = jnp.exp(ms[hh] - m_new)
                p = jnp.exp(s - m_new)
                new_l.append(alpha * ls[hh] + jnp.sum(p, axis=-1, keepdims=True))
                new_m.append(m_new)
                alphas.append(alpha)
                pv.append(_dot(p, vh_ref[hh, pl.ds(off, tq), :], NN))
            acc = jnp.where(m0, alphas[0], alphas[1]) * acc + pv[0] + pv[1]
            return tuple(new_m), tuple(new_l), acc

        init_m = (jnp.full((tq, 1), NEG_INF, F32),) * 2
        init_l = (jnp.zeros((tq, 1), F32),) * 2
        carry = lax.fori_loop(0, i, step, (init_m, init_l, jnp.zeros((tq, LANES), F32)))
        ms, ls, acc = step(i, carry, diagonal=True)
        o = acc / jnp.where(m0, ls[0], ls[1])
        o_ref[0] = o
        on_ref[0] = _rms_pair(o, go_ref[...], m0).astype(BF16)
        lse_ref[0] = jnp.where(m0, ms[0] + jnp.log(ls[0]), ms[1] + jnp.log(ls[1]))

    fixed = lambda b, j, i: (0, 0)
    tile = lambda b, j, i: (b, i, j)
    return pl.pallas_call(
        body, name="fox_fwd", grid=(n_batch, 4, nq),
        in_specs=[pl.BlockSpec((1, tq, LANES), tile), pl.BlockSpec((1, s_len, LANES), lambda b, j, i: (b, 0, 4 + j)),
                  pl.BlockSpec((1, s_len, LANES), lambda b, j, i: (b, 0, 8 + j)),
                  pl.BlockSpec((1, nq, SM_ROWS, tq), lambda b, j, i: (b, 0, 0, 0)),
                  pl.BlockSpec((1, LANES), fixed), pl.BlockSpec((1, LANES), fixed), pl.BlockSpec((1, LANES), fixed)],
        out_specs=[pl.BlockSpec((1, tq, LANES), tile), pl.BlockSpec((1, tq, LANES), tile), pl.BlockSpec((1, tq, LANES), tile)],
        out_shape=[jax.ShapeDtypeStruct((n_batch, s_len, FOX_WIDTH), F32), jax.ShapeDtypeStruct((n_batch, s_len, FOX_WIDTH), BF16),
                   jax.ShapeDtypeStruct((n_batch, s_len, FOX_WIDTH), F32)],
        scratch_shapes=[pltpu.VMEM((2, s_len, LANES), BF16), pltpu.VMEM((2, s_len, LANES), BF16)],
        compiler_params=_cparams(("parallel", "parallel", "arbitrary")),
    )(pf, pf, pf, cb, gq2, gk2, go2)


def _fox_bwd(pf, cb, gq2, gk2, go2, o, lse, don, tq=256):
    n_batch, s_len, _ = pf.shape
    tq = min(tq, s_len)
    nq = s_len // tq
    scale = FOX_HEAD_DIM ** -0.5

    def body(q_ref, k_ref, v_ref, c_ref, gq_ref, gk_ref, go_ref, o_ref, lse_ref, don_ref,
             dq_ref, dk_ref, dv_ref, dc_ref, dgq_ref, dgk_ref, dgo_ref, kh_ref, vh_ref, dka_ref, dva_ref, dca_ref):
        b = pl.program_id(0)
        j = pl.program_id(1)
        i = pl.program_id(2)
        m0 = lax.broadcasted_iota(jnp.int32, (1, LANES), 1) < FOX_HEAD_DIM

        @pl.when((b == 0) & (j == 0) & (i == 0))
        def _():
            dgq_ref[...] = jnp.zeros_like(dgq_ref)
            dgk_ref[...] = jnp.zeros_like(dgk_ref)
            dgo_ref[...] = jnp.zeros_like(dgo_ref)

        @pl.when(i == 0)
        def _():
            kn = _rms_pair(k_ref[0], gk_ref[...], m0)
            kh_ref[0] = jnp.where(m0, kn, 0.0).astype(BF16)
            kh_ref[1] = jnp.where(m0, 0.0, kn).astype(BF16)
            v = v_ref[0]
            vh_ref[0] = jnp.where(m0, v, 0.0).astype(BF16)
            vh_ref[1] = jnp.where(m0, 0.0, v).astype(BF16)
            dka_ref[...] = jnp.zeros_like(dka_ref)
            dva_ref[...] = jnp.zeros_like(dva_ref)
            dca_ref[...] = jnp.zeros_like(dca_ref)

        q = q_ref[0]
        qn = _rms_pair(q, gq_ref[...], m0)
        qs = qn * scale
        qb = qs.astype(BF16)
        qh = (jnp.where(m0, qs, 0.0).astype(BF16), jnp.where(m0, 0.0, qs).astype(BF16))
        ot = o_ref[0]
        do, dgo = _rms_pair_bwd(ot, go_ref[...], don_ref[0], m0)
        dgo_ref[...] += dgo
        dd = do * ot
        delta = (jnp.sum(jnp.where(m0, dd, 0.0), axis=-1, keepdims=True), jnp.sum(jnp.where(m0, 0.0, dd), axis=-1, keepdims=True))
        doh = (jnp.where(m0, do, 0.0).astype(BF16), jnp.where(m0, 0.0, do).astype(BF16))
        lse_t = lse_ref[0]
        lse_h = (lse_t[:, 0:1], lse_t[:, FOX_HEAD_DIM:FOX_HEAD_DIM + 1])

        def step(kb, carry, diagonal=False):
            dqn, rs = carry
            rs = list(rs)
            off = pl.multiple_of(kb * tq, tq)
            for hh in range(2):
                kblk = kh_ref[hh, pl.ds(off, tq), :]
                vblk = vh_ref[hh, pl.ds(off, tq), :]
                s = _dot(qb, kblk, NT)
                s = s - c_ref[0, kb, pl.ds(2 * j + hh, 1), :]
                if diagonal:
                    s = jnp.where(_fox_diagonal_mask(tq), s, NEG_INF)
                p = jnp.exp(s - lse_h[hh])
                dp = _dot(doh[hh], vblk, NT)
                ds = p * (dp - delta[hh])
                dva_ref[pl.ds(off, tq), :] += _dot(p, doh[hh], TN)
                dka_ref[pl.ds(off, tq), :] += _dot(ds, qh[hh], TN)
                dca_ref[kb, hh:hh + 1, :] += -jnp.sum(ds, axis=0, keepdims=True)
                rs[hh] = rs[hh] + jnp.sum(ds, axis=-1, keepdims=True)
                dqn = dqn + _dot(ds, kblk, NN)
            return dqn, tuple(rs)

        carry = lax.fori_loop(0, i, step, (jnp.zeros((tq, LANES), F32), (jnp.zeros((tq, 1), F32),) * 2))
        dqn, rs = step(i, carry, diagonal=True)
        dqn = dqn * scale
        rs_rows = jnp.where(m0, rs[0], rs[1]).T
        dca_ref[i, 0:1, :] += rs_rows[0:1, :]
        dca_ref[i, 1:2, :] += rs_rows[FOX_HEAD_DIM:FOX_HEAD_DIM + 1, :]
        dq, dgq = _rms_pair_bwd(q, gq_ref[...], dqn, m0)
        dq_ref[0] = dq.astype(BF16)
        dgq_ref[...] += dgq

        @pl.when(i == nq - 1)
        def _():
            dk, dgk = _rms_pair_bwd(k_ref[0], gk_ref[...], dka_ref[...], m0)
            dk_ref[0] = dk.astype(BF16)
            dgk_ref[...] += dgk
            dv_ref[0] = dva_ref[...].astype(BF16)
            dc_ref[0, 0] = dca_ref[...]

    fixed = lambda b, j, i: (0, 0)
    tile = lambda b, j, i: (b, i, j)
    full = lambda b, j, i: (b, 0, j)
    wide = jax.ShapeDtypeStruct((n_batch, s_len, FOX_WIDTH), BF16)
    gain = jax.ShapeDtypeStruct((1, LANES), F32)
    return pl.pallas_call(
        body, name="fox_bwd", grid=(n_batch, 4, nq),
        in_specs=[pl.BlockSpec((1, tq, LANES), tile), pl.BlockSpec((1, s_len, LANES), lambda b, j, i: (b, 0, 4 + j)),
                  pl.BlockSpec((1, s_len, LANES), lambda b, j, i: (b, 0, 8 + j)),
                  pl.BlockSpec((1, nq, SM_ROWS, tq), lambda b, j, i: (b, 0, 0, 0)),
                  pl.BlockSpec((1, LANES), fixed), pl.BlockSpec((1, LANES), fixed), pl.BlockSpec((1, LANES), fixed),
                  pl.BlockSpec((1, tq, LANES), tile), pl.BlockSpec((1, tq, LANES), tile), pl.BlockSpec((1, tq, LANES), tile)],
        out_specs=[pl.BlockSpec((1, tq, LANES), tile), pl.BlockSpec((1, s_len, LANES), full), pl.BlockSpec((1, s_len, LANES), full),
                   pl.BlockSpec((1, 1, nq, 8, tq), lambda b, j, i: (b, j, 0, 0, 0)),
                   pl.BlockSpec((1, LANES), fixed), pl.BlockSpec((1, LANES), fixed), pl.BlockSpec((1, LANES), fixed)],
        out_shape=[wide, wide, wide, jax.ShapeDtypeStruct((n_batch, 4, nq, 8, tq), F32), gain, gain, gain],
        scratch_shapes=[pltpu.VMEM((2, s_len, LANES), BF16), pltpu.VMEM((2, s_len, LANES), BF16),
                        pltpu.VMEM((s_len, LANES), F32), pltpu.VMEM((s_len, LANES), F32), pltpu.VMEM((nq, 8, tq), F32)],
        compiler_params=_cparams(("arbitrary", "arbitrary", "arbitrary")),
    )(pf, pf, pf, cb, gq2, gk2, go2, o, lse, don)


def _shift_down(x, k):
    row = lax.broadcasted_iota(jnp.int32, x.shape, 0)
    return jnp.where(row >= k, pltpu.roll(x, k, 0), 0.0)


def _shift_up(x, k):
    n = x.shape[0]
    row = lax.broadcasted_iota(jnp.int32, x.shape, 0)
    return jnp.where(row < n - k, pltpu.roll(x, n - k, 0), 0.0)


def _conv_silu(x, w):
    y = w[3:4] * x + w[2:3] * _shift_down(x, 1) + w[1:2] * _shift_down(x, 2) + w[0:1] * _shift_down(x, 3)
    sig = _sigmoid(y)
    return y, sig, y * sig


def _gdn_pre(pg, conv_w):
    n_batch, s_len, width = pg.shape
    ncb = width // LANES

    def body(x_ref, w_ref, o_ref):
        cb = pl.program_id(1)
        _, _, s = _conv_silu(x_ref[0], w_ref[...])
        sn = s * lax.rsqrt(jnp.sum(s * s, axis=-1, keepdims=True) + EPS)
        o_ref[0] = jnp.where(cb < 2 * GDN_HEADS, sn, s)

    return pl.pallas_call(
        body, name="gdn_pre", grid=(n_batch, ncb),
        in_specs=[pl.BlockSpec((1, s_len, LANES), lambda b, c: (b, 0, c)), pl.BlockSpec((8, LANES), lambda b, c: (0, c))],
        out_specs=pl.BlockSpec((1, s_len, LANES), lambda b, c: (b, 0, c)),
        out_shape=jax.ShapeDtypeStruct(pg.shape, F32),
        compiler_params=_cparams(("parallel", "parallel")),
    )(pg, conv_w)


def _gdn_pre_bwd(pg, conv_w, dout):
    n_batch, s_len, width = pg.shape
    ncb = width // LANES

    def body(x_ref, w_ref, d_ref, dx_ref, dw_ref):
        cb = pl.program_id(0)
        b = pl.program_id(1)
        x = x_ref[0]
        w = w_ref[...]
        d = d_ref[0]
        y, sig, s = _conv_silu(x, w)
        rr = lax.rsqrt(jnp.sum(s * s, axis=-1, keepdims=True) + EPS)
        sn = s * rr
        ds_n = rr * (d - sn * jnp.sum(d * sn, axis=-1, keepdims=True))
        ds = jnp.where(cb < 2 * GDN_HEADS, ds_n, d)
        dy = ds * (sig * (1.0 + y * (1.0 - sig)))
        dyu = [_shift_up(dy, 3 - jj) if jj < 3 else dy for jj in range(CONV_WIDTH)]
        dx = w[0:1] * dyu[0] + w[1:2] * dyu[1] + w[2:3] * dyu[2] + w[3:4] * dyu[3]
        dx_ref[0] = dx.astype(BF16)
        dw = [jnp.sum(dyu[jj] * x, axis=0, keepdims=True) for jj in range(CONV_WIDTH)]
        rows = lax.broadcasted_iota(jnp.int32, (8, LANES), 0)
        dwb = jnp.zeros((8, LANES), F32)
        for jj in range(CONV_WIDTH):
            dwb = dwb + jnp.where(rows == jj, dw[jj], 0.0)

        @pl.when(b == 0)
        def _():
            dw_ref[...] = jnp.zeros_like(dw_ref)

        dw_ref[...] += dwb

    blk = lambda c, b: (b, 0, c)
    return pl.pallas_call(
        body, name="gdn_pre_bwd", grid=(ncb, n_batch),
        in_specs=[pl.BlockSpec((1, s_len, LANES), blk), pl.BlockSpec((8, LANES), lambda c, b: (0, c)), pl.BlockSpec((1, s_len, LANES), blk)],
        out_specs=[pl.BlockSpec((1, s_len, LANES), blk), pl.BlockSpec((8, LANES), lambda c, b: (0, c))],
        out_shape=[jax.ShapeDtypeStruct(pg.shape, BF16), jax.ShapeDtypeStruct((8, width), F32)],
        compiler_params=_cparams(("parallel", "arbitrary")),
    )(pg, conv_w, dout)


def _gdn_gates(smc, smr, a_c, dt_c, a_r, dt_r, h):
    lane = lax.broadcasted_iota(jnp.int32, (1, LANES), 1)
    sub = lax.broadcasted_iota(jnp.int32, (SM_ROWS, 1), 0)
    beta_c = jnp.sum(jnp.where(lane == SM_B + h, _sigmoid(smc), 0.0), axis=1, keepdims=True)
    g_all_c = -jnp.exp(a_c) * _softplus(smc + dt_c)
    g_c = jnp.sum(jnp.where(lane == SM_A + h, g_all_c, 0.0), axis=1, keepdims=True)
    g_all_r = -jnp.exp(a_r) * _softplus(smr + dt_r)
    g_r = jnp.sum(jnp.where(sub == SM_A + h, g_all_r, 0.0), axis=0, keepdims=True)
    return beta_c, g_c, g_r


@jax.custom_vjp
def _known_inverse(a, t):
    return t


_known_inverse.defvjp(lambda a, t: (t, t),
                      lambda t, g: (-_dot3(_dot3(t, g, TN), t, NT), jnp.zeros_like(t)))


def _gdn_group(qkv, z, smc, smr, a_c, dt_c, a_r, dt_r, go, states, inverses=None):
    n_grp = len(qkv)
    c = qkv[0].shape[0]
    hd = GDN_HEAD_DIM
    pairs = [(g, h) for g in range(n_grp) for h in range(GDN_HEADS)]
    ii = lax.broadcasted_iota(jnp.int32, (c, c), 0)
    jj = lax.broadcasted_iota(jnp.int32, (c, c), 1)
    incl = ii >= jj
    col = lambda arr, base, h: arr[:, base + h * hd:base + (h + 1) * hd]

    qs, ks, kbs, vbs, gcs, g_lasts, amats, intras = [], [], [], [], [], [], [], []
    for g, h in pairs:
        beta_c, g_c, g_r = _gdn_gates(smc[g], smr[g], a_c, dt_c, a_r, dt_r, h)
        gc_c = jnp.sum(jnp.where(incl, g_r, 0.0), axis=1, keepdims=True)
        gc_r = jnp.sum(jnp.where(ii <= jj, g_c, 0.0), axis=0, keepdims=True)
        decay = jnp.where(incl, jnp.exp(jnp.where(incl, gc_c - gc_r, 0.0)), 0.0)
        k = col(qkv[g], GDN_WIDTH, h)
        kb = k * beta_c
        qs.append(col(qkv[g], 0, h) * (hd ** -0.5))
        ks.append(k)
        kbs.append(kb)
        vbs.append(col(qkv[g], 2 * GDN_WIDTH, h) * beta_c)
        gcs.append(gc_c)
        g_lasts.append(jnp.sum(g_c, axis=0, keepdims=True))
        both = _mm_nt(jnp.concatenate([kb, qs[-1]], axis=0), k)
        amats.append(jnp.where(ii > jj, both[0:c] * decay, 0.0))
        intras.append(both[c:2 * c] * decay)
    ts = _unit_lower_inverses(amats) if inverses is None else [_known_inverse(a, t) for a, t in zip(amats, inverses)]
    egcs = [jnp.exp(gc) for gc in gcs]
    uws = [_mm_nn(t, jnp.concatenate([vb, kb * e], axis=1)) for t, vb, kb, e in zip(ts, vbs, kbs, egcs)]
    us = [uw[:, 0:hd] for uw in uws]
    ws = [uw[:, hd:2 * hd] for uw in uws]
    qes = [q * e for q, e in zip(qs, egcs)]
    kds = [k * jnp.exp(gl - gc) for k, gl, gc in zip(ks, g_lasts, gcs)]
    sdecs = [jnp.exp(gl) for gl in g_lasts]

    outs = []
    for g in range(n_grp):
        idx = [g * GDN_HEADS + h for h in range(GDN_HEADS)]
        v_new = [us[i] - _mm_nn(ws[i], states[h]) for h, i in enumerate(idx)]
        o = [_mm_nn(jnp.concatenate([qes[i], intras[i]], axis=1), jnp.concatenate([states[h], v_new[h]], axis=0))
             for h, i in enumerate(idx)]
        states = [states[h] * sdecs[i] + _mm_tn(kds[i], v_new[h]) for h, i in enumerate(idx)]
        outs.append([_rms(o[h], go) * (col(z[g], 0, h) * _sigmoid(col(z[g], 0, h))) for h in range(GDN_HEADS)])
    return outs, states, ts


def _gdn_group_size(n_chunks):
    return GDN_GROUP if n_chunks % GDN_GROUP == 0 else 1


def _gdn_fwd(qkvn, z, smc, smr, a_c, dt_c, a_r, dt_r, go):
    n_batch, s_len, _ = qkvn.shape
    c = GDN_CHUNK
    n = s_len // c
    grp = _gdn_group_size(n)
    ng = n // grp
    gc = grp * c
    hd = GDN_HEAD_DIM

    def body(qkv_ref, z_ref, smc_ref, smr_ref, ac_ref, dc_ref, ar_ref, dr_ref, go_ref, og_ref, st_ref, inv_ref, s_ref):
        @pl.when(pl.program_id(1) == 0)
        def _():
            s_ref[...] = jnp.zeros_like(s_ref)

        states = [s_ref[h] for h in range(GDN_HEADS)]
        for h in range(GDN_HEADS):
            st_ref[0, 0, h] = states[h]
        rows = lambda k: slice(k * c, (k + 1) * c)
        outs, nxt, invs = _gdn_group([qkv_ref[0, rows(k), :] for k in range(grp)], [z_ref[0, rows(k), :] for k in range(grp)],
                                     [smc_ref[0, rows(k), :] for k in range(grp)], [smr_ref[k] for k in range(grp)],
                                     ac_ref[...], dc_ref[...], ar_ref[...], dr_ref[...], go_ref[...], states)
        for k in range(grp):
            for h in range(GDN_HEADS):
                og_ref[0, rows(k), h * hd:(h + 1) * hd] = outs[k][h].astype(BF16)
        for p, inv in enumerate(invs):
            inv_ref[0, 0, p] = inv
        for h in range(GDN_HEADS):
            s_ref[h] = nxt[h]

    tok = lambda b, i: (b, i, 0)
    fixed = lambda b, i: (0, 0)
    return pl.pallas_call(
        body, name="gdn_fwd", grid=(n_batch, ng),
        in_specs=[pl.BlockSpec((1, gc, 3 * GDN_WIDTH), tok), pl.BlockSpec((1, gc, GDN_WIDTH), tok), pl.BlockSpec((1, gc, LANES), tok),
                  pl.BlockSpec((grp, SM_ROWS, c), lambda b, i: (b * ng + i, 0, 0)),
                  pl.BlockSpec((1, LANES), fixed), pl.BlockSpec((1, LANES), fixed), pl.BlockSpec((SM_ROWS, 1), fixed),
                  pl.BlockSpec((SM_ROWS, 1), fixed), pl.BlockSpec((1, LANES), fixed)],
        out_specs=[pl.BlockSpec((1, gc, GDN_WIDTH), tok), pl.BlockSpec((1, 1, GDN_HEADS, hd, hd), lambda b, i: (b, i, 0, 0, 0)),
                   pl.BlockSpec((1, 1, grp * GDN_HEADS, c, c), lambda b, i: (b, i, 0, 0, 0))],
        out_shape=[jax.ShapeDtypeStruct((n_batch, s_len, GDN_WIDTH), BF16), jax.ShapeDtypeStruct((n_batch, ng, GDN_HEADS, hd, hd), F32),
                   jax.ShapeDtypeStruct((n_batch, ng, grp * GDN_HEADS, c, c), F32)],
        scratch_shapes=[pltpu.VMEM((GDN_HEADS, hd, hd), F32)],
        compiler_params=_cparams(("parallel", "arbitrary")),
    )(qkvn, z, smc, smr, a_c, dt_c, a_r, dt_r, go)


def _gdn_bwd(qkvn, z, smc, smr, a_c, dt_c, a_r, dt_r, go, states, inverses, dog):
    n_batch, s_len, _ = qkvn.shape
    c = GDN_CHUNK
    n = s_len // c
    grp = _gdn_group_size(n)
    ng = n // grp
    gc = grp * c
    hd = GDN_HEAD_DIM

    def body(qkv_ref, z_ref, smc_ref, smr_ref, ac_ref, dc_ref, ar_ref, dr_ref, go_ref, st_ref, inv_ref, dog_ref,
             dqkv_ref, dz_ref, dsmc_ref, dsmr_ref, dac_ref, ddc_ref, dar_ref, ddr_ref, dgo_ref, ds_ref):
        first = (pl.program_id(0) == 0) & (pl.program_id(1) == 0)

        @pl.when(pl.program_id(1) == 0)
        def _():
            ds_ref[...] = jnp.zeros_like(ds_ref)

        @pl.when(first)
        def _():
            for r in (dac_ref, ddc_ref, dar_ref, ddr_ref, dgo_ref):
                r[...] = jnp.zeros_like(r)

        rows = lambda k: slice(k * c, (k + 1) * c)
        states = [st_ref[0, 0, h] for h in range(GDN_HEADS)]
        prim = ([qkv_ref[0, rows(k), :] for k in range(grp)], [z_ref[0, rows(k), :] for k in range(grp)],
                [smc_ref[0, rows(k), :] for k in range(grp)], [smr_ref[k] for k in range(grp)],
                ac_ref[...], dc_ref[...], ar_ref[...], dr_ref[...], go_ref[...], states)
        invs = [inv_ref[0, 0, p] for p in range(grp * GDN_HEADS)]
        _, vjp = jax.vjp(functools.partial(_gdn_group, inverses=invs), *prim)
        cot = ([[dog_ref[0, rows(k), h * hd:(h + 1) * hd] for h in range(GDN_HEADS)] for k in range(grp)],
               [ds_ref[h] for h in range(GDN_HEADS)], [jnp.zeros((c, c), F32)] * (grp * GDN_HEADS))
        dqkv, dz, dsmc, dsmr, dac, ddc, dar, ddr, dgo, dstates = vjp(cot)
        for k in range(grp):
            dqkv_ref[0, rows(k), :] = dqkv[k]
            dz_ref[0, rows(k), :] = dz[k].astype(BF16)
            dsmc_ref[0, rows(k), :] = dsmc[k]
            dsmr_ref[k] = dsmr[k]
        dac_ref[...] += dac
        ddc_ref[...] += ddc
        dar_ref[...] += dar
        ddr_ref[...] += ddr
        dgo_ref[...] += dgo
        for h in range(GDN_HEADS):
            ds_ref[h] = dstates[h]

    tok = lambda b, i: (b, ng - 1 - i, 0)
    fixed = lambda b, i: (0, 0)
    lane_vec = jax.ShapeDtypeStruct((1, LANES), F32)
    row_vec = jax.ShapeDtypeStruct((SM_ROWS, 1), F32)
    return pl.pallas_call(
        body, name="gdn_bwd", grid=(n_batch, ng),
        in_specs=[pl.BlockSpec((1, gc, 3 * GDN_WIDTH), tok), pl.BlockSpec((1, gc, GDN_WIDTH), tok), pl.BlockSpec((1, gc, LANES), tok),
                  pl.BlockSpec((grp, SM_ROWS, c), lambda b, i: (b * ng + ng - 1 - i, 0, 0)),
                  pl.BlockSpec((1, LANES), fixed), pl.BlockSpec((1, LANES), fixed), pl.BlockSpec((SM_ROWS, 1), fixed),
                  pl.BlockSpec((SM_ROWS, 1), fixed), pl.BlockSpec((1, LANES), fixed),
                  pl.BlockSpec((1, 1, GDN_HEADS, hd, hd), lambda b, i: (b, ng - 1 - i, 0, 0, 0)),
                  pl.BlockSpec((1, 1, grp * GDN_HEADS, c, c), lambda b, i: (b, ng - 1 - i, 0, 0, 0)),
                  pl.BlockSpec((1, gc, GDN_WIDTH), lambda b, i: (b, ng - 1 - i, 1))],
        out_specs=[pl.BlockSpec((1, gc, 3 * GDN_WIDTH), tok), pl.BlockSpec((1, gc, GDN_WIDTH), tok), pl.BlockSpec((1, gc, LANES), tok),
                   pl.BlockSpec((grp, SM_ROWS, c), lambda b, i: (b * ng + ng - 1 - i, 0, 0)),
                   pl.BlockSpec((1, LANES), fixed), pl.BlockSpec((1, LANES), fixed), pl.BlockSpec((SM_ROWS, 1), fixed),
                   pl.BlockSpec((SM_ROWS, 1), fixed), pl.BlockSpec((1, LANES), fixed)],
        out_shape=[jax.ShapeDtypeStruct((n_batch, s_len, 3 * GDN_WIDTH), F32), jax.ShapeDtypeStruct((n_batch, s_len, GDN_WIDTH), BF16),
                   jax.ShapeDtypeStruct((n_batch, s_len, LANES), F32), jax.ShapeDtypeStruct((n_batch * n, SM_ROWS, c), F32),
                   lane_vec, lane_vec, row_vec, row_vec, lane_vec],
        scratch_shapes=[pltpu.VMEM((GDN_HEADS, hd, hd), F32)],
        compiler_params=_cparams(("arbitrary", "arbitrary")),
    )(qkvn, z, smc, smr, a_c, dt_c, a_r, dt_r, go, states, inverses, dog)


def _out_proj(x, oa, ob, w_out, g_x, w_cq, tm=512):
    t_len, d = x.shape
    tm = min(tm, t_len)

    def body(x_ref, oa_ref, ob_ref, wo_ref, g_ref, wq_ref, x1_ref, hq_ref, cq_ref):
        x1 = x_ref[...] + _dot(jnp.concatenate([oa_ref[...], ob_ref[...]], axis=1), wo_ref[...])
        x1_ref[...] = x1
        hq = _rms(x1, g_ref[...]).astype(BF16)
        hq_ref[...] = hq
        cq_ref[...] = _dot(hq, wq_ref[...])

    row = lambda i: (i, 0)
    fixed = lambda i: (0, 0)
    return pl.pallas_call(
        body, name="out_proj", grid=(t_len // tm,),
        in_specs=[pl.BlockSpec((tm, d), row), pl.BlockSpec((tm, FOX_WIDTH), row), pl.BlockSpec((tm, GDN_WIDTH), row),
                  _resident((d, d)), pl.BlockSpec((1, d), fixed), _resident((d, XATTN_WIDTH))],
        out_specs=[pl.BlockSpec((tm, d), row), pl.BlockSpec((tm, d), row), pl.BlockSpec((tm, XATTN_WIDTH), row)],
        out_shape=[jax.ShapeDtypeStruct((t_len, d), F32), jax.ShapeDtypeStruct((t_len, d), BF16), jax.ShapeDtypeStruct((t_len, XATTN_WIDTH), F32)],
        compiler_params=_cparams(("parallel",)),
    )(x, oa, ob, w_out, g_x, w_cq)


def _out_proj_bwd(dx1, w_out, tm=512):
    t_len, d = dx1.shape
    tm = min(tm, t_len)

    def body(dx_ref, w_ref, o_ref):
        o_ref[...] = _dot(dx_ref[...], w_ref[...], NT)

    return pl.pallas_call(
        body, name="out_proj_bwd", grid=(t_len // tm,),
        in_specs=[pl.BlockSpec((tm, d), lambda i: (i, 0)), pl.BlockSpec((d, d), lambda i: (0, 0))],
        out_specs=pl.BlockSpec((tm, d), lambda i: (i, 0)),
        out_shape=jax.ShapeDtypeStruct((t_len, d), F32),
        compiler_params=_cparams(("parallel",)),
    )(dx1, w_out)


def _mem_kv(mem, g, w_ckv, tm=256):
    t_len, d = mem.shape
    tm = min(tm, t_len)

    def body(x_ref, g_ref, w_ref, h_ref, o_ref):
        h = _rms(x_ref[...], g_ref[...]).astype(BF16)
        h_ref[...] = h
        o_ref[...] = _dot(h, w_ref[...])

    row = lambda i: (i, 0)
    fixed = lambda i: (0, 0)
    return pl.pallas_call(
        body, name="mem_kv", grid=(t_len // tm,),
        in_specs=[pl.BlockSpec((tm, d), row), pl.BlockSpec((1, d), fixed), pl.BlockSpec((d, 2 * XATTN_WIDTH), fixed)],
        out_specs=[pl.BlockSpec((tm, d), row), pl.BlockSpec((tm, 2 * XATTN_WIDTH), row)],
        out_shape=[jax.ShapeDtypeStruct((t_len, d), BF16), jax.ShapeDtypeStruct((t_len, 2 * XATTN_WIDTH), F32)],
        compiler_params=_cparams(("parallel",)),
    )(mem, g, w_ckv)


def _mem_kv_bwd(dckv, mem, g, w_ckv, tm=256):
    t_len, d = mem.shape
    tm = min(tm, t_len)

    def body(d_ref, x_ref, g_ref, w_ref, dg_ref):
        @pl.when(pl.program_id(0) == 0)
        def _():
            dg_ref[...] = jnp.zeros_like(dg_ref)

        dh = _dot(d_ref[...], w_ref[...], NT)
        _, dg = _rms_bwd(x_ref[...], g_ref[...], dh)
        dg_ref[...] += dg

    row = lambda i: (i, 0)
    fixed = lambda i: (0, 0)
    return pl.pallas_call(
        body, name="mem_kv_bwd", grid=(t_len // tm,),
        in_specs=[pl.BlockSpec((tm, 2 * XATTN_WIDTH), row), pl.BlockSpec((tm, d), row), pl.BlockSpec((1, d), fixed),
                  pl.BlockSpec((d, 2 * XATTN_WIDTH), fixed)],
        out_specs=pl.BlockSpec((1, d), fixed),
        out_shape=jax.ShapeDtypeStruct((1, d), F32),
        compiler_params=_cparams(("arbitrary",)),
    )(dckv, mem, g, w_ckv)


def _xattn_probs(qn, kn):
    s = _dot(qn, kn, NT) * (XATTN_HEAD_DIM ** -0.5)
    p = jnp.exp(s - jnp.max(s, axis=-1, keepdims=True))
    return p / jnp.sum(p, axis=-1, keepdims=True)


def _xattn_fwd(cq, ckv, x1, gq, gk, w_co, g_mlp, n_batch, s_len, m_len, tq=512):
    d = x1.shape[1]
    tq = min(tq, s_len)
    nq = s_len // tq
    hd = XATTN_HEAD_DIM

    def body(cq_ref, kv_ref, x1_ref, gq_ref, gk_ref, wo_ref, gm_ref, co_ref, x2_ref, hf_ref):
        outs = []
        for h in range(XATTN_HEADS):
            qn = _rms(cq_ref[:, h * hd:(h + 1) * hd], gq_ref[...])
            kn = _rms(kv_ref[:, h * hd:(h + 1) * hd], gk_ref[...])
            p = _xattn_probs(qn, kn)
            outs.append(_dot(p, kv_ref[:, XATTN_WIDTH + h * hd:XATTN_WIDTH + (h + 1) * hd]).astype(BF16))
        for h in range(XATTN_HEADS):
            co_ref[:, h * hd:(h + 1) * hd] = outs[h]
        x2 = x1_ref[...] + _dot(co_ref[...], wo_ref[...])
        x2_ref[...] = x2
        hf_ref[...] = _rms(x2, gm_ref[...]).astype(BF16)

    row = lambda b, i: (b * nq + i, 0)
    fixed = lambda b, i: (0, 0)
    t_len = n_batch * s_len
    return pl.pallas_call(
        body, name="xattn_fwd", grid=(n_batch, nq),
        in_specs=[pl.BlockSpec((tq, XATTN_WIDTH), row), pl.BlockSpec((m_len, 2 * XATTN_WIDTH), lambda b, i: (b, 0)),
                  pl.BlockSpec((tq, d), row), pl.BlockSpec((1, hd), fixed), pl.BlockSpec((1, hd), fixed),
                  pl.BlockSpec((XATTN_WIDTH, d), fixed), pl.BlockSpec((1, d), fixed)],
        out_specs=[pl.BlockSpec((tq, XATTN_WIDTH), row), pl.BlockSpec((tq, d), row), pl.BlockSpec((tq, d), row)],
        out_shape=[jax.ShapeDtypeStruct((t_len, XATTN_WIDTH), BF16), jax.ShapeDtypeStruct((t_len, d), F32),
                   jax.ShapeDtypeStruct((t_len, d), BF16)],
        compiler_params=_cparams(("parallel", "parallel")),
    )(cq, ckv, x1, gq, gk, w_co, g_mlp)


def _xattn_bwd(dx2, cq, ckv, x1, gq, gk, w_co, g_x, w_cq, n_batch, s_len, m_len, tq=512):
    d = x1.shape[1]
    tq = min(tq, s_len)
    nq = s_len // tq
    hd = XATTN_HEAD_DIM
    scale = XATTN_HEAD_DIM ** -0.5

    def body(dx2_ref, cq_ref, kv_ref, x1_ref, gq_ref, gk_ref, wo_ref, gx_ref, wq_ref,
             dx1_ref, dcq_ref, dkv_ref, dgq_ref, dgk_ref, dgx_ref, dk_acc, dv_acc):
        b = pl.program_id(0)
        i = pl.program_id(1)

        @pl.when((b == 0) & (i == 0))
        def _():
            dgq_ref[...] = jnp.zeros_like(dgq_ref)
            dgk_ref[...] = jnp.zeros_like(dgk_ref)
            dgx_ref[...] = jnp.zeros_like(dgx_ref)

        @pl.when(i == 0)
        def _():
            dk_acc[...] = jnp.zeros_like(dk_acc)
            dv_acc[...] = jnp.zeros_like(dv_acc)

        dx2 = dx2_ref[...]
        dco_all = _dot(dx2, wo_ref[...], NT)
        for h in range(XATTN_HEADS):
            sl = slice(h * hd, (h + 1) * hd)
            q = cq_ref[:, sl]
            qn = _rms(q, gq_ref[...])
            kn = _rms(kv_ref[:, sl], gk_ref[...])
            v = kv_ref[:, XATTN_WIDTH + h * hd:XATTN_WIDTH + (h + 1) * hd]
            p = _xattn_probs(qn, kn)
            dco = dco_all[:, sl]
            dv_acc[:, sl] += _dot(p, dco, TN)
            dp = _dot(dco, v, NT)
            ds = p * (dp - jnp.sum(dp * p, axis=-1, keepdims=True))
            dqn = _dot(ds, kn) * scale
            dk_acc[:, sl] += _dot(ds, qn, TN) * scale
            dq, dgq = _rms_bwd(q, gq_ref[...], dqn)
            dgq_ref[...] += dgq
            dcq_ref[:, sl] = dq.astype(BF16)
        dhq = _dot(dcq_ref[...], wq_ref[...], NT)
        dxn, dgx = _rms_bwd(x1_ref[...], gx_ref[...], dhq)
        dgx_ref[...] += dgx
        dx1_ref[...] = dx2 + dxn

        @pl.when(i == nq - 1)
        def _():
            for h in range(XATTN_HEADS):
                sl = slice(h * hd, (h + 1) * hd)
                dk, dgk = _rms_bwd(kv_ref[:, sl], gk_ref[...], dk_acc[:, sl])
                dgk_ref[...] += dgk
                dkv_ref[:, sl] = dk.astype(BF16)
                dkv_ref[:, XATTN_WIDTH + h * hd:XATTN_WIDTH + (h + 1) * hd] = dv_acc[:, sl].astype(BF16)

    row = lambda b, i: (b * nq + i, 0)
    fixed = lambda b, i: (0, 0)
    t_len = n_batch * s_len
    return pl.pallas_call(
        body, name="xattn_bwd", grid=(n_batch, nq),
        in_specs=[pl.BlockSpec((tq, d), row), pl.BlockSpec((tq, XATTN_WIDTH), row), pl.BlockSpec((m_len, 2 * XATTN_WIDTH), lambda b, i: (b, 0)),
                  pl.BlockSpec((tq, d), row), pl.BlockSpec((1, hd), fixed), pl.BlockSpec((1, hd), fixed),
                  pl.BlockSpec((XATTN_WIDTH, d), fixed), pl.BlockSpec((1, d), fixed), pl.BlockSpec((d, XATTN_WIDTH), fixed)],
        out_specs=[pl.BlockSpec((tq, d), row), pl.BlockSpec((tq, XATTN_WIDTH), row), pl.BlockSpec((m_len, 2 * XATTN_WIDTH), lambda b, i: (b, 0)),
                   pl.BlockSpec((1, hd), fixed), pl.BlockSpec((1, hd), fixed), pl.BlockSpec((1, d), fixed)],
        out_shape=[jax.ShapeDtypeStruct((t_len, d), F32), jax.ShapeDtypeStruct((t_len, XATTN_WIDTH), BF16),
                   jax.ShapeDtypeStruct((n_batch * m_len, 2 * XATTN_WIDTH), BF16),
                   jax.ShapeDtypeStruct((1, hd), F32), jax.ShapeDtypeStruct((1, hd), F32), jax.ShapeDtypeStruct((1, d), F32)],
        scratch_shapes=[pltpu.VMEM((m_len, XATTN_WIDTH), F32), pltpu.VMEM((m_len, XATTN_WIDTH), F32)],
        compiler_params=_cparams(("arbitrary", "arbitrary")),
    )(dx2, cq, ckv, x1, gq, gk, w_co, g_x, w_cq)


def _resident(shape):
    return pl.BlockSpec(shape, lambda *_: (0,) * len(shape), pipeline_mode=pl.Buffered(1))


def _mlp_fwd(hf, x2, target, w1, w2, tm=256, tf=1024):
    t_len, d = x2.shape
    f = w1.shape[1]
    tm, tf = min(tm, t_len), min(tf, f)

    def body(hf_ref, x2_ref, tg_ref, w1_ref, w2_ref, u_ref, a_ref, dy_ref, ls_ref):
        hf_t = hf_ref[...]
        for k in range(f // tf):
            cols = slice(k * tf, (k + 1) * tf)
            u = _dot(hf_t, w1_ref[:, cols])
            u_ref[:, cols] = u
            r = jnp.maximum(u, 0.0)
            a_ref[:, cols] = (r * r).astype(BF16)
        y = x2_ref[...] + _dot(a_ref[...], w2_ref[...])
        err = y - tg_ref[...]
        dy_ref[...] = err * (1.0 / d)
        ls_ref[...] = jnp.broadcast_to(jnp.sum(jnp.sum(err * err, axis=-1, keepdims=True) * (1.0 / d), axis=0, keepdims=True), ls_ref.shape)

    row = lambda i: (i, 0)
    return pl.pallas_call(
        body, name="mlp_fwd", grid=(t_len // tm,),
        in_specs=[pl.BlockSpec((tm, d), row), pl.BlockSpec((tm, d), row), pl.BlockSpec((tm, d), row), _resident((d, f)), _resident((f, d))],
        out_specs=[pl.BlockSpec((tm, f), row), pl.BlockSpec((tm, f), row), pl.BlockSpec((tm, d), row),
                   pl.BlockSpec((1, 8, LANES), lambda i: (i, 0, 0))],
        out_shape=[jax.ShapeDtypeStruct((t_len, f), F32), jax.ShapeDtypeStruct((t_len, f), BF16), jax.ShapeDtypeStruct((t_len, d), F32),
                   jax.ShapeDtypeStruct((t_len // tm, 8, LANES), F32)],
        compiler_params=_cparams(("parallel",)),
    )(hf, x2, target, w1, w2)


def _mlp_bwd(dy, u, x2, g, w1, w2, tm=256, tf=1024):
    t_len, d = x2.shape
    f = w1.shape[1]
    tm, tf = min(tm, t_len), min(tf, f)

    def body(dy_ref, u_ref, x2_ref, g_ref, w1_ref, w2_ref, du_ref, dx2_ref, dg_ref):
        @pl.when(pl.program_id(0) == 0)
        def _():
            dg_ref[...] = jnp.zeros_like(dg_ref)

        dy_t = dy_ref[...]
        dyb = dy_t.astype(BF16)
        for k in range(f // tf):
            cols = slice(k * tf, (k + 1) * tf)
            da = _dot(dyb, w2_ref[cols, :], NT)
            du_ref[:, cols] = (da * (2.0 * jnp.maximum(u_ref[:, cols], 0.0))).astype(BF16)
        dhf = _dot(du_ref[...], w1_ref[...], NT)
        dxn, dg = _rms_bwd(x2_ref[...], g_ref[...], dhf)
        dx2_ref[...] = dy_t + dxn
        dg_ref[...] += dg

    row = lambda i: (i, 0)
    fixed = lambda i: (0, 0)
    return pl.pallas_call(
        body, name="mlp_bwd", grid=(t_len // tm,),
        in_specs=[pl.BlockSpec((tm, d), row), pl.BlockSpec((tm, f), row), pl.BlockSpec((tm, d), row), pl.BlockSpec((1, d), fixed),
                  _resident((d, f)), _resident((f, d))],
        out_specs=[pl.BlockSpec((tm, f), row), pl.BlockSpec((tm, d), row), pl.BlockSpec((1, d), fixed)],
        out_shape=[jax.ShapeDtypeStruct((t_len, f), BF16), jax.ShapeDtypeStruct((t_len, d), F32), jax.ShapeDtypeStruct((1, d), F32)],
        compiler_params=_cparams(("arbitrary",)),
    )(dy, u, x2, g, w1, w2)


def _pad_lanes(v, offset=0, width=LANES):
    return jnp.zeros((1, width), F32).at[:, offset:offset + v.shape[1]].set(v)


def _col(v, offset=0, rows=SM_ROWS):
    return jnp.zeros((rows, 1), F32).at[offset:offset + v.shape[1], 0].set(v[0])


def _pack_small(g_mix, dgq, dgk, dbias, dgo, dac, dar, ddc, ddr, g_gdn_o, g_nx, g_mem, g_xq, g_xk, g_mlp, loss_tiles):
    def body(mix_ref, q_ref, k_ref, b_ref, o_ref, ac_ref, ar_ref, dc_ref, dr_ref, go_ref, nx_ref, mem_ref, xq_ref, xk_ref,
             mlp_ref, lt_ref, out_ref):
        lane = lax.broadcasted_iota(jnp.int32, (1, LANES), 1)
        diag = lax.broadcasted_iota(jnp.int32, (SM_ROWS, LANES), 0) == lax.broadcasted_iota(jnp.int32, (SM_ROWS, LANES), 1)

        def rolled(v, shift):
            return pltpu.roll(jnp.broadcast_to(v, (8, LANES)), shift, 1)[0:1, :]

        def rows_to_lanes(col):
            return jnp.sum(jnp.where(diag, col, 0.0), axis=0, keepdims=True)

        def put(row, v, n):
            out_ref[row:row + 1, 0:LANES] = jnp.where(lane < n, v, 0.0)

        out_ref[...] = jnp.zeros_like(out_ref)
        out_ref[0:1, :] = mix_ref[...]
        for row, ref in ((1, q_ref), (2, k_ref), (4, o_ref)):
            put(row, ref[...] + rolled(ref[...], FOX_HEAD_DIM), FOX_HEAD_DIM)
        put(3, rows_to_lanes(b_ref[...]), FOX_HEADS)
        for row, lane_ref, row_ref in ((5, ac_ref, ar_ref), (6, dc_ref, dr_ref)):
            put(row, rolled(lane_ref[...] + rows_to_lanes(row_ref[...]), LANES - SM_A), GDN_HEADS)
        put(7, go_ref[...], LANES)
        out_ref[8:9, :] = nx_ref[...]
        out_ref[9:10, :] = mem_ref[...]
        put(10, xq_ref[...], LANES)
        put(11, xk_ref[...], LANES)
        out_ref[12:13, :] = mlp_ref[...]
        put(LOSS_ROW, 0.5 * jnp.sum(lt_ref[...], axis=0)[0:1, :], 1)

    args = (g_mix, dgq, dgk, dbias, dgo, dac, dar, ddc, ddr, g_gdn_o, g_nx, g_mem, g_xq, g_xk, g_mlp, loss_tiles)
    return pl.pallas_call(body, name="pack_small", out_shape=jax.ShapeDtypeStruct((PACK_ROWS, D_MODEL), F32))(*args)


LATE_WEIGHTS = (("w_out", "w_cq", "w_ckv", "w_co"), ("w_mlp1", "w_mlp2"))
GRAD_GROUPS = (("w_mlp2", "w_mlp1"), ("w_co", "w_cq", "w_ckv", "w_out"), ("w_in", "gdn_conv_w"))


def _local_step(x, mem, target, norm_mix_g, w_in, fox_qnorm_g, fox_knorm_g, fox_f_bias, fox_onorm_g, gdn_conv_w, gdn_A_log,
                gdn_dt_bias, gdn_onorm_g, norm_xattn_g, mem_norm_g, xattn_qnorm_g, xattn_knorm_g, norm_mlp_g,
                late_weights, grads_ready=None, first_token=0.0):
    if grads_ready is None:
        grads_ready = lambda group: 0.0
    n_batch, s_len, d = x.shape
    m_len = mem.shape[1]
    t_len = n_batch * s_len
    tq = min(FOX_BLOCK, s_len)
    nq = s_len // tq
    n_chunks = s_len // GDN_CHUNK
    x2d = x.reshape(t_len, d)

    wp = jnp.concatenate([w_in[0:1536], w_in[1544:3080], w_in[3088:3600], w_in[1536:1544], w_in[3080:3088],
                          jnp.zeros((P_DIM - 3600, d), BF16)], axis=0)
    wst = jnp.concatenate([w_in[1536:1544], w_in[3080:3088]], axis=0)
    conv_w = jnp.concatenate([gdn_conv_w, jnp.zeros((8 - CONV_WIDTH, gdn_conv_w.shape[1]), F32)], axis=0)
    bias_col = _col(fox_f_bias, SM_F)
    gq2, gk2, go2 = (jnp.tile(g, (1, 2)) for g in (fox_qnorm_g, fox_knorm_g, fox_onorm_g))
    a_c, dt_c = _pad_lanes(gdn_A_log, SM_A), _pad_lanes(gdn_dt_bias, SM_A)
    a_r, dt_r = _col(gdn_A_log, SM_A), _col(gdn_dt_bias, SM_A)

    h1, pfox, pgdn, pz, sm, smt = _in_proj(x2d, norm_mix_g + first_token, wp, wst)
    c_rows = _fox_cum(smt, bias_col, n_batch, s_len)
    cb = c_rows.reshape(SM_ROWS, n_batch, nq, tq).transpose(1, 2, 0, 3)
    pf3 = pfox.reshape(n_batch, s_len, 1536)
    o_fox, oa, lse = _fox_fwd(pf3, cb, gq2, gk2, go2, tq)
    pg3 = pgdn.reshape(n_batch, s_len, 1536)
    qkvn = _gdn_pre(pg3, conv_w)
    z3 = pz.reshape(n_batch, s_len, GDN_WIDTH)
    smc = sm.reshape(n_batch, s_len, LANES)
    smr = smt.reshape(SM_ROWS, n_batch * n_chunks, GDN_CHUNK).transpose(1, 0, 2)
    ob, states, inverses = _gdn_fwd(qkvn, z3, smc, smr, a_c, dt_c, a_r, dt_r, gdn_onorm_g)
    oa2, ob2 = oa.reshape(t_len, FOX_WIDTH), ob.reshape(t_len, GDN_WIDTH)
    w_out, w_cq, w_ckv, w_co = late_weights(LATE_WEIGHTS[0], ob2)
    x1, hq, cq = _out_proj(x2d, oa2, ob2, w_out, norm_xattn_g, w_cq)
    mem2d = mem.reshape(n_batch * m_len, d)
    hm, ckv = _mem_kv(mem2d, mem_norm_g, w_ckv)
    co, x2, hf = _xattn_fwd(cq, ckv, x1, xattn_qnorm_g, xattn_knorm_g, w_co, norm_mlp_g, n_batch, s_len, m_len)
    w_mlp1, w_mlp2 = late_weights(LATE_WEIGHTS[1], hf)
    u, a_act, dy, loss_tiles = _mlp_fwd(hf, x2, target.reshape(t_len, d), w_mlp1, w_mlp2)

    grads = {}
    du, dx2, grads["norm_mlp_g"] = _mlp_bwd(dy, u, x2, norm_mlp_g, w_mlp1, w_mlp2)
    grads["w_mlp2"] = _wgrad(a_act, dy, "wgrad_mlp2", bt=2048)
    grads["w_mlp1"] = _wgrad(hf, du, "wgrad_mlp1", bt=2048, column_blocks=D_FF // N_DEV)
    token = grads_ready({k: grads[k] for k in GRAD_GROUPS[0]})
    grads["w_co"] = _wgrad(co, dx2, "wgrad_co", column_blocks=D_MODEL // N_DEV)
    dx1, dcq, dckv, grads["xattn_qnorm_g"], grads["xattn_knorm_g"], grads["norm_xattn_g"] = _xattn_bwd(
        dx2, cq, ckv, x1, xattn_qnorm_g + token, xattn_knorm_g, w_co, norm_xattn_g, w_cq, n_batch, s_len, m_len)
    grads["w_cq"] = _wgrad(hq, dcq, "wgrad_cq")
    grads["w_ckv"] = _wgrad(hm, dckv, "wgrad_ckv")
    grads["mem_norm_g"] = _mem_kv_bwd(dckv, mem2d, mem_norm_g, w_ckv)
    grads["w_out"] = _wgrad_stacked([oa2, ob2], dx1, "wgrad_out", bn=1024)
    token = grads_ready({k: grads[k] for k in GRAD_GROUPS[1]})
    dcat = _out_proj_bwd(dx1, w_out)
    dcat3 = dcat.reshape(n_batch, s_len, d)

    dqkvn, dz, dsmc, dsmr, dac, ddc, dar, ddr, grads["gdn_onorm_g"] = _gdn_bwd(
        qkvn, z3, smc, smr, a_c, dt_c, a_r, dt_r, gdn_onorm_g + token, states, inverses, dcat3)
    dpg, dconv = _gdn_pre_bwd(pg3, conv_w, dqkvn)
    grads["gdn_conv_w"] = dconv[0:CONV_WIDTH]

    dq, dk, dv, dcb, dgq, dgk, dgo = _fox_bwd(pf3, cb, gq2, gk2, go2, o_fox, lse, dcat3, tq)
    dc8 = dcb[:, :, :, 0:2, :].transpose(1, 3, 0, 2, 4).reshape(FOX_HEADS, t_len)
    dc_rows = jnp.concatenate([dc8, jnp.zeros((SM_ROWS - FOX_HEADS, t_len), F32)], axis=0)
    dl_rows, dbias = _fox_cum_bwd(dc_rows, smt, bias_col, n_batch, s_len)
    dsm_rows = jnp.concatenate([dl_rows[0:SM_B], dsmr.transpose(1, 0, 2).reshape(SM_ROWS, t_len)[SM_B:SM_ROWS]], axis=0)

    dprojs = [dq.reshape(t_len, FOX_WIDTH), dk.reshape(t_len, FOX_WIDTH), dv.reshape(t_len, FOX_WIDTH),
              dpg.reshape(t_len, 1536), dz.reshape(t_len, GDN_WIDTH), dsmc.reshape(t_len, LANES)]
    dwp = _wgrad_stacked(dprojs, h1, "wgrad_in")
    dwst = _rows_matmul(dsm_rows, h1, "wgrad_in_rows")
    dw_small = dwp[P_SMALL:P_SMALL + SM_ROWS] + dwst
    grads["w_in"] = jnp.concatenate([dwp[0:1536], dw_small[0:8], dwp[1536:3072], dw_small[8:16], dwp[3072:3584]], axis=0)
    token = grads_ready({k: grads[k] for k in GRAD_GROUPS[2]})
    grad_x, grads["norm_mix_g"] = _in_proj_bwd(dprojs, dsm_rows, x2d, norm_mix_g + token, wp, wst, dx1)
    packed = _pack_small(grads["norm_mix_g"], dgq, dgk, dbias, dgo, dac, dar, ddc, ddr, grads["gdn_onorm_g"], grads["norm_xattn_g"],
                         grads["mem_norm_g"], grads["xattn_qnorm_g"], grads["xattn_knorm_g"], grads["norm_mlp_g"], loss_tiles)
    return packed, grad_x.reshape(n_batch, s_len, d), {k: grads[k] for k in SHARDED}


MESH_ID = pl.DeviceIdType.MESH
ANY_SPEC = pl.BlockSpec(memory_space=pl.ANY)


def _place():
    x, y, c = lax.axis_index("x"), lax.axis_index("y"), lax.axis_index("c")
    return x, y, c, [(1 - x, y), (x, 1 - y), (1 - x, 1 - y)]


def _place_own(src_ref, dst_ref):
    def staged(buf, sem):
        for a, b in ((src_ref, buf), (buf, dst_ref)):
            cp = pltpu.make_async_copy(a, b, sem)
            cp.start()
            cp.wait()

    pl.run_scoped(staged, pltpu.VMEM(src_ref.shape, src_ref.dtype), pltpu.SemaphoreType.DMA)


def _all_gather_body(n, ins, outs, send_sems, recv_sems):
    x, y, c, chips = _place()
    me, sibling = (x, y, c), (x, y, 1 - c)

    def copy(a, k, block, to, src=None):
        dst = outs[a].at[4 * block[0] + 2 * block[1] + block[2]]
        return pltpu.make_async_remote_copy(src_ref=dst if src is None else src, dst_ref=dst, send_sem=send_sems.at[a, k],
                                            recv_sem=recv_sems.at[a, k], device_id=to, device_id_type=MESH_ID)

    first = []
    for a in range(n):
        first.append(copy(a, 0, me, sibling, src=ins[a]))
        first += [copy(a, 1 + j, me, (*chip, c), src=ins[a]) for j, chip in enumerate(chips)]
    for cp in first:
        cp.start()
    for a in range(n):
        _place_own(ins[a], outs[a].at[4 * x + 2 * y + c])
    passed = []
    for j, chip in enumerate(chips):
        for a in range(n):
            copy(a, 1 + j, (*chip, c), me).wait_recv()
            fwd = copy(a, 4 + j, (*chip, c), sibling)
            fwd.start()
            passed.append(fwd)
    for a in range(n):
        copy(a, 0, sibling, me).wait_recv()
        for j, chip in enumerate(chips):
            copy(a, 4 + j, (*chip, 1 - c), me).wait_recv()
    for cp in first + passed:
        cp.wait_send()


def _all_gather_hbm(arrs, name):
    n = len(arrs)

    def body(*refs):
        _all_gather_body(n, refs[:n], refs[n:2 * n], refs[2 * n], refs[2 * n + 1])

    return pl.pallas_call(
        body, name=name, in_specs=[ANY_SPEC] * n, out_specs=[ANY_SPEC] * n,
        out_shape=[jax.ShapeDtypeStruct((N_DEV,) + a.shape, a.dtype) for a in arrs],
        scratch_shapes=[pltpu.SemaphoreType.DMA((n, 7)), pltpu.SemaphoreType.DMA((n, 7))],
        compiler_params=pltpu.CompilerParams(vmem_limit_bytes=VMEM_LIMIT),
    )(*arrs)


def _pair_exchange(arrs, name):
    n = len(arrs)

    def body(*refs):
        ins, outs = refs[:n], refs[n:2 * n]
        send_sems, recv_sems = refs[2 * n:]
        x, y, c, _ = _place()
        copies = []
        for a in range(n):
            for chip in range(4):
                copies.append(pltpu.make_async_remote_copy(
                    src_ref=ins[a].at[2 * chip + (1 - c)], dst_ref=outs[a].at[chip], send_sem=send_sems.at[a, chip],
                    recv_sem=recv_sems.at[a, chip], device_id=(x, y, 1 - c), device_id_type=MESH_ID))
        for cp in copies:
            cp.start()
        for cp in copies:
            cp.wait()

    return pl.pallas_call(
        body, name=name, in_specs=[ANY_SPEC] * n, out_specs=[ANY_SPEC] * n,
        out_shape=[jax.ShapeDtypeStruct((4,) + a.shape[1:], a.dtype) for a in arrs],
        scratch_shapes=[pltpu.SemaphoreType.DMA((n, 4)), pltpu.SemaphoreType.DMA((n, 4))],
    )(*arrs)


HBM_SPEC = pl.BlockSpec(memory_space=pltpu.HBM)
SEM_SPEC = pl.BlockSpec(memory_space=pltpu.SEMAPHORE)
DATAFLOW = pltpu.SideEffectType.DATAFLOW_SIDE_EFFECTING


def _in_hbm(arrs):
    return [pltpu.with_memory_space_constraint(a, pltpu.HBM) for a in arrs]


def _copies_start(name, srcs, lands, make_copies, after):
    n = len(srcs)
    n_copies = len(make_copies(srcs, lands, None, None)[0])

    def body(*refs):
        send_sems, recv_sems = refs[2 * n + 1], refs[2 * n + 2]
        for row in make_copies(refs[:n], refs[n:2 * n], send_sems, recv_sems):
            for cp in row:
                cp.start()
        refs[-1][...] = jnp.zeros_like(refs[-1])

    sems = pltpu.SemaphoreType.DMA((n * n_copies,))
    thru = [pltpu.HBM(a.shape, a.dtype) for a in list(srcs) + list(lands)]
    res = pl.pallas_call(
        body, name=name, in_specs=[HBM_SPEC] * (2 * n) + [ANY_SPEC],
        out_specs=(SEM_SPEC, SEM_SPEC, *[HBM_SPEC] * (2 * n), pl.BlockSpec(memory_space=pltpu.VMEM)),
        out_shape=(sems, sems, *thru, jax.ShapeDtypeStruct((8, LANES), F32)),
        input_output_aliases={i: 2 + i for i in range(2 * n)},
        compiler_params=pltpu.CompilerParams(has_side_effects=DATAFLOW),
    )(*_in_hbm(list(srcs) + list(lands)), after)
    return res[0], res[1], list(res[2:2 + n]), list(res[2 + n:2 + 2 * n]), res[-1]


def _copies_wait(name, send_sems, recv_sems, srcs, lands, after, make_copies, own_block=False):
    n = len(srcs)

    def body(*refs):
        if own_block:
            for a in range(n):
                _place_own(refs[a], _own_part(refs[a], refs[3 * n + 3 + a]))
        for row in make_copies(refs[:n], refs[n:2 * n], refs[2 * n], refs[2 * n + 1]):
            for cp in row:
                cp.wait_send()
                cp.wait_recv()

    res = pl.pallas_call(
        body, name=name, in_specs=[HBM_SPEC] * (2 * n) + [SEM_SPEC, SEM_SPEC, ANY_SPEC],
        out_specs=tuple([HBM_SPEC] * (2 * n)),
        out_shape=tuple(pltpu.HBM(a.shape, a.dtype) for a in list(srcs) + list(lands)),
        input_output_aliases={i: i for i in range(2 * n)},
        compiler_params=pltpu.CompilerParams(has_side_effects=DATAFLOW, vmem_limit_bytes=VMEM_LIMIT),
    )(*srcs, *lands, send_sems, recv_sems, after)
    return list(res[:n]), list(res[n:])


def _own_part(src_ref, land_ref):
    me = 4 * lax.axis_index("x") + 2 * lax.axis_index("y") + lax.axis_index("c")
    rows, cols = src_ref.shape
    if land_ref.shape[0] == N_DEV * rows:
        return land_ref.at[pl.ds(pl.multiple_of(me * rows, rows), rows), :]
    return land_ref.at[:, pl.ds(pl.multiple_of(me * cols, cols), cols)]


def _gather_copies(srcs, lands, send_sems, recv_sems):
    if send_sems is None:
        return [[None] * 7]
    x, y, c, _ = _place()
    rows = []
    for a in range(len(srcs)):
        row = []
        for k in range(7):
            r = k + 1
            to = (1 - x if r & 4 else x, 1 - y if r & 2 else y, 1 - c if r & 1 else c)
            row.append(pltpu.make_async_remote_copy(
                src_ref=srcs[a], dst_ref=_own_part(srcs[a], lands[a]), send_sem=send_sems.at[7 * a + k], recv_sem=recv_sems.at[7 * a + k],
                device_id=to, device_id_type=MESH_ID))
        rows.append(row)
    return rows


def _scatter_copies(srcs, lands, send_sems, recv_sems):
    if send_sems is None:
        return [[None] * 7]
    x, y, c, _ = _place()
    rows = []
    for a in range(len(srcs)):
        row = []
        for k in range(7):
            r = k + 1
            to = (1 - x if r & 4 else x, 1 - y if r & 2 else y, 1 - c if r & 1 else c)
            row.append(pltpu.make_async_remote_copy(
                src_ref=srcs[a].at[4 * to[0] + 2 * to[1] + to[2]], dst_ref=lands[a].at[k], send_sem=send_sems.at[7 * a + k],
                recv_sem=recv_sems.at[7 * a + k], device_id=to, device_id_type=MESH_ID))
        rows.append(row)
    return rows


def _chip_copies(srcs, lands, send_sems, recv_sems):
    if send_sems is None:
        return [[None] * 3]
    x, y, c, chips = _place()
    return [[pltpu.make_async_remote_copy(
        src_ref=srcs[a].at[2 * chip[0] + chip[1]], dst_ref=lands[a].at[j], send_sem=send_sems.at[3 * a + j], recv_sem=recv_sems.at[3 * a + j],
        device_id=(*chip, c), device_id_type=MESH_ID) for j, chip in enumerate(chips)] for a in range(len(srcs))]


def _tile(rows, cols):
    if rows <= 256:
        return rows, cols
    tr = 256 if cols <= 512 else 128
    if rows % tr == 0:
        return tr, cols
    return rows, 512


def _pair_sum(core, own, got, name):
    _, rows, cols = own.shape
    tr, tc = _tile(rows, cols)

    def body(c_ref, own_ref, got_ref, o_ref):
        o_ref[0] = own_ref[0] + got_ref[0]

    return pl.pallas_call(
        body, name=name,
        grid_spec=pltpu.PrefetchScalarGridSpec(
            num_scalar_prefetch=1, grid=(4, rows // tr, cols // tc),
            in_specs=[pl.BlockSpec((1, tr, tc), lambda k, i, j, c: (2 * k + c[0], i, j)),
                      pl.BlockSpec((1, tr, tc), lambda k, i, j, c: (k, i, j))],
            out_specs=pl.BlockSpec((1, tr, tc), lambda k, i, j, c: (k, i, j))),
        out_shape=jax.ShapeDtypeStruct((4, rows, cols), F32),
        compiler_params=_cparams(("parallel", "parallel", "parallel")),
    )(core, own, got)


def _adamw(w, g, m, v):
    m_new = ADAM_B1 * m + (1.0 - ADAM_B1) * g
    v_new = ADAM_B2 * v + (1.0 - ADAM_B2) * (g * g)
    m_hat = m_new / (1.0 - ADAM_B1 ** ADAM_STEP)
    v_hat = v_new / (1.0 - ADAM_B2 ** ADAM_STEP)
    delta = -ADAM_LR * (m_hat / (jnp.sqrt(v_hat) + ADAM_EPS) + ADAM_WD * w)
    return delta, m_new, v_new


def _sum_adam(chip, sums, parts, w, m, v, name):
    n_parts, rows, cols = parts.shape
    tr, tc = _tile(rows, cols)

    def body(chip_ref, own_ref, p_ref, w_ref, m_ref, v_ref, g_ref, d_ref, mo_ref, vo_ref):
        g = own_ref[0]
        for k in range(n_parts):
            g = g + p_ref[k]
        g_ref[...] = g
        d_ref[...], mo_ref[...], vo_ref[...] = _adamw(w_ref[...], g, m_ref[...], v_ref[...])

    tile = pl.BlockSpec((tr, tc), lambda i, j, ch: (i, j))
    out = jax.ShapeDtypeStruct((rows, cols), F32)
    return pl.pallas_call(
        body, name=name,
        grid_spec=pltpu.PrefetchScalarGridSpec(
            num_scalar_prefetch=1, grid=(rows // tr, cols // tc),
            in_specs=[pl.BlockSpec((1, tr, tc), lambda i, j, ch: (ch[0], i, j)),
                      pl.BlockSpec((n_parts, tr, tc), lambda i, j, ch: (0, i, j)), tile, tile, tile],
            out_specs=[tile, tile, tile, tile]),
        out_shape=[out, out, out, out],
        compiler_params=_cparams(("parallel", "parallel")),
    )(chip, sums, parts, w, m, v)


SHARDED = ("w_in", "gdn_conv_w", "w_out", "w_cq", "w_ckv", "w_co", "w_mlp1", "w_mlp2")
TRANSPOSED = ("w_in",)
COLUMN_SHARDED = ("gdn_conv_w", "w_co", "w_mlp1")
REPLICATED = ("norm_mix_g", "fox_qnorm_g", "fox_knorm_g", "fox_f_bias", "fox_onorm_g", "gdn_A_log", "gdn_dt_bias", "gdn_onorm_g",
              "norm_xattn_g", "mem_norm_g", "xattn_qnorm_g", "xattn_knorm_g", "norm_mlp_g")
WEIGHTS = ("norm_mix_g", "w_in", "fox_qnorm_g", "fox_knorm_g", "fox_f_bias", "fox_onorm_g", "gdn_conv_w", "gdn_A_log", "gdn_dt_bias",
           "gdn_onorm_g", "w_out", "norm_xattn_g", "mem_norm_g", "w_cq", "w_ckv", "xattn_qnorm_g", "xattn_knorm_g", "w_co",
           "norm_mlp_g", "w_mlp1", "w_mlp2")
PACK_ROWS = 16
LOSS_ROW = len(REPLICATED)


def _whole(name, gathered):
    if name in COLUMN_SHARDED:
        return gathered.transpose(1, 0, 2).reshape(gathered.shape[1], N_DEV * gathered.shape[2])
    return gathered.reshape(N_DEV * gathered.shape[1], gathered.shape[2])


def _whole_shape(name, shard_shape):
    rows, cols = shard_shape
    return (rows, N_DEV * cols) if name in COLUMN_SHARDED else (N_DEV * rows, cols)


def _blocks(name, whole):
    if whole.ndim == 3:
        return whole
    if name in COLUMN_SHARDED:
        rows, cols = whole.shape
        return whole.reshape(rows, N_DEV, cols // N_DEV).transpose(1, 0, 2)
    return whole.reshape(N_DEV, whole.shape[0] // N_DEV, whole.shape[1])


def _adam_small(everyone, ws, ms, vs):
    n_par = len(ws)

    def body(*refs):
        ev_ref = refs[0]
        w_refs, m_refs, v_refs = (refs[1 + j * n_par:1 + (j + 1) * n_par] for j in range(3))
        outs = refs[1 + 3 * n_par:-1]
        sum_ref = refs[-1]
        total = ev_ref[0]
        for dev in range(1, N_DEV):
            total = total + ev_ref[dev]
        sum_ref[...] = total
        for i in range(n_par):
            n = w_refs[i].shape[1]
            g = sum_ref[i:i + 1, 0:n]
            outs[4 * i][...] = g
            outs[4 * i + 1][...], outs[4 * i + 2][...], outs[4 * i + 3][...] = _adamw(w_refs[i][...], g, m_refs[i][...], v_refs[i][...])
        outs[4 * n_par][...] = sum_ref[LOSS_ROW:LOSS_ROW + 1, 0:1]

    shapes = [jax.ShapeDtypeStruct(a.shape, F32) for a in ws for _ in range(4)] + [jax.ShapeDtypeStruct((1, 1), F32)]
    return pl.pallas_call(body, name="adam_small", out_shape=shapes,
                          scratch_shapes=[pltpu.VMEM((PACK_ROWS, D_MODEL), F32)])(everyone, *ws, *ms, *vs)


def kernel(x, mem, norm_mix_g, w_in, fox_qnorm_g, fox_knorm_g, fox_f_bias, fox_onorm_g, gdn_conv_w, gdn_A_log, gdn_dt_bias, gdn_onorm_g, w_out, norm_xattn_g, mem_norm_g, w_cq, w_ckv, xattn_qnorm_g, xattn_knorm_g, w_co, norm_mlp_g, w_mlp1, w_mlp2, loss_target, m_norm_mix_g, m_w_in, m_fox_qnorm_g, m_fox_knorm_g, m_fox_f_bias, m_fox_onorm_g, m_gdn_conv_w, m_gdn_A_log, m_gdn_dt_bias, m_gdn_onorm_g, m_w_out, m_norm_xattn_g, m_mem_norm_g, m_w_cq, m_w_ckv, m_xattn_qnorm_g, m_xattn_knorm_g, m_w_co, m_norm_mlp_g, m_w_mlp1, m_w_mlp2, v_norm_mix_g, v_w_in, v_fox_qnorm_g, v_fox_knorm_g, v_fox_f_bias, v_fox_onorm_g, v_gdn_conv_w, v_gdn_A_log, v_gdn_dt_bias, v_gdn_onorm_g, v_w_out, v_norm_xattn_g, v_mem_norm_g, v_w_cq, v_w_ckv, v_xattn_qnorm_g, v_xattn_knorm_g, v_w_co, v_norm_mlp_g, v_w_mlp1, v_w_mlp2):
    given = dict(locals())
    w = {k: given[k] for k in WEIGHTS}
    m = {k: given["m_" + k] for k in WEIGHTS}
    v = {k: given["v_" + k] for k in WEIGHTS}

    core = lax.axis_index("c").astype(jnp.int32).reshape(1)
    chip = (2 * lax.axis_index("x") + lax.axis_index("y")).astype(jnp.int32).reshape(1)
    me = 4 * lax.axis_index("x") + 2 * lax.axis_index("y") + lax.axis_index("c")

    local = lambda d: {k: jnp.transpose(d[k][0]) if k in TRANSPOSED else d[k][0] for k in SHARDED}
    w2, m2, v2 = local(w), local(m), local(v)
    shards = {k: w2[k] if k == "gdn_conv_w" else w2[k].astype(BF16) for k in SHARDED}
    early = [k for k in SHARDED if not any(k in group for group in LATE_WEIGHTS)]
    gathered = _all_gather_hbm([shards[k] for k in early], "gather_early")
    whole = {k: _whole(k, g) for k, g in zip(early, gathered)}
    gathers, after = {}, gathered[0]
    for i, group in enumerate(LATE_WEIGHTS):
        lands = [lax.empty(_whole_shape(k, shards[k].shape), BF16) for k in group]
        gathers[group] = _copies_start("gather_late_start_" + str(i), [shards[k] for k in group], lands, _gather_copies, after=after)
        after = gathers[group][4]
    first_token = after[0, 0]

    def late_weights(group, after):
        gather = gathers[group]
        _, lands = _copies_wait("gather_late_wait_" + str(LATE_WEIGHTS.index(group)), gather[0], gather[1], gather[2], gather[3],
                                after, _gather_copies, own_block=True)
        return lands

    pending = []

    def grads_ready(group):
        names = list(group)
        tag = str(len(pending))
        own = [_blocks(k, group[k]) for k in names]
        if "w_in" in names:
            got = _pair_exchange(own, "grad_pair_exchange_" + tag)
            srcs = [_pair_sum(core, o, g, "grad_pair_sum_" + k) for k, o, g in zip(names, own, got)]
            copies, index, n_parts = _chip_copies, chip, 3
        else:
            srcs, copies, index, n_parts = own, _scatter_copies, me.astype(jnp.int32).reshape(1), 7
        lands = [lax.empty((n_parts,) + s.shape[1:], s.dtype) for s in srcs]
        started = _copies_start("grad_exchange_start_" + tag, srcs, lands, copies, after=core)
        pending.append((names, started, copies, index))
        return started[4][0, 0]

    small = {k: w[k] for k in REPLICATED}
    packed, grad_x, _ = _local_step(x, mem, loss_target, **small, **whole, late_weights=late_weights,
                                    grads_ready=grads_ready, first_token=first_token)

    small_lands = [lax.empty((N_DEV * PACK_ROWS, D_MODEL), F32)]
    small_gather = _copies_start("gather_small_start", [packed], small_lands, _gather_copies, after=grad_x)

    out_g, out_d, out_m, out_v = {}, {}, {}, {}
    after = small_gather[4]
    for tag, (names, started, copies, index) in enumerate(pending):
        srcs, parts = _copies_wait("grad_exchange_wait_" + str(tag), started[0], started[1], started[2], started[3], after, copies)
        for k, s, p in zip(names, srcs, parts):
            res = _sum_adam(index, s, p, w2[k], m2[k], v2[k], "adam_" + k)
            out_g[k], out_d[k], out_m[k], out_v[k] = ((jnp.transpose(r) if k in TRANSPOSED else r)[None] for r in res)
            after = res[0]

    _, (everyone,) = _copies_wait("gather_small_wait", small_gather[0], small_gather[1], small_gather[2], small_gather[3], after,
                                  _gather_copies, own_block=True)
    res = _adam_small(everyone.reshape(N_DEV, PACK_ROWS, D_MODEL), [w[k] for k in REPLICATED], [m[k] for k in REPLICATED],
                      [v[k] for k in REPLICATED])
    for i, k in enumerate(REPLICATED):
        out_g[k], out_d[k], out_m[k], out_v[k] = res[4 * i:4 * i + 4]
    loss = res[-1].reshape(())

    return (loss, grad_x, *[out_g[k] for k in WEIGHTS], *[out_d[k] for k in WEIGHTS], *[out_m[k] for k in WEIGHTS],
            *[out_v[k] for k in WEIGHTS])
```

```python
import functools

import jax
import jax.numpy as jnp
import numpy as np
from jax import lax
from jax.experimental import pallas as pl
from jax.experimental.pallas import tpu as pltpu

F32 = jnp.float32
BF16 = jnp.bfloat16

D_MODEL = 1024
FOX_HEADS = 8
FOX_HEAD_DIM = 64
FOX_WIDTH = 512
GDN_HEADS = 4
GDN_HEAD_DIM = 128
GDN_WIDTH = 512
CONV_WIDTH = 4
GDN_CHUNK = 128
GDN_GROUP = 4
FOX_BLOCK = 512
XATTN_HEADS = 4
XATTN_HEAD_DIM = 128
XATTN_WIDTH = 512
D_FF = 4096
EPS = 1e-6
NEG_INF = -1e30
N_DEV = 8

ADAM_LR = 0.001
ADAM_B1 = 0.9
ADAM_B2 = 0.999
ADAM_EPS = 1e-08
ADAM_WD = 0.01
ADAM_STEP = 10

P_FOX = 0
P_GDN = 1536
P_Z = 3072
P_SMALL = 3584
P_DIM = 3712
SM_F = 0
SM_B = 8
SM_A = 12
SM_ROWS = 16

LANES = 128
VMEM_LIMIT = 56 * 1024 * 1024

NN = (((1,), (0,)), ((), ()))
NT = (((1,), (1,)), ((), ()))
TN = (((0,), (0,)), ((), ()))


def _dot(a, b, dims=NN):
    return lax.dot_general(a.astype(BF16), b.astype(BF16), dims, preferred_element_type=F32)


def _cparams(sem=None):
    kw = dict(vmem_limit_bytes=VMEM_LIMIT)
    if sem is not None:
        kw["dimension_semantics"] = sem
    return pltpu.CompilerParams(**kw)


def _sigmoid(x):
    return 0.5 * (jnp.tanh(0.5 * x) + 1.0)


def _softplus(x):
    return jnp.maximum(x, 0.0) + jnp.log1p(jnp.exp(-jnp.abs(x)))


def _log_sigmoid(x):
    return -_softplus(-x)


def _rms(x, g):
    r = lax.rsqrt(jnp.mean(x * x, axis=-1, keepdims=True) + EPS)
    return x * r * g


def _rms_bwd(x, g, dy):
    r = lax.rsqrt(jnp.mean(x * x, axis=-1, keepdims=True) + EPS)
    xh = x * r
    dg = jnp.sum(dy * xh, axis=0, keepdims=True)
    dyg = dy * g
    dx = r * (dyg - xh * jnp.mean(dyg * xh, axis=-1, keepdims=True))
    return dx, dg


def _pair_stat(t, m0):
    s0 = jnp.sum(jnp.where(m0, t, 0.0), axis=-1, keepdims=True)
    s1 = jnp.sum(jnp.where(m0, 0.0, t), axis=-1, keepdims=True)
    return jnp.where(m0, s0, s1)


def _rms_pair(x, g, m0):
    r = lax.rsqrt(_pair_stat(x * x, m0) * (1.0 / FOX_HEAD_DIM) + EPS)
    return x * r * g


def _rms_pair_bwd(x, g, dy, m0):
    r = lax.rsqrt(_pair_stat(x * x, m0) * (1.0 / FOX_HEAD_DIM) + EPS)
    xh = x * r
    dg = jnp.sum(dy * xh, axis=0, keepdims=True)
    dyg = dy * g
    dx = r * (dyg - xh * (_pair_stat(dyg * xh, m0) * (1.0 / FOX_HEAD_DIM)))
    return dx, dg


@jax.custom_vjp
def _mm_nn(a, b):
    return _dot(a, b, NN)


_mm_nn.defvjp(lambda a, b: (_dot(a, b, NN), (a, b)),
              lambda r, g: (_dot(g, r[1], NT), _dot(r[0], g, TN)))


@jax.custom_vjp
def _mm_nt(a, b):
    return _dot(a, b, NT)


_mm_nt.defvjp(lambda a, b: (_dot(a, b, NT), (a, b)),
              lambda r, g: (_dot(g, r[1], NN), _dot(g, r[0], TN)))


@jax.custom_vjp
def _mm_tn(a, b):
    return _dot(a, b, TN)


_mm_tn.defvjp(lambda a, b: (_dot(a, b, TN), (a, b)),
              lambda r, g: (_dot(r[1], g, NT), _dot(r[0], g, NN)))


def _dot3(a, b, dims):
    ah = a.astype(BF16)
    al = (a - ah.astype(F32)).astype(BF16)
    bh = b.astype(BF16)
    bl = (b - bh.astype(F32)).astype(BF16)
    d = functools.partial(lax.dot_general, dimension_numbers=dims, preferred_element_type=F32)
    return d(ah, bh) + d(ah, bl) + d(al, bh)


def _neumann_inverses(mats):
    c = mats[0].shape[0]
    eye = (lax.broadcasted_iota(jnp.int32, (c, c), 0) == lax.broadcasted_iota(jnp.int32, (c, c), 1)).astype(F32)
    xs = [eye - a for a in mats]
    ps = list(mats)
    k = 2
    while k < c + 1:
        ps = [_dot3(p, p, NN) for p in ps]
        xs = [x + _dot3(x, p, NN) for x, p in zip(xs, ps)]
        k *= 2
    return xs


@jax.custom_vjp
def _unit_lower_inverses(mats):
    return _neumann_inverses(mats)


def _unit_lower_inverses_fwd(mats):
    ts = _neumann_inverses(mats)
    return ts, ts


def _unit_lower_inverses_bwd(ts, gs):
    left = [_dot3(t, g, TN) for t, g in zip(ts, gs)]
    return ([-_dot3(m, t, NT) for m, t in zip(left, ts)],)


_unit_lower_inverses.defvjp(_unit_lower_inverses_fwd, _unit_lower_inverses_bwd)


def _wgrad(a, b, name, bk=1024, bn=1024, bt=1024, column_blocks=None):
    t_len, k_len = a.shape
    n_len = b.shape[1]
    bk, bn, bt = min(bk, k_len), min(bn, n_len), min(bt, t_len)
    nt = t_len // bt

    def body(a_ref, b_ref, o_ref, acc_ref):
        t = pl.program_id(2)

        @pl.when(t == 0)
        def _():
            acc_ref[...] = jnp.zeros_like(acc_ref)

        acc_ref[...] += _dot(a_ref[...], b_ref[...], TN)

        @pl.when(t == nt - 1)
        def _():
            if column_blocks:
                for jj in range(bn // column_blocks):
                    o_ref[jj] = acc_ref[:, jj * column_blocks:(jj + 1) * column_blocks]
            else:
                o_ref[...] = acc_ref[...]

    if column_blocks:
        out_spec = pl.BlockSpec((bn // column_blocks, bk, column_blocks), lambda i, j, t: (j, i, 0))
        out_shape = jax.ShapeDtypeStruct((n_len // column_blocks, k_len, column_blocks), F32)
    else:
        out_spec = pl.BlockSpec((bk, bn), lambda i, j, t: (i, j))
        out_shape = jax.ShapeDtypeStruct((k_len, n_len), F32)
    return pl.pallas_call(
        body, name=name, grid=(k_len // bk, n_len // bn, nt),
        in_specs=[pl.BlockSpec((bt, bk), lambda i, j, t: (t, i)), pl.BlockSpec((bt, bn), lambda i, j, t: (t, j))],
        out_specs=out_spec, out_shape=out_shape,
        scratch_shapes=[pltpu.VMEM((bk, bn), F32)],
        compiler_params=_cparams(("parallel", "parallel", "arbitrary")),
    )(a, b)


def _wgrad_stacked(pieces, b, name, bn=512, bt=1024):
    t_len, n_len = b.shape
    n_p = len(pieces)
    starts = [int(s) for s in np.cumsum([0] + [p.shape[1] for p in pieces])]
    bn, bt = min(bn, n_len), min(bt, t_len)
    nt = t_len // bt

    def body(*refs):
        b_ref, o_ref, acc_ref = refs[n_p:]
        t = pl.program_id(1)

        @pl.when(t == 0)
        def _():
            acc_ref[...] = jnp.zeros_like(acc_ref)

        for k in range(n_p):
            acc_ref[starts[k]:starts[k + 1], :] += _dot(refs[k][...], b_ref[...], TN)

        @pl.when(t == nt - 1)
        def _():
            o_ref[...] = acc_ref[...]

    return pl.pallas_call(
        body, name=name, grid=(n_len // bn, nt),
        in_specs=[pl.BlockSpec((bt, p.shape[1]), lambda j, t: (t, 0)) for p in pieces] + [pl.BlockSpec((bt, bn), lambda j, t: (t, j))],
        out_specs=pl.BlockSpec((starts[-1], bn), lambda j, t: (0, j)),
        out_shape=jax.ShapeDtypeStruct((starts[-1], n_len), F32),
        scratch_shapes=[pltpu.VMEM((starts[-1], bn), F32)],
        compiler_params=_cparams(("parallel", "arbitrary")),
    )(*pieces, b)


def _rows_matmul(a, b, name, bt=512):
    r_len, t_len = a.shape
    n_len = b.shape[1]
    bt = min(bt, t_len)
    nt = t_len // bt

    def body(a_ref, b_ref, o_ref):
        t = pl.program_id(0)

        @pl.when(t == 0)
        def _():
            o_ref[...] = jnp.zeros_like(o_ref)

        o_ref[...] += _dot(a_ref[...], b_ref[...], NN)

    return pl.pallas_call(
        body, name=name, grid=(nt,),
        in_specs=[pl.BlockSpec((r_len, bt), lambda t: (0, t)), pl.BlockSpec((bt, n_len), lambda t: (t, 0))],
        out_specs=pl.BlockSpec((r_len, n_len), lambda t: (0, 0)),
        out_shape=jax.ShapeDtypeStruct((r_len, n_len), F32),
        compiler_params=_cparams(("arbitrary",)),
    )(a, b)


def _in_proj(x, g, wp, wst, tm=512):
    t_len, d = x.shape
    tm = min(tm, t_len)

    def body(x_ref, g_ref, wp_ref, wst_ref, h_ref, fox_ref, gdn_ref, z_ref, sm_ref, smt_ref):
        h = _rms(x_ref[...], g_ref[...]).astype(BF16)
        h_ref[...] = h
        p = _dot(h, wp_ref[...], NT)
        fox_ref[...] = p[:, P_FOX:P_GDN]
        gdn_ref[...] = p[:, P_GDN:P_Z]
        z_ref[...] = p[:, P_Z:P_SMALL]
        sm_ref[...] = p[:, P_SMALL:P_DIM]
        smt_ref[...] = _dot(wst_ref[...], h, NT)

    row = lambda i: (i, 0)
    fixed = lambda i: (0, 0)
    return pl.pallas_call(
        body, name="in_proj", grid=(t_len // tm,),
        in_specs=[pl.BlockSpec((tm, d), row), pl.BlockSpec((1, d), fixed), _resident((P_DIM, d)),
                  pl.BlockSpec((SM_ROWS, d), fixed)],
        out_specs=[pl.BlockSpec((tm, d), row), pl.BlockSpec((tm, 1536), row), pl.BlockSpec((tm, 1536), row),
                   pl.BlockSpec((tm, 512), row), pl.BlockSpec((tm, LANES), row), pl.BlockSpec((SM_ROWS, tm), lambda i: (0, i))],
        out_shape=[jax.ShapeDtypeStruct((t_len, d), BF16), jax.ShapeDtypeStruct((t_len, 1536), F32),
                   jax.ShapeDtypeStruct((t_len, 1536), F32), jax.ShapeDtypeStruct((t_len, 512), F32),
                   jax.ShapeDtypeStruct((t_len, LANES), F32), jax.ShapeDtypeStruct((SM_ROWS, t_len), F32)],
        compiler_params=_cparams(("parallel",)),
    )(x, g, wp, wst)


def _in_proj_bwd(dprojs, dsmt, x, g, wp, wst, dx1, tm=512):
    t_len, d = x.shape
    tm = min(tm, t_len)
    n_p = len(dprojs)
    starts = np.cumsum([0] + [p.shape[1] for p in dprojs])

    def body(*refs):
        dp_refs = refs[:n_p]
        dst_ref, x_ref, g_ref, wp_ref, wst_ref, dx1_ref, dx_ref, dg_ref = refs[n_p:]
        i = pl.program_id(0)
        dh = _dot(dst_ref[...], wst_ref[...], TN)
        for k in range(n_p):
            dh = dh + _dot(dp_refs[k][...], wp_ref[int(starts[k]):int(starts[k + 1]), :], NN)
        dxn, dg = _rms_bwd(x_ref[...], g_ref[...], dh)
        dx_ref[...] = dx1_ref[...] + dxn

        @pl.when(i == 0)
        def _():
            dg_ref[...] = jnp.zeros_like(dg_ref)

        dg_ref[...] += dg

    row = lambda i: (i, 0)
    fixed = lambda i: (0, 0)
    return pl.pallas_call(
        body, name="in_proj_bwd", grid=(t_len // tm,),
        in_specs=[pl.BlockSpec((tm, p.shape[1]), row) for p in dprojs] + [
            pl.BlockSpec((SM_ROWS, tm), lambda i: (0, i)), pl.BlockSpec((tm, d), row),
            pl.BlockSpec((1, d), fixed), _resident((P_DIM, d)), pl.BlockSpec((SM_ROWS, d), fixed),
            pl.BlockSpec((tm, d), row)],
        out_specs=[pl.BlockSpec((tm, d), row), pl.BlockSpec((1, d), fixed)],
        out_shape=[jax.ShapeDtypeStruct((t_len, d), F32), jax.ShapeDtypeStruct((1, d), F32)],
        compiler_params=_cparams(("arbitrary",)),
    )(*dprojs, dsmt, x, g, wp, wst, dx1)


def _fox_cum(smt, bias_col, n_batch, s_len, ck=256):
    ck = min(ck, s_len)

    def body(s_ref, b_ref, c_ref):
        tri = (lax.broadcasted_iota(jnp.int32, (ck, ck), 0) <= lax.broadcasted_iota(jnp.int32, (ck, ck), 1)).astype(F32)
        carry = jnp.zeros((SM_ROWS, 1), F32)
        for r in range(s_len // ck):
            ls = _log_sigmoid(s_ref[:, r * ck:(r + 1) * ck] + b_ref[...])
            c = jnp.dot(ls, tri, precision=lax.Precision.HIGHEST, preferred_element_type=F32) + carry
            c_ref[:, r * ck:(r + 1) * ck] = c
            carry = c[:, ck - 1:ck]

    return pl.pallas_call(
        body, name="fox_cum", grid=(n_batch,),
        in_specs=[pl.BlockSpec((SM_ROWS, s_len), lambda b: (0, b)), pl.BlockSpec((SM_ROWS, 1), lambda b: (0, 0))],
        out_specs=pl.BlockSpec((SM_ROWS, s_len), lambda b: (0, b)),
        out_shape=jax.ShapeDtypeStruct(smt.shape, F32),
        compiler_params=_cparams(("parallel",)),
    )(smt, bias_col)


def _fox_cum_bwd(dc, smt, bias_col, n_batch, s_len, ck=256):
    ck = min(ck, s_len)
    nr = s_len // ck

    def body(dc_ref, s_ref, b_ref, dl_ref, db_ref):
        b = pl.program_id(0)
        tri = (lax.broadcasted_iota(jnp.int32, (ck, ck), 0) >= lax.broadcasted_iota(jnp.int32, (ck, ck), 1)).astype(F32)
        carry = jnp.zeros((SM_ROWS, 1), F32)
        tot = jnp.zeros((SM_ROWS, 1), F32)
        for r in reversed(range(nr)):
            sl = slice(r * ck, (r + 1) * ck)
            dls = jnp.dot(dc_ref[:, sl], tri, precision=lax.Precision.HIGHEST, preferred_element_type=F32) + carry
            carry = dls[:, 0:1]
            dl = dls * (1.0 - _sigmoid(s_ref[:, sl] + b_ref[...]))
            dl_ref[:, sl] = dl
            tot = tot + jnp.sum(dl, axis=1, keepdims=True)

        @pl.when(b == 0)
        def _():
            db_ref[...] = jnp.zeros_like(db_ref)

        db_ref[...] += jnp.broadcast_to(tot, db_ref.shape)

    return pl.pallas_call(
        body, name="fox_cum_bwd", grid=(n_batch,),
        in_specs=[pl.BlockSpec((SM_ROWS, s_len), lambda b: (0, b)), pl.BlockSpec((SM_ROWS, s_len), lambda b: (0, b)),
                  pl.BlockSpec((SM_ROWS, 1), lambda b: (0, 0))],
        out_specs=[pl.BlockSpec((SM_ROWS, s_len), lambda b: (0, b)), pl.BlockSpec((SM_ROWS, LANES), lambda b: (0, 0))],
        out_shape=[jax.ShapeDtypeStruct(smt.shape, F32), jax.ShapeDtypeStruct((SM_ROWS, LANES), F32)],
        compiler_params=_cparams(("arbitrary",)),
    )(dc, smt, bias_col)


def _fox_diagonal_mask(tq):
    return lax.broadcasted_iota(jnp.int32, (tq, tq), 1) <= lax.broadcasted_iota(jnp.int32, (tq, tq), 0)


def _fox_fwd(pf, cb, gq2, gk2, go2, tq=256):
    n_batch, s_len, _ = pf.shape
    tq = min(tq, s_len)
    nq = s_len // tq
    scale = FOX_HEAD_DIM ** -0.5

    def body(q_ref, k_ref, v_ref, c_ref, gq_ref, gk_ref, go_ref, o_ref, on_ref, lse_ref, kh_ref, vh_ref):
        j = pl.program_id(1)
        i = pl.program_id(2)
        m0 = lax.broadcasted_iota(jnp.int32, (1, LANES), 1) < FOX_HEAD_DIM

        @pl.when(i == 0)
        def _():
            kn = _rms_pair(k_ref[0], gk_ref[...], m0)
            kh_ref[0] = jnp.where(m0, kn, 0.0).astype(BF16)
            kh_ref[1] = jnp.where(m0, 0.0, kn).astype(BF16)
            v = v_ref[0]
            vh_ref[0] = jnp.where(m0, v, 0.0).astype(BF16)
            vh_ref[1] = jnp.where(m0, 0.0, v).astype(BF16)

        qb = (_rms_pair(q_ref[0], gq_ref[...], m0) * scale).astype(BF16)

        def step(kb, carry, diagonal=False):
            ms, ls, acc = carry
            off = pl.multiple_of(kb * tq, tq)
            new_m, new_l, alphas, pv = [], [], [], []
            for hh in range(2):
                s = _dot(qb, kh_ref[hh, pl.ds(off, tq), :], NT)
                s = s - c_ref[0, kb, pl.ds(2 * j + hh, 1), :]
                if diagonal:
                    s = jnp.where(_fox_diagonal_mask(tq), s, NEG_INF)
                m_new = jnp.maximum(ms[hh], jnp.max(s, axis=-1, keepdims=True))
                alpha = jnp.exp(ms[hh] - m_new)
                p = jnp.exp(s - m_new)
                new_l.append(alpha * ls[hh] + jnp.sum(p, axis=-1, keepdims=True))
                new_m.append(m_new)
                alphas.append(alpha)
                pv.append(_dot(p, vh_ref[hh, pl.ds(off, tq), :], NN))
            acc = jnp.where(m0, alphas[0], alphas[1]) * acc + pv[0] + pv[1]
            return tuple(new_m), tuple(new_l), acc

        init_m = (jnp.full((tq, 1), NEG_INF, F32),) * 2
        init_l = (jnp.zeros((tq, 1), F32),) * 2
        carry = lax.fori_loop(0, i, step, (init_m, init_l, jnp.zeros((tq, LANES), F32)))
        ms, ls, acc = step(i, carry, diagonal=True)
        o = acc / jnp.where(m0, ls[0], ls[1])
        o_ref[0] = o
        on_ref[0] = _rms_pair(o, go_ref[...], m0).astype(BF16)
        lse_ref[0] = jnp.where(m0, ms[0] + jnp.log(ls[0]), ms[1] + jnp.log(ls[1]))

    fixed = lambda b, j, i: (0, 0)
    tile = lambda b, j, i: (b, i, j)
    return pl.pallas_call(
        body, name="fox_fwd", grid=(n_batch, 4, nq),
        in_specs=[pl.BlockSpec((1, tq, LANES), tile), pl.BlockSpec((1, s_len, LANES), lambda b, j, i: (b, 0, 4 + j)),
                  pl.BlockSpec((1, s_len, LANES), lambda b, j, i: (b, 0, 8 + j)),
                  pl.BlockSpec((1, nq, SM_ROWS, tq), lambda b, j, i: (b, 0, 0, 0)),
                  pl.BlockSpec((1, LANES), fixed), pl.BlockSpec((1, LANES), fixed), pl.BlockSpec((1, LANES), fixed)],
        out_specs=[pl.BlockSpec((1, tq, LANES), tile), pl.BlockSpec((1, tq, LANES), tile), pl.BlockSpec((1, tq, LANES), tile)],
        out_shape=[jax.ShapeDtypeStruct((n_batch, s_len, FOX_WIDTH), F32), jax.ShapeDtypeStruct((n_batch, s_len, FOX_WIDTH), BF16),
                   jax.ShapeDtypeStruct((n_batch, s_len, FOX_WIDTH), F32)],
        scratch_shapes=[pltpu.VMEM((2, s_len, LANES), BF16), pltpu.VMEM((2, s_len, LANES), BF16)],
        compiler_params=_cparams(("parallel", "parallel", "arbitrary")),
    )(pf, pf, pf, cb, gq2, gk2, go2)


def _fox_bwd(pf, cb, gq2, gk2, go2, o, lse, don, tq=256):
    n_batch, s_len, _ = pf.shape
    tq = min(tq, s_len)
    nq = s_len // tq
    scale = FOX_HEAD_DIM ** -0.5

    def body(q_ref, k_ref, v_ref, c_ref, gq_ref, gk_ref, go_ref, o_ref, lse_ref, don_ref,
             dq_ref, dk_ref, dv_ref, dc_ref, dgq_ref, dgk_ref, dgo_ref, kh_ref, vh_ref, dka_ref, dva_ref, dca_ref):
        b = pl.program_id(0)
        j = pl.program_id(1)
        i = pl.program_id(2)
        m0 = lax.broadcasted_iota(jnp.int32, (1, LANES), 1) < FOX_HEAD_DIM

        @pl.when((b == 0) & (j == 0) & (i == 0))
        def _():
            dgq_ref[...] = jnp.zeros_like(dgq_ref)
            dgk_ref[...] = jnp.zeros_like(dgk_ref)
            dgo_ref[...] = jnp.zeros_like(dgo_ref)

        @pl.when(i == 0)
        def _():
            kn = _rms_pair(k_ref[0], gk_ref[...], m0)
            kh_ref[0] = jnp.where(m0, kn, 0.0).astype(BF16)
            kh_ref[1] = jnp.where(m0, 0.0, kn).astype(BF16)
            v = v_ref[0]
            vh_ref[0] = jnp.where(m0, v, 0.0).astype(BF16)
            vh_ref[1] = jnp.where(m0, 0.0, v).astype(BF16)
            dka_ref[...] = jnp.zeros_like(dka_ref)
            dva_ref[...] = jnp.zeros_like(dva_ref)
            dca_ref[...] = jnp.zeros_like(dca_ref)

        q = q_ref[0]
        qn = _rms_pair(q, gq_ref[...], m0)
        qs = qn * scale
        qb = qs.astype(BF16)
        qh = (jnp.where(m0, qs, 0.0).astype(BF16), jnp.where(m0, 0.0, qs).astype(BF16))
        ot = o_ref[0]
        do, dgo = _rms_pair_bwd(ot, go_ref[...], don_ref[0], m0)
        dgo_ref[...] += dgo
        dd = do * ot
        delta = (jnp.sum(jnp.where(m0, dd, 0.0), axis=-1, keepdims=True), jnp.sum(jnp.where(m0, 0.0, dd), axis=-1, keepdims=True))
        doh = (jnp.where(m0, do, 0.0).astype(BF16), jnp.where(m0, 0.0, do).astype(BF16))
        lse_t = lse_ref[0]
        lse_h = (lse_t[:, 0:1], lse_t[:, FOX_HEAD_DIM:FOX_HEAD_DIM + 1])

        def step(kb, carry, diagonal=False):
            dqn, rs = carry
            rs = list(rs)
            off = pl.multiple_of(kb * tq, tq)
            for hh in range(2):
                kblk = kh_ref[hh, pl.ds(off, tq), :]
                vblk = vh_ref[hh, pl.ds(off, tq), :]
                s = _dot(qb, kblk, NT)
                s = s - c_ref[0, kb, pl.ds(2 * j + hh, 1), :]
                if diagonal:
                    s = jnp.where(_fox_diagonal_mask(tq), s, NEG_INF)
                p = jnp.exp(s - lse_h[hh])
                dp = _dot(doh[hh], vblk, NT)
                ds = p * (dp - delta[hh])
                dva_ref[pl.ds(off, tq), :] += _dot(p, doh[hh], TN)
                dka_ref[pl.ds(off, tq), :] += _dot(ds, qh[hh], TN)
                dca_ref[kb, hh:hh + 1, :] += -jnp.sum(ds, axis=0, keepdims=True)
                rs[hh] = rs[hh] + jnp.sum(ds, axis=-1, keepdims=True)
                dqn = dqn + _dot(ds, kblk, NN)
            return dqn, tuple(rs)

        carry = lax.fori_loop(0, i, step, (jnp.zeros((tq, LANES), F32), (jnp.zeros((tq, 1), F32),) * 2))
        dqn, rs = step(i, carry, diagonal=True)
        dqn = dqn * scale
        rs_rows = jnp.where(m0, rs[0], rs[1]).T
        dca_ref[i, 0:1, :] += rs_rows[0:1, :]
        dca_ref[i, 1:2, :] += rs_rows[FOX_HEAD_DIM:FOX_HEAD_DIM + 1, :]
        dq, dgq = _rms_pair_bwd(q, gq_ref[...], dqn, m0)
        dq_ref[0] = dq.astype(BF16)
        dgq_ref[...] += dgq

        @pl.when(i == nq - 1)
        def _():
            dk, dgk = _rms_pair_bwd(k_ref[0], gk_ref[...], dka_ref[...], m0)
            dk_ref[0] = dk.astype(BF16)
            dgk_ref[...] += dgk
            dv_ref[0] = dva_ref[...].astype(BF16)
            dc_ref[0, 0] = dca_ref[...]

    fixed = lambda b, j, i: (0, 0)
    tile = lambda b, j, i: (b, i, j)
    full = lambda b, j, i: (b, 0, j)
    wide = jax.ShapeDtypeStruct((n_batch, s_len, FOX_WIDTH), BF16)
    gain = jax.ShapeDtypeStruct((1, LANES), F32)
    return pl.pallas_call(
        body, name="fox_bwd", grid=(n_batch, 4, nq),
        in_specs=[pl.BlockSpec((1, tq, LANES), tile), pl.BlockSpec((1, s_len, LANES), lambda b, j, i: (b, 0, 4 + j)),
                  pl.BlockSpec((1, s_len, LANES), lambda b, j, i: (b, 0, 8 + j)),
                  pl.BlockSpec((1, nq, SM_ROWS, tq), lambda b, j, i: (b, 0, 0, 0)),
                  pl.BlockSpec((1, LANES), fixed), pl.BlockSpec((1, LANES), fixed), pl.BlockSpec((1, LANES), fixed),
                  pl.BlockSpec((1, tq, LANES), tile), pl.BlockSpec((1, tq, LANES), tile), pl.BlockSpec((1, tq, LANES), tile)],
        out_specs=[pl.BlockSpec((1, tq, LANES), tile), pl.BlockSpec((1, s_len, LANES), full), pl.BlockSpec((1, s_len, LANES), full),
                   pl.BlockSpec((1, 1, nq, 8, tq), lambda b, j, i: (b, j, 0, 0, 0)),
                   pl.BlockSpec((1, LANES), fixed), pl.BlockSpec((1, LANES), fixed), pl.BlockSpec((1, LANES), fixed)],
        out_shape=[wide, wide, wide, jax.ShapeDtypeStruct((n_batch, 4, nq, 8, tq), F32), gain, gain, gain],
        scratch_shapes=[pltpu.VMEM((2, s_len, LANES), BF16), pltpu.VMEM((2, s_len, LANES), BF16),
                        pltpu.VMEM((s_len, LANES), F32), pltpu.VMEM((s_len, LANES), F32), pltpu.VMEM((nq, 8, tq), F32)],
        compiler_params=_cparams(("arbitrary", "arbitrary", "arbitrary")),
    )(pf, pf, pf, cb, gq2, gk2, go2, o, lse, don)


def _conv_padded(x_ref, w, pad_ref, s_len):
    pad_ref[0:8, :] = jnp.zeros((8, LANES), F32)
    pad_ref[8:8 + s_len, :] = x_ref[0]
    return (w[3:4] * pad_ref[8:8 + s_len, :] + w[2:3] * pad_ref[7:7 + s_len, :] + w[1:2] * pad_ref[6:6 + s_len, :]
            + w[0:1] * pad_ref[5:5 + s_len, :])


def _gdn_pre(pg, conv_w):
    n_batch, s_len, width = pg.shape
    ncb = width // LANES

    def body(x_ref, w_ref, o_ref, pad_ref):
        cb = pl.program_id(1)
        y = _conv_padded(x_ref, w_ref[...], pad_ref, s_len)
        s = y * _sigmoid(y)
        sn = s * lax.rsqrt(jnp.sum(s * s, axis=-1, keepdims=True) + EPS)
        o_ref[0] = jnp.where(cb < 2 * GDN_HEADS, sn, s)

    return pl.pallas_call(
        body, name="gdn_pre", grid=(n_batch, ncb),
        in_specs=[pl.BlockSpec((1, s_len, LANES), lambda b, c: (b, 0, c)), pl.BlockSpec((8, LANES), lambda b, c: (0, c))],
        out_specs=pl.BlockSpec((1, s_len, LANES), lambda b, c: (b, 0, c)),
        out_shape=jax.ShapeDtypeStruct(pg.shape, F32),
        scratch_shapes=[pltpu.VMEM((s_len + 8, LANES), F32)],
        compiler_params=_cparams(("parallel", "parallel")),
    )(pg, conv_w)


def _gdn_pre_bwd(pg, conv_w, dout):
    n_batch, s_len, width = pg.shape
    ncb = width // LANES

    def body(x_ref, w_ref, d_ref, dx_ref, dw_ref, pad_ref, tail_ref):
        cb = pl.program_id(0)
        b = pl.program_id(1)
        x = x_ref[0]
        w = w_ref[...]
        d = d_ref[0]
        y = _conv_padded(x_ref, w, pad_ref, s_len)
        sig = _sigmoid(y)
        s = y * sig
        rr = lax.rsqrt(jnp.sum(s * s, axis=-1, keepdims=True) + EPS)
        sn = s * rr
        ds_n = rr * (d - sn * jnp.sum(d * sn, axis=-1, keepdims=True))
        ds = jnp.where(cb < 2 * GDN_HEADS, ds_n, d)
        dy = ds * (sig * (1.0 + y * (1.0 - sig)))
        tail_ref[0:s_len, :] = dy
        tail_ref[s_len:s_len + 8, :] = jnp.zeros((8, LANES), F32)
        dyu = [tail_ref[3 - jj:3 - jj + s_len, :] for jj in range(CONV_WIDTH)]
        dx = w[0:1] * dyu[0] + w[1:2] * dyu[1] + w[2:3] * dyu[2] + w[3:4] * dyu[3]
        dx_ref[0] = dx.astype(BF16)
        dw = [jnp.sum(dyu[jj] * x, axis=0, keepdims=True) for jj in range(CONV_WIDTH)]
        rows = lax.broadcasted_iota(jnp.int32, (8, LANES), 0)
        dwb = jnp.zeros((8, LANES), F32)
        for jj in range(CONV_WIDTH):
            dwb = dwb + jnp.where(rows == jj, dw[jj], 0.0)

        @pl.when(b == 0)
        def _():
            dw_ref[...] = jnp.zeros_like(dw_ref)

        dw_ref[...] += dwb

    blk = lambda c, b: (b, 0, c)
    return pl.pallas_call(
        body, name="gdn_pre_bwd", grid=(ncb, n_batch),
        in_specs=[pl.BlockSpec((1, s_len, LANES), blk), pl.BlockSpec((8, LANES), lambda c, b: (0, c)), pl.BlockSpec((1, s_len, LANES), blk)],
        out_specs=[pl.BlockSpec((1, s_len, LANES), blk), pl.BlockSpec((8, LANES), lambda c, b: (0, c))],
        out_shape=[jax.ShapeDtypeStruct(pg.shape, BF16), jax.ShapeDtypeStruct((8, width), F32)],
        scratch_shapes=[pltpu.VMEM((s_len + 8, LANES), F32), pltpu.VMEM((s_len + 8, LANES), F32)],
        compiler_params=_cparams(("parallel", "arbitrary")),
    )(pg, conv_w, dout)


def _gdn_gates(smc, smr, a_c, dt_c, a_r, dt_r, h):
    lane = lax.broadcasted_iota(jnp.int32, (1, LANES), 1)
    sub = lax.broadcasted_iota(jnp.int32, (SM_ROWS, 1), 0)
    beta_c = jnp.sum(jnp.where(lane == SM_B + h, _sigmoid(smc), 0.0), axis=1, keepdims=True)
    g_all_c = -jnp.exp(a_c) * _softplus(smc + dt_c)
    g_c = jnp.sum(jnp.where(lane == SM_A + h, g_all_c, 0.0), axis=1, keepdims=True)
    g_all_r = -jnp.exp(a_r) * _softplus(smr + dt_r)
    g_r = jnp.sum(jnp.where(sub == SM_A + h, g_all_r, 0.0), axis=0, keepdims=True)
    return beta_c, g_c, g_r


@jax.custom_vjp
def _known_inverse(a, t):
    return t


_known_inverse.defvjp(lambda a, t: (t, t),
                      lambda t, g: (-_dot3(_dot3(t, g, TN), t, NT), jnp.zeros_like(t)))


def _gdn_group(qkv, z, smc, smr, a_c, dt_c, a_r, dt_r, go, states, inverses=None):
    n_grp = len(qkv)
    c = qkv[0].shape[0]
    hd = GDN_HEAD_DIM
    pairs = [(g, h) for g in range(n_grp) for h in range(GDN_HEADS)]
    ii = lax.broadcasted_iota(jnp.int32, (c, c), 0)
    jj = lax.broadcasted_iota(jnp.int32, (c, c), 1)
    incl = ii >= jj
    col = lambda arr, base, h: arr[:, base + h * hd:base + (h + 1) * hd]

    qs, ks, kbs, vbs, gcs, g_lasts, amats, intras = [], [], [], [], [], [], [], []
    for g, h in pairs:
        beta_c, g_c, g_r = _gdn_gates(smc[g], smr[g], a_c, dt_c, a_r, dt_r, h)
        gc_c = jnp.sum(jnp.where(incl, g_r, 0.0), axis=1, keepdims=True)
        gc_r = jnp.sum(jnp.where(ii <= jj, g_c, 0.0), axis=0, keepdims=True)
        decay = jnp.where(incl, jnp.exp(jnp.where(incl, gc_c - gc_r, 0.0)), 0.0)
        k = col(qkv[g], GDN_WIDTH, h)
        kb = k * beta_c
        qs.append(col(qkv[g], 0, h) * (hd ** -0.5))
        ks.append(k)
        kbs.append(kb)
        vbs.append(col(qkv[g], 2 * GDN_WIDTH, h) * beta_c)
        gcs.append(gc_c)
        g_lasts.append(jnp.sum(g_c, axis=0, keepdims=True))
        both = _mm_nt(jnp.concatenate([kb, qs[-1]], axis=0), k)
        amats.append(jnp.where(ii > jj, both[0:c] * decay, 0.0))
        intras.append(both[c:2 * c] * decay)
    ts = _unit_lower_inverses(amats) if inverses is None else [_known_inverse(a, t) for a, t in zip(amats, inverses)]
    egcs = [jnp.exp(gc) for gc in gcs]
    uws = [_mm_nn(t, jnp.concatenate([vb, kb * e], axis=1)) for t, vb, kb, e in zip(ts, vbs, kbs, egcs)]
    us = [uw[:, 0:hd] for uw in uws]
    ws = [uw[:, hd:2 * hd] for uw in uws]
    qes = [q * e for q, e in zip(qs, egcs)]
    kds = [k * jnp.exp(gl - gc) for k, gl, gc in zip(ks, g_lasts, gcs)]
    sdecs = [jnp.exp(gl) for gl in g_lasts]

    outs = []
    for g in range(n_grp):
        idx = [g * GDN_HEADS + h for h in range(GDN_HEADS)]
        v_new = [us[i] - _mm_nn(ws[i], states[h]) for h, i in enumerate(idx)]
        o = [_mm_nn(jnp.concatenate([qes[i], intras[i]], axis=1), jnp.concatenate([states[h], v_new[h]], axis=0))
             for h, i in enumerate(idx)]
        states = [states[h] * sdecs[i] + _mm_tn(kds[i], v_new[h]) for h, i in enumerate(idx)]
        outs.append([_rms(o[h], go) * (col(z[g], 0, h) * _sigmoid(col(z[g], 0, h))) for h in range(GDN_HEADS)])
    return outs, states, ts


def _gdn_group_size(n_chunks):
    return GDN_GROUP if n_chunks % GDN_GROUP == 0 else 1


def _gdn_fwd(qkvn, z, smc, smr, a_c, dt_c, a_r, dt_r, go):
    n_batch, s_len, _ = qkvn.shape
    c = GDN_CHUNK
    n = s_len // c
    grp = _gdn_group_size(n)
    ng = n // grp
    gc = grp * c
    hd = GDN_HEAD_DIM

    def body(qkv_ref, z_ref, smc_ref, smr_ref, ac_ref, dc_ref, ar_ref, dr_ref, go_ref, og_ref, st_ref, inv_ref, s_ref):
        @pl.when(pl.program_id(1) == 0)
        def _():
            s_ref[...] = jnp.zeros_like(s_ref)

        states = [s_ref[h] for h in range(GDN_HEADS)]
        for h in range(GDN_HEADS):
            st_ref[0, 0, h] = states[h]
        rows = lambda k: slice(k * c, (k + 1) * c)
        outs, nxt, invs = _gdn_group([qkv_ref[0, rows(k), :] for k in range(grp)], [z_ref[0, rows(k), :] for k in range(grp)],
                                     [smc_ref[0, rows(k), :] for k in range(grp)], [smr_ref[k] for k in range(grp)],
                                     ac_ref[...], dc_ref[...], ar_ref[...], dr_ref[...], go_ref[...], states)
        for k in range(grp):
            for h in range(GDN_HEADS):
                og_ref[0, rows(k), h * hd:(h + 1) * hd] = outs[k][h].astype(BF16)
        for p, inv in enumerate(invs):
            inv_ref[0, 0, p] = inv
        for h in range(GDN_HEADS):
            s_ref[h] = nxt[h]

    tok = lambda b, i: (b, i, 0)
    fixed = lambda b, i: (0, 0)
    return pl.pallas_call(
        body, name="gdn_fwd", grid=(n_batch, ng),
        in_specs=[pl.BlockSpec((1, gc, 3 * GDN_WIDTH), tok), pl.BlockSpec((1, gc, GDN_WIDTH), tok), pl.BlockSpec((1, gc, LANES), tok),
                  pl.BlockSpec((grp, SM_ROWS, c), lambda b, i: (b * ng + i, 0, 0)),
                  pl.BlockSpec((1, LANES), fixed), pl.BlockSpec((1, LANES), fixed), pl.BlockSpec((SM_ROWS, 1), fixed),
                  pl.BlockSpec((SM_ROWS, 1), fixed), pl.BlockSpec((1, LANES), fixed)],
        out_specs=[pl.BlockSpec((1, gc, GDN_WIDTH), tok), pl.BlockSpec((1, 1, GDN_HEADS, hd, hd), lambda b, i: (b, i, 0, 0, 0)),
                   pl.BlockSpec((1, 1, grp * GDN_HEADS, c, c), lambda b, i: (b, i, 0, 0, 0))],
        out_shape=[jax.ShapeDtypeStruct((n_batch, s_len, GDN_WIDTH), BF16), jax.ShapeDtypeStruct((n_batch, ng, GDN_HEADS, hd, hd), F32),
                   jax.ShapeDtypeStruct((n_batch, ng, grp * GDN_HEADS, c, c), F32)],
        scratch_shapes=[pltpu.VMEM((GDN_HEADS, hd, hd), F32)],
        compiler_params=_cparams(("parallel", "arbitrary")),
    )(qkvn, z, smc, smr, a_c, dt_c, a_r, dt_r, go)


def _gdn_bwd(qkvn, z, smc, smr, a_c, dt_c, a_r, dt_r, go, states, inverses, dog):
    n_batch, s_len, _ = qkvn.shape
    c = GDN_CHUNK
    n = s_len // c
    grp = _gdn_group_size(n)
    ng = n // grp
    gc = grp * c
    hd = GDN_HEAD_DIM

    def body(qkv_ref, z_ref, smc_ref, smr_ref, ac_ref, dc_ref, ar_ref, dr_ref, go_ref, st_ref, inv_ref, dog_ref,
             dqkv_ref, dz_ref, dsmc_ref, dsmr_ref, dac_ref, ddc_ref, dar_ref, ddr_ref, dgo_ref, ds_ref):
        first = (pl.program_id(0) == 0) & (pl.program_id(1) == 0)

        @pl.when(pl.program_id(1) == 0)
        def _():
            ds_ref[...] = jnp.zeros_like(ds_ref)

        @pl.when(first)
        def _():
            for r in (dac_ref, ddc_ref, dar_ref, ddr_ref, dgo_ref):
                r[...] = jnp.zeros_like(r)

        rows = lambda k: slice(k * c, (k + 1) * c)
        states = [st_ref[0, 0, h] for h in range(GDN_HEADS)]
        prim = ([qkv_ref[0, rows(k), :] for k in range(grp)], [z_ref[0, rows(k), :] for k in range(grp)],
                [smc_ref[0, rows(k), :] for k in range(grp)], [smr_ref[k] for k in range(grp)],
                ac_ref[...], dc_ref[...], ar_ref[...], dr_ref[...], go_ref[...], states)
        invs = [inv_ref[0, 0, p] for p in range(grp * GDN_HEADS)]
        _, vjp = jax.vjp(functools.partial(_gdn_group, inverses=invs), *prim)
        cot = ([[dog_ref[0, rows(k), h * hd:(h + 1) * hd] for h in range(GDN_HEADS)] for k in range(grp)],
               [ds_ref[h] for h in range(GDN_HEADS)], [jnp.zeros((c, c), F32)] * (grp * GDN_HEADS))
        dqkv, dz, dsmc, dsmr, dac, ddc, dar, ddr, dgo, dstates = vjp(cot)
        for k in range(grp):
            dqkv_ref[0, rows(k), :] = dqkv[k]
            dz_ref[0, rows(k), :] = dz[k].astype(BF16)
            dsmc_ref[0, rows(k), :] = dsmc[k]
            dsmr_ref[k] = dsmr[k]
        dac_ref[...] += dac
        ddc_ref[...] += ddc
        dar_ref[...] += dar
        ddr_ref[...] += ddr
        dgo_ref[...] += dgo
        for h in range(GDN_HEADS):
            ds_ref[h] = dstates[h]

    tok = lambda b, i: (b, ng - 1 - i, 0)
    fixed = lambda b, i: (0, 0)
    lane_vec = jax.ShapeDtypeStruct((1, LANES), F32)
    row_vec = jax.ShapeDtypeStruct((SM_ROWS, 1), F32)
    return pl.pallas_call(
        body, name="gdn_bwd", grid=(n_batch, ng),
        in_specs=[pl.BlockSpec((1, gc, 3 * GDN_WIDTH), tok), pl.BlockSpec((1, gc, GDN_WIDTH), tok), pl.BlockSpec((1, gc, LANES), tok),
                  pl.BlockSpec((grp, SM_ROWS, c), lambda b, i: (b * ng + ng - 1 - i, 0, 0)),
                  pl.BlockSpec((1, LANES), fixed), pl.BlockSpec((1, LANES), fixed), pl.BlockSpec((SM_ROWS, 1), fixed),
                  pl.BlockSpec((SM_ROWS, 1), fixed), pl.BlockSpec((1, LANES), fixed),
                  pl.BlockSpec((1, 1, GDN_HEADS, hd, hd), lambda b, i: (b, ng - 1 - i, 0, 0, 0)),
                  pl.BlockSpec((1, 1, grp * GDN_HEADS, c, c), lambda b, i: (b, ng - 1 - i, 0, 0, 0)),
                  pl.BlockSpec((1, gc, GDN_WIDTH), lambda b, i: (b, ng - 1 - i, 1))],
        out_specs=[pl.BlockSpec((1, gc, 3 * GDN_WIDTH), tok), pl.BlockSpec((1, gc, GDN_WIDTH), tok), pl.BlockSpec((1, gc, LANES), tok),
                   pl.BlockSpec((grp, SM_ROWS, c), lambda b, i: (b * ng + ng - 1 - i, 0, 0)),
                   pl.BlockSpec((1, LANES), fixed), pl.BlockSpec((1, LANES), fixed), pl.BlockSpec((SM_ROWS, 1), fixed),
                   pl.BlockSpec((SM_ROWS, 1), fixed), pl.BlockSpec((1, LANES), fixed)],
        out_shape=[jax.ShapeDtypeStruct((n_batch, s_len, 3 * GDN_WIDTH), F32), jax.ShapeDtypeStruct((n_batch, s_len, GDN_WIDTH), BF16),
                   jax.ShapeDtypeStruct((n_batch, s_len, LANES), F32), jax.ShapeDtypeStruct((n_batch * n, SM_ROWS, c), F32),
                   lane_vec, lane_vec, row_vec, row_vec, lane_vec],
        scratch_shapes=[pltpu.VMEM((GDN_HEADS, hd, hd), F32)],
        compiler_params=_cparams(("arbitrary", "arbitrary")),
    )(qkvn, z, smc, smr, a_c, dt_c, a_r, dt_r, go, states, inverses, dog)


def _out_proj(x, oa, ob, w_out, g_x, w_cq, tm=512):
    t_len, d = x.shape
    tm = min(tm, t_len)

    def body(x_ref, oa_ref, ob_ref, wo_ref, g_ref, wq_ref, x1_ref, hq_ref, cq_ref):
        x1 = x_ref[...] + _dot(jnp.concatenate([oa_ref[...], ob_ref[...]], axis=1), wo_ref[...])
        x1_ref[...] = x1
        hq = _rms(x1, g_ref[...]).astype(BF16)
        hq_ref[...] = hq
        cq_ref[...] = _dot(hq, wq_ref[...])

    row = lambda i: (i, 0)
    fixed = lambda i: (0, 0)
    return pl.pallas_call(
        body, name="out_proj", grid=(t_len // tm,),
        in_specs=[pl.BlockSpec((tm, d), row), pl.BlockSpec((tm, FOX_WIDTH), row), pl.BlockSpec((tm, GDN_WIDTH), row),
                  _resident((d, d)), pl.BlockSpec((1, d), fixed), _resident((d, XATTN_WIDTH))],
        out_specs=[pl.BlockSpec((tm, d), row), pl.BlockSpec((tm, d), row), pl.BlockSpec((tm, XATTN_WIDTH), row)],
        out_shape=[jax.ShapeDtypeStruct((t_len, d), F32), jax.ShapeDtypeStruct((t_len, d), BF16), jax.ShapeDtypeStruct((t_len, XATTN_WIDTH), F32)],
        compiler_params=_cparams(("parallel",)),
    )(x, oa, ob, w_out, g_x, w_cq)


def _out_proj_bwd(dx1, w_out, tm=512):
    t_len, d = dx1.shape
    tm = min(tm, t_len)

    def body(dx_ref, w_ref, o_ref):
        o_ref[...] = _dot(dx_ref[...], w_ref[...], NT)

    return pl.pallas_call(
        body, name="out_proj_bwd", grid=(t_len // tm,),
        in_specs=[pl.BlockSpec((tm, d), lambda i: (i, 0)), pl.BlockSpec((d, d), lambda i: (0, 0))],
        out_specs=pl.BlockSpec((tm, d), lambda i: (i, 0)),
        out_shape=jax.ShapeDtypeStruct((t_len, d), F32),
        compiler_params=_cparams(("parallel",)),
    )(dx1, w_out)


def _mem_kv(mem, g, w_ckv, tm=256):
    t_len, d = mem.shape
    tm = min(tm, t_len)

    def body(x_ref, g_ref, w_ref, h_ref, o_ref):
        h = _rms(x_ref[...], g_ref[...]).astype(BF16)
        h_ref[...] = h
        o_ref[...] = _dot(h, w_ref[...])

    row = lambda i: (i, 0)
    fixed = lambda i: (0, 0)
    return pl.pallas_call(
        body, name="mem_kv", grid=(t_len // tm,),
        in_specs=[pl.BlockSpec((tm, d), row), pl.BlockSpec((1, d), fixed), pl.BlockSpec((d, 2 * XATTN_WIDTH), fixed)],
        out_specs=[pl.BlockSpec((tm, d), row), pl.BlockSpec((tm, 2 * XATTN_WIDTH), row)],
        out_shape=[jax.ShapeDtypeStruct((t_len, d), BF16), jax.ShapeDtypeStruct((t_len, 2 * XATTN_WIDTH), F32)],
        compiler_params=_cparams(("parallel",)),
    )(mem, g, w_ckv)


def _mem_kv_bwd(dckv, mem, g, w_ckv, tm=256):
    t_len, d = mem.shape
    tm = min(tm, t_len)

    def body(d_ref, x_ref, g_ref, w_ref, dg_ref):
        @pl.when(pl.program_id(0) == 0)
        def _():
            dg_ref[...] = jnp.zeros_like(dg_ref)

        dh = _dot(d_ref[...], w_ref[...], NT)
        _, dg = _rms_bwd(x_ref[...], g_ref[...], dh)
        dg_ref[...] += dg

    row = lambda i: (i, 0)
    fixed = lambda i: (0, 0)
    return pl.pallas_call(
        body, name="mem_kv_bwd", grid=(t_len // tm,),
        in_specs=[pl.BlockSpec((tm, 2 * XATTN_WIDTH), row), pl.BlockSpec((tm, d), row), pl.BlockSpec((1, d), fixed),
                  pl.BlockSpec((d, 2 * XATTN_WIDTH), fixed)],
        out_specs=pl.BlockSpec((1, d), fixed),
        out_shape=jax.ShapeDtypeStruct((1, d), F32),
        compiler_params=_cparams(("arbitrary",)),
    )(dckv, mem, g, w_ckv)


def _xattn_probs(qn, kn):
    s = _dot(qn, kn, NT) * (XATTN_HEAD_DIM ** -0.5)
    p = jnp.exp(s - jnp.max(s, axis=-1, keepdims=True))
    return p / jnp.sum(p, axis=-1, keepdims=True)


def _xattn_fwd(cq, ckv, x1, gq, gk, w_co, g_mlp, n_batch, s_len, m_len, tq=512):
    d = x1.shape[1]
    tq = min(tq, s_len)
    nq = s_len // tq
    hd = XATTN_HEAD_DIM

    def body(cq_ref, kv_ref, x1_ref, gq_ref, gk_ref, wo_ref, gm_ref, co_ref, x2_ref, hf_ref):
        outs = []
        for h in range(XATTN_HEADS):
            qn = _rms(cq_ref[:, h * hd:(h + 1) * hd], gq_ref[...])
            kn = _rms(kv_ref[:, h * hd:(h + 1) * hd], gk_ref[...])
            p = _xattn_probs(qn, kn)
            outs.append(_dot(p, kv_ref[:, XATTN_WIDTH + h * hd:XATTN_WIDTH + (h + 1) * hd]).astype(BF16))
        for h in range(XATTN_HEADS):
            co_ref[:, h * hd:(h + 1) * hd] = outs[h]
        x2 = x1_ref[...] + _dot(co_ref[...], wo_ref[...])
        x2_ref[...] = x2
        hf_ref[...] = _rms(x2, gm_ref[...]).astype(BF16)

    row = lambda b, i: (b * nq + i, 0)
    fixed = lambda b, i: (0, 0)
    t_len = n_batch * s_len
    return pl.pallas_call(
        body, name="xattn_fwd", grid=(n_batch, nq),
        in_specs=[pl.BlockSpec((tq, XATTN_WIDTH), row), pl.BlockSpec((m_len, 2 * XATTN_WIDTH), lambda b, i: (b, 0)),
                  pl.BlockSpec((tq, d), row), pl.BlockSpec((1, hd), fixed), pl.BlockSpec((1, hd), fixed),
                  pl.BlockSpec((XATTN_WIDTH, d), fixed), pl.BlockSpec((1, d), fixed)],
        out_specs=[pl.BlockSpec((tq, XATTN_WIDTH), row), pl.BlockSpec((tq, d), row), pl.BlockSpec((tq, d), row)],
        out_shape=[jax.ShapeDtypeStruct((t_len, XATTN_WIDTH), BF16), jax.ShapeDtypeStruct((t_len, d), F32),
                   jax.ShapeDtypeStruct((t_len, d), BF16)],
        compiler_params=_cparams(("parallel", "parallel")),
    )(cq, ckv, x1, gq, gk, w_co, g_mlp)


def _xattn_bwd(dx2, cq, ckv, x1, gq, gk, w_co, g_x, w_cq, n_batch, s_len, m_len, tq=512):
    d = x1.shape[1]
    tq = min(tq, s_len)
    nq = s_len // tq
    hd = XATTN_HEAD_DIM
    scale = XATTN_HEAD_DIM ** -0.5

    def body(dx2_ref, cq_ref, kv_ref, x1_ref, gq_ref, gk_ref, wo_ref, gx_ref, wq_ref,
             dx1_ref, dcq_ref, dkv_ref, dgq_ref, dgk_ref, dgx_ref, dk_acc, dv_acc):
        b = pl.program_id(0)
        i = pl.program_id(1)

        @pl.when((b == 0) & (i == 0))
        def _():
            dgq_ref[...] = jnp.zeros_like(dgq_ref)
            dgk_ref[...] = jnp.zeros_like(dgk_ref)
            dgx_ref[...] = jnp.zeros_like(dgx_ref)

        @pl.when(i == 0)
        def _():
            dk_acc[...] = jnp.zeros_like(dk_acc)
            dv_acc[...] = jnp.zeros_like(dv_acc)

        dx2 = dx2_ref[...]
        dco_all = _dot(dx2, wo_ref[...], NT)
        for h in range(XATTN_HEADS):
            sl = slice(h * hd, (h + 1) * hd)
            q = cq_ref[:, sl]
            qn = _rms(q, gq_ref[...])
            kn = _rms(kv_ref[:, sl], gk_ref[...])
            v = kv_ref[:, XATTN_WIDTH + h * hd:XATTN_WIDTH + (h + 1) * hd]
            p = _xattn_probs(qn, kn)
            dco = dco_all[:, sl]
            dv_acc[:, sl] += _dot(p, dco, TN)
            dp = _dot(dco, v, NT)
            ds = p * (dp - jnp.sum(dp * p, axis=-1, keepdims=True))
            dqn = _dot(ds, kn) * scale
            dk_acc[:, sl] += _dot(ds, qn, TN) * scale
            dq, dgq = _rms_bwd(q, gq_ref[...], dqn)
            dgq_ref[...] += dgq
            dcq_ref[:, sl] = dq.astype(BF16)
        dhq = _dot(dcq_ref[...], wq_ref[...], NT)
        dxn, dgx = _rms_bwd(x1_ref[...], gx_ref[...], dhq)
        dgx_ref[...] += dgx
        dx1_ref[...] = dx2 + dxn

        @pl.when(i == nq - 1)
        def _():
            for h in range(XATTN_HEADS):
                sl = slice(h * hd, (h + 1) * hd)
                dk, dgk = _rms_bwd(kv_ref[:, sl], gk_ref[...], dk_acc[:, sl])
                dgk_ref[...] += dgk
                dkv_ref[:, sl] = dk.astype(BF16)
                dkv_ref[:, XATTN_WIDTH + h * hd:XATTN_WIDTH + (h + 1) * hd] = dv_acc[:, sl].astype(BF16)

    row = lambda b, i: (b * nq + i, 0)
    fixed = lambda b, i: (0, 0)
    t_len = n_batch * s_len
    return pl.pallas_call(
        body, name="xattn_bwd", grid=(n_batch, nq),
        in_specs=[pl.BlockSpec((tq, d), row), pl.BlockSpec((tq, XATTN_WIDTH), row), pl.BlockSpec((m_len, 2 * XATTN_WIDTH), lambda b, i: (b, 0)),
                  pl.BlockSpec((tq, d), row), pl.BlockSpec((1, hd), fixed), pl.BlockSpec((1, hd), fixed),
                  pl.BlockSpec((XATTN_WIDTH, d), fixed), pl.BlockSpec((1, d), fixed), pl.BlockSpec((d, XATTN_WIDTH), fixed)],
        out_specs=[pl.BlockSpec((tq, d), row), pl.BlockSpec((tq, XATTN_WIDTH), row), pl.BlockSpec((m_len, 2 * XATTN_WIDTH), lambda b, i: (b, 0)),
                   pl.BlockSpec((1, hd), fixed), pl.BlockSpec((1, hd), fixed), pl.BlockSpec((1, d), fixed)],
        out_shape=[jax.ShapeDtypeStruct((t_len, d), F32), jax.ShapeDtypeStruct((t_len, XATTN_WIDTH), BF16),
                   jax.ShapeDtypeStruct((n_batch * m_len, 2 * XATTN_WIDTH), BF16),
                   jax.ShapeDtypeStruct((1, hd), F32), jax.ShapeDtypeStruct((1, hd), F32), jax.ShapeDtypeStruct((1, d), F32)],
        scratch_shapes=[pltpu.VMEM((m_len, XATTN_WIDTH), F32), pltpu.VMEM((m_len, XATTN_WIDTH), F32)],
        compiler_params=_cparams(("arbitrary", "arbitrary")),
    )(dx2, cq, ckv, x1, gq, gk, w_co, g_x, w_cq)


def _resident(shape):
    return pl.BlockSpec(shape, lambda *_: (0,) * len(shape), pipeline_mode=pl.Buffered(1))


def _mlp_fwd(hf, x2, target, w1, w2, tm=256, tf=1024):
    t_len, d = x2.shape
    f = w1.shape[1]
    tm, tf = min(tm, t_len), min(tf, f)

    def body(hf_ref, x2_ref, tg_ref, w1_ref, w2_ref, u_ref, a_ref, dy_ref, ls_ref):
        hf_t = hf_ref[...]
        for k in range(f // tf):
            cols = slice(k * tf, (k + 1) * tf)
            u = _dot(hf_t, w1_ref[:, cols])
            u_ref[:, cols] = u
            r = jnp.maximum(u, 0.0)
            a_ref[:, cols] = (r * r).astype(BF16)
        y = x2_ref[...] + _dot(a_ref[...], w2_ref[...])
        err = y - tg_ref[...]
        dy_ref[...] = err * (1.0 / d)
        ls_ref[...] = jnp.broadcast_to(jnp.sum(jnp.sum(err * err, axis=-1, keepdims=True) * (1.0 / d), axis=0, keepdims=True), ls_ref.shape)

    row = lambda i: (i, 0)
    return pl.pallas_call(
        body, name="mlp_fwd", grid=(t_len // tm,),
        in_specs=[pl.BlockSpec((tm, d), row), pl.BlockSpec((tm, d), row), pl.BlockSpec((tm, d), row), _resident((d, f)), _resident((f, d))],
        out_specs=[pl.BlockSpec((tm, f), row), pl.BlockSpec((tm, f), row), pl.BlockSpec((tm, d), row),
                   pl.BlockSpec((1, 8, LANES), lambda i: (i, 0, 0))],
        out_shape=[jax.ShapeDtypeStruct((t_len, f), F32), jax.ShapeDtypeStruct((t_len, f), BF16), jax.ShapeDtypeStruct((t_len, d), F32),
                   jax.ShapeDtypeStruct((t_len // tm, 8, LANES), F32)],
        compiler_params=_cparams(("parallel",)),
    )(hf, x2, target, w1, w2)


def _mlp_bwd(dy, u, x2, g, w1, w2, tm=256, tf=1024):
    t_len, d = x2.shape
    f = w1.shape[1]
    tm, tf = min(tm, t_len), min(tf, f)

    def body(dy_ref, u_ref, x2_ref, g_ref, w1_ref, w2_ref, du_ref, dx2_ref, dg_ref):
        @pl.when(pl.program_id(0) == 0)
        def _():
            dg_ref[...] = jnp.zeros_like(dg_ref)

        dy_t = dy_ref[...]
        dyb = dy_t.astype(BF16)
        for k in range(f // tf):
            cols = slice(k * tf, (k + 1) * tf)
            da = _dot(dyb, w2_ref[cols, :], NT)
            du_ref[:, cols] = (da * (2.0 * jnp.maximum(u_ref[:, cols], 0.0))).astype(BF16)
        dhf = _dot(du_ref[...], w1_ref[...], NT)
        dxn, dg = _rms_bwd(x2_ref[...], g_ref[...], dhf)
        dx2_ref[...] = dy_t + dxn
        dg_ref[...] += dg

    row = lambda i: (i, 0)
    fixed = lambda i: (0, 0)
    return pl.pallas_call(
        body, name="mlp_bwd", grid=(t_len // tm,),
        in_specs=[pl.BlockSpec((tm, d), row), pl.BlockSpec((tm, f), row), pl.BlockSpec((tm, d), row), pl.BlockSpec((1, d), fixed),
                  _resident((d, f)), _resident((f, d))],
        out_specs=[pl.BlockSpec((tm, f), row), pl.BlockSpec((tm, d), row), pl.BlockSpec((1, d), fixed)],
        out_shape=[jax.ShapeDtypeStruct((t_len, f), BF16), jax.ShapeDtypeStruct((t_len, d), F32), jax.ShapeDtypeStruct((1, d), F32)],
        compiler_params=_cparams(("arbitrary",)),
    )(dy, u, x2, g, w1, w2)


def _pad_lanes(v, offset=0, width=LANES):
    return jnp.zeros((1, width), F32).at[:, offset:offset + v.shape[1]].set(v)


def _col(v, offset=0, rows=SM_ROWS):
    return jnp.zeros((rows, 1), F32).at[offset:offset + v.shape[1], 0].set(v[0])


def _pack_small(g_mix, dgq, dgk, dbias, dgo, dac, dar, ddc, ddr, g_gdn_o, g_nx, g_mem, g_xq, g_xk, g_mlp, loss_tiles):
    def body(mix_ref, q_ref, k_ref, b_ref, o_ref, ac_ref, ar_ref, dc_ref, dr_ref, go_ref, nx_ref, mem_ref, xq_ref, xk_ref,
             mlp_ref, lt_ref, out_ref):
        lane = lax.broadcasted_iota(jnp.int32, (1, LANES), 1)
        diag = lax.broadcasted_iota(jnp.int32, (SM_ROWS, LANES), 0) == lax.broadcasted_iota(jnp.int32, (SM_ROWS, LANES), 1)

        def rolled(v, shift):
            return pltpu.roll(jnp.broadcast_to(v, (8, LANES)), shift, 1)[0:1, :]

        def rows_to_lanes(col):
            return jnp.sum(jnp.where(diag, col, 0.0), axis=0, keepdims=True)

        def put(row, v, n):
            out_ref[row:row + 1, 0:LANES] = jnp.where(lane < n, v, 0.0)

        out_ref[...] = jnp.zeros_like(out_ref)
        out_ref[0:1, :] = mix_ref[...]
        for row, ref in ((1, q_ref), (2, k_ref), (4, o_ref)):
            put(row, ref[...] + rolled(ref[...], FOX_HEAD_DIM), FOX_HEAD_DIM)
        put(3, rows_to_lanes(b_ref[...]), FOX_HEADS)
        for row, lane_ref, row_ref in ((5, ac_ref, ar_ref), (6, dc_ref, dr_ref)):
            put(row, rolled(lane_ref[...] + rows_to_lanes(row_ref[...]), LANES - SM_A), GDN_HEADS)
        put(7, go_ref[...], LANES)
        out_ref[8:9, :] = nx_ref[...]
        out_ref[9:10, :] = mem_ref[...]
        put(10, xq_ref[...], LANES)
        put(11, xk_ref[...], LANES)
        out_ref[12:13, :] = mlp_ref[...]
        put(LOSS_ROW, 0.5 * jnp.sum(lt_ref[...], axis=0)[0:1, :], 1)

    args = (g_mix, dgq, dgk, dbias, dgo, dac, dar, ddc, ddr, g_gdn_o, g_nx, g_mem, g_xq, g_xk, g_mlp, loss_tiles)
    return pl.pallas_call(body, name="pack_small", out_shape=jax.ShapeDtypeStruct((PACK_ROWS, D_MODEL), F32))(*args)


LATE_WEIGHTS = (("w_out", "w_cq", "w_ckv", "w_co"), ("w_mlp1", "w_mlp2"))
GRAD_GROUPS = (("w_mlp2", "w_mlp1"), ("w_co", "w_cq", "w_ckv", "w_out"), ("w_in", "gdn_conv_w"))


def _local_step(x, mem, target, norm_mix_g, w_in, fox_qnorm_g, fox_knorm_g, fox_f_bias, fox_onorm_g, gdn_conv_w, gdn_A_log,
                gdn_dt_bias, gdn_onorm_g, norm_xattn_g, mem_norm_g, xattn_qnorm_g, xattn_knorm_g, norm_mlp_g,
                late_weights, grads_ready=None, first_token=0.0):
    if grads_ready is None:
        grads_ready = lambda group: 0.0
    n_batch, s_len, d = x.shape
    m_len = mem.shape[1]
    t_len = n_batch * s_len
    tq = min(FOX_BLOCK, s_len)
    nq = s_len // tq
    n_chunks = s_len // GDN_CHUNK
    x2d = x.reshape(t_len, d)

    wp = jnp.concatenate([w_in[0:1536], w_in[1544:3080], w_in[3088:3600], w_in[1536:1544], w_in[3080:3088],
                          jnp.zeros((P_DIM - 3600, d), BF16)], axis=0)
    wst = jnp.concatenate([w_in[1536:1544], w_in[3080:3088]], axis=0)
    conv_w = jnp.concatenate([gdn_conv_w, jnp.zeros((8 - CONV_WIDTH, gdn_conv_w.shape[1]), F32)], axis=0)
    bias_col = _col(fox_f_bias, SM_F)
    gq2, gk2, go2 = (jnp.tile(g, (1, 2)) for g in (fox_qnorm_g, fox_knorm_g, fox_onorm_g))
    a_c, dt_c = _pad_lanes(gdn_A_log, SM_A), _pad_lanes(gdn_dt_bias, SM_A)
    a_r, dt_r = _col(gdn_A_log, SM_A), _col(gdn_dt_bias, SM_A)

    h1, pfox, pgdn, pz, sm, smt = _in_proj(x2d, norm_mix_g + first_token, wp, wst)
    c_rows = _fox_cum(smt, bias_col, n_batch, s_len)
    cb = c_rows.reshape(SM_ROWS, n_batch, nq, tq).transpose(1, 2, 0, 3)
    pf3 = pfox.reshape(n_batch, s_len, 1536)
    o_fox, oa, lse = _fox_fwd(pf3, cb, gq2, gk2, go2, tq)
    pg3 = pgdn.reshape(n_batch, s_len, 1536)
    qkvn = _gdn_pre(pg3, conv_w)
    z3 = pz.reshape(n_batch, s_len, GDN_WIDTH)
    smc = sm.reshape(n_batch, s_len, LANES)
    smr = smt.reshape(SM_ROWS, n_batch * n_chunks, GDN_CHUNK).transpose(1, 0, 2)
    ob, states, inverses = _gdn_fwd(qkvn, z3, smc, smr, a_c, dt_c, a_r, dt_r, gdn_onorm_g)
    oa2, ob2 = oa.reshape(t_len, FOX_WIDTH), ob.reshape(t_len, GDN_WIDTH)
    w_out, w_cq, w_ckv, w_co = late_weights(LATE_WEIGHTS[0], ob2)
    x1, hq, cq = _out_proj(x2d, oa2, ob2, w_out, norm_xattn_g, w_cq)
    mem2d = mem.reshape(n_batch * m_len, d)
    hm, ckv = _mem_kv(mem2d, mem_norm_g, w_ckv)
    co, x2, hf = _xattn_fwd(cq, ckv, x1, xattn_qnorm_g, xattn_knorm_g, w_co, norm_mlp_g, n_batch, s_len, m_len)
    w_mlp1, w_mlp2 = late_weights(LATE_WEIGHTS[1], hf)
    u, a_act, dy, loss_tiles = _mlp_fwd(hf, x2, target.reshape(t_len, d), w_mlp1, w_mlp2)

    grads = {}
    du, dx2, grads["norm_mlp_g"] = _mlp_bwd(dy, u, x2, norm_mlp_g, w_mlp1, w_mlp2)
    grads["w_mlp2"] = _wgrad(a_act, dy, "wgrad_mlp2", bt=2048)
    grads["w_mlp1"] = _wgrad(hf, du, "wgrad_mlp1", bt=2048, column_blocks=D_FF // N_DEV)
    token = grads_ready({k: grads[k] for k in GRAD_GROUPS[0]})
    grads["w_co"] = _wgrad(co, dx2, "wgrad_co", column_blocks=D_MODEL // N_DEV)
    dx1, dcq, dckv, grads["xattn_qnorm_g"], grads["xattn_knorm_g"], grads["norm_xattn_g"] = _xattn_bwd(
        dx2, cq, ckv, x1, xattn_qnorm_g + token, xattn_knorm_g, w_co, norm_xattn_g, w_cq, n_batch, s_len, m_len)
    grads["w_cq"] = _wgrad(hq, dcq, "wgrad_cq")
    grads["w_ckv"] = _wgrad(hm, dckv, "wgrad_ckv")
    grads["mem_norm_g"] = _mem_kv_bwd(dckv, mem2d, mem_norm_g, w_ckv)
    grads["w_out"] = _wgrad_stacked([oa2, ob2], dx1, "wgrad_out", bn=1024)
    token = grads_ready({k: grads[k] for k in GRAD_GROUPS[1]})
    dcat = _out_proj_bwd(dx1, w_out)
    dcat3 = dcat.reshape(n_batch, s_len, d)

    dqkvn, dz, dsmc, dsmr, dac, ddc, dar, ddr, grads["gdn_onorm_g"] = _gdn_bwd(
        qkvn, z3, smc, smr, a_c, dt_c, a_r, dt_r, gdn_onorm_g + token, states, inverses, dcat3)
    dpg, dconv = _gdn_pre_bwd(pg3, conv_w, dqkvn)
    grads["gdn_conv_w"] = dconv[0:CONV_WIDTH]

    dq, dk, dv, dcb, dgq, dgk, dgo = _fox_bwd(pf3, cb, gq2, gk2, go2, o_fox, lse, dcat3, tq)
    dc8 = dcb[:, :, :, 0:2, :].transpose(1, 3, 0, 2, 4).reshape(FOX_HEADS, t_len)
    dc_rows = jnp.concatenate([dc8, jnp.zeros((SM_ROWS - FOX_HEADS, t_len), F32)], axis=0)
    dl_rows, dbias = _fox_cum_bwd(dc_rows, smt, bias_col, n_batch, s_len)
    dsm_rows = jnp.concatenate([dl_rows[0:SM_B], dsmr.transpose(1, 0, 2).reshape(SM_ROWS, t_len)[SM_B:SM_ROWS]], axis=0)

    dprojs = [dq.reshape(t_len, FOX_WIDTH), dk.reshape(t_len, FOX_WIDTH), dv.reshape(t_len, FOX_WIDTH),
              dpg.reshape(t_len, 1536), dz.reshape(t_len, GDN_WIDTH), dsmc.reshape(t_len, LANES)]
    dwp = _wgrad_stacked(dprojs, h1, "wgrad_in")
    dwst = _rows_matmul(dsm_rows, h1, "wgrad_in_rows")
    dw_small = dwp[P_SMALL:P_SMALL + SM_ROWS] + dwst
    grads["w_in"] = jnp.concatenate([dwp[0:1536], dw_small[0:8], dwp[1536:3072], dw_small[8:16], dwp[3072:3584]], axis=0)
    token = grads_ready({k: grads[k] for k in GRAD_GROUPS[2]})
    grad_x, grads["norm_mix_g"] = _in_proj_bwd(dprojs, dsm_rows, x2d, norm_mix_g + token, wp, wst, dx1)
    packed = _pack_small(grads["norm_mix_g"], dgq, dgk, dbias, dgo, dac, dar, ddc, ddr, grads["gdn_onorm_g"], grads["norm_xattn_g"],
                         grads["mem_norm_g"], grads["xattn_qnorm_g"], grads["xattn_knorm_g"], grads["norm_mlp_g"], loss_tiles)
    return packed, grad_x.reshape(n_batch, s_len, d), {k: grads[k] for k in SHARDED}


MESH_ID = pl.DeviceIdType.MESH
ANY_SPEC = pl.BlockSpec(memory_space=pl.ANY)


def _place():
    x, y, c = lax.axis_index("x"), lax.axis_index("y"), lax.axis_index("c")
    return x, y, c, [(1 - x, y), (x, 1 - y), (1 - x, 1 - y)]


def _place_own(src_ref, dst_ref):
    def staged(buf, sem):
        for a, b in ((src_ref, buf), (buf, dst_ref)):
            cp = pltpu.make_async_copy(a, b, sem)
            cp.start()
            cp.wait()

    pl.run_scoped(staged, pltpu.VMEM(src_ref.shape, src_ref.dtype), pltpu.SemaphoreType.DMA)


def _all_gather_body(n, ins, outs, send_sems, recv_sems):
    x, y, c, chips = _place()
    me, sibling = (x, y, c), (x, y, 1 - c)

    def copy(a, k, block, to, src=None):
        dst = outs[a].at[4 * block[0] + 2 * block[1] + block[2]]
        return pltpu.make_async_remote_copy(src_ref=dst if src is None else src, dst_ref=dst, send_sem=send_sems.at[a, k],
                                            recv_sem=recv_sems.at[a, k], device_id=to, device_id_type=MESH_ID)

    first = []
    for a in range(n):
        first.append(copy(a, 0, me, sibling, src=ins[a]))
        first += [copy(a, 1 + j, me, (*chip, c), src=ins[a]) for j, chip in enumerate(chips)]
    for cp in first:
        cp.start()
    for a in range(n):
        _place_own(ins[a], outs[a].at[4 * x + 2 * y + c])
    passed = []
    for j, chip in enumerate(chips):
        for a in range(n):
            copy(a, 1 + j, (*chip, c), me).wait_recv()
            fwd = copy(a, 4 + j, (*chip, c), sibling)
            fwd.start()
            passed.append(fwd)
    for a in range(n):
        copy(a, 0, sibling, me).wait_recv()
        for j, chip in enumerate(chips):
            copy(a, 4 + j, (*chip, 1 - c), me).wait_recv()
    for cp in first + passed:
        cp.wait_send()


def _all_gather_hbm(arrs, name):
    n = len(arrs)

    def body(*refs):
        _all_gather_body(n, refs[:n], refs[n:2 * n], refs[2 * n], refs[2 * n + 1])

    return pl.pallas_call(
        body, name=name, in_specs=[ANY_SPEC] * n, out_specs=[ANY_SPEC] * n,
        out_shape=[jax.ShapeDtypeStruct((N_DEV,) + a.shape, a.dtype) for a in arrs],
        scratch_shapes=[pltpu.SemaphoreType.DMA((n, 7)), pltpu.SemaphoreType.DMA((n, 7))],
        compiler_params=pltpu.CompilerParams(vmem_limit_bytes=VMEM_LIMIT),
    )(*arrs)


def _pair_exchange(arrs, name):
    n = len(arrs)

    def body(*refs):
        ins, outs = refs[:n], refs[n:2 * n]
        send_sems, recv_sems = refs[2 * n:]
        x, y, c, _ = _place()
        copies = []
        for a in range(n):
            for chip in range(4):
                copies.append(pltpu.make_async_remote_copy(
                    src_ref=ins[a].at[2 * chip + (1 - c)], dst_ref=outs[a].at[chip], send_sem=send_sems.at[a, chip],
                    recv_sem=recv_sems.at[a, chip], device_id=(x, y, 1 - c), device_id_type=MESH_ID))
        for cp in copies:
            cp.start()
        for cp in copies:
            cp.wait()

    return pl.pallas_call(
        body, name=name, in_specs=[ANY_SPEC] * n, out_specs=[ANY_SPEC] * n,
        out_shape=[jax.ShapeDtypeStruct((4,) + a.shape[1:], a.dtype) for a in arrs],
        scratch_shapes=[pltpu.SemaphoreType.DMA((n, 4)), pltpu.SemaphoreType.DMA((n, 4))],
    )(*arrs)


HBM_SPEC = pl.BlockSpec(memory_space=pltpu.HBM)
SEM_SPEC = pl.BlockSpec(memory_space=pltpu.SEMAPHORE)
DATAFLOW = pltpu.SideEffectType.DATAFLOW_SIDE_EFFECTING


def _in_hbm(arrs):
    return [pltpu.with_memory_space_constraint(a, pltpu.HBM) for a in arrs]


def _copies_start(name, srcs, lands, make_copies, after):
    n = len(srcs)
    n_copies = len(make_copies(srcs, lands, None, None)[0])

    def body(*refs):
        send_sems, recv_sems = refs[2 * n + 1], refs[2 * n + 2]
        for row in make_copies(refs[:n], refs[n:2 * n], send_sems, recv_sems):
            for cp in row:
                cp.start()
        refs[-1][...] = jnp.zeros_like(refs[-1])

    sems = pltpu.SemaphoreType.DMA((n * n_copies,))
    thru = [pltpu.HBM(a.shape, a.dtype) for a in list(srcs) + list(lands)]
    res = pl.pallas_call(
        body, name=name, in_specs=[HBM_SPEC] * (2 * n) + [ANY_SPEC],
        out_specs=(SEM_SPEC, SEM_SPEC, *[HBM_SPEC] * (2 * n), pl.BlockSpec(memory_space=pltpu.VMEM)),
        out_shape=(sems, sems, *thru, jax.ShapeDtypeStruct((8, LANES), F32)),
        input_output_aliases={i: 2 + i for i in range(2 * n)},
        compiler_params=pltpu.CompilerParams(has_side_effects=DATAFLOW),
    )(*_in_hbm(list(srcs) + list(lands)), after)
    return res[0], res[1], list(res[2:2 + n]), list(res[2 + n:2 + 2 * n]), res[-1]


def _copies_wait(name, send_sems, recv_sems, srcs, lands, after, make_copies, own_block=False):
    n = len(srcs)

    def body(*refs):
        if own_block:
            for a in range(n):
                _place_own(refs[a], _own_part(refs[a], refs[3 * n + 3 + a]))
        for row in make_copies(refs[:n], refs[n:2 * n], refs[2 * n], refs[2 * n + 1]):
            for cp in row:
                cp.wait_send()
                cp.wait_recv()

    res = pl.pallas_call(
        body, name=name, in_specs=[HBM_SPEC] * (2 * n) + [SEM_SPEC, SEM_SPEC, ANY_SPEC],
        out_specs=tuple([HBM_SPEC] * (2 * n)),
        out_shape=tuple(pltpu.HBM(a.shape, a.dtype) for a in list(srcs) + list(lands)),
        input_output_aliases={i: i for i in range(2 * n)},
        compiler_params=pltpu.CompilerParams(has_side_effects=DATAFLOW, vmem_limit_bytes=VMEM_LIMIT),
    )(*srcs, *lands, send_sems, recv_sems, after)
    return list(res[:n]), list(res[n:])


def _own_part(src_ref, land_ref):
    me = 4 * lax.axis_index("x") + 2 * lax.axis_index("y") + lax.axis_index("c")
    rows, cols = src_ref.shape
    if land_ref.shape[0] == N_DEV * rows:
        return land_ref.at[pl.ds(pl.multiple_of(me * rows, rows), rows), :]
    return land_ref.at[:, pl.ds(pl.multiple_of(me * cols, cols), cols)]


def _gather_copies(srcs, lands, send_sems, recv_sems):
    if send_sems is None:
        return [[None] * 7]
    x, y, c, _ = _place()
    rows = []
    for a in range(len(srcs)):
        row = []
        for k in range(7):
            r = k + 1
            to = (1 - x if r & 4 else x, 1 - y if r & 2 else y, 1 - c if r & 1 else c)
            row.append(pltpu.make_async_remote_copy(
                src_ref=srcs[a], dst_ref=_own_part(srcs[a], lands[a]), send_sem=send_sems.at[7 * a + k], recv_sem=recv_sems.at[7 * a + k],
                device_id=to, device_id_type=MESH_ID))
        rows.append(row)
    return rows


def _scatter_copies(srcs, lands, send_sems, recv_sems):
    if send_sems is None:
        return [[None] * 7]
    x, y, c, _ = _place()
    rows = []
    for a in range(len(srcs)):
        row = []
        for k in range(7):
            r = k + 1
            to = (1 - x if r & 4 else x, 1 - y if r & 2 else y, 1 - c if r & 1 else c)
            row.append(pltpu.make_async_remote_copy(
                src_ref=srcs[a].at[4 * to[0] + 2 * to[1] + to[2]], dst_ref=lands[a].at[k], send_sem=send_sems.at[7 * a + k],
                recv_sem=recv_sems.at[7 * a + k], device_id=to, device_id_type=MESH_ID))
        rows.append(row)
    return rows


def _chip_copies(srcs, lands, send_sems, recv_sems):
    if send_sems is None:
        return [[None] * 3]
    x, y, c, chips = _place()
    return [[pltpu.make_async_remote_copy(
        src_ref=srcs[a].at[2 * chip[0] + chip[1]], dst_ref=lands[a].at[j], send_sem=send_sems.at[3 * a + j], recv_sem=recv_sems.at[3 * a + j],
        device_id=(*chip, c), device_id_type=MESH_ID) for j, chip in enumerate(chips)] for a in range(len(srcs))]


def _tile(rows, cols):
    if rows <= 256:
        return rows, cols
    tr = 256 if cols <= 512 else 128
    if rows % tr == 0:
        return tr, cols
    return rows, 512


def _pair_sum(core, own, got, name):
    _, rows, cols = own.shape
    tr, tc = _tile(rows, cols)

    def body(c_ref, own_ref, got_ref, o_ref):
        o_ref[0] = own_ref[0] + got_ref[0]

    return pl.pallas_call(
        body, name=name,
        grid_spec=pltpu.PrefetchScalarGridSpec(
            num_scalar_prefetch=1, grid=(4, rows // tr, cols // tc),
            in_specs=[pl.BlockSpec((1, tr, tc), lambda k, i, j, c: (2 * k + c[0], i, j)),
                      pl.BlockSpec((1, tr, tc), lambda k, i, j, c: (k, i, j))],
            out_specs=pl.BlockSpec((1, tr, tc), lambda k, i, j, c: (k, i, j))),
        out_shape=jax.ShapeDtypeStruct((4, rows, cols), F32),
        compiler_params=_cparams(("parallel", "parallel", "parallel")),
    )(core, own, got)


def _adamw(w, g, m, v):
    m_new = ADAM_B1 * m + (1.0 - ADAM_B1) * g
    v_new = ADAM_B2 * v + (1.0 - ADAM_B2) * (g * g)
    m_hat = m_new / (1.0 - ADAM_B1 ** ADAM_STEP)
    v_hat = v_new / (1.0 - ADAM_B2 ** ADAM_STEP)
    delta = -ADAM_LR * (m_hat / (jnp.sqrt(v_hat) + ADAM_EPS) + ADAM_WD * w)
    return delta, m_new, v_new


def _sum_adam(chip, sums, parts, w, m, v, name):
    n_parts, rows, cols = parts.shape
    tr, tc = _tile(rows, cols)

    def body(chip_ref, own_ref, p_ref, w_ref, m_ref, v_ref, g_ref, d_ref, mo_ref, vo_ref):
        g = own_ref[0]
        for k in range(n_parts):
            g = g + p_ref[k]
        g_ref[...] = g
        d_ref[...], mo_ref[...], vo_ref[...] = _adamw(w_ref[...], g, m_ref[...], v_ref[...])

    tile = pl.BlockSpec((tr, tc), lambda i, j, ch: (i, j))
    out = jax.ShapeDtypeStruct((rows, cols), F32)
    return pl.pallas_call(
        body, name=name,
        grid_spec=pltpu.PrefetchScalarGridSpec(
            num_scalar_prefetch=1, grid=(rows // tr, cols // tc),
            in_specs=[pl.BlockSpec((1, tr, tc), lambda i, j, ch: (ch[0], i, j)),
                      pl.BlockSpec((n_parts, tr, tc), lambda i, j, ch: (0, i, j)), tile, tile, tile],
            out_specs=[tile, tile, tile, tile]),
        out_shape=[out, out, out, out],
        compiler_params=_cparams(("parallel", "parallel")),
    )(chip, sums, parts, w, m, v)


SHARDED = ("w_in", "gdn_conv_w", "w_out", "w_cq", "w_ckv", "w_co", "w_mlp1", "w_mlp2")
TRANSPOSED = ("w_in",)
COLUMN_SHARDED = ("gdn_conv_w", "w_co", "w_mlp1")
REPLICATED = ("norm_mix_g", "fox_qnorm_g", "fox_knorm_g", "fox_f_bias", "fox_onorm_g", "gdn_A_log", "gdn_dt_bias", "gdn_onorm_g",
              "norm_xattn_g", "mem_norm_g", "xattn_qnorm_g", "xattn_knorm_g", "norm_mlp_g")
WEIGHTS = ("norm_mix_g", "w_in", "fox_qnorm_g", "fox_knorm_g", "fox_f_bias", "fox_onorm_g", "gdn_conv_w", "gdn_A_log", "gdn_dt_bias",
           "gdn_onorm_g", "w_out", "norm_xattn_g", "mem_norm_g", "w_cq", "w_ckv", "xattn_qnorm_g", "xattn_knorm_g", "w_co",
           "norm_mlp_g", "w_mlp1", "w_mlp2")
PACK_ROWS = 16
LOSS_ROW = len(REPLICATED)


def _whole(name, gathered):
    if name in COLUMN_SHARDED:
        return gathered.transpose(1, 0, 2).reshape(gathered.shape[1], N_DEV * gathered.shape[2])
    return gathered.reshape(N_DEV * gathered.shape[1], gathered.shape[2])


def _whole_shape(name, shard_shape):
    rows, cols = shard_shape
    return (rows, N_DEV * cols) if name in COLUMN_SHARDED else (N_DEV * rows, cols)


def _blocks(name, whole):
    if whole.ndim == 3:
        return whole
    if name in COLUMN_SHARDED:
        rows, cols = whole.shape
        return whole.reshape(rows, N_DEV, cols // N_DEV).transpose(1, 0, 2)
    return whole.reshape(N_DEV, whole.shape[0] // N_DEV, whole.shape[1])


def _adam_small(everyone, ws, ms, vs):
    n_par = len(ws)

    def body(*refs):
        ev_ref = refs[0]
        w_refs, m_refs, v_refs = (refs[1 + j * n_par:1 + (j + 1) * n_par] for j in range(3))
        outs = refs[1 + 3 * n_par:-1]
        sum_ref = refs[-1]
        total = ev_ref[0]
        for dev in range(1, N_DEV):
            total = total + ev_ref[dev]
        sum_ref[...] = total
        for i in range(n_par):
            n = w_refs[i].shape[1]
            g = sum_ref[i:i + 1, 0:n]
            outs[4 * i][...] = g
            outs[4 * i + 1][...], outs[4 * i + 2][...], outs[4 * i + 3][...] = _adamw(w_refs[i][...], g, m_refs[i][...], v_refs[i][...])
        outs[4 * n_par][...] = sum_ref[LOSS_ROW:LOSS_ROW + 1, 0:1]

    shapes = [jax.ShapeDtypeStruct(a.shape, F32) for a in ws for _ in range(4)] + [jax.ShapeDtypeStruct((1, 1), F32)]
    return pl.pallas_call(body, name="adam_small", out_shape=shapes,
                          scratch_shapes=[pltpu.VMEM((PACK_ROWS, D_MODEL), F32)])(everyone, *ws, *ms, *vs)


def kernel(x, mem, norm_mix_g, w_in, fox_qnorm_g, fox_knorm_g, fox_f_bias, fox_onorm_g, gdn_conv_w, gdn_A_log, gdn_dt_bias, gdn_onorm_g, w_out, norm_xattn_g, mem_norm_g, w_cq, w_ckv, xattn_qnorm_g, xattn_knorm_g, w_co, norm_mlp_g, w_mlp1, w_mlp2, loss_target, m_norm_mix_g, m_w_in, m_fox_qnorm_g, m_fox_knorm_g, m_fox_f_bias, m_fox_onorm_g, m_gdn_conv_w, m_gdn_A_log, m_gdn_dt_bias, m_gdn_onorm_g, m_w_out, m_norm_xattn_g, m_mem_norm_g, m_w_cq, m_w_ckv, m_xattn_qnorm_g, m_xattn_knorm_g, m_w_co, m_norm_mlp_g, m_w_mlp1, m_w_mlp2, v_norm_mix_g, v_w_in, v_fox_qnorm_g, v_fox_knorm_g, v_fox_f_bias, v_fox_onorm_g, v_gdn_conv_w, v_gdn_A_log, v_gdn_dt_bias, v_gdn_onorm_g, v_w_out, v_norm_xattn_g, v_mem_norm_g, v_w_cq, v_w_ckv, v_xattn_qnorm_g, v_xattn_knorm_g, v_w_co, v_norm_mlp_g, v_w_mlp1, v_w_mlp2):
    given = dict(locals())
    w = {k: given[k] for k in WEIGHTS}
    m = {k: given["m_" + k] for k in WEIGHTS}
    v = {k: given["v_" + k] for k in WEIGHTS}

    core = lax.axis_index("c").astype(jnp.int32).reshape(1)
    chip = (2 * lax.axis_index("x") + lax.axis_index("y")).astype(jnp.int32).reshape(1)
    me = 4 * lax.axis_index("x") + 2 * lax.axis_index("y") + lax.axis_index("c")

    local = lambda d: {k: jnp.transpose(d[k][0]) if k in TRANSPOSED else d[k][0] for k in SHARDED}
    w2, m2, v2 = local(w), local(m), local(v)
    shards = {k: w2[k] if k == "gdn_conv_w" else w2[k].astype(BF16) for k in SHARDED}
    early = [k for k in SHARDED if not any(k in group for group in LATE_WEIGHTS)]
    gathered = _all_gather_hbm([shards[k] for k in early], "gather_early")
    whole = {k: _whole(k, g) for k, g in zip(early, gathered)}
    gathers, after = {}, gathered[0]
    for i, group in enumerate(LATE_WEIGHTS):
        lands = [lax.empty(_whole_shape(k, shards[k].shape), BF16) for k in group]
        gathers[group] = _copies_start("gather_late_start_" + str(i), [shards[k] for k in group], lands, _gather_copies, after=after)
        after = gathers[group][4]
    first_token = after[0, 0]

    def late_weights(group, after):
        gather = gathers[group]
        _, lands = _copies_wait("gather_late_wait_" + str(LATE_WEIGHTS.index(group)), gather[0], gather[1], gather[2], gather[3],
                                after, _gather_copies, own_block=True)
        return lands

    pending = []

    def grads_ready(group):
        names = list(group)
        tag = str(len(pending))
        own = [_blocks(k, group[k]) for k in names]
        if "w_in" in names:
            got = _pair_exchange(own, "grad_pair_exchange_" + tag)
            srcs = [_pair_sum(core, o, g, "grad_pair_sum_" + k) for k, o, g in zip(names, own, got)]
            copies, index, n_parts = _chip_copies, chip, 3
        else:
            srcs, copies, index, n_parts = own, _scatter_copies, me.astype(jnp.int32).reshape(1), 7
        lands = [lax.empty((n_parts,) + s.shape[1:], s.dtype) for s in srcs]
        started = _copies_start("grad_exchange_start_" + tag, srcs, lands, copies, after=core)
        pending.append((names, started, copies, index))
        return started[4][0, 0]

    small = {k: w[k] for k in REPLICATED}
    packed, grad_x, _ = _local_step(x, mem, loss_target, **small, **whole, late_weights=late_weights,
                                    grads_ready=grads_ready, first_token=first_token)

    small_lands = [lax.empty((N_DEV * PACK_ROWS, D_MODEL), F32)]
    small_gather = _copies_start("gather_small_start", [packed], small_lands, _gather_copies, after=grad_x)

    out_g, out_d, out_m, out_v = {}, {}, {}, {}
    after = small_gather[4]
    for tag, (names, started, copies, index) in enumerate(pending):
        srcs, parts = _copies_wait("grad_exchange_wait_" + str(tag), started[0], started[1], started[2], started[3], after, copies)
        for k, s, p in zip(names, srcs, parts):
            res = _sum_adam(index, s, p, w2[k], m2[k], v2[k], "adam_" + k)
            out_g[k], out_d[k], out_m[k], out_v[k] = ((jnp.transpose(r) if k in TRANSPOSED else r)[None] for r in res)
            after = res[0]

    _, (everyone,) = _copies_wait("gather_small_wait", small_gather[0], small_gather[1], small_gather[2], small_gather[3], after,
                                  _gather_copies, own_block=True)
    res = _adam_small(everyone.reshape(N_DEV, PACK_ROWS, D_MODEL), [w[k] for k in REPLICATED], [m[k] for k in REPLICATED],
                      [v[k] for k in REPLICATED])
    for i, k in enumerate(REPLICATED):
        out_g[k], out_d[k], out_m[k], out_v[k] = res[4 * i:4 * i + 4]
    loss = res[-1].reshape(())

    return (loss, grad_x, *[out_g[k] for k in WEIGHTS], *[out_d[k] for k in WEIGHTS], *[out_m[k] for k in WEIGHTS],
            *[out_v[k] for k in WEIGHTS])
```

```python
import functools

import jax
import jax.numpy as jnp
import numpy as np
from jax import lax
from jax.experimental import pallas as pl
from jax.experimental.pallas import tpu as pltpu

F32 = jnp.float32
BF16 = jnp.bfloat16

D_MODEL = 1024
FOX_HEADS = 8
FOX_HEAD_DIM = 64
FOX_WIDTH = 512
GDN_HEADS = 4
GDN_HEAD_DIM = 128
GDN_WIDTH = 512
CONV_WIDTH = 4
GDN_CHUNK = 128
GDN_GROUP = 4
FOX_BLOCK = 512
XATTN_HEADS = 4
XATTN_HEAD_DIM = 128
XATTN_WIDTH = 512
D_FF = 4096
EPS = 1e-6
NEG_INF = -1e30
N_DEV = 8

ADAM_LR = 0.001
ADAM_B1 = 0.9
ADAM_B2 = 0.999
ADAM_EPS = 1e-08
ADAM_WD = 0.01
ADAM_STEP = 10

P_FOX = 0
P_GDN = 1536
P_Z = 3072
P_SMALL = 3584
P_DIM = 3712
SM_F = 0
SM_B = 8
SM_A = 12
SM_ROWS = 16

LANES = 128
VMEM_LIMIT = 56 * 1024 * 1024

NN = (((1,), (0,)), ((), ()))
NT = (((1,), (1,)), ((), ()))
TN = (((0,), (0,)), ((), ()))


def _dot(a, b, dims=NN):
    return lax.dot_general(a.astype(BF16), b.astype(BF16), dims, preferred_element_type=F32)


def _cparams(sem=None):
    kw = dict(vmem_limit_bytes=VMEM_LIMIT)
    if sem is not None:
        kw["dimension_semantics"] = sem
    return pltpu.CompilerParams(**kw)


def _sigmoid(x):
    return 0.5 * (jnp.tanh(0.5 * x) + 1.0)


def _softplus(x):
    return jnp.maximum(x, 0.0) + jnp.log1p(jnp.exp(-jnp.abs(x)))


def _log_sigmoid(x):
    return -_softplus(-x)


def _rms(x, g):
    r = lax.rsqrt(jnp.mean(x * x, axis=-1, keepdims=True) + EPS)
    return x * r * g


def _rms_bwd(x, g, dy):
    r = lax.rsqrt(jnp.mean(x * x, axis=-1, keepdims=True) + EPS)
    xh = x * r
    dg = jnp.sum(dy * xh, axis=0, keepdims=True)
    dyg = dy * g
    dx = r * (dyg - xh * jnp.mean(dyg * xh, axis=-1, keepdims=True))
    return dx, dg


def _pair_stat(t, m0):
    s0 = jnp.sum(jnp.where(m0, t, 0.0), axis=-1, keepdims=True)
    s1 = jnp.sum(jnp.where(m0, 0.0, t), axis=-1, keepdims=True)
    return jnp.where(m0, s0, s1)


def _rms_pair(x, g, m0):
    r = lax.rsqrt(_pair_stat(x * x, m0) * (1.0 / FOX_HEAD_DIM) + EPS)
    return x * r * g


def _rms_pair_bwd(x, g, dy, m0):
    r = lax.rsqrt(_pair_stat(x * x, m0) * (1.0 / FOX_HEAD_DIM) + EPS)
    xh = x * r
    dg = jnp.sum(dy * xh, axis=0, keepdims=True)
    dyg = dy * g
    dx = r * (dyg - xh * (_pair_stat(dyg * xh, m0) * (1.0 / FOX_HEAD_DIM)))
    return dx, dg


@jax.custom_vjp
def _mm_nn(a, b):
    return _dot(a, b, NN)


_mm_nn.defvjp(lambda a, b: (_dot(a, b, NN), (a, b)),
              lambda r, g: (_dot(g, r[1], NT), _dot(r[0], g, TN)))


@jax.custom_vjp
def _mm_nt(a, b):
    return _dot(a, b, NT)


_mm_nt.defvjp(lambda a, b: (_dot(a, b, NT), (a, b)),
              lambda r, g: (_dot(g, r[1], NN), _dot(g, r[0], TN)))


@jax.custom_vjp
def _mm_tn(a, b):
    return _dot(a, b, TN)


_mm_tn.defvjp(lambda a, b: (_dot(a, b, TN), (a, b)),
              lambda r, g: (_dot(r[1], g, NT), _dot(r[0], g, NN)))


def _dot3(a, b, dims):
    ah = a.astype(BF16)
    al = (a - ah.astype(F32)).astype(BF16)
    bh = b.astype(BF16)
    bl = (b - bh.astype(F32)).astype(BF16)
    d = functools.partial(lax.dot_general, dimension_numbers=dims, preferred_element_type=F32)
    return d(ah, bh) + d(ah, bl) + d(al, bh)


def _neumann_inverses(mats):
    c = mats[0].shape[0]
    eye = (lax.broadcasted_iota(jnp.int32, (c, c), 0) == lax.broadcasted_iota(jnp.int32, (c, c), 1)).astype(F32)
    xs = [eye - a for a in mats]
    ps = list(mats)
    k = 2
    while k < c + 1:
        ps = [_dot3(p, p, NN) for p in ps]
        xs = [x + _dot3(x, p, NN) for x, p in zip(xs, ps)]
        k *= 2
    return xs


@jax.custom_vjp
def _unit_lower_inverses(mats):
    return _neumann_inverses(mats)


def _unit_lower_inverses_fwd(mats):
    ts = _neumann_inverses(mats)
    return ts, ts


def _unit_lower_inverses_bwd(ts, gs):
    left = [_dot3(t, g, TN) for t, g in zip(ts, gs)]
    return ([-_dot3(m, t, NT) for m, t in zip(left, ts)],)


_unit_lower_inverses.defvjp(_unit_lower_inverses_fwd, _unit_lower_inverses_bwd)


def _wgrad(a, b, name, bk=1024, bn=1024, bt=1024, column_blocks=None):
    t_len, k_len = a.shape
    n_len = b.shape[1]
    bk, bn, bt = min(bk, k_len), min(bn, n_len), min(bt, t_len)
    nt = t_len // bt

    def body(a_ref, b_ref, o_ref, acc_ref):
        t = pl.program_id(2)

        @pl.when(t == 0)
        def _():
            acc_ref[...] = jnp.zeros_like(acc_ref)

        acc_ref[...] += _dot(a_ref[...], b_ref[...], TN)

        @pl.when(t == nt - 1)
        def _():
            if column_blocks:
                for jj in range(bn // column_blocks):
                    o_ref[jj] = acc_ref[:, jj * column_blocks:(jj + 1) * column_blocks]
            else:
                o_ref[...] = acc_ref[...]

    if column_blocks:
        out_spec = pl.BlockSpec((bn // column_blocks, bk, column_blocks), lambda i, j, t: (j, i, 0))
        out_shape = jax.ShapeDtypeStruct((n_len // column_blocks, k_len, column_blocks), F32)
    else:
        out_spec = pl.BlockSpec((bk, bn), lambda i, j, t: (i, j))
        out_shape = jax.ShapeDtypeStruct((k_len, n_len), F32)
    return pl.pallas_call(
        body, name=name, grid=(k_len // bk, n_len // bn, nt),
        in_specs=[pl.BlockSpec((bt, bk), lambda i, j, t: (t, i)), pl.BlockSpec((bt, bn), lambda i, j, t: (t, j))],
        out_specs=out_spec, out_shape=out_shape,
        scratch_shapes=[pltpu.VMEM((bk, bn), F32)],
        compiler_params=_cparams(("parallel", "parallel", "arbitrary")),
    )(a, b)


def _wgrad_stacked(pieces, b, name, bn=512, bt=1024):
    t_len, n_len = b.shape
    n_p = len(pieces)
    starts = [int(s) for s in np.cumsum([0] + [p.shape[1] for p in pieces])]
    bn, bt = min(bn, n_len), min(bt, t_len)
    nt = t_len // bt

    def body(*refs):
        b_ref, o_ref, acc_ref = refs[n_p:]
        t = pl.program_id(1)

        @pl.when(t == 0)
        def _():
            acc_ref[...] = jnp.zeros_like(acc_ref)

        for k in range(n_p):
            acc_ref[starts[k]:starts[k + 1], :] += _dot(refs[k][...], b_ref[...], TN)

        @pl.when(t == nt - 1)
        def _():
            o_ref[...] = acc_ref[...]

    return pl.pallas_call(
        body, name=name, grid=(n_len // bn, nt),
        in_specs=[pl.BlockSpec((bt, p.shape[1]), lambda j, t: (t, 0)) for p in pieces] + [pl.BlockSpec((bt, bn), lambda j, t: (t, j))],
        out_specs=pl.BlockSpec((starts[-1], bn), lambda j, t: (0, j)),
        out_shape=jax.ShapeDtypeStruct((starts[-1], n_len), F32),
        scratch_shapes=[pltpu.VMEM((starts[-1], bn), F32)],
        compiler_params=_cparams(("parallel", "arbitrary")),
    )(*pieces, b)


def _rows_matmul(a, b, name, bt=512):
    r_len, t_len = a.shape
    n_len = b.shape[1]
    bt = min(bt, t_len)
    nt = t_len // bt

    def body(a_ref, b_ref, o_ref):
        t = pl.program_id(0)

        @pl.when(t == 0)
        def _():
            o_ref[...] = jnp.zeros_like(o_ref)

        o_ref[...] += _dot(a_ref[...], b_ref[...], NN)

    return pl.pallas_call(
        body, name=name, grid=(nt,),
        in_specs=[pl.BlockSpec((r_len, bt), lambda t: (0, t)), pl.BlockSpec((bt, n_len), lambda t: (t, 0))],
        out_specs=pl.BlockSpec((r_len, n_len), lambda t: (0, 0)),
        out_shape=jax.ShapeDtypeStruct((r_len, n_len), F32),
        compiler_params=_cparams(("arbitrary",)),
    )(a, b)


def _in_proj(x, g, wp, wst, tm=512):
    t_len, d = x.shape
    tm = min(tm, t_len)

    def body(x_ref, g_ref, wp_ref, wst_ref, h_ref, fox_ref, gdn_ref, z_ref, sm_ref, smt_ref):
        h = _rms(x_ref[...], g_ref[...]).astype(BF16)
        h_ref[...] = h
        p = _dot(h, wp_ref[...], NT)
        fox_ref[...] = p[:, P_FOX:P_GDN]
        gdn_ref[...] = p[:, P_GDN:P_Z]
        z_ref[...] = p[:, P_Z:P_SMALL]
        sm_ref[...] = p[:, P_SMALL:P_DIM]
        smt_ref[...] = _dot(wst_ref[...], h, NT)

    row = lambda i: (i, 0)
    fixed = lambda i: (0, 0)
    return pl.pallas_call(
        body, name="in_proj", grid=(t_len // tm,),
        in_specs=[pl.BlockSpec((tm, d), row), pl.BlockSpec((1, d), fixed), _resident((P_DIM, d)),
                  pl.BlockSpec((SM_ROWS, d), fixed)],
        out_specs=[pl.BlockSpec((tm, d), row), pl.BlockSpec((tm, 1536), row), pl.BlockSpec((tm, 1536), row),
                   pl.BlockSpec((tm, 512), row), pl.BlockSpec((tm, LANES), row), pl.BlockSpec((SM_ROWS, tm), lambda i: (0, i))],
        out_shape=[jax.ShapeDtypeStruct((t_len, d), BF16), jax.ShapeDtypeStruct((t_len, 1536), F32),
                   jax.ShapeDtypeStruct((t_len, 1536), F32), jax.ShapeDtypeStruct((t_len, 512), F32),
                   jax.ShapeDtypeStruct((t_len, LANES), F32), jax.ShapeDtypeStruct((SM_ROWS, t_len), F32)],
        compiler_params=_cparams(("parallel",)),
    )(x, g, wp, wst)


def _in_proj_bwd(dprojs, dsmt, x, g, wp, wst, dx1, tm=512):
    t_len, d = x.shape
    tm = min(tm, t_len)
    n_p = len(dprojs)
    starts = np.cumsum([0] + [p.shape[1] for p in dprojs])

    def body(*refs):
        dp_refs = refs[:n_p]
        dst_ref, x_ref, g_ref, wp_ref, wst_ref, dx1_ref, dx_ref, dg_ref = refs[n_p:]
        i = pl.program_id(0)
        dh = _dot(dst_ref[...], wst_ref[...], TN)
        for k in range(n_p):
            dh = dh + _dot(dp_refs[k][...], wp_ref[int(starts[k]):int(starts[k + 1]), :], NN)
        dxn, dg = _rms_bwd(x_ref[...], g_ref[...], dh)
        dx_ref[...] = dx1_ref[...] + dxn

        @pl.when(i == 0)
        def _():
            dg_ref[...] = jnp.zeros_like(dg_ref)

        dg_ref[...] += dg

    row = lambda i: (i, 0)
    fixed = lambda i: (0, 0)
    return pl.pallas_call(
        body, name="in_proj_bwd", grid=(t_len // tm,),
        in_specs=[pl.BlockSpec((tm, p.shape[1]), row) for p in dprojs] + [
            pl.BlockSpec((SM_ROWS, tm), lambda i: (0, i)), pl.BlockSpec((tm, d), row),
            pl.BlockSpec((1, d), fixed), _resident((P_DIM, d)), pl.BlockSpec((SM_ROWS, d), fixed),
            pl.BlockSpec((tm, d), row)],
        out_specs=[pl.BlockSpec((tm, d), row), pl.BlockSpec((1, d), fixed)],
        out_shape=[jax.ShapeDtypeStruct((t_len, d), F32), jax.ShapeDtypeStruct((1, d), F32)],
        compiler_params=_cparams(("arbitrary",)),
    )(*dprojs, dsmt, x, g, wp, wst, dx1)


def _fox_cum(smt, bias_col, n_batch, s_len, ck=256):
    ck = min(ck, s_len)

    def body(s_ref, b_ref, c_ref):
        tri = (lax.broadcasted_iota(jnp.int32, (ck, ck), 0) <= lax.broadcasted_iota(jnp.int32, (ck, ck), 1)).astype(F32)
        carry = jnp.zeros((SM_ROWS, 1), F32)
        for r in range(s_len // ck):
            ls = _log_sigmoid(s_ref[:, r * ck:(r + 1) * ck] + b_ref[...])
            c = jnp.dot(ls, tri, precision=lax.Precision.HIGHEST, preferred_element_type=F32) + carry
            c_ref[:, r * ck:(r + 1) * ck] = c
            carry = c[:, ck - 1:ck]

    return pl.pallas_call(
        body, name="fox_cum", grid=(n_batch,),
        in_specs=[pl.BlockSpec((SM_ROWS, s_len), lambda b: (0, b)), pl.BlockSpec((SM_ROWS, 1), lambda b: (0, 0))],
        out_specs=pl.BlockSpec((SM_ROWS, s_len), lambda b: (0, b)),
        out_shape=jax.ShapeDtypeStruct(smt.shape, F32),
        compiler_params=_cparams(("parallel",)),
    )(smt, bias_col)


def _fox_cum_bwd(dc, smt, bias_col, n_batch, s_len, ck=256):
    ck = min(ck, s_len)
    nr = s_len // ck

    def body(dc_ref, s_ref, b_ref, dl_ref, db_ref):
        b = pl.program_id(0)
        tri = (lax.broadcasted_iota(jnp.int32, (ck, ck), 0) >= lax.broadcasted_iota(jnp.int32, (ck, ck), 1)).astype(F32)
        carry = jnp.zeros((SM_ROWS, 1), F32)
        tot = jnp.zeros((SM_ROWS, 1), F32)
        for r in reversed(range(nr)):
            sl = slice(r * ck, (r + 1) * ck)
            dls = jnp.dot(dc_ref[:, sl], tri, precision=lax.Precision.HIGHEST, preferred_element_type=F32) + carry
            carry = dls[:, 0:1]
            dl = dls * (1.0 - _sigmoid(s_ref[:, sl] + b_ref[...]))
            dl_ref[:, sl] = dl
            tot = tot + jnp.sum(dl, axis=1, keepdims=True)

        @pl.when(b == 0)
        def _():
            db_ref[...] = jnp.zeros_like(db_ref)

        db_ref[...] += jnp.broadcast_to(tot, db_ref.shape)

    return pl.pallas_call(
        body, name="fox_cum_bwd", grid=(n_batch,),
        in_specs=[pl.BlockSpec((SM_ROWS, s_len), lambda b: (0, b)), pl.BlockSpec((SM_ROWS, s_len), lambda b: (0, b)),
                  pl.BlockSpec((SM_ROWS, 1), lambda b: (0, 0))],
        out_specs=[pl.BlockSpec((SM_ROWS, s_len), lambda b: (0, b)), pl.BlockSpec((SM_ROWS, LANES), lambda b: (0, 0))],
        out_shape=[jax.ShapeDtypeStruct(smt.shape, F32), jax.ShapeDtypeStruct((SM_ROWS, LANES), F32)],
        compiler_params=_cparams(("arbitrary",)),
    )(dc, smt, bias_col)


def _fox_diagonal_mask(tq):
    return lax.broadcasted_iota(jnp.int32, (tq, tq), 1) <= lax.broadcasted_iota(jnp.int32, (tq, tq), 0)


def _fox_fwd(pf, cb, gq2, gk2, go2, tq=256):
    n_batch, s_len, _ = pf.shape
    tq = min(tq, s_len)
    nq = s_len // tq
    scale = FOX_HEAD_DIM ** -0.5

    def body(q_ref, k_ref, v_ref, c_ref, gq_ref, gk_ref, go_ref, o_ref, on_ref, lse_ref, kh_ref, vh_ref):
        j = pl.program_id(1)
        i = pl.program_id(2)
        m0 = lax.broadcasted_iota(jnp.int32, (1, LANES), 1) < FOX_HEAD_DIM

        @pl.when(i == 0)
        def _():
            kn = _rms_pair(k_ref[0], gk_ref[...], m0)
            kh_ref[0] = jnp.where(m0, kn, 0.0).astype(BF16)
            kh_ref[1] = jnp.where(m0, 0.0, kn).astype(BF16)
            v = v_ref[0]
            vh_ref[0] = jnp.where(m0, v, 0.0).astype(BF16)
            vh_ref[1] = jnp.where(m0, 0.0, v).astype(BF16)

        qb = (_rms_pair(q_ref[0], gq_ref[...], m0) * scale).astype(BF16)

        def step(kb, carry, diagonal=False):
            ms, ls, acc = carry
            off = pl.multiple_of(kb * tq, tq)
            new_m, new_l, alphas, pv = [], [], [], []
            for hh in range(2):
                s = _dot(qb, kh_ref[hh, pl.ds(off, tq), :], NT)
                s = s - c_ref[0, kb, pl.ds(2 * j + hh, 1), :]
                if diagonal:
                    s = jnp.where(_fox_diagonal_mask(tq), s, NEG_INF)
                m_new = jnp.maximum(ms[hh], jnp.max(s, axis=-1, keepdims=True))
                alpha = jnp.exp(ms[hh] - m_new)
                p = jnp.exp(s - m_new)
                new_l.append(alpha * ls[hh] + jnp.sum(p, axis=-1, keepdims=True))
                new_m.append(m_new)
                alphas.append(alpha)
                pv.append(_dot(p, vh_ref[hh, pl.ds(off, tq), :], NN))
            acc = jnp.where(m0, alphas[0], alphas[1]) * acc + pv[0] + pv[1]
            return tuple(new_m), tuple(new_l), acc

        init_m = (jnp.full((tq, 1), NEG_INF, F32),) * 2
        init_l = (jnp.zeros((tq, 1), F32),) * 2
        carry = lax.fori_loop(0, i, step, (init_m, init_l, jnp.zeros((tq, LANES), F32)))
        ms, ls, acc = step(i, carry, diagonal=True)
        o = acc / jnp.where(m0, ls[0], ls[1])
        o_ref[0] = o
        on_ref[0] = _rms_pair(o, go_ref[...], m0).astype(BF16)
        lse_ref[0] = jnp.where(m0, ms[0] + jnp.log(ls[0]), ms[1] + jnp.log(ls[1]))

    fixed = lambda b, j, i: (0, 0)
    tile = lambda b, j, i: (b, i, j)
    return pl.pallas_call(
        body, name="fox_fwd", grid=(n_batch, 4, nq),
        in_specs=[pl.BlockSpec((1, tq, LANES), tile), pl.BlockSpec((1, s_len, LANES), lambda b, j, i: (b, 0, 4 + j)),
                  pl.BlockSpec((1, s_len, LANES), lambda b, j, i: (b, 0, 8 + j)),
                  pl.BlockSpec((1, nq, SM_ROWS, tq), lambda b, j, i: (b, 0, 0, 0)),
                  pl.BlockSpec((1, LANES), fixed), pl.BlockSpec((1, LANES), fixed), pl.BlockSpec((1, LANES), fixed)],
        out_specs=[pl.BlockSpec((1, tq, LANES), tile), pl.BlockSpec((1, tq, LANES), tile), pl.BlockSpec((1, tq, LANES), tile)],
        out_shape=[jax.ShapeDtypeStruct((n_batch, s_len, FOX_WIDTH), F32), jax.ShapeDtypeStruct((n_batch, s_len, FOX_WIDTH), BF16),
                   jax.ShapeDtypeStruct((n_batch, s_len, FOX_WIDTH), F32)],
        scratch_shapes=[pltpu.VMEM((2, s_len, LANES), BF16), pltpu.VMEM((2, s_len, LANES), BF16)],
        compiler_params=_cparams(("parallel", "parallel", "arbitrary")),
    )(pf, pf, pf, cb, gq2, gk2, go2)


def _fox_bwd(pf, cb, gq2, gk2, go2, o, lse, don, tq=256):
    n_batch, s_len, _ = pf.shape
    tq = min(tq, s_len)
    nq = s_len // tq
    scale = FOX_HEAD_DIM ** -0.5

    def body(q_ref, k_ref, v_ref, c_ref, gq_ref, gk_ref, go_ref, o_ref, lse_ref, don_ref,
             dq_ref, dk_ref, dv_ref, dc_ref, dgq_ref, dgk_ref, dgo_ref, kh_ref, vh_ref, dka_ref, dva_ref, dca_ref):
        b = pl.program_id(0)
        j = pl.program_id(1)
        i = pl.program_id(2)
        m0 = lax.broadcasted_iota(jnp.int32, (1, LANES), 1) < FOX_HEAD_DIM

        @pl.when((b == 0) & (j == 0) & (i == 0))
        def _():
            dgq_ref[...] = jnp.zeros_like(dgq_ref)
            dgk_ref[...] = jnp.zeros_like(dgk_ref)
            dgo_ref[...] = jnp.zeros_like(dgo_ref)

        @pl.when(i == 0)
        def _():
            kn = _rms_pair(k_ref[0], gk_ref[...], m0)
            kh_ref[0] = jnp.where(m0, kn, 0.0).astype(BF16)
            kh_ref[1] = jnp.where(m0, 0.0, kn).astype(BF16)
            v = v_ref[0]
            vh_ref[0] = jnp.where(m0, v, 0.0).astype(BF16)
            vh_ref[1] = jnp.where(m0, 0.0, v).astype(BF16)
            dka_ref[...] = jnp.zeros_like(dka_ref)
            dva_ref[...] = jnp.zeros_like(dva_ref)
            dca_ref[...] = jnp.zeros_like(dca_ref)

        q = q_ref[0]
        qn = _rms_pair(q, gq_ref[...], m0)
        qs = qn * scale
        qb = qs.astype(BF16)
        qh = (jnp.where(m0, qs, 0.0).astype(BF16), jnp.where(m0, 0.0, qs).astype(BF16))
        ot = o_ref[0]
        do, dgo = _rms_pair_bwd(ot, go_ref[...], don_ref[0], m0)
        dgo_ref[...] += dgo
        dd = do * ot
        delta = (jnp.sum(jnp.where(m0, dd, 0.0), axis=-1, keepdims=True), jnp.sum(jnp.where(m0, 0.0, dd), axis=-1, keepdims=True))
        doh = (jnp.where(m0, do, 0.0).astype(BF16), jnp.where(m0, 0.0, do).astype(BF16))
        lse_t = lse_ref[0]
        lse_h = (lse_t[:, 0:1], lse_t[:, FOX_HEAD_DIM:FOX_HEAD_DIM + 1])

        def step(kb, carry, diagonal=False):
            dqn, rs = carry
            rs = list(rs)
            off = pl.multiple_of(kb * tq, tq)
            for hh in range(2):
                kblk = kh_ref[hh, pl.ds(off, tq), :]
                vblk = vh_ref[hh, pl.ds(off, tq), :]
                s = _dot(qb, kblk, NT)
                s = s - c_ref[0, kb, pl.ds(2 * j + hh, 1), :]
                if diagonal:
                    s = jnp.where(_fox_diagonal_mask(tq), s, NEG_INF)
                p = jnp.exp(s - lse_h[hh])
                dp = _dot(doh[hh], vblk, NT)
                ds = p * (dp - delta[hh])
                dva_ref[pl.ds(off, tq), :] += _dot(p, doh[hh], TN)
                dka_ref[pl.ds(off, tq), :] += _dot(ds, qh[hh], TN)
                dca_ref[kb, hh:hh + 1, :] += -jnp.sum(ds, axis=0, keepdims=True)
                rs[hh] = rs[hh] + jnp.sum(ds, axis=-1, keepdims=True)
                dqn = dqn + _dot(ds, kblk, NN)
            return dqn, tuple(rs)

        carry = lax.fori_loop(0, i, step, (jnp.zeros((tq, LANES), F32), (jnp.zeros((tq, 1), F32),) * 2))
        dqn, rs = step(i, carry, diagonal=True)
        dqn = dqn * scale
        rs_rows = jnp.where(m0, rs[0], rs[1]).T
        dca_ref[i, 0:1, :] += rs_rows[0:1, :]
        dca_ref[i, 1:2, :] += rs_rows[FOX_HEAD_DIM:FOX_HEAD_DIM + 1, :]
        dq, dgq = _rms_pair_bwd(q, gq_ref[...], dqn, m0)
        dq_ref[0] = dq.astype(BF16)
        dgq_ref[...] += dgq

        @pl.when(i == nq - 1)
        def _():
            dk, dgk = _rms_pair_bwd(k_ref[0], gk_ref[...], dka_ref[...], m0)
            dk_ref[0] = dk.astype(BF16)
            dgk_ref[...] += dgk
            dv_ref[0] = dva_ref[...].astype(BF16)
            dc_ref[0, 0] = dca_ref[...]

    fixed = lambda b, j, i: (0, 0)
    tile = lambda b, j, i: (b, i, j)
    full = lambda b, j, i: (b, 0, j)
    wide = jax.ShapeDtypeStruct((n_batch, s_len, FOX_WIDTH), BF16)
    gain = jax.ShapeDtypeStruct((1, LANES), F32)
    return pl.pallas_call(
        body, name="fox_bwd", grid=(n_batch, 4, nq),
        in_specs=[pl.BlockSpec((1, tq, LANES), tile), pl.BlockSpec((1, s_len, LANES), lambda b, j, i: (b, 0, 4 + j)),
                  pl.BlockSpec((1, s_len, LANES), lambda b, j, i: (b, 0, 8 + j)),
                  pl.BlockSpec((1, nq, SM_ROWS, tq), lambda b, j, i: (b, 0, 0, 0)),
                  pl.BlockSpec((1, LANES), fixed), pl.BlockSpec((1, LANES), fixed), pl.BlockSpec((1, LANES), fixed),
                  pl.BlockSpec((1, tq, LANES), tile), pl.BlockSpec((1, tq, LANES), tile), pl.BlockSpec((1, tq, LANES), tile)],
        out_specs=[pl.BlockSpec((1, tq, LANES), tile), pl.BlockSpec((1, s_len, LANES), full), pl.BlockSpec((1, s_len, LANES), full),
                   pl.BlockSpec((1, 1, nq, 8, tq), lambda b, j, i: (b, j, 0, 0, 0)),
                   pl.BlockSpec((1, LANES), fixed), pl.BlockSpec((1, LANES), fixed), pl.BlockSpec((1, LANES), fixed)],
        out_shape=[wide, wide, wide, jax.ShapeDtypeStruct((n_batch, 4, nq, 8, tq), F32), gain, gain, gain],
        scratch_shapes=[pltpu.VMEM((2, s_len, LANES), BF16), pltpu.VMEM((2, s_len, LANES), BF16),
                        pltpu.VMEM((s_len, LANES), F32), pltpu.VMEM((s_len, LANES), F32), pltpu.VMEM((nq, 8, tq), F32)],
        compiler_params=_cparams(("arbitrary", "arbitrary", "arbitrary")),
    )(pf, pf, pf, cb, gq2, gk2, go2, o, lse, don)


def _conv_padded(x_ref, w, pad_ref, s_len):
    pad_ref[0:8, :] = jnp.zeros((8, pad_ref.shape[1]), F32)
    pad_ref[8:8 + s_len, :] = x_ref[0]
    return (w[3:4] * pad_ref[8:8 + s_len, :] + w[2:3] * pad_ref[7:7 + s_len, :] + w[1:2] * pad_ref[6:6 + s_len, :]
            + w[0:1] * pad_ref[5:5 + s_len, :])


def _gdn_pre(pg, conv_w):
    n_batch, s_len, width = pg.shape
    bw = GDN_WIDTH
    hd = GDN_HEAD_DIM

    def body(x_ref, w_ref, o_ref, pad_ref):
        part = pl.program_id(1)
        y = _conv_padded(x_ref, w_ref[...], pad_ref, s_len)
        s = y * _sigmoid(y)
        for h in range(GDN_HEADS):
            sh = s[:, h * hd:(h + 1) * hd]
            sn = sh * lax.rsqrt(jnp.sum(sh * sh, axis=-1, keepdims=True) + EPS)
            o_ref[0, :, h * hd:(h + 1) * hd] = jnp.where(part < 2, sn, sh)

    return pl.pallas_call(
        body, name="gdn_pre", grid=(n_batch, width // bw),
        in_specs=[pl.BlockSpec((1, s_len, bw), lambda b, c: (b, 0, c)), pl.BlockSpec((8, bw), lambda b, c: (0, c))],
        out_specs=pl.BlockSpec((1, s_len, bw), lambda b, c: (b, 0, c)),
        out_shape=jax.ShapeDtypeStruct(pg.shape, F32),
        scratch_shapes=[pltpu.VMEM((s_len + 8, bw), F32)],
        compiler_params=_cparams(("parallel", "parallel")),
    )(pg, conv_w)


def _gdn_pre_bwd(pg, conv_w, dout):
    n_batch, s_len, width = pg.shape
    ncb = width // LANES

    def body(x_ref, w_ref, d_ref, dx_ref, dw_ref, pad_ref, tail_ref):
        cb = pl.program_id(0)
        b = pl.program_id(1)
        x = x_ref[0]
        w = w_ref[...]
        d = d_ref[0]
        y = _conv_padded(x_ref, w, pad_ref, s_len)
        sig = _sigmoid(y)
        s = y * sig
        rr = lax.rsqrt(jnp.sum(s * s, axis=-1, keepdims=True) + EPS)
        sn = s * rr
        ds_n = rr * (d - sn * jnp.sum(d * sn, axis=-1, keepdims=True))
        ds = jnp.where(cb < 2 * GDN_HEADS, ds_n, d)
        dy = ds * (sig * (1.0 + y * (1.0 - sig)))
        tail_ref[0:s_len, :] = dy
        tail_ref[s_len:s_len + 8, :] = jnp.zeros((8, LANES), F32)
        dyu = [tail_ref[3 - jj:3 - jj + s_len, :] for jj in range(CONV_WIDTH)]
        dx = w[0:1] * dyu[0] + w[1:2] * dyu[1] + w[2:3] * dyu[2] + w[3:4] * dyu[3]
        dx_ref[0] = dx.astype(BF16)
        dw = [jnp.sum(dyu[jj] * x, axis=0, keepdims=True) for jj in range(CONV_WIDTH)]
        rows = lax.broadcasted_iota(jnp.int32, (8, LANES), 0)
        dwb = jnp.zeros((8, LANES), F32)
        for jj in range(CONV_WIDTH):
            dwb = dwb + jnp.where(rows == jj, dw[jj], 0.0)

        @pl.when(b == 0)
        def _():
            dw_ref[...] = jnp.zeros_like(dw_ref)

        dw_ref[...] += dwb

    blk = lambda c, b: (b, 0, c)
    return pl.pallas_call(
        body, name="gdn_pre_bwd", grid=(ncb, n_batch),
        in_specs=[pl.BlockSpec((1, s_len, LANES), blk), pl.BlockSpec((8, LANES), lambda c, b: (0, c)), pl.BlockSpec((1, s_len, LANES), blk)],
        out_specs=[pl.BlockSpec((1, s_len, LANES), blk), pl.BlockSpec((8, LANES), lambda c, b: (0, c))],
        out_shape=[jax.ShapeDtypeStruct(pg.shape, BF16), jax.ShapeDtypeStruct((8, width), F32)],
        scratch_shapes=[pltpu.VMEM((s_len + 8, LANES), F32), pltpu.VMEM((s_len + 8, LANES), F32)],
        compiler_params=_cparams(("parallel", "arbitrary")),
    )(pg, conv_w, dout)


def _gdn_gates(smc, smr, a_c, dt_c, a_r, dt_r, h):
    lane = lax.broadcasted_iota(jnp.int32, (1, LANES), 1)
    sub = lax.broadcasted_iota(jnp.int32, (SM_ROWS, 1), 0)
    beta_c = jnp.sum(jnp.where(lane == SM_B + h, _sigmoid(smc), 0.0), axis=1, keepdims=True)
    g_all_c = -jnp.exp(a_c) * _softplus(smc + dt_c)
    g_c = jnp.sum(jnp.where(lane == SM_A + h, g_all_c, 0.0), axis=1, keepdims=True)
    g_all_r = -jnp.exp(a_r) * _softplus(smr + dt_r)
    g_r = jnp.sum(jnp.where(sub == SM_A + h, g_all_r, 0.0), axis=0, keepdims=True)
    return beta_c, g_c, g_r


@jax.custom_vjp
def _known_inverse(a, t):
    return t


_known_inverse.defvjp(lambda a, t: (t, t),
                      lambda t, g: (-_dot3(_dot3(t, g, TN), t, NT), jnp.zeros_like(t)))


def _gdn_group(qkv, z, smc, smr, a_c, dt_c, a_r, dt_r, go, states, inverses=None):
    n_grp = len(qkv)
    c = qkv[0].shape[0]
    hd = GDN_HEAD_DIM
    pairs = [(g, h) for g in range(n_grp) for h in range(GDN_HEADS)]
    ii = lax.broadcasted_iota(jnp.int32, (c, c), 0)
    jj = lax.broadcasted_iota(jnp.int32, (c, c), 1)
    incl = ii >= jj
    col = lambda arr, base, h: arr[:, base + h * hd:base + (h + 1) * hd]

    qs, ks, kbs, vbs, gcs, g_lasts, amats, intras = [], [], [], [], [], [], [], []
    for g, h in pairs:
        beta_c, g_c, g_r = _gdn_gates(smc[g], smr[g], a_c, dt_c, a_r, dt_r, h)
        gc_c = jnp.sum(jnp.where(incl, g_r, 0.0), axis=1, keepdims=True)
        gc_r = jnp.sum(jnp.where(ii <= jj, g_c, 0.0), axis=0, keepdims=True)
        decay = jnp.where(incl, jnp.exp(jnp.where(incl, gc_c - gc_r, 0.0)), 0.0)
        k = col(qkv[g], GDN_WIDTH, h)
        kb = k * beta_c
        qs.append(col(qkv[g], 0, h) * (hd ** -0.5))
        ks.append(k)
        kbs.append(kb)
        vbs.append(col(qkv[g], 2 * GDN_WIDTH, h) * beta_c)
        gcs.append(gc_c)
        g_lasts.append(jnp.sum(g_c, axis=0, keepdims=True))
        both = _mm_nt(jnp.concatenate([kb, qs[-1]], axis=0), k)
        amats.append(jnp.where(ii > jj, both[0:c] * decay, 0.0))
        intras.append(both[c:2 * c] * decay)
    ts = _unit_lower_inverses(amats) if inverses is None else [_known_inverse(a, t) for a, t in zip(amats, inverses)]
    egcs = [jnp.exp(gc) for gc in gcs]
    uws = [_mm_nn(t, jnp.concatenate([vb, kb * e], axis=1)) for t, vb, kb, e in zip(ts, vbs, kbs, egcs)]
    us = [uw[:, 0:hd] for uw in uws]
    ws = [uw[:, hd:2 * hd] for uw in uws]
    qes = [q * e for q, e in zip(qs, egcs)]
    kds = [k * jnp.exp(gl - gc) for k, gl, gc in zip(ks, g_lasts, gcs)]
    sdecs = [jnp.exp(gl) for gl in g_lasts]

    outs = []
    for g in range(n_grp):
        idx = [g * GDN_HEADS + h for h in range(GDN_HEADS)]
        v_new = [us[i] - _mm_nn(ws[i], states[h]) for h, i in enumerate(idx)]
        o = [_mm_nn(jnp.concatenate([qes[i], intras[i]], axis=1), jnp.concatenate([states[h], v_new[h]], axis=0))
             for h, i in enumerate(idx)]
        states = [states[h] * sdecs[i] + _mm_tn(kds[i], v_new[h]) for h, i in enumerate(idx)]
        outs.append([_rms(o[h], go) * (col(z[g], 0, h) * _sigmoid(col(z[g], 0, h))) for h in range(GDN_HEADS)])
    return outs, states, ts


def _gdn_group_size(n_chunks):
    return GDN_GROUP if n_chunks % GDN_GROUP == 0 else 1


def _gdn_fwd(qkvn, z, smc, smr, a_c, dt_c, a_r, dt_r, go):
    n_batch, s_len, _ = qkvn.shape
    c = GDN_CHUNK
    n = s_len // c
    grp = _gdn_group_size(n)
    ng = n // grp
    gc = grp * c
    hd = GDN_HEAD_DIM

    def body(qkv_ref, z_ref, smc_ref, smr_ref, ac_ref, dc_ref, ar_ref, dr_ref, go_ref, og_ref, st_ref, inv_ref, s_ref):
        @pl.when(pl.program_id(1) == 0)
        def _():
            s_ref[...] = jnp.zeros_like(s_ref)

        states = [s_ref[h] for h in range(GDN_HEADS)]
        for h in range(GDN_HEADS):
            st_ref[0, 0, h] = states[h]
        rows = lambda k: slice(k * c, (k + 1) * c)
        outs, nxt, invs = _gdn_group([qkv_ref[0, rows(k), :] for k in range(grp)], [z_ref[0, rows(k), :] for k in range(grp)],
                                     [smc_ref[0, rows(k), :] for k in range(grp)], [smr_ref[k] for k in range(grp)],
                                     ac_ref[...], dc_ref[...], ar_ref[...], dr_ref[...], go_ref[...], states)
        for k in range(grp):
            for h in range(GDN_HEADS):
                og_ref[0, rows(k), h * hd:(h + 1) * hd] = outs[k][h].astype(BF16)
        for p, inv in enumerate(invs):
            inv_ref[0, 0, p] = inv
        for h in range(GDN_HEADS):
            s_ref[h] = nxt[h]

    tok = lambda b, i: (b, i, 0)
    fixed = lambda b, i: (0, 0)
    return pl.pallas_call(
        body, name="gdn_fwd", grid=(n_batch, ng),
        in_specs=[pl.BlockSpec((1, gc, 3 * GDN_WIDTH), tok), pl.BlockSpec((1, gc, GDN_WIDTH), tok), pl.BlockSpec((1, gc, LANES), tok),
                  pl.BlockSpec((grp, SM_ROWS, c), lambda b, i: (b * ng + i, 0, 0)),
                  pl.BlockSpec((1, LANES), fixed), pl.BlockSpec((1, LANES), fixed), pl.BlockSpec((SM_ROWS, 1), fixed),
                  pl.BlockSpec((SM_ROWS, 1), fixed), pl.BlockSpec((1, LANES), fixed)],
        out_specs=[pl.BlockSpec((1, gc, GDN_WIDTH), tok), pl.BlockSpec((1, 1, GDN_HEADS, hd, hd), lambda b, i: (b, i, 0, 0, 0)),
                   pl.BlockSpec((1, 1, grp * GDN_HEADS, c, c), lambda b, i: (b, i, 0, 0, 0))],
        out_shape=[jax.ShapeDtypeStruct((n_batch, s_len, GDN_WIDTH), BF16), jax.ShapeDtypeStruct((n_batch, ng, GDN_HEADS, hd, hd), F32),
                   jax.ShapeDtypeStruct((n_batch, ng, grp * GDN_HEADS, c, c), F32)],
        scratch_shapes=[pltpu.VMEM((GDN_HEADS, hd, hd), F32)],
        compiler_params=_cparams(("parallel", "arbitrary")),
    )(qkvn, z, smc, smr, a_c, dt_c, a_r, dt_r, go)


def _gdn_bwd(qkvn, z, smc, smr, a_c, dt_c, a_r, dt_r, go, states, inverses, dog):
    n_batch, s_len, _ = qkvn.shape
    c = GDN_CHUNK
    n = s_len // c
    grp = _gdn_group_size(n)
    ng = n // grp
    gc = grp * c
    hd = GDN_HEAD_DIM

    def body(qkv_ref, z_ref, smc_ref, smr_ref, ac_ref, dc_ref, ar_ref, dr_ref, go_ref, st_ref, inv_ref, dog_ref,
             dqkv_ref, dz_ref, dsmc_ref, dsmr_ref, dac_ref, ddc_ref, dar_ref, ddr_ref, dgo_ref, ds_ref):
        first = (pl.program_id(0) == 0) & (pl.program_id(1) == 0)

        @pl.when(pl.program_id(1) == 0)
        def _():
            ds_ref[...] = jnp.zeros_like(ds_ref)

        @pl.when(first)
        def _():
            for r in (dac_ref, ddc_ref, dar_ref, ddr_ref, dgo_ref):
                r[...] = jnp.zeros_like(r)

        rows = lambda k: slice(k * c, (k + 1) * c)
        states = [st_ref[0, 0, h] for h in range(GDN_HEADS)]
        prim = ([qkv_ref[0, rows(k), :] for k in range(grp)], [z_ref[0, rows(k), :] for k in range(grp)],
                [smc_ref[0, rows(k), :] for k in range(grp)], [smr_ref[k] for k in range(grp)],
                ac_ref[...], dc_ref[...], ar_ref[...], dr_ref[...], go_ref[...], states)
        invs = [inv_ref[0, 0, p] for p in range(grp * GDN_HEADS)]
        _, vjp = jax.vjp(functools.partial(_gdn_group, inverses=invs), *prim)
        cot = ([[dog_ref[0, rows(k), h * hd:(h + 1) * hd] for h in range(GDN_HEADS)] for k in range(grp)],
               [ds_ref[h] for h in range(GDN_HEADS)], [jnp.zeros((c, c), F32)] * (grp * GDN_HEADS))
        dqkv, dz, dsmc, dsmr, dac, ddc, dar, ddr, dgo, dstates = vjp(cot)
        for k in range(grp):
            dqkv_ref[0, rows(k), :] = dqkv[k]
            dz_ref[0, rows(k), :] = dz[k].astype(BF16)
            dsmc_ref[0, rows(k), :] = dsmc[k]
            dsmr_ref[k] = dsmr[k]
        dac_ref[...] += dac
        ddc_ref[...] += ddc
        dar_ref[...] += dar
        ddr_ref[...] += ddr
        dgo_ref[...] += dgo
        for h in range(GDN_HEADS):
            ds_ref[h] = dstates[h]

    tok = lambda b, i: (b, ng - 1 - i, 0)
    fixed = lambda b, i: (0, 0)
    lane_vec = jax.ShapeDtypeStruct((1, LANES), F32)
    row_vec = jax.ShapeDtypeStruct((SM_ROWS, 1), F32)
    return pl.pallas_call(
        body, name="gdn_bwd", grid=(n_batch, ng),
        in_specs=[pl.BlockSpec((1, gc, 3 * GDN_WIDTH), tok), pl.BlockSpec((1, gc, GDN_WIDTH), tok), pl.BlockSpec((1, gc, LANES), tok),
                  pl.BlockSpec((grp, SM_ROWS, c), lambda b, i: (b * ng + ng - 1 - i, 0, 0)),
                  pl.BlockSpec((1, LANES), fixed), pl.BlockSpec((1, LANES), fixed), pl.BlockSpec((SM_ROWS, 1), fixed),
                  pl.BlockSpec((SM_ROWS, 1), fixed), pl.BlockSpec((1, LANES), fixed),
                  pl.BlockSpec((1, 1, GDN_HEADS, hd, hd), lambda b, i: (b, ng - 1 - i, 0, 0, 0)),
                  pl.BlockSpec((1, 1, grp * GDN_HEADS, c, c), lambda b, i: (b, ng - 1 - i, 0, 0, 0)),
                  pl.BlockSpec((1, gc, GDN_WIDTH), lambda b, i: (b, ng - 1 - i, 1))],
        out_specs=[pl.BlockSpec((1, gc, 3 * GDN_WIDTH), tok), pl.BlockSpec((1, gc, GDN_WIDTH), tok), pl.BlockSpec((1, gc, LANES), tok),
                   pl.BlockSpec((grp, SM_ROWS, c), lambda b, i: (b * ng + ng - 1 - i, 0, 0)),
                   pl.BlockSpec((1, LANES), fixed), pl.BlockSpec((1, LANES), fixed), pl.BlockSpec((SM_ROWS, 1), fixed),
                   pl.BlockSpec((SM_ROWS, 1), fixed), pl.BlockSpec((1, LANES), fixed)],
        out_shape=[jax.ShapeDtypeStruct((n_batch, s_len, 3 * GDN_WIDTH), F32), jax.ShapeDtypeStruct((n_batch, s_len, GDN_WIDTH), BF16),
                   jax.ShapeDtypeStruct((n_batch, s_len, LANES), F32), jax.ShapeDtypeStruct((n_batch * n, SM_ROWS, c), F32),
                   lane_vec, lane_vec, row_vec, row_vec, lane_vec],
        scratch_shapes=[pltpu.VMEM((GDN_HEADS, hd, hd), F32)],
        compiler_params=_cparams(("arbitrary", "arbitrary")),
    )(qkvn, z, smc, smr, a_c, dt_c, a_r, dt_r, go, states, inverses, dog)


def _out_proj(x, oa, ob, w_out, g_x, w_cq, tm=512):
    t_len, d = x.shape
    tm = min(tm, t_len)

    def body(x_ref, oa_ref, ob_ref, wo_ref, g_ref, wq_ref, x1_ref, hq_ref, cq_ref):
        x1 = x_ref[...] + _dot(jnp.concatenate([oa_ref[...], ob_ref[...]], axis=1), wo_ref[...])
        x1_ref[...] = x1
        hq = _rms(x1, g_ref[...]).astype(BF16)
        hq_ref[...] = hq
        cq_ref[...] = _dot(hq, wq_ref[...])

    row = lambda i: (i, 0)
    fixed = lambda i: (0, 0)
    return pl.pallas_call(
        body, name="out_proj", grid=(t_len // tm,),
        in_specs=[pl.BlockSpec((tm, d), row), pl.BlockSpec((tm, FOX_WIDTH), row), pl.BlockSpec((tm, GDN_WIDTH), row),
                  _resident((d, d)), pl.BlockSpec((1, d), fixed), _resident((d, XATTN_WIDTH))],
        out_specs=[pl.BlockSpec((tm, d), row), pl.BlockSpec((tm, d), row), pl.BlockSpec((tm, XATTN_WIDTH), row)],
        out_shape=[jax.ShapeDtypeStruct((t_len, d), F32), jax.ShapeDtypeStruct((t_len, d), BF16), jax.ShapeDtypeStruct((t_len, XATTN_WIDTH), F32)],
        compiler_params=_cparams(("parallel",)),
    )(x, oa, ob, w_out, g_x, w_cq)


def _out_proj_bwd(dx1, w_out, tm=512):
    t_len, d = dx1.shape
    tm = min(tm, t_len)

    def body(dx_ref, w_ref, o_ref):
        o_ref[...] = _dot(dx_ref[...], w_ref[...], NT)

    return pl.pallas_call(
        body, name="out_proj_bwd", grid=(t_len // tm,),
        in_specs=[pl.BlockSpec((tm, d), lambda i: (i, 0)), pl.BlockSpec((d, d), lambda i: (0, 0))],
        out_specs=pl.BlockSpec((tm, d), lambda i: (i, 0)),
        out_shape=jax.ShapeDtypeStruct((t_len, d), F32),
        compiler_params=_cparams(("parallel",)),
    )(dx1, w_out)


def _mem_kv(mem, g, w_ckv, tm=256):
    t_len, d = mem.shape
    tm = min(tm, t_len)

    def body(x_ref, g_ref, w_ref, h_ref, o_ref):
        h = _rms(x_ref[...], g_ref[...]).astype(BF16)
        h_ref[...] = h
        o_ref[...] = _dot(h, w_ref[...])

    row = lambda i: (i, 0)
    fixed = lambda i: (0, 0)
    return pl.pallas_call(
        body, name="mem_kv", grid=(t_len // tm,),
        in_specs=[pl.BlockSpec((tm, d), row), pl.BlockSpec((1, d), fixed), pl.BlockSpec((d, 2 * XATTN_WIDTH), fixed)],
        out_specs=[pl.BlockSpec((tm, d), row), pl.BlockSpec((tm, 2 * XATTN_WIDTH), row)],
        out_shape=[jax.ShapeDtypeStruct((t_len, d), BF16), jax.ShapeDtypeStruct((t_len, 2 * XATTN_WIDTH), F32)],
        compiler_params=_cparams(("parallel",)),
    )(mem, g, w_ckv)


def _mem_kv_bwd(dckv, mem, g, w_ckv, tm=256):
    t_len, d = mem.shape
    tm = min(tm, t_len)

    def body(d_ref, x_ref, g_ref, w_ref, dg_ref):
        @pl.when(pl.program_id(0) == 0)
        def _():
            dg_ref[...] = jnp.zeros_like(dg_ref)

        dh = _dot(d_ref[...], w_ref[...], NT)
        _, dg = _rms_bwd(x_ref[...], g_ref[...], dh)
        dg_ref[...] += dg

    row = lambda i: (i, 0)
    fixed = lambda i: (0, 0)
    return pl.pallas_call(
        body, name="mem_kv_bwd", grid=(t_len // tm,),
        in_specs=[pl.BlockSpec((tm, 2 * XATTN_WIDTH), row), pl.BlockSpec((tm, d), row), pl.BlockSpec((1, d), fixed),
                  pl.BlockSpec((d, 2 * XATTN_WIDTH), fixed)],
        out_specs=pl.BlockSpec((1, d), fixed),
        out_shape=jax.ShapeDtypeStruct((1, d), F32),
        compiler_params=_cparams(("arbitrary",)),
    )(dckv, mem, g, w_ckv)


def _xattn_probs(qn, kn):
    s = _dot(qn, kn, NT) * (XATTN_HEAD_DIM ** -0.5)
    p = jnp.exp(s - jnp.max(s, axis=-1, keepdims=True))
    return p / jnp.sum(p, axis=-1, keepdims=True)


def _xattn_fwd(cq, ckv, x1, gq, gk, w_co, g_mlp, n_batch, s_len, m_len, tq=512):
    d = x1.shape[1]
    tq = min(tq, s_len)
    nq = s_len // tq
    hd = XATTN_HEAD_DIM

    def body(cq_ref, kv_ref, x1_ref, gq_ref, gk_ref, wo_ref, gm_ref, co_ref, x2_ref, hf_ref):
        outs = []
        for h in range(XATTN_HEADS):
            qn = _rms(cq_ref[:, h * hd:(h + 1) * hd], gq_ref[...])
            kn = _rms(kv_ref[:, h * hd:(h + 1) * hd], gk_ref[...])
            p = _xattn_probs(qn, kn)
            outs.append(_dot(p, kv_ref[:, XATTN_WIDTH + h * hd:XATTN_WIDTH + (h + 1) * hd]).astype(BF16))
        for h in range(XATTN_HEADS):
            co_ref[:, h * hd:(h + 1) * hd] = outs[h]
        x2 = x1_ref[...] + _dot(co_ref[...], wo_ref[...])
        x2_ref[...] = x2
        hf_ref[...] = _rms(x2, gm_ref[...]).astype(BF16)

    row = lambda b, i: (b * nq + i, 0)
    fixed = lambda b, i: (0, 0)
    t_len = n_batch * s_len
    return pl.pallas_call(
        body, name="xattn_fwd", grid=(n_batch, nq),
        in_specs=[pl.BlockSpec((tq, XATTN_WIDTH), row), pl.BlockSpec((m_len, 2 * XATTN_WIDTH), lambda b, i: (b, 0)),
                  pl.BlockSpec((tq, d), row), pl.BlockSpec((1, hd), fixed), pl.BlockSpec((1, hd), fixed),
                  pl.BlockSpec((XATTN_WIDTH, d), fixed), pl.BlockSpec((1, d), fixed)],
        out_specs=[pl.BlockSpec((tq, XATTN_WIDTH), row), pl.BlockSpec((tq, d), row), pl.BlockSpec((tq, d), row)],
        out_shape=[jax.ShapeDtypeStruct((t_len, XATTN_WIDTH), BF16), jax.ShapeDtypeStruct((t_len, d), F32),
                   jax.ShapeDtypeStruct((t_len, d), BF16)],
        compiler_params=_cparams(("parallel", "parallel")),
    )(cq, ckv, x1, gq, gk, w_co, g_mlp)


def _xattn_bwd(dx2, cq, ckv, x1, gq, gk, w_co, g_x, w_cq, n_batch, s_len, m_len, tq=512):
    d = x1.shape[1]
    tq = min(tq, s_len)
    nq = s_len // tq
    hd = XATTN_HEAD_DIM
    scale = XATTN_HEAD_DIM ** -0.5

    def body(dx2_ref, cq_ref, kv_ref, x1_ref, gq_ref, gk_ref, wo_ref, gx_ref, wq_ref,
             dx1_ref, dcq_ref, dkv_ref, dgq_ref, dgk_ref, dgx_ref, dk_acc, dv_acc):
        b = pl.program_id(0)
        i = pl.program_id(1)

        @pl.when((b == 0) & (i == 0))
        def _():
            dgq_ref[...] = jnp.zeros_like(dgq_ref)
            dgk_ref[...] = jnp.zeros_like(dgk_ref)
            dgx_ref[...] = jnp.zeros_like(dgx_ref)

        @pl.when(i == 0)
        def _():
            dk_acc[...] = jnp.zeros_like(dk_acc)
            dv_acc[...] = jnp.zeros_like(dv_acc)

        dx2 = dx2_ref[...]
        dco_all = _dot(dx2, wo_ref[...], NT)
        for h in range(XATTN_HEADS):
            sl = slice(h * hd, (h + 1) * hd)
            q = cq_ref[:, sl]
            qn = _rms(q, gq_ref[...])
            kn = _rms(kv_ref[:, sl], gk_ref[...])
            v = kv_ref[:, XATTN_WIDTH + h * hd:XATTN_WIDTH + (h + 1) * hd]
            p = _xattn_probs(qn, kn)
            dco = dco_all[:, sl]
            dv_acc[:, sl] += _dot(p, dco, TN)
            dp = _dot(dco, v, NT)
            ds = p * (dp - jnp.sum(dp * p, axis=-1, keepdims=True))
            dqn = _dot(ds, kn) * scale
            dk_acc[:, sl] += _dot(ds, qn, TN) * scale
            dq, dgq = _rms_bwd(q, gq_ref[...], dqn)
            dgq_ref[...] += dgq
            dcq_ref[:, sl] = dq.astype(BF16)
        dhq = _dot(dcq_ref[...], wq_ref[...], NT)
        dxn, dgx = _rms_bwd(x1_ref[...], gx_ref[...], dhq)
        dgx_ref[...] += dgx
        dx1_ref[...] = dx2 + dxn

        @pl.when(i == nq - 1)
        def _():
            for h in range(XATTN_HEADS):
                sl = slice(h * hd, (h + 1) * hd)
                dk, dgk = _rms_bwd(kv_ref[:, sl], gk_ref[...], dk_acc[:, sl])
                dgk_ref[...] += dgk
                dkv_ref[:, sl] = dk.astype(BF16)
                dkv_ref[:, XATTN_WIDTH + h * hd:XATTN_WIDTH + (h + 1) * hd] = dv_acc[:, sl].astype(BF16)

    row = lambda b, i: (b * nq + i, 0)
    fixed = lambda b, i: (0, 0)
    t_len = n_batch * s_len
    return pl.pallas_call(
        body, name="xattn_bwd", grid=(n_batch, nq),
        in_specs=[pl.BlockSpec((tq, d), row), pl.BlockSpec((tq, XATTN_WIDTH), row), pl.BlockSpec((m_len, 2 * XATTN_WIDTH), lambda b, i: (b, 0)),
                  pl.BlockSpec((tq, d), row), pl.BlockSpec((1, hd), fixed), pl.BlockSpec((1, hd), fixed),
                  pl.BlockSpec((XATTN_WIDTH, d), fixed), pl.BlockSpec((1, d), fixed), pl.BlockSpec((d, XATTN_WIDTH), fixed)],
        out_specs=[pl.BlockSpec((tq, d), row), pl.BlockSpec((tq, XATTN_WIDTH), row), pl.BlockSpec((m_len, 2 * XATTN_WIDTH), lambda b, i: (b, 0)),
                   pl.BlockSpec((1, hd), fixed), pl.BlockSpec((1, hd), fixed), pl.BlockSpec((1, d), fixed)],
        out_shape=[jax.ShapeDtypeStruct((t_len, d), F32), jax.ShapeDtypeStruct((t_len, XATTN_WIDTH), BF16),
                   jax.ShapeDtypeStruct((n_batch * m_len, 2 * XATTN_WIDTH), BF16),
                   jax.ShapeDtypeStruct((1, hd), F32), jax.ShapeDtypeStruct((1, hd), F32), jax.ShapeDtypeStruct((1, d), F32)],
        scratch_shapes=[pltpu.VMEM((m_len, XATTN_WIDTH), F32), pltpu.VMEM((m_len, XATTN_WIDTH), F32)],
        compiler_params=_cparams(("arbitrary", "arbitrary")),
    )(dx2, cq, ckv, x1, gq, gk, w_co, g_x, w_cq)


def _resident(shape):
    return pl.BlockSpec(shape, lambda *_: (0,) * len(shape), pipeline_mode=pl.Buffered(1))


def _mlp_fwd(hf, x2, target, w1, w2, tm=256, tf=1024):
    t_len, d = x2.shape
    f = w1.shape[1]
    tm, tf = min(tm, t_len), min(tf, f)

    def body(hf_ref, x2_ref, tg_ref, w1_ref, w2_ref, u_ref, a_ref, dy_ref, ls_ref):
        hf_t = hf_ref[...]
        for k in range(f // tf):
            cols = slice(k * tf, (k + 1) * tf)
            u = _dot(hf_t, w1_ref[:, cols])
            u_ref[:, cols] = u
            r = jnp.maximum(u, 0.0)
            a_ref[:, cols] = (r * r).astype(BF16)
        y = x2_ref[...] + _dot(a_ref[...], w2_ref[...])
        err = y - tg_ref[...]
        dy_ref[...] = err * (1.0 / d)
        ls_ref[...] = jnp.broadcast_to(jnp.sum(jnp.sum(err * err, axis=-1, keepdims=True) * (1.0 / d), axis=0, keepdims=True), ls_ref.shape)

    row = lambda i: (i, 0)
    return pl.pallas_call(
        body, name="mlp_fwd", grid=(t_len // tm,),
        in_specs=[pl.BlockSpec((tm, d), row), pl.BlockSpec((tm, d), row), pl.BlockSpec((tm, d), row), _resident((d, f)), _resident((f, d))],
        out_specs=[pl.BlockSpec((tm, f), row), pl.BlockSpec((tm, f), row), pl.BlockSpec((tm, d), row),
                   pl.BlockSpec((1, 8, LANES), lambda i: (i, 0, 0))],
        out_shape=[jax.ShapeDtypeStruct((t_len, f), F32), jax.ShapeDtypeStruct((t_len, f), BF16), jax.ShapeDtypeStruct((t_len, d), F32),
                   jax.ShapeDtypeStruct((t_len // tm, 8, LANES), F32)],
        compiler_params=_cparams(("parallel",)),
    )(hf, x2, target, w1, w2)


def _mlp_bwd(dy, u, x2, g, w1, w2, tm=256, tf=1024):
    t_len, d = x2.shape
    f = w1.shape[1]
    tm, tf = min(tm, t_len), min(tf, f)

    def body(dy_ref, u_ref, x2_ref, g_ref, w1_ref, w2_ref, du_ref, dx2_ref, dg_ref):
        @pl.when(pl.program_id(0) == 0)
        def _():
            dg_ref[...] = jnp.zeros_like(dg_ref)

        dy_t = dy_ref[...]
        dyb = dy_t.astype(BF16)
        for k in range(f // tf):
            cols = slice(k * tf, (k + 1) * tf)
            da = _dot(dyb, w2_ref[cols, :], NT)
            du_ref[:, cols] = (da * (2.0 * jnp.maximum(u_ref[:, cols], 0.0))).astype(BF16)
        dhf = _dot(du_ref[...], w1_ref[...], NT)
        dxn, dg = _rms_bwd(x2_ref[...], g_ref[...], dhf)
        dx2_ref[...] = dy_t + dxn
        dg_ref[...] += dg

    row = lambda i: (i, 0)
    fixed = lambda i: (0, 0)
    return pl.pallas_call(
        body, name="mlp_bwd", grid=(t_len // tm,),
        in_specs=[pl.BlockSpec((tm, d), row), pl.BlockSpec((tm, f), row), pl.BlockSpec((tm, d), row), pl.BlockSpec((1, d), fixed),
                  _resident((d, f)), _resident((f, d))],
        out_specs=[pl.BlockSpec((tm, f), row), pl.BlockSpec((tm, d), row), pl.BlockSpec((1, d), fixed)],
        out_shape=[jax.ShapeDtypeStruct((t_len, f), BF16), jax.ShapeDtypeStruct((t_len, d), F32), jax.ShapeDtypeStruct((1, d), F32)],
        compiler_params=_cparams(("arbitrary",)),
    )(dy, u, x2, g, w1, w2)


def _pad_lanes(v, offset=0, width=LANES):
    return jnp.zeros((1, width), F32).at[:, offset:offset + v.shape[1]].set(v)


def _col(v, offset=0, rows=SM_ROWS):
    return jnp.zeros((rows, 1), F32).at[offset:offset + v.shape[1], 0].set(v[0])


def _pack_small(g_mix, dgq, dgk, dbias, dgo, dac, dar, ddc, ddr, g_gdn_o, g_nx, g_mem, g_xq, g_xk, g_mlp, loss_tiles):
    def body(mix_ref, q_ref, k_ref, b_ref, o_ref, ac_ref, ar_ref, dc_ref, dr_ref, go_ref, nx_ref, mem_ref, xq_ref, xk_ref,
             mlp_ref, lt_ref, out_ref):
        lane = lax.broadcasted_iota(jnp.int32, (1, LANES), 1)
        diag = lax.broadcasted_iota(jnp.int32, (SM_ROWS, LANES), 0) == lax.broadcasted_iota(jnp.int32, (SM_ROWS, LANES), 1)

        def rolled(v, shift):
            return pltpu.roll(jnp.broadcast_to(v, (8, LANES)), shift, 1)[0:1, :]

        def rows_to_lanes(col):
            return jnp.sum(jnp.where(diag, col, 0.0), axis=0, keepdims=True)

        def put(row, v, n):
            out_ref[row:row + 1, 0:LANES] = jnp.where(lane < n, v, 0.0)

        out_ref[...] = jnp.zeros_like(out_ref)
        out_ref[0:1, :] = mix_ref[...]
        for row, ref in ((1, q_ref), (2, k_ref), (4, o_ref)):
            put(row, ref[...] + rolled(ref[...], FOX_HEAD_DIM), FOX_HEAD_DIM)
        put(3, rows_to_lanes(b_ref[...]), FOX_HEADS)
        for row, lane_ref, row_ref in ((5, ac_ref, ar_ref), (6, dc_ref, dr_ref)):
            put(row, rolled(lane_ref[...] + rows_to_lanes(row_ref[...]), LANES - SM_A), GDN_HEADS)
        put(7, go_ref[...], LANES)
        out_ref[8:9, :] = nx_ref[...]
        out_ref[9:10, :] = mem_ref[...]
        put(10, xq_ref[...], LANES)
        put(11, xk_ref[...], LANES)
        out_ref[12:13, :] = mlp_ref[...]
        put(LOSS_ROW, 0.5 * jnp.sum(lt_ref[...], axis=0)[0:1, :], 1)

    args = (g_mix, dgq, dgk, dbias, dgo, dac, dar, ddc, ddr, g_gdn_o, g_nx, g_mem, g_xq, g_xk, g_mlp, loss_tiles)
    return pl.pallas_call(body, name="pack_small", out_shape=jax.ShapeDtypeStruct((PACK_ROWS, D_MODEL), F32))(*args)


LATE_WEIGHTS = (("w_out", "w_cq", "w_ckv", "w_co"), ("w_mlp1", "w_mlp2"))
GRAD_GROUPS = (("w_mlp2", "w_mlp1"), ("w_co", "w_cq", "w_ckv", "w_out"), ("w_in", "gdn_conv_w"))


def _local_step(x, mem, target, norm_mix_g, w_in, fox_qnorm_g, fox_knorm_g, fox_f_bias, fox_onorm_g, gdn_conv_w, gdn_A_log,
                gdn_dt_bias, gdn_onorm_g, norm_xattn_g, mem_norm_g, xattn_qnorm_g, xattn_knorm_g, norm_mlp_g,
                late_weights, grads_ready=None, first_token=0.0):
    if grads_ready is None:
        grads_ready = lambda group: 0.0
    n_batch, s_len, d = x.shape
    m_len = mem.shape[1]
    t_len = n_batch * s_len
    tq = min(FOX_BLOCK, s_len)
    nq = s_len // tq
    n_chunks = s_len // GDN_CHUNK
    x2d = x.reshape(t_len, d)

    wp = jnp.concatenate([w_in[0:1536], w_in[1544:3080], w_in[3088:3600], w_in[1536:1544], w_in[3080:3088],
                          jnp.zeros((P_DIM - 3600, d), BF16)], axis=0)
    wst = jnp.concatenate([w_in[1536:1544], w_in[3080:3088]], axis=0)
    conv_w = jnp.concatenate([gdn_conv_w, jnp.zeros((8 - CONV_WIDTH, gdn_conv_w.shape[1]), F32)], axis=0)
    bias_col = _col(fox_f_bias, SM_F)
    gq2, gk2, go2 = (jnp.tile(g, (1, 2)) for g in (fox_qnorm_g, fox_knorm_g, fox_onorm_g))
    a_c, dt_c = _pad_lanes(gdn_A_log, SM_A), _pad_lanes(gdn_dt_bias, SM_A)
    a_r, dt_r = _col(gdn_A_log, SM_A), _col(gdn_dt_bias, SM_A)

    h1, pfox, pgdn, pz, sm, smt = _in_proj(x2d, norm_mix_g + first_token, wp, wst)
    c_rows = _fox_cum(smt, bias_col, n_batch, s_len)
    cb = c_rows.reshape(SM_ROWS, n_batch, nq, tq).transpose(1, 2, 0, 3)
    pf3 = pfox.reshape(n_batch, s_len, 1536)
    o_fox, oa, lse = _fox_fwd(pf3, cb, gq2, gk2, go2, tq)
    pg3 = pgdn.reshape(n_batch, s_len, 1536)
    qkvn = _gdn_pre(pg3, conv_w)
    z3 = pz.reshape(n_batch, s_len, GDN_WIDTH)
    smc = sm.reshape(n_batch, s_len, LANES)
    smr = smt.reshape(SM_ROWS, n_batch * n_chunks, GDN_CHUNK).transpose(1, 0, 2)
    ob, states, inverses = _gdn_fwd(qkvn, z3, smc, smr, a_c, dt_c, a_r, dt_r, gdn_onorm_g)
    oa2, ob2 = oa.reshape(t_len, FOX_WIDTH), ob.reshape(t_len, GDN_WIDTH)
    w_out, w_cq, w_ckv, w_co = late_weights(LATE_WEIGHTS[0], ob2)
    x1, hq, cq = _out_proj(x2d, oa2, ob2, w_out, norm_xattn_g, w_cq)
    mem2d = mem.reshape(n_batch * m_len, d)
    hm, ckv = _mem_kv(mem2d, mem_norm_g, w_ckv)
    co, x2, hf = _xattn_fwd(cq, ckv, x1, xattn_qnorm_g, xattn_knorm_g, w_co, norm_mlp_g, n_batch, s_len, m_len)
    w_mlp1, w_mlp2 = late_weights(LATE_WEIGHTS[1], hf)
    u, a_act, dy, loss_tiles = _mlp_fwd(hf, x2, target.reshape(t_len, d), w_mlp1, w_mlp2)

    grads = {}
    du, dx2, grads["norm_mlp_g"] = _mlp_bwd(dy, u, x2, norm_mlp_g, w_mlp1, w_mlp2)
    grads["w_mlp2"] = _wgrad(a_act, dy, "wgrad_mlp2", bt=2048)
    grads["w_mlp1"] = _wgrad(hf, du, "wgrad_mlp1", bt=2048, column_blocks=D_FF // N_DEV)
    token = grads_ready({k: grads[k] for k in GRAD_GROUPS[0]})
    grads["w_co"] = _wgrad(co, dx2, "wgrad_co", column_blocks=D_MODEL // N_DEV)
    dx1, dcq, dckv, grads["xattn_qnorm_g"], grads["xattn_knorm_g"], grads["norm_xattn_g"] = _xattn_bwd(
        dx2, cq, ckv, x1, xattn_qnorm_g + token, xattn_knorm_g, w_co, norm_xattn_g, w_cq, n_batch, s_len, m_len)
    grads["w_cq"] = _wgrad(hq, dcq, "wgrad_cq")
    grads["w_ckv"] = _wgrad(hm, dckv, "wgrad_ckv")
    grads["mem_norm_g"] = _mem_kv_bwd(dckv, mem2d, mem_norm_g, w_ckv)
    grads["w_out"] = _wgrad_stacked([oa2, ob2], dx1, "wgrad_out", bn=1024)
    token = grads_ready({k: grads[k] for k in GRAD_GROUPS[1]})
    dcat = _out_proj_bwd(dx1, w_out)
    dcat3 = dcat.reshape(n_batch, s_len, d)

    dqkvn, dz, dsmc, dsmr, dac, ddc, dar, ddr, grads["gdn_onorm_g"] = _gdn_bwd(
        qkvn, z3, smc, smr, a_c, dt_c, a_r, dt_r, gdn_onorm_g + token, states, inverses, dcat3)
    dpg, dconv = _gdn_pre_bwd(pg3, conv_w, dqkvn)
    grads["gdn_conv_w"] = dconv[0:CONV_WIDTH]

    dq, dk, dv, dcb, dgq, dgk, dgo = _fox_bwd(pf3, cb, gq2, gk2, go2, o_fox, lse, dcat3, tq)
    dc8 = dcb[:, :, :, 0:2, :].transpose(1, 3, 0, 2, 4).reshape(FOX_HEADS, t_len)
    dc_rows = jnp.concatenate([dc8, jnp.zeros((SM_ROWS - FOX_HEADS, t_len), F32)], axis=0)
    dl_rows, dbias = _fox_cum_bwd(dc_rows, smt, bias_col, n_batch, s_len)
    dsm_rows = jnp.concatenate([dl_rows[0:SM_B], dsmr.transpose(1, 0, 2).reshape(SM_ROWS, t_len)[SM_B:SM_ROWS]], axis=0)

    dprojs = [dq.reshape(t_len, FOX_WIDTH), dk.reshape(t_len, FOX_WIDTH), dv.reshape(t_len, FOX_WIDTH),
              dpg.reshape(t_len, 1536), dz.reshape(t_len, GDN_WIDTH), dsmc.reshape(t_len, LANES)]
    dwp = _wgrad_stacked(dprojs, h1, "wgrad_in")
    dwst = _rows_matmul(dsm_rows, h1, "wgrad_in_rows")
    dw_small = dwp[P_SMALL:P_SMALL + SM_ROWS] + dwst
    grads["w_in"] = jnp.concatenate([dwp[0:1536], dw_small[0:8], dwp[1536:3072], dw_small[8:16], dwp[3072:3584]], axis=0)
    token = grads_ready({k: grads[k] for k in GRAD_GROUPS[2]})
    grad_x, grads["norm_mix_g"] = _in_proj_bwd(dprojs, dsm_rows, x2d, norm_mix_g + token, wp, wst, dx1)
    packed = _pack_small(grads["norm_mix_g"], dgq, dgk, dbias, dgo, dac, dar, ddc, ddr, grads["gdn_onorm_g"], grads["norm_xattn_g"],
                         grads["mem_norm_g"], grads["xattn_qnorm_g"], grads["xattn_knorm_g"], grads["norm_mlp_g"], loss_tiles)
    return packed, grad_x.reshape(n_batch, s_len, d), {k: grads[k] for k in SHARDED}


MESH_ID = pl.DeviceIdType.MESH
ANY_SPEC = pl.BlockSpec(memory_space=pl.ANY)


def _place():
    x, y, c = lax.axis_index("x"), lax.axis_index("y"), lax.axis_index("c")
    return x, y, c, [(1 - x, y), (x, 1 - y), (1 - x, 1 - y)]


def _place_own(src_ref, dst_ref):
    def staged(buf, sem):
        for a, b in ((src_ref, buf), (buf, dst_ref)):
            cp = pltpu.make_async_copy(a, b, sem)
            cp.start()
            cp.wait()

    pl.run_scoped(staged, pltpu.VMEM(src_ref.shape, src_ref.dtype), pltpu.SemaphoreType.DMA)


def _all_gather_body(n, ins, outs, send_sems, recv_sems):
    x, y, c, chips = _place()
    me, sibling = (x, y, c), (x, y, 1 - c)

    def copy(a, k, block, to, src=None):
        dst = outs[a].at[4 * block[0] + 2 * block[1] + block[2]]
        return pltpu.make_async_remote_copy(src_ref=dst if src is None else src, dst_ref=dst, send_sem=send_sems.at[a, k],
                                            recv_sem=recv_sems.at[a, k], device_id=to, device_id_type=MESH_ID)

    first = []
    for a in range(n):
        first.append(copy(a, 0, me, sibling, src=ins[a]))
        first += [copy(a, 1 + j, me, (*chip, c), src=ins[a]) for j, chip in enumerate(chips)]
    for cp in first:
        cp.start()
    for a in range(n):
        _place_own(ins[a], outs[a].at[4 * x + 2 * y + c])
    passed = []
    for j, chip in enumerate(chips):
        for a in range(n):
            copy(a, 1 + j, (*chip, c), me).wait_recv()
            fwd = copy(a, 4 + j, (*chip, c), sibling)
            fwd.start()
            passed.append(fwd)
    for a in range(n):
        copy(a, 0, sibling, me).wait_recv()
        for j, chip in enumerate(chips):
            copy(a, 4 + j, (*chip, 1 - c), me).wait_recv()
    for cp in first + passed:
        cp.wait_send()


def _all_gather_hbm(arrs, name):
    n = len(arrs)

    def body(*refs):
        _all_gather_body(n, refs[:n], refs[n:2 * n], refs[2 * n], refs[2 * n + 1])

    return pl.pallas_call(
        body, name=name, in_specs=[ANY_SPEC] * n, out_specs=[ANY_SPEC] * n,
        out_shape=[jax.ShapeDtypeStruct((N_DEV,) + a.shape, a.dtype) for a in arrs],
        scratch_shapes=[pltpu.SemaphoreType.DMA((n, 7)), pltpu.SemaphoreType.DMA((n, 7))],
        compiler_params=pltpu.CompilerParams(vmem_limit_bytes=VMEM_LIMIT),
    )(*arrs)


def _pair_exchange(arrs, name):
    n = len(arrs)

    def body(*refs):
        ins, outs = refs[:n], refs[n:2 * n]
        send_sems, recv_sems = refs[2 * n:]
        x, y, c, _ = _place()
        copies = []
        for a in range(n):
            for chip in range(4):
                copies.append(pltpu.make_async_remote_copy(
                    src_ref=ins[a].at[2 * chip + (1 - c)], dst_ref=outs[a].at[chip], send_sem=send_sems.at[a, chip],
                    recv_sem=recv_sems.at[a, chip], device_id=(x, y, 1 - c), device_id_type=MESH_ID))
        for cp in copies:
            cp.start()
        for cp in copies:
            cp.wait()

    return pl.pallas_call(
        body, name=name, in_specs=[ANY_SPEC] * n, out_specs=[ANY_SPEC] * n,
        out_shape=[jax.ShapeDtypeStruct((4,) + a.shape[1:], a.dtype) for a in arrs],
        scratch_shapes=[pltpu.SemaphoreType.DMA((n, 4)), pltpu.SemaphoreType.DMA((n, 4))],
    )(*arrs)


HBM_SPEC = pl.BlockSpec(memory_space=pltpu.HBM)
SEM_SPEC = pl.BlockSpec(memory_space=pltpu.SEMAPHORE)
DATAFLOW = pltpu.SideEffectType.DATAFLOW_SIDE_EFFECTING


def _in_hbm(arrs):
    return [pltpu.with_memory_space_constraint(a, pltpu.HBM) for a in arrs]


def _copies_start(name, srcs, lands, make_copies, after):
    n = len(srcs)
    n_copies = len(make_copies(srcs, lands, None, None)[0])

    def body(*refs):
        send_sems, recv_sems = refs[2 * n + 1], refs[2 * n + 2]
        for row in make_copies(refs[:n], refs[n:2 * n], send_sems, recv_sems):
            for cp in row:
                cp.start()
        refs[-1][...] = jnp.zeros_like(refs[-1])

    sems = pltpu.SemaphoreType.DMA((n * n_copies,))
    thru = [pltpu.HBM(a.shape, a.dtype) for a in list(srcs) + list(lands)]
    res = pl.pallas_call(
        body, name=name, in_specs=[HBM_SPEC] * (2 * n) + [ANY_SPEC],
        out_specs=(SEM_SPEC, SEM_SPEC, *[HBM_SPEC] * (2 * n), pl.BlockSpec(memory_space=pltpu.VMEM)),
        out_shape=(sems, sems, *thru, jax.ShapeDtypeStruct((8, LANES), F32)),
        input_output_aliases={i: 2 + i for i in range(2 * n)},
        compiler_params=pltpu.CompilerParams(has_side_effects=DATAFLOW),
    )(*_in_hbm(list(srcs) + list(lands)), after)
    return res[0], res[1], list(res[2:2 + n]), list(res[2 + n:2 + 2 * n]), res[-1]


def _copies_wait(name, send_sems, recv_sems, srcs, lands, after, make_copies, own_block=False):
    n = len(srcs)

    def body(*refs):
        if own_block:
            for a in range(n):
                _place_own(refs[a], _own_part(refs[a], refs[3 * n + 3 + a]))
        for row in make_copies(refs[:n], refs[n:2 * n], refs[2 * n], refs[2 * n + 1]):
            for cp in row:
                cp.wait_send()
                cp.wait_recv()

    res = pl.pallas_call(
        body, name=name, in_specs=[HBM_SPEC] * (2 * n) + [SEM_SPEC, SEM_SPEC, ANY_SPEC],
        out_specs=tuple([HBM_SPEC] * (2 * n)),
        out_shape=tuple(pltpu.HBM(a.shape, a.dtype) for a in list(srcs) + list(lands)),
        input_output_aliases={i: i for i in range(2 * n)},
        compiler_params=pltpu.CompilerParams(has_side_effects=DATAFLOW, vmem_limit_bytes=VMEM_LIMIT),
    )(*srcs, *lands, send_sems, recv_sems, after)
    return list(res[:n]), list(res[n:])


def _own_part(src_ref, land_ref):
    me = 4 * lax.axis_index("x") + 2 * lax.axis_index("y") + lax.axis_index("c")
    rows, cols = src_ref.shape
    if land_ref.shape[0] == N_DEV * rows:
        return land_ref.at[pl.ds(pl.multiple_of(me * rows, rows), rows), :]
    return land_ref.at[:, pl.ds(pl.multiple_of(me * cols, cols), cols)]


def _gather_copies(srcs, lands, send_sems, recv_sems):
    if send_sems is None:
        return [[None] * 7]
    x, y, c, _ = _place()
    rows = []
    for a in range(len(srcs)):
        row = []
        for k in range(7):
            r = k + 1
            to = (1 - x if r & 4 else x, 1 - y if r & 2 else y, 1 - c if r & 1 else c)
            row.append(pltpu.make_async_remote_copy(
                src_ref=srcs[a], dst_ref=_own_part(srcs[a], lands[a]), send_sem=send_sems.at[7 * a + k], recv_sem=recv_sems.at[7 * a + k],
                device_id=to, device_id_type=MESH_ID))
        rows.append(row)
    return rows


def _scatter_copies(srcs, lands, send_sems, recv_sems):
    if send_sems is None:
        return [[None] * 7]
    x, y, c, _ = _place()
    rows = []
    for a in range(len(srcs)):
        row = []
        for k in range(7):
            r = k + 1
            to = (1 - x if r & 4 else x, 1 - y if r & 2 else y, 1 - c if r & 1 else c)
            row.append(pltpu.make_async_remote_copy(
                src_ref=srcs[a].at[4 * to[0] + 2 * to[1] + to[2]], dst_ref=lands[a].at[k], send_sem=send_sems.at[7 * a + k],
                recv_sem=recv_sems.at[7 * a + k], device_id=to, device_id_type=MESH_ID))
        rows.append(row)
    return rows


def _chip_copies(srcs, lands, send_sems, recv_sems):
    if send_sems is None:
        return [[None] * 3]
    x, y, c, chips = _place()
    return [[pltpu.make_async_remote_copy(
        src_ref=srcs[a].at[2 * chip[0] + chip[1]], dst_ref=lands[a].at[j], send_sem=send_sems.at[3 * a + j], recv_sem=recv_sems.at[3 * a + j],
        device_id=(*chip, c), device_id_type=MESH_ID) for j, chip in enumerate(chips)] for a in range(len(srcs))]


def _tile(rows, cols):
    if rows <= 256:
        return rows, cols
    tr = 256 if cols <= 512 else 128
    if rows % tr == 0:
        return tr, cols
    return rows, 512


def _pair_sum(core, own, got, name):
    _, rows, cols = own.shape
    tr, tc = _tile(rows, cols)

    def body(c_ref, own_ref, got_ref, o_ref):
        o_ref[0] = own_ref[0] + got_ref[0]

    return pl.pallas_call(
        body, name=name,
        grid_spec=pltpu.PrefetchScalarGridSpec(
            num_scalar_prefetch=1, grid=(4, rows // tr, cols // tc),
            in_specs=[pl.BlockSpec((1, tr, tc), lambda k, i, j, c: (2 * k + c[0], i, j)),
                      pl.BlockSpec((1, tr, tc), lambda k, i, j, c: (k, i, j))],
            out_specs=pl.BlockSpec((1, tr, tc), lambda k, i, j, c: (k, i, j))),
        out_shape=jax.ShapeDtypeStruct((4, rows, cols), F32),
        compiler_params=_cparams(("parallel", "parallel", "parallel")),
    )(core, own, got)


def _adamw(w, g, m, v):
    m_new = ADAM_B1 * m + (1.0 - ADAM_B1) * g
    v_new = ADAM_B2 * v + (1.0 - ADAM_B2) * (g * g)
    m_hat = m_new / (1.0 - ADAM_B1 ** ADAM_STEP)
    v_hat = v_new / (1.0 - ADAM_B2 ** ADAM_STEP)
    delta = -ADAM_LR * (m_hat / (jnp.sqrt(v_hat) + ADAM_EPS) + ADAM_WD * w)
    return delta, m_new, v_new


def _sum_adam(chip, sums, parts, w, m, v, name):
    n_parts, rows, cols = parts.shape
    tr, tc = _tile(rows, cols)

    def body(chip_ref, own_ref, p_ref, w_ref, m_ref, v_ref, g_ref, d_ref, mo_ref, vo_ref):
        g = own_ref[0]
        for k in range(n_parts):
            g = g + p_ref[k]
        g_ref[...] = g
        d_ref[...], mo_ref[...], vo_ref[...] = _adamw(w_ref[...], g, m_ref[...], v_ref[...])

    tile = pl.BlockSpec((tr, tc), lambda i, j, ch: (i, j))
    out = jax.ShapeDtypeStruct((rows, cols), F32)
    return pl.pallas_call(
        body, name=name,
        grid_spec=pltpu.PrefetchScalarGridSpec(
            num_scalar_prefetch=1, grid=(rows // tr, cols // tc),
            in_specs=[pl.BlockSpec((1, tr, tc), lambda i, j, ch: (ch[0], i, j)),
                      pl.BlockSpec((n_parts, tr, tc), lambda i, j, ch: (0, i, j)), tile, tile, tile],
            out_specs=[tile, tile, tile, tile]),
        out_shape=[out, out, out, out],
        compiler_params=_cparams(("parallel", "parallel")),
    )(chip, sums, parts, w, m, v)


SHARDED = ("w_in", "gdn_conv_w", "w_out", "w_cq", "w_ckv", "w_co", "w_mlp1", "w_mlp2")
TRANSPOSED = ("w_in",)
COLUMN_SHARDED = ("gdn_conv_w", "w_co", "w_mlp1")
REPLICATED = ("norm_mix_g", "fox_qnorm_g", "fox_knorm_g", "fox_f_bias", "fox_onorm_g", "gdn_A_log", "gdn_dt_bias", "gdn_onorm_g",
              "norm_xattn_g", "mem_norm_g", "xattn_qnorm_g", "xattn_knorm_g", "norm_mlp_g")
WEIGHTS = ("norm_mix_g", "w_in", "fox_qnorm_g", "fox_knorm_g", "fox_f_bias", "fox_onorm_g", "gdn_conv_w", "gdn_A_log", "gdn_dt_bias",
           "gdn_onorm_g", "w_out", "norm_xattn_g", "mem_norm_g", "w_cq", "w_ckv", "xattn_qnorm_g", "xattn_knorm_g", "w_co",
           "norm_mlp_g", "w_mlp1", "w_mlp2")
PACK_ROWS = 16
LOSS_ROW = len(REPLICATED)


def _whole(name, gathered):
    if name in COLUMN_SHARDED:
        return gathered.transpose(1, 0, 2).reshape(gathered.shape[1], N_DEV * gathered.shape[2])
    return gathered.reshape(N_DEV * gathered.shape[1], gathered.shape[2])


def _whole_shape(name, shard_shape):
    rows, cols = shard_shape
    return (rows, N_DEV * cols) if name in COLUMN_SHARDED else (N_DEV * rows, cols)


def _blocks(name, whole):
    if whole.ndim == 3:
        return whole
    if name in COLUMN_SHARDED:
        rows, cols = whole.shape
        return whole.reshape(rows, N_DEV, cols // N_DEV).transpose(1, 0, 2)
    return whole.reshape(N_DEV, whole.shape[0] // N_DEV, whole.shape[1])


def _adam_small(everyone, ws, ms, vs):
    n_par = len(ws)

    def body(*refs):
        ev_ref = refs[0]
        w_refs, m_refs, v_refs = (refs[1 + j * n_par:1 + (j + 1) * n_par] for j in range(3))
        outs = refs[1 + 3 * n_par:-1]
        sum_ref = refs[-1]
        total = ev_ref[0]
        for dev in range(1, N_DEV):
            total = total + ev_ref[dev]
        sum_ref[...] = total
        for i in range(n_par):
            n = w_refs[i].shape[1]
            g = sum_ref[i:i + 1, 0:n]
            outs[4 * i][...] = g
            outs[4 * i + 1][...], outs[4 * i + 2][...], outs[4 * i + 3][...] = _adamw(w_refs[i][...], g, m_refs[i][...], v_refs[i][...])
        outs[4 * n_par][...] = sum_ref[LOSS_ROW:LOSS_ROW + 1, 0:1]

    shapes = [jax.ShapeDtypeStruct(a.shape, F32) for a in ws for _ in range(4)] + [jax.ShapeDtypeStruct((1, 1), F32)]
    return pl.pallas_call(body, name="adam_small", out_shape=shapes,
                          scratch_shapes=[pltpu.VMEM((PACK_ROWS, D_MODEL), F32)])(everyone, *ws, *ms, *vs)


def kernel(x, mem, norm_mix_g, w_in, fox_qnorm_g, fox_knorm_g, fox_f_bias, fox_onorm_g, gdn_conv_w, gdn_A_log, gdn_dt_bias, gdn_onorm_g, w_out, norm_xattn_g, mem_norm_g, w_cq, w_ckv, xattn_qnorm_g, xattn_knorm_g, w_co, norm_mlp_g, w_mlp1, w_mlp2, loss_target, m_norm_mix_g, m_w_in, m_fox_qnorm_g, m_fox_knorm_g, m_fox_f_bias, m_fox_onorm_g, m_gdn_conv_w, m_gdn_A_log, m_gdn_dt_bias, m_gdn_onorm_g, m_w_out, m_norm_xattn_g, m_mem_norm_g, m_w_cq, m_w_ckv, m_xattn_qnorm_g, m_xattn_knorm_g, m_w_co, m_norm_mlp_g, m_w_mlp1, m_w_mlp2, v_norm_mix_g, v_w_in, v_fox_qnorm_g, v_fox_knorm_g, v_fox_f_bias, v_fox_onorm_g, v_gdn_conv_w, v_gdn_A_log, v_gdn_dt_bias, v_gdn_onorm_g, v_w_out, v_norm_xattn_g, v_mem_norm_g, v_w_cq, v_w_ckv, v_xattn_qnorm_g, v_xattn_knorm_g, v_w_co, v_norm_mlp_g, v_w_mlp1, v_w_mlp2):
    given = dict(locals())
    w = {k: given[k] for k in WEIGHTS}
    m = {k: given["m_" + k] for k in WEIGHTS}
    v = {k: given["v_" + k] for k in WEIGHTS}

    core = lax.axis_index("c").astype(jnp.int32).reshape(1)
    chip = (2 * lax.axis_index("x") + lax.axis_index("y")).astype(jnp.int32).reshape(1)
    me = 4 * lax.axis_index("x") + 2 * lax.axis_index("y") + lax.axis_index("c")

    local = lambda d: {k: jnp.transpose(d[k][0]) if k in TRANSPOSED else d[k][0] for k in SHARDED}
    w2, m2, v2 = local(w), local(m), local(v)
    shards = {k: w2[k] if k == "gdn_conv_w" else w2[k].astype(BF16) for k in SHARDED}
    early = [k for k in SHARDED if not any(k in group for group in LATE_WEIGHTS)]
    gathered = _all_gather_hbm([shards[k] for k in early], "gather_early")
    whole = {k: _whole(k, g) for k, g in zip(early, gathered)}
    gathers, after = {}, gathered[0]
    for i, group in enumerate(LATE_WEIGHTS):
        lands = [lax.empty(_whole_shape(k, shards[k].shape), BF16) for k in group]
        gathers[group] = _copies_start("gather_late_start_" + str(i), [shards[k] for k in group], lands, _gather_copies, after=after)
        after = gathers[group][4]
    first_token = after[0, 0]

    def late_weights(group, after):
        gather = gathers[group]
        _, lands = _copies_wait("gather_late_wait_" + str(LATE_WEIGHTS.index(group)), gather[0], gather[1], gather[2], gather[3],
                                after, _gather_copies, own_block=True)
        return lands

    pending = []

    def grads_ready(group):
        names = list(group)
        tag = str(len(pending))
        own = [_blocks(k, group[k]) for k in names]
        if "w_in" in names:
            got = _pair_exchange(own, "grad_pair_exchange_" + tag)
            srcs = [_pair_sum(core, o, g, "grad_pair_sum_" + k) for k, o, g in zip(names, own, got)]
            copies, index, n_parts = _chip_copies, chip, 3
        else:
            srcs, copies, index, n_parts = own, _scatter_copies, me.astype(jnp.int32).reshape(1), 7
        lands = [lax.empty((n_parts,) + s.shape[1:], s.dtype) for s in srcs]
        started = _copies_start("grad_exchange_start_" + tag, srcs, lands, copies, after=core)
        pending.append((names, started, copies, index))
        return started[4][0, 0]

    small = {k: w[k] for k in REPLICATED}
    packed, grad_x, _ = _local_step(x, mem, loss_target, **small, **whole, late_weights=late_weights,
                                    grads_ready=grads_ready, first_token=first_token)

    small_lands = [lax.empty((N_DEV * PACK_ROWS, D_MODEL), F32)]
    small_gather = _copies_start("gather_small_start", [packed], small_lands, _gather_copies, after=grad_x)

    out_g, out_d, out_m, out_v = {}, {}, {}, {}
    after = small_gather[4]
    for tag, (names, started, copies, index) in enumerate(pending):
        srcs, parts = _copies_wait("grad_exchange_wait_" + str(tag), started[0], started[1], started[2], started[3], after, copies)
        for k, s, p in zip(names, srcs, parts):
            res = _sum_adam(index, s, p, w2[k], m2[k], v2[k], "adam_" + k)
            out_g[k], out_d[k], out_m[k], out_v[k] = ((jnp.transpose(r) if k in TRANSPOSED else r)[None] for r in res)
            after = res[0]

    _, (everyone,) = _copies_wait("gather_small_wait", small_gather[0], small_gather[1], small_gather[2], small_gather[3], after,
                                  _gather_copies, own_block=True)
    res = _adam_small(everyone.reshape(N_DEV, PACK_ROWS, D_MODEL), [w[k] for k in REPLICATED], [m[k] for k in REPLICATED],
                      [v[k] for k in REPLICATED])
    for i, k in enumerate(REPLICATED):
        out_g[k], out_d[k], out_m[k], out_v[k] = res[4 * i:4 * i + 4]
    loss = res[-1].reshape(())

    return (loss, grad_x, *[out_g[k] for k in WEIGHTS], *[out_d[k] for k in WEIGHTS], *[out_m[k] for k in WEIGHTS],
            *[out_v[k] for k in WEIGHTS])
```

```python
import functools

import jax
import jax.numpy as jnp
import numpy as np
from jax import lax
from jax.experimental import pallas as pl
from jax.experimental.pallas import tpu as pltpu

F32 = jnp.float32
BF16 = jnp.bfloat16

D_MODEL = 1024
FOX_HEADS = 8
FOX_HEAD_DIM = 64
FOX_WIDTH = 512
GDN_HEADS = 4
GDN_HEAD_DIM = 128
GDN_WIDTH = 512
CONV_WIDTH = 4
GDN_CHUNK = 128
GDN_GROUP = 4
FOX_BLOCK = 512
XATTN_HEADS = 4
XATTN_HEAD_DIM = 128
XATTN_WIDTH = 512
D_FF = 4096
EPS = 1e-6
NEG_INF = -1e30
N_DEV = 8

ADAM_LR = 0.001
ADAM_B1 = 0.9
ADAM_B2 = 0.999
ADAM_EPS = 1e-08
ADAM_WD = 0.01
ADAM_STEP = 10

P_FOX = 0
P_GDN = 1536
P_Z = 3072
P_SMALL = 3584
P_DIM = 3712
SM_F = 0
SM_B = 8
SM_A = 12
SM_ROWS = 16

LANES = 128
VMEM_LIMIT = 56 * 1024 * 1024

NN = (((1,), (0,)), ((), ()))
NT = (((1,), (1,)), ((), ()))
TN = (((0,), (0,)), ((), ()))


def _dot(a, b, dims=NN):
    return lax.dot_general(a.astype(BF16), b.astype(BF16), dims, preferred_element_type=F32)


def _cparams(sem=None):
    kw = dict(vmem_limit_bytes=VMEM_LIMIT)
    if sem is not None:
        kw["dimension_semantics"] = sem
    return pltpu.CompilerParams(**kw)


def _sigmoid(x):
    return 0.5 * (jnp.tanh(0.5 * x) + 1.0)


def _softplus(x):
    return jnp.maximum(x, 0.0) + jnp.log1p(jnp.exp(-jnp.abs(x)))


def _log_sigmoid(x):
    return -_softplus(-x)


def _rms(x, g):
    r = lax.rsqrt(jnp.mean(x * x, axis=-1, keepdims=True) + EPS)
    return x * r * g


def _rms_bwd(x, g, dy):
    r = lax.rsqrt(jnp.mean(x * x, axis=-1, keepdims=True) + EPS)
    xh = x * r
    dg = jnp.sum(dy * xh, axis=0, keepdims=True)
    dyg = dy * g
    dx = r * (dyg - xh * jnp.mean(dyg * xh, axis=-1, keepdims=True))
    return dx, dg


def _pair_stat(t, m0):
    s0 = jnp.sum(jnp.where(m0, t, 0.0), axis=-1, keepdims=True)
    s1 = jnp.sum(jnp.where(m0, 0.0, t), axis=-1, keepdims=True)
    return jnp.where(m0, s0, s1)


def _rms_pair(x, g, m0):
    r = lax.rsqrt(_pair_stat(x * x, m0) * (1.0 / FOX_HEAD_DIM) + EPS)
    return x * r * g


def _rms_pair_bwd(x, g, dy, m0):
    r = lax.rsqrt(_pair_stat(x * x, m0) * (1.0 / FOX_HEAD_DIM) + EPS)
    xh = x * r
    dg = jnp.sum(dy * xh, axis=0, keepdims=True)
    dyg = dy * g
    dx = r * (dyg - xh * (_pair_stat(dyg * xh, m0) * (1.0 / FOX_HEAD_DIM)))
    return dx, dg


@jax.custom_vjp
def _mm_nn(a, b):
    return _dot(a, b, NN)


_mm_nn.defvjp(lambda a, b: (_dot(a, b, NN), (a, b)),
              lambda r, g: (_dot(g, r[1], NT), _dot(r[0], g, TN)))


@jax.custom_vjp
def _mm_nt(a, b):
    return _dot(a, b, NT)


_mm_nt.defvjp(lambda a, b: (_dot(a, b, NT), (a, b)),
              lambda r, g: (_dot(g, r[1], NN), _dot(g, r[0], TN)))


@jax.custom_vjp
def _mm_tn(a, b):
    return _dot(a, b, TN)


_mm_tn.defvjp(lambda a, b: (_dot(a, b, TN), (a, b)),
              lambda r, g: (_dot(r[1], g, NT), _dot(r[0], g, NN)))


def _dot3(a, b, dims):
    ah = a.astype(BF16)
    al = (a - ah.astype(F32)).astype(BF16)
    bh = b.astype(BF16)
    bl = (b - bh.astype(F32)).astype(BF16)
    d = functools.partial(lax.dot_general, dimension_numbers=dims, preferred_element_type=F32)
    return d(ah, bh) + d(ah, bl) + d(al, bh)


def _neumann_inverses(mats):
    c = mats[0].shape[0]
    eye = (lax.broadcasted_iota(jnp.int32, (c, c), 0) == lax.broadcasted_iota(jnp.int32, (c, c), 1)).astype(F32)
    xs = [eye - a for a in mats]
    ps = list(mats)
    k = 2
    while k < c + 1:
        ps = [_dot3(p, p, NN) for p in ps]
        xs = [x + _dot3(x, p, NN) for x, p in zip(xs, ps)]
        k *= 2
    return xs


@jax.custom_vjp
def _unit_lower_inverses(mats):
    return _neumann_inverses(mats)


def _unit_lower_inverses_fwd(mats):
    ts = _neumann_inverses(mats)
    return ts, ts


def _unit_lower_inverses_bwd(ts, gs):
    left = [_dot3(t, g, TN) for t, g in zip(ts, gs)]
    return ([-_dot3(m, t, NT) for m, t in zip(left, ts)],)


_unit_lower_inverses.defvjp(_unit_lower_inverses_fwd, _unit_lower_inverses_bwd)


def _wgrad(a, b, name, bk=1024, bn=1024, bt=1024, column_blocks=None):
    t_len, k_len = a.shape
    n_len = b.shape[1]
    bk, bn, bt = min(bk, k_len), min(bn, n_len), min(bt, t_len)
    nt = t_len // bt

    def body(a_ref, b_ref, o_ref, acc_ref):
        t = pl.program_id(2)

        @pl.when(t == 0)
        def _():
            acc_ref[...] = jnp.zeros_like(acc_ref)

        acc_ref[...] += _dot(a_ref[...], b_ref[...], TN)

        @pl.when(t == nt - 1)
        def _():
            if column_blocks:
                for jj in range(bn // column_blocks):
                    o_ref[jj] = acc_ref[:, jj * column_blocks:(jj + 1) * column_blocks]
            else:
                o_ref[...] = acc_ref[...]

    if column_blocks:
        out_spec = pl.BlockSpec((bn // column_blocks, bk, column_blocks), lambda i, j, t: (j, i, 0))
        out_shape = jax.ShapeDtypeStruct((n_len // column_blocks, k_len, column_blocks), F32)
    else:
        out_spec = pl.BlockSpec((bk, bn), lambda i, j, t: (i, j))
        out_shape = jax.ShapeDtypeStruct((k_len, n_len), F32)
    return pl.pallas_call(
        body, name=name, grid=(k_len // bk, n_len // bn, nt),
        in_specs=[pl.BlockSpec((bt, bk), lambda i, j, t: (t, i)), pl.BlockSpec((bt, bn), lambda i, j, t: (t, j))],
        out_specs=out_spec, out_shape=out_shape,
        scratch_shapes=[pltpu.VMEM((bk, bn), F32)],
        compiler_params=_cparams(("parallel", "parallel", "arbitrary")),
    )(a, b)


def _wgrad_stacked(pieces, b, name, bn=512, bt=1024):
    t_len, n_len = b.shape
    n_p = len(pieces)
    starts = [int(s) for s in np.cumsum([0] + [p.shape[1] for p in pieces])]
    bn, bt = min(bn, n_len), min(bt, t_len)
    nt = t_len // bt

    def body(*refs):
        b_ref, o_ref, acc_ref = refs[n_p:]
        t = pl.program_id(1)

        @pl.when(t == 0)
        def _():
            acc_ref[...] = jnp.zeros_like(acc_ref)

        for k in range(n_p):
            acc_ref[starts[k]:starts[k + 1], :] += _dot(refs[k][...], b_ref[...], TN)

        @pl.when(t == nt - 1)
        def _():
            o_ref[...] = acc_ref[...]

    return pl.pallas_call(
        body, name=name, grid=(n_len // bn, nt),
        in_specs=[pl.BlockSpec((bt, p.shape[1]), lambda j, t: (t, 0)) for p in pieces] + [pl.BlockSpec((bt, bn), lambda j, t: (t, j))],
        out_specs=pl.BlockSpec((starts[-1], bn), lambda j, t: (0, j)),
        out_shape=jax.ShapeDtypeStruct((starts[-1], n_len), F32),
        scratch_shapes=[pltpu.VMEM((starts[-1], bn), F32)],
        compiler_params=_cparams(("parallel", "arbitrary")),
    )(*pieces, b)


def _rows_matmul(a, b, name, bt=512):
    r_len, t_len = a.shape
    n_len = b.shape[1]
    bt = min(bt, t_len)
    nt = t_len // bt

    def body(a_ref, b_ref, o_ref):
        t = pl.program_id(0)

        @pl.when(t == 0)
        def _():
            o_ref[...] = jnp.zeros_like(o_ref)

        o_ref[...] += _dot(a_ref[...], b_ref[...], NN)

    return pl.pallas_call(
        body, name=name, grid=(nt,),
        in_specs=[pl.BlockSpec((r_len, bt), lambda t: (0, t)), pl.BlockSpec((bt, n_len), lambda t: (t, 0))],
        out_specs=pl.BlockSpec((r_len, n_len), lambda t: (0, 0)),
        out_shape=jax.ShapeDtypeStruct((r_len, n_len), F32),
        compiler_params=_cparams(("arbitrary",)),
    )(a, b)


def _in_proj(x, g, wp, wst, tm=512):
    t_len, d = x.shape
    tm = min(tm, t_len)

    def body(x_ref, g_ref, wp_ref, wst_ref, h_ref, fox_ref, gdn_ref, z_ref, sm_ref, smt_ref):
        h = _rms(x_ref[...], g_ref[...]).astype(BF16)
        h_ref[...] = h
        p = _dot(h, wp_ref[...], NT)
        fox_ref[...] = p[:, P_FOX:P_GDN]
        gdn_ref[...] = p[:, P_GDN:P_Z]
        z_ref[...] = p[:, P_Z:P_SMALL]
        sm_ref[...] = p[:, P_SMALL:P_DIM]
        smt_ref[...] = _dot(wst_ref[...], h, NT)

    row = lambda i: (i, 0)
    fixed = lambda i: (0, 0)
    return pl.pallas_call(
        body, name="in_proj", grid=(t_len // tm,),
        in_specs=[pl.BlockSpec((tm, d), row), pl.BlockSpec((1, d), fixed), _resident((P_DIM, d)),
                  pl.BlockSpec((SM_ROWS, d), fixed)],
        out_specs=[pl.BlockSpec((tm, d), row), pl.BlockSpec((tm, 1536), row), pl.BlockSpec((tm, 1536), row),
                   pl.BlockSpec((tm, 512), row), pl.BlockSpec((tm, LANES), row), pl.BlockSpec((SM_ROWS, tm), lambda i: (0, i))],
        out_shape=[jax.ShapeDtypeStruct((t_len, d), BF16), jax.ShapeDtypeStruct((t_len, 1536), F32),
                   jax.ShapeDtypeStruct((t_len, 1536), F32), jax.ShapeDtypeStruct((t_len, 512), F32),
                   jax.ShapeDtypeStruct((t_len, LANES), F32), jax.ShapeDtypeStruct((SM_ROWS, t_len), F32)],
        compiler_params=_cparams(("parallel",)),
    )(x, g, wp, wst)


def _in_proj_bwd(dprojs, dsmt, x, g, wp, wst, dx1, tm=512):
    t_len, d = x.shape
    tm = min(tm, t_len)
    n_p = len(dprojs)
    starts = np.cumsum([0] + [p.shape[1] for p in dprojs])

    def body(*refs):
        dp_refs = refs[:n_p]
        dst_ref, x_ref, g_ref, wp_ref, wst_ref, dx1_ref, dx_ref, dg_ref = refs[n_p:]
        i = pl.program_id(0)
        dh = _dot(dst_ref[...], wst_ref[...], TN)
        for k in range(n_p):
            dh = dh + _dot(dp_refs[k][...], wp_ref[int(starts[k]):int(starts[k + 1]), :], NN)
        dxn, dg = _rms_bwd(x_ref[...], g_ref[...], dh)
        dx_ref[...] = dx1_ref[...] + dxn

        @pl.when(i == 0)
        def _():
            dg_ref[...] = jnp.zeros_like(dg_ref)

        dg_ref[...] += dg

    row = lambda i: (i, 0)
    fixed = lambda i: (0, 0)
    return pl.pallas_call(
        body, name="in_proj_bwd", grid=(t_len // tm,),
        in_specs=[pl.BlockSpec((tm, p.shape[1]), row) for p in dprojs] + [
            pl.BlockSpec((SM_ROWS, tm), lambda i: (0, i)), pl.BlockSpec((tm, d), row),
            pl.BlockSpec((1, d), fixed), _resident((P_DIM, d)), pl.BlockSpec((SM_ROWS, d), fixed),
            pl.BlockSpec((tm, d), row)],
        out_specs=[pl.BlockSpec((tm, d), row), pl.BlockSpec((1, d), fixed)],
        out_shape=[jax.ShapeDtypeStruct((t_len, d), F32), jax.ShapeDtypeStruct((1, d), F32)],
        compiler_params=_cparams(("arbitrary",)),
    )(*dprojs, dsmt, x, g, wp, wst, dx1)


def _fox_cum(smt, bias_col, n_batch, s_len, ck=256):
    ck = min(ck, s_len)

    def body(s_ref, b_ref, c_ref):
        tri = (lax.broadcasted_iota(jnp.int32, (ck, ck), 0) <= lax.broadcasted_iota(jnp.int32, (ck, ck), 1)).astype(F32)
        carry = jnp.zeros((SM_ROWS, 1), F32)
        for r in range(s_len // ck):
            ls = _log_sigmoid(s_ref[:, r * ck:(r + 1) * ck] + b_ref[...])
            c = jnp.dot(ls, tri, precision=lax.Precision.HIGHEST, preferred_element_type=F32) + carry
            c_ref[:, r * ck:(r + 1) * ck] = c
            carry = c[:, ck - 1:ck]

    return pl.pallas_call(
        body, name="fox_cum", grid=(n_batch,),
        in_specs=[pl.BlockSpec((SM_ROWS, s_len), lambda b: (0, b)), pl.BlockSpec((SM_ROWS, 1), lambda b: (0, 0))],
        out_specs=pl.BlockSpec((SM_ROWS, s_len), lambda b: (0, b)),
        out_shape=jax.ShapeDtypeStruct(smt.shape, F32),
        compiler_params=_cparams(("parallel",)),
    )(smt, bias_col)


def _fox_cum_bwd(dc, smt, bias_col, n_batch, s_len, ck=256):
    ck = min(ck, s_len)
    nr = s_len // ck

    def body(dc_ref, s_ref, b_ref, dl_ref, db_ref):
        b = pl.program_id(0)
        tri = (lax.broadcasted_iota(jnp.int32, (ck, ck), 0) >= lax.broadcasted_iota(jnp.int32, (ck, ck), 1)).astype(F32)
        carry = jnp.zeros((SM_ROWS, 1), F32)
        tot = jnp.zeros((SM_ROWS, 1), F32)
        for r in reversed(range(nr)):
            sl = slice(r * ck, (r + 1) * ck)
            dls = jnp.dot(dc_ref[:, sl], tri, precision=lax.Precision.HIGHEST, preferred_element_type=F32) + carry
            carry = dls[:, 0:1]
            dl = dls * (1.0 - _sigmoid(s_ref[:, sl] + b_ref[...]))
            dl_ref[:, sl] = dl
            tot = tot + jnp.sum(dl, axis=1, keepdims=True)

        @pl.when(b == 0)
        def _():
            db_ref[...] = jnp.zeros_like(db_ref)

        db_ref[...] += jnp.broadcast_to(tot, db_ref.shape)

    return pl.pallas_call(
        body, name="fox_cum_bwd", grid=(n_batch,),
        in_specs=[pl.BlockSpec((SM_ROWS, s_len), lambda b: (0, b)), pl.BlockSpec((SM_ROWS, s_len), lambda b: (0, b)),
                  pl.BlockSpec((SM_ROWS, 1), lambda b: (0, 0))],
        out_specs=[pl.BlockSpec((SM_ROWS, s_len), lambda b: (0, b)), pl.BlockSpec((SM_ROWS, LANES), lambda b: (0, 0))],
        out_shape=[jax.ShapeDtypeStruct(smt.shape, F32), jax.ShapeDtypeStruct((SM_ROWS, LANES), F32)],
        compiler_params=_cparams(("arbitrary",)),
    )(dc, smt, bias_col)


def _fox_diagonal_mask(tq):
    return lax.broadcasted_iota(jnp.int32, (tq, tq), 1) <= lax.broadcasted_iota(jnp.int32, (tq, tq), 0)


def _fox_fwd(pf, cb, gq2, gk2, go2, tq=256):
    n_batch, s_len, _ = pf.shape
    tq = min(tq, s_len)
    nq = s_len // tq
    scale = FOX_HEAD_DIM ** -0.5

    def body(q_ref, k_ref, v_ref, c_ref, gq_ref, gk_ref, go_ref, o_ref, on_ref, lse_ref, kh_ref, vh_ref):
        j = pl.program_id(1)
        i = pl.program_id(2)
        m0 = lax.broadcasted_iota(jnp.int32, (1, LANES), 1) < FOX_HEAD_DIM

        @pl.when(i == 0)
        def _():
            kn = _rms_pair(k_ref[0], gk_ref[...], m0)
            kh_ref[0] = jnp.where(m0, kn, 0.0).astype(BF16)
            kh_ref[1] = jnp.where(m0, 0.0, kn).astype(BF16)
            v = v_ref[0]
            vh_ref[0] = jnp.where(m0, v, 0.0).astype(BF16)
            vh_ref[1] = jnp.where(m0, 0.0, v).astype(BF16)

        qb = (_rms_pair(q_ref[0], gq_ref[...], m0) * scale).astype(BF16)

        def step(kb, carry, diagonal=False):
            ms, ls, acc = carry
            off = pl.multiple_of(kb * tq, tq)
            new_m, new_l, alphas, pv = [], [], [], []
            for hh in range(2):
                s = _dot(qb, kh_ref[hh, pl.ds(off, tq), :], NT)
                s = s - c_ref[0, kb, pl.ds(2 * j + hh, 1), :]
                if diagonal:
                    s = jnp.where(_fox_diagonal_mask(tq), s, NEG_INF)
                m_new = jnp.maximum(ms[hh], jnp.max(s, axis=-1, keepdims=True))
                alpha = jnp.exp(ms[hh] - m_new)
                p = jnp.exp(s - m_new)
                new_l.append(alpha * ls[hh] + jnp.sum(p, axis=-1, keepdims=True))
                new_m.append(m_new)
                alphas.append(alpha)
                pv.append(_dot(p, vh_ref[hh, pl.ds(off, tq), :], NN))
            acc = jnp.where(m0, alphas[0], alphas[1]) * acc + pv[0] + pv[1]
            return tuple(new_m), tuple(new_l), acc

        init_m = (jnp.full((tq, 1), NEG_INF, F32),) * 2
        init_l = (jnp.zeros((tq, 1), F32),) * 2
        carry = lax.fori_loop(0, i, step, (init_m, init_l, jnp.zeros((tq, LANES), F32)))
        ms, ls, acc = step(i, carry, diagonal=True)
        o = acc / jnp.where(m0, ls[0], ls[1])
        o_ref[0] = o
        on_ref[0] = _rms_pair(o, go_ref[...], m0).astype(BF16)
        lse_ref[0] = jnp.where(m0, ms[0] + jnp.log(ls[0]), ms[1] + jnp.log(ls[1]))

    fixed = lambda b, j, i: (0, 0)
    tile = lambda b, j, i: (b, i, j)
    return pl.pallas_call(
        body, name="fox_fwd", grid=(n_batch, 4, nq),
        in_specs=[pl.BlockSpec((1, tq, LANES), tile), pl.BlockSpec((1, s_len, LANES), lambda b, j, i: (b, 0, 4 + j)),
                  pl.BlockSpec((1, s_len, LANES), lambda b, j, i: (b, 0, 8 + j)),
                  pl.BlockSpec((1, nq, SM_ROWS, tq), lambda b, j, i: (b, 0, 0, 0)),
                  pl.BlockSpec((1, LANES), fixed), pl.BlockSpec((1, LANES), fixed), pl.BlockSpec((1, LANES), fixed)],
        out_specs=[pl.BlockSpec((1, tq, LANES), tile), pl.BlockSpec((1, tq, LANES), tile), pl.BlockSpec((1, tq, LANES), tile)],
        out_shape=[jax.ShapeDtypeStruct((n_batch, s_len, FOX_WIDTH), F32), jax.ShapeDtypeStruct((n_batch, s_len, FOX_WIDTH), BF16),
                   jax.ShapeDtypeStruct((n_batch, s_len, FOX_WIDTH), F32)],
        scratch_shapes=[pltpu.VMEM((2, s_len, LANES), BF16), pltpu.VMEM((2, s_len, LANES), BF16)],
        compiler_params=_cparams(("parallel", "parallel", "arbitrary")),
    )(pf, pf, pf, cb, gq2, gk2, go2)


def _fox_bwd(pf, cb, gq2, gk2, go2, o, lse, don, tq=256):
    n_batch, s_len, _ = pf.shape
    tq = min(tq, s_len)
    nq = s_len // tq
    scale = FOX_HEAD_DIM ** -0.5

    def body(q_ref, k_ref, v_ref, c_ref, gq_ref, gk_ref, go_ref, o_ref, lse_ref, don_ref,
             dq_ref, dk_ref, dv_ref, dc_ref, dgq_ref, dgk_ref, dgo_ref, kh_ref, vh_ref, dka_ref, dva_ref, dca_ref):
        b = pl.program_id(0)
        j = pl.program_id(1)
        i = pl.program_id(2)
        m0 = lax.broadcasted_iota(jnp.int32, (1, LANES), 1) < FOX_HEAD_DIM

        @pl.when((b == 0) & (j == 0) & (i == 0))
        def _():
            dgq_ref[...] = jnp.zeros_like(dgq_ref)
            dgk_ref[...] = jnp.zeros_like(dgk_ref)
            dgo_ref[...] = jnp.zeros_like(dgo_ref)

        @pl.when(i == 0)
        def _():
            kn = _rms_pair(k_ref[0], gk_ref[...], m0)
            kh_ref[0] = jnp.where(m0, kn, 0.0).astype(BF16)
            kh_ref[1] = jnp.where(m0, 0.0, kn).astype(BF16)
            v = v_ref[0]
            vh_ref[0] = jnp.where(m0, v, 0.0).astype(BF16)
            vh_ref[1] = jnp.where(m0, 0.0, v).astype(BF16)
            dka_ref[...] = jnp.zeros_like(dka_ref)
            dva_ref[...] = jnp.zeros_like(dva_ref)
            dca_ref[...] = jnp.zeros_like(dca_ref)

        q = q_ref[0]
        qn = _rms_pair(q, gq_ref[...], m0)
        qs = qn * scale
        qb = qs.astype(BF16)
        qh = (jnp.where(m0, qs, 0.0).astype(BF16), jnp.where(m0, 0.0, qs).astype(BF16))
        ot = o_ref[0]
        do, dgo = _rms_pair_bwd(ot, go_ref[...], don_ref[0], m0)
        dgo_ref[...] += dgo
        dd = do * ot
        delta = (jnp.sum(jnp.where(m0, dd, 0.0), axis=-1, keepdims=True), jnp.sum(jnp.where(m0, 0.0, dd), axis=-1, keepdims=True))
        doh = (jnp.where(m0, do, 0.0).astype(BF16), jnp.where(m0, 0.0, do).astype(BF16))
        lse_t = lse_ref[0]
        lse_h = (lse_t[:, 0:1], lse_t[:, FOX_HEAD_DIM:FOX_HEAD_DIM + 1])

        def step(kb, carry, diagonal=False):
            dqn, rs = carry
            rs = list(rs)
            off = pl.multiple_of(kb * tq, tq)
            for hh in range(2):
                kblk = kh_ref[hh, pl.ds(off, tq), :]
                vblk = vh_ref[hh, pl.ds(off, tq), :]
                s = _dot(qb, kblk, NT)
                s = s - c_ref[0, kb, pl.ds(2 * j + hh, 1), :]
                if diagonal:
                    s = jnp.where(_fox_diagonal_mask(tq), s, NEG_INF)
                p = jnp.exp(s - lse_h[hh])
                dp = _dot(doh[hh], vblk, NT)
                ds = p * (dp - delta[hh])
                dva_ref[pl.ds(off, tq), :] += _dot(p, doh[hh], TN)
                dka_ref[pl.ds(off, tq), :] += _dot(ds, qh[hh], TN)
                dca_ref[kb, hh:hh + 1, :] += -jnp.sum(ds, axis=0, keepdims=True)
                rs[hh] = rs[hh] + jnp.sum(ds, axis=-1, keepdims=True)
                dqn = dqn + _dot(ds, kblk, NN)
            return dqn, tuple(rs)

        carry = lax.fori_loop(0, i, step, (jnp.zeros((tq, LANES), F32), (jnp.zeros((tq, 1), F32),) * 2))
        dqn, rs = step(i, carry, diagonal=True)
        dqn = dqn * scale
        rs_rows = jnp.where(m0, rs[0], rs[1]).T
        dca_ref[i, 0:1, :] += rs_rows[0:1, :]
        dca_ref[i, 1:2, :] += rs_rows[FOX_HEAD_DIM:FOX_HEAD_DIM + 1, :]
        dq, dgq = _rms_pair_bwd(q, gq_ref[...], dqn, m0)
        dq_ref[0] = dq.astype(BF16)
        dgq_ref[...] += dgq

        @pl.when(i == nq - 1)
        def _():
            dk, dgk = _rms_pair_bwd(k_ref[0], gk_ref[...], dka_ref[...], m0)
            dk_ref[0] = dk.astype(BF16)
            dgk_ref[...] += dgk
            dv_ref[0] = dva_ref[...].astype(BF16)
            dc_ref[0, 0] = dca_ref[...]

    fixed = lambda b, j, i: (0, 0)
    tile = lambda b, j, i: (b, i, j)
    full = lambda b, j, i: (b, 0, j)
    wide = jax.ShapeDtypeStruct((n_batch, s_len, FOX_WIDTH), BF16)
    gain = jax.ShapeDtypeStruct((1, LANES), F32)
    return pl.pallas_call(
        body, name="fox_bwd", grid=(n_batch, 4, nq),
        in_specs=[pl.BlockSpec((1, tq, LANES), tile), pl.BlockSpec((1, s_len, LANES), lambda b, j, i: (b, 0, 4 + j)),
                  pl.BlockSpec((1, s_len, LANES), lambda b, j, i: (b, 0, 8 + j)),
                  pl.BlockSpec((1, nq, SM_ROWS, tq), lambda b, j, i: (b, 0, 0, 0)),
                  pl.BlockSpec((1, LANES), fixed), pl.BlockSpec((1, LANES), fixed), pl.BlockSpec((1, LANES), fixed),
                  pl.BlockSpec((1, tq, LANES), tile), pl.BlockSpec((1, tq, LANES), tile), pl.BlockSpec((1, tq, LANES), tile)],
        out_specs=[pl.BlockSpec((1, tq, LANES), tile), pl.BlockSpec((1, s_len, LANES), full), pl.BlockSpec((1, s_len, LANES), full),
                   pl.BlockSpec((1, 1, nq, 8, tq), lambda b, j, i: (b, j, 0, 0, 0)),
                   pl.BlockSpec((1, LANES), fixed), pl.BlockSpec((1, LANES), fixed), pl.BlockSpec((1, LANES), fixed)],
        out_shape=[wide, wide, wide, jax.ShapeDtypeStruct((n_batch, 4, nq, 8, tq), F32), gain, gain, gain],
        scratch_shapes=[pltpu.VMEM((2, s_len, LANES), BF16), pltpu.VMEM((2, s_len, LANES), BF16),
                        pltpu.VMEM((s_len, LANES), F32), pltpu.VMEM((s_len, LANES), F32), pltpu.VMEM((nq, 8, tq), F32)],
        compiler_params=_cparams(("arbitrary", "arbitrary", "arbitrary")),
    )(pf, pf, pf, cb, gq2, gk2, go2, o, lse, don)


def _conv_padded(x_ref, w, pad_ref, s_len, cols=slice(None)):
    pad_ref[0:8, :] = jnp.zeros((8, pad_ref.shape[1]), F32)
    pad_ref[8:8 + s_len, :] = x_ref[0, :, cols]
    return (w[3:4] * pad_ref[8:8 + s_len, :] + w[2:3] * pad_ref[7:7 + s_len, :] + w[1:2] * pad_ref[6:6 + s_len, :]
            + w[0:1] * pad_ref[5:5 + s_len, :])


def _gdn_pre(pg, conv_w):
    n_batch, s_len, width = pg.shape
    bw = GDN_WIDTH
    hd = GDN_HEAD_DIM

    def body(x_ref, w_ref, o_ref, pad_ref):
        part = pl.program_id(1)
        y = _conv_padded(x_ref, w_ref[...], pad_ref, s_len)
        s = y * _sigmoid(y)
        for h in range(GDN_HEADS):
            sh = s[:, h * hd:(h + 1) * hd]
            sn = sh * lax.rsqrt(jnp.sum(sh * sh, axis=-1, keepdims=True) + EPS)
            o_ref[0, :, h * hd:(h + 1) * hd] = jnp.where(part < 2, sn, sh)

    return pl.pallas_call(
        body, name="gdn_pre", grid=(n_batch, width // bw),
        in_specs=[pl.BlockSpec((1, s_len, bw), lambda b, c: (b, 0, c)), pl.BlockSpec((8, bw), lambda b, c: (0, c))],
        out_specs=pl.BlockSpec((1, s_len, bw), lambda b, c: (b, 0, c)),
        out_shape=jax.ShapeDtypeStruct(pg.shape, F32),
        scratch_shapes=[pltpu.VMEM((s_len + 8, bw), F32)],
        compiler_params=_cparams(("parallel", "parallel")),
    )(pg, conv_w)


def _gdn_pre_bwd(pg, conv_w, dout):
    n_batch, s_len, width = pg.shape
    bw = GDN_WIDTH
    hd = GDN_HEAD_DIM

    def body(x_ref, w_ref, d_ref, dx_ref, dw_ref, pad_ref, tail_ref):
        part = pl.program_id(0)
        b = pl.program_id(1)

        @pl.when(b == 0)
        def _():
            dw_ref[...] = jnp.zeros_like(dw_ref)

        for h in range(GDN_HEADS):
            cols = slice(h * hd, (h + 1) * hd)
            x = x_ref[0, :, cols]
            w = w_ref[:, cols]
            d = d_ref[0, :, cols]
            y = _conv_padded(x_ref, w, pad_ref, s_len, cols)
            sig = _sigmoid(y)
            s = y * sig
            rr = lax.rsqrt(jnp.sum(s * s, axis=-1, keepdims=True) + EPS)
            sn = s * rr
            ds_n = rr * (d - sn * jnp.sum(d * sn, axis=-1, keepdims=True))
            ds = jnp.where(part < 2, ds_n, d)
            dy = ds * (sig * (1.0 + y * (1.0 - sig)))
            tail_ref[0:s_len, :] = dy
            tail_ref[s_len:s_len + 8, :] = jnp.zeros((8, LANES), F32)
            dyu = [tail_ref[3 - jj:3 - jj + s_len, :] for jj in range(CONV_WIDTH)]
            dx = w[0:1] * dyu[0] + w[1:2] * dyu[1] + w[2:3] * dyu[2] + w[3:4] * dyu[3]
            dx_ref[0, :, cols] = dx.astype(BF16)
            dw = [jnp.sum(dyu[jj] * x, axis=0, keepdims=True) for jj in range(CONV_WIDTH)]
            rows = lax.broadcasted_iota(jnp.int32, (8, LANES), 0)
            dwb = jnp.zeros((8, LANES), F32)
            for jj in range(CONV_WIDTH):
                dwb = dwb + jnp.where(rows == jj, dw[jj], 0.0)
            dw_ref[:, cols] += dwb

    blk = lambda c, b: (b, 0, c)
    return pl.pallas_call(
        body, name="gdn_pre_bwd", grid=(width // bw, n_batch),
        in_specs=[pl.BlockSpec((1, s_len, bw), blk), pl.BlockSpec((8, bw), lambda c, b: (0, c)), pl.BlockSpec((1, s_len, bw), blk)],
        out_specs=[pl.BlockSpec((1, s_len, bw), blk), pl.BlockSpec((8, bw), lambda c, b: (0, c))],
        out_shape=[jax.ShapeDtypeStruct(pg.shape, BF16), jax.ShapeDtypeStruct((8, width), F32)],
        scratch_shapes=[pltpu.VMEM((s_len + 8, LANES), F32), pltpu.VMEM((s_len + 8, LANES), F32)],
        compiler_params=_cparams(("parallel", "arbitrary")),
    )(pg, conv_w, dout)


def _gdn_gates(smc, smr, a_c, dt_c, a_r, dt_r, h):
    lane = lax.broadcasted_iota(jnp.int32, (1, LANES), 1)
    sub = lax.broadcasted_iota(jnp.int32, (SM_ROWS, 1), 0)
    beta_c = jnp.sum(jnp.where(lane == SM_B + h, _sigmoid(smc), 0.0), axis=1, keepdims=True)
    g_all_c = -jnp.exp(a_c) * _softplus(smc + dt_c)
    g_c = jnp.sum(jnp.where(lane == SM_A + h, g_all_c, 0.0), axis=1, keepdims=True)
    g_all_r = -jnp.exp(a_r) * _softplus(smr + dt_r)
    g_r = jnp.sum(jnp.where(sub == SM_A + h, g_all_r, 0.0), axis=0, keepdims=True)
    return beta_c, g_c, g_r


@jax.custom_vjp
def _known_inverse(a, t):
    return t


_known_inverse.defvjp(lambda a, t: (t, t),
                      lambda t, g: (-_dot3(_dot3(t, g, TN), t, NT), jnp.zeros_like(t)))


def _gdn_group(qkv, z, smc, smr, a_c, dt_c, a_r, dt_r, go, states, inverses=None):
    n_grp = len(qkv)
    c = qkv[0].shape[0]
    hd = GDN_HEAD_DIM
    pairs = [(g, h) for g in range(n_grp) for h in range(GDN_HEADS)]
    ii = lax.broadcasted_iota(jnp.int32, (c, c), 0)
    jj = lax.broadcasted_iota(jnp.int32, (c, c), 1)
    incl = ii >= jj
    col = lambda arr, base, h: arr[:, base + h * hd:base + (h + 1) * hd]

    qs, ks, kbs, vbs, gcs, g_lasts, amats, intras = [], [], [], [], [], [], [], []
    for g, h in pairs:
        beta_c, g_c, g_r = _gdn_gates(smc[g], smr[g], a_c, dt_c, a_r, dt_r, h)
        gc_c = jnp.sum(jnp.where(incl, g_r, 0.0), axis=1, keepdims=True)
        gc_r = jnp.sum(jnp.where(ii <= jj, g_c, 0.0), axis=0, keepdims=True)
        decay = jnp.where(incl, jnp.exp(jnp.where(incl, gc_c - gc_r, 0.0)), 0.0)
        k = col(qkv[g], GDN_WIDTH, h)
        kb = k * beta_c
        qs.append(col(qkv[g], 0, h) * (hd ** -0.5))
        ks.append(k)
        kbs.append(kb)
        vbs.append(col(qkv[g], 2 * GDN_WIDTH, h) * beta_c)
        gcs.append(gc_c)
        g_lasts.append(jnp.sum(g_c, axis=0, keepdims=True))
        both = _mm_nt(jnp.concatenate([kb, qs[-1]], axis=0), k)
        amats.append(jnp.where(ii > jj, both[0:c] * decay, 0.0))
        intras.append(both[c:2 * c] * decay)
    ts = _unit_lower_inverses(amats) if inverses is None else [_known_inverse(a, t) for a, t in zip(amats, inverses)]
    egcs = [jnp.exp(gc) for gc in gcs]
    uws = [_mm_nn(t, jnp.concatenate([vb, kb * e], axis=1)) for t, vb, kb, e in zip(ts, vbs, kbs, egcs)]
    us = [uw[:, 0:hd] for uw in uws]
    ws = [uw[:, hd:2 * hd] for uw in uws]
    qes = [q * e for q, e in zip(qs, egcs)]
    kds = [k * jnp.exp(gl - gc) for k, gl, gc in zip(ks, g_lasts, gcs)]
    sdecs = [jnp.exp(gl) for gl in g_lasts]

    outs = []
    for g in range(n_grp):
        idx = [g * GDN_HEADS + h for h in range(GDN_HEADS)]
        v_new = [us[i] - _mm_nn(ws[i], states[h]) for h, i in enumerate(idx)]
        o = [_mm_nn(jnp.concatenate([qes[i], intras[i]], axis=1), jnp.concatenate([states[h], v_new[h]], axis=0))
             for h, i in enumerate(idx)]
        states = [states[h] * sdecs[i] + _mm_tn(kds[i], v_new[h]) for h, i in enumerate(idx)]
        outs.append([_rms(o[h], go) * (col(z[g], 0, h) * _sigmoid(col(z[g], 0, h))) for h in range(GDN_HEADS)])
    return outs, states, ts


def _gdn_group_size(n_chunks):
    return GDN_GROUP if n_chunks % GDN_GROUP == 0 else 1


def _gdn_fwd(qkvn, z, smc, smr, a_c, dt_c, a_r, dt_r, go):
    n_batch, s_len, _ = qkvn.shape
    c = GDN_CHUNK
    n = s_len // c
    grp = _gdn_group_size(n)
    ng = n // grp
    gc = grp * c
    hd = GDN_HEAD_DIM

    def body(qkv_ref, z_ref, smc_ref, smr_ref, ac_ref, dc_ref, ar_ref, dr_ref, go_ref, og_ref, st_ref, inv_ref, s_ref):
        @pl.when(pl.program_id(1) == 0)
        def _():
            s_ref[...] = jnp.zeros_like(s_ref)

        states = [s_ref[h] for h in range(GDN_HEADS)]
        for h in range(GDN_HEADS):
            st_ref[0, 0, h] = states[h]
        rows = lambda k: slice(k * c, (k + 1) * c)
        outs, nxt, invs = _gdn_group([qkv_ref[0, rows(k), :] for k in range(grp)], [z_ref[0, rows(k), :] for k in range(grp)],
                                     [smc_ref[0, rows(k), :] for k in range(grp)], [smr_ref[k] for k in range(grp)],
                                     ac_ref[...], dc_ref[...], ar_ref[...], dr_ref[...], go_ref[...], states)
        for k in range(grp):
            for h in range(GDN_HEADS):
                og_ref[0, rows(k), h * hd:(h + 1) * hd] = outs[k][h].astype(BF16)
        for p, inv in enumerate(invs):
            inv_ref[0, 0, p] = inv
        for h in range(GDN_HEADS):
            s_ref[h] = nxt[h]

    tok = lambda b, i: (b, i, 0)
    fixed = lambda b, i: (0, 0)
    return pl.pallas_call(
        body, name="gdn_fwd", grid=(n_batch, ng),
        in_specs=[pl.BlockSpec((1, gc, 3 * GDN_WIDTH), tok), pl.BlockSpec((1, gc, GDN_WIDTH), tok), pl.BlockSpec((1, gc, LANES), tok),
                  pl.BlockSpec((grp, SM_ROWS, c), lambda b, i: (b * ng + i, 0, 0)),
                  pl.BlockSpec((1, LANES), fixed), pl.BlockSpec((1, LANES), fixed), pl.BlockSpec((SM_ROWS, 1), fixed),
                  pl.BlockSpec((SM_ROWS, 1), fixed), pl.BlockSpec((1, LANES), fixed)],
        out_specs=[pl.BlockSpec((1, gc, GDN_WIDTH), tok), pl.BlockSpec((1, 1, GDN_HEADS, hd, hd), lambda b, i: (b, i, 0, 0, 0)),
                   pl.BlockSpec((1, 1, grp * GDN_HEADS, c, c), lambda b, i: (b, i, 0, 0, 0))],
        out_shape=[jax.ShapeDtypeStruct((n_batch, s_len, GDN_WIDTH), BF16), jax.ShapeDtypeStruct((n_batch, ng, GDN_HEADS, hd, hd), F32),
                   jax.ShapeDtypeStruct((n_batch, ng, grp * GDN_HEADS, c, c), F32)],
        scratch_shapes=[pltpu.VMEM((GDN_HEADS, hd, hd), F32)],
        compiler_params=_cparams(("parallel", "arbitrary")),
    )(qkvn, z, smc, smr, a_c, dt_c, a_r, dt_r, go)


def _gdn_bwd(qkvn, z, smc, smr, a_c, dt_c, a_r, dt_r, go, states, inverses, dog):
    n_batch, s_len, _ = qkvn.shape
    c = GDN_CHUNK
    n = s_len // c
    grp = _gdn_group_size(n)
    ng = n // grp
    gc = grp * c
    hd = GDN_HEAD_DIM

    def body(qkv_ref, z_ref, smc_ref, smr_ref, ac_ref, dc_ref, ar_ref, dr_ref, go_ref, st_ref, inv_ref, dog_ref,
             dqkv_ref, dz_ref, dsmc_ref, dsmr_ref, dac_ref, ddc_ref, dar_ref, ddr_ref, dgo_ref, ds_ref):
        first = (pl.program_id(0) == 0) & (pl.program_id(1) == 0)

        @pl.when(pl.program_id(1) == 0)
        def _():
            ds_ref[...] = jnp.zeros_like(ds_ref)

        @pl.when(first)
        def _():
            for r in (dac_ref, ddc_ref, dar_ref, ddr_ref, dgo_ref):
                r[...] = jnp.zeros_like(r)

        rows = lambda k: slice(k * c, (k + 1) * c)
        states = [st_ref[0, 0, h] for h in range(GDN_HEADS)]
        prim = ([qkv_ref[0, rows(k), :] for k in range(grp)], [z_ref[0, rows(k), :] for k in range(grp)],
                [smc_ref[0, rows(k), :] for k in range(grp)], [smr_ref[k] for k in range(grp)],
                ac_ref[...], dc_ref[...], ar_ref[...], dr_ref[...], go_ref[...], states)
        invs = [inv_ref[0, 0, p] for p in range(grp * GDN_HEADS)]
        _, vjp = jax.vjp(functools.partial(_gdn_group, inverses=invs), *prim)
        cot = ([[dog_ref[0, rows(k), h * hd:(h + 1) * hd] for h in range(GDN_HEADS)] for k in range(grp)],
               [ds_ref[h] for h in range(GDN_HEADS)], [jnp.zeros((c, c), F32)] * (grp * GDN_HEADS))
        dqkv, dz, dsmc, dsmr, dac, ddc, dar, ddr, dgo, dstates = vjp(cot)
        for k in range(grp):
            dqkv_ref[0, rows(k), :] = dqkv[k]
            dz_ref[0, rows(k), :] = dz[k].astype(BF16)
            dsmc_ref[0, rows(k), :] = dsmc[k]
            dsmr_ref[k] = dsmr[k]
        dac_ref[...] += dac
        ddc_ref[...] += ddc
        dar_ref[...] += dar
        ddr_ref[...] += ddr
        dgo_ref[...] += dgo
        for h in range(GDN_HEADS):
            ds_ref[h] = dstates[h]

    tok = lambda b, i: (b, ng - 1 - i, 0)
    fixed = lambda b, i: (0, 0)
    lane_vec = jax.ShapeDtypeStruct((1, LANES), F32)
    row_vec = jax.ShapeDtypeStruct((SM_ROWS, 1), F32)
    return pl.pallas_call(
        body, name="gdn_bwd", grid=(n_batch, ng),
        in_specs=[pl.BlockSpec((1, gc, 3 * GDN_WIDTH), tok), pl.BlockSpec((1, gc, GDN_WIDTH), tok), pl.BlockSpec((1, gc, LANES), tok),
                  pl.BlockSpec((grp, SM_ROWS, c), lambda b, i: (b * ng + ng - 1 - i, 0, 0)),
                  pl.BlockSpec((1, LANES), fixed), pl.BlockSpec((1, LANES), fixed), pl.BlockSpec((SM_ROWS, 1), fixed),
                  pl.BlockSpec((SM_ROWS, 1), fixed), pl.BlockSpec((1, LANES), fixed),
                  pl.BlockSpec((1, 1, GDN_HEADS, hd, hd), lambda b, i: (b, ng - 1 - i, 0, 0, 0)),
                  pl.BlockSpec((1, 1, grp * GDN_HEADS, c, c), lambda b, i: (b, ng - 1 - i, 0, 0, 0)),
                  pl.BlockSpec((1, gc, GDN_WIDTH), lambda b, i: (b, ng - 1 - i, 1))],
        out_specs=[pl.BlockSpec((1, gc, 3 * GDN_WIDTH), tok), pl.BlockSpec((1, gc, GDN_WIDTH), tok), pl.BlockSpec((1, gc, LANES), tok),
                   pl.BlockSpec((grp, SM_ROWS, c), lambda b, i: (b * ng + ng - 1 - i, 0, 0)),
                   pl.BlockSpec((1, LANES), fixed), pl.BlockSpec((1, LANES), fixed), pl.BlockSpec((SM_ROWS, 1), fixed),
                   pl.BlockSpec((SM_ROWS, 1), fixed), pl.BlockSpec((1, LANES), fixed)],
        out_shape=[jax.ShapeDtypeStruct((n_batch, s_len, 3 * GDN_WIDTH), F32), jax.ShapeDtypeStruct((n_batch, s_len, GDN_WIDTH), BF16),
                   jax.ShapeDtypeStruct((n_batch, s_len, LANES), F32), jax.ShapeDtypeStruct((n_batch * n, SM_ROWS, c), F32),
                   lane_vec, lane_vec, row_vec, row_vec, lane_vec],
        scratch_shapes=[pltpu.VMEM((GDN_HEADS, hd, hd), F32)],
        compiler_params=_cparams(("arbitrary", "arbitrary")),
    )(qkvn, z, smc, smr, a_c, dt_c, a_r, dt_r, go, states, inverses, dog)


def _out_proj(x, oa, ob, w_out, g_x, w_cq, tm=512):
    t_len, d = x.shape
    tm = min(tm, t_len)

    def body(x_ref, oa_ref, ob_ref, wo_ref, g_ref, wq_ref, x1_ref, hq_ref, cq_ref):
        x1 = x_ref[...] + _dot(jnp.concatenate([oa_ref[...], ob_ref[...]], axis=1), wo_ref[...])
        x1_ref[...] = x1
        hq = _rms(x1, g_ref[...]).astype(BF16)
        hq_ref[...] = hq
        cq_ref[...] = _dot(hq, wq_ref[...])

    row = lambda i: (i, 0)
    fixed = lambda i: (0, 0)
    return pl.pallas_call(
        body, name="out_proj", grid=(t_len // tm,),
        in_specs=[pl.BlockSpec((tm, d), row), pl.BlockSpec((tm, FOX_WIDTH), row), pl.BlockSpec((tm, GDN_WIDTH), row),
                  _resident((d, d)), pl.BlockSpec((1, d), fixed), _resident((d, XATTN_WIDTH))],
        out_specs=[pl.BlockSpec((tm, d), row), pl.BlockSpec((tm, d), row), pl.BlockSpec((tm, XATTN_WIDTH), row)],
        out_shape=[jax.ShapeDtypeStruct((t_len, d), F32), jax.ShapeDtypeStruct((t_len, d), BF16), jax.ShapeDtypeStruct((t_len, XATTN_WIDTH), F32)],
        compiler_params=_cparams(("parallel",)),
    )(x, oa, ob, w_out, g_x, w_cq)


def _out_proj_bwd(dx1, w_out, tm=512):
    t_len, d = dx1.shape
    tm = min(tm, t_len)

    def body(dx_ref, w_ref, o_ref):
        o_ref[...] = _dot(dx_ref[...], w_ref[...], NT)

    return pl.pallas_call(
        body, name="out_proj_bwd", grid=(t_len // tm,),
        in_specs=[pl.BlockSpec((tm, d), lambda i: (i, 0)), pl.BlockSpec((d, d), lambda i: (0, 0))],
        out_specs=pl.BlockSpec((tm, d), lambda i: (i, 0)),
        out_shape=jax.ShapeDtypeStruct((t_len, d), F32),
        compiler_params=_cparams(("parallel",)),
    )(dx1, w_out)


def _mem_kv(mem, g, w_ckv, tm=256):
    t_len, d = mem.shape
    tm = min(tm, t_len)

    def body(x_ref, g_ref, w_ref, h_ref, o_ref):
        h = _rms(x_ref[...], g_ref[...]).astype(BF16)
        h_ref[...] = h
        o_ref[...] = _dot(h, w_ref[...])

    row = lambda i: (i, 0)
    fixed = lambda i: (0, 0)
    return pl.pallas_call(
        body, name="mem_kv", grid=(t_len // tm,),
        in_specs=[pl.BlockSpec((tm, d), row), pl.BlockSpec((1, d), fixed), pl.BlockSpec((d, 2 * XATTN_WIDTH), fixed)],
        out_specs=[pl.BlockSpec((tm, d), row), pl.BlockSpec((tm, 2 * XATTN_WIDTH), row)],
        out_shape=[jax.ShapeDtypeStruct((t_len, d), BF16), jax.ShapeDtypeStruct((t_len, 2 * XATTN_WIDTH), F32)],
        compiler_params=_cparams(("parallel",)),
    )(mem, g, w_ckv)


def _mem_kv_bwd(dckv, mem, g, w_ckv, tm=256):
    t_len, d = mem.shape
    tm = min(tm, t_len)

    def body(d_ref, x_ref, g_ref, w_ref, dg_ref):
        @pl.when(pl.program_id(0) == 0)
        def _():
            dg_ref[...] = jnp.zeros_like(dg_ref)

        dh = _dot(d_ref[...], w_ref[...], NT)
        _, dg = _rms_bwd(x_ref[...], g_ref[...], dh)
        dg_ref[...] += dg

    row = lambda i: (i, 0)
    fixed = lambda i: (0, 0)
    return pl.pallas_call(
        body, name="mem_kv_bwd", grid=(t_len // tm,),
        in_specs=[pl.BlockSpec((tm, 2 * XATTN_WIDTH), row), pl.BlockSpec((tm, d), row), pl.BlockSpec((1, d), fixed),
                  pl.BlockSpec((d, 2 * XATTN_WIDTH), fixed)],
        out_specs=pl.BlockSpec((1, d), fixed),
        out_shape=jax.ShapeDtypeStruct((1, d), F32),
        compiler_params=_cparams(("arbitrary",)),
    )(dckv, mem, g, w_ckv)


def _xattn_probs(qn, kn):
    s = _dot(qn, kn, NT) * (XATTN_HEAD_DIM ** -0.5)
    p = jnp.exp(s - jnp.max(s, axis=-1, keepdims=True))
    return p / jnp.sum(p, axis=-1, keepdims=True)


def _xattn_fwd(cq, ckv, x1, gq, gk, w_co, g_mlp, n_batch, s_len, m_len, tq=512):
    d = x1.shape[1]
    tq = min(tq, s_len)
    nq = s_len // tq
    hd = XATTN_HEAD_DIM

    def body(cq_ref, kv_ref, x1_ref, gq_ref, gk_ref, wo_ref, gm_ref, co_ref, x2_ref, hf_ref):
        outs = []
        for h in range(XATTN_HEADS):
            qn = _rms(cq_ref[:, h * hd:(h + 1) * hd], gq_ref[...])
            kn = _rms(kv_ref[:, h * hd:(h + 1) * hd], gk_ref[...])
            p = _xattn_probs(qn, kn)
            outs.append(_dot(p, kv_ref[:, XATTN_WIDTH + h * hd:XATTN_WIDTH + (h + 1) * hd]).astype(BF16))
        for h in range(XATTN_HEADS):
            co_ref[:, h * hd:(h + 1) * hd] = outs[h]
        x2 = x1_ref[...] + _dot(co_ref[...], wo_ref[...])
        x2_ref[...] = x2
        hf_ref[...] = _rms(x2, gm_ref[...]).astype(BF16)

    row = lambda b, i: (b * nq + i, 0)
    fixed = lambda b, i: (0, 0)
    t_len = n_batch * s_len
    return pl.pallas_call(
        body, name="xattn_fwd", grid=(n_batch, nq),
        in_specs=[pl.BlockSpec((tq, XATTN_WIDTH), row), pl.BlockSpec((m_len, 2 * XATTN_WIDTH), lambda b, i: (b, 0)),
                  pl.BlockSpec((tq, d), row), pl.BlockSpec((1, hd), fixed), pl.BlockSpec((1, hd), fixed),
                  pl.BlockSpec((XATTN_WIDTH, d), fixed), pl.BlockSpec((1, d), fixed)],
        out_specs=[pl.BlockSpec((tq, XATTN_WIDTH), row), pl.BlockSpec((tq, d), row), pl.BlockSpec((tq, d), row)],
        out_shape=[jax.ShapeDtypeStruct((t_len, XATTN_WIDTH), BF16), jax.ShapeDtypeStruct((t_len, d), F32),
                   jax.ShapeDtypeStruct((t_len, d), BF16)],
        compiler_params=_cparams(("parallel", "parallel")),
    )(cq, ckv, x1, gq, gk, w_co, g_mlp)


def _xattn_bwd(dx2, cq, ckv, x1, gq, gk, w_co, g_x, w_cq, n_batch, s_len, m_len, tq=512):
    d = x1.shape[1]
    tq = min(tq, s_len)
    nq = s_len // tq
    hd = XATTN_HEAD_DIM
    scale = XATTN_HEAD_DIM ** -0.5

    def body(dx2_ref, cq_ref, kv_ref, x1_ref, gq_ref, gk_ref, wo_ref, gx_ref, wq_ref,
             dx1_ref, dcq_ref, dkv_ref, dgq_ref, dgk_ref, dgx_ref, dk_acc, dv_acc):
        b = pl.program_id(0)
        i = pl.program_id(1)

        @pl.when((b == 0) & (i == 0))
        def _():
            dgq_ref[...] = jnp.zeros_like(dgq_ref)
            dgk_ref[...] = jnp.zeros_like(dgk_ref)
            dgx_ref[...] = jnp.zeros_like(dgx_ref)

        @pl.when(i == 0)
        def _():
            dk_acc[...] = jnp.zeros_like(dk_acc)
            dv_acc[...] = jnp.zeros_like(dv_acc)

        dx2 = dx2_ref[...]
        dco_all = _dot(dx2, wo_ref[...], NT)
        for h in range(XATTN_HEADS):
            sl = slice(h * hd, (h + 1) * hd)
            q = cq_ref[:, sl]
            qn = _rms(q, gq_ref[...])
            kn = _rms(kv_ref[:, sl], gk_ref[...])
            v = kv_ref[:, XATTN_WIDTH + h * hd:XATTN_WIDTH + (h + 1) * hd]
            p = _xattn_probs(qn, kn)
            dco = dco_all[:, sl]
            dv_acc[:, sl] += _dot(p, dco, TN)
            dp = _dot(dco, v, NT)
            ds = p * (dp - jnp.sum(dp * p, axis=-1, keepdims=True))
            dqn = _dot(ds, kn) * scale
            dk_acc[:, sl] += _dot(ds, qn, TN) * scale
            dq, dgq = _rms_bwd(q, gq_ref[...], dqn)
            dgq_ref[...] += dgq
            dcq_ref[:, sl] = dq.astype(BF16)
        dhq = _dot(dcq_ref[...], wq_ref[...], NT)
        dxn, dgx = _rms_bwd(x1_ref[...], gx_ref[...], dhq)
        dgx_ref[...] += dgx
        dx1_ref[...] = dx2 + dxn

        @pl.when(i == nq - 1)
        def _():
            for h in range(XATTN_HEADS):
                sl = slice(h * hd, (h + 1) * hd)
                dk, dgk = _rms_bwd(kv_ref[:, sl], gk_ref[...], dk_acc[:, sl])
                dgk_ref[...] += dgk
                dkv_ref[:, sl] = dk.astype(BF16)
                dkv_ref[:, XATTN_WIDTH + h * hd:XATTN_WIDTH + (h + 1) * hd] = dv_acc[:, sl].astype(BF16)

    row = lambda b, i: (b * nq + i, 0)
    fixed = lambda b, i: (0, 0)
    t_len = n_batch * s_len
    return pl.pallas_call(
        body, name="xattn_bwd", grid=(n_batch, nq),
        in_specs=[pl.BlockSpec((tq, d), row), pl.BlockSpec((tq, XATTN_WIDTH), row), pl.BlockSpec((m_len, 2 * XATTN_WIDTH), lambda b, i: (b, 0)),
                  pl.BlockSpec((tq, d), row), pl.BlockSpec((1, hd), fixed), pl.BlockSpec((1, hd), fixed),
                  pl.BlockSpec((XATTN_WIDTH, d), fixed), pl.BlockSpec((1, d), fixed), pl.BlockSpec((d, XATTN_WIDTH), fixed)],
        out_specs=[pl.BlockSpec((tq, d), row), pl.BlockSpec((tq, XATTN_WIDTH), row), pl.BlockSpec((m_len, 2 * XATTN_WIDTH), lambda b, i: (b, 0)),
                   pl.BlockSpec((1, hd), fixed), pl.BlockSpec((1, hd), fixed), pl.BlockSpec((1, d), fixed)],
        out_shape=[jax.ShapeDtypeStruct((t_len, d), F32), jax.ShapeDtypeStruct((t_len, XATTN_WIDTH), BF16),
                   jax.ShapeDtypeStruct((n_batch * m_len, 2 * XATTN_WIDTH), BF16),
                   jax.ShapeDtypeStruct((1, hd), F32), jax.ShapeDtypeStruct((1, hd), F32), jax.ShapeDtypeStruct((1, d), F32)],
        scratch_shapes=[pltpu.VMEM((m_len, XATTN_WIDTH), F32), pltpu.VMEM((m_len, XATTN_WIDTH), F32)],
        compiler_params=_cparams(("arbitrary", "arbitrary")),
    )(dx2, cq, ckv, x1, gq, gk, w_co, g_x, w_cq)


def _resident(shape):
    return pl.BlockSpec(shape, lambda *_: (0,) * len(shape), pipeline_mode=pl.Buffered(1))


def _mlp_fwd(hf, x2, target, w1, w2, tm=256, tf=1024):
    t_len, d = x2.shape
    f = w1.shape[1]
    tm, tf = min(tm, t_len), min(tf, f)

    def body(hf_ref, x2_ref, tg_ref, w1_ref, w2_ref, u_ref, a_ref, dy_ref, ls_ref):
        hf_t = hf_ref[...]
        for k in range(f // tf):
            cols = slice(k * tf, (k + 1) * tf)
            u = _dot(hf_t, w1_ref[:, cols])
            u_ref[:, cols] = u
            r = jnp.maximum(u, 0.0)
            a_ref[:, cols] = (r * r).astype(BF16)
        y = x2_ref[...] + _dot(a_ref[...], w2_ref[...])
        err = y - tg_ref[...]
        dy_ref[...] = err * (1.0 / d)
        ls_ref[...] = jnp.broadcast_to(jnp.sum(jnp.sum(err * err, axis=-1, keepdims=True) * (1.0 / d), axis=0, keepdims=True), ls_ref.shape)

    row = lambda i: (i, 0)
    return pl.pallas_call(
        body, name="mlp_fwd", grid=(t_len // tm,),
        in_specs=[pl.BlockSpec((tm, d), row), pl.BlockSpec((tm, d), row), pl.BlockSpec((tm, d), row), _resident((d, f)), _resident((f, d))],
        out_specs=[pl.BlockSpec((tm, f), row), pl.BlockSpec((tm, f), row), pl.BlockSpec((tm, d), row),
                   pl.BlockSpec((1, 8, LANES), lambda i: (i, 0, 0))],
        out_shape=[jax.ShapeDtypeStruct((t_len, f), F32), jax.ShapeDtypeStruct((t_len, f), BF16), jax.ShapeDtypeStruct((t_len, d), F32),
                   jax.ShapeDtypeStruct((t_len // tm, 8, LANES), F32)],
        compiler_params=_cparams(("parallel",)),
    )(hf, x2, target, w1, w2)


def _mlp_bwd(dy, u, x2, g, w1, w2, tm=256, tf=1024):
    t_len, d = x2.shape
    f = w1.shape[1]
    tm, tf = min(tm, t_len), min(tf, f)

    def body(dy_ref, u_ref, x2_ref, g_ref, w1_ref, w2_ref, du_ref, dx2_ref, dg_ref):
        @pl.when(pl.program_id(0) == 0)
        def _():
            dg_ref[...] = jnp.zeros_like(dg_ref)

        dy_t = dy_ref[...]
        dyb = dy_t.astype(BF16)
        for k in range(f // tf):
            cols = slice(k * tf, (k + 1) * tf)
            da = _dot(dyb, w2_ref[cols, :], NT)
            du_ref[:, cols] = (da * (2.0 * jnp.maximum(u_ref[:, cols], 0.0))).astype(BF16)
        dhf = _dot(du_ref[...], w1_ref[...], NT)
        dxn, dg = _rms_bwd(x2_ref[...], g_ref[...], dhf)
        dx2_ref[...] = dy_t + dxn
        dg_ref[...] += dg

    row = lambda i: (i, 0)
    fixed = lambda i: (0, 0)
    return pl.pallas_call(
        body, name="mlp_bwd", grid=(t_len // tm,),
        in_specs=[pl.BlockSpec((tm, d), row), pl.BlockSpec((tm, f), row), pl.BlockSpec((tm, d), row), pl.BlockSpec((1, d), fixed),
                  _resident((d, f)), _resident((f, d))],
        out_specs=[pl.BlockSpec((tm, f), row), pl.BlockSpec((tm, d), row), pl.BlockSpec((1, d), fixed)],
        out_shape=[jax.ShapeDtypeStruct((t_len, f), BF16), jax.ShapeDtypeStruct((t_len, d), F32), jax.ShapeDtypeStruct((1, d), F32)],
        compiler_params=_cparams(("arbitrary",)),
    )(dy, u, x2, g, w1, w2)


def _pad_lanes(v, offset=0, width=LANES):
    return jnp.zeros((1, width), F32).at[:, offset:offset + v.shape[1]].set(v)


def _col(v, offset=0, rows=SM_ROWS):
    return jnp.zeros((rows, 1), F32).at[offset:offset + v.shape[1], 0].set(v[0])


def _pack_small(g_mix, dgq, dgk, dbias, dgo, dac, dar, ddc, ddr, g_gdn_o, g_nx, g_mem, g_xq, g_xk, g_mlp, loss_tiles):
    def body(mix_ref, q_ref, k_ref, b_ref, o_ref, ac_ref, ar_ref, dc_ref, dr_ref, go_ref, nx_ref, mem_ref, xq_ref, xk_ref,
             mlp_ref, lt_ref, out_ref):
        lane = lax.broadcasted_iota(jnp.int32, (1, LANES), 1)
        diag = lax.broadcasted_iota(jnp.int32, (SM_ROWS, LANES), 0) == lax.broadcasted_iota(jnp.int32, (SM_ROWS, LANES), 1)

        def rolled(v, shift):
            return pltpu.roll(jnp.broadcast_to(v, (8, LANES)), shift, 1)[0:1, :]

        def rows_to_lanes(col):
            return jnp.sum(jnp.where(diag, col, 0.0), axis=0, keepdims=True)

        def put(row, v, n):
            out_ref[row:row + 1, 0:LANES] = jnp.where(lane < n, v, 0.0)

        out_ref[...] = jnp.zeros_like(out_ref)
        out_ref[0:1, :] = mix_ref[...]
        for row, ref in ((1, q_ref), (2, k_ref), (4, o_ref)):
            put(row, ref[...] + rolled(ref[...], FOX_HEAD_DIM), FOX_HEAD_DIM)
        put(3, rows_to_lanes(b_ref[...]), FOX_HEADS)
        for row, lane_ref, row_ref in ((5, ac_ref, ar_ref), (6, dc_ref, dr_ref)):
            put(row, rolled(lane_ref[...] + rows_to_lanes(row_ref[...]), LANES - SM_A), GDN_HEADS)
        put(7, go_ref[...], LANES)
        out_ref[8:9, :] = nx_ref[...]
        out_ref[9:10, :] = mem_ref[...]
        put(10, xq_ref[...], LANES)
        put(11, xk_ref[...], LANES)
        out_ref[12:13, :] = mlp_ref[...]
        put(LOSS_ROW, 0.5 * jnp.sum(lt_ref[...], axis=0)[0:1, :], 1)

    args = (g_mix, dgq, dgk, dbias, dgo, dac, dar, ddc, ddr, g_gdn_o, g_nx, g_mem, g_xq, g_xk, g_mlp, loss_tiles)
    return pl.pallas_call(body, name="pack_small", out_shape=jax.ShapeDtypeStruct((PACK_ROWS, D_MODEL), F32))(*args)


LATE_WEIGHTS = (("w_out", "w_cq", "w_ckv", "w_co"), ("w_mlp1", "w_mlp2"))
GRAD_GROUPS = (("w_mlp2", "w_mlp1"), ("w_co", "w_cq", "w_ckv", "w_out"), ("w_in", "gdn_conv_w"))


def _local_step(x, mem, target, norm_mix_g, w_in, fox_qnorm_g, fox_knorm_g, fox_f_bias, fox_onorm_g, gdn_conv_w, gdn_A_log,
                gdn_dt_bias, gdn_onorm_g, norm_xattn_g, mem_norm_g, xattn_qnorm_g, xattn_knorm_g, norm_mlp_g,
                late_weights, grads_ready=None, first_token=0.0):
    if grads_ready is None:
        grads_ready = lambda group: 0.0
    n_batch, s_len, d = x.shape
    m_len = mem.shape[1]
    t_len = n_batch * s_len
    tq = min(FOX_BLOCK, s_len)
    nq = s_len // tq
    n_chunks = s_len // GDN_CHUNK
    x2d = x.reshape(t_len, d)

    wp = jnp.concatenate([w_in[0:1536], w_in[1544:3080], w_in[3088:3600], w_in[1536:1544], w_in[3080:3088],
                          jnp.zeros((P_DIM - 3600, d), BF16)], axis=0)
    wst = jnp.concatenate([w_in[1536:1544], w_in[3080:3088]], axis=0)
    conv_w = jnp.concatenate([gdn_conv_w, jnp.zeros((8 - CONV_WIDTH, gdn_conv_w.shape[1]), F32)], axis=0)
    bias_col = _col(fox_f_bias, SM_F)
    gq2, gk2, go2 = (jnp.tile(g, (1, 2)) for g in (fox_qnorm_g, fox_knorm_g, fox_onorm_g))
    a_c, dt_c = _pad_lanes(gdn_A_log, SM_A), _pad_lanes(gdn_dt_bias, SM_A)
    a_r, dt_r = _col(gdn_A_log, SM_A), _col(gdn_dt_bias, SM_A)

    h1, pfox, pgdn, pz, sm, smt = _in_proj(x2d, norm_mix_g + first_token, wp, wst)
    c_rows = _fox_cum(smt, bias_col, n_batch, s_len)
    cb = c_rows.reshape(SM_ROWS, n_batch, nq, tq).transpose(1, 2, 0, 3)
    pf3 = pfox.reshape(n_batch, s_len, 1536)
    o_fox, oa, lse = _fox_fwd(pf3, cb, gq2, gk2, go2, tq)
    pg3 = pgdn.reshape(n_batch, s_len, 1536)
    qkvn = _gdn_pre(pg3, conv_w)
    z3 = pz.reshape(n_batch, s_len, GDN_WIDTH)
    smc = sm.reshape(n_batch, s_len, LANES)
    smr = smt.reshape(SM_ROWS, n_batch * n_chunks, GDN_CHUNK).transpose(1, 0, 2)
    ob, states, inverses = _gdn_fwd(qkvn, z3, smc, smr, a_c, dt_c, a_r, dt_r, gdn_onorm_g)
    oa2, ob2 = oa.reshape(t_len, FOX_WIDTH), ob.reshape(t_len, GDN_WIDTH)
    w_out, w_cq, w_ckv, w_co = late_weights(LATE_WEIGHTS[0], ob2)
    x1, hq, cq = _out_proj(x2d, oa2, ob2, w_out, norm_xattn_g, w_cq)
    mem2d = mem.reshape(n_batch * m_len, d)
    hm, ckv = _mem_kv(mem2d, mem_norm_g, w_ckv)
    co, x2, hf = _xattn_fwd(cq, ckv, x1, xattn_qnorm_g, xattn_knorm_g, w_co, norm_mlp_g, n_batch, s_len, m_len)
    w_mlp1, w_mlp2 = late_weights(LATE_WEIGHTS[1], hf)
    u, a_act, dy, loss_tiles = _mlp_fwd(hf, x2, target.reshape(t_len, d), w_mlp1, w_mlp2)

    grads = {}
    du, dx2, grads["norm_mlp_g"] = _mlp_bwd(dy, u, x2, norm_mlp_g, w_mlp1, w_mlp2)
    grads["w_mlp2"] = _wgrad(a_act, dy, "wgrad_mlp2", bt=2048)
    grads["w_mlp1"] = _wgrad(hf, du, "wgrad_mlp1", bt=2048, column_blocks=D_FF // N_DEV)
    token = grads_ready({k: grads[k] for k in GRAD_GROUPS[0]})
    grads["w_co"] = _wgrad(co, dx2, "wgrad_co", column_blocks=D_MODEL // N_DEV)
    dx1, dcq, dckv, grads["xattn_qnorm_g"], grads["xattn_knorm_g"], grads["norm_xattn_g"] = _xattn_bwd(
        dx2, cq, ckv, x1, xattn_qnorm_g + token, xattn_knorm_g, w_co, norm_xattn_g, w_cq, n_batch, s_len, m_len)
    grads["w_cq"] = _wgrad(hq, dcq, "wgrad_cq")
    grads["w_ckv"] = _wgrad(hm, dckv, "wgrad_ckv")
    grads["mem_norm_g"] = _mem_kv_bwd(dckv, mem2d, mem_norm_g, w_ckv)
    grads["w_out"] = _wgrad_stacked([oa2, ob2], dx1, "wgrad_out", bn=1024)
    token = grads_ready({k: grads[k] for k in GRAD_GROUPS[1]})
    dcat = _out_proj_bwd(dx1, w_out)
    dcat3 = dcat.reshape(n_batch, s_len, d)

    dqkvn, dz, dsmc, dsmr, dac, ddc, dar, ddr, grads["gdn_onorm_g"] = _gdn_bwd(
        qkvn, z3, smc, smr, a_c, dt_c, a_r, dt_r, gdn_onorm_g + token, states, inverses, dcat3)
    dpg, dconv = _gdn_pre_bwd(pg3, conv_w, dqkvn)
    grads["gdn_conv_w"] = dconv[0:CONV_WIDTH]

    dq, dk, dv, dcb, dgq, dgk, dgo = _fox_bwd(pf3, cb, gq2, gk2, go2, o_fox, lse, dcat3, tq)
    dc8 = dcb[:, :, :, 0:2, :].transpose(1, 3, 0, 2, 4).reshape(FOX_HEADS, t_len)
    dc_rows = jnp.concatenate([dc8, jnp.zeros((SM_ROWS - FOX_HEADS, t_len), F32)], axis=0)
    dl_rows, dbias = _fox_cum_bwd(dc_rows, smt, bias_col, n_batch, s_len)
    dsm_rows = jnp.concatenate([dl_rows[0:SM_B], dsmr.transpose(1, 0, 2).reshape(SM_ROWS, t_len)[SM_B:SM_ROWS]], axis=0)

    dprojs = [dq.reshape(t_len, FOX_WIDTH), dk.reshape(t_len, FOX_WIDTH), dv.reshape(t_len, FOX_WIDTH),
              dpg.reshape(t_len, 1536), dz.reshape(t_len, GDN_WIDTH), dsmc.reshape(t_len, LANES)]
    dwp = _wgrad_stacked(dprojs, h1, "wgrad_in")
    dwst = _rows_matmul(dsm_rows, h1, "wgrad_in_rows")
    dw_small = dwp[P_SMALL:P_SMALL + SM_ROWS] + dwst
    grads["w_in"] = jnp.concatenate([dwp[0:1536], dw_small[0:8], dwp[1536:3072], dw_small[8:16], dwp[3072:3584]], axis=0)
    token = grads_ready({k: grads[k] for k in GRAD_GROUPS[2]})
    grad_x, grads["norm_mix_g"] = _in_proj_bwd(dprojs, dsm_rows, x2d, norm_mix_g + token, wp, wst, dx1)
    packed = _pack_small(grads["norm_mix_g"], dgq, dgk, dbias, dgo, dac, dar, ddc, ddr, grads["gdn_onorm_g"], grads["norm_xattn_g"],
                         grads["mem_norm_g"], grads["xattn_qnorm_g"], grads["xattn_knorm_g"], grads["norm_mlp_g"], loss_tiles)
    return packed, grad_x.reshape(n_batch, s_len, d), {k: grads[k] for k in SHARDED}


MESH_ID = pl.DeviceIdType.MESH
ANY_SPEC = pl.BlockSpec(memory_space=pl.ANY)


def _place():
    x, y, c = lax.axis_index("x"), lax.axis_index("y"), lax.axis_index("c")
    return x, y, c, [(1 - x, y), (x, 1 - y), (1 - x, 1 - y)]


def _place_own(src_ref, dst_ref):
    def staged(buf, sem):
        for a, b in ((src_ref, buf), (buf, dst_ref)):
            cp = pltpu.make_async_copy(a, b, sem)
            cp.start()
            cp.wait()

    pl.run_scoped(staged, pltpu.VMEM(src_ref.shape, src_ref.dtype), pltpu.SemaphoreType.DMA)


def _all_gather_body(n, ins, outs, send_sems, recv_sems):
    x, y, c, chips = _place()
    me, sibling = (x, y, c), (x, y, 1 - c)

    def copy(a, k, block, to, src=None):
        dst = outs[a].at[4 * block[0] + 2 * block[1] + block[2]]
        return pltpu.make_async_remote_copy(src_ref=dst if src is None else src, dst_ref=dst, send_sem=send_sems.at[a, k],
                                            recv_sem=recv_sems.at[a, k], device_id=to, device_id_type=MESH_ID)

    first = []
    for a in range(n):
        first.append(copy(a, 0, me, sibling, src=ins[a]))
        first += [copy(a, 1 + j, me, (*chip, c), src=ins[a]) for j, chip in enumerate(chips)]
    for cp in first:
        cp.start()
    for a in range(n):
        _place_own(ins[a], outs[a].at[4 * x + 2 * y + c])
    passed = []
    for j, chip in enumerate(chips):
        for a in range(n):
            copy(a, 1 + j, (*chip, c), me).wait_recv()
            fwd = copy(a, 4 + j, (*chip, c), sibling)
            fwd.start()
            passed.append(fwd)
    for a in range(n):
        copy(a, 0, sibling, me).wait_recv()
        for j, chip in enumerate(chips):
            copy(a, 4 + j, (*chip, 1 - c), me).wait_recv()
    for cp in first + passed:
        cp.wait_send()


def _all_gather_hbm(arrs, name):
    n = len(arrs)

    def body(*refs):
        _all_gather_body(n, refs[:n], refs[n:2 * n], refs[2 * n], refs[2 * n + 1])

    return pl.pallas_call(
        body, name=name, in_specs=[ANY_SPEC] * n, out_specs=[ANY_SPEC] * n,
        out_shape=[jax.ShapeDtypeStruct((N_DEV,) + a.shape, a.dtype) for a in arrs],
        scratch_shapes=[pltpu.SemaphoreType.DMA((n, 7)), pltpu.SemaphoreType.DMA((n, 7))],
        compiler_params=pltpu.CompilerParams(vmem_limit_bytes=VMEM_LIMIT),
    )(*arrs)


def _pair_exchange(arrs, name):
    n = len(arrs)

    def body(*refs):
        ins, outs = refs[:n], refs[n:2 * n]
        send_sems, recv_sems = refs[2 * n:]
        x, y, c, _ = _place()
        copies = []
        for a in range(n):
            for chip in range(4):
                copies.append(pltpu.make_async_remote_copy(
                    src_ref=ins[a].at[2 * chip + (1 - c)], dst_ref=outs[a].at[chip], send_sem=send_sems.at[a, chip],
                    recv_sem=recv_sems.at[a, chip], device_id=(x, y, 1 - c), device_id_type=MESH_ID))
        for cp in copies:
            cp.start()
        for cp in copies:
            cp.wait()

    return pl.pallas_call(
        body, name=name, in_specs=[ANY_SPEC] * n, out_specs=[ANY_SPEC] * n,
        out_shape=[jax.ShapeDtypeStruct((4,) + a.shape[1:], a.dtype) for a in arrs],
        scratch_shapes=[pltpu.SemaphoreType.DMA((n, 4)), pltpu.SemaphoreType.DMA((n, 4))],
    )(*arrs)


HBM_SPEC = pl.BlockSpec(memory_space=pltpu.HBM)
SEM_SPEC = pl.BlockSpec(memory_space=pltpu.SEMAPHORE)
DATAFLOW = pltpu.SideEffectType.DATAFLOW_SIDE_EFFECTING


def _in_hbm(arrs):
    return [pltpu.with_memory_space_constraint(a, pltpu.HBM) for a in arrs]


def _copies_start(name, srcs, lands, make_copies, after):
    n = len(srcs)
    n_copies = len(make_copies(srcs, lands, None, None)[0])

    def body(*refs):
        send_sems, recv_sems = refs[2 * n + 1], refs[2 * n + 2]
        for row in make_copies(refs[:n], refs[n:2 * n], send_sems, recv_sems):
            for cp in row:
                cp.start()
        refs[-1][...] = jnp.zeros_like(refs[-1])

    sems = pltpu.SemaphoreType.DMA((n * n_copies,))
    thru = [pltpu.HBM(a.shape, a.dtype) for a in list(srcs) + list(lands)]
    res = pl.pallas_call(
        body, name=name, in_specs=[HBM_SPEC] * (2 * n) + [ANY_SPEC],
        out_specs=(SEM_SPEC, SEM_SPEC, *[HBM_SPEC] * (2 * n), pl.BlockSpec(memory_space=pltpu.VMEM)),
        out_shape=(sems, sems, *thru, jax.ShapeDtypeStruct((8, LANES), F32)),
        input_output_aliases={i: 2 + i for i in range(2 * n)},
        compiler_params=pltpu.CompilerParams(has_side_effects=DATAFLOW),
    )(*_in_hbm(list(srcs) + list(lands)), after)
    return res[0], res[1], list(res[2:2 + n]), list(res[2 + n:2 + 2 * n]), res[-1]


def _copies_wait(name, send_sems, recv_sems, srcs, lands, after, make_copies, own_block=False):
    n = len(srcs)

    def body(*refs):
        if own_block:
            for a in range(n):
                _place_own(refs[a], _own_part(refs[a], refs[3 * n + 3 + a]))
        for row in make_copies(refs[:n], refs[n:2 * n], refs[2 * n], refs[2 * n + 1]):
            for cp in row:
                cp.wait_send()
                cp.wait_recv()

    res = pl.pallas_call(
        body, name=name, in_specs=[HBM_SPEC] * (2 * n) + [SEM_SPEC, SEM_SPEC, ANY_SPEC],
        out_specs=tuple([HBM_SPEC] * (2 * n)),
        out_shape=tuple(pltpu.HBM(a.shape, a.dtype) for a in list(srcs) + list(lands)),
        input_output_aliases={i: i for i in range(2 * n)},
        compiler_params=pltpu.CompilerParams(has_side_effects=DATAFLOW, vmem_limit_bytes=VMEM_LIMIT),
    )(*srcs, *lands, send_sems, recv_sems, after)
    return list(res[:n]), list(res[n:])


def _own_part(src_ref, land_ref):
    me = 4 * lax.axis_index("x") + 2 * lax.axis_index("y") + lax.axis_index("c")
    rows, cols = src_ref.shape
    if land_ref.shape[0] == N_DEV * rows:
        return land_ref.at[pl.ds(pl.multiple_of(me * rows, rows), rows), :]
    return land_ref.at[:, pl.ds(pl.multiple_of(me * cols, cols), cols)]


def _gather_copies(srcs, lands, send_sems, recv_sems):
    if send_sems is None:
        return [[None] * 7]
    x, y, c, _ = _place()
    rows = []
    for a in range(len(srcs)):
        row = []
        for k in range(7):
            r = k + 1
            to = (1 - x if r & 4 else x, 1 - y if r & 2 else y, 1 - c if r & 1 else c)
            row.append(pltpu.make_async_remote_copy(
                src_ref=srcs[a], dst_ref=_own_part(srcs[a], lands[a]), send_sem=send_sems.at[7 * a + k], recv_sem=recv_sems.at[7 * a + k],
                device_id=to, device_id_type=MESH_ID))
        rows.append(row)
    return rows


def _scatter_copies(srcs, lands, send_sems, recv_sems):
    if send_sems is None:
        return [[None] * 7]
    x, y, c, _ = _place()
    rows = []
    for a in range(len(srcs)):
        row = []
        for k in range(7):
            r = k + 1
            to = (1 - x if r & 4 else x, 1 - y if r & 2 else y, 1 - c if r & 1 else c)
            row.append(pltpu.make_async_remote_copy(
                src_ref=srcs[a].at[4 * to[0] + 2 * to[1] + to[2]], dst_ref=lands[a].at[k], send_sem=send_sems.at[7 * a + k],
                recv_sem=recv_sems.at[7 * a + k], device_id=to, device_id_type=MESH_ID))
        rows.append(row)
    return rows


def _chip_copies(srcs, lands, send_sems, recv_sems):
    if send_sems is None:
        return [[None] * 3]
    x, y, c, chips = _place()
    return [[pltpu.make_async_remote_copy(
        src_ref=srcs[a].at[2 * chip[0] + chip[1]], dst_ref=lands[a].at[j], send_sem=send_sems.at[3 * a + j], recv_sem=recv_sems.at[3 * a + j],
        device_id=(*chip, c), device_id_type=MESH_ID) for j, chip in enumerate(chips)] for a in range(len(srcs))]


def _tile(rows, cols):
    if rows <= 256:
        return rows, cols
    tr = 256 if cols <= 512 else 128
    if rows % tr == 0:
        return tr, cols
    return rows, 512


def _pair_sum(core, own, got, name):
    _, rows, cols = own.shape
    tr, tc = _tile(rows, cols)

    def body(c_ref, own_ref, got_ref, o_ref):
        o_ref[0] = own_ref[0] + got_ref[0]

    return pl.pallas_call(
        body, name=name,
        grid_spec=pltpu.PrefetchScalarGridSpec(
            num_scalar_prefetch=1, grid=(4, rows // tr, cols // tc),
            in_specs=[pl.BlockSpec((1, tr, tc), lambda k, i, j, c: (2 * k + c[0], i, j)),
                      pl.BlockSpec((1, tr, tc), lambda k, i, j, c: (k, i, j))],
            out_specs=pl.BlockSpec((1, tr, tc), lambda k, i, j, c: (k, i, j))),
        out_shape=jax.ShapeDtypeStruct((4, rows, cols), F32),
        compiler_params=_cparams(("parallel", "parallel", "parallel")),
    )(core, own, got)


def _adamw(w, g, m, v):
    m_new = ADAM_B1 * m + (1.0 - ADAM_B1) * g
    v_new = ADAM_B2 * v + (1.0 - ADAM_B2) * (g * g)
    m_hat = m_new / (1.0 - ADAM_B1 ** ADAM_STEP)
    v_hat = v_new / (1.0 - ADAM_B2 ** ADAM_STEP)
    delta = -ADAM_LR * (m_hat / (jnp.sqrt(v_hat) + ADAM_EPS) + ADAM_WD * w)
    return delta, m_new, v_new


def _sum_adam(chip, sums, parts, w, m, v, name):
    n_parts, rows, cols = parts.shape
    tr, tc = _tile(rows, cols)

    def body(chip_ref, own_ref, p_ref, w_ref, m_ref, v_ref, g_ref, d_ref, mo_ref, vo_ref):
        g = own_ref[0]
        for k in range(n_parts):
            g = g + p_ref[k]
        g_ref[...] = g
        d_ref[...], mo_ref[...], vo_ref[...] = _adamw(w_ref[...], g, m_ref[...], v_ref[...])

    tile = pl.BlockSpec((tr, tc), lambda i, j, ch: (i, j))
    out = jax.ShapeDtypeStruct((rows, cols), F32)
    return pl.pallas_call(
        body, name=name,
        grid_spec=pltpu.PrefetchScalarGridSpec(
            num_scalar_prefetch=1, grid=(rows // tr, cols // tc),
            in_specs=[pl.BlockSpec((1, tr, tc), lambda i, j, ch: (ch[0], i, j)),
                      pl.BlockSpec((n_parts, tr, tc), lambda i, j, ch: (0, i, j)), tile, tile, tile],
            out_specs=[tile, tile, tile, tile]),
        out_shape=[out, out, out, out],
        compiler_params=_cparams(("parallel", "parallel")),
    )(chip, sums, parts, w, m, v)


SHARDED = ("w_in", "gdn_conv_w", "w_out", "w_cq", "w_ckv", "w_co", "w_mlp1", "w_mlp2")
TRANSPOSED = ("w_in",)
COLUMN_SHARDED = ("gdn_conv_w", "w_co", "w_mlp1")
REPLICATED = ("norm_mix_g", "fox_qnorm_g", "fox_knorm_g", "fox_f_bias", "fox_onorm_g", "gdn_A_log", "gdn_dt_bias", "gdn_onorm_g",
              "norm_xattn_g", "mem_norm_g", "xattn_qnorm_g", "xattn_knorm_g", "norm_mlp_g")
WEIGHTS = ("norm_mix_g", "w_in", "fox_qnorm_g", "fox_knorm_g", "fox_f_bias", "fox_onorm_g", "gdn_conv_w", "gdn_A_log", "gdn_dt_bias",
           "gdn_onorm_g", "w_out", "norm_xattn_g", "mem_norm_g", "w_cq", "w_ckv", "xattn_qnorm_g", "xattn_knorm_g", "w_co",
           "norm_mlp_g", "w_mlp1", "w_mlp2")
PACK_ROWS = 16
LOSS_ROW = len(REPLICATED)


def _whole(name, gathered):
    if name in COLUMN_SHARDED:
        return gathered.transpose(1, 0, 2).reshape(gathered.shape[1], N_DEV * gathered.shape[2])
    return gathered.reshape(N_DEV * gathered.shape[1], gathered.shape[2])


def _whole_shape(name, shard_shape):
    rows, cols = shard_shape
    return (rows, N_DEV * cols) if name in COLUMN_SHARDED else (N_DEV * rows, cols)


def _blocks(name, whole):
    if whole.ndim == 3:
        return whole
    if name in COLUMN_SHARDED:
        rows, cols = whole.shape
        return whole.reshape(rows, N_DEV, cols // N_DEV).transpose(1, 0, 2)
    return whole.reshape(N_DEV, whole.shape[0] // N_DEV, whole.shape[1])


def _adam_small(everyone, ws, ms, vs):
    n_par = len(ws)

    def body(*refs):
        ev_ref = refs[0]
        w_refs, m_refs, v_refs = (refs[1 + j * n_par:1 + (j + 1) * n_par] for j in range(3))
        outs = refs[1 + 3 * n_par:-1]
        sum_ref = refs[-1]
        total = ev_ref[0]
        for dev in range(1, N_DEV):
            total = total + ev_ref[dev]
        sum_ref[...] = total
        for i in range(n_par):
            n = w_refs[i].shape[1]
            g = sum_ref[i:i + 1, 0:n]
            outs[4 * i][...] = g
            outs[4 * i + 1][...], outs[4 * i + 2][...], outs[4 * i + 3][...] = _adamw(w_refs[i][...], g, m_refs[i][...], v_refs[i][...])
        outs[4 * n_par][...] = sum_ref[LOSS_ROW:LOSS_ROW + 1, 0:1]

    shapes = [jax.ShapeDtypeStruct(a.shape, F32) for a in ws for _ in range(4)] + [jax.ShapeDtypeStruct((1, 1), F32)]
    return pl.pallas_call(body, name="adam_small", out_shape=shapes,
                          scratch_shapes=[pltpu.VMEM((PACK_ROWS, D_MODEL), F32)])(everyone, *ws, *ms, *vs)


def kernel(x, mem, norm_mix_g, w_in, fox_qnorm_g, fox_knorm_g, fox_f_bias, fox_onorm_g, gdn_conv_w, gdn_A_log, gdn_dt_bias, gdn_onorm_g, w_out, norm_xattn_g, mem_norm_g, w_cq, w_ckv, xattn_qnorm_g, xattn_knorm_g, w_co, norm_mlp_g, w_mlp1, w_mlp2, loss_target, m_norm_mix_g, m_w_in, m_fox_qnorm_g, m_fox_knorm_g, m_fox_f_bias, m_fox_onorm_g, m_gdn_conv_w, m_gdn_A_log, m_gdn_dt_bias, m_gdn_onorm_g, m_w_out, m_norm_xattn_g, m_mem_norm_g, m_w_cq, m_w_ckv, m_xattn_qnorm_g, m_xattn_knorm_g, m_w_co, m_norm_mlp_g, m_w_mlp1, m_w_mlp2, v_norm_mix_g, v_w_in, v_fox_qnorm_g, v_fox_knorm_g, v_fox_f_bias, v_fox_onorm_g, v_gdn_conv_w, v_gdn_A_log, v_gdn_dt_bias, v_gdn_onorm_g, v_w_out, v_norm_xattn_g, v_mem_norm_g, v_w_cq, v_w_ckv, v_xattn_qnorm_g, v_xattn_knorm_g, v_w_co, v_norm_mlp_g, v_w_mlp1, v_w_mlp2):
    given = dict(locals())
    w = {k: given[k] for k in WEIGHTS}
    m = {k: given["m_" + k] for k in WEIGHTS}
    v = {k: given["v_" + k] for k in WEIGHTS}

    core = lax.axis_index("c").astype(jnp.int32).reshape(1)
    chip = (2 * lax.axis_index("x") + lax.axis_index("y")).astype(jnp.int32).reshape(1)
    me = 4 * lax.axis_index("x") + 2 * lax.axis_index("y") + lax.axis_index("c")

    local = lambda d: {k: jnp.transpose(d[k][0]) if k in TRANSPOSED else d[k][0] for k in SHARDED}
    w2, m2, v2 = local(w), local(m), local(v)
    shards = {k: w2[k] if k == "gdn_conv_w" else w2[k].astype(BF16) for k in SHARDED}
    early = [k for k in SHARDED if not any(k in group for group in LATE_WEIGHTS)]
    gathered = _all_gather_hbm([shards[k] for k in early], "gather_early")
    whole = {k: _whole(k, g) for k, g in zip(early, gathered)}
    gathers, after = {}, gathered[0]
    for i, group in enumerate(LATE_WEIGHTS):
        lands = [lax.empty(_whole_shape(k, shards[k].shape), BF16) for k in group]
        gathers[group] = _copies_start("gather_late_start_" + str(i), [shards[k] for k in group], lands, _gather_copies, after=after)
        after = gathers[group][4]
    first_token = after[0, 0]

    def late_weights(group, after):
        gather = gathers[group]
        _, lands = _copies_wait("gather_late_wait_" + str(LATE_WEIGHTS.index(group)), gather[0], gather[1], gather[2], gather[3],
                                after, _gather_copies, own_block=True)
        return lands

    pending = []

    def grads_ready(group):
        names = list(group)
        tag = str(len(pending))
        own = [_blocks(k, group[k]) for k in names]
        if "w_in" in names:
            got = _pair_exchange(own, "grad_pair_exchange_" + tag)
            srcs = [_pair_sum(core, o, g, "grad_pair_sum_" + k) for k, o, g in zip(names, own, got)]
            copies, index, n_parts = _chip_copies, chip, 3
        else:
            srcs, copies, index, n_parts = own, _scatter_copies, me.astype(jnp.int32).reshape(1), 7
        lands = [lax.empty((n_parts,) + s.shape[1:], s.dtype) for s in srcs]
        started = _copies_start("grad_exchange_start_" + tag, srcs, lands, copies, after=core)
        pending.append((names, started, copies, index))
        return started[4][0, 0]

    small = {k: w[k] for k in REPLICATED}
    packed, grad_x, _ = _local_step(x, mem, loss_target, **small, **whole, late_weights=late_weights,
                                    grads_ready=grads_ready, first_token=first_token)

    small_lands = [lax.empty((N_DEV * PACK_ROWS, D_MODEL), F32)]
    small_gather = _copies_start("gather_small_start", [packed], small_lands, _gather_copies, after=grad_x)

    out_g, out_d, out_m, out_v = {}, {}, {}, {}
    after = small_gather[4]
    for tag, (names, started, copies, index) in enumerate(pending):
        srcs, parts = _copies_wait("grad_exchange_wait_" + str(tag), started[0], started[1], started[2], started[3], after, copies)
        for k, s, p in zip(names, srcs, parts):
            res = _sum_adam(index, s, p, w2[k], m2[k], v2[k], "adam_" + k)
            out_g[k], out_d[k], out_m[k], out_v[k] = ((jnp.transpose(r) if k in TRANSPOSED else r)[None] for r in res)
            after = res[0]

    _, (everyone,) = _copies_wait("gather_small_wait", small_gather[0], small_gather[1], small_gather[2], small_gather[3], after,
                                  _gather_copies, own_block=True)
    res = _adam_small(everyone.reshape(N_DEV, PACK_ROWS, D_MODEL), [w[k] for k in REPLICATED], [m[k] for k in REPLICATED],
                      [v[k] for k in REPLICATED])
    for i, k in enumerate(REPLICATED):
        out_g[k], out_d[k], out_m[k], out_v[k] = res[4 * i:4 * i + 4]
    loss = res[-1].reshape(())

    return (loss, grad_x, *[out_g[k] for k in WEIGHTS], *[out_d[k] for k in WEIGHTS], *[out_m[k] for k in WEIGHTS],
            *[out_v[k] for k in WEIGHTS])
```

```python
import functools

import jax
import jax.numpy as jnp
import numpy as np
from jax import lax
from jax.experimental import pallas as pl
from jax.experimental.pallas import tpu as pltpu

F32 = jnp.float32
BF16 = jnp.bfloat16

D_MODEL = 1024
FOX_HEADS = 8
FOX_HEAD_DIM = 64
FOX_WIDTH = 512
GDN_HEADS = 4
GDN_HEAD_DIM = 128
GDN_WIDTH = 512
CONV_WIDTH = 4
GDN_CHUNK = 128
GDN_GROUP = 4
FOX_BLOCK = 512
XATTN_HEADS = 4
XATTN_HEAD_DIM = 128
XATTN_WIDTH = 512
D_FF = 4096
EPS = 1e-6
NEG_INF = -1e30
N_DEV = 8

ADAM_LR = 0.001
ADAM_B1 = 0.9
ADAM_B2 = 0.999
ADAM_EPS = 1e-08
ADAM_WD = 0.01
ADAM_STEP = 10

P_FOX = 0
P_GDN = 1536
P_Z = 3072
P_SMALL = 3584
P_DIM = 3712
SM_F = 0
SM_B = 8
SM_A = 12
SM_ROWS = 16

LANES = 128
VMEM_LIMIT = 56 * 1024 * 1024

NN = (((1,), (0,)), ((), ()))
NT = (((1,), (1,)), ((), ()))
TN = (((0,), (0,)), ((), ()))


def _dot(a, b, dims=NN):
    return lax.dot_general(a.astype(BF16), b.astype(BF16), dims, preferred_element_type=F32)


def _cparams(sem=None):
    kw = dict(vmem_limit_bytes=VMEM_LIMIT)
    if sem is not None:
        kw["dimension_semantics"] = sem
    return pltpu.CompilerParams(**kw)


def _sigmoid(x):
    return 0.5 * (jnp.tanh(0.5 * x) + 1.0)


def _softplus(x):
    return jnp.maximum(x, 0.0) + jnp.log1p(jnp.exp(-jnp.abs(x)))


def _log_sigmoid(x):
    return -_softplus(-x)


def _rms(x, g):
    r = lax.rsqrt(jnp.mean(x * x, axis=-1, keepdims=True) + EPS)
    return x * r * g


def _rms_bwd(x, g, dy):
    r = lax.rsqrt(jnp.mean(x * x, axis=-1, keepdims=True) + EPS)
    xh = x * r
    dg = jnp.sum(dy * xh, axis=0, keepdims=True)
    dyg = dy * g
    dx = r * (dyg - xh * jnp.mean(dyg * xh, axis=-1, keepdims=True))
    return dx, dg


def _pair_stat(t, m0):
    s0 = jnp.sum(jnp.where(m0, t, 0.0), axis=-1, keepdims=True)
    s1 = jnp.sum(jnp.where(m0, 0.0, t), axis=-1, keepdims=True)
    return jnp.where(m0, s0, s1)


def _rms_pair(x, g, m0):
    r = lax.rsqrt(_pair_stat(x * x, m0) * (1.0 / FOX_HEAD_DIM) + EPS)
    return x * r * g


def _rms_pair_bwd(x, g, dy, m0):
    r = lax.rsqrt(_pair_stat(x * x, m0) * (1.0 / FOX_HEAD_DIM) + EPS)
    xh = x * r
    dg = jnp.sum(dy * xh, axis=0, keepdims=True)
    dyg = dy * g
    dx = r * (dyg - xh * (_pair_stat(dyg * xh, m0) * (1.0 / FOX_HEAD_DIM)))
    return dx, dg


@jax.custom_vjp
def _mm_nn(a, b):
    return _dot(a, b, NN)


_mm_nn.defvjp(lambda a, b: (_dot(a, b, NN), (a, b)),
              lambda r, g: (_dot(g, r[1], NT), _dot(r[0], g, TN)))


@jax.custom_vjp
def _mm_nt(a, b):
    return _dot(a, b, NT)


_mm_nt.defvjp(lambda a, b: (_dot(a, b, NT), (a, b)),
              lambda r, g: (_dot(g, r[1], NN), _dot(g, r[0], TN)))


@jax.custom_vjp
def _mm_tn(a, b):
    return _dot(a, b, TN)


_mm_tn.defvjp(lambda a, b: (_dot(a, b, TN), (a, b)),
              lambda r, g: (_dot(r[1], g, NT), _dot(r[0], g, NN)))


def _dot3(a, b, dims):
    ah = a.astype(BF16)
    al = (a - ah.astype(F32)).astype(BF16)
    bh = b.astype(BF16)
    bl = (b - bh.astype(F32)).astype(BF16)
    d = functools.partial(lax.dot_general, dimension_numbers=dims, preferred_element_type=F32)
    return d(ah, bh) + d(ah, bl) + d(al, bh)


def _neumann_inverses(mats):
    c = mats[0].shape[0]
    eye = (lax.broadcasted_iota(jnp.int32, (c, c), 0) == lax.broadcasted_iota(jnp.int32, (c, c), 1)).astype(F32)
    xs = [eye - a for a in mats]
    ps = list(mats)
    k = 2
    while k < c + 1:
        ps = [_dot3(p, p, NN) for p in ps]
        xs = [x + _dot3(x, p, NN) for x, p in zip(xs, ps)]
        k *= 2
    return xs


@jax.custom_vjp
def _unit_lower_inverses(mats):
    return _neumann_inverses(mats)


def _unit_lower_inverses_fwd(mats):
    ts = _neumann_inverses(mats)
    return ts, ts


def _unit_lower_inverses_bwd(ts, gs):
    left = [_dot3(t, g, TN) for t, g in zip(ts, gs)]
    return ([-_dot3(m, t, NT) for m, t in zip(left, ts)],)


_unit_lower_inverses.defvjp(_unit_lower_inverses_fwd, _unit_lower_inverses_bwd)


def _wgrad(a, b, name, bk=1024, bn=1024, bt=1024, column_blocks=None):
    t_len, k_len = a.shape
    n_len = b.shape[1]
    bk, bn, bt = min(bk, k_len), min(bn, n_len), min(bt, t_len)
    nt = t_len // bt

    def body(a_ref, b_ref, o_ref, acc_ref):
        t = pl.program_id(2)

        @pl.when(t == 0)
        def _():
            acc_ref[...] = jnp.zeros_like(acc_ref)

        acc_ref[...] += _dot(a_ref[...], b_ref[...], TN)

        @pl.when(t == nt - 1)
        def _():
            if column_blocks:
                for jj in range(bn // column_blocks):
                    o_ref[jj] = acc_ref[:, jj * column_blocks:(jj + 1) * column_blocks]
            else:
                o_ref[...] = acc_ref[...]

    if column_blocks:
        out_spec = pl.BlockSpec((bn // column_blocks, bk, column_blocks), lambda i, j, t: (j, i, 0))
        out_shape = jax.ShapeDtypeStruct((n_len // column_blocks, k_len, column_blocks), F32)
    else:
        out_spec = pl.BlockSpec((bk, bn), lambda i, j, t: (i, j))
        out_shape = jax.ShapeDtypeStruct((k_len, n_len), F32)
    return pl.pallas_call(
        body, name=name, grid=(k_len // bk, n_len // bn, nt),
        in_specs=[pl.BlockSpec((bt, bk), lambda i, j, t: (t, i)), pl.BlockSpec((bt, bn), lambda i, j, t: (t, j))],
        out_specs=out_spec, out_shape=out_shape,
        scratch_shapes=[pltpu.VMEM((bk, bn), F32)],
        compiler_params=_cparams(("parallel", "parallel", "arbitrary")),
    )(a, b)


def _wgrad_stacked(pieces, b, name, bn=512, bt=1024):
    t_len, n_len = b.shape
    n_p = len(pieces)
    starts = [int(s) for s in np.cumsum([0] + [p.shape[1] for p in pieces])]
    bn, bt = min(bn, n_len), min(bt, t_len)
    nt = t_len // bt

    def body(*refs):
        b_ref, o_ref, acc_ref = refs[n_p:]
        t = pl.program_id(1)

        @pl.when(t == 0)
        def _():
            acc_ref[...] = jnp.zeros_like(acc_ref)

        for k in range(n_p):
            acc_ref[starts[k]:starts[k + 1], :] += _dot(refs[k][...], b_ref[...], TN)

        @pl.when(t == nt - 1)
        def _():
            o_ref[...] = acc_ref[...]

    return pl.pallas_call(
        body, name=name, grid=(n_len // bn, nt),
        in_specs=[pl.BlockSpec((bt, p.shape[1]), lambda j, t: (t, 0)) for p in pieces] + [pl.BlockSpec((bt, bn), lambda j, t: (t, j))],
        out_specs=pl.BlockSpec((starts[-1], bn), lambda j, t: (0, j)),
        out_shape=jax.ShapeDtypeStruct((starts[-1], n_len), F32),
        scratch_shapes=[pltpu.VMEM((starts[-1], bn), F32)],
        compiler_params=_cparams(("parallel", "arbitrary")),
    )(*pieces, b)


def _rows_matmul(a, b, name, bt=512):
    r_len, t_len = a.shape
    n_len = b.shape[1]
    bt = min(bt, t_len)
    nt = t_len // bt

    def body(a_ref, b_ref, o_ref):
        t = pl.program_id(0)

        @pl.when(t == 0)
        def _():
            o_ref[...] = jnp.zeros_like(o_ref)

        o_ref[...] += _dot(a_ref[...], b_ref[...], NN)

    return pl.pallas_call(
        body, name=name, grid=(nt,),
        in_specs=[pl.BlockSpec((r_len, bt), lambda t: (0, t)), pl.BlockSpec((bt, n_len), lambda t: (t, 0))],
        out_specs=pl.BlockSpec((r_len, n_len), lambda t: (0, 0)),
        out_shape=jax.ShapeDtypeStruct((r_len, n_len), F32),
        compiler_params=_cparams(("arbitrary",)),
    )(a, b)


def _in_proj(x, g, wp, wst, tm=512):
    t_len, d = x.shape
    tm = min(tm, t_len)

    def body(x_ref, g_ref, wp_ref, wst_ref, h_ref, fox_ref, gdn_ref, z_ref, sm_ref, smt_ref):
        h = _rms(x_ref[...], g_ref[...]).astype(BF16)
        h_ref[...] = h
        p = _dot(h, wp_ref[...], NT)
        fox_ref[...] = p[:, P_FOX:P_GDN]
        gdn_ref[...] = p[:, P_GDN:P_Z]
        z_ref[...] = p[:, P_Z:P_SMALL]
        sm_ref[...] = p[:, P_SMALL:P_DIM]
        smt_ref[...] = _dot(wst_ref[...], h, NT)

    row = lambda i: (i, 0)
    fixed = lambda i: (0, 0)
    return pl.pallas_call(
        body, name="in_proj", grid=(t_len // tm,),
        in_specs=[pl.BlockSpec((tm, d), row), pl.BlockSpec((1, d), fixed), _resident((P_DIM, d)),
                  pl.BlockSpec((SM_ROWS, d), fixed)],
        out_specs=[pl.BlockSpec((tm, d), row), pl.BlockSpec((tm, 1536), row), pl.BlockSpec((tm, 1536), row),
                   pl.BlockSpec((tm, 512), row), pl.BlockSpec((tm, LANES), row), pl.BlockSpec((SM_ROWS, tm), lambda i: (0, i))],
        out_shape=[jax.ShapeDtypeStruct((t_len, d), BF16), jax.ShapeDtypeStruct((t_len, 1536), F32),
                   jax.ShapeDtypeStruct((t_len, 1536), F32), jax.ShapeDtypeStruct((t_len, 512), F32),
                   jax.ShapeDtypeStruct((t_len, LANES), F32), jax.ShapeDtypeStruct((SM_ROWS, t_len), F32)],
        compiler_params=_cparams(("parallel",)),
    )(x, g, wp, wst)


def _in_proj_bwd(dprojs, dsmt, x, g, wp, wst, dx1, tm=512):
    t_len, d = x.shape
    tm = min(tm, t_len)
    n_p = len(dprojs)
    starts = np.cumsum([0] + [p.shape[1] for p in dprojs])

    def body(*refs):
        dp_refs = refs[:n_p]
        dst_ref, x_ref, g_ref, wp_ref, wst_ref, dx1_ref, dx_ref, dg_ref = refs[n_p:]
        i = pl.program_id(0)
        dh = _dot(dst_ref[...], wst_ref[...], TN)
        for k in range(n_p):
            dh = dh + _dot(dp_refs[k][...], wp_ref[int(starts[k]):int(starts[k + 1]), :], NN)
        dxn, dg = _rms_bwd(x_ref[...], g_ref[...], dh)
        dx_ref[...] = dx1_ref[...] + dxn

        @pl.when(i == 0)
        def _():
            dg_ref[...] = jnp.zeros_like(dg_ref)

        dg_ref[...] += dg

    row = lambda i: (i, 0)
    fixed = lambda i: (0, 0)
    return pl.pallas_call(
        body, name="in_proj_bwd", grid=(t_len // tm,),
        in_specs=[pl.BlockSpec((tm, p.shape[1]), row) for p in dprojs] + [
            pl.BlockSpec((SM_ROWS, tm), lambda i: (0, i)), pl.BlockSpec((tm, d), row),
            pl.BlockSpec((1, d), fixed), _resident((P_DIM, d)), pl.BlockSpec((SM_ROWS, d), fixed),
            pl.BlockSpec((tm, d), row)],
        out_specs=[pl.BlockSpec((tm, d), row), pl.BlockSpec((1, d), fixed)],
        out_shape=[jax.ShapeDtypeStruct((t_len, d), F32), jax.ShapeDtypeStruct((1, d), F32)],
        compiler_params=_cparams(("arbitrary",)),
    )(*dprojs, dsmt, x, g, wp, wst, dx1)


def _fox_cum(smt, bias_col, n_batch, s_len, ck=256):
    ck = min(ck, s_len)

    def body(s_ref, b_ref, c_ref):
        tri = (lax.broadcasted_iota(jnp.int32, (ck, ck), 0) <= lax.broadcasted_iota(jnp.int32, (ck, ck), 1)).astype(F32)
        carry = jnp.zeros((SM_ROWS, 1), F32)
        for r in range(s_len // ck):
            ls = _log_sigmoid(s_ref[:, r * ck:(r + 1) * ck] + b_ref[...])
            c = jnp.dot(ls, tri, precision=lax.Precision.HIGHEST, preferred_element_type=F32) + carry
            c_ref[:, r * ck:(r + 1) * ck] = c
            carry = c[:, ck - 1:ck]

    return pl.pallas_call(
        body, name="fox_cum", grid=(n_batch,),
        in_specs=[pl.BlockSpec((SM_ROWS, s_len), lambda b: (0, b)), pl.BlockSpec((SM_ROWS, 1), lambda b: (0, 0))],
        out_specs=pl.BlockSpec((SM_ROWS, s_len), lambda b: (0, b)),
        out_shape=jax.ShapeDtypeStruct(smt.shape, F32),
        compiler_params=_cparams(("parallel",)),
    )(smt, bias_col)


def _fox_cum_bwd(dc, smt, bias_col, n_batch, s_len, ck=256):
    ck = min(ck, s_len)
    nr = s_len // ck

    def body(dc_ref, s_ref, b_ref, dl_ref, db_ref):
        b = pl.program_id(0)
        tri = (lax.broadcasted_iota(jnp.int32, (ck, ck), 0) >= lax.broadcasted_iota(jnp.int32, (ck, ck), 1)).astype(F32)
        carry = jnp.zeros((SM_ROWS, 1), F32)
        tot = jnp.zeros((SM_ROWS, 1), F32)
        for r in reversed(range(nr)):
            sl = slice(r * ck, (r + 1) * ck)
            dls = jnp.dot(dc_ref[:, sl], tri, precision=lax.Precision.HIGHEST, preferred_element_type=F32) + carry
            carry = dls[:, 0:1]
            dl = dls * (1.0 - _sigmoid(s_ref[:, sl] + b_ref[...]))
            dl_ref[:, sl] = dl
            tot = tot + jnp.sum(dl, axis=1, keepdims=True)

        @pl.when(b == 0)
        def _():
            db_ref[...] = jnp.zeros_like(db_ref)

        db_ref[...] += jnp.broadcast_to(tot, db_ref.shape)

    return pl.pallas_call(
        body, name="fox_cum_bwd", grid=(n_batch,),
        in_specs=[pl.BlockSpec((SM_ROWS, s_len), lambda b: (0, b)), pl.BlockSpec((SM_ROWS, s_len), lambda b: (0, b)),
                  pl.BlockSpec((SM_ROWS, 1), lambda b: (0, 0))],
        out_specs=[pl.BlockSpec((SM_ROWS, s_len), lambda b: (0, b)), pl.BlockSpec((SM_ROWS, LANES), lambda b: (0, 0))],
        out_shape=[jax.ShapeDtypeStruct(smt.shape, F32), jax.ShapeDtypeStruct((SM_ROWS, LANES), F32)],
        compiler_params=_cparams(("arbitrary",)),
    )(dc, smt, bias_col)


def _fox_diagonal_mask(tq):
    return lax.broadcasted_iota(jnp.int32, (tq, tq), 1) <= lax.broadcasted_iota(jnp.int32, (tq, tq), 0)


def _fox_fwd(pf, cb, gq2, gk2, go2, tq=256):
    n_batch, s_len, _ = pf.shape
    tq = min(tq, s_len)
    nq = s_len // tq
    scale = FOX_HEAD_DIM ** -0.5

    def body(q_ref, k_ref, v_ref, c_ref, gq_ref, gk_ref, go_ref, o_ref, on_ref, lse_ref, kh_ref, vh_ref):
        j = pl.program_id(1)
        i = pl.program_id(2)
        m0 = lax.broadcasted_iota(jnp.int32, (1, LANES), 1) < FOX_HEAD_DIM

        @pl.when(i == 0)
        def _():
            kn = _rms_pair(k_ref[0], gk_ref[...], m0)
            kh_ref[0] = jnp.where(m0, kn, 0.0).astype(BF16)
            kh_ref[1] = jnp.where(m0, 0.0, kn).astype(BF16)
            v = v_ref[0]
            vh_ref[0] = jnp.where(m0, v, 0.0).astype(BF16)
            vh_ref[1] = jnp.where(m0, 0.0, v).astype(BF16)

        qb = (_rms_pair(q_ref[0], gq_ref[...], m0) * scale).astype(BF16)

        def step(kb, carry, diagonal=False):
            ms, ls, acc = carry
            off = pl.multiple_of(kb * tq, tq)
            new_m, new_l, alphas, pv = [], [], [], []
            for hh in range(2):
                s = _dot(qb, kh_ref[hh, pl.ds(off, tq), :], NT)
                s = s - c_ref[0, kb, pl.ds(2 * j + hh, 1), :]
                if diagonal:
                    s = jnp.where(_fox_diagonal_mask(tq), s, NEG_INF)
                m_new = jnp.maximum(ms[hh], jnp.max(s, axis=-1, keepdims=True))
                alpha = jnp.exp(ms[hh] - m_new)
                p = jnp.exp(s - m_new)
                new_l.append(alpha * ls[hh] + jnp.sum(p, axis=-1, keepdims=True))
                new_m.append(m_new)
                alphas.append(alpha)
                pv.append(_dot(p, vh_ref[hh, pl.ds(off, tq), :], NN))
            acc = jnp.where(m0, alphas[0], alphas[1]) * acc + pv[0] + pv[1]
            return tuple(new_m), tuple(new_l), acc

        init_m = (jnp.full((tq, 1), NEG_INF, F32),) * 2
        init_l = (jnp.zeros((tq, 1), F32),) * 2
        carry = lax.fori_loop(0, i, step, (init_m, init_l, jnp.zeros((tq, LANES), F32)))
        ms, ls, acc = step(i, carry, diagonal=True)
        o = acc / jnp.where(m0, ls[0], ls[1])
        o_ref[0] = o
        on_ref[0] = _rms_pair(o, go_ref[...], m0).astype(BF16)
        lse_ref[0] = jnp.where(m0, ms[0] + jnp.log(ls[0]), ms[1] + jnp.log(ls[1]))

    fixed = lambda b, j, i: (0, 0)
    tile = lambda b, j, i: (b, i, j)
    return pl.pallas_call(
        body, name="fox_fwd", grid=(n_batch, 4, nq),
        in_specs=[pl.BlockSpec((1, tq, LANES), tile), pl.BlockSpec((1, s_len, LANES), lambda b, j, i: (b, 0, 4 + j)),
                  pl.BlockSpec((1, s_len, LANES), lambda b, j, i: (b, 0, 8 + j)),
                  pl.BlockSpec((1, nq, SM_ROWS, tq), lambda b, j, i: (b, 0, 0, 0)),
                  pl.BlockSpec((1, LANES), fixed), pl.BlockSpec((1, LANES), fixed), pl.BlockSpec((1, LANES), fixed)],
        out_specs=[pl.BlockSpec((1, tq, LANES), tile), pl.BlockSpec((1, tq, LANES), tile), pl.BlockSpec((1, tq, LANES), tile)],
        out_shape=[jax.ShapeDtypeStruct((n_batch, s_len, FOX_WIDTH), F32), jax.ShapeDtypeStruct((n_batch, s_len, FOX_WIDTH), BF16),
                   jax.ShapeDtypeStruct((n_batch, s_len, FOX_WIDTH), F32)],
        scratch_shapes=[pltpu.VMEM((2, s_len, LANES), BF16), pltpu.VMEM((2, s_len, LANES), BF16)],
        compiler_params=_cparams(("parallel", "parallel", "arbitrary")),
    )(pf, pf, pf, cb, gq2, gk2, go2)


def _fox_bwd(pf, cb, gq2, gk2, go2, o, lse, don, tq=256):
    n_batch, s_len, _ = pf.shape
    tq = min(tq, s_len)
    nq = s_len // tq
    scale = FOX_HEAD_DIM ** -0.5

    def body(q_ref, k_ref, v_ref, c_ref, gq_ref, gk_ref, go_ref, o_ref, lse_ref, don_ref,
             dq_ref, dk_ref, dv_ref, dc_ref, dgq_ref, dgk_ref, dgo_ref, kh_ref, vh_ref, dka_ref, dva_ref, dca_ref):
        b = pl.program_id(0)
        j = pl.program_id(1)
        i = pl.program_id(2)
        m0 = lax.broadcasted_iota(jnp.int32, (1, LANES), 1) < FOX_HEAD_DIM

        @pl.when((b == 0) & (j == 0) & (i == 0))
        def _():
            dgq_ref[...] = jnp.zeros_like(dgq_ref)
            dgk_ref[...] = jnp.zeros_like(dgk_ref)
            dgo_ref[...] = jnp.zeros_like(dgo_ref)

        @pl.when(i == 0)
        def _():
            kn = _rms_pair(k_ref[0], gk_ref[...], m0)
            kh_ref[0] = jnp.where(m0, kn, 0.0).astype(BF16)
            kh_ref[1] = jnp.where(m0, 0.0, kn).astype(BF16)
            v = v_ref[0]
            vh_ref[0] = jnp.where(m0, v, 0.0).astype(BF16)
            vh_ref[1] = jnp.where(m0, 0.0, v).astype(BF16)
            dka_ref[...] = jnp.zeros_like(dka_ref)
            dva_ref[...] = jnp.zeros_like(dva_ref)
            dca_ref[...] = jnp.zeros_like(dca_ref)

        q = q_ref[0]
        qn = _rms_pair(q, gq_ref[...], m0)
        qs = qn * scale
        qb = qs.astype(BF16)
        qh = (jnp.where(m0, qs, 0.0).astype(BF16), jnp.where(m0, 0.0, qs).astype(BF16))
        ot = o_ref[0]
        do, dgo = _rms_pair_bwd(ot, go_ref[...], don_ref[0], m0)
        dgo_ref[...] += dgo
        dd = do * ot
        delta = (jnp.sum(jnp.where(m0, dd, 0.0), axis=-1, keepdims=True), jnp.sum(jnp.where(m0, 0.0, dd), axis=-1, keepdims=True))
        doh = (jnp.where(m0, do, 0.0).astype(BF16), jnp.where(m0, 0.0, do).astype(BF16))
        lse_t = lse_ref[0]
        lse_h = (lse_t[:, 0:1], lse_t[:, FOX_HEAD_DIM:FOX_HEAD_DIM + 1])

        def step(kb, carry, diagonal=False):
            dqn, rs = carry
            rs = list(rs)
            off = pl.multiple_of(kb * tq, tq)
            for hh in range(2):
                kblk = kh_ref[hh, pl.ds(off, tq), :]
                vblk = vh_ref[hh, pl.ds(off, tq), :]
                s = _dot(qb, kblk, NT)
                s = s - c_ref[0, kb, pl.ds(2 * j + hh, 1), :]
                if diagonal:
                    s = jnp.where(_fox_diagonal_mask(tq), s, NEG_INF)
                p = jnp.exp(s - lse_h[hh])
                dp = _dot(doh[hh], vblk, NT)
                ds = p * (dp - delta[hh])
                dva_ref[pl.ds(off, tq), :] += _dot(p, doh[hh], TN)
                dka_ref[pl.ds(off, tq), :] += _dot(ds, qh[hh], TN)
                dca_ref[kb, hh:hh + 1, :] += -jnp.sum(ds, axis=0, keepdims=True)
                rs[hh] = rs[hh] + jnp.sum(ds, axis=-1, keepdims=True)
                dqn = dqn + _dot(ds, kblk, NN)
            return dqn, tuple(rs)

        carry = lax.fori_loop(0, i, step, (jnp.zeros((tq, LANES), F32), (jnp.zeros((tq, 1), F32),) * 2))
        dqn, rs = step(i, carry, diagonal=True)
        dqn = dqn * scale
        rs_rows = jnp.where(m0, rs[0], rs[1]).T
        dca_ref[i, 0:1, :] += rs_rows[0:1, :]
        dca_ref[i, 1:2, :] += rs_rows[FOX_HEAD_DIM:FOX_HEAD_DIM + 1, :]
        dq, dgq = _rms_pair_bwd(q, gq_ref[...], dqn, m0)
        dq_ref[0] = dq.astype(BF16)
        dgq_ref[...] += dgq

        @pl.when(i == nq - 1)
        def _():
            dk, dgk = _rms_pair_bwd(k_ref[0], gk_ref[...], dka_ref[...], m0)
            dk_ref[0] = dk.astype(BF16)
            dgk_ref[...] += dgk
            dv_ref[0] = dva_ref[...].astype(BF16)
            dc_ref[0, 0] = dca_ref[...]

    fixed = lambda b, j, i: (0, 0)
    tile = lambda b, j, i: (b, i, j)
    full = lambda b, j, i: (b, 0, j)
    wide = jax.ShapeDtypeStruct((n_batch, s_len, FOX_WIDTH), BF16)
    gain = jax.ShapeDtypeStruct((1, LANES), F32)
    return pl.pallas_call(
        body, name="fox_bwd", grid=(n_batch, 4, nq),
        in_specs=[pl.BlockSpec((1, tq, LANES), tile), pl.BlockSpec((1, s_len, LANES), lambda b, j, i: (b, 0, 4 + j)),
                  pl.BlockSpec((1, s_len, LANES), lambda b, j, i: (b, 0, 8 + j)),
                  pl.BlockSpec((1, nq, SM_ROWS, tq), lambda b, j, i: (b, 0, 0, 0)),
                  pl.BlockSpec((1, LANES), fixed), pl.BlockSpec((1, LANES), fixed), pl.BlockSpec((1, LANES), fixed),
                  pl.BlockSpec((1, tq, LANES), tile), pl.BlockSpec((1, tq, LANES), tile), pl.BlockSpec((1, tq, LANES), tile)],
        out_specs=[pl.BlockSpec((1, tq, LANES), tile), pl.BlockSpec((1, s_len, LANES), full), pl.BlockSpec((1, s_len, LANES), full),
                   pl.BlockSpec((1, 1, nq, 8, tq), lambda b, j, i: (b, j, 0, 0, 0)),
                   pl.BlockSpec((1, LANES), fixed), pl.BlockSpec((1, LANES), fixed), pl.BlockSpec((1, LANES), fixed)],
        out_shape=[wide, wide, wide, jax.ShapeDtypeStruct((n_batch, 4, nq, 8, tq), F32), gain, gain, gain],
        scratch_shapes=[pltpu.VMEM((2, s_len, LANES), BF16), pltpu.VMEM((2, s_len, LANES), BF16),
                        pltpu.VMEM((s_len, LANES), F32), pltpu.VMEM((s_len, LANES), F32), pltpu.VMEM((nq, 8, tq), F32)],
        compiler_params=_cparams(("arbitrary", "arbitrary", "arbitrary")),
    )(pf, pf, pf, cb, gq2, gk2, go2, o, lse, don)


def _conv_padded(x_ref, w, pad_ref, s_len):
    pad_ref[0:8, :] = jnp.zeros((8, pad_ref.shape[1]), F32)
    pad_ref[8:8 + s_len, :] = x_ref[0]
    return (w[3:4] * pad_ref[8:8 + s_len, :] + w[2:3] * pad_ref[7:7 + s_len, :] + w[1:2] * pad_ref[6:6 + s_len, :]
            + w[0:1] * pad_ref[5:5 + s_len, :])


def _gdn_pre(pg, conv_w):
    n_batch, s_len, width = pg.shape
    bw = GDN_WIDTH
    hd = GDN_HEAD_DIM

    def body(x_ref, w_ref, o_ref, pad_ref):
        part = pl.program_id(1)
        y = _conv_padded(x_ref, w_ref[...], pad_ref, s_len)
        s = y * _sigmoid(y)
        for h in range(GDN_HEADS):
            sh = s[:, h * hd:(h + 1) * hd]
            sn = sh * lax.rsqrt(jnp.sum(sh * sh, axis=-1, keepdims=True) + EPS)
            o_ref[0, :, h * hd:(h + 1) * hd] = jnp.where(part < 2, sn, sh)

    return pl.pallas_call(
        body, name="gdn_pre", grid=(n_batch, width // bw),
        in_specs=[pl.BlockSpec((1, s_len, bw), lambda b, c: (b, 0, c)), pl.BlockSpec((8, bw), lambda b, c: (0, c))],
        out_specs=pl.BlockSpec((1, s_len, bw), lambda b, c: (b, 0, c)),
        out_shape=jax.ShapeDtypeStruct(pg.shape, F32),
        scratch_shapes=[pltpu.VMEM((s_len + 8, bw), F32)],
        compiler_params=_cparams(("parallel", "parallel")),
    )(pg, conv_w)


def _gdn_pre_bwd(pg, conv_w, dout):
    n_batch, s_len, width = pg.shape
    ncb = width // LANES

    steps = ncb * n_batch
    ring = 3

    def body(x_hbm, w_ref, d_hbm, dx_ref, dw_ref, xb_ref, db_ref, sems, pad_ref, tail_ref):
        cb = pl.program_id(0)
        b = pl.program_id(1)
        at = cb * n_batch + b

        def copies(t):
            slot = lax.rem(t, ring)
            col = pl.multiple_of(lax.div(t, n_batch) * LANES, LANES)
            row = lax.rem(t, n_batch)
            return (pltpu.make_async_copy(x_hbm.at[row, :, pl.ds(col, LANES)], xb_ref.at[slot, 0], sems.at[slot]),
                    pltpu.make_async_copy(d_hbm.at[row, :, pl.ds(col, LANES)], db_ref.at[slot, 0], sems.at[ring + slot]))

        @pl.when(at == 0)
        def _():
            for t in range(ring - 1):
                for c in copies(jnp.int32(t)):
                    c.start()

        @pl.when(at + ring - 1 < steps)
        def _():
            for c in copies(at + ring - 1):
                c.start()

        for c in copies(at):
            c.wait()
        x_ref = xb_ref.at[lax.rem(at, ring)]
        d_ref = db_ref.at[lax.rem(at, ring)]
        x = x_ref[0]
        w = w_ref[...]
        d = d_ref[0]
        y = _conv_padded(x_ref, w, pad_ref, s_len)
        sig = _sigmoid(y)
        s = y * sig
        rr = lax.rsqrt(jnp.sum(s * s, axis=-1, keepdims=True) + EPS)
        sn = s * rr
        ds_n = rr * (d - sn * jnp.sum(d * sn, axis=-1, keepdims=True))
        ds = jnp.where(cb < 2 * GDN_HEADS, ds_n, d)
        dy = ds * (sig * (1.0 + y * (1.0 - sig)))
        tail_ref[0:s_len, :] = dy
        tail_ref[s_len:s_len + 8, :] = jnp.zeros((8, LANES), F32)
        dyu = [tail_ref[3 - jj:3 - jj + s_len, :] for jj in range(CONV_WIDTH)]
        dx = w[0:1] * dyu[0] + w[1:2] * dyu[1] + w[2:3] * dyu[2] + w[3:4] * dyu[3]
        dx_ref[0] = dx.astype(BF16)
        dw = [jnp.sum(dyu[jj] * x, axis=0, keepdims=True) for jj in range(CONV_WIDTH)]
        rows = lax.broadcasted_iota(jnp.int32, (8, LANES), 0)
        dwb = jnp.zeros((8, LANES), F32)
        for jj in range(CONV_WIDTH):
            dwb = dwb + jnp.where(rows == jj, dw[jj], 0.0)

        @pl.when(b == 0)
        def _():
            dw_ref[...] = jnp.zeros_like(dw_ref)

        dw_ref[...] += dwb

    blk = lambda c, b: (b, 0, c)
    anywhere = pl.BlockSpec(memory_space=pl.ANY)
    return pl.pallas_call(
        body, name="gdn_pre_bwd", grid=(ncb, n_batch),
        in_specs=[anywhere, pl.BlockSpec((8, LANES), lambda c, b: (0, c)), anywhere],
        out_specs=[pl.BlockSpec((1, s_len, LANES), blk), pl.BlockSpec((8, LANES), lambda c, b: (0, c))],
        out_shape=[jax.ShapeDtypeStruct(pg.shape, BF16), jax.ShapeDtypeStruct((8, width), F32)],
        scratch_shapes=[pltpu.VMEM((ring, 1, s_len, LANES), pg.dtype), pltpu.VMEM((ring, 1, s_len, LANES), dout.dtype),
                        pltpu.SemaphoreType.DMA((2 * ring,)),
                        pltpu.VMEM((s_len + 8, LANES), F32), pltpu.VMEM((s_len + 8, LANES), F32)],
        compiler_params=_cparams(("arbitrary", "arbitrary")),
    )(pg, conv_w, dout)


def _gdn_gates(smc, smr, a_c, dt_c, a_r, dt_r, h):
    lane = lax.broadcasted_iota(jnp.int32, (1, LANES), 1)
    sub = lax.broadcasted_iota(jnp.int32, (SM_ROWS, 1), 0)
    beta_c = jnp.sum(jnp.where(lane == SM_B + h, _sigmoid(smc), 0.0), axis=1, keepdims=True)
    g_all_c = -jnp.exp(a_c) * _softplus(smc + dt_c)
    g_c = jnp.sum(jnp.where(lane == SM_A + h, g_all_c, 0.0), axis=1, keepdims=True)
    g_all_r = -jnp.exp(a_r) * _softplus(smr + dt_r)
    g_r = jnp.sum(jnp.where(sub == SM_A + h, g_all_r, 0.0), axis=0, keepdims=True)
    return beta_c, g_c, g_r


@jax.custom_vjp
def _known_inverse(a, t):
    return t


_known_inverse.defvjp(lambda a, t: (t, t),
                      lambda t, g: (-_dot3(_dot3(t, g, TN), t, NT), jnp.zeros_like(t)))


def _gdn_group(qkv, z, smc, smr, a_c, dt_c, a_r, dt_r, go, states, inverses=None):
    n_grp = len(qkv)
    c = qkv[0].shape[0]
    hd = GDN_HEAD_DIM
    pairs = [(g, h) for g in range(n_grp) for h in range(GDN_HEADS)]
    ii = lax.broadcasted_iota(jnp.int32, (c, c), 0)
    jj = lax.broadcasted_iota(jnp.int32, (c, c), 1)
    incl = ii >= jj
    col = lambda arr, base, h: arr[:, base + h * hd:base + (h + 1) * hd]

    qs, ks, kbs, vbs, gcs, g_lasts, amats, intras = [], [], [], [], [], [], [], []
    for g, h in pairs:
        beta_c, g_c, g_r = _gdn_gates(smc[g], smr[g], a_c, dt_c, a_r, dt_r, h)
        gc_c = jnp.sum(jnp.where(incl, g_r, 0.0), axis=1, keepdims=True)
        gc_r = jnp.sum(jnp.where(ii <= jj, g_c, 0.0), axis=0, keepdims=True)
        decay = jnp.where(incl, jnp.exp(jnp.where(incl, gc_c - gc_r, 0.0)), 0.0)
        k = col(qkv[g], GDN_WIDTH, h)
        kb = k * beta_c
        qs.append(col(qkv[g], 0, h) * (hd ** -0.5))
        ks.append(k)
        kbs.append(kb)
        vbs.append(col(qkv[g], 2 * GDN_WIDTH, h) * beta_c)
        gcs.append(gc_c)
        g_lasts.append(jnp.sum(g_c, axis=0, keepdims=True))
        both = _mm_nt(jnp.concatenate([kb, qs[-1]], axis=0), k)
        amats.append(jnp.where(ii > jj, both[0:c] * decay, 0.0))
        intras.append(both[c:2 * c] * decay)
    ts = _unit_lower_inverses(amats) if inverses is None else [_known_inverse(a, t) for a, t in zip(amats, inverses)]
    egcs = [jnp.exp(gc) for gc in gcs]
    uws = [_mm_nn(t, jnp.concatenate([vb, kb * e], axis=1)) for t, vb, kb, e in zip(ts, vbs, kbs, egcs)]
    us = [uw[:, 0:hd] for uw in uws]
    ws = [uw[:, hd:2 * hd] for uw in uws]
    qes = [q * e for q, e in zip(qs, egcs)]
    kds = [k * jnp.exp(gl - gc) for k, gl, gc in zip(ks, g_lasts, gcs)]
    sdecs = [jnp.exp(gl) for gl in g_lasts]

    outs = []
    for g in range(n_grp):
        idx = [g * GDN_HEADS + h for h in range(GDN_HEADS)]
        v_new = [us[i] - _mm_nn(ws[i], states[h]) for h, i in enumerate(idx)]
        o = [_mm_nn(jnp.concatenate([qes[i], intras[i]], axis=1), jnp.concatenate([states[h], v_new[h]], axis=0))
             for h, i in enumerate(idx)]
        states = [states[h] * sdecs[i] + _mm_tn(kds[i], v_new[h]) for h, i in enumerate(idx)]
        outs.append([_rms(o[h], go) * (col(z[g], 0, h) * _sigmoid(col(z[g], 0, h))) for h in range(GDN_HEADS)])
    return outs, states, ts


def _gdn_group_size(n_chunks):
    return GDN_GROUP if n_chunks % GDN_GROUP == 0 else 1


def _gdn_fwd(qkvn, z, smc, smr, a_c, dt_c, a_r, dt_r, go):
    n_batch, s_len, _ = qkvn.shape
    c = GDN_CHUNK
    n = s_len // c
    grp = _gdn_group_size(n)
    ng = n // grp
    gc = grp * c
    hd = GDN_HEAD_DIM

    def body(qkv_ref, z_ref, smc_ref, smr_ref, ac_ref, dc_ref, ar_ref, dr_ref, go_ref, og_ref, st_ref, inv_ref, s_ref):
        @pl.when(pl.program_id(1) == 0)
        def _():
            s_ref[...] = jnp.zeros_like(s_ref)

        states = [s_ref[h] for h in range(GDN_HEADS)]
        for h in range(GDN_HEADS):
            st_ref[0, 0, h] = states[h]
        rows = lambda k: slice(k * c, (k + 1) * c)
        outs, nxt, invs = _gdn_group([qkv_ref[0, rows(k), :] for k in range(grp)], [z_ref[0, rows(k), :] for k in range(grp)],
                                     [smc_ref[0, rows(k), :] for k in range(grp)], [smr_ref[k] for k in range(grp)],
                                     ac_ref[...], dc_ref[...], ar_ref[...], dr_ref[...], go_ref[...], states)
        for k in range(grp):
            for h in range(GDN_HEADS):
                og_ref[0, rows(k), h * hd:(h + 1) * hd] = outs[k][h].astype(BF16)
        for p, inv in enumerate(invs):
            inv_ref[0, 0, p] = inv
        for h in range(GDN_HEADS):
            s_ref[h] = nxt[h]

    tok = lambda b, i: (b, i, 0)
    fixed = lambda b, i: (0, 0)
    return pl.pallas_call(
        body, name="gdn_fwd", grid=(n_batch, ng),
        in_specs=[pl.BlockSpec((1, gc, 3 * GDN_WIDTH), tok), pl.BlockSpec((1, gc, GDN_WIDTH), tok), pl.BlockSpec((1, gc, LANES), tok),
                  pl.BlockSpec((grp, SM_ROWS, c), lambda b, i: (b * ng + i, 0, 0)),
                  pl.BlockSpec((1, LANES), fixed), pl.BlockSpec((1, LANES), fixed), pl.BlockSpec((SM_ROWS, 1), fixed),
                  pl.BlockSpec((SM_ROWS, 1), fixed), pl.BlockSpec((1, LANES), fixed)],
        out_specs=[pl.BlockSpec((1, gc, GDN_WIDTH), tok), pl.BlockSpec((1, 1, GDN_HEADS, hd, hd), lambda b, i: (b, i, 0, 0, 0)),
                   pl.BlockSpec((1, 1, grp * GDN_HEADS, c, c), lambda b, i: (b, i, 0, 0, 0))],
        out_shape=[jax.ShapeDtypeStruct((n_batch, s_len, GDN_WIDTH), BF16), jax.ShapeDtypeStruct((n_batch, ng, GDN_HEADS, hd, hd), F32),
                   jax.ShapeDtypeStruct((n_batch, ng, grp * GDN_HEADS, c, c), F32)],
        scratch_shapes=[pltpu.VMEM((GDN_HEADS, hd, hd), F32)],
        compiler_params=_cparams(("parallel", "arbitrary")),
    )(qkvn, z, smc, smr, a_c, dt_c, a_r, dt_r, go)


def _gdn_bwd(qkvn, z, smc, smr, a_c, dt_c, a_r, dt_r, go, states, inverses, dog):
    n_batch, s_len, _ = qkvn.shape
    c = GDN_CHUNK
    n = s_len // c
    grp = _gdn_group_size(n)
    ng = n // grp
    gc = grp * c
    hd = GDN_HEAD_DIM

    def body(qkv_ref, z_ref, smc_ref, smr_ref, ac_ref, dc_ref, ar_ref, dr_ref, go_ref, st_ref, inv_ref, dog_ref,
             dqkv_ref, dz_ref, dsmc_ref, dsmr_ref, dac_ref, ddc_ref, dar_ref, ddr_ref, dgo_ref, ds_ref):
        first = (pl.program_id(0) == 0) & (pl.program_id(1) == 0)

        @pl.when(pl.program_id(1) == 0)
        def _():
            ds_ref[...] = jnp.zeros_like(ds_ref)

        @pl.when(first)
        def _():
            for r in (dac_ref, ddc_ref, dar_ref, ddr_ref, dgo_ref):
                r[...] = jnp.zeros_like(r)

        rows = lambda k: slice(k * c, (k + 1) * c)
        states = [st_ref[0, 0, h] for h in range(GDN_HEADS)]
        prim = ([qkv_ref[0, rows(k), :] for k in range(grp)], [z_ref[0, rows(k), :] for k in range(grp)],
                [smc_ref[0, rows(k), :] for k in range(grp)], [smr_ref[k] for k in range(grp)],
                ac_ref[...], dc_ref[...], ar_ref[...], dr_ref[...], go_ref[...], states)
        invs = [inv_ref[0, 0, p] for p in range(grp * GDN_HEADS)]
        _, vjp = jax.vjp(functools.partial(_gdn_group, inverses=invs), *prim)
        cot = ([[dog_ref[0, rows(k), h * hd:(h + 1) * hd] for h in range(GDN_HEADS)] for k in range(grp)],
               [ds_ref[h] for h in range(GDN_HEADS)], [jnp.zeros((c, c), F32)] * (grp * GDN_HEADS))
        dqkv, dz, dsmc, dsmr, dac, ddc, dar, ddr, dgo, dstates = vjp(cot)
        for k in range(grp):
            dqkv_ref[0, rows(k), :] = dqkv[k]
            dz_ref[0, rows(k), :] = dz[k].astype(BF16)
            dsmc_ref[0, rows(k), :] = dsmc[k]
            dsmr_ref[k] = dsmr[k]
        dac_ref[...] += dac
        ddc_ref[...] += ddc
        dar_ref[...] += dar
        ddr_ref[...] += ddr
        dgo_ref[...] += dgo
        for h in range(GDN_HEADS):
            ds_ref[h] = dstates[h]

    tok = lambda b, i: (b, ng - 1 - i, 0)
    fixed = lambda b, i: (0, 0)
    lane_vec = jax.ShapeDtypeStruct((1, LANES), F32)
    row_vec = jax.ShapeDtypeStruct((SM_ROWS, 1), F32)
    return pl.pallas_call(
        body, name="gdn_bwd", grid=(n_batch, ng),
        in_specs=[pl.BlockSpec((1, gc, 3 * GDN_WIDTH), tok), pl.BlockSpec((1, gc, GDN_WIDTH), tok), pl.BlockSpec((1, gc, LANES), tok),
                  pl.BlockSpec((grp, SM_ROWS, c), lambda b, i: (b * ng + ng - 1 - i, 0, 0)),
                  pl.BlockSpec((1, LANES), fixed), pl.BlockSpec((1, LANES), fixed), pl.BlockSpec((SM_ROWS, 1), fixed),
                  pl.BlockSpec((SM_ROWS, 1), fixed), pl.BlockSpec((1, LANES), fixed),
                  pl.BlockSpec((1, 1, GDN_HEADS, hd, hd), lambda b, i: (b, ng - 1 - i, 0, 0, 0)),
                  pl.BlockSpec((1, 1, grp * GDN_HEADS, c, c), lambda b, i: (b, ng - 1 - i, 0, 0, 0)),
                  pl.BlockSpec((1, gc, GDN_WIDTH), lambda b, i: (b, ng - 1 - i, 1))],
        out_specs=[pl.BlockSpec((1, gc, 3 * GDN_WIDTH), tok), pl.BlockSpec((1, gc, GDN_WIDTH), tok), pl.BlockSpec((1, gc, LANES), tok),
                   pl.BlockSpec((grp, SM_ROWS, c), lambda b, i: (b * ng + ng - 1 - i, 0, 0)),
                   pl.BlockSpec((1, LANES), fixed), pl.BlockSpec((1, LANES), fixed), pl.BlockSpec((SM_ROWS, 1), fixed),
                   pl.BlockSpec((SM_ROWS, 1), fixed), pl.BlockSpec((1, LANES), fixed)],
        out_shape=[jax.ShapeDtypeStruct((n_batch, s_len, 3 * GDN_WIDTH), F32), jax.ShapeDtypeStruct((n_batch, s_len, GDN_WIDTH), BF16),
                   jax.ShapeDtypeStruct((n_batch, s_len, LANES), F32), jax.ShapeDtypeStruct((n_batch * n, SM_ROWS, c), F32),
                   lane_vec, lane_vec, row_vec, row_vec, lane_vec],
        scratch_shapes=[pltpu.VMEM((GDN_HEADS, hd, hd), F32)],
        compiler_params=_cparams(("arbitrary", "arbitrary")),
    )(qkvn, z, smc, smr, a_c, dt_c, a_r, dt_r, go, states, inverses, dog)


def _out_proj(x, oa, ob, w_out, g_x, w_cq, tm=512):
    t_len, d = x.shape
    tm = min(tm, t_len)

    def body(x_ref, oa_ref, ob_ref, wo_ref, g_ref, wq_ref, x1_ref, hq_ref, cq_ref):
        x1 = x_ref[...] + _dot(jnp.concatenate([oa_ref[...], ob_ref[...]], axis=1), wo_ref[...])
        x1_ref[...] = x1
        hq = _rms(x1, g_ref[...]).astype(BF16)
        hq_ref[...] = hq
        cq_ref[...] = _dot(hq, wq_ref[...])

    row = lambda i: (i, 0)
    fixed = lambda i: (0, 0)
    return pl.pallas_call(
        body, name="out_proj", grid=(t_len // tm,),
        in_specs=[pl.BlockSpec((tm, d), row), pl.BlockSpec((tm, FOX_WIDTH), row), pl.BlockSpec((tm, GDN_WIDTH), row),
                  _resident((d, d)), pl.BlockSpec((1, d), fixed), _resident((d, XATTN_WIDTH))],
        out_specs=[pl.BlockSpec((tm, d), row), pl.BlockSpec((tm, d), row), pl.BlockSpec((tm, XATTN_WIDTH), row)],
        out_shape=[jax.ShapeDtypeStruct((t_len, d), F32), jax.ShapeDtypeStruct((t_len, d), BF16), jax.ShapeDtypeStruct((t_len, XATTN_WIDTH), F32)],
        compiler_params=_cparams(("parallel",)),
    )(x, oa, ob, w_out, g_x, w_cq)


def _out_proj_bwd(dx1, w_out, tm=512):
    t_len, d = dx1.shape
    tm = min(tm, t_len)

    def body(dx_ref, w_ref, o_ref):
        o_ref[...] = _dot(dx_ref[...], w_ref[...], NT)

    return pl.pallas_call(
        body, name="out_proj_bwd", grid=(t_len // tm,),
        in_specs=[pl.BlockSpec((tm, d), lambda i: (i, 0)), pl.BlockSpec((d, d), lambda i: (0, 0))],
        out_specs=pl.BlockSpec((tm, d), lambda i: (i, 0)),
        out_shape=jax.ShapeDtypeStruct((t_len, d), F32),
        compiler_params=_cparams(("parallel",)),
    )(dx1, w_out)


def _mem_kv(mem, g, w_ckv, tm=256):
    t_len, d = mem.shape
    tm = min(tm, t_len)

    def body(x_ref, g_ref, w_ref, h_ref, o_ref):
        h = _rms(x_ref[...], g_ref[...]).astype(BF16)
        h_ref[...] = h
        o_ref[...] = _dot(h, w_ref[...])

    row = lambda i: (i, 0)
    fixed = lambda i: (0, 0)
    return pl.pallas_call(
        body, name="mem_kv", grid=(t_len // tm,),
        in_specs=[pl.BlockSpec((tm, d), row), pl.BlockSpec((1, d), fixed), pl.BlockSpec((d, 2 * XATTN_WIDTH), fixed)],
        out_specs=[pl.BlockSpec((tm, d), row), pl.BlockSpec((tm, 2 * XATTN_WIDTH), row)],
        out_shape=[jax.ShapeDtypeStruct((t_len, d), BF16), jax.ShapeDtypeStruct((t_len, 2 * XATTN_WIDTH), F32)],
        compiler_params=_cparams(("parallel",)),
    )(mem, g, w_ckv)


def _mem_kv_bwd(dckv, mem, g, w_ckv, tm=256):
    t_len, d = mem.shape
    tm = min(tm, t_len)

    def body(d_ref, x_ref, g_ref, w_ref, dg_ref):
        @pl.when(pl.program_id(0) == 0)
        def _():
            dg_ref[...] = jnp.zeros_like(dg_ref)

        dh = _dot(d_ref[...], w_ref[...], NT)
        _, dg = _rms_bwd(x_ref[...], g_ref[...], dh)
        dg_ref[...] += dg

    row = lambda i: (i, 0)
    fixed = lambda i: (0, 0)
    return pl.pallas_call(
        body, name="mem_kv_bwd", grid=(t_len // tm,),
        in_specs=[pl.BlockSpec((tm, 2 * XATTN_WIDTH), row), pl.BlockSpec((tm, d), row), pl.BlockSpec((1, d), fixed),
                  pl.BlockSpec((d, 2 * XATTN_WIDTH), fixed)],
        out_specs=pl.BlockSpec((1, d), fixed),
        out_shape=jax.ShapeDtypeStruct((1, d), F32),
        compiler_params=_cparams(("arbitrary",)),
    )(dckv, mem, g, w_ckv)


def _xattn_probs(qn, kn):
    s = _dot(qn, kn, NT) * (XATTN_HEAD_DIM ** -0.5)
    p = jnp.exp(s - jnp.max(s, axis=-1, keepdims=True))
    return p / jnp.sum(p, axis=-1, keepdims=True)


def _xattn_fwd(cq, ckv, x1, gq, gk, w_co, g_mlp, n_batch, s_len, m_len, tq=512):
    d = x1.shape[1]
    tq = min(tq, s_len)
    nq = s_len // tq
    hd = XATTN_HEAD_DIM

    def body(cq_ref, kv_ref, x1_ref, gq_ref, gk_ref, wo_ref, gm_ref, co_ref, x2_ref, hf_ref):
        outs = []
        for h in range(XATTN_HEADS):
            qn = _rms(cq_ref[:, h * hd:(h + 1) * hd], gq_ref[...])
            kn = _rms(kv_ref[:, h * hd:(h + 1) * hd], gk_ref[...])
            p = _xattn_probs(qn, kn)
            outs.append(_dot(p, kv_ref[:, XATTN_WIDTH + h * hd:XATTN_WIDTH + (h + 1) * hd]).astype(BF16))
        for h in range(XATTN_HEADS):
            co_ref[:, h * hd:(h + 1) * hd] = outs[h]
        x2 = x1_ref[...] + _dot(co_ref[...], wo_ref[...])
        x2_ref[...] = x2
        hf_ref[...] = _rms(x2, gm_ref[...]).astype(BF16)

    row = lambda b, i: (b * nq + i, 0)
    fixed = lambda b, i: (0, 0)
    t_len = n_batch * s_len
    return pl.pallas_call(
        body, name="xattn_fwd", grid=(n_batch, nq),
        in_specs=[pl.BlockSpec((tq, XATTN_WIDTH), row), pl.BlockSpec((m_len, 2 * XATTN_WIDTH), lambda b, i: (b, 0)),
                  pl.BlockSpec((tq, d), row), pl.BlockSpec((1, hd), fixed), pl.BlockSpec((1, hd), fixed),
                  pl.BlockSpec((XATTN_WIDTH, d), fixed), pl.BlockSpec((1, d), fixed)],
        out_specs=[pl.BlockSpec((tq, XATTN_WIDTH), row), pl.BlockSpec((tq, d), row), pl.BlockSpec((tq, d), row)],
        out_shape=[jax.ShapeDtypeStruct((t_len, XATTN_WIDTH), BF16), jax.ShapeDtypeStruct((t_len, d), F32),
                   jax.ShapeDtypeStruct((t_len, d), BF16)],
        compiler_params=_cparams(("parallel", "parallel")),
    )(cq, ckv, x1, gq, gk, w_co, g_mlp)


def _xattn_bwd(dx2, cq, ckv, x1, gq, gk, w_co, g_x, w_cq, n_batch, s_len, m_len, tq=512):
    d = x1.shape[1]
    tq = min(tq, s_len)
    nq = s_len // tq
    hd = XATTN_HEAD_DIM
    scale = XATTN_HEAD_DIM ** -0.5

    def body(dx2_ref, cq_ref, kv_ref, x1_ref, gq_ref, gk_ref, wo_ref, gx_ref, wq_ref,
             dx1_ref, dcq_ref, dkv_ref, dgq_ref, dgk_ref, dgx_ref, dk_acc, dv_acc):
        b = pl.program_id(0)
        i = pl.program_id(1)

        @pl.when((b == 0) & (i == 0))
        def _():
            dgq_ref[...] = jnp.zeros_like(dgq_ref)
            dgk_ref[...] = jnp.zeros_like(dgk_ref)
            dgx_ref[...] = jnp.zeros_like(dgx_ref)

        @pl.when(i == 0)
        def _():
            dk_acc[...] = jnp.zeros_like(dk_acc)
            dv_acc[...] = jnp.zeros_like(dv_acc)

        dx2 = dx2_ref[...]
        dco_all = _dot(dx2, wo_ref[...], NT)
        for h in range(XATTN_HEADS):
            sl = slice(h * hd, (h + 1) * hd)
            q = cq_ref[:, sl]
            qn = _rms(q, gq_ref[...])
            kn = _rms(kv_ref[:, sl], gk_ref[...])
            v = kv_ref[:, XATTN_WIDTH + h * hd:XATTN_WIDTH + (h + 1) * hd]
            p = _xattn_probs(qn, kn)
            dco = dco_all[:, sl]
            dv_acc[:, sl] += _dot(p, dco, TN)
            dp = _dot(dco, v, NT)
            ds = p * (dp - jnp.sum(dp * p, axis=-1, keepdims=True))
            dqn = _dot(ds, kn) * scale
            dk_acc[:, sl] += _dot(ds, qn, TN) * scale
            dq, dgq = _rms_bwd(q, gq_ref[...], dqn)
            dgq_ref[...] += dgq
            dcq_ref[:, sl] = dq.astype(BF16)
        dhq = _dot(dcq_ref[...], wq_ref[...], NT)
        dxn, dgx = _rms_bwd(x1_ref[...], gx_ref[...], dhq)
        dgx_ref[...] += dgx
        dx1_ref[...] = dx2 + dxn

        @pl.when(i == nq - 1)
        def _():
            for h in range(XATTN_HEADS):
                sl = slice(h * hd, (h + 1) * hd)
                dk, dgk = _rms_bwd(kv_ref[:, sl], gk_ref[...], dk_acc[:, sl])
                dgk_ref[...] += dgk
                dkv_ref[:, sl] = dk.astype(BF16)
                dkv_ref[:, XATTN_WIDTH + h * hd:XATTN_WIDTH + (h + 1) * hd] = dv_acc[:, sl].astype(BF16)

    row = lambda b, i: (b * nq + i, 0)
    fixed = lambda b, i: (0, 0)
    t_len = n_batch * s_len
    return pl.pallas_call(
        body, name="xattn_bwd", grid=(n_batch, nq),
        in_specs=[pl.BlockSpec((tq, d), row), pl.BlockSpec((tq, XATTN_WIDTH), row), pl.BlockSpec((m_len, 2 * XATTN_WIDTH), lambda b, i: (b, 0)),
                  pl.BlockSpec((tq, d), row), pl.BlockSpec((1, hd), fixed), pl.BlockSpec((1, hd), fixed),
                  pl.BlockSpec((XATTN_WIDTH, d), fixed), pl.BlockSpec((1, d), fixed), pl.BlockSpec((d, XATTN_WIDTH), fixed)],
        out_specs=[pl.BlockSpec((tq, d), row), pl.BlockSpec((tq, XATTN_WIDTH), row), pl.BlockSpec((m_len, 2 * XATTN_WIDTH), lambda b, i: (b, 0)),
                   pl.BlockSpec((1, hd), fixed), pl.BlockSpec((1, hd), fixed), pl.BlockSpec((1, d), fixed)],
        out_shape=[jax.ShapeDtypeStruct((t_len, d), F32), jax.ShapeDtypeStruct((t_len, XATTN_WIDTH), BF16),
                   jax.ShapeDtypeStruct((n_batch * m_len, 2 * XATTN_WIDTH), BF16),
                   jax.ShapeDtypeStruct((1, hd), F32), jax.ShapeDtypeStruct((1, hd), F32), jax.ShapeDtypeStruct((1, d), F32)],
        scratch_shapes=[pltpu.VMEM((m_len, XATTN_WIDTH), F32), pltpu.VMEM((m_len, XATTN_WIDTH), F32)],
        compiler_params=_cparams(("arbitrary", "arbitrary")),
    )(dx2, cq, ckv, x1, gq, gk, w_co, g_x, w_cq)


def _resident(shape):
    return pl.BlockSpec(shape, lambda *_: (0,) * len(shape), pipeline_mode=pl.Buffered(1))


def _mlp_fwd(hf, x2, target, w1, w2, tm=256, tf=1024):
    t_len, d = x2.shape
    f = w1.shape[1]
    tm, tf = min(tm, t_len), min(tf, f)

    def body(hf_ref, x2_ref, tg_ref, w1_ref, w2_ref, u_ref, a_ref, dy_ref, ls_ref):
        hf_t = hf_ref[...]
        for k in range(f // tf):
            cols = slice(k * tf, (k + 1) * tf)
            u = _dot(hf_t, w1_ref[:, cols])
            u_ref[:, cols] = u
            r = jnp.maximum(u, 0.0)
            a_ref[:, cols] = (r * r).astype(BF16)
        y = x2_ref[...] + _dot(a_ref[...], w2_ref[...])
        err = y - tg_ref[...]
        dy_ref[...] = err * (1.0 / d)
        ls_ref[...] = jnp.broadcast_to(jnp.sum(jnp.sum(err * err, axis=-1, keepdims=True) * (1.0 / d), axis=0, keepdims=True), ls_ref.shape)

    row = lambda i: (i, 0)
    return pl.pallas_call(
        body, name="mlp_fwd", grid=(t_len // tm,),
        in_specs=[pl.BlockSpec((tm, d), row), pl.BlockSpec((tm, d), row), pl.BlockSpec((tm, d), row), _resident((d, f)), _resident((f, d))],
        out_specs=[pl.BlockSpec((tm, f), row), pl.BlockSpec((tm, f), row), pl.BlockSpec((tm, d), row),
                   pl.BlockSpec((1, 8, LANES), lambda i: (i, 0, 0))],
        out_shape=[jax.ShapeDtypeStruct((t_len, f), F32), jax.ShapeDtypeStruct((t_len, f), BF16), jax.ShapeDtypeStruct((t_len, d), F32),
                   jax.ShapeDtypeStruct((t_len // tm, 8, LANES), F32)],
        compiler_params=_cparams(("parallel",)),
    )(hf, x2, target, w1, w2)


def _mlp_bwd(dy, u, x2, g, w1, w2, tm=256, tf=1024):
    t_len, d = x2.shape
    f = w1.shape[1]
    tm, tf = min(tm, t_len), min(tf, f)

    def body(dy_ref, u_ref, x2_ref, g_ref, w1_ref, w2_ref, du_ref, dx2_ref, dg_ref):
        @pl.when(pl.program_id(0) == 0)
        def _():
            dg_ref[...] = jnp.zeros_like(dg_ref)

        dy_t = dy_ref[...]
        dyb = dy_t.astype(BF16)
        for k in range(f // tf):
            cols = slice(k * tf, (k + 1) * tf)
            da = _dot(dyb, w2_ref[cols, :], NT)
            du_ref[:, cols] = (da * (2.0 * jnp.maximum(u_ref[:, cols], 0.0))).astype(BF16)
        dhf = _dot(du_ref[...], w1_ref[...], NT)
        dxn, dg = _rms_bwd(x2_ref[...], g_ref[...], dhf)
        dx2_ref[...] = dy_t + dxn
        dg_ref[...] += dg

    row = lambda i: (i, 0)
    fixed = lambda i: (0, 0)
    return pl.pallas_call(
        body, name="mlp_bwd", grid=(t_len // tm,),
        in_specs=[pl.BlockSpec((tm, d), row), pl.BlockSpec((tm, f), row), pl.BlockSpec((tm, d), row), pl.BlockSpec((1, d), fixed),
                  _resident((d, f)), _resident((f, d))],
        out_specs=[pl.BlockSpec((tm, f), row), pl.BlockSpec((tm, d), row), pl.BlockSpec((1, d), fixed)],
        out_shape=[jax.ShapeDtypeStruct((t_len, f), BF16), jax.ShapeDtypeStruct((t_len, d), F32), jax.ShapeDtypeStruct((1, d), F32)],
        compiler_params=_cparams(("arbitrary",)),
    )(dy, u, x2, g, w1, w2)


def _pad_lanes(v, offset=0, width=LANES):
    return jnp.zeros((1, width), F32).at[:, offset:offset + v.shape[1]].set(v)


def _col(v, offset=0, rows=SM_ROWS):
    return jnp.zeros((rows, 1), F32).at[offset:offset + v.shape[1], 0].set(v[0])


def _pack_small(g_mix, dgq, dgk, dbias, dgo, dac, dar, ddc, ddr, g_gdn_o, g_nx, g_mem, g_xq, g_xk, g_mlp, loss_tiles):
    def body(mix_ref, q_ref, k_ref, b_ref, o_ref, ac_ref, ar_ref, dc_ref, dr_ref, go_ref, nx_ref, mem_ref, xq_ref, xk_ref,
             mlp_ref, lt_ref, out_ref):
        lane = lax.broadcasted_iota(jnp.int32, (1, LANES), 1)
        diag = lax.broadcasted_iota(jnp.int32, (SM_ROWS, LANES), 0) == lax.broadcasted_iota(jnp.int32, (SM_ROWS, LANES), 1)

        def rolled(v, shift):
            return pltpu.roll(jnp.broadcast_to(v, (8, LANES)), shift, 1)[0:1, :]

        def rows_to_lanes(col):
            return jnp.sum(jnp.where(diag, col, 0.0), axis=0, keepdims=True)

        def put(row, v, n):
            out_ref[row:row + 1, 0:LANES] = jnp.where(lane < n, v, 0.0)

        out_ref[...] = jnp.zeros_like(out_ref)
        out_ref[0:1, :] = mix_ref[...]
        for row, ref in ((1, q_ref), (2, k_ref), (4, o_ref)):
            put(row, ref[...] + rolled(ref[...], FOX_HEAD_DIM), FOX_HEAD_DIM)
        put(3, rows_to_lanes(b_ref[...]), FOX_HEADS)
        for row, lane_ref, row_ref in ((5, ac_ref, ar_ref), (6, dc_ref, dr_ref)):
            put(row, rolled(lane_ref[...] + rows_to_lanes(row_ref[...]), LANES - SM_A), GDN_HEADS)
        put(7, go_ref[...], LANES)
        out_ref[8:9, :] = nx_ref[...]
        out_ref[9:10, :] = mem_ref[...]
        put(10, xq_ref[...], LANES)
        put(11, xk_ref[...], LANES)
        out_ref[12:13, :] = mlp_ref[...]
        put(LOSS_ROW, 0.5 * jnp.sum(lt_ref[...], axis=0)[0:1, :], 1)

    args = (g_mix, dgq, dgk, dbias, dgo, dac, dar, ddc, ddr, g_gdn_o, g_nx, g_mem, g_xq, g_xk, g_mlp, loss_tiles)
    return pl.pallas_call(body, name="pack_small", out_shape=jax.ShapeDtypeStruct((PACK_ROWS, D_MODEL), F32))(*args)


LATE_WEIGHTS = (("w_out", "w_cq", "w_ckv", "w_co"), ("w_mlp1", "w_mlp2"))
GRAD_GROUPS = (("w_mlp2", "w_mlp1"), ("w_co", "w_cq", "w_ckv", "w_out"), ("w_in", "gdn_conv_w"))


def _local_step(x, mem, target, norm_mix_g, w_in, fox_qnorm_g, fox_knorm_g, fox_f_bias, fox_onorm_g, gdn_conv_w, gdn_A_log,
                gdn_dt_bias, gdn_onorm_g, norm_xattn_g, mem_norm_g, xattn_qnorm_g, xattn_knorm_g, norm_mlp_g,
                late_weights, grads_ready=None, first_token=0.0):
    if grads_ready is None:
        grads_ready = lambda group: 0.0
    n_batch, s_len, d = x.shape
    m_len = mem.shape[1]
    t_len = n_batch * s_len
    tq = min(FOX_BLOCK, s_len)
    nq = s_len // tq
    n_chunks = s_len // GDN_CHUNK
    x2d = x.reshape(t_len, d)

    wp = jnp.concatenate([w_in[0:1536], w_in[1544:3080], w_in[3088:3600], w_in[1536:1544], w_in[3080:3088],
                          jnp.zeros((P_DIM - 3600, d), BF16)], axis=0)
    wst = jnp.concatenate([w_in[1536:1544], w_in[3080:3088]], axis=0)
    conv_w = jnp.concatenate([gdn_conv_w, jnp.zeros((8 - CONV_WIDTH, gdn_conv_w.shape[1]), F32)], axis=0)
    bias_col = _col(fox_f_bias, SM_F)
    gq2, gk2, go2 = (jnp.tile(g, (1, 2)) for g in (fox_qnorm_g, fox_knorm_g, fox_onorm_g))
    a_c, dt_c = _pad_lanes(gdn_A_log, SM_A), _pad_lanes(gdn_dt_bias, SM_A)
    a_r, dt_r = _col(gdn_A_log, SM_A), _col(gdn_dt_bias, SM_A)

    h1, pfox, pgdn, pz, sm, smt = _in_proj(x2d, norm_mix_g + first_token, wp, wst)
    c_rows = _fox_cum(smt, bias_col, n_batch, s_len)
    cb = c_rows.reshape(SM_ROWS, n_batch, nq, tq).transpose(1, 2, 0, 3)
    pf3 = pfox.reshape(n_batch, s_len, 1536)
    o_fox, oa, lse = _fox_fwd(pf3, cb, gq2, gk2, go2, tq)
    pg3 = pgdn.reshape(n_batch, s_len, 1536)
    qkvn = _gdn_pre(pg3, conv_w)
    z3 = pz.reshape(n_batch, s_len, GDN_WIDTH)
    smc = sm.reshape(n_batch, s_len, LANES)
    smr = smt.reshape(SM_ROWS, n_batch * n_chunks, GDN_CHUNK).transpose(1, 0, 2)
    ob, states, inverses = _gdn_fwd(qkvn, z3, smc, smr, a_c, dt_c, a_r, dt_r, gdn_onorm_g)
    oa2, ob2 = oa.reshape(t_len, FOX_WIDTH), ob.reshape(t_len, GDN_WIDTH)
    w_out, w_cq, w_ckv, w_co = late_weights(LATE_WEIGHTS[0], ob2)
    x1, hq, cq = _out_proj(x2d, oa2, ob2, w_out, norm_xattn_g, w_cq)
    mem2d = mem.reshape(n_batch * m_len, d)
    hm, ckv = _mem_kv(mem2d, mem_norm_g, w_ckv)
    co, x2, hf = _xattn_fwd(cq, ckv, x1, xattn_qnorm_g, xattn_knorm_g, w_co, norm_mlp_g, n_batch, s_len, m_len)
    w_mlp1, w_mlp2 = late_weights(LATE_WEIGHTS[1], hf)
    u, a_act, dy, loss_tiles = _mlp_fwd(hf, x2, target.reshape(t_len, d), w_mlp1, w_mlp2)

    grads = {}
    du, dx2, grads["norm_mlp_g"] = _mlp_bwd(dy, u, x2, norm_mlp_g, w_mlp1, w_mlp2)
    grads["w_mlp2"] = _wgrad(a_act, dy, "wgrad_mlp2", bt=2048)
    grads["w_mlp1"] = _wgrad(hf, du, "wgrad_mlp1", bt=2048, column_blocks=D_FF // N_DEV)
    token = grads_ready({k: grads[k] for k in GRAD_GROUPS[0]})
    grads["w_co"] = _wgrad(co, dx2, "wgrad_co", column_blocks=D_MODEL // N_DEV)
    dx1, dcq, dckv, grads["xattn_qnorm_g"], grads["xattn_knorm_g"], grads["norm_xattn_g"] = _xattn_bwd(
        dx2, cq, ckv, x1, xattn_qnorm_g + token, xattn_knorm_g, w_co, norm_xattn_g, w_cq, n_batch, s_len, m_len)
    grads["w_cq"] = _wgrad(hq, dcq, "wgrad_cq")
    grads["w_ckv"] = _wgrad(hm, dckv, "wgrad_ckv")
    grads["mem_norm_g"] = _mem_kv_bwd(dckv, mem2d, mem_norm_g, w_ckv)
    grads["w_out"] = _wgrad_stacked([oa2, ob2], dx1, "wgrad_out", bn=1024)
    token = grads_ready({k: grads[k] for k in GRAD_GROUPS[1]})
    dcat = _out_proj_bwd(dx1, w_out)
    dcat3 = dcat.reshape(n_batch, s_len, d)

    dqkvn, dz, dsmc, dsmr, dac, ddc, dar, ddr, grads["gdn_onorm_g"] = _gdn_bwd(
        qkvn, z3, smc, smr, a_c, dt_c, a_r, dt_r, gdn_onorm_g + token, states, inverses, dcat3)
    dpg, dconv = _gdn_pre_bwd(pg3, conv_w, dqkvn)
    grads["gdn_conv_w"] = dconv[0:CONV_WIDTH]

    dq, dk, dv, dcb, dgq, dgk, dgo = _fox_bwd(pf3, cb, gq2, gk2, go2, o_fox, lse, dcat3, tq)
    dc8 = dcb[:, :, :, 0:2, :].transpose(1, 3, 0, 2, 4).reshape(FOX_HEADS, t_len)
    dc_rows = jnp.concatenate([dc8, jnp.zeros((SM_ROWS - FOX_HEADS, t_len), F32)], axis=0)
    dl_rows, dbias = _fox_cum_bwd(dc_rows, smt, bias_col, n_batch, s_len)
    dsm_rows = jnp.concatenate([dl_rows[0:SM_B], dsmr.transpose(1, 0, 2).reshape(SM_ROWS, t_len)[SM_B:SM_ROWS]], axis=0)

    dprojs = [dq.reshape(t_len, FOX_WIDTH), dk.reshape(t_len, FOX_WIDTH), dv.reshape(t_len, FOX_WIDTH),
              dpg.reshape(t_len, 1536), dz.reshape(t_len, GDN_WIDTH), dsmc.reshape(t_len, LANES)]
    dwp = _wgrad_stacked(dprojs, h1, "wgrad_in")
    dwst = _rows_matmul(dsm_rows, h1, "wgrad_in_rows")
    dw_small = dwp[P_SMALL:P_SMALL + SM_ROWS] + dwst
    grads["w_in"] = jnp.concatenate([dwp[0:1536], dw_small[0:8], dwp[1536:3072], dw_small[8:16], dwp[3072:3584]], axis=0)
    token = grads_ready({k: grads[k] for k in GRAD_GROUPS[2]})
    grad_x, grads["norm_mix_g"] = _in_proj_bwd(dprojs, dsm_rows, x2d, norm_mix_g + token, wp, wst, dx1)
    packed = _pack_small(grads["norm_mix_g"], dgq, dgk, dbias, dgo, dac, dar, ddc, ddr, grads["gdn_onorm_g"], grads["norm_xattn_g"],
                         grads["mem_norm_g"], grads["xattn_qnorm_g"], grads["xattn_knorm_g"], grads["norm_mlp_g"], loss_tiles)
    return packed, grad_x.reshape(n_batch, s_len, d), {k: grads[k] for k in SHARDED}


MESH_ID = pl.DeviceIdType.MESH
ANY_SPEC = pl.BlockSpec(memory_space=pl.ANY)


def _place():
    x, y, c = lax.axis_index("x"), lax.axis_index("y"), lax.axis_index("c")
    return x, y, c, [(1 - x, y), (x, 1 - y), (1 - x, 1 - y)]


def _place_own(src_ref, dst_ref):
    def staged(buf, sem):
        for a, b in ((src_ref, buf), (buf, dst_ref)):
            cp = pltpu.make_async_copy(a, b, sem)
            cp.start()
            cp.wait()

    pl.run_scoped(staged, pltpu.VMEM(src_ref.shape, src_ref.dtype), pltpu.SemaphoreType.DMA)


def _all_gather_body(n, ins, outs, send_sems, recv_sems):
    x, y, c, chips = _place()
    me, sibling = (x, y, c), (x, y, 1 - c)

    def copy(a, k, block, to, src=None):
        dst = outs[a].at[4 * block[0] + 2 * block[1] + block[2]]
        return pltpu.make_async_remote_copy(src_ref=dst if src is None else src, dst_ref=dst, send_sem=send_sems.at[a, k],
                                            recv_sem=recv_sems.at[a, k], device_id=to, device_id_type=MESH_ID)

    first = []
    for a in range(n):
        first.append(copy(a, 0, me, sibling, src=ins[a]))
        first += [copy(a, 1 + j, me, (*chip, c), src=ins[a]) for j, chip in enumerate(chips)]
    for cp in first:
        cp.start()
    for a in range(n):
        _place_own(ins[a], outs[a].at[4 * x + 2 * y + c])
    passed = []
    for j, chip in enumerate(chips):
        for a in range(n):
            copy(a, 1 + j, (*chip, c), me).wait_recv()
            fwd = copy(a, 4 + j, (*chip, c), sibling)
            fwd.start()
            passed.append(fwd)
    for a in range(n):
        copy(a, 0, sibling, me).wait_recv()
        for j, chip in enumerate(chips):
            copy(a, 4 + j, (*chip, 1 - c), me).wait_recv()
    for cp in first + passed:
        cp.wait_send()


def _all_gather_hbm(arrs, name):
    n = len(arrs)

    def body(*refs):
        _all_gather_body(n, refs[:n], refs[n:2 * n], refs[2 * n], refs[2 * n + 1])

    return pl.pallas_call(
        body, name=name, in_specs=[ANY_SPEC] * n, out_specs=[ANY_SPEC] * n,
        out_shape=[jax.ShapeDtypeStruct((N_DEV,) + a.shape, a.dtype) for a in arrs],
        scratch_shapes=[pltpu.SemaphoreType.DMA((n, 7)), pltpu.SemaphoreType.DMA((n, 7))],
        compiler_params=pltpu.CompilerParams(vmem_limit_bytes=VMEM_LIMIT),
    )(*arrs)


def _pair_exchange(arrs, name):
    n = len(arrs)

    def body(*refs):
        ins, outs = refs[:n], refs[n:2 * n]
        send_sems, recv_sems = refs[2 * n:]
        x, y, c, _ = _place()
        copies = []
        for a in range(n):
            for chip in range(4):
                copies.append(pltpu.make_async_remote_copy(
                    src_ref=ins[a].at[2 * chip + (1 - c)], dst_ref=outs[a].at[chip], send_sem=send_sems.at[a, chip],
                    recv_sem=recv_sems.at[a, chip], device_id=(x, y, 1 - c), device_id_type=MESH_ID))
        for cp in copies:
            cp.start()
        for cp in copies:
            cp.wait()

    return pl.pallas_call(
        body, name=name, in_specs=[ANY_SPEC] * n, out_specs=[ANY_SPEC] * n,
        out_shape=[jax.ShapeDtypeStruct((4,) + a.shape[1:], a.dtype) for a in arrs],
        scratch_shapes=[pltpu.SemaphoreType.DMA((n, 4)), pltpu.SemaphoreType.DMA((n, 4))],
    )(*arrs)


HBM_SPEC = pl.BlockSpec(memory_space=pltpu.HBM)
SEM_SPEC = pl.BlockSpec(memory_space=pltpu.SEMAPHORE)
DATAFLOW = pltpu.SideEffectType.DATAFLOW_SIDE_EFFECTING


def _in_hbm(arrs):
    return [pltpu.with_memory_space_constraint(a, pltpu.HBM) for a in arrs]


def _copies_start(name, srcs, lands, make_copies, after):
    n = len(srcs)
    n_copies = len(make_copies(srcs, lands, None, None)[0])

    def body(*refs):
        send_sems, recv_sems = refs[2 * n + 1], refs[2 * n + 2]
        for row in make_copies(refs[:n], refs[n:2 * n], send_sems, recv_sems):
            for cp in row:
                cp.start()
        refs[-1][...] = jnp.zeros_like(refs[-1])

    sems = pltpu.SemaphoreType.DMA((n * n_copies,))
    thru = [pltpu.HBM(a.shape, a.dtype) for a in list(srcs) + list(lands)]
    res = pl.pallas_call(
        body, name=name, in_specs=[HBM_SPEC] * (2 * n) + [ANY_SPEC],
        out_specs=(SEM_SPEC, SEM_SPEC, *[HBM_SPEC] * (2 * n), pl.BlockSpec(memory_space=pltpu.VMEM)),
        out_shape=(sems, sems, *thru, jax.ShapeDtypeStruct((8, LANES), F32)),
        input_output_aliases={i: 2 + i for i in range(2 * n)},
        compiler_params=pltpu.CompilerParams(has_side_effects=DATAFLOW),
    )(*_in_hbm(list(srcs) + list(lands)), after)
    return res[0], res[1], list(res[2:2 + n]), list(res[2 + n:2 + 2 * n]), res[-1]


def _copies_wait(name, send_sems, recv_sems, srcs, lands, after, make_copies, own_block=False):
    n = len(srcs)

    def body(*refs):
        if own_block:
            for a in range(n):
                _place_own(refs[a], _own_part(refs[a], refs[3 * n + 3 + a]))
        for row in make_copies(refs[:n], refs[n:2 * n], refs[2 * n], refs[2 * n + 1]):
            for cp in row:
                cp.wait_send()
                cp.wait_recv()

    res = pl.pallas_call(
        body, name=name, in_specs=[HBM_SPEC] * (2 * n) + [SEM_SPEC, SEM_SPEC, ANY_SPEC],
        out_specs=tuple([HBM_SPEC] * (2 * n)),
        out_shape=tuple(pltpu.HBM(a.shape, a.dtype) for a in list(srcs) + list(lands)),
        input_output_aliases={i: i for i in range(2 * n)},
        compiler_params=pltpu.CompilerParams(has_side_effects=DATAFLOW, vmem_limit_bytes=VMEM_LIMIT),
    )(*srcs, *lands, send_sems, recv_sems, after)
    return list(res[:n]), list(res[n:])


def _own_part(src_ref, land_ref):
    me = 4 * lax.axis_index("x") + 2 * lax.axis_index("y") + lax.axis_index("c")
    rows, cols = src_ref.shape
    if land_ref.shape[0] == N_DEV * rows:
        return land_ref.at[pl.ds(pl.multiple_of(me * rows, rows), rows), :]
    return land_ref.at[:, pl.ds(pl.multiple_of(me * cols, cols), cols)]


def _gather_copies(srcs, lands, send_sems, recv_sems):
    if send_sems is None:
        return [[None] * 7]
    x, y, c, _ = _place()
    rows = []
    for a in range(len(srcs)):
        row = []
        for k in range(7):
            r = k + 1
            to = (1 - x if r & 4 else x, 1 - y if r & 2 else y, 1 - c if r & 1 else c)
            row.append(pltpu.make_async_remote_copy(
                src_ref=srcs[a], dst_ref=_own_part(srcs[a], lands[a]), send_sem=send_sems.at[7 * a + k], recv_sem=recv_sems.at[7 * a + k],
                device_id=to, device_id_type=MESH_ID))
        rows.append(row)
    return rows


def _scatter_copies(srcs, lands, send_sems, recv_sems):
    if send_sems is None:
        return [[None] * 7]
    x, y, c, _ = _place()
    rows = []
    for a in range(len(srcs)):
        row = []
        for k in range(7):
            r = k + 1
            to = (1 - x if r & 4 else x, 1 - y if r & 2 else y, 1 - c if r & 1 else c)
            row.append(pltpu.make_async_remote_copy(
                src_ref=srcs[a].at[4 * to[0] + 2 * to[1] + to[2]], dst_ref=lands[a].at[k], send_sem=send_sems.at[7 * a + k],
                recv_sem=recv_sems.at[7 * a + k], device_id=to, device_id_type=MESH_ID))
        rows.append(row)
    return rows


def _chip_copies(srcs, lands, send_sems, recv_sems):
    if send_sems is None:
        return [[None] * 3]
    x, y, c, chips = _place()
    return [[pltpu.make_async_remote_copy(
        src_ref=srcs[a].at[2 * chip[0] + chip[1]], dst_ref=lands[a].at[j], send_sem=send_sems.at[3 * a + j], recv_sem=recv_sems.at[3 * a + j],
        device_id=(*chip, c), device_id_type=MESH_ID) for j, chip in enumerate(chips)] for a in range(len(srcs))]


def _tile(rows, cols):
    if rows <= 256:
        return rows, cols
    tr = 256 if cols <= 512 else 128
    if rows % tr == 0:
        return tr, cols
    return rows, 512


def _pair_sum(core, own, got, name):
    _, rows, cols = own.shape
    tr, tc = _tile(rows, cols)

    def body(c_ref, own_ref, got_ref, o_ref):
        o_ref[0] = own_ref[0] + got_ref[0]

    return pl.pallas_call(
        body, name=name,
        grid_spec=pltpu.PrefetchScalarGridSpec(
            num_scalar_prefetch=1, grid=(4, rows // tr, cols // tc),
            in_specs=[pl.BlockSpec((1, tr, tc), lambda k, i, j, c: (2 * k + c[0], i, j)),
                      pl.BlockSpec((1, tr, tc), lambda k, i, j, c: (k, i, j))],
            out_specs=pl.BlockSpec((1, tr, tc), lambda k, i, j, c: (k, i, j))),
        out_shape=jax.ShapeDtypeStruct((4, rows, cols), F32),
        compiler_params=_cparams(("parallel", "parallel", "parallel")),
    )(core, own, got)


def _adamw(w, g, m, v):
    m_new = ADAM_B1 * m + (1.0 - ADAM_B1) * g
    v_new = ADAM_B2 * v + (1.0 - ADAM_B2) * (g * g)
    m_hat = m_new / (1.0 - ADAM_B1 ** ADAM_STEP)
    v_hat = v_new / (1.0 - ADAM_B2 ** ADAM_STEP)
    delta = -ADAM_LR * (m_hat / (jnp.sqrt(v_hat) + ADAM_EPS) + ADAM_WD * w)
    return delta, m_new, v_new


def _sum_adam(chip, sums, parts, w, m, v, name):
    n_parts, rows, cols = parts.shape
    tr, tc = _tile(rows, cols)

    def body(chip_ref, own_ref, p_ref, w_ref, m_ref, v_ref, g_ref, d_ref, mo_ref, vo_ref):
        g = own_ref[0]
        for k in range(n_parts):
            g = g + p_ref[k]
        g_ref[...] = g
        d_ref[...], mo_ref[...], vo_ref[...] = _adamw(w_ref[...], g, m_ref[...], v_ref[...])

    tile = pl.BlockSpec((tr, tc), lambda i, j, ch: (i, j))
    out = jax.ShapeDtypeStruct((rows, cols), F32)
    return pl.pallas_call(
        body, name=name,
        grid_spec=pltpu.PrefetchScalarGridSpec(
            num_scalar_prefetch=1, grid=(rows // tr, cols // tc),
            in_specs=[pl.BlockSpec((1, tr, tc), lambda i, j, ch: (ch[0], i, j)),
                      pl.BlockSpec((n_parts, tr, tc), lambda i, j, ch: (0, i, j)), tile, tile, tile],
            out_specs=[tile, tile, tile, tile]),
        out_shape=[out, out, out, out],
        compiler_params=_cparams(("parallel", "parallel")),
    )(chip, sums, parts, w, m, v)


SHARDED = ("w_in", "gdn_conv_w", "w_out", "w_cq", "w_ckv", "w_co", "w_mlp1", "w_mlp2")
TRANSPOSED = ("w_in",)
COLUMN_SHARDED = ("gdn_conv_w", "w_co", "w_mlp1")
REPLICATED = ("norm_mix_g", "fox_qnorm_g", "fox_knorm_g", "fox_f_bias", "fox_onorm_g", "gdn_A_log", "gdn_dt_bias", "gdn_onorm_g",
              "norm_xattn_g", "mem_norm_g", "xattn_qnorm_g", "xattn_knorm_g", "norm_mlp_g")
WEIGHTS = ("norm_mix_g", "w_in", "fox_qnorm_g", "fox_knorm_g", "fox_f_bias", "fox_onorm_g", "gdn_conv_w", "gdn_A_log", "gdn_dt_bias",
           "gdn_onorm_g", "w_out", "norm_xattn_g", "mem_norm_g", "w_cq", "w_ckv", "xattn_qnorm_g", "xattn_knorm_g", "w_co",
           "norm_mlp_g", "w_mlp1", "w_mlp2")
PACK_ROWS = 16
LOSS_ROW = len(REPLICATED)


def _whole(name, gathered):
    if name in COLUMN_SHARDED:
        return gathered.transpose(1, 0, 2).reshape(gathered.shape[1], N_DEV * gathered.shape[2])
    return gathered.reshape(N_DEV * gathered.shape[1], gathered.shape[2])


def _whole_shape(name, shard_shape):
    rows, cols = shard_shape
    return (rows, N_DEV * cols) if name in COLUMN_SHARDED else (N_DEV * rows, cols)


def _blocks(name, whole):
    if whole.ndim == 3:
        return whole
    if name in COLUMN_SHARDED:
        rows, cols = whole.shape
        return whole.reshape(rows, N_DEV, cols // N_DEV).transpose(1, 0, 2)
    return whole.reshape(N_DEV, whole.shape[0] // N_DEV, whole.shape[1])


def _adam_small(everyone, ws, ms, vs):
    n_par = len(ws)

    def body(*refs):
        ev_ref = refs[0]
        w_refs, m_refs, v_refs = (refs[1 + j * n_par:1 + (j + 1) * n_par] for j in range(3))
        outs = refs[1 + 3 * n_par:-1]
        sum_ref = refs[-1]
        total = ev_ref[0]
        for dev in range(1, N_DEV):
            total = total + ev_ref[dev]
        sum_ref[...] = total
        for i in range(n_par):
            n = w_refs[i].shape[1]
            g = sum_ref[i:i + 1, 0:n]
            outs[4 * i][...] = g
            outs[4 * i + 1][...], outs[4 * i + 2][...], outs[4 * i + 3][...] = _adamw(w_refs[i][...], g, m_refs[i][...], v_refs[i][...])
        outs[4 * n_par][...] = sum_ref[LOSS_ROW:LOSS_ROW + 1, 0:1]

    shapes = [jax.ShapeDtypeStruct(a.shape, F32) for a in ws for _ in range(4)] + [jax.ShapeDtypeStruct((1, 1), F32)]
    return pl.pallas_call(body, name="adam_small", out_shape=shapes,
                          scratch_shapes=[pltpu.VMEM((PACK_ROWS, D_MODEL), F32)])(everyone, *ws, *ms, *vs)


def kernel(x, mem, norm_mix_g, w_in, fox_qnorm_g, fox_knorm_g, fox_f_bias, fox_onorm_g, gdn_conv_w, gdn_A_log, gdn_dt_bias, gdn_onorm_g, w_out, norm_xattn_g, mem_norm_g, w_cq, w_ckv, xattn_qnorm_g, xattn_knorm_g, w_co, norm_mlp_g, w_mlp1, w_mlp2, loss_target, m_norm_mix_g, m_w_in, m_fox_qnorm_g, m_fox_knorm_g, m_fox_f_bias, m_fox_onorm_g, m_gdn_conv_w, m_gdn_A_log, m_gdn_dt_bias, m_gdn_onorm_g, m_w_out, m_norm_xattn_g, m_mem_norm_g, m_w_cq, m_w_ckv, m_xattn_qnorm_g, m_xattn_knorm_g, m_w_co, m_norm_mlp_g, m_w_mlp1, m_w_mlp2, v_norm_mix_g, v_w_in, v_fox_qnorm_g, v_fox_knorm_g, v_fox_f_bias, v_fox_onorm_g, v_gdn_conv_w, v_gdn_A_log, v_gdn_dt_bias, v_gdn_onorm_g, v_w_out, v_norm_xattn_g, v_mem_norm_g, v_w_cq, v_w_ckv, v_xattn_qnorm_g, v_xattn_knorm_g, v_w_co, v_norm_mlp_g, v_w_mlp1, v_w_mlp2):
    given = dict(locals())
    w = {k: given[k] for k in WEIGHTS}
    m = {k: given["m_" + k] for k in WEIGHTS}
    v = {k: given["v_" + k] for k in WEIGHTS}

    core = lax.axis_index("c").astype(jnp.int32).reshape(1)
    chip = (2 * lax.axis_index("x") + lax.axis_index("y")).astype(jnp.int32).reshape(1)
    me = 4 * lax.axis_index("x") + 2 * lax.axis_index("y") + lax.axis_index("c")

    local = lambda d: {k: jnp.transpose(d[k][0]) if k in TRANSPOSED else d[k][0] for k in SHARDED}
    w2, m2, v2 = local(w), local(m), local(v)
    shards = {k: w2[k] if k == "gdn_conv_w" else w2[k].astype(BF16) for k in SHARDED}
    early = [k for k in SHARDED if not any(k in group for group in LATE_WEIGHTS)]
    gathered = _all_gather_hbm([shards[k] for k in early], "gather_early")
    whole = {k: _whole(k, g) for k, g in zip(early, gathered)}
    gathers, after = {}, gathered[0]
    for i, group in enumerate(LATE_WEIGHTS):
        lands = [lax.empty(_whole_shape(k, shards[k].shape), BF16) for k in group]
        gathers[group] = _copies_start("gather_late_start_" + str(i), [shards[k] for k in group], lands, _gather_copies, after=after)
        after = gathers[group][4]
    first_token = after[0, 0]

    def late_weights(group, after):
        gather = gathers[group]
        _, lands = _copies_wait("gather_late_wait_" + str(LATE_WEIGHTS.index(group)), gather[0], gather[1], gather[2], gather[3],
                                after, _gather_copies, own_block=True)
        return lands

    pending = []

    def grads_ready(group):
        names = list(group)
        tag = str(len(pending))
        own = [_blocks(k, group[k]) for k in names]
        if "w_in" in names:
            got = _pair_exchange(own, "grad_pair_exchange_" + tag)
            srcs = [_pair_sum(core, o, g, "grad_pair_sum_" + k) for k, o, g in zip(names, own, got)]
            copies, index, n_parts = _chip_copies, chip, 3
        else:
            srcs, copies, index, n_parts = own, _scatter_copies, me.astype(jnp.int32).reshape(1), 7
        lands = [lax.empty((n_parts,) + s.shape[1:], s.dtype) for s in srcs]
        started = _copies_start("grad_exchange_start_" + tag, srcs, lands, copies, after=core)
        pending.append((names, started, copies, index))
        return started[4][0, 0]

    small = {k: w[k] for k in REPLICATED}
    packed, grad_x, _ = _local_step(x, mem, loss_target, **small, **whole, late_weights=late_weights,
                                    grads_ready=grads_ready, first_token=first_token)

    small_lands = [lax.empty((N_DEV * PACK_ROWS, D_MODEL), F32)]
    small_gather = _copies_start("gather_small_start", [packed], small_lands, _gather_copies, after=grad_x)

    out_g, out_d, out_m, out_v = {}, {}, {}, {}
    after = small_gather[4]
    for tag, (names, started, copies, index) in enumerate(pending):
        srcs, parts = _copies_wait("grad_exchange_wait_" + str(tag), started[0], started[1], started[2], started[3], after, copies)
        for k, s, p in zip(names, srcs, parts):
            res = _sum_adam(index, s, p, w2[k], m2[k], v2[k], "adam_" + k)
            out_g[k], out_d[k], out_m[k], out_v[k] = ((jnp.transpose(r) if k in TRANSPOSED else r)[None] for r in res)
            after = res[0]

    _, (everyone,) = _copies_wait("gather_small_wait", small_gather[0], small_gather[1], small_gather[2], small_gather[3], after,
                                  _gather_copies, own_block=True)
    res = _adam_small(everyone.reshape(N_DEV, PACK_ROWS, D_MODEL), [w[k] for k in REPLICATED], [m[k] for k in REPLICATED],
                      [v[k] for k in REPLICATED])
    for i, k in enumerate(REPLICATED):
        out_g[k], out_d[k], out_m[k], out_v[k] = res[4 * i:4 * i + 4]
    loss = res[-1].reshape(())

    return (loss, grad_x, *[out_g[k] for k in WEIGHTS], *[out_d[k] for k in WEIGHTS], *[out_m[k] for k in WEIGHTS],
            *[out_v[k] for k in WEIGHTS])
```
